```python
import jax, jax.numpy as jnp
from jax import lax
import numpy as np

D_MODEL = 1024
BATCH = 8
SEQ = 8192
DEPTH = 1

D_PLE = 256
D_MIX = 2 * D_MODEL
GM_WIDTH = D_MIX // 2
GM_HEADS = 8
GM_HEAD_DIM = GM_WIDTH // GM_HEADS
GM_CHUNK = 128
SSM_WIDTH = D_MIX - GM_WIDTH
SSM_HEAD_DIM = 64
SSM_HEADS = SSM_WIDTH // SSM_HEAD_DIM
SSM_GROUPS = 2
SSM_STATE = 128
SSM_CONV = 4
SSM_CHUNK = 128
SSM_CONV_DIM = SSM_WIDTH + 2 * SSM_GROUPS * SSM_STATE
D_FF = 256 * ((8 * D_MODEL // 3 + 255) // 256)
EPS = 1e-6
IN_SPLITS = (GM_WIDTH, 2 * GM_WIDTH, 2 * GM_WIDTH + SSM_WIDTH, 2 * GM_WIDTH + SSM_WIDTH + SSM_CONV_DIM)
IN_PROJ_DIM = 2 * GM_WIDTH + SSM_WIDTH + SSM_CONV_DIM + SSM_HEADS

kernel_name = "hybrid_gmlp_ssd_macaron_block"


def rmsnorm(x, g):
    xf = x.astype(jnp.float32)
    y = xf * lax.rsqrt(jnp.mean(xf * xf, axis=-1, keepdims=True) + EPS)
    return (y * g.astype(jnp.float32)).astype(x.dtype)


def layernorm(x, g, b):
    xf = x.astype(jnp.float32)
    mu = jnp.mean(xf, axis=-1, keepdims=True)
    xc = xf - mu
    y = xc * lax.rsqrt(jnp.mean(xc * xc, axis=-1, keepdims=True) + EPS)
    return (y * g.astype(jnp.float32) + b.astype(jnp.float32)).astype(x.dtype)


def swiglu(x, w_gate, w_up, w_down):
    return (jax.nn.silu(x @ w_gate) * (x @ w_up)) @ w_down


def chunked_spatial_gating(u, v, ln_g, ln_b, w_s, b_s):
    bsz, L, _ = u.shape
    nc = L // GM_CHUNK
    v = layernorm(v, ln_g, ln_b).reshape(bsz, nc, GM_CHUNK, GM_HEADS, GM_HEAD_DIM)
    mask = jnp.tril(jnp.ones((GM_CHUNK, GM_CHUNK), dtype=bool))
    w = jnp.where(mask, w_s, jnp.zeros_like(w_s)).astype(v.dtype)
    mixed = jnp.einsum("hts,bcshd->bcthd", w, v) + b_s.T.astype(v.dtype)[None, None, :, :, None]
    return u * mixed.reshape(bsz, L, GM_WIDTH)


def causal_depthwise_conv(x, w, b):
    y = lax.conv_general_dilated(
        x, w[:, None, :].astype(x.dtype), window_strides=(1,), padding=[(SSM_CONV - 1, 0)],
        dimension_numbers=("NWC", "WIO", "NWC"), feature_group_count=x.shape[-1])
    return y + b.astype(x.dtype)


def segsum_exp(cs):
    T = cs.shape[-1]
    diff = cs[..., :, None] - cs[..., None, :]
    mask = jnp.tril(jnp.ones((T, T), dtype=bool))
    return jnp.exp(jnp.where(mask, diff, -jnp.inf))


def ssd_chunked(x, dt, a, bm, cm):
    bsz, L, H, P = x.shape
    nc = L // SSM_CHUNK
    k = H // SSM_GROUPS
    xdt = (x * dt[..., None]).reshape(bsz, nc, SSM_CHUNK, SSM_GROUPS, k, P)
    adt = (dt * a).reshape(bsz, nc, SSM_CHUNK, SSM_GROUPS, k).transpose(0, 3, 4, 1, 2)
    bm = bm.reshape(bsz, nc, SSM_CHUNK, SSM_GROUPS, SSM_STATE)
    cm = cm.reshape(bsz, nc, SSM_CHUNK, SSM_GROUPS, SSM_STATE)
    a_cs = jnp.cumsum(adt, axis=-1)
    decay = segsum_exp(a_cs)
    cb = jnp.einsum("bclgn,bcsgn->bgcls", cm, bm)
    y_diag = jnp.einsum("bgkcls,bcsgkp->bclgkp", cb[:, :, None] * decay, xdt)
    decay_states = jnp.exp(a_cs[..., -1:] - a_cs).transpose(0, 3, 4, 1, 2)
    states = jnp.einsum("bclgn,bclgkp->bcgkpn", bm, xdt * decay_states[..., None])
    chunk_tot = jnp.pad(a_cs[..., -1], ((0, 0), (0, 0), (0, 0), (1, 0)))
    decay_chunk = segsum_exp(jnp.cumsum(chunk_tot, axis=-1))
    states = jnp.concatenate([jnp.zeros_like(states[:, :1]), states], axis=1)
    new_states = jnp.einsum("bgkzc,bcgkpn->bzgkpn", decay_chunk, states)
    prev_states = new_states[:, :-1]
    out_decay = jnp.exp(a_cs).transpose(0, 3, 4, 1, 2)
    y_off = jnp.einsum("bclgn,bcgkpn->bclgkp", cm, prev_states) * out_decay[..., None]
    return (y_diag + y_off).reshape(bsz, L, H, P)


def mamba2_mixer(z, xbc, dt_raw, conv_w, conv_b, dt_bias, a_log, d_skip, norm_g):
    bsz, L, _ = z.shape
    f32 = jnp.float32
    xbc = jax.nn.silu(causal_depthwise_conv(xbc, conv_w, conv_b))
    xs, bm, cm = jnp.split(xbc, [SSM_WIDTH, SSM_WIDTH + SSM_GROUPS * SSM_STATE], axis=-1)
    xs = xs.reshape(bsz, L, SSM_HEADS, SSM_HEAD_DIM).astype(f32)
    bm = bm.reshape(bsz, L, SSM_GROUPS, SSM_STATE).astype(f32)
    cm = cm.reshape(bsz, L, SSM_GROUPS, SSM_STATE).astype(f32)
    dt = jax.nn.softplus(dt_raw.astype(f32) + dt_bias.astype(f32))
    a = -jnp.exp(a_log.astype(f32))
    y = ssd_chunked(xs, dt, a, bm, cm) + xs * d_skip.astype(f32)[:, None]
    y = y.reshape(bsz, L, SSM_WIDTH) * jax.nn.silu(z.astype(f32))
    y = y.reshape(bsz, L, SSM_GROUPS, SSM_WIDTH // SSM_GROUPS)
    y = y * lax.rsqrt(jnp.mean(y * y, axis=-1, keepdims=True) + EPS)
    return (y.reshape(bsz, L, SSM_WIDTH) * norm_g.astype(f32)).astype(z.dtype)


def _fwd_setup_inputs(seed: int = 0) -> dict:
    key = jax.random.key(seed)
    ks = iter(jax.random.split(key, 40))

    def nrm(shape, scale):
        return jax.random.normal(next(ks), shape, jnp.float32) * scale

    def gain(shape):
        return 1.0 + 0.1 * jax.random.normal(next(ks), shape, jnp.float32)

    L = DEPTH
    x = jax.random.normal(next(ks), (BATCH, SEQ, D_MODEL), jnp.float32)
    p = jax.random.normal(next(ks), (DEPTH, BATCH, SEQ, D_PLE), jnp.float32)
    dt0 = jnp.exp(jax.random.uniform(next(ks), (L, SSM_HEADS), jnp.float32)
                  * (np.log(0.1) - np.log(0.001)) + np.log(0.001))
    dt0 = jnp.maximum(dt0, 1e-4)
    dt_bias = dt0 + jnp.log(-jnp.expm1(-dt0))
    a_log = jnp.log(jax.random.uniform(next(ks), (L, SSM_HEADS), jnp.float32, 1.0, 16.0))
    return {
        "x": x,
        "p": p,
        "ffn1_norm": gain((L, D_MODEL)),
        "ffn1_w_gate": nrm((L, D_MODEL, D_FF), D_MODEL ** -0.5),
        "ffn1_w_up": nrm((L, D_MODEL, D_FF), D_MODEL ** -0.5),
        "ffn1_w_down": nrm((L, D_FF, D_MODEL), D_FF ** -0.5),
        "mix_norm": gain((L, D_MODEL)),
        "w_in": nrm((L, D_MODEL, IN_PROJ_DIM), D_MODEL ** -0.5),
        "gm_ln_g": gain((L, GM_WIDTH)),
        "gm_ln_b": nrm((L, GM_WIDTH), 0.02),
        "gm_w_s": nrm((L, GM_HEADS, GM_CHUNK, GM_CHUNK), 0.5 * GM_CHUNK ** -0.5),
        "gm_b_s": gain((L, GM_HEADS, GM_CHUNK)),
        "gm_out_norm": gain((L, GM_WIDTH)),
        "conv_w": nrm((L, SSM_CONV, SSM_CONV_DIM), SSM_CONV ** -0.5),
        "conv_b": nrm((L, SSM_CONV_DIM), 0.02),
        "dt_bias": dt_bias,
        "a_log": a_log,
        "d_skip": gain((L, SSM_HEADS)),
        "ssm_norm": gain((L, SSM_WIDTH)),
        "w_out": nrm((L, D_MIX, D_MODEL), D_MIX ** -0.5),
        "ffn2_norm": gain((L, D_MODEL)),
        "ffn2_w_gate": nrm((L, D_MODEL, D_FF), D_MODEL ** -0.5),
        "ffn2_w_up": nrm((L, D_MODEL, D_FF), D_MODEL ** -0.5),
        "ffn2_w_down": nrm((L, D_FF, D_MODEL), D_FF ** -0.5),
        "ple_norm": gain((L, D_MODEL)),
        "ple_w_gate": nrm((L, D_MODEL, D_MODEL), D_MODEL ** -0.5),
        "ple_b_gate": nrm((L, D_MODEL), 0.02),
        "ple_w_proj": nrm((L, D_PLE, D_MODEL), D_PLE ** -0.5),
        "final_norm": gain((D_MODEL,)),
    }


def _fwd_reference(x, p, ffn1_norm, ffn1_w_gate, ffn1_w_up, ffn1_w_down, mix_norm, w_in,
              gm_ln_g, gm_ln_b, gm_w_s, gm_b_s, gm_out_norm, conv_w, conv_b, dt_bias, a_log,
              d_skip, ssm_norm, w_out, ffn2_norm, ffn2_w_gate, ffn2_w_up, ffn2_w_down,
              ple_norm, ple_w_gate, ple_b_gate, ple_w_proj, final_norm):
    h = x
    for i in range(DEPTH):
        h = h + 0.5 * swiglu(rmsnorm(h, ffn1_norm[i]), ffn1_w_gate[i], ffn1_w_up[i], ffn1_w_down[i])
        n = rmsnorm(h, mix_norm[i])
        proj = n @ w_in[i]
        u, v, z, xbc, dt_raw = jnp.split(proj, IN_SPLITS, axis=-1)
        ya = chunked_spatial_gating(jax.nn.gelu(u, approximate=False), jax.nn.gelu(v, approximate=False),
                                    gm_ln_g[i], gm_ln_b[i], gm_w_s[i], gm_b_s[i])
        ya = rmsnorm(ya, gm_out_norm[i])
        yb = mamba2_mixer(z, xbc, dt_raw, conv_w[i], conv_b[i], dt_bias[i], a_log[i], d_skip[i], ssm_norm[i])
        h = h + jnp.concatenate([ya, yb], axis=-1) @ w_out[i]
        h = h + 0.5 * swiglu(rmsnorm(h, ffn2_norm[i]), ffn2_w_gate[i], ffn2_w_up[i], ffn2_w_down[i])
        gate = jax.nn.sigmoid(rmsnorm(h, ple_norm[i]) @ ple_w_gate[i] + ple_b_gate[i])
        h = h + gate * (p[i] @ ple_w_proj[i])
    return rmsnorm(h, final_norm)


import jax as _jax
import jax.numpy as _jnp

TWIN_FORMAT = 'train_step'
FWD_PARAMS = ['x', 'p', 'ffn1_norm', 'ffn1_w_gate', 'ffn1_w_up', 'ffn1_w_down', 'mix_norm', 'w_in', 'gm_ln_g', 'gm_ln_b', 'gm_w_s', 'gm_b_s', 'gm_out_norm', 'conv_w', 'conv_b', 'dt_bias', 'a_log', 'd_skip', 'ssm_norm', 'w_out', 'ffn2_norm', 'ffn2_w_gate', 'ffn2_w_up', 'ffn2_w_down', 'ple_norm', 'ple_w_gate', 'ple_b_gate', 'ple_w_proj', 'final_norm']
TWIN_WEIGHTS = ['ffn1_norm', 'ffn1_w_gate', 'ffn1_w_up', 'ffn1_w_down', 'mix_norm', 'w_in', 'gm_ln_g', 'gm_ln_b', 'gm_w_s', 'gm_b_s', 'gm_out_norm', 'conv_w', 'conv_b', 'dt_bias', 'a_log', 'd_skip', 'ssm_norm', 'w_out', 'ffn2_norm', 'ffn2_w_gate', 'ffn2_w_up', 'ffn2_w_down', 'ple_norm', 'ple_w_gate', 'ple_b_gate', 'ple_w_proj', 'final_norm']
TWIN_DIFF_INPUT = 'x'
TWIN_INPUTS = ['x', 'p', 'ffn1_norm', 'ffn1_w_gate', 'ffn1_w_up', 'ffn1_w_down', 'mix_norm', 'w_in', 'gm_ln_g', 'gm_ln_b', 'gm_w_s', 'gm_b_s', 'gm_out_norm', 'conv_w', 'conv_b', 'dt_bias', 'a_log', 'd_skip', 'ssm_norm', 'w_out', 'ffn2_norm', 'ffn2_w_gate', 'ffn2_w_up', 'ffn2_w_down', 'ple_norm', 'ple_w_gate', 'ple_b_gate', 'ple_w_proj', 'final_norm', 'loss_target', 'm_ffn1_norm', 'm_ffn1_w_gate', 'm_ffn1_w_up', 'm_ffn1_w_down', 'm_mix_norm', 'm_w_in', 'm_gm_ln_g', 'm_gm_ln_b', 'm_gm_w_s', 'm_gm_b_s', 'm_gm_out_norm', 'm_conv_w', 'm_conv_b', 'm_dt_bias', 'm_a_log', 'm_d_skip', 'm_ssm_norm', 'm_w_out', 'm_ffn2_norm', 'm_ffn2_w_gate', 'm_ffn2_w_up', 'm_ffn2_w_down', 'm_ple_norm', 'm_ple_w_gate', 'm_ple_b_gate', 'm_ple_w_proj', 'm_final_norm', 'v_ffn1_norm', 'v_ffn1_w_gate', 'v_ffn1_w_up', 'v_ffn1_w_down', 'v_mix_norm', 'v_w_in', 'v_gm_ln_g', 'v_gm_ln_b', 'v_gm_w_s', 'v_gm_b_s', 'v_gm_out_norm', 'v_conv_w', 'v_conv_b', 'v_dt_bias', 'v_a_log', 'v_d_skip', 'v_ssm_norm', 'v_w_out', 'v_ffn2_norm', 'v_ffn2_w_gate', 'v_ffn2_w_up', 'v_ffn2_w_down', 'v_ple_norm', 'v_ple_w_gate', 'v_ple_b_gate', 'v_ple_w_proj', 'v_final_norm']
TWIN_OUTPUTS = ['loss', 'grad_x', 'grad_ffn1_norm', 'grad_ffn1_w_gate', 'grad_ffn1_w_up', 'grad_ffn1_w_down', 'grad_mix_norm', 'grad_w_in', 'grad_gm_ln_g', 'grad_gm_ln_b', 'grad_gm_w_s', 'grad_gm_b_s', 'grad_gm_out_norm', 'grad_conv_w', 'grad_conv_b', 'grad_dt_bias', 'grad_a_log', 'grad_d_skip', 'grad_ssm_norm', 'grad_w_out', 'grad_ffn2_norm', 'grad_ffn2_w_gate', 'grad_ffn2_w_up', 'grad_ffn2_w_down', 'grad_ple_norm', 'grad_ple_w_gate', 'grad_ple_b_gate', 'grad_ple_w_proj', 'grad_final_norm', 'delta_ffn1_norm', 'delta_ffn1_w_gate', 'delta_ffn1_w_up', 'delta_ffn1_w_down', 'delta_mix_norm', 'delta_w_in', 'delta_gm_ln_g', 'delta_gm_ln_b', 'delta_gm_w_s', 'delta_gm_b_s', 'delta_gm_out_norm', 'delta_conv_w', 'delta_conv_b', 'delta_dt_bias', 'delta_a_log', 'delta_d_skip', 'delta_ssm_norm', 'delta_w_out', 'delta_ffn2_norm', 'delta_ffn2_w_gate', 'delta_ffn2_w_up', 'delta_ffn2_w_down', 'delta_ple_norm', 'delta_ple_w_gate', 'delta_ple_b_gate', 'delta_ple_w_proj', 'delta_final_norm', 'new_m_ffn1_norm', 'new_m_ffn1_w_gate', 'new_m_ffn1_w_up', 'new_m_ffn1_w_down', 'new_m_mix_norm', 'new_m_w_in', 'new_m_gm_ln_g', 'new_m_gm_ln_b', 'new_m_gm_w_s', 'new_m_gm_b_s', 'new_m_gm_out_norm', 'new_m_conv_w', 'new_m_conv_b', 'new_m_dt_bias', 'new_m_a_log', 'new_m_d_skip', 'new_m_ssm_norm', 'new_m_w_out', 'new_m_ffn2_norm', 'new_m_ffn2_w_gate', 'new_m_ffn2_w_up', 'new_m_ffn2_w_down', 'new_m_ple_norm', 'new_m_ple_w_gate', 'new_m_ple_b_gate', 'new_m_ple_w_proj', 'new_m_final_norm', 'new_v_ffn1_norm', 'new_v_ffn1_w_gate', 'new_v_ffn1_w_up', 'new_v_ffn1_w_down', 'new_v_mix_norm', 'new_v_w_in', 'new_v_gm_ln_g', 'new_v_gm_ln_b', 'new_v_gm_w_s', 'new_v_gm_b_s', 'new_v_gm_out_norm', 'new_v_conv_w', 'new_v_conv_b', 'new_v_dt_bias', 'new_v_a_log', 'new_v_d_skip', 'new_v_ssm_norm', 'new_v_w_out', 'new_v_ffn2_norm', 'new_v_ffn2_w_gate', 'new_v_ffn2_w_up', 'new_v_ffn2_w_down', 'new_v_ple_norm', 'new_v_ple_w_gate', 'new_v_ple_b_gate', 'new_v_ple_w_proj', 'new_v_final_norm']
TWIN_LEAF_KINDS = {'loss': 'loss', 'grad_x': 'grad_x', 'grad_ffn1_norm': 'grad_w', 'grad_ffn1_w_gate': 'grad_w', 'grad_ffn1_w_up': 'grad_w', 'grad_ffn1_w_down': 'grad_w', 'grad_mix_norm': 'grad_w', 'grad_w_in': 'grad_w', 'grad_gm_ln_g': 'grad_w', 'grad_gm_ln_b': 'grad_w', 'grad_gm_w_s': 'grad_w', 'grad_gm_b_s': 'grad_w', 'grad_gm_out_norm': 'grad_w', 'grad_conv_w': 'grad_w', 'grad_conv_b': 'grad_w', 'grad_dt_bias': 'grad_w', 'grad_a_log': 'grad_w', 'grad_d_skip': 'grad_w', 'grad_ssm_norm': 'grad_w', 'grad_w_out': 'grad_w', 'grad_ffn2_norm': 'grad_w', 'grad_ffn2_w_gate': 'grad_w', 'grad_ffn2_w_up': 'grad_w', 'grad_ffn2_w_down': 'grad_w', 'grad_ple_norm': 'grad_w', 'grad_ple_w_gate': 'grad_w', 'grad_ple_b_gate': 'grad_w', 'grad_ple_w_proj': 'grad_w', 'grad_final_norm': 'grad_w', 'delta_ffn1_norm': 'delta_w', 'delta_ffn1_w_gate': 'delta_w', 'delta_ffn1_w_up': 'delta_w', 'delta_ffn1_w_down': 'delta_w', 'delta_mix_norm': 'delta_w', 'delta_w_in': 'delta_w', 'delta_gm_ln_g': 'delta_w', 'delta_gm_ln_b': 'delta_w', 'delta_gm_w_s': 'delta_w', 'delta_gm_b_s': 'delta_w', 'delta_gm_out_norm': 'delta_w', 'delta_conv_w': 'delta_w', 'delta_conv_b': 'delta_w', 'delta_dt_bias': 'delta_w', 'delta_a_log': 'delta_w', 'delta_d_skip': 'delta_w', 'delta_ssm_norm': 'delta_w', 'delta_w_out': 'delta_w', 'delta_ffn2_norm': 'delta_w', 'delta_ffn2_w_gate': 'delta_w', 'delta_ffn2_w_up': 'delta_w', 'delta_ffn2_w_down': 'delta_w', 'delta_ple_norm': 'delta_w', 'delta_ple_w_gate': 'delta_w', 'delta_ple_b_gate': 'delta_w', 'delta_ple_w_proj': 'delta_w', 'delta_final_norm': 'delta_w', 'new_m_ffn1_norm': 'new_m', 'new_m_ffn1_w_gate': 'new_m', 'new_m_ffn1_w_up': 'new_m', 'new_m_ffn1_w_down': 'new_m', 'new_m_mix_norm': 'new_m', 'new_m_w_in': 'new_m', 'new_m_gm_ln_g': 'new_m', 'new_m_gm_ln_b': 'new_m', 'new_m_gm_w_s': 'new_m', 'new_m_gm_b_s': 'new_m', 'new_m_gm_out_norm': 'new_m', 'new_m_conv_w': 'new_m', 'new_m_conv_b': 'new_m', 'new_m_dt_bias': 'new_m', 'new_m_a_log': 'new_m', 'new_m_d_skip': 'new_m', 'new_m_ssm_norm': 'new_m', 'new_m_w_out': 'new_m', 'new_m_ffn2_norm': 'new_m', 'new_m_ffn2_w_gate': 'new_m', 'new_m_ffn2_w_up': 'new_m', 'new_m_ffn2_w_down': 'new_m', 'new_m_ple_norm': 'new_m', 'new_m_ple_w_gate': 'new_m', 'new_m_ple_b_gate': 'new_m', 'new_m_ple_w_proj': 'new_m', 'new_m_final_norm': 'new_m', 'new_v_ffn1_norm': 'new_v', 'new_v_ffn1_w_gate': 'new_v', 'new_v_ffn1_w_up': 'new_v', 'new_v_ffn1_w_down': 'new_v', 'new_v_mix_norm': 'new_v', 'new_v_w_in': 'new_v', 'new_v_gm_ln_g': 'new_v', 'new_v_gm_ln_b': 'new_v', 'new_v_gm_w_s': 'new_v', 'new_v_gm_b_s': 'new_v', 'new_v_gm_out_norm': 'new_v', 'new_v_conv_w': 'new_v', 'new_v_conv_b': 'new_v', 'new_v_dt_bias': 'new_v', 'new_v_a_log': 'new_v', 'new_v_d_skip': 'new_v', 'new_v_ssm_norm': 'new_v', 'new_v_w_out': 'new_v', 'new_v_ffn2_norm': 'new_v', 'new_v_ffn2_w_gate': 'new_v', 'new_v_ffn2_w_up': 'new_v', 'new_v_ffn2_w_down': 'new_v', 'new_v_ple_norm': 'new_v', 'new_v_ple_w_gate': 'new_v', 'new_v_ple_b_gate': 'new_v', 'new_v_ple_w_proj': 'new_v', 'new_v_final_norm': 'new_v'}


def _forward(args):
    return _fwd_reference(*[args[k] for k in FWD_PARAMS])


def _output_shape():
    out = _jax.eval_shape(lambda: _forward(_fwd_setup_inputs(0)))
    return out.shape, out.dtype

N_MICROBATCH = 1
ADAM_LR = 0.001
ADAM_B1 = 0.9
ADAM_B2 = 0.999
ADAM_EPS = 1e-08
ADAM_WD = 0.01
ADAM_STEP = 10
PER_EXAMPLE_BATCH_AXIS = {'x': 0, 'p': 1, 'loss_target': 0}
SHARED_INPUTS = []
_WEIGHT_DTYPES = {'ffn1_norm': _jnp.float32, 'ffn1_w_gate': _jnp.float32, 'ffn1_w_up': _jnp.float32, 'ffn1_w_down': _jnp.float32, 'mix_norm': _jnp.float32, 'w_in': _jnp.float32, 'gm_ln_g': _jnp.float32, 'gm_ln_b': _jnp.float32, 'gm_w_s': _jnp.float32, 'gm_b_s': _jnp.float32, 'gm_out_norm': _jnp.float32, 'conv_w': _jnp.float32, 'conv_b': _jnp.float32, 'dt_bias': _jnp.float32, 'a_log': _jnp.float32, 'd_skip': _jnp.float32, 'ssm_norm': _jnp.float32, 'w_out': _jnp.float32, 'ffn2_norm': _jnp.float32, 'ffn2_w_gate': _jnp.float32, 'ffn2_w_up': _jnp.float32, 'ffn2_w_down': _jnp.float32, 'ple_norm': _jnp.float32, 'ple_w_gate': _jnp.float32, 'ple_b_gate': _jnp.float32, 'ple_w_proj': _jnp.float32, 'final_norm': _jnp.float32}
MOMENT_SCALE = {'ffn1_norm': 1.180280e-01, 'ffn1_w_gate': 5.171534e-02, 'ffn1_w_up': 5.010704e-02, 'ffn1_w_down': 8.299430e-02, 'mix_norm': 2.327297e-01, 'w_in': 1.076270e-01, 'gm_ln_g': 4.501461e-02, 'gm_ln_b': 4.011162e-02, 'gm_w_s': 8.074279e-02, 'gm_b_s': 1.063634e-01, 'gm_out_norm': 2.855721e-01, 'conv_w': 1.250228e-01, 'conv_b': 3.459868e-01, 'dt_bias': 3.020585e-01, 'a_log': 8.128197e-01, 'd_skip': 7.422878e-01, 'ssm_norm': 1.749992e-01, 'w_out': 3.310176e-01, 'ffn2_norm': 8.343166e-02, 'ffn2_w_gate': 3.235917e-02, 'ffn2_w_up': 3.349133e-02, 'ffn2_w_down': 5.597672e-02, 'ple_norm': 6.656120e-02, 'ple_w_gate': 6.666174e-02, 'ple_b_gate': 4.619996e-01, 'ple_w_proj': 9.558574e-02, 'final_norm': 6.416343e+01}


def _to_microbatches(a, axis):
    t = _jnp.moveaxis(a, axis, 0)
    t = t.reshape((N_MICROBATCH, t.shape[0] // N_MICROBATCH) + t.shape[1:])
    return _jnp.moveaxis(t, 1, axis + 1)


def setup_inputs(seed: int = 0) -> dict:
    inp = _fwd_setup_inputs(seed)
    key = _jax.random.fold_in(_jax.random.key(seed), 7919)
    shape, _ = _output_shape()
    out = dict(inp)
    out["loss_target"] = _jax.random.normal(_jax.random.fold_in(key, 0), shape, _jnp.float32)
    for i, name in enumerate(TWIN_WEIGHTS):
        w = inp[name].astype(_jnp.float32)
        if MOMENT_SCALE is None:
            s = _jnp.sqrt(_jnp.mean(_jnp.square(w)) + 1e-30)
        else:
            s = MOMENT_SCALE[name]
        km, kv = _jax.random.split(_jax.random.fold_in(key, i + 1))
        out[name] = w
        out["m_" + name] = s * _jax.random.normal(km, w.shape, _jnp.float32)
        out["v_" + name] = (s * s) * _jax.random.uniform(kv, w.shape, _jnp.float32, 0.5, 1.5)
    if N_MICROBATCH > 1:
        for name, axis in PER_EXAMPLE_BATCH_AXIS.items():
            out[name] = _to_microbatches(out[name], axis)
    return {'x': out['x'], 'p': out['p'], 'ffn1_norm': out['ffn1_norm'], 'ffn1_w_gate': out['ffn1_w_gate'], 'ffn1_w_up': out['ffn1_w_up'], 'ffn1_w_down': out['ffn1_w_down'], 'mix_norm': out['mix_norm'], 'w_in': out['w_in'], 'gm_ln_g': out['gm_ln_g'], 'gm_ln_b': out['gm_ln_b'], 'gm_w_s': out['gm_w_s'], 'gm_b_s': out['gm_b_s'], 'gm_out_norm': out['gm_out_norm'], 'conv_w': out['conv_w'], 'conv_b': out['conv_b'], 'dt_bias': out['dt_bias'], 'a_log': out['a_log'], 'd_skip': out['d_skip'], 'ssm_norm': out['ssm_norm'], 'w_out': out['w_out'], 'ffn2_norm': out['ffn2_norm'], 'ffn2_w_gate': out['ffn2_w_gate'], 'ffn2_w_up': out['ffn2_w_up'], 'ffn2_w_down': out['ffn2_w_down'], 'ple_norm': out['ple_norm'], 'ple_w_gate': out['ple_w_gate'], 'ple_b_gate': out['ple_b_gate'], 'ple_w_proj': out['ple_w_proj'], 'final_norm': out['final_norm'], 'loss_target': out['loss_target'], 'm_ffn1_norm': out['m_ffn1_norm'], 'm_ffn1_w_gate': out['m_ffn1_w_gate'], 'm_ffn1_w_up': out['m_ffn1_w_up'], 'm_ffn1_w_down': out['m_ffn1_w_down'], 'm_mix_norm': out['m_mix_norm'], 'm_w_in': out['m_w_in'], 'm_gm_ln_g': out['m_gm_ln_g'], 'm_gm_ln_b': out['m_gm_ln_b'], 'm_gm_w_s': out['m_gm_w_s'], 'm_gm_b_s': out['m_gm_b_s'], 'm_gm_out_norm': out['m_gm_out_norm'], 'm_conv_w': out['m_conv_w'], 'm_conv_b': out['m_conv_b'], 'm_dt_bias': out['m_dt_bias'], 'm_a_log': out['m_a_log'], 'm_d_skip': out['m_d_skip'], 'm_ssm_norm': out['m_ssm_norm'], 'm_w_out': out['m_w_out'], 'm_ffn2_norm': out['m_ffn2_norm'], 'm_ffn2_w_gate': out['m_ffn2_w_gate'], 'm_ffn2_w_up': out['m_ffn2_w_up'], 'm_ffn2_w_down': out['m_ffn2_w_down'], 'm_ple_norm': out['m_ple_norm'], 'm_ple_w_gate': out['m_ple_w_gate'], 'm_ple_b_gate': out['m_ple_b_gate'], 'm_ple_w_proj': out['m_ple_w_proj'], 'm_final_norm': out['m_final_norm'], 'v_ffn1_norm': out['v_ffn1_norm'], 'v_ffn1_w_gate': out['v_ffn1_w_gate'], 'v_ffn1_w_up': out['v_ffn1_w_up'], 'v_ffn1_w_down': out['v_ffn1_w_down'], 'v_mix_norm': out['v_mix_norm'], 'v_w_in': out['v_w_in'], 'v_gm_ln_g': out['v_gm_ln_g'], 'v_gm_ln_b': out['v_gm_ln_b'], 'v_gm_w_s': out['v_gm_w_s'], 'v_gm_b_s': out['v_gm_b_s'], 'v_gm_out_norm': out['v_gm_out_norm'], 'v_conv_w': out['v_conv_w'], 'v_conv_b': out['v_conv_b'], 'v_dt_bias': out['v_dt_bias'], 'v_a_log': out['v_a_log'], 'v_d_skip': out['v_d_skip'], 'v_ssm_norm': out['v_ssm_norm'], 'v_w_out': out['v_w_out'], 'v_ffn2_norm': out['v_ffn2_norm'], 'v_ffn2_w_gate': out['v_ffn2_w_gate'], 'v_ffn2_w_up': out['v_ffn2_w_up'], 'v_ffn2_w_down': out['v_ffn2_w_down'], 'v_ple_norm': out['v_ple_norm'], 'v_ple_w_gate': out['v_ple_w_gate'], 'v_ple_b_gate': out['v_ple_b_gate'], 'v_ple_w_proj': out['v_ple_w_proj'], 'v_final_norm': out['v_final_norm']}


def _loss(weights, diff, rest, loss_target):
    with _jax.named_scope("forward"):
        args = {**rest, TWIN_DIFF_INPUT: diff, **{k: w.astype(_WEIGHT_DTYPES[k]) for k, w in weights.items()}}
        y = _forward(args)
    with _jax.named_scope("loss_head"):
        err = _jnp.square(y.astype(_jnp.float32) - loss_target)
        return 0.5 * _jnp.sum(_jnp.mean(err, axis=-1)) if err.ndim else 0.5 * err


def _adamw(w, g, m, v):
    m = ADAM_B1 * m + (1.0 - ADAM_B1) * g
    v = ADAM_B2 * v + (1.0 - ADAM_B2) * _jnp.square(g)
    m_hat = m / (1.0 - ADAM_B1 ** ADAM_STEP)
    v_hat = v / (1.0 - ADAM_B2 ** ADAM_STEP)
    delta = -ADAM_LR * (m_hat / (_jnp.sqrt(v_hat) + ADAM_EPS) + ADAM_WD * w)
    return delta, m, v


def reference(x, p, ffn1_norm, ffn1_w_gate, ffn1_w_up, ffn1_w_down, mix_norm, w_in, gm_ln_g, gm_ln_b, gm_w_s, gm_b_s, gm_out_norm, conv_w, conv_b, dt_bias, a_log, d_skip, ssm_norm, w_out, ffn2_norm, ffn2_w_gate, ffn2_w_up, ffn2_w_down, ple_norm, ple_w_gate, ple_b_gate, ple_w_proj, final_norm, loss_target, m_ffn1_norm, m_ffn1_w_gate, m_ffn1_w_up, m_ffn1_w_down, m_mix_norm, m_w_in, m_gm_ln_g, m_gm_ln_b, m_gm_w_s, m_gm_b_s, m_gm_out_norm, m_conv_w, m_conv_b, m_dt_bias, m_a_log, m_d_skip, m_ssm_norm, m_w_out, m_ffn2_norm, m_ffn2_w_gate, m_ffn2_w_up, m_ffn2_w_down, m_ple_norm, m_ple_w_gate, m_ple_b_gate, m_ple_w_proj, m_final_norm, v_ffn1_norm, v_ffn1_w_gate, v_ffn1_w_up, v_ffn1_w_down, v_mix_norm, v_w_in, v_gm_ln_g, v_gm_ln_b, v_gm_w_s, v_gm_b_s, v_gm_out_norm, v_conv_w, v_conv_b, v_dt_bias, v_a_log, v_d_skip, v_ssm_norm, v_w_out, v_ffn2_norm, v_ffn2_w_gate, v_ffn2_w_up, v_ffn2_w_down, v_ple_norm, v_ple_w_gate, v_ple_b_gate, v_ple_w_proj, v_final_norm):
    given = dict(x=x, p=p, ffn1_norm=ffn1_norm, ffn1_w_gate=ffn1_w_gate, ffn1_w_up=ffn1_w_up, ffn1_w_down=ffn1_w_down, mix_norm=mix_norm, w_in=w_in, gm_ln_g=gm_ln_g, gm_ln_b=gm_ln_b, gm_w_s=gm_w_s, gm_b_s=gm_b_s, gm_out_norm=gm_out_norm, conv_w=conv_w, conv_b=conv_b, dt_bias=dt_bias, a_log=a_log, d_skip=d_skip, ssm_norm=ssm_norm, w_out=w_out, ffn2_norm=ffn2_norm, ffn2_w_gate=ffn2_w_gate, ffn2_w_up=ffn2_w_up, ffn2_w_down=ffn2_w_down, ple_norm=ple_norm, ple_w_gate=ple_w_gate, ple_b_gate=ple_b_gate, ple_w_proj=ple_w_proj, final_norm=final_norm, loss_target=loss_target, m_ffn1_norm=m_ffn1_norm, m_ffn1_w_gate=m_ffn1_w_gate, m_ffn1_w_up=m_ffn1_w_up, m_ffn1_w_down=m_ffn1_w_down, m_mix_norm=m_mix_norm, m_w_in=m_w_in, m_gm_ln_g=m_gm_ln_g, m_gm_ln_b=m_gm_ln_b, m_gm_w_s=m_gm_w_s, m_gm_b_s=m_gm_b_s, m_gm_out_norm=m_gm_out_norm, m_conv_w=m_conv_w, m_conv_b=m_conv_b, m_dt_bias=m_dt_bias, m_a_log=m_a_log, m_d_skip=m_d_skip, m_ssm_norm=m_ssm_norm, m_w_out=m_w_out, m_ffn2_norm=m_ffn2_norm, m_ffn2_w_gate=m_ffn2_w_gate, m_ffn2_w_up=m_ffn2_w_up, m_ffn2_w_down=m_ffn2_w_down, m_ple_norm=m_ple_norm, m_ple_w_gate=m_ple_w_gate, m_ple_b_gate=m_ple_b_gate, m_ple_w_proj=m_ple_w_proj, m_final_norm=m_final_norm, v_ffn1_norm=v_ffn1_norm, v_ffn1_w_gate=v_ffn1_w_gate, v_ffn1_w_up=v_ffn1_w_up, v_ffn1_w_down=v_ffn1_w_down, v_mix_norm=v_mix_norm, v_w_in=v_w_in, v_gm_ln_g=v_gm_ln_g, v_gm_ln_b=v_gm_ln_b, v_gm_w_s=v_gm_w_s, v_gm_b_s=v_gm_b_s, v_gm_out_norm=v_gm_out_norm, v_conv_w=v_conv_w, v_conv_b=v_conv_b, v_dt_bias=v_dt_bias, v_a_log=v_a_log, v_d_skip=v_d_skip, v_ssm_norm=v_ssm_norm, v_w_out=v_w_out, v_ffn2_norm=v_ffn2_norm, v_ffn2_w_gate=v_ffn2_w_gate, v_ffn2_w_up=v_ffn2_w_up, v_ffn2_w_down=v_ffn2_w_down, v_ple_norm=v_ple_norm, v_ple_w_gate=v_ple_w_gate, v_ple_b_gate=v_ple_b_gate, v_ple_w_proj=v_ple_w_proj, v_final_norm=v_final_norm)
    weights = {n: given[n] for n in TWIN_WEIGHTS}
    shared = {n: given[n] for n in SHARED_INPUTS}
    per_example = {n: given[n] for n in ['x', 'p']}
    grad_fn = _jax.value_and_grad(_loss, argnums=(0, 1))

    def one_microbatch(ex, loss_target):
        ex = dict(ex)
        diff = ex.pop(TWIN_DIFF_INPUT)
        return grad_fn(weights, diff, {**shared, **ex}, loss_target)

    if N_MICROBATCH == 1:
        loss, (grad_w, grad_x) = one_microbatch(per_example, given["loss_target"])
    else:
        def body(carry, xs):
            loss_sum, grad_sum = carry
            l_k, (gw_k, gx_k) = one_microbatch(xs[0], xs[1])
            with _jax.named_scope("update"):
                return (loss_sum + l_k, _jax.tree.map(_jnp.add, grad_sum, gw_k)), gx_k

        init = (_jnp.zeros((), _jnp.float32), _jax.tree.map(_jnp.zeros_like, weights))
        (loss, grad_w), grad_x = _jax.lax.scan(body, init, (per_example, given["loss_target"]))
    with _jax.named_scope("update"):
        delta_w, new_m, new_v = {}, {}, {}
        for n in TWIN_WEIGHTS:
            delta_w[n], new_m[n], new_v[n] = _adamw(weights[n], grad_w[n], given["m_" + n], given["v_" + n])
    return (loss, grad_x, *[grad_w[n] for n in TWIN_WEIGHTS], *[delta_w[n] for n in TWIN_WEIGHTS],
            *[new_m[n] for n in TWIN_WEIGHTS], *[new_v[n] for n in TWIN_WEIGHTS])
```

```python
import functools

import jax
import jax.numpy as jnp
from jax import lax
from jax.experimental import pallas as pl
from jax.experimental.pallas import tpu as pltpu

f32 = jnp.float32
bf16 = jnp.bfloat16
MESH = pl.DeviceIdType.MESH
HIGHEST = lax.Precision.HIGHEST

EPS = 1e-6
N_CHIPS = 4
N_DEV = 8
D_MODEL = 1024
D_FF = 2816
D_PLE = 256
GM_WIDTH = 1024
GM_HEADS = 8
CHUNK = 128
SSM_WIDTH = 1024
SSM_HEADS = 16
SSM_HEAD_DIM = 64
SSM_GROUPS = 2
SSM_STATE = 128
SSM_CONV = 4
CONV_DIM = SSM_WIDTH + 2 * SSM_GROUPS * SSM_STATE
IN_PROJ = 2 * GM_WIDTH + SSM_WIDTH + CONV_DIM + SSM_HEADS
LANES = 128
ZXD = SSM_WIDTH + CONV_DIM + LANES

ADAM_LR = 0.001
ADAM_B1 = 0.9
ADAM_B2 = 0.999
ADAM_EPS = 1e-08
ADAM_WD = 0.01
ADAM_STEP = 10

VMEM_LIMIT = 56 * 1024 * 1024

PACK_C = 512
PACK_SPLIT = 16 * 2 * 4


def _dot(a, b):
    return jnp.dot(a, b, preferred_element_type=f32)


def _dot_nt(a, b):
    return lax.dot_general(a, b, (((1,), (1,)), ((), ())), preferred_element_type=f32)


def _dot_tn(a, b):
    return lax.dot_general(a, b, (((0,), (0,)), ((), ())), preferred_element_type=f32)


def _rms(x, g):
    return x * lax.rsqrt(jnp.mean(x * x, axis=-1, keepdims=True) + EPS) * g


def _gelu(x):
    return 0.5 * x * (1.0 + lax.erf(x * 0.7071067811865476))


def _layernorm(x, g, b):
    mu = jnp.mean(x, axis=-1, keepdims=True)
    xc = x - mu
    return xc * lax.rsqrt(jnp.mean(xc * xc, axis=-1, keepdims=True) + EPS) * g + b


def _sigmoid(x):
    return 1.0 / (1.0 + jnp.exp(-x))


def _softplus(x):
    return jnp.maximum(x, 0.0) + jnp.log(1.0 + jnp.exp(-jnp.abs(x)))


def _full(shape):
    nd = len(shape)
    return pl.BlockSpec(shape, lambda *_: (0,) * nd, pipeline_mode=pl.Buffered(1))


def _acc(shape):
    nd = len(shape)
    return pl.BlockSpec(shape, lambda *_: (0,) * nd)


def _rows(tm, ncols):
    return pl.BlockSpec((tm, ncols), lambda i: (i, 0))


def _params(sem):
    return pltpu.CompilerParams(dimension_semantics=sem, vmem_limit_bytes=VMEM_LIMIT)


def _row_tile(rows, target, mult=8):
    best = rows
    for t in range(mult, min(rows, target) + 1, mult):
        if rows % t == 0:
            best = t
    return best if best <= target else rows


def _ffn_fwd(name, h, g, wg, wu, wd, pre=None, tm=256, fc=256):
    T, D = h.shape
    F = wg.shape[1]
    tm = min(tm, T)

    def body(*refs):
        if pre is None:
            h_ref, g_ref, wg_ref, wu_ref, wd_ref, ho_ref, n_ref, a_ref, b_ref = refs
            hin = h_ref[...]
        else:
            (h_ref, ya_ref, yb_ref, woa_ref, wob_ref, g_ref, wg_ref, wu_ref, wd_ref,
             hi_ref, ho_ref, n_ref, a_ref, b_ref) = refs
            hin = h_ref[...] + _dot(ya_ref[...], woa_ref[...]) + _dot(yb_ref[...], wob_ref[...])
            hi_ref[...] = hin
        n = _rms(hin, g_ref[...]).astype(bf16)
        n_ref[...] = n
        acc = jnp.zeros((tm, D), f32)
        for c in range(F // fc):
            sl = slice(c * fc, (c + 1) * fc)
            a = _dot(n, wg_ref[:, sl]).astype(bf16)
            b = _dot(n, wu_ref[:, sl]).astype(bf16)
            a_ref[:, sl] = a
            b_ref[:, sl] = b
            af = a.astype(f32)
            hm = (af * _sigmoid(af) * b.astype(f32)).astype(bf16)
            acc = acc + _dot(hm, wd_ref[sl, :])
        ho_ref[...] = hin + 0.5 * acc

    ins = [h] + (list(pre) if pre is not None else []) + [g, wg, wu, wd]
    in_specs = [_rows(tm, D)]
    if pre is not None:
        in_specs += [_rows(tm, pre[0].shape[1]), _rows(tm, pre[1].shape[1]), _full(pre[2].shape), _full(pre[3].shape)]
    in_specs += [_full(g.shape), _full(wg.shape), _full(wu.shape), _full(wd.shape)]
    outs = [jax.ShapeDtypeStruct((T, D), f32), jax.ShapeDtypeStruct((T, D), bf16),
            jax.ShapeDtypeStruct((T, F), bf16), jax.ShapeDtypeStruct((T, F), bf16)]
    out_specs = [_rows(tm, D), _rows(tm, D), _rows(tm, F), _rows(tm, F)]
    if pre is not None:
        outs = [jax.ShapeDtypeStruct((T, D), f32)] + outs
        out_specs = [_rows(tm, D)] + out_specs
    return pl.pallas_call(body, name=name, grid=(T // tm,), in_specs=in_specs, out_specs=out_specs,
                          out_shape=outs, compiler_params=_params(("parallel",)))(*ins)


def _ffn_bwd(name, dh, hin, g, a, b, wg, wu, wd, post=None, tm=256, fc=256):
    T, D = dh.shape
    F = wg.shape[1]
    tm = min(tm, T)

    def body(*refs):
        if post is None:
            (dh_ref, hin_ref, g_ref, a_ref, b_ref, wg_ref, wu_ref, wd_ref,
             dhi_ref, da_ref, db_ref, hm_ref, dg_ref) = refs
        else:
            (dh_ref, hin_ref, g_ref, a_ref, b_ref, wg_ref, wu_ref, wd_ref, woa_ref, wob_ref,
             dhi_ref, da_ref, db_ref, hm_ref, dg_ref, dya_ref, dyb_ref) = refs

        @pl.when(pl.program_id(0) == 0)
        def _():
            dg_ref[...] = jnp.zeros_like(dg_ref)

        dh_ = dh_ref[...]
        dhb = (0.5 * dh_).astype(bf16)
        dn = jnp.zeros((tm, D), f32)
        for c in range(F // fc):
            sl = slice(c * fc, (c + 1) * fc)
            dhm = _dot_nt(dhb, wd_ref[sl, :])
            af = a_ref[:, sl].astype(f32)
            bf = b_ref[:, sl].astype(f32)
            sg = _sigmoid(af)
            sl_ = af * sg
            da = (dhm * bf * (sg * (1.0 + af * (1.0 - sg)))).astype(bf16)
            db = (dhm * sl_).astype(bf16)
            da_ref[:, sl] = da
            db_ref[:, sl] = db
            hm_ref[:, sl] = (sl_ * bf).astype(bf16)
            dn = dn + _dot_nt(da, wg_ref[:, sl]) + _dot_nt(db, wu_ref[:, sl])
        _, vjp = jax.vjp(_rms, hin_ref[...], g_ref[...])
        dx, dg = vjp(dn)
        dhi = dh_ + dx
        dhi_ref[...] = dhi
        dg_ref[...] += dg
        if post is not None:
            dhib = dhi.astype(bf16)
            dya_ref[...] = _dot_nt(dhib, woa_ref[...]).astype(bf16)
            dyb_ref[...] = _dot_nt(dhib, wob_ref[...]).astype(bf16)

    ins = [dh, hin, g, a, b, wg, wu, wd]
    in_specs = [_rows(tm, D), _rows(tm, D), _full(g.shape), _rows(tm, F), _rows(tm, F),
                _full(wg.shape), _full(wu.shape), _full(wd.shape)]
    outs = [jax.ShapeDtypeStruct((T, D), f32), jax.ShapeDtypeStruct((T, F), bf16), jax.ShapeDtypeStruct((T, F), bf16),
            jax.ShapeDtypeStruct((T, F), bf16), jax.ShapeDtypeStruct(g.shape, f32)]
    out_specs = [_rows(tm, D), _rows(tm, F), _rows(tm, F), _rows(tm, F), _acc(g.shape)]
    if post is not None:
        ins += list(post)
        in_specs += [_full(post[0].shape), _full(post[1].shape)]
        outs += [jax.ShapeDtypeStruct((T, post[0].shape[0]), bf16), jax.ShapeDtypeStruct((T, post[1].shape[0]), bf16)]
        out_specs += [_rows(tm, post[0].shape[0]), _rows(tm, post[1].shape[0])]
    return pl.pallas_call(body, name=name, grid=(T // tm,), in_specs=in_specs, out_specs=out_specs,
                          out_shape=outs, compiler_params=_params(("arbitrary",)))(*ins)


def _matmul_tn(name, a, b, scale=1.0, tk=1024, tn=None):
    T, M = a.shape
    N = b.shape[1]
    tk = min(tk, T)
    if tn is None:
        tn = LANES * max(d for d in range(1, N // LANES + 1) if (N // LANES) % d == 0 and (d == 1 or M * d * LANES * 4 <= 6 * 1024 * 1024))
    nk = T // tk

    def body(a_ref, b_ref, o_ref):
        k = pl.program_id(1)

        @pl.when(k == 0)
        def _():
            o_ref[...] = jnp.zeros_like(o_ref)

        bb = b_ref[...]
        if scale != 1.0:
            bb = bb * scale
        o_ref[...] += _dot_tn(a_ref[...].astype(bf16), bb.astype(bf16))

    return pl.pallas_call(
        body, name=name, grid=(N // tn, nk),
        in_specs=[pl.BlockSpec((tk, M), lambda j, k: (k, 0)), pl.BlockSpec((tk, tn), lambda j, k: (k, j))],
        out_specs=pl.BlockSpec((M, tn), lambda j, k: (0, j)),
        out_shape=jax.ShapeDtypeStruct((M, N), f32),
        compiler_params=_params(("parallel", "arbitrary")))(a, b)


def _gm_pre(u, v, ln_g, ln_b):
    return _gelu(u), _layernorm(_gelu(v), ln_g, ln_b)


def _tril_mask():
    r = lax.broadcasted_iota(jnp.int32, (CHUNK, CHUNK), 0)
    c = lax.broadcasted_iota(jnp.int32, (CHUNK, CHUNK), 1)
    return c <= r


def _gm_mix(vnb, ws_ref, bst, mixed_sc, tm):
    mask = _tril_mask()
    for h in range(GM_HEADS):
        wt = jnp.where(mask, ws_ref[h], 0.0).astype(bf16)
        bias = bst[:, h:h + 1]
        for q in range(tm // CHUNK):
            rs = slice(q * CHUNK, (q + 1) * CHUNK)
            cs = slice(h * CHUNK, (h + 1) * CHUNK)
            mixed_sc[rs, cs] = _dot(wt, vnb[rs, cs]) + bias


def _mix_fwd(h1, gmix, w_uv, w_zxd, ln_g, ln_b, w_s, b_st, gout, tm=512):
    T, D = h1.shape
    tm = min(tm, T)
    G = GM_WIDTH

    def body(h_ref, g_ref, wuv_ref, wzxd_ref, lng_ref, lnb_ref, ws_ref, bst_ref, gout_ref,
             n_ref, uv_ref, z_ref, xbc_ref, dt_ref, ya_ref, mixed_sc):
        n = _rms(h_ref[...], g_ref[...]).astype(bf16)
        n_ref[...] = n
        u = _dot(n, wuv_ref[:, :G]).astype(bf16)
        v = _dot(n, wuv_ref[:, G:]).astype(bf16)
        uv_ref[:, :G] = u
        uv_ref[:, G:] = v
        z_ref[...] = _dot(n, wzxd_ref[:, :SSM_WIDTH]).astype(bf16)
        xbc_ref[...] = _dot(n, wzxd_ref[:, SSM_WIDTH:SSM_WIDTH + CONV_DIM]).astype(bf16)
        dt_ref[...] = _dot(n, wzxd_ref[:, SSM_WIDTH + CONV_DIM:])
        ug, vn = _gm_pre(u.astype(f32), v.astype(f32), lng_ref[...], lnb_ref[...])
        _gm_mix(vn.astype(bf16), ws_ref, bst_ref[...], mixed_sc, tm)
        ya_ref[...] = _rms(ug * mixed_sc[...], gout_ref[...]).astype(bf16)

    ins = [h1, gmix, w_uv, w_zxd, ln_g, ln_b, w_s, b_st, gout]
    in_specs = [_rows(tm, D)] + [_full(x.shape) for x in ins[1:]]
    outs = [jax.ShapeDtypeStruct((T, D), bf16), jax.ShapeDtypeStruct((T, 2 * G), bf16),
            jax.ShapeDtypeStruct((T, SSM_WIDTH), bf16), jax.ShapeDtypeStruct((T, CONV_DIM), bf16),
            jax.ShapeDtypeStruct((T, LANES), f32), jax.ShapeDtypeStruct((T, G), bf16)]
    out_specs = [_rows(tm, D), _rows(tm, 2 * G), _rows(tm, SSM_WIDTH), _rows(tm, CONV_DIM), _rows(tm, LANES), _rows(tm, G)]
    return pl.pallas_call(body, name="mix_fwd", grid=(T // tm,), in_specs=in_specs, out_specs=out_specs,
                          out_shape=outs, scratch_shapes=[pltpu.VMEM((tm, G), f32)],
                          compiler_params=_params(("parallel",)))(*ins)


def _gm_bwd(uv, dya, ln_g, ln_b, w_s, b_st, gout, tm=256):
    T = uv.shape[0]
    tm = min(tm, T)
    G = GM_WIDTH

    def body(uv_ref, dya_ref, lng_ref, lnb_ref, ws_ref, bst_ref, gout_ref,
             duv_ref, dlng_ref, dlnb_ref, dws_ref, dbst_ref, dgout_ref, mixed_sc, dvn_sc):
        @pl.when(pl.program_id(0) == 0)
        def _():
            for r in (dlng_ref, dlnb_ref, dws_ref, dbst_ref, dgout_ref):
                r[...] = jnp.zeros_like(r)

        u = uv_ref[:, :G].astype(f32)
        v = uv_ref[:, G:].astype(f32)
        (ug, vn), pre_vjp = jax.vjp(_gm_pre, u, v, lng_ref[...], lnb_ref[...])
        vnb = vn.astype(bf16)
        _gm_mix(vnb, ws_ref, bst_ref[...], mixed_sc, tm)
        mixed = mixed_sc[...]
        _, out_vjp = jax.vjp(_rms, ug * mixed, gout_ref[...])
        dpre, dgout = out_vjp(dya_ref[...].astype(f32))
        dgout_ref[...] += dgout
        dug = dpre * mixed
        dmixed = dpre * ug
        mask = _tril_mask()
        lane = lax.broadcasted_iota(jnp.int32, (1, GM_HEADS), 1)
        dbst = jnp.zeros((CHUNK, GM_HEADS), f32)
        for h in range(GM_HEADS):
            wt = jnp.where(mask, ws_ref[h], 0.0).astype(bf16)
            cs = slice(h * CHUNK, (h + 1) * CHUNK)
            dw = jnp.zeros((CHUNK, CHUNK), f32)
            for q in range(tm // CHUNK):
                rs = slice(q * CHUNK, (q + 1) * CHUNK)
                dm = dmixed[rs, cs]
                dmb = dm.astype(bf16)
                dw = dw + _dot_nt(dmb, vnb[rs, cs])
                dbst = dbst + jnp.sum(dm, axis=1, keepdims=True) * (lane == h).astype(f32)
                dvn_sc[rs, cs] = _dot_tn(wt, dmb)
            dws_ref[h] += jnp.where(mask, dw, 0.0)
        dbst_ref[...] += dbst
        du, dv, dlng, dlnb = pre_vjp((dug, dvn_sc[...]))
        duv_ref[:, :G] = du.astype(bf16)
        duv_ref[:, G:] = dv.astype(bf16)
        dlng_ref[...] += dlng
        dlnb_ref[...] += dlnb

    ins = [uv, dya, ln_g, ln_b, w_s, b_st, gout]
    in_specs = [_rows(tm, 2 * G), _rows(tm, G)] + [_full(x.shape) for x in ins[2:]]
    outs = [jax.ShapeDtypeStruct((T, 2 * G), bf16)] + [jax.ShapeDtypeStruct(x.shape, f32) for x in (ln_g, ln_b, w_s, b_st, gout)]
    out_specs = [_rows(tm, 2 * G)] + [_acc(x.shape) for x in (ln_g, ln_b, w_s, b_st, gout)]
    return pl.pallas_call(body, name="gm_bwd", grid=(T // tm,), in_specs=in_specs, out_specs=out_specs,
                          out_shape=outs, scratch_shapes=[pltpu.VMEM((tm, G), f32), pltpu.VMEM((tm, G), f32)],
                          compiler_params=_params(("arbitrary",)))(*ins)


def _mix_bwd(dh, h1, gmix, duv, dzxd, w_uv, w_zxd, tm=512):
    T, D = dh.shape
    tm = min(tm, T)

    def body(dh_ref, h_ref, g_ref, duv_ref, dzxd_ref, wuv_ref, wzxd_ref, dhi_ref, dg_ref):
        @pl.when(pl.program_id(0) == 0)
        def _():
            dg_ref[...] = jnp.zeros_like(dg_ref)

        dn = _dot_nt(duv_ref[...], wuv_ref[...]) + _dot_nt(dzxd_ref[...], wzxd_ref[...])
        _, vjp = jax.vjp(_rms, h_ref[...], g_ref[...])
        dx, dg = vjp(dn)
        dhi_ref[...] = dh_ref[...] + dx
        dg_ref[...] += dg

    ins = [dh, h1, gmix, duv, dzxd, w_uv, w_zxd]
    in_specs = [_rows(tm, D), _rows(tm, D), _full(gmix.shape), _rows(tm, duv.shape[1]), _rows(tm, dzxd.shape[1]),
                _full(w_uv.shape), _full(w_zxd.shape)]
    return pl.pallas_call(body, name="mix_bwd", grid=(T // tm,), in_specs=in_specs,
                          out_specs=[_rows(tm, D), _acc(gmix.shape)],
                          out_shape=[jax.ShapeDtypeStruct((T, D), f32), jax.ShapeDtypeStruct(gmix.shape, f32)],
                          compiler_params=_params(("arbitrary",)))(*ins)


HALO = 16


def _ssd_front(c, xbc_ref, halo_ref, dtr_ref, cw_ref, cb_ref, dtb_ref, alog_ref, ext_sc):
    halo = jnp.where(c > 0, halo_ref[...].astype(f32), 0.0)
    ext_sc[0:HALO, :] = halo
    ext_sc[HALO:HALO + CHUNK, :] = xbc_ref[...].astype(f32)
    xc = cb_ref[...] + jnp.zeros((CHUNK, CONV_DIM), f32)
    for j in range(SSM_CONV):
        xc = xc + cw_ref[j:j + 1, :] * ext_sc[pl.ds(HALO - SSM_CONV + 1 + j, CHUNK), :]
    sg = _sigmoid(xc)
    xa = xc * sg
    dt = _softplus(dtr_ref[...] + dtb_ref[...])
    a = -jnp.exp(alog_ref[...])
    adt = dt * a
    tri = _tril_mask().astype(f32)
    acs = jnp.dot(tri, adt, preferred_element_type=f32, precision=HIGHEST)
    return xc, sg, xa, dt, a, adt, acs


def _ssd_head(h, xa, dt, acs, acs_t, cbm, sp, dsk):
    cmb, bmb, cb = cbm
    P = SSM_HEAD_DIM
    x = xa[:, h * P:(h + 1) * P]
    dtc = dt[:, h:h + 1]
    acs_h = acs[:, h:h + 1]
    diff = acs_h - acs_t[h:h + 1, :]
    dk = jnp.exp(jnp.where(_tril_mask(), diff, -jnp.inf))
    m = cb * dk
    xd = x * dtc
    e = jnp.exp(acs_h)
    z = _dot_nt(cmb, sp.astype(bf16))
    y = _dot(m.astype(bf16), xd.astype(bf16)) + e * z + x * dsk[:, h:h + 1]
    tot = acs[CHUNK - 1:CHUNK, h:h + 1]
    r = jnp.exp(tot - acs_h)
    return x, dtc, dk, m, xd, e, z, y, tot, r


def _gated_norm(y, z, g):
    yg = y * (z * _sigmoid(z))
    half = SSM_WIDTH // SSM_GROUPS
    parts = []
    for k in range(SSM_GROUPS):
        s = yg[:, k * half:(k + 1) * half]
        parts.append(s * lax.rsqrt(jnp.mean(s * s, axis=-1, keepdims=True) + EPS))
    return jnp.concatenate(parts, axis=1) * g


def _group_mats(xa):
    out = []
    for g in range(SSM_GROUPS):
        bm = xa[:, SSM_WIDTH + g * SSM_STATE:SSM_WIDTH + (g + 1) * SSM_STATE].astype(bf16)
        cm = xa[:, SSM_WIDTH + (SSM_GROUPS + g) * SSM_STATE:SSM_WIDTH + (SSM_GROUPS + g + 1) * SSM_STATE].astype(bf16)
        out.append((cm, bm, _dot_nt(cm, bm)))
    return out


def _ssd_in_specs(nc, rev):
    def ci(i):
        return nc - 1 - i if rev else i
    hp = CHUNK // HALO
    return [pl.BlockSpec((CHUNK, CONV_DIM), lambda i: (ci(i), 0)),
            pl.BlockSpec((HALO, CONV_DIM), lambda i: (jnp.maximum(ci(i) * hp - 1, 0), 0)),
            pl.BlockSpec((CHUNK, SSM_WIDTH), lambda i: (ci(i), 0)),
            pl.BlockSpec((CHUNK, LANES), lambda i: (ci(i), 0))]


def _ssd_fwd(xbc, z, dtr, conv_w, conv_b, dt_bias, a_log, d_skip, ssm_norm):
    T = xbc.shape[0]
    nc = T // CHUNK
    H, P, N = SSM_HEADS, SSM_HEAD_DIM, SSM_STATE

    def body(xbc_ref, halo_ref, z_ref, dtr_ref, cw_ref, cb_ref, dtb_ref, alog_ref, dsk_ref, g_ref,
             yb_ref, sprev_ref, s_sc, ext_sc, y_sc):
        c = pl.program_id(0)

        @pl.when(c == 0)
        def _():
            s_sc[...] = jnp.zeros_like(s_sc)

        _, _, xa, dt, _, _, acs = _ssd_front(c, xbc_ref, halo_ref, dtr_ref, cw_ref, cb_ref, dtb_ref, alog_ref, ext_sc)
        acs_t = acs.T
        groups = _group_mats(xa)
        dsk = dsk_ref[...]
        for h in range(H):
            cbm = groups[h // (H // SSM_GROUPS)]
            sp = s_sc[h]
            _, _, _, _, xd, _, _, y, tot, r = _ssd_head(h, xa, dt, acs, acs_t, cbm, sp, dsk)
            y_sc[:, h * P:(h + 1) * P] = y
            sprev_ref[0, h] = sp
            s_sc[h] = jnp.exp(tot) * sp + _dot_tn((xd * r).astype(bf16), cbm[1])
        yb_ref[...] = _gated_norm(y_sc[...], z_ref[...].astype(f32), g_ref[...]).astype(bf16)

    params = [conv_w, conv_b, dt_bias, a_log, d_skip, ssm_norm]
    return pl.pallas_call(
        body, name="ssd_fwd", grid=(nc,),
        in_specs=_ssd_in_specs(nc, False) + [_full(x.shape) for x in params],
        out_specs=[pl.BlockSpec((CHUNK, SSM_WIDTH), lambda i: (i, 0)), pl.BlockSpec((1, H, P, N), lambda i: (i, 0, 0, 0))],
        out_shape=[jax.ShapeDtypeStruct((T, SSM_WIDTH), bf16), jax.ShapeDtypeStruct((nc, H, P, N), f32)],
        scratch_shapes=[pltpu.VMEM((H, P, N), f32), pltpu.VMEM((HALO + CHUNK, CONV_DIM), f32), pltpu.VMEM((CHUNK, SSM_WIDTH), f32)],
        compiler_params=_params(("arbitrary",)))(xbc, xbc, z, dtr, *params)


NEXT = 8


def _ssd_bwd(xbc, z, dtr, sprev, dyb, conv_w, conv_b, dt_bias, a_log, d_skip, ssm_norm):
    T = xbc.shape[0]
    nc = T // CHUNK
    H, P, N = SSM_HEADS, SSM_HEAD_DIM, SSM_STATE
    HG = H // SSM_GROUPS

    def body(xbc_ref, halo_ref, z_ref, dtr_ref, sprev_ref, dyb_ref, cw_ref, cb_ref, dtb_ref, alog_ref, dsk_ref, g_ref,
             dzxd_ref, dcw_ref, dcb_ref, ddtb_ref, dalog_ref, ddsk_ref, dg_ref,
             ds_sc, ext_sc, dxc_sc, y_sc, dxa_sc):
        i = pl.program_id(0)
        c = nc - 1 - i

        @pl.when(i == 0)
        def _():
            ds_sc[...] = jnp.zeros_like(ds_sc)
            dxc_sc[...] = jnp.zeros_like(dxc_sc)
            for r_ in (dcw_ref, dcb_ref, ddtb_ref, dalog_ref, ddsk_ref, dg_ref):
                r_[...] = jnp.zeros_like(r_)

        xc, sg, xa, dt, a, adt, acs = _ssd_front(c, xbc_ref, halo_ref, dtr_ref, cw_ref, cb_ref, dtb_ref, alog_ref, ext_sc)
        acs_t = acs.T
        groups = _group_mats(xa)
        dsk = dsk_ref[...]
        heads = []
        for h in range(H):
            res = _ssd_head(h, xa, dt, acs, acs_t, groups[h // HG], sprev_ref[0, h], dsk)
            y_sc[:, h * P:(h + 1) * P] = res[7]
            heads.append(res)
        zf = z_ref[...].astype(f32)
        _, gn_vjp = jax.vjp(_gated_norm, y_sc[...], zf, g_ref[...])
        dy, dz, dg = gn_vjp(dyb_ref[...].astype(f32))
        dg_ref[...] += dg
        dzxd_ref[:, :SSM_WIDTH] = dz.astype(bf16)

        lane = lax.broadcasted_iota(jnp.int32, (1, LANES), 1)
        sub = lax.broadcasted_iota(jnp.int32, (LANES, 1), 0)
        dacs = jnp.zeros((CHUNK, LANES), f32)
        dacs_r = jnp.zeros((LANES, CHUNK), f32)
        ddt = jnp.zeros((CHUNK, LANES), f32)
        dtot = jnp.zeros((1, LANES), f32)
        ddsk = jnp.zeros((1, LANES), f32)
        dcb = [jnp.zeros((CHUNK, CHUNK), f32) for _ in range(SSM_GROUPS)]
        dcm = [jnp.zeros((CHUNK, N), f32) for _ in range(SSM_GROUPS)]
        dbm = [jnp.zeros((CHUNK, N), f32) for _ in range(SSM_GROUPS)]
        for h in range(H):
            g = h // HG
            cmb, bmb, cb = groups[g]
            x, dtc, dk, m, xd, e, zz, _, tot, r = heads[h]
            sp = sprev_ref[0, h]
            oh = (lane == h).astype(f32)
            dyh = dy[:, h * P:(h + 1) * P]
            dyb_ = dyh.astype(bf16)
            dx = dyh * dsk[:, h:h + 1]
            ddsk = ddsk + jnp.sum(dyh * x) * oh
            dm = _dot_nt(dyb_, xd.astype(bf16))
            dxd = _dot_tn(m.astype(bf16), dyb_)
            w = dm * m
            dac = jnp.sum(w, axis=1, keepdims=True)
            dacs_r = dacs_r - (sub == h).astype(f32) * jnp.sum(w, axis=0, keepdims=True)
            dcb[g] = dcb[g] + dm * dk
            dzz = dyh * e
            dac = dac + jnp.sum(dzz * zz, axis=1, keepdims=True)
            dzb = dzz.astype(bf16)
            dcm[g] = dcm[g] + _dot(dzb, sp.astype(bf16))
            dsp = _dot_tn(dzb, cmb)
            dsn = ds_sc[h]
            dsnb = dsn.astype(bf16)
            et = jnp.exp(tot)
            dsp = dsp + et * dsn
            dt_h = et * jnp.sum(dsn * sp)
            gmat = xd * r
            dgm = _dot_nt(bmb, dsnb)
            dbm[g] = dbm[g] + _dot(gmat.astype(bf16), dsnb)
            dxd = dxd + dgm * r
            dr = jnp.sum(dgm * xd, axis=1, keepdims=True) * r
            dt_h = dt_h + jnp.sum(dr)
            dac = dac - dr
            dx = dx + dxd * dtc
            ddt = ddt + jnp.sum(dxd * x, axis=1, keepdims=True) * oh
            dacs = dacs + dac * oh
            dtot = dtot + dt_h * oh
            ds_sc[h] = dsp
            dxa_sc[:, h * P:(h + 1) * P] = dx
        rowi = lax.broadcasted_iota(jnp.int32, (CHUNK, 1), 0)
        dacs = dacs + dacs_r.T + jnp.where(rowi == CHUNK - 1, dtot, 0.0)
        r2 = lax.broadcasted_iota(jnp.int32, (CHUNK, CHUNK), 0)
        c2 = lax.broadcasted_iota(jnp.int32, (CHUNK, CHUNK), 1)
        upper = (c2 >= r2).astype(f32)
        dadt = jnp.dot(upper, dacs, preferred_element_type=f32, precision=HIGHEST)
        ddt = ddt + dadt * a
        dalog_ref[...] += jnp.sum(dadt * dt, axis=0, keepdims=True) * a
        ddsk_ref[...] += ddsk
        ddtr = jnp.where(lane < H, ddt * _sigmoid(dtr_ref[...] + dtb_ref[...]), 0.0)
        ddtb_ref[...] += jnp.sum(ddtr, axis=0, keepdims=True)
        dzxd_ref[:, SSM_WIDTH + CONV_DIM:] = ddtr.astype(bf16)
        for g in range(SSM_GROUPS):
            cmb, bmb, _ = groups[g]
            dcbb = dcb[g].astype(bf16)
            dxa_sc[:, SSM_WIDTH + g * N:SSM_WIDTH + (g + 1) * N] = dbm[g] + _dot_tn(dcbb, cmb)
            dxa_sc[:, SSM_WIDTH + (SSM_GROUPS + g) * N:SSM_WIDTH + (SSM_GROUPS + g + 1) * N] = dcm[g] + _dot(dcbb, bmb)
        dxc = dxa_sc[...] * (sg * (1.0 + xc * (1.0 - sg)))
        dxc_sc[0:CHUNK, :] = dxc
        dxbc = jnp.zeros((CHUNK, CONV_DIM), f32)
        dcw = []
        for j in range(SSM_CONV):
            dxbc = dxbc + cw_ref[j:j + 1, :] * dxc_sc[pl.ds(SSM_CONV - 1 - j, CHUNK), :]
            dcw.append(jnp.sum(dxc * ext_sc[pl.ds(HALO - SSM_CONV + 1 + j, CHUNK), :], axis=0, keepdims=True))
        dzxd_ref[:, SSM_WIDTH:SSM_WIDTH + CONV_DIM] = dxbc.astype(bf16)
        dcw_ref[...] += jnp.concatenate(dcw, axis=0)
        dcb_ref[...] += jnp.sum(dxc, axis=0, keepdims=True)
        dxc_sc[CHUNK:CHUNK + NEXT, :] = dxc[0:NEXT, :]

    params = [conv_w, conv_b, dt_bias, a_log, d_skip, ssm_norm]

    def rc(i):
        return nc - 1 - i

    in_specs = (_ssd_in_specs(nc, True)
                + [pl.BlockSpec((1, H, P, N), lambda i: (rc(i), 0, 0, 0)), pl.BlockSpec((CHUNK, SSM_WIDTH), lambda i: (rc(i), 0))]
                + [_full(x.shape) for x in params])
    return pl.pallas_call(
        body, name="ssd_bwd", grid=(nc,), in_specs=in_specs,
        out_specs=[pl.BlockSpec((CHUNK, ZXD), lambda i: (rc(i), 0))] + [_acc(x.shape) for x in params],
        out_shape=[jax.ShapeDtypeStruct((T, ZXD), bf16)] + [jax.ShapeDtypeStruct(x.shape, f32) for x in params],
        scratch_shapes=[pltpu.VMEM((H, P, N), f32), pltpu.VMEM((HALO + CHUNK, CONV_DIM), f32),
                        pltpu.VMEM((CHUNK + NEXT, CONV_DIM), f32), pltpu.VMEM((CHUNK, SSM_WIDTH), f32),
                        pltpu.VMEM((CHUNK, CONV_DIM), f32)],
        compiler_params=_params(("arbitrary",)))(xbc, xbc, z, dtr, sprev, dyb, *params)


def _tail(h3, p, tgt, gp, wpg, bpg, wpp, gf, tm=512):
    T, D = h3.shape
    tm = min(tm, T)

    def head(gpre, pp, h, gf_, t):
        gate = _sigmoid(gpre)
        y = _rms(h + gate * pp, gf_)
        err = y - t
        return 0.5 * jnp.sum(jnp.mean(err * err, axis=-1))

    def body(h_ref, p_ref, t_ref, gp_ref, wpg_ref, bpg_ref, wpp_ref, gf_ref,
             dh_ref, loss_ref, dgp_ref, dwpg_ref, dbpg_ref, dwpp_ref, dgf_ref):
        @pl.when(pl.program_id(0) == 0)
        def _():
            for r in (loss_ref, dgp_ref, dwpg_ref, dbpg_ref, dwpp_ref, dgf_ref):
                r[...] = jnp.zeros_like(r)

        h = h_ref[...]
        npf, np_vjp = jax.vjp(_rms, h, gp_ref[...])
        npb = npf.astype(bf16)
        pb = p_ref[...].astype(bf16)
        gpre = _dot(npb, wpg_ref[...]) + bpg_ref[...]
        pp = _dot(pb, wpp_ref[...])
        loss, head_vjp = jax.vjp(head, gpre, pp, h, gf_ref[...], t_ref[...])
        dgpre, dpp, dh_a, dgf, _ = head_vjp(jnp.ones((), f32))
        loss_ref[...] += loss
        dgf_ref[...] += dgf
        dbpg_ref[...] += jnp.sum(dgpre, axis=0, keepdims=True)
        dgb = dgpre.astype(bf16)
        dwpg_ref[...] += _dot_tn(npb, dgb)
        dwpp_ref[...] += _dot_tn(pb, dpp.astype(bf16))
        dh_b, dgp = np_vjp(_dot_nt(dgb, wpg_ref[...]))
        dgp_ref[...] += dgp
        dh_ref[...] = dh_a + dh_b

    ins = [h3, p, tgt, gp, wpg, bpg, wpp, gf]
    in_specs = [_rows(tm, D), _rows(tm, p.shape[1]), _rows(tm, D)] + [_full(x.shape) for x in ins[3:]]
    acc_shapes = [(1, LANES), gp.shape, wpg.shape, bpg.shape, wpp.shape, gf.shape]
    return pl.pallas_call(
        body, name="tail", grid=(T // tm,), in_specs=in_specs,
        out_specs=[_rows(tm, D)] + [_acc(s) for s in acc_shapes],
        out_shape=[jax.ShapeDtypeStruct((T, D), f32)] + [jax.ShapeDtypeStruct(s, f32) for s in acc_shapes],
        compiler_params=_params(("arbitrary",)))(*ins)


def _adamw(name, w, g, m, v, tr=256):
    R, C = w.shape
    tr = _row_tile(R, tr)

    def body(w_ref, g_ref, m_ref, v_ref, d_ref, mo_ref, vo_ref):
        g_ = g_ref[...]
        m_ = ADAM_B1 * m_ref[...] + (1.0 - ADAM_B1) * g_
        v_ = ADAM_B2 * v_ref[...] + (1.0 - ADAM_B2) * jnp.square(g_)
        m_hat = m_ / (1.0 - ADAM_B1 ** ADAM_STEP)
        v_hat = v_ / (1.0 - ADAM_B2 ** ADAM_STEP)
        d_ref[...] = -ADAM_LR * (m_hat / (jnp.sqrt(v_hat) + ADAM_EPS) + ADAM_WD * w_ref[...])
        mo_ref[...] = m_
        vo_ref[...] = v_

    spec = pl.BlockSpec((tr, C), lambda i: (i, 0))
    return pl.pallas_call(body, name=name, grid=(R // tr,), in_specs=[spec] * 4, out_specs=[spec] * 3,
                          out_shape=[jax.ShapeDtypeStruct((R, C), f32)] * 3,
                          compiler_params=_params(("parallel",)))(w, g, m, v)


HBM = pl.BlockSpec(memory_space=pltpu.HBM)


def _me():
    return lax.axis_index("x"), lax.axis_index("y"), lax.axis_index("c")


def _other_chips(x, y):
    return [(1 - x, y), (x, 1 - y), (1 - x, 1 - y)]


def _remote(src, dst, send_sem, recv_sem, dev):
    return pltpu.make_async_remote_copy(src_ref=src, dst_ref=dst, send_sem=send_sem, recv_sem=recv_sem,
                                        device_id=dev, device_id_type=MESH)


def _gather_weights(shard, nchunk=4):
    R, C = shard.shape
    rows = R // 2 // nchunk
    n = 3 * nchunk

    def body(in_ref, out_ref, ici_send, ici_recv, d2d_send, d2d_recv, local_sem):
        x, y, c = _me()
        sibling = (x, y, 1 - c)
        chips = _other_chips(x, y)

        def piece(chip, half, k):
            return out_ref.at[2 * chip[0] + chip[1], pl.ds((half * nchunk + k) * rows, rows), :]

        mine = pltpu.make_async_copy(in_ref, out_ref.at[2 * x + y], local_sem)
        mine.start()
        sends = []
        for j, chip in enumerate(chips):
            for k in range(nchunk):
                src = in_ref.at[pl.ds((c * nchunk + k) * rows, rows), :]
                cp = _remote(src, piece((x, y), c, k), ici_send.at[j * nchunk + k], ici_recv.at[j * nchunk + k], (*chip, c))
                cp.start()
                sends.append(cp)
        passed = []
        for k in range(nchunk):
            for j, chip in enumerate(chips):
                s = j * nchunk + k
                _remote(piece(chip, c, k), piece(chip, c, k), ici_send.at[s], ici_recv.at[s], (*chip, c)).wait_recv()
                cp = _remote(piece(chip, c, k), piece(chip, c, k), d2d_send.at[s], d2d_recv.at[s], sibling)
                cp.start()
                passed.append(cp)
        for k in range(nchunk):
            for j, chip in enumerate(chips):
                s = j * nchunk + k
                _remote(piece(chip, 1 - c, k), piece(chip, 1 - c, k), d2d_send.at[s], d2d_recv.at[s], sibling).wait_recv()
        for cp in sends + passed:
            cp.wait_send()
        mine.wait()

    return pl.pallas_call(
        body, name="gather_weights", out_shape=jax.ShapeDtypeStruct((N_CHIPS, R, C), shard.dtype),
        in_specs=[HBM], out_specs=HBM,
        scratch_shapes=[pltpu.SemaphoreType.DMA((n,)), pltpu.SemaphoreType.DMA((n,)),
                        pltpu.SemaphoreType.DMA((n,)), pltpu.SemaphoreType.DMA((n,)), pltpu.SemaphoreType.DMA])(shard)


def _swap_halves(grads):
    K, R, C = grads.shape
    H = R // 2

    def body(g_ref, out_ref, send_sem, recv_sem):
        x, y, c = _me()
        cp = _remote(g_ref.at[:, pl.ds((1 - c) * H, H), :], out_ref, send_sem, recv_sem, (x, y, 1 - c))
        cp.start()
        cp.wait()

    return pl.pallas_call(
        body, name="swap_halves", out_shape=jax.ShapeDtypeStruct((K, H, C), grads.dtype),
        in_specs=[HBM], out_specs=HBM,
        scratch_shapes=[pltpu.SemaphoreType.DMA, pltpu.SemaphoreType.DMA])(grads)


def _add_halves(grads, other, c_idx, th=448):
    K, R, C = grads.shape
    H = R // 2
    th = _row_tile(H, th, 16)
    nb = H // th

    def body(c_ref, g_ref, o_ref, out_ref):
        out_ref[...] = (g_ref[...].astype(f32) + o_ref[...].astype(f32)).astype(bf16)

    grid_spec = pltpu.PrefetchScalarGridSpec(
        num_scalar_prefetch=1, grid=(nb,),
        in_specs=[pl.BlockSpec((K, th, C), lambda i, c: (0, c[0] * nb + i, 0)),
                  pl.BlockSpec((K, th, C), lambda i, c: (0, i, 0))],
        out_specs=pl.BlockSpec((K, th, C), lambda i, c: (0, i, 0)))
    return pl.pallas_call(body, name="add_halves", grid_spec=grid_spec,
                          out_shape=jax.ShapeDtypeStruct((K, H, C), bf16),
                          compiler_params=_params(("parallel",)))(c_idx, grads, other)


def _exchange_partials(part, small):
    K, H, C = part.shape

    def body(p_ref, s_ref, po_ref, so_ref, psend, precv, ssend, srecv, local_sems):
        x, y, c = _me()
        my_chip = 2 * x + y
        my_dev = 4 * x + 2 * y + c
        l0 = pltpu.make_async_copy(p_ref.at[my_chip], po_ref.at[my_chip], local_sems.at[0])
        l1 = pltpu.make_async_copy(s_ref, so_ref.at[my_dev], local_sems.at[1])
        l0.start()
        l1.start()
        copies = []
        for j, chip in enumerate(_other_chips(x, y)):
            cp = _remote(p_ref.at[2 * chip[0] + chip[1]], po_ref.at[my_chip], psend.at[j], precv.at[j], (*chip, c))
            cp.start()
            copies.append(cp)
        k = 0
        for dx in range(2):
            for dy in range(2):
                for dc in range(2):
                    if dx or dy or dc:
                        peer = (x ^ dx, y ^ dy, c ^ dc)
                        cp = _remote(s_ref, so_ref.at[my_dev], ssend.at[k], srecv.at[k], peer)
                        cp.start()
                        copies.append(cp)
                        k += 1
        for cp in copies:
            cp.wait()
        l0.wait()
        l1.wait()

    return pl.pallas_call(
        body, name="exchange_partials",
        out_shape=[jax.ShapeDtypeStruct((K, H, C), part.dtype), jax.ShapeDtypeStruct((N_DEV,) + small.shape, small.dtype)],
        in_specs=[HBM, HBM], out_specs=[HBM, HBM],
        scratch_shapes=[pltpu.SemaphoreType.DMA((3,)), pltpu.SemaphoreType.DMA((3,)),
                        pltpu.SemaphoreType.DMA((7,)), pltpu.SemaphoreType.DMA((7,)), pltpu.SemaphoreType.DMA((2,))])(part, small)


def _sum_slots(name, slots, tr):
    K, R, C = slots.shape
    tr = _row_tile(R, tr, 16)

    def body(s_ref, o_ref):
        acc = s_ref[0].astype(f32)
        for k in range(1, K):
            acc = acc + s_ref[k].astype(f32)
        o_ref[...] = acc

    return pl.pallas_call(body, name=name, grid=(R // tr,),
                          in_specs=[pl.BlockSpec((K, tr, C), lambda i: (0, i, 0))],
                          out_specs=pl.BlockSpec((tr, C), lambda i: (i, 0)),
                          out_shape=jax.ShapeDtypeStruct((R, C), f32),
                          compiler_params=_params(("parallel",)))(slots)


def _share_halves(half):
    H, C = half.shape

    def body(h_ref, out_ref, send_sem, recv_sem, local_sem):
        x, y, c = _me()
        lc = pltpu.make_async_copy(h_ref, out_ref.at[c], local_sem)
        lc.start()
        cp = _remote(h_ref, out_ref.at[c], send_sem, recv_sem, (x, y, 1 - c))
        cp.start()
        cp.wait()
        lc.wait()

    return pl.pallas_call(
        body, name="share_halves", out_shape=jax.ShapeDtypeStruct((2, H, C), half.dtype),
        in_specs=[HBM], out_specs=HBM,
        scratch_shapes=[pltpu.SemaphoreType.DMA, pltpu.SemaphoreType.DMA, pltpu.SemaphoreType.DMA])(half)


BIG = (("ffn1_w_gate", 1), ("ffn1_w_up", 1), ("ffn1_w_down", 0), ("w_in", 1), ("w_out", 0),
       ("ffn2_w_gate", 1), ("ffn2_w_up", 1), ("ffn2_w_down", 0), ("ple_w_gate", 0), ("ple_w_proj", 1))


def _pack_rows(n_elems):
    rows = -(-n_elems // PACK_C)
    return -(-rows // PACK_SPLIT) * PACK_SPLIT


def _pack_shards(shards, conv_w):
    parts = [shards[n].astype(bf16).reshape(-1) for n, _ in BIG]
    parts.append(lax.bitcast_convert_type(conv_w, bf16).reshape(-1))
    flat = jnp.concatenate(parts)
    R = _pack_rows(flat.shape[0])
    return jnp.pad(flat, (0, R * PACK_C - flat.shape[0])).reshape(R, PACK_C)


def _unpack_full(packs, shapes, conv_shape):
    flat = packs.reshape(N_CHIPS, -1)
    out, off = {}, 0
    for n, axis in BIG:
        r, c = shapes[n]
        w = flat[:, off:off + r * c].reshape(N_CHIPS, r, c)
        off += r * c
        out[n] = w.reshape(N_CHIPS * r, c) if axis == 0 else jnp.transpose(w, (1, 0, 2)).reshape(r, N_CHIPS * c)
    r, c = conv_shape
    cw = lax.bitcast_convert_type(flat[:, off:off + 2 * r * c].reshape(N_CHIPS, r, c, 2), f32)
    return out, jnp.transpose(cw, (1, 0, 2)).reshape(r, N_CHIPS * c)


def _pack_grads(grads, shapes):
    parts = []
    for n, axis in BIG:
        r, c = shapes[n]
        g = grads[n]
        g = g.reshape(N_CHIPS, r * c) if axis == 0 else jnp.transpose(g.reshape(r, N_CHIPS, c), (1, 0, 2)).reshape(N_CHIPS, r * c)
        parts.append(g.astype(bf16))
    flat = jnp.concatenate(parts, axis=1)
    R = _pack_rows(flat.shape[1] + 2 * SSM_CONV * (CONV_DIM // N_CHIPS))
    return jnp.pad(flat, ((0, 0), (0, R * PACK_C - flat.shape[1]))).reshape(N_CHIPS, R, PACK_C)


def _unpack_shard_grads(pack, shapes):
    flat = pack.reshape(-1)
    out, off = {}, 0
    for n, _ in BIG:
        r, c = shapes[n]
        out[n] = flat[off:off + r * c].reshape(r, c)
        off += r * c
    return out


SMALL = ("ffn1_norm", "mix_norm", "gm_ln_g", "gm_ln_b", "gm_w_s", "gm_b_s", "gm_out_norm", "conv_b", "dt_bias", "a_log",
         "d_skip", "ssm_norm", "ffn2_norm", "ple_norm", "ple_b_gate", "final_norm")
SMALL_C = 1024


def _pack_small(vals):
    parts = []
    for v in vals:
        f = v.astype(f32).reshape(-1)
        parts.append(jnp.pad(f, (0, -f.shape[0] % SMALL_C)))
    flat = jnp.concatenate(parts)
    rows = flat.shape[0] // SMALL_C
    return jnp.pad(flat, (0, (-rows % 8) * SMALL_C)).reshape(-1, SMALL_C)


def _unpack_small(pack, shapes):
    flat = pack.reshape(-1)
    out, off = [], 0
    for s in shapes:
        n = 1
        for d in s:
            n *= d
        out.append(flat[off:off + n].reshape(s))
        off += n + (-n % SMALL_C)
    return out


def _pad_lanes(v):
    return jnp.pad(v, ((0, 0), (0, LANES - v.shape[1])))


def _local_step(x, p, tgt, W, conv_w, S):
    G = GM_WIDTH
    w_in = W["w_in"]
    w_uv = w_in[:, :2 * G]
    w_zxd = jnp.pad(w_in[:, 2 * G:], ((0, 0), (0, ZXD - (IN_PROJ - 2 * G))))
    woa, wob = W["w_out"][:G], W["w_out"][G:]
    b_st = S["gm_b_s"][0].T
    w_s = S["gm_w_s"][0]
    dtb, alog, dsk = _pad_lanes(S["dt_bias"]), _pad_lanes(S["a_log"]), _pad_lanes(S["d_skip"])
    gfin = S["final_norm"].reshape(1, -1)

    h1, n1, a1, b1 = _ffn_fwd("ffn1_fwd", x, S["ffn1_norm"], W["ffn1_w_gate"], W["ffn1_w_up"], W["ffn1_w_down"])
    n2, uv, z, xbc, dtr, ya = _mix_fwd(h1, S["mix_norm"], w_uv, w_zxd, S["gm_ln_g"], S["gm_ln_b"], w_s, b_st, S["gm_out_norm"])
    yb, sprev = _ssd_fwd(xbc, z, dtr, conv_w, S["conv_b"], dtb, alog, dsk, S["ssm_norm"])
    h2, h3, n3, a2, b2 = _ffn_fwd("ffn2_fwd", h1, S["ffn2_norm"], W["ffn2_w_gate"], W["ffn2_w_up"], W["ffn2_w_down"],
                                  pre=(ya, yb, woa, wob))
    dh3, loss, dgp, dwpg, dbpg, dwpp, dgf = _tail(h3, p, tgt, S["ple_norm"], W["ple_w_gate"], S["ple_b_gate"], W["ple_w_proj"], gfin)
    dh2, da2, db2, hm2, dg_ffn2, dya, dyb = _ffn_bwd("ffn2_bwd", dh3, h2, S["ffn2_norm"], a2, b2, W["ffn2_w_gate"],
                                                     W["ffn2_w_up"], W["ffn2_w_down"], post=(woa, wob))
    gW = {}
    gW["ffn2_w_gate"] = _matmul_tn("dw_ffn2_gate", n3, da2)
    gW["ffn2_w_up"] = _matmul_tn("dw_ffn2_up", n3, db2)
    gW["ffn2_w_down"] = _matmul_tn("dw_ffn2_down", hm2, dh3, scale=0.5)
    gW["w_out"] = jnp.concatenate([_matmul_tn("dw_out_a", ya, dh2), _matmul_tn("dw_out_b", yb, dh2)], axis=0)
    duv, dlng, dlnb, dws, dbst, dgout = _gm_bwd(uv, dya, S["gm_ln_g"], S["gm_ln_b"], w_s, b_st, S["gm_out_norm"])
    dzxd, dcw, dcb, ddtb, dalog, ddsk, dgssm = _ssd_bwd(xbc, z, dtr, sprev, dyb, conv_w, S["conv_b"], dtb, alog, dsk, S["ssm_norm"])
    dh1, dg_mix = _mix_bwd(dh2, h1, S["mix_norm"], duv, dzxd, w_uv, w_zxd)
    gW["w_in"] = jnp.concatenate([_matmul_tn("dw_in_uv", n2, duv), _matmul_tn("dw_in_zxd", n2, dzxd)[:, :IN_PROJ - 2 * G]], axis=1)
    dx, da1, db1, hm1, dg_ffn1 = _ffn_bwd("ffn1_bwd", dh1, x, S["ffn1_norm"], a1, b1, W["ffn1_w_gate"], W["ffn1_w_up"], W["ffn1_w_down"])
    gW["ffn1_w_gate"] = _matmul_tn("dw_ffn1_gate", n1, da1)
    gW["ffn1_w_up"] = _matmul_tn("dw_ffn1_up", n1, db1)
    gW["ffn1_w_down"] = _matmul_tn("dw_ffn1_down", hm1, dh1, scale=0.5)
    gW["ple_w_gate"] = dwpg
    gW["ple_w_proj"] = dwpp
    nh = SSM_HEADS
    gS = {"ffn1_norm": dg_ffn1, "mix_norm": dg_mix, "gm_ln_g": dlng, "gm_ln_b": dlnb, "gm_w_s": dws[None], "gm_b_s": dbst.T[None],
          "gm_out_norm": dgout, "conv_b": dcb, "dt_bias": ddtb[:, :nh], "a_log": dalog[:, :nh], "d_skip": ddsk[:, :nh],
          "ssm_norm": dgssm, "ffn2_norm": dg_ffn2, "ple_norm": dgp, "ple_b_gate": dbpg, "final_norm": dgf.reshape(-1)}
    return loss, dx, gW, dcw, gS


_WEIGHTS = ("ffn1_norm", "ffn1_w_gate", "ffn1_w_up", "ffn1_w_down", "mix_norm", "w_in", "gm_ln_g", "gm_ln_b", "gm_w_s", "gm_b_s",
            "gm_out_norm", "conv_w", "conv_b", "dt_bias", "a_log", "d_skip", "ssm_norm", "w_out", "ffn2_norm", "ffn2_w_gate",
            "ffn2_w_up", "ffn2_w_down", "ple_norm", "ple_w_gate", "ple_b_gate", "ple_w_proj", "final_norm")
_BIG_NAMES = tuple(n for n, _ in BIG)


def kernel(x, p, ffn1_norm, ffn1_w_gate, ffn1_w_up, ffn1_w_down, mix_norm, w_in, gm_ln_g, gm_ln_b, gm_w_s, gm_b_s, gm_out_norm, conv_w, conv_b, dt_bias, a_log, d_skip, ssm_norm, w_out, ffn2_norm, ffn2_w_gate, ffn2_w_up, ffn2_w_down, ple_norm, ple_w_gate, ple_b_gate, ple_w_proj, final_norm, loss_target, m_ffn1_norm, m_ffn1_w_gate, m_ffn1_w_up, m_ffn1_w_down, m_mix_norm, m_w_in, m_gm_ln_g, m_gm_ln_b, m_gm_w_s, m_gm_b_s, m_gm_out_norm, m_conv_w, m_conv_b, m_dt_bias, m_a_log, m_d_skip, m_ssm_norm, m_w_out, m_ffn2_norm, m_ffn2_w_gate, m_ffn2_w_up, m_ffn2_w_down, m_ple_norm, m_ple_w_gate, m_ple_b_gate, m_ple_w_proj, m_final_norm, v_ffn1_norm, v_ffn1_w_gate, v_ffn1_w_up, v_ffn1_w_down, v_mix_norm, v_w_in, v_gm_ln_g, v_gm_ln_b, v_gm_w_s, v_gm_b_s, v_gm_out_norm, v_conv_w, v_conv_b, v_dt_bias, v_a_log, v_d_skip, v_ssm_norm, v_w_out, v_ffn2_norm, v_ffn2_w_gate, v_ffn2_w_up, v_ffn2_w_down, v_ple_norm, v_ple_w_gate, v_ple_b_gate, v_ple_w_proj, v_final_norm):
    given = dict(locals())
    w = {n: given[n] for n in _WEIGHTS}
    m = {n: given["m_" + n] for n in _WEIGHTS}
    v = {n: given["v_" + n] for n in _WEIGHTS}

    shards = {n: w[n][0] for n in _BIG_NAMES}
    shapes = {n: shards[n].shape for n in _BIG_NAMES}
    cw_shard = w["conv_w"][0]
    packs = _gather_weights(_pack_shards(shards, cw_shard))
    W, cw_full = _unpack_full(packs, shapes, cw_shard.shape)

    S = {n: w[n] for n in SMALL}
    loss, dx, gW, dcw, gS = _local_step(x[0], p[0, 0], loss_target[0], W, cw_full, S)

    c_idx = lax.axis_index("c").astype(jnp.int32).reshape(1)
    gpack = _pack_grads(gW, shapes)
    part = _add_halves(gpack, _swap_halves(gpack), c_idx)
    small = _pack_small([gS[n] for n in SMALL] + [dcw, loss[:, :1]])
    parts, smalls = _exchange_partials(part, small)
    half = _sum_slots("sum_partials", parts, 448)
    gshard = _unpack_shard_grads(_share_halves(half).reshape(-1, PACK_C), shapes)
    small_shapes = [w[n].shape for n in SMALL] + [dcw.shape, (1, 1)]
    small_sum = _unpack_small(_sum_slots("sum_small", smalls, 512), small_shapes)
    g = {n: small_sum[i] for i, n in enumerate(SMALL)}
    chip = 2 * lax.axis_index("x") + lax.axis_index("y")
    cshard = cw_shard.shape[1]
    g["conv_w"] = lax.dynamic_slice_in_dim(small_sum[len(SMALL)], chip * cshard, cshard, axis=1)[None]
    loss_total = small_sum[len(SMALL) + 1].reshape(())
    for n in _BIG_NAMES:
        g[n] = gshard[n][None]

    delta, new_m, new_v = {}, {}, {}
    sm_names = SMALL + ("conv_w",)
    sm_shapes = [w[n].shape for n in sm_names]
    d_s, m_s, v_s = _adamw("adamw_small", _pack_small([w[n] for n in sm_names]), _pack_small([g[n] for n in sm_names]),
                           _pack_small([m[n] for n in sm_names]), _pack_small([v[n] for n in sm_names]))
    for dst, src in ((delta, d_s), (new_m, m_s), (new_v, v_s)):
        for n, val in zip(sm_names, _unpack_small(src, sm_shapes)):
            dst[n] = val
    for n in _BIG_NAMES:
        d_, m_, v_ = _adamw("adamw_" + n, w[n][0], g[n][0], m[n][0], v[n][0])
        delta[n], new_m[n], new_v[n] = d_[None], m_[None], v_[None]

    return (loss_total, dx[None], *[g[n] for n in _WEIGHTS], *[delta[n] for n in _WEIGHTS],
            *[new_m[n] for n in _WEIGHTS], *[new_v[n] for n in _WEIGHTS])
```

```python
import functools

import jax
import jax.numpy as jnp
from jax import lax
from jax.experimental import pallas as pl
from jax.experimental.pallas import tpu as pltpu

f32 = jnp.float32
bf16 = jnp.bfloat16
MESH = pl.DeviceIdType.MESH
HIGHEST = lax.Precision.HIGHEST

EPS = 1e-6
N_CHIPS = 4
N_DEV = 8
D_MODEL = 1024
D_FF = 2816
D_PLE = 256
GM_WIDTH = 1024
GM_HEADS = 8
CHUNK = 128
SSM_WIDTH = 1024
SSM_HEADS = 16
SSM_HEAD_DIM = 64
SSM_GROUPS = 2
SSM_STATE = 128
SSM_CONV = 4
CONV_DIM = SSM_WIDTH + 2 * SSM_GROUPS * SSM_STATE
IN_PROJ = 2 * GM_WIDTH + SSM_WIDTH + CONV_DIM + SSM_HEADS
LANES = 128
ZXD = SSM_WIDTH + CONV_DIM + LANES

ADAM_LR = 0.001
ADAM_B1 = 0.9
ADAM_B2 = 0.999
ADAM_EPS = 1e-08
ADAM_WD = 0.01
ADAM_STEP = 10

VMEM_LIMIT = 56 * 1024 * 1024


def _dot(a, b):
    return jnp.dot(a, b, preferred_element_type=f32)


def _dot_nt(a, b):
    return lax.dot_general(a, b, (((1,), (1,)), ((), ())), preferred_element_type=f32)


def _dot_tn(a, b):
    return lax.dot_general(a, b, (((0,), (0,)), ((), ())), preferred_element_type=f32)


def _rms(x, g):
    return x * lax.rsqrt(jnp.mean(x * x, axis=-1, keepdims=True) + EPS) * g


def _gelu(x):
    return 0.5 * x * (1.0 + lax.erf(x * 0.7071067811865476))


def _layernorm(x, g, b):
    mu = jnp.mean(x, axis=-1, keepdims=True)
    xc = x - mu
    return xc * lax.rsqrt(jnp.mean(xc * xc, axis=-1, keepdims=True) + EPS) * g + b


def _sigmoid(x):
    return 1.0 / (1.0 + jnp.exp(-x))


def _softplus(x):
    return jnp.maximum(x, 0.0) + jnp.log(1.0 + jnp.exp(-jnp.abs(x)))


def _full(shape):
    nd = len(shape)
    return pl.BlockSpec(shape, lambda *_: (0,) * nd, pipeline_mode=pl.Buffered(1))


def _acc(shape):
    nd = len(shape)
    return pl.BlockSpec(shape, lambda *_: (0,) * nd)


def _rows(tm, ncols):
    return pl.BlockSpec((tm, ncols), lambda i: (i, 0))


def _params(sem):
    return pltpu.CompilerParams(dimension_semantics=sem, vmem_limit_bytes=VMEM_LIMIT)


def _row_tile(rows, target, mult=8):
    best = rows
    for t in range(mult, min(rows, target) + 1, mult):
        if rows % t == 0:
            best = t
    return best if best <= target else rows


def _stack_rows(k, tm, ncols):
    return pl.BlockSpec((k, tm, ncols), lambda i: (0, i, 0))


def _ffn_fwd(name, h, g, wg, wu, wd, pre=None, tm=256):
    T, D = h.shape
    K, _, Fs = wg.shape
    tm = min(tm, T)

    def body(*refs):
        if pre is None:
            h_ref, g_ref, wg_ref, wu_ref, wd_ref, ho_ref, n_ref, a_ref, b_ref = refs
            hin = h_ref[...]
        else:
            (h_ref, ya_ref, yb_ref, wo_ref, g_ref, wg_ref, wu_ref, wd_ref,
             hi_ref, ho_ref, n_ref, a_ref, b_ref) = refs
            ga = ya_ref.shape[1]
            hin = h_ref[...] + _dot(ya_ref[...], wo_ref[:ga, :]) + _dot(yb_ref[...], wo_ref[ga:, :])
            hi_ref[...] = hin
        n = _rms(hin, g_ref[...]).astype(bf16)
        n_ref[...] = n
        acc = jnp.zeros((tm, D), f32)
        for k in range(K):
            a = _dot(n, wg_ref[k]).astype(bf16)
            b = _dot(n, wu_ref[k]).astype(bf16)
            a_ref[k] = a
            b_ref[k] = b
            af = a.astype(f32)
            hm = (af * _sigmoid(af) * b.astype(f32)).astype(bf16)
            acc = acc + _dot(hm, wd_ref[k])
        ho_ref[...] = hin + 0.5 * acc

    ins = [h] + (list(pre) if pre is not None else []) + [g, wg, wu, wd]
    in_specs = [_rows(tm, D)]
    if pre is not None:
        in_specs += [_rows(tm, pre[0].shape[1]), _rows(tm, pre[1].shape[1]), _full(pre[2].shape)]
    in_specs += [_full(g.shape), _full(wg.shape), _full(wu.shape), _full(wd.shape)]
    outs = [jax.ShapeDtypeStruct((T, D), f32), jax.ShapeDtypeStruct((T, D), bf16),
            jax.ShapeDtypeStruct((K, T, Fs), bf16), jax.ShapeDtypeStruct((K, T, Fs), bf16)]
    out_specs = [_rows(tm, D), _rows(tm, D), _stack_rows(K, tm, Fs), _stack_rows(K, tm, Fs)]
    if pre is not None:
        outs = [jax.ShapeDtypeStruct((T, D), f32)] + outs
        out_specs = [_rows(tm, D)] + out_specs
    return pl.pallas_call(body, name=name, grid=(T // tm,), in_specs=in_specs, out_specs=out_specs,
                          out_shape=outs, compiler_params=_params(("parallel",)))(*ins)


def _ffn_bwd(name, dh, hin, g, a, b, wg, wu, wd, wo=None, ga=0, tm=256):
    T, D = dh.shape
    K, _, Fs = wg.shape
    tm = min(tm, T)

    def body(*refs):
        if wo is None:
            (dh_ref, hin_ref, g_ref, a_ref, b_ref, wg_ref, wu_ref, wd_ref,
             dhi_ref, da_ref, db_ref, hm_ref, dg_ref) = refs
        else:
            (dh_ref, hin_ref, g_ref, a_ref, b_ref, wg_ref, wu_ref, wd_ref, wo_ref,
             dhi_ref, da_ref, db_ref, hm_ref, dg_ref, dya_ref, dyb_ref) = refs

        @pl.when(pl.program_id(0) == 0)
        def _():
            dg_ref[...] = jnp.zeros_like(dg_ref)

        dh_ = dh_ref[...]
        dhb = (0.5 * dh_).astype(bf16)
        dn = jnp.zeros((tm, D), f32)
        for k in range(K):
            dhm = _dot_nt(dhb, wd_ref[k])
            af = a_ref[k].astype(f32)
            bf = b_ref[k].astype(f32)
            sg = _sigmoid(af)
            sl_ = af * sg
            da = (dhm * bf * (sg * (1.0 + af * (1.0 - sg)))).astype(bf16)
            db = (dhm * sl_).astype(bf16)
            da_ref[k] = da
            db_ref[k] = db
            hm_ref[k] = (sl_ * bf).astype(bf16)
            dn = dn + _dot_nt(da, wg_ref[k]) + _dot_nt(db, wu_ref[k])
        _, vjp = jax.vjp(_rms, hin_ref[...], g_ref[...])
        dx, dg = vjp(dn)
        dhi = dh_ + dx
        dhi_ref[...] = dhi
        dg_ref[...] += dg
        if wo is not None:
            dhib = dhi.astype(bf16)
            dya_ref[...] = _dot_nt(dhib, wo_ref[:ga, :]).astype(bf16)
            dyb_ref[...] = _dot_nt(dhib, wo_ref[ga:, :]).astype(bf16)

    ins = [dh, hin, g, a, b, wg, wu, wd]
    in_specs = [_rows(tm, D), _rows(tm, D), _full(g.shape), _stack_rows(K, tm, Fs), _stack_rows(K, tm, Fs),
                _full(wg.shape), _full(wu.shape), _full(wd.shape)]
    act = jax.ShapeDtypeStruct((K, T, Fs), bf16)
    outs = [jax.ShapeDtypeStruct((T, D), f32), act, act, act, jax.ShapeDtypeStruct(g.shape, f32)]
    out_specs = [_rows(tm, D), _stack_rows(K, tm, Fs), _stack_rows(K, tm, Fs), _stack_rows(K, tm, Fs), _acc(g.shape)]
    if wo is not None:
        gb = wo.shape[0] - ga
        ins += [wo]
        in_specs += [_full(wo.shape)]
        outs += [jax.ShapeDtypeStruct((T, ga), bf16), jax.ShapeDtypeStruct((T, gb), bf16)]
        out_specs += [_rows(tm, ga), _rows(tm, gb)]
    return pl.pallas_call(body, name=name, grid=(T // tm,), in_specs=in_specs, out_specs=out_specs,
                          out_shape=outs, compiler_params=_params(("arbitrary",)))(*ins)


def _matmul_tn(name, a, b, scale=1.0, tk=1024):
    ka = a.shape[0] if a.ndim == 3 else 0
    kb = b.shape[0] if b.ndim == 3 else 0
    K = max(ka, kb)
    T, M = a.shape[-2:]
    N = b.shape[-1]
    tk = min(tk, T)
    nk = T // tk
    if K:
        tn, nj = N, K
    else:
        tn = LANES * max(d for d in range(1, N // LANES + 1) if (N // LANES) % d == 0 and (d == 1 or M * d * LANES * 4 <= 6 * 1024 * 1024))
        nj = N // tn

    def body(a_ref, b_ref, o_ref, acc):
        k = pl.program_id(1)

        @pl.when(k == 0)
        def _():
            acc[...] = jnp.zeros_like(acc)

        bb = b_ref[...]
        if scale != 1.0:
            bb = bb * scale
        acc[...] += _dot_tn(a_ref[...].astype(bf16), bb.astype(bf16))

        @pl.when(k == nk - 1)
        def _():
            o_ref[...] = acc[...].astype(bf16)

    a_spec = pl.BlockSpec((None, tk, M), lambda j, k: (j, k, 0)) if ka else pl.BlockSpec((tk, M), lambda j, k: (k, 0))
    if kb:
        b_spec = pl.BlockSpec((None, tk, N), lambda j, k: (j, k, 0))
    elif K:
        b_spec = pl.BlockSpec((tk, N), lambda j, k: (k, 0))
    else:
        b_spec = pl.BlockSpec((tk, tn), lambda j, k: (k, j))
    if K:
        o_spec, o_shape = pl.BlockSpec((None, M, N), lambda j, k: (j, 0, 0)), (K, M, N)
    else:
        o_spec, o_shape = pl.BlockSpec((M, tn), lambda j, k: (0, j)), (M, N)
    return pl.pallas_call(
        body, name=name, grid=(nj, nk), in_specs=[a_spec, b_spec], out_specs=o_spec,
        out_shape=jax.ShapeDtypeStruct(o_shape, bf16), scratch_shapes=[pltpu.VMEM((M, tn), f32)],
        compiler_params=_params(("parallel", "arbitrary")))(a, b)


def _gm_pre(u, v, ln_g, ln_b):
    return _gelu(u), _layernorm(_gelu(v), ln_g, ln_b)


def _tril_mask():
    r = lax.broadcasted_iota(jnp.int32, (CHUNK, CHUNK), 0)
    c = lax.broadcasted_iota(jnp.int32, (CHUNK, CHUNK), 1)
    return c <= r


def _gm_mix(vnb, ws_ref, bst, mixed_sc, tm):
    mask = _tril_mask()
    for h in range(GM_HEADS):
        wt = jnp.where(mask, ws_ref[h], 0.0).astype(bf16)
        bias = bst[:, h:h + 1]
        for q in range(tm // CHUNK):
            rs = slice(q * CHUNK, (q + 1) * CHUNK)
            cs = slice(h * CHUNK, (h + 1) * CHUNK)
            mixed_sc[rs, cs] = _dot(wt, vnb[rs, cs]) + bias


def _mix_fwd(h1, gmix, w_uv, w_zxd, ln_g, ln_b, w_s, b_st, gout, tm=512):
    T, D = h1.shape
    tm = min(tm, T)
    G = GM_WIDTH

    def body(h_ref, g_ref, wuv_ref, wzxd_ref, lng_ref, lnb_ref, ws_ref, bst_ref, gout_ref,
             n_ref, uv_ref, z_ref, xbc_ref, dt_ref, ya_ref, mixed_sc):
        n = _rms(h_ref[...], g_ref[...]).astype(bf16)
        n_ref[...] = n
        u = _dot(n, wuv_ref[:, :G]).astype(bf16)
        v = _dot(n, wuv_ref[:, G:]).astype(bf16)
        uv_ref[:, :G] = u
        uv_ref[:, G:] = v
        z_ref[...] = _dot(n, wzxd_ref[:, :SSM_WIDTH]).astype(bf16)
        xbc_ref[...] = _dot(n, wzxd_ref[:, SSM_WIDTH:SSM_WIDTH + CONV_DIM]).astype(bf16)
        dt_ref[...] = _dot(n, wzxd_ref[:, SSM_WIDTH + CONV_DIM:])
        ug, vn = _gm_pre(u.astype(f32), v.astype(f32), lng_ref[...], lnb_ref[...])
        _gm_mix(vn.astype(bf16), ws_ref, bst_ref[...], mixed_sc, tm)
        ya_ref[...] = _rms(ug * mixed_sc[...], gout_ref[...]).astype(bf16)

    ins = [h1, gmix, w_uv, w_zxd, ln_g, ln_b, w_s, b_st, gout]
    in_specs = [_rows(tm, D)] + [_full(x.shape) for x in ins[1:]]
    outs = [jax.ShapeDtypeStruct((T, D), bf16), jax.ShapeDtypeStruct((T, 2 * G), bf16),
            jax.ShapeDtypeStruct((T, SSM_WIDTH), bf16), jax.ShapeDtypeStruct((T, CONV_DIM), bf16),
            jax.ShapeDtypeStruct((T, LANES), f32), jax.ShapeDtypeStruct((T, G), bf16)]
    out_specs = [_rows(tm, D), _rows(tm, 2 * G), _rows(tm, SSM_WIDTH), _rows(tm, CONV_DIM), _rows(tm, LANES), _rows(tm, G)]
    return pl.pallas_call(body, name="mix_fwd", grid=(T // tm,), in_specs=in_specs, out_specs=out_specs,
                          out_shape=outs, scratch_shapes=[pltpu.VMEM((tm, G), f32)],
                          compiler_params=_params(("parallel",)))(*ins)


def _gm_bwd(uv, dya, ln_g, ln_b, w_s, b_st, gout, tm=256):
    T = uv.shape[0]
    tm = min(tm, T)
    G = GM_WIDTH

    def body(uv_ref, dya_ref, lng_ref, lnb_ref, ws_ref, bst_ref, gout_ref,
             duv_ref, dlng_ref, dlnb_ref, dws_ref, dbst_ref, dgout_ref, mixed_sc, dvn_sc):
        @pl.when(pl.program_id(0) == 0)
        def _():
            for r in (dlng_ref, dlnb_ref, dws_ref, dbst_ref, dgout_ref):
                r[...] = jnp.zeros_like(r)

        u = uv_ref[:, :G].astype(f32)
        v = uv_ref[:, G:].astype(f32)
        (ug, vn), pre_vjp = jax.vjp(_gm_pre, u, v, lng_ref[...], lnb_ref[...])
        vnb = vn.astype(bf16)
        _gm_mix(vnb, ws_ref, bst_ref[...], mixed_sc, tm)
        mixed = mixed_sc[...]
        _, out_vjp = jax.vjp(_rms, ug * mixed, gout_ref[...])
        dpre, dgout = out_vjp(dya_ref[...].astype(f32))
        dgout_ref[...] += dgout
        dug = dpre * mixed
        dmixed = dpre * ug
        mask = _tril_mask()
        lane = lax.broadcasted_iota(jnp.int32, (1, GM_HEADS), 1)
        dbst = jnp.zeros((CHUNK, GM_HEADS), f32)
        for h in range(GM_HEADS):
            wt = jnp.where(mask, ws_ref[h], 0.0).astype(bf16)
            cs = slice(h * CHUNK, (h + 1) * CHUNK)
            dw = jnp.zeros((CHUNK, CHUNK), f32)
            for q in range(tm // CHUNK):
                rs = slice(q * CHUNK, (q + 1) * CHUNK)
                dm = dmixed[rs, cs]
                dmb = dm.astype(bf16)
                dw = dw + _dot_nt(dmb, vnb[rs, cs])
                dbst = dbst + jnp.sum(dm, axis=1, keepdims=True) * (lane == h).astype(f32)
                dvn_sc[rs, cs] = _dot_tn(wt, dmb)
            dws_ref[h] += jnp.where(mask, dw, 0.0)
        dbst_ref[...] += dbst
        du, dv, dlng, dlnb = pre_vjp((dug, dvn_sc[...]))
        duv_ref[:, :G] = du.astype(bf16)
        duv_ref[:, G:] = dv.astype(bf16)
        dlng_ref[...] += dlng
        dlnb_ref[...] += dlnb

    ins = [uv, dya, ln_g, ln_b, w_s, b_st, gout]
    in_specs = [_rows(tm, 2 * G), _rows(tm, G)] + [_full(x.shape) for x in ins[2:]]
    outs = [jax.ShapeDtypeStruct((T, 2 * G), bf16)] + [jax.ShapeDtypeStruct(x.shape, f32) for x in (ln_g, ln_b, w_s, b_st, gout)]
    out_specs = [_rows(tm, 2 * G)] + [_acc(x.shape) for x in (ln_g, ln_b, w_s, b_st, gout)]
    return pl.pallas_call(body, name="gm_bwd", grid=(T // tm,), in_specs=in_specs, out_specs=out_specs,
                          out_shape=outs, scratch_shapes=[pltpu.VMEM((tm, G), f32), pltpu.VMEM((tm, G), f32)],
                          compiler_params=_params(("arbitrary",)))(*ins)


def _mix_bwd(dh, h1, gmix, duv, dzxd, w_uv, w_zxd, tm=512):
    T, D = dh.shape
    tm = min(tm, T)

    def body(dh_ref, h_ref, g_ref, duv_ref, dzxd_ref, wuv_ref, wzxd_ref, dhi_ref, dg_ref):
        @pl.when(pl.program_id(0) == 0)
        def _():
            dg_ref[...] = jnp.zeros_like(dg_ref)

        dn = _dot_nt(duv_ref[...], wuv_ref[...]) + _dot_nt(dzxd_ref[...], wzxd_ref[...])
        _, vjp = jax.vjp(_rms, h_ref[...], g_ref[...])
        dx, dg = vjp(dn)
        dhi_ref[...] = dh_ref[...] + dx
        dg_ref[...] += dg

    ins = [dh, h1, gmix, duv, dzxd, w_uv, w_zxd]
    in_specs = [_rows(tm, D), _rows(tm, D), _full(gmix.shape), _rows(tm, duv.shape[1]), _rows(tm, dzxd.shape[1]),
                _full(w_uv.shape), _full(w_zxd.shape)]
    return pl.pallas_call(body, name="mix_bwd", grid=(T // tm,), in_specs=in_specs,
                          out_specs=[_rows(tm, D), _acc(gmix.shape)],
                          out_shape=[jax.ShapeDtypeStruct((T, D), f32), jax.ShapeDtypeStruct(gmix.shape, f32)],
                          compiler_params=_params(("arbitrary",)))(*ins)


HALO = 16


def _ssd_front(c, xbc_ref, halo_ref, dtr_ref, cw_ref, cb_ref, dtb_ref, alog_ref, ext_sc):
    halo = jnp.where(c > 0, halo_ref[...].astype(f32), 0.0)
    ext_sc[0:HALO, :] = halo
    ext_sc[HALO:HALO + CHUNK, :] = xbc_ref[...].astype(f32)
    xc = cb_ref[...] + jnp.zeros((CHUNK, CONV_DIM), f32)
    for j in range(SSM_CONV):
        xc = xc + cw_ref[j:j + 1, :] * ext_sc[pl.ds(HALO - SSM_CONV + 1 + j, CHUNK), :]
    sg = _sigmoid(xc)
    xa = xc * sg
    dt = _softplus(dtr_ref[...] + dtb_ref[...])
    a = -jnp.exp(alog_ref[...])
    adt = dt * a
    tri = _tril_mask().astype(f32)
    acs = jnp.dot(tri, adt, preferred_element_type=f32, precision=HIGHEST)
    return xc, sg, xa, dt, a, adt, acs


def _ssd_head(h, xa, dt, acs, acs_t, cbm, sp, dsk):
    cmb, bmb, cb = cbm
    P = SSM_HEAD_DIM
    x = xa[:, h * P:(h + 1) * P]
    dtc = dt[:, h:h + 1]
    acs_h = acs[:, h:h + 1]
    diff = acs_h - acs_t[h:h + 1, :]
    dk = jnp.exp(jnp.where(_tril_mask(), diff, -jnp.inf))
    m = cb * dk
    xd = x * dtc
    e = jnp.exp(acs_h)
    z = _dot_nt(cmb, sp.astype(bf16))
    y = _dot(m.astype(bf16), xd.astype(bf16)) + e * z + x * dsk[:, h:h + 1]
    tot = acs[CHUNK - 1:CHUNK, h:h + 1]
    r = jnp.exp(tot - acs_h)
    return x, dtc, dk, m, xd, e, z, y, tot, r


def _gated_norm(y, z, g):
    yg = y * (z * _sigmoid(z))
    half = SSM_WIDTH // SSM_GROUPS
    parts = []
    for k in range(SSM_GROUPS):
        s = yg[:, k * half:(k + 1) * half]
        parts.append(s * lax.rsqrt(jnp.mean(s * s, axis=-1, keepdims=True) + EPS))
    return jnp.concatenate(parts, axis=1) * g


def _group_mats(xa):
    out = []
    for g in range(SSM_GROUPS):
        bm = xa[:, SSM_WIDTH + g * SSM_STATE:SSM_WIDTH + (g + 1) * SSM_STATE].astype(bf16)
        cm = xa[:, SSM_WIDTH + (SSM_GROUPS + g) * SSM_STATE:SSM_WIDTH + (SSM_GROUPS + g + 1) * SSM_STATE].astype(bf16)
        out.append((cm, bm, _dot_nt(cm, bm)))
    return out


def _ssd_in_specs(nc, rev):
    def ci(i):
        return nc - 1 - i if rev else i
    hp = CHUNK // HALO
    return [pl.BlockSpec((CHUNK, CONV_DIM), lambda i: (ci(i), 0)),
            pl.BlockSpec((HALO, CONV_DIM), lambda i: (jnp.maximum(ci(i) * hp - 1, 0), 0)),
            pl.BlockSpec((CHUNK, SSM_WIDTH), lambda i: (ci(i), 0)),
            pl.BlockSpec((CHUNK, LANES), lambda i: (ci(i), 0))]


def _ssd_fwd(xbc, z, dtr, conv_w, conv_b, dt_bias, a_log, d_skip, ssm_norm):
    T = xbc.shape[0]
    nc = T // CHUNK
    H, P, N = SSM_HEADS, SSM_HEAD_DIM, SSM_STATE

    def body(xbc_ref, halo_ref, z_ref, dtr_ref, cw_ref, cb_ref, dtb_ref, alog_ref, dsk_ref, g_ref,
             yb_ref, sprev_ref, s_sc, ext_sc, y_sc):
        c = pl.program_id(0)

        @pl.when(c == 0)
        def _():
            s_sc[...] = jnp.zeros_like(s_sc)

        _, _, xa, dt, _, _, acs = _ssd_front(c, xbc_ref, halo_ref, dtr_ref, cw_ref, cb_ref, dtb_ref, alog_ref, ext_sc)
        acs_t = acs.T
        groups = _group_mats(xa)
        dsk = dsk_ref[...]
        for h in range(H):
            cbm = groups[h // (H // SSM_GROUPS)]
            sp = s_sc[h]
            _, _, _, _, xd, _, _, y, tot, r = _ssd_head(h, xa, dt, acs, acs_t, cbm, sp, dsk)
            y_sc[:, h * P:(h + 1) * P] = y
            sprev_ref[0, h] = sp
            s_sc[h] = jnp.exp(tot) * sp + _dot_tn((xd * r).astype(bf16), cbm[1])
        yb_ref[...] = _gated_norm(y_sc[...], z_ref[...].astype(f32), g_ref[...]).astype(bf16)

    params = [conv_w, conv_b, dt_bias, a_log, d_skip, ssm_norm]
    return pl.pallas_call(
        body, name="ssd_fwd", grid=(nc,),
        in_specs=_ssd_in_specs(nc, False) + [_full(x.shape) for x in params],
        out_specs=[pl.BlockSpec((CHUNK, SSM_WIDTH), lambda i: (i, 0)), pl.BlockSpec((1, H, P, N), lambda i: (i, 0, 0, 0))],
        out_shape=[jax.ShapeDtypeStruct((T, SSM_WIDTH), bf16), jax.ShapeDtypeStruct((nc, H, P, N), f32)],
        scratch_shapes=[pltpu.VMEM((H, P, N), f32), pltpu.VMEM((HALO + CHUNK, CONV_DIM), f32), pltpu.VMEM((CHUNK, SSM_WIDTH), f32)],
        compiler_params=_params(("arbitrary",)))(xbc, xbc, z, dtr, *params)


NEXT = 8


def _ssd_bwd(xbc, z, dtr, sprev, dyb, conv_w, conv_b, dt_bias, a_log, d_skip, ssm_norm):
    T = xbc.shape[0]
    nc = T // CHUNK
    H, P, N = SSM_HEADS, SSM_HEAD_DIM, SSM_STATE
    HG = H // SSM_GROUPS

    def body(xbc_ref, halo_ref, z_ref, dtr_ref, sprev_ref, dyb_ref, cw_ref, cb_ref, dtb_ref, alog_ref, dsk_ref, g_ref,
             dzxd_ref, dcw_ref, dcb_ref, ddtb_ref, dalog_ref, ddsk_ref, dg_ref,
             ds_sc, ext_sc, dxc_sc, y_sc, dxa_sc):
        i = pl.program_id(0)
        c = nc - 1 - i

        @pl.when(i == 0)
        def _():
            ds_sc[...] = jnp.zeros_like(ds_sc)
            dxc_sc[...] = jnp.zeros_like(dxc_sc)
            for r_ in (dcw_ref, dcb_ref, ddtb_ref, dalog_ref, ddsk_ref, dg_ref):
                r_[...] = jnp.zeros_like(r_)

        xc, sg, xa, dt, a, adt, acs = _ssd_front(c, xbc_ref, halo_ref, dtr_ref, cw_ref, cb_ref, dtb_ref, alog_ref, ext_sc)
        acs_t = acs.T
        groups = _group_mats(xa)
        dsk = dsk_ref[...]
        heads = []
        for h in range(H):
            res = _ssd_head(h, xa, dt, acs, acs_t, groups[h // HG], sprev_ref[0, h], dsk)
            y_sc[:, h * P:(h + 1) * P] = res[7]
            heads.append(res)
        zf = z_ref[...].astype(f32)
        _, gn_vjp = jax.vjp(_gated_norm, y_sc[...], zf, g_ref[...])
        dy, dz, dg = gn_vjp(dyb_ref[...].astype(f32))
        dg_ref[...] += dg
        dzxd_ref[:, :SSM_WIDTH] = dz.astype(bf16)

        lane = lax.broadcasted_iota(jnp.int32, (1, LANES), 1)
        sub = lax.broadcasted_iota(jnp.int32, (LANES, 1), 0)
        dacs = jnp.zeros((CHUNK, LANES), f32)
        dacs_r = jnp.zeros((LANES, CHUNK), f32)
        ddt = jnp.zeros((CHUNK, LANES), f32)
        dtot = jnp.zeros((1, LANES), f32)
        ddsk = jnp.zeros((1, LANES), f32)
        dcb = [jnp.zeros((CHUNK, CHUNK), f32) for _ in range(SSM_GROUPS)]
        dcm = [jnp.zeros((CHUNK, N), f32) for _ in range(SSM_GROUPS)]
        dbm = [jnp.zeros((CHUNK, N), f32) for _ in range(SSM_GROUPS)]
        for h in range(H):
            g = h // HG
            cmb, bmb, cb = groups[g]
            x, dtc, dk, m, xd, e, zz, _, tot, r = heads[h]
            sp = sprev_ref[0, h]
            oh = (lane == h).astype(f32)
            dyh = dy[:, h * P:(h + 1) * P]
            dyb_ = dyh.astype(bf16)
            dx = dyh * dsk[:, h:h + 1]
            ddsk = ddsk + jnp.sum(dyh * x) * oh
            dm = _dot_nt(dyb_, xd.astype(bf16))
            dxd = _dot_tn(m.astype(bf16), dyb_)
            w = dm * m
            dac = jnp.sum(w, axis=1, keepdims=True)
            dacs_r = dacs_r - (sub == h).astype(f32) * jnp.sum(w, axis=0, keepdims=True)
            dcb[g] = dcb[g] + dm * dk
            dzz = dyh * e
            dac = dac + jnp.sum(dzz * zz, axis=1, keepdims=True)
            dzb = dzz.astype(bf16)
            dcm[g] = dcm[g] + _dot(dzb, sp.astype(bf16))
            dsp = _dot_tn(dzb, cmb)
            dsn = ds_sc[h]
            dsnb = dsn.astype(bf16)
            et = jnp.exp(tot)
            dsp = dsp + et * dsn
            dt_h = et * jnp.sum(dsn * sp)
            gmat = xd * r
            dgm = _dot_nt(bmb, dsnb)
            dbm[g] = dbm[g] + _dot(gmat.astype(bf16), dsnb)
            dxd = dxd + dgm * r
            dr = jnp.sum(dgm * xd, axis=1, keepdims=True) * r
            dt_h = dt_h + jnp.sum(dr)
            dac = dac - dr
            dx = dx + dxd * dtc
            ddt = ddt + jnp.sum(dxd * x, axis=1, keepdims=True) * oh
            dacs = dacs + dac * oh
            dtot = dtot + dt_h * oh
            ds_sc[h] = dsp
            dxa_sc[:, h * P:(h + 1) * P] = dx
        rowi = lax.broadcasted_iota(jnp.int32, (CHUNK, 1), 0)
        dacs = dacs + dacs_r.T + jnp.where(rowi == CHUNK - 1, dtot, 0.0)
        r2 = lax.broadcasted_iota(jnp.int32, (CHUNK, CHUNK), 0)
        c2 = lax.broadcasted_iota(jnp.int32, (CHUNK, CHUNK), 1)
        upper = (c2 >= r2).astype(f32)
        dadt = jnp.dot(upper, dacs, preferred_element_type=f32, precision=HIGHEST)
        ddt = ddt + dadt * a
        dalog_ref[...] += jnp.sum(dadt * dt, axis=0, keepdims=True) * a
        ddsk_ref[...] += ddsk
        ddtr = jnp.where(lane < H, ddt * _sigmoid(dtr_ref[...] + dtb_ref[...]), 0.0)
        ddtb_ref[...] += jnp.sum(ddtr, axis=0, keepdims=True)
        dzxd_ref[:, SSM_WIDTH + CONV_DIM:] = ddtr.astype(bf16)
        for g in range(SSM_GROUPS):
            cmb, bmb, _ = groups[g]
            dcbb = dcb[g].astype(bf16)
            dxa_sc[:, SSM_WIDTH + g * N:SSM_WIDTH + (g + 1) * N] = dbm[g] + _dot_tn(dcbb, cmb)
            dxa_sc[:, SSM_WIDTH + (SSM_GROUPS + g) * N:SSM_WIDTH + (SSM_GROUPS + g + 1) * N] = dcm[g] + _dot(dcbb, bmb)
        dxc = dxa_sc[...] * (sg * (1.0 + xc * (1.0 - sg)))
        dxc_sc[0:CHUNK, :] = dxc
        dxbc = jnp.zeros((CHUNK, CONV_DIM), f32)
        dcw = []
        for j in range(SSM_CONV):
            dxbc = dxbc + cw_ref[j:j + 1, :] * dxc_sc[pl.ds(SSM_CONV - 1 - j, CHUNK), :]
            dcw.append(jnp.sum(dxc * ext_sc[pl.ds(HALO - SSM_CONV + 1 + j, CHUNK), :], axis=0, keepdims=True))
        dzxd_ref[:, SSM_WIDTH:SSM_WIDTH + CONV_DIM] = dxbc.astype(bf16)
        dcw_ref[...] += jnp.concatenate(dcw, axis=0)
        dcb_ref[...] += jnp.sum(dxc, axis=0, keepdims=True)
        dxc_sc[CHUNK:CHUNK + NEXT, :] = dxc[0:NEXT, :]

    params = [conv_w, conv_b, dt_bias, a_log, d_skip, ssm_norm]

    def rc(i):
        return nc - 1 - i

    in_specs = (_ssd_in_specs(nc, True)
                + [pl.BlockSpec((1, H, P, N), lambda i: (rc(i), 0, 0, 0)), pl.BlockSpec((CHUNK, SSM_WIDTH), lambda i: (rc(i), 0))]
                + [_full(x.shape) for x in params])
    return pl.pallas_call(
        body, name="ssd_bwd", grid=(nc,), in_specs=in_specs,
        out_specs=[pl.BlockSpec((CHUNK, ZXD), lambda i: (rc(i), 0))] + [_acc(x.shape) for x in params],
        out_shape=[jax.ShapeDtypeStruct((T, ZXD), bf16)] + [jax.ShapeDtypeStruct(x.shape, f32) for x in params],
        scratch_shapes=[pltpu.VMEM((H, P, N), f32), pltpu.VMEM((HALO + CHUNK, CONV_DIM), f32),
                        pltpu.VMEM((CHUNK + NEXT, CONV_DIM), f32), pltpu.VMEM((CHUNK, SSM_WIDTH), f32),
                        pltpu.VMEM((CHUNK, CONV_DIM), f32)],
        compiler_params=_params(("arbitrary",)))(xbc, xbc, z, dtr, sprev, dyb, *params)


def _tail(h3, p, tgt, gp, wpg, bpg, wpp, gf, tm=512):
    T, D = h3.shape
    tm = min(tm, T)

    def head(gpre, pp, h, gf_, t):
        gate = _sigmoid(gpre)
        y = _rms(h + gate * pp, gf_)
        err = y - t
        return 0.5 * jnp.sum(jnp.mean(err * err, axis=-1))

    def body(h_ref, p_ref, t_ref, gp_ref, wpg_ref, bpg_ref, wpp_ref, gf_ref,
             dh_ref, loss_ref, dgp_ref, dwpg_ref, dbpg_ref, dwpp_ref, dgf_ref):
        @pl.when(pl.program_id(0) == 0)
        def _():
            for r in (loss_ref, dgp_ref, dwpg_ref, dbpg_ref, dwpp_ref, dgf_ref):
                r[...] = jnp.zeros_like(r)

        h = h_ref[...]
        npf, np_vjp = jax.vjp(_rms, h, gp_ref[...])
        npb = npf.astype(bf16)
        pb = p_ref[...].astype(bf16)
        gpre = _dot(npb, wpg_ref[...]) + bpg_ref[...]
        kp, _, cp = wpp_ref.shape
        pp = jnp.concatenate([_dot(pb, wpp_ref[k]) for k in range(kp)], axis=1)
        loss, head_vjp = jax.vjp(head, gpre, pp, h, gf_ref[...], t_ref[...])
        dgpre, dpp, dh_a, dgf, _ = head_vjp(jnp.ones((), f32))
        loss_ref[...] += loss
        dgf_ref[...] += dgf
        dbpg_ref[...] += jnp.sum(dgpre, axis=0, keepdims=True)
        dgb = dgpre.astype(bf16)
        dwpg_ref[...] += _dot_tn(npb, dgb)
        dppb = dpp.astype(bf16)
        for k in range(kp):
            dwpp_ref[k] += _dot_tn(pb, dppb[:, k * cp:(k + 1) * cp])
        dh_b, dgp = np_vjp(_dot_nt(dgb, wpg_ref[...]))
        dgp_ref[...] += dgp
        dh_ref[...] = dh_a + dh_b

    ins = [h3, p, tgt, gp, wpg, bpg, wpp, gf]
    in_specs = [_rows(tm, D), _rows(tm, p.shape[1]), _rows(tm, D)] + [_full(x.shape) for x in ins[3:]]
    acc_shapes = [(1, LANES), gp.shape, wpg.shape, bpg.shape, wpp.shape, gf.shape]
    return pl.pallas_call(
        body, name="tail", grid=(T // tm,), in_specs=in_specs,
        out_specs=[_rows(tm, D)] + [_acc(s) for s in acc_shapes],
        out_shape=[jax.ShapeDtypeStruct((T, D), f32)] + [jax.ShapeDtypeStruct(s, f32) for s in acc_shapes],
        compiler_params=_params(("arbitrary",)))(*ins)


def _adamw(name, w, g, m, v, tr=256):
    R, C = w.shape
    tr = _row_tile(R, tr)

    def body(w_ref, g_ref, m_ref, v_ref, d_ref, mo_ref, vo_ref):
        g_ = g_ref[...]
        m_ = ADAM_B1 * m_ref[...] + (1.0 - ADAM_B1) * g_
        v_ = ADAM_B2 * v_ref[...] + (1.0 - ADAM_B2) * jnp.square(g_)
        m_hat = m_ / (1.0 - ADAM_B1 ** ADAM_STEP)
        v_hat = v_ / (1.0 - ADAM_B2 ** ADAM_STEP)
        d_ref[...] = -ADAM_LR * (m_hat / (jnp.sqrt(v_hat) + ADAM_EPS) + ADAM_WD * w_ref[...])
        mo_ref[...] = m_
        vo_ref[...] = v_

    spec = pl.BlockSpec((tr, C), lambda i: (i, 0))
    return pl.pallas_call(body, name=name, grid=(R // tr,), in_specs=[spec] * 4, out_specs=[spec] * 3,
                          out_shape=[jax.ShapeDtypeStruct((R, C), f32)] * 3,
                          compiler_params=_params(("parallel",)))(w, g, m, v)


HBM = pl.BlockSpec(memory_space=pltpu.HBM)


def _me():
    return lax.axis_index("x"), lax.axis_index("y"), lax.axis_index("c")


def _other_chips(x, y):
    return [(1 - x, y), (x, 1 - y), (1 - x, 1 - y)]


def _remote(src, dst, send_sem, recv_sem, dev):
    return pltpu.make_async_remote_copy(src_ref=src, dst_ref=dst, send_sem=send_sem, recv_sem=recv_sem,
                                        device_id=dev, device_id_type=MESH)


def _sems(n):
    return [pltpu.SemaphoreType.DMA((n,)), pltpu.SemaphoreType.DMA((n,))]


def _gather_weights(shards, split):
    n = len(shards)

    def body(*refs):
        ins, outs = refs[:n], refs[n:2 * n]
        own_send, own_recv, ici_send, ici_recv, d2d_send, d2d_recv = refs[2 * n:]
        x, y, c = _me()
        my_chip = 2 * x + y
        sibling = (x, y, 1 - c)
        chips = _other_chips(x, y)

        def rows(i, half):
            hr = shards[i].shape[0] // 2
            return pl.ds(half * hr, hr) if split[i] else pl.ds(0, shards[i].shape[0])

        sends = []
        for i in range(n):
            for j, chip in enumerate(chips):
                cp = _remote(ins[i].at[rows(i, c)], outs[i].at[my_chip, rows(i, c)],
                             ici_send.at[3 * i + j], ici_recv.at[3 * i + j], (*chip, c))
                cp.start()
                sends.append(cp)
            cp = _remote(ins[i], outs[i].at[my_chip], own_send.at[i], own_recv.at[i], sibling)
            cp.start()
            sends.append(cp)
        for i in range(n):
            for j, chip in enumerate(chips):
                s = 3 * i + j
                land = outs[i].at[2 * chip[0] + chip[1], rows(i, c)]
                _remote(land, land, ici_send.at[s], ici_recv.at[s], (*chip, c)).wait_recv()
                if split[i]:
                    cp = _remote(land, land, d2d_send.at[s], d2d_recv.at[s], sibling)
                    cp.start()
                    sends.append(cp)
        for i in range(n):
            _remote(ins[i], outs[i].at[my_chip], own_send.at[i], own_recv.at[i], sibling).wait_recv()
            if split[i]:
                for j, chip in enumerate(chips):
                    s = 3 * i + j
                    land = outs[i].at[2 * chip[0] + chip[1], rows(i, 1 - c)]
                    _remote(land, land, d2d_send.at[s], d2d_recv.at[s], sibling).wait_recv()
        for cp in sends:
            cp.wait_send()

    return pl.pallas_call(
        body, name="gather_weights", out_shape=[jax.ShapeDtypeStruct((N_CHIPS,) + s.shape, s.dtype) for s in shards],
        in_specs=[HBM] * n, out_specs=[HBM] * n,
        scratch_shapes=_sems(n) + _sems(3 * n) + _sems(3 * n))(*shards)


def _swap_halves(grads):
    n = len(grads)

    def body(*refs):
        ins, outs, send, recv = refs[:n], refs[n:2 * n], refs[2 * n], refs[2 * n + 1]
        x, y, c = _me()
        copies = []
        for i in range(n):
            hr = grads[i].shape[1] // 2
            cp = _remote(ins[i].at[:, pl.ds((1 - c) * hr, hr), :], outs[i], send.at[i], recv.at[i], (x, y, 1 - c))
            cp.start()
            copies.append(cp)
        for cp in copies:
            cp.wait()

    return pl.pallas_call(
        body, name="swap_halves",
        out_shape=[jax.ShapeDtypeStruct((g.shape[0], g.shape[1] // 2, g.shape[2]), g.dtype) for g in grads],
        in_specs=[HBM] * n, out_specs=[HBM] * n, scratch_shapes=_sems(n))(*grads)


def _add_halves(name, grads, other, c_idx, th=256):
    K, R, C = grads.shape
    H = R // 2
    th = _row_tile(H, th, 16)
    nb = H // th

    def body(c_ref, g_ref, o_ref, out_ref):
        out_ref[...] = (g_ref[...].astype(f32) + o_ref[...].astype(f32)).astype(bf16)

    grid_spec = pltpu.PrefetchScalarGridSpec(
        num_scalar_prefetch=1, grid=(nb,),
        in_specs=[pl.BlockSpec((K, th, C), lambda i, c: (0, c[0] * nb + i, 0)),
                  pl.BlockSpec((K, th, C), lambda i, c: (0, i, 0))],
        out_specs=pl.BlockSpec((K, th, C), lambda i, c: (0, i, 0)))
    return pl.pallas_call(body, name=name, grid_spec=grid_spec,
                          out_shape=jax.ShapeDtypeStruct((K, H, C), bf16),
                          compiler_params=_params(("parallel",)))(c_idx, grads, other)


def _exchange_partials(parts, smalls):
    n, ns = len(parts), len(smalls)

    def body(*refs):
        p_in, s_in = refs[:n], refs[n:n + ns]
        p_out, s_out = refs[n + ns:2 * n + ns], refs[2 * n + ns:2 * n + 2 * ns]
        psend, precv, ssend, srecv, local_sems = refs[2 * n + 2 * ns:]
        x, y, c = _me()
        my_dev = 4 * x + 2 * y + c
        local = [pltpu.make_async_copy(s_in[i], s_out[i].at[my_dev], local_sems.at[i]) for i in range(ns)]
        for cp in local:
            cp.start()
        copies = []
        for i in range(n):
            for j, chip in enumerate(_other_chips(x, y)):
                cp = _remote(p_in[i].at[2 * chip[0] + chip[1]], p_out[i].at[j], psend.at[3 * i + j], precv.at[3 * i + j], (*chip, c))
                cp.start()
                copies.append(cp)
        for i in range(ns):
            k = 0
            for dx in range(2):
                for dy in range(2):
                    for dc in range(2):
                        if dx or dy or dc:
                            cp = _remote(s_in[i], s_out[i].at[my_dev], ssend.at[7 * i + k], srecv.at[7 * i + k], (x ^ dx, y ^ dy, c ^ dc))
                            cp.start()
                            copies.append(cp)
                            k += 1
        for cp in copies:
            cp.wait()
        for cp in local:
            cp.wait()

    return pl.pallas_call(
        body, name="exchange_partials",
        out_shape=([jax.ShapeDtypeStruct((3,) + p_.shape[1:], p_.dtype) for p_ in parts]
                   + [jax.ShapeDtypeStruct((N_DEV,) + s.shape, s.dtype) for s in smalls]),
        in_specs=[HBM] * (n + ns), out_specs=[HBM] * (n + ns),
        scratch_shapes=_sems(3 * n) + _sems(7 * ns) + [pltpu.SemaphoreType.DMA((ns,))])(*parts, *smalls)


def _sum_partials(name, part, recv, chip_idx, th=256):
    K, H, C = part.shape
    th = _row_tile(H, th, 16)

    def body(chip_ref, p_ref, r_ref, o_ref):
        acc = p_ref[...].astype(f32)
        for j in range(3):
            acc = acc + r_ref[j].astype(f32)
        o_ref[...] = acc

    grid_spec = pltpu.PrefetchScalarGridSpec(
        num_scalar_prefetch=1, grid=(H // th,),
        in_specs=[pl.BlockSpec((None, th, C), lambda i, chip: (chip[0], i, 0)),
                  pl.BlockSpec((3, th, C), lambda i, chip: (0, i, 0))],
        out_specs=pl.BlockSpec((th, C), lambda i, chip: (i, 0)))
    return pl.pallas_call(body, name=name, grid_spec=grid_spec, out_shape=jax.ShapeDtypeStruct((H, C), f32),
                          compiler_params=_params(("parallel",)))(chip_idx, part, recv)


def _sum_slots(name, slots, tr):
    K, R, C = slots.shape
    tr = _row_tile(R, tr, 16)

    def body(s_ref, o_ref):
        acc = s_ref[0].astype(f32)
        for k in range(1, K):
            acc = acc + s_ref[k].astype(f32)
        o_ref[...] = acc

    return pl.pallas_call(body, name=name, grid=(R // tr,),
                          in_specs=[pl.BlockSpec((K, tr, C), lambda i: (0, i, 0))],
                          out_specs=pl.BlockSpec((tr, C), lambda i: (i, 0)),
                          out_shape=jax.ShapeDtypeStruct((R, C), f32),
                          compiler_params=_params(("parallel",)))(slots)


def _share_halves(halves):
    n = len(halves)

    def body(*refs):
        ins, outs, send, recv = refs[:n], refs[n:2 * n], refs[2 * n], refs[2 * n + 1]
        x, y, c = _me()
        copies = []
        for i in range(n):
            cp = _remote(ins[i], outs[i], send.at[i], recv.at[i], (x, y, 1 - c))
            cp.start()
            copies.append(cp)
        for cp in copies:
            cp.wait()

    return pl.pallas_call(
        body, name="share_halves", out_shape=[jax.ShapeDtypeStruct(h.shape, h.dtype) for h in halves],
        in_specs=[HBM] * n, out_specs=[HBM] * n, scratch_shapes=_sems(n))(*halves)


def _adamw_big(name, w, g_mine, g_theirs, m, v, c_idx, tr=256):
    R, C = w.shape
    H = R // 2
    tr = _row_tile(H, tr)
    nb = H // tr

    def body(c_ref, w_ref, gm_ref, gt_ref, m_ref, v_ref, g_ref, d_ref, mo_ref, vo_ref):
        g_ = jnp.where(pl.program_id(0) // nb == c_ref[0], gm_ref[...], gt_ref[...])
        g_ref[...] = g_
        m_ = ADAM_B1 * m_ref[...] + (1.0 - ADAM_B1) * g_
        v_ = ADAM_B2 * v_ref[...] + (1.0 - ADAM_B2) * jnp.square(g_)
        m_hat = m_ / (1.0 - ADAM_B1 ** ADAM_STEP)
        v_hat = v_ / (1.0 - ADAM_B2 ** ADAM_STEP)
        d_ref[...] = -ADAM_LR * (m_hat / (jnp.sqrt(v_hat) + ADAM_EPS) + ADAM_WD * w_ref[...])
        mo_ref[...] = m_
        vo_ref[...] = v_

    full = pl.BlockSpec((tr, C), lambda i, c: (i, 0))
    half = pl.BlockSpec((tr, C), lambda i, c: (i % nb, 0))
    grid_spec = pltpu.PrefetchScalarGridSpec(num_scalar_prefetch=1, grid=(2 * nb,),
                                             in_specs=[full, half, half, full, full], out_specs=[full] * 4)
    return pl.pallas_call(body, name=name, grid_spec=grid_spec, out_shape=[jax.ShapeDtypeStruct((R, C), f32)] * 4,
                          compiler_params=_params(("parallel",)))(c_idx, w, g_mine, g_theirs, m, v)


BIG = ("ffn1_w_gate", "ffn1_w_up", "ffn1_w_down", "w_in", "w_out", "ffn2_w_gate", "ffn2_w_up", "ffn2_w_down",
       "ple_w_gate", "ple_w_proj")


SMALL = ("ffn1_norm", "mix_norm", "gm_ln_g", "gm_ln_b", "gm_w_s", "gm_b_s", "gm_out_norm", "conv_b", "dt_bias", "a_log",
         "d_skip", "ssm_norm", "ffn2_norm", "ple_norm", "ple_b_gate", "final_norm")
SMALL_C = 1024


def _pack_small(vals):
    parts = []
    for v in vals:
        f = v.astype(f32).reshape(-1)
        parts.append(jnp.pad(f, (0, -f.shape[0] % SMALL_C)))
    flat = jnp.concatenate(parts)
    rows = flat.shape[0] // SMALL_C
    return jnp.pad(flat, (0, (-rows % 8) * SMALL_C)).reshape(-1, SMALL_C)


def _unpack_small(pack, shapes):
    flat = pack.reshape(-1)
    out, off = [], 0
    for s in shapes:
        n = 1
        for d in s:
            n *= d
        out.append(flat[off:off + n].reshape(s))
        off += n + (-n % SMALL_C)
    return out


def _pad_lanes(v):
    return jnp.pad(v, ((0, 0), (0, LANES - v.shape[1])))


def _local_step(x, p, tgt, W, conv_w, S):
    G = GM_WIDTH
    K = N_CHIPS
    w_in = jnp.concatenate([W["w_in"][k] for k in range(K)], axis=1)
    w_uv = w_in[:, :2 * G]
    w_zxd = jnp.pad(w_in[:, 2 * G:], ((0, 0), (0, ZXD - (IN_PROJ - 2 * G))))
    wo = W["w_out"].reshape(-1, D_MODEL)
    wpg = W["ple_w_gate"].reshape(-1, D_MODEL)
    b_st = S["gm_b_s"][0].T
    w_s = S["gm_w_s"][0]
    dtb, alog, dsk = _pad_lanes(S["dt_bias"]), _pad_lanes(S["a_log"]), _pad_lanes(S["d_skip"])
    gfin = S["final_norm"].reshape(1, -1)

    h1, n1, a1, b1 = _ffn_fwd("ffn1_fwd", x, S["ffn1_norm"], W["ffn1_w_gate"], W["ffn1_w_up"], W["ffn1_w_down"])
    n2, uv, z, xbc, dtr, ya = _mix_fwd(h1, S["mix_norm"], w_uv, w_zxd, S["gm_ln_g"], S["gm_ln_b"], w_s, b_st, S["gm_out_norm"])
    yb, sprev = _ssd_fwd(xbc, z, dtr, conv_w, S["conv_b"], dtb, alog, dsk, S["ssm_norm"])
    h2, h3, n3, a2, b2 = _ffn_fwd("ffn2_fwd", h1, S["ffn2_norm"], W["ffn2_w_gate"], W["ffn2_w_up"], W["ffn2_w_down"],
                                  pre=(ya, yb, wo))
    dh3, loss, dgp, dwpg, dbpg, dwpp, dgf = _tail(h3, p, tgt, S["ple_norm"], wpg, S["ple_b_gate"], W["ple_w_proj"], gfin)
    dh2, da2, db2, hm2, dg_ffn2, dya, dyb = _ffn_bwd("ffn2_bwd", dh3, h2, S["ffn2_norm"], a2, b2, W["ffn2_w_gate"],
                                                     W["ffn2_w_up"], W["ffn2_w_down"], wo=wo, ga=G)
    gW = {}
    gW["ffn2_w_gate"] = _matmul_tn("dw_ffn2_gate", n3, da2)
    gW["ffn2_w_up"] = _matmul_tn("dw_ffn2_up", n3, db2)
    gW["ffn2_w_down"] = _matmul_tn("dw_ffn2_down", hm2, dh3, scale=0.5)
    gW["w_out"] = jnp.concatenate([_matmul_tn("dw_out_a", ya, dh2), _matmul_tn("dw_out_b", yb, dh2)], axis=0).reshape(W["w_out"].shape)
    duv, dlng, dlnb, dws, dbst, dgout = _gm_bwd(uv, dya, S["gm_ln_g"], S["gm_ln_b"], w_s, b_st, S["gm_out_norm"])
    dzxd, dcw, dcb, ddtb, dalog, ddsk, dgssm = _ssd_bwd(xbc, z, dtr, sprev, dyb, conv_w, S["conv_b"], dtb, alog, dsk, S["ssm_norm"])
    dh1, dg_mix = _mix_bwd(dh2, h1, S["mix_norm"], duv, dzxd, w_uv, w_zxd)
    dw_in = jnp.concatenate([_matmul_tn("dw_in_uv", n2, duv), _matmul_tn("dw_in_zxd", n2, dzxd)[:, :IN_PROJ - 2 * G]], axis=1)
    gW["w_in"] = jnp.transpose(dw_in.reshape(D_MODEL, K, IN_PROJ // K), (1, 0, 2))
    dx, da1, db1, hm1, dg_ffn1 = _ffn_bwd("ffn1_bwd", dh1, x, S["ffn1_norm"], a1, b1, W["ffn1_w_gate"], W["ffn1_w_up"], W["ffn1_w_down"])
    gW["ffn1_w_gate"] = _matmul_tn("dw_ffn1_gate", n1, da1)
    gW["ffn1_w_up"] = _matmul_tn("dw_ffn1_up", n1, db1)
    gW["ffn1_w_down"] = _matmul_tn("dw_ffn1_down", hm1, dh1, scale=0.5)
    gW["ple_w_gate"] = dwpg.astype(bf16).reshape(W["ple_w_gate"].shape)
    gW["ple_w_proj"] = dwpp.astype(bf16)
    nh = SSM_HEADS
    gS = {"ffn1_norm": dg_ffn1, "mix_norm": dg_mix, "gm_ln_g": dlng, "gm_ln_b": dlnb, "gm_w_s": dws[None], "gm_b_s": dbst.T[None],
          "gm_out_norm": dgout, "conv_b": dcb, "dt_bias": ddtb[:, :nh], "a_log": dalog[:, :nh], "d_skip": ddsk[:, :nh],
          "ssm_norm": dgssm, "ffn2_norm": dg_ffn2, "ple_norm": dgp, "ple_b_gate": dbpg, "final_norm": dgf.reshape(-1)}
    return loss, dx, gW, dcw, gS


_WEIGHTS = ("ffn1_norm", "ffn1_w_gate", "ffn1_w_up", "ffn1_w_down", "mix_norm", "w_in", "gm_ln_g", "gm_ln_b", "gm_w_s", "gm_b_s",
            "gm_out_norm", "conv_w", "conv_b", "dt_bias", "a_log", "d_skip", "ssm_norm", "w_out", "ffn2_norm", "ffn2_w_gate",
            "ffn2_w_up", "ffn2_w_down", "ple_norm", "ple_w_gate", "ple_b_gate", "ple_w_proj", "final_norm")
_BIG_NAMES = BIG


def kernel(x, p, ffn1_norm, ffn1_w_gate, ffn1_w_up, ffn1_w_down, mix_norm, w_in, gm_ln_g, gm_ln_b, gm_w_s, gm_b_s, gm_out_norm, conv_w, conv_b, dt_bias, a_log, d_skip, ssm_norm, w_out, ffn2_norm, ffn2_w_gate, ffn2_w_up, ffn2_w_down, ple_norm, ple_w_gate, ple_b_gate, ple_w_proj, final_norm, loss_target, m_ffn1_norm, m_ffn1_w_gate, m_ffn1_w_up, m_ffn1_w_down, m_mix_norm, m_w_in, m_gm_ln_g, m_gm_ln_b, m_gm_w_s, m_gm_b_s, m_gm_out_norm, m_conv_w, m_conv_b, m_dt_bias, m_a_log, m_d_skip, m_ssm_norm, m_w_out, m_ffn2_norm, m_ffn2_w_gate, m_ffn2_w_up, m_ffn2_w_down, m_ple_norm, m_ple_w_gate, m_ple_b_gate, m_ple_w_proj, m_final_norm, v_ffn1_norm, v_ffn1_w_gate, v_ffn1_w_up, v_ffn1_w_down, v_mix_norm, v_w_in, v_gm_ln_g, v_gm_ln_b, v_gm_w_s, v_gm_b_s, v_gm_out_norm, v_conv_w, v_conv_b, v_dt_bias, v_a_log, v_d_skip, v_ssm_norm, v_w_out, v_ffn2_norm, v_ffn2_w_gate, v_ffn2_w_up, v_ffn2_w_down, v_ple_norm, v_ple_w_gate, v_ple_b_gate, v_ple_w_proj, v_final_norm):
    given = dict(locals())
    w = {n: given[n] for n in _WEIGHTS}
    m = {n: given["m_" + n] for n in _WEIGHTS}
    v = {n: given["v_" + n] for n in _WEIGHTS}

    cw_shard = w["conv_w"][0]
    nbig = len(BIG)
    gathered = _gather_weights([w[n][0].astype(bf16) for n in BIG] + [cw_shard], [True] * nbig + [False])
    W = dict(zip(BIG, gathered[:nbig]))
    cw_full = jnp.transpose(gathered[nbig], (1, 0, 2)).reshape(cw_shard.shape[0], -1)

    S = {n: w[n] for n in SMALL}
    loss, dx, gW, dcw, gS = _local_step(x[0], p[0, 0], loss_target[0], W, cw_full, S)

    c_idx = lax.axis_index("c").astype(jnp.int32).reshape(1)
    chip = 2 * lax.axis_index("x") + lax.axis_index("y")
    chip_idx = chip.astype(jnp.int32).reshape(1)
    grads = [gW[n] for n in BIG]
    others = _swap_halves(grads)
    parts = [_add_halves("add_" + n, g_, o_, c_idx) for n, g_, o_ in zip(BIG, grads, others)]
    small = _pack_small([gS[n] for n in SMALL] + [dcw, loss[:, :1]])
    *recv, smalls = _exchange_partials(parts, [small])
    mine = [_sum_partials("sum_" + n, p_, r_, chip_idx) for n, p_, r_ in zip(BIG, parts, recv)]
    theirs = _share_halves(mine)
    small_shapes = [w[n].shape for n in SMALL] + [dcw.shape, (1, 1)]
    small_sum = _unpack_small(_sum_slots("sum_small", smalls, 512), small_shapes)
    g = {n: small_sum[i] for i, n in enumerate(SMALL)}
    cshard = cw_shard.shape[1]
    g["conv_w"] = lax.dynamic_slice_in_dim(small_sum[len(SMALL)], chip * cshard, cshard, axis=1)[None]
    loss_total = small_sum[len(SMALL) + 1].reshape(())

    delta, new_m, new_v = {}, {}, {}
    sm_names = SMALL + ("conv_w",)
    sm_shapes = [w[n].shape for n in sm_names]
    d_s, m_s, v_s = _adamw("adamw_small", _pack_small([w[n] for n in sm_names]), _pack_small([g[n] for n in sm_names]),
                           _pack_small([m[n] for n in sm_names]), _pack_small([v[n] for n in sm_names]))
    for dst, src in ((delta, d_s), (new_m, m_s), (new_v, v_s)):
        for n, val in zip(sm_names, _unpack_small(src, sm_shapes)):
            dst[n] = val
    for n, gm_, gt_ in zip(BIG, mine, theirs):
        g_, d_, m_, v_ = _adamw_big("adamw_" + n, w[n][0], gm_, gt_, m[n][0], v[n][0], c_idx)
        g[n], delta[n], new_m[n], new_v[n] = g_[None], d_[None], m_[None], v_[None]

    return (loss_total, dx[None], *[g[n] for n in _WEIGHTS], *[delta[n] for n in _WEIGHTS],
            *[new_m[n] for n in _WEIGHTS], *[new_v[n] for n in _WEIGHTS])
```

```python
import functools

import jax
import jax.numpy as jnp
from jax import lax
from jax.experimental import pallas as pl
from jax.experimental.pallas import tpu as pltpu

f32 = jnp.float32
bf16 = jnp.bfloat16
MESH = pl.DeviceIdType.MESH
HIGHEST = lax.Precision.HIGHEST

EPS = 1e-6
N_CHIPS = 4
N_DEV = 8
D_MODEL = 1024
D_FF = 2816
D_PLE = 256
GM_WIDTH = 1024
GM_HEADS = 8
CHUNK = 128
SSM_WIDTH = 1024
SSM_HEADS = 16
SSM_HEAD_DIM = 64
SSM_GROUPS = 2
SSM_STATE = 128
SSM_CONV = 4
CONV_DIM = SSM_WIDTH + 2 * SSM_GROUPS * SSM_STATE
IN_PROJ = 2 * GM_WIDTH + SSM_WIDTH + CONV_DIM + SSM_HEADS
LANES = 128
ZXD = SSM_WIDTH + CONV_DIM + LANES

ADAM_LR = 0.001
ADAM_B1 = 0.9
ADAM_B2 = 0.999
ADAM_EPS = 1e-08
ADAM_WD = 0.01
ADAM_STEP = 10

VMEM_LIMIT = 56 * 1024 * 1024


def _dot(a, b):
    return jnp.dot(a, b, preferred_element_type=f32)


def _dot_nt(a, b):
    return lax.dot_general(a, b, (((1,), (1,)), ((), ())), preferred_element_type=f32)


def _dot_tn(a, b):
    return lax.dot_general(a, b, (((0,), (0,)), ((), ())), preferred_element_type=f32)


def _rms(x, g):
    return x * lax.rsqrt(jnp.mean(x * x, axis=-1, keepdims=True) + EPS) * g


def _gelu(x):
    return 0.5 * x * (1.0 + lax.erf(x * 0.7071067811865476))


def _layernorm(x, g, b):
    mu = jnp.mean(x, axis=-1, keepdims=True)
    xc = x - mu
    return xc * lax.rsqrt(jnp.mean(xc * xc, axis=-1, keepdims=True) + EPS) * g + b


def _sigmoid(x):
    return 1.0 / (1.0 + jnp.exp(-x))


def _softplus(x):
    return jnp.maximum(x, 0.0) + jnp.log(1.0 + jnp.exp(-jnp.abs(x)))


def _full(shape):
    nd = len(shape)
    return pl.BlockSpec(shape, lambda *_: (0,) * nd, pipeline_mode=pl.Buffered(1))


def _acc(shape):
    nd = len(shape)
    return pl.BlockSpec(shape, lambda *_: (0,) * nd)


def _rows(tm, ncols):
    return pl.BlockSpec((tm, ncols), lambda i: (i, 0))


def _params(sem):
    return pltpu.CompilerParams(dimension_semantics=sem, vmem_limit_bytes=VMEM_LIMIT)


def _row_tile(rows, target, mult=8):
    best = rows
    for t in range(mult, min(rows, target) + 1, mult):
        if rows % t == 0:
            best = t
    return best if best <= target else rows


def _stack_rows(k, tm, ncols):
    return pl.BlockSpec((k, tm, ncols), lambda i: (0, i, 0))


def _ffn_fwd(name, h, g, wg, wu, wd, pre=None, tm=256):
    T, D = h.shape
    K, _, Fs = wg.shape
    tm = min(tm, T)

    def body(*refs):
        if pre is None:
            h_ref, g_ref, wg_ref, wu_ref, wd_ref, ho_ref, n_ref, a_ref, b_ref = refs
            hin = h_ref[...]
        else:
            (h_ref, ya_ref, yb_ref, wo_ref, g_ref, wg_ref, wu_ref, wd_ref,
             hi_ref, ho_ref, n_ref, a_ref, b_ref) = refs
            ga = ya_ref.shape[1]
            hin = h_ref[...] + _dot(ya_ref[...], wo_ref[:ga, :]) + _dot(yb_ref[...], wo_ref[ga:, :])
            hi_ref[...] = hin
        n = _rms(hin, g_ref[...]).astype(bf16)
        n_ref[...] = n
        acc = jnp.zeros((tm, D), f32)
        for k in range(K):
            a = _dot(n, wg_ref[k]).astype(bf16)
            b = _dot(n, wu_ref[k]).astype(bf16)
            a_ref[k] = a
            b_ref[k] = b
            af = a.astype(f32)
            hm = (af * _sigmoid(af) * b.astype(f32)).astype(bf16)
            acc = acc + _dot(hm, wd_ref[k])
        ho_ref[...] = hin + 0.5 * acc

    ins = [h] + (list(pre) if pre is not None else []) + [g, wg, wu, wd]
    in_specs = [_rows(tm, D)]
    if pre is not None:
        in_specs += [_rows(tm, pre[0].shape[1]), _rows(tm, pre[1].shape[1]), _full(pre[2].shape)]
    in_specs += [_full(g.shape), _full(wg.shape), _full(wu.shape), _full(wd.shape)]
    outs = [jax.ShapeDtypeStruct((T, D), f32), jax.ShapeDtypeStruct((T, D), bf16),
            jax.ShapeDtypeStruct((K, T, Fs), bf16), jax.ShapeDtypeStruct((K, T, Fs), bf16)]
    out_specs = [_rows(tm, D), _rows(tm, D), _stack_rows(K, tm, Fs), _stack_rows(K, tm, Fs)]
    if pre is not None:
        outs = [jax.ShapeDtypeStruct((T, D), f32)] + outs
        out_specs = [_rows(tm, D)] + out_specs
    return pl.pallas_call(body, name=name, grid=(T // tm,), in_specs=in_specs, out_specs=out_specs,
                          out_shape=outs, compiler_params=_params(("parallel",)))(*ins)


def _ffn_bwd(name, dh, hin, g, a, b, wg, wu, wd, wo=None, ga=0, tm=256):
    T, D = dh.shape
    K, _, Fs = wg.shape
    tm = min(tm, T)

    def body(*refs):
        if wo is None:
            (dh_ref, hin_ref, g_ref, a_ref, b_ref, wg_ref, wu_ref, wd_ref,
             dhi_ref, da_ref, db_ref, hm_ref, dg_ref) = refs
        else:
            (dh_ref, hin_ref, g_ref, a_ref, b_ref, wg_ref, wu_ref, wd_ref, wo_ref,
             dhi_ref, da_ref, db_ref, hm_ref, dg_ref, dya_ref, dyb_ref) = refs

        @pl.when(pl.program_id(0) == 0)
        def _():
            dg_ref[...] = jnp.zeros_like(dg_ref)

        dh_ = dh_ref[...]
        dhb = (0.5 * dh_).astype(bf16)
        dn = jnp.zeros((tm, D), f32)
        for k in range(K):
            dhm = _dot_nt(dhb, wd_ref[k])
            af = a_ref[k].astype(f32)
            bf = b_ref[k].astype(f32)
            sg = _sigmoid(af)
            sl_ = af * sg
            da = (dhm * bf * (sg * (1.0 + af * (1.0 - sg)))).astype(bf16)
            db = (dhm * sl_).astype(bf16)
            da_ref[k] = da
            db_ref[k] = db
            hm_ref[k] = (sl_ * bf).astype(bf16)
            dn = dn + _dot_nt(da, wg_ref[k]) + _dot_nt(db, wu_ref[k])
        _, vjp = jax.vjp(_rms, hin_ref[...], g_ref[...])
        dx, dg = vjp(dn)
        dhi = dh_ + dx
        dhi_ref[...] = dhi
        dg_ref[...] += dg
        if wo is not None:
            dhib = dhi.astype(bf16)
            dya_ref[...] = _dot_nt(dhib, wo_ref[:ga, :]).astype(bf16)
            dyb_ref[...] = _dot_nt(dhib, wo_ref[ga:, :]).astype(bf16)

    ins = [dh, hin, g, a, b, wg, wu, wd]
    in_specs = [_rows(tm, D), _rows(tm, D), _full(g.shape), _stack_rows(K, tm, Fs), _stack_rows(K, tm, Fs),
                _full(wg.shape), _full(wu.shape), _full(wd.shape)]
    act = jax.ShapeDtypeStruct((K, T, Fs), bf16)
    outs = [jax.ShapeDtypeStruct((T, D), f32), act, act, act, jax.ShapeDtypeStruct(g.shape, f32)]
    out_specs = [_rows(tm, D), _stack_rows(K, tm, Fs), _stack_rows(K, tm, Fs), _stack_rows(K, tm, Fs), _acc(g.shape)]
    if wo is not None:
        gb = wo.shape[0] - ga
        ins += [wo]
        in_specs += [_full(wo.shape)]
        outs += [jax.ShapeDtypeStruct((T, ga), bf16), jax.ShapeDtypeStruct((T, gb), bf16)]
        out_specs += [_rows(tm, ga), _rows(tm, gb)]
    return pl.pallas_call(body, name=name, grid=(T // tm,), in_specs=in_specs, out_specs=out_specs,
                          out_shape=outs, compiler_params=_params(("arbitrary",)))(*ins)


def _matmul_tn(name, a, b, scale=1.0, tk=1024):
    ka = a.shape[0] if a.ndim == 3 else 0
    kb = b.shape[0] if b.ndim == 3 else 0
    K = max(ka, kb)
    T, M = a.shape[-2:]
    N = b.shape[-1]
    tk = min(tk, T)
    nk = T // tk
    if K:
        tn, nj = N, K
    else:
        tn = LANES * max(d for d in range(1, N // LANES + 1) if (N // LANES) % d == 0 and (d == 1 or M * d * LANES * 4 <= 6 * 1024 * 1024))
        nj = N // tn

    def body(a_ref, b_ref, o_ref, acc):
        k = pl.program_id(1)

        @pl.when(k == 0)
        def _():
            acc[...] = jnp.zeros_like(acc)

        bb = b_ref[...]
        if scale != 1.0:
            bb = bb * scale
        acc[...] += _dot_tn(a_ref[...].astype(bf16), bb.astype(bf16))

        @pl.when(k == nk - 1)
        def _():
            o_ref[...] = acc[...].astype(bf16)

    a_spec = pl.BlockSpec((None, tk, M), lambda j, k: (j, k, 0)) if ka else pl.BlockSpec((tk, M), lambda j, k: (k, 0))
    if kb:
        b_spec = pl.BlockSpec((None, tk, N), lambda j, k: (j, k, 0))
    elif K:
        b_spec = pl.BlockSpec((tk, N), lambda j, k: (k, 0))
    else:
        b_spec = pl.BlockSpec((tk, tn), lambda j, k: (k, j))
    if K:
        o_spec, o_shape = pl.BlockSpec((None, M, N), lambda j, k: (j, 0, 0)), (K, M, N)
    else:
        o_spec, o_shape = pl.BlockSpec((M, tn), lambda j, k: (0, j)), (M, N)
    return pl.pallas_call(
        body, name=name, grid=(nj, nk), in_specs=[a_spec, b_spec], out_specs=o_spec,
        out_shape=jax.ShapeDtypeStruct(o_shape, bf16), scratch_shapes=[pltpu.VMEM((M, tn), f32)],
        compiler_params=_params(("parallel", "arbitrary")))(a, b)


def _gm_pre(u, v, ln_g, ln_b):
    return _gelu(u), _layernorm(_gelu(v), ln_g, ln_b)


def _tril_mask():
    r = lax.broadcasted_iota(jnp.int32, (CHUNK, CHUNK), 0)
    c = lax.broadcasted_iota(jnp.int32, (CHUNK, CHUNK), 1)
    return c <= r


def _gm_mix(vnb, ws_ref, bst, mixed_sc, tm):
    mask = _tril_mask()
    for h in range(GM_HEADS):
        wt = jnp.where(mask, ws_ref[h], 0.0).astype(bf16)
        bias = bst[:, h:h + 1]
        for q in range(tm // CHUNK):
            rs = slice(q * CHUNK, (q + 1) * CHUNK)
            cs = slice(h * CHUNK, (h + 1) * CHUNK)
            mixed_sc[rs, cs] = _dot(wt, vnb[rs, cs]) + bias


def _mix_fwd(h1, gmix, w_uv, w_zxd, ln_g, ln_b, w_s, b_st, gout, tm=512):
    T, D = h1.shape
    tm = min(tm, T)
    G = GM_WIDTH

    def body(h_ref, g_ref, wuv_ref, wzxd_ref, lng_ref, lnb_ref, ws_ref, bst_ref, gout_ref,
             n_ref, uv_ref, z_ref, xbc_ref, dt_ref, ya_ref, mixed_sc):
        n = _rms(h_ref[...], g_ref[...]).astype(bf16)
        n_ref[...] = n
        u = _dot(n, wuv_ref[:, :G]).astype(bf16)
        v = _dot(n, wuv_ref[:, G:]).astype(bf16)
        uv_ref[:, :G] = u
        uv_ref[:, G:] = v
        z_ref[...] = _dot(n, wzxd_ref[:, :SSM_WIDTH]).astype(bf16)
        xbc_ref[...] = _dot(n, wzxd_ref[:, SSM_WIDTH:SSM_WIDTH + CONV_DIM]).astype(bf16)
        dt_ref[...] = _dot(n, wzxd_ref[:, SSM_WIDTH + CONV_DIM:])
        ug, vn = _gm_pre(u.astype(f32), v.astype(f32), lng_ref[...], lnb_ref[...])
        _gm_mix(vn.astype(bf16), ws_ref, bst_ref[...], mixed_sc, tm)
        ya_ref[...] = _rms(ug * mixed_sc[...], gout_ref[...]).astype(bf16)

    ins = [h1, gmix, w_uv, w_zxd, ln_g, ln_b, w_s, b_st, gout]
    in_specs = [_rows(tm, D)] + [_full(x.shape) for x in ins[1:]]
    outs = [jax.ShapeDtypeStruct((T, D), bf16), jax.ShapeDtypeStruct((T, 2 * G), bf16),
            jax.ShapeDtypeStruct((T, SSM_WIDTH), bf16), jax.ShapeDtypeStruct((T, CONV_DIM), bf16),
            jax.ShapeDtypeStruct((T, LANES), f32), jax.ShapeDtypeStruct((T, G), bf16)]
    out_specs = [_rows(tm, D), _rows(tm, 2 * G), _rows(tm, SSM_WIDTH), _rows(tm, CONV_DIM), _rows(tm, LANES), _rows(tm, G)]
    return pl.pallas_call(body, name="mix_fwd", grid=(T // tm,), in_specs=in_specs, out_specs=out_specs,
                          out_shape=outs, scratch_shapes=[pltpu.VMEM((tm, G), f32)],
                          compiler_params=_params(("parallel",)))(*ins)


def _gm_bwd(uv, dya, ln_g, ln_b, w_s, b_st, gout, tm=256):
    T = uv.shape[0]
    tm = min(tm, T)
    G = GM_WIDTH

    def body(uv_ref, dya_ref, lng_ref, lnb_ref, ws_ref, bst_ref, gout_ref,
             duv_ref, dlng_ref, dlnb_ref, dws_ref, dbst_ref, dgout_ref, mixed_sc, dvn_sc):
        @pl.when(pl.program_id(0) == 0)
        def _():
            for r in (dlng_ref, dlnb_ref, dws_ref, dbst_ref, dgout_ref):
                r[...] = jnp.zeros_like(r)

        u = uv_ref[:, :G].astype(f32)
        v = uv_ref[:, G:].astype(f32)
        (ug, vn), pre_vjp = jax.vjp(_gm_pre, u, v, lng_ref[...], lnb_ref[...])
        vnb = vn.astype(bf16)
        _gm_mix(vnb, ws_ref, bst_ref[...], mixed_sc, tm)
        mixed = mixed_sc[...]
        _, out_vjp = jax.vjp(_rms, ug * mixed, gout_ref[...])
        dpre, dgout = out_vjp(dya_ref[...].astype(f32))
        dgout_ref[...] += dgout
        dug = dpre * mixed
        dmixed = dpre * ug
        mask = _tril_mask()
        lane = lax.broadcasted_iota(jnp.int32, (1, GM_HEADS), 1)
        dbst = jnp.zeros((CHUNK, GM_HEADS), f32)
        for h in range(GM_HEADS):
            wt = jnp.where(mask, ws_ref[h], 0.0).astype(bf16)
            cs = slice(h * CHUNK, (h + 1) * CHUNK)
            dw = jnp.zeros((CHUNK, CHUNK), f32)
            for q in range(tm // CHUNK):
                rs = slice(q * CHUNK, (q + 1) * CHUNK)
                dm = dmixed[rs, cs]
                dmb = dm.astype(bf16)
                dw = dw + _dot_nt(dmb, vnb[rs, cs])
                dbst = dbst + jnp.sum(dm, axis=1, keepdims=True) * (lane == h).astype(f32)
                dvn_sc[rs, cs] = _dot_tn(wt, dmb)
            dws_ref[h] += jnp.where(mask, dw, 0.0)
        dbst_ref[...] += dbst
        du, dv, dlng, dlnb = pre_vjp((dug, dvn_sc[...]))
        duv_ref[:, :G] = du.astype(bf16)
        duv_ref[:, G:] = dv.astype(bf16)
        dlng_ref[...] += dlng
        dlnb_ref[...] += dlnb

    ins = [uv, dya, ln_g, ln_b, w_s, b_st, gout]
    in_specs = [_rows(tm, 2 * G), _rows(tm, G)] + [_full(x.shape) for x in ins[2:]]
    outs = [jax.ShapeDtypeStruct((T, 2 * G), bf16)] + [jax.ShapeDtypeStruct(x.shape, f32) for x in (ln_g, ln_b, w_s, b_st, gout)]
    out_specs = [_rows(tm, 2 * G)] + [_acc(x.shape) for x in (ln_g, ln_b, w_s, b_st, gout)]
    return pl.pallas_call(body, name="gm_bwd", grid=(T // tm,), in_specs=in_specs, out_specs=out_specs,
                          out_shape=outs, scratch_shapes=[pltpu.VMEM((tm, G), f32), pltpu.VMEM((tm, G), f32)],
                          compiler_params=_params(("arbitrary",)))(*ins)


def _mix_bwd(dh, h1, gmix, duv, dzxd, w_uv, w_zxd, tm=512):
    T, D = dh.shape
    tm = min(tm, T)

    def body(dh_ref, h_ref, g_ref, duv_ref, dzxd_ref, wuv_ref, wzxd_ref, dhi_ref, dg_ref):
        @pl.when(pl.program_id(0) == 0)
        def _():
            dg_ref[...] = jnp.zeros_like(dg_ref)

        dn = _dot_nt(duv_ref[...], wuv_ref[...]) + _dot_nt(dzxd_ref[...], wzxd_ref[...])
        _, vjp = jax.vjp(_rms, h_ref[...], g_ref[...])
        dx, dg = vjp(dn)
        dhi_ref[...] = dh_ref[...] + dx
        dg_ref[...] += dg

    ins = [dh, h1, gmix, duv, dzxd, w_uv, w_zxd]
    in_specs = [_rows(tm, D), _rows(tm, D), _full(gmix.shape), _rows(tm, duv.shape[1]), _rows(tm, dzxd.shape[1]),
                _full(w_uv.shape), _full(w_zxd.shape)]
    return pl.pallas_call(body, name="mix_bwd", grid=(T // tm,), in_specs=in_specs,
                          out_specs=[_rows(tm, D), _acc(gmix.shape)],
                          out_shape=[jax.ShapeDtypeStruct((T, D), f32), jax.ShapeDtypeStruct(gmix.shape, f32)],
                          compiler_params=_params(("arbitrary",)))(*ins)


HALO = 16
PAIRS = SSM_HEADS // 2
PAIR_W = 2 * SSM_HEAD_DIM


def _split(x, n):
    parts = []
    for _ in range(n):
        p = x.astype(bf16)
        parts.append(p)
        x = x - p.astype(f32)
    return parts


def _dot_sel(x, sel, n):
    out = None
    for p in _split(x, n):
        t = _dot(p, sel)
        out = t if out is None else out + t
    return out


def _sel_dot(sel, x, n):
    out = None
    for p in _split(x, n):
        t = _dot(sel, p)
        out = t if out is None else out + t
    return out


def _expand_mat():
    r = lax.broadcasted_iota(jnp.int32, (LANES, SSM_WIDTH), 0)
    c = lax.broadcasted_iota(jnp.int32, (LANES, SSM_WIDTH), 1)
    return (lax.shift_right_logical(c, 6) == r).astype(bf16)


def _reduce_mat():
    r = lax.broadcasted_iota(jnp.int32, (SSM_WIDTH, LANES), 0)
    c = lax.broadcasted_iota(jnp.int32, (SSM_WIDTH, LANES), 1)
    return (lax.shift_right_logical(r, 6) == c).astype(bf16)


def _shift_mat(rows, cols, off):
    r = lax.broadcasted_iota(jnp.int32, (rows, cols), 0)
    c = lax.broadcasted_iota(jnp.int32, (rows, cols), 1)
    return (c == r + off).astype(bf16)


def _ssd_front(c, xbc_ref, halo_ref, dtr_ref, cw_ref, cb_ref, dtb_ref, alog_ref):
    halo = halo_ref[...]
    ext = jnp.concatenate([jnp.where(c > 0, halo, jnp.zeros_like(halo)), xbc_ref[...]], axis=0)
    taps = [_dot(_shift_mat(CHUNK, HALO + CHUNK, HALO - SSM_CONV + 1 + j), ext) for j in range(SSM_CONV - 1)]
    taps.append(xbc_ref[...].astype(f32))
    xc = cb_ref[...] + cw_ref[0:1, :] * taps[0]
    for j in range(1, SSM_CONV):
        xc = xc + cw_ref[j:j + 1, :] * taps[j]
    sg = _sigmoid(xc)
    xa = xc * sg
    dt = _softplus(dtr_ref[...] + dtb_ref[...])
    a = -jnp.exp(alog_ref[...])
    acs = jnp.dot(_tril_mask().astype(f32), dt * a, preferred_element_type=f32, precision=HIGHEST)
    return taps, xc, sg, xa, dt, a, acs


def _ssd_wide(xa, dt, acs, dsk):
    ex = _expand_mat()
    dt_x = _dot_sel(dt, ex, 3)
    acs_x = _dot_sel(acs, ex, 3)
    dsk_x = _dot_sel(jnp.broadcast_to(dsk, (8, LANES)), ex, 3)[0:1]
    e_x = jnp.exp(acs_x)
    r_x = jnp.exp(acs_x[CHUNK - 1:CHUNK, :] - acs_x)
    xs = xa[:, :SSM_WIDTH]
    xd = xs * dt_x
    return dt_x, dsk_x, e_x, r_x, xs, xd, xd * r_x


def _pair_stack(v, lo):
    return jnp.concatenate([jnp.where(lo, v, 0.0), jnp.where(lo, 0.0, v)], axis=0)


def _ssd_pair(j, acs, acs_t, cb):
    out = []
    tril = _tril_mask()
    for h in (2 * j, 2 * j + 1):
        dk = jnp.exp(jnp.where(tril, acs[:, h:h + 1] - acs_t[h:h + 1, :], -jnp.inf))
        out.append((dk, cb * dk))
    return out


def _pair_col(row_lo, tot, j):
    return jnp.exp(jnp.where(row_lo, tot[:, 2 * j:2 * j + 1], tot[:, 2 * j + 1:2 * j + 2]))


def _gated_norm(y, z, g):
    yg = y * (z * _sigmoid(z))
    half = SSM_WIDTH // SSM_GROUPS
    parts = []
    for k in range(SSM_GROUPS):
        s = yg[:, k * half:(k + 1) * half]
        parts.append(s * lax.rsqrt(jnp.mean(s * s, axis=-1, keepdims=True) + EPS))
    return jnp.concatenate(parts, axis=1) * g


def _group_mats(xa):
    out = []
    for g in range(SSM_GROUPS):
        bm = xa[:, SSM_WIDTH + g * SSM_STATE:SSM_WIDTH + (g + 1) * SSM_STATE].astype(bf16)
        cm = xa[:, SSM_WIDTH + (SSM_GROUPS + g) * SSM_STATE:SSM_WIDTH + (SSM_GROUPS + g + 1) * SSM_STATE].astype(bf16)
        out.append((cm, bm, _dot_nt(cm, bm)))
    return out


def _ssd_in_specs(nc, rev):
    def ci(i):
        return nc - 1 - i if rev else i
    hp = CHUNK // HALO
    return [pl.BlockSpec((CHUNK, CONV_DIM), lambda i: (ci(i), 0)),
            pl.BlockSpec((HALO, CONV_DIM), lambda i: (jnp.maximum(ci(i) * hp - 1, 0), 0)),
            pl.BlockSpec((CHUNK, SSM_WIDTH), lambda i: (ci(i), 0)),
            pl.BlockSpec((CHUNK, LANES), lambda i: (ci(i), 0))]


def _ssd_fwd(xbc, z, dtr, conv_w, conv_b, dt_bias, a_log, d_skip, ssm_norm):
    T = xbc.shape[0]
    nc = T // CHUNK
    N = SSM_STATE

    def body(xbc_ref, halo_ref, z_ref, dtr_ref, cw_ref, cb_ref, dtb_ref, alog_ref, dsk_ref, g_ref,
             yb_ref, sprev_ref, s_sc):
        c = pl.program_id(0)

        @pl.when(c == 0)
        def _():
            s_sc[...] = jnp.zeros_like(s_sc)

        _, _, _, xa, dt, _, acs = _ssd_front(c, xbc_ref, halo_ref, dtr_ref, cw_ref, cb_ref, dtb_ref, alog_ref)
        _, dsk_x, e_x, _, xs, xd, gm = _ssd_wide(xa, dt, acs, dsk_ref[...])
        acs_t = acs.T
        tot = acs[CHUNK - 1:CHUNK, :]
        groups = _group_mats(xa)
        lo = lax.broadcasted_iota(jnp.int32, (CHUNK, PAIR_W), 1) < SSM_HEAD_DIM
        row_lo = lax.broadcasted_iota(jnp.int32, (PAIR_W, 1), 0) < SSM_HEAD_DIM
        ys = []
        for j in range(PAIRS):
            cmb, bmb, cb = groups[j // (PAIRS // SSM_GROUPS)]
            ps = slice(j * PAIR_W, (j + 1) * PAIR_W)
            (_, m0), (_, m1) = _ssd_pair(j, acs, acs_t, cb)
            sp = s_sc[j]
            yd = _dot(jnp.concatenate([m0, m1], axis=1).astype(bf16), _pair_stack(xd[:, ps], lo).astype(bf16))
            ys.append(yd + e_x[:, ps] * _dot_nt(cmb, sp.astype(bf16)))
            sprev_ref[0, j] = sp
            s_sc[j] = _pair_col(row_lo, tot, j) * sp + _dot_tn(gm[:, ps].astype(bf16), bmb)
        y = jnp.concatenate(ys, axis=1) + xs * dsk_x
        yb_ref[...] = _gated_norm(y, z_ref[...].astype(f32), g_ref[...]).astype(bf16)

    params = [conv_w, conv_b, dt_bias, a_log, d_skip, ssm_norm]
    return pl.pallas_call(
        body, name="ssd_fwd", grid=(nc,),
        in_specs=_ssd_in_specs(nc, False) + [_full(x.shape) for x in params],
        out_specs=[pl.BlockSpec((CHUNK, SSM_WIDTH), lambda i: (i, 0)), pl.BlockSpec((1, PAIRS, PAIR_W, N), lambda i: (i, 0, 0, 0))],
        out_shape=[jax.ShapeDtypeStruct((T, SSM_WIDTH), bf16), jax.ShapeDtypeStruct((nc, PAIRS, PAIR_W, N), f32)],
        scratch_shapes=[pltpu.VMEM((PAIRS, PAIR_W, N), f32)],
        compiler_params=_params(("arbitrary",)))(xbc, xbc, z, dtr, *params)


def _ssd_bwd(xbc, z, dtr, sprev, dyb, conv_w, conv_b, dt_bias, a_log, d_skip, ssm_norm):
    T = xbc.shape[0]
    nc = T // CHUNK
    H, N = SSM_HEADS, SSM_STATE
    PG = PAIRS // SSM_GROUPS

    def body(xbc_ref, halo_ref, z_ref, dtr_ref, sprev_ref, dyb_ref, cw_ref, cb_ref, dtb_ref, alog_ref, dsk_ref, g_ref,
             dzxd_ref, dcw_ref, dcb_ref, ddtb_ref, dalog_ref, ddsk_ref, dg_ref, ds_sc, next_sc):
        i = pl.program_id(0)
        c = nc - 1 - i

        @pl.when(i == 0)
        def _():
            ds_sc[...] = jnp.zeros_like(ds_sc)
            next_sc[...] = jnp.zeros_like(next_sc)
            for r_ in (dcw_ref, dcb_ref, ddtb_ref, dalog_ref, ddsk_ref, dg_ref):
                r_[...] = jnp.zeros_like(r_)

        taps, xc, sg, xa, dt, a, acs = _ssd_front(c, xbc_ref, halo_ref, dtr_ref, cw_ref, cb_ref, dtb_ref, alog_ref)
        dt_x, dsk_x, e_x, r_x, xs, xd, gm = _ssd_wide(xa, dt, acs, dsk_ref[...])
        acs_t = acs.T
        tot = acs[CHUNK - 1:CHUNK, :]
        groups = _group_mats(xa)
        lo = lax.broadcasted_iota(jnp.int32, (CHUNK, PAIR_W), 1) < SSM_HEAD_DIM
        row_lo = lax.broadcasted_iota(jnp.int32, (PAIR_W, 1), 0) < SSM_HEAD_DIM
        pairs, zs, yds = [], [], []
        for j in range(PAIRS):
            cmb, _, cb = groups[j // PG]
            ps = slice(j * PAIR_W, (j + 1) * PAIR_W)
            pairs.append(_ssd_pair(j, acs, acs_t, cb))
            (_, m0), (_, m1) = pairs[j]
            zs.append(_dot_nt(cmb, sprev_ref[0, j].astype(bf16)))
            yds.append(_dot(jnp.concatenate([m0, m1], axis=1).astype(bf16), _pair_stack(xd[:, ps], lo).astype(bf16)))
        zf = jnp.concatenate(zs, axis=1)
        y = jnp.concatenate(yds, axis=1) + e_x * zf + xs * dsk_x
        _, gn_vjp = jax.vjp(_gated_norm, y, z_ref[...].astype(f32), g_ref[...])
        dy, dz, dg = gn_vjp(dyb_ref[...].astype(f32))
        dg_ref[...] += dg
        dzxd_ref[:, :SSM_WIDTH] = dz.astype(bf16)

        lane = lax.broadcasted_iota(jnp.int32, (1, LANES), 1)
        sub = lax.broadcasted_iota(jnp.int32, (LANES, 1), 0)
        dacs = jnp.zeros((CHUNK, LANES), f32)
        dacs_r = jnp.zeros((LANES, CHUNK), f32)
        dtot = jnp.zeros((1, LANES), f32)
        dcb = [jnp.zeros((CHUNK, CHUNK), f32) for _ in range(SSM_GROUPS)]
        dcm = [jnp.zeros((CHUNK, N), f32) for _ in range(SSM_GROUPS)]
        dbm = [jnp.zeros((CHUNK, N), f32) for _ in range(SSM_GROUPS)]
        dxds, dgms = [], []
        for j in range(PAIRS):
            g = j // PG
            cmb, bmb, _ = groups[g]
            ps = slice(j * PAIR_W, (j + 1) * PAIR_W)
            (dk0, m0), (dk1, m1) = pairs[j]
            oh0, oh1 = (lane == 2 * j).astype(f32), (lane == 2 * j + 1).astype(f32)
            dyp = dy[:, ps]
            dy2 = _pair_stack(dyp, lo).astype(bf16)
            dm2 = _dot_nt(dy2, xd[:, ps].astype(bf16))
            m2 = jnp.concatenate([m0, m1], axis=0)
            dxds.append(_dot_tn(m2.astype(bf16), dy2))
            w2 = dm2 * m2
            rs = jnp.sum(w2, axis=1, keepdims=True)
            dacs = dacs + rs[:CHUNK] * oh0 + rs[CHUNK:] * oh1
            dacs_r = dacs_r - ((sub == 2 * j).astype(f32) * jnp.sum(w2[:CHUNK], axis=0, keepdims=True)
                               + (sub == 2 * j + 1).astype(f32) * jnp.sum(w2[CHUNK:], axis=0, keepdims=True))
            dcb[g] = dcb[g] + dm2[:CHUNK] * dk0 + dm2[CHUNK:] * dk1
            sp = sprev_ref[0, j]
            dzb = (dyp * e_x[:, ps]).astype(bf16)
            dcm[g] = dcm[g] + _dot(dzb, sp.astype(bf16))
            dsn = ds_sc[j]
            dsnb = dsn.astype(bf16)
            et = _pair_col(row_lo, tot, j)
            rr = jnp.sum(dsn * sp, axis=1, keepdims=True) * et
            dtot = dtot + jnp.sum(rr[:SSM_HEAD_DIM]) * oh0 + jnp.sum(rr[SSM_HEAD_DIM:]) * oh1
            dgms.append(_dot_nt(bmb, dsnb))
            dbm[g] = dbm[g] + _dot(gm[:, ps].astype(bf16), dsnb)
            ds_sc[j] = _dot_tn(dzb, cmb) + et * dsn
        dgm = jnp.concatenate(dgms, axis=1)
        dxd = jnp.concatenate(dxds, axis=1) + dgm * r_x
        dr = dgm * gm
        red = _dot_sel(jnp.concatenate([dy * e_x * zf - dr, dr, dxd * xs, dy * xs], axis=0), _reduce_mat(), 2)
        rowi = lax.broadcasted_iota(jnp.int32, (CHUNK, 1), 0)
        dtot = dtot + jnp.sum(red[CHUNK:2 * CHUNK], axis=0, keepdims=True)
        dacs = dacs + red[:CHUNK] + dacs_r.T + jnp.where(rowi == CHUNK - 1, dtot, 0.0)
        r2 = lax.broadcasted_iota(jnp.int32, (CHUNK, CHUNK), 0)
        c2 = lax.broadcasted_iota(jnp.int32, (CHUNK, CHUNK), 1)
        dadt = jnp.dot((c2 >= r2).astype(f32), dacs, preferred_element_type=f32, precision=HIGHEST)
        ddt = red[2 * CHUNK:3 * CHUNK] + dadt * a
        dalog_ref[...] += jnp.sum(dadt * dt, axis=0, keepdims=True) * a
        ddsk_ref[...] += jnp.sum(red[3 * CHUNK:], axis=0, keepdims=True)
        ddtr = jnp.where(lane < H, ddt * _sigmoid(dtr_ref[...] + dtb_ref[...]), 0.0)
        ddtb_ref[...] += jnp.sum(ddtr, axis=0, keepdims=True)
        dzxd_ref[:, SSM_WIDTH + CONV_DIM:] = ddtr.astype(bf16)
        dxa_bm, dxa_cm = [], []
        for g in range(SSM_GROUPS):
            cmb, bmb, _ = groups[g]
            dcbb = dcb[g].astype(bf16)
            dxa_bm.append(dbm[g] + _dot_tn(dcbb, cmb))
            dxa_cm.append(dcm[g] + _dot(dcbb, bmb))
        dxc = jnp.concatenate([dy * dsk_x + dxd * dt_x] + dxa_bm + dxa_cm, axis=1) * (sg * (1.0 + xc * (1.0 - sg)))
        ext = jnp.concatenate([dxc, next_sc[...]], axis=0)
        dxbc = cw_ref[SSM_CONV - 1:SSM_CONV, :] * dxc
        for s in range(1, SSM_CONV):
            dxbc = dxbc + cw_ref[SSM_CONV - 1 - s:SSM_CONV - s, :] * _sel_dot(_shift_mat(CHUNK, CHUNK + HALO, s), ext, 2)
        dzxd_ref[:, SSM_WIDTH:SSM_WIDTH + CONV_DIM] = dxbc.astype(bf16)
        dcw_ref[...] += jnp.concatenate([jnp.sum(dxc * t, axis=0, keepdims=True) for t in taps], axis=0)
        dcb_ref[...] += jnp.sum(dxc, axis=0, keepdims=True)
        next_sc[...] = dxc[0:HALO, :]

    params = [conv_w, conv_b, dt_bias, a_log, d_skip, ssm_norm]

    def rc(i):
        return nc - 1 - i

    in_specs = (_ssd_in_specs(nc, True)
                + [pl.BlockSpec((1, PAIRS, PAIR_W, N), lambda i: (rc(i), 0, 0, 0)), pl.BlockSpec((CHUNK, SSM_WIDTH), lambda i: (rc(i), 0))]
                + [_full(x.shape) for x in params])
    return pl.pallas_call(
        body, name="ssd_bwd", grid=(nc,), in_specs=in_specs,
        out_specs=[pl.BlockSpec((CHUNK, ZXD), lambda i: (rc(i), 0))] + [_acc(x.shape) for x in params],
        out_shape=[jax.ShapeDtypeStruct((T, ZXD), bf16)] + [jax.ShapeDtypeStruct(x.shape, f32) for x in params],
        scratch_shapes=[pltpu.VMEM((PAIRS, PAIR_W, N), f32), pltpu.VMEM((HALO, CONV_DIM), f32)],
        compiler_params=_params(("arbitrary",)))(xbc, xbc, z, dtr, sprev, dyb, *params)


def _tail(h3, p, tgt, gp, wpg, bpg, wpp, gf, tm=512):
    T, D = h3.shape
    tm = min(tm, T)

    def head(gpre, pp, h, gf_, t):
        gate = _sigmoid(gpre)
        y = _rms(h + gate * pp, gf_)
        err = y - t
        return 0.5 * jnp.sum(jnp.mean(err * err, axis=-1))

    def body(h_ref, p_ref, t_ref, gp_ref, wpg_ref, bpg_ref, wpp_ref, gf_ref,
             dh_ref, loss_ref, dgp_ref, dwpg_ref, dbpg_ref, dwpp_ref, dgf_ref):
        @pl.when(pl.program_id(0) == 0)
        def _():
            for r in (loss_ref, dgp_ref, dwpg_ref, dbpg_ref, dwpp_ref, dgf_ref):
                r[...] = jnp.zeros_like(r)

        h = h_ref[...]
        npf, np_vjp = jax.vjp(_rms, h, gp_ref[...])
        npb = npf.astype(bf16)
        pb = p_ref[...].astype(bf16)
        gpre = _dot(npb, wpg_ref[...]) + bpg_ref[...]
        kp, _, cp = wpp_ref.shape
        pp = jnp.concatenate([_dot(pb, wpp_ref[k]) for k in range(kp)], axis=1)
        loss, head_vjp = jax.vjp(head, gpre, pp, h, gf_ref[...], t_ref[...])
        dgpre, dpp, dh_a, dgf, _ = head_vjp(jnp.ones((), f32))
        loss_ref[...] += loss
        dgf_ref[...] += dgf
        dbpg_ref[...] += jnp.sum(dgpre, axis=0, keepdims=True)
        dgb = dgpre.astype(bf16)
        dwpg_ref[...] += _dot_tn(npb, dgb)
        dppb = dpp.astype(bf16)
        for k in range(kp):
            dwpp_ref[k] += _dot_tn(pb, dppb[:, k * cp:(k + 1) * cp])
        dh_b, dgp = np_vjp(_dot_nt(dgb, wpg_ref[...]))
        dgp_ref[...] += dgp
        dh_ref[...] = dh_a + dh_b

    ins = [h3, p, tgt, gp, wpg, bpg, wpp, gf]
    in_specs = [_rows(tm, D), _rows(tm, p.shape[1]), _rows(tm, D)] + [_full(x.shape) for x in ins[3:]]
    acc_shapes = [(1, LANES), gp.shape, wpg.shape, bpg.shape, wpp.shape, gf.shape]
    return pl.pallas_call(
        body, name="tail", grid=(T // tm,), in_specs=in_specs,
        out_specs=[_rows(tm, D)] + [_acc(s) for s in acc_shapes],
        out_shape=[jax.ShapeDtypeStruct((T, D), f32)] + [jax.ShapeDtypeStruct(s, f32) for s in acc_shapes],
        compiler_params=_params(("arbitrary",)))(*ins)


def _adamw(name, w, g, m, v, tr=256):
    R, C = w.shape
    tr = _row_tile(R, tr)

    def body(w_ref, g_ref, m_ref, v_ref, d_ref, mo_ref, vo_ref):
        g_ = g_ref[...]
        m_ = ADAM_B1 * m_ref[...] + (1.0 - ADAM_B1) * g_
        v_ = ADAM_B2 * v_ref[...] + (1.0 - ADAM_B2) * jnp.square(g_)
        m_hat = m_ / (1.0 - ADAM_B1 ** ADAM_STEP)
        v_hat = v_ / (1.0 - ADAM_B2 ** ADAM_STEP)
        d_ref[...] = -ADAM_LR * (m_hat / (jnp.sqrt(v_hat) + ADAM_EPS) + ADAM_WD * w_ref[...])
        mo_ref[...] = m_
        vo_ref[...] = v_

    spec = pl.BlockSpec((tr, C), lambda i: (i, 0))
    return pl.pallas_call(body, name=name, grid=(R // tr,), in_specs=[spec] * 4, out_specs=[spec] * 3,
                          out_shape=[jax.ShapeDtypeStruct((R, C), f32)] * 3,
                          compiler_params=_params(("parallel",)))(w, g, m, v)


HBM = pl.BlockSpec(memory_space=pltpu.HBM)


def _me():
    return lax.axis_index("x"), lax.axis_index("y"), lax.axis_index("c")


def _other_chips(x, y):
    return [(1 - x, y), (x, 1 - y), (1 - x, 1 - y)]


def _remote(src, dst, send_sem, recv_sem, dev):
    return pltpu.make_async_remote_copy(src_ref=src, dst_ref=dst, send_sem=send_sem, recv_sem=recv_sem,
                                        device_id=dev, device_id_type=MESH)


def _sems(n):
    return [pltpu.SemaphoreType.DMA((n,)), pltpu.SemaphoreType.DMA((n,))]


def _gather_weights(shards, split):
    n = len(shards)

    def body(*refs):
        ins, outs = refs[:n], refs[n:2 * n]
        own_send, own_recv, ici_send, ici_recv, d2d_send, d2d_recv = refs[2 * n:]
        x, y, c = _me()
        my_chip = 2 * x + y
        sibling = (x, y, 1 - c)
        chips = _other_chips(x, y)

        def rows(i, half):
            hr = shards[i].shape[0] // 2
            return pl.ds(half * hr, hr) if split[i] else pl.ds(0, shards[i].shape[0])

        sends = []
        for i in range(n):
            for j, chip in enumerate(chips):
                cp = _remote(ins[i].at[rows(i, c)], outs[i].at[my_chip, rows(i, c)],
                             ici_send.at[3 * i + j], ici_recv.at[3 * i + j], (*chip, c))
                cp.start()
                sends.append(cp)
            cp = _remote(ins[i], outs[i].at[my_chip], own_send.at[i], own_recv.at[i], sibling)
            cp.start()
            sends.append(cp)
        for i in range(n):
            for j, chip in enumerate(chips):
                s = 3 * i + j
                land = outs[i].at[2 * chip[0] + chip[1], rows(i, c)]
                _remote(land, land, ici_send.at[s], ici_recv.at[s], (*chip, c)).wait_recv()
                if split[i]:
                    cp = _remote(land, land, d2d_send.at[s], d2d_recv.at[s], sibling)
                    cp.start()
                    sends.append(cp)
        for i in range(n):
            _remote(ins[i], outs[i].at[my_chip], own_send.at[i], own_recv.at[i], sibling).wait_recv()
            if split[i]:
                for j, chip in enumerate(chips):
                    s = 3 * i + j
                    land = outs[i].at[2 * chip[0] + chip[1], rows(i, 1 - c)]
                    _remote(land, land, d2d_send.at[s], d2d_recv.at[s], sibling).wait_recv()
        for cp in sends:
            cp.wait_send()

    return pl.pallas_call(
        body, name="gather_weights", out_shape=[jax.ShapeDtypeStruct((N_CHIPS,) + s.shape, s.dtype) for s in shards],
        in_specs=[HBM] * n, out_specs=[HBM] * n,
        scratch_shapes=_sems(n) + _sems(3 * n) + _sems(3 * n))(*shards)


def _swap_halves(grads):
    n = len(grads)

    def body(*refs):
        ins, outs, send, recv = refs[:n], refs[n:2 * n], refs[2 * n], refs[2 * n + 1]
        x, y, c = _me()
        copies = []
        for i in range(n):
            hr = grads[i].shape[1] // 2
            cp = _remote(ins[i].at[:, pl.ds((1 - c) * hr, hr), :], outs[i], send.at[i], recv.at[i], (x, y, 1 - c))
            cp.start()
            copies.append(cp)
        for cp in copies:
            cp.wait()

    return pl.pallas_call(
        body, name="swap_halves",
        out_shape=[jax.ShapeDtypeStruct((g.shape[0], g.shape[1] // 2, g.shape[2]), g.dtype) for g in grads],
        in_specs=[HBM] * n, out_specs=[HBM] * n, scratch_shapes=_sems(n))(*grads)


def _add_halves(name, grads, other, c_idx, th=256):
    K, R, C = grads.shape
    H = R // 2
    th = _row_tile(H, th, 16)
    nb = H // th

    def body(c_ref, g_ref, o_ref, out_ref):
        out_ref[...] = (g_ref[...].astype(f32) + o_ref[...].astype(f32)).astype(bf16)

    grid_spec = pltpu.PrefetchScalarGridSpec(
        num_scalar_prefetch=1, grid=(nb,),
        in_specs=[pl.BlockSpec((K, th, C), lambda i, c: (0, c[0] * nb + i, 0)),
                  pl.BlockSpec((K, th, C), lambda i, c: (0, i, 0))],
        out_specs=pl.BlockSpec((K, th, C), lambda i, c: (0, i, 0)))
    return pl.pallas_call(body, name=name, grid_spec=grid_spec,
                          out_shape=jax.ShapeDtypeStruct((K, H, C), bf16),
                          compiler_params=_params(("parallel",)))(c_idx, grads, other)


def _exchange_partials(parts, smalls):
    n, ns = len(parts), len(smalls)

    def body(*refs):
        p_in, s_in = refs[:n], refs[n:n + ns]
        p_out, s_out = refs[n + ns:2 * n + ns], refs[2 * n + ns:2 * n + 2 * ns]
        psend, precv, ssend, srecv, local_sems = refs[2 * n + 2 * ns:]
        x, y, c = _me()
        my_dev = 4 * x + 2 * y + c
        local = [pltpu.make_async_copy(s_in[i], s_out[i].at[my_dev], local_sems.at[i]) for i in range(ns)]
        for cp in local:
            cp.start()
        copies = []
        for i in range(n):
            for j, chip in enumerate(_other_chips(x, y)):
                cp = _remote(p_in[i].at[2 * chip[0] + chip[1]], p_out[i].at[j], psend.at[3 * i + j], precv.at[3 * i + j], (*chip, c))
                cp.start()
                copies.append(cp)
        for i in range(ns):
            k = 0
            for dx in range(2):
                for dy in range(2):
                    for dc in range(2):
                        if dx or dy or dc:
                            cp = _remote(s_in[i], s_out[i].at[my_dev], ssend.at[7 * i + k], srecv.at[7 * i + k], (x ^ dx, y ^ dy, c ^ dc))
                            cp.start()
                            copies.append(cp)
                            k += 1
        for cp in copies:
            cp.wait()
        for cp in local:
            cp.wait()

    return pl.pallas_call(
        body, name="exchange_partials",
        out_shape=([jax.ShapeDtypeStruct((3,) + p_.shape[1:], p_.dtype) for p_ in parts]
                   + [jax.ShapeDtypeStruct((N_DEV,) + s.shape, s.dtype) for s in smalls]),
        in_specs=[HBM] * (n + ns), out_specs=[HBM] * (n + ns),
        scratch_shapes=_sems(3 * n) + _sems(7 * ns) + [pltpu.SemaphoreType.DMA((ns,))])(*parts, *smalls)


def _sum_partials(name, part, recv, chip_idx, th=256):
    K, H, C = part.shape
    th = _row_tile(H, th, 16)

    def body(chip_ref, p_ref, r_ref, o_ref):
        acc = p_ref[...].astype(f32)
        for j in range(3):
            acc = acc + r_ref[j].astype(f32)
        o_ref[...] = acc

    grid_spec = pltpu.PrefetchScalarGridSpec(
        num_scalar_prefetch=1, grid=(H // th,),
        in_specs=[pl.BlockSpec((None, th, C), lambda i, chip: (chip[0], i, 0)),
                  pl.BlockSpec((3, th, C), lambda i, chip: (0, i, 0))],
        out_specs=pl.BlockSpec((th, C), lambda i, chip: (i, 0)))
    return pl.pallas_call(body, name=name, grid_spec=grid_spec, out_shape=jax.ShapeDtypeStruct((H, C), f32),
                          compiler_params=_params(("parallel",)))(chip_idx, part, recv)


def _sum_slots(name, slots, tr):
    K, R, C = slots.shape
    tr = _row_tile(R, tr, 16)

    def body(s_ref, o_ref):
        acc = s_ref[0].astype(f32)
        for k in range(1, K):
            acc = acc + s_ref[k].astype(f32)
        o_ref[...] = acc

    return pl.pallas_call(body, name=name, grid=(R // tr,),
                          in_specs=[pl.BlockSpec((K, tr, C), lambda i: (0, i, 0))],
                          out_specs=pl.BlockSpec((tr, C), lambda i: (i, 0)),
                          out_shape=jax.ShapeDtypeStruct((R, C), f32),
                          compiler_params=_params(("parallel",)))(slots)


def _share_halves(halves):
    n = len(halves)

    def body(*refs):
        ins, outs, send, recv = refs[:n], refs[n:2 * n], refs[2 * n], refs[2 * n + 1]
        x, y, c = _me()
        copies = []
        for i in range(n):
            cp = _remote(ins[i], outs[i], send.at[i], recv.at[i], (x, y, 1 - c))
            cp.start()
            copies.append(cp)
        for cp in copies:
            cp.wait()

    return pl.pallas_call(
        body, name="share_halves", out_shape=[jax.ShapeDtypeStruct(h.shape, h.dtype) for h in halves],
        in_specs=[HBM] * n, out_specs=[HBM] * n, scratch_shapes=_sems(n))(*halves)


def _adamw_big(name, w, g_mine, g_theirs, m, v, c_idx, tr=256):
    R, C = w.shape
    H = R // 2
    tr = _row_tile(H, tr)
    nb = H // tr

    def body(c_ref, w_ref, gm_ref, gt_ref, m_ref, v_ref, g_ref, d_ref, mo_ref, vo_ref):
        g_ = jnp.where(pl.program_id(0) // nb == c_ref[0], gm_ref[...], gt_ref[...])
        g_ref[...] = g_
        m_ = ADAM_B1 * m_ref[...] + (1.0 - ADAM_B1) * g_
        v_ = ADAM_B2 * v_ref[...] + (1.0 - ADAM_B2) * jnp.square(g_)
        m_hat = m_ / (1.0 - ADAM_B1 ** ADAM_STEP)
        v_hat = v_ / (1.0 - ADAM_B2 ** ADAM_STEP)
        d_ref[...] = -ADAM_LR * (m_hat / (jnp.sqrt(v_hat) + ADAM_EPS) + ADAM_WD * w_ref[...])
        mo_ref[...] = m_
        vo_ref[...] = v_

    full = pl.BlockSpec((tr, C), lambda i, c: (i, 0))
    half = pl.BlockSpec((tr, C), lambda i, c: (i % nb, 0))
    grid_spec = pltpu.PrefetchScalarGridSpec(num_scalar_prefetch=1, grid=(2 * nb,),
                                             in_specs=[full, half, half, full, full], out_specs=[full] * 4)
    return pl.pallas_call(body, name=name, grid_spec=grid_spec, out_shape=[jax.ShapeDtypeStruct((R, C), f32)] * 4,
                          compiler_params=_params(("parallel",)))(c_idx, w, g_mine, g_theirs, m, v)


BIG = ("ffn1_w_gate", "ffn1_w_up", "ffn1_w_down", "w_in", "w_out", "ffn2_w_gate", "ffn2_w_up", "ffn2_w_down",
       "ple_w_gate", "ple_w_proj")


SMALL = ("ffn1_norm", "mix_norm", "gm_ln_g", "gm_ln_b", "gm_w_s", "gm_b_s", "gm_out_norm", "conv_b", "dt_bias", "a_log",
         "d_skip", "ssm_norm", "ffn2_norm", "ple_norm", "ple_b_gate", "final_norm")
SMALL_C = 1024


def _pack_small(vals):
    parts = []
    for v in vals:
        f = v.astype(f32).reshape(-1)
        parts.append(jnp.pad(f, (0, -f.shape[0] % SMALL_C)))
    flat = jnp.concatenate(parts)
    rows = flat.shape[0] // SMALL_C
    return jnp.pad(flat, (0, (-rows % 8) * SMALL_C)).reshape(-1, SMALL_C)


def _unpack_small(pack, shapes):
    flat = pack.reshape(-1)
    out, off = [], 0
    for s in shapes:
        n = 1
        for d in s:
            n *= d
        out.append(flat[off:off + n].reshape(s))
        off += n + (-n % SMALL_C)
    return out


def _pad_lanes(v):
    return jnp.pad(v, ((0, 0), (0, LANES - v.shape[1])))


def _local_step(x, p, tgt, W, conv_w, S):
    G = GM_WIDTH
    K = N_CHIPS
    w_in = jnp.concatenate([W["w_in"][k] for k in range(K)], axis=1)
    w_uv = w_in[:, :2 * G]
    w_zxd = jnp.pad(w_in[:, 2 * G:], ((0, 0), (0, ZXD - (IN_PROJ - 2 * G))))
    wo = W["w_out"].reshape(-1, D_MODEL)
    wpg = W["ple_w_gate"].reshape(-1, D_MODEL)
    b_st = S["gm_b_s"][0].T
    w_s = S["gm_w_s"][0]
    dtb, alog, dsk = _pad_lanes(S["dt_bias"]), _pad_lanes(S["a_log"]), _pad_lanes(S["d_skip"])
    gfin = S["final_norm"].reshape(1, -1)

    h1, n1, a1, b1 = _ffn_fwd("ffn1_fwd", x, S["ffn1_norm"], W["ffn1_w_gate"], W["ffn1_w_up"], W["ffn1_w_down"])
    n2, uv, z, xbc, dtr, ya = _mix_fwd(h1, S["mix_norm"], w_uv, w_zxd, S["gm_ln_g"], S["gm_ln_b"], w_s, b_st, S["gm_out_norm"])
    yb, sprev = _ssd_fwd(xbc, z, dtr, conv_w, S["conv_b"], dtb, alog, dsk, S["ssm_norm"])
    h2, h3, n3, a2, b2 = _ffn_fwd("ffn2_fwd", h1, S["ffn2_norm"], W["ffn2_w_gate"], W["ffn2_w_up"], W["ffn2_w_down"],
                                  pre=(ya, yb, wo))
    dh3, loss, dgp, dwpg, dbpg, dwpp, dgf = _tail(h3, p, tgt, S["ple_norm"], wpg, S["ple_b_gate"], W["ple_w_proj"], gfin)
    dh2, da2, db2, hm2, dg_ffn2, dya, dyb = _ffn_bwd("ffn2_bwd", dh3, h2, S["ffn2_norm"], a2, b2, W["ffn2_w_gate"],
                                                     W["ffn2_w_up"], W["ffn2_w_down"], wo=wo, ga=G)
    gW = {}
    gW["ffn2_w_gate"] = _matmul_tn("dw_ffn2_gate", n3, da2)
    gW["ffn2_w_up"] = _matmul_tn("dw_ffn2_up", n3, db2)
    gW["ffn2_w_down"] = _matmul_tn("dw_ffn2_down", hm2, dh3, scale=0.5)
    gW["w_out"] = jnp.concatenate([_matmul_tn("dw_out_a", ya, dh2), _matmul_tn("dw_out_b", yb, dh2)], axis=0).reshape(W["w_out"].shape)
    duv, dlng, dlnb, dws, dbst, dgout = _gm_bwd(uv, dya, S["gm_ln_g"], S["gm_ln_b"], w_s, b_st, S["gm_out_norm"])
    dzxd, dcw, dcb, ddtb, dalog, ddsk, dgssm = _ssd_bwd(xbc, z, dtr, sprev, dyb, conv_w, S["conv_b"], dtb, alog, dsk, S["ssm_norm"])
    dh1, dg_mix = _mix_bwd(dh2, h1, S["mix_norm"], duv, dzxd, w_uv, w_zxd)
    dw_in = jnp.concatenate([_matmul_tn("dw_in_uv", n2, duv), _matmul_tn("dw_in_zxd", n2, dzxd)[:, :IN_PROJ - 2 * G]], axis=1)
    gW["w_in"] = jnp.transpose(dw_in.reshape(D_MODEL, K, IN_PROJ // K), (1, 0, 2))
    dx, da1, db1, hm1, dg_ffn1 = _ffn_bwd("ffn1_bwd", dh1, x, S["ffn1_norm"], a1, b1, W["ffn1_w_gate"], W["ffn1_w_up"], W["ffn1_w_down"])
    gW["ffn1_w_gate"] = _matmul_tn("dw_ffn1_gate", n1, da1)
    gW["ffn1_w_up"] = _matmul_tn("dw_ffn1_up", n1, db1)
    gW["ffn1_w_down"] = _matmul_tn("dw_ffn1_down", hm1, dh1, scale=0.5)
    gW["ple_w_gate"] = dwpg.astype(bf16).reshape(W["ple_w_gate"].shape)
    gW["ple_w_proj"] = dwpp.astype(bf16)
    nh = SSM_HEADS
    gS = {"ffn1_norm": dg_ffn1, "mix_norm": dg_mix, "gm_ln_g": dlng, "gm_ln_b": dlnb, "gm_w_s": dws[None], "gm_b_s": dbst.T[None],
          "gm_out_norm": dgout, "conv_b": dcb, "dt_bias": ddtb[:, :nh], "a_log": dalog[:, :nh], "d_skip": ddsk[:, :nh],
          "ssm_norm": dgssm, "ffn2_norm": dg_ffn2, "ple_norm": dgp, "ple_b_gate": dbpg, "final_norm": dgf.reshape(-1)}
    return loss, dx, gW, dcw, gS


_WEIGHTS = ("ffn1_norm", "ffn1_w_gate", "ffn1_w_up", "ffn1_w_down", "mix_norm", "w_in", "gm_ln_g", "gm_ln_b", "gm_w_s", "gm_b_s",
            "gm_out_norm", "conv_w", "conv_b", "dt_bias", "a_log", "d_skip", "ssm_norm", "w_out", "ffn2_norm", "ffn2_w_gate",
            "ffn2_w_up", "ffn2_w_down", "ple_norm", "ple_w_gate", "ple_b_gate", "ple_w_proj", "final_norm")
_BIG_NAMES = BIG


def kernel(x, p, ffn1_norm, ffn1_w_gate, ffn1_w_up, ffn1_w_down, mix_norm, w_in, gm_ln_g, gm_ln_b, gm_w_s, gm_b_s, gm_out_norm, conv_w, conv_b, dt_bias, a_log, d_skip, ssm_norm, w_out, ffn2_norm, ffn2_w_gate, ffn2_w_up, ffn2_w_down, ple_norm, ple_w_gate, ple_b_gate, ple_w_proj, final_norm, loss_target, m_ffn1_norm, m_ffn1_w_gate, m_ffn1_w_up, m_ffn1_w_down, m_mix_norm, m_w_in, m_gm_ln_g, m_gm_ln_b, m_gm_w_s, m_gm_b_s, m_gm_out_norm, m_conv_w, m_conv_b, m_dt_bias, m_a_log, m_d_skip, m_ssm_norm, m_w_out, m_ffn2_norm, m_ffn2_w_gate, m_ffn2_w_up, m_ffn2_w_down, m_ple_norm, m_ple_w_gate, m_ple_b_gate, m_ple_w_proj, m_final_norm, v_ffn1_norm, v_ffn1_w_gate, v_ffn1_w_up, v_ffn1_w_down, v_mix_norm, v_w_in, v_gm_ln_g, v_gm_ln_b, v_gm_w_s, v_gm_b_s, v_gm_out_norm, v_conv_w, v_conv_b, v_dt_bias, v_a_log, v_d_skip, v_ssm_norm, v_w_out, v_ffn2_norm, v_ffn2_w_gate, v_ffn2_w_up, v_ffn2_w_down, v_ple_norm, v_ple_w_gate, v_ple_b_gate, v_ple_w_proj, v_final_norm):
    given = dict(locals())
    w = {n: given[n] for n in _WEIGHTS}
    m = {n: given["m_" + n] for n in _WEIGHTS}
    v = {n: given["v_" + n] for n in _WEIGHTS}

    cw_shard = w["conv_w"][0]
    nbig = len(BIG)
    gathered = _gather_weights([w[n][0].astype(bf16) for n in BIG] + [cw_shard], [True] * nbig + [False])
    W = dict(zip(BIG, gathered[:nbig]))
    cw_full = jnp.transpose(gathered[nbig], (1, 0, 2)).reshape(cw_shard.shape[0], -1)

    S = {n: w[n] for n in SMALL}
    loss, dx, gW, dcw, gS = _local_step(x[0], p[0, 0], loss_target[0], W, cw_full, S)

    c_idx = lax.axis_index("c").astype(jnp.int32).reshape(1)
    chip = 2 * lax.axis_index("x") + lax.axis_index("y")
    chip_idx = chip.astype(jnp.int32).reshape(1)
    grads = [gW[n] for n in BIG]
    others = _swap_halves(grads)
    parts = [_add_halves("add_" + n, g_, o_, c_idx) for n, g_, o_ in zip(BIG, grads, others)]
    small = _pack_small([gS[n] for n in SMALL] + [dcw, loss[:, :1]])
    *recv, smalls = _exchange_partials(parts, [small])
    mine = [_sum_partials("sum_" + n, p_, r_, chip_idx) for n, p_, r_ in zip(BIG, parts, recv)]
    theirs = _share_halves(mine)
    small_shapes = [w[n].shape for n in SMALL] + [dcw.shape, (1, 1)]
    small_sum = _unpack_small(_sum_slots("sum_small", smalls, 512), small_shapes)
    g = {n: small_sum[i] for i, n in enumerate(SMALL)}
    cshard = cw_shard.shape[1]
    g["conv_w"] = lax.dynamic_slice_in_dim(small_sum[len(SMALL)], chip * cshard, cshard, axis=1)[None]
    loss_total = small_sum[len(SMALL) + 1].reshape(())

    delta, new_m, new_v = {}, {}, {}
    sm_names = SMALL + ("conv_w",)
    sm_shapes = [w[n].shape for n in sm_names]
    d_s, m_s, v_s = _adamw("adamw_small", _pack_small([w[n] for n in sm_names]), _pack_small([g[n] for n in sm_names]),
                           _pack_small([m[n] for n in sm_names]), _pack_small([v[n] for n in sm_names]))
    for dst, src in ((delta, d_s), (new_m, m_s), (new_v, v_s)):
        for n, val in zip(sm_names, _unpack_small(src, sm_shapes)):
            dst[n] = val
    for n, gm_, gt_ in zip(BIG, mine, theirs):
        g_, d_, m_, v_ = _adamw_big("adamw_" + n, w[n][0], gm_, gt_, m[n][0], v[n][0], c_idx)
        g[n], delta[n], new_m[n], new_v[n] = g_[None], d_[None], m_[None], v_[None]

    return (loss_total, dx[None], *[g[n] for n in _WEIGHTS], *[delta[n] for n in _WEIGHTS],
            *[new_m[n] for n in _WEIGHTS], *[new_v[n] for n in _WEIGHTS])
```

```python
import functools

import jax
import jax.numpy as jnp
from jax import lax
from jax.experimental import pallas as pl
from jax.experimental.pallas import tpu as pltpu

f32 = jnp.float32
bf16 = jnp.bfloat16
MESH = pl.DeviceIdType.MESH
HIGHEST = lax.Precision.HIGHEST

EPS = 1e-6
N_CHIPS = 4
N_DEV = 8
D_MODEL = 1024
D_FF = 2816
D_PLE = 256
GM_WIDTH = 1024
GM_HEADS = 8
CHUNK = 128
SSM_WIDTH = 1024
SSM_HEADS = 16
SSM_HEAD_DIM = 64
SSM_GROUPS = 2
SSM_STATE = 128
SSM_CONV = 4
CONV_DIM = SSM_WIDTH + 2 * SSM_GROUPS * SSM_STATE
IN_PROJ = 2 * GM_WIDTH + SSM_WIDTH + CONV_DIM + SSM_HEADS
LANES = 128
ZXD = SSM_WIDTH + CONV_DIM + LANES

ADAM_LR = 0.001
ADAM_B1 = 0.9
ADAM_B2 = 0.999
ADAM_EPS = 1e-08
ADAM_WD = 0.01
ADAM_STEP = 10

VMEM_LIMIT = 56 * 1024 * 1024


def _dot(a, b):
    return jnp.dot(a, b, preferred_element_type=f32)


def _dot_nt(a, b):
    return lax.dot_general(a, b, (((1,), (1,)), ((), ())), preferred_element_type=f32)


def _dot_tn(a, b):
    return lax.dot_general(a, b, (((0,), (0,)), ((), ())), preferred_element_type=f32)


def _rms(x, g):
    return x * lax.rsqrt(jnp.mean(x * x, axis=-1, keepdims=True) + EPS) * g


def _gelu(x):
    return 0.5 * x * (1.0 + lax.erf(x * 0.7071067811865476))


def _layernorm(x, g, b):
    mu = jnp.mean(x, axis=-1, keepdims=True)
    xc = x - mu
    return xc * lax.rsqrt(jnp.mean(xc * xc, axis=-1, keepdims=True) + EPS) * g + b


def _sigmoid(x):
    return 1.0 / (1.0 + jnp.exp(-x))


def _softplus(x):
    return jnp.maximum(x, 0.0) + jnp.log(1.0 + jnp.exp(-jnp.abs(x)))


def _full(shape):
    nd = len(shape)
    return pl.BlockSpec(shape, lambda *_: (0,) * nd, pipeline_mode=pl.Buffered(1))


def _acc(shape):
    nd = len(shape)
    return pl.BlockSpec(shape, lambda *_: (0,) * nd)


def _rows(tm, ncols):
    return pl.BlockSpec((tm, ncols), lambda i: (i, 0))


def _params(sem):
    return pltpu.CompilerParams(dimension_semantics=sem, vmem_limit_bytes=VMEM_LIMIT)


def _row_tile(rows, target, mult=8):
    best = rows
    for t in range(mult, min(rows, target) + 1, mult):
        if rows % t == 0:
            best = t
    return best if best <= target else rows


def _stack_rows(k, tm, ncols):
    return pl.BlockSpec((k, tm, ncols), lambda i: (0, i, 0))


def _ffn_fwd(name, h, g, wg, wu, wd, pre=None, tm=256):
    T, D = h.shape
    K, _, Fs = wg.shape
    tm = min(tm, T)

    def body(*refs):
        if pre is None:
            h_ref, g_ref, wg_ref, wu_ref, wd_ref, ho_ref, n_ref, a_ref, b_ref = refs
            hin = h_ref[...]
        else:
            (h_ref, ya_ref, yb_ref, wo_ref, g_ref, wg_ref, wu_ref, wd_ref,
             hi_ref, ho_ref, n_ref, a_ref, b_ref) = refs
            ga = ya_ref.shape[1]
            hin = h_ref[...] + _dot(ya_ref[...], wo_ref[:ga, :]) + _dot(yb_ref[...], wo_ref[ga:, :])
            hi_ref[...] = hin
        n = _rms(hin, g_ref[...]).astype(bf16)
        n_ref[...] = n
        acc = jnp.zeros((tm, D), f32)
        for k in range(K):
            a = _dot(n, wg_ref[k]).astype(bf16)
            b = _dot(n, wu_ref[k]).astype(bf16)
            a_ref[k] = a
            b_ref[k] = b
            af = a.astype(f32)
            hm = (af * _sigmoid(af) * b.astype(f32)).astype(bf16)
            acc = acc + _dot(hm, wd_ref[k])
        ho_ref[...] = hin + 0.5 * acc

    ins = [h] + (list(pre) if pre is not None else []) + [g, wg, wu, wd]
    in_specs = [_rows(tm, D)]
    if pre is not None:
        in_specs += [_rows(tm, pre[0].shape[1]), _rows(tm, pre[1].shape[1]), _full(pre[2].shape)]
    in_specs += [_full(g.shape), _full(wg.shape), _full(wu.shape), _full(wd.shape)]
    outs = [jax.ShapeDtypeStruct((T, D), f32), jax.ShapeDtypeStruct((T, D), bf16),
            jax.ShapeDtypeStruct((K, T, Fs), bf16), jax.ShapeDtypeStruct((K, T, Fs), bf16)]
    out_specs = [_rows(tm, D), _rows(tm, D), _stack_rows(K, tm, Fs), _stack_rows(K, tm, Fs)]
    if pre is not None:
        outs = [jax.ShapeDtypeStruct((T, D), f32)] + outs
        out_specs = [_rows(tm, D)] + out_specs
    return pl.pallas_call(body, name=name, grid=(T // tm,), in_specs=in_specs, out_specs=out_specs,
                          out_shape=outs, compiler_params=_params(("parallel",)))(*ins)


def _ffn_bwd(name, dh, hin, g, a, b, wg, wu, wd, wo=None, ga=0, tm=256):
    T, D = dh.shape
    K, _, Fs = wg.shape
    tm = min(tm, T)

    def body(*refs):
        if wo is None:
            (dh_ref, hin_ref, g_ref, a_ref, b_ref, wg_ref, wu_ref, wd_ref,
             dhi_ref, da_ref, db_ref, hm_ref, dg_ref) = refs
        else:
            (dh_ref, hin_ref, g_ref, a_ref, b_ref, wg_ref, wu_ref, wd_ref, wo_ref,
             dhi_ref, da_ref, db_ref, hm_ref, dg_ref, dya_ref, dyb_ref) = refs

        @pl.when(pl.program_id(0) == 0)
        def _():
            dg_ref[...] = jnp.zeros_like(dg_ref)

        dh_ = dh_ref[...]
        dhb = (0.5 * dh_).astype(bf16)
        dn = jnp.zeros((tm, D), f32)
        for k in range(K):
            dhm = _dot_nt(dhb, wd_ref[k])
            af = a_ref[k].astype(f32)
            bf = b_ref[k].astype(f32)
            sg = _sigmoid(af)
            sl_ = af * sg
            da = (dhm * bf * (sg * (1.0 + af * (1.0 - sg)))).astype(bf16)
            db = (dhm * sl_).astype(bf16)
            da_ref[k] = da
            db_ref[k] = db
            hm_ref[k] = (sl_ * bf).astype(bf16)
            dn = dn + _dot_nt(da, wg_ref[k]) + _dot_nt(db, wu_ref[k])
        _, vjp = jax.vjp(_rms, hin_ref[...], g_ref[...])
        dx, dg = vjp(dn)
        dhi = dh_ + dx
        dhi_ref[...] = dhi
        dg_ref[...] += dg
        if wo is not None:
            dhib = dhi.astype(bf16)
            dya_ref[...] = _dot_nt(dhib, wo_ref[:ga, :]).astype(bf16)
            dyb_ref[...] = _dot_nt(dhib, wo_ref[ga:, :]).astype(bf16)

    ins = [dh, hin, g, a, b, wg, wu, wd]
    in_specs = [_rows(tm, D), _rows(tm, D), _full(g.shape), _stack_rows(K, tm, Fs), _stack_rows(K, tm, Fs),
                _full(wg.shape), _full(wu.shape), _full(wd.shape)]
    act = jax.ShapeDtypeStruct((K, T, Fs), bf16)
    outs = [jax.ShapeDtypeStruct((T, D), f32), act, act, act, jax.ShapeDtypeStruct(g.shape, f32)]
    out_specs = [_rows(tm, D), _stack_rows(K, tm, Fs), _stack_rows(K, tm, Fs), _stack_rows(K, tm, Fs), _acc(g.shape)]
    if wo is not None:
        gb = wo.shape[0] - ga
        ins += [wo]
        in_specs += [_full(wo.shape)]
        outs += [jax.ShapeDtypeStruct((T, ga), bf16), jax.ShapeDtypeStruct((T, gb), bf16)]
        out_specs += [_rows(tm, ga), _rows(tm, gb)]
    return pl.pallas_call(body, name=name, grid=(T // tm,), in_specs=in_specs, out_specs=out_specs,
                          out_shape=outs, compiler_params=_params(("arbitrary",)))(*ins)


def _matmul_tn(name, a, b, scale=1.0, tk=1024):
    ka = a.shape[0] if a.ndim == 3 else 0
    kb = b.shape[0] if b.ndim == 3 else 0
    K = max(ka, kb)
    T, M = a.shape[-2:]
    N = b.shape[-1]
    tk = min(tk, T)
    nk = T // tk
    if K:
        tn, nj = N, K
    else:
        tn = LANES * max(d for d in range(1, N // LANES + 1) if (N // LANES) % d == 0 and (d == 1 or M * d * LANES * 4 <= 6 * 1024 * 1024))
        nj = N // tn

    def body(a_ref, b_ref, o_ref, acc):
        k = pl.program_id(1)

        @pl.when(k == 0)
        def _():
            acc[...] = jnp.zeros_like(acc)

        bb = b_ref[...]
        if scale != 1.0:
            bb = bb * scale
        acc[...] += _dot_tn(a_ref[...].astype(bf16), bb.astype(bf16))

        @pl.when(k == nk - 1)
        def _():
            o_ref[...] = acc[...].astype(bf16)

    a_spec = pl.BlockSpec((None, tk, M), lambda j, k: (j, k, 0)) if ka else pl.BlockSpec((tk, M), lambda j, k: (k, 0))
    if kb:
        b_spec = pl.BlockSpec((None, tk, N), lambda j, k: (j, k, 0))
    elif K:
        b_spec = pl.BlockSpec((tk, N), lambda j, k: (k, 0))
    else:
        b_spec = pl.BlockSpec((tk, tn), lambda j, k: (k, j))
    if K:
        o_spec, o_shape = pl.BlockSpec((None, M, N), lambda j, k: (j, 0, 0)), (K, M, N)
    else:
        o_spec, o_shape = pl.BlockSpec((M, tn), lambda j, k: (0, j)), (M, N)
    return pl.pallas_call(
        body, name=name, grid=(nj, nk), in_specs=[a_spec, b_spec], out_specs=o_spec,
        out_shape=jax.ShapeDtypeStruct(o_shape, bf16), scratch_shapes=[pltpu.VMEM((M, tn), f32)],
        compiler_params=_params(("parallel", "arbitrary")))(a, b)


def _gm_pre(u, v, ln_g, ln_b):
    return _gelu(u), _layernorm(_gelu(v), ln_g, ln_b)


def _tril_mask():
    r = lax.broadcasted_iota(jnp.int32, (CHUNK, CHUNK), 0)
    c = lax.broadcasted_iota(jnp.int32, (CHUNK, CHUNK), 1)
    return c <= r


def _gm_mix(vnb, ws_ref, bst, mixed_sc, tm):
    mask = _tril_mask()
    for h in range(GM_HEADS):
        wt = jnp.where(mask, ws_ref[h], 0.0).astype(bf16)
        bias = bst[:, h:h + 1]
        for q in range(tm // CHUNK):
            rs = slice(q * CHUNK, (q + 1) * CHUNK)
            cs = slice(h * CHUNK, (h + 1) * CHUNK)
            mixed_sc[rs, cs] = _dot(wt, vnb[rs, cs]) + bias


def _mix_fwd(h1, gmix, w_uv, w_zxd, ln_g, ln_b, w_s, b_st, gout, tm=512):
    T, D = h1.shape
    tm = min(tm, T)
    G = GM_WIDTH

    def body(h_ref, g_ref, wuv_ref, wzxd_ref, lng_ref, lnb_ref, ws_ref, bst_ref, gout_ref,
             n_ref, uv_ref, z_ref, xbc_ref, dt_ref, ya_ref, mixed_sc):
        n = _rms(h_ref[...], g_ref[...]).astype(bf16)
        n_ref[...] = n
        u = _dot(n, wuv_ref[:, :G]).astype(bf16)
        v = _dot(n, wuv_ref[:, G:]).astype(bf16)
        uv_ref[:, :G] = u
        uv_ref[:, G:] = v
        z_ref[...] = _dot(n, wzxd_ref[:, :SSM_WIDTH]).astype(bf16)
        xbc_ref[...] = _dot(n, wzxd_ref[:, SSM_WIDTH:SSM_WIDTH + CONV_DIM]).astype(bf16)
        dt_ref[...] = _dot(n, wzxd_ref[:, SSM_WIDTH + CONV_DIM:])
        ug, vn = _gm_pre(u.astype(f32), v.astype(f32), lng_ref[...], lnb_ref[...])
        _gm_mix(vn.astype(bf16), ws_ref, bst_ref[...], mixed_sc, tm)
        ya_ref[...] = _rms(ug * mixed_sc[...], gout_ref[...]).astype(bf16)

    ins = [h1, gmix, w_uv, w_zxd, ln_g, ln_b, w_s, b_st, gout]
    in_specs = [_rows(tm, D)] + [_full(x.shape) for x in ins[1:]]
    outs = [jax.ShapeDtypeStruct((T, D), bf16), jax.ShapeDtypeStruct((T, 2 * G), bf16),
            jax.ShapeDtypeStruct((T, SSM_WIDTH), bf16), jax.ShapeDtypeStruct((T, CONV_DIM), bf16),
            jax.ShapeDtypeStruct((T, LANES), f32), jax.ShapeDtypeStruct((T, G), bf16)]
    out_specs = [_rows(tm, D), _rows(tm, 2 * G), _rows(tm, SSM_WIDTH), _rows(tm, CONV_DIM), _rows(tm, LANES), _rows(tm, G)]
    return pl.pallas_call(body, name="mix_fwd", grid=(T // tm,), in_specs=in_specs, out_specs=out_specs,
                          out_shape=outs, scratch_shapes=[pltpu.VMEM((tm, G), f32)],
                          compiler_params=_params(("parallel",)))(*ins)


def _gm_bwd(uv, dya, ln_g, ln_b, w_s, b_st, gout, tm=256):
    T = uv.shape[0]
    tm = min(tm, T)
    G = GM_WIDTH

    def body(uv_ref, dya_ref, lng_ref, lnb_ref, ws_ref, bst_ref, gout_ref,
             duv_ref, dlng_ref, dlnb_ref, dws_ref, dbst_ref, dgout_ref, mixed_sc, dvn_sc):
        @pl.when(pl.program_id(0) == 0)
        def _():
            for r in (dlng_ref, dlnb_ref, dws_ref, dbst_ref, dgout_ref):
                r[...] = jnp.zeros_like(r)

        u = uv_ref[:, :G].astype(f32)
        v = uv_ref[:, G:].astype(f32)
        (ug, vn), pre_vjp = jax.vjp(_gm_pre, u, v, lng_ref[...], lnb_ref[...])
        vnb = vn.astype(bf16)
        _gm_mix(vnb, ws_ref, bst_ref[...], mixed_sc, tm)
        mixed = mixed_sc[...]
        _, out_vjp = jax.vjp(_rms, ug * mixed, gout_ref[...])
        dpre, dgout = out_vjp(dya_ref[...].astype(f32))
        dgout_ref[...] += dgout
        dug = dpre * mixed
        dmixed = dpre * ug
        mask = _tril_mask()
        lane = lax.broadcasted_iota(jnp.int32, (1, GM_HEADS), 1)
        dbst = jnp.zeros((CHUNK, GM_HEADS), f32)
        for h in range(GM_HEADS):
            wt = jnp.where(mask, ws_ref[h], 0.0).astype(bf16)
            cs = slice(h * CHUNK, (h + 1) * CHUNK)
            dw = jnp.zeros((CHUNK, CHUNK), f32)
            for q in range(tm // CHUNK):
                rs = slice(q * CHUNK, (q + 1) * CHUNK)
                dm = dmixed[rs, cs]
                dmb = dm.astype(bf16)
                dw = dw + _dot_nt(dmb, vnb[rs, cs])
                dbst = dbst + jnp.sum(dm, axis=1, keepdims=True) * (lane == h).astype(f32)
                dvn_sc[rs, cs] = _dot_tn(wt, dmb)
            dws_ref[h] += jnp.where(mask, dw, 0.0)
        dbst_ref[...] += dbst
        du, dv, dlng, dlnb = pre_vjp((dug, dvn_sc[...]))
        duv_ref[:, :G] = du.astype(bf16)
        duv_ref[:, G:] = dv.astype(bf16)
        dlng_ref[...] += dlng
        dlnb_ref[...] += dlnb

    ins = [uv, dya, ln_g, ln_b, w_s, b_st, gout]
    in_specs = [_rows(tm, 2 * G), _rows(tm, G)] + [_full(x.shape) for x in ins[2:]]
    outs = [jax.ShapeDtypeStruct((T, 2 * G), bf16)] + [jax.ShapeDtypeStruct(x.shape, f32) for x in (ln_g, ln_b, w_s, b_st, gout)]
    out_specs = [_rows(tm, 2 * G)] + [_acc(x.shape) for x in (ln_g, ln_b, w_s, b_st, gout)]
    return pl.pallas_call(body, name="gm_bwd", grid=(T // tm,), in_specs=in_specs, out_specs=out_specs,
                          out_shape=outs, scratch_shapes=[pltpu.VMEM((tm, G), f32), pltpu.VMEM((tm, G), f32)],
                          compiler_params=_params(("arbitrary",)))(*ins)


def _mix_bwd(dh, h1, gmix, duv, dzxd, w_uv, w_zxd, tm=512):
    T, D = dh.shape
    tm = min(tm, T)

    def body(dh_ref, h_ref, g_ref, duv_ref, dzxd_ref, wuv_ref, wzxd_ref, dhi_ref, dg_ref):
        @pl.when(pl.program_id(0) == 0)
        def _():
            dg_ref[...] = jnp.zeros_like(dg_ref)

        dn = _dot_nt(duv_ref[...], wuv_ref[...]) + _dot_nt(dzxd_ref[...], wzxd_ref[...])
        _, vjp = jax.vjp(_rms, h_ref[...], g_ref[...])
        dx, dg = vjp(dn)
        dhi_ref[...] = dh_ref[...] + dx
        dg_ref[...] += dg

    ins = [dh, h1, gmix, duv, dzxd, w_uv, w_zxd]
    in_specs = [_rows(tm, D), _rows(tm, D), _full(gmix.shape), _rows(tm, duv.shape[1]), _rows(tm, dzxd.shape[1]),
                _full(w_uv.shape), _full(w_zxd.shape)]
    return pl.pallas_call(body, name="mix_bwd", grid=(T // tm,), in_specs=in_specs,
                          out_specs=[_rows(tm, D), _acc(gmix.shape)],
                          out_shape=[jax.ShapeDtypeStruct((T, D), f32), jax.ShapeDtypeStruct(gmix.shape, f32)],
                          compiler_params=_params(("arbitrary",)))(*ins)


HALO = 16
PAIRS = SSM_HEADS // 2
PAIR_W = 2 * SSM_HEAD_DIM


def _split(x, n):
    parts = []
    for _ in range(n):
        p = x.astype(bf16)
        parts.append(p)
        x = x - p.astype(f32)
    return parts


def _dot_sel(x, sel, n):
    out = None
    for p in _split(x, n):
        t = _dot(p, sel)
        out = t if out is None else out + t
    return out


def _sel_dot(sel, x, n):
    out = None
    for p in _split(x, n):
        t = _dot(sel, p)
        out = t if out is None else out + t
    return out


def _expand_mat():
    r = lax.broadcasted_iota(jnp.int32, (LANES, SSM_WIDTH), 0)
    c = lax.broadcasted_iota(jnp.int32, (LANES, SSM_WIDTH), 1)
    return (lax.shift_right_logical(c, 6) == r).astype(bf16)


def _reduce_mat():
    r = lax.broadcasted_iota(jnp.int32, (SSM_WIDTH, LANES), 0)
    c = lax.broadcasted_iota(jnp.int32, (SSM_WIDTH, LANES), 1)
    return (lax.shift_right_logical(r, 6) == c).astype(bf16)


def _shift_mat(rows, cols, off):
    r = lax.broadcasted_iota(jnp.int32, (rows, cols), 0)
    c = lax.broadcasted_iota(jnp.int32, (rows, cols), 1)
    return (c == r + off).astype(bf16)


def _ssd_front(c, xbc_ref, halo_ref, dtr_ref, cw_ref, cb_ref, dtb_ref, alog_ref):
    halo = halo_ref[...]
    ext = jnp.concatenate([jnp.where(c > 0, halo, jnp.zeros_like(halo)), xbc_ref[...]], axis=0)
    taps = [_dot(_shift_mat(CHUNK, HALO + CHUNK, HALO - SSM_CONV + 1 + j), ext) for j in range(SSM_CONV - 1)]
    taps.append(xbc_ref[...].astype(f32))
    xc = cb_ref[...] + cw_ref[0:1, :] * taps[0]
    for j in range(1, SSM_CONV):
        xc = xc + cw_ref[j:j + 1, :] * taps[j]
    sg = _sigmoid(xc)
    xa = xc * sg
    dt = _softplus(dtr_ref[...] + dtb_ref[...])
    a = -jnp.exp(alog_ref[...])
    acs = jnp.dot(_tril_mask().astype(f32), dt * a, preferred_element_type=f32, precision=HIGHEST)
    return taps, xc, sg, xa, dt, a, acs


def _ssd_wide(xa, dt, acs, dsk):
    ex = _expand_mat()
    dt_x = _dot_sel(dt, ex, 3)
    acs_x = _dot_sel(acs, ex, 3)
    dsk_x = _dot_sel(jnp.broadcast_to(dsk, (8, LANES)), ex, 3)[0:1]
    e_x = jnp.exp(acs_x)
    r_x = jnp.exp(acs_x[CHUNK - 1:CHUNK, :] - acs_x)
    xs = xa[:, :SSM_WIDTH]
    xd = xs * dt_x
    return dt_x, dsk_x, e_x, r_x, xs, xd, xd * r_x


def _pair_stack(v, lo):
    return jnp.concatenate([jnp.where(lo, v, 0.0), jnp.where(lo, 0.0, v)], axis=0)


def _ssd_pair(j, acs, acs_t, cb):
    out = []
    tril = _tril_mask()
    for h in (2 * j, 2 * j + 1):
        dk = jnp.exp(jnp.where(tril, acs[:, h:h + 1] - acs_t[h:h + 1, :], -jnp.inf))
        out.append((dk, cb * dk))
    return out


def _pair_col(row_lo, tot, j):
    return jnp.exp(jnp.where(row_lo, tot[:, 2 * j:2 * j + 1], tot[:, 2 * j + 1:2 * j + 2]))


def _gated_norm(y, z, g):
    yg = y * (z * _sigmoid(z))
    half = SSM_WIDTH // SSM_GROUPS
    parts = []
    for k in range(SSM_GROUPS):
        s = yg[:, k * half:(k + 1) * half]
        parts.append(s * lax.rsqrt(jnp.mean(s * s, axis=-1, keepdims=True) + EPS))
    return jnp.concatenate(parts, axis=1) * g


def _group_mats(xa):
    out = []
    for g in range(SSM_GROUPS):
        bm = xa[:, SSM_WIDTH + g * SSM_STATE:SSM_WIDTH + (g + 1) * SSM_STATE].astype(bf16)
        cm = xa[:, SSM_WIDTH + (SSM_GROUPS + g) * SSM_STATE:SSM_WIDTH + (SSM_GROUPS + g + 1) * SSM_STATE].astype(bf16)
        out.append((cm, bm, _dot_nt(cm, bm)))
    return out


def _ssd_in_specs(nc, rev):
    def ci(i):
        return nc - 1 - i if rev else i
    hp = CHUNK // HALO
    return [pl.BlockSpec((CHUNK, CONV_DIM), lambda i: (ci(i), 0)),
            pl.BlockSpec((HALO, CONV_DIM), lambda i: (jnp.maximum(ci(i) * hp - 1, 0), 0)),
            pl.BlockSpec((CHUNK, SSM_WIDTH), lambda i: (ci(i), 0)),
            pl.BlockSpec((CHUNK, LANES), lambda i: (ci(i), 0))]


def _ssd_fwd(xbc, z, dtr, conv_w, conv_b, dt_bias, a_log, d_skip, ssm_norm):
    T = xbc.shape[0]
    nc = T // CHUNK
    N = SSM_STATE

    def body(xbc_ref, halo_ref, z_ref, dtr_ref, cw_ref, cb_ref, dtb_ref, alog_ref, dsk_ref, g_ref,
             yb_ref, sprev_ref, s_sc):
        c = pl.program_id(0)

        @pl.when(c == 0)
        def _():
            s_sc[...] = jnp.zeros_like(s_sc)

        _, _, _, xa, dt, _, acs = _ssd_front(c, xbc_ref, halo_ref, dtr_ref, cw_ref, cb_ref, dtb_ref, alog_ref)
        _, dsk_x, e_x, _, xs, xd, gm = _ssd_wide(xa, dt, acs, dsk_ref[...])
        acs_t = acs.T
        tot = acs[CHUNK - 1:CHUNK, :]
        groups = _group_mats(xa)
        lo = lax.broadcasted_iota(jnp.int32, (CHUNK, PAIR_W), 1) < SSM_HEAD_DIM
        row_lo = lax.broadcasted_iota(jnp.int32, (PAIR_W, 1), 0) < SSM_HEAD_DIM
        ys = []
        for j in range(PAIRS):
            cmb, bmb, cb = groups[j // (PAIRS // SSM_GROUPS)]
            ps = slice(j * PAIR_W, (j + 1) * PAIR_W)
            (_, m0), (_, m1) = _ssd_pair(j, acs, acs_t, cb)
            sp = s_sc[j]
            yd = _dot(jnp.concatenate([m0, m1], axis=1).astype(bf16), _pair_stack(xd[:, ps], lo).astype(bf16))
            ys.append(yd + e_x[:, ps] * _dot_nt(cmb, sp.astype(bf16)))
            sprev_ref[0, j] = sp
            s_sc[j] = _pair_col(row_lo, tot, j) * sp + _dot_tn(gm[:, ps].astype(bf16), bmb)
        y = jnp.concatenate(ys, axis=1) + xs * dsk_x
        yb_ref[...] = _gated_norm(y, z_ref[...].astype(f32), g_ref[...]).astype(bf16)

    params = [conv_w, conv_b, dt_bias, a_log, d_skip, ssm_norm]
    return pl.pallas_call(
        body, name="ssd_fwd", grid=(nc,),
        in_specs=_ssd_in_specs(nc, False) + [_full(x.shape) for x in params],
        out_specs=[pl.BlockSpec((CHUNK, SSM_WIDTH), lambda i: (i, 0)), pl.BlockSpec((1, PAIRS, PAIR_W, N), lambda i: (i, 0, 0, 0))],
        out_shape=[jax.ShapeDtypeStruct((T, SSM_WIDTH), bf16), jax.ShapeDtypeStruct((nc, PAIRS, PAIR_W, N), f32)],
        scratch_shapes=[pltpu.VMEM((PAIRS, PAIR_W, N), f32)],
        compiler_params=_params(("arbitrary",)))(xbc, xbc, z, dtr, *params)


def _ssd_bwd(xbc, z, dtr, sprev, dyb, conv_w, conv_b, dt_bias, a_log, d_skip, ssm_norm):
    T = xbc.shape[0]
    nc = T // CHUNK
    H, N = SSM_HEADS, SSM_STATE
    PG = PAIRS // SSM_GROUPS

    def body(xbc_ref, halo_ref, z_ref, dtr_ref, sprev_ref, dyb_ref, cw_ref, cb_ref, dtb_ref, alog_ref, dsk_ref, g_ref,
             dzxd_ref, dcw_ref, dcb_ref, ddtb_ref, dalog_ref, ddsk_ref, dg_ref, ds_sc, next_sc):
        i = pl.program_id(0)
        c = nc - 1 - i

        @pl.when(i == 0)
        def _():
            ds_sc[...] = jnp.zeros_like(ds_sc)
            next_sc[...] = jnp.zeros_like(next_sc)
            for r_ in (dcw_ref, dcb_ref, ddtb_ref, dalog_ref, ddsk_ref, dg_ref):
                r_[...] = jnp.zeros_like(r_)

        taps, xc, sg, xa, dt, a, acs = _ssd_front(c, xbc_ref, halo_ref, dtr_ref, cw_ref, cb_ref, dtb_ref, alog_ref)
        dt_x, dsk_x, e_x, r_x, xs, xd, gm = _ssd_wide(xa, dt, acs, dsk_ref[...])
        acs_t = acs.T
        tot = acs[CHUNK - 1:CHUNK, :]
        groups = _group_mats(xa)
        lo = lax.broadcasted_iota(jnp.int32, (CHUNK, PAIR_W), 1) < SSM_HEAD_DIM
        row_lo = lax.broadcasted_iota(jnp.int32, (PAIR_W, 1), 0) < SSM_HEAD_DIM
        pairs, zs, yds = [], [], []
        for j in range(PAIRS):
            cmb, _, cb = groups[j // PG]
            ps = slice(j * PAIR_W, (j + 1) * PAIR_W)
            pairs.append(_ssd_pair(j, acs, acs_t, cb))
            (_, m0), (_, m1) = pairs[j]
            zs.append(_dot_nt(cmb, sprev_ref[0, j].astype(bf16)))
            yds.append(_dot(jnp.concatenate([m0, m1], axis=1).astype(bf16), _pair_stack(xd[:, ps], lo).astype(bf16)))
        zf = jnp.concatenate(zs, axis=1)
        y = jnp.concatenate(yds, axis=1) + e_x * zf + xs * dsk_x
        _, gn_vjp = jax.vjp(_gated_norm, y, z_ref[...].astype(f32), g_ref[...])
        dy, dz, dg = gn_vjp(dyb_ref[...].astype(f32))
        dg_ref[...] += dg
        dzxd_ref[:, :SSM_WIDTH] = dz.astype(bf16)

        lane = lax.broadcasted_iota(jnp.int32, (1, LANES), 1)
        sub = lax.broadcasted_iota(jnp.int32, (LANES, 1), 0)
        dacs = jnp.zeros((CHUNK, LANES), f32)
        dacs_r = jnp.zeros((LANES, CHUNK), f32)
        dtot = jnp.zeros((1, LANES), f32)
        dcb = [jnp.zeros((CHUNK, CHUNK), f32) for _ in range(SSM_GROUPS)]
        dcm = [jnp.zeros((CHUNK, N), f32) for _ in range(SSM_GROUPS)]
        dbm = [jnp.zeros((CHUNK, N), f32) for _ in range(SSM_GROUPS)]
        dxds, dgms = [], []
        for j in range(PAIRS):
            g = j // PG
            cmb, bmb, _ = groups[g]
            ps = slice(j * PAIR_W, (j + 1) * PAIR_W)
            (dk0, m0), (dk1, m1) = pairs[j]
            oh0, oh1 = (lane == 2 * j).astype(f32), (lane == 2 * j + 1).astype(f32)
            dyp = dy[:, ps]
            dy2 = _pair_stack(dyp, lo).astype(bf16)
            dm2 = _dot_nt(dy2, xd[:, ps].astype(bf16))
            m2 = jnp.concatenate([m0, m1], axis=0)
            dxds.append(_dot_tn(m2.astype(bf16), dy2))
            w2 = dm2 * m2
            rs = jnp.sum(w2, axis=1, keepdims=True)
            dacs = dacs + rs[:CHUNK] * oh0 + rs[CHUNK:] * oh1
            dacs_r = dacs_r - ((sub == 2 * j).astype(f32) * jnp.sum(w2[:CHUNK], axis=0, keepdims=True)
                               + (sub == 2 * j + 1).astype(f32) * jnp.sum(w2[CHUNK:], axis=0, keepdims=True))
            dcb[g] = dcb[g] + dm2[:CHUNK] * dk0 + dm2[CHUNK:] * dk1
            sp = sprev_ref[0, j]
            dzb = (dyp * e_x[:, ps]).astype(bf16)
            dcm[g] = dcm[g] + _dot(dzb, sp.astype(bf16))
            dsn = ds_sc[j]
            dsnb = dsn.astype(bf16)
            et = _pair_col(row_lo, tot, j)
            rr = jnp.sum(dsn * sp, axis=1, keepdims=True) * et
            dtot = dtot + jnp.sum(rr[:SSM_HEAD_DIM]) * oh0 + jnp.sum(rr[SSM_HEAD_DIM:]) * oh1
            dgms.append(_dot_nt(bmb, dsnb))
            dbm[g] = dbm[g] + _dot(gm[:, ps].astype(bf16), dsnb)
            ds_sc[j] = _dot_tn(dzb, cmb) + et * dsn
        dgm = jnp.concatenate(dgms, axis=1)
        dxd = jnp.concatenate(dxds, axis=1) + dgm * r_x
        dr = dgm * gm
        red = _dot_sel(jnp.concatenate([dy * e_x * zf - dr, dr, dxd * xs, dy * xs], axis=0), _reduce_mat(), 2)
        rowi = lax.broadcasted_iota(jnp.int32, (CHUNK, 1), 0)
        dtot = dtot + jnp.sum(red[CHUNK:2 * CHUNK], axis=0, keepdims=True)
        dacs = dacs + red[:CHUNK] + dacs_r.T + jnp.where(rowi == CHUNK - 1, dtot, 0.0)
        r2 = lax.broadcasted_iota(jnp.int32, (CHUNK, CHUNK), 0)
        c2 = lax.broadcasted_iota(jnp.int32, (CHUNK, CHUNK), 1)
        dadt = jnp.dot((c2 >= r2).astype(f32), dacs, preferred_element_type=f32, precision=HIGHEST)
        ddt = red[2 * CHUNK:3 * CHUNK] + dadt * a
        dalog_ref[...] += jnp.sum(dadt * dt, axis=0, keepdims=True) * a
        ddsk_ref[...] += jnp.sum(red[3 * CHUNK:], axis=0, keepdims=True)
        ddtr = jnp.where(lane < H, ddt * _sigmoid(dtr_ref[...] + dtb_ref[...]), 0.0)
        ddtb_ref[...] += jnp.sum(ddtr, axis=0, keepdims=True)
        dzxd_ref[:, SSM_WIDTH + CONV_DIM:] = ddtr.astype(bf16)
        dxa_bm, dxa_cm = [], []
        for g in range(SSM_GROUPS):
            cmb, bmb, _ = groups[g]
            dcbb = dcb[g].astype(bf16)
            dxa_bm.append(dbm[g] + _dot_tn(dcbb, cmb))
            dxa_cm.append(dcm[g] + _dot(dcbb, bmb))
        dxc = jnp.concatenate([dy * dsk_x + dxd * dt_x] + dxa_bm + dxa_cm, axis=1) * (sg * (1.0 + xc * (1.0 - sg)))
        ext = jnp.concatenate([dxc, next_sc[...]], axis=0)
        dxbc = cw_ref[SSM_CONV - 1:SSM_CONV, :] * dxc
        for s in range(1, SSM_CONV):
            dxbc = dxbc + cw_ref[SSM_CONV - 1 - s:SSM_CONV - s, :] * _sel_dot(_shift_mat(CHUNK, CHUNK + HALO, s), ext, 2)
        dzxd_ref[:, SSM_WIDTH:SSM_WIDTH + CONV_DIM] = dxbc.astype(bf16)
        dcw_ref[...] += jnp.concatenate([jnp.sum(dxc * t, axis=0, keepdims=True) for t in taps], axis=0)
        dcb_ref[...] += jnp.sum(dxc, axis=0, keepdims=True)
        next_sc[...] = dxc[0:HALO, :]

    params = [conv_w, conv_b, dt_bias, a_log, d_skip, ssm_norm]

    def rc(i):
        return nc - 1 - i

    in_specs = (_ssd_in_specs(nc, True)
                + [pl.BlockSpec((1, PAIRS, PAIR_W, N), lambda i: (rc(i), 0, 0, 0)), pl.BlockSpec((CHUNK, SSM_WIDTH), lambda i: (rc(i), 0))]
                + [_full(x.shape) for x in params])
    return pl.pallas_call(
        body, name="ssd_bwd", grid=(nc,), in_specs=in_specs,
        out_specs=[pl.BlockSpec((CHUNK, ZXD), lambda i: (rc(i), 0))] + [_acc(x.shape) for x in params],
        out_shape=[jax.ShapeDtypeStruct((T, ZXD), bf16)] + [jax.ShapeDtypeStruct(x.shape, f32) for x in params],
        scratch_shapes=[pltpu.VMEM((PAIRS, PAIR_W, N), f32), pltpu.VMEM((HALO, CONV_DIM), f32)],
        compiler_params=_params(("arbitrary",)))(xbc, xbc, z, dtr, sprev, dyb, *params)


def _tail(h3, p, tgt, gp, wpg, bpg, wpp, gf, tm=512):
    T, D = h3.shape
    tm = min(tm, T)

    def head(gpre, pp, h, gf_, t):
        gate = _sigmoid(gpre)
        y = _rms(h + gate * pp, gf_)
        err = y - t
        return 0.5 * jnp.sum(jnp.mean(err * err, axis=-1))

    def body(h_ref, p_ref, t_ref, gp_ref, wpg_ref, bpg_ref, wpp_ref, gf_ref,
             dh_ref, loss_ref, dgp_ref, dwpg_ref, dbpg_ref, dwpp_ref, dgf_ref):
        @pl.when(pl.program_id(0) == 0)
        def _():
            for r in (loss_ref, dgp_ref, dwpg_ref, dbpg_ref, dwpp_ref, dgf_ref):
                r[...] = jnp.zeros_like(r)

        h = h_ref[...]
        npf, np_vjp = jax.vjp(_rms, h, gp_ref[...])
        npb = npf.astype(bf16)
        pb = p_ref[...].astype(bf16)
        gpre = _dot(npb, wpg_ref[...]) + bpg_ref[...]
        kp, _, cp = wpp_ref.shape
        pp = jnp.concatenate([_dot(pb, wpp_ref[k]) for k in range(kp)], axis=1)
        loss, head_vjp = jax.vjp(head, gpre, pp, h, gf_ref[...], t_ref[...])
        dgpre, dpp, dh_a, dgf, _ = head_vjp(jnp.ones((), f32))
        loss_ref[...] += loss
        dgf_ref[...] += dgf
        dbpg_ref[...] += jnp.sum(dgpre, axis=0, keepdims=True)
        dgb = dgpre.astype(bf16)
        dwpg_ref[...] += _dot_tn(npb, dgb)
        dppb = dpp.astype(bf16)
        for k in range(kp):
            dwpp_ref[k] += _dot_tn(pb, dppb[:, k * cp:(k + 1) * cp])
        dh_b, dgp = np_vjp(_dot_nt(dgb, wpg_ref[...]))
        dgp_ref[...] += dgp
        dh_ref[...] = dh_a + dh_b

    ins = [h3, p, tgt, gp, wpg, bpg, wpp, gf]
    in_specs = [_rows(tm, D), _rows(tm, p.shape[1]), _rows(tm, D)] + [_full(x.shape) for x in ins[3:]]
    acc_shapes = [(1, LANES), gp.shape, wpg.shape, bpg.shape, wpp.shape, gf.shape]
    return pl.pallas_call(
        body, name="tail", grid=(T // tm,), in_specs=in_specs,
        out_specs=[_rows(tm, D)] + [_acc(s) for s in acc_shapes],
        out_shape=[jax.ShapeDtypeStruct((T, D), f32)] + [jax.ShapeDtypeStruct(s, f32) for s in acc_shapes],
        compiler_params=_params(("arbitrary",)))(*ins)


def _adamw(name, w, g, m, v, tr=256):
    R, C = w.shape
    tr = _row_tile(R, tr)

    def body(w_ref, g_ref, m_ref, v_ref, d_ref, mo_ref, vo_ref):
        g_ = g_ref[...]
        m_ = ADAM_B1 * m_ref[...] + (1.0 - ADAM_B1) * g_
        v_ = ADAM_B2 * v_ref[...] + (1.0 - ADAM_B2) * jnp.square(g_)
        m_hat = m_ / (1.0 - ADAM_B1 ** ADAM_STEP)
        v_hat = v_ / (1.0 - ADAM_B2 ** ADAM_STEP)
        d_ref[...] = -ADAM_LR * (m_hat / (jnp.sqrt(v_hat) + ADAM_EPS) + ADAM_WD * w_ref[...])
        mo_ref[...] = m_
        vo_ref[...] = v_

    spec = pl.BlockSpec((tr, C), lambda i: (i, 0))
    return pl.pallas_call(body, name=name, grid=(R // tr,), in_specs=[spec] * 4, out_specs=[spec] * 3,
                          out_shape=[jax.ShapeDtypeStruct((R, C), f32)] * 3,
                          compiler_params=_params(("parallel",)))(w, g, m, v)


HBM = pl.BlockSpec(memory_space=pltpu.HBM)


def _me():
    return lax.axis_index("x"), lax.axis_index("y"), lax.axis_index("c")


def _other_chips(x, y):
    return [(1 - x, y), (x, 1 - y), (1 - x, 1 - y)]


def _remote(src, dst, send_sem, recv_sem, dev):
    return pltpu.make_async_remote_copy(src_ref=src, dst_ref=dst, send_sem=send_sem, recv_sem=recv_sem,
                                        device_id=dev, device_id_type=MESH)


def _sems(n):
    return [pltpu.SemaphoreType.DMA((n,)), pltpu.SemaphoreType.DMA((n,))]


def _gather_weights(shards, split):
    n = len(shards)

    def body(*refs):
        ins, outs = refs[:n], refs[n:2 * n]
        own_send, own_recv, ici_send, ici_recv, d2d_send, d2d_recv = refs[2 * n:]
        x, y, c = _me()
        my_chip = 2 * x + y
        sibling = (x, y, 1 - c)
        chips = _other_chips(x, y)

        def rows(i, half):
            hr = shards[i].shape[0] // 2
            return pl.ds(half * hr, hr) if split[i] else pl.ds(0, shards[i].shape[0])

        sends = []
        for i in range(n):
            for j, chip in enumerate(chips):
                cp = _remote(ins[i].at[rows(i, c)], outs[i].at[my_chip, rows(i, c)],
                             ici_send.at[3 * i + j], ici_recv.at[3 * i + j], (*chip, c))
                cp.start()
                sends.append(cp)
            cp = _remote(ins[i], outs[i].at[my_chip], own_send.at[i], own_recv.at[i], sibling)
            cp.start()
            sends.append(cp)
        for i in range(n):
            for j, chip in enumerate(chips):
                s = 3 * i + j
                land = outs[i].at[2 * chip[0] + chip[1], rows(i, c)]
                _remote(land, land, ici_send.at[s], ici_recv.at[s], (*chip, c)).wait_recv()
                if split[i]:
                    cp = _remote(land, land, d2d_send.at[s], d2d_recv.at[s], sibling)
                    cp.start()
                    sends.append(cp)
        for i in range(n):
            _remote(ins[i], outs[i].at[my_chip], own_send.at[i], own_recv.at[i], sibling).wait_recv()
            if split[i]:
                for j, chip in enumerate(chips):
                    s = 3 * i + j
                    land = outs[i].at[2 * chip[0] + chip[1], rows(i, 1 - c)]
                    _remote(land, land, d2d_send.at[s], d2d_recv.at[s], sibling).wait_recv()
        for cp in sends:
            cp.wait_send()

    return pl.pallas_call(
        body, name="gather_weights", out_shape=[jax.ShapeDtypeStruct((N_CHIPS,) + s.shape, s.dtype) for s in shards],
        in_specs=[HBM] * n, out_specs=[HBM] * n,
        scratch_shapes=_sems(n) + _sems(3 * n) + _sems(3 * n))(*shards)


def _swap_halves(name, grads):
    n = len(grads)

    def body(*refs):
        ins, outs, send, recv = refs[:n], refs[n:2 * n], refs[2 * n], refs[2 * n + 1]
        x, y, c = _me()
        copies = []
        for i in range(n):
            hr = grads[i].shape[1] // 2
            cp = _remote(ins[i].at[:, pl.ds((1 - c) * hr, hr), :], outs[i], send.at[i], recv.at[i], (x, y, 1 - c))
            cp.start()
            copies.append(cp)
        for cp in copies:
            cp.wait()

    return pl.pallas_call(
        body, name=name,
        out_shape=[jax.ShapeDtypeStruct((g.shape[0], g.shape[1] // 2, g.shape[2]), g.dtype) for g in grads],
        in_specs=[HBM] * n, out_specs=[HBM] * n, scratch_shapes=_sems(n))(*grads)


def _add_halves(name, grads, other, c_idx, th=256):
    K, R, C = grads.shape
    H = R // 2
    th = _row_tile(H, th, 16)
    nb = H // th

    def body(c_ref, g_ref, o_ref, out_ref):
        out_ref[...] = (g_ref[...].astype(f32) + o_ref[...].astype(f32)).astype(bf16)

    grid_spec = pltpu.PrefetchScalarGridSpec(
        num_scalar_prefetch=1, grid=(nb,),
        in_specs=[pl.BlockSpec((K, th, C), lambda i, c: (0, c[0] * nb + i, 0)),
                  pl.BlockSpec((K, th, C), lambda i, c: (0, i, 0))],
        out_specs=pl.BlockSpec((K, th, C), lambda i, c: (0, i, 0)))
    return pl.pallas_call(body, name=name, grid_spec=grid_spec,
                          out_shape=jax.ShapeDtypeStruct((K, H, C), bf16),
                          compiler_params=_params(("parallel",)))(c_idx, grads, other)


SEM = pl.BlockSpec(memory_space=pltpu.SEMAPHORE)
ANY = pl.BlockSpec(memory_space=pl.ANY)
EFFECT = pltpu.SideEffectType.DATAFLOW_SIDE_EFFECTING


def _copies_start(name, srcs, land_shapes, n_copies, make_copies, after):
    ns, nl = len(srcs), len(land_shapes)
    lands = [lax.empty(s.shape, s.dtype) for s in land_shapes]

    def body(*refs):
        src_refs, land_refs = refs[:ns], refs[ns:ns + nl]
        send, recv, token = refs[ns + nl + 1], refs[ns + nl + 2], refs[-1]
        for cp in make_copies(src_refs, land_refs, send, recv):
            cp.start()
        token[...] = jnp.zeros_like(token)

    buffers = list(srcs) + lands
    out = pl.pallas_call(
        body, name=name,
        out_shape=(pltpu.SemaphoreType.DMA((n_copies,)), pltpu.SemaphoreType.DMA((n_copies,)),
                   *[pltpu.HBM(b.shape, b.dtype) for b in buffers], jax.ShapeDtypeStruct((8, LANES), f32)),
        in_specs=[HBM] * (ns + nl) + [ANY],
        out_specs=(SEM, SEM, *[HBM] * (ns + nl), pl.BlockSpec(memory_space=pltpu.VMEM)),
        input_output_aliases={i: 2 + i for i in range(ns + nl)},
        compiler_params=pltpu.CompilerParams(has_side_effects=EFFECT),
    )(*[pltpu.with_memory_space_constraint(b, pltpu.HBM) for b in buffers], after)
    return out[0], out[1], list(out[2:2 + ns]), list(out[2 + ns:2 + ns + nl]), out[-1]


def _copies_wait(name, started, make_copies, after):
    send, recv, srcs, lands, _ = started
    ns, nl = len(srcs), len(lands)

    def body(*refs):
        src_refs, land_refs = refs[:ns], refs[ns:ns + nl]
        for cp in make_copies(src_refs, land_refs, refs[ns + nl], refs[ns + nl + 1]):
            cp.wait_send()
            cp.wait_recv()

    buffers = list(srcs) + list(lands)
    out = pl.pallas_call(
        body, name=name, out_shape=tuple(pltpu.HBM(b.shape, b.dtype) for b in buffers),
        in_specs=[HBM] * (ns + nl) + [SEM, SEM, ANY], out_specs=tuple([HBM] * (ns + nl)),
        input_output_aliases={i: i for i in range(ns + nl)},
        compiler_params=pltpu.CompilerParams(has_side_effects=EFFECT),
    )(*buffers, send, recv, after)
    return list(out[:ns]), list(out[ns:])


def _gather_copies(src_refs, land_refs, send, recv):
    x, y, c = _me()
    my_chip = 2 * x + y
    peers = [(*chip, c) for chip in _other_chips(x, y)] + [(x, y, 1 - c)]
    return [_remote(src_refs[i], land_refs[i].at[my_chip], send.at[4 * i + j], recv.at[4 * i + j], peer)
            for i in range(len(src_refs)) for j, peer in enumerate(peers)]


def _partial_copies(src_refs, land_refs, send, recv):
    x, y, c = _me()
    return [_remote(src_refs[i].at[2 * chip[0] + chip[1]], land_refs[i].at[j], send.at[3 * i + j], recv.at[3 * i + j], (*chip, c))
            for i in range(len(src_refs)) for j, chip in enumerate(_other_chips(x, y))]


def _exchange_small(small):
    def body(s_in, s_out, send, recv, local_sem):
        x, y, c = _me()
        my_dev = 4 * x + 2 * y + c
        local = pltpu.make_async_copy(s_in, s_out.at[my_dev], local_sem)
        local.start()
        copies = []
        for k in range(1, N_DEV):
            cp = _remote(s_in, s_out.at[my_dev], send.at[k - 1], recv.at[k - 1], (x ^ (k >> 2), y ^ ((k >> 1) & 1), c ^ (k & 1)))
            cp.start()
            copies.append(cp)
        for cp in copies:
            cp.wait()
        local.wait()

    return pl.pallas_call(
        body, name="exchange_small", out_shape=jax.ShapeDtypeStruct((N_DEV,) + small.shape, small.dtype),
        in_specs=[HBM], out_specs=HBM, scratch_shapes=_sems(N_DEV - 1) + [pltpu.SemaphoreType.DMA])(small)


def _sum_partials(name, part, recv, chip_idx, th=256):
    K, H, C = part.shape
    th = _row_tile(H, th, 16)

    def body(chip_ref, p_ref, r_ref, o_ref):
        acc = p_ref[...].astype(f32)
        for j in range(3):
            acc = acc + r_ref[j].astype(f32)
        o_ref[...] = acc

    grid_spec = pltpu.PrefetchScalarGridSpec(
        num_scalar_prefetch=1, grid=(H // th,),
        in_specs=[pl.BlockSpec((None, th, C), lambda i, chip: (chip[0], i, 0)),
                  pl.BlockSpec((3, th, C), lambda i, chip: (0, i, 0))],
        out_specs=pl.BlockSpec((th, C), lambda i, chip: (i, 0)))
    return pl.pallas_call(body, name=name, grid_spec=grid_spec, out_shape=jax.ShapeDtypeStruct((H, C), f32),
                          compiler_params=_params(("parallel",)))(chip_idx, part, recv)


def _sum_slots(name, slots, tr):
    K, R, C = slots.shape
    tr = _row_tile(R, tr, 16)

    def body(s_ref, o_ref):
        acc = s_ref[0].astype(f32)
        for k in range(1, K):
            acc = acc + s_ref[k].astype(f32)
        o_ref[...] = acc

    return pl.pallas_call(body, name=name, grid=(R // tr,),
                          in_specs=[pl.BlockSpec((K, tr, C), lambda i: (0, i, 0))],
                          out_specs=pl.BlockSpec((tr, C), lambda i: (i, 0)),
                          out_shape=jax.ShapeDtypeStruct((R, C), f32),
                          compiler_params=_params(("parallel",)))(slots)


def _share_halves(name, halves):
    n = len(halves)

    def body(*refs):
        ins, outs, send, recv = refs[:n], refs[n:2 * n], refs[2 * n], refs[2 * n + 1]
        x, y, c = _me()
        copies = []
        for i in range(n):
            cp = _remote(ins[i], outs[i], send.at[i], recv.at[i], (x, y, 1 - c))
            cp.start()
            copies.append(cp)
        for cp in copies:
            cp.wait()

    return pl.pallas_call(
        body, name=name, out_shape=[jax.ShapeDtypeStruct(h.shape, h.dtype) for h in halves],
        in_specs=[HBM] * n, out_specs=[HBM] * n, scratch_shapes=_sems(n))(*halves)


def _adamw_big(name, w, g_mine, g_theirs, m, v, c_idx, tr=256):
    R, C = w.shape
    H = R // 2
    tr = _row_tile(H, tr)
    nb = H // tr

    def body(c_ref, w_ref, gm_ref, gt_ref, m_ref, v_ref, g_ref, d_ref, mo_ref, vo_ref):
        g_ = jnp.where(pl.program_id(0) // nb == c_ref[0], gm_ref[...], gt_ref[...])
        g_ref[...] = g_
        m_ = ADAM_B1 * m_ref[...] + (1.0 - ADAM_B1) * g_
        v_ = ADAM_B2 * v_ref[...] + (1.0 - ADAM_B2) * jnp.square(g_)
        m_hat = m_ / (1.0 - ADAM_B1 ** ADAM_STEP)
        v_hat = v_ / (1.0 - ADAM_B2 ** ADAM_STEP)
        d_ref[...] = -ADAM_LR * (m_hat / (jnp.sqrt(v_hat) + ADAM_EPS) + ADAM_WD * w_ref[...])
        mo_ref[...] = m_
        vo_ref[...] = v_

    full = pl.BlockSpec((tr, C), lambda i, c: (i, 0))
    half = pl.BlockSpec((tr, C), lambda i, c: (i % nb, 0))
    grid_spec = pltpu.PrefetchScalarGridSpec(num_scalar_prefetch=1, grid=(2 * nb,),
                                             in_specs=[full, half, half, full, full], out_specs=[full] * 4)
    return pl.pallas_call(body, name=name, grid_spec=grid_spec, out_shape=[jax.ShapeDtypeStruct((R, C), f32)] * 4,
                          compiler_params=_params(("parallel",)))(c_idx, w, g_mine, g_theirs, m, v)


BIG = ("ffn1_w_gate", "ffn1_w_up", "ffn1_w_down", "w_in", "w_out", "ffn2_w_gate", "ffn2_w_up", "ffn2_w_down",
       "ple_w_gate", "ple_w_proj")


SMALL = ("ffn1_norm", "mix_norm", "gm_ln_g", "gm_ln_b", "gm_w_s", "gm_b_s", "gm_out_norm", "conv_b", "dt_bias", "a_log",
         "d_skip", "ssm_norm", "ffn2_norm", "ple_norm", "ple_b_gate", "final_norm")
SMALL_C = 1024


def _pack_small(vals):
    parts = []
    for v in vals:
        f = v.astype(f32).reshape(-1)
        parts.append(jnp.pad(f, (0, -f.shape[0] % SMALL_C)))
    flat = jnp.concatenate(parts)
    rows = flat.shape[0] // SMALL_C
    return jnp.pad(flat, (0, (-rows % 8) * SMALL_C)).reshape(-1, SMALL_C)


def _unpack_small(pack, shapes):
    flat = pack.reshape(-1)
    out, off = [], 0
    for s in shapes:
        n = 1
        for d in s:
            n *= d
        out.append(flat[off:off + n].reshape(s))
        off += n + (-n % SMALL_C)
    return out


def _pad_lanes(v):
    return jnp.pad(v, ((0, 0), (0, LANES - v.shape[1])))


FETCH = (("ffn1_w_gate", "ffn1_w_up", "ffn1_w_down"), ("w_in", "conv_w", "w_out"),
         ("ffn2_w_gate", "ffn2_w_up", "ffn2_w_down", "ple_w_gate", "ple_w_proj"))
DONE = (("ffn2_w_gate", "ffn2_w_up", "ffn2_w_down", "w_out", "ple_w_gate", "ple_w_proj"),
        ("w_in", "ffn1_w_gate", "ffn1_w_up", "ffn1_w_down"))


def _local_step(x, p, tgt, fetch, S, on_grads):
    G = GM_WIDTH
    K = N_CHIPS
    b_st = S["gm_b_s"][0].T
    w_s = S["gm_w_s"][0]
    dtb, alog, dsk = _pad_lanes(S["dt_bias"]), _pad_lanes(S["a_log"]), _pad_lanes(S["d_skip"])
    gfin = S["final_norm"].reshape(1, -1)

    wg1, wu1, wd1 = fetch(0, None)
    h1, n1, a1, b1 = _ffn_fwd("ffn1_fwd", x, S["ffn1_norm"], wg1, wu1, wd1)
    w_in4, cw4, wo4 = fetch(1, h1)
    w_in = jnp.concatenate([w_in4[k] for k in range(K)], axis=1)
    w_uv = w_in[:, :2 * G]
    w_zxd = jnp.pad(w_in[:, 2 * G:], ((0, 0), (0, ZXD - (IN_PROJ - 2 * G))))
    conv_w = jnp.transpose(cw4, (1, 0, 2)).reshape(SSM_CONV, CONV_DIM)
    wo = wo4.reshape(-1, D_MODEL)
    n2, uv, z, xbc, dtr, ya = _mix_fwd(h1, S["mix_norm"], w_uv, w_zxd, S["gm_ln_g"], S["gm_ln_b"], w_s, b_st, S["gm_out_norm"])
    yb, sprev = _ssd_fwd(xbc, z, dtr, conv_w, S["conv_b"], dtb, alog, dsk, S["ssm_norm"])
    wg2, wu2, wd2, wpg4, wpp4 = fetch(2, yb)
    h2, h3, n3, a2, b2 = _ffn_fwd("ffn2_fwd", h1, S["ffn2_norm"], wg2, wu2, wd2, pre=(ya, yb, wo))
    dh3, loss, dgp, dwpg, dbpg, dwpp, dgf = _tail(h3, p, tgt, S["ple_norm"], wpg4.reshape(-1, D_MODEL), S["ple_b_gate"], wpp4, gfin)
    dh2, da2, db2, hm2, dg_ffn2, dya, dyb = _ffn_bwd("ffn2_bwd", dh3, h2, S["ffn2_norm"], a2, b2, wg2, wu2, wd2, wo=wo, ga=G)
    dw_out = jnp.concatenate([_matmul_tn("dw_out_a", ya, dh2), _matmul_tn("dw_out_b", yb, dh2)], axis=0).reshape(wo4.shape)
    zero = on_grads(0, [_matmul_tn("dw_ffn2_gate", n3, da2), _matmul_tn("dw_ffn2_up", n3, db2),
                        _matmul_tn("dw_ffn2_down", hm2, dh3, scale=0.5), dw_out,
                        dwpg.astype(bf16).reshape(wpg4.shape), dwpp.astype(bf16)])
    duv, dlng, dlnb, dws, dbst, dgout = _gm_bwd(uv, dya, S["gm_ln_g"], S["gm_ln_b"], w_s, b_st, S["gm_out_norm"] + zero)
    dzxd, dcw, dcb, ddtb, dalog, ddsk, dgssm = _ssd_bwd(xbc, z, dtr, sprev, dyb, conv_w, S["conv_b"], dtb, alog, dsk,
                                                        S["ssm_norm"] + zero)
    dh1, dg_mix = _mix_bwd(dh2, h1, S["mix_norm"], duv, dzxd, w_uv, w_zxd)
    dw_in = jnp.concatenate([_matmul_tn("dw_in_uv", n2, duv), _matmul_tn("dw_in_zxd", n2, dzxd)[:, :IN_PROJ - 2 * G]], axis=1)
    dx, da1, db1, hm1, dg_ffn1 = _ffn_bwd("ffn1_bwd", dh1, x, S["ffn1_norm"], a1, b1, wg1, wu1, wd1)
    on_grads(1, [jnp.transpose(dw_in.reshape(D_MODEL, K, IN_PROJ // K), (1, 0, 2)), _matmul_tn("dw_ffn1_gate", n1, da1),
                 _matmul_tn("dw_ffn1_up", n1, db1), _matmul_tn("dw_ffn1_down", hm1, dh1, scale=0.5)])
    nh = SSM_HEADS
    gS = {"ffn1_norm": dg_ffn1, "mix_norm": dg_mix, "gm_ln_g": dlng, "gm_ln_b": dlnb, "gm_w_s": dws[None], "gm_b_s": dbst.T[None],
          "gm_out_norm": dgout, "conv_b": dcb, "dt_bias": ddtb[:, :nh], "a_log": dalog[:, :nh], "d_skip": ddsk[:, :nh],
          "ssm_norm": dgssm, "ffn2_norm": dg_ffn2, "ple_norm": dgp, "ple_b_gate": dbpg, "final_norm": dgf.reshape(-1)}
    return loss, dx, dcw, gS


_WEIGHTS = ("ffn1_norm", "ffn1_w_gate", "ffn1_w_up", "ffn1_w_down", "mix_norm", "w_in", "gm_ln_g", "gm_ln_b", "gm_w_s", "gm_b_s",
            "gm_out_norm", "conv_w", "conv_b", "dt_bias", "a_log", "d_skip", "ssm_norm", "w_out", "ffn2_norm", "ffn2_w_gate",
            "ffn2_w_up", "ffn2_w_down", "ple_norm", "ple_w_gate", "ple_b_gate", "ple_w_proj", "final_norm")
_BIG_NAMES = BIG


def kernel(x, p, ffn1_norm, ffn1_w_gate, ffn1_w_up, ffn1_w_down, mix_norm, w_in, gm_ln_g, gm_ln_b, gm_w_s, gm_b_s, gm_out_norm, conv_w, conv_b, dt_bias, a_log, d_skip, ssm_norm, w_out, ffn2_norm, ffn2_w_gate, ffn2_w_up, ffn2_w_down, ple_norm, ple_w_gate, ple_b_gate, ple_w_proj, final_norm, loss_target, m_ffn1_norm, m_ffn1_w_gate, m_ffn1_w_up, m_ffn1_w_down, m_mix_norm, m_w_in, m_gm_ln_g, m_gm_ln_b, m_gm_w_s, m_gm_b_s, m_gm_out_norm, m_conv_w, m_conv_b, m_dt_bias, m_a_log, m_d_skip, m_ssm_norm, m_w_out, m_ffn2_norm, m_ffn2_w_gate, m_ffn2_w_up, m_ffn2_w_down, m_ple_norm, m_ple_w_gate, m_ple_b_gate, m_ple_w_proj, m_final_norm, v_ffn1_norm, v_ffn1_w_gate, v_ffn1_w_up, v_ffn1_w_down, v_mix_norm, v_w_in, v_gm_ln_g, v_gm_ln_b, v_gm_w_s, v_gm_b_s, v_gm_out_norm, v_conv_w, v_conv_b, v_dt_bias, v_a_log, v_d_skip, v_ssm_norm, v_w_out, v_ffn2_norm, v_ffn2_w_gate, v_ffn2_w_up, v_ffn2_w_down, v_ple_norm, v_ple_w_gate, v_ple_b_gate, v_ple_w_proj, v_final_norm):
    given = dict(locals())
    w = {n: given[n] for n in _WEIGHTS}
    m = {n: given["m_" + n] for n in _WEIGHTS}
    v = {n: given["v_" + n] for n in _WEIGHTS}

    c_idx = lax.axis_index("c").astype(jnp.int32).reshape(1)
    chip = 2 * lax.axis_index("x") + lax.axis_index("y")
    chip_idx = chip.astype(jnp.int32).reshape(1)

    shard = {n: w[n][0].astype(bf16) for n in BIG}
    shard["conv_w"] = w["conv_w"][0]
    first = _gather_weights([shard[n] for n in FETCH[0]], [True] * len(FETCH[0]))
    fetching, after = [], first[-1]
    for k in (1, 2):
        srcs = [shard[n] for n in FETCH[k]]
        lands = [jax.ShapeDtypeStruct((N_CHIPS,) + s.shape, s.dtype) for s in srcs]
        fetching.append(_copies_start("gather%d_start" % k, srcs, lands, 4 * len(srcs), _gather_copies, after))
        after = fetching[-1][4]

    def fetch(k, after_):
        return first if k == 0 else _copies_wait("gather%d_wait" % k, fetching[k - 1], _gather_copies, after_)[1]

    exchanging = []

    def on_grads(k, grads):
        others = _swap_halves("swap%d" % k, grads)
        parts = [_add_halves("add_" + n, g_, o_, c_idx) for n, g_, o_ in zip(DONE[k], grads, others)]
        lands = [jax.ShapeDtypeStruct((3,) + p_.shape[1:], p_.dtype) for p_ in parts]
        exchanging.append(_copies_start("exchange%d_start" % k, parts, lands, 3 * len(parts), _partial_copies, c_idx))
        return exchanging[-1][4][0, 0]

    S = {n: w[n] for n in SMALL}
    S["ffn1_norm"] = S["ffn1_norm"] + after[0, 0]
    loss, dx, dcw, gS = _local_step(x[0], p[0, 0], loss_target[0], fetch, S, on_grads)

    smalls = _exchange_small(_pack_small([gS[n] for n in SMALL] + [dcw, loss[:, :1]]))
    small_shapes = [w[n].shape for n in SMALL] + [dcw.shape, (1, 1)]
    small_sum = _unpack_small(_sum_slots("sum_small", smalls, 512), small_shapes)
    g = {n: small_sum[i] for i, n in enumerate(SMALL)}
    cshard = w["conv_w"].shape[2]
    g["conv_w"] = lax.dynamic_slice_in_dim(small_sum[len(SMALL)], chip * cshard, cshard, axis=1)[None]
    loss_total = small_sum[len(SMALL) + 1].reshape(())

    delta, new_m, new_v = {}, {}, {}
    after = smalls
    for k in range(len(DONE)):
        parts, recv = _copies_wait("exchange%d_wait" % k, exchanging[k], _partial_copies, after)
        mine = [_sum_partials("sum_" + n, p_, r_, chip_idx) for n, p_, r_ in zip(DONE[k], parts, recv)]
        theirs = _share_halves("share%d" % k, mine)
        for n, gm_, gt_ in zip(DONE[k], mine, theirs):
            g_, d_, m_, v_ = _adamw_big("adamw_" + n, w[n][0], gm_, gt_, m[n][0], v[n][0], c_idx)
            g[n], delta[n], new_m[n], new_v[n] = g_[None], d_[None], m_[None], v_[None]
        after = new_v[DONE[k][-1]]
    sm_names = SMALL + ("conv_w",)
    sm_shapes = [w[n].shape for n in sm_names]
    d_s, m_s, v_s = _adamw("adamw_small", _pack_small([w[n] for n in sm_names]), _pack_small([g[n] for n in sm_names]),
                           _pack_small([m[n] for n in sm_names]), _pack_small([v[n] for n in sm_names]))
    for dst, src in ((delta, d_s), (new_m, m_s), (new_v, v_s)):
        for n, val in zip(sm_names, _unpack_small(src, sm_shapes)):
            dst[n] = val

    return (loss_total, dx[None], *[g[n] for n in _WEIGHTS], *[delta[n] for n in _WEIGHTS],
            *[new_m[n] for n in _WEIGHTS], *[new_v[n] for n in _WEIGHTS])
```

```python
import functools

import jax
import jax.numpy as jnp
from jax import lax
from jax.experimental import pallas as pl
from jax.experimental.pallas import tpu as pltpu

f32 = jnp.float32
bf16 = jnp.bfloat16
MESH = pl.DeviceIdType.MESH
HIGHEST = lax.Precision.HIGHEST

EPS = 1e-6
N_CHIPS = 4
N_DEV = 8
D_MODEL = 1024
D_FF = 2816
D_PLE = 256
GM_WIDTH = 1024
GM_HEADS = 8
CHUNK = 128
SSM_WIDTH = 1024
SSM_HEADS = 16
SSM_HEAD_DIM = 64
SSM_GROUPS = 2
SSM_STATE = 128
SSM_CONV = 4
CONV_DIM = SSM_WIDTH + 2 * SSM_GROUPS * SSM_STATE
IN_PROJ = 2 * GM_WIDTH + SSM_WIDTH + CONV_DIM + SSM_HEADS
LANES = 128
ZXD = SSM_WIDTH + CONV_DIM + LANES

ADAM_LR = 0.001
ADAM_B1 = 0.9
ADAM_B2 = 0.999
ADAM_EPS = 1e-08
ADAM_WD = 0.01
ADAM_STEP = 10

VMEM_LIMIT = 56 * 1024 * 1024


def _dot(a, b):
    return jnp.dot(a, b, preferred_element_type=f32)


def _dot_nt(a, b):
    return lax.dot_general(a, b, (((1,), (1,)), ((), ())), preferred_element_type=f32)


def _dot_tn(a, b):
    return lax.dot_general(a, b, (((0,), (0,)), ((), ())), preferred_element_type=f32)


def _rms(x, g):
    return x * lax.rsqrt(jnp.mean(x * x, axis=-1, keepdims=True) + EPS) * g


def _gelu(x):
    return 0.5 * x * (1.0 + lax.erf(x * 0.7071067811865476))


def _layernorm(x, g, b):
    mu = jnp.mean(x, axis=-1, keepdims=True)
    xc = x - mu
    return xc * lax.rsqrt(jnp.mean(xc * xc, axis=-1, keepdims=True) + EPS) * g + b


def _sigmoid(x):
    return 1.0 / (1.0 + jnp.exp(-x))


def _softplus(x):
    return jnp.maximum(x, 0.0) + jnp.log(1.0 + jnp.exp(-jnp.abs(x)))


def _full(shape):
    nd = len(shape)
    return pl.BlockSpec(shape, lambda *_: (0,) * nd, pipeline_mode=pl.Buffered(1))


def _acc(shape):
    nd = len(shape)
    return pl.BlockSpec(shape, lambda *_: (0,) * nd)


def _rows(tm, ncols):
    return pl.BlockSpec((tm, ncols), lambda i: (i, 0))


def _params(sem):
    return pltpu.CompilerParams(dimension_semantics=sem, vmem_limit_bytes=VMEM_LIMIT)


def _row_tile(rows, target, mult=8):
    best = rows
    for t in range(mult, min(rows, target) + 1, mult):
        if rows % t == 0:
            best = t
    return best if best <= target else rows


def _stack_rows(k, tm, ncols):
    return pl.BlockSpec((k, tm, ncols), lambda i: (0, i, 0))


def _ffn_fwd(name, h, g, wg, wu, wd, pre=None, tm=256):
    T, D = h.shape
    K, _, Fs = wg.shape
    tm = min(tm, T)

    def body(*refs):
        if pre is None:
            h_ref, g_ref, wg_ref, wu_ref, wd_ref, ho_ref, n_ref, a_ref, b_ref = refs
            hin = h_ref[...]
        else:
            (h_ref, ya_ref, yb_ref, wo_ref, g_ref, wg_ref, wu_ref, wd_ref,
             hi_ref, ho_ref, n_ref, a_ref, b_ref) = refs
            ga = ya_ref.shape[1]
            hin = h_ref[...] + _dot(ya_ref[...], wo_ref[:ga, :]) + _dot(yb_ref[...], wo_ref[ga:, :])
            hi_ref[...] = hin
        n = _rms(hin, g_ref[...]).astype(bf16)
        n_ref[...] = n
        acc = jnp.zeros((tm, D), f32)
        for k in range(K):
            a = _dot(n, wg_ref[k]).astype(bf16)
            b = _dot(n, wu_ref[k]).astype(bf16)
            a_ref[k] = a
            b_ref[k] = b
            af = a.astype(f32)
            hm = (af * _sigmoid(af) * b.astype(f32)).astype(bf16)
            acc = acc + _dot(hm, wd_ref[k])
        ho_ref[...] = hin + 0.5 * acc

    ins = [h] + (list(pre) if pre is not None else []) + [g, wg, wu, wd]
    in_specs = [_rows(tm, D)]
    if pre is not None:
        in_specs += [_rows(tm, pre[0].shape[1]), _rows(tm, pre[1].shape[1]), _full(pre[2].shape)]
    in_specs += [_full(g.shape), _full(wg.shape), _full(wu.shape), _full(wd.shape)]
    outs = [jax.ShapeDtypeStruct((T, D), f32), jax.ShapeDtypeStruct((T, D), bf16),
            jax.ShapeDtypeStruct((K, T, Fs), bf16), jax.ShapeDtypeStruct((K, T, Fs), bf16)]
    out_specs = [_rows(tm, D), _rows(tm, D), _stack_rows(K, tm, Fs), _stack_rows(K, tm, Fs)]
    if pre is not None:
        outs = [jax.ShapeDtypeStruct((T, D), f32)] + outs
        out_specs = [_rows(tm, D)] + out_specs
    return pl.pallas_call(body, name=name, grid=(T // tm,), in_specs=in_specs, out_specs=out_specs,
                          out_shape=outs, compiler_params=_params(("parallel",)))(*ins)


def _ffn_bwd(name, dh, hin, g, a, b, wg, wu, wd, wo=None, ga=0, tm=256):
    T, D = dh.shape
    K, _, Fs = wg.shape
    tm = min(tm, T)

    def body(*refs):
        if wo is None:
            (dh_ref, hin_ref, g_ref, a_ref, b_ref, wg_ref, wu_ref, wd_ref,
             dhi_ref, da_ref, db_ref, hm_ref, dg_ref) = refs
        else:
            (dh_ref, hin_ref, g_ref, a_ref, b_ref, wg_ref, wu_ref, wd_ref, wo_ref,
             dhi_ref, da_ref, db_ref, hm_ref, dg_ref, dya_ref, dyb_ref) = refs

        @pl.when(pl.program_id(0) == 0)
        def _():
            dg_ref[...] = jnp.zeros_like(dg_ref)

        dh_ = dh_ref[...]
        dhb = (0.5 * dh_).astype(bf16)
        dn = jnp.zeros((tm, D), f32)
        for k in range(K):
            dhm = _dot_nt(dhb, wd_ref[k])
            af = a_ref[k].astype(f32)
            bf = b_ref[k].astype(f32)
            sg = _sigmoid(af)
            sl_ = af * sg
            da = (dhm * bf * (sg * (1.0 + af * (1.0 - sg)))).astype(bf16)
            db = (dhm * sl_).astype(bf16)
            da_ref[k] = da
            db_ref[k] = db
            hm_ref[k] = (sl_ * bf).astype(bf16)
            dn = dn + _dot_nt(da, wg_ref[k]) + _dot_nt(db, wu_ref[k])
        _, vjp = jax.vjp(_rms, hin_ref[...], g_ref[...])
        dx, dg = vjp(dn)
        dhi = dh_ + dx
        dhi_ref[...] = dhi
        dg_ref[...] += dg
        if wo is not None:
            dhib = dhi.astype(bf16)
            dya_ref[...] = _dot_nt(dhib, wo_ref[:ga, :]).astype(bf16)
            dyb_ref[...] = _dot_nt(dhib, wo_ref[ga:, :]).astype(bf16)

    ins = [dh, hin, g, a, b, wg, wu, wd]
    in_specs = [_rows(tm, D), _rows(tm, D), _full(g.shape), _stack_rows(K, tm, Fs), _stack_rows(K, tm, Fs),
                _full(wg.shape), _full(wu.shape), _full(wd.shape)]
    act = jax.ShapeDtypeStruct((K, T, Fs), bf16)
    outs = [jax.ShapeDtypeStruct((T, D), f32), act, act, act, jax.ShapeDtypeStruct(g.shape, f32)]
    out_specs = [_rows(tm, D), _stack_rows(K, tm, Fs), _stack_rows(K, tm, Fs), _stack_rows(K, tm, Fs), _acc(g.shape)]
    if wo is not None:
        gb = wo.shape[0] - ga
        ins += [wo]
        in_specs += [_full(wo.shape)]
        outs += [jax.ShapeDtypeStruct((T, ga), bf16), jax.ShapeDtypeStruct((T, gb), bf16)]
        out_specs += [_rows(tm, ga), _rows(tm, gb)]
    return pl.pallas_call(body, name=name, grid=(T // tm,), in_specs=in_specs, out_specs=out_specs,
                          out_shape=outs, compiler_params=_params(("arbitrary",)))(*ins)


def _matmul_tn(name, a, b, scale=1.0, tk=1024):
    ka = a.shape[0] if a.ndim == 3 else 0
    kb = b.shape[0] if b.ndim == 3 else 0
    K = max(ka, kb)
    T, M = a.shape[-2:]
    N = b.shape[-1]
    tk = min(tk, T)
    nk = T // tk
    if K:
        tn, nj = N, K
    else:
        tn = LANES * max(d for d in range(1, N // LANES + 1) if (N // LANES) % d == 0 and (d == 1 or M * d * LANES * 4 <= 6 * 1024 * 1024))
        nj = N // tn

    def body(a_ref, b_ref, o_ref, acc):
        k = pl.program_id(1)

        @pl.when(k == 0)
        def _():
            acc[...] = jnp.zeros_like(acc)

        bb = b_ref[...]
        if scale != 1.0:
            bb = bb * scale
        acc[...] += _dot_tn(a_ref[...].astype(bf16), bb.astype(bf16))

        @pl.when(k == nk - 1)
        def _():
            o_ref[...] = acc[...].astype(bf16)

    a_spec = pl.BlockSpec((None, tk, M), lambda j, k: (j, k, 0)) if ka else pl.BlockSpec((tk, M), lambda j, k: (k, 0))
    if kb:
        b_spec = pl.BlockSpec((None, tk, N), lambda j, k: (j, k, 0))
    elif K:
        b_spec = pl.BlockSpec((tk, N), lambda j, k: (k, 0))
    else:
        b_spec = pl.BlockSpec((tk, tn), lambda j, k: (k, j))
    if K:
        o_spec, o_shape = pl.BlockSpec((None, M, N), lambda j, k: (j, 0, 0)), (K, M, N)
    else:
        o_spec, o_shape = pl.BlockSpec((M, tn), lambda j, k: (0, j)), (M, N)
    return pl.pallas_call(
        body, name=name, grid=(nj, nk), in_specs=[a_spec, b_spec], out_specs=o_spec,
        out_shape=jax.ShapeDtypeStruct(o_shape, bf16), scratch_shapes=[pltpu.VMEM((M, tn), f32)],
        compiler_params=_params(("parallel", "arbitrary")))(a, b)


def _gm_pre(u, v, ln_g, ln_b):
    return _gelu(u), _layernorm(_gelu(v), ln_g, ln_b)


def _tril_mask():
    r = lax.broadcasted_iota(jnp.int32, (CHUNK, CHUNK), 0)
    c = lax.broadcasted_iota(jnp.int32, (CHUNK, CHUNK), 1)
    return c <= r


def _gm_mix(vnb, ws_ref, bst, mixed_sc, tm):
    mask = _tril_mask()
    for h in range(GM_HEADS):
        wt = jnp.where(mask, ws_ref[h], 0.0).astype(bf16)
        bias = bst[:, h:h + 1]
        for q in range(tm // CHUNK):
            rs = slice(q * CHUNK, (q + 1) * CHUNK)
            cs = slice(h * CHUNK, (h + 1) * CHUNK)
            mixed_sc[rs, cs] = _dot(wt, vnb[rs, cs]) + bias


def _mix_fwd(h1, gmix, w_uv, w_zxd, ln_g, ln_b, w_s, b_st, gout, tm=512):
    T, D = h1.shape
    tm = min(tm, T)
    G = GM_WIDTH

    def body(h_ref, g_ref, wuv_ref, wzxd_ref, lng_ref, lnb_ref, ws_ref, bst_ref, gout_ref,
             n_ref, uv_ref, z_ref, xbc_ref, dt_ref, ya_ref, mixed_sc):
        n = _rms(h_ref[...], g_ref[...]).astype(bf16)
        n_ref[...] = n
        u = _dot(n, wuv_ref[:, :G]).astype(bf16)
        v = _dot(n, wuv_ref[:, G:]).astype(bf16)
        uv_ref[:, :G] = u
        uv_ref[:, G:] = v
        z_ref[...] = _dot(n, wzxd_ref[:, :SSM_WIDTH]).astype(bf16)
        xbc_ref[...] = _dot(n, wzxd_ref[:, SSM_WIDTH:SSM_WIDTH + CONV_DIM]).astype(bf16)
        dt_ref[...] = _dot(n, wzxd_ref[:, SSM_WIDTH + CONV_DIM:])
        ug, vn = _gm_pre(u.astype(f32), v.astype(f32), lng_ref[...], lnb_ref[...])
        _gm_mix(vn.astype(bf16), ws_ref, bst_ref[...], mixed_sc, tm)
        ya_ref[...] = _rms(ug * mixed_sc[...], gout_ref[...]).astype(bf16)

    ins = [h1, gmix, w_uv, w_zxd, ln_g, ln_b, w_s, b_st, gout]
    in_specs = [_rows(tm, D)] + [_full(x.shape) for x in ins[1:]]
    outs = [jax.ShapeDtypeStruct((T, D), bf16), jax.ShapeDtypeStruct((T, 2 * G), bf16),
            jax.ShapeDtypeStruct((T, SSM_WIDTH), bf16), jax.ShapeDtypeStruct((T, CONV_DIM), bf16),
            jax.ShapeDtypeStruct((T, LANES), f32), jax.ShapeDtypeStruct((T, G), bf16)]
    out_specs = [_rows(tm, D), _rows(tm, 2 * G), _rows(tm, SSM_WIDTH), _rows(tm, CONV_DIM), _rows(tm, LANES), _rows(tm, G)]
    return pl.pallas_call(body, name="mix_fwd", grid=(T // tm,), in_specs=in_specs, out_specs=out_specs,
                          out_shape=outs, scratch_shapes=[pltpu.VMEM((tm, G), f32)],
                          compiler_params=_params(("parallel",)))(*ins)


def _gm_bwd(uv, dya, ln_g, ln_b, w_s, b_st, gout, tm=256):
    T = uv.shape[0]
    tm = min(tm, T)
    G = GM_WIDTH

    def body(uv_ref, dya_ref, lng_ref, lnb_ref, ws_ref, bst_ref, gout_ref,
             duv_ref, dlng_ref, dlnb_ref, dws_ref, dbst_ref, dgout_ref, mixed_sc, dvn_sc):
        @pl.when(pl.program_id(0) == 0)
        def _():
            for r in (dlng_ref, dlnb_ref, dws_ref, dbst_ref, dgout_ref):
                r[...] = jnp.zeros_like(r)

        u = uv_ref[:, :G].astype(f32)
        v = uv_ref[:, G:].astype(f32)
        (ug, vn), pre_vjp = jax.vjp(_gm_pre, u, v, lng_ref[...], lnb_ref[...])
        vnb = vn.astype(bf16)
        _gm_mix(vnb, ws_ref, bst_ref[...], mixed_sc, tm)
        mixed = mixed_sc[...]
        _, out_vjp = jax.vjp(_rms, ug * mixed, gout_ref[...])
        dpre, dgout = out_vjp(dya_ref[...].astype(f32))
        dgout_ref[...] += dgout
        dug = dpre * mixed
        dmixed = dpre * ug
        mask = _tril_mask()
        lane = lax.broadcasted_iota(jnp.int32, (1, GM_HEADS), 1)
        dbst = jnp.zeros((CHUNK, GM_HEADS), f32)
        for h in range(GM_HEADS):
            wt = jnp.where(mask, ws_ref[h], 0.0).astype(bf16)
            cs = slice(h * CHUNK, (h + 1) * CHUNK)
            dw = jnp.zeros((CHUNK, CHUNK), f32)
            for q in range(tm // CHUNK):
                rs = slice(q * CHUNK, (q + 1) * CHUNK)
                dm = dmixed[rs, cs]
                dmb = dm.astype(bf16)
                dw = dw + _dot_nt(dmb, vnb[rs, cs])
                dbst = dbst + jnp.sum(dm, axis=1, keepdims=True) * (lane == h).astype(f32)
                dvn_sc[rs, cs] = _dot_tn(wt, dmb)
            dws_ref[h] += jnp.where(mask, dw, 0.0)
        dbst_ref[...] += dbst
        du, dv, dlng, dlnb = pre_vjp((dug, dvn_sc[...]))
        duv_ref[:, :G] = du.astype(bf16)
        duv_ref[:, G:] = dv.astype(bf16)
        dlng_ref[...] += dlng
        dlnb_ref[...] += dlnb

    ins = [uv, dya, ln_g, ln_b, w_s, b_st, gout]
    in_specs = [_rows(tm, 2 * G), _rows(tm, G)] + [_full(x.shape) for x in ins[2:]]
    outs = [jax.ShapeDtypeStruct((T, 2 * G), bf16)] + [jax.ShapeDtypeStruct(x.shape, f32) for x in (ln_g, ln_b, w_s, b_st, gout)]
    out_specs = [_rows(tm, 2 * G)] + [_acc(x.shape) for x in (ln_g, ln_b, w_s, b_st, gout)]
    return pl.pallas_call(body, name="gm_bwd", grid=(T // tm,), in_specs=in_specs, out_specs=out_specs,
                          out_shape=outs, scratch_shapes=[pltpu.VMEM((tm, G), f32), pltpu.VMEM((tm, G), f32)],
                          compiler_params=_params(("arbitrary",)))(*ins)


def _mix_bwd(dh, h1, gmix, duv, dzxd, w_uv, w_zxd, tm=512):
    T, D = dh.shape
    tm = min(tm, T)

    def body(dh_ref, h_ref, g_ref, duv_ref, dzxd_ref, wuv_ref, wzxd_ref, dhi_ref, dg_ref):
        @pl.when(pl.program_id(0) == 0)
        def _():
            dg_ref[...] = jnp.zeros_like(dg_ref)

        dn = _dot_nt(duv_ref[...], wuv_ref[...]) + _dot_nt(dzxd_ref[...], wzxd_ref[...])
        _, vjp = jax.vjp(_rms, h_ref[...], g_ref[...])
        dx, dg = vjp(dn)
        dhi_ref[...] = dh_ref[...] + dx
        dg_ref[...] += dg

    ins = [dh, h1, gmix, duv, dzxd, w_uv, w_zxd]
    in_specs = [_rows(tm, D), _rows(tm, D), _full(gmix.shape), _rows(tm, duv.shape[1]), _rows(tm, dzxd.shape[1]),
                _full(w_uv.shape), _full(w_zxd.shape)]
    return pl.pallas_call(body, name="mix_bwd", grid=(T // tm,), in_specs=in_specs,
                          out_specs=[_rows(tm, D), _acc(gmix.shape)],
                          out_shape=[jax.ShapeDtypeStruct((T, D), f32), jax.ShapeDtypeStruct(gmix.shape, f32)],
                          compiler_params=_params(("arbitrary",)))(*ins)


HALO = 16
PAIRS = SSM_HEADS // 2
PAIR_W = 2 * SSM_HEAD_DIM


def _split(x, n):
    parts = []
    for _ in range(n):
        p = x.astype(bf16)
        parts.append(p)
        x = x - p.astype(f32)
    return parts


def _dot_sel(x, sel, n):
    out = None
    for p in _split(x, n):
        t = _dot(p, sel)
        out = t if out is None else out + t
    return out


def _sel_dot(sel, x, n):
    out = None
    for p in _split(x, n):
        t = _dot(sel, p)
        out = t if out is None else out + t
    return out


def _expand_mat():
    r = lax.broadcasted_iota(jnp.int32, (LANES, SSM_WIDTH), 0)
    c = lax.broadcasted_iota(jnp.int32, (LANES, SSM_WIDTH), 1)
    return (lax.shift_right_logical(c, 6) == r).astype(bf16)


def _reduce_mat():
    r = lax.broadcasted_iota(jnp.int32, (SSM_WIDTH, LANES), 0)
    c = lax.broadcasted_iota(jnp.int32, (SSM_WIDTH, LANES), 1)
    return (lax.shift_right_logical(r, 6) == c).astype(bf16)


def _shift_mat(rows, cols, off):
    r = lax.broadcasted_iota(jnp.int32, (rows, cols), 0)
    c = lax.broadcasted_iota(jnp.int32, (rows, cols), 1)
    return (c == r + off).astype(bf16)


def _ssd_front(c, xbc_ref, halo_ref, dtr_ref, cw_ref, cb_ref, dtb_ref, alog_ref):
    halo = halo_ref[...]
    ext = jnp.concatenate([jnp.where(c > 0, halo, jnp.zeros_like(halo)), xbc_ref[...]], axis=0)
    taps = [_dot(_shift_mat(CHUNK, HALO + CHUNK, HALO - SSM_CONV + 1 + j), ext) for j in range(SSM_CONV - 1)]
    taps.append(xbc_ref[...].astype(f32))
    xc = cb_ref[...] + cw_ref[0:1, :] * taps[0]
    for j in range(1, SSM_CONV):
        xc = xc + cw_ref[j:j + 1, :] * taps[j]
    sg = _sigmoid(xc)
    xa = xc * sg
    dt = _softplus(dtr_ref[...] + dtb_ref[...])
    a = -jnp.exp(alog_ref[...])
    acs = jnp.dot(_tril_mask().astype(f32), dt * a, preferred_element_type=f32, precision=HIGHEST)
    return taps, xc, sg, xa, dt, a, acs


def _ssd_wide(xa, dt, acs, dsk):
    ex = _expand_mat()
    dt_x = _dot_sel(dt, ex, 3)
    acs_x = _dot_sel(acs, ex, 3)
    dsk_x = _dot_sel(jnp.broadcast_to(dsk, (8, LANES)), ex, 3)[0:1]
    e_x = jnp.exp(acs_x)
    r_x = jnp.exp(acs_x[CHUNK - 1:CHUNK, :] - acs_x)
    xs = xa[:, :SSM_WIDTH]
    xd = xs * dt_x
    return dt_x, dsk_x, e_x, r_x, xs, xd, xd * r_x


def _pair_stack(v, lo):
    return jnp.concatenate([jnp.where(lo, v, 0.0), jnp.where(lo, 0.0, v)], axis=0)


def _ssd_pair(j, acs, acs_t, cb):
    out = []
    tril = _tril_mask()
    for h in (2 * j, 2 * j + 1):
        dk = jnp.exp(jnp.where(tril, acs[:, h:h + 1] - acs_t[h:h + 1, :], -jnp.inf))
        out.append((dk, cb * dk))
    return out


def _pair_col(row_lo, tot, j):
    return jnp.exp(jnp.where(row_lo, tot[:, 2 * j:2 * j + 1], tot[:, 2 * j + 1:2 * j + 2]))


def _gated_norm(y, z, g):
    yg = y * (z * _sigmoid(z))
    half = SSM_WIDTH // SSM_GROUPS
    parts = []
    for k in range(SSM_GROUPS):
        s = yg[:, k * half:(k + 1) * half]
        parts.append(s * lax.rsqrt(jnp.mean(s * s, axis=-1, keepdims=True) + EPS))
    return jnp.concatenate(parts, axis=1) * g


def _group_mats(xa):
    out = []
    for g in range(SSM_GROUPS):
        bm = xa[:, SSM_WIDTH + g * SSM_STATE:SSM_WIDTH + (g + 1) * SSM_STATE].astype(bf16)
        cm = xa[:, SSM_WIDTH + (SSM_GROUPS + g) * SSM_STATE:SSM_WIDTH + (SSM_GROUPS + g + 1) * SSM_STATE].astype(bf16)
        out.append((cm, bm, _dot_nt(cm, bm)))
    return out


def _ssd_in_specs(nc, rev):
    def ci(i):
        return nc - 1 - i if rev else i
    hp = CHUNK // HALO
    return [pl.BlockSpec((CHUNK, CONV_DIM), lambda i: (ci(i), 0)),
            pl.BlockSpec((HALO, CONV_DIM), lambda i: (jnp.maximum(ci(i) * hp - 1, 0), 0)),
            pl.BlockSpec((CHUNK, SSM_WIDTH), lambda i: (ci(i), 0)),
            pl.BlockSpec((CHUNK, LANES), lambda i: (ci(i), 0))]


def _ssd_fwd(xbc, z, dtr, conv_w, conv_b, dt_bias, a_log, d_skip, ssm_norm):
    T = xbc.shape[0]
    nc = T // CHUNK
    N = SSM_STATE

    def body(xbc_ref, halo_ref, z_ref, dtr_ref, cw_ref, cb_ref, dtb_ref, alog_ref, dsk_ref, g_ref,
             yb_ref, sprev_ref, s_sc):
        c = pl.program_id(0)

        @pl.when(c == 0)
        def _():
            s_sc[...] = jnp.zeros_like(s_sc)

        _, _, _, xa, dt, _, acs = _ssd_front(c, xbc_ref, halo_ref, dtr_ref, cw_ref, cb_ref, dtb_ref, alog_ref)
        _, dsk_x, e_x, _, xs, xd, gm = _ssd_wide(xa, dt, acs, dsk_ref[...])
        acs_t = acs.T
        tot = acs[CHUNK - 1:CHUNK, :]
        groups = _group_mats(xa)
        lo = lax.broadcasted_iota(jnp.int32, (CHUNK, PAIR_W), 1) < SSM_HEAD_DIM
        row_lo = lax.broadcasted_iota(jnp.int32, (PAIR_W, 1), 0) < SSM_HEAD_DIM
        ys = []
        for j in range(PAIRS):
            cmb, bmb, cb = groups[j // (PAIRS // SSM_GROUPS)]
            ps = slice(j * PAIR_W, (j + 1) * PAIR_W)
            (_, m0), (_, m1) = _ssd_pair(j, acs, acs_t, cb)
            sp = s_sc[j]
            yd = _dot(jnp.concatenate([m0, m1], axis=1).astype(bf16), _pair_stack(xd[:, ps], lo).astype(bf16))
            ys.append(yd + e_x[:, ps] * _dot_nt(cmb, sp.astype(bf16)))
            sprev_ref[0, j] = sp
            s_sc[j] = _pair_col(row_lo, tot, j) * sp + _dot_tn(gm[:, ps].astype(bf16), bmb)
        y = jnp.concatenate(ys, axis=1) + xs * dsk_x
        yb_ref[...] = _gated_norm(y, z_ref[...].astype(f32), g_ref[...]).astype(bf16)

    params = [conv_w, conv_b, dt_bias, a_log, d_skip, ssm_norm]
    return pl.pallas_call(
        body, name="ssd_fwd", grid=(nc,),
        in_specs=_ssd_in_specs(nc, False) + [_full(x.shape) for x in params],
        out_specs=[pl.BlockSpec((CHUNK, SSM_WIDTH), lambda i: (i, 0)), pl.BlockSpec((1, PAIRS, PAIR_W, N), lambda i: (i, 0, 0, 0))],
        out_shape=[jax.ShapeDtypeStruct((T, SSM_WIDTH), bf16), jax.ShapeDtypeStruct((nc, PAIRS, PAIR_W, N), f32)],
        scratch_shapes=[pltpu.VMEM((PAIRS, PAIR_W, N), f32)],
        compiler_params=_params(("arbitrary",)))(xbc, xbc, z, dtr, *params)


def _ssd_bwd(xbc, z, dtr, sprev, dyb, conv_w, conv_b, dt_bias, a_log, d_skip, ssm_norm):
    T = xbc.shape[0]
    nc = T // CHUNK
    H, N = SSM_HEADS, SSM_STATE
    PG = PAIRS // SSM_GROUPS

    def body(xbc_ref, halo_ref, z_ref, dtr_ref, sprev_ref, dyb_ref, cw_ref, cb_ref, dtb_ref, alog_ref, dsk_ref, g_ref,
             dzxd_ref, dcw_ref, dcb_ref, ddtb_ref, dalog_ref, ddsk_ref, dg_ref, ds_sc, next_sc):
        i = pl.program_id(0)
        c = nc - 1 - i

        @pl.when(i == 0)
        def _():
            ds_sc[...] = jnp.zeros_like(ds_sc)
            next_sc[...] = jnp.zeros_like(next_sc)
            for r_ in (dcw_ref, dcb_ref, ddtb_ref, dalog_ref, ddsk_ref, dg_ref):
                r_[...] = jnp.zeros_like(r_)

        taps, xc, sg, xa, dt, a, acs = _ssd_front(c, xbc_ref, halo_ref, dtr_ref, cw_ref, cb_ref, dtb_ref, alog_ref)
        dt_x, dsk_x, e_x, r_x, xs, xd, gm = _ssd_wide(xa, dt, acs, dsk_ref[...])
        acs_t = acs.T
        tot = acs[CHUNK - 1:CHUNK, :]
        groups = _group_mats(xa)
        lo = lax.broadcasted_iota(jnp.int32, (CHUNK, PAIR_W), 1) < SSM_HEAD_DIM
        row_lo = lax.broadcasted_iota(jnp.int32, (PAIR_W, 1), 0) < SSM_HEAD_DIM
        pairs, zs, yds = [], [], []
        for j in range(PAIRS):
            cmb, _, cb = groups[j // PG]
            ps = slice(j * PAIR_W, (j + 1) * PAIR_W)
            pairs.append(_ssd_pair(j, acs, acs_t, cb))
            (_, m0), (_, m1) = pairs[j]
            zs.append(_dot_nt(cmb, sprev_ref[0, j].astype(bf16)))
            yds.append(_dot(jnp.concatenate([m0, m1], axis=1).astype(bf16), _pair_stack(xd[:, ps], lo).astype(bf16)))
        zf = jnp.concatenate(zs, axis=1)
        y = jnp.concatenate(yds, axis=1) + e_x * zf + xs * dsk_x
        _, gn_vjp = jax.vjp(_gated_norm, y, z_ref[...].astype(f32), g_ref[...])
        dy, dz, dg = gn_vjp(dyb_ref[...].astype(f32))
        dg_ref[...] += dg
        dzxd_ref[:, :SSM_WIDTH] = dz.astype(bf16)

        lane = lax.broadcasted_iota(jnp.int32, (1, LANES), 1)
        sub = lax.broadcasted_iota(jnp.int32, (LANES, 1), 0)
        dacs = jnp.zeros((CHUNK, LANES), f32)
        dacs_r = jnp.zeros((LANES, CHUNK), f32)
        dtot = jnp.zeros((1, LANES), f32)
        dcb = [jnp.zeros((CHUNK, CHUNK), f32) for _ in range(SSM_GROUPS)]
        dcm = [jnp.zeros((CHUNK, N), f32) for _ in range(SSM_GROUPS)]
        dbm = [jnp.zeros((CHUNK, N), f32) for _ in range(SSM_GROUPS)]
        dxds, dgms = [], []
        for j in range(PAIRS):
            g = j // PG
            cmb, bmb, _ = groups[g]
            ps = slice(j * PAIR_W, (j + 1) * PAIR_W)
            (dk0, m0), (dk1, m1) = pairs[j]
            oh0, oh1 = (lane == 2 * j).astype(f32), (lane == 2 * j + 1).astype(f32)
            dyp = dy[:, ps]
            dy2 = _pair_stack(dyp, lo).astype(bf16)
            dm2 = _dot_nt(dy2, xd[:, ps].astype(bf16))
            m2 = jnp.concatenate([m0, m1], axis=0)
            dxds.append(_dot_tn(m2.astype(bf16), dy2))
            w2 = dm2 * m2
            rs = jnp.sum(w2, axis=1, keepdims=True)
            dacs = dacs + rs[:CHUNK] * oh0 + rs[CHUNK:] * oh1
            dacs_r = dacs_r - ((sub == 2 * j).astype(f32) * jnp.sum(w2[:CHUNK], axis=0, keepdims=True)
                               + (sub == 2 * j + 1).astype(f32) * jnp.sum(w2[CHUNK:], axis=0, keepdims=True))
            dcb[g] = dcb[g] + dm2[:CHUNK] * dk0 + dm2[CHUNK:] * dk1
            sp = sprev_ref[0, j]
            dzb = (dyp * e_x[:, ps]).astype(bf16)
            dcm[g] = dcm[g] + _dot(dzb, sp.astype(bf16))
            dsn = ds_sc[j]
            dsnb = dsn.astype(bf16)
            et = _pair_col(row_lo, tot, j)
            rr = jnp.sum(dsn * sp, axis=1, keepdims=True) * et
            dtot = dtot + jnp.sum(rr[:SSM_HEAD_DIM]) * oh0 + jnp.sum(rr[SSM_HEAD_DIM:]) * oh1
            dgms.append(_dot_nt(bmb, dsnb))
            dbm[g] = dbm[g] + _dot(gm[:, ps].astype(bf16), dsnb)
            ds_sc[j] = _dot_tn(dzb, cmb) + et * dsn
        dgm = jnp.concatenate(dgms, axis=1)
        dxd = jnp.concatenate(dxds, axis=1) + dgm * r_x
        dr = dgm * gm
        red = _dot_sel(jnp.concatenate([dy * e_x * zf - dr, dr, dxd * xs, dy * xs], axis=0), _reduce_mat(), 2)
        rowi = lax.broadcasted_iota(jnp.int32, (CHUNK, 1), 0)
        dtot = dtot + jnp.sum(red[CHUNK:2 * CHUNK], axis=0, keepdims=True)
        dacs = dacs + red[:CHUNK] + dacs_r.T + jnp.where(rowi == CHUNK - 1, dtot, 0.0)
        r2 = lax.broadcasted_iota(jnp.int32, (CHUNK, CHUNK), 0)
        c2 = lax.broadcasted_iota(jnp.int32, (CHUNK, CHUNK), 1)
        dadt = jnp.dot((c2 >= r2).astype(f32), dacs, preferred_element_type=f32, precision=HIGHEST)
        ddt = red[2 * CHUNK:3 * CHUNK] + dadt * a
        dalog_ref[...] += jnp.sum(dadt * dt, axis=0, keepdims=True) * a
        ddsk_ref[...] += jnp.sum(red[3 * CHUNK:], axis=0, keepdims=True)
        ddtr = jnp.where(lane < H, ddt * _sigmoid(dtr_ref[...] + dtb_ref[...]), 0.0)
        ddtb_ref[...] += jnp.sum(ddtr, axis=0, keepdims=True)
        dzxd_ref[:, SSM_WIDTH + CONV_DIM:] = ddtr.astype(bf16)
        dxa_bm, dxa_cm = [], []
        for g in range(SSM_GROUPS):
            cmb, bmb, _ = groups[g]
            dcbb = dcb[g].astype(bf16)
            dxa_bm.append(dbm[g] + _dot_tn(dcbb, cmb))
            dxa_cm.append(dcm[g] + _dot(dcbb, bmb))
        dxc = jnp.concatenate([dy * dsk_x + dxd * dt_x] + dxa_bm + dxa_cm, axis=1) * (sg * (1.0 + xc * (1.0 - sg)))
        ext = jnp.concatenate([dxc, next_sc[...]], axis=0)
        dxbc = cw_ref[SSM_CONV - 1:SSM_CONV, :] * dxc
        for s in range(1, SSM_CONV):
            dxbc = dxbc + cw_ref[SSM_CONV - 1 - s:SSM_CONV - s, :] * _sel_dot(_shift_mat(CHUNK, CHUNK + HALO, s), ext, 2)
        dzxd_ref[:, SSM_WIDTH:SSM_WIDTH + CONV_DIM] = dxbc.astype(bf16)
        dcw_ref[...] += jnp.concatenate([jnp.sum(dxc * t, axis=0, keepdims=True) for t in taps], axis=0)
        dcb_ref[...] += jnp.sum(dxc, axis=0, keepdims=True)
        next_sc[...] = dxc[0:HALO, :]

    params = [conv_w, conv_b, dt_bias, a_log, d_skip, ssm_norm]

    def rc(i):
        return nc - 1 - i

    in_specs = (_ssd_in_specs(nc, True)
                + [pl.BlockSpec((1, PAIRS, PAIR_W, N), lambda i: (rc(i), 0, 0, 0)), pl.BlockSpec((CHUNK, SSM_WIDTH), lambda i: (rc(i), 0))]
                + [_full(x.shape) for x in params])
    return pl.pallas_call(
        body, name="ssd_bwd", grid=(nc,), in_specs=in_specs,
        out_specs=[pl.BlockSpec((CHUNK, ZXD), lambda i: (rc(i), 0))] + [_acc(x.shape) for x in params],
        out_shape=[jax.ShapeDtypeStruct((T, ZXD), bf16)] + [jax.ShapeDtypeStruct(x.shape, f32) for x in params],
        scratch_shapes=[pltpu.VMEM((PAIRS, PAIR_W, N), f32), pltpu.VMEM((HALO, CONV_DIM), f32)],
        compiler_params=_params(("arbitrary",)))(xbc, xbc, z, dtr, sprev, dyb, *params)


def _tail(h3, p, tgt, gp, wpg, bpg, wpp, gf, tm=512):
    T, D = h3.shape
    tm = min(tm, T)

    def head(gpre, pp, h, gf_, t):
        gate = _sigmoid(gpre)
        y = _rms(h + gate * pp, gf_)
        err = y - t
        return 0.5 * jnp.sum(jnp.mean(err * err, axis=-1))

    def body(h_ref, p_ref, t_ref, gp_ref, wpg_ref, bpg_ref, wpp_ref, gf_ref,
             dh_ref, loss_ref, dgp_ref, dwpg_ref, dbpg_ref, dwpp_ref, dgf_ref):
        @pl.when(pl.program_id(0) == 0)
        def _():
            for r in (loss_ref, dgp_ref, dwpg_ref, dbpg_ref, dwpp_ref, dgf_ref):
                r[...] = jnp.zeros_like(r)

        h = h_ref[...]
        npf, np_vjp = jax.vjp(_rms, h, gp_ref[...])
        npb = npf.astype(bf16)
        pb = p_ref[...].astype(bf16)
        gpre = _dot(npb, wpg_ref[...]) + bpg_ref[...]
        kp, _, cp = wpp_ref.shape
        pp = jnp.concatenate([_dot(pb, wpp_ref[k]) for k in range(kp)], axis=1)
        loss, head_vjp = jax.vjp(head, gpre, pp, h, gf_ref[...], t_ref[...])
        dgpre, dpp, dh_a, dgf, _ = head_vjp(jnp.ones((), f32))
        loss_ref[...] += loss
        dgf_ref[...] += dgf
        dbpg_ref[...] += jnp.sum(dgpre, axis=0, keepdims=True)
        dgb = dgpre.astype(bf16)
        dwpg_ref[...] += _dot_tn(npb, dgb)
        dppb = dpp.astype(bf16)
        for k in range(kp):
            dwpp_ref[k] += _dot_tn(pb, dppb[:, k * cp:(k + 1) * cp])
        dh_b, dgp = np_vjp(_dot_nt(dgb, wpg_ref[...]))
        dgp_ref[...] += dgp
        dh_ref[...] = dh_a + dh_b

    ins = [h3, p, tgt, gp, wpg, bpg, wpp, gf]
    in_specs = [_rows(tm, D), _rows(tm, p.shape[1]), _rows(tm, D)] + [_full(x.shape) for x in ins[3:]]
    acc_shapes = [(1, LANES), gp.shape, wpg.shape, bpg.shape, wpp.shape, gf.shape]
    return pl.pallas_call(
        body, name="tail", grid=(T // tm,), in_specs=in_specs,
        out_specs=[_rows(tm, D)] + [_acc(s) for s in acc_shapes],
        out_shape=[jax.ShapeDtypeStruct((T, D), f32)] + [jax.ShapeDtypeStruct(s, f32) for s in acc_shapes],
        compiler_params=_params(("arbitrary",)))(*ins)


def _adamw(name, w, g, m, v, tr=256):
    R, C = w.shape
    tr = _row_tile(R, tr)

    def body(w_ref, g_ref, m_ref, v_ref, d_ref, mo_ref, vo_ref):
        g_ = g_ref[...]
        m_ = ADAM_B1 * m_ref[...] + (1.0 - ADAM_B1) * g_
        v_ = ADAM_B2 * v_ref[...] + (1.0 - ADAM_B2) * jnp.square(g_)
        m_hat = m_ / (1.0 - ADAM_B1 ** ADAM_STEP)
        v_hat = v_ / (1.0 - ADAM_B2 ** ADAM_STEP)
        d_ref[...] = -ADAM_LR * (m_hat / (jnp.sqrt(v_hat) + ADAM_EPS) + ADAM_WD * w_ref[...])
        mo_ref[...] = m_
        vo_ref[...] = v_

    spec = pl.BlockSpec((tr, C), lambda i: (i, 0))
    return pl.pallas_call(body, name=name, grid=(R // tr,), in_specs=[spec] * 4, out_specs=[spec] * 3,
                          out_shape=[jax.ShapeDtypeStruct((R, C), f32)] * 3,
                          compiler_params=_params(("parallel",)))(w, g, m, v)


HBM = pl.BlockSpec(memory_space=pltpu.HBM)


def _me():
    return lax.axis_index("x"), lax.axis_index("y"), lax.axis_index("c")


def _other_chips(x, y):
    return [(1 - x, y), (x, 1 - y), (1 - x, 1 - y)]


def _remote(src, dst, send_sem, recv_sem, dev):
    return pltpu.make_async_remote_copy(src_ref=src, dst_ref=dst, send_sem=send_sem, recv_sem=recv_sem,
                                        device_id=dev, device_id_type=MESH)


def _sems(n):
    return [pltpu.SemaphoreType.DMA((n,)), pltpu.SemaphoreType.DMA((n,))]


def _gather_weights(shards, split):
    n = len(shards)

    def body(*refs):
        ins, outs = refs[:n], refs[n:2 * n]
        own_send, own_recv, ici_send, ici_recv, d2d_send, d2d_recv = refs[2 * n:]
        x, y, c = _me()
        my_chip = 2 * x + y
        sibling = (x, y, 1 - c)
        chips = _other_chips(x, y)

        def rows(i, half):
            hr = shards[i].shape[0] // 2
            return pl.ds(half * hr, hr) if split[i] else pl.ds(0, shards[i].shape[0])

        sends = []
        for i in range(n):
            for j, chip in enumerate(chips):
                cp = _remote(ins[i].at[rows(i, c)], outs[i].at[my_chip, rows(i, c)],
                             ici_send.at[3 * i + j], ici_recv.at[3 * i + j], (*chip, c))
                cp.start()
                sends.append(cp)
            cp = _remote(ins[i], outs[i].at[my_chip], own_send.at[i], own_recv.at[i], sibling)
            cp.start()
            sends.append(cp)
        for i in range(n):
            for j, chip in enumerate(chips):
                s = 3 * i + j
                land = outs[i].at[2 * chip[0] + chip[1], rows(i, c)]
                _remote(land, land, ici_send.at[s], ici_recv.at[s], (*chip, c)).wait_recv()
                if split[i]:
                    cp = _remote(land, land, d2d_send.at[s], d2d_recv.at[s], sibling)
                    cp.start()
                    sends.append(cp)
        for i in range(n):
            _remote(ins[i], outs[i].at[my_chip], own_send.at[i], own_recv.at[i], sibling).wait_recv()
            if split[i]:
                for j, chip in enumerate(chips):
                    s = 3 * i + j
                    land = outs[i].at[2 * chip[0] + chip[1], rows(i, 1 - c)]
                    _remote(land, land, d2d_send.at[s], d2d_recv.at[s], sibling).wait_recv()
        for cp in sends:
            cp.wait_send()

    return pl.pallas_call(
        body, name="gather_weights", out_shape=[jax.ShapeDtypeStruct((N_CHIPS,) + s.shape, s.dtype) for s in shards],
        in_specs=[HBM] * n, out_specs=[HBM] * n,
        scratch_shapes=_sems(n) + _sems(3 * n) + _sems(3 * n))(*shards)


def _swap_halves(name, grads):
    n = len(grads)

    def body(*refs):
        ins, outs, send, recv = refs[:n], refs[n:2 * n], refs[2 * n], refs[2 * n + 1]
        x, y, c = _me()
        copies = []
        for i in range(n):
            hr = grads[i].shape[1] // 2
            cp = _remote(ins[i].at[:, pl.ds((1 - c) * hr, hr), :], outs[i], send.at[i], recv.at[i], (x, y, 1 - c))
            cp.start()
            copies.append(cp)
        for cp in copies:
            cp.wait()

    return pl.pallas_call(
        body, name=name,
        out_shape=[jax.ShapeDtypeStruct((g.shape[0], g.shape[1] // 2, g.shape[2]), g.dtype) for g in grads],
        in_specs=[HBM] * n, out_specs=[HBM] * n, scratch_shapes=_sems(n))(*grads)


def _add_halves(name, grads, other, c_idx, th=256):
    K, R, C = grads.shape
    H = R // 2
    th = _row_tile(H, th, 16)
    nb = H // th

    def body(c_ref, g_ref, o_ref, out_ref):
        out_ref[...] = (g_ref[...].astype(f32) + o_ref[...].astype(f32)).astype(bf16)

    grid_spec = pltpu.PrefetchScalarGridSpec(
        num_scalar_prefetch=1, grid=(nb,),
        in_specs=[pl.BlockSpec((K, th, C), lambda i, c: (0, c[0] * nb + i, 0)),
                  pl.BlockSpec((K, th, C), lambda i, c: (0, i, 0))],
        out_specs=pl.BlockSpec((K, th, C), lambda i, c: (0, i, 0)))
    return pl.pallas_call(body, name=name, grid_spec=grid_spec,
                          out_shape=jax.ShapeDtypeStruct((K, H, C), bf16),
                          compiler_params=_params(("parallel",)))(c_idx, grads, other)


SEM = pl.BlockSpec(memory_space=pltpu.SEMAPHORE)
ANY = pl.BlockSpec(memory_space=pl.ANY)
EFFECT = pltpu.SideEffectType.DATAFLOW_SIDE_EFFECTING


def _copies_start(name, srcs, land_shapes, n_copies, make_copies, after):
    ns, nl = len(srcs), len(land_shapes)
    lands = [lax.empty(s.shape, s.dtype) for s in land_shapes]

    def body(*refs):
        src_refs, land_refs = refs[:ns], refs[ns:ns + nl]
        send, recv, token = refs[ns + nl + 1], refs[ns + nl + 2], refs[-1]
        for cp in make_copies(src_refs, land_refs, send, recv):
            cp.start()
        token[...] = jnp.zeros_like(token)

    buffers = list(srcs) + lands
    out = pl.pallas_call(
        body, name=name,
        out_shape=(pltpu.SemaphoreType.DMA((n_copies,)), pltpu.SemaphoreType.DMA((n_copies,)),
                   *[pltpu.HBM(b.shape, b.dtype) for b in buffers], jax.ShapeDtypeStruct((8, LANES), f32)),
        in_specs=[HBM] * (ns + nl) + [ANY],
        out_specs=(SEM, SEM, *[HBM] * (ns + nl), pl.BlockSpec(memory_space=pltpu.VMEM)),
        input_output_aliases={i: 2 + i for i in range(ns + nl)},
        compiler_params=pltpu.CompilerParams(has_side_effects=EFFECT),
    )(*[pltpu.with_memory_space_constraint(b, pltpu.HBM) for b in buffers], after)
    return out[0], out[1], list(out[2:2 + ns]), list(out[2 + ns:2 + ns + nl]), out[-1]


def _copies_wait(name, started, make_copies, after):
    send, recv, srcs, lands, _ = started
    ns, nl = len(srcs), len(lands)

    def body(*refs):
        src_refs, land_refs = refs[:ns], refs[ns:ns + nl]
        for cp in make_copies(src_refs, land_refs, refs[ns + nl], refs[ns + nl + 1]):
            cp.wait_send()
            cp.wait_recv()

    buffers = list(srcs) + list(lands)
    out = pl.pallas_call(
        body, name=name, out_shape=tuple(pltpu.HBM(b.shape, b.dtype) for b in buffers),
        in_specs=[HBM] * (ns + nl) + [SEM, SEM, ANY], out_specs=tuple([HBM] * (ns + nl)),
        input_output_aliases={i: i for i in range(ns + nl)},
        compiler_params=pltpu.CompilerParams(has_side_effects=EFFECT),
    )(*buffers, send, recv, after)
    return list(out[:ns]), list(out[ns:])


def _gather_copies(src_refs, land_refs, send, recv):
    x, y, c = _me()
    my_chip = 2 * x + y
    peers = [(*chip, c) for chip in _other_chips(x, y)] + [(x, y, 1 - c)]
    return [_remote(src_refs[i], land_refs[i].at[my_chip], send.at[4 * i + j], recv.at[4 * i + j], peer)
            for i in range(len(src_refs)) for j, peer in enumerate(peers)]


def _partial_copies(src_refs, land_refs, send, recv):
    x, y, c = _me()
    return [_remote(src_refs[i].at[2 * chip[0] + chip[1]], land_refs[i].at[j], send.at[3 * i + j], recv.at[3 * i + j], (*chip, c))
            for i in range(len(src_refs)) for j, chip in enumerate(_other_chips(x, y))]


def _small_copies(src_refs, land_refs, send, recv):
    x, y, c = _me()
    return [_remote(src_refs[0], land_refs[0].at[k - 1], send.at[k - 1], recv.at[k - 1], (x ^ (k >> 2), y ^ ((k >> 1) & 1), c ^ (k & 1)))
            for k in range(1, N_DEV)]


def _sum_small(own, slots, dev_idx):
    R, C = own.shape

    def body(dev_ref, own_ref, s_ref, o_ref):
        me = dev_ref[0]
        acc = jnp.zeros((R, C), f32)
        for d in range(N_DEV):
            k = me ^ d
            acc = acc + jnp.where(k == 0, own_ref[...], s_ref[jnp.maximum(k - 1, 0)])
        o_ref[...] = acc

    grid_spec = pltpu.PrefetchScalarGridSpec(
        num_scalar_prefetch=1, grid=(1,),
        in_specs=[pl.BlockSpec((R, C), lambda i, dev: (0, 0)), pl.BlockSpec((N_DEV - 1, R, C), lambda i, dev: (0, 0, 0))],
        out_specs=pl.BlockSpec((R, C), lambda i, dev: (0, 0)))
    return pl.pallas_call(body, name="sum_small", grid_spec=grid_spec, out_shape=jax.ShapeDtypeStruct((R, C), f32),
                          compiler_params=_params(("arbitrary",)))(dev_idx, own, slots)


def _sum_partials(name, part, recv, chip_idx, th=256):
    K, H, C = part.shape
    th = _row_tile(H, th, 16)

    def body(chip_ref, p_ref, r_ref, o_ref):
        acc = p_ref[...].astype(f32)
        for j in range(3):
            acc = acc + r_ref[j].astype(f32)
        o_ref[...] = acc

    grid_spec = pltpu.PrefetchScalarGridSpec(
        num_scalar_prefetch=1, grid=(H // th,),
        in_specs=[pl.BlockSpec((None, th, C), lambda i, chip: (chip[0], i, 0)),
                  pl.BlockSpec((3, th, C), lambda i, chip: (0, i, 0))],
        out_specs=pl.BlockSpec((th, C), lambda i, chip: (i, 0)))
    return pl.pallas_call(body, name=name, grid_spec=grid_spec, out_shape=jax.ShapeDtypeStruct((H, C), f32),
                          compiler_params=_params(("parallel",)))(chip_idx, part, recv)


def _share_halves(name, halves):
    n = len(halves)

    def body(*refs):
        ins, outs, send, recv = refs[:n], refs[n:2 * n], refs[2 * n], refs[2 * n + 1]
        x, y, c = _me()
        copies = []
        for i in range(n):
            cp = _remote(ins[i], outs[i], send.at[i], recv.at[i], (x, y, 1 - c))
            cp.start()
            copies.append(cp)
        for cp in copies:
            cp.wait()

    return pl.pallas_call(
        body, name=name, out_shape=[jax.ShapeDtypeStruct(h.shape, h.dtype) for h in halves],
        in_specs=[HBM] * n, out_specs=[HBM] * n, scratch_shapes=_sems(n))(*halves)


def _adamw_big(name, w, g_mine, g_theirs, m, v, c_idx, tr=256):
    R, C = w.shape
    H = R // 2
    tr = _row_tile(H, tr)
    nb = H // tr

    def body(c_ref, w_ref, gm_ref, gt_ref, m_ref, v_ref, g_ref, d_ref, mo_ref, vo_ref):
        g_ = jnp.where(pl.program_id(0) // nb == c_ref[0], gm_ref[...], gt_ref[...])
        g_ref[...] = g_
        m_ = ADAM_B1 * m_ref[...] + (1.0 - ADAM_B1) * g_
        v_ = ADAM_B2 * v_ref[...] + (1.0 - ADAM_B2) * jnp.square(g_)
        m_hat = m_ / (1.0 - ADAM_B1 ** ADAM_STEP)
        v_hat = v_ / (1.0 - ADAM_B2 ** ADAM_STEP)
        d_ref[...] = -ADAM_LR * (m_hat / (jnp.sqrt(v_hat) + ADAM_EPS) + ADAM_WD * w_ref[...])
        mo_ref[...] = m_
        vo_ref[...] = v_

    full = pl.BlockSpec((tr, C), lambda i, c: (i, 0))
    half = pl.BlockSpec((tr, C), lambda i, c: (i % nb, 0))
    grid_spec = pltpu.PrefetchScalarGridSpec(num_scalar_prefetch=1, grid=(2 * nb,),
                                             in_specs=[full, half, half, full, full], out_specs=[full] * 4)
    return pl.pallas_call(body, name=name, grid_spec=grid_spec, out_shape=[jax.ShapeDtypeStruct((R, C), f32)] * 4,
                          compiler_params=_params(("parallel",)))(c_idx, w, g_mine, g_theirs, m, v)


BIG = ("ffn1_w_gate", "ffn1_w_up", "ffn1_w_down", "w_in", "w_out", "ffn2_w_gate", "ffn2_w_up", "ffn2_w_down",
       "ple_w_gate", "ple_w_proj")


SMALL = ("ffn1_norm", "mix_norm", "gm_ln_g", "gm_ln_b", "gm_w_s", "gm_b_s", "gm_out_norm", "conv_b", "dt_bias", "a_log",
         "d_skip", "ssm_norm", "ffn2_norm", "ple_norm", "ple_b_gate", "final_norm")
SMALL_C = 1024


def _pack_small(vals):
    parts = []
    for v in vals:
        f = v.astype(f32).reshape(-1)
        parts.append(jnp.pad(f, (0, -f.shape[0] % SMALL_C)))
    flat = jnp.concatenate(parts)
    rows = flat.shape[0] // SMALL_C
    return jnp.pad(flat, (0, (-rows % 8) * SMALL_C)).reshape(-1, SMALL_C)


def _unpack_small(pack, shapes):
    flat = pack.reshape(-1)
    out, off = [], 0
    for s in shapes:
        n = 1
        for d in s:
            n *= d
        out.append(flat[off:off + n].reshape(s))
        off += n + (-n % SMALL_C)
    return out


def _pad_lanes(v):
    return jnp.pad(v, ((0, 0), (0, LANES - v.shape[1])))


FETCH = (("ffn1_w_gate", "ffn1_w_up", "ffn1_w_down"), ("w_in", "conv_w", "w_out"),
         ("ffn2_w_gate", "ffn2_w_up", "ffn2_w_down", "ple_w_gate", "ple_w_proj"))
DONE = (("ffn2_w_gate", "ffn2_w_up", "ffn2_w_down", "w_out", "ple_w_gate", "ple_w_proj"), ("w_in",),
        ("ffn1_w_gate", "ffn1_w_up", "ffn1_w_down"))


def _local_step(x, p, tgt, fetch, S, on_grads):
    G = GM_WIDTH
    K = N_CHIPS
    b_st = S["gm_b_s"][0].T
    w_s = S["gm_w_s"][0]
    dtb, alog, dsk = _pad_lanes(S["dt_bias"]), _pad_lanes(S["a_log"]), _pad_lanes(S["d_skip"])
    gfin = S["final_norm"].reshape(1, -1)

    wg1, wu1, wd1 = fetch(0, None)
    h1, n1, a1, b1 = _ffn_fwd("ffn1_fwd", x, S["ffn1_norm"], wg1, wu1, wd1)
    w_in4, cw4, wo4 = fetch(1, h1)
    w_in = jnp.concatenate([w_in4[k] for k in range(K)], axis=1)
    w_uv = w_in[:, :2 * G]
    w_zxd = jnp.pad(w_in[:, 2 * G:], ((0, 0), (0, ZXD - (IN_PROJ - 2 * G))))
    conv_w = jnp.transpose(cw4, (1, 0, 2)).reshape(SSM_CONV, CONV_DIM)
    wo = wo4.reshape(-1, D_MODEL)
    n2, uv, z, xbc, dtr, ya = _mix_fwd(h1, S["mix_norm"], w_uv, w_zxd, S["gm_ln_g"], S["gm_ln_b"], w_s, b_st, S["gm_out_norm"])
    yb, sprev = _ssd_fwd(xbc, z, dtr, conv_w, S["conv_b"], dtb, alog, dsk, S["ssm_norm"])
    wg2, wu2, wd2, wpg4, wpp4 = fetch(2, yb)
    h2, h3, n3, a2, b2 = _ffn_fwd("ffn2_fwd", h1, S["ffn2_norm"], wg2, wu2, wd2, pre=(ya, yb, wo))
    dh3, loss, dgp, dwpg, dbpg, dwpp, dgf = _tail(h3, p, tgt, S["ple_norm"], wpg4.reshape(-1, D_MODEL), S["ple_b_gate"], wpp4, gfin)
    dh2, da2, db2, hm2, dg_ffn2, dya, dyb = _ffn_bwd("ffn2_bwd", dh3, h2, S["ffn2_norm"], a2, b2, wg2, wu2, wd2, wo=wo, ga=G)
    dw_out = jnp.concatenate([_matmul_tn("dw_out_a", ya, dh2), _matmul_tn("dw_out_b", yb, dh2)], axis=0).reshape(wo4.shape)
    zero = on_grads(0, [_matmul_tn("dw_ffn2_gate", n3, da2), _matmul_tn("dw_ffn2_up", n3, db2),
                        _matmul_tn("dw_ffn2_down", hm2, dh3, scale=0.5), dw_out,
                        dwpg.astype(bf16).reshape(wpg4.shape), dwpp.astype(bf16)])
    duv, dlng, dlnb, dws, dbst, dgout = _gm_bwd(uv, dya, S["gm_ln_g"], S["gm_ln_b"], w_s, b_st, S["gm_out_norm"] + zero)
    dzxd, dcw, dcb, ddtb, dalog, ddsk, dgssm = _ssd_bwd(xbc, z, dtr, sprev, dyb, conv_w, S["conv_b"], dtb, alog, dsk,
                                                        S["ssm_norm"] + zero)
    dh1, dg_mix = _mix_bwd(dh2, h1, S["mix_norm"], duv, dzxd, w_uv, w_zxd)
    dw_in = jnp.concatenate([_matmul_tn("dw_in_uv", n2, duv), _matmul_tn("dw_in_zxd", n2, dzxd)[:, :IN_PROJ - 2 * G]], axis=1)
    zero = on_grads(1, [jnp.transpose(dw_in.reshape(D_MODEL, K, IN_PROJ // K), (1, 0, 2))])
    dx, da1, db1, hm1, dg_ffn1 = _ffn_bwd("ffn1_bwd", dh1, x, S["ffn1_norm"] + zero, a1, b1, wg1, wu1, wd1)
    zero = on_grads(2, [_matmul_tn("dw_ffn1_gate", n1, da1), _matmul_tn("dw_ffn1_up", n1, db1),
                        _matmul_tn("dw_ffn1_down", hm1, dh1, scale=0.5)])
    loss = loss + zero
    nh = SSM_HEADS
    gS = {"ffn1_norm": dg_ffn1, "mix_norm": dg_mix, "gm_ln_g": dlng, "gm_ln_b": dlnb, "gm_w_s": dws[None], "gm_b_s": dbst.T[None],
          "gm_out_norm": dgout, "conv_b": dcb, "dt_bias": ddtb[:, :nh], "a_log": dalog[:, :nh], "d_skip": ddsk[:, :nh],
          "ssm_norm": dgssm, "ffn2_norm": dg_ffn2, "ple_norm": dgp, "ple_b_gate": dbpg, "final_norm": dgf.reshape(-1)}
    return loss, dx, dcw, gS


_WEIGHTS = ("ffn1_norm", "ffn1_w_gate", "ffn1_w_up", "ffn1_w_down", "mix_norm", "w_in", "gm_ln_g", "gm_ln_b", "gm_w_s", "gm_b_s",
            "gm_out_norm", "conv_w", "conv_b", "dt_bias", "a_log", "d_skip", "ssm_norm", "w_out", "ffn2_norm", "ffn2_w_gate",
            "ffn2_w_up", "ffn2_w_down", "ple_norm", "ple_w_gate", "ple_b_gate", "ple_w_proj", "final_norm")
_BIG_NAMES = BIG


def kernel(x, p, ffn1_norm, ffn1_w_gate, ffn1_w_up, ffn1_w_down, mix_norm, w_in, gm_ln_g, gm_ln_b, gm_w_s, gm_b_s, gm_out_norm, conv_w, conv_b, dt_bias, a_log, d_skip, ssm_norm, w_out, ffn2_norm, ffn2_w_gate, ffn2_w_up, ffn2_w_down, ple_norm, ple_w_gate, ple_b_gate, ple_w_proj, final_norm, loss_target, m_ffn1_norm, m_ffn1_w_gate, m_ffn1_w_up, m_ffn1_w_down, m_mix_norm, m_w_in, m_gm_ln_g, m_gm_ln_b, m_gm_w_s, m_gm_b_s, m_gm_out_norm, m_conv_w, m_conv_b, m_dt_bias, m_a_log, m_d_skip, m_ssm_norm, m_w_out, m_ffn2_norm, m_ffn2_w_gate, m_ffn2_w_up, m_ffn2_w_down, m_ple_norm, m_ple_w_gate, m_ple_b_gate, m_ple_w_proj, m_final_norm, v_ffn1_norm, v_ffn1_w_gate, v_ffn1_w_up, v_ffn1_w_down, v_mix_norm, v_w_in, v_gm_ln_g, v_gm_ln_b, v_gm_w_s, v_gm_b_s, v_gm_out_norm, v_conv_w, v_conv_b, v_dt_bias, v_a_log, v_d_skip, v_ssm_norm, v_w_out, v_ffn2_norm, v_ffn2_w_gate, v_ffn2_w_up, v_ffn2_w_down, v_ple_norm, v_ple_w_gate, v_ple_b_gate, v_ple_w_proj, v_final_norm):
    given = dict(locals())
    w = {n: given[n] for n in _WEIGHTS}
    m = {n: given["m_" + n] for n in _WEIGHTS}
    v = {n: given["v_" + n] for n in _WEIGHTS}

    c_idx = lax.axis_index("c").astype(jnp.int32).reshape(1)
    chip = 2 * lax.axis_index("x") + lax.axis_index("y")
    chip_idx = chip.astype(jnp.int32).reshape(1)

    shard = {n: w[n][0].astype(bf16) for n in BIG}
    shard["conv_w"] = w["conv_w"][0]
    first = _gather_weights([shard[n] for n in FETCH[0]], [True] * len(FETCH[0]))
    fetching, after = [], first[-1]
    for k in (1, 2):
        srcs = [shard[n] for n in FETCH[k]]
        lands = [jax.ShapeDtypeStruct((N_CHIPS,) + s.shape, s.dtype) for s in srcs]
        fetching.append(_copies_start("gather%d_start" % k, srcs, lands, 4 * len(srcs), _gather_copies, after))
        after = fetching[-1][4]

    def fetch(k, after_):
        return first if k == 0 else _copies_wait("gather%d_wait" % k, fetching[k - 1], _gather_copies, after_)[1]

    exchanging = []

    def on_grads(k, grads):
        others = _swap_halves("swap%d" % k, grads)
        parts = [_add_halves("add_" + n, g_, o_, c_idx) for n, g_, o_ in zip(DONE[k], grads, others)]
        lands = [jax.ShapeDtypeStruct((3,) + p_.shape[1:], p_.dtype) for p_ in parts]
        exchanging.append(_copies_start("exchange%d_start" % k, parts, lands, 3 * len(parts), _partial_copies, c_idx))
        return exchanging[-1][4][0, 0]

    S = {n: w[n] for n in SMALL}
    S["ffn1_norm"] = S["ffn1_norm"] + after[0, 0]
    loss, dx, dcw, gS = _local_step(x[0], p[0, 0], loss_target[0], fetch, S, on_grads)

    small = _pack_small([gS[n] for n in SMALL] + [dcw, loss[:, :1]])
    small_lands = [jax.ShapeDtypeStruct((N_DEV - 1,) + small.shape, small.dtype)]
    small_st = _copies_start("small_start", [small], small_lands, N_DEV - 1, _small_copies, c_idx)

    g, delta, new_m, new_v = {}, {}, {}, {}
    after = small_st[4]
    for k in range(len(DONE)):
        parts, recv = _copies_wait("exchange%d_wait" % k, exchanging[k], _partial_copies, after)
        mine = [_sum_partials("sum_" + n, p_, r_, chip_idx) for n, p_, r_ in zip(DONE[k], parts, recv)]
        theirs = _share_halves("share%d" % k, mine)
        for n, gm_, gt_ in zip(DONE[k], mine, theirs):
            g_, d_, m_, v_ = _adamw_big("adamw_" + n, w[n][0], gm_, gt_, m[n][0], v[n][0], c_idx)
            g[n], delta[n], new_m[n], new_v[n] = g_[None], d_[None], m_[None], v_[None]
        after = new_v[DONE[k][-1]]
    (own,), (slots,) = _copies_wait("small_wait", small_st, _small_copies, after)
    dev_idx = (2 * chip + lax.axis_index("c")).astype(jnp.int32).reshape(1)
    small_shapes = [w[n].shape for n in SMALL] + [dcw.shape, (1, 1)]
    small_sum = _unpack_small(_sum_small(own, slots, dev_idx), small_shapes)
    g.update({n: small_sum[i] for i, n in enumerate(SMALL)})
    cshard = w["conv_w"].shape[2]
    g["conv_w"] = lax.dynamic_slice_in_dim(small_sum[len(SMALL)], chip * cshard, cshard, axis=1)[None]
    loss_total = small_sum[len(SMALL) + 1].reshape(())
    sm_names = SMALL + ("conv_w",)
    sm_shapes = [w[n].shape for n in sm_names]
    d_s, m_s, v_s = _adamw("adamw_small", _pack_small([w[n] for n in sm_names]), _pack_small([g[n] for n in sm_names]),
                           _pack_small([m[n] for n in sm_names]), _pack_small([v[n] for n in sm_names]))
    for dst, src in ((delta, d_s), (new_m, m_s), (new_v, v_s)):
        for n, val in zip(sm_names, _unpack_small(src, sm_shapes)):
            dst[n] = val

    return (loss_total, dx[None], *[g[n] for n in _WEIGHTS], *[delta[n] for n in _WEIGHTS],
            *[new_m[n] for n in _WEIGHTS], *[new_v[n] for n in _WEIGHTS])
```

```python
import functools

import jax
import jax.numpy as jnp
from jax import lax
from jax.experimental import pallas as pl
from jax.experimental.pallas import tpu as pltpu

f32 = jnp.float32
bf16 = jnp.bfloat16
MESH = pl.DeviceIdType.MESH
HIGHEST = lax.Precision.HIGHEST

EPS = 1e-6
N_CHIPS = 4
N_DEV = 8
D_MODEL = 1024
D_FF = 2816
D_PLE = 256
GM_WIDTH = 1024
GM_HEADS = 8
CHUNK = 128
SSM_WIDTH = 1024
SSM_HEADS = 16
SSM_HEAD_DIM = 64
SSM_GROUPS = 2
SSM_STATE = 128
SSM_CONV = 4
CONV_DIM = SSM_WIDTH + 2 * SSM_GROUPS * SSM_STATE
IN_PROJ = 2 * GM_WIDTH + SSM_WIDTH + CONV_DIM + SSM_HEADS
LANES = 128
ZXD = SSM_WIDTH + CONV_DIM + LANES

ADAM_LR = 0.001
ADAM_B1 = 0.9
ADAM_B2 = 0.999
ADAM_EPS = 1e-08
ADAM_WD = 0.01
ADAM_STEP = 10

VMEM_LIMIT = 56 * 1024 * 1024


def _dot(a, b):
    return jnp.dot(a, b, preferred_element_type=f32)


def _dot_nt(a, b):
    return lax.dot_general(a, b, (((1,), (1,)), ((), ())), preferred_element_type=f32)


def _dot_tn(a, b):
    return lax.dot_general(a, b, (((0,), (0,)), ((), ())), preferred_element_type=f32)


def _rms(x, g):
    return x * lax.rsqrt(jnp.mean(x * x, axis=-1, keepdims=True) + EPS) * g


def _gelu(x):
    return 0.5 * x * (1.0 + lax.erf(x * 0.7071067811865476))


def _layernorm(x, g, b):
    mu = jnp.mean(x, axis=-1, keepdims=True)
    xc = x - mu
    return xc * lax.rsqrt(jnp.mean(xc * xc, axis=-1, keepdims=True) + EPS) * g + b


def _sigmoid(x):
    return 1.0 / (1.0 + jnp.exp(-x))


def _softplus(x):
    return jnp.maximum(x, 0.0) + jnp.log(1.0 + jnp.exp(-jnp.abs(x)))


def _full(shape):
    nd = len(shape)
    return pl.BlockSpec(shape, lambda *_: (0,) * nd, pipeline_mode=pl.Buffered(1))


def _acc(shape):
    nd = len(shape)
    return pl.BlockSpec(shape, lambda *_: (0,) * nd)


def _rows(tm, ncols):
    return pl.BlockSpec((tm, ncols), lambda i: (i, 0))


def _params(sem):
    return pltpu.CompilerParams(dimension_semantics=sem, vmem_limit_bytes=VMEM_LIMIT)


def _row_tile(rows, target, mult=8):
    best = rows
    for t in range(mult, min(rows, target) + 1, mult):
        if rows % t == 0:
            best = t
    return best if best <= target else rows


def _stack_rows(k, tm, ncols):
    return pl.BlockSpec((k, tm, ncols), lambda i: (0, i, 0))


def _ffn_fwd(name, h, g, wg, wu, wd, pre=None, tm=256):
    T, D = h.shape
    K, _, Fs = wg.shape
    tm = min(tm, T)

    def body(*refs):
        if pre is None:
            h_ref, g_ref, wg_ref, wu_ref, wd_ref, ho_ref, n_ref, a_ref, b_ref = refs
            hin = h_ref[...]
        else:
            (h_ref, ya_ref, yb_ref, wo_ref, g_ref, wg_ref, wu_ref, wd_ref,
             hi_ref, ho_ref, n_ref, a_ref, b_ref) = refs
            ga = ya_ref.shape[1]
            hin = h_ref[...] + _dot(ya_ref[...], wo_ref[:ga, :]) + _dot(yb_ref[...], wo_ref[ga:, :])
            hi_ref[...] = hin
        n = _rms(hin, g_ref[...]).astype(bf16)
        n_ref[...] = n
        acc = jnp.zeros((tm, D), f32)
        for k in range(K):
            a = _dot(n, wg_ref[k]).astype(bf16)
            b = _dot(n, wu_ref[k]).astype(bf16)
            a_ref[k] = a
            b_ref[k] = b
            af = a.astype(f32)
            hm = (af * _sigmoid(af) * b.astype(f32)).astype(bf16)
            acc = acc + _dot(hm, wd_ref[k])
        ho_ref[...] = hin + 0.5 * acc

    ins = [h] + (list(pre) if pre is not None else []) + [g, wg, wu, wd]
    in_specs = [_rows(tm, D)]
    if pre is not None:
        in_specs += [_rows(tm, pre[0].shape[1]), _rows(tm, pre[1].shape[1]), _full(pre[2].shape)]
    in_specs += [_full(g.shape), _full(wg.shape), _full(wu.shape), _full(wd.shape)]
    outs = [jax.ShapeDtypeStruct((T, D), f32), jax.ShapeDtypeStruct((T, D), bf16),
            jax.ShapeDtypeStruct((K, T, Fs), bf16), jax.ShapeDtypeStruct((K, T, Fs), bf16)]
    out_specs = [_rows(tm, D), _rows(tm, D), _stack_rows(K, tm, Fs), _stack_rows(K, tm, Fs)]
    if pre is not None:
        outs = [jax.ShapeDtypeStruct((T, D), f32)] + outs
        out_specs = [_rows(tm, D)] + out_specs
    return pl.pallas_call(body, name=name, grid=(T // tm,), in_specs=in_specs, out_specs=out_specs,
                          out_shape=outs, compiler_params=_params(("parallel",)))(*ins)


def _ffn_bwd(name, dh, hin, g, a, b, wg, wu, wd, wo=None, ga=0, tm=256):
    T, D = dh.shape
    K, _, Fs = wg.shape
    tm = min(tm, T)

    def body(*refs):
        if wo is None:
            (dh_ref, hin_ref, g_ref, a_ref, b_ref, wg_ref, wu_ref, wd_ref,
             dhi_ref, da_ref, db_ref, hm_ref, dg_ref) = refs
        else:
            (dh_ref, hin_ref, g_ref, a_ref, b_ref, wg_ref, wu_ref, wd_ref, wo_ref,
             dhi_ref, da_ref, db_ref, hm_ref, dg_ref, dya_ref, dyb_ref) = refs

        @pl.when(pl.program_id(0) == 0)
        def _():
            dg_ref[...] = jnp.zeros_like(dg_ref)

        dh_ = dh_ref[...]
        dhb = (0.5 * dh_).astype(bf16)
        dn = jnp.zeros((tm, D), f32)
        for k in range(K):
            dhm = _dot_nt(dhb, wd_ref[k])
            af = a_ref[k].astype(f32)
            bf = b_ref[k].astype(f32)
            sg = _sigmoid(af)
            sl_ = af * sg
            da = (dhm * bf * (sg * (1.0 + af * (1.0 - sg)))).astype(bf16)
            db = (dhm * sl_).astype(bf16)
            da_ref[k] = da
            db_ref[k] = db
            hm_ref[k] = (sl_ * bf).astype(bf16)
            dn = dn + _dot_nt(da, wg_ref[k]) + _dot_nt(db, wu_ref[k])
        _, vjp = jax.vjp(_rms, hin_ref[...], g_ref[...])
        dx, dg = vjp(dn)
        dhi = dh_ + dx
        dhi_ref[...] = dhi
        dg_ref[...] += dg
        if wo is not None:
            dhib = dhi.astype(bf16)
            dya_ref[...] = _dot_nt(dhib, wo_ref[:ga, :]).astype(bf16)
            dyb_ref[...] = _dot_nt(dhib, wo_ref[ga:, :]).astype(bf16)

    ins = [dh, hin, g, a, b, wg, wu, wd]
    in_specs = [_rows(tm, D), _rows(tm, D), _full(g.shape), _stack_rows(K, tm, Fs), _stack_rows(K, tm, Fs),
                _full(wg.shape), _full(wu.shape), _full(wd.shape)]
    act = jax.ShapeDtypeStruct((K, T, Fs), bf16)
    outs = [jax.ShapeDtypeStruct((T, D), f32), act, act, act, jax.ShapeDtypeStruct(g.shape, f32)]
    out_specs = [_rows(tm, D), _stack_rows(K, tm, Fs), _stack_rows(K, tm, Fs), _stack_rows(K, tm, Fs), _acc(g.shape)]
    if wo is not None:
        gb = wo.shape[0] - ga
        ins += [wo]
        in_specs += [_full(wo.shape)]
        outs += [jax.ShapeDtypeStruct((T, ga), bf16), jax.ShapeDtypeStruct((T, gb), bf16)]
        out_specs += [_rows(tm, ga), _rows(tm, gb)]
    return pl.pallas_call(body, name=name, grid=(T // tm,), in_specs=in_specs, out_specs=out_specs,
                          out_shape=outs, compiler_params=_params(("arbitrary",)))(*ins)


def _matmul_tn(name, a, b, scale=1.0, tk=1024):
    ka = a.shape[0] if a.ndim == 3 else 0
    kb = b.shape[0] if b.ndim == 3 else 0
    K = max(ka, kb)
    T, M = a.shape[-2:]
    N = b.shape[-1]
    tk = min(tk, T)
    nk = T // tk
    if K:
        tn, nj = N, K
    else:
        tn = LANES * max(d for d in range(1, N // LANES + 1) if (N // LANES) % d == 0 and (d == 1 or M * d * LANES * 4 <= 6 * 1024 * 1024))
        nj = N // tn

    def body(a_ref, b_ref, o_ref, acc):
        k = pl.program_id(1)

        @pl.when(k == 0)
        def _():
            acc[...] = jnp.zeros_like(acc)

        bb = b_ref[...]
        if scale != 1.0:
            bb = bb * scale
        acc[...] += _dot_tn(a_ref[...].astype(bf16), bb.astype(bf16))

        @pl.when(k == nk - 1)
        def _():
            o_ref[...] = acc[...].astype(bf16)

    a_spec = pl.BlockSpec((None, tk, M), lambda j, k: (j, k, 0)) if ka else pl.BlockSpec((tk, M), lambda j, k: (k, 0))
    if kb:
        b_spec = pl.BlockSpec((None, tk, N), lambda j, k: (j, k, 0))
    elif K:
        b_spec = pl.BlockSpec((tk, N), lambda j, k: (k, 0))
    else:
        b_spec = pl.BlockSpec((tk, tn), lambda j, k: (k, j))
    if K:
        o_spec, o_shape = pl.BlockSpec((None, M, N), lambda j, k: (j, 0, 0)), (K, M, N)
    else:
        o_spec, o_shape = pl.BlockSpec((M, tn), lambda j, k: (0, j)), (M, N)
    return pl.pallas_call(
        body, name=name, grid=(nj, nk), in_specs=[a_spec, b_spec], out_specs=o_spec,
        out_shape=jax.ShapeDtypeStruct(o_shape, bf16), scratch_shapes=[pltpu.VMEM((M, tn), f32)],
        compiler_params=_params(("parallel", "arbitrary")))(a, b)


def _gm_pre(u, v, ln_g, ln_b):
    return _gelu(u), _layernorm(_gelu(v), ln_g, ln_b)


def _tril_mask():
    r = lax.broadcasted_iota(jnp.int32, (CHUNK, CHUNK), 0)
    c = lax.broadcasted_iota(jnp.int32, (CHUNK, CHUNK), 1)
    return c <= r


def _gm_mix(vnb, ws_ref, bst, mixed_sc, tm):
    mask = _tril_mask()
    for h in range(GM_HEADS):
        wt = jnp.where(mask, ws_ref[h], 0.0).astype(bf16)
        bias = bst[:, h:h + 1]
        for q in range(tm // CHUNK):
            rs = slice(q * CHUNK, (q + 1) * CHUNK)
            cs = slice(h * CHUNK, (h + 1) * CHUNK)
            mixed_sc[rs, cs] = _dot(wt, vnb[rs, cs]) + bias


def _mix_fwd(h1, gmix, w_uv, w_zxd, ln_g, ln_b, w_s, b_st, gout, tm=512):
    T, D = h1.shape
    tm = min(tm, T)
    G = GM_WIDTH

    def body(h_ref, g_ref, wuv_ref, wzxd_ref, lng_ref, lnb_ref, ws_ref, bst_ref, gout_ref,
             n_ref, uv_ref, z_ref, xbc_ref, dt_ref, ya_ref, mixed_sc):
        n = _rms(h_ref[...], g_ref[...]).astype(bf16)
        n_ref[...] = n
        u = _dot(n, wuv_ref[:, :G]).astype(bf16)
        v = _dot(n, wuv_ref[:, G:]).astype(bf16)
        uv_ref[:, :G] = u
        uv_ref[:, G:] = v
        z_ref[...] = _dot(n, wzxd_ref[:, :SSM_WIDTH]).astype(bf16)
        xbc_ref[...] = _dot(n, wzxd_ref[:, SSM_WIDTH:SSM_WIDTH + CONV_DIM]).astype(bf16)
        dt_ref[...] = _dot(n, wzxd_ref[:, SSM_WIDTH + CONV_DIM:])
        ug, vn = _gm_pre(u.astype(f32), v.astype(f32), lng_ref[...], lnb_ref[...])
        _gm_mix(vn.astype(bf16), ws_ref, bst_ref[...], mixed_sc, tm)
        ya_ref[...] = _rms(ug * mixed_sc[...], gout_ref[...]).astype(bf16)

    ins = [h1, gmix, w_uv, w_zxd, ln_g, ln_b, w_s, b_st, gout]
    in_specs = [_rows(tm, D)] + [_full(x.shape) for x in ins[1:]]
    outs = [jax.ShapeDtypeStruct((T, D), bf16), jax.ShapeDtypeStruct((T, 2 * G), bf16),
            jax.ShapeDtypeStruct((T, SSM_WIDTH), bf16), jax.ShapeDtypeStruct((T, CONV_DIM), bf16),
            jax.ShapeDtypeStruct((T, LANES), f32), jax.ShapeDtypeStruct((T, G), bf16)]
    out_specs = [_rows(tm, D), _rows(tm, 2 * G), _rows(tm, SSM_WIDTH), _rows(tm, CONV_DIM), _rows(tm, LANES), _rows(tm, G)]
    return pl.pallas_call(body, name="mix_fwd", grid=(T // tm,), in_specs=in_specs, out_specs=out_specs,
                          out_shape=outs, scratch_shapes=[pltpu.VMEM((tm, G), f32)],
                          compiler_params=_params(("parallel",)))(*ins)


def _gm_bwd(uv, dya, ln_g, ln_b, w_s, b_st, gout, tm=256):
    T = uv.shape[0]
    tm = min(tm, T)
    G = GM_WIDTH

    def body(uv_ref, dya_ref, lng_ref, lnb_ref, ws_ref, bst_ref, gout_ref,
             duv_ref, dlng_ref, dlnb_ref, dws_ref, dbst_ref, dgout_ref, mixed_sc, dvn_sc):
        @pl.when(pl.program_id(0) == 0)
        def _():
            for r in (dlng_ref, dlnb_ref, dws_ref, dbst_ref, dgout_ref):
                r[...] = jnp.zeros_like(r)

        u = uv_ref[:, :G].astype(f32)
        v = uv_ref[:, G:].astype(f32)
        (ug, vn), pre_vjp = jax.vjp(_gm_pre, u, v, lng_ref[...], lnb_ref[...])
        vnb = vn.astype(bf16)
        _gm_mix(vnb, ws_ref, bst_ref[...], mixed_sc, tm)
        mixed = mixed_sc[...]
        _, out_vjp = jax.vjp(_rms, ug * mixed, gout_ref[...])
        dpre, dgout = out_vjp(dya_ref[...].astype(f32))
        dgout_ref[...] += dgout
        dug = dpre * mixed
        dmixed = dpre * ug
        mask = _tril_mask()
        lane = lax.broadcasted_iota(jnp.int32, (1, GM_HEADS), 1)
        dbst = jnp.zeros((CHUNK, GM_HEADS), f32)
        for h in range(GM_HEADS):
            wt = jnp.where(mask, ws_ref[h], 0.0).astype(bf16)
            cs = slice(h * CHUNK, (h + 1) * CHUNK)
            dw = jnp.zeros((CHUNK, CHUNK), f32)
            for q in range(tm // CHUNK):
                rs = slice(q * CHUNK, (q + 1) * CHUNK)
                dm = dmixed[rs, cs]
                dmb = dm.astype(bf16)
                dw = dw + _dot_nt(dmb, vnb[rs, cs])
                dbst = dbst + jnp.sum(dm, axis=1, keepdims=True) * (lane == h).astype(f32)
                dvn_sc[rs, cs] = _dot_tn(wt, dmb)
            dws_ref[h] += jnp.where(mask, dw, 0.0)
        dbst_ref[...] += dbst
        du, dv, dlng, dlnb = pre_vjp((dug, dvn_sc[...]))
        duv_ref[:, :G] = du.astype(bf16)
        duv_ref[:, G:] = dv.astype(bf16)
        dlng_ref[...] += dlng
        dlnb_ref[...] += dlnb

    ins = [uv, dya, ln_g, ln_b, w_s, b_st, gout]
    in_specs = [_rows(tm, 2 * G), _rows(tm, G)] + [_full(x.shape) for x in ins[2:]]
    outs = [jax.ShapeDtypeStruct((T, 2 * G), bf16)] + [jax.ShapeDtypeStruct(x.shape, f32) for x in (ln_g, ln_b, w_s, b_st, gout)]
    out_specs = [_rows(tm, 2 * G)] + [_acc(x.shape) for x in (ln_g, ln_b, w_s, b_st, gout)]
    return pl.pallas_call(body, name="gm_bwd", grid=(T // tm,), in_specs=in_specs, out_specs=out_specs,
                          out_shape=outs, scratch_shapes=[pltpu.VMEM((tm, G), f32), pltpu.VMEM((tm, G), f32)],
                          compiler_params=_params(("arbitrary",)))(*ins)


def _mix_bwd(dh, h1, gmix, duv, dzxd, w_uv, w_zxd, tm=512):
    T, D = dh.shape
    tm = min(tm, T)

    def body(dh_ref, h_ref, g_ref, duv_ref, dzxd_ref, wuv_ref, wzxd_ref, dhi_ref, dg_ref):
        @pl.when(pl.program_id(0) == 0)
        def _():
            dg_ref[...] = jnp.zeros_like(dg_ref)

        dn = _dot_nt(duv_ref[...], wuv_ref[...]) + _dot_nt(dzxd_ref[...], wzxd_ref[...])
        _, vjp = jax.vjp(_rms, h_ref[...], g_ref[...])
        dx, dg = vjp(dn)
        dhi_ref[...] = dh_ref[...] + dx
        dg_ref[...] += dg

    ins = [dh, h1, gmix, duv, dzxd, w_uv, w_zxd]
    in_specs = [_rows(tm, D), _rows(tm, D), _full(gmix.shape), _rows(tm, duv.shape[1]), _rows(tm, dzxd.shape[1]),
                _full(w_uv.shape), _full(w_zxd.shape)]
    return pl.pallas_call(body, name="mix_bwd", grid=(T // tm,), in_specs=in_specs,
                          out_specs=[_rows(tm, D), _acc(gmix.shape)],
                          out_shape=[jax.ShapeDtypeStruct((T, D), f32), jax.ShapeDtypeStruct(gmix.shape, f32)],
                          compiler_params=_params(("arbitrary",)))(*ins)


HALO = 16
PAIRS = SSM_HEADS // 2
PAIR_W = 2 * SSM_HEAD_DIM


def _split(x, n):
    parts = []
    for _ in range(n):
        p = x.astype(bf16)
        parts.append(p)
        x = x - p.astype(f32)
    return parts


def _dot_sel(x, sel, n):
    out = None
    for p in _split(x, n):
        t = _dot(p, sel)
        out = t if out is None else out + t
    return out


def _sel_dot(sel, x, n):
    out = None
    for p in _split(x, n):
        t = _dot(sel, p)
        out = t if out is None else out + t
    return out


def _expand_mat():
    r = lax.broadcasted_iota(jnp.int32, (LANES, SSM_WIDTH), 0)
    c = lax.broadcasted_iota(jnp.int32, (LANES, SSM_WIDTH), 1)
    return (lax.shift_right_logical(c, 6) == r).astype(bf16)


def _reduce_mat():
    r = lax.broadcasted_iota(jnp.int32, (SSM_WIDTH, LANES), 0)
    c = lax.broadcasted_iota(jnp.int32, (SSM_WIDTH, LANES), 1)
    return (lax.shift_right_logical(r, 6) == c).astype(bf16)


def _shift_mat(rows, cols, off):
    r = lax.broadcasted_iota(jnp.int32, (rows, cols), 0)
    c = lax.broadcasted_iota(jnp.int32, (rows, cols), 1)
    return (c == r + off).astype(bf16)


def _ssd_front(c, xbc_ref, halo_ref, dtr_ref, cw_ref, cb_ref, dtb_ref, alog_ref):
    halo = halo_ref[...]
    ext = jnp.concatenate([jnp.where(c > 0, halo, jnp.zeros_like(halo)), xbc_ref[...]], axis=0)
    taps = [_dot(_shift_mat(CHUNK, HALO + CHUNK, HALO - SSM_CONV + 1 + j), ext) for j in range(SSM_CONV - 1)]
    taps.append(xbc_ref[...].astype(f32))
    xc = cb_ref[...] + cw_ref[0:1, :] * taps[0]
    for j in range(1, SSM_CONV):
        xc = xc + cw_ref[j:j + 1, :] * taps[j]
    sg = _sigmoid(xc)
    xa = xc * sg
    dt = _softplus(dtr_ref[...] + dtb_ref[...])
    a = -jnp.exp(alog_ref[...])
    acs = jnp.dot(_tril_mask().astype(f32), dt * a, preferred_element_type=f32, precision=HIGHEST)
    return taps, xc, sg, xa, dt, a, acs


def _ssd_wide(xa, dt, acs, dsk):
    ex = _expand_mat()
    dt_x = _dot_sel(dt, ex, 3)
    acs_x = _dot_sel(acs, ex, 3)
    dsk_x = _dot_sel(jnp.broadcast_to(dsk, (8, LANES)), ex, 3)[0:1]
    e_x = jnp.exp(acs_x)
    r_x = jnp.exp(acs_x[CHUNK - 1:CHUNK, :] - acs_x)
    xs = xa[:, :SSM_WIDTH]
    xd = xs * dt_x
    return dt_x, dsk_x, e_x, r_x, xs, xd, xd * r_x


def _pair_stack(v, lo):
    return jnp.concatenate([jnp.where(lo, v, 0.0), jnp.where(lo, 0.0, v)], axis=0)


def _ssd_pair(j, acs, acs_t, cb):
    out = []
    tril = _tril_mask()
    for h in (2 * j, 2 * j + 1):
        dk = jnp.exp(jnp.where(tril, acs[:, h:h + 1] - acs_t[h:h + 1, :], -jnp.inf))
        out.append((dk, cb * dk))
    return out


def _pair_col(row_lo, tot, j):
    return jnp.exp(jnp.where(row_lo, tot[:, 2 * j:2 * j + 1], tot[:, 2 * j + 1:2 * j + 2]))


def _gated_norm(y, z, g):
    yg = y * (z * _sigmoid(z))
    half = SSM_WIDTH // SSM_GROUPS
    parts = []
    for k in range(SSM_GROUPS):
        s = yg[:, k * half:(k + 1) * half]
        parts.append(s * lax.rsqrt(jnp.mean(s * s, axis=-1, keepdims=True) + EPS))
    return jnp.concatenate(parts, axis=1) * g


def _group_mats(xa):
    out = []
    for g in range(SSM_GROUPS):
        bm = xa[:, SSM_WIDTH + g * SSM_STATE:SSM_WIDTH + (g + 1) * SSM_STATE].astype(bf16)
        cm = xa[:, SSM_WIDTH + (SSM_GROUPS + g) * SSM_STATE:SSM_WIDTH + (SSM_GROUPS + g + 1) * SSM_STATE].astype(bf16)
        out.append((cm, bm, _dot_nt(cm, bm)))
    return out


def _ssd_in_specs(nc, rev):
    def ci(i):
        return nc - 1 - i if rev else i
    hp = CHUNK // HALO
    return [pl.BlockSpec((CHUNK, CONV_DIM), lambda i: (ci(i), 0)),
            pl.BlockSpec((HALO, CONV_DIM), lambda i: (jnp.maximum(ci(i) * hp - 1, 0), 0)),
            pl.BlockSpec((CHUNK, SSM_WIDTH), lambda i: (ci(i), 0)),
            pl.BlockSpec((CHUNK, LANES), lambda i: (ci(i), 0))]


def _ssd_fwd(xbc, z, dtr, conv_w, conv_b, dt_bias, a_log, d_skip, ssm_norm):
    T = xbc.shape[0]
    nc = T // CHUNK
    N = SSM_STATE

    def body(xbc_ref, halo_ref, z_ref, dtr_ref, cw_ref, cb_ref, dtb_ref, alog_ref, dsk_ref, g_ref,
             yb_ref, sprev_ref, s_sc):
        c = pl.program_id(0)

        @pl.when(c == 0)
        def _():
            s_sc[...] = jnp.zeros_like(s_sc)

        _, _, _, xa, dt, _, acs = _ssd_front(c, xbc_ref, halo_ref, dtr_ref, cw_ref, cb_ref, dtb_ref, alog_ref)
        _, dsk_x, e_x, _, xs, xd, gm = _ssd_wide(xa, dt, acs, dsk_ref[...])
        acs_t = acs.T
        tot = acs[CHUNK - 1:CHUNK, :]
        groups = _group_mats(xa)
        lo = lax.broadcasted_iota(jnp.int32, (CHUNK, PAIR_W), 1) < SSM_HEAD_DIM
        row_lo = lax.broadcasted_iota(jnp.int32, (PAIR_W, 1), 0) < SSM_HEAD_DIM
        ys = []
        for j in range(PAIRS):
            cmb, bmb, cb = groups[j // (PAIRS // SSM_GROUPS)]
            ps = slice(j * PAIR_W, (j + 1) * PAIR_W)
            (_, m0), (_, m1) = _ssd_pair(j, acs, acs_t, cb)
            sp = s_sc[j]
            yd = _dot(jnp.concatenate([m0, m1], axis=1).astype(bf16), _pair_stack(xd[:, ps], lo).astype(bf16))
            ys.append(yd + e_x[:, ps] * _dot_nt(cmb, sp.astype(bf16)))
            sprev_ref[0, j] = sp
            s_sc[j] = _pair_col(row_lo, tot, j) * sp + _dot_tn(gm[:, ps].astype(bf16), bmb)
        y = jnp.concatenate(ys, axis=1) + xs * dsk_x
        yb_ref[...] = _gated_norm(y, z_ref[...].astype(f32), g_ref[...]).astype(bf16)

    params = [conv_w, conv_b, dt_bias, a_log, d_skip, ssm_norm]
    return pl.pallas_call(
        body, name="ssd_fwd", grid=(nc,),
        in_specs=_ssd_in_specs(nc, False) + [_full(x.shape) for x in params],
        out_specs=[pl.BlockSpec((CHUNK, SSM_WIDTH), lambda i: (i, 0)), pl.BlockSpec((1, PAIRS, PAIR_W, N), lambda i: (i, 0, 0, 0))],
        out_shape=[jax.ShapeDtypeStruct((T, SSM_WIDTH), bf16), jax.ShapeDtypeStruct((nc, PAIRS, PAIR_W, N), f32)],
        scratch_shapes=[pltpu.VMEM((PAIRS, PAIR_W, N), f32)],
        compiler_params=_params(("arbitrary",)))(xbc, xbc, z, dtr, *params)


def _ssd_bwd(xbc, z, dtr, sprev, dyb, conv_w, conv_b, dt_bias, a_log, d_skip, ssm_norm):
    T = xbc.shape[0]
    nc = T // CHUNK
    H, N = SSM_HEADS, SSM_STATE
    PG = PAIRS // SSM_GROUPS

    def body(xbc_ref, halo_ref, z_ref, dtr_ref, sprev_ref, dyb_ref, cw_ref, cb_ref, dtb_ref, alog_ref, dsk_ref, g_ref,
             dzxd_ref, dcw_ref, dcb_ref, ddtb_ref, dalog_ref, ddsk_ref, dg_ref, ds_sc, next_sc):
        i = pl.program_id(0)
        c = nc - 1 - i

        @pl.when(i == 0)
        def _():
            ds_sc[...] = jnp.zeros_like(ds_sc)
            next_sc[...] = jnp.zeros_like(next_sc)
            for r_ in (dcw_ref, dcb_ref, ddtb_ref, dalog_ref, ddsk_ref, dg_ref):
                r_[...] = jnp.zeros_like(r_)

        taps, xc, sg, xa, dt, a, acs = _ssd_front(c, xbc_ref, halo_ref, dtr_ref, cw_ref, cb_ref, dtb_ref, alog_ref)
        dt_x, dsk_x, e_x, r_x, xs, xd, gm = _ssd_wide(xa, dt, acs, dsk_ref[...])
        acs_t = acs.T
        tot = acs[CHUNK - 1:CHUNK, :]
        groups = _group_mats(xa)
        lo = lax.broadcasted_iota(jnp.int32, (CHUNK, PAIR_W), 1) < SSM_HEAD_DIM
        row_lo = lax.broadcasted_iota(jnp.int32, (PAIR_W, 1), 0) < SSM_HEAD_DIM
        pairs, zs, yds = [], [], []
        for j in range(PAIRS):
            cmb, _, cb = groups[j // PG]
            ps = slice(j * PAIR_W, (j + 1) * PAIR_W)
            pairs.append(_ssd_pair(j, acs, acs_t, cb))
            (_, m0), (_, m1) = pairs[j]
            zs.append(_dot_nt(cmb, sprev_ref[0, j].astype(bf16)))
            yds.append(_dot(jnp.concatenate([m0, m1], axis=1).astype(bf16), _pair_stack(xd[:, ps], lo).astype(bf16)))
        zf = jnp.concatenate(zs, axis=1)
        y = jnp.concatenate(yds, axis=1) + e_x * zf + xs * dsk_x
        _, gn_vjp = jax.vjp(_gated_norm, y, z_ref[...].astype(f32), g_ref[...])
        dy, dz, dg = gn_vjp(dyb_ref[...].astype(f32))
        dg_ref[...] += dg
        dzxd_ref[:, :SSM_WIDTH] = dz.astype(bf16)

        lane = lax.broadcasted_iota(jnp.int32, (1, LANES), 1)
        sub = lax.broadcasted_iota(jnp.int32, (LANES, 1), 0)
        dacs = jnp.zeros((CHUNK, LANES), f32)
        dacs_r = jnp.zeros((LANES, CHUNK), f32)
        dtot = jnp.zeros((1, LANES), f32)
        dcb = [jnp.zeros((CHUNK, CHUNK), f32) for _ in range(SSM_GROUPS)]
        dcm = [jnp.zeros((CHUNK, N), f32) for _ in range(SSM_GROUPS)]
        dbm = [jnp.zeros((CHUNK, N), f32) for _ in range(SSM_GROUPS)]
        dxds, dgms = [], []
        for j in range(PAIRS):
            g = j // PG
            cmb, bmb, _ = groups[g]
            ps = slice(j * PAIR_W, (j + 1) * PAIR_W)
            (dk0, m0), (dk1, m1) = pairs[j]
            oh0, oh1 = (lane == 2 * j).astype(f32), (lane == 2 * j + 1).astype(f32)
            dyp = dy[:, ps]
            dy2 = _pair_stack(dyp, lo).astype(bf16)
            dm2 = _dot_nt(dy2, xd[:, ps].astype(bf16))
            m2 = jnp.concatenate([m0, m1], axis=0)
            dxds.append(_dot_tn(m2.astype(bf16), dy2))
            w2 = dm2 * m2
            rs = jnp.sum(w2, axis=1, keepdims=True)
            dacs = dacs + rs[:CHUNK] * oh0 + rs[CHUNK:] * oh1
            dacs_r = dacs_r - ((sub == 2 * j).astype(f32) * jnp.sum(w2[:CHUNK], axis=0, keepdims=True)
                               + (sub == 2 * j + 1).astype(f32) * jnp.sum(w2[CHUNK:], axis=0, keepdims=True))
            dcb[g] = dcb[g] + dm2[:CHUNK] * dk0 + dm2[CHUNK:] * dk1
            sp = sprev_ref[0, j]
            dzb = (dyp * e_x[:, ps]).astype(bf16)
            dcm[g] = dcm[g] + _dot(dzb, sp.astype(bf16))
            dsn = ds_sc[j]
            dsnb = dsn.astype(bf16)
            et = _pair_col(row_lo, tot, j)
            rr = jnp.sum(dsn * sp, axis=1, keepdims=True) * et
            dtot = dtot + jnp.sum(rr[:SSM_HEAD_DIM]) * oh0 + jnp.sum(rr[SSM_HEAD_DIM:]) * oh1
            dgms.append(_dot_nt(bmb, dsnb))
            dbm[g] = dbm[g] + _dot(gm[:, ps].astype(bf16), dsnb)
            ds_sc[j] = _dot_tn(dzb, cmb) + et * dsn
        dgm = jnp.concatenate(dgms, axis=1)
        dxd = jnp.concatenate(dxds, axis=1) + dgm * r_x
        dr = dgm * gm
        red = _dot_sel(jnp.concatenate([dy * e_x * zf - dr, dr, dxd * xs, dy * xs], axis=0), _reduce_mat(), 2)
        rowi = lax.broadcasted_iota(jnp.int32, (CHUNK, 1), 0)
        dtot = dtot + jnp.sum(red[CHUNK:2 * CHUNK], axis=0, keepdims=True)
        dacs = dacs + red[:CHUNK] + dacs_r.T + jnp.where(rowi == CHUNK - 1, dtot, 0.0)
        r2 = lax.broadcasted_iota(jnp.int32, (CHUNK, CHUNK), 0)
        c2 = lax.broadcasted_iota(jnp.int32, (CHUNK, CHUNK), 1)
        dadt = jnp.dot((c2 >= r2).astype(f32), dacs, preferred_element_type=f32, precision=HIGHEST)
        ddt = red[2 * CHUNK:3 * CHUNK] + dadt * a
        dalog_ref[...] += jnp.sum(dadt * dt, axis=0, keepdims=True) * a
        ddsk_ref[...] += jnp.sum(red[3 * CHUNK:], axis=0, keepdims=True)
        ddtr = jnp.where(lane < H, ddt * _sigmoid(dtr_ref[...] + dtb_ref[...]), 0.0)
        ddtb_ref[...] += jnp.sum(ddtr, axis=0, keepdims=True)
        dzxd_ref[:, SSM_WIDTH + CONV_DIM:] = ddtr.astype(bf16)
        dxa_bm, dxa_cm = [], []
        for g in range(SSM_GROUPS):
            cmb, bmb, _ = groups[g]
            dcbb = dcb[g].astype(bf16)
            dxa_bm.append(dbm[g] + _dot_tn(dcbb, cmb))
            dxa_cm.append(dcm[g] + _dot(dcbb, bmb))
        dxc = jnp.concatenate([dy * dsk_x + dxd * dt_x] + dxa_bm + dxa_cm, axis=1) * (sg * (1.0 + xc * (1.0 - sg)))
        ext = jnp.concatenate([dxc, next_sc[...]], axis=0)
        dxbc = cw_ref[SSM_CONV - 1:SSM_CONV, :] * dxc
        for s in range(1, SSM_CONV):
            dxbc = dxbc + cw_ref[SSM_CONV - 1 - s:SSM_CONV - s, :] * _sel_dot(_shift_mat(CHUNK, CHUNK + HALO, s), ext, 2)
        dzxd_ref[:, SSM_WIDTH:SSM_WIDTH + CONV_DIM] = dxbc.astype(bf16)
        dcw_ref[...] += jnp.concatenate([jnp.sum(dxc * t, axis=0, keepdims=True) for t in taps], axis=0)
        dcb_ref[...] += jnp.sum(dxc, axis=0, keepdims=True)
        next_sc[...] = dxc[0:HALO, :]

    params = [conv_w, conv_b, dt_bias, a_log, d_skip, ssm_norm]

    def rc(i):
        return nc - 1 - i

    in_specs = (_ssd_in_specs(nc, True)
                + [pl.BlockSpec((1, PAIRS, PAIR_W, N), lambda i: (rc(i), 0, 0, 0)), pl.BlockSpec((CHUNK, SSM_WIDTH), lambda i: (rc(i), 0))]
                + [_full(x.shape) for x in params])
    return pl.pallas_call(
        body, name="ssd_bwd", grid=(nc,), in_specs=in_specs,
        out_specs=[pl.BlockSpec((CHUNK, ZXD), lambda i: (rc(i), 0))] + [_acc(x.shape) for x in params],
        out_shape=[jax.ShapeDtypeStruct((T, ZXD), bf16)] + [jax.ShapeDtypeStruct(x.shape, f32) for x in params],
        scratch_shapes=[pltpu.VMEM((PAIRS, PAIR_W, N), f32), pltpu.VMEM((HALO, CONV_DIM), f32)],
        compiler_params=_params(("arbitrary",)))(xbc, xbc, z, dtr, sprev, dyb, *params)


def _tail(h3, p, tgt, gp, wpg, bpg, wpp, gf, tm=512):
    T, D = h3.shape
    tm = min(tm, T)

    def head(gpre, pp, h, gf_, t):
        gate = _sigmoid(gpre)
        y = _rms(h + gate * pp, gf_)
        err = y - t
        return 0.5 * jnp.sum(jnp.mean(err * err, axis=-1))

    def body(h_ref, p_ref, t_ref, gp_ref, wpg_ref, bpg_ref, wpp_ref, gf_ref,
             dh_ref, loss_ref, dgp_ref, dwpg_ref, dbpg_ref, dwpp_ref, dgf_ref):
        @pl.when(pl.program_id(0) == 0)
        def _():
            for r in (loss_ref, dgp_ref, dwpg_ref, dbpg_ref, dwpp_ref, dgf_ref):
                r[...] = jnp.zeros_like(r)

        h = h_ref[...]
        npf, np_vjp = jax.vjp(_rms, h, gp_ref[...])
        npb = npf.astype(bf16)
        pb = p_ref[...].astype(bf16)
        gpre = _dot(npb, wpg_ref[...]) + bpg_ref[...]
        kp, _, cp = wpp_ref.shape
        pp = jnp.concatenate([_dot(pb, wpp_ref[k]) for k in range(kp)], axis=1)
        loss, head_vjp = jax.vjp(head, gpre, pp, h, gf_ref[...], t_ref[...])
        dgpre, dpp, dh_a, dgf, _ = head_vjp(jnp.ones((), f32))
        loss_ref[...] += loss
        dgf_ref[...] += dgf
        dbpg_ref[...] += jnp.sum(dgpre, axis=0, keepdims=True)
        dgb = dgpre.astype(bf16)
        dwpg_ref[...] += _dot_tn(npb, dgb)
        dppb = dpp.astype(bf16)
        for k in range(kp):
            dwpp_ref[k] += _dot_tn(pb, dppb[:, k * cp:(k + 1) * cp])
        dh_b, dgp = np_vjp(_dot_nt(dgb, wpg_ref[...]))
        dgp_ref[...] += dgp
        dh_ref[...] = dh_a + dh_b

    ins = [h3, p, tgt, gp, wpg, bpg, wpp, gf]
    in_specs = [_rows(tm, D), _rows(tm, p.shape[1]), _rows(tm, D)] + [_full(x.shape) for x in ins[3:]]
    acc_shapes = [(1, LANES), gp.shape, wpg.shape, bpg.shape, wpp.shape, gf.shape]
    return pl.pallas_call(
        body, name="tail", grid=(T // tm,), in_specs=in_specs,
        out_specs=[_rows(tm, D)] + [_acc(s) for s in acc_shapes],
        out_shape=[jax.ShapeDtypeStruct((T, D), f32)] + [jax.ShapeDtypeStruct(s, f32) for s in acc_shapes],
        compiler_params=_params(("arbitrary",)))(*ins)


def _adamw(name, w, g, m, v, tr=256):
    R, C = w.shape
    tr = _row_tile(R, tr)

    def body(w_ref, g_ref, m_ref, v_ref, d_ref, mo_ref, vo_ref):
        g_ = g_ref[...]
        m_ = ADAM_B1 * m_ref[...] + (1.0 - ADAM_B1) * g_
        v_ = ADAM_B2 * v_ref[...] + (1.0 - ADAM_B2) * jnp.square(g_)
        m_hat = m_ / (1.0 - ADAM_B1 ** ADAM_STEP)
        v_hat = v_ / (1.0 - ADAM_B2 ** ADAM_STEP)
        d_ref[...] = -ADAM_LR * (m_hat / (jnp.sqrt(v_hat) + ADAM_EPS) + ADAM_WD * w_ref[...])
        mo_ref[...] = m_
        vo_ref[...] = v_

    spec = pl.BlockSpec((tr, C), lambda i: (i, 0))
    return pl.pallas_call(body, name=name, grid=(R // tr,), in_specs=[spec] * 4, out_specs=[spec] * 3,
                          out_shape=[jax.ShapeDtypeStruct((R, C), f32)] * 3,
                          compiler_params=_params(("parallel",)))(w, g, m, v)


HBM = pl.BlockSpec(memory_space=pltpu.HBM)


def _me():
    return lax.axis_index("x"), lax.axis_index("y"), lax.axis_index("c")


def _other_chips(x, y):
    return [(1 - x, y), (x, 1 - y), (1 - x, 1 - y)]


def _remote(src, dst, send_sem, recv_sem, dev):
    return pltpu.make_async_remote_copy(src_ref=src, dst_ref=dst, send_sem=send_sem, recv_sem=recv_sem,
                                        device_id=dev, device_id_type=MESH)


def _sems(n):
    return [pltpu.SemaphoreType.DMA((n,)), pltpu.SemaphoreType.DMA((n,))]


def _gather_weights(shards, split):
    n = len(shards)

    def body(*refs):
        ins, outs = refs[:n], refs[n:2 * n]
        own_send, own_recv, ici_send, ici_recv, d2d_send, d2d_recv = refs[2 * n:]
        x, y, c = _me()
        my_chip = 2 * x + y
        sibling = (x, y, 1 - c)
        chips = _other_chips(x, y)

        def rows(i, half):
            hr = shards[i].shape[0] // 2
            return pl.ds(half * hr, hr) if split[i] else pl.ds(0, shards[i].shape[0])

        sends = []
        for i in range(n):
            for j, chip in enumerate(chips):
                cp = _remote(ins[i].at[rows(i, c)], outs[i].at[my_chip, rows(i, c)],
                             ici_send.at[3 * i + j], ici_recv.at[3 * i + j], (*chip, c))
                cp.start()
                sends.append(cp)
            cp = _remote(ins[i], outs[i].at[my_chip], own_send.at[i], own_recv.at[i], sibling)
            cp.start()
            sends.append(cp)
        for i in range(n):
            for j, chip in enumerate(chips):
                s = 3 * i + j
                land = outs[i].at[2 * chip[0] + chip[1], rows(i, c)]
                _remote(land, land, ici_send.at[s], ici_recv.at[s], (*chip, c)).wait_recv()
                if split[i]:
                    cp = _remote(land, land, d2d_send.at[s], d2d_recv.at[s], sibling)
                    cp.start()
                    sends.append(cp)
        for i in range(n):
            _remote(ins[i], outs[i].at[my_chip], own_send.at[i], own_recv.at[i], sibling).wait_recv()
            if split[i]:
                for j, chip in enumerate(chips):
                    s = 3 * i + j
                    land = outs[i].at[2 * chip[0] + chip[1], rows(i, 1 - c)]
                    _remote(land, land, d2d_send.at[s], d2d_recv.at[s], sibling).wait_recv()
        for cp in sends:
            cp.wait_send()

    return pl.pallas_call(
        body, name="gather_weights", out_shape=[jax.ShapeDtypeStruct((N_CHIPS,) + s.shape, s.dtype) for s in shards],
        in_specs=[HBM] * n, out_specs=[HBM] * n,
        scratch_shapes=_sems(n) + _sems(3 * n) + _sems(3 * n))(*shards)


def _swap_halves(name, grads):
    n = len(grads)

    def body(*refs):
        ins, outs, send, recv = refs[:n], refs[n:2 * n], refs[2 * n], refs[2 * n + 1]
        x, y, c = _me()
        copies = []
        for i in range(n):
            hr = grads[i].shape[1] // 2
            cp = _remote(ins[i].at[:, pl.ds((1 - c) * hr, hr), :], outs[i], send.at[i], recv.at[i], (x, y, 1 - c))
            cp.start()
            copies.append(cp)
        for cp in copies:
            cp.wait()

    return pl.pallas_call(
        body, name=name,
        out_shape=[jax.ShapeDtypeStruct((g.shape[0], g.shape[1] // 2, g.shape[2]), g.dtype) for g in grads],
        in_specs=[HBM] * n, out_specs=[HBM] * n, scratch_shapes=_sems(n))(*grads)


def _add_halves(name, grads, other, c_idx, th=256):
    K, R, C = grads.shape
    H = R // 2
    th = _row_tile(H, th, 16)
    nb = H // th

    def body(c_ref, g_ref, o_ref, out_ref):
        out_ref[...] = (g_ref[...].astype(f32) + o_ref[...].astype(f32)).astype(bf16)

    grid_spec = pltpu.PrefetchScalarGridSpec(
        num_scalar_prefetch=1, grid=(nb,),
        in_specs=[pl.BlockSpec((K, th, C), lambda i, c: (0, c[0] * nb + i, 0)),
                  pl.BlockSpec((K, th, C), lambda i, c: (0, i, 0))],
        out_specs=pl.BlockSpec((K, th, C), lambda i, c: (0, i, 0)))
    return pl.pallas_call(body, name=name, grid_spec=grid_spec,
                          out_shape=jax.ShapeDtypeStruct((K, H, C), bf16),
                          compiler_params=_params(("parallel",)))(c_idx, grads, other)


SEM = pl.BlockSpec(memory_space=pltpu.SEMAPHORE)
ANY = pl.BlockSpec(memory_space=pl.ANY)
EFFECT = pltpu.SideEffectType.DATAFLOW_SIDE_EFFECTING


def _copies_start(name, srcs, land_shapes, n_copies, make_copies, after):
    ns, nl = len(srcs), len(land_shapes)
    lands = [lax.empty(s.shape, s.dtype) for s in land_shapes]

    def body(*refs):
        src_refs, land_refs = refs[:ns], refs[ns:ns + nl]
        send, recv, token = refs[ns + nl + 1], refs[ns + nl + 2], refs[-1]
        for cp in make_copies(src_refs, land_refs, send, recv):
            cp.start()
        token[...] = jnp.zeros_like(token)

    buffers = list(srcs) + lands
    out = pl.pallas_call(
        body, name=name,
        out_shape=(pltpu.SemaphoreType.DMA((n_copies,)), pltpu.SemaphoreType.DMA((n_copies,)),
                   *[pltpu.HBM(b.shape, b.dtype) for b in buffers], jax.ShapeDtypeStruct((8, LANES), f32)),
        in_specs=[HBM] * (ns + nl) + [ANY],
        out_specs=(SEM, SEM, *[HBM] * (ns + nl), pl.BlockSpec(memory_space=pltpu.VMEM)),
        input_output_aliases={i: 2 + i for i in range(ns + nl)},
        compiler_params=pltpu.CompilerParams(has_side_effects=EFFECT),
    )(*[pltpu.with_memory_space_constraint(b, pltpu.HBM) for b in buffers], after)
    return out[0], out[1], list(out[2:2 + ns]), list(out[2 + ns:2 + ns + nl]), out[-1]


def _copies_wait(name, started, make_copies, after):
    send, recv, srcs, lands, _ = started
    ns, nl = len(srcs), len(lands)

    def body(*refs):
        src_refs, land_refs = refs[:ns], refs[ns:ns + nl]
        for cp in make_copies(src_refs, land_refs, refs[ns + nl], refs[ns + nl + 1]):
            cp.wait_send()
            cp.wait_recv()

    buffers = list(srcs) + list(lands)
    out = pl.pallas_call(
        body, name=name, out_shape=tuple(pltpu.HBM(b.shape, b.dtype) for b in buffers),
        in_specs=[HBM] * (ns + nl) + [SEM, SEM, ANY], out_specs=tuple([HBM] * (ns + nl)),
        input_output_aliases={i: i for i in range(ns + nl)},
        compiler_params=pltpu.CompilerParams(has_side_effects=EFFECT),
    )(*buffers, send, recv, after)
    return list(out[:ns]), list(out[ns:])


def _gather_copies(src_refs, land_refs, send, recv):
    x, y, c = _me()
    my_chip = 2 * x + y
    peers = [(*chip, c) for chip in _other_chips(x, y)] + [(x, y, 1 - c)]
    return [_remote(src_refs[i], land_refs[i].at[my_chip], send.at[4 * i + j], recv.at[4 * i + j], peer)
            for i in range(len(src_refs)) for j, peer in enumerate(peers)]


def _partial_copies(src_refs, land_refs, send, recv):
    x, y, c = _me()
    return [_remote(src_refs[i].at[2 * chip[0] + chip[1]], land_refs[i].at[j], send.at[3 * i + j], recv.at[3 * i + j], (*chip, c))
            for i in range(len(src_refs)) for j, chip in enumerate(_other_chips(x, y))]


def _small_copies(src_refs, land_refs, send, recv):
    x, y, c = _me()
    return [_remote(src_refs[0], land_refs[0].at[k - 1], send.at[k - 1], recv.at[k - 1], (x ^ (k >> 2), y ^ ((k >> 1) & 1), c ^ (k & 1)))
            for k in range(1, N_DEV)]


def _sum_small(own, slots, dev_idx):
    R, C = own.shape

    def body(dev_ref, own_ref, s_ref, o_ref):
        me = dev_ref[0]
        acc = jnp.zeros((R, C), f32)
        for d in range(N_DEV):
            k = me ^ d
            acc = acc + jnp.where(k == 0, own_ref[...], s_ref[jnp.maximum(k - 1, 0)])
        o_ref[...] = acc

    grid_spec = pltpu.PrefetchScalarGridSpec(
        num_scalar_prefetch=1, grid=(1,),
        in_specs=[pl.BlockSpec((R, C), lambda i, dev: (0, 0)), pl.BlockSpec((N_DEV - 1, R, C), lambda i, dev: (0, 0, 0))],
        out_specs=pl.BlockSpec((R, C), lambda i, dev: (0, 0)))
    return pl.pallas_call(body, name="sum_small", grid_spec=grid_spec, out_shape=jax.ShapeDtypeStruct((R, C), f32),
                          compiler_params=_params(("arbitrary",)))(dev_idx, own, slots)


def _sum_partials(name, part, recv, chip_idx, th=256):
    K, H, C = part.shape
    th = _row_tile(H, th, 16)

    def body(chip_ref, p_ref, r_ref, o_ref):
        acc = p_ref[...].astype(f32)
        for j in range(3):
            acc = acc + r_ref[j].astype(f32)
        o_ref[...] = acc

    grid_spec = pltpu.PrefetchScalarGridSpec(
        num_scalar_prefetch=1, grid=(H // th,),
        in_specs=[pl.BlockSpec((None, th, C), lambda i, chip: (chip[0], i, 0)),
                  pl.BlockSpec((3, th, C), lambda i, chip: (0, i, 0))],
        out_specs=pl.BlockSpec((th, C), lambda i, chip: (i, 0)))
    return pl.pallas_call(body, name=name, grid_spec=grid_spec, out_shape=jax.ShapeDtypeStruct((H, C), f32),
                          compiler_params=_params(("parallel",)))(chip_idx, part, recv)


def _share_halves(name, halves):
    n = len(halves)

    def body(*refs):
        ins, outs, send, recv = refs[:n], refs[n:2 * n], refs[2 * n], refs[2 * n + 1]
        x, y, c = _me()
        copies = []
        for i in range(n):
            cp = _remote(ins[i], outs[i], send.at[i], recv.at[i], (x, y, 1 - c))
            cp.start()
            copies.append(cp)
        for cp in copies:
            cp.wait()

    return pl.pallas_call(
        body, name=name, out_shape=[jax.ShapeDtypeStruct(h.shape, h.dtype) for h in halves],
        in_specs=[HBM] * n, out_specs=[HBM] * n, scratch_shapes=_sems(n))(*halves)


def _adamw_big(name, w, g_mine, g_theirs, m, v, c_idx, tr=256):
    R, C = w.shape
    H = R // 2
    tr = _row_tile(H, tr)
    nb = H // tr

    def body(c_ref, w_ref, gm_ref, gt_ref, m_ref, v_ref, g_ref, d_ref, mo_ref, vo_ref):
        g_ = jnp.where(pl.program_id(0) // nb == c_ref[0], gm_ref[...], gt_ref[...])
        g_ref[...] = g_
        m_ = ADAM_B1 * m_ref[...] + (1.0 - ADAM_B1) * g_
        v_ = ADAM_B2 * v_ref[...] + (1.0 - ADAM_B2) * jnp.square(g_)
        m_hat = m_ / (1.0 - ADAM_B1 ** ADAM_STEP)
        v_hat = v_ / (1.0 - ADAM_B2 ** ADAM_STEP)
        d_ref[...] = -ADAM_LR * (m_hat / (jnp.sqrt(v_hat) + ADAM_EPS) + ADAM_WD * w_ref[...])
        mo_ref[...] = m_
        vo_ref[...] = v_

    full = pl.BlockSpec((tr, C), lambda i, c: (i, 0))
    half = pl.BlockSpec((tr, C), lambda i, c: (i % nb, 0))
    grid_spec = pltpu.PrefetchScalarGridSpec(num_scalar_prefetch=1, grid=(2 * nb,),
                                             in_specs=[full, half, half, full, full], out_specs=[full] * 4)
    return pl.pallas_call(body, name=name, grid_spec=grid_spec, out_shape=[jax.ShapeDtypeStruct((R, C), f32)] * 4,
                          compiler_params=_params(("parallel",)))(c_idx, w, g_mine, g_theirs, m, v)


BIG = ("ffn1_w_gate", "ffn1_w_up", "ffn1_w_down", "w_in", "w_out", "ffn2_w_gate", "ffn2_w_up", "ffn2_w_down",
       "ple_w_gate", "ple_w_proj")


SMALL = ("ffn1_norm", "mix_norm", "gm_ln_g", "gm_ln_b", "gm_w_s", "gm_b_s", "gm_out_norm", "conv_b", "dt_bias", "a_log",
         "d_skip", "ssm_norm", "ffn2_norm", "ple_norm", "ple_b_gate", "final_norm")
SMALL_C = 1024


def _pack_small(vals):
    parts = []
    for v in vals:
        f = v.astype(f32).reshape(-1)
        parts.append(jnp.pad(f, (0, -f.shape[0] % SMALL_C)))
    flat = jnp.concatenate(parts)
    rows = flat.shape[0] // SMALL_C
    return jnp.pad(flat, (0, (-rows % 8) * SMALL_C)).reshape(-1, SMALL_C)


def _unpack_small(pack, shapes):
    flat = pack.reshape(-1)
    out, off = [], 0
    for s in shapes:
        n = 1
        for d in s:
            n *= d
        out.append(flat[off:off + n].reshape(s))
        off += n + (-n % SMALL_C)
    return out


def _pad_lanes(v):
    return jnp.pad(v, ((0, 0), (0, LANES - v.shape[1])))


FETCH = (("ffn1_w_gate", "ffn1_w_up", "ffn1_w_down"), ("w_in", "conv_w", "w_out"),
         ("ffn2_w_gate", "ffn2_w_up", "ffn2_w_down", "ple_w_gate", "ple_w_proj"))
TRANSPOSED = ("ffn1_w_gate", "ffn1_w_up", "ffn2_w_gate", "ffn2_w_up")
DONE =(("ffn2_w_gate", "ffn2_w_up", "ffn2_w_down", "w_out", "ple_w_gate", "ple_w_proj"), ("w_in",),
        ("ffn1_w_gate", "ffn1_w_up", "ffn1_w_down"))


def _local_step(x, p, tgt, fetch, S, on_grads):
    G = GM_WIDTH
    K = N_CHIPS
    b_st = S["gm_b_s"][0].T
    w_s = S["gm_w_s"][0]
    dtb, alog, dsk = _pad_lanes(S["dt_bias"]), _pad_lanes(S["a_log"]), _pad_lanes(S["d_skip"])
    gfin = S["final_norm"].reshape(1, -1)

    wg1, wu1, wd1 = fetch(0, None)
    h1, n1, a1, b1 = _ffn_fwd("ffn1_fwd", x, S["ffn1_norm"], wg1, wu1, wd1)
    w_in4, cw4, wo4 = fetch(1, h1)
    w_in = jnp.concatenate([w_in4[k] for k in range(K)], axis=1)
    w_uv = w_in[:, :2 * G]
    w_zxd = jnp.pad(w_in[:, 2 * G:], ((0, 0), (0, ZXD - (IN_PROJ - 2 * G))))
    conv_w = jnp.transpose(cw4, (1, 0, 2)).reshape(SSM_CONV, CONV_DIM)
    wo = wo4.reshape(-1, D_MODEL)
    n2, uv, z, xbc, dtr, ya = _mix_fwd(h1, S["mix_norm"], w_uv, w_zxd, S["gm_ln_g"], S["gm_ln_b"], w_s, b_st, S["gm_out_norm"])
    yb, sprev = _ssd_fwd(xbc, z, dtr, conv_w, S["conv_b"], dtb, alog, dsk, S["ssm_norm"])
    wg2, wu2, wd2, wpg4, wpp4 = fetch(2, yb)
    h2, h3, n3, a2, b2 = _ffn_fwd("ffn2_fwd", h1, S["ffn2_norm"], wg2, wu2, wd2, pre=(ya, yb, wo))
    dh3, loss, dgp, dwpg, dbpg, dwpp, dgf = _tail(h3, p, tgt, S["ple_norm"], wpg4.reshape(-1, D_MODEL), S["ple_b_gate"], wpp4, gfin)
    dh2, da2, db2, hm2, dg_ffn2, dya, dyb = _ffn_bwd("ffn2_bwd", dh3, h2, S["ffn2_norm"], a2, b2, wg2, wu2, wd2, wo=wo, ga=G)
    dw_out = jnp.concatenate([_matmul_tn("dw_out_a", ya, dh2), _matmul_tn("dw_out_b", yb, dh2)], axis=0).reshape(wo4.shape)
    zero = on_grads(0, [_matmul_tn("dw_ffn2_gate", da2, n3), _matmul_tn("dw_ffn2_up", db2, n3),
                        _matmul_tn("dw_ffn2_down", hm2, dh3, scale=0.5), dw_out,
                        dwpg.astype(bf16).reshape(wpg4.shape), dwpp.astype(bf16)])
    duv, dlng, dlnb, dws, dbst, dgout = _gm_bwd(uv, dya, S["gm_ln_g"], S["gm_ln_b"], w_s, b_st, S["gm_out_norm"] + zero)
    dzxd, dcw, dcb, ddtb, dalog, ddsk, dgssm = _ssd_bwd(xbc, z, dtr, sprev, dyb, conv_w, S["conv_b"], dtb, alog, dsk,
                                                        S["ssm_norm"] + zero)
    dh1, dg_mix = _mix_bwd(dh2, h1, S["mix_norm"], duv, dzxd, w_uv, w_zxd)
    dw_in = jnp.concatenate([_matmul_tn("dw_in_uv", n2, duv), _matmul_tn("dw_in_zxd", n2, dzxd)[:, :IN_PROJ - 2 * G]], axis=1)
    zero = on_grads(1, [jnp.transpose(dw_in.reshape(D_MODEL, K, IN_PROJ // K), (1, 0, 2))])
    dx, da1, db1, hm1, dg_ffn1 = _ffn_bwd("ffn1_bwd", dh1, x, S["ffn1_norm"] + zero, a1, b1, wg1, wu1, wd1)
    zero = on_grads(2, [_matmul_tn("dw_ffn1_gate", da1, n1), _matmul_tn("dw_ffn1_up", db1, n1),
                        _matmul_tn("dw_ffn1_down", hm1, dh1, scale=0.5)])
    loss = loss + zero
    nh = SSM_HEADS
    gS = {"ffn1_norm": dg_ffn1, "mix_norm": dg_mix, "gm_ln_g": dlng, "gm_ln_b": dlnb, "gm_w_s": dws[None], "gm_b_s": dbst.T[None],
          "gm_out_norm": dgout, "conv_b": dcb, "dt_bias": ddtb[:, :nh], "a_log": dalog[:, :nh], "d_skip": ddsk[:, :nh],
          "ssm_norm": dgssm, "ffn2_norm": dg_ffn2, "ple_norm": dgp, "ple_b_gate": dbpg, "final_norm": dgf.reshape(-1)}
    return loss, dx, dcw, gS


_WEIGHTS = ("ffn1_norm", "ffn1_w_gate", "ffn1_w_up", "ffn1_w_down", "mix_norm", "w_in", "gm_ln_g", "gm_ln_b", "gm_w_s", "gm_b_s",
            "gm_out_norm", "conv_w", "conv_b", "dt_bias", "a_log", "d_skip", "ssm_norm", "w_out", "ffn2_norm", "ffn2_w_gate",
            "ffn2_w_up", "ffn2_w_down", "ple_norm", "ple_w_gate", "ple_b_gate", "ple_w_proj", "final_norm")
_BIG_NAMES = BIG


def kernel(x, p, ffn1_norm, ffn1_w_gate, ffn1_w_up, ffn1_w_down, mix_norm, w_in, gm_ln_g, gm_ln_b, gm_w_s, gm_b_s, gm_out_norm, conv_w, conv_b, dt_bias, a_log, d_skip, ssm_norm, w_out, ffn2_norm, ffn2_w_gate, ffn2_w_up, ffn2_w_down, ple_norm, ple_w_gate, ple_b_gate, ple_w_proj, final_norm, loss_target, m_ffn1_norm, m_ffn1_w_gate, m_ffn1_w_up, m_ffn1_w_down, m_mix_norm, m_w_in, m_gm_ln_g, m_gm_ln_b, m_gm_w_s, m_gm_b_s, m_gm_out_norm, m_conv_w, m_conv_b, m_dt_bias, m_a_log, m_d_skip, m_ssm_norm, m_w_out, m_ffn2_norm, m_ffn2_w_gate, m_ffn2_w_up, m_ffn2_w_down, m_ple_norm, m_ple_w_gate, m_ple_b_gate, m_ple_w_proj, m_final_norm, v_ffn1_norm, v_ffn1_w_gate, v_ffn1_w_up, v_ffn1_w_down, v_mix_norm, v_w_in, v_gm_ln_g, v_gm_ln_b, v_gm_w_s, v_gm_b_s, v_gm_out_norm, v_conv_w, v_conv_b, v_dt_bias, v_a_log, v_d_skip, v_ssm_norm, v_w_out, v_ffn2_norm, v_ffn2_w_gate, v_ffn2_w_up, v_ffn2_w_down, v_ple_norm, v_ple_w_gate, v_ple_b_gate, v_ple_w_proj, v_final_norm):
    given = dict(locals())
    w = {n: given[n] for n in _WEIGHTS}
    m = {n: given["m_" + n] for n in _WEIGHTS}
    v = {n: given["v_" + n] for n in _WEIGHTS}

    c_idx = lax.axis_index("c").astype(jnp.int32).reshape(1)
    chip = 2 * lax.axis_index("x") + lax.axis_index("y")
    chip_idx = chip.astype(jnp.int32).reshape(1)

    shard = {n: w[n][0].astype(bf16) for n in BIG}
    shard["conv_w"] = w["conv_w"][0]
    first = _gather_weights([shard[n] for n in FETCH[0]], [True] * len(FETCH[0]))
    fetching, after = [], first[-1]
    for k in (1, 2):
        srcs = [shard[n] for n in FETCH[k]]
        lands = [jax.ShapeDtypeStruct((N_CHIPS,) + s.shape, s.dtype) for s in srcs]
        fetching.append(_copies_start("gather%d_start" % k, srcs, lands, 4 * len(srcs), _gather_copies, after))
        after = fetching[-1][4]

    def fetch(k, after_):
        return first if k == 0 else _copies_wait("gather%d_wait" % k, fetching[k - 1], _gather_copies, after_)[1]

    exchanging = []

    def on_grads(k, grads):
        others = _swap_halves("swap%d" % k, grads)
        parts = [_add_halves("add_" + n, g_, o_, c_idx) for n, g_, o_ in zip(DONE[k], grads, others)]
        lands = [jax.ShapeDtypeStruct((3,) + p_.shape[1:], p_.dtype) for p_ in parts]
        exchanging.append(_copies_start("exchange%d_start" % k, parts, lands, 3 * len(parts), _partial_copies, c_idx))
        return exchanging[-1][4][0, 0]

    S = {n: w[n] for n in SMALL}
    S["ffn1_norm"] = S["ffn1_norm"] + after[0, 0]
    loss, dx, dcw, gS = _local_step(x[0], p[0, 0], loss_target[0], fetch, S, on_grads)

    small = _pack_small([gS[n] for n in SMALL] + [dcw, loss[:, :1]])
    small_lands = [jax.ShapeDtypeStruct((N_DEV - 1,) + small.shape, small.dtype)]
    small_st = _copies_start("small_start", [small], small_lands, N_DEV - 1, _small_copies, c_idx)

    g, delta, new_m, new_v = {}, {}, {}, {}
    after = small_st[4]
    for k in range(len(DONE)):
        parts, recv = _copies_wait("exchange%d_wait" % k, exchanging[k], _partial_copies, after)
        mine = [_sum_partials("sum_" + n, p_, r_, chip_idx) for n, p_, r_ in zip(DONE[k], parts, recv)]
        theirs = _share_halves("share%d" % k, mine)
        for n, gm_, gt_ in zip(DONE[k], mine, theirs):
            flip = (lambda a: jnp.swapaxes(a, 0, 1)) if n in TRANSPOSED else (lambda a: a)
            outs = _adamw_big("adamw_" + n, flip(w[n][0]), gm_, gt_, flip(m[n][0]), flip(v[n][0]), c_idx)
            g[n], delta[n], new_m[n], new_v[n] = [flip(o)[None] for o in outs]
        after = new_v[DONE[k][-1]]
    (own,), (slots,) = _copies_wait("small_wait", small_st, _small_copies, after)
    dev_idx = (2 * chip + lax.axis_index("c")).astype(jnp.int32).reshape(1)
    small_shapes = [w[n].shape for n in SMALL] + [dcw.shape, (1, 1)]
    small_sum = _unpack_small(_sum_small(own, slots, dev_idx), small_shapes)
    g.update({n: small_sum[i] for i, n in enumerate(SMALL)})
    cshard = w["conv_w"].shape[2]
    g["conv_w"] = lax.dynamic_slice_in_dim(small_sum[len(SMALL)], chip * cshard, cshard, axis=1)[None]
    loss_total = small_sum[len(SMALL) + 1].reshape(())
    sm_names = SMALL + ("conv_w",)
    sm_shapes = [w[n].shape for n in sm_names]
    d_s, m_s, v_s = _adamw("adamw_small", _pack_small([w[n] for n in sm_names]), _pack_small([g[n] for n in sm_names]),
                           _pack_small([m[n] for n in sm_names]), _pack_small([v[n] for n in sm_names]))
    for dst, src in ((delta, d_s), (new_m, m_s), (new_v, v_s)):
        for n, val in zip(sm_names, _unpack_small(src, sm_shapes)):
            dst[n] = val

    return (loss_total, dx[None], *[g[n] for n in _WEIGHTS], *[delta[n] for n in _WEIGHTS],
            *[new_m[n] for n in _WEIGHTS], *[new_v[n] for n in _WEIGHTS])
```

```python
import functools

import jax
import jax.numpy as jnp
from jax import lax
from jax.experimental import pallas as pl
from jax.experimental.pallas import tpu as pltpu

f32 = jnp.float32
bf16 = jnp.bfloat16
MESH = pl.DeviceIdType.MESH
HIGHEST = lax.Precision.HIGHEST

EPS = 1e-6
N_CHIPS = 4
N_DEV = 8
D_MODEL = 1024
D_FF = 2816
D_PLE = 256
GM_WIDTH = 1024
GM_HEADS = 8
CHUNK = 128
SSM_WIDTH = 1024
SSM_HEADS = 16
SSM_HEAD_DIM = 64
SSM_GROUPS = 2
SSM_STATE = 128
SSM_CONV = 4
CONV_DIM = SSM_WIDTH + 2 * SSM_GROUPS * SSM_STATE
IN_PROJ = 2 * GM_WIDTH + SSM_WIDTH + CONV_DIM + SSM_HEADS
LANES = 128
ZXD = SSM_WIDTH + CONV_DIM + LANES

ADAM_LR = 0.001
ADAM_B1 = 0.9
ADAM_B2 = 0.999
ADAM_EPS = 1e-08
ADAM_WD = 0.01
ADAM_STEP = 10

VMEM_LIMIT = 56 * 1024 * 1024


def _dot(a, b):
    return jnp.dot(a, b, preferred_element_type=f32)


def _dot_nt(a, b):
    return lax.dot_general(a, b, (((1,), (1,)), ((), ())), preferred_element_type=f32)


def _dot_tn(a, b):
    return lax.dot_general(a, b, (((0,), (0,)), ((), ())), preferred_element_type=f32)


def _rms(x, g):
    return x * lax.rsqrt(jnp.mean(x * x, axis=-1, keepdims=True) + EPS) * g


def _gelu(x):
    return 0.5 * x * (1.0 + lax.erf(x * 0.7071067811865476))


def _layernorm(x, g, b):
    mu = jnp.mean(x, axis=-1, keepdims=True)
    xc = x - mu
    return xc * lax.rsqrt(jnp.mean(xc * xc, axis=-1, keepdims=True) + EPS) * g + b


def _sigmoid(x):
    return 1.0 / (1.0 + jnp.exp(-x))


def _softplus(x):
    return jnp.maximum(x, 0.0) + jnp.log(1.0 + jnp.exp(-jnp.abs(x)))


def _full(shape):
    nd = len(shape)
    return pl.BlockSpec(shape, lambda *_: (0,) * nd, pipeline_mode=pl.Buffered(1))


def _acc(shape):
    nd = len(shape)
    return pl.BlockSpec(shape, lambda *_: (0,) * nd)


def _rows(tm, ncols):
    return pl.BlockSpec((tm, ncols), lambda i: (i, 0))


def _params(sem):
    return pltpu.CompilerParams(dimension_semantics=sem, vmem_limit_bytes=VMEM_LIMIT)


def _row_tile(rows, target, mult=8):
    best = rows
    for t in range(mult, min(rows, target) + 1, mult):
        if rows % t == 0:
            best = t
    return best if best <= target else rows


def _stack_rows(k, tm, ncols):
    return pl.BlockSpec((k, tm, ncols), lambda i: (0, i, 0))


def _ffn_fwd(name, h, g, wg, wu, wd, pre=None, tm=256):
    T, D = h.shape
    K, _, Fs = wg.shape
    tm = min(tm, T)

    def body(*refs):
        if pre is None:
            h_ref, g_ref, wg_ref, wu_ref, wd_ref, ho_ref, n_ref, a_ref, b_ref = refs
            hin = h_ref[...]
        else:
            (h_ref, ya_ref, yb_ref, wo_ref, g_ref, wg_ref, wu_ref, wd_ref,
             hi_ref, ho_ref, n_ref, a_ref, b_ref) = refs
            ga = ya_ref.shape[1]
            hin = h_ref[...] + _dot(ya_ref[...], wo_ref[:ga, :]) + _dot(yb_ref[...], wo_ref[ga:, :])
            hi_ref[...] = hin
        n = _rms(hin, g_ref[...]).astype(bf16)
        n_ref[...] = n
        acc = jnp.zeros((tm, D), f32)
        for k in range(K):
            a = _dot(n, wg_ref[k]).astype(bf16)
            b = _dot(n, wu_ref[k]).astype(bf16)
            a_ref[k] = a
            b_ref[k] = b
            af = a.astype(f32)
            hm = (af * _sigmoid(af) * b.astype(f32)).astype(bf16)
            acc = acc + _dot(hm, wd_ref[k])
        ho_ref[...] = hin + 0.5 * acc

    ins = [h] + (list(pre) if pre is not None else []) + [g, wg, wu, wd]
    in_specs = [_rows(tm, D)]
    if pre is not None:
        in_specs += [_rows(tm, pre[0].shape[1]), _rows(tm, pre[1].shape[1]), _full(pre[2].shape)]
    in_specs += [_full(g.shape), _full(wg.shape), _full(wu.shape), _full(wd.shape)]
    outs = [jax.ShapeDtypeStruct((T, D), f32), jax.ShapeDtypeStruct((T, D), bf16),
            jax.ShapeDtypeStruct((K, T, Fs), bf16), jax.ShapeDtypeStruct((K, T, Fs), bf16)]
    out_specs = [_rows(tm, D), _rows(tm, D), _stack_rows(K, tm, Fs), _stack_rows(K, tm, Fs)]
    if pre is not None:
        outs = [jax.ShapeDtypeStruct((T, D), f32)] + outs
        out_specs = [_rows(tm, D)] + out_specs
    return pl.pallas_call(body, name=name, grid=(T // tm,), in_specs=in_specs, out_specs=out_specs,
                          out_shape=outs, compiler_params=_params(("parallel",)))(*ins)


def _ffn_bwd(name, dh, hin, g, a, b, wg, wu, wd, wo=None, ga=0, tm=256):
    T, D = dh.shape
    K, _, Fs = wg.shape
    tm = min(tm, T)

    def body(*refs):
        if wo is None:
            (dh_ref, hin_ref, g_ref, a_ref, b_ref, wg_ref, wu_ref, wd_ref,
             dhi_ref, da_ref, db_ref, hm_ref, dg_ref) = refs
        else:
            (dh_ref, hin_ref, g_ref, a_ref, b_ref, wg_ref, wu_ref, wd_ref, wo_ref,
             dhi_ref, da_ref, db_ref, hm_ref, dg_ref, dya_ref, dyb_ref) = refs

        @pl.when(pl.program_id(0) == 0)
        def _():
            dg_ref[...] = jnp.zeros_like(dg_ref)

        dh_ = dh_ref[...]
        dhb = (0.5 * dh_).astype(bf16)
        dn = jnp.zeros((tm, D), f32)
        for k in range(K):
            dhm = _dot_nt(dhb, wd_ref[k])
            af = a_ref[k].astype(f32)
            bf = b_ref[k].astype(f32)
            sg = _sigmoid(af)
            sl_ = af * sg
            da = (dhm * bf * (sg * (1.0 + af * (1.0 - sg)))).astype(bf16)
            db = (dhm * sl_).astype(bf16)
            da_ref[k] = da
            db_ref[k] = db
            hm_ref[k] = (sl_ * bf).astype(bf16)
            dn = dn + _dot_nt(da, wg_ref[k]) + _dot_nt(db, wu_ref[k])
        _, vjp = jax.vjp(_rms, hin_ref[...], g_ref[...])
        dx, dg = vjp(dn)
        dhi = dh_ + dx
        dhi_ref[...] = dhi
        dg_ref[...] += dg
        if wo is not None:
            dhib = dhi.astype(bf16)
            dya_ref[...] = _dot_nt(dhib, wo_ref[:ga, :]).astype(bf16)
            dyb_ref[...] = _dot_nt(dhib, wo_ref[ga:, :]).astype(bf16)

    ins = [dh, hin, g, a, b, wg, wu, wd]
    in_specs = [_rows(tm, D), _rows(tm, D), _full(g.shape), _stack_rows(K, tm, Fs), _stack_rows(K, tm, Fs),
                _full(wg.shape), _full(wu.shape), _full(wd.shape)]
    act = jax.ShapeDtypeStruct((K, T, Fs), bf16)
    outs = [jax.ShapeDtypeStruct((T, D), f32), act, act, act, jax.ShapeDtypeStruct(g.shape, f32)]
    out_specs = [_rows(tm, D), _stack_rows(K, tm, Fs), _stack_rows(K, tm, Fs), _stack_rows(K, tm, Fs), _acc(g.shape)]
    if wo is not None:
        gb = wo.shape[0] - ga
        ins += [wo]
        in_specs += [_full(wo.shape)]
        outs += [jax.ShapeDtypeStruct((T, ga), bf16), jax.ShapeDtypeStruct((T, gb), bf16)]
        out_specs += [_rows(tm, ga), _rows(tm, gb)]
    return pl.pallas_call(body, name=name, grid=(T // tm,), in_specs=in_specs, out_specs=out_specs,
                          out_shape=outs, compiler_params=_params(("arbitrary",)))(*ins)


def _matmul_tn(name, a, b, scale=1.0, tk=1024):
    ka = a.shape[0] if a.ndim == 3 else 0
    kb = b.shape[0] if b.ndim == 3 else 0
    K = max(ka, kb)
    T, M = a.shape[-2:]
    N = b.shape[-1]
    tk = min(tk, T)
    nk = T // tk
    if K:
        tn, nj = N, K
    else:
        tn = LANES * max(d for d in range(1, N // LANES + 1) if (N // LANES) % d == 0 and (d == 1 or M * d * LANES * 4 <= 6 * 1024 * 1024))
        nj = N // tn

    def body(a_ref, b_ref, o_ref, acc):
        k = pl.program_id(1)

        @pl.when(k == 0)
        def _():
            acc[...] = jnp.zeros_like(acc)

        bb = b_ref[...]
        if scale != 1.0:
            bb = bb * scale
        acc[...] += _dot_tn(a_ref[...].astype(bf16), bb.astype(bf16))

        @pl.when(k == nk - 1)
        def _():
            o_ref[...] = acc[...].astype(bf16)

    a_spec = pl.BlockSpec((None, tk, M), lambda j, k: (j, k, 0)) if ka else pl.BlockSpec((tk, M), lambda j, k: (k, 0))
    if kb:
        b_spec = pl.BlockSpec((None, tk, N), lambda j, k: (j, k, 0))
    elif K:
        b_spec = pl.BlockSpec((tk, N), lambda j, k: (k, 0))
    else:
        b_spec = pl.BlockSpec((tk, tn), lambda j, k: (k, j))
    if K:
        o_spec, o_shape = pl.BlockSpec((None, M, N), lambda j, k: (j, 0, 0)), (K, M, N)
    else:
        o_spec, o_shape = pl.BlockSpec((M, tn), lambda j, k: (0, j)), (M, N)
    return pl.pallas_call(
        body, name=name, grid=(nj, nk), in_specs=[a_spec, b_spec], out_specs=o_spec,
        out_shape=jax.ShapeDtypeStruct(o_shape, bf16), scratch_shapes=[pltpu.VMEM((M, tn), f32)],
        compiler_params=_params(("parallel", "arbitrary")))(a, b)


def _gm_pre(u, v, ln_g, ln_b):
    return _gelu(u), _layernorm(_gelu(v), ln_g, ln_b)


def _tril_mask():
    r = lax.broadcasted_iota(jnp.int32, (CHUNK, CHUNK), 0)
    c = lax.broadcasted_iota(jnp.int32, (CHUNK, CHUNK), 1)
    return c <= r


def _gm_mix(vnb, ws_ref, bst, mixed_sc, tm):
    mask = _tril_mask()
    for h in range(GM_HEADS):
        wt = jnp.where(mask, ws_ref[h], 0.0).astype(bf16)
        bias = bst[:, h:h + 1]
        for q in range(tm // CHUNK):
            rs = slice(q * CHUNK, (q + 1) * CHUNK)
            cs = slice(h * CHUNK, (h + 1) * CHUNK)
            mixed_sc[rs, cs] = _dot(wt, vnb[rs, cs]) + bias


def _mix_fwd(h1, gmix, w_uv, w_zxd, ln_g, ln_b, w_s, b_st, gout, tm=512):
    T, D = h1.shape
    tm = min(tm, T)
    G = GM_WIDTH

    def body(h_ref, g_ref, wuv_ref, wzxd_ref, lng_ref, lnb_ref, ws_ref, bst_ref, gout_ref,
             n_ref, uv_ref, z_ref, xbc_ref, dt_ref, ya_ref, mixed_sc):
        n = _rms(h_ref[...], g_ref[...]).astype(bf16)
        n_ref[...] = n
        u = _dot(n, wuv_ref[:, :G]).astype(bf16)
        v = _dot(n, wuv_ref[:, G:]).astype(bf16)
        uv_ref[:, :G] = u
        uv_ref[:, G:] = v
        z_ref[...] = _dot(n, wzxd_ref[:, :SSM_WIDTH]).astype(bf16)
        xbc_ref[...] = _dot(n, wzxd_ref[:, SSM_WIDTH:SSM_WIDTH + CONV_DIM]).astype(bf16)
        dt_ref[...] = _dot(n, wzxd_ref[:, SSM_WIDTH + CONV_DIM:])
        ug, vn = _gm_pre(u.astype(f32), v.astype(f32), lng_ref[...], lnb_ref[...])
        _gm_mix(vn.astype(bf16), ws_ref, bst_ref[...], mixed_sc, tm)
        ya_ref[...] = _rms(ug * mixed_sc[...], gout_ref[...]).astype(bf16)

    ins = [h1, gmix, w_uv, w_zxd, ln_g, ln_b, w_s, b_st, gout]
    in_specs = [_rows(tm, D)] + [_full(x.shape) for x in ins[1:]]
    outs = [jax.ShapeDtypeStruct((T, D), bf16), jax.ShapeDtypeStruct((T, 2 * G), bf16),
            jax.ShapeDtypeStruct((T, SSM_WIDTH), bf16), jax.ShapeDtypeStruct((T, CONV_DIM), bf16),
            jax.ShapeDtypeStruct((T, LANES), f32), jax.ShapeDtypeStruct((T, G), bf16)]
    out_specs = [_rows(tm, D), _rows(tm, 2 * G), _rows(tm, SSM_WIDTH), _rows(tm, CONV_DIM), _rows(tm, LANES), _rows(tm, G)]
    return pl.pallas_call(body, name="mix_fwd", grid=(T // tm,), in_specs=in_specs, out_specs=out_specs,
                          out_shape=outs, scratch_shapes=[pltpu.VMEM((tm, G), f32)],
                          compiler_params=_params(("parallel",)))(*ins)


def _gm_bwd(uv, dya, ln_g, ln_b, w_s, b_st, gout, tm=256):
    T = uv.shape[0]
    tm = min(tm, T)
    G = GM_WIDTH

    def body(uv_ref, dya_ref, lng_ref, lnb_ref, ws_ref, bst_ref, gout_ref,
             duv_ref, dlng_ref, dlnb_ref, dws_ref, dbst_ref, dgout_ref, mixed_sc, dvn_sc):
        @pl.when(pl.program_id(0) == 0)
        def _():
            for r in (dlng_ref, dlnb_ref, dws_ref, dbst_ref, dgout_ref):
                r[...] = jnp.zeros_like(r)

        u = uv_ref[:, :G].astype(f32)
        v = uv_ref[:, G:].astype(f32)
        (ug, vn), pre_vjp = jax.vjp(_gm_pre, u, v, lng_ref[...], lnb_ref[...])
        vnb = vn.astype(bf16)
        _gm_mix(vnb, ws_ref, bst_ref[...], mixed_sc, tm)
        mixed = mixed_sc[...]
        _, out_vjp = jax.vjp(_rms, ug * mixed, gout_ref[...])
        dpre, dgout = out_vjp(dya_ref[...].astype(f32))
        dgout_ref[...] += dgout
        dug = dpre * mixed
        dmixed = dpre * ug
        mask = _tril_mask()
        lane = lax.broadcasted_iota(jnp.int32, (1, GM_HEADS), 1)
        dbst = jnp.zeros((CHUNK, GM_HEADS), f32)
        for h in range(GM_HEADS):
            wt = jnp.where(mask, ws_ref[h], 0.0).astype(bf16)
            cs = slice(h * CHUNK, (h + 1) * CHUNK)
            dw = jnp.zeros((CHUNK, CHUNK), f32)
            for q in range(tm // CHUNK):
                rs = slice(q * CHUNK, (q + 1) * CHUNK)
                dm = dmixed[rs, cs]
                dmb = dm.astype(bf16)
                dw = dw + _dot_nt(dmb, vnb[rs, cs])
                dbst = dbst + jnp.sum(dm, axis=1, keepdims=True) * (lane == h).astype(f32)
                dvn_sc[rs, cs] = _dot_tn(wt, dmb)
            dws_ref[h] += jnp.where(mask, dw, 0.0)
        dbst_ref[...] += dbst
        du, dv, dlng, dlnb = pre_vjp((dug, dvn_sc[...]))
        duv_ref[:, :G] = du.astype(bf16)
        duv_ref[:, G:] = dv.astype(bf16)
        dlng_ref[...] += dlng
        dlnb_ref[...] += dlnb

    ins = [uv, dya, ln_g, ln_b, w_s, b_st, gout]
    in_specs = [_rows(tm, 2 * G), _rows(tm, G)] + [_full(x.shape) for x in ins[2:]]
    outs = [jax.ShapeDtypeStruct((T, 2 * G), bf16)] + [jax.ShapeDtypeStruct(x.shape, f32) for x in (ln_g, ln_b, w_s, b_st, gout)]
    out_specs = [_rows(tm, 2 * G)] + [_acc(x.shape) for x in (ln_g, ln_b, w_s, b_st, gout)]
    return pl.pallas_call(body, name="gm_bwd", grid=(T // tm,), in_specs=in_specs, out_specs=out_specs,
                          out_shape=outs, scratch_shapes=[pltpu.VMEM((tm, G), f32), pltpu.VMEM((tm, G), f32)],
                          compiler_params=_params(("arbitrary",)))(*ins)


def _mix_bwd(dh, h1, gmix, duv, dzxd, w_uv, w_zxd, tm=512):
    T, D = dh.shape
    tm = min(tm, T)

    def body(dh_ref, h_ref, g_ref, duv_ref, dzxd_ref, wuv_ref, wzxd_ref, dhi_ref, dg_ref):
        @pl.when(pl.program_id(0) == 0)
        def _():
            dg_ref[...] = jnp.zeros_like(dg_ref)

        dn = _dot_nt(duv_ref[...], wuv_ref[...]) + _dot_nt(dzxd_ref[...], wzxd_ref[...])
        _, vjp = jax.vjp(_rms, h_ref[...], g_ref[...])
        dx, dg = vjp(dn)
        dhi_ref[...] = dh_ref[...] + dx
        dg_ref[...] += dg

    ins = [dh, h1, gmix, duv, dzxd, w_uv, w_zxd]
    in_specs = [_rows(tm, D), _rows(tm, D), _full(gmix.shape), _rows(tm, duv.shape[1]), _rows(tm, dzxd.shape[1]),
                _full(w_uv.shape), _full(w_zxd.shape)]
    return pl.pallas_call(body, name="mix_bwd", grid=(T // tm,), in_specs=in_specs,
                          out_specs=[_rows(tm, D), _acc(gmix.shape)],
                          out_shape=[jax.ShapeDtypeStruct((T, D), f32), jax.ShapeDtypeStruct(gmix.shape, f32)],
                          compiler_params=_params(("arbitrary",)))(*ins)


HALO = 16
PAIRS = SSM_HEADS // 2
PAIR_W = 2 * SSM_HEAD_DIM


def _split(x, n):
    parts = []
    for _ in range(n):
        p = x.astype(bf16)
        parts.append(p)
        x = x - p.astype(f32)
    return parts


def _dot_sel(x, sel_n, n):
    return _dot(jnp.concatenate(_split(x, n), axis=1), sel_n)


def _sel_dot(sel, x, n):
    return _dot(jnp.concatenate([sel] * n, axis=1), jnp.concatenate(_split(x, n), axis=0))


EXPAND_SPLIT = 3
REDUCE_SPLIT = 2


def _head_mats():
    ex = (jnp.arange(SSM_WIDTH)[None, :] // SSM_HEAD_DIM == jnp.arange(LANES)[:, None]).astype(bf16)
    return jnp.tile(ex, (EXPAND_SPLIT, 1)), jnp.tile(ex.T, (REDUCE_SPLIT, 1))


def _shift_mat(rows, cols, off):
    r = lax.broadcasted_iota(jnp.int32, (rows, cols), 0)
    c = lax.broadcasted_iota(jnp.int32, (rows, cols), 1)
    return (c == r + off).astype(bf16)


def _ssd_front(c, xbc_ref, halo_ref, dtr_ref, cw_ref, cb_ref, dtb_ref, alog_ref):
    halo = halo_ref[...]
    ext = jnp.concatenate([jnp.where(c > 0, halo, jnp.zeros_like(halo)), xbc_ref[...]], axis=0)
    taps = [_dot(_shift_mat(CHUNK, HALO + CHUNK, HALO - SSM_CONV + 1 + j), ext) for j in range(SSM_CONV - 1)]
    taps.append(xbc_ref[...].astype(f32))
    xc = cb_ref[...] + cw_ref[0:1, :] * taps[0]
    for j in range(1, SSM_CONV):
        xc = xc + cw_ref[j:j + 1, :] * taps[j]
    sg = _sigmoid(xc)
    xa = xc * sg
    dt = _softplus(dtr_ref[...] + dtb_ref[...])
    a = -jnp.exp(alog_ref[...])
    acs = jnp.dot(_tril_mask().astype(f32), dt * a, preferred_element_type=f32, precision=HIGHEST)
    return taps, xc, sg, xa, dt, a, acs


def _ssd_wide(xa, dt, acs, dsk, ex):
    dt_x = _dot_sel(dt, ex, EXPAND_SPLIT)
    acs_x = _dot_sel(acs, ex, EXPAND_SPLIT)
    dsk_x = _dot_sel(jnp.broadcast_to(dsk, (8, LANES)), ex, EXPAND_SPLIT)[0:1]
    e_x = jnp.exp(acs_x)
    r_x = jnp.exp(acs_x[CHUNK - 1:CHUNK, :] - acs_x)
    xs = xa[:, :SSM_WIDTH]
    xd = xs * dt_x
    return dt_x, dsk_x, e_x, r_x, xs, xd, xd * r_x


def _pair_stack(v, lo):
    return jnp.concatenate([jnp.where(lo, v, 0.0), jnp.where(lo, 0.0, v)], axis=0)


def _ssd_pair(j, acs, acs_t, cb):
    out = []
    tril = _tril_mask()
    for h in (2 * j, 2 * j + 1):
        dk = jnp.exp(jnp.where(tril, acs[:, h:h + 1] - acs_t[h:h + 1, :], -jnp.inf))
        out.append((dk, cb * dk))
    return out


def _pair_col(row_lo, tot, j):
    return jnp.exp(jnp.where(row_lo, tot[:, 2 * j:2 * j + 1], tot[:, 2 * j + 1:2 * j + 2]))


def _gated_norm(y, z, g):
    yg = y * (z * _sigmoid(z))
    half = SSM_WIDTH // SSM_GROUPS
    parts = []
    for k in range(SSM_GROUPS):
        s = yg[:, k * half:(k + 1) * half]
        parts.append(s * lax.rsqrt(jnp.mean(s * s, axis=-1, keepdims=True) + EPS))
    return jnp.concatenate(parts, axis=1) * g


def _group_mats(xa):
    out = []
    for g in range(SSM_GROUPS):
        bm = xa[:, SSM_WIDTH + g * SSM_STATE:SSM_WIDTH + (g + 1) * SSM_STATE].astype(bf16)
        cm = xa[:, SSM_WIDTH + (SSM_GROUPS + g) * SSM_STATE:SSM_WIDTH + (SSM_GROUPS + g + 1) * SSM_STATE].astype(bf16)
        out.append((cm, bm, _dot_nt(cm, bm)))
    return out


def _ssd_in_specs(nc, rev):
    def ci(i):
        return nc - 1 - i if rev else i
    hp = CHUNK // HALO
    return [pl.BlockSpec((CHUNK, CONV_DIM), lambda i: (ci(i), 0)),
            pl.BlockSpec((HALO, CONV_DIM), lambda i: (jnp.maximum(ci(i) * hp - 1, 0), 0)),
            pl.BlockSpec((CHUNK, SSM_WIDTH), lambda i: (ci(i), 0)),
            pl.BlockSpec((CHUNK, LANES), lambda i: (ci(i), 0))]


def _ssd_fwd(xbc, z, dtr, conv_w, conv_b, dt_bias, a_log, d_skip, ssm_norm):
    T = xbc.shape[0]
    nc = T // CHUNK
    N = SSM_STATE

    def body(xbc_ref, halo_ref, z_ref, dtr_ref, cw_ref, cb_ref, dtb_ref, alog_ref, dsk_ref, g_ref, ex_ref,
             yb_ref, sprev_ref, s_sc):
        c = pl.program_id(0)

        @pl.when(c == 0)
        def _():
            s_sc[...] = jnp.zeros_like(s_sc)

        _, _, _, xa, dt, _, acs = _ssd_front(c, xbc_ref, halo_ref, dtr_ref, cw_ref, cb_ref, dtb_ref, alog_ref)
        _, dsk_x, e_x, _, xs, xd, gm = _ssd_wide(xa, dt, acs, dsk_ref[...], ex_ref[...])
        acs_t = acs.T
        tot = acs[CHUNK - 1:CHUNK, :]
        groups = _group_mats(xa)
        lo = lax.broadcasted_iota(jnp.int32, (CHUNK, PAIR_W), 1) < SSM_HEAD_DIM
        row_lo = lax.broadcasted_iota(jnp.int32, (PAIR_W, 1), 0) < SSM_HEAD_DIM
        ys = []
        for j in range(PAIRS):
            cmb, bmb, cb = groups[j // (PAIRS // SSM_GROUPS)]
            ps = slice(j * PAIR_W, (j + 1) * PAIR_W)
            (_, m0), (_, m1) = _ssd_pair(j, acs, acs_t, cb)
            sp = s_sc[j]
            yd = _dot(jnp.concatenate([m0, m1], axis=1).astype(bf16), _pair_stack(xd[:, ps], lo).astype(bf16))
            ys.append(yd + e_x[:, ps] * _dot_nt(cmb, sp.astype(bf16)))
            sprev_ref[0, j] = sp
            s_sc[j] = _pair_col(row_lo, tot, j) * sp + _dot_tn(gm[:, ps].astype(bf16), bmb)
        y = jnp.concatenate(ys, axis=1) + xs * dsk_x
        yb_ref[...] = _gated_norm(y, z_ref[...].astype(f32), g_ref[...]).astype(bf16)

    params = [conv_w, conv_b, dt_bias, a_log, d_skip, ssm_norm, _head_mats()[0]]
    return pl.pallas_call(
        body, name="ssd_fwd", grid=(nc,),
        in_specs=_ssd_in_specs(nc, False) + [_full(x.shape) for x in params],
        out_specs=[pl.BlockSpec((CHUNK, SSM_WIDTH), lambda i: (i, 0)), pl.BlockSpec((1, PAIRS, PAIR_W, N), lambda i: (i, 0, 0, 0))],
        out_shape=[jax.ShapeDtypeStruct((T, SSM_WIDTH), bf16), jax.ShapeDtypeStruct((nc, PAIRS, PAIR_W, N), f32)],
        scratch_shapes=[pltpu.VMEM((PAIRS, PAIR_W, N), f32)],
        compiler_params=_params(("arbitrary",)))(xbc, xbc, z, dtr, *params)


def _ssd_bwd(xbc, z, dtr, sprev, dyb, conv_w, conv_b, dt_bias, a_log, d_skip, ssm_norm):
    T = xbc.shape[0]
    nc = T // CHUNK
    H, N = SSM_HEADS, SSM_STATE
    PG = PAIRS // SSM_GROUPS

    def body(xbc_ref, halo_ref, z_ref, dtr_ref, sprev_ref, dyb_ref, cw_ref, cb_ref, dtb_ref, alog_ref, dsk_ref, g_ref,
             ex_ref, rd_ref, dzxd_ref, dcw_ref, dcb_ref, ddtb_ref, dalog_ref, ddsk_ref, dg_ref, ds_sc, next_sc):
        i = pl.program_id(0)
        c = nc - 1 - i

        @pl.when(i == 0)
        def _():
            ds_sc[...] = jnp.zeros_like(ds_sc)
            next_sc[...] = jnp.zeros_like(next_sc)
            for r_ in (dcw_ref, dcb_ref, ddtb_ref, dalog_ref, ddsk_ref, dg_ref):
                r_[...] = jnp.zeros_like(r_)

        taps, xc, sg, xa, dt, a, acs = _ssd_front(c, xbc_ref, halo_ref, dtr_ref, cw_ref, cb_ref, dtb_ref, alog_ref)
        dt_x, dsk_x, e_x, r_x, xs, xd, gm = _ssd_wide(xa, dt, acs, dsk_ref[...], ex_ref[...])
        acs_t = acs.T
        tot = acs[CHUNK - 1:CHUNK, :]
        groups = _group_mats(xa)
        lo = lax.broadcasted_iota(jnp.int32, (CHUNK, PAIR_W), 1) < SSM_HEAD_DIM
        row_lo = lax.broadcasted_iota(jnp.int32, (PAIR_W, 1), 0) < SSM_HEAD_DIM
        pairs, zs, yds = [], [], []
        for j in range(PAIRS):
            cmb, _, cb = groups[j // PG]
            ps = slice(j * PAIR_W, (j + 1) * PAIR_W)
            pairs.append(_ssd_pair(j, acs, acs_t, cb))
            (_, m0), (_, m1) = pairs[j]
            zs.append(_dot_nt(cmb, sprev_ref[0, j].astype(bf16)))
            yds.append(_dot(jnp.concatenate([m0, m1], axis=1).astype(bf16), _pair_stack(xd[:, ps], lo).astype(bf16)))
        zf = jnp.concatenate(zs, axis=1)
        y = jnp.concatenate(yds, axis=1) + e_x * zf + xs * dsk_x
        _, gn_vjp = jax.vjp(_gated_norm, y, z_ref[...].astype(f32), g_ref[...])
        dy, dz, dg = gn_vjp(dyb_ref[...].astype(f32))
        dg_ref[...] += dg
        dzxd_ref[:, :SSM_WIDTH] = dz.astype(bf16)

        lane = lax.broadcasted_iota(jnp.int32, (1, LANES), 1)
        sub = lax.broadcasted_iota(jnp.int32, (LANES, 1), 0)
        dacs = jnp.zeros((CHUNK, LANES), f32)
        dacs_r = jnp.zeros((LANES, CHUNK), f32)
        dtot = jnp.zeros((1, LANES), f32)
        dcb = [jnp.zeros((CHUNK, CHUNK), f32) for _ in range(SSM_GROUPS)]
        dcm = [jnp.zeros((CHUNK, N), f32) for _ in range(SSM_GROUPS)]
        dbm = [jnp.zeros((CHUNK, N), f32) for _ in range(SSM_GROUPS)]
        dxds, dgms = [], []
        for j in range(PAIRS):
            g = j // PG
            cmb, bmb, _ = groups[g]
            ps = slice(j * PAIR_W, (j + 1) * PAIR_W)
            (dk0, m0), (dk1, m1) = pairs[j]
            oh0, oh1 = (lane == 2 * j).astype(f32), (lane == 2 * j + 1).astype(f32)
            dyp = dy[:, ps]
            dy2 = _pair_stack(dyp, lo).astype(bf16)
            dm2 = _dot_nt(dy2, xd[:, ps].astype(bf16))
            m2 = jnp.concatenate([m0, m1], axis=0)
            dxds.append(_dot_tn(m2.astype(bf16), dy2))
            w2 = dm2 * m2
            rs = jnp.sum(w2, axis=1, keepdims=True)
            dacs = dacs + rs[:CHUNK] * oh0 + rs[CHUNK:] * oh1
            dacs_r = dacs_r - ((sub == 2 * j).astype(f32) * jnp.sum(w2[:CHUNK], axis=0, keepdims=True)
                               + (sub == 2 * j + 1).astype(f32) * jnp.sum(w2[CHUNK:], axis=0, keepdims=True))
            dcb[g] = dcb[g] + dm2[:CHUNK] * dk0 + dm2[CHUNK:] * dk1
            sp = sprev_ref[0, j]
            dzb = (dyp * e_x[:, ps]).astype(bf16)
            dcm[g] = dcm[g] + _dot(dzb, sp.astype(bf16))
            dsn = ds_sc[j]
            dsnb = dsn.astype(bf16)
            et = _pair_col(row_lo, tot, j)
            rr = jnp.sum(dsn * sp, axis=1, keepdims=True) * et
            dtot = dtot + jnp.sum(rr[:SSM_HEAD_DIM]) * oh0 + jnp.sum(rr[SSM_HEAD_DIM:]) * oh1
            dgms.append(_dot_nt(bmb, dsnb))
            dbm[g] = dbm[g] + _dot(gm[:, ps].astype(bf16), dsnb)
            ds_sc[j] = _dot_tn(dzb, cmb) + et * dsn
        dgm = jnp.concatenate(dgms, axis=1)
        dxd = jnp.concatenate(dxds, axis=1) + dgm * r_x
        dr = dgm * gm
        red = _dot_sel(jnp.concatenate([dy * e_x * zf - dr, dr, dxd * xs, dy * xs], axis=0), rd_ref[...], REDUCE_SPLIT)
        rowi = lax.broadcasted_iota(jnp.int32, (CHUNK, 1), 0)
        dtot = dtot + jnp.sum(red[CHUNK:2 * CHUNK], axis=0, keepdims=True)
        dacs = dacs + red[:CHUNK] + dacs_r.T + jnp.where(rowi == CHUNK - 1, dtot, 0.0)
        r2 = lax.broadcasted_iota(jnp.int32, (CHUNK, CHUNK), 0)
        c2 = lax.broadcasted_iota(jnp.int32, (CHUNK, CHUNK), 1)
        dadt = jnp.dot((c2 >= r2).astype(f32), dacs, preferred_element_type=f32, precision=HIGHEST)
        ddt = red[2 * CHUNK:3 * CHUNK] + dadt * a
        dalog_ref[...] += jnp.sum(dadt * dt, axis=0, keepdims=True) * a
        ddsk_ref[...] += jnp.sum(red[3 * CHUNK:], axis=0, keepdims=True)
        ddtr = jnp.where(lane < H, ddt * _sigmoid(dtr_ref[...] + dtb_ref[...]), 0.0)
        ddtb_ref[...] += jnp.sum(ddtr, axis=0, keepdims=True)
        dzxd_ref[:, SSM_WIDTH + CONV_DIM:] = ddtr.astype(bf16)
        dxa_bm, dxa_cm = [], []
        for g in range(SSM_GROUPS):
            cmb, bmb, _ = groups[g]
            dcbb = dcb[g].astype(bf16)
            dxa_bm.append(dbm[g] + _dot_tn(dcbb, cmb))
            dxa_cm.append(dcm[g] + _dot(dcbb, bmb))
        dxc = jnp.concatenate([dy * dsk_x + dxd * dt_x] + dxa_bm + dxa_cm, axis=1) * (sg * (1.0 + xc * (1.0 - sg)))
        ext = jnp.concatenate([dxc, next_sc[...]], axis=0)
        dxbc = cw_ref[SSM_CONV - 1:SSM_CONV, :] * dxc
        for s in range(1, SSM_CONV):
            dxbc = dxbc + cw_ref[SSM_CONV - 1 - s:SSM_CONV - s, :] * _sel_dot(_shift_mat(CHUNK, CHUNK + HALO, s), ext, 2)
        dzxd_ref[:, SSM_WIDTH:SSM_WIDTH + CONV_DIM] = dxbc.astype(bf16)
        dcw_ref[...] += jnp.concatenate([jnp.sum(dxc * t, axis=0, keepdims=True) for t in taps], axis=0)
        dcb_ref[...] += jnp.sum(dxc, axis=0, keepdims=True)
        next_sc[...] = dxc[0:HALO, :]

    params = [conv_w, conv_b, dt_bias, a_log, d_skip, ssm_norm]
    mats = list(_head_mats())

    def rc(i):
        return nc - 1 - i

    in_specs = (_ssd_in_specs(nc, True)
                + [pl.BlockSpec((1, PAIRS, PAIR_W, N), lambda i: (rc(i), 0, 0, 0)), pl.BlockSpec((CHUNK, SSM_WIDTH), lambda i: (rc(i), 0))]
                + [_full(x.shape) for x in params + mats])
    return pl.pallas_call(
        body, name="ssd_bwd", grid=(nc,), in_specs=in_specs,
        out_specs=[pl.BlockSpec((CHUNK, ZXD), lambda i: (rc(i), 0))] + [_acc(x.shape) for x in params],
        out_shape=[jax.ShapeDtypeStruct((T, ZXD), bf16)] + [jax.ShapeDtypeStruct(x.shape, f32) for x in params],
        scratch_shapes=[pltpu.VMEM((PAIRS, PAIR_W, N), f32), pltpu.VMEM((HALO, CONV_DIM), f32)],
        compiler_params=_params(("arbitrary",)))(xbc, xbc, z, dtr, sprev, dyb, *params, *mats)


def _tail(h3, p, tgt, gp, wpg, bpg, wpp, gf, tm=512):
    T, D = h3.shape
    tm = min(tm, T)

    def head(gpre, pp, h, gf_, t):
        gate = _sigmoid(gpre)
        y = _rms(h + gate * pp, gf_)
        err = y - t
        return 0.5 * jnp.sum(jnp.mean(err * err, axis=-1))

    def body(h_ref, p_ref, t_ref, gp_ref, wpg_ref, bpg_ref, wpp_ref, gf_ref,
             dh_ref, loss_ref, dgp_ref, dwpg_ref, dbpg_ref, dwpp_ref, dgf_ref):
        @pl.when(pl.program_id(0) == 0)
        def _():
            for r in (loss_ref, dgp_ref, dwpg_ref, dbpg_ref, dwpp_ref, dgf_ref):
                r[...] = jnp.zeros_like(r)

        h = h_ref[...]
        npf, np_vjp = jax.vjp(_rms, h, gp_ref[...])
        npb = npf.astype(bf16)
        pb = p_ref[...].astype(bf16)
        gpre = _dot(npb, wpg_ref[...]) + bpg_ref[...]
        kp, _, cp = wpp_ref.shape
        pp = jnp.concatenate([_dot(pb, wpp_ref[k]) for k in range(kp)], axis=1)
        loss, head_vjp = jax.vjp(head, gpre, pp, h, gf_ref[...], t_ref[...])
        dgpre, dpp, dh_a, dgf, _ = head_vjp(jnp.ones((), f32))
        loss_ref[...] += loss
        dgf_ref[...] += dgf
        dbpg_ref[...] += jnp.sum(dgpre, axis=0, keepdims=True)
        dgb = dgpre.astype(bf16)
        dwpg_ref[...] += _dot_tn(npb, dgb)
        dppb = dpp.astype(bf16)
        for k in range(kp):
            dwpp_ref[k] += _dot_tn(pb, dppb[:, k * cp:(k + 1) * cp])
        dh_b, dgp = np_vjp(_dot_nt(dgb, wpg_ref[...]))
        dgp_ref[...] += dgp
        dh_ref[...] = dh_a + dh_b

    ins = [h3, p, tgt, gp, wpg, bpg, wpp, gf]
    in_specs = [_rows(tm, D), _rows(tm, p.shape[1]), _rows(tm, D)] + [_full(x.shape) for x in ins[3:]]
    acc_shapes = [(1, LANES), gp.shape, wpg.shape, bpg.shape, wpp.shape, gf.shape]
    return pl.pallas_call(
        body, name="tail", grid=(T // tm,), in_specs=in_specs,
        out_specs=[_rows(tm, D)] + [_acc(s) for s in acc_shapes],
        out_shape=[jax.ShapeDtypeStruct((T, D), f32)] + [jax.ShapeDtypeStruct(s, f32) for s in acc_shapes],
        compiler_params=_params(("arbitrary",)))(*ins)


def _adamw(name, w, g, m, v, tr=256):
    R, C = w.shape
    tr = _row_tile(R, tr)

    def body(w_ref, g_ref, m_ref, v_ref, d_ref, mo_ref, vo_ref):
        g_ = g_ref[...]
        m_ = ADAM_B1 * m_ref[...] + (1.0 - ADAM_B1) * g_
        v_ = ADAM_B2 * v_ref[...] + (1.0 - ADAM_B2) * jnp.square(g_)
        m_hat = m_ / (1.0 - ADAM_B1 ** ADAM_STEP)
        v_hat = v_ / (1.0 - ADAM_B2 ** ADAM_STEP)
        d_ref[...] = -ADAM_LR * (m_hat / (jnp.sqrt(v_hat) + ADAM_EPS) + ADAM_WD * w_ref[...])
        mo_ref[...] = m_
        vo_ref[...] = v_

    spec = pl.BlockSpec((tr, C), lambda i: (i, 0))
    return pl.pallas_call(body, name=name, grid=(R // tr,), in_specs=[spec] * 4, out_specs=[spec] * 3,
                          out_shape=[jax.ShapeDtypeStruct((R, C), f32)] * 3,
                          compiler_params=_params(("parallel",)))(w, g, m, v)


HBM = pl.BlockSpec(memory_space=pltpu.HBM)


def _me():
    return lax.axis_index("x"), lax.axis_index("y"), lax.axis_index("c")


def _other_chips(x, y):
    return [(1 - x, y), (x, 1 - y), (1 - x, 1 - y)]


def _remote(src, dst, send_sem, recv_sem, dev):
    return pltpu.make_async_remote_copy(src_ref=src, dst_ref=dst, send_sem=send_sem, recv_sem=recv_sem,
                                        device_id=dev, device_id_type=MESH)


def _sems(n):
    return [pltpu.SemaphoreType.DMA((n,)), pltpu.SemaphoreType.DMA((n,))]


def _gather_weights(shards, split):
    n = len(shards)

    def body(*refs):
        ins, outs = refs[:n], refs[n:2 * n]
        own_send, own_recv, ici_send, ici_recv, d2d_send, d2d_recv = refs[2 * n:]
        x, y, c = _me()
        my_chip = 2 * x + y
        sibling = (x, y, 1 - c)
        chips = _other_chips(x, y)

        def rows(i, half):
            hr = shards[i].shape[0] // 2
            return pl.ds(half * hr, hr) if split[i] else pl.ds(0, shards[i].shape[0])

        sends = []
        for i in range(n):
            for j, chip in enumerate(chips):
                cp = _remote(ins[i].at[rows(i, c)], outs[i].at[my_chip, rows(i, c)],
                             ici_send.at[3 * i + j], ici_recv.at[3 * i + j], (*chip, c))
                cp.start()
                sends.append(cp)
            cp = _remote(ins[i], outs[i].at[my_chip], own_send.at[i], own_recv.at[i], sibling)
            cp.start()
            sends.append(cp)
        for i in range(n):
            for j, chip in enumerate(chips):
                s = 3 * i + j
                land = outs[i].at[2 * chip[0] + chip[1], rows(i, c)]
                _remote(land, land, ici_send.at[s], ici_recv.at[s], (*chip, c)).wait_recv()
                if split[i]:
                    cp = _remote(land, land, d2d_send.at[s], d2d_recv.at[s], sibling)
                    cp.start()
                    sends.append(cp)
        for i in range(n):
            _remote(ins[i], outs[i].at[my_chip], own_send.at[i], own_recv.at[i], sibling).wait_recv()
            if split[i]:
                for j, chip in enumerate(chips):
                    s = 3 * i + j
                    land = outs[i].at[2 * chip[0] + chip[1], rows(i, 1 - c)]
                    _remote(land, land, d2d_send.at[s], d2d_recv.at[s], sibling).wait_recv()
        for cp in sends:
            cp.wait_send()

    return pl.pallas_call(
        body, name="gather_weights", out_shape=[jax.ShapeDtypeStruct((N_CHIPS,) + s.shape, s.dtype) for s in shards],
        in_specs=[HBM] * n, out_specs=[HBM] * n,
        scratch_shapes=_sems(n) + _sems(3 * n) + _sems(3 * n))(*shards)


def _swap_halves(name, grads):
    n = len(grads)

    def body(*refs):
        ins, outs, send, recv = refs[:n], refs[n:2 * n], refs[2 * n], refs[2 * n + 1]
        x, y, c = _me()
        copies = []
        for i in range(n):
            hr = grads[i].shape[1] // 2
            cp = _remote(ins[i].at[:, pl.ds((1 - c) * hr, hr), :], outs[i], send.at[i], recv.at[i], (x, y, 1 - c))
            cp.start()
            copies.append(cp)
        for cp in copies:
            cp.wait()

    return pl.pallas_call(
        body, name=name,
        out_shape=[jax.ShapeDtypeStruct((g.shape[0], g.shape[1] // 2, g.shape[2]), g.dtype) for g in grads],
        in_specs=[HBM] * n, out_specs=[HBM] * n, scratch_shapes=_sems(n))(*grads)


def _add_halves(name, grads, other, c_idx, th=256):
    K, R, C = grads.shape
    H = R // 2
    th = _row_tile(H, th, 16)
    nb = H // th

    def body(c_ref, g_ref, o_ref, out_ref):
        out_ref[...] = (g_ref[...].astype(f32) + o_ref[...].astype(f32)).astype(bf16)

    grid_spec = pltpu.PrefetchScalarGridSpec(
        num_scalar_prefetch=1, grid=(nb,),
        in_specs=[pl.BlockSpec((K, th, C), lambda i, c: (0, c[0] * nb + i, 0)),
                  pl.BlockSpec((K, th, C), lambda i, c: (0, i, 0))],
        out_specs=pl.BlockSpec((K, th, C), lambda i, c: (0, i, 0)))
    return pl.pallas_call(body, name=name, grid_spec=grid_spec,
                          out_shape=jax.ShapeDtypeStruct((K, H, C), bf16),
                          compiler_params=_params(("parallel",)))(c_idx, grads, other)


SEM = pl.BlockSpec(memory_space=pltpu.SEMAPHORE)
ANY = pl.BlockSpec(memory_space=pl.ANY)
EFFECT = pltpu.SideEffectType.DATAFLOW_SIDE_EFFECTING


def _copies_start(name, srcs, land_shapes, n_copies, make_copies, after):
    ns, nl = len(srcs), len(land_shapes)
    lands = [lax.empty(s.shape, s.dtype) for s in land_shapes]

    def body(*refs):
        src_refs, land_refs = refs[:ns], refs[ns:ns + nl]
        send, recv, token = refs[ns + nl + 1], refs[ns + nl + 2], refs[-1]
        for cp in make_copies(src_refs, land_refs, send, recv):
            cp.start()
        token[...] = jnp.zeros_like(token)

    buffers = list(srcs) + lands
    out = pl.pallas_call(
        body, name=name,
        out_shape=(pltpu.SemaphoreType.DMA((n_copies,)), pltpu.SemaphoreType.DMA((n_copies,)),
                   *[pltpu.HBM(b.shape, b.dtype) for b in buffers], jax.ShapeDtypeStruct((8, LANES), f32)),
        in_specs=[HBM] * (ns + nl) + [ANY],
        out_specs=(SEM, SEM, *[HBM] * (ns + nl), pl.BlockSpec(memory_space=pltpu.VMEM)),
        input_output_aliases={i: 2 + i for i in range(ns + nl)},
        compiler_params=pltpu.CompilerParams(has_side_effects=EFFECT),
    )(*[pltpu.with_memory_space_constraint(b, pltpu.HBM) for b in buffers], after)
    return out[0], out[1], list(out[2:2 + ns]), list(out[2 + ns:2 + ns + nl]), out[-1]


def _copies_wait(name, started, make_copies, after):
    send, recv, srcs, lands, _ = started
    ns, nl = len(srcs), len(lands)
    after = list(after)

    def body(*refs):
        src_refs, land_refs = refs[:ns], refs[ns:ns + nl]
        for cp in make_copies(src_refs, land_refs, refs[ns + nl], refs[ns + nl + 1]):
            cp.wait_send()
            cp.wait_recv()

    buffers = list(srcs) + list(lands)
    out = pl.pallas_call(
        body, name=name, out_shape=tuple(pltpu.HBM(b.shape, b.dtype) for b in buffers),
        in_specs=[HBM] * (ns + nl) + [SEM, SEM] + [ANY] * len(after), out_specs=tuple([HBM] * (ns + nl)),
        input_output_aliases={i: i for i in range(ns + nl)},
        compiler_params=pltpu.CompilerParams(has_side_effects=EFFECT),
    )(*buffers, send, recv, *after)
    return list(out[:ns]), list(out[ns:])


def _gather_copies(src_refs, land_refs, send, recv):
    x, y, c = _me()
    my_chip = 2 * x + y
    peers = [(*chip, c) for chip in _other_chips(x, y)] + [(x, y, 1 - c)]
    return [_remote(src_refs[i], land_refs[i].at[my_chip], send.at[4 * i + j], recv.at[4 * i + j], peer)
            for i in range(len(src_refs)) for j, peer in enumerate(peers)]


def _partial_copies(src_refs, land_refs, send, recv):
    x, y, c = _me()
    return [_remote(src_refs[i].at[2 * chip[0] + chip[1]], land_refs[i].at[j], send.at[3 * i + j], recv.at[3 * i + j], (*chip, c))
            for i in range(len(src_refs)) for j, chip in enumerate(_other_chips(x, y))]


def _small_copies(src_refs, land_refs, send, recv):
    x, y, c = _me()
    return [_remote(src_refs[0], land_refs[0].at[k - 1], send.at[k - 1], recv.at[k - 1], (x ^ (k >> 2), y ^ ((k >> 1) & 1), c ^ (k & 1)))
            for k in range(1, N_DEV)]


def _sum_small(own, slots, dev_idx):
    R, C = own.shape

    def body(dev_ref, own_ref, s_ref, o_ref):
        me = dev_ref[0]
        acc = jnp.zeros((R, C), f32)
        for d in range(N_DEV):
            k = me ^ d
            acc = acc + jnp.where(k == 0, own_ref[...], s_ref[jnp.maximum(k - 1, 0)])
        o_ref[...] = acc

    grid_spec = pltpu.PrefetchScalarGridSpec(
        num_scalar_prefetch=1, grid=(1,),
        in_specs=[pl.BlockSpec((R, C), lambda i, dev: (0, 0)), pl.BlockSpec((N_DEV - 1, R, C), lambda i, dev: (0, 0, 0))],
        out_specs=pl.BlockSpec((R, C), lambda i, dev: (0, 0)))
    return pl.pallas_call(body, name="sum_small", grid_spec=grid_spec, out_shape=jax.ShapeDtypeStruct((R, C), f32),
                          compiler_params=_params(("arbitrary",)))(dev_idx, own, slots)


def _sum_partials(name, part, recv, chip_idx, th=256):
    K, H, C = part.shape
    th = _row_tile(H, th, 16)

    def body(chip_ref, p_ref, r_ref, o_ref):
        acc = p_ref[...].astype(f32)
        for j in range(3):
            acc = acc + r_ref[j].astype(f32)
        o_ref[...] = acc

    grid_spec = pltpu.PrefetchScalarGridSpec(
        num_scalar_prefetch=1, grid=(H // th,),
        in_specs=[pl.BlockSpec((None, th, C), lambda i, chip: (chip[0], i, 0)),
                  pl.BlockSpec((3, th, C), lambda i, chip: (0, i, 0))],
        out_specs=pl.BlockSpec((th, C), lambda i, chip: (i, 0)))
    return pl.pallas_call(body, name=name, grid_spec=grid_spec, out_shape=jax.ShapeDtypeStruct((H, C), f32),
                          compiler_params=_params(("parallel",)))(chip_idx, part, recv)


def _share_halves(name, halves):
    n = len(halves)

    def body(*refs):
        ins, outs, send, recv = refs[:n], refs[n:2 * n], refs[2 * n], refs[2 * n + 1]
        x, y, c = _me()
        copies = []
        for i in range(n):
            cp = _remote(ins[i], outs[i], send.at[i], recv.at[i], (x, y, 1 - c))
            cp.start()
            copies.append(cp)
        for cp in copies:
            cp.wait()

    return pl.pallas_call(
        body, name=name, out_shape=[jax.ShapeDtypeStruct(h.shape, h.dtype) for h in halves],
        in_specs=[HBM] * n, out_specs=[HBM] * n, scratch_shapes=_sems(n))(*halves)


def _adamw_big(name, w, g_mine, g_theirs, m, v, c_idx, tr=256):
    R, C = w.shape
    H = R // 2
    tr = _row_tile(H, tr)
    nb = H // tr

    def body(c_ref, w_ref, gm_ref, gt_ref, m_ref, v_ref, g_ref, d_ref, mo_ref, vo_ref):
        g_ = jnp.where(pl.program_id(0) // nb == c_ref[0], gm_ref[...], gt_ref[...])
        g_ref[...] = g_
        m_ = ADAM_B1 * m_ref[...] + (1.0 - ADAM_B1) * g_
        v_ = ADAM_B2 * v_ref[...] + (1.0 - ADAM_B2) * jnp.square(g_)
        m_hat = m_ / (1.0 - ADAM_B1 ** ADAM_STEP)
        v_hat = v_ / (1.0 - ADAM_B2 ** ADAM_STEP)
        d_ref[...] = -ADAM_LR * (m_hat / (jnp.sqrt(v_hat) + ADAM_EPS) + ADAM_WD * w_ref[...])
        mo_ref[...] = m_
        vo_ref[...] = v_

    full = pl.BlockSpec((tr, C), lambda i, c: (i, 0))
    half = pl.BlockSpec((tr, C), lambda i, c: (i % nb, 0))
    grid_spec = pltpu.PrefetchScalarGridSpec(num_scalar_prefetch=1, grid=(2 * nb,),
                                             in_specs=[full, half, half, full, full], out_specs=[full] * 4)
    return pl.pallas_call(body, name=name, grid_spec=grid_spec, out_shape=[jax.ShapeDtypeStruct((R, C), f32)] * 4,
                          compiler_params=_params(("parallel",)))(c_idx, w, g_mine, g_theirs, m, v)


BIG = ("ffn1_w_gate", "ffn1_w_up", "ffn1_w_down", "w_in", "w_out", "ffn2_w_gate", "ffn2_w_up", "ffn2_w_down",
       "ple_w_gate", "ple_w_proj")


SMALL = ("ffn1_norm", "mix_norm", "gm_ln_g", "gm_ln_b", "gm_w_s", "gm_b_s", "gm_out_norm", "conv_b", "dt_bias", "a_log",
         "d_skip", "ssm_norm", "ffn2_norm", "ple_norm", "ple_b_gate", "final_norm")
SMALL_C = 1024


def _pack_small(vals):
    parts = []
    for v in vals:
        f = v.astype(f32).reshape(-1)
        parts.append(jnp.pad(f, (0, -f.shape[0] % SMALL_C)))
    flat = jnp.concatenate(parts)
    rows = flat.shape[0] // SMALL_C
    return jnp.pad(flat, (0, (-rows % 8) * SMALL_C)).reshape(-1, SMALL_C)


def _unpack_small(pack, shapes):
    flat = pack.reshape(-1)
    out, off = [], 0
    for s in shapes:
        n = 1
        for d in s:
            n *= d
        out.append(flat[off:off + n].reshape(s))
        off += n + (-n % SMALL_C)
    return out


def _pad_lanes(v):
    return jnp.pad(v, ((0, 0), (0, LANES - v.shape[1])))


FETCH = (("ffn1_w_gate", "ffn1_w_up", "ffn1_w_down"), ("w_in", "conv_w", "w_out"),
         ("ffn2_w_gate", "ffn2_w_up", "ffn2_w_down", "ple_w_gate", "ple_w_proj"))
TRANSPOSED = ("ffn1_w_gate", "ffn1_w_up", "ffn2_w_gate", "ffn2_w_up")
DONE =(("ffn2_w_gate", "ffn2_w_up", "ffn2_w_down", "w_out", "ple_w_gate", "ple_w_proj"), ("w_in",),
        ("ffn1_w_gate", "ffn1_w_up", "ffn1_w_down"))


def _local_step(x, p, tgt, fetch, S, on_grads):
    G = GM_WIDTH
    K = N_CHIPS
    b_st = S["gm_b_s"][0].T
    w_s = S["gm_w_s"][0]
    dtb, alog, dsk = _pad_lanes(S["dt_bias"]), _pad_lanes(S["a_log"]), _pad_lanes(S["d_skip"])
    gfin = S["final_norm"].reshape(1, -1)

    wg1, wu1, wd1 = fetch(0, None)
    h1, n1, a1, b1 = _ffn_fwd("ffn1_fwd", x, S["ffn1_norm"], wg1, wu1, wd1)
    w_in4, cw4, wo4 = fetch(1, h1)
    w_in = jnp.concatenate([w_in4[k] for k in range(K)], axis=1)
    w_uv = w_in[:, :2 * G]
    w_zxd = jnp.pad(w_in[:, 2 * G:], ((0, 0), (0, ZXD - (IN_PROJ - 2 * G))))
    conv_w = jnp.transpose(cw4, (1, 0, 2)).reshape(SSM_CONV, CONV_DIM)
    wo = wo4.reshape(-1, D_MODEL)
    n2, uv, z, xbc, dtr, ya = _mix_fwd(h1, S["mix_norm"], w_uv, w_zxd, S["gm_ln_g"], S["gm_ln_b"], w_s, b_st, S["gm_out_norm"])
    yb, sprev = _ssd_fwd(xbc, z, dtr, conv_w, S["conv_b"], dtb, alog, dsk, S["ssm_norm"])
    wg2, wu2, wd2, wpg4, wpp4 = fetch(2, yb)
    h2, h3, n3, a2, b2 = _ffn_fwd("ffn2_fwd", h1, S["ffn2_norm"], wg2, wu2, wd2, pre=(ya, yb, wo))
    dh3, loss, dgp, dwpg, dbpg, dwpp, dgf = _tail(h3, p, tgt, S["ple_norm"], wpg4.reshape(-1, D_MODEL), S["ple_b_gate"], wpp4, gfin)
    dh2, da2, db2, hm2, dg_ffn2, dya, dyb = _ffn_bwd("ffn2_bwd", dh3, h2, S["ffn2_norm"], a2, b2, wg2, wu2, wd2, wo=wo, ga=G)
    dw_out = jnp.concatenate([_matmul_tn("dw_out_a", ya, dh2), _matmul_tn("dw_out_b", yb, dh2)], axis=0).reshape(wo4.shape)
    zero = on_grads(0, [_matmul_tn("dw_ffn2_gate", da2, n3), _matmul_tn("dw_ffn2_up", db2, n3),
                        _matmul_tn("dw_ffn2_down", hm2, dh3, scale=0.5), dw_out,
                        dwpg.astype(bf16).reshape(wpg4.shape), dwpp.astype(bf16)])
    duv, dlng, dlnb, dws, dbst, dgout = _gm_bwd(uv, dya, S["gm_ln_g"], S["gm_ln_b"], w_s, b_st, S["gm_out_norm"] + zero)
    dzxd, dcw, dcb, ddtb, dalog, ddsk, dgssm = _ssd_bwd(xbc, z, dtr, sprev, dyb, conv_w, S["conv_b"], dtb, alog, dsk,
                                                        S["ssm_norm"] + zero)
    dh1, dg_mix = _mix_bwd(dh2, h1, S["mix_norm"], duv, dzxd, w_uv, w_zxd)
    dw_in = jnp.concatenate([_matmul_tn("dw_in_uv", n2, duv), _matmul_tn("dw_in_zxd", n2, dzxd)[:, :IN_PROJ - 2 * G]], axis=1)
    zero = on_grads(1, [jnp.transpose(dw_in.reshape(D_MODEL, K, IN_PROJ // K), (1, 0, 2))])
    dx, da1, db1, hm1, dg_ffn1 = _ffn_bwd("ffn1_bwd", dh1, x, S["ffn1_norm"] + zero, a1, b1, wg1, wu1, wd1)
    zero = on_grads(2, [_matmul_tn("dw_ffn1_gate", da1, n1), _matmul_tn("dw_ffn1_up", db1, n1),
                        _matmul_tn("dw_ffn1_down", hm1, dh1, scale=0.5)])
    loss = loss + zero
    nh = SSM_HEADS
    gS = {"ffn1_norm": dg_ffn1, "mix_norm": dg_mix, "gm_ln_g": dlng, "gm_ln_b": dlnb, "gm_w_s": dws[None], "gm_b_s": dbst.T[None],
          "gm_out_norm": dgout, "conv_b": dcb, "dt_bias": ddtb[:, :nh], "a_log": dalog[:, :nh], "d_skip": ddsk[:, :nh],
          "ssm_norm": dgssm, "ffn2_norm": dg_ffn2, "ple_norm": dgp, "ple_b_gate": dbpg, "final_norm": dgf.reshape(-1)}
    return loss, dx, dcw, gS


_WEIGHTS = ("ffn1_norm", "ffn1_w_gate", "ffn1_w_up", "ffn1_w_down", "mix_norm", "w_in", "gm_ln_g", "gm_ln_b", "gm_w_s", "gm_b_s",
            "gm_out_norm", "conv_w", "conv_b", "dt_bias", "a_log", "d_skip", "ssm_norm", "w_out", "ffn2_norm", "ffn2_w_gate",
            "ffn2_w_up", "ffn2_w_down", "ple_norm", "ple_w_gate", "ple_b_gate", "ple_w_proj", "final_norm")
_BIG_NAMES = BIG


def kernel(x, p, ffn1_norm, ffn1_w_gate, ffn1_w_up, ffn1_w_down, mix_norm, w_in, gm_ln_g, gm_ln_b, gm_w_s, gm_b_s, gm_out_norm, conv_w, conv_b, dt_bias, a_log, d_skip, ssm_norm, w_out, ffn2_norm, ffn2_w_gate, ffn2_w_up, ffn2_w_down, ple_norm, ple_w_gate, ple_b_gate, ple_w_proj, final_norm, loss_target, m_ffn1_norm, m_ffn1_w_gate, m_ffn1_w_up, m_ffn1_w_down, m_mix_norm, m_w_in, m_gm_ln_g, m_gm_ln_b, m_gm_w_s, m_gm_b_s, m_gm_out_norm, m_conv_w, m_conv_b, m_dt_bias, m_a_log, m_d_skip, m_ssm_norm, m_w_out, m_ffn2_norm, m_ffn2_w_gate, m_ffn2_w_up, m_ffn2_w_down, m_ple_norm, m_ple_w_gate, m_ple_b_gate, m_ple_w_proj, m_final_norm, v_ffn1_norm, v_ffn1_w_gate, v_ffn1_w_up, v_ffn1_w_down, v_mix_norm, v_w_in, v_gm_ln_g, v_gm_ln_b, v_gm_w_s, v_gm_b_s, v_gm_out_norm, v_conv_w, v_conv_b, v_dt_bias, v_a_log, v_d_skip, v_ssm_norm, v_w_out, v_ffn2_norm, v_ffn2_w_gate, v_ffn2_w_up, v_ffn2_w_down, v_ple_norm, v_ple_w_gate, v_ple_b_gate, v_ple_w_proj, v_final_norm):
    given = dict(locals())
    w = {n: given[n] for n in _WEIGHTS}
    m = {n: given["m_" + n] for n in _WEIGHTS}
    v = {n: given["v_" + n] for n in _WEIGHTS}

    c_idx = lax.axis_index("c").astype(jnp.int32).reshape(1)
    chip = 2 * lax.axis_index("x") + lax.axis_index("y")
    chip_idx = chip.astype(jnp.int32).reshape(1)

    shard = {n: w[n][0].astype(bf16) for n in BIG}
    shard["conv_w"] = w["conv_w"][0]
    first = _gather_weights([shard[n] for n in FETCH[0]], [True] * len(FETCH[0]))
    fetching, after = [], first[-1]
    for k in (1, 2):
        srcs = [shard[n] for n in FETCH[k]]
        lands = [jax.ShapeDtypeStruct((N_CHIPS,) + s.shape, s.dtype) for s in srcs]
        fetching.append(_copies_start("gather%d_start" % k, srcs, lands, 4 * len(srcs), _gather_copies, after))
        after = fetching[-1][4]

    def fetch(k, after_):
        return first if k == 0 else _copies_wait("gather%d_wait" % k, fetching[k - 1], _gather_copies, [after_])[1]

    exchanging = []

    def on_grads(k, grads):
        others = _swap_halves("swap%d" % k, grads)
        parts = [_add_halves("add_" + n, g_, o_, c_idx) for n, g_, o_ in zip(DONE[k], grads, others)]
        lands = [jax.ShapeDtypeStruct((3,) + p_.shape[1:], p_.dtype) for p_ in parts]
        exchanging.append(_copies_start("exchange%d_start" % k, parts, lands, 3 * len(parts), _partial_copies, c_idx))
        return exchanging[-1][4][0, 0]

    S = {n: w[n] for n in SMALL}
    S["ffn1_norm"] = S["ffn1_norm"] + after[0, 0]
    loss, dx, dcw, gS = _local_step(x[0], p[0, 0], loss_target[0], fetch, S, on_grads)

    small = _pack_small([gS[n] for n in SMALL] + [dcw, loss[:, :1]])
    small_lands = [jax.ShapeDtypeStruct((N_DEV - 1,) + small.shape, small.dtype)]
    small_st = _copies_start("small_start", [small], small_lands, N_DEV - 1, _small_copies, c_idx)

    g, delta, new_m, new_v = {}, {}, {}, {}
    after = [small_st[4]]
    for k in range(len(DONE)):
        parts, recv = _copies_wait("exchange%d_wait" % k, exchanging[k], _partial_copies, after)
        mine = [_sum_partials("sum_" + n, p_, r_, chip_idx) for n, p_, r_ in zip(DONE[k], parts, recv)]
        theirs = _share_halves("share%d" % k, mine)
        after = []
        for n, gm_, gt_ in zip(DONE[k], mine, theirs):
            flip = (lambda a: jnp.swapaxes(a, 0, 1)) if n in TRANSPOSED else (lambda a: a)
            outs = _adamw_big("adamw_" + n, flip(w[n][0]), gm_, gt_, flip(m[n][0]), flip(v[n][0]), c_idx)
            g[n], delta[n], new_m[n], new_v[n] = [flip(o)[None] for o in outs]
            after.append(outs[3])
    (own,), (slots,) = _copies_wait("small_wait", small_st, _small_copies, after)
    dev_idx = (2 * chip + lax.axis_index("c")).astype(jnp.int32).reshape(1)
    small_shapes = [w[n].shape for n in SMALL] + [dcw.shape, (1, 1)]
    small_sum = _unpack_small(_sum_small(own, slots, dev_idx), small_shapes)
    g.update({n: small_sum[i] for i, n in enumerate(SMALL)})
    cshard = w["conv_w"].shape[2]
    g["conv_w"] = lax.dynamic_slice_in_dim(small_sum[len(SMALL)], chip * cshard, cshard, axis=1)[None]
    loss_total = small_sum[len(SMALL) + 1].reshape(())
    sm_names = SMALL + ("conv_w",)
    sm_shapes = [w[n].shape for n in sm_names]
    d_s, m_s, v_s = _adamw("adamw_small", _pack_small([w[n] for n in sm_names]), _pack_small([g[n] for n in sm_names]),
                           _pack_small([m[n] for n in sm_names]), _pack_small([v[n] for n in sm_names]))
    for dst, src in ((delta, d_s), (new_m, m_s), (new_v, v_s)):
        for n, val in zip(sm_names, _unpack_small(src, sm_shapes)):
            dst[n] = val

    return (loss_total, dx[None], *[g[n] for n in _WEIGHTS], *[delta[n] for n in _WEIGHTS],
            *[new_m[n] for n in _WEIGHTS], *[new_v[n] for n in _WEIGHTS])
```

```python
import functools

import jax
import jax.numpy as jnp
from jax import lax
from jax.experimental import pallas as pl
from jax.experimental.pallas import tpu as pltpu

f32 = jnp.float32
bf16 = jnp.bfloat16
MESH = pl.DeviceIdType.MESH
HIGHEST = lax.Precision.HIGHEST

EPS = 1e-6
N_CHIPS = 4
N_DEV = 8
D_MODEL = 1024
D_FF = 2816
D_PLE = 256
GM_WIDTH = 1024
GM_HEADS = 8
CHUNK = 128
SSM_WIDTH = 1024
SSM_HEADS = 16
SSM_HEAD_DIM = 64
SSM_GROUPS = 2
SSM_STATE = 128
SSM_CONV = 4
CONV_DIM = SSM_WIDTH + 2 * SSM_GROUPS * SSM_STATE
IN_PROJ = 2 * GM_WIDTH + SSM_WIDTH + CONV_DIM + SSM_HEADS
LANES = 128
ZXD = SSM_WIDTH + CONV_DIM + LANES

ADAM_LR = 0.001
ADAM_B1 = 0.9
ADAM_B2 = 0.999
ADAM_EPS = 1e-08
ADAM_WD = 0.01
ADAM_STEP = 10

VMEM_LIMIT = 56 * 1024 * 1024


def _dot(a, b):
    return jnp.dot(a, b, preferred_element_type=f32)


def _dot_nt(a, b):
    return lax.dot_general(a, b, (((1,), (1,)), ((), ())), preferred_element_type=f32)


def _dot_tn(a, b):
    return lax.dot_general(a, b, (((0,), (0,)), ((), ())), preferred_element_type=f32)


def _rms(x, g):
    return x * lax.rsqrt(jnp.mean(x * x, axis=-1, keepdims=True) + EPS) * g


def _gelu(x):
    return 0.5 * x * (1.0 + lax.erf(x * 0.7071067811865476))


def _layernorm(x, g, b):
    mu = jnp.mean(x, axis=-1, keepdims=True)
    xc = x - mu
    return xc * lax.rsqrt(jnp.mean(xc * xc, axis=-1, keepdims=True) + EPS) * g + b


def _sigmoid(x):
    return 1.0 / (1.0 + jnp.exp(-x))


def _softplus(x):
    return jnp.maximum(x, 0.0) + jnp.log(1.0 + jnp.exp(-jnp.abs(x)))


def _full(shape):
    nd = len(shape)
    return pl.BlockSpec(shape, lambda *_: (0,) * nd, pipeline_mode=pl.Buffered(1))


def _acc(shape):
    nd = len(shape)
    return pl.BlockSpec(shape, lambda *_: (0,) * nd)


def _rows(tm, ncols):
    return pl.BlockSpec((tm, ncols), lambda i: (i, 0))


def _params(sem):
    return pltpu.CompilerParams(dimension_semantics=sem, vmem_limit_bytes=VMEM_LIMIT)


def _row_tile(rows, target, mult=8):
    best = rows
    for t in range(mult, min(rows, target) + 1, mult):
        if rows % t == 0:
            best = t
    return best if best <= target else rows


def _stack_rows(k, tm, ncols):
    return pl.BlockSpec((k, tm, ncols), lambda i: (0, i, 0))


def _ffn_fwd(name, h, g, wg, wu, wd, pre=None, tm=256):
    T, D = h.shape
    K, _, Fs = wg.shape
    tm = min(tm, T)

    def body(*refs):
        if pre is None:
            h_ref, g_ref, wg_ref, wu_ref, wd_ref, ho_ref, n_ref, a_ref, b_ref = refs
            hin = h_ref[...]
        else:
            (h_ref, ya_ref, yb_ref, wo_ref, g_ref, wg_ref, wu_ref, wd_ref,
             hi_ref, ho_ref, n_ref, a_ref, b_ref) = refs
            ga = ya_ref.shape[1]
            hin = h_ref[...] + _dot(ya_ref[...], wo_ref[:ga, :]) + _dot(yb_ref[...], wo_ref[ga:, :])
            hi_ref[...] = hin
        n = _rms(hin, g_ref[...]).astype(bf16)
        n_ref[...] = n
        acc = jnp.zeros((tm, D), f32)
        for k in range(K):
            a = _dot(n, wg_ref[k]).astype(bf16)
            b = _dot(n, wu_ref[k]).astype(bf16)
            a_ref[k] = a
            b_ref[k] = b
            af = a.astype(f32)
            hm = (af * _sigmoid(af) * b.astype(f32)).astype(bf16)
            acc = acc + _dot(hm, wd_ref[k])
        ho_ref[...] = hin + 0.5 * acc

    ins = [h] + (list(pre) if pre is not None else []) + [g, wg, wu, wd]
    in_specs = [_rows(tm, D)]
    if pre is not None:
        in_specs += [_rows(tm, pre[0].shape[1]), _rows(tm, pre[1].shape[1]), _full(pre[2].shape)]
    in_specs += [_full(g.shape), _full(wg.shape), _full(wu.shape), _full(wd.shape)]
    outs = [jax.ShapeDtypeStruct((T, D), f32), jax.ShapeDtypeStruct((T, D), bf16),
            jax.ShapeDtypeStruct((K, T, Fs), bf16), jax.ShapeDtypeStruct((K, T, Fs), bf16)]
    out_specs = [_rows(tm, D), _rows(tm, D), _stack_rows(K, tm, Fs), _stack_rows(K, tm, Fs)]
    if pre is not None:
        outs = [jax.ShapeDtypeStruct((T, D), f32)] + outs
        out_specs = [_rows(tm, D)] + out_specs
    return pl.pallas_call(body, name=name, grid=(T // tm,), in_specs=in_specs, out_specs=out_specs,
                          out_shape=outs, compiler_params=_params(("parallel",)))(*ins)


def _ffn_bwd(name, dh, hin, g, a, b, wg, wu, wd, wo=None, ga=0, tm=256):
    T, D = dh.shape
    K, _, Fs = wg.shape
    tm = min(tm, T)

    def body(*refs):
        if wo is None:
            (dh_ref, hin_ref, g_ref, a_ref, b_ref, wg_ref, wu_ref, wd_ref,
             dhi_ref, da_ref, db_ref, hm_ref, dg_ref) = refs
        else:
            (dh_ref, hin_ref, g_ref, a_ref, b_ref, wg_ref, wu_ref, wd_ref, wo_ref,
             dhi_ref, da_ref, db_ref, hm_ref, dg_ref, dya_ref, dyb_ref) = refs

        @pl.when(pl.program_id(0) == 0)
        def _():
            dg_ref[...] = jnp.zeros_like(dg_ref)

        dh_ = dh_ref[...]
        dhb = (0.5 * dh_).astype(bf16)
        dn = jnp.zeros((tm, D), f32)
        for k in range(K):
            dhm = _dot_nt(dhb, wd_ref[k])
            af = a_ref[k].astype(f32)
            bf = b_ref[k].astype(f32)
            sg = _sigmoid(af)
            sl_ = af * sg
            da = (dhm * bf * (sg * (1.0 + af * (1.0 - sg)))).astype(bf16)
            db = (dhm * sl_).astype(bf16)
            da_ref[k] = da
            db_ref[k] = db
            hm_ref[k] = (sl_ * bf).astype(bf16)
            dn = dn + _dot_nt(da, wg_ref[k]) + _dot_nt(db, wu_ref[k])
        _, vjp = jax.vjp(_rms, hin_ref[...], g_ref[...])
        dx, dg = vjp(dn)
        dhi = dh_ + dx
        dhi_ref[...] = dhi
        dg_ref[...] += dg
        if wo is not None:
            dhib = dhi.astype(bf16)
            dya_ref[...] = _dot_nt(dhib, wo_ref[:ga, :]).astype(bf16)
            dyb_ref[...] = _dot_nt(dhib, wo_ref[ga:, :]).astype(bf16)

    ins = [dh, hin, g, a, b, wg, wu, wd]
    in_specs = [_rows(tm, D), _rows(tm, D), _full(g.shape), _stack_rows(K, tm, Fs), _stack_rows(K, tm, Fs),
                _full(wg.shape), _full(wu.shape), _full(wd.shape)]
    act = jax.ShapeDtypeStruct((K, T, Fs), bf16)
    outs = [jax.ShapeDtypeStruct((T, D), f32), act, act, act, jax.ShapeDtypeStruct(g.shape, f32)]
    out_specs = [_rows(tm, D), _stack_rows(K, tm, Fs), _stack_rows(K, tm, Fs), _stack_rows(K, tm, Fs), _acc(g.shape)]
    if wo is not None:
        gb = wo.shape[0] - ga
        ins += [wo]
        in_specs += [_full(wo.shape)]
        outs += [jax.ShapeDtypeStruct((T, ga), bf16), jax.ShapeDtypeStruct((T, gb), bf16)]
        out_specs += [_rows(tm, ga), _rows(tm, gb)]
    return pl.pallas_call(body, name=name, grid=(T // tm,), in_specs=in_specs, out_specs=out_specs,
                          out_shape=outs, compiler_params=_params(("arbitrary",)))(*ins)


def _matmul_tn(name, a, b, scale=1.0, tk=1024):
    ka = a.shape[0] if a.ndim == 3 else 0
    kb = b.shape[0] if b.ndim == 3 else 0
    K = max(ka, kb)
    T, M = a.shape[-2:]
    N = b.shape[-1]
    tk = min(tk, T)
    nk = T // tk
    if K:
        tn, nj = N, K
    else:
        tn = LANES * max(d for d in range(1, N // LANES + 1) if (N // LANES) % d == 0 and (d == 1 or M * d * LANES * 4 <= 6 * 1024 * 1024))
        nj = N // tn

    def body(a_ref, b_ref, o_ref, acc):
        k = pl.program_id(1)

        @pl.when(k == 0)
        def _():
            acc[...] = jnp.zeros_like(acc)

        bb = b_ref[...]
        if scale != 1.0:
            bb = bb * scale
        acc[...] += _dot_tn(a_ref[...].astype(bf16), bb.astype(bf16))

        @pl.when(k == nk - 1)
        def _():
            o_ref[...] = acc[...].astype(bf16)

    a_spec = pl.BlockSpec((None, tk, M), lambda j, k: (j, k, 0)) if ka else pl.BlockSpec((tk, M), lambda j, k: (k, 0))
    if kb:
        b_spec = pl.BlockSpec((None, tk, N), lambda j, k: (j, k, 0))
    elif K:
        b_spec = pl.BlockSpec((tk, N), lambda j, k: (k, 0))
    else:
        b_spec = pl.BlockSpec((tk, tn), lambda j, k: (k, j))
    if K:
        o_spec, o_shape = pl.BlockSpec((None, M, N), lambda j, k: (j, 0, 0)), (K, M, N)
    else:
        o_spec, o_shape = pl.BlockSpec((M, tn), lambda j, k: (0, j)), (M, N)
    return pl.pallas_call(
        body, name=name, grid=(nj, nk), in_specs=[a_spec, b_spec], out_specs=o_spec,
        out_shape=jax.ShapeDtypeStruct(o_shape, bf16), scratch_shapes=[pltpu.VMEM((M, tn), f32)],
        compiler_params=_params(("parallel", "arbitrary")))(a, b)


def _gm_pre(u, v, ln_g, ln_b):
    return _gelu(u), _layernorm(_gelu(v), ln_g, ln_b)


def _tril_mask():
    r = lax.broadcasted_iota(jnp.int32, (CHUNK, CHUNK), 0)
    c = lax.broadcasted_iota(jnp.int32, (CHUNK, CHUNK), 1)
    return c <= r


def _gm_mix(vnb, ws_ref, bst, mixed_sc, tm):
    mask = _tril_mask()
    for h in range(GM_HEADS):
        wt = jnp.where(mask, ws_ref[h], 0.0).astype(bf16)
        bias = bst[:, h:h + 1]
        for q in range(tm // CHUNK):
            rs = slice(q * CHUNK, (q + 1) * CHUNK)
            cs = slice(h * CHUNK, (h + 1) * CHUNK)
            mixed_sc[rs, cs] = _dot(wt, vnb[rs, cs]) + bias


def _mix_fwd(h1, gmix, w_uv, w_zxd, ln_g, ln_b, w_s, b_st, gout, tm=512):
    T, D = h1.shape
    tm = min(tm, T)
    G = GM_WIDTH

    def body(h_ref, g_ref, wuv_ref, wzxd_ref, lng_ref, lnb_ref, ws_ref, bst_ref, gout_ref,
             n_ref, uv_ref, z_ref, xbc_ref, dt_ref, ya_ref, mixed_sc):
        n = _rms(h_ref[...], g_ref[...]).astype(bf16)
        n_ref[...] = n
        u = _dot_nt(n, wuv_ref[:G, :]).astype(bf16)
        v = _dot_nt(n, wuv_ref[G:, :]).astype(bf16)
        uv_ref[:, :G] = u
        uv_ref[:, G:] = v
        z_ref[...] = _dot_nt(n, wzxd_ref[:SSM_WIDTH, :]).astype(bf16)
        xbc_ref[...] = _dot_nt(n, wzxd_ref[SSM_WIDTH:SSM_WIDTH + CONV_DIM, :]).astype(bf16)
        dt_ref[...] = _dot_nt(n, wzxd_ref[SSM_WIDTH + CONV_DIM:, :])
        ug, vn = _gm_pre(u.astype(f32), v.astype(f32), lng_ref[...], lnb_ref[...])
        _gm_mix(vn.astype(bf16), ws_ref, bst_ref[...], mixed_sc, tm)
        ya_ref[...] = _rms(ug * mixed_sc[...], gout_ref[...]).astype(bf16)

    ins = [h1, gmix, w_uv, w_zxd, ln_g, ln_b, w_s, b_st, gout]
    in_specs = [_rows(tm, D)] + [_full(x.shape) for x in ins[1:]]
    outs = [jax.ShapeDtypeStruct((T, D), bf16), jax.ShapeDtypeStruct((T, 2 * G), bf16),
            jax.ShapeDtypeStruct((T, SSM_WIDTH), bf16), jax.ShapeDtypeStruct((T, CONV_DIM), bf16),
            jax.ShapeDtypeStruct((T, LANES), f32), jax.ShapeDtypeStruct((T, G), bf16)]
    out_specs = [_rows(tm, D), _rows(tm, 2 * G), _rows(tm, SSM_WIDTH), _rows(tm, CONV_DIM), _rows(tm, LANES), _rows(tm, G)]
    return pl.pallas_call(body, name="mix_fwd", grid=(T // tm,), in_specs=in_specs, out_specs=out_specs,
                          out_shape=outs, scratch_shapes=[pltpu.VMEM((tm, G), f32)],
                          compiler_params=_params(("parallel",)))(*ins)


def _gm_bwd(uv, dya, ln_g, ln_b, w_s, b_st, gout, tm=256):
    T = uv.shape[0]
    tm = min(tm, T)
    G = GM_WIDTH

    def body(uv_ref, dya_ref, lng_ref, lnb_ref, ws_ref, bst_ref, gout_ref,
             duv_ref, dlng_ref, dlnb_ref, dws_ref, dbst_ref, dgout_ref, mixed_sc, dvn_sc):
        @pl.when(pl.program_id(0) == 0)
        def _():
            for r in (dlng_ref, dlnb_ref, dws_ref, dbst_ref, dgout_ref):
                r[...] = jnp.zeros_like(r)

        u = uv_ref[:, :G].astype(f32)
        v = uv_ref[:, G:].astype(f32)
        (ug, vn), pre_vjp = jax.vjp(_gm_pre, u, v, lng_ref[...], lnb_ref[...])
        vnb = vn.astype(bf16)
        _gm_mix(vnb, ws_ref, bst_ref[...], mixed_sc, tm)
        mixed = mixed_sc[...]
        _, out_vjp = jax.vjp(_rms, ug * mixed, gout_ref[...])
        dpre, dgout = out_vjp(dya_ref[...].astype(f32))
        dgout_ref[...] += dgout
        dug = dpre * mixed
        dmixed = dpre * ug
        mask = _tril_mask()
        lane = lax.broadcasted_iota(jnp.int32, (1, GM_HEADS), 1)
        dbst = jnp.zeros((CHUNK, GM_HEADS), f32)
        for h in range(GM_HEADS):
            wt = jnp.where(mask, ws_ref[h], 0.0).astype(bf16)
            cs = slice(h * CHUNK, (h + 1) * CHUNK)
            dw = jnp.zeros((CHUNK, CHUNK), f32)
            for q in range(tm // CHUNK):
                rs = slice(q * CHUNK, (q + 1) * CHUNK)
                dm = dmixed[rs, cs]
                dmb = dm.astype(bf16)
                dw = dw + _dot_nt(dmb, vnb[rs, cs])
                dbst = dbst + jnp.sum(dm, axis=1, keepdims=True) * (lane == h).astype(f32)
                dvn_sc[rs, cs] = _dot_tn(wt, dmb)
            dws_ref[h] += jnp.where(mask, dw, 0.0)
        dbst_ref[...] += dbst
        du, dv, dlng, dlnb = pre_vjp((dug, dvn_sc[...]))
        duv_ref[:, :G] = du.astype(bf16)
        duv_ref[:, G:] = dv.astype(bf16)
        dlng_ref[...] += dlng
        dlnb_ref[...] += dlnb

    ins = [uv, dya, ln_g, ln_b, w_s, b_st, gout]
    in_specs = [_rows(tm, 2 * G), _rows(tm, G)] + [_full(x.shape) for x in ins[2:]]
    outs = [jax.ShapeDtypeStruct((T, 2 * G), bf16)] + [jax.ShapeDtypeStruct(x.shape, f32) for x in (ln_g, ln_b, w_s, b_st, gout)]
    out_specs = [_rows(tm, 2 * G)] + [_acc(x.shape) for x in (ln_g, ln_b, w_s, b_st, gout)]
    return pl.pallas_call(body, name="gm_bwd", grid=(T // tm,), in_specs=in_specs, out_specs=out_specs,
                          out_shape=outs, scratch_shapes=[pltpu.VMEM((tm, G), f32), pltpu.VMEM((tm, G), f32)],
                          compiler_params=_params(("arbitrary",)))(*ins)


def _mix_bwd(dh, h1, gmix, duv, dzxd, w_uv, w_zxd, tm=512):
    T, D = dh.shape
    tm = min(tm, T)

    def body(dh_ref, h_ref, g_ref, duv_ref, dzxd_ref, wuv_ref, wzxd_ref, dhi_ref, dg_ref):
        @pl.when(pl.program_id(0) == 0)
        def _():
            dg_ref[...] = jnp.zeros_like(dg_ref)

        dn = _dot(duv_ref[...], wuv_ref[...]) + _dot(dzxd_ref[...], wzxd_ref[...])
        _, vjp = jax.vjp(_rms, h_ref[...], g_ref[...])
        dx, dg = vjp(dn)
        dhi_ref[...] = dh_ref[...] + dx
        dg_ref[...] += dg

    ins = [dh, h1, gmix, duv, dzxd, w_uv, w_zxd]
    in_specs = [_rows(tm, D), _rows(tm, D), _full(gmix.shape), _rows(tm, duv.shape[1]), _rows(tm, dzxd.shape[1]),
                _full(w_uv.shape), _full(w_zxd.shape)]
    return pl.pallas_call(body, name="mix_bwd", grid=(T // tm,), in_specs=in_specs,
                          out_specs=[_rows(tm, D), _acc(gmix.shape)],
                          out_shape=[jax.ShapeDtypeStruct((T, D), f32), jax.ShapeDtypeStruct(gmix.shape, f32)],
                          compiler_params=_params(("arbitrary",)))(*ins)


HALO = 16
PAIRS = SSM_HEADS // 2
PAIR_W = 2 * SSM_HEAD_DIM


def _split(x, n):
    parts = []
    for _ in range(n):
        p = x.astype(bf16)
        parts.append(p)
        x = x - p.astype(f32)
    return parts


def _dot_sel(x, sel_n, n):
    return _dot(jnp.concatenate(_split(x, n), axis=1), sel_n)


def _sel_dot(sel, x, n):
    return _dot(jnp.concatenate([sel] * n, axis=1), jnp.concatenate(_split(x, n), axis=0))


EXPAND_SPLIT = 3
REDUCE_SPLIT = 2


def _head_mats():
    ex = (jnp.arange(SSM_WIDTH)[None, :] // SSM_HEAD_DIM == jnp.arange(LANES)[:, None]).astype(bf16)
    return jnp.tile(ex, (EXPAND_SPLIT, 1)), jnp.tile(ex.T, (REDUCE_SPLIT, 1))


def _shift_mat(rows, cols, off):
    r = lax.broadcasted_iota(jnp.int32, (rows, cols), 0)
    c = lax.broadcasted_iota(jnp.int32, (rows, cols), 1)
    return (c == r + off).astype(bf16)


def _ssd_front(c, xbc_ref, halo_ref, dtr_ref, cw_ref, cb_ref, dtb_ref, alog_ref):
    halo = halo_ref[...]
    ext = jnp.concatenate([jnp.where(c > 0, halo, jnp.zeros_like(halo)), xbc_ref[...]], axis=0)
    taps = [_dot(_shift_mat(CHUNK, HALO + CHUNK, HALO - SSM_CONV + 1 + j), ext) for j in range(SSM_CONV - 1)]
    taps.append(xbc_ref[...].astype(f32))
    xc = cb_ref[...] + cw_ref[0:1, :] * taps[0]
    for j in range(1, SSM_CONV):
        xc = xc + cw_ref[j:j + 1, :] * taps[j]
    sg = _sigmoid(xc)
    xa = xc * sg
    dt = _softplus(dtr_ref[...] + dtb_ref[...])
    a = -jnp.exp(alog_ref[...])
    acs = jnp.dot(_tril_mask().astype(f32), dt * a, preferred_element_type=f32, precision=HIGHEST)
    return taps, xc, sg, xa, dt, a, acs


def _ssd_wide(xa, dt, acs, dsk, ex):
    dt_x = _dot_sel(dt, ex, EXPAND_SPLIT)
    acs_x = _dot_sel(acs, ex, EXPAND_SPLIT)
    dsk_x = _dot_sel(jnp.broadcast_to(dsk, (8, LANES)), ex, EXPAND_SPLIT)[0:1]
    e_x = jnp.exp(acs_x)
    r_x = jnp.exp(acs_x[CHUNK - 1:CHUNK, :] - acs_x)
    xs = xa[:, :SSM_WIDTH]
    xd = xs * dt_x
    return dt_x, dsk_x, e_x, r_x, xs, xd, xd * r_x


def _pair_stack(v, lo):
    return jnp.concatenate([jnp.where(lo, v, 0.0), jnp.where(lo, 0.0, v)], axis=0)


def _ssd_pair(j, acs, acs_t, cb):
    out = []
    tril = _tril_mask()
    for h in (2 * j, 2 * j + 1):
        dk = jnp.exp(jnp.where(tril, acs[:, h:h + 1] - acs_t[h:h + 1, :], -jnp.inf))
        out.append((dk, cb * dk))
    return out


def _pair_col(row_lo, tot, j):
    return jnp.exp(jnp.where(row_lo, tot[:, 2 * j:2 * j + 1], tot[:, 2 * j + 1:2 * j + 2]))


def _gated_norm(y, z, g):
    yg = y * (z * _sigmoid(z))
    half = SSM_WIDTH // SSM_GROUPS
    parts = []
    for k in range(SSM_GROUPS):
        s = yg[:, k * half:(k + 1) * half]
        parts.append(s * lax.rsqrt(jnp.mean(s * s, axis=-1, keepdims=True) + EPS))
    return jnp.concatenate(parts, axis=1) * g


def _group_mats(xa):
    out = []
    for g in range(SSM_GROUPS):
        bm = xa[:, SSM_WIDTH + g * SSM_STATE:SSM_WIDTH + (g + 1) * SSM_STATE].astype(bf16)
        cm = xa[:, SSM_WIDTH + (SSM_GROUPS + g) * SSM_STATE:SSM_WIDTH + (SSM_GROUPS + g + 1) * SSM_STATE].astype(bf16)
        out.append((cm, bm, _dot_nt(cm, bm)))
    return out


def _ssd_in_specs(nc, rev):
    def ci(i):
        return nc - 1 - i if rev else i
    hp = CHUNK // HALO
    return [pl.BlockSpec((CHUNK, CONV_DIM), lambda i: (ci(i), 0)),
            pl.BlockSpec((HALO, CONV_DIM), lambda i: (jnp.maximum(ci(i) * hp - 1, 0), 0)),
            pl.BlockSpec((CHUNK, SSM_WIDTH), lambda i: (ci(i), 0)),
            pl.BlockSpec((CHUNK, LANES), lambda i: (ci(i), 0))]


def _ssd_fwd(xbc, z, dtr, conv_w, conv_b, dt_bias, a_log, d_skip, ssm_norm):
    T = xbc.shape[0]
    nc = T // CHUNK
    N = SSM_STATE

    def body(xbc_ref, halo_ref, z_ref, dtr_ref, cw_ref, cb_ref, dtb_ref, alog_ref, dsk_ref, g_ref, ex_ref,
             yb_ref, sprev_ref, s_sc):
        c = pl.program_id(0)

        @pl.when(c == 0)
        def _():
            s_sc[...] = jnp.zeros_like(s_sc)

        _, _, _, xa, dt, _, acs = _ssd_front(c, xbc_ref, halo_ref, dtr_ref, cw_ref, cb_ref, dtb_ref, alog_ref)
        _, dsk_x, e_x, _, xs, xd, gm = _ssd_wide(xa, dt, acs, dsk_ref[...], ex_ref[...])
        acs_t = acs.T
        tot = acs[CHUNK - 1:CHUNK, :]
        groups = _group_mats(xa)
        lo = lax.broadcasted_iota(jnp.int32, (CHUNK, PAIR_W), 1) < SSM_HEAD_DIM
        row_lo = lax.broadcasted_iota(jnp.int32, (PAIR_W, 1), 0) < SSM_HEAD_DIM
        ys = []
        for j in range(PAIRS):
            cmb, bmb, cb = groups[j // (PAIRS // SSM_GROUPS)]
            ps = slice(j * PAIR_W, (j + 1) * PAIR_W)
            (_, m0), (_, m1) = _ssd_pair(j, acs, acs_t, cb)
            sp = s_sc[j]
            yd = _dot(jnp.concatenate([m0, m1], axis=1).astype(bf16), _pair_stack(xd[:, ps], lo).astype(bf16))
            ys.append(yd + e_x[:, ps] * _dot_nt(cmb, sp.astype(bf16)))
            sprev_ref[0, j] = sp
            s_sc[j] = _pair_col(row_lo, tot, j) * sp + _dot_tn(gm[:, ps].astype(bf16), bmb)
        y = jnp.concatenate(ys, axis=1) + xs * dsk_x
        yb_ref[...] = _gated_norm(y, z_ref[...].astype(f32), g_ref[...]).astype(bf16)

    params = [conv_w, conv_b, dt_bias, a_log, d_skip, ssm_norm, _head_mats()[0]]
    return pl.pallas_call(
        body, name="ssd_fwd", grid=(nc,),
        in_specs=_ssd_in_specs(nc, False) + [_full(x.shape) for x in params],
        out_specs=[pl.BlockSpec((CHUNK, SSM_WIDTH), lambda i: (i, 0)), pl.BlockSpec((1, PAIRS, PAIR_W, N), lambda i: (i, 0, 0, 0))],
        out_shape=[jax.ShapeDtypeStruct((T, SSM_WIDTH), bf16), jax.ShapeDtypeStruct((nc, PAIRS, PAIR_W, N), f32)],
        scratch_shapes=[pltpu.VMEM((PAIRS, PAIR_W, N), f32)],
        compiler_params=_params(("arbitrary",)))(xbc, xbc, z, dtr, *params)


def _ssd_bwd(xbc, z, dtr, sprev, dyb, conv_w, conv_b, dt_bias, a_log, d_skip, ssm_norm):
    T = xbc.shape[0]
    nc = T // CHUNK
    H, N = SSM_HEADS, SSM_STATE
    PG = PAIRS // SSM_GROUPS

    def body(xbc_ref, halo_ref, z_ref, dtr_ref, sprev_ref, dyb_ref, cw_ref, cb_ref, dtb_ref, alog_ref, dsk_ref, g_ref,
             ex_ref, rd_ref, dzxd_ref, dcw_ref, dcb_ref, ddtb_ref, dalog_ref, ddsk_ref, dg_ref, ds_sc, next_sc):
        i = pl.program_id(0)
        c = nc - 1 - i

        @pl.when(i == 0)
        def _():
            ds_sc[...] = jnp.zeros_like(ds_sc)
            next_sc[...] = jnp.zeros_like(next_sc)
            for r_ in (dcw_ref, dcb_ref, ddtb_ref, dalog_ref, ddsk_ref, dg_ref):
                r_[...] = jnp.zeros_like(r_)

        taps, xc, sg, xa, dt, a, acs = _ssd_front(c, xbc_ref, halo_ref, dtr_ref, cw_ref, cb_ref, dtb_ref, alog_ref)
        dt_x, dsk_x, e_x, r_x, xs, xd, gm = _ssd_wide(xa, dt, acs, dsk_ref[...], ex_ref[...])
        acs_t = acs.T
        tot = acs[CHUNK - 1:CHUNK, :]
        groups = _group_mats(xa)
        lo = lax.broadcasted_iota(jnp.int32, (CHUNK, PAIR_W), 1) < SSM_HEAD_DIM
        row_lo = lax.broadcasted_iota(jnp.int32, (PAIR_W, 1), 0) < SSM_HEAD_DIM
        pairs, zs, yds = [], [], []
        for j in range(PAIRS):
            cmb, _, cb = groups[j // PG]
            ps = slice(j * PAIR_W, (j + 1) * PAIR_W)
            pairs.append(_ssd_pair(j, acs, acs_t, cb))
            (_, m0), (_, m1) = pairs[j]
            zs.append(_dot_nt(cmb, sprev_ref[0, j].astype(bf16)))
            yds.append(_dot(jnp.concatenate([m0, m1], axis=1).astype(bf16), _pair_stack(xd[:, ps], lo).astype(bf16)))
        zf = jnp.concatenate(zs, axis=1)
        y = jnp.concatenate(yds, axis=1) + e_x * zf + xs * dsk_x
        _, gn_vjp = jax.vjp(_gated_norm, y, z_ref[...].astype(f32), g_ref[...])
        dy, dz, dg = gn_vjp(dyb_ref[...].astype(f32))
        dg_ref[...] += dg
        dzxd_ref[:, :SSM_WIDTH] = dz.astype(bf16)

        lane = lax.broadcasted_iota(jnp.int32, (1, LANES), 1)
        sub = lax.broadcasted_iota(jnp.int32, (LANES, 1), 0)
        dacs = jnp.zeros((CHUNK, LANES), f32)
        dacs_r = jnp.zeros((LANES, CHUNK), f32)
        dtot = jnp.zeros((1, LANES), f32)
        dcb = [jnp.zeros((CHUNK, CHUNK), f32) for _ in range(SSM_GROUPS)]
        dcm = [jnp.zeros((CHUNK, N), f32) for _ in range(SSM_GROUPS)]
        dbm = [jnp.zeros((CHUNK, N), f32) for _ in range(SSM_GROUPS)]
        dxds, dgms = [], []
        for j in range(PAIRS):
            g = j // PG
            cmb, bmb, _ = groups[g]
            ps = slice(j * PAIR_W, (j + 1) * PAIR_W)
            (dk0, m0), (dk1, m1) = pairs[j]
            oh0, oh1 = (lane == 2 * j).astype(f32), (lane == 2 * j + 1).astype(f32)
            dyp = dy[:, ps]
            dy2 = _pair_stack(dyp, lo).astype(bf16)
            dm2 = _dot_nt(dy2, xd[:, ps].astype(bf16))
            m2 = jnp.concatenate([m0, m1], axis=0)
            dxds.append(_dot_tn(m2.astype(bf16), dy2))
            w2 = dm2 * m2
            rs = jnp.sum(w2, axis=1, keepdims=True)
            dacs = dacs + rs[:CHUNK] * oh0 + rs[CHUNK:] * oh1
            dacs_r = dacs_r - ((sub == 2 * j).astype(f32) * jnp.sum(w2[:CHUNK], axis=0, keepdims=True)
                               + (sub == 2 * j + 1).astype(f32) * jnp.sum(w2[CHUNK:], axis=0, keepdims=True))
            dcb[g] = dcb[g] + dm2[:CHUNK] * dk0 + dm2[CHUNK:] * dk1
            sp = sprev_ref[0, j]
            dzb = (dyp * e_x[:, ps]).astype(bf16)
            dcm[g] = dcm[g] + _dot(dzb, sp.astype(bf16))
            dsn = ds_sc[j]
            dsnb = dsn.astype(bf16)
            et = _pair_col(row_lo, tot, j)
            rr = jnp.sum(dsn * sp, axis=1, keepdims=True) * et
            dtot = dtot + jnp.sum(rr[:SSM_HEAD_DIM]) * oh0 + jnp.sum(rr[SSM_HEAD_DIM:]) * oh1
            dgms.append(_dot_nt(bmb, dsnb))
            dbm[g] = dbm[g] + _dot(gm[:, ps].astype(bf16), dsnb)
            ds_sc[j] = _dot_tn(dzb, cmb) + et * dsn
        dgm = jnp.concatenate(dgms, axis=1)
        dxd = jnp.concatenate(dxds, axis=1) + dgm * r_x
        dr = dgm * gm
        red = _dot_sel(jnp.concatenate([dy * e_x * zf - dr, dr, dxd * xs, dy * xs], axis=0), rd_ref[...], REDUCE_SPLIT)
        rowi = lax.broadcasted_iota(jnp.int32, (CHUNK, 1), 0)
        dtot = dtot + jnp.sum(red[CHUNK:2 * CHUNK], axis=0, keepdims=True)
        dacs = dacs + red[:CHUNK] + dacs_r.T + jnp.where(rowi == CHUNK - 1, dtot, 0.0)
        r2 = lax.broadcasted_iota(jnp.int32, (CHUNK, CHUNK), 0)
        c2 = lax.broadcasted_iota(jnp.int32, (CHUNK, CHUNK), 1)
        dadt = jnp.dot((c2 >= r2).astype(f32), dacs, preferred_element_type=f32, precision=HIGHEST)
        ddt = red[2 * CHUNK:3 * CHUNK] + dadt * a
        dalog_ref[...] += jnp.sum(dadt * dt, axis=0, keepdims=True) * a
        ddsk_ref[...] += jnp.sum(red[3 * CHUNK:], axis=0, keepdims=True)
        ddtr = jnp.where(lane < H, ddt * _sigmoid(dtr_ref[...] + dtb_ref[...]), 0.0)
        ddtb_ref[...] += jnp.sum(ddtr, axis=0, keepdims=True)
        dzxd_ref[:, SSM_WIDTH + CONV_DIM:] = ddtr.astype(bf16)
        dxa_bm, dxa_cm = [], []
        for g in range(SSM_GROUPS):
            cmb, bmb, _ = groups[g]
            dcbb = dcb[g].astype(bf16)
            dxa_bm.append(dbm[g] + _dot_tn(dcbb, cmb))
            dxa_cm.append(dcm[g] + _dot(dcbb, bmb))
        dxc = jnp.concatenate([dy * dsk_x + dxd * dt_x] + dxa_bm + dxa_cm, axis=1) * (sg * (1.0 + xc * (1.0 - sg)))
        ext = jnp.concatenate([dxc, next_sc[...]], axis=0)
        dxbc = cw_ref[SSM_CONV - 1:SSM_CONV, :] * dxc
        for s in range(1, SSM_CONV):
            dxbc = dxbc + cw_ref[SSM_CONV - 1 - s:SSM_CONV - s, :] * _sel_dot(_shift_mat(CHUNK, CHUNK + HALO, s), ext, 2)
        dzxd_ref[:, SSM_WIDTH:SSM_WIDTH + CONV_DIM] = dxbc.astype(bf16)
        dcw_ref[...] += jnp.concatenate([jnp.sum(dxc * t, axis=0, keepdims=True) for t in taps], axis=0)
        dcb_ref[...] += jnp.sum(dxc, axis=0, keepdims=True)
        next_sc[...] = dxc[0:HALO, :]

    params = [conv_w, conv_b, dt_bias, a_log, d_skip, ssm_norm]
    mats = list(_head_mats())

    def rc(i):
        return nc - 1 - i

    in_specs = (_ssd_in_specs(nc, True)
                + [pl.BlockSpec((1, PAIRS, PAIR_W, N), lambda i: (rc(i), 0, 0, 0)), pl.BlockSpec((CHUNK, SSM_WIDTH), lambda i: (rc(i), 0))]
                + [_full(x.shape) for x in params + mats])
    return pl.pallas_call(
        body, name="ssd_bwd", grid=(nc,), in_specs=in_specs,
        out_specs=[pl.BlockSpec((CHUNK, ZXD), lambda i: (rc(i), 0))] + [_acc(x.shape) for x in params],
        out_shape=[jax.ShapeDtypeStruct((T, ZXD), bf16)] + [jax.ShapeDtypeStruct(x.shape, f32) for x in params],
        scratch_shapes=[pltpu.VMEM((PAIRS, PAIR_W, N), f32), pltpu.VMEM((HALO, CONV_DIM), f32)],
        compiler_params=_params(("arbitrary",)))(xbc, xbc, z, dtr, sprev, dyb, *params, *mats)


def _tail(h3, p, tgt, gp, wpg, bpg, wpp, gf, tm=512):
    T, D = h3.shape
    tm = min(tm, T)

    def head(gpre, pp, h, gf_, t):
        gate = _sigmoid(gpre)
        y = _rms(h + gate * pp, gf_)
        err = y - t
        return 0.5 * jnp.sum(jnp.mean(err * err, axis=-1))

    def body(h_ref, p_ref, t_ref, gp_ref, wpg_ref, bpg_ref, wpp_ref, gf_ref,
             dh_ref, loss_ref, dgp_ref, dwpg_ref, dbpg_ref, dwpp_ref, dgf_ref):
        @pl.when(pl.program_id(0) == 0)
        def _():
            for r in (loss_ref, dgp_ref, dwpg_ref, dbpg_ref, dwpp_ref, dgf_ref):
                r[...] = jnp.zeros_like(r)

        h = h_ref[...]
        npf, np_vjp = jax.vjp(_rms, h, gp_ref[...])
        npb = npf.astype(bf16)
        pb = p_ref[...].astype(bf16)
        gpre = _dot(npb, wpg_ref[...]) + bpg_ref[...]
        kp, _, cp = wpp_ref.shape
        pp = jnp.concatenate([_dot(pb, wpp_ref[k]) for k in range(kp)], axis=1)
        loss, head_vjp = jax.vjp(head, gpre, pp, h, gf_ref[...], t_ref[...])
        dgpre, dpp, dh_a, dgf, _ = head_vjp(jnp.ones((), f32))
        loss_ref[...] += loss
        dgf_ref[...] += dgf
        dbpg_ref[...] += jnp.sum(dgpre, axis=0, keepdims=True)
        dgb = dgpre.astype(bf16)
        dwpg_ref[...] += _dot_tn(npb, dgb)
        dppb = dpp.astype(bf16)
        for k in range(kp):
            dwpp_ref[k] += _dot_tn(pb, dppb[:, k * cp:(k + 1) * cp])
        dh_b, dgp = np_vjp(_dot_nt(dgb, wpg_ref[...]))
        dgp_ref[...] += dgp
        dh_ref[...] = dh_a + dh_b

    ins = [h3, p, tgt, gp, wpg, bpg, wpp, gf]
    in_specs = [_rows(tm, D), _rows(tm, p.shape[1]), _rows(tm, D)] + [_full(x.shape) for x in ins[3:]]
    acc_shapes = [(1, LANES), gp.shape, wpg.shape, bpg.shape, wpp.shape, gf.shape]
    return pl.pallas_call(
        body, name="tail", grid=(T // tm,), in_specs=in_specs,
        out_specs=[_rows(tm, D)] + [_acc(s) for s in acc_shapes],
        out_shape=[jax.ShapeDtypeStruct((T, D), f32)] + [jax.ShapeDtypeStruct(s, f32) for s in acc_shapes],
        compiler_params=_params(("arbitrary",)))(*ins)


def _adamw(name, w, g, m, v, tr=256):
    R, C = w.shape
    tr = _row_tile(R, tr)

    def body(w_ref, g_ref, m_ref, v_ref, d_ref, mo_ref, vo_ref):
        g_ = g_ref[...]
        m_ = ADAM_B1 * m_ref[...] + (1.0 - ADAM_B1) * g_
        v_ = ADAM_B2 * v_ref[...] + (1.0 - ADAM_B2) * jnp.square(g_)
        m_hat = m_ / (1.0 - ADAM_B1 ** ADAM_STEP)
        v_hat = v_ / (1.0 - ADAM_B2 ** ADAM_STEP)
        d_ref[...] = -ADAM_LR * (m_hat / (jnp.sqrt(v_hat) + ADAM_EPS) + ADAM_WD * w_ref[...])
        mo_ref[...] = m_
        vo_ref[...] = v_

    spec = pl.BlockSpec((tr, C), lambda i: (i, 0))
    return pl.pallas_call(body, name=name, grid=(R // tr,), in_specs=[spec] * 4, out_specs=[spec] * 3,
                          out_shape=[jax.ShapeDtypeStruct((R, C), f32)] * 3,
                          compiler_params=_params(("parallel",)))(w, g, m, v)


HBM = pl.BlockSpec(memory_space=pltpu.HBM)


def _me():
    return lax.axis_index("x"), lax.axis_index("y"), lax.axis_index("c")


def _other_chips(x, y):
    return [(1 - x, y), (x, 1 - y), (1 - x, 1 - y)]


def _remote(src, dst, send_sem, recv_sem, dev):
    return pltpu.make_async_remote_copy(src_ref=src, dst_ref=dst, send_sem=send_sem, recv_sem=recv_sem,
                                        device_id=dev, device_id_type=MESH)


def _sems(n):
    return [pltpu.SemaphoreType.DMA((n,)), pltpu.SemaphoreType.DMA((n,))]


def _gather_weights(shards, split):
    n = len(shards)

    def body(*refs):
        ins, outs = refs[:n], refs[n:2 * n]
        own_send, own_recv, ici_send, ici_recv, d2d_send, d2d_recv = refs[2 * n:]
        x, y, c = _me()
        my_chip = 2 * x + y
        sibling = (x, y, 1 - c)
        chips = _other_chips(x, y)

        def rows(i, half):
            hr = shards[i].shape[0] // 2
            return pl.ds(half * hr, hr) if split[i] else pl.ds(0, shards[i].shape[0])

        sends = []
        for i in range(n):
            for j, chip in enumerate(chips):
                cp = _remote(ins[i].at[rows(i, c)], outs[i].at[my_chip, rows(i, c)],
                             ici_send.at[3 * i + j], ici_recv.at[3 * i + j], (*chip, c))
                cp.start()
                sends.append(cp)
            cp = _remote(ins[i], outs[i].at[my_chip], own_send.at[i], own_recv.at[i], sibling)
            cp.start()
            sends.append(cp)
        for i in range(n):
            for j, chip in enumerate(chips):
                s = 3 * i + j
                land = outs[i].at[2 * chip[0] + chip[1], rows(i, c)]
                _remote(land, land, ici_send.at[s], ici_recv.at[s], (*chip, c)).wait_recv()
                if split[i]:
                    cp = _remote(land, land, d2d_send.at[s], d2d_recv.at[s], sibling)
                    cp.start()
                    sends.append(cp)
        for i in range(n):
            _remote(ins[i], outs[i].at[my_chip], own_send.at[i], own_recv.at[i], sibling).wait_recv()
            if split[i]:
                for j, chip in enumerate(chips):
                    s = 3 * i + j
                    land = outs[i].at[2 * chip[0] + chip[1], rows(i, 1 - c)]
                    _remote(land, land, d2d_send.at[s], d2d_recv.at[s], sibling).wait_recv()
        for cp in sends:
            cp.wait_send()

    return pl.pallas_call(
        body, name="gather_weights", out_shape=[jax.ShapeDtypeStruct((N_CHIPS,) + s.shape, s.dtype) for s in shards],
        in_specs=[HBM] * n, out_specs=[HBM] * n,
        scratch_shapes=_sems(n) + _sems(3 * n) + _sems(3 * n))(*shards)


def _swap_halves(name, grads):
    n = len(grads)

    def body(*refs):
        ins, outs, send, recv = refs[:n], refs[n:2 * n], refs[2 * n], refs[2 * n + 1]
        x, y, c = _me()
        copies = []
        for i in range(n):
            hr = grads[i].shape[1] // 2
            cp = _remote(ins[i].at[:, pl.ds((1 - c) * hr, hr), :], outs[i], send.at[i], recv.at[i], (x, y, 1 - c))
            cp.start()
            copies.append(cp)
        for cp in copies:
            cp.wait()

    return pl.pallas_call(
        body, name=name,
        out_shape=[jax.ShapeDtypeStruct((g.shape[0], g.shape[1] // 2, g.shape[2]), g.dtype) for g in grads],
        in_specs=[HBM] * n, out_specs=[HBM] * n, scratch_shapes=_sems(n))(*grads)


def _add_halves(name, grads, other, c_idx, th=592):
    K, R, C = grads.shape
    H = R // 2
    th = _row_tile(H, th, 16)
    nb = H // th

    def body(c_ref, g_ref, o_ref, out_ref):
        out_ref[...] = (g_ref[...].astype(f32) + o_ref[...].astype(f32)).astype(bf16)

    grid_spec = pltpu.PrefetchScalarGridSpec(
        num_scalar_prefetch=1, grid=(nb,),
        in_specs=[pl.BlockSpec((K, th, C), lambda i, c: (0, c[0] * nb + i, 0)),
                  pl.BlockSpec((K, th, C), lambda i, c: (0, i, 0))],
        out_specs=pl.BlockSpec((K, th, C), lambda i, c: (0, i, 0)))
    return pl.pallas_call(body, name=name, grid_spec=grid_spec,
                          out_shape=jax.ShapeDtypeStruct((K, H, C), bf16),
                          compiler_params=_params(("parallel",)))(c_idx, grads, other)


SEM = pl.BlockSpec(memory_space=pltpu.SEMAPHORE)
ANY = pl.BlockSpec(memory_space=pl.ANY)
EFFECT = pltpu.SideEffectType.DATAFLOW_SIDE_EFFECTING


def _copies_start(name, srcs, land_shapes, n_copies, make_copies, after):
    ns, nl = len(srcs), len(land_shapes)
    lands = [lax.empty(s.shape, s.dtype) for s in land_shapes]

    def body(*refs):
        src_refs, land_refs = refs[:ns], refs[ns:ns + nl]
        send, recv, token = refs[ns + nl + 1], refs[ns + nl + 2], refs[-1]
        for cp in make_copies(src_refs, land_refs, send, recv):
            cp.start()
        token[...] = jnp.zeros_like(token)

    buffers = list(srcs) + lands
    out = pl.pallas_call(
        body, name=name,
        out_shape=(pltpu.SemaphoreType.DMA((n_copies,)), pltpu.SemaphoreType.DMA((n_copies,)),
                   *[pltpu.HBM(b.shape, b.dtype) for b in buffers], jax.ShapeDtypeStruct((8, LANES), f32)),
        in_specs=[HBM] * (ns + nl) + [ANY],
        out_specs=(SEM, SEM, *[HBM] * (ns + nl), pl.BlockSpec(memory_space=pltpu.VMEM)),
        input_output_aliases={i: 2 + i for i in range(ns + nl)},
        compiler_params=pltpu.CompilerParams(has_side_effects=EFFECT),
    )(*[pltpu.with_memory_space_constraint(b, pltpu.HBM) for b in buffers], after)
    return out[0], out[1], list(out[2:2 + ns]), list(out[2 + ns:2 + ns + nl]), out[-1]


def _copies_wait(name, started, make_copies, after):
    send, recv, srcs, lands, _ = started
    ns, nl = len(srcs), len(lands)
    after = list(after)

    def body(*refs):
        src_refs, land_refs = refs[:ns], refs[ns:ns + nl]
        for cp in make_copies(src_refs, land_refs, refs[ns + nl], refs[ns + nl + 1]):
            cp.wait_send()
            cp.wait_recv()

    buffers = list(srcs) + list(lands)
    out = pl.pallas_call(
        body, name=name, out_shape=tuple(pltpu.HBM(b.shape, b.dtype) for b in buffers),
        in_specs=[HBM] * (ns + nl) + [SEM, SEM] + [ANY] * len(after), out_specs=tuple([HBM] * (ns + nl)),
        input_output_aliases={i: i for i in range(ns + nl)},
        compiler_params=pltpu.CompilerParams(has_side_effects=EFFECT),
    )(*buffers, send, recv, *after)
    return list(out[:ns]), list(out[ns:])


def _gather_copies(src_refs, land_refs, send, recv):
    x, y, c = _me()
    my_chip = 2 * x + y
    peers = [(*chip, c) for chip in _other_chips(x, y)] + [(x, y, 1 - c)]
    return [_remote(src_refs[i], land_refs[i].at[my_chip], send.at[4 * i + j], recv.at[4 * i + j], peer)
            for i in range(len(src_refs)) for j, peer in enumerate(peers)]


def _partial_copies(src_refs, land_refs, send, recv):
    x, y, c = _me()
    return [_remote(src_refs[i].at[2 * chip[0] + chip[1]], land_refs[i].at[j], send.at[3 * i + j], recv.at[3 * i + j], (*chip, c))
            for i in range(len(src_refs)) for j, chip in enumerate(_other_chips(x, y))]


def _small_copies(src_refs, land_refs, send, recv):
    x, y, c = _me()
    return [_remote(src_refs[0], land_refs[0].at[k - 1], send.at[k - 1], recv.at[k - 1], (x ^ (k >> 2), y ^ ((k >> 1) & 1), c ^ (k & 1)))
            for k in range(1, N_DEV)]


def _sum_small(own, slots, dev_idx):
    R, C = own.shape

    def body(dev_ref, own_ref, s_ref, o_ref):
        me = dev_ref[0]
        acc = jnp.zeros((R, C), f32)
        for d in range(N_DEV):
            k = me ^ d
            acc = acc + jnp.where(k == 0, own_ref[...], s_ref[jnp.maximum(k - 1, 0)])
        o_ref[...] = acc

    grid_spec = pltpu.PrefetchScalarGridSpec(
        num_scalar_prefetch=1, grid=(1,),
        in_specs=[pl.BlockSpec((R, C), lambda i, dev: (0, 0)), pl.BlockSpec((N_DEV - 1, R, C), lambda i, dev: (0, 0, 0))],
        out_specs=pl.BlockSpec((R, C), lambda i, dev: (0, 0)))
    return pl.pallas_call(body, name="sum_small", grid_spec=grid_spec, out_shape=jax.ShapeDtypeStruct((R, C), f32),
                          compiler_params=_params(("arbitrary",)))(dev_idx, own, slots)


def _sum_partials(name, part, recv, chip_idx, th=592):
    K, H, C = part.shape
    th = _row_tile(H, th, 16)

    def body(chip_ref, p_ref, r_ref, o_ref):
        acc = p_ref[...].astype(f32)
        for j in range(3):
            acc = acc + r_ref[j].astype(f32)
        o_ref[...] = acc

    grid_spec = pltpu.PrefetchScalarGridSpec(
        num_scalar_prefetch=1, grid=(H // th,),
        in_specs=[pl.BlockSpec((None, th, C), lambda i, chip: (chip[0], i, 0)),
                  pl.BlockSpec((3, th, C), lambda i, chip: (0, i, 0))],
        out_specs=pl.BlockSpec((th, C), lambda i, chip: (i, 0)))
    return pl.pallas_call(body, name=name, grid_spec=grid_spec, out_shape=jax.ShapeDtypeStruct((H, C), f32),
                          compiler_params=_params(("parallel",)))(chip_idx, part, recv)


def _share_halves(name, halves):
    n = len(halves)

    def body(*refs):
        ins, outs, send, recv = refs[:n], refs[n:2 * n], refs[2 * n], refs[2 * n + 1]
        x, y, c = _me()
        copies = []
        for i in range(n):
            cp = _remote(ins[i], outs[i], send.at[i], recv.at[i], (x, y, 1 - c))
            cp.start()
            copies.append(cp)
        for cp in copies:
            cp.wait()

    return pl.pallas_call(
        body, name=name, out_shape=[jax.ShapeDtypeStruct(h.shape, h.dtype) for h in halves],
        in_specs=[HBM] * n, out_specs=[HBM] * n, scratch_shapes=_sems(n))(*halves)


def _adamw_big(name, w, g_mine, g_theirs, m, v, c_idx, tr=320):
    R, C = w.shape
    H = R // 2
    tr = _row_tile(H, tr)
    nb = H // tr

    def body(c_ref, w_ref, gm_ref, gt_ref, m_ref, v_ref, g_ref, d_ref, mo_ref, vo_ref):
        g_ = jnp.where(pl.program_id(0) // nb == c_ref[0], gm_ref[...], gt_ref[...])
        g_ref[...] = g_
        m_ = ADAM_B1 * m_ref[...] + (1.0 - ADAM_B1) * g_
        v_ = ADAM_B2 * v_ref[...] + (1.0 - ADAM_B2) * jnp.square(g_)
        m_hat = m_ / (1.0 - ADAM_B1 ** ADAM_STEP)
        v_hat = v_ / (1.0 - ADAM_B2 ** ADAM_STEP)
        d_ref[...] = -ADAM_LR * (m_hat / (jnp.sqrt(v_hat) + ADAM_EPS) + ADAM_WD * w_ref[...])
        mo_ref[...] = m_
        vo_ref[...] = v_

    full = pl.BlockSpec((tr, C), lambda i, c: (i, 0))
    half = pl.BlockSpec((tr, C), lambda i, c: (i % nb, 0))
    grid_spec = pltpu.PrefetchScalarGridSpec(num_scalar_prefetch=1, grid=(2 * nb,),
                                             in_specs=[full, half, half, full, full], out_specs=[full] * 4)
    return pl.pallas_call(body, name=name, grid_spec=grid_spec, out_shape=[jax.ShapeDtypeStruct((R, C), f32)] * 4,
                          compiler_params=_params(("parallel",)))(c_idx, w, g_mine, g_theirs, m, v)


BIG = ("ffn1_w_gate", "ffn1_w_up", "ffn1_w_down", "w_in", "w_out", "ffn2_w_gate", "ffn2_w_up", "ffn2_w_down",
       "ple_w_gate", "ple_w_proj")


SMALL = ("ffn1_norm", "mix_norm", "gm_ln_g", "gm_ln_b", "gm_w_s", "gm_b_s", "gm_out_norm", "conv_b", "dt_bias", "a_log",
         "d_skip", "ssm_norm", "ffn2_norm", "ple_norm", "ple_b_gate", "final_norm")
SMALL_C = 1024


def _pack_small(vals):
    parts = []
    for v in vals:
        f = v.astype(f32).reshape(-1)
        parts.append(jnp.pad(f, (0, -f.shape[0] % SMALL_C)))
    flat = jnp.concatenate(parts)
    rows = flat.shape[0] // SMALL_C
    return jnp.pad(flat, (0, (-rows % 8) * SMALL_C)).reshape(-1, SMALL_C)


def _unpack_small(pack, shapes):
    flat = pack.reshape(-1)
    out, off = [], 0
    for s in shapes:
        n = 1
        for d in s:
            n *= d
        out.append(flat[off:off + n].reshape(s))
        off += n + (-n % SMALL_C)
    return out


def _pad_lanes(v):
    return jnp.pad(v, ((0, 0), (0, LANES - v.shape[1])))


def _pad_rows(a):
    pad = [(0, 0)] * a.ndim
    pad[-2] = (0, -a.shape[-2] % ROW_PAD)
    return jnp.pad(a, pad) if pad[-2][1] else a


FETCH = (("ffn1_w_gate", "ffn1_w_up", "ffn1_w_down"), ("w_in", "conv_w", "w_out"),
         ("ffn2_w_gate", "ffn2_w_up", "ffn2_w_down", "ple_w_gate", "ple_w_proj"))
TRANSPOSED = ("ffn1_w_gate", "ffn1_w_up", "ffn2_w_gate", "ffn2_w_up", "w_in")
ROW_PAD = 32
DONE = (("ffn2_w_gate", "ffn2_w_up", "ffn2_w_down", "w_out", "ple_w_gate", "ple_w_proj"), ("w_in",),
        ("ffn1_w_gate", "ffn1_w_up", "ffn1_w_down"))


def _local_step(x, p, tgt, fetch, S, on_grads):
    G = GM_WIDTH
    K = N_CHIPS
    b_st = S["gm_b_s"][0].T
    w_s = S["gm_w_s"][0]
    dtb, alog, dsk = _pad_lanes(S["dt_bias"]), _pad_lanes(S["a_log"]), _pad_lanes(S["d_skip"])
    gfin = S["final_norm"].reshape(1, -1)

    wg1, wu1, wd1 = fetch(0, None)
    h1, n1, a1, b1 = _ffn_fwd("ffn1_fwd", x, S["ffn1_norm"], wg1, wu1, wd1)
    w_in4, cw4, wo4 = fetch(1, h1)
    w_in = w_in4.reshape(IN_PROJ, D_MODEL)
    w_uv = w_in[:2 * G]
    w_zxd = jnp.pad(w_in[2 * G:], ((0, ZXD - (IN_PROJ - 2 * G)), (0, 0)))
    conv_w = jnp.transpose(cw4, (1, 0, 2)).reshape(SSM_CONV, CONV_DIM)
    wo = wo4.reshape(-1, D_MODEL)
    n2, uv, z, xbc, dtr, ya = _mix_fwd(h1, S["mix_norm"], w_uv, w_zxd, S["gm_ln_g"], S["gm_ln_b"], w_s, b_st, S["gm_out_norm"])
    yb, sprev = _ssd_fwd(xbc, z, dtr, conv_w, S["conv_b"], dtb, alog, dsk, S["ssm_norm"])
    wg2, wu2, wd2, wpg4, wpp4 = fetch(2, yb)
    h2, h3, n3, a2, b2 = _ffn_fwd("ffn2_fwd", h1, S["ffn2_norm"], wg2, wu2, wd2, pre=(ya, yb, wo))
    dh3, loss, dgp, dwpg, dbpg, dwpp, dgf = _tail(h3, p, tgt, S["ple_norm"], wpg4.reshape(-1, D_MODEL), S["ple_b_gate"], wpp4, gfin)
    dh2, da2, db2, hm2, dg_ffn2, dya, dyb = _ffn_bwd("ffn2_bwd", dh3, h2, S["ffn2_norm"], a2, b2, wg2, wu2, wd2, wo=wo, ga=G)
    dw_out = jnp.concatenate([_matmul_tn("dw_out_a", ya, dh2), _matmul_tn("dw_out_b", yb, dh2)], axis=0).reshape(wo4.shape)
    zero = on_grads(0, [_matmul_tn("dw_ffn2_gate", da2, n3), _matmul_tn("dw_ffn2_up", db2, n3),
                        _matmul_tn("dw_ffn2_down", hm2, dh3, scale=0.5), dw_out,
                        dwpg.astype(bf16).reshape(wpg4.shape), dwpp.astype(bf16)])
    duv, dlng, dlnb, dws, dbst, dgout = _gm_bwd(uv, dya, S["gm_ln_g"], S["gm_ln_b"], w_s, b_st, S["gm_out_norm"] + zero)
    dzxd, dcw, dcb, ddtb, dalog, ddsk, dgssm = _ssd_bwd(xbc, z, dtr, sprev, dyb, conv_w, S["conv_b"], dtb, alog, dsk,
                                                        S["ssm_norm"] + zero)
    dh1, dg_mix = _mix_bwd(dh2, h1, S["mix_norm"], duv, dzxd, w_uv, w_zxd)
    dw_in = jnp.concatenate([_matmul_tn("dw_in_uv", duv, n2), _matmul_tn("dw_in_zxd", dzxd, n2)[:IN_PROJ - 2 * G]], axis=0)
    zero = on_grads(1, [dw_in.reshape(w_in4.shape)])
    dx, da1, db1, hm1, dg_ffn1 = _ffn_bwd("ffn1_bwd", dh1, x, S["ffn1_norm"] + zero, a1, b1, wg1, wu1, wd1)
    zero = on_grads(2, [_matmul_tn("dw_ffn1_gate", da1, n1), _matmul_tn("dw_ffn1_up", db1, n1),
                        _matmul_tn("dw_ffn1_down", hm1, dh1, scale=0.5)])
    loss = loss + zero
    nh = SSM_HEADS
    gS = {"ffn1_norm": dg_ffn1, "mix_norm": dg_mix, "gm_ln_g": dlng, "gm_ln_b": dlnb, "gm_w_s": dws[None], "gm_b_s": dbst.T[None],
          "gm_out_norm": dgout, "conv_b": dcb, "dt_bias": ddtb[:, :nh], "a_log": dalog[:, :nh], "d_skip": ddsk[:, :nh],
          "ssm_norm": dgssm, "ffn2_norm": dg_ffn2, "ple_norm": dgp, "ple_b_gate": dbpg, "final_norm": dgf.reshape(-1)}
    return loss, dx, dcw, gS


_WEIGHTS = ("ffn1_norm", "ffn1_w_gate", "ffn1_w_up", "ffn1_w_down", "mix_norm", "w_in", "gm_ln_g", "gm_ln_b", "gm_w_s", "gm_b_s",
            "gm_out_norm", "conv_w", "conv_b", "dt_bias", "a_log", "d_skip", "ssm_norm", "w_out", "ffn2_norm", "ffn2_w_gate",
            "ffn2_w_up", "ffn2_w_down", "ple_norm", "ple_w_gate", "ple_b_gate", "ple_w_proj", "final_norm")
_BIG_NAMES = BIG


def kernel(x, p, ffn1_norm, ffn1_w_gate, ffn1_w_up, ffn1_w_down, mix_norm, w_in, gm_ln_g, gm_ln_b, gm_w_s, gm_b_s, gm_out_norm, conv_w, conv_b, dt_bias, a_log, d_skip, ssm_norm, w_out, ffn2_norm, ffn2_w_gate, ffn2_w_up, ffn2_w_down, ple_norm, ple_w_gate, ple_b_gate, ple_w_proj, final_norm, loss_target, m_ffn1_norm, m_ffn1_w_gate, m_ffn1_w_up, m_ffn1_w_down, m_mix_norm, m_w_in, m_gm_ln_g, m_gm_ln_b, m_gm_w_s, m_gm_b_s, m_gm_out_norm, m_conv_w, m_conv_b, m_dt_bias, m_a_log, m_d_skip, m_ssm_norm, m_w_out, m_ffn2_norm, m_ffn2_w_gate, m_ffn2_w_up, m_ffn2_w_down, m_ple_norm, m_ple_w_gate, m_ple_b_gate, m_ple_w_proj, m_final_norm, v_ffn1_norm, v_ffn1_w_gate, v_ffn1_w_up, v_ffn1_w_down, v_mix_norm, v_w_in, v_gm_ln_g, v_gm_ln_b, v_gm_w_s, v_gm_b_s, v_gm_out_norm, v_conv_w, v_conv_b, v_dt_bias, v_a_log, v_d_skip, v_ssm_norm, v_w_out, v_ffn2_norm, v_ffn2_w_gate, v_ffn2_w_up, v_ffn2_w_down, v_ple_norm, v_ple_w_gate, v_ple_b_gate, v_ple_w_proj, v_final_norm):
    given = dict(locals())
    w = {n: given[n] for n in _WEIGHTS}
    m = {n: given["m_" + n] for n in _WEIGHTS}
    v = {n: given["v_" + n] for n in _WEIGHTS}

    c_idx = lax.axis_index("c").astype(jnp.int32).reshape(1)
    chip = 2 * lax.axis_index("x") + lax.axis_index("y")
    chip_idx = chip.astype(jnp.int32).reshape(1)

    shard = {n: w[n][0].astype(bf16) for n in BIG}
    shard["w_in"] = jnp.swapaxes(w["w_in"][0], 0, 1).astype(bf16)
    shard["conv_w"] = w["conv_w"][0]
    first = _gather_weights([shard[n] for n in FETCH[0]], [True] * len(FETCH[0]))
    fetching, after = [], first[-1]
    for k in (1, 2):
        srcs = [shard[n] for n in FETCH[k]]
        lands = [jax.ShapeDtypeStruct((N_CHIPS,) + s.shape, s.dtype) for s in srcs]
        fetching.append(_copies_start("gather%d_start" % k, srcs, lands, 4 * len(srcs), _gather_copies, after))
        after = fetching[-1][4]

    def fetch(k, after_):
        return first if k == 0 else _copies_wait("gather%d_wait" % k, fetching[k - 1], _gather_copies, [after_])[1]

    exchanging = []

    def on_grads(k, grads):
        grads = [_pad_rows(g_) for g_ in grads]
        others = _swap_halves("swap%d" % k, grads)
        parts = [_add_halves("add_" + n, g_, o_, c_idx) for n, g_, o_ in zip(DONE[k], grads, others)]
        lands = [jax.ShapeDtypeStruct((3,) + p_.shape[1:], p_.dtype) for p_ in parts]
        exchanging.append(_copies_start("exchange%d_start" % k, parts, lands, 3 * len(parts), _partial_copies, c_idx))
        return exchanging[-1][4][0, 0]

    S = {n: w[n] for n in SMALL}
    S["ffn1_norm"] = S["ffn1_norm"] + after[0, 0]
    loss, dx, dcw, gS = _local_step(x[0], p[0, 0], loss_target[0], fetch, S, on_grads)

    small = _pack_small([gS[n] for n in SMALL] + [dcw, loss[:, :1]])
    small_lands = [jax.ShapeDtypeStruct((N_DEV - 1,) + small.shape, small.dtype)]
    small_st = _copies_start("small_start", [small], small_lands, N_DEV - 1, _small_copies, c_idx)

    g, delta, new_m, new_v = {}, {}, {}, {}
    after = [small_st[4]]
    for k in range(len(DONE)):
        parts, recv = _copies_wait("exchange%d_wait" % k, exchanging[k], _partial_copies, after)
        mine = [_sum_partials("sum_" + n, p_, r_, chip_idx) for n, p_, r_ in zip(DONE[k], parts, recv)]
        theirs = _share_halves("share%d" % k, mine)
        after = []
        for n, gm_, gt_ in zip(DONE[k], mine, theirs):
            flip = (lambda a: jnp.swapaxes(a, 0, 1)) if n in TRANSPOSED else (lambda a: a)
            rows = flip(w[n][0]).shape[0]
            w_, m_, v_ = [_pad_rows(flip(a[n][0])) for a in (w, m, v)]
            outs = _adamw_big("adamw_" + n, w_, gm_, gt_, m_, v_, c_idx)
            g[n], delta[n], new_m[n], new_v[n] = [flip(o[:rows])[None] for o in outs]
            after.append(outs[3])
    (own,), (slots,) = _copies_wait("small_wait", small_st, _small_copies, after)
    dev_idx = (2 * chip + lax.axis_index("c")).astype(jnp.int32).reshape(1)
    small_shapes = [w[n].shape for n in SMALL] + [dcw.shape, (1, 1)]
    small_sum = _unpack_small(_sum_small(own, slots, dev_idx), small_shapes)
    g.update({n: small_sum[i] for i, n in enumerate(SMALL)})
    cshard = w["conv_w"].shape[2]
    g["conv_w"] = lax.dynamic_slice_in_dim(small_sum[len(SMALL)], chip * cshard, cshard, axis=1)[None]
    loss_total = small_sum[len(SMALL) + 1].reshape(())
    sm_names = SMALL + ("conv_w",)
    sm_shapes = [w[n].shape for n in sm_names]
    d_s, m_s, v_s = _adamw("adamw_small", _pack_small([w[n] for n in sm_names]), _pack_small([g[n] for n in sm_names]),
                           _pack_small([m[n] for n in sm_names]), _pack_small([v[n] for n in sm_names]))
    for dst, src in ((delta, d_s), (new_m, m_s), (new_v, v_s)):
        for n, val in zip(sm_names, _unpack_small(src, sm_shapes)):
            dst[n] = val

    return (loss_total, dx[None], *[g[n] for n in _WEIGHTS], *[delta[n] for n in _WEIGHTS],
            *[new_m[n] for n in _WEIGHTS], *[new_v[n] for n in _WEIGHTS])
```

```python
import functools

import jax
import jax.numpy as jnp
from jax import lax
from jax.experimental import pallas as pl
from jax.experimental.pallas import tpu as pltpu

f32 = jnp.float32
bf16 = jnp.bfloat16
MESH = pl.DeviceIdType.MESH
HIGHEST = lax.Precision.HIGHEST

EPS = 1e-6
N_CHIPS = 4
N_DEV = 8
D_MODEL = 1024
D_FF = 2816
D_PLE = 256
GM_WIDTH = 1024
GM_HEADS = 8
CHUNK = 128
SSM_WIDTH = 1024
SSM_HEADS = 16
SSM_HEAD_DIM = 64
SSM_GROUPS = 2
SSM_STATE = 128
SSM_CONV = 4
CONV_DIM = SSM_WIDTH + 2 * SSM_GROUPS * SSM_STATE
IN_PROJ = 2 * GM_WIDTH + SSM_WIDTH + CONV_DIM + SSM_HEADS
LANES = 128
ZXD = SSM_WIDTH + CONV_DIM + LANES

ADAM_LR = 0.001
ADAM_B1 = 0.9
ADAM_B2 = 0.999
ADAM_EPS = 1e-08
ADAM_WD = 0.01
ADAM_STEP = 10

VMEM_LIMIT = 56 * 1024 * 1024


def _dot(a, b):
    return jnp.dot(a, b, preferred_element_type=f32)


def _dot_nt(a, b):
    return lax.dot_general(a, b, (((1,), (1,)), ((), ())), preferred_element_type=f32)


def _dot_tn(a, b):
    return lax.dot_general(a, b, (((0,), (0,)), ((), ())), preferred_element_type=f32)


def _rms(x, g):
    return x * lax.rsqrt(jnp.mean(x * x, axis=-1, keepdims=True) + EPS) * g


def _gelu(x):
    return 0.5 * x * (1.0 + lax.erf(x * 0.7071067811865476))


def _layernorm(x, g, b):
    mu = jnp.mean(x, axis=-1, keepdims=True)
    xc = x - mu
    return xc * lax.rsqrt(jnp.mean(xc * xc, axis=-1, keepdims=True) + EPS) * g + b


def _sigmoid(x):
    return 1.0 / (1.0 + jnp.exp(-x))


def _softplus(x):
    return jnp.maximum(x, 0.0) + jnp.log(1.0 + jnp.exp(-jnp.abs(x)))


def _full(shape):
    nd = len(shape)
    return pl.BlockSpec(shape, lambda *_: (0,) * nd, pipeline_mode=pl.Buffered(1))


def _acc(shape):
    nd = len(shape)
    return pl.BlockSpec(shape, lambda *_: (0,) * nd)


def _rows(tm, ncols):
    return pl.BlockSpec((tm, ncols), lambda i: (i, 0))


def _params(sem):
    return pltpu.CompilerParams(dimension_semantics=sem, vmem_limit_bytes=VMEM_LIMIT)


def _row_tile(rows, target, mult=8):
    best = rows
    for t in range(mult, min(rows, target) + 1, mult):
        if rows % t == 0:
            best = t
    return best if best <= target else rows


FC = 256


def _ffn_fwd(name, h, g, wg, wu, wd, pre=None, tm=256):
    T, D = h.shape
    F = wg.shape[0]
    tm = min(tm, T)

    def body(*refs):
        if pre is None:
            h_ref, g_ref, wg_ref, wu_ref, wd_ref, ho_ref, n_ref, a_ref, b_ref = refs
            hin = h_ref[...]
        else:
            (h_ref, ya_ref, yb_ref, wo_ref, g_ref, wg_ref, wu_ref, wd_ref,
             hi_ref, ho_ref, n_ref, a_ref, b_ref) = refs
            ga = ya_ref.shape[1]
            hin = h_ref[...] + _dot(ya_ref[...], wo_ref[:ga, :]) + _dot(yb_ref[...], wo_ref[ga:, :])
            hi_ref[...] = hin
        n = _rms(hin, g_ref[...]).astype(bf16)
        n_ref[...] = n
        acc = jnp.zeros((tm, D), f32)
        for c in range(F // FC):
            rs = slice(c * FC, (c + 1) * FC)
            a = _dot_nt(n, wg_ref[rs, :]).astype(bf16)
            b = _dot_nt(n, wu_ref[rs, :]).astype(bf16)
            a_ref[:, rs] = a
            b_ref[:, rs] = b
            af = a.astype(f32)
            hm = (af * _sigmoid(af) * b.astype(f32)).astype(bf16)
            acc = acc + _dot(hm, wd_ref[rs, :])
        ho_ref[...] = hin + 0.5 * acc

    ins = [h] + (list(pre) if pre is not None else []) + [g, wg, wu, wd]
    in_specs = [_rows(tm, D)]
    if pre is not None:
        in_specs += [_rows(tm, pre[0].shape[1]), _rows(tm, pre[1].shape[1]), _full(pre[2].shape)]
    in_specs += [_full(g.shape), _full(wg.shape), _full(wu.shape), _full(wd.shape)]
    outs = [jax.ShapeDtypeStruct((T, D), f32), jax.ShapeDtypeStruct((T, D), bf16),
            jax.ShapeDtypeStruct((T, F), bf16), jax.ShapeDtypeStruct((T, F), bf16)]
    out_specs = [_rows(tm, D), _rows(tm, D), _rows(tm, F), _rows(tm, F)]
    if pre is not None:
        outs = [jax.ShapeDtypeStruct((T, D), f32)] + outs
        out_specs = [_rows(tm, D)] + out_specs
    return pl.pallas_call(body, name=name, grid=(T // tm,), in_specs=in_specs, out_specs=out_specs,
                          out_shape=outs, compiler_params=_params(("parallel",)))(*ins)


def _ffn_bwd(name, dh, hin, g, a, b, wg, wu, wd, wo=None, ga=0, tm=256):
    T, D = dh.shape
    F = wg.shape[0]
    tm = min(tm, T)

    def body(*refs):
        if wo is None:
            (dh_ref, hin_ref, g_ref, a_ref, b_ref, wg_ref, wu_ref, wd_ref,
             dhi_ref, da_ref, db_ref, hm_ref, dg_ref) = refs
        else:
            (dh_ref, hin_ref, g_ref, a_ref, b_ref, wg_ref, wu_ref, wd_ref, wo_ref,
             dhi_ref, da_ref, db_ref, hm_ref, dg_ref, dya_ref, dyb_ref) = refs

        @pl.when(pl.program_id(0) == 0)
        def _():
            dg_ref[...] = jnp.zeros_like(dg_ref)

        dh_ = dh_ref[...]
        dhb = (0.5 * dh_).astype(bf16)
        dn = jnp.zeros((tm, D), f32)
        for c in range(F // FC):
            rs = slice(c * FC, (c + 1) * FC)
            dhm = _dot_nt(dhb, wd_ref[rs, :])
            af = a_ref[:, rs].astype(f32)
            bf = b_ref[:, rs].astype(f32)
            sg = _sigmoid(af)
            sl_ = af * sg
            da = (dhm * bf * (sg * (1.0 + af * (1.0 - sg)))).astype(bf16)
            db = (dhm * sl_).astype(bf16)
            da_ref[:, rs] = da
            db_ref[:, rs] = db
            hm_ref[:, rs] = (sl_ * bf).astype(bf16)
            dn = dn + _dot(da, wg_ref[rs, :]) + _dot(db, wu_ref[rs, :])
        _, vjp = jax.vjp(_rms, hin_ref[...], g_ref[...])
        dx, dg = vjp(dn)
        dhi = dh_ + dx
        dhi_ref[...] = dhi
        dg_ref[...] += dg
        if wo is not None:
            dhib = dhi.astype(bf16)
            dya_ref[...] = _dot_nt(dhib, wo_ref[:ga, :]).astype(bf16)
            dyb_ref[...] = _dot_nt(dhib, wo_ref[ga:, :]).astype(bf16)

    ins = [dh, hin, g, a, b, wg, wu, wd]
    in_specs = [_rows(tm, D), _rows(tm, D), _full(g.shape), _rows(tm, F), _rows(tm, F),
                _full(wg.shape), _full(wu.shape), _full(wd.shape)]
    act = jax.ShapeDtypeStruct((T, F), bf16)
    outs = [jax.ShapeDtypeStruct((T, D), f32), act, act, act, jax.ShapeDtypeStruct(g.shape, f32)]
    out_specs = [_rows(tm, D), _rows(tm, F), _rows(tm, F), _rows(tm, F), _acc(g.shape)]
    if wo is not None:
        gb = wo.shape[0] - ga
        ins += [wo]
        in_specs += [_full(wo.shape)]
        outs += [jax.ShapeDtypeStruct((T, ga), bf16), jax.ShapeDtypeStruct((T, gb), bf16)]
        out_specs += [_rows(tm, ga), _rows(tm, gb)]
    return pl.pallas_call(body, name=name, grid=(T // tm,), in_specs=in_specs, out_specs=out_specs,
                          out_shape=outs, compiler_params=_params(("arbitrary",)))(*ins)


def _matmul_tn(name, a, b, scale=1.0, tk=1024):
    T, M = a.shape
    N = b.shape[1]
    tk = min(tk, T)
    nk = T // tk
    tn = LANES * max(d for d in range(1, N // LANES + 1) if (N // LANES) % d == 0 and (d == 1 or M * d * LANES * 4 <= 6 * 1024 * 1024))

    def body(a_ref, b_ref, o_ref, acc):
        k = pl.program_id(1)

        @pl.when(k == 0)
        def _():
            acc[...] = jnp.zeros_like(acc)

        bb = b_ref[...]
        if scale != 1.0:
            bb = bb * scale
        acc[...] += _dot_tn(a_ref[...].astype(bf16), bb.astype(bf16))

        @pl.when(k == nk - 1)
        def _():
            o_ref[...] = acc[...].astype(bf16)

    return pl.pallas_call(
        body, name=name, grid=(N // tn, nk),
        in_specs=[pl.BlockSpec((tk, M), lambda j, k: (k, 0)), pl.BlockSpec((tk, tn), lambda j, k: (k, j))],
        out_specs=pl.BlockSpec((M, tn), lambda j, k: (0, j)),
        out_shape=jax.ShapeDtypeStruct((M, N), bf16), scratch_shapes=[pltpu.VMEM((M, tn), f32)],
        compiler_params=_params(("parallel", "arbitrary")))(a, b)


def _gm_pre(u, v, ln_g, ln_b):
    return _gelu(u), _layernorm(_gelu(v), ln_g, ln_b)


def _tril_mask():
    r = lax.broadcasted_iota(jnp.int32, (CHUNK, CHUNK), 0)
    c = lax.broadcasted_iota(jnp.int32, (CHUNK, CHUNK), 1)
    return c <= r


def _gm_mix(vnb, ws_ref, bst, mixed_sc, tm):
    mask = _tril_mask()
    for h in range(GM_HEADS):
        wt = jnp.where(mask, ws_ref[h], 0.0).astype(bf16)
        bias = bst[:, h:h + 1]
        for q in range(tm // CHUNK):
            rs = slice(q * CHUNK, (q + 1) * CHUNK)
            cs = slice(h * CHUNK, (h + 1) * CHUNK)
            mixed_sc[rs, cs] = _dot(wt, vnb[rs, cs]) + bias


def _mix_fwd(h1, gmix, w_uv, w_zxd, ln_g, ln_b, w_s, b_st, gout, tm=512):
    T, D = h1.shape
    tm = min(tm, T)
    G = GM_WIDTH

    def body(h_ref, g_ref, wuv_ref, wzxd_ref, lng_ref, lnb_ref, ws_ref, bst_ref, gout_ref,
             n_ref, uv_ref, z_ref, xbc_ref, dt_ref, ya_ref, mixed_sc):
        n = _rms(h_ref[...], g_ref[...]).astype(bf16)
        n_ref[...] = n
        u = _dot_nt(n, wuv_ref[:G, :]).astype(bf16)
        v = _dot_nt(n, wuv_ref[G:, :]).astype(bf16)
        uv_ref[:, :G] = u
        uv_ref[:, G:] = v
        z_ref[...] = _dot_nt(n, wzxd_ref[:SSM_WIDTH, :]).astype(bf16)
        xbc_ref[...] = _dot_nt(n, wzxd_ref[SSM_WIDTH:SSM_WIDTH + CONV_DIM, :]).astype(bf16)
        dt_ref[...] = _dot_nt(n, wzxd_ref[SSM_WIDTH + CONV_DIM:, :])
        ug, vn = _gm_pre(u.astype(f32), v.astype(f32), lng_ref[...], lnb_ref[...])
        _gm_mix(vn.astype(bf16), ws_ref, bst_ref[...], mixed_sc, tm)
        ya_ref[...] = _rms(ug * mixed_sc[...], gout_ref[...]).astype(bf16)

    ins = [h1, gmix, w_uv, w_zxd, ln_g, ln_b, w_s, b_st, gout]
    in_specs = [_rows(tm, D)] + [_full(x.shape) for x in ins[1:]]
    outs = [jax.ShapeDtypeStruct((T, D), bf16), jax.ShapeDtypeStruct((T, 2 * G), bf16),
            jax.ShapeDtypeStruct((T, SSM_WIDTH), bf16), jax.ShapeDtypeStruct((T, CONV_DIM), bf16),
            jax.ShapeDtypeStruct((T, LANES), f32), jax.ShapeDtypeStruct((T, G), bf16)]
    out_specs = [_rows(tm, D), _rows(tm, 2 * G), _rows(tm, SSM_WIDTH), _rows(tm, CONV_DIM), _rows(tm, LANES), _rows(tm, G)]
    return pl.pallas_call(body, name="mix_fwd", grid=(T // tm,), in_specs=in_specs, out_specs=out_specs,
                          out_shape=outs, scratch_shapes=[pltpu.VMEM((tm, G), f32)],
                          compiler_params=_params(("parallel",)))(*ins)


def _gm_bwd(uv, dya, ln_g, ln_b, w_s, b_st, gout, tm=256):
    T = uv.shape[0]
    tm = min(tm, T)
    G = GM_WIDTH

    def body(uv_ref, dya_ref, lng_ref, lnb_ref, ws_ref, bst_ref, gout_ref,
             duv_ref, dlng_ref, dlnb_ref, dws_ref, dbst_ref, dgout_ref, mixed_sc, dvn_sc):
        @pl.when(pl.program_id(0) == 0)
        def _():
            for r in (dlng_ref, dlnb_ref, dws_ref, dbst_ref, dgout_ref):
                r[...] = jnp.zeros_like(r)

        u = uv_ref[:, :G].astype(f32)
        v = uv_ref[:, G:].astype(f32)
        (ug, vn), pre_vjp = jax.vjp(_gm_pre, u, v, lng_ref[...], lnb_ref[...])
        vnb = vn.astype(bf16)
        _gm_mix(vnb, ws_ref, bst_ref[...], mixed_sc, tm)
        mixed = mixed_sc[...]
        _, out_vjp = jax.vjp(_rms, ug * mixed, gout_ref[...])
        dpre, dgout = out_vjp(dya_ref[...].astype(f32))
        dgout_ref[...] += dgout
        dug = dpre * mixed
        dmixed = dpre * ug
        mask = _tril_mask()
        lane = lax.broadcasted_iota(jnp.int32, (1, GM_HEADS), 1)
        dbst = jnp.zeros((CHUNK, GM_HEADS), f32)
        for h in range(GM_HEADS):
            wt = jnp.where(mask, ws_ref[h], 0.0).astype(bf16)
            cs = slice(h * CHUNK, (h + 1) * CHUNK)
            dw = jnp.zeros((CHUNK, CHUNK), f32)
            for q in range(tm // CHUNK):
                rs = slice(q * CHUNK, (q + 1) * CHUNK)
                dm = dmixed[rs, cs]
                dmb = dm.astype(bf16)
                dw = dw + _dot_nt(dmb, vnb[rs, cs])
                dbst = dbst + jnp.sum(dm, axis=1, keepdims=True) * (lane == h).astype(f32)
                dvn_sc[rs, cs] = _dot_tn(wt, dmb)
            dws_ref[h] += jnp.where(mask, dw, 0.0)
        dbst_ref[...] += dbst
        du, dv, dlng, dlnb = pre_vjp((dug, dvn_sc[...]))
        duv_ref[:, :G] = du.astype(bf16)
        duv_ref[:, G:] = dv.astype(bf16)
        dlng_ref[...] += dlng
        dlnb_ref[...] += dlnb

    ins = [uv, dya, ln_g, ln_b, w_s, b_st, gout]
    in_specs = [_rows(tm, 2 * G), _rows(tm, G)] + [_full(x.shape) for x in ins[2:]]
    outs = [jax.ShapeDtypeStruct((T, 2 * G), bf16)] + [jax.ShapeDtypeStruct(x.shape, f32) for x in (ln_g, ln_b, w_s, b_st, gout)]
    out_specs = [_rows(tm, 2 * G)] + [_acc(x.shape) for x in (ln_g, ln_b, w_s, b_st, gout)]
    return pl.pallas_call(body, name="gm_bwd", grid=(T // tm,), in_specs=in_specs, out_specs=out_specs,
                          out_shape=outs, scratch_shapes=[pltpu.VMEM((tm, G), f32), pltpu.VMEM((tm, G), f32)],
                          compiler_params=_params(("arbitrary",)))(*ins)


def _mix_bwd(dh, h1, gmix, duv, dzxd, w_uv, w_zxd, tm=512):
    T, D = dh.shape
    tm = min(tm, T)

    def body(dh_ref, h_ref, g_ref, duv_ref, dzxd_ref, wuv_ref, wzxd_ref, dhi_ref, dg_ref):
        @pl.when(pl.program_id(0) == 0)
        def _():
            dg_ref[...] = jnp.zeros_like(dg_ref)

        dn = _dot(duv_ref[...], wuv_ref[...]) + _dot(dzxd_ref[...], wzxd_ref[...])
        _, vjp = jax.vjp(_rms, h_ref[...], g_ref[...])
        dx, dg = vjp(dn)
        dhi_ref[...] = dh_ref[...] + dx
        dg_ref[...] += dg

    ins = [dh, h1, gmix, duv, dzxd, w_uv, w_zxd]
    in_specs = [_rows(tm, D), _rows(tm, D), _full(gmix.shape), _rows(tm, duv.shape[1]), _rows(tm, dzxd.shape[1]),
                _full(w_uv.shape), _full(w_zxd.shape)]
    return pl.pallas_call(body, name="mix_bwd", grid=(T // tm,), in_specs=in_specs,
                          out_specs=[_rows(tm, D), _acc(gmix.shape)],
                          out_shape=[jax.ShapeDtypeStruct((T, D), f32), jax.ShapeDtypeStruct(gmix.shape, f32)],
                          compiler_params=_params(("arbitrary",)))(*ins)


HALO = 16
PAIRS = SSM_HEADS // 2
PAIR_W = 2 * SSM_HEAD_DIM


def _split(x, n):
    parts = []
    for _ in range(n):
        p = x.astype(bf16)
        parts.append(p)
        x = x - p.astype(f32)
    return parts


def _dot_sel(x, sel_n, n):
    return _dot(jnp.concatenate(_split(x, n), axis=1), sel_n)


def _sel_dot(sel, x, n):
    return _dot(jnp.concatenate([sel] * n, axis=1), jnp.concatenate(_split(x, n), axis=0))


EXPAND_SPLIT = 3
REDUCE_SPLIT = 2


def _head_mats():
    ex = (jnp.arange(SSM_WIDTH)[None, :] // SSM_HEAD_DIM == jnp.arange(LANES)[:, None]).astype(bf16)
    return jnp.tile(ex, (EXPAND_SPLIT, 1)), jnp.tile(ex.T, (REDUCE_SPLIT, 1))


def _shift_mat(rows, cols, off):
    r = lax.broadcasted_iota(jnp.int32, (rows, cols), 0)
    c = lax.broadcasted_iota(jnp.int32, (rows, cols), 1)
    return (c == r + off).astype(bf16)


def _ssd_front(c, xbc_ref, halo_ref, dtr_ref, cw_ref, cb_ref, dtb_ref, alog_ref):
    halo = halo_ref[...]
    ext = jnp.concatenate([jnp.where(c > 0, halo, jnp.zeros_like(halo)), xbc_ref[...]], axis=0)
    taps = [_dot(_shift_mat(CHUNK, HALO + CHUNK, HALO - SSM_CONV + 1 + j), ext) for j in range(SSM_CONV - 1)]
    taps.append(xbc_ref[...].astype(f32))
    xc = cb_ref[...] + cw_ref[0:1, :] * taps[0]
    for j in range(1, SSM_CONV):
        xc = xc + cw_ref[j:j + 1, :] * taps[j]
    sg = _sigmoid(xc)
    xa = xc * sg
    dt = _softplus(dtr_ref[...] + dtb_ref[...])
    a = -jnp.exp(alog_ref[...])
    acs = jnp.dot(_tril_mask().astype(f32), dt * a, preferred_element_type=f32, precision=HIGHEST)
    return taps, xc, sg, xa, dt, a, acs


def _ssd_wide(xa, dt, acs, dsk, ex):
    dt_x = _dot_sel(dt, ex, EXPAND_SPLIT)
    acs_x = _dot_sel(acs, ex, EXPAND_SPLIT)
    dsk_x = _dot_sel(jnp.broadcast_to(dsk, (8, LANES)), ex, EXPAND_SPLIT)[0:1]
    e_x = jnp.exp(acs_x)
    r_x = jnp.exp(acs_x[CHUNK - 1:CHUNK, :] - acs_x)
    xs = xa[:, :SSM_WIDTH]
    xd = xs * dt_x
    return dt_x, dsk_x, e_x, r_x, xs, xd, xd * r_x


def _pair_stack(v, lo):
    return jnp.concatenate([jnp.where(lo, v, 0.0), jnp.where(lo, 0.0, v)], axis=0)


def _ssd_pair(j, acs, acs_t, cb):
    out = []
    tril = _tril_mask()
    for h in (2 * j, 2 * j + 1):
        dk = jnp.exp(jnp.where(tril, acs[:, h:h + 1] - acs_t[h:h + 1, :], -jnp.inf))
        out.append((dk, cb * dk))
    return out


def _pair_col(row_lo, tot, j):
    return jnp.exp(jnp.where(row_lo, tot[:, 2 * j:2 * j + 1], tot[:, 2 * j + 1:2 * j + 2]))


def _gated_norm(y, z, g):
    yg = y * (z * _sigmoid(z))
    half = SSM_WIDTH // SSM_GROUPS
    parts = []
    for k in range(SSM_GROUPS):
        s = yg[:, k * half:(k + 1) * half]
        parts.append(s * lax.rsqrt(jnp.mean(s * s, axis=-1, keepdims=True) + EPS))
    return jnp.concatenate(parts, axis=1) * g


def _group_mats(xa):
    out = []
    for g in range(SSM_GROUPS):
        bm = xa[:, SSM_WIDTH + g * SSM_STATE:SSM_WIDTH + (g + 1) * SSM_STATE].astype(bf16)
        cm = xa[:, SSM_WIDTH + (SSM_GROUPS + g) * SSM_STATE:SSM_WIDTH + (SSM_GROUPS + g + 1) * SSM_STATE].astype(bf16)
        out.append((cm, bm, _dot_nt(cm, bm)))
    return out


def _ssd_in_specs(nc, rev):
    def ci(i):
        return nc - 1 - i if rev else i
    hp = CHUNK // HALO
    return [pl.BlockSpec((CHUNK, CONV_DIM), lambda i: (ci(i), 0)),
            pl.BlockSpec((HALO, CONV_DIM), lambda i: (jnp.maximum(ci(i) * hp - 1, 0), 0)),
            pl.BlockSpec((CHUNK, SSM_WIDTH), lambda i: (ci(i), 0)),
            pl.BlockSpec((CHUNK, LANES), lambda i: (ci(i), 0))]


def _ssd_fwd(xbc, z, dtr, conv_w, conv_b, dt_bias, a_log, d_skip, ssm_norm):
    T = xbc.shape[0]
    nc = T // CHUNK
    N = SSM_STATE

    def body(xbc_ref, halo_ref, z_ref, dtr_ref, cw_ref, cb_ref, dtb_ref, alog_ref, dsk_ref, g_ref, ex_ref,
             yb_ref, sprev_ref, s_sc):
        c = pl.program_id(0)

        @pl.when(c == 0)
        def _():
            s_sc[...] = jnp.zeros_like(s_sc)

        _, _, _, xa, dt, _, acs = _ssd_front(c, xbc_ref, halo_ref, dtr_ref, cw_ref, cb_ref, dtb_ref, alog_ref)
        _, dsk_x, e_x, _, xs, xd, gm = _ssd_wide(xa, dt, acs, dsk_ref[...], ex_ref[...])
        acs_t = acs.T
        tot = acs[CHUNK - 1:CHUNK, :]
        groups = _group_mats(xa)
        lo = lax.broadcasted_iota(jnp.int32, (CHUNK, PAIR_W), 1) < SSM_HEAD_DIM
        row_lo = lax.broadcasted_iota(jnp.int32, (PAIR_W, 1), 0) < SSM_HEAD_DIM
        ys = []
        for j in range(PAIRS):
            cmb, bmb, cb = groups[j // (PAIRS // SSM_GROUPS)]
            ps = slice(j * PAIR_W, (j + 1) * PAIR_W)
            (_, m0), (_, m1) = _ssd_pair(j, acs, acs_t, cb)
            sp = s_sc[j]
            yd = _dot(jnp.concatenate([m0, m1], axis=1).astype(bf16), _pair_stack(xd[:, ps], lo).astype(bf16))
            ys.append(yd + e_x[:, ps] * _dot_nt(cmb, sp.astype(bf16)))
            sprev_ref[0, j] = sp
            s_sc[j] = _pair_col(row_lo, tot, j) * sp + _dot_tn(gm[:, ps].astype(bf16), bmb)
        y = jnp.concatenate(ys, axis=1) + xs * dsk_x
        yb_ref[...] = _gated_norm(y, z_ref[...].astype(f32), g_ref[...]).astype(bf16)

    params = [conv_w, conv_b, dt_bias, a_log, d_skip, ssm_norm, _head_mats()[0]]
    return pl.pallas_call(
        body, name="ssd_fwd", grid=(nc,),
        in_specs=_ssd_in_specs(nc, False) + [_full(x.shape) for x in params],
        out_specs=[pl.BlockSpec((CHUNK, SSM_WIDTH), lambda i: (i, 0)), pl.BlockSpec((1, PAIRS, PAIR_W, N), lambda i: (i, 0, 0, 0))],
        out_shape=[jax.ShapeDtypeStruct((T, SSM_WIDTH), bf16), jax.ShapeDtypeStruct((nc, PAIRS, PAIR_W, N), f32)],
        scratch_shapes=[pltpu.VMEM((PAIRS, PAIR_W, N), f32)],
        compiler_params=_params(("arbitrary",)))(xbc, xbc, z, dtr, *params)


def _ssd_bwd(xbc, z, dtr, sprev, dyb, conv_w, conv_b, dt_bias, a_log, d_skip, ssm_norm):
    T = xbc.shape[0]
    nc = T // CHUNK
    H, N = SSM_HEADS, SSM_STATE
    PG = PAIRS // SSM_GROUPS

    def body(xbc_ref, halo_ref, z_ref, dtr_ref, sprev_ref, dyb_ref, cw_ref, cb_ref, dtb_ref, alog_ref, dsk_ref, g_ref,
             ex_ref, rd_ref, dzxd_ref, dcw_ref, dcb_ref, ddtb_ref, dalog_ref, ddsk_ref, dg_ref, ds_sc, next_sc):
        i = pl.program_id(0)
        c = nc - 1 - i

        @pl.when(i == 0)
        def _():
            ds_sc[...] = jnp.zeros_like(ds_sc)
            next_sc[...] = jnp.zeros_like(next_sc)
            for r_ in (dcw_ref, dcb_ref, ddtb_ref, dalog_ref, ddsk_ref, dg_ref):
                r_[...] = jnp.zeros_like(r_)

        taps, xc, sg, xa, dt, a, acs = _ssd_front(c, xbc_ref, halo_ref, dtr_ref, cw_ref, cb_ref, dtb_ref, alog_ref)
        dt_x, dsk_x, e_x, r_x, xs, xd, gm = _ssd_wide(xa, dt, acs, dsk_ref[...], ex_ref[...])
        acs_t = acs.T
        tot = acs[CHUNK - 1:CHUNK, :]
        groups = _group_mats(xa)
        lo = lax.broadcasted_iota(jnp.int32, (CHUNK, PAIR_W), 1) < SSM_HEAD_DIM
        row_lo = lax.broadcasted_iota(jnp.int32, (PAIR_W, 1), 0) < SSM_HEAD_DIM
        pairs, zs, yds = [], [], []
        for j in range(PAIRS):
            cmb, _, cb = groups[j // PG]
            ps = slice(j * PAIR_W, (j + 1) * PAIR_W)
            pairs.append(_ssd_pair(j, acs, acs_t, cb))
            (_, m0), (_, m1) = pairs[j]
            zs.append(_dot_nt(cmb, sprev_ref[0, j].astype(bf16)))
            yds.append(_dot(jnp.concatenate([m0, m1], axis=1).astype(bf16), _pair_stack(xd[:, ps], lo).astype(bf16)))
        zf = jnp.concatenate(zs, axis=1)
        y = jnp.concatenate(yds, axis=1) + e_x * zf + xs * dsk_x
        _, gn_vjp = jax.vjp(_gated_norm, y, z_ref[...].astype(f32), g_ref[...])
        dy, dz, dg = gn_vjp(dyb_ref[...].astype(f32))
        dg_ref[...] += dg
        dzxd_ref[:, :SSM_WIDTH] = dz.astype(bf16)

        lane = lax.broadcasted_iota(jnp.int32, (1, LANES), 1)
        sub = lax.broadcasted_iota(jnp.int32, (LANES, 1), 0)
        dacs = jnp.zeros((CHUNK, LANES), f32)
        dacs_r = jnp.zeros((LANES, CHUNK), f32)
        dtot = jnp.zeros((1, LANES), f32)
        dcb = [jnp.zeros((CHUNK, CHUNK), f32) for _ in range(SSM_GROUPS)]
        dcm = [jnp.zeros((CHUNK, N), f32) for _ in range(SSM_GROUPS)]
        dbm = [jnp.zeros((CHUNK, N), f32) for _ in range(SSM_GROUPS)]
        dxds, dgms = [], []
        for j in range(PAIRS):
            g = j // PG
            cmb, bmb, _ = groups[g]
            ps = slice(j * PAIR_W, (j + 1) * PAIR_W)
            (dk0, m0), (dk1, m1) = pairs[j]
            oh0, oh1 = (lane == 2 * j).astype(f32), (lane == 2 * j + 1).astype(f32)
            dyp = dy[:, ps]
            dy2 = _pair_stack(dyp, lo).astype(bf16)
            dm2 = _dot_nt(dy2, xd[:, ps].astype(bf16))
            m2 = jnp.concatenate([m0, m1], axis=0)
            dxds.append(_dot_tn(m2.astype(bf16), dy2))
            w2 = dm2 * m2
            rs = jnp.sum(w2, axis=1, keepdims=True)
            dacs = dacs + rs[:CHUNK] * oh0 + rs[CHUNK:] * oh1
            dacs_r = dacs_r - ((sub == 2 * j).astype(f32) * jnp.sum(w2[:CHUNK], axis=0, keepdims=True)
                               + (sub == 2 * j + 1).astype(f32) * jnp.sum(w2[CHUNK:], axis=0, keepdims=True))
            dcb[g] = dcb[g] + dm2[:CHUNK] * dk0 + dm2[CHUNK:] * dk1
            sp = sprev_ref[0, j]
            dzb = (dyp * e_x[:, ps]).astype(bf16)
            dcm[g] = dcm[g] + _dot(dzb, sp.astype(bf16))
            dsn = ds_sc[j]
            dsnb = dsn.astype(bf16)
            et = _pair_col(row_lo, tot, j)
            rr = jnp.sum(dsn * sp, axis=1, keepdims=True) * et
            dtot = dtot + jnp.sum(rr[:SSM_HEAD_DIM]) * oh0 + jnp.sum(rr[SSM_HEAD_DIM:]) * oh1
            dgms.append(_dot_nt(bmb, dsnb))
            dbm[g] = dbm[g] + _dot(gm[:, ps].astype(bf16), dsnb)
            ds_sc[j] = _dot_tn(dzb, cmb) + et * dsn
        dgm = jnp.concatenate(dgms, axis=1)
        dxd = jnp.concatenate(dxds, axis=1) + dgm * r_x
        dr = dgm * gm
        red = _dot_sel(jnp.concatenate([dy * e_x * zf - dr, dr, dxd * xs, dy * xs], axis=0), rd_ref[...], REDUCE_SPLIT)
        rowi = lax.broadcasted_iota(jnp.int32, (CHUNK, 1), 0)
        dtot = dtot + jnp.sum(red[CHUNK:2 * CHUNK], axis=0, keepdims=True)
        dacs = dacs + red[:CHUNK] + dacs_r.T + jnp.where(rowi == CHUNK - 1, dtot, 0.0)
        r2 = lax.broadcasted_iota(jnp.int32, (CHUNK, CHUNK), 0)
        c2 = lax.broadcasted_iota(jnp.int32, (CHUNK, CHUNK), 1)
        dadt = jnp.dot((c2 >= r2).astype(f32), dacs, preferred_element_type=f32, precision=HIGHEST)
        ddt = red[2 * CHUNK:3 * CHUNK] + dadt * a
        dalog_ref[...] += jnp.sum(dadt * dt, axis=0, keepdims=True) * a
        ddsk_ref[...] += jnp.sum(red[3 * CHUNK:], axis=0, keepdims=True)
        ddtr = jnp.where(lane < H, ddt * _sigmoid(dtr_ref[...] + dtb_ref[...]), 0.0)
        ddtb_ref[...] += jnp.sum(ddtr, axis=0, keepdims=True)
        dzxd_ref[:, SSM_WIDTH + CONV_DIM:] = ddtr.astype(bf16)
        dxa_bm, dxa_cm = [], []
        for g in range(SSM_GROUPS):
            cmb, bmb, _ = groups[g]
            dcbb = dcb[g].astype(bf16)
            dxa_bm.append(dbm[g] + _dot_tn(dcbb, cmb))
            dxa_cm.append(dcm[g] + _dot(dcbb, bmb))
        dxc = jnp.concatenate([dy * dsk_x + dxd * dt_x] + dxa_bm + dxa_cm, axis=1) * (sg * (1.0 + xc * (1.0 - sg)))
        ext = jnp.concatenate([dxc, next_sc[...]], axis=0)
        dxbc = cw_ref[SSM_CONV - 1:SSM_CONV, :] * dxc
        for s in range(1, SSM_CONV):
            dxbc = dxbc + cw_ref[SSM_CONV - 1 - s:SSM_CONV - s, :] * _sel_dot(_shift_mat(CHUNK, CHUNK + HALO, s), ext, 2)
        dzxd_ref[:, SSM_WIDTH:SSM_WIDTH + CONV_DIM] = dxbc.astype(bf16)
        dcw_ref[...] += jnp.concatenate([jnp.sum(dxc * t, axis=0, keepdims=True) for t in taps], axis=0)
        dcb_ref[...] += jnp.sum(dxc, axis=0, keepdims=True)
        next_sc[...] = dxc[0:HALO, :]

    params = [conv_w, conv_b, dt_bias, a_log, d_skip, ssm_norm]
    mats = list(_head_mats())

    def rc(i):
        return nc - 1 - i

    in_specs = (_ssd_in_specs(nc, True)
                + [pl.BlockSpec((1, PAIRS, PAIR_W, N), lambda i: (rc(i), 0, 0, 0)), pl.BlockSpec((CHUNK, SSM_WIDTH), lambda i: (rc(i), 0))]
                + [_full(x.shape) for x in params + mats])
    return pl.pallas_call(
        body, name="ssd_bwd", grid=(nc,), in_specs=in_specs,
        out_specs=[pl.BlockSpec((CHUNK, ZXD), lambda i: (rc(i), 0))] + [_acc(x.shape) for x in params],
        out_shape=[jax.ShapeDtypeStruct((T, ZXD), bf16)] + [jax.ShapeDtypeStruct(x.shape, f32) for x in params],
        scratch_shapes=[pltpu.VMEM((PAIRS, PAIR_W, N), f32), pltpu.VMEM((HALO, CONV_DIM), f32)],
        compiler_params=_params(("arbitrary",)))(xbc, xbc, z, dtr, sprev, dyb, *params, *mats)


def _tail(h3, p, tgt, gp, wpg, bpg, wpp, gf, tm=512):
    T, D = h3.shape
    tm = min(tm, T)

    def head(gpre, pp, h, gf_, t):
        gate = _sigmoid(gpre)
        y = _rms(h + gate * pp, gf_)
        err = y - t
        return 0.5 * jnp.sum(jnp.mean(err * err, axis=-1))

    def body(h_ref, p_ref, t_ref, gp_ref, wpg_ref, bpg_ref, wpp_ref, gf_ref,
             dh_ref, loss_ref, dgp_ref, dwpg_ref, dbpg_ref, dwpp_ref, dgf_ref):
        @pl.when(pl.program_id(0) == 0)
        def _():
            for r in (loss_ref, dgp_ref, dwpg_ref, dbpg_ref, dwpp_ref, dgf_ref):
                r[...] = jnp.zeros_like(r)

        h = h_ref[...]
        npf, np_vjp = jax.vjp(_rms, h, gp_ref[...])
        npb = npf.astype(bf16)
        pb = p_ref[...].astype(bf16)
        gpre = _dot(npb, wpg_ref[...]) + bpg_ref[...]
        kp, _, cp = wpp_ref.shape
        pp = jnp.concatenate([_dot(pb, wpp_ref[k]) for k in range(kp)], axis=1)
        loss, head_vjp = jax.vjp(head, gpre, pp, h, gf_ref[...], t_ref[...])
        dgpre, dpp, dh_a, dgf, _ = head_vjp(jnp.ones((), f32))
        loss_ref[...] += loss
        dgf_ref[...] += dgf
        dbpg_ref[...] += jnp.sum(dgpre, axis=0, keepdims=True)
        dgb = dgpre.astype(bf16)
        dwpg_ref[...] += _dot_tn(npb, dgb)
        dppb = dpp.astype(bf16)
        for k in range(kp):
            dwpp_ref[k] += _dot_tn(pb, dppb[:, k * cp:(k + 1) * cp])
        dh_b, dgp = np_vjp(_dot_nt(dgb, wpg_ref[...]))
        dgp_ref[...] += dgp
        dh_ref[...] = dh_a + dh_b

    ins = [h3, p, tgt, gp, wpg, bpg, wpp, gf]
    in_specs = [_rows(tm, D), _rows(tm, p.shape[1]), _rows(tm, D)] + [_full(x.shape) for x in ins[3:]]
    acc_shapes = [(1, LANES), gp.shape, wpg.shape, bpg.shape, wpp.shape, gf.shape]
    return pl.pallas_call(
        body, name="tail", grid=(T // tm,), in_specs=in_specs,
        out_specs=[_rows(tm, D)] + [_acc(s) for s in acc_shapes],
        out_shape=[jax.ShapeDtypeStruct((T, D), f32)] + [jax.ShapeDtypeStruct(s, f32) for s in acc_shapes],
        compiler_params=_params(("arbitrary",)))(*ins)


def _adamw(name, w, g, m, v, tr=256):
    R, C = w.shape
    tr = _row_tile(R, tr)

    def body(w_ref, g_ref, m_ref, v_ref, d_ref, mo_ref, vo_ref):
        g_ = g_ref[...]
        m_ = ADAM_B1 * m_ref[...] + (1.0 - ADAM_B1) * g_
        v_ = ADAM_B2 * v_ref[...] + (1.0 - ADAM_B2) * jnp.square(g_)
        m_hat = m_ / (1.0 - ADAM_B1 ** ADAM_STEP)
        v_hat = v_ / (1.0 - ADAM_B2 ** ADAM_STEP)
        d_ref[...] = -ADAM_LR * (m_hat / (jnp.sqrt(v_hat) + ADAM_EPS) + ADAM_WD * w_ref[...])
        mo_ref[...] = m_
        vo_ref[...] = v_

    spec = pl.BlockSpec((tr, C), lambda i: (i, 0))
    return pl.pallas_call(body, name=name, grid=(R // tr,), in_specs=[spec] * 4, out_specs=[spec] * 3,
                          out_shape=[jax.ShapeDtypeStruct((R, C), f32)] * 3,
                          compiler_params=_params(("parallel",)))(w, g, m, v)


HBM = pl.BlockSpec(memory_space=pltpu.HBM)


def _me():
    return lax.axis_index("x"), lax.axis_index("y"), lax.axis_index("c")


def _other_chips(x, y):
    return [(1 - x, y), (x, 1 - y), (1 - x, 1 - y)]


def _remote(src, dst, send_sem, recv_sem, dev):
    return pltpu.make_async_remote_copy(src_ref=src, dst_ref=dst, send_sem=send_sem, recv_sem=recv_sem,
                                        device_id=dev, device_id_type=MESH)


def _sems(n):
    return [pltpu.SemaphoreType.DMA((n,)), pltpu.SemaphoreType.DMA((n,))]


def _gather_weights(shards, split):
    n = len(shards)

    def body(*refs):
        ins, outs = refs[:n], refs[n:2 * n]
        own_send, own_recv, ici_send, ici_recv, d2d_send, d2d_recv = refs[2 * n:]
        x, y, c = _me()
        my_chip = 2 * x + y
        sibling = (x, y, 1 - c)
        chips = _other_chips(x, y)

        def rows(i, half):
            hr = shards[i].shape[0] // 2
            return pl.ds(half * hr, hr) if split[i] else pl.ds(0, shards[i].shape[0])

        sends = []
        for i in range(n):
            for j, chip in enumerate(chips):
                cp = _remote(ins[i].at[rows(i, c)], outs[i].at[my_chip, rows(i, c)],
                             ici_send.at[3 * i + j], ici_recv.at[3 * i + j], (*chip, c))
                cp.start()
                sends.append(cp)
            cp = _remote(ins[i], outs[i].at[my_chip], own_send.at[i], own_recv.at[i], sibling)
            cp.start()
            sends.append(cp)
        for i in range(n):
            for j, chip in enumerate(chips):
                s = 3 * i + j
                land = outs[i].at[2 * chip[0] + chip[1], rows(i, c)]
                _remote(land, land, ici_send.at[s], ici_recv.at[s], (*chip, c)).wait_recv()
                if split[i]:
                    cp = _remote(land, land, d2d_send.at[s], d2d_recv.at[s], sibling)
                    cp.start()
                    sends.append(cp)
        for i in range(n):
            _remote(ins[i], outs[i].at[my_chip], own_send.at[i], own_recv.at[i], sibling).wait_recv()
            if split[i]:
                for j, chip in enumerate(chips):
                    s = 3 * i + j
                    land = outs[i].at[2 * chip[0] + chip[1], rows(i, 1 - c)]
                    _remote(land, land, d2d_send.at[s], d2d_recv.at[s], sibling).wait_recv()
        for cp in sends:
            cp.wait_send()

    return pl.pallas_call(
        body, name="gather_weights", out_shape=[jax.ShapeDtypeStruct((N_CHIPS,) + s.shape, s.dtype) for s in shards],
        in_specs=[HBM] * n, out_specs=[HBM] * n,
        scratch_shapes=_sems(n) + _sems(3 * n) + _sems(3 * n))(*shards)


def _swap_halves(name, grads):
    n = len(grads)

    def body(*refs):
        ins, outs, send, recv = refs[:n], refs[n:2 * n], refs[2 * n], refs[2 * n + 1]
        x, y, c = _me()
        copies = []
        for i in range(n):
            hr = grads[i].shape[1] // 2
            cp = _remote(ins[i].at[:, pl.ds((1 - c) * hr, hr), :], outs[i], send.at[i], recv.at[i], (x, y, 1 - c))
            cp.start()
            copies.append(cp)
        for cp in copies:
            cp.wait()

    return pl.pallas_call(
        body, name=name,
        out_shape=[jax.ShapeDtypeStruct((g.shape[0], g.shape[1] // 2, g.shape[2]), g.dtype) for g in grads],
        in_specs=[HBM] * n, out_specs=[HBM] * n, scratch_shapes=_sems(n))(*grads)


def _add_halves(name, grads, other, c_idx, th=592):
    K, R, C = grads.shape
    H = R // 2
    th = _row_tile(H, th, 16)
    nb = H // th

    def body(c_ref, g_ref, o_ref, out_ref):
        out_ref[...] = (g_ref[...].astype(f32) + o_ref[...].astype(f32)).astype(bf16)

    grid_spec = pltpu.PrefetchScalarGridSpec(
        num_scalar_prefetch=1, grid=(nb,),
        in_specs=[pl.BlockSpec((K, th, C), lambda i, c: (0, c[0] * nb + i, 0)),
                  pl.BlockSpec((K, th, C), lambda i, c: (0, i, 0))],
        out_specs=pl.BlockSpec((K, th, C), lambda i, c: (0, i, 0)))
    return pl.pallas_call(body, name=name, grid_spec=grid_spec,
                          out_shape=jax.ShapeDtypeStruct((K, H, C), bf16),
                          compiler_params=_params(("parallel",)))(c_idx, grads, other)


SEM = pl.BlockSpec(memory_space=pltpu.SEMAPHORE)
ANY = pl.BlockSpec(memory_space=pl.ANY)
EFFECT = pltpu.SideEffectType.DATAFLOW_SIDE_EFFECTING


def _copies_start(name, srcs, land_shapes, n_copies, make_copies, after):
    ns, nl = len(srcs), len(land_shapes)
    lands = [lax.empty(s.shape, s.dtype) for s in land_shapes]

    def body(*refs):
        src_refs, land_refs = refs[:ns], refs[ns:ns + nl]
        send, recv, token = refs[ns + nl + 1], refs[ns + nl + 2], refs[-1]
        for cp in make_copies(src_refs, land_refs, send, recv):
            cp.start()
        token[...] = jnp.zeros_like(token)

    buffers = list(srcs) + lands
    out = pl.pallas_call(
        body, name=name,
        out_shape=(pltpu.SemaphoreType.DMA((n_copies,)), pltpu.SemaphoreType.DMA((n_copies,)),
                   *[pltpu.HBM(b.shape, b.dtype) for b in buffers], jax.ShapeDtypeStruct((8, LANES), f32)),
        in_specs=[HBM] * (ns + nl) + [ANY],
        out_specs=(SEM, SEM, *[HBM] * (ns + nl), pl.BlockSpec(memory_space=pltpu.VMEM)),
        input_output_aliases={i: 2 + i for i in range(ns + nl)},
        compiler_params=pltpu.CompilerParams(has_side_effects=EFFECT),
    )(*[pltpu.with_memory_space_constraint(b, pltpu.HBM) for b in buffers], after)
    return out[0], out[1], list(out[2:2 + ns]), list(out[2 + ns:2 + ns + nl]), out[-1]


def _copies_wait(name, started, make_copies, after):
    send, recv, srcs, lands, _ = started
    ns, nl = len(srcs), len(lands)
    after = list(after)

    def body(*refs):
        src_refs, land_refs = refs[:ns], refs[ns:ns + nl]
        for cp in make_copies(src_refs, land_refs, refs[ns + nl], refs[ns + nl + 1]):
            cp.wait_send()
            cp.wait_recv()

    buffers = list(srcs) + list(lands)
    out = pl.pallas_call(
        body, name=name, out_shape=tuple(pltpu.HBM(b.shape, b.dtype) for b in buffers),
        in_specs=[HBM] * (ns + nl) + [SEM, SEM] + [ANY] * len(after), out_specs=tuple([HBM] * (ns + nl)),
        input_output_aliases={i: i for i in range(ns + nl)},
        compiler_params=pltpu.CompilerParams(has_side_effects=EFFECT),
    )(*buffers, send, recv, *after)
    return list(out[:ns]), list(out[ns:])


def _gather_copies(src_refs, land_refs, send, recv):
    x, y, c = _me()
    my_chip = 2 * x + y
    peers = [(*chip, c) for chip in _other_chips(x, y)] + [(x, y, 1 - c)]
    return [_remote(src_refs[i], land_refs[i].at[my_chip], send.at[4 * i + j], recv.at[4 * i + j], peer)
            for i in range(len(src_refs)) for j, peer in enumerate(peers)]


def _partial_copies(src_refs, land_refs, send, recv):
    x, y, c = _me()
    return [_remote(src_refs[i].at[2 * chip[0] + chip[1]], land_refs[i].at[j], send.at[3 * i + j], recv.at[3 * i + j], (*chip, c))
            for i in range(len(src_refs)) for j, chip in enumerate(_other_chips(x, y))]


def _small_copies(src_refs, land_refs, send, recv):
    x, y, c = _me()
    return [_remote(src_refs[0], land_refs[0].at[k - 1], send.at[k - 1], recv.at[k - 1], (x ^ (k >> 2), y ^ ((k >> 1) & 1), c ^ (k & 1)))
            for k in range(1, N_DEV)]


def _sum_small(own, slots, dev_idx):
    R, C = own.shape

    def body(dev_ref, own_ref, s_ref, o_ref):
        me = dev_ref[0]
        acc = jnp.zeros((R, C), f32)
        for d in range(N_DEV):
            k = me ^ d
            acc = acc + jnp.where(k == 0, own_ref[...], s_ref[jnp.maximum(k - 1, 0)])
        o_ref[...] = acc

    grid_spec = pltpu.PrefetchScalarGridSpec(
        num_scalar_prefetch=1, grid=(1,),
        in_specs=[pl.BlockSpec((R, C), lambda i, dev: (0, 0)), pl.BlockSpec((N_DEV - 1, R, C), lambda i, dev: (0, 0, 0))],
        out_specs=pl.BlockSpec((R, C), lambda i, dev: (0, 0)))
    return pl.pallas_call(body, name="sum_small", grid_spec=grid_spec, out_shape=jax.ShapeDtypeStruct((R, C), f32),
                          compiler_params=_params(("arbitrary",)))(dev_idx, own, slots)


def _sum_partials(name, part, recv, chip_idx, th=592):
    K, H, C = part.shape
    th = _row_tile(H, th, 16)

    def body(chip_ref, p_ref, r_ref, o_ref):
        acc = p_ref[...].astype(f32)
        for j in range(3):
            acc = acc + r_ref[j].astype(f32)
        o_ref[...] = acc

    grid_spec = pltpu.PrefetchScalarGridSpec(
        num_scalar_prefetch=1, grid=(H // th,),
        in_specs=[pl.BlockSpec((None, th, C), lambda i, chip: (chip[0], i, 0)),
                  pl.BlockSpec((3, th, C), lambda i, chip: (0, i, 0))],
        out_specs=pl.BlockSpec((th, C), lambda i, chip: (i, 0)))
    return pl.pallas_call(body, name=name, grid_spec=grid_spec, out_shape=jax.ShapeDtypeStruct((H, C), f32),
                          compiler_params=_params(("parallel",)))(chip_idx, part, recv)


def _share_halves(name, halves):
    n = len(halves)

    def body(*refs):
        ins, outs, send, recv = refs[:n], refs[n:2 * n], refs[2 * n], refs[2 * n + 1]
        x, y, c = _me()
        copies = []
        for i in range(n):
            cp = _remote(ins[i], outs[i], send.at[i], recv.at[i], (x, y, 1 - c))
            cp.start()
            copies.append(cp)
        for cp in copies:
            cp.wait()

    return pl.pallas_call(
        body, name=name, out_shape=[jax.ShapeDtypeStruct(h.shape, h.dtype) for h in halves],
        in_specs=[HBM] * n, out_specs=[HBM] * n, scratch_shapes=_sems(n))(*halves)


def _adamw_big(name, w, g_mine, g_theirs, m, v, c_idx, tr=320):
    R, C = w.shape
    H = R // 2
    tr = _row_tile(H, tr)
    nb = H // tr

    def body(c_ref, w_ref, gm_ref, gt_ref, m_ref, v_ref, g_ref, d_ref, mo_ref, vo_ref):
        g_ = jnp.where(pl.program_id(0) // nb == c_ref[0], gm_ref[...], gt_ref[...])
        g_ref[...] = g_
        m_ = ADAM_B1 * m_ref[...] + (1.0 - ADAM_B1) * g_
        v_ = ADAM_B2 * v_ref[...] + (1.0 - ADAM_B2) * jnp.square(g_)
        m_hat = m_ / (1.0 - ADAM_B1 ** ADAM_STEP)
        v_hat = v_ / (1.0 - ADAM_B2 ** ADAM_STEP)
        d_ref[...] = -ADAM_LR * (m_hat / (jnp.sqrt(v_hat) + ADAM_EPS) + ADAM_WD * w_ref[...])
        mo_ref[...] = m_
        vo_ref[...] = v_

    full = pl.BlockSpec((tr, C), lambda i, c: (i, 0))
    half = pl.BlockSpec((tr, C), lambda i, c: (i % nb, 0))
    grid_spec = pltpu.PrefetchScalarGridSpec(num_scalar_prefetch=1, grid=(2 * nb,),
                                             in_specs=[full, half, half, full, full], out_specs=[full] * 4)
    return pl.pallas_call(body, name=name, grid_spec=grid_spec, out_shape=[jax.ShapeDtypeStruct((R, C), f32)] * 4,
                          compiler_params=_params(("parallel",)))(c_idx, w, g_mine, g_theirs, m, v)


BIG = ("ffn1_w_gate", "ffn1_w_up", "ffn1_w_down", "w_in", "w_out", "ffn2_w_gate", "ffn2_w_up", "ffn2_w_down",
       "ple_w_gate", "ple_w_proj")


SMALL = ("ffn1_norm", "mix_norm", "gm_ln_g", "gm_ln_b", "gm_w_s", "gm_b_s", "gm_out_norm", "conv_b", "dt_bias", "a_log",
         "d_skip", "ssm_norm", "ffn2_norm", "ple_norm", "ple_b_gate", "final_norm")
SMALL_C = 1024


def _pack_small(vals):
    parts = []
    for v in vals:
        f = v.astype(f32).reshape(-1)
        parts.append(jnp.pad(f, (0, -f.shape[0] % SMALL_C)))
    flat = jnp.concatenate(parts)
    rows = flat.shape[0] // SMALL_C
    return jnp.pad(flat, (0, (-rows % 8) * SMALL_C)).reshape(-1, SMALL_C)


def _unpack_small(pack, shapes):
    flat = pack.reshape(-1)
    out, off = [], 0
    for s in shapes:
        n = 1
        for d in s:
            n *= d
        out.append(flat[off:off + n].reshape(s))
        off += n + (-n % SMALL_C)
    return out


def _pad_lanes(v):
    return jnp.pad(v, ((0, 0), (0, LANES - v.shape[1])))


def _pad_rows(a):
    pad = [(0, 0)] * a.ndim
    pad[-2] = (0, -a.shape[-2] % ROW_PAD)
    return jnp.pad(a, pad) if pad[-2][1] else a


FETCH = (("ffn1_w_gate", "ffn1_w_up", "ffn1_w_down"), ("w_in", "conv_w", "w_out"),
         ("ffn2_w_gate", "ffn2_w_up", "ffn2_w_down", "ple_w_gate", "ple_w_proj"))
TRANSPOSED = ("ffn1_w_gate", "ffn1_w_up", "ffn2_w_gate", "ffn2_w_up", "w_in")
ROW_PAD = 32
DONE = (("ffn2_w_gate", "ffn2_w_up", "ffn2_w_down", "w_out", "ple_w_gate", "ple_w_proj"), ("w_in",),
        ("ffn1_w_gate", "ffn1_w_up", "ffn1_w_down"))


def _local_step(x, p, tgt, fetch, S, on_grads):
    G = GM_WIDTH
    K = N_CHIPS
    b_st = S["gm_b_s"][0].T
    w_s = S["gm_w_s"][0]
    dtb, alog, dsk = _pad_lanes(S["dt_bias"]), _pad_lanes(S["a_log"]), _pad_lanes(S["d_skip"])
    gfin = S["final_norm"].reshape(1, -1)

    def rows(a):
        return a.reshape(-1, D_MODEL)

    def shards(a):
        return a.reshape(K, -1, D_MODEL)

    wg1, wu1, wd1 = [rows(a) for a in fetch(0, None)]
    h1, n1, a1, b1 = _ffn_fwd("ffn1_fwd", x, S["ffn1_norm"], wg1, wu1, wd1)
    w_in4, cw4, wo4 = fetch(1, h1)
    w_in = w_in4.reshape(IN_PROJ, D_MODEL)
    w_uv = w_in[:2 * G]
    w_zxd = jnp.pad(w_in[2 * G:], ((0, ZXD - (IN_PROJ - 2 * G)), (0, 0)))
    conv_w = jnp.transpose(cw4, (1, 0, 2)).reshape(SSM_CONV, CONV_DIM)
    wo = wo4.reshape(-1, D_MODEL)
    n2, uv, z, xbc, dtr, ya = _mix_fwd(h1, S["mix_norm"], w_uv, w_zxd, S["gm_ln_g"], S["gm_ln_b"], w_s, b_st, S["gm_out_norm"])
    yb, sprev = _ssd_fwd(xbc, z, dtr, conv_w, S["conv_b"], dtb, alog, dsk, S["ssm_norm"])
    wg2, wu2, wd2, wpg4, wpp4 = fetch(2, yb)
    wg2, wu2, wd2 = rows(wg2), rows(wu2), rows(wd2)
    h2, h3, n3, a2, b2 = _ffn_fwd("ffn2_fwd", h1, S["ffn2_norm"], wg2, wu2, wd2, pre=(ya, yb, wo))
    dh3, loss, dgp, dwpg, dbpg, dwpp, dgf = _tail(h3, p, tgt, S["ple_norm"], wpg4.reshape(-1, D_MODEL), S["ple_b_gate"], wpp4, gfin)
    dh2, da2, db2, hm2, dg_ffn2, dya, dyb = _ffn_bwd("ffn2_bwd", dh3, h2, S["ffn2_norm"], a2, b2, wg2, wu2, wd2, wo=wo, ga=G)
    dw_out = jnp.concatenate([_matmul_tn("dw_out_a", ya, dh2), _matmul_tn("dw_out_b", yb, dh2)], axis=0).reshape(wo4.shape)
    zero = on_grads(0, [shards(_matmul_tn("dw_ffn2_gate", da2, n3)), shards(_matmul_tn("dw_ffn2_up", db2, n3)),
                        shards(_matmul_tn("dw_ffn2_down", hm2, dh3, scale=0.5)), dw_out,
                        dwpg.astype(bf16).reshape(wpg4.shape), dwpp.astype(bf16)])
    duv, dlng, dlnb, dws, dbst, dgout = _gm_bwd(uv, dya, S["gm_ln_g"], S["gm_ln_b"], w_s, b_st, S["gm_out_norm"] + zero)
    dzxd, dcw, dcb, ddtb, dalog, ddsk, dgssm = _ssd_bwd(xbc, z, dtr, sprev, dyb, conv_w, S["conv_b"], dtb, alog, dsk,
                                                        S["ssm_norm"] + zero)
    dh1, dg_mix = _mix_bwd(dh2, h1, S["mix_norm"], duv, dzxd, w_uv, w_zxd)
    dw_in = jnp.concatenate([_matmul_tn("dw_in_uv", duv, n2), _matmul_tn("dw_in_zxd", dzxd, n2)[:IN_PROJ - 2 * G]], axis=0)
    zero = on_grads(1, [dw_in.reshape(w_in4.shape)])
    dx, da1, db1, hm1, dg_ffn1 = _ffn_bwd("ffn1_bwd", dh1, x, S["ffn1_norm"] + zero, a1, b1, wg1, wu1, wd1)
    zero = on_grads(2, [shards(_matmul_tn("dw_ffn1_gate", da1, n1)), shards(_matmul_tn("dw_ffn1_up", db1, n1)),
                        shards(_matmul_tn("dw_ffn1_down", hm1, dh1, scale=0.5))])
    loss = loss + zero
    nh = SSM_HEADS
    gS = {"ffn1_norm": dg_ffn1, "mix_norm": dg_mix, "gm_ln_g": dlng, "gm_ln_b": dlnb, "gm_w_s": dws[None], "gm_b_s": dbst.T[None],
          "gm_out_norm": dgout, "conv_b": dcb, "dt_bias": ddtb[:, :nh], "a_log": dalog[:, :nh], "d_skip": ddsk[:, :nh],
          "ssm_norm": dgssm, "ffn2_norm": dg_ffn2, "ple_norm": dgp, "ple_b_gate": dbpg, "final_norm": dgf.reshape(-1)}
    return loss, dx, dcw, gS


_WEIGHTS = ("ffn1_norm", "ffn1_w_gate", "ffn1_w_up", "ffn1_w_down", "mix_norm", "w_in", "gm_ln_g", "gm_ln_b", "gm_w_s", "gm_b_s",
            "gm_out_norm", "conv_w", "conv_b", "dt_bias", "a_log", "d_skip", "ssm_norm", "w_out", "ffn2_norm", "ffn2_w_gate",
            "ffn2_w_up", "ffn2_w_down", "ple_norm", "ple_w_gate", "ple_b_gate", "ple_w_proj", "final_norm")
_BIG_NAMES = BIG


def kernel(x, p, ffn1_norm, ffn1_w_gate, ffn1_w_up, ffn1_w_down, mix_norm, w_in, gm_ln_g, gm_ln_b, gm_w_s, gm_b_s, gm_out_norm, conv_w, conv_b, dt_bias, a_log, d_skip, ssm_norm, w_out, ffn2_norm, ffn2_w_gate, ffn2_w_up, ffn2_w_down, ple_norm, ple_w_gate, ple_b_gate, ple_w_proj, final_norm, loss_target, m_ffn1_norm, m_ffn1_w_gate, m_ffn1_w_up, m_ffn1_w_down, m_mix_norm, m_w_in, m_gm_ln_g, m_gm_ln_b, m_gm_w_s, m_gm_b_s, m_gm_out_norm, m_conv_w, m_conv_b, m_dt_bias, m_a_log, m_d_skip, m_ssm_norm, m_w_out, m_ffn2_norm, m_ffn2_w_gate, m_ffn2_w_up, m_ffn2_w_down, m_ple_norm, m_ple_w_gate, m_ple_b_gate, m_ple_w_proj, m_final_norm, v_ffn1_norm, v_ffn1_w_gate, v_ffn1_w_up, v_ffn1_w_down, v_mix_norm, v_w_in, v_gm_ln_g, v_gm_ln_b, v_gm_w_s, v_gm_b_s, v_gm_out_norm, v_conv_w, v_conv_b, v_dt_bias, v_a_log, v_d_skip, v_ssm_norm, v_w_out, v_ffn2_norm, v_ffn2_w_gate, v_ffn2_w_up, v_ffn2_w_down, v_ple_norm, v_ple_w_gate, v_ple_b_gate, v_ple_w_proj, v_final_norm):
    given = dict(locals())
    w = {n: given[n] for n in _WEIGHTS}
    m = {n: given["m_" + n] for n in _WEIGHTS}
    v = {n: given["v_" + n] for n in _WEIGHTS}

    c_idx = lax.axis_index("c").astype(jnp.int32).reshape(1)
    chip = 2 * lax.axis_index("x") + lax.axis_index("y")
    chip_idx = chip.astype(jnp.int32).reshape(1)

    shard = {n: (jnp.swapaxes(w[n][0], 0, 1) if n in TRANSPOSED else w[n][0]).astype(bf16) for n in BIG}
    shard["conv_w"] = w["conv_w"][0]
    first = _gather_weights([shard[n] for n in FETCH[0]], [True] * len(FETCH[0]))
    fetching, after = [], first[-1]
    for k in (1, 2):
        srcs = [shard[n] for n in FETCH[k]]
        lands = [jax.ShapeDtypeStruct((N_CHIPS,) + s.shape, s.dtype) for s in srcs]
        fetching.append(_copies_start("gather%d_start" % k, srcs, lands, 4 * len(srcs), _gather_copies, after))
        after = fetching[-1][4]

    def fetch(k, after_):
        return first if k == 0 else _copies_wait("gather%d_wait" % k, fetching[k - 1], _gather_copies, [after_])[1]

    exchanging = []

    def on_grads(k, grads):
        grads = [_pad_rows(g_) for g_ in grads]
        others = _swap_halves("swap%d" % k, grads)
        parts = [_add_halves("add_" + n, g_, o_, c_idx) for n, g_, o_ in zip(DONE[k], grads, others)]
        lands = [jax.ShapeDtypeStruct((3,) + p_.shape[1:], p_.dtype) for p_ in parts]
        exchanging.append(_copies_start("exchange%d_start" % k, parts, lands, 3 * len(parts), _partial_copies, c_idx))
        return exchanging[-1][4][0, 0]

    S = {n: w[n] for n in SMALL}
    S["ffn1_norm"] = S["ffn1_norm"] + after[0, 0]
    loss, dx, dcw, gS = _local_step(x[0], p[0, 0], loss_target[0], fetch, S, on_grads)

    small = _pack_small([gS[n] for n in SMALL] + [dcw, loss[:, :1]])
    small_lands = [jax.ShapeDtypeStruct((N_DEV - 1,) + small.shape, small.dtype)]
    small_st = _copies_start("small_start", [small], small_lands, N_DEV - 1, _small_copies, c_idx)

    g, delta, new_m, new_v = {}, {}, {}, {}
    after = [small_st[4]]
    for k in range(len(DONE)):
        parts, recv = _copies_wait("exchange%d_wait" % k, exchanging[k], _partial_copies, after)
        mine = [_sum_partials("sum_" + n, p_, r_, chip_idx) for n, p_, r_ in zip(DONE[k], parts, recv)]
        theirs = _share_halves("share%d" % k, mine)
        after = []
        for n, gm_, gt_ in zip(DONE[k], mine, theirs):
            flip = (lambda a: jnp.swapaxes(a, 0, 1)) if n in TRANSPOSED else (lambda a: a)
            rows = flip(w[n][0]).shape[0]
            w_, m_, v_ = [_pad_rows(flip(a[n][0])) for a in (w, m, v)]
            outs = _adamw_big("adamw_" + n, w_, gm_, gt_, m_, v_, c_idx)
            g[n], delta[n], new_m[n], new_v[n] = [flip(o[:rows])[None] for o in outs]
            after.append(outs[3])
    (own,), (slots,) = _copies_wait("small_wait", small_st, _small_copies, after)
    dev_idx = (2 * chip + lax.axis_index("c")).astype(jnp.int32).reshape(1)
    small_shapes = [w[n].shape for n in SMALL] + [dcw.shape, (1, 1)]
    small_sum = _unpack_small(_sum_small(own, slots, dev_idx), small_shapes)
    g.update({n: small_sum[i] for i, n in enumerate(SMALL)})
    cshard = w["conv_w"].shape[2]
    g["conv_w"] = lax.dynamic_slice_in_dim(small_sum[len(SMALL)], chip * cshard, cshard, axis=1)[None]
    loss_total = small_sum[len(SMALL) + 1].reshape(())
    sm_names = SMALL + ("conv_w",)
    sm_shapes = [w[n].shape for n in sm_names]
    d_s, m_s, v_s = _adamw("adamw_small", _pack_small([w[n] for n in sm_names]), _pack_small([g[n] for n in sm_names]),
                           _pack_small([m[n] for n in sm_names]), _pack_small([v[n] for n in sm_names]))
    for dst, src in ((delta, d_s), (new_m, m_s), (new_v, v_s)):
        for n, val in zip(sm_names, _unpack_small(src, sm_shapes)):
            dst[n] = val

    return (loss_total, dx[None], *[g[n] for n in _WEIGHTS], *[delta[n] for n in _WEIGHTS],
            *[new_m[n] for n in _WEIGHTS], *[new_v[n] for n in _WEIGHTS])
```

```python
import functools

import jax
import jax.numpy as jnp
from jax import lax
from jax.experimental import pallas as pl
from jax.experimental.pallas import tpu as pltpu

f32 = jnp.float32
bf16 = jnp.bfloat16
MESH = pl.DeviceIdType.MESH
HIGHEST = lax.Precision.HIGHEST

EPS = 1e-6
N_CHIPS = 4
N_DEV = 8
D_MODEL = 1024
D_FF = 2816
D_PLE = 256
GM_WIDTH = 1024
GM_HEADS = 8
CHUNK = 128
SSM_WIDTH = 1024
SSM_HEADS = 16
SSM_HEAD_DIM = 64
SSM_GROUPS = 2
SSM_STATE = 128
SSM_CONV = 4
CONV_DIM = SSM_WIDTH + 2 * SSM_GROUPS * SSM_STATE
IN_PROJ = 2 * GM_WIDTH + SSM_WIDTH + CONV_DIM + SSM_HEADS
LANES = 128
ZXD = SSM_WIDTH + CONV_DIM + LANES

ADAM_LR = 0.001
ADAM_B1 = 0.9
ADAM_B2 = 0.999
ADAM_EPS = 1e-08
ADAM_WD = 0.01
ADAM_STEP = 10

VMEM_LIMIT = 56 * 1024 * 1024


def _dot(a, b):
    return jnp.dot(a, b, preferred_element_type=f32)


def _dot_nt(a, b):
    return lax.dot_general(a, b, (((1,), (1,)), ((), ())), preferred_element_type=f32)


def _dot_tn(a, b):
    return lax.dot_general(a, b, (((0,), (0,)), ((), ())), preferred_element_type=f32)


def _rms(x, g):
    return x * lax.rsqrt(jnp.mean(x * x, axis=-1, keepdims=True) + EPS) * g


def _gelu(x):
    return 0.5 * x * (1.0 + lax.erf(x * 0.7071067811865476))


def _layernorm(x, g, b):
    mu = jnp.mean(x, axis=-1, keepdims=True)
    xc = x - mu
    return xc * lax.rsqrt(jnp.mean(xc * xc, axis=-1, keepdims=True) + EPS) * g + b


def _sigmoid(x):
    return 1.0 / (1.0 + jnp.exp(-x))


def _softplus(x):
    return jnp.maximum(x, 0.0) + jnp.log(1.0 + jnp.exp(-jnp.abs(x)))


def _full(shape):
    nd = len(shape)
    return pl.BlockSpec(shape, lambda *_: (0,) * nd, pipeline_mode=pl.Buffered(1))


def _acc(shape):
    nd = len(shape)
    return pl.BlockSpec(shape, lambda *_: (0,) * nd)


def _rows(tm, ncols):
    return pl.BlockSpec((tm, ncols), lambda i: (i, 0))


def _params(sem):
    return pltpu.CompilerParams(dimension_semantics=sem, vmem_limit_bytes=VMEM_LIMIT)


def _row_tile(rows, target, mult=8):
    best = rows
    for t in range(mult, min(rows, target) + 1, mult):
        if rows % t == 0:
            best = t
    return best if best <= target else rows


def _ffn_fwd(name, h, g, wg, wu, wd, pre=None, tm=256):
    T, D = h.shape
    F = wg.shape[0]
    tm = min(tm, T)

    def body(*refs):
        if pre is None:
            h_ref, g_ref, wg_ref, wu_ref, wd_ref, ho_ref, n_ref, a_ref, b_ref = refs
            hin = h_ref[...]
        else:
            (h_ref, ya_ref, yb_ref, wo_ref, g_ref, wg_ref, wu_ref, wd_ref,
             hi_ref, ho_ref, n_ref, a_ref, b_ref) = refs
            ga = ya_ref.shape[1]
            hin = h_ref[...] + _dot(ya_ref[...], wo_ref[:ga, :]) + _dot(yb_ref[...], wo_ref[ga:, :])
            hi_ref[...] = hin
        n = _rms(hin, g_ref[...]).astype(bf16)
        n_ref[...] = n
        a = _dot_nt(n, wg_ref[...]).astype(bf16)
        b = _dot_nt(n, wu_ref[...]).astype(bf16)
        a_ref[...] = a
        b_ref[...] = b
        af = a.astype(f32)
        hm = (af * _sigmoid(af) * b.astype(f32)).astype(bf16)
        ho_ref[...] = hin + 0.5 * _dot(hm, wd_ref[...])

    ins = [h] + (list(pre) if pre is not None else []) + [g, wg, wu, wd]
    in_specs = [_rows(tm, D)]
    if pre is not None:
        in_specs += [_rows(tm, pre[0].shape[1]), _rows(tm, pre[1].shape[1]), _full(pre[2].shape)]
    in_specs += [_full(g.shape), _full(wg.shape), _full(wu.shape), _full(wd.shape)]
    outs = [jax.ShapeDtypeStruct((T, D), f32), jax.ShapeDtypeStruct((T, D), bf16),
            jax.ShapeDtypeStruct((T, F), bf16), jax.ShapeDtypeStruct((T, F), bf16)]
    out_specs = [_rows(tm, D), _rows(tm, D), _rows(tm, F), _rows(tm, F)]
    if pre is not None:
        outs = [jax.ShapeDtypeStruct((T, D), f32)] + outs
        out_specs = [_rows(tm, D)] + out_specs
    return pl.pallas_call(body, name=name, grid=(T // tm,), in_specs=in_specs, out_specs=out_specs,
                          out_shape=outs, compiler_params=_params(("parallel",)))(*ins)


def _ffn_bwd(name, dh, hin, g, a, b, wg, wu, wd, wo=None, ga=0, tm=256):
    T, D = dh.shape
    F = wg.shape[0]
    tm = min(tm, T)

    def body(*refs):
        if wo is None:
            (dh_ref, hin_ref, g_ref, a_ref, b_ref, wg_ref, wu_ref, wd_ref,
             dhi_ref, da_ref, db_ref, hm_ref, dg_ref) = refs
        else:
            (dh_ref, hin_ref, g_ref, a_ref, b_ref, wg_ref, wu_ref, wd_ref, wo_ref,
             dhi_ref, da_ref, db_ref, hm_ref, dg_ref, dya_ref, dyb_ref) = refs

        @pl.when(pl.program_id(0) == 0)
        def _():
            dg_ref[...] = jnp.zeros_like(dg_ref)

        dh_ = dh_ref[...]
        dhb = (0.5 * dh_).astype(bf16)
        dhm = _dot_nt(dhb, wd_ref[...])
        af = a_ref[...].astype(f32)
        bf = b_ref[...].astype(f32)
        sg = _sigmoid(af)
        sl_ = af * sg
        da = (dhm * bf * (sg * (1.0 + af * (1.0 - sg)))).astype(bf16)
        db = (dhm * sl_).astype(bf16)
        da_ref[...] = da
        db_ref[...] = db
        hm_ref[...] = (sl_ * bf).astype(bf16)
        dn = _dot(da, wg_ref[...]) + _dot(db, wu_ref[...])
        _, vjp = jax.vjp(_rms, hin_ref[...], g_ref[...])
        dx, dg = vjp(dn)
        dhi = dh_ + dx
        dhi_ref[...] = dhi
        dg_ref[...] += dg
        if wo is not None:
            dhib = dhi.astype(bf16)
            dya_ref[...] = _dot_nt(dhib, wo_ref[:ga, :]).astype(bf16)
            dyb_ref[...] = _dot_nt(dhib, wo_ref[ga:, :]).astype(bf16)

    ins = [dh, hin, g, a, b, wg, wu, wd]
    in_specs = [_rows(tm, D), _rows(tm, D), _full(g.shape), _rows(tm, F), _rows(tm, F),
                _full(wg.shape), _full(wu.shape), _full(wd.shape)]
    act = jax.ShapeDtypeStruct((T, F), bf16)
    outs = [jax.ShapeDtypeStruct((T, D), f32), act, act, act, jax.ShapeDtypeStruct(g.shape, f32)]
    out_specs = [_rows(tm, D), _rows(tm, F), _rows(tm, F), _rows(tm, F), _acc(g.shape)]
    if wo is not None:
        gb = wo.shape[0] - ga
        ins += [wo]
        in_specs += [_full(wo.shape)]
        outs += [jax.ShapeDtypeStruct((T, ga), bf16), jax.ShapeDtypeStruct((T, gb), bf16)]
        out_specs += [_rows(tm, ga), _rows(tm, gb)]
    return pl.pallas_call(body, name=name, grid=(T // tm,), in_specs=in_specs, out_specs=out_specs,
                          out_shape=outs, compiler_params=_params(("arbitrary",)))(*ins)


def _matmul_tn(name, a, b, scale=1.0, tk=1024):
    T, M = a.shape
    N = b.shape[1]
    tk = min(tk, T)
    nk = T // tk
    tn = LANES * max(d for d in range(1, N // LANES + 1) if (N // LANES) % d == 0 and (d == 1 or M * d * LANES * 4 <= 6 * 1024 * 1024))

    def body(a_ref, b_ref, o_ref, acc):
        k = pl.program_id(1)

        @pl.when(k == 0)
        def _():
            acc[...] = jnp.zeros_like(acc)

        bb = b_ref[...]
        if scale != 1.0:
            bb = bb * scale
        acc[...] += _dot_tn(a_ref[...].astype(bf16), bb.astype(bf16))

        @pl.when(k == nk - 1)
        def _():
            o_ref[...] = acc[...].astype(bf16)

    return pl.pallas_call(
        body, name=name, grid=(N // tn, nk),
        in_specs=[pl.BlockSpec((tk, M), lambda j, k: (k, 0)), pl.BlockSpec((tk, tn), lambda j, k: (k, j))],
        out_specs=pl.BlockSpec((M, tn), lambda j, k: (0, j)),
        out_shape=jax.ShapeDtypeStruct((M, N), bf16), scratch_shapes=[pltpu.VMEM((M, tn), f32)],
        compiler_params=_params(("parallel", "arbitrary")))(a, b)


def _gm_pre(u, v, ln_g, ln_b):
    return _gelu(u), _layernorm(_gelu(v), ln_g, ln_b)


def _tril_mask():
    r = lax.broadcasted_iota(jnp.int32, (CHUNK, CHUNK), 0)
    c = lax.broadcasted_iota(jnp.int32, (CHUNK, CHUNK), 1)
    return c <= r


def _gm_mix(vnb, ws_ref, bst, mixed_sc, tm):
    mask = _tril_mask()
    for h in range(GM_HEADS):
        wt = jnp.where(mask, ws_ref[h], 0.0).astype(bf16)
        bias = bst[:, h:h + 1]
        for q in range(tm // CHUNK):
            rs = slice(q * CHUNK, (q + 1) * CHUNK)
            cs = slice(h * CHUNK, (h + 1) * CHUNK)
            mixed_sc[rs, cs] = _dot(wt, vnb[rs, cs]) + bias


def _mix_fwd(h1, gmix, w_uv, w_zxd, ln_g, ln_b, w_s, b_st, gout, tm=512):
    T, D = h1.shape
    tm = min(tm, T)
    G = GM_WIDTH

    def body(h_ref, g_ref, wuv_ref, wzxd_ref, lng_ref, lnb_ref, ws_ref, bst_ref, gout_ref,
             n_ref, uv_ref, z_ref, xbc_ref, dt_ref, ya_ref, mixed_sc):
        n = _rms(h_ref[...], g_ref[...]).astype(bf16)
        n_ref[...] = n
        u = _dot_nt(n, wuv_ref[:G, :]).astype(bf16)
        v = _dot_nt(n, wuv_ref[G:, :]).astype(bf16)
        uv_ref[:, :G] = u
        uv_ref[:, G:] = v
        z_ref[...] = _dot_nt(n, wzxd_ref[:SSM_WIDTH, :]).astype(bf16)
        xbc_ref[...] = _dot_nt(n, wzxd_ref[SSM_WIDTH:SSM_WIDTH + CONV_DIM, :]).astype(bf16)
        dt_ref[...] = _dot_nt(n, wzxd_ref[SSM_WIDTH + CONV_DIM:, :])
        ug, vn = _gm_pre(u.astype(f32), v.astype(f32), lng_ref[...], lnb_ref[...])
        _gm_mix(vn.astype(bf16), ws_ref, bst_ref[...], mixed_sc, tm)
        ya_ref[...] = _rms(ug * mixed_sc[...], gout_ref[...]).astype(bf16)

    ins = [h1, gmix, w_uv, w_zxd, ln_g, ln_b, w_s, b_st, gout]
    in_specs = [_rows(tm, D)] + [_full(x.shape) for x in ins[1:]]
    outs = [jax.ShapeDtypeStruct((T, D), bf16), jax.ShapeDtypeStruct((T, 2 * G), bf16),
            jax.ShapeDtypeStruct((T, SSM_WIDTH), bf16), jax.ShapeDtypeStruct((T, CONV_DIM), bf16),
            jax.ShapeDtypeStruct((T, LANES), f32), jax.ShapeDtypeStruct((T, G), bf16)]
    out_specs = [_rows(tm, D), _rows(tm, 2 * G), _rows(tm, SSM_WIDTH), _rows(tm, CONV_DIM), _rows(tm, LANES), _rows(tm, G)]
    return pl.pallas_call(body, name="mix_fwd", grid=(T // tm,), in_specs=in_specs, out_specs=out_specs,
                          out_shape=outs, scratch_shapes=[pltpu.VMEM((tm, G), f32)],
                          compiler_params=_params(("parallel",)))(*ins)


def _gm_bwd(uv, dya, ln_g, ln_b, w_s, b_st, gout, tm=256):
    T = uv.shape[0]
    tm = min(tm, T)
    G = GM_WIDTH

    def body(uv_ref, dya_ref, lng_ref, lnb_ref, ws_ref, bst_ref, gout_ref,
             duv_ref, dlng_ref, dlnb_ref, dws_ref, dbst_ref, dgout_ref, mixed_sc, dvn_sc):
        @pl.when(pl.program_id(0) == 0)
        def _():
            for r in (dlng_ref, dlnb_ref, dws_ref, dbst_ref, dgout_ref):
                r[...] = jnp.zeros_like(r)

        u = uv_ref[:, :G].astype(f32)
        v = uv_ref[:, G:].astype(f32)
        (ug, vn), pre_vjp = jax.vjp(_gm_pre, u, v, lng_ref[...], lnb_ref[...])
        vnb = vn.astype(bf16)
        _gm_mix(vnb, ws_ref, bst_ref[...], mixed_sc, tm)
        mixed = mixed_sc[...]
        _, out_vjp = jax.vjp(_rms, ug * mixed, gout_ref[...])
        dpre, dgout = out_vjp(dya_ref[...].astype(f32))
        dgout_ref[...] += dgout
        dug = dpre * mixed
        dmixed = dpre * ug
        mask = _tril_mask()
        lane = lax.broadcasted_iota(jnp.int32, (1, GM_HEADS), 1)
        dbst = jnp.zeros((CHUNK, GM_HEADS), f32)
        for h in range(GM_HEADS):
            wt = jnp.where(mask, ws_ref[h], 0.0).astype(bf16)
            cs = slice(h * CHUNK, (h + 1) * CHUNK)
            dw = jnp.zeros((CHUNK, CHUNK), f32)
            for q in range(tm // CHUNK):
                rs = slice(q * CHUNK, (q + 1) * CHUNK)
                dm = dmixed[rs, cs]
                dmb = dm.astype(bf16)
                dw = dw + _dot_nt(dmb, vnb[rs, cs])
                dbst = dbst + jnp.sum(dm, axis=1, keepdims=True) * (lane == h).astype(f32)
                dvn_sc[rs, cs] = _dot_tn(wt, dmb)
            dws_ref[h] += jnp.where(mask, dw, 0.0)
        dbst_ref[...] += dbst
        du, dv, dlng, dlnb = pre_vjp((dug, dvn_sc[...]))
        duv_ref[:, :G] = du.astype(bf16)
        duv_ref[:, G:] = dv.astype(bf16)
        dlng_ref[...] += dlng
        dlnb_ref[...] += dlnb

    ins = [uv, dya, ln_g, ln_b, w_s, b_st, gout]
    in_specs = [_rows(tm, 2 * G), _rows(tm, G)] + [_full(x.shape) for x in ins[2:]]
    outs = [jax.ShapeDtypeStruct((T, 2 * G), bf16)] + [jax.ShapeDtypeStruct(x.shape, f32) for x in (ln_g, ln_b, w_s, b_st, gout)]
    out_specs = [_rows(tm, 2 * G)] + [_acc(x.shape) for x in (ln_g, ln_b, w_s, b_st, gout)]
    return pl.pallas_call(body, name="gm_bwd", grid=(T // tm,), in_specs=in_specs, out_specs=out_specs,
                          out_shape=outs, scratch_shapes=[pltpu.VMEM((tm, G), f32), pltpu.VMEM((tm, G), f32)],
                          compiler_params=_params(("arbitrary",)))(*ins)


def _mix_bwd(dh, h1, gmix, duv, dzxd, w_uv, w_zxd, tm=512):
    T, D = dh.shape
    tm = min(tm, T)

    def body(dh_ref, h_ref, g_ref, duv_ref, dzxd_ref, wuv_ref, wzxd_ref, dhi_ref, dg_ref):
        @pl.when(pl.program_id(0) == 0)
        def _():
            dg_ref[...] = jnp.zeros_like(dg_ref)

        dn = _dot(duv_ref[...], wuv_ref[...]) + _dot(dzxd_ref[...], wzxd_ref[...])
        _, vjp = jax.vjp(_rms, h_ref[...], g_ref[...])
        dx, dg = vjp(dn)
        dhi_ref[...] = dh_ref[...] + dx
        dg_ref[...] += dg

    ins = [dh, h1, gmix, duv, dzxd, w_uv, w_zxd]
    in_specs = [_rows(tm, D), _rows(tm, D), _full(gmix.shape), _rows(tm, duv.shape[1]), _rows(tm, dzxd.shape[1]),
                _full(w_uv.shape), _full(w_zxd.shape)]
    return pl.pallas_call(body, name="mix_bwd", grid=(T // tm,), in_specs=in_specs,
                          out_specs=[_rows(tm, D), _acc(gmix.shape)],
                          out_shape=[jax.ShapeDtypeStruct((T, D), f32), jax.ShapeDtypeStruct(gmix.shape, f32)],
                          compiler_params=_params(("arbitrary",)))(*ins)


HALO = 16
PAIRS = SSM_HEADS // 2
PAIR_W = 2 * SSM_HEAD_DIM


def _split(x, n):
    parts = []
    for _ in range(n):
        p = x.astype(bf16)
        parts.append(p)
        x = x - p.astype(f32)
    return parts


def _dot_sel(x, sel_n, n):
    return _dot(jnp.concatenate(_split(x, n), axis=1), sel_n)


def _sel_dot(sel, x, n):
    return _dot(jnp.concatenate([sel] * n, axis=1), jnp.concatenate(_split(x, n), axis=0))


EXPAND_SPLIT = 3
REDUCE_SPLIT = 2


def _head_mats():
    ex = (jnp.arange(SSM_WIDTH)[None, :] // SSM_HEAD_DIM == jnp.arange(LANES)[:, None]).astype(bf16)
    return jnp.tile(ex, (EXPAND_SPLIT, 1)), jnp.tile(ex.T, (REDUCE_SPLIT, 1))


def _shift_mat(rows, cols, off):
    r = lax.broadcasted_iota(jnp.int32, (rows, cols), 0)
    c = lax.broadcasted_iota(jnp.int32, (rows, cols), 1)
    return (c == r + off).astype(bf16)


def _ssd_front(c, xbc_ref, halo_ref, dtr_ref, cw_ref, cb_ref, dtb_ref, alog_ref):
    halo = halo_ref[...]
    ext = jnp.concatenate([jnp.where(c > 0, halo, jnp.zeros_like(halo)), xbc_ref[...]], axis=0)
    taps = [_dot(_shift_mat(CHUNK, HALO + CHUNK, HALO - SSM_CONV + 1 + j), ext) for j in range(SSM_CONV - 1)]
    taps.append(xbc_ref[...].astype(f32))
    xc = cb_ref[...] + cw_ref[0:1, :] * taps[0]
    for j in range(1, SSM_CONV):
        xc = xc + cw_ref[j:j + 1, :] * taps[j]
    sg = _sigmoid(xc)
    xa = xc * sg
    dt = _softplus(dtr_ref[...] + dtb_ref[...])
    a = -jnp.exp(alog_ref[...])
    acs = jnp.dot(_tril_mask().astype(f32), dt * a, preferred_element_type=f32, precision=HIGHEST)
    return taps, xc, sg, xa, dt, a, acs


def _ssd_wide(xa, dt, acs, dsk, ex):
    dt_x = _dot_sel(dt, ex, EXPAND_SPLIT)
    acs_x = _dot_sel(acs, ex, EXPAND_SPLIT)
    dsk_x = _dot_sel(jnp.broadcast_to(dsk, (8, LANES)), ex, EXPAND_SPLIT)[0:1]
    e_x = jnp.exp(acs_x)
    r_x = jnp.exp(acs_x[CHUNK - 1:CHUNK, :] - acs_x)
    xs = xa[:, :SSM_WIDTH]
    xd = xs * dt_x
    return dt_x, dsk_x, e_x, r_x, xs, xd, xd * r_x


def _pair_stack(v, lo):
    return jnp.concatenate([jnp.where(lo, v, 0.0), jnp.where(lo, 0.0, v)], axis=0)


def _ssd_pair(j, acs, acs_t, cb):
    out = []
    tril = _tril_mask()
    for h in (2 * j, 2 * j + 1):
        dk = jnp.exp(jnp.where(tril, acs[:, h:h + 1] - acs_t[h:h + 1, :], -jnp.inf))
        out.append((dk, cb * dk))
    return out


def _pair_col(row_lo, tot, j):
    return jnp.exp(jnp.where(row_lo, tot[:, 2 * j:2 * j + 1], tot[:, 2 * j + 1:2 * j + 2]))


def _gated_norm(y, z, g):
    yg = y * (z * _sigmoid(z))
    half = SSM_WIDTH // SSM_GROUPS
    parts = []
    for k in range(SSM_GROUPS):
        s = yg[:, k * half:(k + 1) * half]
        parts.append(s * lax.rsqrt(jnp.mean(s * s, axis=-1, keepdims=True) + EPS))
    return jnp.concatenate(parts, axis=1) * g


def _group_mats(xa):
    out = []
    for g in range(SSM_GROUPS):
        bm = xa[:, SSM_WIDTH + g * SSM_STATE:SSM_WIDTH + (g + 1) * SSM_STATE].astype(bf16)
        cm = xa[:, SSM_WIDTH + (SSM_GROUPS + g) * SSM_STATE:SSM_WIDTH + (SSM_GROUPS + g + 1) * SSM_STATE].astype(bf16)
        out.append((cm, bm, _dot_nt(cm, bm)))
    return out


def _ssd_in_specs(nc, rev):
    def ci(i):
        return nc - 1 - i if rev else i
    hp = CHUNK // HALO
    return [pl.BlockSpec((CHUNK, CONV_DIM), lambda i: (ci(i), 0)),
            pl.BlockSpec((HALO, CONV_DIM), lambda i: (jnp.maximum(ci(i) * hp - 1, 0), 0)),
            pl.BlockSpec((CHUNK, SSM_WIDTH), lambda i: (ci(i), 0)),
            pl.BlockSpec((CHUNK, LANES), lambda i: (ci(i), 0))]


def _ssd_fwd(xbc, z, dtr, conv_w, conv_b, dt_bias, a_log, d_skip, ssm_norm):
    T = xbc.shape[0]
    nc = T // CHUNK
    N = SSM_STATE

    def body(xbc_ref, halo_ref, z_ref, dtr_ref, cw_ref, cb_ref, dtb_ref, alog_ref, dsk_ref, g_ref, ex_ref,
             yb_ref, sprev_ref, s_sc):
        c = pl.program_id(0)

        @pl.when(c == 0)
        def _():
            s_sc[...] = jnp.zeros_like(s_sc)

        _, _, _, xa, dt, _, acs = _ssd_front(c, xbc_ref, halo_ref, dtr_ref, cw_ref, cb_ref, dtb_ref, alog_ref)
        _, dsk_x, e_x, _, xs, xd, gm = _ssd_wide(xa, dt, acs, dsk_ref[...], ex_ref[...])
        acs_t = acs.T
        tot = acs[CHUNK - 1:CHUNK, :]
        groups = _group_mats(xa)
        lo = lax.broadcasted_iota(jnp.int32, (CHUNK, PAIR_W), 1) < SSM_HEAD_DIM
        row_lo = lax.broadcasted_iota(jnp.int32, (PAIR_W, 1), 0) < SSM_HEAD_DIM
        ys = []
        for j in range(PAIRS):
            cmb, bmb, cb = groups[j // (PAIRS // SSM_GROUPS)]
            ps = slice(j * PAIR_W, (j + 1) * PAIR_W)
            (_, m0), (_, m1) = _ssd_pair(j, acs, acs_t, cb)
            sp = s_sc[j]
            yd = _dot(jnp.concatenate([m0, m1], axis=1).astype(bf16), _pair_stack(xd[:, ps], lo).astype(bf16))
            ys.append(yd + e_x[:, ps] * _dot_nt(cmb, sp.astype(bf16)))
            sprev_ref[0, j] = sp
            s_sc[j] = _pair_col(row_lo, tot, j) * sp + _dot_tn(gm[:, ps].astype(bf16), bmb)
        y = jnp.concatenate(ys, axis=1) + xs * dsk_x
        yb_ref[...] = _gated_norm(y, z_ref[...].astype(f32), g_ref[...]).astype(bf16)

    params = [conv_w, conv_b, dt_bias, a_log, d_skip, ssm_norm, _head_mats()[0]]
    return pl.pallas_call(
        body, name="ssd_fwd", grid=(nc,),
        in_specs=_ssd_in_specs(nc, False) + [_full(x.shape) for x in params],
        out_specs=[pl.BlockSpec((CHUNK, SSM_WIDTH), lambda i: (i, 0)), pl.BlockSpec((1, PAIRS, PAIR_W, N), lambda i: (i, 0, 0, 0))],
        out_shape=[jax.ShapeDtypeStruct((T, SSM_WIDTH), bf16), jax.ShapeDtypeStruct((nc, PAIRS, PAIR_W, N), f32)],
        scratch_shapes=[pltpu.VMEM((PAIRS, PAIR_W, N), f32)],
        compiler_params=_params(("arbitrary",)))(xbc, xbc, z, dtr, *params)


def _ssd_bwd(xbc, z, dtr, sprev, dyb, conv_w, conv_b, dt_bias, a_log, d_skip, ssm_norm):
    T = xbc.shape[0]
    nc = T // CHUNK
    H, N = SSM_HEADS, SSM_STATE
    PG = PAIRS // SSM_GROUPS

    def body(xbc_ref, halo_ref, z_ref, dtr_ref, sprev_ref, dyb_ref, cw_ref, cb_ref, dtb_ref, alog_ref, dsk_ref, g_ref,
             ex_ref, rd_ref, dzxd_ref, dcw_ref, dcb_ref, ddtb_ref, dalog_ref, ddsk_ref, dg_ref, ds_sc, next_sc):
        i = pl.program_id(0)
        c = nc - 1 - i

        @pl.when(i == 0)
        def _():
            ds_sc[...] = jnp.zeros_like(ds_sc)
            next_sc[...] = jnp.zeros_like(next_sc)
            for r_ in (dcw_ref, dcb_ref, ddtb_ref, dalog_ref, ddsk_ref, dg_ref):
                r_[...] = jnp.zeros_like(r_)

        taps, xc, sg, xa, dt, a, acs = _ssd_front(c, xbc_ref, halo_ref, dtr_ref, cw_ref, cb_ref, dtb_ref, alog_ref)
        dt_x, dsk_x, e_x, r_x, xs, xd, gm = _ssd_wide(xa, dt, acs, dsk_ref[...], ex_ref[...])
        acs_t = acs.T
        tot = acs[CHUNK - 1:CHUNK, :]
        groups = _group_mats(xa)
        lo = lax.broadcasted_iota(jnp.int32, (CHUNK, PAIR_W), 1) < SSM_HEAD_DIM
        row_lo = lax.broadcasted_iota(jnp.int32, (PAIR_W, 1), 0) < SSM_HEAD_DIM
        pairs, zs, yds = [], [], []
        for j in range(PAIRS):
            cmb, _, cb = groups[j // PG]
            ps = slice(j * PAIR_W, (j + 1) * PAIR_W)
            pairs.append(_ssd_pair(j, acs, acs_t, cb))
            (_, m0), (_, m1) = pairs[j]
            zs.append(_dot_nt(cmb, sprev_ref[0, j].astype(bf16)))
            yds.append(_dot(jnp.concatenate([m0, m1], axis=1).astype(bf16), _pair_stack(xd[:, ps], lo).astype(bf16)))
        zf = jnp.concatenate(zs, axis=1)
        y = jnp.concatenate(yds, axis=1) + e_x * zf + xs * dsk_x
        _, gn_vjp = jax.vjp(_gated_norm, y, z_ref[...].astype(f32), g_ref[...])
        dy, dz, dg = gn_vjp(dyb_ref[...].astype(f32))
        dg_ref[...] += dg
        dzxd_ref[:, :SSM_WIDTH] = dz.astype(bf16)

        lane = lax.broadcasted_iota(jnp.int32, (1, LANES), 1)
        sub = lax.broadcasted_iota(jnp.int32, (LANES, 1), 0)
        dacs = jnp.zeros((CHUNK, LANES), f32)
        dacs_r = jnp.zeros((LANES, CHUNK), f32)
        dtot = jnp.zeros((1, LANES), f32)
        dcb = [jnp.zeros((CHUNK, CHUNK), f32) for _ in range(SSM_GROUPS)]
        dcm = [jnp.zeros((CHUNK, N), f32) for _ in range(SSM_GROUPS)]
        dbm = [jnp.zeros((CHUNK, N), f32) for _ in range(SSM_GROUPS)]
        dxds, dgms = [], []
        for j in range(PAIRS):
            g = j // PG
            cmb, bmb, _ = groups[g]
            ps = slice(j * PAIR_W, (j + 1) * PAIR_W)
            (dk0, m0), (dk1, m1) = pairs[j]
            oh0, oh1 = (lane == 2 * j).astype(f32), (lane == 2 * j + 1).astype(f32)
            dyp = dy[:, ps]
            dy2 = _pair_stack(dyp, lo).astype(bf16)
            dm2 = _dot_nt(dy2, xd[:, ps].astype(bf16))
            m2 = jnp.concatenate([m0, m1], axis=0)
            dxds.append(_dot_tn(m2.astype(bf16), dy2))
            w2 = dm2 * m2
            rs = jnp.sum(w2, axis=1, keepdims=True)
            dacs = dacs + rs[:CHUNK] * oh0 + rs[CHUNK:] * oh1
            dacs_r = dacs_r - ((sub == 2 * j).astype(f32) * jnp.sum(w2[:CHUNK], axis=0, keepdims=True)
                               + (sub == 2 * j + 1).astype(f32) * jnp.sum(w2[CHUNK:], axis=0, keepdims=True))
            dcb[g] = dcb[g] + dm2[:CHUNK] * dk0 + dm2[CHUNK:] * dk1
            sp = sprev_ref[0, j]
            dzb = (dyp * e_x[:, ps]).astype(bf16)
            dcm[g] = dcm[g] + _dot(dzb, sp.astype(bf16))
            dsn = ds_sc[j]
            dsnb = dsn.astype(bf16)
            et = _pair_col(row_lo, tot, j)
            rr = jnp.sum(dsn * sp, axis=1, keepdims=True) * et
            dtot = dtot + jnp.sum(rr[:SSM_HEAD_DIM]) * oh0 + jnp.sum(rr[SSM_HEAD_DIM:]) * oh1
            dgms.append(_dot_nt(bmb, dsnb))
            dbm[g] = dbm[g] + _dot(gm[:, ps].astype(bf16), dsnb)
            ds_sc[j] = _dot_tn(dzb, cmb) + et * dsn
        dgm = jnp.concatenate(dgms, axis=1)
        dxd = jnp.concatenate(dxds, axis=1) + dgm * r_x
        dr = dgm * gm
        red = _dot_sel(jnp.concatenate([dy * e_x * zf - dr, dr, dxd * xs, dy * xs], axis=0), rd_ref[...], REDUCE_SPLIT)
        rowi = lax.broadcasted_iota(jnp.int32, (CHUNK, 1), 0)
        dtot = dtot + jnp.sum(red[CHUNK:2 * CHUNK], axis=0, keepdims=True)
        dacs = dacs + red[:CHUNK] + dacs_r.T + jnp.where(rowi == CHUNK - 1, dtot, 0.0)
        r2 = lax.broadcasted_iota(jnp.int32, (CHUNK, CHUNK), 0)
        c2 = lax.broadcasted_iota(jnp.int32, (CHUNK, CHUNK), 1)
        dadt = jnp.dot((c2 >= r2).astype(f32), dacs, preferred_element_type=f32, precision=HIGHEST)
        ddt = red[2 * CHUNK:3 * CHUNK] + dadt * a
        dalog_ref[...] += jnp.sum(dadt * dt, axis=0, keepdims=True) * a
        ddsk_ref[...] += jnp.sum(red[3 * CHUNK:], axis=0, keepdims=True)
        ddtr = jnp.where(lane < H, ddt * _sigmoid(dtr_ref[...] + dtb_ref[...]), 0.0)
        ddtb_ref[...] += jnp.sum(ddtr, axis=0, keepdims=True)
        dzxd_ref[:, SSM_WIDTH + CONV_DIM:] = ddtr.astype(bf16)
        dxa_bm, dxa_cm = [], []
        for g in range(SSM_GROUPS):
            cmb, bmb, _ = groups[g]
            dcbb = dcb[g].astype(bf16)
            dxa_bm.append(dbm[g] + _dot_tn(dcbb, cmb))
            dxa_cm.append(dcm[g] + _dot(dcbb, bmb))
        dxc = jnp.concatenate([dy * dsk_x + dxd * dt_x] + dxa_bm + dxa_cm, axis=1) * (sg * (1.0 + xc * (1.0 - sg)))
        ext = jnp.concatenate([dxc, next_sc[...]], axis=0)
        dxbc = cw_ref[SSM_CONV - 1:SSM_CONV, :] * dxc
        for s in range(1, SSM_CONV):
            dxbc = dxbc + cw_ref[SSM_CONV - 1 - s:SSM_CONV - s, :] * _sel_dot(_shift_mat(CHUNK, CHUNK + HALO, s), ext, 2)
        dzxd_ref[:, SSM_WIDTH:SSM_WIDTH + CONV_DIM] = dxbc.astype(bf16)
        dcw_ref[...] += jnp.concatenate([jnp.sum(dxc * t, axis=0, keepdims=True) for t in taps], axis=0)
        dcb_ref[...] += jnp.sum(dxc, axis=0, keepdims=True)
        next_sc[...] = dxc[0:HALO, :]

    params = [conv_w, conv_b, dt_bias, a_log, d_skip, ssm_norm]
    mats = list(_head_mats())

    def rc(i):
        return nc - 1 - i

    in_specs = (_ssd_in_specs(nc, True)
                + [pl.BlockSpec((1, PAIRS, PAIR_W, N), lambda i: (rc(i), 0, 0, 0)), pl.BlockSpec((CHUNK, SSM_WIDTH), lambda i: (rc(i), 0))]
                + [_full(x.shape) for x in params + mats])
    return pl.pallas_call(
        body, name="ssd_bwd", grid=(nc,), in_specs=in_specs,
        out_specs=[pl.BlockSpec((CHUNK, ZXD), lambda i: (rc(i), 0))] + [_acc(x.shape) for x in params],
        out_shape=[jax.ShapeDtypeStruct((T, ZXD), bf16)] + [jax.ShapeDtypeStruct(x.shape, f32) for x in params],
        scratch_shapes=[pltpu.VMEM((PAIRS, PAIR_W, N), f32), pltpu.VMEM((HALO, CONV_DIM), f32)],
        compiler_params=_params(("arbitrary",)))(xbc, xbc, z, dtr, sprev, dyb, *params, *mats)


def _tail(h3, p, tgt, gp, wpg, bpg, wpp, gf, tm=512):
    T, D = h3.shape
    tm = min(tm, T)

    def head(gpre, pp, h, gf_, t):
        gate = _sigmoid(gpre)
        y = _rms(h + gate * pp, gf_)
        err = y - t
        return 0.5 * jnp.sum(jnp.mean(err * err, axis=-1))

    def body(h_ref, p_ref, t_ref, gp_ref, wpg_ref, bpg_ref, wpp_ref, gf_ref,
             dh_ref, loss_ref, dgp_ref, dwpg_ref, dbpg_ref, dwpp_ref, dgf_ref):
        @pl.when(pl.program_id(0) == 0)
        def _():
            for r in (loss_ref, dgp_ref, dwpg_ref, dbpg_ref, dwpp_ref, dgf_ref):
                r[...] = jnp.zeros_like(r)

        h = h_ref[...]
        npf, np_vjp = jax.vjp(_rms, h, gp_ref[...])
        npb = npf.astype(bf16)
        pb = p_ref[...].astype(bf16)
        gpre = _dot(npb, wpg_ref[...]) + bpg_ref[...]
        kp, _, cp = wpp_ref.shape
        pp = jnp.concatenate([_dot(pb, wpp_ref[k]) for k in range(kp)], axis=1)
        loss, head_vjp = jax.vjp(head, gpre, pp, h, gf_ref[...], t_ref[...])
        dgpre, dpp, dh_a, dgf, _ = head_vjp(jnp.ones((), f32))
        loss_ref[...] += loss
        dgf_ref[...] += dgf
        dbpg_ref[...] += jnp.sum(dgpre, axis=0, keepdims=True)
        dgb = dgpre.astype(bf16)
        dwpg_ref[...] += _dot_tn(npb, dgb)
        dppb = dpp.astype(bf16)
        for k in range(kp):
            dwpp_ref[k] += _dot_tn(pb, dppb[:, k * cp:(k + 1) * cp])
        dh_b, dgp = np_vjp(_dot_nt(dgb, wpg_ref[...]))
        dgp_ref[...] += dgp
        dh_ref[...] = dh_a + dh_b

    ins = [h3, p, tgt, gp, wpg, bpg, wpp, gf]
    in_specs = [_rows(tm, D), _rows(tm, p.shape[1]), _rows(tm, D)] + [_full(x.shape) for x in ins[3:]]
    acc_shapes = [(1, LANES), gp.shape, wpg.shape, bpg.shape, wpp.shape, gf.shape]
    return pl.pallas_call(
        body, name="tail", grid=(T // tm,), in_specs=in_specs,
        out_specs=[_rows(tm, D)] + [_acc(s) for s in acc_shapes],
        out_shape=[jax.ShapeDtypeStruct((T, D), f32)] + [jax.ShapeDtypeStruct(s, f32) for s in acc_shapes],
        compiler_params=_params(("arbitrary",)))(*ins)


def _adamw(name, w, g, m, v, tr=256):
    R, C = w.shape
    tr = _row_tile(R, tr)

    def body(w_ref, g_ref, m_ref, v_ref, d_ref, mo_ref, vo_ref):
        g_ = g_ref[...]
        m_ = ADAM_B1 * m_ref[...] + (1.0 - ADAM_B1) * g_
        v_ = ADAM_B2 * v_ref[...] + (1.0 - ADAM_B2) * jnp.square(g_)
        m_hat = m_ / (1.0 - ADAM_B1 ** ADAM_STEP)
        v_hat = v_ / (1.0 - ADAM_B2 ** ADAM_STEP)
        d_ref[...] = -ADAM_LR * (m_hat / (jnp.sqrt(v_hat) + ADAM_EPS) + ADAM_WD * w_ref[...])
        mo_ref[...] = m_
        vo_ref[...] = v_

    spec = pl.BlockSpec((tr, C), lambda i: (i, 0))
    return pl.pallas_call(body, name=name, grid=(R // tr,), in_specs=[spec] * 4, out_specs=[spec] * 3,
                          out_shape=[jax.ShapeDtypeStruct((R, C), f32)] * 3,
                          compiler_params=_params(("parallel",)))(w, g, m, v)


HBM = pl.BlockSpec(memory_space=pltpu.HBM)


def _me():
    return lax.axis_index("x"), lax.axis_index("y"), lax.axis_index("c")


def _other_chips(x, y):
    return [(1 - x, y), (x, 1 - y), (1 - x, 1 - y)]


def _remote(src, dst, send_sem, recv_sem, dev):
    return pltpu.make_async_remote_copy(src_ref=src, dst_ref=dst, send_sem=send_sem, recv_sem=recv_sem,
                                        device_id=dev, device_id_type=MESH)


def _sems(n):
    return [pltpu.SemaphoreType.DMA((n,)), pltpu.SemaphoreType.DMA((n,))]


def _gather_weights(shards, split):
    n = len(shards)

    def body(*refs):
        ins, outs = refs[:n], refs[n:2 * n]
        own_send, own_recv, ici_send, ici_recv, d2d_send, d2d_recv = refs[2 * n:]
        x, y, c = _me()
        my_chip = 2 * x + y
        sibling = (x, y, 1 - c)
        chips = _other_chips(x, y)

        def rows(i, half):
            hr = shards[i].shape[0] // 2
            return pl.ds(half * hr, hr) if split[i] else pl.ds(0, shards[i].shape[0])

        sends = []
        for i in range(n):
            for j, chip in enumerate(chips):
                cp = _remote(ins[i].at[rows(i, c)], outs[i].at[my_chip, rows(i, c)],
                             ici_send.at[3 * i + j], ici_recv.at[3 * i + j], (*chip, c))
                cp.start()
                sends.append(cp)
            cp = _remote(ins[i], outs[i].at[my_chip], own_send.at[i], own_recv.at[i], sibling)
            cp.start()
            sends.append(cp)
        for i in range(n):
            for j, chip in enumerate(chips):
                s = 3 * i + j
                land = outs[i].at[2 * chip[0] + chip[1], rows(i, c)]
                _remote(land, land, ici_send.at[s], ici_recv.at[s], (*chip, c)).wait_recv()
                if split[i]:
                    cp = _remote(land, land, d2d_send.at[s], d2d_recv.at[s], sibling)
                    cp.start()
                    sends.append(cp)
        for i in range(n):
            _remote(ins[i], outs[i].at[my_chip], own_send.at[i], own_recv.at[i], sibling).wait_recv()
            if split[i]:
                for j, chip in enumerate(chips):
                    s = 3 * i + j
                    land = outs[i].at[2 * chip[0] + chip[1], rows(i, 1 - c)]
                    _remote(land, land, d2d_send.at[s], d2d_recv.at[s], sibling).wait_recv()
        for cp in sends:
            cp.wait_send()

    return pl.pallas_call(
        body, name="gather_weights", out_shape=[jax.ShapeDtypeStruct((N_CHIPS,) + s.shape, s.dtype) for s in shards],
        in_specs=[HBM] * n, out_specs=[HBM] * n,
        scratch_shapes=_sems(n) + _sems(3 * n) + _sems(3 * n))(*shards)


def _swap_halves(name, grads):
    n = len(grads)

    def body(*refs):
        ins, outs, send, recv = refs[:n], refs[n:2 * n], refs[2 * n], refs[2 * n + 1]
        x, y, c = _me()
        copies = []
        for i in range(n):
            hr = grads[i].shape[1] // 2
            cp = _remote(ins[i].at[:, pl.ds((1 - c) * hr, hr), :], outs[i], send.at[i], recv.at[i], (x, y, 1 - c))
            cp.start()
            copies.append(cp)
        for cp in copies:
            cp.wait()

    return pl.pallas_call(
        body, name=name,
        out_shape=[jax.ShapeDtypeStruct((g.shape[0], g.shape[1] // 2, g.shape[2]), g.dtype) for g in grads],
        in_specs=[HBM] * n, out_specs=[HBM] * n, scratch_shapes=_sems(n))(*grads)


def _add_halves(name, grads, other, c_idx, th=592):
    K, R, C = grads.shape
    H = R // 2
    th = _row_tile(H, th, 16)
    nb = H // th

    def body(c_ref, g_ref, o_ref, out_ref):
        out_ref[...] = (g_ref[...].astype(f32) + o_ref[...].astype(f32)).astype(bf16)

    grid_spec = pltpu.PrefetchScalarGridSpec(
        num_scalar_prefetch=1, grid=(nb,),
        in_specs=[pl.BlockSpec((K, th, C), lambda i, c: (0, c[0] * nb + i, 0)),
                  pl.BlockSpec((K, th, C), lambda i, c: (0, i, 0))],
        out_specs=pl.BlockSpec((K, th, C), lambda i, c: (0, i, 0)))
    return pl.pallas_call(body, name=name, grid_spec=grid_spec,
                          out_shape=jax.ShapeDtypeStruct((K, H, C), bf16),
                          compiler_params=_params(("parallel",)))(c_idx, grads, other)


SEM = pl.BlockSpec(memory_space=pltpu.SEMAPHORE)
ANY = pl.BlockSpec(memory_space=pl.ANY)
EFFECT = pltpu.SideEffectType.DATAFLOW_SIDE_EFFECTING


def _copies_start(name, srcs, land_shapes, n_copies, make_copies, after):
    ns, nl = len(srcs), len(land_shapes)
    lands = [lax.empty(s.shape, s.dtype) for s in land_shapes]

    def body(*refs):
        src_refs, land_refs = refs[:ns], refs[ns:ns + nl]
        send, recv, token = refs[ns + nl + 1], refs[ns + nl + 2], refs[-1]
        for cp in make_copies(src_refs, land_refs, send, recv):
            cp.start()
        token[...] = jnp.zeros_like(token)

    buffers = list(srcs) + lands
    out = pl.pallas_call(
        body, name=name,
        out_shape=(pltpu.SemaphoreType.DMA((n_copies,)), pltpu.SemaphoreType.DMA((n_copies,)),
                   *[pltpu.HBM(b.shape, b.dtype) for b in buffers], jax.ShapeDtypeStruct((8, LANES), f32)),
        in_specs=[HBM] * (ns + nl) + [ANY],
        out_specs=(SEM, SEM, *[HBM] * (ns + nl), pl.BlockSpec(memory_space=pltpu.VMEM)),
        input_output_aliases={i: 2 + i for i in range(ns + nl)},
        compiler_params=pltpu.CompilerParams(has_side_effects=EFFECT),
    )(*[pltpu.with_memory_space_constraint(b, pltpu.HBM) for b in buffers], after)
    return out[0], out[1], list(out[2:2 + ns]), list(out[2 + ns:2 + ns + nl]), out[-1]


def _copies_wait(name, started, make_copies, after):
    send, recv, srcs, lands, _ = started
    ns, nl = len(srcs), len(lands)
    after = list(after)

    def body(*refs):
        src_refs, land_refs = refs[:ns], refs[ns:ns + nl]
        for cp in make_copies(src_refs, land_refs, refs[ns + nl], refs[ns + nl + 1]):
            cp.wait_send()
            cp.wait_recv()

    buffers = list(srcs) + list(lands)
    out = pl.pallas_call(
        body, name=name, out_shape=tuple(pltpu.HBM(b.shape, b.dtype) for b in buffers),
        in_specs=[HBM] * (ns + nl) + [SEM, SEM] + [ANY] * len(after), out_specs=tuple([HBM] * (ns + nl)),
        input_output_aliases={i: i for i in range(ns + nl)},
        compiler_params=pltpu.CompilerParams(has_side_effects=EFFECT),
    )(*buffers, send, recv, *after)
    return list(out[:ns]), list(out[ns:])


def _gather_copies(src_refs, land_refs, send, recv):
    x, y, c = _me()
    my_chip = 2 * x + y
    peers = [(*chip, c) for chip in _other_chips(x, y)] + [(x, y, 1 - c)]
    return [_remote(src_refs[i], land_refs[i].at[my_chip], send.at[4 * i + j], recv.at[4 * i + j], peer)
            for i in range(len(src_refs)) for j, peer in enumerate(peers)]


def _partial_copies(src_refs, land_refs, send, recv):
    x, y, c = _me()
    return [_remote(src_refs[i].at[2 * chip[0] + chip[1]], land_refs[i].at[j], send.at[3 * i + j], recv.at[3 * i + j], (*chip, c))
            for i in range(len(src_refs)) for j, chip in enumerate(_other_chips(x, y))]


def _small_copies(src_refs, land_refs, send, recv):
    x, y, c = _me()
    return [_remote(src_refs[0], land_refs[0].at[k - 1], send.at[k - 1], recv.at[k - 1], (x ^ (k >> 2), y ^ ((k >> 1) & 1), c ^ (k & 1)))
            for k in range(1, N_DEV)]


def _sum_small(own, slots, dev_idx):
    R, C = own.shape

    def body(dev_ref, own_ref, s_ref, o_ref):
        me = dev_ref[0]
        acc = jnp.zeros((R, C), f32)
        for d in range(N_DEV):
            k = me ^ d
            acc = acc + jnp.where(k == 0, own_ref[...], s_ref[jnp.maximum(k - 1, 0)])
        o_ref[...] = acc

    grid_spec = pltpu.PrefetchScalarGridSpec(
        num_scalar_prefetch=1, grid=(1,),
        in_specs=[pl.BlockSpec((R, C), lambda i, dev: (0, 0)), pl.BlockSpec((N_DEV - 1, R, C), lambda i, dev: (0, 0, 0))],
        out_specs=pl.BlockSpec((R, C), lambda i, dev: (0, 0)))
    return pl.pallas_call(body, name="sum_small", grid_spec=grid_spec, out_shape=jax.ShapeDtypeStruct((R, C), f32),
                          compiler_params=_params(("arbitrary",)))(dev_idx, own, slots)


def _sum_partials(name, part, recv, chip_idx, th=592):
    K, H, C = part.shape
    th = _row_tile(H, th, 16)

    def body(chip_ref, p_ref, r_ref, o_ref):
        acc = p_ref[...].astype(f32)
        for j in range(3):
            acc = acc + r_ref[j].astype(f32)
        o_ref[...] = acc

    grid_spec = pltpu.PrefetchScalarGridSpec(
        num_scalar_prefetch=1, grid=(H // th,),
        in_specs=[pl.BlockSpec((None, th, C), lambda i, chip: (chip[0], i, 0)),
                  pl.BlockSpec((3, th, C), lambda i, chip: (0, i, 0))],
        out_specs=pl.BlockSpec((th, C), lambda i, chip: (i, 0)))
    return pl.pallas_call(body, name=name, grid_spec=grid_spec, out_shape=jax.ShapeDtypeStruct((H, C), f32),
                          compiler_params=_params(("parallel",)))(chip_idx, part, recv)


def _share_halves(name, halves):
    n = len(halves)

    def body(*refs):
        ins, outs, send, recv = refs[:n], refs[n:2 * n], refs[2 * n], refs[2 * n + 1]
        x, y, c = _me()
        copies = []
        for i in range(n):
            cp = _remote(ins[i], outs[i], send.at[i], recv.at[i], (x, y, 1 - c))
            cp.start()
            copies.append(cp)
        for cp in copies:
            cp.wait()

    return pl.pallas_call(
        body, name=name, out_shape=[jax.ShapeDtypeStruct(h.shape, h.dtype) for h in halves],
        in_specs=[HBM] * n, out_specs=[HBM] * n, scratch_shapes=_sems(n))(*halves)


def _adamw_big(name, w, g_mine, g_theirs, m, v, c_idx, tr=320):
    R, C = w.shape
    H = R // 2
    tr = _row_tile(H, tr)
    nb = H // tr

    def body(c_ref, w_ref, gm_ref, gt_ref, m_ref, v_ref, g_ref, d_ref, mo_ref, vo_ref):
        g_ = jnp.where(pl.program_id(0) // nb == c_ref[0], gm_ref[...], gt_ref[...])
        g_ref[...] = g_
        m_ = ADAM_B1 * m_ref[...] + (1.0 - ADAM_B1) * g_
        v_ = ADAM_B2 * v_ref[...] + (1.0 - ADAM_B2) * jnp.square(g_)
        m_hat = m_ / (1.0 - ADAM_B1 ** ADAM_STEP)
        v_hat = v_ / (1.0 - ADAM_B2 ** ADAM_STEP)
        d_ref[...] = -ADAM_LR * (m_hat / (jnp.sqrt(v_hat) + ADAM_EPS) + ADAM_WD * w_ref[...])
        mo_ref[...] = m_
        vo_ref[...] = v_

    full = pl.BlockSpec((tr, C), lambda i, c: (i, 0))
    half = pl.BlockSpec((tr, C), lambda i, c: (i % nb, 0))
    grid_spec = pltpu.PrefetchScalarGridSpec(num_scalar_prefetch=1, grid=(2 * nb,),
                                             in_specs=[full, half, half, full, full], out_specs=[full] * 4)
    return pl.pallas_call(body, name=name, grid_spec=grid_spec, out_shape=[jax.ShapeDtypeStruct((R, C), f32)] * 4,
                          compiler_params=_params(("parallel",)))(c_idx, w, g_mine, g_theirs, m, v)


BIG = ("ffn1_w_gate", "ffn1_w_up", "ffn1_w_down", "w_in", "w_out", "ffn2_w_gate", "ffn2_w_up", "ffn2_w_down",
       "ple_w_gate", "ple_w_proj")


SMALL = ("ffn1_norm", "mix_norm", "gm_ln_g", "gm_ln_b", "gm_w_s", "gm_b_s", "gm_out_norm", "conv_b", "dt_bias", "a_log",
         "d_skip", "ssm_norm", "ffn2_norm", "ple_norm", "ple_b_gate", "final_norm")
SMALL_C = 1024


def _pack_small(vals):
    parts = []
    for v in vals:
        f = v.astype(f32).reshape(-1)
        parts.append(jnp.pad(f, (0, -f.shape[0] % SMALL_C)))
    flat = jnp.concatenate(parts)
    rows = flat.shape[0] // SMALL_C
    return jnp.pad(flat, (0, (-rows % 8) * SMALL_C)).reshape(-1, SMALL_C)


def _unpack_small(pack, shapes):
    flat = pack.reshape(-1)
    out, off = [], 0
    for s in shapes:
        n = 1
        for d in s:
            n *= d
        out.append(flat[off:off + n].reshape(s))
        off += n + (-n % SMALL_C)
    return out


def _pad_lanes(v):
    return jnp.pad(v, ((0, 0), (0, LANES - v.shape[1])))


def _pad_rows(a):
    pad = [(0, 0)] * a.ndim
    pad[-2] = (0, -a.shape[-2] % ROW_PAD)
    return jnp.pad(a, pad) if pad[-2][1] else a


FETCH = (("ffn1_w_gate", "ffn1_w_up", "ffn1_w_down"), ("w_in", "conv_w", "w_out"),
         ("ffn2_w_gate", "ffn2_w_up", "ffn2_w_down", "ple_w_gate", "ple_w_proj"))
TRANSPOSED = ("ffn1_w_gate", "ffn1_w_up", "ffn2_w_gate", "ffn2_w_up", "w_in")
ROW_PAD = 32
DONE = (("ffn2_w_gate", "ffn2_w_up", "ffn2_w_down", "w_out", "ple_w_gate", "ple_w_proj"), ("w_in",),
        ("ffn1_w_gate", "ffn1_w_up", "ffn1_w_down"))


def _local_step(x, p, tgt, fetch, S, on_grads):
    G = GM_WIDTH
    K = N_CHIPS
    b_st = S["gm_b_s"][0].T
    w_s = S["gm_w_s"][0]
    dtb, alog, dsk = _pad_lanes(S["dt_bias"]), _pad_lanes(S["a_log"]), _pad_lanes(S["d_skip"])
    gfin = S["final_norm"].reshape(1, -1)

    def rows(a):
        return a.reshape(-1, D_MODEL)

    def shards(a):
        return a.reshape(K, -1, D_MODEL)

    wg1, wu1, wd1 = [rows(a) for a in fetch(0, None)]
    h1, n1, a1, b1 = _ffn_fwd("ffn1_fwd", x, S["ffn1_norm"], wg1, wu1, wd1)
    w_in4, cw4, wo4 = fetch(1, h1)
    w_in = w_in4.reshape(IN_PROJ, D_MODEL)
    w_uv = w_in[:2 * G]
    w_zxd = jnp.pad(w_in[2 * G:], ((0, ZXD - (IN_PROJ - 2 * G)), (0, 0)))
    conv_w = jnp.transpose(cw4, (1, 0, 2)).reshape(SSM_CONV, CONV_DIM)
    wo = wo4.reshape(-1, D_MODEL)
    n2, uv, z, xbc, dtr, ya = _mix_fwd(h1, S["mix_norm"], w_uv, w_zxd, S["gm_ln_g"], S["gm_ln_b"], w_s, b_st, S["gm_out_norm"])
    yb, sprev = _ssd_fwd(xbc, z, dtr, conv_w, S["conv_b"], dtb, alog, dsk, S["ssm_norm"])
    wg2, wu2, wd2, wpg4, wpp4 = fetch(2, yb)
    wg2, wu2, wd2 = rows(wg2), rows(wu2), rows(wd2)
    h2, h3, n3, a2, b2 = _ffn_fwd("ffn2_fwd", h1, S["ffn2_norm"], wg2, wu2, wd2, pre=(ya, yb, wo))
    dh3, loss, dgp, dwpg, dbpg, dwpp, dgf = _tail(h3, p, tgt, S["ple_norm"], wpg4.reshape(-1, D_MODEL), S["ple_b_gate"], wpp4, gfin)
    dh2, da2, db2, hm2, dg_ffn2, dya, dyb = _ffn_bwd("ffn2_bwd", dh3, h2, S["ffn2_norm"], a2, b2, wg2, wu2, wd2, wo=wo, ga=G)
    dw_out = jnp.concatenate([_matmul_tn("dw_out_a", ya, dh2), _matmul_tn("dw_out_b", yb, dh2)], axis=0).reshape(wo4.shape)
    zero = on_grads(0, [shards(_matmul_tn("dw_ffn2_gate", da2, n3)), shards(_matmul_tn("dw_ffn2_up", db2, n3)),
                        shards(_matmul_tn("dw_ffn2_down", hm2, dh3, scale=0.5)), dw_out,
                        dwpg.astype(bf16).reshape(wpg4.shape), dwpp.astype(bf16)])
    duv, dlng, dlnb, dws, dbst, dgout = _gm_bwd(uv, dya, S["gm_ln_g"], S["gm_ln_b"], w_s, b_st, S["gm_out_norm"] + zero)
    dzxd, dcw, dcb, ddtb, dalog, ddsk, dgssm = _ssd_bwd(xbc, z, dtr, sprev, dyb, conv_w, S["conv_b"], dtb, alog, dsk,
                                                        S["ssm_norm"] + zero)
    dh1, dg_mix = _mix_bwd(dh2, h1, S["mix_norm"], duv, dzxd, w_uv, w_zxd)
    dw_in = jnp.concatenate([_matmul_tn("dw_in_uv", duv, n2), _matmul_tn("dw_in_zxd", dzxd, n2)[:IN_PROJ - 2 * G]], axis=0)
    zero = on_grads(1, [dw_in.reshape(w_in4.shape)])
    dx, da1, db1, hm1, dg_ffn1 = _ffn_bwd("ffn1_bwd", dh1, x, S["ffn1_norm"] + zero, a1, b1, wg1, wu1, wd1)
    zero = on_grads(2, [shards(_matmul_tn("dw_ffn1_gate", da1, n1)), shards(_matmul_tn("dw_ffn1_up", db1, n1)),
                        shards(_matmul_tn("dw_ffn1_down", hm1, dh1, scale=0.5))])
    loss = loss + zero
    nh = SSM_HEADS
    gS = {"ffn1_norm": dg_ffn1, "mix_norm": dg_mix, "gm_ln_g": dlng, "gm_ln_b": dlnb, "gm_w_s": dws[None], "gm_b_s": dbst.T[None],
          "gm_out_norm": dgout, "conv_b": dcb, "dt_bias": ddtb[:, :nh], "a_log": dalog[:, :nh], "d_skip": ddsk[:, :nh],
          "ssm_norm": dgssm, "ffn2_norm": dg_ffn2, "ple_norm": dgp, "ple_b_gate": dbpg, "final_norm": dgf.reshape(-1)}
    return loss, dx, dcw, gS


_WEIGHTS = ("ffn1_norm", "ffn1_w_gate", "ffn1_w_up", "ffn1_w_down", "mix_norm", "w_in", "gm_ln_g", "gm_ln_b", "gm_w_s", "gm_b_s",
            "gm_out_norm", "conv_w", "conv_b", "dt_bias", "a_log", "d_skip", "ssm_norm", "w_out", "ffn2_norm", "ffn2_w_gate",
            "ffn2_w_up", "ffn2_w_down", "ple_norm", "ple_w_gate", "ple_b_gate", "ple_w_proj", "final_norm")
_BIG_NAMES = BIG


def kernel(x, p, ffn1_norm, ffn1_w_gate, ffn1_w_up, ffn1_w_down, mix_norm, w_in, gm_ln_g, gm_ln_b, gm_w_s, gm_b_s, gm_out_norm, conv_w, conv_b, dt_bias, a_log, d_skip, ssm_norm, w_out, ffn2_norm, ffn2_w_gate, ffn2_w_up, ffn2_w_down, ple_norm, ple_w_gate, ple_b_gate, ple_w_proj, final_norm, loss_target, m_ffn1_norm, m_ffn1_w_gate, m_ffn1_w_up, m_ffn1_w_down, m_mix_norm, m_w_in, m_gm_ln_g, m_gm_ln_b, m_gm_w_s, m_gm_b_s, m_gm_out_norm, m_conv_w, m_conv_b, m_dt_bias, m_a_log, m_d_skip, m_ssm_norm, m_w_out, m_ffn2_norm, m_ffn2_w_gate, m_ffn2_w_up, m_ffn2_w_down, m_ple_norm, m_ple_w_gate, m_ple_b_gate, m_ple_w_proj, m_final_norm, v_ffn1_norm, v_ffn1_w_gate, v_ffn1_w_up, v_ffn1_w_down, v_mix_norm, v_w_in, v_gm_ln_g, v_gm_ln_b, v_gm_w_s, v_gm_b_s, v_gm_out_norm, v_conv_w, v_conv_b, v_dt_bias, v_a_log, v_d_skip, v_ssm_norm, v_w_out, v_ffn2_norm, v_ffn2_w_gate, v_ffn2_w_up, v_ffn2_w_down, v_ple_norm, v_ple_w_gate, v_ple_b_gate, v_ple_w_proj, v_final_norm):
    given = dict(locals())
    w = {n: given[n] for n in _WEIGHTS}
    m = {n: given["m_" + n] for n in _WEIGHTS}
    v = {n: given["v_" + n] for n in _WEIGHTS}

    c_idx = lax.axis_index("c").astype(jnp.int32).reshape(1)
    chip = 2 * lax.axis_index("x") + lax.axis_index("y")
    chip_idx = chip.astype(jnp.int32).reshape(1)

    shard = {n: (jnp.swapaxes(w[n][0], 0, 1) if n in TRANSPOSED else w[n][0]).astype(bf16) for n in BIG}
    shard["conv_w"] = w["conv_w"][0]
    first = _gather_weights([shard[n] for n in FETCH[0]], [True] * len(FETCH[0]))
    fetching, after = [], first[-1]
    for k in (1, 2):
        srcs = [shard[n] for n in FETCH[k]]
        lands = [jax.ShapeDtypeStruct((N_CHIPS,) + s.shape, s.dtype) for s in srcs]
        fetching.append(_copies_start("gather%d_start" % k, srcs, lands, 4 * len(srcs), _gather_copies, after))
        after = fetching[-1][4]

    def fetch(k, after_):
        return first if k == 0 else _copies_wait("gather%d_wait" % k, fetching[k - 1], _gather_copies, [after_])[1]

    exchanging = []

    def on_grads(k, grads):
        grads = [_pad_rows(g_) for g_ in grads]
        others = _swap_halves("swap%d" % k, grads)
        parts = [_add_halves("add_" + n, g_, o_, c_idx) for n, g_, o_ in zip(DONE[k], grads, others)]
        lands = [jax.ShapeDtypeStruct((3,) + p_.shape[1:], p_.dtype) for p_ in parts]
        exchanging.append(_copies_start("exchange%d_start" % k, parts, lands, 3 * len(parts), _partial_copies, c_idx))
        return exchanging[-1][4][0, 0]

    S = {n: w[n] for n in SMALL}
    S["ffn1_norm"] = S["ffn1_norm"] + after[0, 0]
    loss, dx, dcw, gS = _local_step(x[0], p[0, 0], loss_target[0], fetch, S, on_grads)

    small = _pack_small([gS[n] for n in SMALL] + [dcw, loss[:, :1]])
    small_lands = [jax.ShapeDtypeStruct((N_DEV - 1,) + small.shape, small.dtype)]
    small_st = _copies_start("small_start", [small], small_lands, N_DEV - 1, _small_copies, c_idx)

    g, delta, new_m, new_v = {}, {}, {}, {}
    after = [small_st[4]]
    for k in range(len(DONE)):
        parts, recv = _copies_wait("exchange%d_wait" % k, exchanging[k], _partial_copies, after)
        mine = [_sum_partials("sum_" + n, p_, r_, chip_idx) for n, p_, r_ in zip(DONE[k], parts, recv)]
        theirs = _share_halves("share%d" % k, mine)
        after = []
        for n, gm_, gt_ in zip(DONE[k], mine, theirs):
            flip = (lambda a: jnp.swapaxes(a, 0, 1)) if n in TRANSPOSED else (lambda a: a)
            rows = flip(w[n][0]).shape[0]
            w_, m_, v_ = [_pad_rows(flip(a[n][0])) for a in (w, m, v)]
            outs = _adamw_big("adamw_" + n, w_, gm_, gt_, m_, v_, c_idx)
            g[n], delta[n], new_m[n], new_v[n] = [flip(o[:rows])[None] for o in outs]
            after.append(outs[3])
    (own,), (slots,) = _copies_wait("small_wait", small_st, _small_copies, after)
    dev_idx = (2 * chip + lax.axis_index("c")).astype(jnp.int32).reshape(1)
    small_shapes = [w[n].shape for n in SMALL] + [dcw.shape, (1, 1)]
    small_sum = _unpack_small(_sum_small(own, slots, dev_idx), small_shapes)
    g.update({n: small_sum[i] for i, n in enumerate(SMALL)})
    cshard = w["conv_w"].shape[2]
    g["conv_w"] = lax.dynamic_slice_in_dim(small_sum[len(SMALL)], chip * cshard, cshard, axis=1)[None]
    loss_total = small_sum[len(SMALL) + 1].reshape(())
    sm_names = SMALL + ("conv_w",)
    sm_shapes = [w[n].shape for n in sm_names]
    d_s, m_s, v_s = _adamw("adamw_small", _pack_small([w[n] for n in sm_names]), _pack_small([g[n] for n in sm_names]),
                           _pack_small([m[n] for n in sm_names]), _pack_small([v[n] for n in sm_names]))
    for dst, src in ((delta, d_s), (new_m, m_s), (new_v, v_s)):
        for n, val in zip(sm_names, _unpack_small(src, sm_shapes)):
            dst[n] = val

    return (loss_total, dx[None], *[g[n] for n in _WEIGHTS], *[delta[n] for n in _WEIGHTS],
            *[new_m[n] for n in _WEIGHTS], *[new_v[n] for n in _WEIGHTS])
```

```python
import functools

import jax
import jax.numpy as jnp
from jax import lax
from jax.experimental import pallas as pl
from jax.experimental.pallas import tpu as pltpu

f32 = jnp.float32
bf16 = jnp.bfloat16
MESH = pl.DeviceIdType.MESH
HIGHEST = lax.Precision.HIGHEST

EPS = 1e-6
N_CHIPS = 4
N_DEV = 8
D_MODEL = 1024
D_FF = 2816
D_PLE = 256
GM_WIDTH = 1024
GM_HEADS = 8
CHUNK = 128
SSM_WIDTH = 1024
SSM_HEADS = 16
SSM_HEAD_DIM = 64
SSM_GROUPS = 2
SSM_STATE = 128
SSM_CONV = 4
CONV_DIM = SSM_WIDTH + 2 * SSM_GROUPS * SSM_STATE
IN_PROJ = 2 * GM_WIDTH + SSM_WIDTH + CONV_DIM + SSM_HEADS
LANES = 128
ZXD = SSM_WIDTH + CONV_DIM + LANES

ADAM_LR = 0.001
ADAM_B1 = 0.9
ADAM_B2 = 0.999
ADAM_EPS = 1e-08
ADAM_WD = 0.01
ADAM_STEP = 10

VMEM_LIMIT = 56 * 1024 * 1024


def _dot(a, b):
    return jnp.dot(a, b, preferred_element_type=f32)


def _dot_nt(a, b):
    return lax.dot_general(a, b, (((1,), (1,)), ((), ())), preferred_element_type=f32)


def _dot_tn(a, b):
    return lax.dot_general(a, b, (((0,), (0,)), ((), ())), preferred_element_type=f32)


def _rms(x, g):
    return x * lax.rsqrt(jnp.mean(x * x, axis=-1, keepdims=True) + EPS) * g


def _gelu(x):
    return 0.5 * x * (1.0 + lax.erf(x * 0.7071067811865476))


def _layernorm(x, g, b):
    mu = jnp.mean(x, axis=-1, keepdims=True)
    xc = x - mu
    return xc * lax.rsqrt(jnp.mean(xc * xc, axis=-1, keepdims=True) + EPS) * g + b


def _sigmoid(x):
    return 1.0 / (1.0 + jnp.exp(-x))


def _softplus(x):
    return jnp.maximum(x, 0.0) + jnp.log(1.0 + jnp.exp(-jnp.abs(x)))


def _full(shape):
    nd = len(shape)
    return pl.BlockSpec(shape, lambda *_: (0,) * nd, pipeline_mode=pl.Buffered(1))


def _acc(shape):
    nd = len(shape)
    return pl.BlockSpec(shape, lambda *_: (0,) * nd)


def _rows(tm, ncols):
    return pl.BlockSpec((tm, ncols), lambda i: (i, 0))


def _params(sem):
    return pltpu.CompilerParams(dimension_semantics=sem, vmem_limit_bytes=VMEM_LIMIT)


def _row_tile(rows, target, mult=8):
    best = rows
    for t in range(mult, min(rows, target) + 1, mult):
        if rows % t == 0:
            best = t
    return best if best <= target else rows


def _ffn_fwd(name, h, g, wg, wu, wd, pre=None, tm=256):
    T, D = h.shape
    F = wg.shape[0]
    tm = min(tm, T)

    def body(*refs):
        if pre is None:
            h_ref, g_ref, wg_ref, wu_ref, wd_ref, ho_ref, n_ref, a_ref, b_ref = refs
            hin = h_ref[...]
        else:
            (h_ref, ya_ref, yb_ref, wo_ref, g_ref, wg_ref, wu_ref, wd_ref,
             hi_ref, ho_ref, n_ref, a_ref, b_ref) = refs
            ga = ya_ref.shape[1]
            hin = h_ref[...] + _dot(ya_ref[...], wo_ref[:ga, :]) + _dot(yb_ref[...], wo_ref[ga:, :])
            hi_ref[...] = hin
        n = _rms(hin, g_ref[...]).astype(bf16)
        n_ref[...] = n
        a = _dot_nt(n, wg_ref[...]).astype(bf16)
        b = _dot_nt(n, wu_ref[...]).astype(bf16)
        a_ref[...] = a
        b_ref[...] = b
        af = a.astype(f32)
        hm = (af * _sigmoid(af) * b.astype(f32)).astype(bf16)
        ho_ref[...] = hin + 0.5 * _dot(hm, wd_ref[...])

    ins = [h] + (list(pre) if pre is not None else []) + [g, wg, wu, wd]
    in_specs = [_rows(tm, D)]
    if pre is not None:
        in_specs += [_rows(tm, pre[0].shape[1]), _rows(tm, pre[1].shape[1]), _full(pre[2].shape)]
    in_specs += [_full(g.shape), _full(wg.shape), _full(wu.shape), _full(wd.shape)]
    outs = [jax.ShapeDtypeStruct((T, D), f32), jax.ShapeDtypeStruct((T, D), bf16),
            jax.ShapeDtypeStruct((T, F), bf16), jax.ShapeDtypeStruct((T, F), bf16)]
    out_specs = [_rows(tm, D), _rows(tm, D), _rows(tm, F), _rows(tm, F)]
    if pre is not None:
        outs = [jax.ShapeDtypeStruct((T, D), f32)] + outs
        out_specs = [_rows(tm, D)] + out_specs
    return pl.pallas_call(body, name=name, grid=(T // tm,), in_specs=in_specs, out_specs=out_specs,
                          out_shape=outs, compiler_params=_params(("parallel",)))(*ins)


def _ffn_bwd(name, dh, hin, g, a, b, wg, wu, wd, wo=None, ga=0, tm=256):
    T, D = dh.shape
    F = wg.shape[0]
    tm = min(tm, T)

    def body(*refs):
        if wo is None:
            (dh_ref, hin_ref, g_ref, a_ref, b_ref, wg_ref, wu_ref, wd_ref,
             dhi_ref, da_ref, db_ref, hm_ref, dg_ref) = refs
        else:
            (dh_ref, hin_ref, g_ref, a_ref, b_ref, wg_ref, wu_ref, wd_ref, wo_ref,
             dhi_ref, da_ref, db_ref, hm_ref, dg_ref, dya_ref, dyb_ref) = refs

        @pl.when(pl.program_id(0) == 0)
        def _():
            dg_ref[...] = jnp.zeros_like(dg_ref)

        dh_ = dh_ref[...]
        dhb = (0.5 * dh_).astype(bf16)
        dhm = _dot_nt(dhb, wd_ref[...])
        af = a_ref[...].astype(f32)
        bf = b_ref[...].astype(f32)
        sg = _sigmoid(af)
        sl_ = af * sg
        da = (dhm * bf * (sg * (1.0 + af * (1.0 - sg)))).astype(bf16)
        db = (dhm * sl_).astype(bf16)
        da_ref[...] = da
        db_ref[...] = db
        hm_ref[...] = (sl_ * bf).astype(bf16)
        dn = _dot(da, wg_ref[...]) + _dot(db, wu_ref[...])
        _, vjp = jax.vjp(_rms, hin_ref[...], g_ref[...])
        dx, dg = vjp(dn)
        dhi = dh_ + dx
        dhi_ref[...] = dhi
        dg_ref[...] += dg
        if wo is not None:
            dhib = dhi.astype(bf16)
            dya_ref[...] = _dot_nt(dhib, wo_ref[:ga, :]).astype(bf16)
            dyb_ref[...] = _dot_nt(dhib, wo_ref[ga:, :]).astype(bf16)

    ins = [dh, hin, g, a, b, wg, wu, wd]
    in_specs = [_rows(tm, D), _rows(tm, D), _full(g.shape), _rows(tm, F), _rows(tm, F),
                _full(wg.shape), _full(wu.shape), _full(wd.shape)]
    act = jax.ShapeDtypeStruct((T, F), bf16)
    outs = [jax.ShapeDtypeStruct((T, D), f32), act, act, act, jax.ShapeDtypeStruct(g.shape, f32)]
    out_specs = [_rows(tm, D), _rows(tm, F), _rows(tm, F), _rows(tm, F), _acc(g.shape)]
    if wo is not None:
        gb = wo.shape[0] - ga
        ins += [wo]
        in_specs += [_full(wo.shape)]
        outs += [jax.ShapeDtypeStruct((T, ga), bf16), jax.ShapeDtypeStruct((T, gb), bf16)]
        out_specs += [_rows(tm, ga), _rows(tm, gb)]
    return pl.pallas_call(body, name=name, grid=(T // tm,), in_specs=in_specs, out_specs=out_specs,
                          out_shape=outs, compiler_params=_params(("arbitrary",)))(*ins)


def _matmul_tn(name, a, b, scale=1.0, tk=2048):
    T, M = a.shape
    N = b.shape[1]
    tk = min(tk, T)
    nk = T // tk
    tn = LANES * max(d for d in range(1, N // LANES + 1) if (N // LANES) % d == 0 and (d == 1 or M * d * LANES * 4 <= 6 * 1024 * 1024))

    def body(a_ref, b_ref, o_ref, acc):
        k = pl.program_id(1)

        @pl.when(k == 0)
        def _():
            acc[...] = jnp.zeros_like(acc)

        bb = b_ref[...]
        if scale != 1.0:
            bb = bb * scale
        acc[...] += _dot_tn(a_ref[...].astype(bf16), bb.astype(bf16))

        @pl.when(k == nk - 1)
        def _():
            o_ref[...] = acc[...].astype(bf16)

    return pl.pallas_call(
        body, name=name, grid=(N // tn, nk),
        in_specs=[pl.BlockSpec((tk, M), lambda j, k: (k, 0)), pl.BlockSpec((tk, tn), lambda j, k: (k, j))],
        out_specs=pl.BlockSpec((M, tn), lambda j, k: (0, j)),
        out_shape=jax.ShapeDtypeStruct((M, N), bf16), scratch_shapes=[pltpu.VMEM((M, tn), f32)],
        compiler_params=_params(("parallel", "arbitrary")))(a, b)


def _gm_pre(u, v, ln_g, ln_b):
    return _gelu(u), _layernorm(_gelu(v), ln_g, ln_b)


def _tril_mask():
    r = lax.broadcasted_iota(jnp.int32, (CHUNK, CHUNK), 0)
    c = lax.broadcasted_iota(jnp.int32, (CHUNK, CHUNK), 1)
    return c <= r


def _gm_mix(vnb, ws_ref, bst, mixed_sc, tm):
    mask = _tril_mask()
    for h in range(GM_HEADS):
        wt = jnp.where(mask, ws_ref[h], 0.0).astype(bf16)
        bias = bst[:, h:h + 1]
        for q in range(tm // CHUNK):
            rs = slice(q * CHUNK, (q + 1) * CHUNK)
            cs = slice(h * CHUNK, (h + 1) * CHUNK)
            mixed_sc[rs, cs] = _dot(wt, vnb[rs, cs]) + bias


def _mix_fwd(h1, gmix, w_uv, w_zxd, ln_g, ln_b, w_s, b_st, gout, tm=512):
    T, D = h1.shape
    tm = min(tm, T)
    G = GM_WIDTH

    def body(h_ref, g_ref, wuv_ref, wzxd_ref, lng_ref, lnb_ref, ws_ref, bst_ref, gout_ref,
             n_ref, uv_ref, z_ref, xbc_ref, dt_ref, ya_ref, mixed_sc):
        n = _rms(h_ref[...], g_ref[...]).astype(bf16)
        n_ref[...] = n
        u = _dot_nt(n, wuv_ref[:G, :]).astype(bf16)
        v = _dot_nt(n, wuv_ref[G:, :]).astype(bf16)
        uv_ref[:, :G] = u
        uv_ref[:, G:] = v
        z_ref[...] = _dot_nt(n, wzxd_ref[:SSM_WIDTH, :]).astype(bf16)
        xbc_ref[...] = _dot_nt(n, wzxd_ref[SSM_WIDTH:SSM_WIDTH + CONV_DIM, :]).astype(bf16)
        dt_ref[...] = _dot_nt(n, wzxd_ref[SSM_WIDTH + CONV_DIM:, :])
        ug, vn = _gm_pre(u.astype(f32), v.astype(f32), lng_ref[...], lnb_ref[...])
        _gm_mix(vn.astype(bf16), ws_ref, bst_ref[...], mixed_sc, tm)
        ya_ref[...] = _rms(ug * mixed_sc[...], gout_ref[...]).astype(bf16)

    ins = [h1, gmix, w_uv, w_zxd, ln_g, ln_b, w_s, b_st, gout]
    in_specs = [_rows(tm, D)] + [_full(x.shape) for x in ins[1:]]
    outs = [jax.ShapeDtypeStruct((T, D), bf16), jax.ShapeDtypeStruct((T, 2 * G), bf16),
            jax.ShapeDtypeStruct((T, SSM_WIDTH), bf16), jax.ShapeDtypeStruct((T, CONV_DIM), bf16),
            jax.ShapeDtypeStruct((T, LANES), f32), jax.ShapeDtypeStruct((T, G), bf16)]
    out_specs = [_rows(tm, D), _rows(tm, 2 * G), _rows(tm, SSM_WIDTH), _rows(tm, CONV_DIM), _rows(tm, LANES), _rows(tm, G)]
    return pl.pallas_call(body, name="mix_fwd", grid=(T // tm,), in_specs=in_specs, out_specs=out_specs,
                          out_shape=outs, scratch_shapes=[pltpu.VMEM((tm, G), f32)],
                          compiler_params=_params(("parallel",)))(*ins)


def _mix_bwd(dh, h1, gmix, uv, dya, dzxd, w_uv, w_zxd, ln_g, ln_b, w_s, b_st, gout, tm=256):
    T, D = dh.shape
    tm = min(tm, T)
    G = GM_WIDTH

    def body(dh_ref, h_ref, g_ref, uv_ref, dya_ref, dzxd_ref, wuv_ref, wzxd_ref, lng_ref, lnb_ref, ws_ref, bst_ref, gout_ref,
             dhi_ref, duv_ref, dg_ref, dlng_ref, dlnb_ref, dws_ref, dbst_ref, dgout_ref, mixed_sc, dvn_sc):
        @pl.when(pl.program_id(0) == 0)
        def _():
            for r in (dg_ref, dlng_ref, dlnb_ref, dws_ref, dbst_ref, dgout_ref):
                r[...] = jnp.zeros_like(r)

        dn_z = _dot(dzxd_ref[...], wzxd_ref[...])
        u = uv_ref[:, :G].astype(f32)
        v = uv_ref[:, G:].astype(f32)
        (ug, vn), pre_vjp = jax.vjp(_gm_pre, u, v, lng_ref[...], lnb_ref[...])
        vnb = vn.astype(bf16)
        _gm_mix(vnb, ws_ref, bst_ref[...], mixed_sc, tm)
        mixed = mixed_sc[...]
        _, out_vjp = jax.vjp(_rms, ug * mixed, gout_ref[...])
        dpre, dgout = out_vjp(dya_ref[...].astype(f32))
        dgout_ref[...] += dgout
        dug = dpre * mixed
        dmixed = dpre * ug
        mask = _tril_mask()
        lane = lax.broadcasted_iota(jnp.int32, (1, GM_HEADS), 1)
        dbst = jnp.zeros((CHUNK, GM_HEADS), f32)
        for h in range(GM_HEADS):
            wt = jnp.where(mask, ws_ref[h], 0.0).astype(bf16)
            cs = slice(h * CHUNK, (h + 1) * CHUNK)
            dw = jnp.zeros((CHUNK, CHUNK), f32)
            for q in range(tm // CHUNK):
                rs = slice(q * CHUNK, (q + 1) * CHUNK)
                dm = dmixed[rs, cs]
                dmb = dm.astype(bf16)
                dw = dw + _dot_nt(dmb, vnb[rs, cs])
                dbst = dbst + jnp.sum(dm, axis=1, keepdims=True) * (lane == h).astype(f32)
                dvn_sc[rs, cs] = _dot_tn(wt, dmb)
            dws_ref[h] += jnp.where(mask, dw, 0.0)
        dbst_ref[...] += dbst
        du, dv, dlng, dlnb = pre_vjp((dug, dvn_sc[...]))
        duv = jnp.concatenate([du.astype(bf16), dv.astype(bf16)], axis=1)
        duv_ref[...] = duv
        dlng_ref[...] += dlng
        dlnb_ref[...] += dlnb
        dn = dn_z + _dot(duv, wuv_ref[...])
        _, vjp = jax.vjp(_rms, h_ref[...], g_ref[...])
        dx, dg = vjp(dn)
        dhi_ref[...] = dh_ref[...] + dx
        dg_ref[...] += dg

    ins = [dh, h1, gmix, uv, dya, dzxd, w_uv, w_zxd, ln_g, ln_b, w_s, b_st, gout]
    in_specs = ([_rows(tm, D), _rows(tm, D), _full(gmix.shape), _rows(tm, 2 * G), _rows(tm, G), _rows(tm, dzxd.shape[1])]
                + [_full(x.shape) for x in ins[6:]])
    accs = (gmix, ln_g, ln_b, w_s, b_st, gout)
    outs = ([jax.ShapeDtypeStruct((T, D), f32), jax.ShapeDtypeStruct((T, 2 * G), bf16)]
            + [jax.ShapeDtypeStruct(x.shape, f32) for x in accs])
    out_specs = [_rows(tm, D), _rows(tm, 2 * G)] + [_acc(x.shape) for x in accs]
    return pl.pallas_call(body, name="mix_bwd", grid=(T // tm,), in_specs=in_specs, out_specs=out_specs,
                          out_shape=outs, scratch_shapes=[pltpu.VMEM((tm, G), f32), pltpu.VMEM((tm, G), f32)],
                          compiler_params=_params(("arbitrary",)))(*ins)


HALO = 16
PAIRS = SSM_HEADS // 2
PAIR_W = 2 * SSM_HEAD_DIM


def _split(x, n):
    parts = []
    for _ in range(n):
        p = x.astype(bf16)
        parts.append(p)
        x = x - p.astype(f32)
    return parts


def _dot_sel(x, sel_n, n):
    return _dot(jnp.concatenate(_split(x, n), axis=1), sel_n)


def _sel_dot(sel, x, n):
    return _dot(jnp.concatenate([sel] * n, axis=1), jnp.concatenate(_split(x, n), axis=0))


EXPAND_SPLIT = 3
REDUCE_SPLIT = 2


def _head_mats():
    ex = (jnp.arange(SSM_WIDTH)[None, :] // SSM_HEAD_DIM == jnp.arange(LANES)[:, None]).astype(bf16)
    return jnp.tile(ex, (EXPAND_SPLIT, 1)), jnp.tile(ex.T, (REDUCE_SPLIT, 1))


def _shift_mat(rows, cols, off):
    r = lax.broadcasted_iota(jnp.int32, (rows, cols), 0)
    c = lax.broadcasted_iota(jnp.int32, (rows, cols), 1)
    return (c == r + off).astype(bf16)


def _ssd_front(c, xbc_ref, halo_ref, dtr_ref, cw_ref, cb_ref, dtb_ref, alog_ref):
    halo = halo_ref[...]
    ext = jnp.concatenate([jnp.where(c > 0, halo, jnp.zeros_like(halo)), xbc_ref[...]], axis=0)
    taps = [_dot(_shift_mat(CHUNK, HALO + CHUNK, HALO - SSM_CONV + 1 + j), ext) for j in range(SSM_CONV - 1)]
    taps.append(xbc_ref[...].astype(f32))
    xc = cb_ref[...] + cw_ref[0:1, :] * taps[0]
    for j in range(1, SSM_CONV):
        xc = xc + cw_ref[j:j + 1, :] * taps[j]
    sg = _sigmoid(xc)
    xa = xc * sg
    dt = _softplus(dtr_ref[...] + dtb_ref[...])
    a = -jnp.exp(alog_ref[...])
    acs = jnp.dot(_tril_mask().astype(f32), dt * a, preferred_element_type=f32, precision=HIGHEST)
    return taps, xc, sg, xa, dt, a, acs


def _ssd_wide(xa, dt, acs, dsk, ex):
    dt_x = _dot_sel(dt, ex, EXPAND_SPLIT)
    acs_x = _dot_sel(acs, ex, EXPAND_SPLIT)
    dsk_x = _dot_sel(jnp.broadcast_to(dsk, (8, LANES)), ex, EXPAND_SPLIT)[0:1]
    e_x = jnp.exp(acs_x)
    r_x = jnp.exp(acs_x[CHUNK - 1:CHUNK, :] - acs_x)
    xs = xa[:, :SSM_WIDTH]
    xd = xs * dt_x
    return dt_x, dsk_x, e_x, r_x, xs, xd, xd * r_x


def _pair_stack(v, lo):
    return jnp.concatenate([jnp.where(lo, v, 0.0), jnp.where(lo, 0.0, v)], axis=0)


def _ssd_pair(j, acs, acs_t, cb):
    out = []
    tril = _tril_mask()
    for h in (2 * j, 2 * j + 1):
        dk = jnp.exp(jnp.where(tril, acs[:, h:h + 1] - acs_t[h:h + 1, :], -jnp.inf))
        out.append((dk, cb * dk))
    return out


def _pair_col(row_lo, tot, j):
    return jnp.exp(jnp.where(row_lo, tot[:, 2 * j:2 * j + 1], tot[:, 2 * j + 1:2 * j + 2]))


def _gated_norm(y, z, g):
    yg = y * (z * _sigmoid(z))
    half = SSM_WIDTH // SSM_GROUPS
    parts = []
    for k in range(SSM_GROUPS):
        s = yg[:, k * half:(k + 1) * half]
        parts.append(s * lax.rsqrt(jnp.mean(s * s, axis=-1, keepdims=True) + EPS))
    return jnp.concatenate(parts, axis=1) * g


def _group_mats(xa):
    out = []
    for g in range(SSM_GROUPS):
        bm = xa[:, SSM_WIDTH + g * SSM_STATE:SSM_WIDTH + (g + 1) * SSM_STATE].astype(bf16)
        cm = xa[:, SSM_WIDTH + (SSM_GROUPS + g) * SSM_STATE:SSM_WIDTH + (SSM_GROUPS + g + 1) * SSM_STATE].astype(bf16)
        out.append((cm, bm, _dot_nt(cm, bm)))
    return out


def _ssd_in_specs(nc, rev):
    def ci(i):
        return nc - 1 - i if rev else i
    hp = CHUNK // HALO
    return [pl.BlockSpec((CHUNK, CONV_DIM), lambda i: (ci(i), 0)),
            pl.BlockSpec((HALO, CONV_DIM), lambda i: (jnp.maximum(ci(i) * hp - 1, 0), 0)),
            pl.BlockSpec((CHUNK, SSM_WIDTH), lambda i: (ci(i), 0)),
            pl.BlockSpec((CHUNK, LANES), lambda i: (ci(i), 0))]


def _ssd_fwd(xbc, z, dtr, conv_w, conv_b, dt_bias, a_log, d_skip, ssm_norm):
    T = xbc.shape[0]
    nc = T // CHUNK
    N = SSM_STATE

    def body(xbc_ref, halo_ref, z_ref, dtr_ref, cw_ref, cb_ref, dtb_ref, alog_ref, dsk_ref, g_ref, ex_ref,
             yb_ref, sprev_ref, s_sc):
        c = pl.program_id(0)

        @pl.when(c == 0)
        def _():
            s_sc[...] = jnp.zeros_like(s_sc)

        _, _, _, xa, dt, _, acs = _ssd_front(c, xbc_ref, halo_ref, dtr_ref, cw_ref, cb_ref, dtb_ref, alog_ref)
        _, dsk_x, e_x, _, xs, xd, gm = _ssd_wide(xa, dt, acs, dsk_ref[...], ex_ref[...])
        acs_t = acs.T
        tot = acs[CHUNK - 1:CHUNK, :]
        groups = _group_mats(xa)
        lo = lax.broadcasted_iota(jnp.int32, (CHUNK, PAIR_W), 1) < SSM_HEAD_DIM
        row_lo = lax.broadcasted_iota(jnp.int32, (PAIR_W, 1), 0) < SSM_HEAD_DIM
        ys = []
        for j in range(PAIRS):
            cmb, bmb, cb = groups[j // (PAIRS // SSM_GROUPS)]
            ps = slice(j * PAIR_W, (j + 1) * PAIR_W)
            (_, m0), (_, m1) = _ssd_pair(j, acs, acs_t, cb)
            sp = s_sc[j]
            yd = _dot(jnp.concatenate([m0, m1], axis=1).astype(bf16), _pair_stack(xd[:, ps], lo).astype(bf16))
            ys.append(yd + e_x[:, ps] * _dot_nt(cmb, sp.astype(bf16)))
            sprev_ref[0, j] = sp
            s_sc[j] = _pair_col(row_lo, tot, j) * sp + _dot_tn(gm[:, ps].astype(bf16), bmb)
        y = jnp.concatenate(ys, axis=1) + xs * dsk_x
        yb_ref[...] = _gated_norm(y, z_ref[...].astype(f32), g_ref[...]).astype(bf16)

    params = [conv_w, conv_b, dt_bias, a_log, d_skip, ssm_norm, _head_mats()[0]]
    return pl.pallas_call(
        body, name="ssd_fwd", grid=(nc,),
        in_specs=_ssd_in_specs(nc, False) + [_full(x.shape) for x in params],
        out_specs=[pl.BlockSpec((CHUNK, SSM_WIDTH), lambda i: (i, 0)), pl.BlockSpec((1, PAIRS, PAIR_W, N), lambda i: (i, 0, 0, 0))],
        out_shape=[jax.ShapeDtypeStruct((T, SSM_WIDTH), bf16), jax.ShapeDtypeStruct((nc, PAIRS, PAIR_W, N), f32)],
        scratch_shapes=[pltpu.VMEM((PAIRS, PAIR_W, N), f32)],
        compiler_params=_params(("arbitrary",)))(xbc, xbc, z, dtr, *params)


def _ssd_bwd(xbc, z, dtr, sprev, dyb, conv_w, conv_b, dt_bias, a_log, d_skip, ssm_norm):
    T = xbc.shape[0]
    nc = T // CHUNK
    H, N = SSM_HEADS, SSM_STATE
    PG = PAIRS // SSM_GROUPS

    def body(xbc_ref, halo_ref, z_ref, dtr_ref, sprev_ref, dyb_ref, cw_ref, cb_ref, dtb_ref, alog_ref, dsk_ref, g_ref,
             ex_ref, rd_ref, dzxd_ref, dcw_ref, dcb_ref, ddtb_ref, dalog_ref, ddsk_ref, dg_ref, ds_sc, next_sc):
        i = pl.program_id(0)
        c = nc - 1 - i

        @pl.when(i == 0)
        def _():
            ds_sc[...] = jnp.zeros_like(ds_sc)
            next_sc[...] = jnp.zeros_like(next_sc)
            for r_ in (dcw_ref, dcb_ref, ddtb_ref, dalog_ref, ddsk_ref, dg_ref):
                r_[...] = jnp.zeros_like(r_)

        taps, xc, sg, xa, dt, a, acs = _ssd_front(c, xbc_ref, halo_ref, dtr_ref, cw_ref, cb_ref, dtb_ref, alog_ref)
        dt_x, dsk_x, e_x, r_x, xs, xd, gm = _ssd_wide(xa, dt, acs, dsk_ref[...], ex_ref[...])
        acs_t = acs.T
        tot = acs[CHUNK - 1:CHUNK, :]
        groups = _group_mats(xa)
        lo = lax.broadcasted_iota(jnp.int32, (CHUNK, PAIR_W), 1) < SSM_HEAD_DIM
        row_lo = lax.broadcasted_iota(jnp.int32, (PAIR_W, 1), 0) < SSM_HEAD_DIM
        pairs, zs, yds = [], [], []
        for j in range(PAIRS):
            cmb, _, cb = groups[j // PG]
            ps = slice(j * PAIR_W, (j + 1) * PAIR_W)
            pairs.append(_ssd_pair(j, acs, acs_t, cb))
            (_, m0), (_, m1) = pairs[j]
            zs.append(_dot_nt(cmb, sprev_ref[0, j].astype(bf16)))
            yds.append(_dot(jnp.concatenate([m0, m1], axis=1).astype(bf16), _pair_stack(xd[:, ps], lo).astype(bf16)))
        zf = jnp.concatenate(zs, axis=1)
        y = jnp.concatenate(yds, axis=1) + e_x * zf + xs * dsk_x
        _, gn_vjp = jax.vjp(_gated_norm, y, z_ref[...].astype(f32), g_ref[...])
        dy, dz, dg = gn_vjp(dyb_ref[...].astype(f32))
        dg_ref[...] += dg
        dzxd_ref[:, :SSM_WIDTH] = dz.astype(bf16)

        lane = lax.broadcasted_iota(jnp.int32, (1, LANES), 1)
        sub = lax.broadcasted_iota(jnp.int32, (LANES, 1), 0)
        dacs = jnp.zeros((CHUNK, LANES), f32)
        dacs_r = jnp.zeros((LANES, CHUNK), f32)
        dtot = jnp.zeros((1, LANES), f32)
        dcb = [jnp.zeros((CHUNK, CHUNK), f32) for _ in range(SSM_GROUPS)]
        dcm = [jnp.zeros((CHUNK, N), f32) for _ in range(SSM_GROUPS)]
        dbm = [jnp.zeros((CHUNK, N), f32) for _ in range(SSM_GROUPS)]
        dxds, dgms = [], []
        for j in range(PAIRS):
            g = j // PG
            cmb, bmb, _ = groups[g]
            ps = slice(j * PAIR_W, (j + 1) * PAIR_W)
            (dk0, m0), (dk1, m1) = pairs[j]
            oh0, oh1 = (lane == 2 * j).astype(f32), (lane == 2 * j + 1).astype(f32)
            dyp = dy[:, ps]
            dy2 = _pair_stack(dyp, lo).astype(bf16)
            dm2 = _dot_nt(dy2, xd[:, ps].astype(bf16))
            m2 = jnp.concatenate([m0, m1], axis=0)
            dxds.append(_dot_tn(m2.astype(bf16), dy2))
            w2 = dm2 * m2
            rs = jnp.sum(w2, axis=1, keepdims=True)
            dacs = dacs + rs[:CHUNK] * oh0 + rs[CHUNK:] * oh1
            dacs_r = dacs_r - ((sub == 2 * j).astype(f32) * jnp.sum(w2[:CHUNK], axis=0, keepdims=True)
                               + (sub == 2 * j + 1).astype(f32) * jnp.sum(w2[CHUNK:], axis=0, keepdims=True))
            dcb[g] = dcb[g] + dm2[:CHUNK] * dk0 + dm2[CHUNK:] * dk1
            sp = sprev_ref[0, j]
            dzb = (dyp * e_x[:, ps]).astype(bf16)
            dcm[g] = dcm[g] + _dot(dzb, sp.astype(bf16))
            dsn = ds_sc[j]
            dsnb = dsn.astype(bf16)
            et = _pair_col(row_lo, tot, j)
            rr = jnp.sum(dsn * sp, axis=1, keepdims=True) * et
            dtot = dtot + jnp.sum(rr[:SSM_HEAD_DIM]) * oh0 + jnp.sum(rr[SSM_HEAD_DIM:]) * oh1
            dgms.append(_dot_nt(bmb, dsnb))
            dbm[g] = dbm[g] + _dot(gm[:, ps].astype(bf16), dsnb)
            ds_sc[j] = _dot_tn(dzb, cmb) + et * dsn
        dgm = jnp.concatenate(dgms, axis=1)
        dxd = jnp.concatenate(dxds, axis=1) + dgm * r_x
        dr = dgm * gm
        red = _dot_sel(jnp.concatenate([dy * e_x * zf - dr, dr, dxd * xs, dy * xs], axis=0), rd_ref[...], REDUCE_SPLIT)
        rowi = lax.broadcasted_iota(jnp.int32, (CHUNK, 1), 0)
        dtot = dtot + jnp.sum(red[CHUNK:2 * CHUNK], axis=0, keepdims=True)
        dacs = dacs + red[:CHUNK] + dacs_r.T + jnp.where(rowi == CHUNK - 1, dtot, 0.0)
        r2 = lax.broadcasted_iota(jnp.int32, (CHUNK, CHUNK), 0)
        c2 = lax.broadcasted_iota(jnp.int32, (CHUNK, CHUNK), 1)
        dadt = jnp.dot((c2 >= r2).astype(f32), dacs, preferred_element_type=f32, precision=HIGHEST)
        ddt = red[2 * CHUNK:3 * CHUNK] + dadt * a
        dalog_ref[...] += jnp.sum(dadt * dt, axis=0, keepdims=True) * a
        ddsk_ref[...] += jnp.sum(red[3 * CHUNK:], axis=0, keepdims=True)
        ddtr = jnp.where(lane < H, ddt * _sigmoid(dtr_ref[...] + dtb_ref[...]), 0.0)
        ddtb_ref[...] += jnp.sum(ddtr, axis=0, keepdims=True)
        dzxd_ref[:, SSM_WIDTH + CONV_DIM:] = ddtr.astype(bf16)
        dxa_bm, dxa_cm = [], []
        for g in range(SSM_GROUPS):
            cmb, bmb, _ = groups[g]
            dcbb = dcb[g].astype(bf16)
            dxa_bm.append(dbm[g] + _dot_tn(dcbb, cmb))
            dxa_cm.append(dcm[g] + _dot(dcbb, bmb))
        dxc = jnp.concatenate([dy * dsk_x + dxd * dt_x] + dxa_bm + dxa_cm, axis=1) * (sg * (1.0 + xc * (1.0 - sg)))
        ext = jnp.concatenate([dxc, next_sc[...]], axis=0)
        dxbc = cw_ref[SSM_CONV - 1:SSM_CONV, :] * dxc
        for s in range(1, SSM_CONV):
            dxbc = dxbc + cw_ref[SSM_CONV - 1 - s:SSM_CONV - s, :] * _sel_dot(_shift_mat(CHUNK, CHUNK + HALO, s), ext, 2)
        dzxd_ref[:, SSM_WIDTH:SSM_WIDTH + CONV_DIM] = dxbc.astype(bf16)
        dcw_ref[...] += jnp.concatenate([jnp.sum(dxc * t, axis=0, keepdims=True) for t in taps], axis=0)
        dcb_ref[...] += jnp.sum(dxc, axis=0, keepdims=True)
        next_sc[...] = dxc[0:HALO, :]

    params = [conv_w, conv_b, dt_bias, a_log, d_skip, ssm_norm]
    mats = list(_head_mats())

    def rc(i):
        return nc - 1 - i

    in_specs = (_ssd_in_specs(nc, True)
                + [pl.BlockSpec((1, PAIRS, PAIR_W, N), lambda i: (rc(i), 0, 0, 0)), pl.BlockSpec((CHUNK, SSM_WIDTH), lambda i: (rc(i), 0))]
                + [_full(x.shape) for x in params + mats])
    return pl.pallas_call(
        body, name="ssd_bwd", grid=(nc,), in_specs=in_specs,
        out_specs=[pl.BlockSpec((CHUNK, ZXD), lambda i: (rc(i), 0))] + [_acc(x.shape) for x in params],
        out_shape=[jax.ShapeDtypeStruct((T, ZXD), bf16)] + [jax.ShapeDtypeStruct(x.shape, f32) for x in params],
        scratch_shapes=[pltpu.VMEM((PAIRS, PAIR_W, N), f32), pltpu.VMEM((HALO, CONV_DIM), f32)],
        compiler_params=_params(("arbitrary",)))(xbc, xbc, z, dtr, sprev, dyb, *params, *mats)


def _tail(h3, p, tgt, gp, wpg, bpg, wpp, gf, tm=512):
    T, D = h3.shape
    tm = min(tm, T)

    def head(gpre, pp, h, gf_, t):
        gate = _sigmoid(gpre)
        y = _rms(h + gate * pp, gf_)
        err = y - t
        return 0.5 * jnp.sum(jnp.mean(err * err, axis=-1))

    def body(h_ref, p_ref, t_ref, gp_ref, wpg_ref, bpg_ref, wpp_ref, gf_ref,
             dh_ref, loss_ref, dgp_ref, dwpg_ref, dbpg_ref, dwpp_ref, dgf_ref):
        @pl.when(pl.program_id(0) == 0)
        def _():
            for r in (loss_ref, dgp_ref, dwpg_ref, dbpg_ref, dwpp_ref, dgf_ref):
                r[...] = jnp.zeros_like(r)

        h = h_ref[...]
        npf, np_vjp = jax.vjp(_rms, h, gp_ref[...])
        npb = npf.astype(bf16)
        pb = p_ref[...].astype(bf16)
        gpre = _dot(npb, wpg_ref[...]) + bpg_ref[...]
        kp, _, cp = wpp_ref.shape
        pp = jnp.concatenate([_dot(pb, wpp_ref[k]) for k in range(kp)], axis=1)
        loss, head_vjp = jax.vjp(head, gpre, pp, h, gf_ref[...], t_ref[...])
        dgpre, dpp, dh_a, dgf, _ = head_vjp(jnp.ones((), f32))
        loss_ref[...] += loss
        dgf_ref[...] += dgf
        dbpg_ref[...] += jnp.sum(dgpre, axis=0, keepdims=True)
        dgb = dgpre.astype(bf16)
        dwpg_ref[...] += _dot_tn(npb, dgb)
        dppb = dpp.astype(bf16)
        for k in range(kp):
            dwpp_ref[k] += _dot_tn(pb, dppb[:, k * cp:(k + 1) * cp])
        dh_b, dgp = np_vjp(_dot_nt(dgb, wpg_ref[...]))
        dgp_ref[...] += dgp
        dh_ref[...] = dh_a + dh_b

    ins = [h3, p, tgt, gp, wpg, bpg, wpp, gf]
    in_specs = [_rows(tm, D), _rows(tm, p.shape[1]), _rows(tm, D)] + [_full(x.shape) for x in ins[3:]]
    acc_shapes = [(1, LANES), gp.shape, wpg.shape, bpg.shape, wpp.shape, gf.shape]
    return pl.pallas_call(
        body, name="tail", grid=(T // tm,), in_specs=in_specs,
        out_specs=[_rows(tm, D)] + [_acc(s) for s in acc_shapes],
        out_shape=[jax.ShapeDtypeStruct((T, D), f32)] + [jax.ShapeDtypeStruct(s, f32) for s in acc_shapes],
        compiler_params=_params(("arbitrary",)))(*ins)


def _adamw(name, w, g, m, v, tr=256):
    R, C = w.shape
    tr = _row_tile(R, tr)

    def body(w_ref, g_ref, m_ref, v_ref, d_ref, mo_ref, vo_ref):
        g_ = g_ref[...]
        m_ = ADAM_B1 * m_ref[...] + (1.0 - ADAM_B1) * g_
        v_ = ADAM_B2 * v_ref[...] + (1.0 - ADAM_B2) * jnp.square(g_)
        m_hat = m_ / (1.0 - ADAM_B1 ** ADAM_STEP)
        v_hat = v_ / (1.0 - ADAM_B2 ** ADAM_STEP)
        d_ref[...] = -ADAM_LR * (m_hat / (jnp.sqrt(v_hat) + ADAM_EPS) + ADAM_WD * w_ref[...])
        mo_ref[...] = m_
        vo_ref[...] = v_

    spec = pl.BlockSpec((tr, C), lambda i: (i, 0))
    return pl.pallas_call(body, name=name, grid=(R // tr,), in_specs=[spec] * 4, out_specs=[spec] * 3,
                          out_shape=[jax.ShapeDtypeStruct((R, C), f32)] * 3,
                          compiler_params=_params(("parallel",)))(w, g, m, v)


HBM = pl.BlockSpec(memory_space=pltpu.HBM)


def _me():
    return lax.axis_index("x"), lax.axis_index("y"), lax.axis_index("c")


def _other_chips(x, y):
    return [(1 - x, y), (x, 1 - y), (1 - x, 1 - y)]


def _remote(src, dst, send_sem, recv_sem, dev):
    return pltpu.make_async_remote_copy(src_ref=src, dst_ref=dst, send_sem=send_sem, recv_sem=recv_sem,
                                        device_id=dev, device_id_type=MESH)


def _sems(n):
    return [pltpu.SemaphoreType.DMA((n,)), pltpu.SemaphoreType.DMA((n,))]


def _gather_weights(shards, split):
    n = len(shards)

    def body(*refs):
        ins, outs = refs[:n], refs[n:2 * n]
        own_send, own_recv, ici_send, ici_recv, d2d_send, d2d_recv = refs[2 * n:]
        x, y, c = _me()
        my_chip = 2 * x + y
        sibling = (x, y, 1 - c)
        chips = _other_chips(x, y)

        def rows(i, half):
            hr = shards[i].shape[0] // 2
            return pl.ds(half * hr, hr) if split[i] else pl.ds(0, shards[i].shape[0])

        sends = []
        for i in range(n):
            for j, chip in enumerate(chips):
                cp = _remote(ins[i].at[rows(i, c)], outs[i].at[my_chip, rows(i, c)],
                             ici_send.at[3 * i + j], ici_recv.at[3 * i + j], (*chip, c))
                cp.start()
                sends.append(cp)
            cp = _remote(ins[i], outs[i].at[my_chip], own_send.at[i], own_recv.at[i], sibling)
            cp.start()
            sends.append(cp)
        for i in range(n):
            for j, chip in enumerate(chips):
                s = 3 * i + j
                land = outs[i].at[2 * chip[0] + chip[1], rows(i, c)]
                _remote(land, land, ici_send.at[s], ici_recv.at[s], (*chip, c)).wait_recv()
                if split[i]:
                    cp = _remote(land, land, d2d_send.at[s], d2d_recv.at[s], sibling)
                    cp.start()
                    sends.append(cp)
        for i in range(n):
            _remote(ins[i], outs[i].at[my_chip], own_send.at[i], own_recv.at[i], sibling).wait_recv()
            if split[i]:
                for j, chip in enumerate(chips):
                    s = 3 * i + j
                    land = outs[i].at[2 * chip[0] + chip[1], rows(i, 1 - c)]
                    _remote(land, land, d2d_send.at[s], d2d_recv.at[s], sibling).wait_recv()
        for cp in sends:
            cp.wait_send()

    return pl.pallas_call(
        body, name="gather_weights", out_shape=[jax.ShapeDtypeStruct((N_CHIPS,) + s.shape, s.dtype) for s in shards],
        in_specs=[HBM] * n, out_specs=[HBM] * n,
        scratch_shapes=_sems(n) + _sems(3 * n) + _sems(3 * n))(*shards)


def _swap_halves(name, grads):
    n = len(grads)

    def body(*refs):
        ins, outs, send, recv = refs[:n], refs[n:2 * n], refs[2 * n], refs[2 * n + 1]
        x, y, c = _me()
        copies = []
        for i in range(n):
            hr = grads[i].shape[1] // 2
            cp = _remote(ins[i].at[:, pl.ds((1 - c) * hr, hr), :], outs[i], send.at[i], recv.at[i], (x, y, 1 - c))
            cp.start()
            copies.append(cp)
        for cp in copies:
            cp.wait()

    return pl.pallas_call(
        body, name=name,
        out_shape=[jax.ShapeDtypeStruct((g.shape[0], g.shape[1] // 2, g.shape[2]), g.dtype) for g in grads],
        in_specs=[HBM] * n, out_specs=[HBM] * n, scratch_shapes=_sems(n))(*grads)


def _add_halves(name, grads, other, c_idx, th=592):
    K, R, C = grads.shape
    H = R // 2
    th = _row_tile(H, th, 16)
    nb = H // th

    def body(c_ref, g_ref, o_ref, out_ref):
        out_ref[...] = (g_ref[...].astype(f32) + o_ref[...].astype(f32)).astype(bf16)

    grid_spec = pltpu.PrefetchScalarGridSpec(
        num_scalar_prefetch=1, grid=(nb,),
        in_specs=[pl.BlockSpec((K, th, C), lambda i, c: (0, c[0] * nb + i, 0)),
                  pl.BlockSpec((K, th, C), lambda i, c: (0, i, 0))],
        out_specs=pl.BlockSpec((K, th, C), lambda i, c: (0, i, 0)))
    return pl.pallas_call(body, name=name, grid_spec=grid_spec,
                          out_shape=jax.ShapeDtypeStruct((K, H, C), bf16),
                          compiler_params=_params(("parallel",)))(c_idx, grads, other)


SEM = pl.BlockSpec(memory_space=pltpu.SEMAPHORE)
ANY = pl.BlockSpec(memory_space=pl.ANY)
EFFECT = pltpu.SideEffectType.DATAFLOW_SIDE_EFFECTING


def _copies_start(name, srcs, land_shapes, n_copies, make_copies, after):
    ns, nl = len(srcs), len(land_shapes)
    lands = [lax.empty(s.shape, s.dtype) for s in land_shapes]

    def body(*refs):
        src_refs, land_refs = refs[:ns], refs[ns:ns + nl]
        send, recv, token = refs[ns + nl + 1], refs[ns + nl + 2], refs[-1]
        for cp in make_copies(src_refs, land_refs, send, recv):
            cp.start()
        token[...] = jnp.zeros_like(token)

    buffers = list(srcs) + lands
    out = pl.pallas_call(
        body, name=name,
        out_shape=(pltpu.SemaphoreType.DMA((n_copies,)), pltpu.SemaphoreType.DMA((n_copies,)),
                   *[pltpu.HBM(b.shape, b.dtype) for b in buffers], jax.ShapeDtypeStruct((8, LANES), f32)),
        in_specs=[HBM] * (ns + nl) + [ANY],
        out_specs=(SEM, SEM, *[HBM] * (ns + nl), pl.BlockSpec(memory_space=pltpu.VMEM)),
        input_output_aliases={i: 2 + i for i in range(ns + nl)},
        compiler_params=pltpu.CompilerParams(has_side_effects=EFFECT),
    )(*[pltpu.with_memory_space_constraint(b, pltpu.HBM) for b in buffers], after)
    return out[0], out[1], list(out[2:2 + ns]), list(out[2 + ns:2 + ns + nl]), out[-1]


def _copies_wait(name, started, make_copies, after):
    send, recv, srcs, lands, _ = started
    ns, nl = len(srcs), len(lands)
    after = list(after)

    def body(*refs):
        src_refs, land_refs = refs[:ns], refs[ns:ns + nl]
        for cp in make_copies(src_refs, land_refs, refs[ns + nl], refs[ns + nl + 1]):
            cp.wait_send()
            cp.wait_recv()

    buffers = list(srcs) + list(lands)
    out = pl.pallas_call(
        body, name=name, out_shape=tuple(pltpu.HBM(b.shape, b.dtype) for b in buffers),
        in_specs=[HBM] * (ns + nl) + [SEM, SEM] + [ANY] * len(after), out_specs=tuple([HBM] * (ns + nl)),
        input_output_aliases={i: i for i in range(ns + nl)},
        compiler_params=pltpu.CompilerParams(has_side_effects=EFFECT),
    )(*buffers, send, recv, *after)
    return list(out[:ns]), list(out[ns:])


def _gather_copies(src_refs, land_refs, send, recv):
    x, y, c = _me()
    my_chip = 2 * x + y
    peers = [(*chip, c) for chip in _other_chips(x, y)] + [(x, y, 1 - c)]
    return [_remote(src_refs[i], land_refs[i].at[my_chip], send.at[4 * i + j], recv.at[4 * i + j], peer)
            for i in range(len(src_refs)) for j, peer in enumerate(peers)]


def _partial_copies(src_refs, land_refs, send, recv):
    x, y, c = _me()
    return [_remote(src_refs[i].at[2 * chip[0] + chip[1]], land_refs[i].at[j], send.at[3 * i + j], recv.at[3 * i + j], (*chip, c))
            for i in range(len(src_refs)) for j, chip in enumerate(_other_chips(x, y))]


def _small_copies(src_refs, land_refs, send, recv):
    x, y, c = _me()
    return [_remote(src_refs[0], land_refs[0].at[k - 1], send.at[k - 1], recv.at[k - 1], (x ^ (k >> 2), y ^ ((k >> 1) & 1), c ^ (k & 1)))
            for k in range(1, N_DEV)]


def _sum_small(own, slots, dev_idx):
    R, C = own.shape

    def body(dev_ref, own_ref, s_ref, o_ref):
        me = dev_ref[0]
        acc = jnp.zeros((R, C), f32)
        for d in range(N_DEV):
            k = me ^ d
            acc = acc + jnp.where(k == 0, own_ref[...], s_ref[jnp.maximum(k - 1, 0)])
        o_ref[...] = acc

    grid_spec = pltpu.PrefetchScalarGridSpec(
        num_scalar_prefetch=1, grid=(1,),
        in_specs=[pl.BlockSpec((R, C), lambda i, dev: (0, 0)), pl.BlockSpec((N_DEV - 1, R, C), lambda i, dev: (0, 0, 0))],
        out_specs=pl.BlockSpec((R, C), lambda i, dev: (0, 0)))
    return pl.pallas_call(body, name="sum_small", grid_spec=grid_spec, out_shape=jax.ShapeDtypeStruct((R, C), f32),
                          compiler_params=_params(("arbitrary",)))(dev_idx, own, slots)


def _sum_partials(name, part, recv, chip_idx, th=592):
    K, H, C = part.shape
    th = _row_tile(H, th, 16)

    def body(chip_ref, p_ref, r_ref, o_ref):
        acc = p_ref[...].astype(f32)
        for j in range(3):
            acc = acc + r_ref[j].astype(f32)
        o_ref[...] = acc

    grid_spec = pltpu.PrefetchScalarGridSpec(
        num_scalar_prefetch=1, grid=(H // th,),
        in_specs=[pl.BlockSpec((None, th, C), lambda i, chip: (chip[0], i, 0)),
                  pl.BlockSpec((3, th, C), lambda i, chip: (0, i, 0))],
        out_specs=pl.BlockSpec((th, C), lambda i, chip: (i, 0)))
    return pl.pallas_call(body, name=name, grid_spec=grid_spec, out_shape=jax.ShapeDtypeStruct((H, C), f32),
                          compiler_params=_params(("parallel",)))(chip_idx, part, recv)


def _share_halves(name, halves):
    n = len(halves)

    def body(*refs):
        ins, outs, send, recv = refs[:n], refs[n:2 * n], refs[2 * n], refs[2 * n + 1]
        x, y, c = _me()
        copies = []
        for i in range(n):
            cp = _remote(ins[i], outs[i], send.at[i], recv.at[i], (x, y, 1 - c))
            cp.start()
            copies.append(cp)
        for cp in copies:
            cp.wait()

    return pl.pallas_call(
        body, name=name, out_shape=[jax.ShapeDtypeStruct(h.shape, h.dtype) for h in halves],
        in_specs=[HBM] * n, out_specs=[HBM] * n, scratch_shapes=_sems(n))(*halves)


def _adamw_big(name, w, g_mine, g_theirs, m, v, c_idx, tr=320):
    R, C = w.shape
    H = R // 2
    tr = _row_tile(H, tr)
    nb = H // tr

    def body(c_ref, w_ref, gm_ref, gt_ref, m_ref, v_ref, g_ref, d_ref, mo_ref, vo_ref):
        g_ = jnp.where(pl.program_id(0) // nb == c_ref[0], gm_ref[...], gt_ref[...])
        g_ref[...] = g_
        m_ = ADAM_B1 * m_ref[...] + (1.0 - ADAM_B1) * g_
        v_ = ADAM_B2 * v_ref[...] + (1.0 - ADAM_B2) * jnp.square(g_)
        m_hat = m_ / (1.0 - ADAM_B1 ** ADAM_STEP)
        v_hat = v_ / (1.0 - ADAM_B2 ** ADAM_STEP)
        d_ref[...] = -ADAM_LR * (m_hat / (jnp.sqrt(v_hat) + ADAM_EPS) + ADAM_WD * w_ref[...])
        mo_ref[...] = m_
        vo_ref[...] = v_

    full = pl.BlockSpec((tr, C), lambda i, c: (i, 0))
    half = pl.BlockSpec((tr, C), lambda i, c: (i % nb, 0))
    grid_spec = pltpu.PrefetchScalarGridSpec(num_scalar_prefetch=1, grid=(2 * nb,),
                                             in_specs=[full, half, half, full, full], out_specs=[full] * 4)
    return pl.pallas_call(body, name=name, grid_spec=grid_spec, out_shape=[jax.ShapeDtypeStruct((R, C), f32)] * 4,
                          compiler_params=_params(("parallel",)))(c_idx, w, g_mine, g_theirs, m, v)


BIG = ("ffn1_w_gate", "ffn1_w_up", "ffn1_w_down", "w_in", "w_out", "ffn2_w_gate", "ffn2_w_up", "ffn2_w_down",
       "ple_w_gate", "ple_w_proj")


SMALL = ("ffn1_norm", "mix_norm", "gm_ln_g", "gm_ln_b", "gm_w_s", "gm_b_s", "gm_out_norm", "conv_b", "dt_bias", "a_log",
         "d_skip", "ssm_norm", "ffn2_norm", "ple_norm", "ple_b_gate", "final_norm")
SMALL_C = 1024


def _pack_small(vals):
    parts = []
    for v in vals:
        f = v.astype(f32).reshape(-1)
        parts.append(jnp.pad(f, (0, -f.shape[0] % SMALL_C)))
    flat = jnp.concatenate(parts)
    rows = flat.shape[0] // SMALL_C
    return jnp.pad(flat, (0, (-rows % 8) * SMALL_C)).reshape(-1, SMALL_C)


def _unpack_small(pack, shapes):
    flat = pack.reshape(-1)
    out, off = [], 0
    for s in shapes:
        n = 1
        for d in s:
            n *= d
        out.append(flat[off:off + n].reshape(s))
        off += n + (-n % SMALL_C)
    return out


def _pad_lanes(v):
    return jnp.pad(v, ((0, 0), (0, LANES - v.shape[1])))


def _pad_rows(a):
    pad = [(0, 0)] * a.ndim
    pad[-2] = (0, -a.shape[-2] % ROW_PAD)
    return jnp.pad(a, pad) if pad[-2][1] else a


FETCH = (("ffn1_w_gate", "ffn1_w_up", "ffn1_w_down"), ("w_in", "conv_w", "w_out"),
         ("ffn2_w_gate", "ffn2_w_up", "ffn2_w_down", "ple_w_gate", "ple_w_proj"))
TRANSPOSED = ("ffn1_w_gate", "ffn1_w_up", "ffn2_w_gate", "ffn2_w_up", "w_in")
ROW_PAD = 32
DONE = (("ffn2_w_gate", "ffn2_w_up", "ffn2_w_down", "w_out", "ple_w_gate", "ple_w_proj"), ("w_in",),
        ("ffn1_w_gate", "ffn1_w_up", "ffn1_w_down"))


def _local_step(x, p, tgt, fetch, S, on_grads):
    G = GM_WIDTH
    K = N_CHIPS
    b_st = S["gm_b_s"][0].T
    w_s = S["gm_w_s"][0]
    dtb, alog, dsk = _pad_lanes(S["dt_bias"]), _pad_lanes(S["a_log"]), _pad_lanes(S["d_skip"])
    gfin = S["final_norm"].reshape(1, -1)

    def rows(a):
        return a.reshape(-1, D_MODEL)

    def shards(a):
        return a.reshape(K, -1, D_MODEL)

    wg1, wu1, wd1 = [rows(a) for a in fetch(0, None)]
    h1, n1, a1, b1 = _ffn_fwd("ffn1_fwd", x, S["ffn1_norm"], wg1, wu1, wd1)
    w_in4, cw4, wo4 = fetch(1, h1)
    w_in = w_in4.reshape(IN_PROJ, D_MODEL)
    w_uv = w_in[:2 * G]
    w_zxd = jnp.pad(w_in[2 * G:], ((0, ZXD - (IN_PROJ - 2 * G)), (0, 0)))
    conv_w = jnp.transpose(cw4, (1, 0, 2)).reshape(SSM_CONV, CONV_DIM)
    wo = wo4.reshape(-1, D_MODEL)
    n2, uv, z, xbc, dtr, ya = _mix_fwd(h1, S["mix_norm"], w_uv, w_zxd, S["gm_ln_g"], S["gm_ln_b"], w_s, b_st, S["gm_out_norm"])
    yb, sprev = _ssd_fwd(xbc, z, dtr, conv_w, S["conv_b"], dtb, alog, dsk, S["ssm_norm"])
    wg2, wu2, wd2, wpg4, wpp4 = fetch(2, yb)
    wg2, wu2, wd2 = rows(wg2), rows(wu2), rows(wd2)
    h2, h3, n3, a2, b2 = _ffn_fwd("ffn2_fwd", h1, S["ffn2_norm"], wg2, wu2, wd2, pre=(ya, yb, wo))
    dh3, loss, dgp, dwpg, dbpg, dwpp, dgf = _tail(h3, p, tgt, S["ple_norm"], wpg4.reshape(-1, D_MODEL), S["ple_b_gate"], wpp4, gfin)
    dh2, da2, db2, hm2, dg_ffn2, dya, dyb = _ffn_bwd("ffn2_bwd", dh3, h2, S["ffn2_norm"], a2, b2, wg2, wu2, wd2, wo=wo, ga=G)
    dw_out = jnp.concatenate([_matmul_tn("dw_out_a", ya, dh2), _matmul_tn("dw_out_b", yb, dh2)], axis=0).reshape(wo4.shape)
    zero = on_grads(0, [shards(_matmul_tn("dw_ffn2_gate", da2, n3)), shards(_matmul_tn("dw_ffn2_up", db2, n3)),
                        shards(_matmul_tn("dw_ffn2_down", hm2, dh3, scale=0.5)), dw_out,
                        dwpg.astype(bf16).reshape(wpg4.shape), dwpp.astype(bf16)])
    dzxd, dcw, dcb, ddtb, dalog, ddsk, dgssm = _ssd_bwd(xbc, z, dtr, sprev, dyb, conv_w, S["conv_b"], dtb, alog, dsk,
                                                        S["ssm_norm"] + zero)
    dh1, duv, dg_mix, dlng, dlnb, dws, dbst, dgout = _mix_bwd(dh2, h1, S["mix_norm"], uv, dya, dzxd, w_uv, w_zxd, S["gm_ln_g"],
                                                              S["gm_ln_b"], w_s, b_st, S["gm_out_norm"])
    dw_in = jnp.concatenate([_matmul_tn("dw_in_uv", duv, n2), _matmul_tn("dw_in_zxd", dzxd, n2)[:IN_PROJ - 2 * G]], axis=0)
    zero = on_grads(1, [dw_in.reshape(w_in4.shape)])
    dx, da1, db1, hm1, dg_ffn1 = _ffn_bwd("ffn1_bwd", dh1, x, S["ffn1_norm"] + zero, a1, b1, wg1, wu1, wd1)
    zero = on_grads(2, [shards(_matmul_tn("dw_ffn1_gate", da1, n1)), shards(_matmul_tn("dw_ffn1_up", db1, n1)),
                        shards(_matmul_tn("dw_ffn1_down", hm1, dh1, scale=0.5))])
    loss = loss + zero
    nh = SSM_HEADS
    gS = {"ffn1_norm": dg_ffn1, "mix_norm": dg_mix, "gm_ln_g": dlng, "gm_ln_b": dlnb, "gm_w_s": dws[None], "gm_b_s": dbst.T[None],
          "gm_out_norm": dgout, "conv_b": dcb, "dt_bias": ddtb[:, :nh], "a_log": dalog[:, :nh], "d_skip": ddsk[:, :nh],
          "ssm_norm": dgssm, "ffn2_norm": dg_ffn2, "ple_norm": dgp, "ple_b_gate": dbpg, "final_norm": dgf.reshape(-1)}
    return loss, dx, dcw, gS


_WEIGHTS = ("ffn1_norm", "ffn1_w_gate", "ffn1_w_up", "ffn1_w_down", "mix_norm", "w_in", "gm_ln_g", "gm_ln_b", "gm_w_s", "gm_b_s",
            "gm_out_norm", "conv_w", "conv_b", "dt_bias", "a_log", "d_skip", "ssm_norm", "w_out", "ffn2_norm", "ffn2_w_gate",
            "ffn2_w_up", "ffn2_w_down", "ple_norm", "ple_w_gate", "ple_b_gate", "ple_w_proj", "final_norm")
_BIG_NAMES = BIG


def kernel(x, p, ffn1_norm, ffn1_w_gate, ffn1_w_up, ffn1_w_down, mix_norm, w_in, gm_ln_g, gm_ln_b, gm_w_s, gm_b_s, gm_out_norm, conv_w, conv_b, dt_bias, a_log, d_skip, ssm_norm, w_out, ffn2_norm, ffn2_w_gate, ffn2_w_up, ffn2_w_down, ple_norm, ple_w_gate, ple_b_gate, ple_w_proj, final_norm, loss_target, m_ffn1_norm, m_ffn1_w_gate, m_ffn1_w_up, m_ffn1_w_down, m_mix_norm, m_w_in, m_gm_ln_g, m_gm_ln_b, m_gm_w_s, m_gm_b_s, m_gm_out_norm, m_conv_w, m_conv_b, m_dt_bias, m_a_log, m_d_skip, m_ssm_norm, m_w_out, m_ffn2_norm, m_ffn2_w_gate, m_ffn2_w_up, m_ffn2_w_down, m_ple_norm, m_ple_w_gate, m_ple_b_gate, m_ple_w_proj, m_final_norm, v_ffn1_norm, v_ffn1_w_gate, v_ffn1_w_up, v_ffn1_w_down, v_mix_norm, v_w_in, v_gm_ln_g, v_gm_ln_b, v_gm_w_s, v_gm_b_s, v_gm_out_norm, v_conv_w, v_conv_b, v_dt_bias, v_a_log, v_d_skip, v_ssm_norm, v_w_out, v_ffn2_norm, v_ffn2_w_gate, v_ffn2_w_up, v_ffn2_w_down, v_ple_norm, v_ple_w_gate, v_ple_b_gate, v_ple_w_proj, v_final_norm):
    given = dict(locals())
    w = {n: given[n] for n in _WEIGHTS}
    m = {n: given["m_" + n] for n in _WEIGHTS}
    v = {n: given["v_" + n] for n in _WEIGHTS}

    c_idx = lax.axis_index("c").astype(jnp.int32).reshape(1)
    chip = 2 * lax.axis_index("x") + lax.axis_index("y")
    chip_idx = chip.astype(jnp.int32).reshape(1)

    shard = {n: (jnp.swapaxes(w[n][0], 0, 1) if n in TRANSPOSED else w[n][0]).astype(bf16) for n in BIG}
    shard["conv_w"] = w["conv_w"][0]
    first = _gather_weights([shard[n] for n in FETCH[0]], [True] * len(FETCH[0]))
    fetching, after = [], first[-1]
    for k in (1, 2):
        srcs = [shard[n] for n in FETCH[k]]
        lands = [jax.ShapeDtypeStruct((N_CHIPS,) + s.shape, s.dtype) for s in srcs]
        fetching.append(_copies_start("gather%d_start" % k, srcs, lands, 4 * len(srcs), _gather_copies, after))
        after = fetching[-1][4]

    def fetch(k, after_):
        return first if k == 0 else _copies_wait("gather%d_wait" % k, fetching[k - 1], _gather_copies, [after_])[1]

    exchanging = []

    def on_grads(k, grads):
        grads = [_pad_rows(g_) for g_ in grads]
        others = _swap_halves("swap%d" % k, grads)
        parts = [_add_halves("add_" + n, g_, o_, c_idx) for n, g_, o_ in zip(DONE[k], grads, others)]
        lands = [jax.ShapeDtypeStruct((3,) + p_.shape[1:], p_.dtype) for p_ in parts]
        exchanging.append(_copies_start("exchange%d_start" % k, parts, lands, 3 * len(parts), _partial_copies, c_idx))
        return exchanging[-1][4][0, 0]

    S = {n: w[n] for n in SMALL}
    S["ffn1_norm"] = S["ffn1_norm"] + after[0, 0]
    loss, dx, dcw, gS = _local_step(x[0], p[0, 0], loss_target[0], fetch, S, on_grads)

    small = _pack_small([gS[n] for n in SMALL] + [dcw, loss[:, :1]])
    small_lands = [jax.ShapeDtypeStruct((N_DEV - 1,) + small.shape, small.dtype)]
    small_st = _copies_start("small_start", [small], small_lands, N_DEV - 1, _small_copies, c_idx)

    g, delta, new_m, new_v = {}, {}, {}, {}
    after = [small_st[4]]
    for k in range(len(DONE)):
        parts, recv = _copies_wait("exchange%d_wait" % k, exchanging[k], _partial_copies, after)
        mine = [_sum_partials("sum_" + n, p_, r_, chip_idx) for n, p_, r_ in zip(DONE[k], parts, recv)]
        theirs = _share_halves("share%d" % k, mine)
        after = []
        for n, gm_, gt_ in zip(DONE[k], mine, theirs):
            flip = (lambda a: jnp.swapaxes(a, 0, 1)) if n in TRANSPOSED else (lambda a: a)
            rows = flip(w[n][0]).shape[0]
            w_, m_, v_ = [_pad_rows(flip(a[n][0])) for a in (w, m, v)]
            outs = _adamw_big("adamw_" + n, w_, gm_, gt_, m_, v_, c_idx)
            g[n], delta[n], new_m[n], new_v[n] = [flip(o[:rows])[None] for o in outs]
            after.append(outs[3])
    (own,), (slots,) = _copies_wait("small_wait", small_st, _small_copies, after)
    dev_idx = (2 * chip + lax.axis_index("c")).astype(jnp.int32).reshape(1)
    small_shapes = [w[n].shape for n in SMALL] + [dcw.shape, (1, 1)]
    small_sum = _unpack_small(_sum_small(own, slots, dev_idx), small_shapes)
    g.update({n: small_sum[i] for i, n in enumerate(SMALL)})
    cshard = w["conv_w"].shape[2]
    g["conv_w"] = lax.dynamic_slice_in_dim(small_sum[len(SMALL)], chip * cshard, cshard, axis=1)[None]
    loss_total = small_sum[len(SMALL) + 1].reshape(())
    sm_names = SMALL + ("conv_w",)
    sm_shapes = [w[n].shape for n in sm_names]
    d_s, m_s, v_s = _adamw("adamw_small", _pack_small([w[n] for n in sm_names]), _pack_small([g[n] for n in sm_names]),
                           _pack_small([m[n] for n in sm_names]), _pack_small([v[n] for n in sm_names]))
    for dst, src in ((delta, d_s), (new_m, m_s), (new_v, v_s)):
        for n, val in zip(sm_names, _unpack_small(src, sm_shapes)):
            dst[n] = val

    return (loss_total, dx[None], *[g[n] for n in _WEIGHTS], *[delta[n] for n in _WEIGHTS],
            *[new_m[n] for n in _WEIGHTS], *[new_v[n] for n in _WEIGHTS])
```

```python
import functools

import jax
import jax.numpy as jnp
from jax import lax
from jax.experimental import pallas as pl
from jax.experimental.pallas import tpu as pltpu

f32 = jnp.float32
bf16 = jnp.bfloat16
MESH = pl.DeviceIdType.MESH
HIGHEST = lax.Precision.HIGHEST

EPS = 1e-6
N_CHIPS = 4
N_DEV = 8
D_MODEL = 1024
D_FF = 2816
D_PLE = 256
GM_WIDTH = 1024
GM_HEADS = 8
CHUNK = 128
SSM_WIDTH = 1024
SSM_HEADS = 16
SSM_HEAD_DIM = 64
SSM_GROUPS = 2
SSM_STATE = 128
SSM_CONV = 4
CONV_DIM = SSM_WIDTH + 2 * SSM_GROUPS * SSM_STATE
IN_PROJ = 2 * GM_WIDTH + SSM_WIDTH + CONV_DIM + SSM_HEADS
LANES = 128
ZXD = SSM_WIDTH + CONV_DIM + LANES

ADAM_LR = 0.001
ADAM_B1 = 0.9
ADAM_B2 = 0.999
ADAM_EPS = 1e-08
ADAM_WD = 0.01
ADAM_STEP = 10

VMEM_LIMIT = 56 * 1024 * 1024


def _dot(a, b):
    return jnp.dot(a, b, preferred_element_type=f32)


def _dot_nt(a, b):
    return lax.dot_general(a, b, (((1,), (1,)), ((), ())), preferred_element_type=f32)


def _dot_tn(a, b):
    return lax.dot_general(a, b, (((0,), (0,)), ((), ())), preferred_element_type=f32)


def _rms(x, g):
    return x * lax.rsqrt(jnp.mean(x * x, axis=-1, keepdims=True) + EPS) * g


def _gelu(x):
    return 0.5 * x * (1.0 + lax.erf(x * 0.7071067811865476))


def _layernorm(x, g, b):
    mu = jnp.mean(x, axis=-1, keepdims=True)
    xc = x - mu
    return xc * lax.rsqrt(jnp.mean(xc * xc, axis=-1, keepdims=True) + EPS) * g + b


def _sigmoid(x):
    return 1.0 / (1.0 + jnp.exp(-x))


def _softplus(x):
    return jnp.maximum(x, 0.0) + jnp.log(1.0 + jnp.exp(-jnp.abs(x)))


def _full(shape):
    nd = len(shape)
    return pl.BlockSpec(shape, lambda *_: (0,) * nd, pipeline_mode=pl.Buffered(1))


def _acc(shape):
    nd = len(shape)
    return pl.BlockSpec(shape, lambda *_: (0,) * nd)


def _rows(tm, ncols):
    return pl.BlockSpec((tm, ncols), lambda i: (i, 0))


def _params(sem):
    return pltpu.CompilerParams(dimension_semantics=sem, vmem_limit_bytes=VMEM_LIMIT)


def _row_tile(rows, target, mult=8):
    best = rows
    for t in range(mult, min(rows, target) + 1, mult):
        if rows % t == 0:
            best = t
    return best if best <= target else rows


def _ffn_fwd(name, h, g, wg, wu, wd, pre=None, tm=256):
    T, D = h.shape
    F = wg.shape[0]
    tm = min(tm, T)

    def body(*refs):
        if pre is None:
            h_ref, g_ref, wg_ref, wu_ref, wd_ref, ho_ref, n_ref, a_ref, b_ref = refs
            hin = h_ref[...]
        else:
            (h_ref, ya_ref, yb_ref, wo_ref, g_ref, wg_ref, wu_ref, wd_ref,
             hi_ref, ho_ref, n_ref, a_ref, b_ref) = refs
            ga = ya_ref.shape[1]
            hin = h_ref[...] + _dot(ya_ref[...], wo_ref[:ga, :]) + _dot(yb_ref[...], wo_ref[ga:, :])
            hi_ref[...] = hin
        n = _rms(hin, g_ref[...]).astype(bf16)
        n_ref[...] = n
        a = _dot_nt(n, wg_ref[...]).astype(bf16)
        b = _dot_nt(n, wu_ref[...]).astype(bf16)
        a_ref[...] = a
        b_ref[...] = b
        af = a.astype(f32)
        hm = (af * _sigmoid(af) * b.astype(f32)).astype(bf16)
        ho_ref[...] = hin + 0.5 * _dot(hm, wd_ref[...])

    ins = [h] + (list(pre) if pre is not None else []) + [g, wg, wu, wd]
    in_specs = [_rows(tm, D)]
    if pre is not None:
        in_specs += [_rows(tm, pre[0].shape[1]), _rows(tm, pre[1].shape[1]), _full(pre[2].shape)]
    in_specs += [_full(g.shape), _full(wg.shape), _full(wu.shape), _full(wd.shape)]
    outs = [jax.ShapeDtypeStruct((T, D), f32), jax.ShapeDtypeStruct((T, D), bf16),
            jax.ShapeDtypeStruct((T, F), bf16), jax.ShapeDtypeStruct((T, F), bf16)]
    out_specs = [_rows(tm, D), _rows(tm, D), _rows(tm, F), _rows(tm, F)]
    if pre is not None:
        outs = [jax.ShapeDtypeStruct((T, D), f32)] + outs
        out_specs = [_rows(tm, D)] + out_specs
    return pl.pallas_call(body, name=name, grid=(T // tm,), in_specs=in_specs, out_specs=out_specs,
                          out_shape=outs, compiler_params=_params(("parallel",)))(*ins)


def _ffn_bwd(name, dh, hin, g, a, b, wg, wu, wd, wo=None, ga=0, tm=256):
    T, D = dh.shape
    F = wg.shape[0]
    tm = min(tm, T)

    def body(*refs):
        if wo is None:
            (dh_ref, hin_ref, g_ref, a_ref, b_ref, wg_ref, wu_ref, wd_ref,
             dhi_ref, da_ref, db_ref, hm_ref, dg_ref) = refs
        else:
            (dh_ref, hin_ref, g_ref, a_ref, b_ref, wg_ref, wu_ref, wd_ref, wo_ref,
             dhi_ref, da_ref, db_ref, hm_ref, dg_ref, dya_ref, dyb_ref) = refs

        @pl.when(pl.program_id(0) == 0)
        def _():
            dg_ref[...] = jnp.zeros_like(dg_ref)

        dh_ = dh_ref[...]
        dhb = (0.5 * dh_).astype(bf16)
        dhm = _dot_nt(dhb, wd_ref[...])
        af = a_ref[...].astype(f32)
        bf = b_ref[...].astype(f32)
        sg = _sigmoid(af)
        sl_ = af * sg
        da = (dhm * bf * (sg * (1.0 + af * (1.0 - sg)))).astype(bf16)
        db = (dhm * sl_).astype(bf16)
        da_ref[...] = da
        db_ref[...] = db
        hm_ref[...] = (sl_ * bf).astype(bf16)
        dn = _dot(da, wg_ref[...]) + _dot(db, wu_ref[...])
        _, vjp = jax.vjp(_rms, hin_ref[...], g_ref[...])
        dx, dg = vjp(dn)
        dhi = dh_ + dx
        dhi_ref[...] = dhi
        dg_ref[...] += dg
        if wo is not None:
            dhib = dhi.astype(bf16)
            dya_ref[...] = _dot_nt(dhib, wo_ref[:ga, :]).astype(bf16)
            dyb_ref[...] = _dot_nt(dhib, wo_ref[ga:, :]).astype(bf16)

    ins = [dh, hin, g, a, b, wg, wu, wd]
    in_specs = [_rows(tm, D), _rows(tm, D), _full(g.shape), _rows(tm, F), _rows(tm, F),
                _full(wg.shape), _full(wu.shape), _full(wd.shape)]
    act = jax.ShapeDtypeStruct((T, F), bf16)
    outs = [jax.ShapeDtypeStruct((T, D), f32), act, act, act, jax.ShapeDtypeStruct(g.shape, f32)]
    out_specs = [_rows(tm, D), _rows(tm, F), _rows(tm, F), _rows(tm, F), _acc(g.shape)]
    if wo is not None:
        gb = wo.shape[0] - ga
        ins += [wo]
        in_specs += [_full(wo.shape)]
        outs += [jax.ShapeDtypeStruct((T, ga), bf16), jax.ShapeDtypeStruct((T, gb), bf16)]
        out_specs += [_rows(tm, ga), _rows(tm, gb)]
    return pl.pallas_call(body, name=name, grid=(T // tm,), in_specs=in_specs, out_specs=out_specs,
                          out_shape=outs, compiler_params=_params(("arbitrary",)))(*ins)


def _matmul_tn(name, a, b, scale=1.0, tk=2048):
    T, M = a.shape
    N = b.shape[1]
    tk = min(tk, T)
    nk = T // tk
    tn = LANES * max(d for d in range(1, N // LANES + 1) if (N // LANES) % d == 0 and (d == 1 or M * d * LANES * 4 <= 6 * 1024 * 1024))

    def body(a_ref, b_ref, o_ref, acc):
        k = pl.program_id(1)

        @pl.when(k == 0)
        def _():
            acc[...] = jnp.zeros_like(acc)

        bb = b_ref[...]
        if scale != 1.0:
            bb = bb * scale
        acc[...] += _dot_tn(a_ref[...].astype(bf16), bb.astype(bf16))

        @pl.when(k == nk - 1)
        def _():
            o_ref[...] = acc[...].astype(bf16)

    return pl.pallas_call(
        body, name=name, grid=(N // tn, nk),
        in_specs=[pl.BlockSpec((tk, M), lambda j, k: (k, 0)), pl.BlockSpec((tk, tn), lambda j, k: (k, j))],
        out_specs=pl.BlockSpec((M, tn), lambda j, k: (0, j)),
        out_shape=jax.ShapeDtypeStruct((M, N), bf16), scratch_shapes=[pltpu.VMEM((M, tn), f32)],
        compiler_params=_params(("parallel", "arbitrary")))(a, b)


def _gm_pre(u, v, ln_g, ln_b):
    return _gelu(u), _layernorm(_gelu(v), ln_g, ln_b)


def _tril_mask():
    r = lax.broadcasted_iota(jnp.int32, (CHUNK, CHUNK), 0)
    c = lax.broadcasted_iota(jnp.int32, (CHUNK, CHUNK), 1)
    return c <= r


def _gm_mix(vnb, ws_ref, bst, mixed_sc, tm):
    mask = _tril_mask()
    for h in range(GM_HEADS):
        wt = jnp.where(mask, ws_ref[h], 0.0).astype(bf16)
        bias = bst[:, h:h + 1]
        for q in range(tm // CHUNK):
            rs = slice(q * CHUNK, (q + 1) * CHUNK)
            cs = slice(h * CHUNK, (h + 1) * CHUNK)
            mixed_sc[rs, cs] = _dot(wt, vnb[rs, cs]) + bias


def _mix_fwd(h1, gmix, w_uv, w_zxd, ln_g, ln_b, w_s, b_st, gout, tm=512):
    T, D = h1.shape
    tm = min(tm, T)
    G = GM_WIDTH

    def body(h_ref, g_ref, wuv_ref, wzxd_ref, lng_ref, lnb_ref, ws_ref, bst_ref, gout_ref,
             n_ref, uv_ref, z_ref, xbc_ref, dt_ref, ya_ref, mixed_sc):
        n = _rms(h_ref[...], g_ref[...]).astype(bf16)
        n_ref[...] = n
        u = _dot_nt(n, wuv_ref[:G, :]).astype(bf16)
        v = _dot_nt(n, wuv_ref[G:, :]).astype(bf16)
        uv_ref[:, :G] = u
        uv_ref[:, G:] = v
        z_ref[...] = _dot_nt(n, wzxd_ref[:SSM_WIDTH, :]).astype(bf16)
        xbc_ref[...] = _dot_nt(n, wzxd_ref[SSM_WIDTH:SSM_WIDTH + CONV_DIM, :]).astype(bf16)
        dt_ref[...] = _dot_nt(n, wzxd_ref[SSM_WIDTH + CONV_DIM:, :])
        ug, vn = _gm_pre(u.astype(f32), v.astype(f32), lng_ref[...], lnb_ref[...])
        _gm_mix(vn.astype(bf16), ws_ref, bst_ref[...], mixed_sc, tm)
        ya_ref[...] = _rms(ug * mixed_sc[...], gout_ref[...]).astype(bf16)

    ins = [h1, gmix, w_uv, w_zxd, ln_g, ln_b, w_s, b_st, gout]
    in_specs = [_rows(tm, D)] + [_full(x.shape) for x in ins[1:]]
    outs = [jax.ShapeDtypeStruct((T, D), bf16), jax.ShapeDtypeStruct((T, 2 * G), bf16),
            jax.ShapeDtypeStruct((T, SSM_WIDTH), bf16), jax.ShapeDtypeStruct((T, CONV_DIM), bf16),
            jax.ShapeDtypeStruct((T, LANES), f32), jax.ShapeDtypeStruct((T, G), bf16)]
    out_specs = [_rows(tm, D), _rows(tm, 2 * G), _rows(tm, SSM_WIDTH), _rows(tm, CONV_DIM), _rows(tm, LANES), _rows(tm, G)]
    return pl.pallas_call(body, name="mix_fwd", grid=(T // tm,), in_specs=in_specs, out_specs=out_specs,
                          out_shape=outs, scratch_shapes=[pltpu.VMEM((tm, G), f32)],
                          compiler_params=_params(("parallel",)))(*ins)


def _mix_bwd(dh, h1, gmix, uv, dya, dzxd, w_uv, w_zxd, ln_g, ln_b, w_s, b_st, gout, tm=256):
    T, D = dh.shape
    tm = min(tm, T)
    G = GM_WIDTH

    def body(dh_ref, h_ref, g_ref, uv_ref, dya_ref, dzxd_ref, wuv_ref, wzxd_ref, lng_ref, lnb_ref, ws_ref, bst_ref, gout_ref,
             dhi_ref, duv_ref, dg_ref, dlng_ref, dlnb_ref, dws_ref, dbst_ref, dgout_ref, mixed_sc, dvn_sc):
        @pl.when(pl.program_id(0) == 0)
        def _():
            for r in (dg_ref, dlng_ref, dlnb_ref, dws_ref, dbst_ref, dgout_ref):
                r[...] = jnp.zeros_like(r)

        dn_z = _dot(dzxd_ref[...], wzxd_ref[...])
        u = uv_ref[:, :G].astype(f32)
        v = uv_ref[:, G:].astype(f32)
        (ug, vn), pre_vjp = jax.vjp(_gm_pre, u, v, lng_ref[...], lnb_ref[...])
        vnb = vn.astype(bf16)
        _gm_mix(vnb, ws_ref, bst_ref[...], mixed_sc, tm)
        mixed = mixed_sc[...]
        _, out_vjp = jax.vjp(_rms, ug * mixed, gout_ref[...])
        dpre, dgout = out_vjp(dya_ref[...].astype(f32))
        dgout_ref[...] += dgout
        dug = dpre * mixed
        dmixed = dpre * ug
        mask = _tril_mask()
        lane = lax.broadcasted_iota(jnp.int32, (1, GM_HEADS), 1)
        dbst = jnp.zeros((CHUNK, GM_HEADS), f32)
        for h in range(GM_HEADS):
            wt = jnp.where(mask, ws_ref[h], 0.0).astype(bf16)
            cs = slice(h * CHUNK, (h + 1) * CHUNK)
            dw = jnp.zeros((CHUNK, CHUNK), f32)
            for q in range(tm // CHUNK):
                rs = slice(q * CHUNK, (q + 1) * CHUNK)
                dm = dmixed[rs, cs]
                dmb = dm.astype(bf16)
                dw = dw + _dot_nt(dmb, vnb[rs, cs])
                dbst = dbst + jnp.sum(dm, axis=1, keepdims=True) * (lane == h).astype(f32)
                dvn_sc[rs, cs] = _dot_tn(wt, dmb)
            dws_ref[h] += jnp.where(mask, dw, 0.0)
        dbst_ref[...] += dbst
        du, dv, dlng, dlnb = pre_vjp((dug, dvn_sc[...]))
        duv = jnp.concatenate([du.astype(bf16), dv.astype(bf16)], axis=1)
        duv_ref[...] = duv
        dlng_ref[...] += dlng
        dlnb_ref[...] += dlnb
        dn = dn_z + _dot(duv, wuv_ref[...])
        _, vjp = jax.vjp(_rms, h_ref[...], g_ref[...])
        dx, dg = vjp(dn)
        dhi_ref[...] = dh_ref[...] + dx
        dg_ref[...] += dg

    ins = [dh, h1, gmix, uv, dya, dzxd, w_uv, w_zxd, ln_g, ln_b, w_s, b_st, gout]
    in_specs = ([_rows(tm, D), _rows(tm, D), _full(gmix.shape), _rows(tm, 2 * G), _rows(tm, G), _rows(tm, dzxd.shape[1])]
                + [_full(x.shape) for x in ins[6:]])
    accs = (gmix, ln_g, ln_b, w_s, b_st, gout)
    outs = ([jax.ShapeDtypeStruct((T, D), f32), jax.ShapeDtypeStruct((T, 2 * G), bf16)]
            + [jax.ShapeDtypeStruct(x.shape, f32) for x in accs])
    out_specs = [_rows(tm, D), _rows(tm, 2 * G)] + [_acc(x.shape) for x in accs]
    return pl.pallas_call(body, name="mix_bwd", grid=(T // tm,), in_specs=in_specs, out_specs=out_specs,
                          out_shape=outs, scratch_shapes=[pltpu.VMEM((tm, G), f32), pltpu.VMEM((tm, G), f32)],
                          compiler_params=_params(("arbitrary",)))(*ins)


HALO = 16
PAIRS = SSM_HEADS // 2
PAIR_W = 2 * SSM_HEAD_DIM


def _split(x, n):
    parts = []
    for _ in range(n):
        p = x.astype(bf16)
        parts.append(p)
        x = x - p.astype(f32)
    return parts


def _dot_sel(x, sel_n, n):
    return _dot(jnp.concatenate(_split(x, n), axis=1), sel_n)


def _sel_dot(sel, x, n):
    return _dot(jnp.concatenate([sel] * n, axis=1), jnp.concatenate(_split(x, n), axis=0))


EXPAND_SPLIT = 3
REDUCE_SPLIT = 2


def _head_mats():
    ex = (jnp.arange(SSM_WIDTH)[None, :] // SSM_HEAD_DIM == jnp.arange(LANES)[:, None]).astype(bf16)
    return jnp.tile(ex, (EXPAND_SPLIT, 1)), jnp.tile(ex.T, (REDUCE_SPLIT, 1))


def _shift_mat(rows, cols, off):
    r = lax.broadcasted_iota(jnp.int32, (rows, cols), 0)
    c = lax.broadcasted_iota(jnp.int32, (rows, cols), 1)
    return (c == r + off).astype(bf16)


def _ssd_conv(c, xbc_ref, halo_ref, cw_ref, cb_ref):
    halo = halo_ref[...]
    ext = jnp.concatenate([jnp.where(c > 0, halo, jnp.zeros_like(halo)), xbc_ref[...]], axis=0)
    xc = cb_ref[...] + cw_ref[SSM_CONV - 1:SSM_CONV, :] * xbc_ref[...].astype(f32)
    for j in range(SSM_CONV - 1):
        xc = xc + cw_ref[j:j + 1, :] * _dot(_shift_mat(CHUNK, HALO + CHUNK, HALO - SSM_CONV + 1 + j), ext)
    return xc


def _ssd_front(xc, dtr_ref, dtb_ref, alog_ref):
    sg = _sigmoid(xc)
    xa = xc * sg
    dt = _softplus(dtr_ref[...] + dtb_ref[...])
    a = -jnp.exp(alog_ref[...])
    acs = jnp.dot(_tril_mask().astype(f32), dt * a, preferred_element_type=f32, precision=HIGHEST)
    return sg, xa, dt, a, acs


def _ssd_wide(xa, dt, acs, dsk, ex):
    dt_x = _dot_sel(dt, ex, EXPAND_SPLIT)
    acs_x = _dot_sel(acs, ex, EXPAND_SPLIT)
    dsk_x = _dot_sel(jnp.broadcast_to(dsk, (8, LANES)), ex, EXPAND_SPLIT)[0:1]
    e_x = jnp.exp(acs_x)
    r_x = jnp.exp(acs_x[CHUNK - 1:CHUNK, :] - acs_x)
    xs = xa[:, :SSM_WIDTH]
    xd = xs * dt_x
    return dt_x, dsk_x, e_x, r_x, xs, xd, xd * r_x


def _pair_stack(v, lo):
    return jnp.concatenate([jnp.where(lo, v, 0.0), jnp.where(lo, 0.0, v)], axis=0)


def _ssd_pair(j, acs, acs_t, cb):
    out = []
    tril = _tril_mask()
    for h in (2 * j, 2 * j + 1):
        dk = jnp.exp(jnp.where(tril, acs[:, h:h + 1] - acs_t[h:h + 1, :], -jnp.inf))
        out.append((dk, cb * dk))
    return out


def _pair_col(row_lo, tot, j):
    return jnp.exp(jnp.where(row_lo, tot[:, 2 * j:2 * j + 1], tot[:, 2 * j + 1:2 * j + 2]))


def _gated_norm(y, z, g):
    yg = y * (z * _sigmoid(z))
    half = SSM_WIDTH // SSM_GROUPS
    parts = []
    for k in range(SSM_GROUPS):
        s = yg[:, k * half:(k + 1) * half]
        parts.append(s * lax.rsqrt(jnp.mean(s * s, axis=-1, keepdims=True) + EPS))
    return jnp.concatenate(parts, axis=1) * g


def _group_mats(xa):
    out = []
    for g in range(SSM_GROUPS):
        bm = xa[:, SSM_WIDTH + g * SSM_STATE:SSM_WIDTH + (g + 1) * SSM_STATE].astype(bf16)
        cm = xa[:, SSM_WIDTH + (SSM_GROUPS + g) * SSM_STATE:SSM_WIDTH + (SSM_GROUPS + g + 1) * SSM_STATE].astype(bf16)
        out.append((cm, bm, _dot_nt(cm, bm)))
    return out


def _ssd_fwd(xbc, z, dtr, conv_w, conv_b, dt_bias, a_log, d_skip, ssm_norm):
    T = xbc.shape[0]
    nc = T // CHUNK
    N = SSM_STATE

    def body(xbc_ref, halo_ref, z_ref, dtr_ref, cw_ref, cb_ref, dtb_ref, alog_ref, dsk_ref, g_ref, ex_ref,
             yb_ref, xc_ref, sprev_ref, s_sc):
        c = pl.program_id(0)

        @pl.when(c == 0)
        def _():
            s_sc[...] = jnp.zeros_like(s_sc)

        xc = _ssd_conv(c, xbc_ref, halo_ref, cw_ref, cb_ref)
        xc_ref[...] = xc
        _, xa, dt, _, acs = _ssd_front(xc, dtr_ref, dtb_ref, alog_ref)
        _, dsk_x, e_x, _, xs, xd, gm = _ssd_wide(xa, dt, acs, dsk_ref[...], ex_ref[...])
        acs_t = acs.T
        tot = acs[CHUNK - 1:CHUNK, :]
        groups = _group_mats(xa)
        lo = lax.broadcasted_iota(jnp.int32, (CHUNK, PAIR_W), 1) < SSM_HEAD_DIM
        row_lo = lax.broadcasted_iota(jnp.int32, (PAIR_W, 1), 0) < SSM_HEAD_DIM
        ys = []
        for j in range(PAIRS):
            cmb, bmb, cb = groups[j // (PAIRS // SSM_GROUPS)]
            ps = slice(j * PAIR_W, (j + 1) * PAIR_W)
            (_, m0), (_, m1) = _ssd_pair(j, acs, acs_t, cb)
            sp = s_sc[j]
            yd = _dot(jnp.concatenate([m0, m1], axis=1).astype(bf16), _pair_stack(xd[:, ps], lo).astype(bf16))
            ys.append(yd + e_x[:, ps] * _dot_nt(cmb, sp.astype(bf16)))
            sprev_ref[0, j] = sp
            s_sc[j] = _pair_col(row_lo, tot, j) * sp + _dot_tn(gm[:, ps].astype(bf16), bmb)
        y = jnp.concatenate(ys, axis=1) + xs * dsk_x
        yb_ref[...] = _gated_norm(y, z_ref[...].astype(f32), g_ref[...]).astype(bf16)

    params = [conv_w, conv_b, dt_bias, a_log, d_skip, ssm_norm, _head_mats()[0]]
    hp = CHUNK // HALO
    in_specs = [_rows(CHUNK, CONV_DIM), pl.BlockSpec((HALO, CONV_DIM), lambda i: (jnp.maximum(i * hp - 1, 0), 0)),
                _rows(CHUNK, SSM_WIDTH), _rows(CHUNK, LANES)] + [_full(x.shape) for x in params]
    return pl.pallas_call(
        body, name="ssd_fwd", grid=(nc,), in_specs=in_specs,
        out_specs=[_rows(CHUNK, SSM_WIDTH), _rows(CHUNK, CONV_DIM), pl.BlockSpec((1, PAIRS, PAIR_W, N), lambda i: (i, 0, 0, 0))],
        out_shape=[jax.ShapeDtypeStruct((T, SSM_WIDTH), bf16), jax.ShapeDtypeStruct((T, CONV_DIM), f32),
                   jax.ShapeDtypeStruct((nc, PAIRS, PAIR_W, N), f32)],
        scratch_shapes=[pltpu.VMEM((PAIRS, PAIR_W, N), f32)],
        compiler_params=_params(("arbitrary",)))(xbc, xbc, z, dtr, *params)


def _ssd_bwd(xbc, xc, z, dtr, sprev, dyb, conv_w, conv_b, dt_bias, a_log, d_skip, ssm_norm):
    T = xbc.shape[0]
    nc = T // CHUNK
    H, N = SSM_HEADS, SSM_STATE
    PG = PAIRS // SSM_GROUPS

    def body(xbc_ref, xc_ref, z_ref, dtr_ref, sprev_ref, dyb_ref, cw_ref, cb_ref, dtb_ref, alog_ref, dsk_ref, g_ref,
             ex_ref, rd_ref, dzxd_ref, dcw_ref, dcb_ref, ddtb_ref, dalog_ref, ddsk_ref, dg_ref, ds_sc, next_sc):
        i = pl.program_id(0)

        @pl.when(i == 0)
        def _():
            ds_sc[...] = jnp.zeros_like(ds_sc)
            next_sc[...] = jnp.zeros_like(next_sc)
            for r_ in (dcw_ref, dcb_ref, ddtb_ref, dalog_ref, ddsk_ref, dg_ref):
                r_[...] = jnp.zeros_like(r_)

        xc = xc_ref[...]
        sg, xa, dt, a, acs = _ssd_front(xc, dtr_ref, dtb_ref, alog_ref)
        dt_x, dsk_x, e_x, r_x, xs, xd, gm = _ssd_wide(xa, dt, acs, dsk_ref[...], ex_ref[...])
        acs_t = acs.T
        tot = acs[CHUNK - 1:CHUNK, :]
        groups = _group_mats(xa)
        lo = lax.broadcasted_iota(jnp.int32, (CHUNK, PAIR_W), 1) < SSM_HEAD_DIM
        row_lo = lax.broadcasted_iota(jnp.int32, (PAIR_W, 1), 0) < SSM_HEAD_DIM
        pairs, zs, yds = [], [], []
        for j in range(PAIRS):
            cmb, _, cb = groups[j // PG]
            ps = slice(j * PAIR_W, (j + 1) * PAIR_W)
            pairs.append(_ssd_pair(j, acs, acs_t, cb))
            (_, m0), (_, m1) = pairs[j]
            zs.append(_dot_nt(cmb, sprev_ref[0, j].astype(bf16)))
            yds.append(_dot(jnp.concatenate([m0, m1], axis=1).astype(bf16), _pair_stack(xd[:, ps], lo).astype(bf16)))
        zf = jnp.concatenate(zs, axis=1)
        y = jnp.concatenate(yds, axis=1) + e_x * zf + xs * dsk_x
        _, gn_vjp = jax.vjp(_gated_norm, y, z_ref[...].astype(f32), g_ref[...])
        dy, dz, dg = gn_vjp(dyb_ref[...].astype(f32))
        dg_ref[...] += dg
        dzxd_ref[:, :SSM_WIDTH] = dz.astype(bf16)

        lane = lax.broadcasted_iota(jnp.int32, (1, LANES), 1)
        sub = lax.broadcasted_iota(jnp.int32, (LANES, 1), 0)
        dacs = jnp.zeros((CHUNK, LANES), f32)
        dacs_r = jnp.zeros((LANES, CHUNK), f32)
        dtot = jnp.zeros((1, LANES), f32)
        dcb = [jnp.zeros((CHUNK, CHUNK), f32) for _ in range(SSM_GROUPS)]
        dcm = [jnp.zeros((CHUNK, N), f32) for _ in range(SSM_GROUPS)]
        dbm = [jnp.zeros((CHUNK, N), f32) for _ in range(SSM_GROUPS)]
        dxds, dgms = [], []
        for j in range(PAIRS):
            g = j // PG
            cmb, bmb, _ = groups[g]
            ps = slice(j * PAIR_W, (j + 1) * PAIR_W)
            (dk0, m0), (dk1, m1) = pairs[j]
            oh0, oh1 = (lane == 2 * j).astype(f32), (lane == 2 * j + 1).astype(f32)
            dyp = dy[:, ps]
            dy2 = _pair_stack(dyp, lo).astype(bf16)
            dm2 = _dot_nt(dy2, xd[:, ps].astype(bf16))
            m2 = jnp.concatenate([m0, m1], axis=0)
            dxds.append(_dot_tn(m2.astype(bf16), dy2))
            w2 = dm2 * m2
            rs = jnp.sum(w2, axis=1, keepdims=True)
            dacs = dacs + rs[:CHUNK] * oh0 + rs[CHUNK:] * oh1
            dacs_r = dacs_r - ((sub == 2 * j).astype(f32) * jnp.sum(w2[:CHUNK], axis=0, keepdims=True)
                               + (sub == 2 * j + 1).astype(f32) * jnp.sum(w2[CHUNK:], axis=0, keepdims=True))
            dcb[g] = dcb[g] + dm2[:CHUNK] * dk0 + dm2[CHUNK:] * dk1
            sp = sprev_ref[0, j]
            dzb = (dyp * e_x[:, ps]).astype(bf16)
            dcm[g] = dcm[g] + _dot(dzb, sp.astype(bf16))
            dsn = ds_sc[j]
            dsnb = dsn.astype(bf16)
            et = _pair_col(row_lo, tot, j)
            rr = jnp.sum(dsn * sp, axis=1, keepdims=True) * et
            dtot = dtot + jnp.sum(rr[:SSM_HEAD_DIM]) * oh0 + jnp.sum(rr[SSM_HEAD_DIM:]) * oh1
            dgms.append(_dot_nt(bmb, dsnb))
            dbm[g] = dbm[g] + _dot(gm[:, ps].astype(bf16), dsnb)
            ds_sc[j] = _dot_tn(dzb, cmb) + et * dsn
        dgm = jnp.concatenate(dgms, axis=1)
        dxd = jnp.concatenate(dxds, axis=1) + dgm * r_x
        dr = dgm * gm
        red = _dot_sel(jnp.concatenate([dy * e_x * zf - dr, dr, dxd * xs, dy * xs], axis=0), rd_ref[...], REDUCE_SPLIT)
        rowi = lax.broadcasted_iota(jnp.int32, (CHUNK, 1), 0)
        dtot = dtot + jnp.sum(red[CHUNK:2 * CHUNK], axis=0, keepdims=True)
        dacs = dacs + red[:CHUNK] + dacs_r.T + jnp.where(rowi == CHUNK - 1, dtot, 0.0)
        r2 = lax.broadcasted_iota(jnp.int32, (CHUNK, CHUNK), 0)
        c2 = lax.broadcasted_iota(jnp.int32, (CHUNK, CHUNK), 1)
        dadt = jnp.dot((c2 >= r2).astype(f32), dacs, preferred_element_type=f32, precision=HIGHEST)
        ddt = red[2 * CHUNK:3 * CHUNK] + dadt * a
        dalog_ref[...] += jnp.sum(dadt * dt, axis=0, keepdims=True) * a
        ddsk_ref[...] += jnp.sum(red[3 * CHUNK:], axis=0, keepdims=True)
        ddtr = jnp.where(lane < H, ddt * _sigmoid(dtr_ref[...] + dtb_ref[...]), 0.0)
        ddtb_ref[...] += jnp.sum(ddtr, axis=0, keepdims=True)
        dzxd_ref[:, SSM_WIDTH + CONV_DIM:] = ddtr.astype(bf16)
        dxa_bm, dxa_cm = [], []
        for g in range(SSM_GROUPS):
            cmb, bmb, _ = groups[g]
            dcbb = dcb[g].astype(bf16)
            dxa_bm.append(dbm[g] + _dot_tn(dcbb, cmb))
            dxa_cm.append(dcm[g] + _dot(dcbb, bmb))
        dxc = jnp.concatenate([dy * dsk_x + dxd * dt_x] + dxa_bm + dxa_cm, axis=1) * (sg * (1.0 + xc * (1.0 - sg)))
        ext = jnp.concatenate([dxc, next_sc[...]], axis=0)
        xin = xbc_ref[...].astype(f32)
        dxbc = cw_ref[SSM_CONV - 1:SSM_CONV, :] * dxc
        dcw = [jnp.sum(dxc * xin, axis=0, keepdims=True)]
        for s in range(1, SSM_CONV):
            later = _sel_dot(_shift_mat(CHUNK, CHUNK + HALO, s), ext, 2)
            dxbc = dxbc + cw_ref[SSM_CONV - 1 - s:SSM_CONV - s, :] * later
            dcw.insert(0, jnp.sum(later * xin, axis=0, keepdims=True))
        dzxd_ref[:, SSM_WIDTH:SSM_WIDTH + CONV_DIM] = dxbc.astype(bf16)
        dcw_ref[...] += jnp.concatenate(dcw, axis=0)
        dcb_ref[...] += jnp.sum(dxc, axis=0, keepdims=True)
        next_sc[...] = dxc[0:HALO, :]

    params = [conv_w, conv_b, dt_bias, a_log, d_skip, ssm_norm]
    mats = list(_head_mats())

    def rev(ncols):
        return pl.BlockSpec((CHUNK, ncols), lambda i: (nc - 1 - i, 0))

    in_specs = ([rev(CONV_DIM), rev(CONV_DIM), rev(SSM_WIDTH), rev(LANES),
                 pl.BlockSpec((1, PAIRS, PAIR_W, N), lambda i: (nc - 1 - i, 0, 0, 0)), rev(SSM_WIDTH)]
                + [_full(x.shape) for x in params + mats])
    return pl.pallas_call(
        body, name="ssd_bwd", grid=(nc,), in_specs=in_specs,
        out_specs=[rev(ZXD)] + [_acc(x.shape) for x in params],
        out_shape=[jax.ShapeDtypeStruct((T, ZXD), bf16)] + [jax.ShapeDtypeStruct(x.shape, f32) for x in params],
        scratch_shapes=[pltpu.VMEM((PAIRS, PAIR_W, N), f32), pltpu.VMEM((HALO, CONV_DIM), f32)],
        compiler_params=_params(("arbitrary",)))(xbc, xc, z, dtr, sprev, dyb, *params, *mats)


def _tail(h3, p, tgt, gp, wpg, bpg, wpp, gf, tm=512):
    T, D = h3.shape
    tm = min(tm, T)

    def head(gpre, pp, h, gf_, t):
        gate = _sigmoid(gpre)
        y = _rms(h + gate * pp, gf_)
        err = y - t
        return 0.5 * jnp.sum(jnp.mean(err * err, axis=-1))

    def body(h_ref, p_ref, t_ref, gp_ref, wpg_ref, bpg_ref, wpp_ref, gf_ref,
             dh_ref, loss_ref, dgp_ref, dwpg_ref, dbpg_ref, dwpp_ref, dgf_ref):
        @pl.when(pl.program_id(0) == 0)
        def _():
            for r in (loss_ref, dgp_ref, dwpg_ref, dbpg_ref, dwpp_ref, dgf_ref):
                r[...] = jnp.zeros_like(r)

        h = h_ref[...]
        npf, np_vjp = jax.vjp(_rms, h, gp_ref[...])
        npb = npf.astype(bf16)
        pb = p_ref[...].astype(bf16)
        gpre = _dot(npb, wpg_ref[...]) + bpg_ref[...]
        kp, _, cp = wpp_ref.shape
        pp = jnp.concatenate([_dot(pb, wpp_ref[k]) for k in range(kp)], axis=1)
        loss, head_vjp = jax.vjp(head, gpre, pp, h, gf_ref[...], t_ref[...])
        dgpre, dpp, dh_a, dgf, _ = head_vjp(jnp.ones((), f32))
        loss_ref[...] += loss
        dgf_ref[...] += dgf
        dbpg_ref[...] += jnp.sum(dgpre, axis=0, keepdims=True)
        dgb = dgpre.astype(bf16)
        dwpg_ref[...] += _dot_tn(npb, dgb)
        dppb = dpp.astype(bf16)
        for k in range(kp):
            dwpp_ref[k] += _dot_tn(pb, dppb[:, k * cp:(k + 1) * cp])
        dh_b, dgp = np_vjp(_dot_nt(dgb, wpg_ref[...]))
        dgp_ref[...] += dgp
        dh_ref[...] = dh_a + dh_b

    ins = [h3, p, tgt, gp, wpg, bpg, wpp, gf]
    in_specs = [_rows(tm, D), _rows(tm, p.shape[1]), _rows(tm, D)] + [_full(x.shape) for x in ins[3:]]
    acc_shapes = [(1, LANES), gp.shape, wpg.shape, bpg.shape, wpp.shape, gf.shape]
    return pl.pallas_call(
        body, name="tail", grid=(T // tm,), in_specs=in_specs,
        out_specs=[_rows(tm, D)] + [_acc(s) for s in acc_shapes],
        out_shape=[jax.ShapeDtypeStruct((T, D), f32)] + [jax.ShapeDtypeStruct(s, f32) for s in acc_shapes],
        compiler_params=_params(("arbitrary",)))(*ins)


def _adamw(name, w, g, m, v, tr=256):
    R, C = w.shape
    tr = _row_tile(R, tr)

    def body(w_ref, g_ref, m_ref, v_ref, d_ref, mo_ref, vo_ref):
        g_ = g_ref[...]
        m_ = ADAM_B1 * m_ref[...] + (1.0 - ADAM_B1) * g_
        v_ = ADAM_B2 * v_ref[...] + (1.0 - ADAM_B2) * jnp.square(g_)
        m_hat = m_ / (1.0 - ADAM_B1 ** ADAM_STEP)
        v_hat = v_ / (1.0 - ADAM_B2 ** ADAM_STEP)
        d_ref[...] = -ADAM_LR * (m_hat / (jnp.sqrt(v_hat) + ADAM_EPS) + ADAM_WD * w_ref[...])
        mo_ref[...] = m_
        vo_ref[...] = v_

    spec = pl.BlockSpec((tr, C), lambda i: (i, 0))
    return pl.pallas_call(body, name=name, grid=(R // tr,), in_specs=[spec] * 4, out_specs=[spec] * 3,
                          out_shape=[jax.ShapeDtypeStruct((R, C), f32)] * 3,
                          compiler_params=_params(("parallel",)))(w, g, m, v)


HBM = pl.BlockSpec(memory_space=pltpu.HBM)


def _me():
    return lax.axis_index("x"), lax.axis_index("y"), lax.axis_index("c")


def _other_chips(x, y):
    return [(1 - x, y), (x, 1 - y), (1 - x, 1 - y)]


def _remote(src, dst, send_sem, recv_sem, dev):
    return pltpu.make_async_remote_copy(src_ref=src, dst_ref=dst, send_sem=send_sem, recv_sem=recv_sem,
                                        device_id=dev, device_id_type=MESH)


def _sems(n):
    return [pltpu.SemaphoreType.DMA((n,)), pltpu.SemaphoreType.DMA((n,))]


def _gather_weights(shards, split):
    n = len(shards)

    def body(*refs):
        ins, outs = refs[:n], refs[n:2 * n]
        own_send, own_recv, ici_send, ici_recv, d2d_send, d2d_recv = refs[2 * n:]
        x, y, c = _me()
        my_chip = 2 * x + y
        sibling = (x, y, 1 - c)
        chips = _other_chips(x, y)

        def rows(i, half):
            hr = shards[i].shape[0] // 2
            return pl.ds(half * hr, hr) if split[i] else pl.ds(0, shards[i].shape[0])

        sends = []
        for i in range(n):
            for j, chip in enumerate(chips):
                cp = _remote(ins[i].at[rows(i, c)], outs[i].at[my_chip, rows(i, c)],
                             ici_send.at[3 * i + j], ici_recv.at[3 * i + j], (*chip, c))
                cp.start()
                sends.append(cp)
            cp = _remote(ins[i], outs[i].at[my_chip], own_send.at[i], own_recv.at[i], sibling)
            cp.start()
            sends.append(cp)
        for i in range(n):
            for j, chip in enumerate(chips):
                s = 3 * i + j
                land = outs[i].at[2 * chip[0] + chip[1], rows(i, c)]
                _remote(land, land, ici_send.at[s], ici_recv.at[s], (*chip, c)).wait_recv()
                if split[i]:
                    cp = _remote(land, land, d2d_send.at[s], d2d_recv.at[s], sibling)
                    cp.start()
                    sends.append(cp)
        for i in range(n):
            _remote(ins[i], outs[i].at[my_chip], own_send.at[i], own_recv.at[i], sibling).wait_recv()
            if split[i]:
                for j, chip in enumerate(chips):
                    s = 3 * i + j
                    land = outs[i].at[2 * chip[0] + chip[1], rows(i, 1 - c)]
                    _remote(land, land, d2d_send.at[s], d2d_recv.at[s], sibling).wait_recv()
        for cp in sends:
            cp.wait_send()

    return pl.pallas_call(
        body, name="gather_weights", out_shape=[jax.ShapeDtypeStruct((N_CHIPS,) + s.shape, s.dtype) for s in shards],
        in_specs=[HBM] * n, out_specs=[HBM] * n,
        scratch_shapes=_sems(n) + _sems(3 * n) + _sems(3 * n))(*shards)


def _swap_halves(name, grads):
    n = len(grads)

    def body(*refs):
        ins, outs, send, recv = refs[:n], refs[n:2 * n], refs[2 * n], refs[2 * n + 1]
        x, y, c = _me()
        copies = []
        for i in range(n):
            hr = grads[i].shape[1] // 2
            cp = _remote(ins[i].at[:, pl.ds((1 - c) * hr, hr), :], outs[i], send.at[i], recv.at[i], (x, y, 1 - c))
            cp.start()
            copies.append(cp)
        for cp in copies:
            cp.wait()

    return pl.pallas_call(
        body, name=name,
        out_shape=[jax.ShapeDtypeStruct((g.shape[0], g.shape[1] // 2, g.shape[2]), g.dtype) for g in grads],
        in_specs=[HBM] * n, out_specs=[HBM] * n, scratch_shapes=_sems(n))(*grads)


def _add_halves(name, grads, other, c_idx, th=592):
    K, R, C = grads.shape
    H = R // 2
    th = _row_tile(H, th, 16)
    nb = H // th

    def body(c_ref, g_ref, o_ref, out_ref):
        out_ref[...] = (g_ref[...].astype(f32) + o_ref[...].astype(f32)).astype(bf16)

    grid_spec = pltpu.PrefetchScalarGridSpec(
        num_scalar_prefetch=1, grid=(nb,),
        in_specs=[pl.BlockSpec((K, th, C), lambda i, c: (0, c[0] * nb + i, 0)),
                  pl.BlockSpec((K, th, C), lambda i, c: (0, i, 0))],
        out_specs=pl.BlockSpec((K, th, C), lambda i, c: (0, i, 0)))
    return pl.pallas_call(body, name=name, grid_spec=grid_spec,
                          out_shape=jax.ShapeDtypeStruct((K, H, C), bf16),
                          compiler_params=_params(("parallel",)))(c_idx, grads, other)


SEM = pl.BlockSpec(memory_space=pltpu.SEMAPHORE)
ANY = pl.BlockSpec(memory_space=pl.ANY)
EFFECT = pltpu.SideEffectType.DATAFLOW_SIDE_EFFECTING


def _copies_start(name, srcs, land_shapes, n_copies, make_copies, after):
    ns, nl = len(srcs), len(land_shapes)
    lands = [lax.empty(s.shape, s.dtype) for s in land_shapes]

    def body(*refs):
        src_refs, land_refs = refs[:ns], refs[ns:ns + nl]
        send, recv, token = refs[ns + nl + 1], refs[ns + nl + 2], refs[-1]
        for cp in make_copies(src_refs, land_refs, send, recv):
            cp.start()
        token[...] = jnp.zeros_like(token)

    buffers = list(srcs) + lands
    out = pl.pallas_call(
        body, name=name,
        out_shape=(pltpu.SemaphoreType.DMA((n_copies,)), pltpu.SemaphoreType.DMA((n_copies,)),
                   *[pltpu.HBM(b.shape, b.dtype) for b in buffers], jax.ShapeDtypeStruct((8, LANES), f32)),
        in_specs=[HBM] * (ns + nl) + [ANY],
        out_specs=(SEM, SEM, *[HBM] * (ns + nl), pl.BlockSpec(memory_space=pltpu.VMEM)),
        input_output_aliases={i: 2 + i for i in range(ns + nl)},
        compiler_params=pltpu.CompilerParams(has_side_effects=EFFECT),
    )(*[pltpu.with_memory_space_constraint(b, pltpu.HBM) for b in buffers], after)
    return out[0], out[1], list(out[2:2 + ns]), list(out[2 + ns:2 + ns + nl]), out[-1]


def _copies_wait(name, started, make_copies, after):
    send, recv, srcs, lands, _ = started
    ns, nl = len(srcs), len(lands)
    after = list(after)

    def body(*refs):
        src_refs, land_refs = refs[:ns], refs[ns:ns + nl]
        for cp in make_copies(src_refs, land_refs, refs[ns + nl], refs[ns + nl + 1]):
            cp.wait_send()
            cp.wait_recv()

    buffers = list(srcs) + list(lands)
    out = pl.pallas_call(
        body, name=name, out_shape=tuple(pltpu.HBM(b.shape, b.dtype) for b in buffers),
        in_specs=[HBM] * (ns + nl) + [SEM, SEM] + [ANY] * len(after), out_specs=tuple([HBM] * (ns + nl)),
        input_output_aliases={i: i for i in range(ns + nl)},
        compiler_params=pltpu.CompilerParams(has_side_effects=EFFECT),
    )(*buffers, send, recv, *after)
    return list(out[:ns]), list(out[ns:])


def _gather_copies(src_refs, land_refs, send, recv):
    x, y, c = _me()
    my_chip = 2 * x + y
    peers = [(*chip, c) for chip in _other_chips(x, y)] + [(x, y, 1 - c)]
    return [_remote(src_refs[i], land_refs[i].at[my_chip], send.at[4 * i + j], recv.at[4 * i + j], peer)
            for i in range(len(src_refs)) for j, peer in enumerate(peers)]


def _partial_copies(src_refs, land_refs, send, recv):
    x, y, c = _me()
    return [_remote(src_refs[i].at[2 * chip[0] + chip[1]], land_refs[i].at[j], send.at[3 * i + j], recv.at[3 * i + j], (*chip, c))
            for i in range(len(src_refs)) for j, chip in enumerate(_other_chips(x, y))]


def _small_copies(src_refs, land_refs, send, recv):
    x, y, c = _me()
    return [_remote(src_refs[0], land_refs[0].at[k - 1], send.at[k - 1], recv.at[k - 1], (x ^ (k >> 2), y ^ ((k >> 1) & 1), c ^ (k & 1)))
            for k in range(1, N_DEV)]


def _sum_small(own, slots, dev_idx):
    R, C = own.shape

    def body(dev_ref, own_ref, s_ref, o_ref):
        me = dev_ref[0]
        acc = jnp.zeros((R, C), f32)
        for d in range(N_DEV):
            k = me ^ d
            acc = acc + jnp.where(k == 0, own_ref[...], s_ref[jnp.maximum(k - 1, 0)])
        o_ref[...] = acc

    grid_spec = pltpu.PrefetchScalarGridSpec(
        num_scalar_prefetch=1, grid=(1,),
        in_specs=[pl.BlockSpec((R, C), lambda i, dev: (0, 0)), pl.BlockSpec((N_DEV - 1, R, C), lambda i, dev: (0, 0, 0))],
        out_specs=pl.BlockSpec((R, C), lambda i, dev: (0, 0)))
    return pl.pallas_call(body, name="sum_small", grid_spec=grid_spec, out_shape=jax.ShapeDtypeStruct((R, C), f32),
                          compiler_params=_params(("arbitrary",)))(dev_idx, own, slots)


def _sum_partials(name, part, recv, chip_idx, th=592):
    K, H, C = part.shape
    th = _row_tile(H, th, 16)

    def body(chip_ref, p_ref, r_ref, o_ref):
        acc = p_ref[...].astype(f32)
        for j in range(3):
            acc = acc + r_ref[j].astype(f32)
        o_ref[...] = acc

    grid_spec = pltpu.PrefetchScalarGridSpec(
        num_scalar_prefetch=1, grid=(H // th,),
        in_specs=[pl.BlockSpec((None, th, C), lambda i, chip: (chip[0], i, 0)),
                  pl.BlockSpec((3, th, C), lambda i, chip: (0, i, 0))],
        out_specs=pl.BlockSpec((th, C), lambda i, chip: (i, 0)))
    return pl.pallas_call(body, name=name, grid_spec=grid_spec, out_shape=jax.ShapeDtypeStruct((H, C), f32),
                          compiler_params=_params(("parallel",)))(chip_idx, part, recv)


def _share_halves(name, halves):
    n = len(halves)

    def body(*refs):
        ins, outs, send, recv = refs[:n], refs[n:2 * n], refs[2 * n], refs[2 * n + 1]
        x, y, c = _me()
        copies = []
        for i in range(n):
            cp = _remote(ins[i], outs[i], send.at[i], recv.at[i], (x, y, 1 - c))
            cp.start()
            copies.append(cp)
        for cp in copies:
            cp.wait()

    return pl.pallas_call(
        body, name=name, out_shape=[jax.ShapeDtypeStruct(h.shape, h.dtype) for h in halves],
        in_specs=[HBM] * n, out_specs=[HBM] * n, scratch_shapes=_sems(n))(*halves)


def _adamw_big(name, w, g_mine, g_theirs, m, v, c_idx, tr=320):
    R, C = w.shape
    H = R // 2
    tr = _row_tile(H, tr)
    nb = H // tr

    def body(c_ref, w_ref, gm_ref, gt_ref, m_ref, v_ref, g_ref, d_ref, mo_ref, vo_ref):
        g_ = jnp.where(pl.program_id(0) // nb == c_ref[0], gm_ref[...], gt_ref[...])
        g_ref[...] = g_
        m_ = ADAM_B1 * m_ref[...] + (1.0 - ADAM_B1) * g_
        v_ = ADAM_B2 * v_ref[...] + (1.0 - ADAM_B2) * jnp.square(g_)
        m_hat = m_ / (1.0 - ADAM_B1 ** ADAM_STEP)
        v_hat = v_ / (1.0 - ADAM_B2 ** ADAM_STEP)
        d_ref[...] = -ADAM_LR * (m_hat / (jnp.sqrt(v_hat) + ADAM_EPS) + ADAM_WD * w_ref[...])
        mo_ref[...] = m_
        vo_ref[...] = v_

    full = pl.BlockSpec((tr, C), lambda i, c: (i, 0))
    half = pl.BlockSpec((tr, C), lambda i, c: (i % nb, 0))
    grid_spec = pltpu.PrefetchScalarGridSpec(num_scalar_prefetch=1, grid=(2 * nb,),
                                             in_specs=[full, half, half, full, full], out_specs=[full] * 4)
    return pl.pallas_call(body, name=name, grid_spec=grid_spec, out_shape=[jax.ShapeDtypeStruct((R, C), f32)] * 4,
                          compiler_params=_params(("parallel",)))(c_idx, w, g_mine, g_theirs, m, v)


BIG = ("ffn1_w_gate", "ffn1_w_up", "ffn1_w_down", "w_in", "w_out", "ffn2_w_gate", "ffn2_w_up", "ffn2_w_down",
       "ple_w_gate", "ple_w_proj")


SMALL = ("ffn1_norm", "mix_norm", "gm_ln_g", "gm_ln_b", "gm_w_s", "gm_b_s", "gm_out_norm", "conv_b", "dt_bias", "a_log",
         "d_skip", "ssm_norm", "ffn2_norm", "ple_norm", "ple_b_gate", "final_norm")
SMALL_C = 1024


def _pack_small(vals):
    parts = []
    for v in vals:
        f = v.astype(f32).reshape(-1)
        parts.append(jnp.pad(f, (0, -f.shape[0] % SMALL_C)))
    flat = jnp.concatenate(parts)
    rows = flat.shape[0] // SMALL_C
    return jnp.pad(flat, (0, (-rows % 8) * SMALL_C)).reshape(-1, SMALL_C)


def _unpack_small(pack, shapes):
    flat = pack.reshape(-1)
    out, off = [], 0
    for s in shapes:
        n = 1
        for d in s:
            n *= d
        out.append(flat[off:off + n].reshape(s))
        off += n + (-n % SMALL_C)
    return out


def _pad_lanes(v):
    return jnp.pad(v, ((0, 0), (0, LANES - v.shape[1])))


def _pad_rows(a):
    pad = [(0, 0)] * a.ndim
    pad[-2] = (0, -a.shape[-2] % ROW_PAD)
    return jnp.pad(a, pad) if pad[-2][1] else a


FETCH = (("ffn1_w_gate", "ffn1_w_up", "ffn1_w_down"), ("w_in", "conv_w", "w_out"),
         ("ffn2_w_gate", "ffn2_w_up", "ffn2_w_down", "ple_w_gate", "ple_w_proj"))
TRANSPOSED = ("ffn1_w_gate", "ffn1_w_up", "ffn2_w_gate", "ffn2_w_up", "w_in")
ROW_PAD = 32
DONE = (("ffn2_w_gate", "ffn2_w_up", "ffn2_w_down", "w_out", "ple_w_gate", "ple_w_proj"), ("w_in",),
        ("ffn1_w_gate", "ffn1_w_up", "ffn1_w_down"))


def _local_step(x, p, tgt, fetch, S, on_grads):
    G = GM_WIDTH
    K = N_CHIPS
    b_st = S["gm_b_s"][0].T
    w_s = S["gm_w_s"][0]
    dtb, alog, dsk = _pad_lanes(S["dt_bias"]), _pad_lanes(S["a_log"]), _pad_lanes(S["d_skip"])
    gfin = S["final_norm"].reshape(1, -1)

    def rows(a):
        return a.reshape(-1, D_MODEL)

    def shards(a):
        return a.reshape(K, -1, D_MODEL)

    wg1, wu1, wd1 = [rows(a) for a in fetch(0, None)]
    h1, n1, a1, b1 = _ffn_fwd("ffn1_fwd", x, S["ffn1_norm"], wg1, wu1, wd1)
    w_in4, cw4, wo4 = fetch(1, h1)
    w_in = w_in4.reshape(IN_PROJ, D_MODEL)
    w_uv = w_in[:2 * G]
    w_zxd = jnp.pad(w_in[2 * G:], ((0, ZXD - (IN_PROJ - 2 * G)), (0, 0)))
    conv_w = jnp.transpose(cw4, (1, 0, 2)).reshape(SSM_CONV, CONV_DIM)
    wo = wo4.reshape(-1, D_MODEL)
    n2, uv, z, xbc, dtr, ya = _mix_fwd(h1, S["mix_norm"], w_uv, w_zxd, S["gm_ln_g"], S["gm_ln_b"], w_s, b_st, S["gm_out_norm"])
    yb, xc, sprev = _ssd_fwd(xbc, z, dtr, conv_w, S["conv_b"], dtb, alog, dsk, S["ssm_norm"])
    wg2, wu2, wd2, wpg4, wpp4 = fetch(2, yb)
    wg2, wu2, wd2 = rows(wg2), rows(wu2), rows(wd2)
    h2, h3, n3, a2, b2 = _ffn_fwd("ffn2_fwd", h1, S["ffn2_norm"], wg2, wu2, wd2, pre=(ya, yb, wo))
    dh3, loss, dgp, dwpg, dbpg, dwpp, dgf = _tail(h3, p, tgt, S["ple_norm"], wpg4.reshape(-1, D_MODEL), S["ple_b_gate"], wpp4, gfin)
    dh2, da2, db2, hm2, dg_ffn2, dya, dyb = _ffn_bwd("ffn2_bwd", dh3, h2, S["ffn2_norm"], a2, b2, wg2, wu2, wd2, wo=wo, ga=G)
    dw_out = jnp.concatenate([_matmul_tn("dw_out_a", ya, dh2), _matmul_tn("dw_out_b", yb, dh2)], axis=0).reshape(wo4.shape)
    zero = on_grads(0, [shards(_matmul_tn("dw_ffn2_gate", da2, n3)), shards(_matmul_tn("dw_ffn2_up", db2, n3)),
                        shards(_matmul_tn("dw_ffn2_down", hm2, dh3, scale=0.5)), dw_out,
                        dwpg.astype(bf16).reshape(wpg4.shape), dwpp.astype(bf16)])
    dzxd, dcw, dcb, ddtb, dalog, ddsk, dgssm = _ssd_bwd(xbc, xc, z, dtr, sprev, dyb, conv_w, S["conv_b"], dtb, alog, dsk,
                                                        S["ssm_norm"] + zero)
    dh1, duv, dg_mix, dlng, dlnb, dws, dbst, dgout = _mix_bwd(dh2, h1, S["mix_norm"], uv, dya, dzxd, w_uv, w_zxd, S["gm_ln_g"],
                                                              S["gm_ln_b"], w_s, b_st, S["gm_out_norm"])
    dw_in = jnp.concatenate([_matmul_tn("dw_in_uv", duv, n2), _matmul_tn("dw_in_zxd", dzxd, n2)[:IN_PROJ - 2 * G]], axis=0)
    zero = on_grads(1, [dw_in.reshape(w_in4.shape)])
    dx, da1, db1, hm1, dg_ffn1 = _ffn_bwd("ffn1_bwd", dh1, x, S["ffn1_norm"] + zero, a1, b1, wg1, wu1, wd1)
    zero = on_grads(2, [shards(_matmul_tn("dw_ffn1_gate", da1, n1)), shards(_matmul_tn("dw_ffn1_up", db1, n1)),
                        shards(_matmul_tn("dw_ffn1_down", hm1, dh1, scale=0.5))])
    loss = loss + zero
    nh = SSM_HEADS
    gS = {"ffn1_norm": dg_ffn1, "mix_norm": dg_mix, "gm_ln_g": dlng, "gm_ln_b": dlnb, "gm_w_s": dws[None], "gm_b_s": dbst.T[None],
          "gm_out_norm": dgout, "conv_b": dcb, "dt_bias": ddtb[:, :nh], "a_log": dalog[:, :nh], "d_skip": ddsk[:, :nh],
          "ssm_norm": dgssm, "ffn2_norm": dg_ffn2, "ple_norm": dgp, "ple_b_gate": dbpg, "final_norm": dgf.reshape(-1)}
    return loss, dx, dcw, gS


_WEIGHTS = ("ffn1_norm", "ffn1_w_gate", "ffn1_w_up", "ffn1_w_down", "mix_norm", "w_in", "gm_ln_g", "gm_ln_b", "gm_w_s", "gm_b_s",
            "gm_out_norm", "conv_w", "conv_b", "dt_bias", "a_log", "d_skip", "ssm_norm", "w_out", "ffn2_norm", "ffn2_w_gate",
            "ffn2_w_up", "ffn2_w_down", "ple_norm", "ple_w_gate", "ple_b_gate", "ple_w_proj", "final_norm")
_BIG_NAMES = BIG


def kernel(x, p, ffn1_norm, ffn1_w_gate, ffn1_w_up, ffn1_w_down, mix_norm, w_in, gm_ln_g, gm_ln_b, gm_w_s, gm_b_s, gm_out_norm, conv_w, conv_b, dt_bias, a_log, d_skip, ssm_norm, w_out, ffn2_norm, ffn2_w_gate, ffn2_w_up, ffn2_w_down, ple_norm, ple_w_gate, ple_b_gate, ple_w_proj, final_norm, loss_target, m_ffn1_norm, m_ffn1_w_gate, m_ffn1_w_up, m_ffn1_w_down, m_mix_norm, m_w_in, m_gm_ln_g, m_gm_ln_b, m_gm_w_s, m_gm_b_s, m_gm_out_norm, m_conv_w, m_conv_b, m_dt_bias, m_a_log, m_d_skip, m_ssm_norm, m_w_out, m_ffn2_norm, m_ffn2_w_gate, m_ffn2_w_up, m_ffn2_w_down, m_ple_norm, m_ple_w_gate, m_ple_b_gate, m_ple_w_proj, m_final_norm, v_ffn1_norm, v_ffn1_w_gate, v_ffn1_w_up, v_ffn1_w_down, v_mix_norm, v_w_in, v_gm_ln_g, v_gm_ln_b, v_gm_w_s, v_gm_b_s, v_gm_out_norm, v_conv_w, v_conv_b, v_dt_bias, v_a_log, v_d_skip, v_ssm_norm, v_w_out, v_ffn2_norm, v_ffn2_w_gate, v_ffn2_w_up, v_ffn2_w_down, v_ple_norm, v_ple_w_gate, v_ple_b_gate, v_ple_w_proj, v_final_norm):
    given = dict(locals())
    w = {n: given[n] for n in _WEIGHTS}
    m = {n: given["m_" + n] for n in _WEIGHTS}
    v = {n: given["v_" + n] for n in _WEIGHTS}

    c_idx = lax.axis_index("c").astype(jnp.int32).reshape(1)
    chip = 2 * lax.axis_index("x") + lax.axis_index("y")
    chip_idx = chip.astype(jnp.int32).reshape(1)

    shard = {n: (jnp.swapaxes(w[n][0], 0, 1) if n in TRANSPOSED else w[n][0]).astype(bf16) for n in BIG}
    shard["conv_w"] = w["conv_w"][0]
    first = _gather_weights([shard[n] for n in FETCH[0]], [True] * len(FETCH[0]))
    fetching, after = [], first[-1]
    for k in (1, 2):
        srcs = [shard[n] for n in FETCH[k]]
        lands = [jax.ShapeDtypeStruct((N_CHIPS,) + s.shape, s.dtype) for s in srcs]
        fetching.append(_copies_start("gather%d_start" % k, srcs, lands, 4 * len(srcs), _gather_copies, after))
        after = fetching[-1][4]

    def fetch(k, after_):
        return first if k == 0 else _copies_wait("gather%d_wait" % k, fetching[k - 1], _gather_copies, [after_])[1]

    exchanging = []

    def on_grads(k, grads):
        grads = [_pad_rows(g_) for g_ in grads]
        others = _swap_halves("swap%d" % k, grads)
        parts = [_add_halves("add_" + n, g_, o_, c_idx) for n, g_, o_ in zip(DONE[k], grads, others)]
        lands = [jax.ShapeDtypeStruct((3,) + p_.shape[1:], p_.dtype) for p_ in parts]
        exchanging.append(_copies_start("exchange%d_start" % k, parts, lands, 3 * len(parts), _partial_copies, c_idx))
        return exchanging[-1][4][0, 0]

    S = {n: w[n] for n in SMALL}
    S["ffn1_norm"] = S["ffn1_norm"] + after[0, 0]
    loss, dx, dcw, gS = _local_step(x[0], p[0, 0], loss_target[0], fetch, S, on_grads)

    small = _pack_small([gS[n] for n in SMALL] + [dcw, loss[:, :1]])
    small_lands = [jax.ShapeDtypeStruct((N_DEV - 1,) + small.shape, small.dtype)]
    small_st = _copies_start("small_start", [small], small_lands, N_DEV - 1, _small_copies, c_idx)

    g, delta, new_m, new_v = {}, {}, {}, {}
    after = [small_st[4]]
    for k in range(len(DONE)):
        parts, recv = _copies_wait("exchange%d_wait" % k, exchanging[k], _partial_copies, after)
        mine = [_sum_partials("sum_" + n, p_, r_, chip_idx) for n, p_, r_ in zip(DONE[k], parts, recv)]
        theirs = _share_halves("share%d" % k, mine)
        after = []
        for n, gm_, gt_ in zip(DONE[k], mine, theirs):
            flip = (lambda a: jnp.swapaxes(a, 0, 1)) if n in TRANSPOSED else (lambda a: a)
            rows = flip(w[n][0]).shape[0]
            w_, m_, v_ = [_pad_rows(flip(a[n][0])) for a in (w, m, v)]
            outs = _adamw_big("adamw_" + n, w_, gm_, gt_, m_, v_, c_idx)
            g[n], delta[n], new_m[n], new_v[n] = [flip(o[:rows])[None] for o in outs]
            after.append(outs[3])
    (own,), (slots,) = _copies_wait("small_wait", small_st, _small_copies, after)
    dev_idx = (2 * chip + lax.axis_index("c")).astype(jnp.int32).reshape(1)
    small_shapes = [w[n].shape for n in SMALL] + [dcw.shape, (1, 1)]
    small_sum = _unpack_small(_sum_small(own, slots, dev_idx), small_shapes)
    g.update({n: small_sum[i] for i, n in enumerate(SMALL)})
    cshard = w["conv_w"].shape[2]
    g["conv_w"] = lax.dynamic_slice_in_dim(small_sum[len(SMALL)], chip * cshard, cshard, axis=1)[None]
    loss_total = small_sum[len(SMALL) + 1].reshape(())
    sm_names = SMALL + ("conv_w",)
    sm_shapes = [w[n].shape for n in sm_names]
    d_s, m_s, v_s = _adamw("adamw_small", _pack_small([w[n] for n in sm_names]), _pack_small([g[n] for n in sm_names]),
                           _pack_small([m[n] for n in sm_names]), _pack_small([v[n] for n in sm_names]))
    for dst, src in ((delta, d_s), (new_m, m_s), (new_v, v_s)):
        for n, val in zip(sm_names, _unpack_small(src, sm_shapes)):
            dst[n] = val

    return (loss_total, dx[None], *[g[n] for n in _WEIGHTS], *[delta[n] for n in _WEIGHTS],
            *[new_m[n] for n in _WEIGHTS], *[new_v[n] for n in _WEIGHTS])
```

```python
import functools

import jax
import jax.numpy as jnp
from jax import lax
from jax.experimental import pallas as pl
from jax.experimental.pallas import tpu as pltpu

f32 = jnp.float32
bf16 = jnp.bfloat16
MESH = pl.DeviceIdType.MESH
HIGHEST = lax.Precision.HIGHEST

EPS = 1e-6
N_CHIPS = 4
N_DEV = 8
D_MODEL = 1024
D_FF = 2816
D_PLE = 256
GM_WIDTH = 1024
GM_HEADS = 8
CHUNK = 128
SSM_WIDTH = 1024
SSM_HEADS = 16
SSM_HEAD_DIM = 64
SSM_GROUPS = 2
SSM_STATE = 128
SSM_CONV = 4
CONV_DIM = SSM_WIDTH + 2 * SSM_GROUPS * SSM_STATE
IN_PROJ = 2 * GM_WIDTH + SSM_WIDTH + CONV_DIM + SSM_HEADS
LANES = 128
ZXD = SSM_WIDTH + CONV_DIM + LANES

ADAM_LR = 0.001
ADAM_B1 = 0.9
ADAM_B2 = 0.999
ADAM_EPS = 1e-08
ADAM_WD = 0.01
ADAM_STEP = 10

VMEM_LIMIT = 56 * 1024 * 1024


def _dot(a, b):
    return jnp.dot(a, b, preferred_element_type=f32)


def _dot_nt(a, b):
    return lax.dot_general(a, b, (((1,), (1,)), ((), ())), preferred_element_type=f32)


def _dot_tn(a, b):
    return lax.dot_general(a, b, (((0,), (0,)), ((), ())), preferred_element_type=f32)


def _rms(x, g):
    return x * lax.rsqrt(jnp.mean(x * x, axis=-1, keepdims=True) + EPS) * g


def _gelu(x):
    return 0.5 * x * (1.0 + lax.erf(x * 0.7071067811865476))


def _layernorm(x, g, b):
    mu = jnp.mean(x, axis=-1, keepdims=True)
    xc = x - mu
    return xc * lax.rsqrt(jnp.mean(xc * xc, axis=-1, keepdims=True) + EPS) * g + b


def _sigmoid(x):
    return 1.0 / (1.0 + jnp.exp(-x))


def _softplus(x):
    return jnp.maximum(x, 0.0) + jnp.log(1.0 + jnp.exp(-jnp.abs(x)))


def _full(shape):
    nd = len(shape)
    return pl.BlockSpec(shape, lambda *_: (0,) * nd, pipeline_mode=pl.Buffered(1))


def _acc(shape):
    nd = len(shape)
    return pl.BlockSpec(shape, lambda *_: (0,) * nd)


def _rows(tm, ncols):
    return pl.BlockSpec((tm, ncols), lambda i: (i, 0))


def _params(sem):
    return pltpu.CompilerParams(dimension_semantics=sem, vmem_limit_bytes=VMEM_LIMIT)


def _row_tile(rows, target, mult=8):
    best = rows
    for t in range(mult, min(rows, target) + 1, mult):
        if rows % t == 0:
            best = t
    return best if best <= target else rows


def _ffn_fwd(name, h, g, wg, wu, wd, pre=None, tm=256):
    T, D = h.shape
    F = wg.shape[0]
    tm = min(tm, T)

    def body(*refs):
        if pre is None:
            h_ref, g_ref, wg_ref, wu_ref, wd_ref, ho_ref, n_ref, a_ref, b_ref = refs
            hin = h_ref[...]
        else:
            (h_ref, ya_ref, yb_ref, wo_ref, g_ref, wg_ref, wu_ref, wd_ref,
             hi_ref, ho_ref, n_ref, a_ref, b_ref) = refs
            ga = ya_ref.shape[1]
            hin = h_ref[...] + _dot(ya_ref[...], wo_ref[:ga, :]) + _dot(yb_ref[...], wo_ref[ga:, :])
            hi_ref[...] = hin
        n = _rms(hin, g_ref[...]).astype(bf16)
        n_ref[...] = n
        a = _dot_nt(n, wg_ref[...]).astype(bf16)
        b = _dot_nt(n, wu_ref[...]).astype(bf16)
        a_ref[...] = a
        b_ref[...] = b
        af = a.astype(f32)
        hm = (af * _sigmoid(af) * b.astype(f32)).astype(bf16)
        ho_ref[...] = hin + 0.5 * _dot(hm, wd_ref[...])

    ins = [h] + (list(pre) if pre is not None else []) + [g, wg, wu, wd]
    in_specs = [_rows(tm, D)]
    if pre is not None:
        in_specs += [_rows(tm, pre[0].shape[1]), _rows(tm, pre[1].shape[1]), _full(pre[2].shape)]
    in_specs += [_full(g.shape), _full(wg.shape), _full(wu.shape), _full(wd.shape)]
    outs = [jax.ShapeDtypeStruct((T, D), f32), jax.ShapeDtypeStruct((T, D), bf16),
            jax.ShapeDtypeStruct((T, F), bf16), jax.ShapeDtypeStruct((T, F), bf16)]
    out_specs = [_rows(tm, D), _rows(tm, D), _rows(tm, F), _rows(tm, F)]
    if pre is not None:
        outs = [jax.ShapeDtypeStruct((T, D), f32)] + outs
        out_specs = [_rows(tm, D)] + out_specs
    return pl.pallas_call(body, name=name, grid=(T // tm,), in_specs=in_specs, out_specs=out_specs,
                          out_shape=outs, compiler_params=_params(("parallel",)))(*ins)


def _ffn_bwd(name, dh, hin, g, a, b, wg, wu, wd, wo=None, ga=0, tm=256):
    T, D = dh.shape
    F = wg.shape[0]
    tm = min(tm, T)

    def body(*refs):
        if wo is None:
            (dh_ref, hin_ref, g_ref, a_ref, b_ref, wg_ref, wu_ref, wd_ref,
             dhi_ref, da_ref, db_ref, hm_ref, dg_ref) = refs
        else:
            (dh_ref, hin_ref, g_ref, a_ref, b_ref, wg_ref, wu_ref, wd_ref, wo_ref,
             dhi_ref, da_ref, db_ref, hm_ref, dg_ref, dya_ref, dyb_ref) = refs

        @pl.when(pl.program_id(0) == 0)
        def _():
            dg_ref[...] = jnp.zeros_like(dg_ref)

        dh_ = dh_ref[...]
        dhb = (0.5 * dh_).astype(bf16)
        dhm = _dot_nt(dhb, wd_ref[...])
        af = a_ref[...].astype(f32)
        bf = b_ref[...].astype(f32)
        sg = _sigmoid(af)
        sl_ = af * sg
        da = (dhm * bf * (sg * (1.0 + af * (1.0 - sg)))).astype(bf16)
        db = (dhm * sl_).astype(bf16)
        da_ref[...] = da
        db_ref[...] = db
        hm_ref[...] = (sl_ * bf).astype(bf16)
        dn = _dot(da, wg_ref[...]) + _dot(db, wu_ref[...])
        _, vjp = jax.vjp(_rms, hin_ref[...], g_ref[...])
        dx, dg = vjp(dn)
        dhi = dh_ + dx
        dhi_ref[...] = dhi
        dg_ref[...] += dg
        if wo is not None:
            dhib = dhi.astype(bf16)
            dya_ref[...] = _dot_nt(dhib, wo_ref[:ga, :]).astype(bf16)
            dyb_ref[...] = _dot_nt(dhib, wo_ref[ga:, :]).astype(bf16)

    ins = [dh, hin, g, a, b, wg, wu, wd]
    in_specs = [_rows(tm, D), _rows(tm, D), _full(g.shape), _rows(tm, F), _rows(tm, F),
                _full(wg.shape), _full(wu.shape), _full(wd.shape)]
    act = jax.ShapeDtypeStruct((T, F), bf16)
    outs = [jax.ShapeDtypeStruct((T, D), f32), act, act, act, jax.ShapeDtypeStruct(g.shape, f32)]
    out_specs = [_rows(tm, D), _rows(tm, F), _rows(tm, F), _rows(tm, F), _acc(g.shape)]
    if wo is not None:
        gb = wo.shape[0] - ga
        ins += [wo]
        in_specs += [_full(wo.shape)]
        outs += [jax.ShapeDtypeStruct((T, ga), bf16), jax.ShapeDtypeStruct((T, gb), bf16)]
        out_specs += [_rows(tm, ga), _rows(tm, gb)]
    return pl.pallas_call(body, name=name, grid=(T // tm,), in_specs=in_specs, out_specs=out_specs,
                          out_shape=outs, compiler_params=_params(("arbitrary",)))(*ins)


def _matmul_tn(name, a, b, scale=1.0, tk=2048):
    T, M = a.shape
    N = b.shape[1]
    tk = min(tk, T)
    nk = T // tk
    tn = LANES * max(d for d in range(1, N // LANES + 1) if (N // LANES) % d == 0 and (d == 1 or M * d * LANES * 4 <= 6 * 1024 * 1024))

    def body(a_ref, b_ref, o_ref, acc):
        k = pl.program_id(1)

        @pl.when(k == 0)
        def _():
            acc[...] = jnp.zeros_like(acc)

        bb = b_ref[...]
        if scale != 1.0:
            bb = bb * scale
        acc[...] += _dot_tn(a_ref[...].astype(bf16), bb.astype(bf16))

        @pl.when(k == nk - 1)
        def _():
            o_ref[...] = acc[...].astype(bf16)

    return pl.pallas_call(
        body, name=name, grid=(N // tn, nk),
        in_specs=[pl.BlockSpec((tk, M), lambda j, k: (k, 0)), pl.BlockSpec((tk, tn), lambda j, k: (k, j))],
        out_specs=pl.BlockSpec((M, tn), lambda j, k: (0, j)),
        out_shape=jax.ShapeDtypeStruct((M, N), bf16), scratch_shapes=[pltpu.VMEM((M, tn), f32)],
        compiler_params=_params(("parallel", "arbitrary")))(a, b)


def _gm_pre(u, v, ln_g, ln_b):
    return _gelu(u), _layernorm(_gelu(v), ln_g, ln_b)


def _tril_mask():
    r = lax.broadcasted_iota(jnp.int32, (CHUNK, CHUNK), 0)
    c = lax.broadcasted_iota(jnp.int32, (CHUNK, CHUNK), 1)
    return c <= r


def _gm_mix(vnb, ws_ref, bst, mixed_sc, tm):
    mask = _tril_mask()
    for h in range(GM_HEADS):
        wt = jnp.where(mask, ws_ref[h], 0.0).astype(bf16)
        bias = bst[:, h:h + 1]
        for q in range(tm // CHUNK):
            rs = slice(q * CHUNK, (q + 1) * CHUNK)
            cs = slice(h * CHUNK, (h + 1) * CHUNK)
            mixed_sc[rs, cs] = _dot(wt, vnb[rs, cs]) + bias


def _mix_fwd(h1, gmix, w_uv, w_zxd, ln_g, ln_b, w_s, b_st, gout, tm=512):
    T, D = h1.shape
    tm = min(tm, T)
    G = GM_WIDTH

    def body(h_ref, g_ref, wuv_ref, wzxd_ref, lng_ref, lnb_ref, ws_ref, bst_ref, gout_ref,
             n_ref, uv_ref, z_ref, xbc_ref, dt_ref, ya_ref, mixed_sc):
        n = _rms(h_ref[...], g_ref[...]).astype(bf16)
        n_ref[...] = n
        u = _dot_nt(n, wuv_ref[:G, :]).astype(bf16)
        v = _dot_nt(n, wuv_ref[G:, :]).astype(bf16)
        uv_ref[:, :G] = u
        uv_ref[:, G:] = v
        z_ref[...] = _dot_nt(n, wzxd_ref[:SSM_WIDTH, :]).astype(bf16)
        xbc_ref[...] = _dot_nt(n, wzxd_ref[SSM_WIDTH:SSM_WIDTH + CONV_DIM, :]).astype(bf16)
        dt_ref[...] = _dot_nt(n, wzxd_ref[SSM_WIDTH + CONV_DIM:, :])
        ug, vn = _gm_pre(u.astype(f32), v.astype(f32), lng_ref[...], lnb_ref[...])
        _gm_mix(vn.astype(bf16), ws_ref, bst_ref[...], mixed_sc, tm)
        ya_ref[...] = _rms(ug * mixed_sc[...], gout_ref[...]).astype(bf16)

    ins = [h1, gmix, w_uv, w_zxd, ln_g, ln_b, w_s, b_st, gout]
    in_specs = [_rows(tm, D)] + [_full(x.shape) for x in ins[1:]]
    outs = [jax.ShapeDtypeStruct((T, D), bf16), jax.ShapeDtypeStruct((T, 2 * G), bf16),
            jax.ShapeDtypeStruct((T, SSM_WIDTH), bf16), jax.ShapeDtypeStruct((T, CONV_DIM), bf16),
            jax.ShapeDtypeStruct((T, LANES), f32), jax.ShapeDtypeStruct((T, G), bf16)]
    out_specs = [_rows(tm, D), _rows(tm, 2 * G), _rows(tm, SSM_WIDTH), _rows(tm, CONV_DIM), _rows(tm, LANES), _rows(tm, G)]
    return pl.pallas_call(body, name="mix_fwd", grid=(T // tm,), in_specs=in_specs, out_specs=out_specs,
                          out_shape=outs, scratch_shapes=[pltpu.VMEM((tm, G), f32)],
                          compiler_params=_params(("parallel",)))(*ins)


def _mix_bwd(dh, h1, gmix, uv, dya, dzxd, w_uv, w_zxd, ln_g, ln_b, w_s, b_st, gout, tm=256):
    T, D = dh.shape
    tm = min(tm, T)
    G = GM_WIDTH

    def body(dh_ref, h_ref, g_ref, uv_ref, dya_ref, dzxd_ref, wuv_ref, wzxd_ref, lng_ref, lnb_ref, ws_ref, bst_ref, gout_ref,
             dhi_ref, duv_ref, dg_ref, dlng_ref, dlnb_ref, dws_ref, dbst_ref, dgout_ref, mixed_sc, dvn_sc):
        @pl.when(pl.program_id(0) == 0)
        def _():
            for r in (dg_ref, dlng_ref, dlnb_ref, dws_ref, dbst_ref, dgout_ref):
                r[...] = jnp.zeros_like(r)

        dn_z = _dot(dzxd_ref[...], wzxd_ref[...])
        u = uv_ref[:, :G].astype(f32)
        v = uv_ref[:, G:].astype(f32)
        (ug, vn), pre_vjp = jax.vjp(_gm_pre, u, v, lng_ref[...], lnb_ref[...])
        vnb = vn.astype(bf16)
        _gm_mix(vnb, ws_ref, bst_ref[...], mixed_sc, tm)
        mixed = mixed_sc[...]
        _, out_vjp = jax.vjp(_rms, ug * mixed, gout_ref[...])
        dpre, dgout = out_vjp(dya_ref[...].astype(f32))
        dgout_ref[...] += dgout
        dug = dpre * mixed
        dmixed = dpre * ug
        mask = _tril_mask()
        lane = lax.broadcasted_iota(jnp.int32, (1, GM_HEADS), 1)
        dbst = jnp.zeros((CHUNK, GM_HEADS), f32)
        for h in range(GM_HEADS):
            wt = jnp.where(mask, ws_ref[h], 0.0).astype(bf16)
            cs = slice(h * CHUNK, (h + 1) * CHUNK)
            dw = jnp.zeros((CHUNK, CHUNK), f32)
            for q in range(tm // CHUNK):
                rs = slice(q * CHUNK, (q + 1) * CHUNK)
                dm = dmixed[rs, cs]
                dmb = dm.astype(bf16)
                dw = dw + _dot_nt(dmb, vnb[rs, cs])
                dbst = dbst + jnp.sum(dm, axis=1, keepdims=True) * (lane == h).astype(f32)
                dvn_sc[rs, cs] = _dot_tn(wt, dmb)
            dws_ref[h] += jnp.where(mask, dw, 0.0)
        dbst_ref[...] += dbst
        du, dv, dlng, dlnb = pre_vjp((dug, dvn_sc[...]))
        duv = jnp.concatenate([du.astype(bf16), dv.astype(bf16)], axis=1)
        duv_ref[...] = duv
        dlng_ref[...] += dlng
        dlnb_ref[...] += dlnb
        dn = dn_z + _dot(duv, wuv_ref[...])
        _, vjp = jax.vjp(_rms, h_ref[...], g_ref[...])
        dx, dg = vjp(dn)
        dhi_ref[...] = dh_ref[...] + dx
        dg_ref[...] += dg

    ins = [dh, h1, gmix, uv, dya, dzxd, w_uv, w_zxd, ln_g, ln_b, w_s, b_st, gout]
    in_specs = ([_rows(tm, D), _rows(tm, D), _full(gmix.shape), _rows(tm, 2 * G), _rows(tm, G), _rows(tm, dzxd.shape[1])]
                + [_full(x.shape) for x in ins[6:]])
    accs = (gmix, ln_g, ln_b, w_s, b_st, gout)
    outs = ([jax.ShapeDtypeStruct((T, D), f32), jax.ShapeDtypeStruct((T, 2 * G), bf16)]
            + [jax.ShapeDtypeStruct(x.shape, f32) for x in accs])
    out_specs = [_rows(tm, D), _rows(tm, 2 * G)] + [_acc(x.shape) for x in accs]
    return pl.pallas_call(body, name="mix_bwd", grid=(T // tm,), in_specs=in_specs, out_specs=out_specs,
                          out_shape=outs, scratch_shapes=[pltpu.VMEM((tm, G), f32), pltpu.VMEM((tm, G), f32)],
                          compiler_params=_params(("arbitrary",)))(*ins)


HALO = 16
PAIRS = SSM_HEADS // 2
PAIR_W = 2 * SSM_HEAD_DIM


def _split(x, n):
    parts = []
    for _ in range(n):
        p = x.astype(bf16)
        parts.append(p)
        x = x - p.astype(f32)
    return parts


def _dot_sel(x, sel_n, n):
    return _dot(jnp.concatenate(_split(x, n), axis=1), sel_n)


def _sel_dot(sel, x, n):
    return _dot(jnp.concatenate([sel] * n, axis=1), jnp.concatenate(_split(x, n), axis=0))


EXPAND_SPLIT = 3
REDUCE_SPLIT = 2


def _head_mats():
    ex = (jnp.arange(SSM_WIDTH)[None, :] // SSM_HEAD_DIM == jnp.arange(LANES)[:, None]).astype(bf16)
    return jnp.tile(ex, (EXPAND_SPLIT, 1)), jnp.tile(ex.T, (REDUCE_SPLIT, 1))


def _shift_mat(rows, cols, off):
    r = lax.broadcasted_iota(jnp.int32, (rows, cols), 0)
    c = lax.broadcasted_iota(jnp.int32, (rows, cols), 1)
    return (c == r + off).astype(bf16)


def _ssd_conv(c, xbc_ref, halo_ref, cw_ref, cb_ref):
    halo = halo_ref[...]
    ext = jnp.concatenate([jnp.where(c > 0, halo, jnp.zeros_like(halo)), xbc_ref[...]], axis=0)
    xc = cb_ref[...] + cw_ref[SSM_CONV - 1:SSM_CONV, :] * xbc_ref[...].astype(f32)
    for j in range(SSM_CONV - 1):
        xc = xc + cw_ref[j:j + 1, :] * _dot(_shift_mat(CHUNK, HALO + CHUNK, HALO - SSM_CONV + 1 + j), ext)
    return xc


def _ssd_front(dtr_ref, dtb_ref, alog_ref):
    dt = _softplus(dtr_ref[...] + dtb_ref[...])
    a = -jnp.exp(alog_ref[...])
    acs = jnp.dot(_tril_mask().astype(f32), dt * a, preferred_element_type=f32, precision=HIGHEST)
    return dt, a, acs


def _ssd_wide(xa, dt, acs, dsk, ex):
    dt_x = _dot_sel(dt, ex, EXPAND_SPLIT)
    acs_x = _dot_sel(acs, ex, EXPAND_SPLIT)
    dsk_x = _dot_sel(jnp.broadcast_to(dsk, (8, LANES)), ex, EXPAND_SPLIT)[0:1]
    e_x = jnp.exp(acs_x)
    r_x = jnp.exp(acs_x[CHUNK - 1:CHUNK, :] - acs_x)
    xs = xa[:, :SSM_WIDTH]
    xd = xs * dt_x
    return dt_x, dsk_x, e_x, r_x, xs, xd, xd * r_x


def _pair_stack(v, lo):
    return jnp.concatenate([jnp.where(lo, v, 0.0), jnp.where(lo, 0.0, v)], axis=0)


def _ssd_pair(j, acs, acs_t, cb):
    out = []
    tril = _tril_mask()
    for h in (2 * j, 2 * j + 1):
        dk = jnp.exp(jnp.where(tril, acs[:, h:h + 1] - acs_t[h:h + 1, :], -jnp.inf))
        out.append((dk, cb * dk))
    return out


def _pair_col(row_lo, tot, j):
    return jnp.exp(jnp.where(row_lo, tot[:, 2 * j:2 * j + 1], tot[:, 2 * j + 1:2 * j + 2]))


def _gated_norm(y, z, g):
    yg = y * (z * _sigmoid(z))
    half = SSM_WIDTH // SSM_GROUPS
    parts = []
    for k in range(SSM_GROUPS):
        s = yg[:, k * half:(k + 1) * half]
        parts.append(s * lax.rsqrt(jnp.mean(s * s, axis=-1, keepdims=True) + EPS))
    return jnp.concatenate(parts, axis=1) * g


def _group_mats(xa):
    out = []
    for g in range(SSM_GROUPS):
        bm = xa[:, SSM_WIDTH + g * SSM_STATE:SSM_WIDTH + (g + 1) * SSM_STATE].astype(bf16)
        cm = xa[:, SSM_WIDTH + (SSM_GROUPS + g) * SSM_STATE:SSM_WIDTH + (SSM_GROUPS + g + 1) * SSM_STATE].astype(bf16)
        out.append((cm, bm, _dot_nt(cm, bm)))
    return out


def _ssd_fwd(xbc, z, dtr, conv_w, conv_b, dt_bias, a_log, d_skip, ssm_norm):
    T = xbc.shape[0]
    nc = T // CHUNK
    N = SSM_STATE

    def body(xbc_ref, halo_ref, z_ref, dtr_ref, cw_ref, cb_ref, dtb_ref, alog_ref, dsk_ref, g_ref, ex_ref,
             yb_ref, xc_ref, sg_ref, y_ref, sprev_ref, s_sc):
        c = pl.program_id(0)

        @pl.when(c == 0)
        def _():
            s_sc[...] = jnp.zeros_like(s_sc)

        xc = _ssd_conv(c, xbc_ref, halo_ref, cw_ref, cb_ref)
        sg = _sigmoid(xc)
        xc_ref[...] = xc
        sg_ref[...] = sg
        xa = xc * sg
        dt, _, acs = _ssd_front(dtr_ref, dtb_ref, alog_ref)
        _, dsk_x, e_x, _, xs, xd, gm = _ssd_wide(xa, dt, acs, dsk_ref[...], ex_ref[...])
        acs_t = acs.T
        tot = acs[CHUNK - 1:CHUNK, :]
        groups = _group_mats(xa)
        lo = lax.broadcasted_iota(jnp.int32, (CHUNK, PAIR_W), 1) < SSM_HEAD_DIM
        row_lo = lax.broadcasted_iota(jnp.int32, (PAIR_W, 1), 0) < SSM_HEAD_DIM
        ys = []
        for j in range(PAIRS):
            cmb, bmb, cb = groups[j // (PAIRS // SSM_GROUPS)]
            ps = slice(j * PAIR_W, (j + 1) * PAIR_W)
            (_, m0), (_, m1) = _ssd_pair(j, acs, acs_t, cb)
            sp = s_sc[j]
            yd = _dot(jnp.concatenate([m0, m1], axis=1).astype(bf16), _pair_stack(xd[:, ps], lo).astype(bf16))
            ys.append(yd + e_x[:, ps] * _dot_nt(cmb, sp.astype(bf16)))
            sprev_ref[0, j] = sp
            s_sc[j] = _pair_col(row_lo, tot, j) * sp + _dot_tn(gm[:, ps].astype(bf16), bmb)
        y = jnp.concatenate(ys, axis=1) + xs * dsk_x
        y_ref[...] = y
        yb_ref[...] = _gated_norm(y, z_ref[...].astype(f32), g_ref[...]).astype(bf16)

    params = [conv_w, conv_b, dt_bias, a_log, d_skip, ssm_norm, _head_mats()[0]]
    hp = CHUNK // HALO
    in_specs = [_rows(CHUNK, CONV_DIM), pl.BlockSpec((HALO, CONV_DIM), lambda i: (jnp.maximum(i * hp - 1, 0), 0)),
                _rows(CHUNK, SSM_WIDTH), _rows(CHUNK, LANES)] + [_full(x.shape) for x in params]
    return pl.pallas_call(
        body, name="ssd_fwd", grid=(nc,), in_specs=in_specs,
        out_specs=[_rows(CHUNK, SSM_WIDTH), _rows(CHUNK, CONV_DIM), _rows(CHUNK, CONV_DIM), _rows(CHUNK, SSM_WIDTH),
                   pl.BlockSpec((1, PAIRS, PAIR_W, N), lambda i: (i, 0, 0, 0))],
        out_shape=[jax.ShapeDtypeStruct((T, SSM_WIDTH), bf16), jax.ShapeDtypeStruct((T, CONV_DIM), f32),
                   jax.ShapeDtypeStruct((T, CONV_DIM), f32), jax.ShapeDtypeStruct((T, SSM_WIDTH), f32),
                   jax.ShapeDtypeStruct((nc, PAIRS, PAIR_W, N), f32)],
        scratch_shapes=[pltpu.VMEM((PAIRS, PAIR_W, N), f32)],
        compiler_params=_params(("arbitrary",)))(xbc, xbc, z, dtr, *params)


def _ssd_bwd(xbc, xc, sg, y, z, dtr, sprev, dyb, conv_w, conv_b, dt_bias, a_log, d_skip, ssm_norm):
    T = xbc.shape[0]
    nc = T // CHUNK
    H, N = SSM_HEADS, SSM_STATE
    PG = PAIRS // SSM_GROUPS

    def body(xbc_ref, xc_ref, sg_ref, y_ref, z_ref, dtr_ref, sprev_ref, dyb_ref, cw_ref, cb_ref, dtb_ref, alog_ref, dsk_ref,
             g_ref, ex_ref, rd_ref, dzxd_ref, dcw_ref, dcb_ref, ddtb_ref, dalog_ref, ddsk_ref, dg_ref, ds_sc, next_sc):
        i = pl.program_id(0)

        @pl.when(i == 0)
        def _():
            ds_sc[...] = jnp.zeros_like(ds_sc)
            next_sc[...] = jnp.zeros_like(next_sc)
            for r_ in (dcw_ref, dcb_ref, ddtb_ref, dalog_ref, ddsk_ref, dg_ref):
                r_[...] = jnp.zeros_like(r_)

        xc = xc_ref[...]
        sg = sg_ref[...]
        xa = xc * sg
        dt, a, acs = _ssd_front(dtr_ref, dtb_ref, alog_ref)
        dt_x, dsk_x, e_x, r_x, xs, xd, gm = _ssd_wide(xa, dt, acs, dsk_ref[...], ex_ref[...])
        acs_t = acs.T
        tot = acs[CHUNK - 1:CHUNK, :]
        groups = _group_mats(xa)
        lo = lax.broadcasted_iota(jnp.int32, (CHUNK, PAIR_W), 1) < SSM_HEAD_DIM
        row_lo = lax.broadcasted_iota(jnp.int32, (PAIR_W, 1), 0) < SSM_HEAD_DIM
        pairs, zs = [], []
        for j in range(PAIRS):
            cmb, _, cb = groups[j // PG]
            pairs.append(_ssd_pair(j, acs, acs_t, cb))
            zs.append(_dot_nt(cmb, sprev_ref[0, j].astype(bf16)))
        zf = jnp.concatenate(zs, axis=1)
        _, gn_vjp = jax.vjp(_gated_norm, y_ref[...], z_ref[...].astype(f32), g_ref[...])
        dy, dz, dg = gn_vjp(dyb_ref[...].astype(f32))
        dg_ref[...] += dg
        dzxd_ref[:, :SSM_WIDTH] = dz.astype(bf16)

        lane = lax.broadcasted_iota(jnp.int32, (1, LANES), 1)
        sub = lax.broadcasted_iota(jnp.int32, (LANES, 1), 0)
        dacs = jnp.zeros((CHUNK, LANES), f32)
        dacs_r = jnp.zeros((LANES, CHUNK), f32)
        dtot = jnp.zeros((1, LANES), f32)
        dcb = [jnp.zeros((CHUNK, CHUNK), f32) for _ in range(SSM_GROUPS)]
        dcm = [jnp.zeros((CHUNK, N), f32) for _ in range(SSM_GROUPS)]
        dbm = [jnp.zeros((CHUNK, N), f32) for _ in range(SSM_GROUPS)]
        dxds, dgms = [], []
        for j in range(PAIRS):
            g = j // PG
            cmb, bmb, _ = groups[g]
            ps = slice(j * PAIR_W, (j + 1) * PAIR_W)
            (dk0, m0), (dk1, m1) = pairs[j]
            oh0, oh1 = (lane == 2 * j).astype(f32), (lane == 2 * j + 1).astype(f32)
            dyp = dy[:, ps]
            dy2 = _pair_stack(dyp, lo).astype(bf16)
            dm2 = _dot_nt(dy2, xd[:, ps].astype(bf16))
            m2 = jnp.concatenate([m0, m1], axis=0)
            dxds.append(_dot_tn(m2.astype(bf16), dy2))
            w2 = dm2 * m2
            rs = jnp.sum(w2, axis=1, keepdims=True)
            dacs = dacs + rs[:CHUNK] * oh0 + rs[CHUNK:] * oh1
            dacs_r = dacs_r - ((sub == 2 * j).astype(f32) * jnp.sum(w2[:CHUNK], axis=0, keepdims=True)
                               + (sub == 2 * j + 1).astype(f32) * jnp.sum(w2[CHUNK:], axis=0, keepdims=True))
            dcb[g] = dcb[g] + dm2[:CHUNK] * dk0 + dm2[CHUNK:] * dk1
            sp = sprev_ref[0, j]
            dzb = (dyp * e_x[:, ps]).astype(bf16)
            dcm[g] = dcm[g] + _dot(dzb, sp.astype(bf16))
            dsn = ds_sc[j]
            dsnb = dsn.astype(bf16)
            et = _pair_col(row_lo, tot, j)
            rr = jnp.sum(dsn * sp, axis=1, keepdims=True) * et
            dtot = dtot + jnp.sum(rr[:SSM_HEAD_DIM]) * oh0 + jnp.sum(rr[SSM_HEAD_DIM:]) * oh1
            dgms.append(_dot_nt(bmb, dsnb))
            dbm[g] = dbm[g] + _dot(gm[:, ps].astype(bf16), dsnb)
            ds_sc[j] = _dot_tn(dzb, cmb) + et * dsn
        dgm = jnp.concatenate(dgms, axis=1)
        dxd = jnp.concatenate(dxds, axis=1) + dgm * r_x
        dr = dgm * gm
        red = _dot_sel(jnp.concatenate([dy * e_x * zf - dr, dr, dxd * xs, dy * xs], axis=0), rd_ref[...], REDUCE_SPLIT)
        rowi = lax.broadcasted_iota(jnp.int32, (CHUNK, 1), 0)
        dtot = dtot + jnp.sum(red[CHUNK:2 * CHUNK], axis=0, keepdims=True)
        dacs = dacs + red[:CHUNK] + dacs_r.T + jnp.where(rowi == CHUNK - 1, dtot, 0.0)
        r2 = lax.broadcasted_iota(jnp.int32, (CHUNK, CHUNK), 0)
        c2 = lax.broadcasted_iota(jnp.int32, (CHUNK, CHUNK), 1)
        dadt = jnp.dot((c2 >= r2).astype(f32), dacs, preferred_element_type=f32, precision=HIGHEST)
        ddt = red[2 * CHUNK:3 * CHUNK] + dadt * a
        dalog_ref[...] += jnp.sum(dadt * dt, axis=0, keepdims=True) * a
        ddsk_ref[...] += jnp.sum(red[3 * CHUNK:], axis=0, keepdims=True)
        ddtr = jnp.where(lane < H, ddt * _sigmoid(dtr_ref[...] + dtb_ref[...]), 0.0)
        ddtb_ref[...] += jnp.sum(ddtr, axis=0, keepdims=True)
        dzxd_ref[:, SSM_WIDTH + CONV_DIM:] = ddtr.astype(bf16)
        dxa_bm, dxa_cm = [], []
        for g in range(SSM_GROUPS):
            cmb, bmb, _ = groups[g]
            dcbb = dcb[g].astype(bf16)
            dxa_bm.append(dbm[g] + _dot_tn(dcbb, cmb))
            dxa_cm.append(dcm[g] + _dot(dcbb, bmb))
        dxc = jnp.concatenate([dy * dsk_x + dxd * dt_x] + dxa_bm + dxa_cm, axis=1) * (sg * (1.0 + xc * (1.0 - sg)))
        ext = jnp.concatenate([dxc, next_sc[...]], axis=0)
        xin = xbc_ref[...].astype(f32)
        dxbc = cw_ref[SSM_CONV - 1:SSM_CONV, :] * dxc
        dcw = [jnp.sum(dxc * xin, axis=0, keepdims=True)]
        for s in range(1, SSM_CONV):
            later = _sel_dot(_shift_mat(CHUNK, CHUNK + HALO, s), ext, 2)
            dxbc = dxbc + cw_ref[SSM_CONV - 1 - s:SSM_CONV - s, :] * later
            dcw.insert(0, jnp.sum(later * xin, axis=0, keepdims=True))
        dzxd_ref[:, SSM_WIDTH:SSM_WIDTH + CONV_DIM] = dxbc.astype(bf16)
        dcw_ref[...] += jnp.concatenate(dcw, axis=0)
        dcb_ref[...] += jnp.sum(dxc, axis=0, keepdims=True)
        next_sc[...] = dxc[0:HALO, :]

    params = [conv_w, conv_b, dt_bias, a_log, d_skip, ssm_norm]
    mats = list(_head_mats())

    def rev(ncols):
        return pl.BlockSpec((CHUNK, ncols), lambda i: (nc - 1 - i, 0))

    in_specs = ([rev(CONV_DIM), rev(CONV_DIM), rev(CONV_DIM), rev(SSM_WIDTH), rev(SSM_WIDTH), rev(LANES),
                 pl.BlockSpec((1, PAIRS, PAIR_W, N), lambda i: (nc - 1 - i, 0, 0, 0)), rev(SSM_WIDTH)]
                + [_full(x.shape) for x in params + mats])
    return pl.pallas_call(
        body, name="ssd_bwd", grid=(nc,), in_specs=in_specs,
        out_specs=[rev(ZXD)] + [_acc(x.shape) for x in params],
        out_shape=[jax.ShapeDtypeStruct((T, ZXD), bf16)] + [jax.ShapeDtypeStruct(x.shape, f32) for x in params],
        scratch_shapes=[pltpu.VMEM((PAIRS, PAIR_W, N), f32), pltpu.VMEM((HALO, CONV_DIM), f32)],
        compiler_params=_params(("arbitrary",)))(xbc, xc, sg, y, z, dtr, sprev, dyb, *params, *mats)


def _tail(h3, p, tgt, gp, wpg, bpg, wpp, gf, tm=512):
    T, D = h3.shape
    tm = min(tm, T)

    def head(gpre, pp, h, gf_, t):
        gate = _sigmoid(gpre)
        y = _rms(h + gate * pp, gf_)
        err = y - t
        return 0.5 * jnp.sum(jnp.mean(err * err, axis=-1))

    def body(h_ref, p_ref, t_ref, gp_ref, wpg_ref, bpg_ref, wpp_ref, gf_ref,
             dh_ref, loss_ref, dgp_ref, dwpg_ref, dbpg_ref, dwpp_ref, dgf_ref):
        @pl.when(pl.program_id(0) == 0)
        def _():
            for r in (loss_ref, dgp_ref, dwpg_ref, dbpg_ref, dwpp_ref, dgf_ref):
                r[...] = jnp.zeros_like(r)

        h = h_ref[...]
        npf, np_vjp = jax.vjp(_rms, h, gp_ref[...])
        npb = npf.astype(bf16)
        pb = p_ref[...].astype(bf16)
        gpre = _dot(npb, wpg_ref[...]) + bpg_ref[...]
        kp, _, cp = wpp_ref.shape
        pp = jnp.concatenate([_dot(pb, wpp_ref[k]) for k in range(kp)], axis=1)
        loss, head_vjp = jax.vjp(head, gpre, pp, h, gf_ref[...], t_ref[...])
        dgpre, dpp, dh_a, dgf, _ = head_vjp(jnp.ones((), f32))
        loss_ref[...] += loss
        dgf_ref[...] += dgf
        dbpg_ref[...] += jnp.sum(dgpre, axis=0, keepdims=True)
        dgb = dgpre.astype(bf16)
        dwpg_ref[...] += _dot_tn(npb, dgb)
        dppb = dpp.astype(bf16)
        for k in range(kp):
            dwpp_ref[k] += _dot_tn(pb, dppb[:, k * cp:(k + 1) * cp])
        dh_b, dgp = np_vjp(_dot_nt(dgb, wpg_ref[...]))
        dgp_ref[...] += dgp
        dh_ref[...] = dh_a + dh_b

    ins = [h3, p, tgt, gp, wpg, bpg, wpp, gf]
    in_specs = [_rows(tm, D), _rows(tm, p.shape[1]), _rows(tm, D)] + [_full(x.shape) for x in ins[3:]]
    acc_shapes = [(1, LANES), gp.shape, wpg.shape, bpg.shape, wpp.shape, gf.shape]
    return pl.pallas_call(
        body, name="tail", grid=(T // tm,), in_specs=in_specs,
        out_specs=[_rows(tm, D)] + [_acc(s) for s in acc_shapes],
        out_shape=[jax.ShapeDtypeStruct((T, D), f32)] + [jax.ShapeDtypeStruct(s, f32) for s in acc_shapes],
        compiler_params=_params(("arbitrary",)))(*ins)


def _adamw(name, w, g, m, v, tr=256):
    R, C = w.shape
    tr = _row_tile(R, tr)

    def body(w_ref, g_ref, m_ref, v_ref, d_ref, mo_ref, vo_ref):
        g_ = g_ref[...]
        m_ = ADAM_B1 * m_ref[...] + (1.0 - ADAM_B1) * g_
        v_ = ADAM_B2 * v_ref[...] + (1.0 - ADAM_B2) * jnp.square(g_)
        m_hat = m_ / (1.0 - ADAM_B1 ** ADAM_STEP)
        v_hat = v_ / (1.0 - ADAM_B2 ** ADAM_STEP)
        d_ref[...] = -ADAM_LR * (m_hat / (jnp.sqrt(v_hat) + ADAM_EPS) + ADAM_WD * w_ref[...])
        mo_ref[...] = m_
        vo_ref[...] = v_

    spec = pl.BlockSpec((tr, C), lambda i: (i, 0))
    return pl.pallas_call(body, name=name, grid=(R // tr,), in_specs=[spec] * 4, out_specs=[spec] * 3,
                          out_shape=[jax.ShapeDtypeStruct((R, C), f32)] * 3,
                          compiler_params=_params(("parallel",)))(w, g, m, v)


HBM = pl.BlockSpec(memory_space=pltpu.HBM)


def _me():
    return lax.axis_index("x"), lax.axis_index("y"), lax.axis_index("c")


def _other_chips(x, y):
    return [(1 - x, y), (x, 1 - y), (1 - x, 1 - y)]


def _remote(src, dst, send_sem, recv_sem, dev):
    return pltpu.make_async_remote_copy(src_ref=src, dst_ref=dst, send_sem=send_sem, recv_sem=recv_sem,
                                        device_id=dev, device_id_type=MESH)


def _sems(n):
    return [pltpu.SemaphoreType.DMA((n,)), pltpu.SemaphoreType.DMA((n,))]


def _gather_weights(shards, split):
    n = len(shards)

    def body(*refs):
        ins, outs = refs[:n], refs[n:2 * n]
        own_send, own_recv, ici_send, ici_recv, d2d_send, d2d_recv = refs[2 * n:]
        x, y, c = _me()
        my_chip = 2 * x + y
        sibling = (x, y, 1 - c)
        chips = _other_chips(x, y)

        def rows(i, half):
            hr = shards[i].shape[0] // 2
            return pl.ds(half * hr, hr) if split[i] else pl.ds(0, shards[i].shape[0])

        sends = []
        for i in range(n):
            for j, chip in enumerate(chips):
                cp = _remote(ins[i].at[rows(i, c)], outs[i].at[my_chip, rows(i, c)],
                             ici_send.at[3 * i + j], ici_recv.at[3 * i + j], (*chip, c))
                cp.start()
                sends.append(cp)
            cp = _remote(ins[i], outs[i].at[my_chip], own_send.at[i], own_recv.at[i], sibling)
            cp.start()
            sends.append(cp)
        for i in range(n):
            for j, chip in enumerate(chips):
                s = 3 * i + j
                land = outs[i].at[2 * chip[0] + chip[1], rows(i, c)]
                _remote(land, land, ici_send.at[s], ici_recv.at[s], (*chip, c)).wait_recv()
                if split[i]:
                    cp = _remote(land, land, d2d_send.at[s], d2d_recv.at[s], sibling)
                    cp.start()
                    sends.append(cp)
        for i in range(n):
            _remote(ins[i], outs[i].at[my_chip], own_send.at[i], own_recv.at[i], sibling).wait_recv()
            if split[i]:
                for j, chip in enumerate(chips):
                    s = 3 * i + j
                    land = outs[i].at[2 * chip[0] + chip[1], rows(i, 1 - c)]
                    _remote(land, land, d2d_send.at[s], d2d_recv.at[s], sibling).wait_recv()
        for cp in sends:
            cp.wait_send()

    return pl.pallas_call(
        body, name="gather_weights", out_shape=[jax.ShapeDtypeStruct((N_CHIPS,) + s.shape, s.dtype) for s in shards],
        in_specs=[HBM] * n, out_specs=[HBM] * n,
        scratch_shapes=_sems(n) + _sems(3 * n) + _sems(3 * n))(*shards)


def _swap_halves(name, grads):
    n = len(grads)

    def body(*refs):
        ins, outs, send, recv = refs[:n], refs[n:2 * n], refs[2 * n], refs[2 * n + 1]
        x, y, c = _me()
        copies = []
        for i in range(n):
            hr = grads[i].shape[1] // 2
            cp = _remote(ins[i].at[:, pl.ds((1 - c) * hr, hr), :], outs[i], send.at[i], recv.at[i], (x, y, 1 - c))
            cp.start()
            copies.append(cp)
        for cp in copies:
            cp.wait()

    return pl.pallas_call(
        body, name=name,
        out_shape=[jax.ShapeDtypeStruct((g.shape[0], g.shape[1] // 2, g.shape[2]), g.dtype) for g in grads],
        in_specs=[HBM] * n, out_specs=[HBM] * n, scratch_shapes=_sems(n))(*grads)


def _add_halves(name, grads, other, c_idx, th=592):
    K, R, C = grads.shape
    H = R // 2
    th = _row_tile(H, th, 16)
    nb = H // th

    def body(c_ref, g_ref, o_ref, out_ref):
        out_ref[...] = (g_ref[...].astype(f32) + o_ref[...].astype(f32)).astype(bf16)

    grid_spec = pltpu.PrefetchScalarGridSpec(
        num_scalar_prefetch=1, grid=(nb,),
        in_specs=[pl.BlockSpec((K, th, C), lambda i, c: (0, c[0] * nb + i, 0)),
                  pl.BlockSpec((K, th, C), lambda i, c: (0, i, 0))],
        out_specs=pl.BlockSpec((K, th, C), lambda i, c: (0, i, 0)))
    return pl.pallas_call(body, name=name, grid_spec=grid_spec,
                          out_shape=jax.ShapeDtypeStruct((K, H, C), bf16),
                          compiler_params=_params(("parallel",)))(c_idx, grads, other)


SEM = pl.BlockSpec(memory_space=pltpu.SEMAPHORE)
ANY = pl.BlockSpec(memory_space=pl.ANY)
EFFECT = pltpu.SideEffectType.DATAFLOW_SIDE_EFFECTING


def _copies_start(name, srcs, land_shapes, n_copies, make_copies, after):
    ns, nl = len(srcs), len(land_shapes)
    lands = [lax.empty(s.shape, s.dtype) for s in land_shapes]

    def body(*refs):
        src_refs, land_refs = refs[:ns], refs[ns:ns + nl]
        send, recv, token = refs[ns + nl + 1], refs[ns + nl + 2], refs[-1]
        for cp in make_copies(src_refs, land_refs, send, recv):
            cp.start()
        token[...] = jnp.zeros_like(token)

    buffers = list(srcs) + lands
    out = pl.pallas_call(
        body, name=name,
        out_shape=(pltpu.SemaphoreType.DMA((n_copies,)), pltpu.SemaphoreType.DMA((n_copies,)),
                   *[pltpu.HBM(b.shape, b.dtype) for b in buffers], jax.ShapeDtypeStruct((8, LANES), f32)),
        in_specs=[HBM] * (ns + nl) + [ANY],
        out_specs=(SEM, SEM, *[HBM] * (ns + nl), pl.BlockSpec(memory_space=pltpu.VMEM)),
        input_output_aliases={i: 2 + i for i in range(ns + nl)},
        compiler_params=pltpu.CompilerParams(has_side_effects=EFFECT),
    )(*[pltpu.with_memory_space_constraint(b, pltpu.HBM) for b in buffers], after)
    return out[0], out[1], list(out[2:2 + ns]), list(out[2 + ns:2 + ns + nl]), out[-1]


def _copies_wait(name, started, make_copies, after):
    send, recv, srcs, lands, _ = started
    ns, nl = len(srcs), len(lands)
    after = list(after)

    def body(*refs):
        src_refs, land_refs = refs[:ns], refs[ns:ns + nl]
        for cp in make_copies(src_refs, land_refs, refs[ns + nl], refs[ns + nl + 1]):
            cp.wait_send()
            cp.wait_recv()

    buffers = list(srcs) + list(lands)
    out = pl.pallas_call(
        body, name=name, out_shape=tuple(pltpu.HBM(b.shape, b.dtype) for b in buffers),
        in_specs=[HBM] * (ns + nl) + [SEM, SEM] + [ANY] * len(after), out_specs=tuple([HBM] * (ns + nl)),
        input_output_aliases={i: i for i in range(ns + nl)},
        compiler_params=pltpu.CompilerParams(has_side_effects=EFFECT),
    )(*buffers, send, recv, *after)
    return list(out[:ns]), list(out[ns:])


def _gather_copies(src_refs, land_refs, send, recv):
    x, y, c = _me()
    my_chip = 2 * x + y
    peers = [(*chip, c) for chip in _other_chips(x, y)] + [(x, y, 1 - c)]
    return [_remote(src_refs[i], land_refs[i].at[my_chip], send.at[4 * i + j], recv.at[4 * i + j], peer)
            for i in range(len(src_refs)) for j, peer in enumerate(peers)]


def _partial_copies(src_refs, land_refs, send, recv):
    x, y, c = _me()
    return [_remote(src_refs[i].at[2 * chip[0] + chip[1]], land_refs[i].at[j], send.at[3 * i + j], recv.at[3 * i + j], (*chip, c))
            for i in range(len(src_refs)) for j, chip in enumerate(_other_chips(x, y))]


def _small_copies(src_refs, land_refs, send, recv):
    x, y, c = _me()
    return [_remote(src_refs[0], land_refs[0].at[k - 1], send.at[k - 1], recv.at[k - 1], (x ^ (k >> 2), y ^ ((k >> 1) & 1), c ^ (k & 1)))
            for k in range(1, N_DEV)]


def _sum_small(own, slots, dev_idx):
    R, C = own.shape

    def body(dev_ref, own_ref, s_ref, o_ref):
        me = dev_ref[0]
        acc = jnp.zeros((R, C), f32)
        for d in range(N_DEV):
            k = me ^ d
            acc = acc + jnp.where(k == 0, own_ref[...], s_ref[jnp.maximum(k - 1, 0)])
        o_ref[...] = acc

    grid_spec = pltpu.PrefetchScalarGridSpec(
        num_scalar_prefetch=1, grid=(1,),
        in_specs=[pl.BlockSpec((R, C), lambda i, dev: (0, 0)), pl.BlockSpec((N_DEV - 1, R, C), lambda i, dev: (0, 0, 0))],
        out_specs=pl.BlockSpec((R, C), lambda i, dev: (0, 0)))
    return pl.pallas_call(body, name="sum_small", grid_spec=grid_spec, out_shape=jax.ShapeDtypeStruct((R, C), f32),
                          compiler_params=_params(("arbitrary",)))(dev_idx, own, slots)


def _sum_partials(name, part, recv, chip_idx, th=592):
    K, H, C = part.shape
    th = _row_tile(H, th, 16)

    def body(chip_ref, p_ref, r_ref, o_ref):
        acc = p_ref[...].astype(f32)
        for j in range(3):
            acc = acc + r_ref[j].astype(f32)
        o_ref[...] = acc

    grid_spec = pltpu.PrefetchScalarGridSpec(
        num_scalar_prefetch=1, grid=(H // th,),
        in_specs=[pl.BlockSpec((None, th, C), lambda i, chip: (chip[0], i, 0)),
                  pl.BlockSpec((3, th, C), lambda i, chip: (0, i, 0))],
        out_specs=pl.BlockSpec((th, C), lambda i, chip: (i, 0)))
    return pl.pallas_call(body, name=name, grid_spec=grid_spec, out_shape=jax.ShapeDtypeStruct((H, C), f32),
                          compiler_params=_params(("parallel",)))(chip_idx, part, recv)


def _share_halves(name, halves):
    n = len(halves)

    def body(*refs):
        ins, outs, send, recv = refs[:n], refs[n:2 * n], refs[2 * n], refs[2 * n + 1]
        x, y, c = _me()
        copies = []
        for i in range(n):
            cp = _remote(ins[i], outs[i], send.at[i], recv.at[i], (x, y, 1 - c))
            cp.start()
            copies.append(cp)
        for cp in copies:
            cp.wait()

    return pl.pallas_call(
        body, name=name, out_shape=[jax.ShapeDtypeStruct(h.shape, h.dtype) for h in halves],
        in_specs=[HBM] * n, out_specs=[HBM] * n, scratch_shapes=_sems(n))(*halves)


def _adamw_big(name, w, g_mine, g_theirs, m, v, c_idx, tr=320):
    R, C = w.shape
    H = R // 2
    tr = _row_tile(H, tr)
    nb = H // tr

    def body(c_ref, w_ref, gm_ref, gt_ref, m_ref, v_ref, g_ref, d_ref, mo_ref, vo_ref):
        g_ = jnp.where(pl.program_id(0) // nb == c_ref[0], gm_ref[...], gt_ref[...])
        g_ref[...] = g_
        m_ = ADAM_B1 * m_ref[...] + (1.0 - ADAM_B1) * g_
        v_ = ADAM_B2 * v_ref[...] + (1.0 - ADAM_B2) * jnp.square(g_)
        m_hat = m_ / (1.0 - ADAM_B1 ** ADAM_STEP)
        v_hat = v_ / (1.0 - ADAM_B2 ** ADAM_STEP)
        d_ref[...] = -ADAM_LR * (m_hat / (jnp.sqrt(v_hat) + ADAM_EPS) + ADAM_WD * w_ref[...])
        mo_ref[...] = m_
        vo_ref[...] = v_

    full = pl.BlockSpec((tr, C), lambda i, c: (i, 0))
    half = pl.BlockSpec((tr, C), lambda i, c: (i % nb, 0))
    grid_spec = pltpu.PrefetchScalarGridSpec(num_scalar_prefetch=1, grid=(2 * nb,),
                                             in_specs=[full, half, half, full, full], out_specs=[full] * 4)
    return pl.pallas_call(body, name=name, grid_spec=grid_spec, out_shape=[jax.ShapeDtypeStruct((R, C), f32)] * 4,
                          compiler_params=_params(("parallel",)))(c_idx, w, g_mine, g_theirs, m, v)


BIG = ("ffn1_w_gate", "ffn1_w_up", "ffn1_w_down", "w_in", "w_out", "ffn2_w_gate", "ffn2_w_up", "ffn2_w_down",
       "ple_w_gate", "ple_w_proj")


SMALL = ("ffn1_norm", "mix_norm", "gm_ln_g", "gm_ln_b", "gm_w_s", "gm_b_s", "gm_out_norm", "conv_b", "dt_bias", "a_log",
         "d_skip", "ssm_norm", "ffn2_norm", "ple_norm", "ple_b_gate", "final_norm")
SMALL_C = 1024


def _pack_small(vals):
    parts = []
    for v in vals:
        f = v.astype(f32).reshape(-1)
        parts.append(jnp.pad(f, (0, -f.shape[0] % SMALL_C)))
    flat = jnp.concatenate(parts)
    rows = flat.shape[0] // SMALL_C
    return jnp.pad(flat, (0, (-rows % 8) * SMALL_C)).reshape(-1, SMALL_C)


def _unpack_small(pack, shapes):
    flat = pack.reshape(-1)
    out, off = [], 0
    for s in shapes:
        n = 1
        for d in s:
            n *= d
        out.append(flat[off:off + n].reshape(s))
        off += n + (-n % SMALL_C)
    return out


def _pad_lanes(v):
    return jnp.pad(v, ((0, 0), (0, LANES - v.shape[1])))


def _pad_rows(a):
    pad = [(0, 0)] * a.ndim
    pad[-2] = (0, -a.shape[-2] % ROW_PAD)
    return jnp.pad(a, pad) if pad[-2][1] else a


FETCH = (("ffn1_w_gate", "ffn1_w_up", "ffn1_w_down"), ("w_in", "conv_w", "w_out"),
         ("ffn2_w_gate", "ffn2_w_up", "ffn2_w_down", "ple_w_gate", "ple_w_proj"))
TRANSPOSED = ("ffn1_w_gate", "ffn1_w_up", "ffn2_w_gate", "ffn2_w_up", "w_in")
ROW_PAD = 32
DONE = (("ffn2_w_gate", "ffn2_w_up", "ffn2_w_down", "w_out", "ple_w_gate", "ple_w_proj"), ("w_in",),
        ("ffn1_w_gate", "ffn1_w_up", "ffn1_w_down"))


def _local_step(x, p, tgt, fetch, S, on_grads):
    G = GM_WIDTH
    K = N_CHIPS
    b_st = S["gm_b_s"][0].T
    w_s = S["gm_w_s"][0]
    dtb, alog, dsk = _pad_lanes(S["dt_bias"]), _pad_lanes(S["a_log"]), _pad_lanes(S["d_skip"])
    gfin = S["final_norm"].reshape(1, -1)

    def rows(a):
        return a.reshape(-1, D_MODEL)

    def shards(a):
        return a.reshape(K, -1, D_MODEL)

    wg1, wu1, wd1 = [rows(a) for a in fetch(0, None)]
    h1, n1, a1, b1 = _ffn_fwd("ffn1_fwd", x, S["ffn1_norm"], wg1, wu1, wd1)
    w_in4, cw4, wo4 = fetch(1, h1)
    w_in = w_in4.reshape(IN_PROJ, D_MODEL)
    w_uv = w_in[:2 * G]
    w_zxd = jnp.pad(w_in[2 * G:], ((0, ZXD - (IN_PROJ - 2 * G)), (0, 0)))
    conv_w = jnp.transpose(cw4, (1, 0, 2)).reshape(SSM_CONV, CONV_DIM)
    wo = wo4.reshape(-1, D_MODEL)
    n2, uv, z, xbc, dtr, ya = _mix_fwd(h1, S["mix_norm"], w_uv, w_zxd, S["gm_ln_g"], S["gm_ln_b"], w_s, b_st, S["gm_out_norm"])
    yb, xc, sg, y_ssd, sprev = _ssd_fwd(xbc, z, dtr, conv_w, S["conv_b"], dtb, alog, dsk, S["ssm_norm"])
    wg2, wu2, wd2, wpg4, wpp4 = fetch(2, yb)
    wg2, wu2, wd2 = rows(wg2), rows(wu2), rows(wd2)
    h2, h3, n3, a2, b2 = _ffn_fwd("ffn2_fwd", h1, S["ffn2_norm"], wg2, wu2, wd2, pre=(ya, yb, wo))
    dh3, loss, dgp, dwpg, dbpg, dwpp, dgf = _tail(h3, p, tgt, S["ple_norm"], wpg4.reshape(-1, D_MODEL), S["ple_b_gate"], wpp4, gfin)
    dh2, da2, db2, hm2, dg_ffn2, dya, dyb = _ffn_bwd("ffn2_bwd", dh3, h2, S["ffn2_norm"], a2, b2, wg2, wu2, wd2, wo=wo, ga=G)
    dw_out = jnp.concatenate([_matmul_tn("dw_out_a", ya, dh2), _matmul_tn("dw_out_b", yb, dh2)], axis=0).reshape(wo4.shape)
    zero = on_grads(0, [shards(_matmul_tn("dw_ffn2_gate", da2, n3)), shards(_matmul_tn("dw_ffn2_up", db2, n3)),
                        shards(_matmul_tn("dw_ffn2_down", hm2, dh3, scale=0.5)), dw_out,
                        dwpg.astype(bf16).reshape(wpg4.shape), dwpp.astype(bf16)])
    dzxd, dcw, dcb, ddtb, dalog, ddsk, dgssm = _ssd_bwd(xbc, xc, sg, y_ssd, z, dtr, sprev, dyb, conv_w, S["conv_b"], dtb, alog, dsk,
                                                        S["ssm_norm"] + zero)
    dh1, duv, dg_mix, dlng, dlnb, dws, dbst, dgout = _mix_bwd(dh2, h1, S["mix_norm"], uv, dya, dzxd, w_uv, w_zxd, S["gm_ln_g"],
                                                              S["gm_ln_b"], w_s, b_st, S["gm_out_norm"])
    dw_in = jnp.concatenate([_matmul_tn("dw_in_uv", duv, n2), _matmul_tn("dw_in_zxd", dzxd, n2)[:IN_PROJ - 2 * G]], axis=0)
    zero = on_grads(1, [dw_in.reshape(w_in4.shape)])
    dx, da1, db1, hm1, dg_ffn1 = _ffn_bwd("ffn1_bwd", dh1, x, S["ffn1_norm"] + zero, a1, b1, wg1, wu1, wd1)
    zero = on_grads(2, [shards(_matmul_tn("dw_ffn1_gate", da1, n1)), shards(_matmul_tn("dw_ffn1_up", db1, n1)),
                        shards(_matmul_tn("dw_ffn1_down", hm1, dh1, scale=0.5))])
    loss = loss + zero
    nh = SSM_HEADS
    gS = {"ffn1_norm": dg_ffn1, "mix_norm": dg_mix, "gm_ln_g": dlng, "gm_ln_b": dlnb, "gm_w_s": dws[None], "gm_b_s": dbst.T[None],
          "gm_out_norm": dgout, "conv_b": dcb, "dt_bias": ddtb[:, :nh], "a_log": dalog[:, :nh], "d_skip": ddsk[:, :nh],
          "ssm_norm": dgssm, "ffn2_norm": dg_ffn2, "ple_norm": dgp, "ple_b_gate": dbpg, "final_norm": dgf.reshape(-1)}
    return loss, dx, dcw, gS


_WEIGHTS = ("ffn1_norm", "ffn1_w_gate", "ffn1_w_up", "ffn1_w_down", "mix_norm", "w_in", "gm_ln_g", "gm_ln_b", "gm_w_s", "gm_b_s",
            "gm_out_norm", "conv_w", "conv_b", "dt_bias", "a_log", "d_skip", "ssm_norm", "w_out", "ffn2_norm", "ffn2_w_gate",
            "ffn2_w_up", "ffn2_w_down", "ple_norm", "ple_w_gate", "ple_b_gate", "ple_w_proj", "final_norm")
_BIG_NAMES = BIG


def kernel(x, p, ffn1_norm, ffn1_w_gate, ffn1_w_up, ffn1_w_down, mix_norm, w_in, gm_ln_g, gm_ln_b, gm_w_s, gm_b_s, gm_out_norm, conv_w, conv_b, dt_bias, a_log, d_skip, ssm_norm, w_out, ffn2_norm, ffn2_w_gate, ffn2_w_up, ffn2_w_down, ple_norm, ple_w_gate, ple_b_gate, ple_w_proj, final_norm, loss_target, m_ffn1_norm, m_ffn1_w_gate, m_ffn1_w_up, m_ffn1_w_down, m_mix_norm, m_w_in, m_gm_ln_g, m_gm_ln_b, m_gm_w_s, m_gm_b_s, m_gm_out_norm, m_conv_w, m_conv_b, m_dt_bias, m_a_log, m_d_skip, m_ssm_norm, m_w_out, m_ffn2_norm, m_ffn2_w_gate, m_ffn2_w_up, m_ffn2_w_down, m_ple_norm, m_ple_w_gate, m_ple_b_gate, m_ple_w_proj, m_final_norm, v_ffn1_norm, v_ffn1_w_gate, v_ffn1_w_up, v_ffn1_w_down, v_mix_norm, v_w_in, v_gm_ln_g, v_gm_ln_b, v_gm_w_s, v_gm_b_s, v_gm_out_norm, v_conv_w, v_conv_b, v_dt_bias, v_a_log, v_d_skip, v_ssm_norm, v_w_out, v_ffn2_norm, v_ffn2_w_gate, v_ffn2_w_up, v_ffn2_w_down, v_ple_norm, v_ple_w_gate, v_ple_b_gate, v_ple_w_proj, v_final_norm):
    given = dict(locals())
    w = {n: given[n] for n in _WEIGHTS}
    m = {n: given["m_" + n] for n in _WEIGHTS}
    v = {n: given["v_" + n] for n in _WEIGHTS}

    c_idx = lax.axis_index("c").astype(jnp.int32).reshape(1)
    chip = 2 * lax.axis_index("x") + lax.axis_index("y")
    chip_idx = chip.astype(jnp.int32).reshape(1)

    shard = {n: (jnp.swapaxes(w[n][0], 0, 1) if n in TRANSPOSED else w[n][0]).astype(bf16) for n in BIG}
    shard["conv_w"] = w["conv_w"][0]
    first = _gather_weights([shard[n] for n in FETCH[0]], [True] * len(FETCH[0]))
    fetching, after = [], first[-1]
    for k in (1, 2):
        srcs = [shard[n] for n in FETCH[k]]
        lands = [jax.ShapeDtypeStruct((N_CHIPS,) + s.shape, s.dtype) for s in srcs]
        fetching.append(_copies_start("gather%d_start" % k, srcs, lands, 4 * len(srcs), _gather_copies, after))
        after = fetching[-1][4]

    def fetch(k, after_):
        return first if k == 0 else _copies_wait("gather%d_wait" % k, fetching[k - 1], _gather_copies, [after_])[1]

    exchanging = []

    def on_grads(k, grads):
        grads = [_pad_rows(g_) for g_ in grads]
        others = _swap_halves("swap%d" % k, grads)
        parts = [_add_halves("add_" + n, g_, o_, c_idx) for n, g_, o_ in zip(DONE[k], grads, others)]
        lands = [jax.ShapeDtypeStruct((3,) + p_.shape[1:], p_.dtype) for p_ in parts]
        exchanging.append(_copies_start("exchange%d_start" % k, parts, lands, 3 * len(parts), _partial_copies, c_idx))
        return exchanging[-1][4][0, 0]

    S = {n: w[n] for n in SMALL}
    S["ffn1_norm"] = S["ffn1_norm"] + after[0, 0]
    loss, dx, dcw, gS = _local_step(x[0], p[0, 0], loss_target[0], fetch, S, on_grads)

    small = _pack_small([gS[n] for n in SMALL] + [dcw, loss[:, :1]])
    small_lands = [jax.ShapeDtypeStruct((N_DEV - 1,) + small.shape, small.dtype)]
    small_st = _copies_start("small_start", [small], small_lands, N_DEV - 1, _small_copies, c_idx)

    g, delta, new_m, new_v = {}, {}, {}, {}
    after = [small_st[4]]
    for k in range(len(DONE)):
        parts, recv = _copies_wait("exchange%d_wait" % k, exchanging[k], _partial_copies, after)
        mine = [_sum_partials("sum_" + n, p_, r_, chip_idx) for n, p_, r_ in zip(DONE[k], parts, recv)]
        theirs = _share_halves("share%d" % k, mine)
        after = []
        for n, gm_, gt_ in zip(DONE[k], mine, theirs):
            flip = (lambda a: jnp.swapaxes(a, 0, 1)) if n in TRANSPOSED else (lambda a: a)
            rows = flip(w[n][0]).shape[0]
            w_, m_, v_ = [_pad_rows(flip(a[n][0])) for a in (w, m, v)]
            outs = _adamw_big("adamw_" + n, w_, gm_, gt_, m_, v_, c_idx)
            g[n], delta[n], new_m[n], new_v[n] = [flip(o[:rows])[None] for o in outs]
            after.append(outs[3])
    (own,), (slots,) = _copies_wait("small_wait", small_st, _small_copies, after)
    dev_idx = (2 * chip + lax.axis_index("c")).astype(jnp.int32).reshape(1)
    small_shapes = [w[n].shape for n in SMALL] + [dcw.shape, (1, 1)]
    small_sum = _unpack_small(_sum_small(own, slots, dev_idx), small_shapes)
    g.update({n: small_sum[i] for i, n in enumerate(SMALL)})
    cshard = w["conv_w"].shape[2]
    g["conv_w"] = lax.dynamic_slice_in_dim(small_sum[len(SMALL)], chip * cshard, cshard, axis=1)[None]
    loss_total = small_sum[len(SMALL) + 1].reshape(())
    sm_names = SMALL + ("conv_w",)
    sm_shapes = [w[n].shape for n in sm_names]
    d_s, m_s, v_s = _adamw("adamw_small", _pack_small([w[n] for n in sm_names]), _pack_small([g[n] for n in sm_names]),
                           _pack_small([m[n] for n in sm_names]), _pack_small([v[n] for n in sm_names]))
    for dst, src in ((delta, d_s), (new_m, m_s), (new_v, v_s)):
        for n, val in zip(sm_names, _unpack_small(src, sm_shapes)):
            dst[n] = val

    return (loss_total, dx[None], *[g[n] for n in _WEIGHTS], *[delta[n] for n in _WEIGHTS],
            *[new_m[n] for n in _WEIGHTS], *[new_v[n] for n in _WEIGHTS])
```

```python
import jax
import jax.numpy as jnp
from jax import lax
from jax.experimental import pallas as pl
from jax.experimental.pallas import tpu as pltpu

f32 = jnp.float32
bf16 = jnp.bfloat16
MESH = pl.DeviceIdType.MESH
HIGHEST = lax.Precision.HIGHEST

EPS = 1e-6
N_CHIPS = 4
N_DEV = 8
D_MODEL = 1024
GM_WIDTH = 1024
GM_HEADS = 8
CHUNK = 128
SSM_WIDTH = 1024
SSM_HEADS = 16
SSM_HEAD_DIM = 64
SSM_GROUPS = 2
SSM_STATE = 128
SSM_CONV = 4
CONV_DIM = SSM_WIDTH + 2 * SSM_GROUPS * SSM_STATE
IN_PROJ = 2 * GM_WIDTH + SSM_WIDTH + CONV_DIM + SSM_HEADS
LANES = 128
ZXD = SSM_WIDTH + CONV_DIM + LANES

ADAM_LR = 0.001
ADAM_B1 = 0.9
ADAM_B2 = 0.999
ADAM_EPS = 1e-08
ADAM_WD = 0.01
ADAM_STEP = 10

VMEM_LIMIT = 56 * 1024 * 1024
HALF_ROWS_BF16 = 592
HALF_ROWS_F32 = 320


def _dot(a, b):
    return jnp.dot(a, b, preferred_element_type=f32)


def _dot_nt(a, b):
    return lax.dot_general(a, b, (((1,), (1,)), ((), ())), preferred_element_type=f32)


def _dot_tn(a, b):
    return lax.dot_general(a, b, (((0,), (0,)), ((), ())), preferred_element_type=f32)


def _rms(x, g):
    return x * lax.rsqrt(jnp.mean(x * x, axis=-1, keepdims=True) + EPS) * g


def _gelu(x):
    return 0.5 * x * (1.0 + lax.erf(x * 0.7071067811865476))


def _layernorm(x, g, b):
    mu = jnp.mean(x, axis=-1, keepdims=True)
    xc = x - mu
    return xc * lax.rsqrt(jnp.mean(xc * xc, axis=-1, keepdims=True) + EPS) * g + b


def _sigmoid(x):
    return 1.0 / (1.0 + jnp.exp(-x))


def _softplus(x):
    return jnp.maximum(x, 0.0) + jnp.log(1.0 + jnp.exp(-jnp.abs(x)))


def _full(shape):
    nd = len(shape)
    return pl.BlockSpec(shape, lambda *_: (0,) * nd, pipeline_mode=pl.Buffered(1))


def _acc(shape):
    nd = len(shape)
    return pl.BlockSpec(shape, lambda *_: (0,) * nd)


def _rows(tm, ncols):
    return pl.BlockSpec((tm, ncols), lambda i: (i, 0))


def _params(sem):
    return pltpu.CompilerParams(dimension_semantics=sem, vmem_limit_bytes=VMEM_LIMIT)


def _row_tile(rows, target, mult=8):
    best = rows
    for t in range(mult, min(rows, target) + 1, mult):
        if rows % t == 0:
            best = t
    return best if best <= target else rows


def _ffn_fwd(name, h, g, wg, wu, wd, pre=None, tm=256):
    T, D = h.shape
    F = wg.shape[0]
    tm = min(tm, T)

    def body(*refs):
        if pre is None:
            h_ref, g_ref, wg_ref, wu_ref, wd_ref, ho_ref, n_ref, a_ref, b_ref = refs
            hin = h_ref[...]
        else:
            (h_ref, ya_ref, yb_ref, wo_ref, g_ref, wg_ref, wu_ref, wd_ref,
             hi_ref, ho_ref, n_ref, a_ref, b_ref) = refs
            ga = ya_ref.shape[1]
            hin = h_ref[...] + _dot(ya_ref[...], wo_ref[:ga, :]) + _dot(yb_ref[...], wo_ref[ga:, :])
            hi_ref[...] = hin
        n = _rms(hin, g_ref[...]).astype(bf16)
        n_ref[...] = n
        a = _dot_nt(n, wg_ref[...]).astype(bf16)
        b = _dot_nt(n, wu_ref[...]).astype(bf16)
        a_ref[...] = a
        b_ref[...] = b
        af = a.astype(f32)
        hm = (af * _sigmoid(af) * b.astype(f32)).astype(bf16)
        ho_ref[...] = hin + 0.5 * _dot(hm, wd_ref[...])

    ins = [h] + (list(pre) if pre is not None else []) + [g, wg, wu, wd]
    in_specs = [_rows(tm, D)]
    if pre is not None:
        in_specs += [_rows(tm, pre[0].shape[1]), _rows(tm, pre[1].shape[1]), _full(pre[2].shape)]
    in_specs += [_full(g.shape), _full(wg.shape), _full(wu.shape), _full(wd.shape)]
    outs = [jax.ShapeDtypeStruct((T, D), f32), jax.ShapeDtypeStruct((T, D), bf16),
            jax.ShapeDtypeStruct((T, F), bf16), jax.ShapeDtypeStruct((T, F), bf16)]
    out_specs = [_rows(tm, D), _rows(tm, D), _rows(tm, F), _rows(tm, F)]
    if pre is not None:
        outs = [jax.ShapeDtypeStruct((T, D), f32)] + outs
        out_specs = [_rows(tm, D)] + out_specs
    return pl.pallas_call(body, name=name, grid=(T // tm,), in_specs=in_specs, out_specs=out_specs,
                          out_shape=outs, compiler_params=_params(("parallel",)))(*ins)


def _ffn_bwd(name, dh, hin, g, a, b, wg, wu, wd, wo=None, ga=0, tm=256):
    T, D = dh.shape
    F = wg.shape[0]
    tm = min(tm, T)

    def body(*refs):
        if wo is None:
            (dh_ref, hin_ref, g_ref, a_ref, b_ref, wg_ref, wu_ref, wd_ref,
             dhi_ref, da_ref, db_ref, hm_ref, dg_ref) = refs
        else:
            (dh_ref, hin_ref, g_ref, a_ref, b_ref, wg_ref, wu_ref, wd_ref, wo_ref,
             dhi_ref, da_ref, db_ref, hm_ref, dg_ref, dya_ref, dyb_ref) = refs

        @pl.when(pl.program_id(0) == 0)
        def _():
            dg_ref[...] = jnp.zeros_like(dg_ref)

        dh_ = dh_ref[...]
        dhb = (0.5 * dh_).astype(bf16)
        dhm = _dot_nt(dhb, wd_ref[...])
        af = a_ref[...].astype(f32)
        bf = b_ref[...].astype(f32)
        sg = _sigmoid(af)
        sl_ = af * sg
        da = (dhm * bf * (sg * (1.0 + af * (1.0 - sg)))).astype(bf16)
        db = (dhm * sl_).astype(bf16)
        da_ref[...] = da
        db_ref[...] = db
        hm_ref[...] = (sl_ * bf).astype(bf16)
        dn = _dot(da, wg_ref[...]) + _dot(db, wu_ref[...])
        _, vjp = jax.vjp(_rms, hin_ref[...], g_ref[...])
        dx, dg = vjp(dn)
        dhi = dh_ + dx
        dhi_ref[...] = dhi
        dg_ref[...] += dg
        if wo is not None:
            dhib = dhi.astype(bf16)
            dya_ref[...] = _dot_nt(dhib, wo_ref[:ga, :]).astype(bf16)
            dyb_ref[...] = _dot_nt(dhib, wo_ref[ga:, :]).astype(bf16)

    ins = [dh, hin, g, a, b, wg, wu, wd]
    in_specs = [_rows(tm, D), _rows(tm, D), _full(g.shape), _rows(tm, F), _rows(tm, F),
                _full(wg.shape), _full(wu.shape), _full(wd.shape)]
    act = jax.ShapeDtypeStruct((T, F), bf16)
    outs = [jax.ShapeDtypeStruct((T, D), f32), act, act, act, jax.ShapeDtypeStruct(g.shape, f32)]
    out_specs = [_rows(tm, D), _rows(tm, F), _rows(tm, F), _rows(tm, F), _acc(g.shape)]
    if wo is not None:
        gb = wo.shape[0] - ga
        ins += [wo]
        in_specs += [_full(wo.shape)]
        outs += [jax.ShapeDtypeStruct((T, ga), bf16), jax.ShapeDtypeStruct((T, gb), bf16)]
        out_specs += [_rows(tm, ga), _rows(tm, gb)]
    return pl.pallas_call(body, name=name, grid=(T // tm,), in_specs=in_specs, out_specs=out_specs,
                          out_shape=outs, compiler_params=_params(("arbitrary",)))(*ins)


def _matmul_tn(name, a, b, scale=1.0, tk=2048):
    T, M = a.shape
    N = b.shape[1]
    tk = min(tk, T)
    nk = T // tk
    tn = LANES * max(d for d in range(1, N // LANES + 1) if (N // LANES) % d == 0 and (d == 1 or M * d * LANES * 4 <= 6 * 1024 * 1024))

    def body(a_ref, b_ref, o_ref, acc):
        k = pl.program_id(1)

        @pl.when(k == 0)
        def _():
            acc[...] = jnp.zeros_like(acc)

        bb = b_ref[...]
        if scale != 1.0:
            bb = bb * scale
        acc[...] += _dot_tn(a_ref[...].astype(bf16), bb.astype(bf16))

        @pl.when(k == nk - 1)
        def _():
            o_ref[...] = acc[...].astype(bf16)

    return pl.pallas_call(
        body, name=name, grid=(N // tn, nk),
        in_specs=[pl.BlockSpec((tk, M), lambda j, k: (k, 0)), pl.BlockSpec((tk, tn), lambda j, k: (k, j))],
        out_specs=pl.BlockSpec((M, tn), lambda j, k: (0, j)),
        out_shape=jax.ShapeDtypeStruct((M, N), bf16), scratch_shapes=[pltpu.VMEM((M, tn), f32)],
        compiler_params=_params(("parallel", "arbitrary")))(a, b)


def _gm_pre(u, v, ln_g, ln_b):
    return _gelu(u), _layernorm(_gelu(v), ln_g, ln_b)


def _tril_mask():
    r = lax.broadcasted_iota(jnp.int32, (CHUNK, CHUNK), 0)
    c = lax.broadcasted_iota(jnp.int32, (CHUNK, CHUNK), 1)
    return c <= r


def _gm_mix(vnb, ws_ref, bst, mixed_sc, tm):
    mask = _tril_mask()
    for h in range(GM_HEADS):
        wt = jnp.where(mask, ws_ref[h], 0.0).astype(bf16)
        bias = bst[:, h:h + 1]
        for q in range(tm // CHUNK):
            rs = slice(q * CHUNK, (q + 1) * CHUNK)
            cs = slice(h * CHUNK, (h + 1) * CHUNK)
            mixed_sc[rs, cs] = _dot(wt, vnb[rs, cs]) + bias


def _mix_fwd(h1, gmix, w_uv, w_zxd, ln_g, ln_b, w_s, b_st, gout, tm=512):
    T, D = h1.shape
    tm = min(tm, T)
    G = GM_WIDTH

    def body(h_ref, g_ref, wuv_ref, wzxd_ref, lng_ref, lnb_ref, ws_ref, bst_ref, gout_ref,
             n_ref, uv_ref, z_ref, xbc_ref, dt_ref, ya_ref, mixed_sc):
        n = _rms(h_ref[...], g_ref[...]).astype(bf16)
        n_ref[...] = n
        uv = _dot_nt(n, wuv_ref[...]).astype(bf16)
        uv_ref[...] = uv
        u, v = uv[:, :G], uv[:, G:]
        zxd = _dot_nt(n, wzxd_ref[...])
        z_ref[...] = zxd[:, :SSM_WIDTH].astype(bf16)
        xbc_ref[...] = zxd[:, SSM_WIDTH:SSM_WIDTH + CONV_DIM].astype(bf16)
        dt_ref[...] = zxd[:, SSM_WIDTH + CONV_DIM:]
        ug, vn = _gm_pre(u.astype(f32), v.astype(f32), lng_ref[...], lnb_ref[...])
        _gm_mix(vn.astype(bf16), ws_ref, bst_ref[...], mixed_sc, tm)
        ya_ref[...] = _rms(ug * mixed_sc[...], gout_ref[...]).astype(bf16)

    ins = [h1, gmix, w_uv, w_zxd, ln_g, ln_b, w_s, b_st, gout]
    in_specs = [_rows(tm, D)] + [_full(x.shape) for x in ins[1:]]
    outs = [jax.ShapeDtypeStruct((T, D), bf16), jax.ShapeDtypeStruct((T, 2 * G), bf16),
            jax.ShapeDtypeStruct((T, SSM_WIDTH), bf16), jax.ShapeDtypeStruct((T, CONV_DIM), bf16),
            jax.ShapeDtypeStruct((T, LANES), f32), jax.ShapeDtypeStruct((T, G), bf16)]
    out_specs = [_rows(tm, D), _rows(tm, 2 * G), _rows(tm, SSM_WIDTH), _rows(tm, CONV_DIM), _rows(tm, LANES), _rows(tm, G)]
    return pl.pallas_call(body, name="mix_fwd", grid=(T // tm,), in_specs=in_specs, out_specs=out_specs,
                          out_shape=outs, scratch_shapes=[pltpu.VMEM((tm, G), f32)],
                          compiler_params=_params(("parallel",)))(*ins)


def _mix_bwd(dh, h1, gmix, uv, dya, dzxd, w_uv, w_zxd, ln_g, ln_b, w_s, b_st, gout, tm=256):
    T, D = dh.shape
    tm = min(tm, T)
    G = GM_WIDTH

    def body(dh_ref, h_ref, g_ref, uv_ref, dya_ref, dzxd_ref, wuv_ref, wzxd_ref, lng_ref, lnb_ref, ws_ref, bst_ref, gout_ref,
             dhi_ref, duv_ref, dg_ref, dlng_ref, dlnb_ref, dws_ref, dbst_ref, dgout_ref, mixed_sc, dvn_sc):
        @pl.when(pl.program_id(0) == 0)
        def _():
            for r in (dg_ref, dlng_ref, dlnb_ref, dws_ref, dbst_ref, dgout_ref):
                r[...] = jnp.zeros_like(r)

        dn_z = _dot(dzxd_ref[...], wzxd_ref[...])
        u = uv_ref[:, :G].astype(f32)
        v = uv_ref[:, G:].astype(f32)
        (ug, vn), pre_vjp = jax.vjp(_gm_pre, u, v, lng_ref[...], lnb_ref[...])
        vnb = vn.astype(bf16)
        _gm_mix(vnb, ws_ref, bst_ref[...], mixed_sc, tm)
        mixed = mixed_sc[...]
        _, out_vjp = jax.vjp(_rms, ug * mixed, gout_ref[...])
        dpre, dgout = out_vjp(dya_ref[...].astype(f32))
        dgout_ref[...] += dgout
        dug = dpre * mixed
        dmixed = dpre * ug
        mask = _tril_mask()
        lane = lax.broadcasted_iota(jnp.int32, (1, GM_HEADS), 1)
        dbst = jnp.zeros((CHUNK, GM_HEADS), f32)
        for h in range(GM_HEADS):
            wt = jnp.where(mask, ws_ref[h], 0.0).astype(bf16)
            cs = slice(h * CHUNK, (h + 1) * CHUNK)
            dw = jnp.zeros((CHUNK, CHUNK), f32)
            for q in range(tm // CHUNK):
                rs = slice(q * CHUNK, (q + 1) * CHUNK)
                dm = dmixed[rs, cs]
                dmb = dm.astype(bf16)
                dw = dw + _dot_nt(dmb, vnb[rs, cs])
                dbst = dbst + jnp.sum(dm, axis=1, keepdims=True) * (lane == h).astype(f32)
                dvn_sc[rs, cs] = _dot_tn(wt, dmb)
            dws_ref[h] += jnp.where(mask, dw, 0.0)
        dbst_ref[...] += dbst
        du, dv, dlng, dlnb = pre_vjp((dug, dvn_sc[...]))
        duv = jnp.concatenate([du.astype(bf16), dv.astype(bf16)], axis=1)
        duv_ref[...] = duv
        dlng_ref[...] += dlng
        dlnb_ref[...] += dlnb
        dn = dn_z + _dot(duv, wuv_ref[...])
        _, vjp = jax.vjp(_rms, h_ref[...], g_ref[...])
        dx, dg = vjp(dn)
        dhi_ref[...] = dh_ref[...] + dx
        dg_ref[...] += dg

    ins = [dh, h1, gmix, uv, dya, dzxd, w_uv, w_zxd, ln_g, ln_b, w_s, b_st, gout]
    in_specs = ([_rows(tm, D), _rows(tm, D), _full(gmix.shape), _rows(tm, 2 * G), _rows(tm, G), _rows(tm, dzxd.shape[1])]
                + [_full(x.shape) for x in ins[6:]])
    accs = (gmix, ln_g, ln_b, w_s, b_st, gout)
    outs = ([jax.ShapeDtypeStruct((T, D), f32), jax.ShapeDtypeStruct((T, 2 * G), bf16)]
            + [jax.ShapeDtypeStruct(x.shape, f32) for x in accs])
    out_specs = [_rows(tm, D), _rows(tm, 2 * G)] + [_acc(x.shape) for x in accs]
    return pl.pallas_call(body, name="mix_bwd", grid=(T // tm,), in_specs=in_specs, out_specs=out_specs,
                          out_shape=outs, scratch_shapes=[pltpu.VMEM((tm, G), f32), pltpu.VMEM((tm, G), f32)],
                          compiler_params=_params(("arbitrary",)))(*ins)


HALO = 16
PAIRS = SSM_HEADS // 2
PAIR_W = 2 * SSM_HEAD_DIM


def _split(x, n):
    parts = []
    for _ in range(n):
        p = x.astype(bf16)
        parts.append(p)
        x = x - p.astype(f32)
    return parts


def _dot_sel(x, sel_n, n):
    return _dot(jnp.concatenate(_split(x, n), axis=1), sel_n)


def _sel_dot(sel, x, n):
    return _dot(jnp.concatenate([sel] * n, axis=1), jnp.concatenate(_split(x, n), axis=0))


EXPAND_SPLIT = 3
REDUCE_SPLIT = 2


def _head_mats():
    ex = (jnp.arange(SSM_WIDTH)[None, :] // SSM_HEAD_DIM == jnp.arange(LANES)[:, None]).astype(bf16)
    return jnp.tile(ex, (EXPAND_SPLIT, 1)), jnp.tile(ex.T, (REDUCE_SPLIT, 1))


def _shift_mat(rows, cols, off):
    r = lax.broadcasted_iota(jnp.int32, (rows, cols), 0)
    c = lax.broadcasted_iota(jnp.int32, (rows, cols), 1)
    return (c == r + off).astype(bf16)


def _ssd_conv(c, xbc_ref, halo_ref, cw_ref, cb_ref):
    halo = halo_ref[...]
    ext = jnp.concatenate([jnp.where(c > 0, halo, jnp.zeros_like(halo)), xbc_ref[...]], axis=0)
    xc = cb_ref[...] + cw_ref[SSM_CONV - 1:SSM_CONV, :] * xbc_ref[...].astype(f32)
    for j in range(SSM_CONV - 1):
        xc = xc + cw_ref[j:j + 1, :] * _dot(_shift_mat(CHUNK, HALO + CHUNK, HALO - SSM_CONV + 1 + j), ext)
    return xc


def _ssd_front(dtr_ref, dtb_ref, alog_ref):
    dt = _softplus(dtr_ref[...] + dtb_ref[...])
    a = -jnp.exp(alog_ref[...])
    acs = jnp.dot(_tril_mask().astype(f32), dt * a, preferred_element_type=f32, precision=HIGHEST)
    return dt, a, acs


def _ssd_wide(xa, dt, acs, dsk, ex):
    dt_x = _dot_sel(dt, ex, EXPAND_SPLIT)
    acs_x = _dot_sel(acs, ex, EXPAND_SPLIT)
    dsk_x = _dot_sel(jnp.broadcast_to(dsk, (8, LANES)), ex, EXPAND_SPLIT)[0:1]
    e_x = jnp.exp(acs_x)
    r_x = jnp.exp(acs_x[CHUNK - 1:CHUNK, :] - acs_x)
    xs = xa[:, :SSM_WIDTH]
    xd = xs * dt_x
    return dt_x, dsk_x, e_x, r_x, xs, xd, xd * r_x


def _pair_stack(v, lo):
    return jnp.concatenate([jnp.where(lo, v, 0.0), jnp.where(lo, 0.0, v)], axis=0)


def _ssd_pair(j, acs, acs_t, cb):
    out = []
    tril = _tril_mask()
    for h in (2 * j, 2 * j + 1):
        dk = jnp.exp(jnp.where(tril, acs[:, h:h + 1] - acs_t[h:h + 1, :], -jnp.inf))
        out.append((dk, cb * dk))
    return out


def _pair_col(row_lo, tot, j):
    return jnp.exp(jnp.where(row_lo, tot[:, 2 * j:2 * j + 1], tot[:, 2 * j + 1:2 * j + 2]))


def _gated_norm(y, z, g):
    yg = y * (z * _sigmoid(z))
    half = SSM_WIDTH // SSM_GROUPS
    parts = []
    for k in range(SSM_GROUPS):
        s = yg[:, k * half:(k + 1) * half]
        parts.append(s * lax.rsqrt(jnp.mean(s * s, axis=-1, keepdims=True) + EPS))
    return jnp.concatenate(parts, axis=1) * g


def _group_mats(xa):
    out = []
    for g in range(SSM_GROUPS):
        bm = xa[:, SSM_WIDTH + g * SSM_STATE:SSM_WIDTH + (g + 1) * SSM_STATE].astype(bf16)
        cm = xa[:, SSM_WIDTH + (SSM_GROUPS + g) * SSM_STATE:SSM_WIDTH + (SSM_GROUPS + g + 1) * SSM_STATE].astype(bf16)
        out.append((cm, bm, _dot_nt(cm, bm)))
    return out


def _ssd_fwd(xbc, z, dtr, conv_w, conv_b, dt_bias, a_log, d_skip, ssm_norm):
    T = xbc.shape[0]
    nc = T // CHUNK
    N = SSM_STATE

    def body(xbc_ref, halo_ref, z_ref, dtr_ref, cw_ref, cb_ref, dtb_ref, alog_ref, dsk_ref, g_ref, ex_ref,
             yb_ref, xc_ref, sg_ref, y_ref, sprev_ref, s_sc):
        c = pl.program_id(0)

        @pl.when(c == 0)
        def _():
            s_sc[...] = jnp.zeros_like(s_sc)

        xc = _ssd_conv(c, xbc_ref, halo_ref, cw_ref, cb_ref)
        sg = _sigmoid(xc)
        xc_ref[...] = xc
        sg_ref[...] = sg
        xa = xc * sg
        dt, _, acs = _ssd_front(dtr_ref, dtb_ref, alog_ref)
        _, dsk_x, e_x, _, xs, xd, gm = _ssd_wide(xa, dt, acs, dsk_ref[...], ex_ref[...])
        acs_t = acs.T
        tot = acs[CHUNK - 1:CHUNK, :]
        groups = _group_mats(xa)
        lo = lax.broadcasted_iota(jnp.int32, (CHUNK, PAIR_W), 1) < SSM_HEAD_DIM
        row_lo = lax.broadcasted_iota(jnp.int32, (PAIR_W, 1), 0) < SSM_HEAD_DIM
        ys = []
        for j in range(PAIRS):
            cmb, bmb, cb = groups[j // (PAIRS // SSM_GROUPS)]
            ps = slice(j * PAIR_W, (j + 1) * PAIR_W)
            (_, m0), (_, m1) = _ssd_pair(j, acs, acs_t, cb)
            sp = s_sc[j]
            yd = _dot(jnp.concatenate([m0, m1], axis=1).astype(bf16), _pair_stack(xd[:, ps], lo).astype(bf16))
            ys.append(yd + e_x[:, ps] * _dot_nt(cmb, sp.astype(bf16)))
            sprev_ref[0, j] = sp
            s_sc[j] = _pair_col(row_lo, tot, j) * sp + _dot_tn(gm[:, ps].astype(bf16), bmb)
        y = jnp.concatenate(ys, axis=1) + xs * dsk_x
        y_ref[...] = y
        yb_ref[...] = _gated_norm(y, z_ref[...].astype(f32), g_ref[...]).astype(bf16)

    params = [conv_w, conv_b, dt_bias, a_log, d_skip, ssm_norm, _head_mats()[0]]
    hp = CHUNK // HALO
    in_specs = [_rows(CHUNK, CONV_DIM), pl.BlockSpec((HALO, CONV_DIM), lambda i: (jnp.maximum(i * hp - 1, 0), 0)),
                _rows(CHUNK, SSM_WIDTH), _rows(CHUNK, LANES)] + [_full(x.shape) for x in params]
    return pl.pallas_call(
        body, name="ssd_fwd", grid=(nc,), in_specs=in_specs,
        out_specs=[_rows(CHUNK, SSM_WIDTH), _rows(CHUNK, CONV_DIM), _rows(CHUNK, CONV_DIM), _rows(CHUNK, SSM_WIDTH),
                   pl.BlockSpec((1, PAIRS, PAIR_W, N), lambda i: (i, 0, 0, 0))],
        out_shape=[jax.ShapeDtypeStruct((T, SSM_WIDTH), bf16), jax.ShapeDtypeStruct((T, CONV_DIM), f32),
                   jax.ShapeDtypeStruct((T, CONV_DIM), f32), jax.ShapeDtypeStruct((T, SSM_WIDTH), f32),
                   jax.ShapeDtypeStruct((nc, PAIRS, PAIR_W, N), f32)],
        scratch_shapes=[pltpu.VMEM((PAIRS, PAIR_W, N), f32)],
        compiler_params=_params(("arbitrary",)))(xbc, xbc, z, dtr, *params)


def _ssd_bwd(xbc, xc, sg, y, z, dtr, sprev, dyb, conv_w, conv_b, dt_bias, a_log, d_skip, ssm_norm):
    T = xbc.shape[0]
    nc = T // CHUNK
    H, N = SSM_HEADS, SSM_STATE
    PG = PAIRS // SSM_GROUPS

    def body(xbc_ref, xc_ref, sg_ref, y_ref, z_ref, dtr_ref, sprev_ref, dyb_ref, cw_ref, cb_ref, dtb_ref, alog_ref, dsk_ref,
             g_ref, ex_ref, rd_ref, dzxd_ref, dcw_ref, dcb_ref, ddtb_ref, dalog_ref, ddsk_ref, dg_ref, ds_sc, next_sc):
        i = pl.program_id(0)

        @pl.when(i == 0)
        def _():
            ds_sc[...] = jnp.zeros_like(ds_sc)
            next_sc[...] = jnp.zeros_like(next_sc)
            for r_ in (dcw_ref, dcb_ref, ddtb_ref, dalog_ref, ddsk_ref, dg_ref):
                r_[...] = jnp.zeros_like(r_)

        xc = xc_ref[...]
        sg = sg_ref[...]
        xa = xc * sg
        dt, a, acs = _ssd_front(dtr_ref, dtb_ref, alog_ref)
        dt_x, dsk_x, e_x, r_x, xs, xd, gm = _ssd_wide(xa, dt, acs, dsk_ref[...], ex_ref[...])
        acs_t = acs.T
        tot = acs[CHUNK - 1:CHUNK, :]
        groups = _group_mats(xa)
        lo = lax.broadcasted_iota(jnp.int32, (CHUNK, PAIR_W), 1) < SSM_HEAD_DIM
        row_lo = lax.broadcasted_iota(jnp.int32, (PAIR_W, 1), 0) < SSM_HEAD_DIM
        pairs, zs = [], []
        for j in range(PAIRS):
            cmb, _, cb = groups[j // PG]
            pairs.append(_ssd_pair(j, acs, acs_t, cb))
            zs.append(_dot_nt(cmb, sprev_ref[0, j].astype(bf16)))
        zf = jnp.concatenate(zs, axis=1)
        _, gn_vjp = jax.vjp(_gated_norm, y_ref[...], z_ref[...].astype(f32), g_ref[...])
        dy, dz, dg = gn_vjp(dyb_ref[...].astype(f32))
        dg_ref[...] += dg
        dzxd_ref[:, :SSM_WIDTH] = dz.astype(bf16)

        lane = lax.broadcasted_iota(jnp.int32, (1, LANES), 1)
        sub = lax.broadcasted_iota(jnp.int32, (LANES, 1), 0)
        dacs = jnp.zeros((CHUNK, LANES), f32)
        dacs_r = jnp.zeros((LANES, CHUNK), f32)
        dtot = jnp.zeros((1, LANES), f32)
        dcb = [jnp.zeros((CHUNK, CHUNK), f32) for _ in range(SSM_GROUPS)]
        dcm = [jnp.zeros((CHUNK, N), f32) for _ in range(SSM_GROUPS)]
        dbm = [jnp.zeros((CHUNK, N), f32) for _ in range(SSM_GROUPS)]
        dxds, dgms = [], []
        for j in range(PAIRS):
            g = j // PG
            cmb, bmb, _ = groups[g]
            ps = slice(j * PAIR_W, (j + 1) * PAIR_W)
            (dk0, m0), (dk1, m1) = pairs[j]
            oh0, oh1 = (lane == 2 * j).astype(f32), (lane == 2 * j + 1).astype(f32)
            dyp = dy[:, ps]
            dy2 = _pair_stack(dyp, lo).astype(bf16)
            dm2 = _dot_nt(dy2, xd[:, ps].astype(bf16))
            m2 = jnp.concatenate([m0, m1], axis=0)
            dxds.append(_dot_tn(m2.astype(bf16), dy2))
            w2 = dm2 * m2
            rs = jnp.sum(w2, axis=1, keepdims=True)
            dacs = dacs + rs[:CHUNK] * oh0 + rs[CHUNK:] * oh1
            dacs_r = dacs_r - ((sub == 2 * j).astype(f32) * jnp.sum(w2[:CHUNK], axis=0, keepdims=True)
                               + (sub == 2 * j + 1).astype(f32) * jnp.sum(w2[CHUNK:], axis=0, keepdims=True))
            dcb[g] = dcb[g] + dm2[:CHUNK] * dk0 + dm2[CHUNK:] * dk1
            sp = sprev_ref[0, j]
            dzb = (dyp * e_x[:, ps]).astype(bf16)
            dcm[g] = dcm[g] + _dot(dzb, sp.astype(bf16))
            dsn = ds_sc[j]
            dsnb = dsn.astype(bf16)
            et = _pair_col(row_lo, tot, j)
            rr = jnp.sum(dsn * sp, axis=1, keepdims=True) * et
            dtot = dtot + jnp.sum(rr[:SSM_HEAD_DIM]) * oh0 + jnp.sum(rr[SSM_HEAD_DIM:]) * oh1
            dgms.append(_dot_nt(bmb, dsnb))
            dbm[g] = dbm[g] + _dot(gm[:, ps].astype(bf16), dsnb)
            ds_sc[j] = _dot_tn(dzb, cmb) + et * dsn
        dgm = jnp.concatenate(dgms, axis=1)
        dxd = jnp.concatenate(dxds, axis=1) + dgm * r_x
        dr = dgm * gm
        red = _dot_sel(jnp.concatenate([dy * e_x * zf - dr, dr, dxd * xs, dy * xs], axis=0), rd_ref[...], REDUCE_SPLIT)
        rowi = lax.broadcasted_iota(jnp.int32, (CHUNK, 1), 0)
        dtot = dtot + jnp.sum(red[CHUNK:2 * CHUNK], axis=0, keepdims=True)
        dacs = dacs + red[:CHUNK] + dacs_r.T + jnp.where(rowi == CHUNK - 1, dtot, 0.0)
        r2 = lax.broadcasted_iota(jnp.int32, (CHUNK, CHUNK), 0)
        c2 = lax.broadcasted_iota(jnp.int32, (CHUNK, CHUNK), 1)
        dadt = jnp.dot((c2 >= r2).astype(f32), dacs, preferred_element_type=f32, precision=HIGHEST)
        ddt = red[2 * CHUNK:3 * CHUNK] + dadt * a
        dalog_ref[...] += jnp.sum(dadt * dt, axis=0, keepdims=True) * a
        ddsk_ref[...] += jnp.sum(red[3 * CHUNK:], axis=0, keepdims=True)
        ddtr = jnp.where(lane < H, ddt * _sigmoid(dtr_ref[...] + dtb_ref[...]), 0.0)
        ddtb_ref[...] += jnp.sum(ddtr, axis=0, keepdims=True)
        dzxd_ref[:, SSM_WIDTH + CONV_DIM:] = ddtr.astype(bf16)
        dxa_bm, dxa_cm = [], []
        for g in range(SSM_GROUPS):
            cmb, bmb, _ = groups[g]
            dcbb = dcb[g].astype(bf16)
            dxa_bm.append(dbm[g] + _dot_tn(dcbb, cmb))
            dxa_cm.append(dcm[g] + _dot(dcbb, bmb))
        dxc = jnp.concatenate([dy * dsk_x + dxd * dt_x] + dxa_bm + dxa_cm, axis=1) * (sg * (1.0 + xc * (1.0 - sg)))
        ext = jnp.concatenate([dxc, next_sc[...]], axis=0)
        xin = xbc_ref[...].astype(f32)
        dxbc = cw_ref[SSM_CONV - 1:SSM_CONV, :] * dxc
        dcw = [jnp.sum(dxc * xin, axis=0, keepdims=True)]
        for s in range(1, SSM_CONV):
            later = _sel_dot(_shift_mat(CHUNK, CHUNK + HALO, s), ext, 2)
            dxbc = dxbc + cw_ref[SSM_CONV - 1 - s:SSM_CONV - s, :] * later
            dcw.insert(0, jnp.sum(later * xin, axis=0, keepdims=True))
        dzxd_ref[:, SSM_WIDTH:SSM_WIDTH + CONV_DIM] = dxbc.astype(bf16)
        dcw_ref[...] += jnp.concatenate(dcw, axis=0)
        dcb_ref[...] += jnp.sum(dxc, axis=0, keepdims=True)
        next_sc[...] = dxc[0:HALO, :]

    params = [conv_w, conv_b, dt_bias, a_log, d_skip, ssm_norm]
    mats = list(_head_mats())

    def rev(ncols):
        return pl.BlockSpec((CHUNK, ncols), lambda i: (nc - 1 - i, 0))

    in_specs = ([rev(CONV_DIM), rev(CONV_DIM), rev(CONV_DIM), rev(SSM_WIDTH), rev(SSM_WIDTH), rev(LANES),
                 pl.BlockSpec((1, PAIRS, PAIR_W, N), lambda i: (nc - 1 - i, 0, 0, 0)), rev(SSM_WIDTH)]
                + [_full(x.shape) for x in params + mats])
    return pl.pallas_call(
        body, name="ssd_bwd", grid=(nc,), in_specs=in_specs,
        out_specs=[rev(ZXD)] + [_acc(x.shape) for x in params],
        out_shape=[jax.ShapeDtypeStruct((T, ZXD), bf16)] + [jax.ShapeDtypeStruct(x.shape, f32) for x in params],
        scratch_shapes=[pltpu.VMEM((PAIRS, PAIR_W, N), f32), pltpu.VMEM((HALO, CONV_DIM), f32)],
        compiler_params=_params(("arbitrary",)))(xbc, xc, sg, y, z, dtr, sprev, dyb, *params, *mats)


def _tail(h3, p, tgt, gp, wpg, bpg, wpp, gf, tm=512):
    T, D = h3.shape
    tm = min(tm, T)

    def head(gpre, pp, h, gf_, t):
        gate = _sigmoid(gpre)
        y = _rms(h + gate * pp, gf_)
        err = y - t
        return 0.5 * jnp.sum(jnp.mean(err * err, axis=-1))

    def body(h_ref, p_ref, t_ref, gp_ref, wpg_ref, bpg_ref, wpp_ref, gf_ref,
             dh_ref, loss_ref, dgp_ref, dwpg_ref, dbpg_ref, dwpp_ref, dgf_ref):
        @pl.when(pl.program_id(0) == 0)
        def _():
            for r in (loss_ref, dgp_ref, dwpg_ref, dbpg_ref, dwpp_ref, dgf_ref):
                r[...] = jnp.zeros_like(r)

        h = h_ref[...]
        npf, np_vjp = jax.vjp(_rms, h, gp_ref[...])
        npb = npf.astype(bf16)
        pb = p_ref[...].astype(bf16)
        gpre = _dot(npb, wpg_ref[...]) + bpg_ref[...]
        kp, _, cp = wpp_ref.shape
        pp = jnp.concatenate([_dot(pb, wpp_ref[k]) for k in range(kp)], axis=1)
        loss, head_vjp = jax.vjp(head, gpre, pp, h, gf_ref[...], t_ref[...])
        dgpre, dpp, dh_a, dgf, _ = head_vjp(jnp.ones((), f32))
        loss_ref[...] += loss
        dgf_ref[...] += dgf
        dbpg_ref[...] += jnp.sum(dgpre, axis=0, keepdims=True)
        dgb = dgpre.astype(bf16)
        dwpg_ref[...] += _dot_tn(npb, dgb)
        dppb = dpp.astype(bf16)
        for k in range(kp):
            dwpp_ref[k] += _dot_tn(pb, dppb[:, k * cp:(k + 1) * cp])
        dh_b, dgp = np_vjp(_dot_nt(dgb, wpg_ref[...]))
        dgp_ref[...] += dgp
        dh_ref[...] = dh_a + dh_b

    ins = [h3, p, tgt, gp, wpg, bpg, wpp, gf]
    in_specs = [_rows(tm, D), _rows(tm, p.shape[1]), _rows(tm, D)] + [_full(x.shape) for x in ins[3:]]
    acc_shapes = [(1, LANES), gp.shape, wpg.shape, bpg.shape, wpp.shape, gf.shape]
    return pl.pallas_call(
        body, name="tail", grid=(T // tm,), in_specs=in_specs,
        out_specs=[_rows(tm, D)] + [_acc(s) for s in acc_shapes],
        out_shape=[jax.ShapeDtypeStruct((T, D), f32)] + [jax.ShapeDtypeStruct(s, f32) for s in acc_shapes],
        compiler_params=_params(("arbitrary",)))(*ins)


def _adamw(name, w, g, m, v, tr=256):
    R, C = w.shape
    tr = _row_tile(R, tr)

    def body(w_ref, g_ref, m_ref, v_ref, d_ref, mo_ref, vo_ref):
        g_ = g_ref[...]
        m_ = ADAM_B1 * m_ref[...] + (1.0 - ADAM_B1) * g_
        v_ = ADAM_B2 * v_ref[...] + (1.0 - ADAM_B2) * jnp.square(g_)
        m_hat = m_ / (1.0 - ADAM_B1 ** ADAM_STEP)
        v_hat = v_ / (1.0 - ADAM_B2 ** ADAM_STEP)
        d_ref[...] = -ADAM_LR * (m_hat / (jnp.sqrt(v_hat) + ADAM_EPS) + ADAM_WD * w_ref[...])
        mo_ref[...] = m_
        vo_ref[...] = v_

    spec = pl.BlockSpec((tr, C), lambda i: (i, 0))
    return pl.pallas_call(body, name=name, grid=(R // tr,), in_specs=[spec] * 4, out_specs=[spec] * 3,
                          out_shape=[jax.ShapeDtypeStruct((R, C), f32)] * 3,
                          compiler_params=_params(("parallel",)))(w, g, m, v)


HBM = pl.BlockSpec(memory_space=pltpu.HBM)


def _me():
    return lax.axis_index("x"), lax.axis_index("y"), lax.axis_index("c")


def _other_chips(x, y):
    return [(1 - x, y), (x, 1 - y), (1 - x, 1 - y)]


def _remote(src, dst, send_sem, recv_sem, dev):
    return pltpu.make_async_remote_copy(src_ref=src, dst_ref=dst, send_sem=send_sem, recv_sem=recv_sem,
                                        device_id=dev, device_id_type=MESH)


def _sems(n):
    return [pltpu.SemaphoreType.DMA((n,)), pltpu.SemaphoreType.DMA((n,))]


def _gather_weights(shards, split):
    n = len(shards)

    def body(*refs):
        ins, outs = refs[:n], refs[n:2 * n]
        own_send, own_recv, ici_send, ici_recv, d2d_send, d2d_recv = refs[2 * n:]
        x, y, c = _me()
        my_chip = 2 * x + y
        sibling = (x, y, 1 - c)
        chips = _other_chips(x, y)

        def rows(i, half):
            hr = shards[i].shape[0] // 2
            return pl.ds(half * hr, hr) if split[i] else pl.ds(0, shards[i].shape[0])

        sends = []
        for i in range(n):
            for j, chip in enumerate(chips):
                cp = _remote(ins[i].at[rows(i, c)], outs[i].at[my_chip, rows(i, c)],
                             ici_send.at[3 * i + j], ici_recv.at[3 * i + j], (*chip, c))
                cp.start()
                sends.append(cp)
            cp = _remote(ins[i], outs[i].at[my_chip], own_send.at[i], own_recv.at[i], sibling)
            cp.start()
            sends.append(cp)
        for i in range(n):
            for j, chip in enumerate(chips):
                s = 3 * i + j
                land = outs[i].at[2 * chip[0] + chip[1], rows(i, c)]
                _remote(land, land, ici_send.at[s], ici_recv.at[s], (*chip, c)).wait_recv()
                if split[i]:
                    cp = _remote(land, land, d2d_send.at[s], d2d_recv.at[s], sibling)
                    cp.start()
                    sends.append(cp)
        for i in range(n):
            _remote(ins[i], outs[i].at[my_chip], own_send.at[i], own_recv.at[i], sibling).wait_recv()
            if split[i]:
                for j, chip in enumerate(chips):
                    s = 3 * i + j
                    land = outs[i].at[2 * chip[0] + chip[1], rows(i, 1 - c)]
                    _remote(land, land, d2d_send.at[s], d2d_recv.at[s], sibling).wait_recv()
        for cp in sends:
            cp.wait_send()

    return pl.pallas_call(
        body, name="gather_weights", out_shape=[jax.ShapeDtypeStruct((N_CHIPS,) + s.shape, s.dtype) for s in shards],
        in_specs=[HBM] * n, out_specs=[HBM] * n,
        scratch_shapes=_sems(n) + _sems(3 * n) + _sems(3 * n))(*shards)


def _swap_halves(name, grads):
    n = len(grads)

    def body(*refs):
        ins, outs, send, recv = refs[:n], refs[n:2 * n], refs[2 * n], refs[2 * n + 1]
        x, y, c = _me()
        copies = []
        for i in range(n):
            hr = grads[i].shape[1] // 2
            cp = _remote(ins[i].at[:, pl.ds((1 - c) * hr, hr), :], outs[i], send.at[i], recv.at[i], (x, y, 1 - c))
            cp.start()
            copies.append(cp)
        for cp in copies:
            cp.wait()

    return pl.pallas_call(
        body, name=name,
        out_shape=[jax.ShapeDtypeStruct((g.shape[0], g.shape[1] // 2, g.shape[2]), g.dtype) for g in grads],
        in_specs=[HBM] * n, out_specs=[HBM] * n, scratch_shapes=_sems(n))(*grads)


def _add_halves(name, grads, other, c_idx, th=HALF_ROWS_BF16):
    K, R, C = grads.shape
    H = R // 2
    th = _row_tile(H, th, 16)
    nb = H // th

    def body(c_ref, g_ref, o_ref, out_ref):
        out_ref[...] = (g_ref[...].astype(f32) + o_ref[...].astype(f32)).astype(bf16)

    grid_spec = pltpu.PrefetchScalarGridSpec(
        num_scalar_prefetch=1, grid=(nb,),
        in_specs=[pl.BlockSpec((K, th, C), lambda i, c: (0, c[0] * nb + i, 0)),
                  pl.BlockSpec((K, th, C), lambda i, c: (0, i, 0))],
        out_specs=pl.BlockSpec((K, th, C), lambda i, c: (0, i, 0)))
    return pl.pallas_call(body, name=name, grid_spec=grid_spec,
                          out_shape=jax.ShapeDtypeStruct((K, H, C), bf16),
                          compiler_params=_params(("parallel",)))(c_idx, grads, other)


SEM = pl.BlockSpec(memory_space=pltpu.SEMAPHORE)
ANY = pl.BlockSpec(memory_space=pl.ANY)
EFFECT = pltpu.SideEffectType.DATAFLOW_SIDE_EFFECTING


def _copies_start(name, srcs, land_shapes, n_copies, make_copies, after):
    ns, nl = len(srcs), len(land_shapes)
    lands = [lax.empty(s.shape, s.dtype) for s in land_shapes]

    def body(*refs):
        src_refs, land_refs = refs[:ns], refs[ns:ns + nl]
        send, recv, token = refs[ns + nl + 1], refs[ns + nl + 2], refs[-1]
        for cp in make_copies(src_refs, land_refs, send, recv):
            cp.start()
        token[...] = jnp.zeros_like(token)

    buffers = list(srcs) + lands
    out = pl.pallas_call(
        body, name=name,
        out_shape=(pltpu.SemaphoreType.DMA((n_copies,)), pltpu.SemaphoreType.DMA((n_copies,)),
                   *[pltpu.HBM(b.shape, b.dtype) for b in buffers], jax.ShapeDtypeStruct((8, LANES), f32)),
        in_specs=[HBM] * (ns + nl) + [ANY],
        out_specs=(SEM, SEM, *[HBM] * (ns + nl), pl.BlockSpec(memory_space=pltpu.VMEM)),
        input_output_aliases={i: 2 + i for i in range(ns + nl)},
        compiler_params=pltpu.CompilerParams(has_side_effects=EFFECT),
    )(*[pltpu.with_memory_space_constraint(b, pltpu.HBM) for b in buffers], after)
    return out[0], out[1], list(out[2:2 + ns]), list(out[2 + ns:2 + ns + nl]), out[-1]


def _copies_wait(name, started, make_copies, after):
    send, recv, srcs, lands, _ = started
    ns, nl = len(srcs), len(lands)
    after = list(after)

    def body(*refs):
        src_refs, land_refs = refs[:ns], refs[ns:ns + nl]
        for cp in make_copies(src_refs, land_refs, refs[ns + nl], refs[ns + nl + 1]):
            cp.wait_send()
            cp.wait_recv()

    buffers = list(srcs) + list(lands)
    out = pl.pallas_call(
        body, name=name, out_shape=tuple(pltpu.HBM(b.shape, b.dtype) for b in buffers),
        in_specs=[HBM] * (ns + nl) + [SEM, SEM] + [ANY] * len(after), out_specs=tuple([HBM] * (ns + nl)),
        input_output_aliases={i: i for i in range(ns + nl)},
        compiler_params=pltpu.CompilerParams(has_side_effects=EFFECT),
    )(*buffers, send, recv, *after)
    return list(out[:ns]), list(out[ns:])


def _gather_copies(src_refs, land_refs, send, recv):
    x, y, c = _me()
    my_chip = 2 * x + y
    peers = [(*chip, c) for chip in _other_chips(x, y)] + [(x, y, 1 - c)]
    return [_remote(src_refs[i], land_refs[i].at[my_chip], send.at[4 * i + j], recv.at[4 * i + j], peer)
            for i in range(len(src_refs)) for j, peer in enumerate(peers)]


def _partial_copies(src_refs, land_refs, send, recv):
    x, y, c = _me()
    return [_remote(src_refs[i].at[2 * chip[0] + chip[1]], land_refs[i].at[j], send.at[3 * i + j], recv.at[3 * i + j], (*chip, c))
            for i in range(len(src_refs)) for j, chip in enumerate(_other_chips(x, y))]


def _small_copies(src_refs, land_refs, send, recv):
    x, y, c = _me()
    return [_remote(src_refs[0], land_refs[0].at[k - 1], send.at[k - 1], recv.at[k - 1], (x ^ (k >> 2), y ^ ((k >> 1) & 1), c ^ (k & 1)))
            for k in range(1, N_DEV)]


def _sum_small(own, slots, dev_idx):
    R, C = own.shape

    def body(dev_ref, own_ref, s_ref, o_ref):
        me = dev_ref[0]
        acc = jnp.zeros((R, C), f32)
        for d in range(N_DEV):
            k = me ^ d
            acc = acc + jnp.where(k == 0, own_ref[...], s_ref[jnp.maximum(k - 1, 0)])
        o_ref[...] = acc

    grid_spec = pltpu.PrefetchScalarGridSpec(
        num_scalar_prefetch=1, grid=(1,),
        in_specs=[pl.BlockSpec((R, C), lambda i, dev: (0, 0)), pl.BlockSpec((N_DEV - 1, R, C), lambda i, dev: (0, 0, 0))],
        out_specs=pl.BlockSpec((R, C), lambda i, dev: (0, 0)))
    return pl.pallas_call(body, name="sum_small", grid_spec=grid_spec, out_shape=jax.ShapeDtypeStruct((R, C), f32),
                          compiler_params=_params(("arbitrary",)))(dev_idx, own, slots)


def _sum_partials(name, part, recv, chip_idx, th=HALF_ROWS_BF16):
    K, H, C = part.shape
    th = _row_tile(H, th, 16)

    def body(chip_ref, p_ref, r_ref, o_ref):
        acc = p_ref[...].astype(f32)
        for j in range(3):
            acc = acc + r_ref[j].astype(f32)
        o_ref[...] = acc

    grid_spec = pltpu.PrefetchScalarGridSpec(
        num_scalar_prefetch=1, grid=(H // th,),
        in_specs=[pl.BlockSpec((None, th, C), lambda i, chip: (chip[0], i, 0)),
                  pl.BlockSpec((3, th, C), lambda i, chip: (0, i, 0))],
        out_specs=pl.BlockSpec((th, C), lambda i, chip: (i, 0)))
    return pl.pallas_call(body, name=name, grid_spec=grid_spec, out_shape=jax.ShapeDtypeStruct((H, C), f32),
                          compiler_params=_params(("parallel",)))(chip_idx, part, recv)


def _share_halves(name, halves):
    n = len(halves)

    def body(*refs):
        ins, outs, send, recv = refs[:n], refs[n:2 * n], refs[2 * n], refs[2 * n + 1]
        x, y, c = _me()
        copies = []
        for i in range(n):
            cp = _remote(ins[i], outs[i], send.at[i], recv.at[i], (x, y, 1 - c))
            cp.start()
            copies.append(cp)
        for cp in copies:
            cp.wait()

    return pl.pallas_call(
        body, name=name, out_shape=[jax.ShapeDtypeStruct(h.shape, h.dtype) for h in halves],
        in_specs=[HBM] * n, out_specs=[HBM] * n, scratch_shapes=_sems(n))(*halves)


def _adamw_big(name, w, g_mine, g_theirs, m, v, c_idx, tr=HALF_ROWS_F32):
    R, C = w.shape
    H = R // 2
    tr = _row_tile(H, tr)
    nb = H // tr

    def body(c_ref, w_ref, gm_ref, gt_ref, m_ref, v_ref, g_ref, d_ref, mo_ref, vo_ref):
        g_ = jnp.where(pl.program_id(0) // nb == c_ref[0], gm_ref[...], gt_ref[...])
        g_ref[...] = g_
        m_ = ADAM_B1 * m_ref[...] + (1.0 - ADAM_B1) * g_
        v_ = ADAM_B2 * v_ref[...] + (1.0 - ADAM_B2) * jnp.square(g_)
        m_hat = m_ / (1.0 - ADAM_B1 ** ADAM_STEP)
        v_hat = v_ / (1.0 - ADAM_B2 ** ADAM_STEP)
        d_ref[...] = -ADAM_LR * (m_hat / (jnp.sqrt(v_hat) + ADAM_EPS) + ADAM_WD * w_ref[...])
        mo_ref[...] = m_
        vo_ref[...] = v_

    full = pl.BlockSpec((tr, C), lambda i, c: (i, 0))
    half = pl.BlockSpec((tr, C), lambda i, c: (i % nb, 0))
    grid_spec = pltpu.PrefetchScalarGridSpec(num_scalar_prefetch=1, grid=(2 * nb,),
                                             in_specs=[full, half, half, full, full], out_specs=[full] * 4)
    return pl.pallas_call(body, name=name, grid_spec=grid_spec, out_shape=[jax.ShapeDtypeStruct((R, C), f32)] * 4,
                          compiler_params=_params(("parallel",)))(c_idx, w, g_mine, g_theirs, m, v)


BIG = ("ffn1_w_gate", "ffn1_w_up", "ffn1_w_down", "w_in", "w_out", "ffn2_w_gate", "ffn2_w_up", "ffn2_w_down",
       "ple_w_gate", "ple_w_proj")


SMALL = ("ffn1_norm", "mix_norm", "gm_ln_g", "gm_ln_b", "gm_w_s", "gm_b_s", "gm_out_norm", "conv_b", "dt_bias", "a_log",
         "d_skip", "ssm_norm", "ffn2_norm", "ple_norm", "ple_b_gate", "final_norm")
SMALL_C = 1024


def _pack_small(vals):
    parts = []
    for v in vals:
        f = v.astype(f32).reshape(-1)
        parts.append(jnp.pad(f, (0, -f.shape[0] % SMALL_C)))
    flat = jnp.concatenate(parts)
    rows = flat.shape[0] // SMALL_C
    return jnp.pad(flat, (0, (-rows % 8) * SMALL_C)).reshape(-1, SMALL_C)


def _unpack_small(pack, shapes):
    flat = pack.reshape(-1)
    out, off = [], 0
    for s in shapes:
        n = 1
        for d in s:
            n *= d
        out.append(flat[off:off + n].reshape(s))
        off += n + (-n % SMALL_C)
    return out


def _pad_lanes(v):
    return jnp.pad(v, ((0, 0), (0, LANES - v.shape[1])))


def _pad_rows(a):
    pad = [(0, 0)] * a.ndim
    pad[-2] = (0, -a.shape[-2] % ROW_PAD)
    return jnp.pad(a, pad) if pad[-2][1] else a


FETCH = (("ffn1_w_gate", "ffn1_w_up", "ffn1_w_down"), ("w_in", "conv_w", "w_out"),
         ("ffn2_w_gate", "ffn2_w_up", "ffn2_w_down", "ple_w_gate", "ple_w_proj"))
TRANSPOSED = ("ffn1_w_gate", "ffn1_w_up", "ffn2_w_gate", "ffn2_w_up", "w_in")
ROW_PAD = 32
DONE = (("ffn2_w_gate", "ffn2_w_up", "ffn2_w_down", "w_out", "ple_w_gate", "ple_w_proj"), ("w_in",),
        ("ffn1_w_gate", "ffn1_w_up", "ffn1_w_down"))


def _local_step(x, p, tgt, fetch, S, on_grads):
    G = GM_WIDTH
    K = N_CHIPS
    b_st = S["gm_b_s"][0].T
    w_s = S["gm_w_s"][0]
    dtb, alog, dsk = _pad_lanes(S["dt_bias"]), _pad_lanes(S["a_log"]), _pad_lanes(S["d_skip"])
    gfin = S["final_norm"].reshape(1, -1)

    def rows(a):
        return a.reshape(-1, D_MODEL)

    def shards(a):
        return a.reshape(K, -1, D_MODEL)

    wg1, wu1, wd1 = [rows(a) for a in fetch(0, None)]
    h1, n1, a1, b1 = _ffn_fwd("ffn1_fwd", x, S["ffn1_norm"], wg1, wu1, wd1)
    w_in4, cw4, wo4 = fetch(1, h1)
    w_in = w_in4.reshape(IN_PROJ, D_MODEL)
    w_uv = w_in[:2 * G]
    w_zxd = jnp.pad(w_in[2 * G:], ((0, ZXD - (IN_PROJ - 2 * G)), (0, 0)))
    conv_w = jnp.transpose(cw4, (1, 0, 2)).reshape(SSM_CONV, CONV_DIM)
    wo = wo4.reshape(-1, D_MODEL)
    n2, uv, z, xbc, dtr, ya = _mix_fwd(h1, S["mix_norm"], w_uv, w_zxd, S["gm_ln_g"], S["gm_ln_b"], w_s, b_st, S["gm_out_norm"])
    yb, xc, sg, y_ssd, sprev = _ssd_fwd(xbc, z, dtr, conv_w, S["conv_b"], dtb, alog, dsk, S["ssm_norm"])
    wg2, wu2, wd2, wpg4, wpp4 = fetch(2, yb)
    wg2, wu2, wd2 = rows(wg2), rows(wu2), rows(wd2)
    h2, h3, n3, a2, b2 = _ffn_fwd("ffn2_fwd", h1, S["ffn2_norm"], wg2, wu2, wd2, pre=(ya, yb, wo))
    dh3, loss, dgp, dwpg, dbpg, dwpp, dgf = _tail(h3, p, tgt, S["ple_norm"], wpg4.reshape(-1, D_MODEL), S["ple_b_gate"], wpp4, gfin)
    dh2, da2, db2, hm2, dg_ffn2, dya, dyb = _ffn_bwd("ffn2_bwd", dh3, h2, S["ffn2_norm"], a2, b2, wg2, wu2, wd2, wo=wo, ga=G)
    dw_out = jnp.concatenate([_matmul_tn("dw_out_a", ya, dh2), _matmul_tn("dw_out_b", yb, dh2)], axis=0).reshape(wo4.shape)
    zero = on_grads(0, [shards(_matmul_tn("dw_ffn2_gate", da2, n3)), shards(_matmul_tn("dw_ffn2_up", db2, n3)),
                        shards(_matmul_tn("dw_ffn2_down", hm2, dh3, scale=0.5)), dw_out,
                        dwpg.astype(bf16).reshape(wpg4.shape), dwpp.astype(bf16)])
    dzxd, dcw, dcb, ddtb, dalog, ddsk, dgssm = _ssd_bwd(xbc, xc, sg, y_ssd, z, dtr, sprev, dyb, conv_w, S["conv_b"], dtb, alog, dsk,
                                                        S["ssm_norm"] + zero)
    dh1, duv, dg_mix, dlng, dlnb, dws, dbst, dgout = _mix_bwd(dh2, h1, S["mix_norm"], uv, dya, dzxd, w_uv, w_zxd, S["gm_ln_g"],
                                                              S["gm_ln_b"], w_s, b_st, S["gm_out_norm"])
    dw_in = jnp.concatenate([_matmul_tn("dw_in_uv", duv, n2), _matmul_tn("dw_in_zxd", dzxd, n2)[:IN_PROJ - 2 * G]], axis=0)
    zero = on_grads(1, [dw_in.reshape(w_in4.shape)])
    dx, da1, db1, hm1, dg_ffn1 = _ffn_bwd("ffn1_bwd", dh1, x, S["ffn1_norm"] + zero, a1, b1, wg1, wu1, wd1)
    zero = on_grads(2, [shards(_matmul_tn("dw_ffn1_gate", da1, n1)), shards(_matmul_tn("dw_ffn1_up", db1, n1)),
                        shards(_matmul_tn("dw_ffn1_down", hm1, dh1, scale=0.5))])
    loss = loss + zero
    nh = SSM_HEADS
    gS = {"ffn1_norm": dg_ffn1, "mix_norm": dg_mix, "gm_ln_g": dlng, "gm_ln_b": dlnb, "gm_w_s": dws[None], "gm_b_s": dbst.T[None],
          "gm_out_norm": dgout, "conv_b": dcb, "dt_bias": ddtb[:, :nh], "a_log": dalog[:, :nh], "d_skip": ddsk[:, :nh],
          "ssm_norm": dgssm, "ffn2_norm": dg_ffn2, "ple_norm": dgp, "ple_b_gate": dbpg, "final_norm": dgf.reshape(-1)}
    return loss, dx, dcw, gS


_WEIGHTS = ("ffn1_norm", "ffn1_w_gate", "ffn1_w_up", "ffn1_w_down", "mix_norm", "w_in", "gm_ln_g", "gm_ln_b", "gm_w_s", "gm_b_s",
            "gm_out_norm", "conv_w", "conv_b", "dt_bias", "a_log", "d_skip", "ssm_norm", "w_out", "ffn2_norm", "ffn2_w_gate",
            "ffn2_w_up", "ffn2_w_down", "ple_norm", "ple_w_gate", "ple_b_gate", "ple_w_proj", "final_norm")
_BIG_NAMES = BIG


def kernel(x, p, ffn1_norm, ffn1_w_gate, ffn1_w_up, ffn1_w_down, mix_norm, w_in, gm_ln_g, gm_ln_b, gm_w_s, gm_b_s, gm_out_norm, conv_w, conv_b, dt_bias, a_log, d_skip, ssm_norm, w_out, ffn2_norm, ffn2_w_gate, ffn2_w_up, ffn2_w_down, ple_norm, ple_w_gate, ple_b_gate, ple_w_proj, final_norm, loss_target, m_ffn1_norm, m_ffn1_w_gate, m_ffn1_w_up, m_ffn1_w_down, m_mix_norm, m_w_in, m_gm_ln_g, m_gm_ln_b, m_gm_w_s, m_gm_b_s, m_gm_out_norm, m_conv_w, m_conv_b, m_dt_bias, m_a_log, m_d_skip, m_ssm_norm, m_w_out, m_ffn2_norm, m_ffn2_w_gate, m_ffn2_w_up, m_ffn2_w_down, m_ple_norm, m_ple_w_gate, m_ple_b_gate, m_ple_w_proj, m_final_norm, v_ffn1_norm, v_ffn1_w_gate, v_ffn1_w_up, v_ffn1_w_down, v_mix_norm, v_w_in, v_gm_ln_g, v_gm_ln_b, v_gm_w_s, v_gm_b_s, v_gm_out_norm, v_conv_w, v_conv_b, v_dt_bias, v_a_log, v_d_skip, v_ssm_norm, v_w_out, v_ffn2_norm, v_ffn2_w_gate, v_ffn2_w_up, v_ffn2_w_down, v_ple_norm, v_ple_w_gate, v_ple_b_gate, v_ple_w_proj, v_final_norm):
    given = dict(locals())
    w = {n: given[n] for n in _WEIGHTS}
    m = {n: given["m_" + n] for n in _WEIGHTS}
    v = {n: given["v_" + n] for n in _WEIGHTS}

    c_idx = lax.axis_index("c").astype(jnp.int32).reshape(1)
    chip = 2 * lax.axis_index("x") + lax.axis_index("y")
    chip_idx = chip.astype(jnp.int32).reshape(1)

    shard = {n: (jnp.swapaxes(w[n][0], 0, 1) if n in TRANSPOSED else w[n][0]).astype(bf16) for n in BIG}
    shard["conv_w"] = w["conv_w"][0]
    first = _gather_weights([shard[n] for n in FETCH[0]], [True] * len(FETCH[0]))
    fetching, after = [], first[-1]
    for k in (1, 2):
        srcs = [shard[n] for n in FETCH[k]]
        lands = [jax.ShapeDtypeStruct((N_CHIPS,) + s.shape, s.dtype) for s in srcs]
        fetching.append(_copies_start("gather%d_start" % k, srcs, lands, 4 * len(srcs), _gather_copies, after))
        after = fetching[-1][4]

    def fetch(k, after_):
        return first if k == 0 else _copies_wait("gather%d_wait" % k, fetching[k - 1], _gather_copies, [after_])[1]

    exchanging = []

    def on_grads(k, grads):
        grads = [_pad_rows(g_) for g_ in grads]
        others = _swap_halves("swap%d" % k, grads)
        parts = [_add_halves("add_" + n, g_, o_, c_idx) for n, g_, o_ in zip(DONE[k], grads, others)]
        lands = [jax.ShapeDtypeStruct((3,) + p_.shape[1:], p_.dtype) for p_ in parts]
        exchanging.append(_copies_start("exchange%d_start" % k, parts, lands, 3 * len(parts), _partial_copies, c_idx))
        return exchanging[-1][4][0, 0]

    S = {n: w[n] for n in SMALL}
    S["ffn1_norm"] = S["ffn1_norm"] + after[0, 0]
    loss, dx, dcw, gS = _local_step(x[0], p[0, 0], loss_target[0], fetch, S, on_grads)

    small = _pack_small([gS[n] for n in SMALL] + [dcw, loss[:, :1]])
    small_lands = [jax.ShapeDtypeStruct((N_DEV - 1,) + small.shape, small.dtype)]
    small_st = _copies_start("small_start", [small], small_lands, N_DEV - 1, _small_copies, c_idx)

    g, delta, new_m, new_v = {}, {}, {}, {}
    after = [small_st[4]]
    for k in range(len(DONE)):
        parts, recv = _copies_wait("exchange%d_wait" % k, exchanging[k], _partial_copies, after)
        mine = [_sum_partials("sum_" + n, p_, r_, chip_idx) for n, p_, r_ in zip(DONE[k], parts, recv)]
        theirs = _share_halves("share%d" % k, mine)
        after = []
        for n, gm_, gt_ in zip(DONE[k], mine, theirs):
            flip = (lambda a: jnp.swapaxes(a, 0, 1)) if n in TRANSPOSED else (lambda a: a)
            rows = flip(w[n][0]).shape[0]
            w_, m_, v_ = [_pad_rows(flip(a[n][0])) for a in (w, m, v)]
            outs = _adamw_big("adamw_" + n, w_, gm_, gt_, m_, v_, c_idx)
            g[n], delta[n], new_m[n], new_v[n] = [flip(o[:rows])[None] for o in outs]
            after.append(outs[3])
    (own,), (slots,) = _copies_wait("small_wait", small_st, _small_copies, after)
    dev_idx = (2 * chip + lax.axis_index("c")).astype(jnp.int32).reshape(1)
    small_shapes = [w[n].shape for n in SMALL] + [dcw.shape, (1, 1)]
    small_sum = _unpack_small(_sum_small(own, slots, dev_idx), small_shapes)
    g.update({n: small_sum[i] for i, n in enumerate(SMALL)})
    cshard = w["conv_w"].shape[2]
    g["conv_w"] = lax.dynamic_slice_in_dim(small_sum[len(SMALL)], chip * cshard, cshard, axis=1)[None]
    loss_total = small_sum[len(SMALL) + 1].reshape(())
    sm_names = SMALL + ("conv_w",)
    sm_shapes = [w[n].shape for n in sm_names]
    d_s, m_s, v_s = _adamw("adamw_small", _pack_small([w[n] for n in sm_names]), _pack_small([g[n] for n in sm_names]),
                           _pack_small([m[n] for n in sm_names]), _pack_small([v[n] for n in sm_names]))
    for dst, src in ((delta, d_s), (new_m, m_s), (new_v, v_s)):
        for n, val in zip(sm_names, _unpack_small(src, sm_shapes)):
            dst[n] = val

    return (loss_total, dx[None], *[g[n] for n in _WEIGHTS], *[delta[n] for n in _WEIGHTS],
            *[new_m[n] for n in _WEIGHTS], *[new_v[n] for n in _WEIGHTS])
```

```python
import jax
import jax.numpy as jnp
from jax import lax
from jax.experimental import pallas as pl
from jax.experimental.pallas import tpu as pltpu

f32 = jnp.float32
bf16 = jnp.bfloat16
MESH = pl.DeviceIdType.MESH
HIGHEST = lax.Precision.HIGHEST

EPS = 1e-6
N_CHIPS = 4
N_DEV = 8
D_MODEL = 1024
GM_WIDTH = 1024
GM_HEADS = 8
CHUNK = 128
SSM_WIDTH = 1024
SSM_HEADS = 16
SSM_HEAD_DIM = 64
SSM_GROUPS = 2
SSM_STATE = 128
SSM_CONV = 4
CONV_DIM = SSM_WIDTH + 2 * SSM_GROUPS * SSM_STATE
IN_PROJ = 2 * GM_WIDTH + SSM_WIDTH + CONV_DIM + SSM_HEADS
LANES = 128
ZXD = SSM_WIDTH + CONV_DIM + LANES

ADAM_LR = 0.001
ADAM_B1 = 0.9
ADAM_B2 = 0.999
ADAM_EPS = 1e-08
ADAM_WD = 0.01
ADAM_STEP = 10

VMEM_LIMIT = 56 * 1024 * 1024
HALF_ROWS_BF16 = 592
HALF_ROWS_F32 = 320


def _dot(a, b):
    return jnp.dot(a, b, preferred_element_type=f32)


def _dot_nt(a, b):
    return lax.dot_general(a, b, (((1,), (1,)), ((), ())), preferred_element_type=f32)


def _dot_tn(a, b):
    return lax.dot_general(a, b, (((0,), (0,)), ((), ())), preferred_element_type=f32)


def _rms(x, g):
    return x * lax.rsqrt(jnp.mean(x * x, axis=-1, keepdims=True) + EPS) * g


def _layernorm(x, g, b):
    mu = jnp.mean(x, axis=-1, keepdims=True)
    xc = x - mu
    return xc * lax.rsqrt(jnp.mean(xc * xc, axis=-1, keepdims=True) + EPS) * g + b


def _sigmoid(x):
    return 1.0 / (1.0 + jnp.exp(-x))


def _softplus(x):
    return jnp.maximum(x, 0.0) + jnp.log(1.0 + jnp.exp(-jnp.abs(x)))


def _full(shape):
    nd = len(shape)
    return pl.BlockSpec(shape, lambda *_: (0,) * nd, pipeline_mode=pl.Buffered(1))


def _acc(shape):
    nd = len(shape)
    return pl.BlockSpec(shape, lambda *_: (0,) * nd)


def _rows(tm, ncols):
    return pl.BlockSpec((tm, ncols), lambda i: (i, 0))


def _params(sem):
    return pltpu.CompilerParams(dimension_semantics=sem, vmem_limit_bytes=VMEM_LIMIT)


def _row_tile(rows, target, mult=8):
    best = rows
    for t in range(mult, min(rows, target) + 1, mult):
        if rows % t == 0:
            best = t
    return best if best <= target else rows


def _ffn_fwd(name, h, g, wg, wu, wd, pre=None, tm=256):
    T, D = h.shape
    F = wg.shape[0]
    tm = min(tm, T)

    def body(*refs):
        if pre is None:
            h_ref, g_ref, wg_ref, wu_ref, wd_ref, ho_ref, n_ref, a_ref, b_ref = refs
            hin = h_ref[...]
        else:
            (h_ref, ya_ref, yb_ref, wo_ref, g_ref, wg_ref, wu_ref, wd_ref,
             hi_ref, ho_ref, n_ref, a_ref, b_ref) = refs
            ga = ya_ref.shape[1]
            hin = h_ref[...] + _dot(ya_ref[...], wo_ref[:ga, :]) + _dot(yb_ref[...], wo_ref[ga:, :])
            hi_ref[...] = hin
        n = _rms(hin, g_ref[...]).astype(bf16)
        n_ref[...] = n
        a = _dot_nt(n, wg_ref[...]).astype(bf16)
        b = _dot_nt(n, wu_ref[...]).astype(bf16)
        a_ref[...] = a
        b_ref[...] = b
        af = a.astype(f32)
        hm = (af * _sigmoid(af) * b.astype(f32)).astype(bf16)
        ho_ref[...] = hin + 0.5 * _dot(hm, wd_ref[...])

    ins = [h] + (list(pre) if pre is not None else []) + [g, wg, wu, wd]
    in_specs = [_rows(tm, D)]
    if pre is not None:
        in_specs += [_rows(tm, pre[0].shape[1]), _rows(tm, pre[1].shape[1]), _full(pre[2].shape)]
    in_specs += [_full(g.shape), _full(wg.shape), _full(wu.shape), _full(wd.shape)]
    outs = [jax.ShapeDtypeStruct((T, D), f32), jax.ShapeDtypeStruct((T, D), bf16),
            jax.ShapeDtypeStruct((T, F), bf16), jax.ShapeDtypeStruct((T, F), bf16)]
    out_specs = [_rows(tm, D), _rows(tm, D), _rows(tm, F), _rows(tm, F)]
    if pre is not None:
        outs = [jax.ShapeDtypeStruct((T, D), f32)] + outs
        out_specs = [_rows(tm, D)] + out_specs
    return pl.pallas_call(body, name=name, grid=(T // tm,), in_specs=in_specs, out_specs=out_specs,
                          out_shape=outs, compiler_params=_params(("parallel",)))(*ins)


def _ffn_bwd(name, dh, hin, g, a, b, wg, wu, wd, wo=None, ga=0, tm=256):
    T, D = dh.shape
    F = wg.shape[0]
    tm = min(tm, T)

    def body(*refs):
        if wo is None:
            (dh_ref, hin_ref, g_ref, a_ref, b_ref, wg_ref, wu_ref, wd_ref,
             dhi_ref, da_ref, db_ref, hm_ref, dg_ref) = refs
        else:
            (dh_ref, hin_ref, g_ref, a_ref, b_ref, wg_ref, wu_ref, wd_ref, wo_ref,
             dhi_ref, da_ref, db_ref, hm_ref, dg_ref, dya_ref, dyb_ref) = refs

        @pl.when(pl.program_id(0) == 0)
        def _():
            dg_ref[...] = jnp.zeros_like(dg_ref)

        dh_ = dh_ref[...]
        dhb = (0.5 * dh_).astype(bf16)
        dhm = _dot_nt(dhb, wd_ref[...])
        af = a_ref[...].astype(f32)
        bf = b_ref[...].astype(f32)
        sg = _sigmoid(af)
        sl_ = af * sg
        da = (dhm * bf * (sg * (1.0 + af * (1.0 - sg)))).astype(bf16)
        db = (dhm * sl_).astype(bf16)
        da_ref[...] = da
        db_ref[...] = db
        hm_ref[...] = (sl_ * bf).astype(bf16)
        dn = _dot(da, wg_ref[...]) + _dot(db, wu_ref[...])
        _, vjp = jax.vjp(_rms, hin_ref[...], g_ref[...])
        dx, dg = vjp(dn)
        dhi = dh_ + dx
        dhi_ref[...] = dhi
        dg_ref[...] += dg
        if wo is not None:
            dhib = dhi.astype(bf16)
            dya_ref[...] = _dot_nt(dhib, wo_ref[:ga, :]).astype(bf16)
            dyb_ref[...] = _dot_nt(dhib, wo_ref[ga:, :]).astype(bf16)

    ins = [dh, hin, g, a, b, wg, wu, wd]
    in_specs = [_rows(tm, D), _rows(tm, D), _full(g.shape), _rows(tm, F), _rows(tm, F),
                _full(wg.shape), _full(wu.shape), _full(wd.shape)]
    act = jax.ShapeDtypeStruct((T, F), bf16)
    outs = [jax.ShapeDtypeStruct((T, D), f32), act, act, act, jax.ShapeDtypeStruct(g.shape, f32)]
    out_specs = [_rows(tm, D), _rows(tm, F), _rows(tm, F), _rows(tm, F), _acc(g.shape)]
    if wo is not None:
        gb = wo.shape[0] - ga
        ins += [wo]
        in_specs += [_full(wo.shape)]
        outs += [jax.ShapeDtypeStruct((T, ga), bf16), jax.ShapeDtypeStruct((T, gb), bf16)]
        out_specs += [_rows(tm, ga), _rows(tm, gb)]
    return pl.pallas_call(body, name=name, grid=(T // tm,), in_specs=in_specs, out_specs=out_specs,
                          out_shape=outs, compiler_params=_params(("arbitrary",)))(*ins)


def _matmul_tn(name, a, b, scale=1.0, tk=2048):
    T, M = a.shape
    N = b.shape[1]
    tk = min(tk, T)
    nk = T // tk
    tn = LANES * max(d for d in range(1, N // LANES + 1) if (N // LANES) % d == 0 and (d == 1 or M * d * LANES * 4 <= 6 * 1024 * 1024))

    def body(a_ref, b_ref, o_ref, acc):
        k = pl.program_id(1)

        @pl.when(k == 0)
        def _():
            acc[...] = jnp.zeros_like(acc)

        bb = b_ref[...]
        if scale != 1.0:
            bb = bb * scale
        acc[...] += _dot_tn(a_ref[...].astype(bf16), bb.astype(bf16))

        @pl.when(k == nk - 1)
        def _():
            o_ref[...] = acc[...].astype(bf16)

    return pl.pallas_call(
        body, name=name, grid=(N // tn, nk),
        in_specs=[pl.BlockSpec((tk, M), lambda j, k: (k, 0)), pl.BlockSpec((tk, tn), lambda j, k: (k, j))],
        out_specs=pl.BlockSpec((M, tn), lambda j, k: (0, j)),
        out_shape=jax.ShapeDtypeStruct((M, N), bf16), scratch_shapes=[pltpu.VMEM((M, tn), f32)],
        compiler_params=_params(("parallel", "arbitrary")))(a, b)


def _gelu_and_slope(x):
    cdf = 0.5 * (1.0 + lax.erf(x * 0.7071067811865476))
    return x * cdf, cdf + x * (0.3989422804014327 * jnp.exp(-0.5 * x * x))


def _tril_mask():
    r = lax.broadcasted_iota(jnp.int32, (CHUNK, CHUNK), 0)
    c = lax.broadcasted_iota(jnp.int32, (CHUNK, CHUNK), 1)
    return c <= r


def _gm_mix(vnb, ws_ref, bst, mixed_sc, tm):
    mask = _tril_mask()
    for h in range(GM_HEADS):
        wt = jnp.where(mask, ws_ref[h], 0.0).astype(bf16)
        bias = bst[:, h:h + 1]
        for q in range(tm // CHUNK):
            rs = slice(q * CHUNK, (q + 1) * CHUNK)
            cs = slice(h * CHUNK, (h + 1) * CHUNK)
            mixed_sc[rs, cs] = _dot(wt, vnb[rs, cs]) + bias


def _mix_fwd(h1, gmix, w_uv, w_zxd, ln_g, ln_b, w_s, b_st, gout, tm=512):
    T, D = h1.shape
    tm = min(tm, T)
    G = GM_WIDTH

    def body(h_ref, g_ref, wuv_ref, wzxd_ref, lng_ref, lnb_ref, ws_ref, bst_ref, gout_ref,
             n_ref, act_ref, slope_ref, z_ref, xbc_ref, dt_ref, ya_ref, mixed_sc):
        n = _rms(h_ref[...], g_ref[...]).astype(bf16)
        n_ref[...] = n
        uv = _dot_nt(n, wuv_ref[...]).astype(bf16)
        zxd = _dot_nt(n, wzxd_ref[...])
        z_ref[...] = zxd[:, :SSM_WIDTH].astype(bf16)
        xbc_ref[...] = zxd[:, SSM_WIDTH:SSM_WIDTH + CONV_DIM].astype(bf16)
        dt_ref[...] = zxd[:, SSM_WIDTH + CONV_DIM:]
        act, slope = _gelu_and_slope(uv.astype(f32))
        act = act.astype(bf16)
        act_ref[...] = act
        slope_ref[...] = slope.astype(bf16)
        ug, vg = act[:, :G].astype(f32), act[:, G:].astype(f32)
        _gm_mix(_layernorm(vg, lng_ref[...], lnb_ref[...]).astype(bf16), ws_ref, bst_ref[...], mixed_sc, tm)
        ya_ref[...] = _rms(ug * mixed_sc[...], gout_ref[...]).astype(bf16)

    ins = [h1, gmix, w_uv, w_zxd, ln_g, ln_b, w_s, b_st, gout]
    in_specs = [_rows(tm, D)] + [_full(x.shape) for x in ins[1:]]
    outs = [jax.ShapeDtypeStruct((T, D), bf16), jax.ShapeDtypeStruct((T, 2 * G), bf16), jax.ShapeDtypeStruct((T, 2 * G), bf16),
            jax.ShapeDtypeStruct((T, SSM_WIDTH), bf16), jax.ShapeDtypeStruct((T, CONV_DIM), bf16),
            jax.ShapeDtypeStruct((T, LANES), f32), jax.ShapeDtypeStruct((T, G), bf16)]
    out_specs = [_rows(tm, D), _rows(tm, 2 * G), _rows(tm, 2 * G), _rows(tm, SSM_WIDTH), _rows(tm, CONV_DIM), _rows(tm, LANES),
                 _rows(tm, G)]
    return pl.pallas_call(body, name="mix_fwd", grid=(T // tm,), in_specs=in_specs, out_specs=out_specs,
                          out_shape=outs, scratch_shapes=[pltpu.VMEM((tm, G), f32)],
                          compiler_params=_params(("parallel",)))(*ins)


def _mix_bwd(dh, h1, gmix, act, slope, dya, dzxd, w_uv, w_zxd, ln_g, ln_b, w_s, b_st, gout, tm=256):
    T, D = dh.shape
    tm = min(tm, T)
    G = GM_WIDTH

    def body(dh_ref, h_ref, g_ref, act_ref, slope_ref, dya_ref, dzxd_ref, wuv_ref, wzxd_ref, lng_ref, lnb_ref, ws_ref,
             bst_ref, gout_ref, dhi_ref, duv_ref, dg_ref, dlng_ref, dlnb_ref, dws_ref, dbst_ref, dgout_ref, mixed_sc, dvn_sc):
        @pl.when(pl.program_id(0) == 0)
        def _():
            for r in (dg_ref, dlng_ref, dlnb_ref, dws_ref, dbst_ref, dgout_ref):
                r[...] = jnp.zeros_like(r)

        dn_z = _dot(dzxd_ref[...], wzxd_ref[...])
        ug = act_ref[:, :G].astype(f32)
        vn, ln_vjp = jax.vjp(_layernorm, act_ref[:, G:].astype(f32), lng_ref[...], lnb_ref[...])
        vnb = vn.astype(bf16)
        _gm_mix(vnb, ws_ref, bst_ref[...], mixed_sc, tm)
        mixed = mixed_sc[...]
        _, out_vjp = jax.vjp(_rms, ug * mixed, gout_ref[...])
        dpre, dgout = out_vjp(dya_ref[...].astype(f32))
        dgout_ref[...] += dgout
        dug = dpre * mixed
        dmixed = dpre * ug
        mask = _tril_mask()
        lane = lax.broadcasted_iota(jnp.int32, (1, GM_HEADS), 1)
        dbst = jnp.zeros((CHUNK, GM_HEADS), f32)
        for h in range(GM_HEADS):
            wt = jnp.where(mask, ws_ref[h], 0.0).astype(bf16)
            cs = slice(h * CHUNK, (h + 1) * CHUNK)
            dw = jnp.zeros((CHUNK, CHUNK), f32)
            for q in range(tm // CHUNK):
                rs = slice(q * CHUNK, (q + 1) * CHUNK)
                dm = dmixed[rs, cs]
                dmb = dm.astype(bf16)
                dw = dw + _dot_nt(dmb, vnb[rs, cs])
                dbst = dbst + jnp.sum(dm, axis=1, keepdims=True) * (lane == h).astype(f32)
                dvn_sc[rs, cs] = _dot_tn(wt, dmb)
            dws_ref[h] += jnp.where(mask, dw, 0.0)
        dbst_ref[...] += dbst
        dvg, dlng, dlnb = ln_vjp(dvn_sc[...])
        duv = (jnp.concatenate([dug, dvg], axis=1) * slope_ref[...].astype(f32)).astype(bf16)
        duv_ref[...] = duv
        dlng_ref[...] += dlng
        dlnb_ref[...] += dlnb
        dn = dn_z + _dot(duv, wuv_ref[...])
        _, vjp = jax.vjp(_rms, h_ref[...], g_ref[...])
        dx, dg = vjp(dn)
        dhi_ref[...] = dh_ref[...] + dx
        dg_ref[...] += dg

    ins = [dh, h1, gmix, act, slope, dya, dzxd, w_uv, w_zxd, ln_g, ln_b, w_s, b_st, gout]
    in_specs = ([_rows(tm, D), _rows(tm, D), _full(gmix.shape), _rows(tm, 2 * G), _rows(tm, 2 * G), _rows(tm, G),
                 _rows(tm, dzxd.shape[1])] + [_full(x.shape) for x in ins[7:]])
    accs = (gmix, ln_g, ln_b, w_s, b_st, gout)
    outs = ([jax.ShapeDtypeStruct((T, D), f32), jax.ShapeDtypeStruct((T, 2 * G), bf16)]
            + [jax.ShapeDtypeStruct(x.shape, f32) for x in accs])
    out_specs = [_rows(tm, D), _rows(tm, 2 * G)] + [_acc(x.shape) for x in accs]
    return pl.pallas_call(body, name="mix_bwd", grid=(T // tm,), in_specs=in_specs, out_specs=out_specs,
                          out_shape=outs, scratch_shapes=[pltpu.VMEM((tm, G), f32), pltpu.VMEM((tm, G), f32)],
                          compiler_params=_params(("arbitrary",)))(*ins)


HALO = 16
PAIRS = SSM_HEADS // 2
PAIR_W = 2 * SSM_HEAD_DIM


def _split(x, n):
    parts = []
    for _ in range(n):
        p = x.astype(bf16)
        parts.append(p)
        x = x - p.astype(f32)
    return parts


def _dot_sel(x, sel_n, n):
    return _dot(jnp.concatenate(_split(x, n), axis=1), sel_n)


def _sel_dot(sel, x, n):
    return _dot(jnp.concatenate([sel] * n, axis=1), jnp.concatenate(_split(x, n), axis=0))


EXPAND_SPLIT = 3
REDUCE_SPLIT = 2


def _head_mats():
    ex = (jnp.arange(SSM_WIDTH)[None, :] // SSM_HEAD_DIM == jnp.arange(LANES)[:, None]).astype(bf16)
    return jnp.tile(ex, (EXPAND_SPLIT, 1)), jnp.tile(ex.T, (REDUCE_SPLIT, 1))


def _shift_mat(rows, cols, off):
    r = lax.broadcasted_iota(jnp.int32, (rows, cols), 0)
    c = lax.broadcasted_iota(jnp.int32, (rows, cols), 1)
    return (c == r + off).astype(bf16)


def _ssd_conv(c, xbc_ref, halo_ref, cw_ref, cb_ref):
    halo = halo_ref[...]
    ext = jnp.concatenate([jnp.where(c > 0, halo, jnp.zeros_like(halo)), xbc_ref[...]], axis=0)
    xc = cb_ref[...] + cw_ref[SSM_CONV - 1:SSM_CONV, :] * xbc_ref[...].astype(f32)
    for j in range(SSM_CONV - 1):
        xc = xc + cw_ref[j:j + 1, :] * _dot(_shift_mat(CHUNK, HALO + CHUNK, HALO - SSM_CONV + 1 + j), ext)
    return xc


def _ssd_front(dtr_ref, dtb_ref, alog_ref):
    dt = _softplus(dtr_ref[...] + dtb_ref[...])
    a = -jnp.exp(alog_ref[...])
    acs = jnp.dot(_tril_mask().astype(f32), dt * a, preferred_element_type=f32, precision=HIGHEST)
    return dt, a, acs


def _ssd_wide(xa, dt, acs, dsk, ex):
    dt_x = _dot_sel(dt, ex, EXPAND_SPLIT)
    acs_x = _dot_sel(acs, ex, EXPAND_SPLIT)
    dsk_x = _dot_sel(jnp.broadcast_to(dsk, (8, LANES)), ex, EXPAND_SPLIT)[0:1]
    e_x = jnp.exp(acs_x)
    r_x = jnp.exp(acs_x[CHUNK - 1:CHUNK, :] - acs_x)
    xs = xa[:, :SSM_WIDTH]
    xd = xs * dt_x
    return dt_x, dsk_x, e_x, r_x, xs, xd, xd * r_x


def _pair_stack(v, lo):
    return jnp.concatenate([jnp.where(lo, v, 0.0), jnp.where(lo, 0.0, v)], axis=0)


def _ssd_pair(j, acs, acs_t, cb):
    out = []
    tril = _tril_mask()
    for h in (2 * j, 2 * j + 1):
        dk = jnp.exp(jnp.where(tril, acs[:, h:h + 1] - acs_t[h:h + 1, :], -jnp.inf))
        out.append((dk, cb * dk))
    return out


def _pair_col(row_lo, tot, j):
    return jnp.exp(jnp.where(row_lo, tot[:, 2 * j:2 * j + 1], tot[:, 2 * j + 1:2 * j + 2]))


def _gated_norm(y, z, g):
    yg = y * (z * _sigmoid(z))
    half = SSM_WIDTH // SSM_GROUPS
    parts = []
    for k in range(SSM_GROUPS):
        s = yg[:, k * half:(k + 1) * half]
        parts.append(s * lax.rsqrt(jnp.mean(s * s, axis=-1, keepdims=True) + EPS))
    return jnp.concatenate(parts, axis=1) * g


def _group_mats(xa):
    out = []
    for g in range(SSM_GROUPS):
        bm = xa[:, SSM_WIDTH + g * SSM_STATE:SSM_WIDTH + (g + 1) * SSM_STATE].astype(bf16)
        cm = xa[:, SSM_WIDTH + (SSM_GROUPS + g) * SSM_STATE:SSM_WIDTH + (SSM_GROUPS + g + 1) * SSM_STATE].astype(bf16)
        out.append((cm, bm, _dot_nt(cm, bm)))
    return out


def _ssd_fwd(xbc, z, dtr, conv_w, conv_b, dt_bias, a_log, d_skip, ssm_norm):
    T = xbc.shape[0]
    nc = T // CHUNK
    N = SSM_STATE

    def body(xbc_ref, halo_ref, z_ref, dtr_ref, cw_ref, cb_ref, dtb_ref, alog_ref, dsk_ref, g_ref, ex_ref,
             yb_ref, xc_ref, sg_ref, y_ref, sprev_ref, s_sc):
        c = pl.program_id(0)

        @pl.when(c == 0)
        def _():
            s_sc[...] = jnp.zeros_like(s_sc)

        xc = _ssd_conv(c, xbc_ref, halo_ref, cw_ref, cb_ref)
        sg = _sigmoid(xc)
        xc_ref[...] = xc
        sg_ref[...] = sg
        xa = xc * sg
        dt, _, acs = _ssd_front(dtr_ref, dtb_ref, alog_ref)
        _, dsk_x, e_x, _, xs, xd, gm = _ssd_wide(xa, dt, acs, dsk_ref[...], ex_ref[...])
        acs_t = acs.T
        tot = acs[CHUNK - 1:CHUNK, :]
        groups = _group_mats(xa)
        lo = lax.broadcasted_iota(jnp.int32, (CHUNK, PAIR_W), 1) < SSM_HEAD_DIM
        row_lo = lax.broadcasted_iota(jnp.int32, (PAIR_W, 1), 0) < SSM_HEAD_DIM
        ys = []
        for j in range(PAIRS):
            cmb, bmb, cb = groups[j // (PAIRS // SSM_GROUPS)]
            ps = slice(j * PAIR_W, (j + 1) * PAIR_W)
            (_, m0), (_, m1) = _ssd_pair(j, acs, acs_t, cb)
            sp = s_sc[j]
            yd = _dot(jnp.concatenate([m0, m1], axis=1).astype(bf16), _pair_stack(xd[:, ps], lo).astype(bf16))
            ys.append(yd + e_x[:, ps] * _dot_nt(cmb, sp.astype(bf16)))
            sprev_ref[0, j] = sp
            s_sc[j] = _pair_col(row_lo, tot, j) * sp + _dot_tn(gm[:, ps].astype(bf16), bmb)
        y = jnp.concatenate(ys, axis=1) + xs * dsk_x
        y_ref[...] = y
        yb_ref[...] = _gated_norm(y, z_ref[...].astype(f32), g_ref[...]).astype(bf16)

    params = [conv_w, conv_b, dt_bias, a_log, d_skip, ssm_norm, _head_mats()[0]]
    hp = CHUNK // HALO
    in_specs = [_rows(CHUNK, CONV_DIM), pl.BlockSpec((HALO, CONV_DIM), lambda i: (jnp.maximum(i * hp - 1, 0), 0)),
                _rows(CHUNK, SSM_WIDTH), _rows(CHUNK, LANES)] + [_full(x.shape) for x in params]
    return pl.pallas_call(
        body, name="ssd_fwd", grid=(nc,), in_specs=in_specs,
        out_specs=[_rows(CHUNK, SSM_WIDTH), _rows(CHUNK, CONV_DIM), _rows(CHUNK, CONV_DIM), _rows(CHUNK, SSM_WIDTH),
                   pl.BlockSpec((1, PAIRS, PAIR_W, N), lambda i: (i, 0, 0, 0))],
        out_shape=[jax.ShapeDtypeStruct((T, SSM_WIDTH), bf16), jax.ShapeDtypeStruct((T, CONV_DIM), f32),
                   jax.ShapeDtypeStruct((T, CONV_DIM), f32), jax.ShapeDtypeStruct((T, SSM_WIDTH), f32),
                   jax.ShapeDtypeStruct((nc, PAIRS, PAIR_W, N), f32)],
        scratch_shapes=[pltpu.VMEM((PAIRS, PAIR_W, N), f32)],
        compiler_params=_params(("arbitrary",)))(xbc, xbc, z, dtr, *params)


def _ssd_bwd(xbc, xc, sg, y, z, dtr, sprev, dyb, conv_w, conv_b, dt_bias, a_log, d_skip, ssm_norm):
    T = xbc.shape[0]
    nc = T // CHUNK
    H, N = SSM_HEADS, SSM_STATE
    PG = PAIRS // SSM_GROUPS

    def body(xbc_ref, xc_ref, sg_ref, y_ref, z_ref, dtr_ref, sprev_ref, dyb_ref, cw_ref, cb_ref, dtb_ref, alog_ref, dsk_ref,
             g_ref, ex_ref, rd_ref, dzxd_ref, dcw_ref, dcb_ref, ddtb_ref, dalog_ref, ddsk_ref, dg_ref, ds_sc, next_sc):
        i = pl.program_id(0)

        @pl.when(i == 0)
        def _():
            ds_sc[...] = jnp.zeros_like(ds_sc)
            next_sc[...] = jnp.zeros_like(next_sc)
            for r_ in (dcw_ref, dcb_ref, ddtb_ref, dalog_ref, ddsk_ref, dg_ref):
                r_[...] = jnp.zeros_like(r_)

        xc = xc_ref[...]
        sg = sg_ref[...]
        xa = xc * sg
        dt, a, acs = _ssd_front(dtr_ref, dtb_ref, alog_ref)
        dt_x, dsk_x, e_x, r_x, xs, xd, gm = _ssd_wide(xa, dt, acs, dsk_ref[...], ex_ref[...])
        acs_t = acs.T
        tot = acs[CHUNK - 1:CHUNK, :]
        groups = _group_mats(xa)
        lo = lax.broadcasted_iota(jnp.int32, (CHUNK, PAIR_W), 1) < SSM_HEAD_DIM
        row_lo = lax.broadcasted_iota(jnp.int32, (PAIR_W, 1), 0) < SSM_HEAD_DIM
        pairs, zs = [], []
        for j in range(PAIRS):
            cmb, _, cb = groups[j // PG]
            pairs.append(_ssd_pair(j, acs, acs_t, cb))
            zs.append(_dot_nt(cmb, sprev_ref[0, j].astype(bf16)))
        zf = jnp.concatenate(zs, axis=1)
        _, gn_vjp = jax.vjp(_gated_norm, y_ref[...], z_ref[...].astype(f32), g_ref[...])
        dy, dz, dg = gn_vjp(dyb_ref[...].astype(f32))
        dg_ref[...] += dg
        dzxd_ref[:, :SSM_WIDTH] = dz.astype(bf16)

        lane = lax.broadcasted_iota(jnp.int32, (1, LANES), 1)
        sub = lax.broadcasted_iota(jnp.int32, (LANES, 1), 0)
        dacs = jnp.zeros((CHUNK, LANES), f32)
        dacs_r = jnp.zeros((LANES, CHUNK), f32)
        dtot = jnp.zeros((1, LANES), f32)
        dcb = [jnp.zeros((CHUNK, CHUNK), f32) for _ in range(SSM_GROUPS)]
        dcm = [jnp.zeros((CHUNK, N), f32) for _ in range(SSM_GROUPS)]
        dbm = [jnp.zeros((CHUNK, N), f32) for _ in range(SSM_GROUPS)]
        dxds, dgms = [], []
        for j in range(PAIRS):
            g = j // PG
            cmb, bmb, _ = groups[g]
            ps = slice(j * PAIR_W, (j + 1) * PAIR_W)
            (dk0, m0), (dk1, m1) = pairs[j]
            oh0, oh1 = (lane == 2 * j).astype(f32), (lane == 2 * j + 1).astype(f32)
            dyp = dy[:, ps]
            dy2 = _pair_stack(dyp, lo).astype(bf16)
            dm2 = _dot_nt(dy2, xd[:, ps].astype(bf16))
            m2 = jnp.concatenate([m0, m1], axis=0)
            dxds.append(_dot_tn(m2.astype(bf16), dy2))
            w2 = dm2 * m2
            rs = jnp.sum(w2, axis=1, keepdims=True)
            dacs = dacs + rs[:CHUNK] * oh0 + rs[CHUNK:] * oh1
            dacs_r = dacs_r - ((sub == 2 * j).astype(f32) * jnp.sum(w2[:CHUNK], axis=0, keepdims=True)
                               + (sub == 2 * j + 1).astype(f32) * jnp.sum(w2[CHUNK:], axis=0, keepdims=True))
            dcb[g] = dcb[g] + dm2[:CHUNK] * dk0 + dm2[CHUNK:] * dk1
            sp = sprev_ref[0, j]
            dzb = (dyp * e_x[:, ps]).astype(bf16)
            dcm[g] = dcm[g] + _dot(dzb, sp.astype(bf16))
            dsn = ds_sc[j]
            dsnb = dsn.astype(bf16)
            et = _pair_col(row_lo, tot, j)
            rr = jnp.sum(dsn * sp, axis=1, keepdims=True) * et
            dtot = dtot + jnp.sum(rr[:SSM_HEAD_DIM]) * oh0 + jnp.sum(rr[SSM_HEAD_DIM:]) * oh1
            dgms.append(_dot_nt(bmb, dsnb))
            dbm[g] = dbm[g] + _dot(gm[:, ps].astype(bf16), dsnb)
            ds_sc[j] = _dot_tn(dzb, cmb) + et * dsn
        dgm = jnp.concatenate(dgms, axis=1)
        dxd = jnp.concatenate(dxds, axis=1) + dgm * r_x
        dr = dgm * gm
        red = _dot_sel(jnp.concatenate([dy * e_x * zf - dr, dr, dxd * xs, dy * xs], axis=0), rd_ref[...], REDUCE_SPLIT)
        rowi = lax.broadcasted_iota(jnp.int32, (CHUNK, 1), 0)
        dtot = dtot + jnp.sum(red[CHUNK:2 * CHUNK], axis=0, keepdims=True)
        dacs = dacs + red[:CHUNK] + dacs_r.T + jnp.where(rowi == CHUNK - 1, dtot, 0.0)
        r2 = lax.broadcasted_iota(jnp.int32, (CHUNK, CHUNK), 0)
        c2 = lax.broadcasted_iota(jnp.int32, (CHUNK, CHUNK), 1)
        dadt = jnp.dot((c2 >= r2).astype(f32), dacs, preferred_element_type=f32, precision=HIGHEST)
        ddt = red[2 * CHUNK:3 * CHUNK] + dadt * a
        dalog_ref[...] += jnp.sum(dadt * dt, axis=0, keepdims=True) * a
        ddsk_ref[...] += jnp.sum(red[3 * CHUNK:], axis=0, keepdims=True)
        ddtr = jnp.where(lane < H, ddt * _sigmoid(dtr_ref[...] + dtb_ref[...]), 0.0)
        ddtb_ref[...] += jnp.sum(ddtr, axis=0, keepdims=True)
        dzxd_ref[:, SSM_WIDTH + CONV_DIM:] = ddtr.astype(bf16)
        dxa_bm, dxa_cm = [], []
        for g in range(SSM_GROUPS):
            cmb, bmb, _ = groups[g]
            dcbb = dcb[g].astype(bf16)
            dxa_bm.append(dbm[g] + _dot_tn(dcbb, cmb))
            dxa_cm.append(dcm[g] + _dot(dcbb, bmb))
        dxc = jnp.concatenate([dy * dsk_x + dxd * dt_x] + dxa_bm + dxa_cm, axis=1) * (sg * (1.0 + xc * (1.0 - sg)))
        ext = jnp.concatenate([dxc, next_sc[...]], axis=0)
        xin = xbc_ref[...].astype(f32)
        dxbc = cw_ref[SSM_CONV - 1:SSM_CONV, :] * dxc
        dcw = [jnp.sum(dxc * xin, axis=0, keepdims=True)]
        for s in range(1, SSM_CONV):
            later = _sel_dot(_shift_mat(CHUNK, CHUNK + HALO, s), ext, 2)
            dxbc = dxbc + cw_ref[SSM_CONV - 1 - s:SSM_CONV - s, :] * later
            dcw.insert(0, jnp.sum(later * xin, axis=0, keepdims=True))
        dzxd_ref[:, SSM_WIDTH:SSM_WIDTH + CONV_DIM] = dxbc.astype(bf16)
        dcw_ref[...] += jnp.concatenate(dcw, axis=0)
        dcb_ref[...] += jnp.sum(dxc, axis=0, keepdims=True)
        next_sc[...] = dxc[0:HALO, :]

    params = [conv_w, conv_b, dt_bias, a_log, d_skip, ssm_norm]
    mats = list(_head_mats())

    def rev(ncols):
        return pl.BlockSpec((CHUNK, ncols), lambda i: (nc - 1 - i, 0))

    in_specs = ([rev(CONV_DIM), rev(CONV_DIM), rev(CONV_DIM), rev(SSM_WIDTH), rev(SSM_WIDTH), rev(LANES),
                 pl.BlockSpec((1, PAIRS, PAIR_W, N), lambda i: (nc - 1 - i, 0, 0, 0)), rev(SSM_WIDTH)]
                + [_full(x.shape) for x in params + mats])
    return pl.pallas_call(
        body, name="ssd_bwd", grid=(nc,), in_specs=in_specs,
        out_specs=[rev(ZXD)] + [_acc(x.shape) for x in params],
        out_shape=[jax.ShapeDtypeStruct((T, ZXD), bf16)] + [jax.ShapeDtypeStruct(x.shape, f32) for x in params],
        scratch_shapes=[pltpu.VMEM((PAIRS, PAIR_W, N), f32), pltpu.VMEM((HALO, CONV_DIM), f32)],
        compiler_params=_params(("arbitrary",)))(xbc, xc, sg, y, z, dtr, sprev, dyb, *params, *mats)


def _tail(h3, p, tgt, gp, wpg, bpg, wpp, gf, tm=512):
    T, D = h3.shape
    tm = min(tm, T)

    def head(gpre, pp, h, gf_, t):
        gate = _sigmoid(gpre)
        y = _rms(h + gate * pp, gf_)
        err = y - t
        return 0.5 * jnp.sum(jnp.mean(err * err, axis=-1))

    def body(h_ref, p_ref, t_ref, gp_ref, wpg_ref, bpg_ref, wpp_ref, gf_ref,
             dh_ref, loss_ref, dgp_ref, dwpg_ref, dbpg_ref, dwpp_ref, dgf_ref):
        @pl.when(pl.program_id(0) == 0)
        def _():
            for r in (loss_ref, dgp_ref, dwpg_ref, dbpg_ref, dwpp_ref, dgf_ref):
                r[...] = jnp.zeros_like(r)

        h = h_ref[...]
        npf, np_vjp = jax.vjp(_rms, h, gp_ref[...])
        npb = npf.astype(bf16)
        pb = p_ref[...].astype(bf16)
        gpre = _dot(npb, wpg_ref[...]) + bpg_ref[...]
        kp, _, cp = wpp_ref.shape
        pp = jnp.concatenate([_dot(pb, wpp_ref[k]) for k in range(kp)], axis=1)
        loss, head_vjp = jax.vjp(head, gpre, pp, h, gf_ref[...], t_ref[...])
        dgpre, dpp, dh_a, dgf, _ = head_vjp(jnp.ones((), f32))
        loss_ref[...] += loss
        dgf_ref[...] += dgf
        dbpg_ref[...] += jnp.sum(dgpre, axis=0, keepdims=True)
        dgb = dgpre.astype(bf16)
        dwpg_ref[...] += _dot_tn(npb, dgb)
        dppb = dpp.astype(bf16)
        for k in range(kp):
            dwpp_ref[k] += _dot_tn(pb, dppb[:, k * cp:(k + 1) * cp])
        dh_b, dgp = np_vjp(_dot_nt(dgb, wpg_ref[...]))
        dgp_ref[...] += dgp
        dh_ref[...] = dh_a + dh_b

    ins = [h3, p, tgt, gp, wpg, bpg, wpp, gf]
    in_specs = [_rows(tm, D), _rows(tm, p.shape[1]), _rows(tm, D)] + [_full(x.shape) for x in ins[3:]]
    acc_shapes = [(1, LANES), gp.shape, wpg.shape, bpg.shape, wpp.shape, gf.shape]
    return pl.pallas_call(
        body, name="tail", grid=(T // tm,), in_specs=in_specs,
        out_specs=[_rows(tm, D)] + [_acc(s) for s in acc_shapes],
        out_shape=[jax.ShapeDtypeStruct((T, D), f32)] + [jax.ShapeDtypeStruct(s, f32) for s in acc_shapes],
        compiler_params=_params(("arbitrary",)))(*ins)


def _adamw(name, w, g, m, v, tr=256):
    R, C = w.shape
    tr = _row_tile(R, tr)

    def body(w_ref, g_ref, m_ref, v_ref, d_ref, mo_ref, vo_ref):
        g_ = g_ref[...]
        m_ = ADAM_B1 * m_ref[...] + (1.0 - ADAM_B1) * g_
        v_ = ADAM_B2 * v_ref[...] + (1.0 - ADAM_B2) * jnp.square(g_)
        m_hat = m_ / (1.0 - ADAM_B1 ** ADAM_STEP)
        v_hat = v_ / (1.0 - ADAM_B2 ** ADAM_STEP)
        d_ref[...] = -ADAM_LR * (m_hat / (jnp.sqrt(v_hat) + ADAM_EPS) + ADAM_WD * w_ref[...])
        mo_ref[...] = m_
        vo_ref[...] = v_

    spec = pl.BlockSpec((tr, C), lambda i: (i, 0))
    return pl.pallas_call(body, name=name, grid=(R // tr,), in_specs=[spec] * 4, out_specs=[spec] * 3,
                          out_shape=[jax.ShapeDtypeStruct((R, C), f32)] * 3,
                          compiler_params=_params(("parallel",)))(w, g, m, v)


HBM = pl.BlockSpec(memory_space=pltpu.HBM)


def _me():
    return lax.axis_index("x"), lax.axis_index("y"), lax.axis_index("c")


def _other_chips(x, y):
    return [(1 - x, y), (x, 1 - y), (1 - x, 1 - y)]


def _remote(src, dst, send_sem, recv_sem, dev):
    return pltpu.make_async_remote_copy(src_ref=src, dst_ref=dst, send_sem=send_sem, recv_sem=recv_sem,
                                        device_id=dev, device_id_type=MESH)


def _sems(n):
    return [pltpu.SemaphoreType.DMA((n,)), pltpu.SemaphoreType.DMA((n,))]


def _gather_weights(shards, split):
    n = len(shards)

    def body(*refs):
        ins, outs = refs[:n], refs[n:2 * n]
        own_send, own_recv, ici_send, ici_recv, d2d_send, d2d_recv = refs[2 * n:]
        x, y, c = _me()
        my_chip = 2 * x + y
        sibling = (x, y, 1 - c)
        chips = _other_chips(x, y)

        def rows(i, half):
            hr = shards[i].shape[0] // 2
            return pl.ds(half * hr, hr) if split[i] else pl.ds(0, shards[i].shape[0])

        sends = []
        for i in range(n):
            for j, chip in enumerate(chips):
                cp = _remote(ins[i].at[rows(i, c)], outs[i].at[my_chip, rows(i, c)],
                             ici_send.at[3 * i + j], ici_recv.at[3 * i + j], (*chip, c))
                cp.start()
                sends.append(cp)
            cp = _remote(ins[i], outs[i].at[my_chip], own_send.at[i], own_recv.at[i], sibling)
            cp.start()
            sends.append(cp)
        for i in range(n):
            for j, chip in enumerate(chips):
                s = 3 * i + j
                land = outs[i].at[2 * chip[0] + chip[1], rows(i, c)]
                _remote(land, land, ici_send.at[s], ici_recv.at[s], (*chip, c)).wait_recv()
                if split[i]:
                    cp = _remote(land, land, d2d_send.at[s], d2d_recv.at[s], sibling)
                    cp.start()
                    sends.append(cp)
        for i in range(n):
            _remote(ins[i], outs[i].at[my_chip], own_send.at[i], own_recv.at[i], sibling).wait_recv()
            if split[i]:
                for j, chip in enumerate(chips):
                    s = 3 * i + j
                    land = outs[i].at[2 * chip[0] + chip[1], rows(i, 1 - c)]
                    _remote(land, land, d2d_send.at[s], d2d_recv.at[s], sibling).wait_recv()
        for cp in sends:
            cp.wait_send()

    return pl.pallas_call(
        body, name="gather_weights", out_shape=[jax.ShapeDtypeStruct((N_CHIPS,) + s.shape, s.dtype) for s in shards],
        in_specs=[HBM] * n, out_specs=[HBM] * n,
        scratch_shapes=_sems(n) + _sems(3 * n) + _sems(3 * n))(*shards)


def _swap_halves(name, grads):
    n = len(grads)

    def body(*refs):
        ins, outs, send, recv = refs[:n], refs[n:2 * n], refs[2 * n], refs[2 * n + 1]
        x, y, c = _me()
        copies = []
        for i in range(n):
            hr = grads[i].shape[1] // 2
            cp = _remote(ins[i].at[:, pl.ds((1 - c) * hr, hr), :], outs[i], send.at[i], recv.at[i], (x, y, 1 - c))
            cp.start()
            copies.append(cp)
        for cp in copies:
            cp.wait()

    return pl.pallas_call(
        body, name=name,
        out_shape=[jax.ShapeDtypeStruct((g.shape[0], g.shape[1] // 2, g.shape[2]), g.dtype) for g in grads],
        in_specs=[HBM] * n, out_specs=[HBM] * n, scratch_shapes=_sems(n))(*grads)


def _add_halves(name, grads, other, c_idx, th=HALF_ROWS_BF16):
    K, R, C = grads.shape
    H = R // 2
    th = _row_tile(H, th, 16)
    nb = H // th

    def body(c_ref, g_ref, o_ref, out_ref):
        out_ref[...] = (g_ref[...].astype(f32) + o_ref[...].astype(f32)).astype(bf16)

    grid_spec = pltpu.PrefetchScalarGridSpec(
        num_scalar_prefetch=1, grid=(nb,),
        in_specs=[pl.BlockSpec((K, th, C), lambda i, c: (0, c[0] * nb + i, 0)),
                  pl.BlockSpec((K, th, C), lambda i, c: (0, i, 0))],
        out_specs=pl.BlockSpec((K, th, C), lambda i, c: (0, i, 0)))
    return pl.pallas_call(body, name=name, grid_spec=grid_spec,
                          out_shape=jax.ShapeDtypeStruct((K, H, C), bf16),
                          compiler_params=_params(("parallel",)))(c_idx, grads, other)


SEM = pl.BlockSpec(memory_space=pltpu.SEMAPHORE)
ANY = pl.BlockSpec(memory_space=pl.ANY)
EFFECT = pltpu.SideEffectType.DATAFLOW_SIDE_EFFECTING


def _copies_start(name, srcs, land_shapes, n_copies, make_copies, after):
    ns, nl = len(srcs), len(land_shapes)
    lands = [lax.empty(s.shape, s.dtype) for s in land_shapes]

    def body(*refs):
        src_refs, land_refs = refs[:ns], refs[ns:ns + nl]
        send, recv, token = refs[ns + nl + 1], refs[ns + nl + 2], refs[-1]
        for cp in make_copies(src_refs, land_refs, send, recv):
            cp.start()
        token[...] = jnp.zeros_like(token)

    buffers = list(srcs) + lands
    out = pl.pallas_call(
        body, name=name,
        out_shape=(pltpu.SemaphoreType.DMA((n_copies,)), pltpu.SemaphoreType.DMA((n_copies,)),
                   *[pltpu.HBM(b.shape, b.dtype) for b in buffers], jax.ShapeDtypeStruct((8, LANES), f32)),
        in_specs=[HBM] * (ns + nl) + [ANY],
        out_specs=(SEM, SEM, *[HBM] * (ns + nl), pl.BlockSpec(memory_space=pltpu.VMEM)),
        input_output_aliases={i: 2 + i for i in range(ns + nl)},
        compiler_params=pltpu.CompilerParams(has_side_effects=EFFECT),
    )(*[pltpu.with_memory_space_constraint(b, pltpu.HBM) for b in buffers], after)
    return out[0], out[1], list(out[2:2 + ns]), list(out[2 + ns:2 + ns + nl]), out[-1]


def _copies_wait(name, started, make_copies, after):
    send, recv, srcs, lands, _ = started
    ns, nl = len(srcs), len(lands)
    after = list(after)

    def body(*refs):
        src_refs, land_refs = refs[:ns], refs[ns:ns + nl]
        for cp in make_copies(src_refs, land_refs, refs[ns + nl], refs[ns + nl + 1]):
            cp.wait_send()
            cp.wait_recv()

    buffers = list(srcs) + list(lands)
    out = pl.pallas_call(
        body, name=name, out_shape=tuple(pltpu.HBM(b.shape, b.dtype) for b in buffers),
        in_specs=[HBM] * (ns + nl) + [SEM, SEM] + [ANY] * len(after), out_specs=tuple([HBM] * (ns + nl)),
        input_output_aliases={i: i for i in range(ns + nl)},
        compiler_params=pltpu.CompilerParams(has_side_effects=EFFECT),
    )(*buffers, send, recv, *after)
    return list(out[:ns]), list(out[ns:])


def _gather_copies(src_refs, land_refs, send, recv):
    x, y, c = _me()
    my_chip = 2 * x + y
    peers = [(*chip, c) for chip in _other_chips(x, y)] + [(x, y, 1 - c)]
    return [_remote(src_refs[i], land_refs[i].at[my_chip], send.at[4 * i + j], recv.at[4 * i + j], peer)
            for i in range(len(src_refs)) for j, peer in enumerate(peers)]


def _partial_copies(src_refs, land_refs, send, recv):
    x, y, c = _me()
    return [_remote(src_refs[i].at[2 * chip[0] + chip[1]], land_refs[i].at[j], send.at[3 * i + j], recv.at[3 * i + j], (*chip, c))
            for i in range(len(src_refs)) for j, chip in enumerate(_other_chips(x, y))]


def _small_copies(src_refs, land_refs, send, recv):
    x, y, c = _me()
    return [_remote(src_refs[0], land_refs[0].at[k - 1], send.at[k - 1], recv.at[k - 1], (x ^ (k >> 2), y ^ ((k >> 1) & 1), c ^ (k & 1)))
            for k in range(1, N_DEV)]


def _sum_small(own, slots, dev_idx):
    R, C = own.shape

    def body(dev_ref, own_ref, s_ref, o_ref):
        me = dev_ref[0]
        acc = jnp.zeros((R, C), f32)
        for d in range(N_DEV):
            k = me ^ d
            acc = acc + jnp.where(k == 0, own_ref[...], s_ref[jnp.maximum(k - 1, 0)])
        o_ref[...] = acc

    grid_spec = pltpu.PrefetchScalarGridSpec(
        num_scalar_prefetch=1, grid=(1,),
        in_specs=[pl.BlockSpec((R, C), lambda i, dev: (0, 0)), pl.BlockSpec((N_DEV - 1, R, C), lambda i, dev: (0, 0, 0))],
        out_specs=pl.BlockSpec((R, C), lambda i, dev: (0, 0)))
    return pl.pallas_call(body, name="sum_small", grid_spec=grid_spec, out_shape=jax.ShapeDtypeStruct((R, C), f32),
                          compiler_params=_params(("arbitrary",)))(dev_idx, own, slots)


def _sum_partials(name, part, recv, chip_idx, th=HALF_ROWS_BF16):
    K, H, C = part.shape
    th = _row_tile(H, th, 16)

    def body(chip_ref, p_ref, r_ref, o_ref):
        acc = p_ref[...].astype(f32)
        for j in range(3):
            acc = acc + r_ref[j].astype(f32)
        o_ref[...] = acc

    grid_spec = pltpu.PrefetchScalarGridSpec(
        num_scalar_prefetch=1, grid=(H // th,),
        in_specs=[pl.BlockSpec((None, th, C), lambda i, chip: (chip[0], i, 0)),
                  pl.BlockSpec((3, th, C), lambda i, chip: (0, i, 0))],
        out_specs=pl.BlockSpec((th, C), lambda i, chip: (i, 0)))
    return pl.pallas_call(body, name=name, grid_spec=grid_spec, out_shape=jax.ShapeDtypeStruct((H, C), f32),
                          compiler_params=_params(("parallel",)))(chip_idx, part, recv)


def _share_halves(name, halves):
    n = len(halves)

    def body(*refs):
        ins, outs, send, recv = refs[:n], refs[n:2 * n], refs[2 * n], refs[2 * n + 1]
        x, y, c = _me()
        copies = []
        for i in range(n):
            cp = _remote(ins[i], outs[i], send.at[i], recv.at[i], (x, y, 1 - c))
            cp.start()
            copies.append(cp)
        for cp in copies:
            cp.wait()

    return pl.pallas_call(
        body, name=name, out_shape=[jax.ShapeDtypeStruct(h.shape, h.dtype) for h in halves],
        in_specs=[HBM] * n, out_specs=[HBM] * n, scratch_shapes=_sems(n))(*halves)


def _adamw_big(name, w, g_mine, g_theirs, m, v, c_idx, tr=HALF_ROWS_F32):
    R, C = w.shape
    H = R // 2
    tr = _row_tile(H, tr)
    nb = H // tr

    def body(c_ref, w_ref, gm_ref, gt_ref, m_ref, v_ref, g_ref, d_ref, mo_ref, vo_ref):
        g_ = jnp.where(pl.program_id(0) // nb == c_ref[0], gm_ref[...], gt_ref[...])
        g_ref[...] = g_
        m_ = ADAM_B1 * m_ref[...] + (1.0 - ADAM_B1) * g_
        v_ = ADAM_B2 * v_ref[...] + (1.0 - ADAM_B2) * jnp.square(g_)
        m_hat = m_ / (1.0 - ADAM_B1 ** ADAM_STEP)
        v_hat = v_ / (1.0 - ADAM_B2 ** ADAM_STEP)
        d_ref[...] = -ADAM_LR * (m_hat / (jnp.sqrt(v_hat) + ADAM_EPS) + ADAM_WD * w_ref[...])
        mo_ref[...] = m_
        vo_ref[...] = v_

    full = pl.BlockSpec((tr, C), lambda i, c: (i, 0))
    half = pl.BlockSpec((tr, C), lambda i, c: (i % nb, 0))
    grid_spec = pltpu.PrefetchScalarGridSpec(num_scalar_prefetch=1, grid=(2 * nb,),
                                             in_specs=[full, half, half, full, full], out_specs=[full] * 4)
    return pl.pallas_call(body, name=name, grid_spec=grid_spec, out_shape=[jax.ShapeDtypeStruct((R, C), f32)] * 4,
                          compiler_params=_params(("parallel",)))(c_idx, w, g_mine, g_theirs, m, v)


BIG = ("ffn1_w_gate", "ffn1_w_up", "ffn1_w_down", "w_in", "w_out", "ffn2_w_gate", "ffn2_w_up", "ffn2_w_down",
       "ple_w_gate", "ple_w_proj")


SMALL = ("ffn1_norm", "mix_norm", "gm_ln_g", "gm_ln_b", "gm_w_s", "gm_b_s", "gm_out_norm", "conv_b", "dt_bias", "a_log",
         "d_skip", "ssm_norm", "ffn2_norm", "ple_norm", "ple_b_gate", "final_norm")
SMALL_C = 1024


def _pack_small(vals):
    parts = []
    for v in vals:
        f = v.astype(f32).reshape(-1)
        parts.append(jnp.pad(f, (0, -f.shape[0] % SMALL_C)))
    flat = jnp.concatenate(parts)
    rows = flat.shape[0] // SMALL_C
    return jnp.pad(flat, (0, (-rows % 8) * SMALL_C)).reshape(-1, SMALL_C)


def _unpack_small(pack, shapes):
    flat = pack.reshape(-1)
    out, off = [], 0
    for s in shapes:
        n = 1
        for d in s:
            n *= d
        out.append(flat[off:off + n].reshape(s))
        off += n + (-n % SMALL_C)
    return out


def _pad_lanes(v):
    return jnp.pad(v, ((0, 0), (0, LANES - v.shape[1])))


def _pad_rows(a):
    pad = [(0, 0)] * a.ndim
    pad[-2] = (0, -a.shape[-2] % ROW_PAD)
    return jnp.pad(a, pad) if pad[-2][1] else a


FETCH = (("ffn1_w_gate", "ffn1_w_up", "ffn1_w_down"), ("w_in", "conv_w", "w_out"),
         ("ffn2_w_gate", "ffn2_w_up", "ffn2_w_down", "ple_w_gate", "ple_w_proj"))
TRANSPOSED = ("ffn1_w_gate", "ffn1_w_up", "ffn2_w_gate", "ffn2_w_up", "w_in")
ROW_PAD = 32
DONE = (("ffn2_w_gate", "ffn2_w_up", "ffn2_w_down", "w_out", "ple_w_gate", "ple_w_proj"), ("w_in",),
        ("ffn1_w_gate", "ffn1_w_up", "ffn1_w_down"))


def _local_step(x, p, tgt, fetch, S, on_grads):
    G = GM_WIDTH
    K = N_CHIPS
    b_st = S["gm_b_s"][0].T
    w_s = S["gm_w_s"][0]
    dtb, alog, dsk = _pad_lanes(S["dt_bias"]), _pad_lanes(S["a_log"]), _pad_lanes(S["d_skip"])
    gfin = S["final_norm"].reshape(1, -1)

    def rows(a):
        return a.reshape(-1, D_MODEL)

    def shards(a):
        return a.reshape(K, -1, D_MODEL)

    wg1, wu1, wd1 = [rows(a) for a in fetch(0, None)]
    h1, n1, a1, b1 = _ffn_fwd("ffn1_fwd", x, S["ffn1_norm"], wg1, wu1, wd1)
    w_in4, cw4, wo4 = fetch(1, h1)
    w_in = w_in4.reshape(IN_PROJ, D_MODEL)
    w_uv = w_in[:2 * G]
    w_zxd = jnp.pad(w_in[2 * G:], ((0, ZXD - (IN_PROJ - 2 * G)), (0, 0)))
    conv_w = jnp.transpose(cw4, (1, 0, 2)).reshape(SSM_CONV, CONV_DIM)
    wo = wo4.reshape(-1, D_MODEL)
    n2, act, slope, z, xbc, dtr, ya = _mix_fwd(h1, S["mix_norm"], w_uv, w_zxd, S["gm_ln_g"], S["gm_ln_b"], w_s, b_st,
                                               S["gm_out_norm"])
    yb, xc, sg, y_ssd, sprev = _ssd_fwd(xbc, z, dtr, conv_w, S["conv_b"], dtb, alog, dsk, S["ssm_norm"])
    wg2, wu2, wd2, wpg4, wpp4 = fetch(2, yb)
    wg2, wu2, wd2 = rows(wg2), rows(wu2), rows(wd2)
    h2, h3, n3, a2, b2 = _ffn_fwd("ffn2_fwd", h1, S["ffn2_norm"], wg2, wu2, wd2, pre=(ya, yb, wo))
    dh3, loss, dgp, dwpg, dbpg, dwpp, dgf = _tail(h3, p, tgt, S["ple_norm"], wpg4.reshape(-1, D_MODEL), S["ple_b_gate"], wpp4, gfin)
    dh2, da2, db2, hm2, dg_ffn2, dya, dyb = _ffn_bwd("ffn2_bwd", dh3, h2, S["ffn2_norm"], a2, b2, wg2, wu2, wd2, wo=wo, ga=G)
    dw_out = jnp.concatenate([_matmul_tn("dw_out_a", ya, dh2), _matmul_tn("dw_out_b", yb, dh2)], axis=0).reshape(wo4.shape)
    zero = on_grads(0, [shards(_matmul_tn("dw_ffn2_gate", da2, n3)), shards(_matmul_tn("dw_ffn2_up", db2, n3)),
                        shards(_matmul_tn("dw_ffn2_down", hm2, dh3, scale=0.5)), dw_out,
                        dwpg.astype(bf16).reshape(wpg4.shape), dwpp.astype(bf16)])
    dzxd, dcw, dcb, ddtb, dalog, ddsk, dgssm = _ssd_bwd(xbc, xc, sg, y_ssd, z, dtr, sprev, dyb, conv_w, S["conv_b"], dtb, alog, dsk,
                                                        S["ssm_norm"] + zero)
    dh1, duv, dg_mix, dlng, dlnb, dws, dbst, dgout = _mix_bwd(dh2, h1, S["mix_norm"], act, slope, dya, dzxd, w_uv, w_zxd, S["gm_ln_g"],
                                                              S["gm_ln_b"], w_s, b_st, S["gm_out_norm"])
    dw_in = jnp.concatenate([_matmul_tn("dw_in_uv", duv, n2), _matmul_tn("dw_in_zxd", dzxd, n2)[:IN_PROJ - 2 * G]], axis=0)
    zero = on_grads(1, [dw_in.reshape(w_in4.shape)])
    dx, da1, db1, hm1, dg_ffn1 = _ffn_bwd("ffn1_bwd", dh1, x, S["ffn1_norm"] + zero, a1, b1, wg1, wu1, wd1)
    zero = on_grads(2, [shards(_matmul_tn("dw_ffn1_gate", da1, n1)), shards(_matmul_tn("dw_ffn1_up", db1, n1)),
                        shards(_matmul_tn("dw_ffn1_down", hm1, dh1, scale=0.5))])
    loss = loss + zero
    nh = SSM_HEADS
    gS = {"ffn1_norm": dg_ffn1, "mix_norm": dg_mix, "gm_ln_g": dlng, "gm_ln_b": dlnb, "gm_w_s": dws[None], "gm_b_s": dbst.T[None],
          "gm_out_norm": dgout, "conv_b": dcb, "dt_bias": ddtb[:, :nh], "a_log": dalog[:, :nh], "d_skip": ddsk[:, :nh],
          "ssm_norm": dgssm, "ffn2_norm": dg_ffn2, "ple_norm": dgp, "ple_b_gate": dbpg, "final_norm": dgf.reshape(-1)}
    return loss, dx, dcw, gS


_WEIGHTS = ("ffn1_norm", "ffn1_w_gate", "ffn1_w_up", "ffn1_w_down", "mix_norm", "w_in", "gm_ln_g", "gm_ln_b", "gm_w_s", "gm_b_s",
            "gm_out_norm", "conv_w", "conv_b", "dt_bias", "a_log", "d_skip", "ssm_norm", "w_out", "ffn2_norm", "ffn2_w_gate",
            "ffn2_w_up", "ffn2_w_down", "ple_norm", "ple_w_gate", "ple_b_gate", "ple_w_proj", "final_norm")
_BIG_NAMES = BIG


def kernel(x, p, ffn1_norm, ffn1_w_gate, ffn1_w_up, ffn1_w_down, mix_norm, w_in, gm_ln_g, gm_ln_b, gm_w_s, gm_b_s, gm_out_norm, conv_w, conv_b, dt_bias, a_log, d_skip, ssm_norm, w_out, ffn2_norm, ffn2_w_gate, ffn2_w_up, ffn2_w_down, ple_norm, ple_w_gate, ple_b_gate, ple_w_proj, final_norm, loss_target, m_ffn1_norm, m_ffn1_w_gate, m_ffn1_w_up, m_ffn1_w_down, m_mix_norm, m_w_in, m_gm_ln_g, m_gm_ln_b, m_gm_w_s, m_gm_b_s, m_gm_out_norm, m_conv_w, m_conv_b, m_dt_bias, m_a_log, m_d_skip, m_ssm_norm, m_w_out, m_ffn2_norm, m_ffn2_w_gate, m_ffn2_w_up, m_ffn2_w_down, m_ple_norm, m_ple_w_gate, m_ple_b_gate, m_ple_w_proj, m_final_norm, v_ffn1_norm, v_ffn1_w_gate, v_ffn1_w_up, v_ffn1_w_down, v_mix_norm, v_w_in, v_gm_ln_g, v_gm_ln_b, v_gm_w_s, v_gm_b_s, v_gm_out_norm, v_conv_w, v_conv_b, v_dt_bias, v_a_log, v_d_skip, v_ssm_norm, v_w_out, v_ffn2_norm, v_ffn2_w_gate, v_ffn2_w_up, v_ffn2_w_down, v_ple_norm, v_ple_w_gate, v_ple_b_gate, v_ple_w_proj, v_final_norm):
    given = dict(locals())
    w = {n: given[n] for n in _WEIGHTS}
    m = {n: given["m_" + n] for n in _WEIGHTS}
    v = {n: given["v_" + n] for n in _WEIGHTS}

    c_idx = lax.axis_index("c").astype(jnp.int32).reshape(1)
    chip = 2 * lax.axis_index("x") + lax.axis_index("y")
    chip_idx = chip.astype(jnp.int32).reshape(1)

    shard = {n: (jnp.swapaxes(w[n][0], 0, 1) if n in TRANSPOSED else w[n][0]).astype(bf16) for n in BIG}
    shard["conv_w"] = w["conv_w"][0]
    first = _gather_weights([shard[n] for n in FETCH[0]], [True] * len(FETCH[0]))
    fetching, after = [], first[-1]
    for k in (1, 2):
        srcs = [shard[n] for n in FETCH[k]]
        lands = [jax.ShapeDtypeStruct((N_CHIPS,) + s.shape, s.dtype) for s in srcs]
        fetching.append(_copies_start("gather%d_start" % k, srcs, lands, 4 * len(srcs), _gather_copies, after))
        after = fetching[-1][4]

    def fetch(k, after_):
        return first if k == 0 else _copies_wait("gather%d_wait" % k, fetching[k - 1], _gather_copies, [after_])[1]

    exchanging = []

    def on_grads(k, grads):
        grads = [_pad_rows(g_) for g_ in grads]
        others = _swap_halves("swap%d" % k, grads)
        parts = [_add_halves("add_" + n, g_, o_, c_idx) for n, g_, o_ in zip(DONE[k], grads, others)]
        lands = [jax.ShapeDtypeStruct((3,) + p_.shape[1:], p_.dtype) for p_ in parts]
        exchanging.append(_copies_start("exchange%d_start" % k, parts, lands, 3 * len(parts), _partial_copies, c_idx))
        return exchanging[-1][4][0, 0]

    S = {n: w[n] for n in SMALL}
    S["ffn1_norm"] = S["ffn1_norm"] + after[0, 0]
    loss, dx, dcw, gS = _local_step(x[0], p[0, 0], loss_target[0], fetch, S, on_grads)

    small = _pack_small([gS[n] for n in SMALL] + [dcw, loss[:, :1]])
    small_lands = [jax.ShapeDtypeStruct((N_DEV - 1,) + small.shape, small.dtype)]
    small_st = _copies_start("small_start", [small], small_lands, N_DEV - 1, _small_copies, c_idx)

    g, delta, new_m, new_v = {}, {}, {}, {}
    after = [small_st[4]]
    for k in range(len(DONE)):
        parts, recv = _copies_wait("exchange%d_wait" % k, exchanging[k], _partial_copies, after)
        mine = [_sum_partials("sum_" + n, p_, r_, chip_idx) for n, p_, r_ in zip(DONE[k], parts, recv)]
        theirs = _share_halves("share%d" % k, mine)
        after = []
        for n, gm_, gt_ in zip(DONE[k], mine, theirs):
            flip = (lambda a: jnp.swapaxes(a, 0, 1)) if n in TRANSPOSED else (lambda a: a)
            rows = flip(w[n][0]).shape[0]
            w_, m_, v_ = [_pad_rows(flip(a[n][0])) for a in (w, m, v)]
            outs = _adamw_big("adamw_" + n, w_, gm_, gt_, m_, v_, c_idx)
            g[n], delta[n], new_m[n], new_v[n] = [flip(o[:rows])[None] for o in outs]
            after.append(outs[3])
    (own,), (slots,) = _copies_wait("small_wait", small_st, _small_copies, after)
    dev_idx = (2 * chip + lax.axis_index("c")).astype(jnp.int32).reshape(1)
    small_shapes = [w[n].shape for n in SMALL] + [dcw.shape, (1, 1)]
    small_sum = _unpack_small(_sum_small(own, slots, dev_idx), small_shapes)
    g.update({n: small_sum[i] for i, n in enumerate(SMALL)})
    cshard = w["conv_w"].shape[2]
    g["conv_w"] = lax.dynamic_slice_in_dim(small_sum[len(SMALL)], chip * cshard, cshard, axis=1)[None]
    loss_total = small_sum[len(SMALL) + 1].reshape(())
    sm_names = SMALL + ("conv_w",)
    sm_shapes = [w[n].shape for n in sm_names]
    d_s, m_s, v_s = _adamw("adamw_small", _pack_small([w[n] for n in sm_names]), _pack_small([g[n] for n in sm_names]),
                           _pack_small([m[n] for n in sm_names]), _pack_small([v[n] for n in sm_names]))
    for dst, src in ((delta, d_s), (new_m, m_s), (new_v, v_s)):
        for n, val in zip(sm_names, _unpack_small(src, sm_shapes)):
            dst[n] = val

    return (loss_total, dx[None], *[g[n] for n in _WEIGHTS], *[delta[n] for n in _WEIGHTS],
            *[new_m[n] for n in _WEIGHTS], *[new_v[n] for n in _WEIGHTS])
```

```python
import jax
import jax.numpy as jnp
from jax import lax
from jax.experimental import pallas as pl
from jax.experimental.pallas import tpu as pltpu

f32 = jnp.float32
bf16 = jnp.bfloat16
MESH = pl.DeviceIdType.MESH
HIGHEST = lax.Precision.HIGHEST

EPS = 1e-6
N_CHIPS = 4
N_DEV = 8
D_MODEL = 1024
GM_WIDTH = 1024
GM_HEADS = 8
CHUNK = 128
SSM_WIDTH = 1024
SSM_HEADS = 16
SSM_HEAD_DIM = 64
SSM_GROUPS = 2
SSM_STATE = 128
SSM_CONV = 4
CONV_DIM = SSM_WIDTH + 2 * SSM_GROUPS * SSM_STATE
IN_PROJ = 2 * GM_WIDTH + SSM_WIDTH + CONV_DIM + SSM_HEADS
LANES = 128
ZXD = SSM_WIDTH + CONV_DIM + LANES

ADAM_LR = 0.001
ADAM_B1 = 0.9
ADAM_B2 = 0.999
ADAM_EPS = 1e-08
ADAM_WD = 0.01
ADAM_STEP = 10

VMEM_LIMIT = 56 * 1024 * 1024
HALF_ROWS_BF16 = 592
HALF_ROWS_F32 = 320


def _dot(a, b):
    return jnp.dot(a, b, preferred_element_type=f32)


def _dot_nt(a, b):
    return lax.dot_general(a, b, (((1,), (1,)), ((), ())), preferred_element_type=f32)


def _dot_tn(a, b):
    return lax.dot_general(a, b, (((0,), (0,)), ((), ())), preferred_element_type=f32)


def _rms(x, g):
    return x * lax.rsqrt(jnp.mean(x * x, axis=-1, keepdims=True) + EPS) * g


def _layernorm(x, g, b):
    mu = jnp.mean(x, axis=-1, keepdims=True)
    xc = x - mu
    return xc * lax.rsqrt(jnp.mean(xc * xc, axis=-1, keepdims=True) + EPS) * g + b


def _sigmoid(x):
    return 1.0 / (1.0 + jnp.exp(-x))


def _softplus(x):
    return jnp.maximum(x, 0.0) + jnp.log(1.0 + jnp.exp(-jnp.abs(x)))


def _full(shape):
    nd = len(shape)
    return pl.BlockSpec(shape, lambda *_: (0,) * nd, pipeline_mode=pl.Buffered(1))


def _acc(shape):
    nd = len(shape)
    return pl.BlockSpec(shape, lambda *_: (0,) * nd)


def _rows(tm, ncols):
    return pl.BlockSpec((tm, ncols), lambda i: (i, 0))


def _params(sem):
    return pltpu.CompilerParams(dimension_semantics=sem, vmem_limit_bytes=VMEM_LIMIT)


def _row_tile(rows, target, mult=8):
    best = rows
    for t in range(mult, min(rows, target) + 1, mult):
        if rows % t == 0:
            best = t
    return best if best <= target else rows


def _ffn_fwd(name, h, g, wg, wu, wd, pre=None, tm=256):
    T, D = h.shape
    F = wg.shape[0]
    tm = min(tm, T)

    def body(*refs):
        if pre is None:
            h_ref, g_ref, wg_ref, wu_ref, wd_ref, ho_ref, n_ref, a_ref, b_ref = refs
            hin = h_ref[...]
        else:
            (h_ref, ya_ref, yb_ref, wo_ref, g_ref, wg_ref, wu_ref, wd_ref,
             hi_ref, ho_ref, n_ref, a_ref, b_ref) = refs
            ga = ya_ref.shape[1]
            hin = h_ref[...] + _dot(ya_ref[...], wo_ref[:ga, :]) + _dot(yb_ref[...], wo_ref[ga:, :])
            hi_ref[...] = hin
        n = _rms(hin, g_ref[...]).astype(bf16)
        n_ref[...] = n
        a = _dot_nt(n, wg_ref[...]).astype(bf16)
        b = _dot_nt(n, wu_ref[...]).astype(bf16)
        a_ref[...] = a
        b_ref[...] = b
        af = a.astype(f32)
        hm = (af * _sigmoid(af) * b.astype(f32)).astype(bf16)
        ho_ref[...] = hin + 0.5 * _dot(hm, wd_ref[...])

    ins = [h] + (list(pre) if pre is not None else []) + [g, wg, wu, wd]
    in_specs = [_rows(tm, D)]
    if pre is not None:
        in_specs += [_rows(tm, pre[0].shape[1]), _rows(tm, pre[1].shape[1]), _full(pre[2].shape)]
    in_specs += [_full(g.shape), _full(wg.shape), _full(wu.shape), _full(wd.shape)]
    outs = [jax.ShapeDtypeStruct((T, D), f32), jax.ShapeDtypeStruct((T, D), bf16),
            jax.ShapeDtypeStruct((T, F), bf16), jax.ShapeDtypeStruct((T, F), bf16)]
    out_specs = [_rows(tm, D), _rows(tm, D), _rows(tm, F), _rows(tm, F)]
    if pre is not None:
        outs = [jax.ShapeDtypeStruct((T, D), f32)] + outs
        out_specs = [_rows(tm, D)] + out_specs
    return pl.pallas_call(body, name=name, grid=(T // tm,), in_specs=in_specs, out_specs=out_specs,
                          out_shape=outs, compiler_params=_params(("parallel",)))(*ins)


def _ffn_bwd(name, dh, hin, g, a, b, wg, wu, wd, wo=None, ga=0, tm=256):
    T, D = dh.shape
    F = wg.shape[0]
    tm = min(tm, T)

    def body(*refs):
        if wo is None:
            (dh_ref, hin_ref, g_ref, a_ref, b_ref, wg_ref, wu_ref, wd_ref,
             dhi_ref, da_ref, db_ref, hm_ref, dg_ref) = refs
        else:
            (dh_ref, hin_ref, g_ref, a_ref, b_ref, wg_ref, wu_ref, wd_ref, wo_ref,
             dhi_ref, da_ref, db_ref, hm_ref, dg_ref, dya_ref, dyb_ref) = refs

        @pl.when(pl.program_id(0) == 0)
        def _():
            dg_ref[...] = jnp.zeros_like(dg_ref)

        dh_ = dh_ref[...]
        dhb = (0.5 * dh_).astype(bf16)
        dhm = _dot_nt(dhb, wd_ref[...])
        af = a_ref[...].astype(f32)
        bf = b_ref[...].astype(f32)
        sg = _sigmoid(af)
        sl_ = af * sg
        da = (dhm * bf * (sg * (1.0 + af * (1.0 - sg)))).astype(bf16)
        db = (dhm * sl_).astype(bf16)
        da_ref[...] = da
        db_ref[...] = db
        hm_ref[...] = (sl_ * bf).astype(bf16)
        dn = _dot(da, wg_ref[...]) + _dot(db, wu_ref[...])
        _, vjp = jax.vjp(_rms, hin_ref[...], g_ref[...])
        dx, dg = vjp(dn)
        dhi = dh_ + dx
        dhi_ref[...] = dhi
        dg_ref[...] += dg
        if wo is not None:
            dhib = dhi.astype(bf16)
            dya_ref[...] = _dot_nt(dhib, wo_ref[:ga, :]).astype(bf16)
            dyb_ref[...] = _dot_nt(dhib, wo_ref[ga:, :]).astype(bf16)

    ins = [dh, hin, g, a, b, wg, wu, wd]
    in_specs = [_rows(tm, D), _rows(tm, D), _full(g.shape), _rows(tm, F), _rows(tm, F),
                _full(wg.shape), _full(wu.shape), _full(wd.shape)]
    act = jax.ShapeDtypeStruct((T, F), bf16)
    outs = [jax.ShapeDtypeStruct((T, D), f32), act, act, act, jax.ShapeDtypeStruct(g.shape, f32)]
    out_specs = [_rows(tm, D), _rows(tm, F), _rows(tm, F), _rows(tm, F), _acc(g.shape)]
    if wo is not None:
        gb = wo.shape[0] - ga
        ins += [wo]
        in_specs += [_full(wo.shape)]
        outs += [jax.ShapeDtypeStruct((T, ga), bf16), jax.ShapeDtypeStruct((T, gb), bf16)]
        out_specs += [_rows(tm, ga), _rows(tm, gb)]
    return pl.pallas_call(body, name=name, grid=(T // tm,), in_specs=in_specs, out_specs=out_specs,
                          out_shape=outs, compiler_params=_params(("arbitrary",)))(*ins)


def _matmul_tn(name, a, b, scale=1.0, tk=2048):
    T, M = a.shape
    N = b.shape[1]
    tk = min(tk, T)
    nk = T // tk
    tn = LANES * max(d for d in range(1, N // LANES + 1) if (N // LANES) % d == 0 and (d == 1 or M * d * LANES * 4 <= 6 * 1024 * 1024))

    def body(a_ref, b_ref, o_ref, acc):
        k = pl.program_id(1)

        @pl.when(k == 0)
        def _():
            acc[...] = jnp.zeros_like(acc)

        bb = b_ref[...]
        if scale != 1.0:
            bb = bb * scale
        acc[...] += _dot_tn(a_ref[...].astype(bf16), bb.astype(bf16))

        @pl.when(k == nk - 1)
        def _():
            o_ref[...] = acc[...].astype(bf16)

    return pl.pallas_call(
        body, name=name, grid=(N // tn, nk),
        in_specs=[pl.BlockSpec((tk, M), lambda j, k: (k, 0)), pl.BlockSpec((tk, tn), lambda j, k: (k, j))],
        out_specs=pl.BlockSpec((M, tn), lambda j, k: (0, j)),
        out_shape=jax.ShapeDtypeStruct((M, N), bf16), scratch_shapes=[pltpu.VMEM((M, tn), f32)],
        compiler_params=_params(("parallel", "arbitrary")))(a, b)


def _gelu_and_slope(x):
    cdf = 0.5 * (1.0 + lax.erf(x * 0.7071067811865476))
    return x * cdf, cdf + x * (0.3989422804014327 * jnp.exp(-0.5 * x * x))


def _tril_mask():
    r = lax.broadcasted_iota(jnp.int32, (CHUNK, CHUNK), 0)
    c = lax.broadcasted_iota(jnp.int32, (CHUNK, CHUNK), 1)
    return c <= r


def _gm_mix(vnb, ws_ref, bst, mixed_sc, tm):
    mask = _tril_mask()
    for h in range(GM_HEADS):
        wt = jnp.where(mask, ws_ref[h], 0.0).astype(bf16)
        bias = bst[:, h:h + 1]
        for q in range(tm // CHUNK):
            rs = slice(q * CHUNK, (q + 1) * CHUNK)
            cs = slice(h * CHUNK, (h + 1) * CHUNK)
            mixed_sc[rs, cs] = _dot(wt, vnb[rs, cs]) + bias


def _mix_fwd(h1, gmix, w_uv, w_zxd, ln_g, ln_b, w_s, b_st, gout, tm=512):
    T, D = h1.shape
    tm = min(tm, T)
    G = GM_WIDTH

    def body(h_ref, g_ref, wuv_ref, wzxd_ref, lng_ref, lnb_ref, ws_ref, bst_ref, gout_ref,
             n_ref, act_ref, slope_ref, z_ref, xbc_ref, dt_ref, ya_ref, mixed_sc):
        n = _rms(h_ref[...], g_ref[...]).astype(bf16)
        n_ref[...] = n
        uv = _dot_nt(n, wuv_ref[...]).astype(bf16)
        zxd = _dot_nt(n, wzxd_ref[...])
        z_ref[...] = zxd[:, :SSM_WIDTH].astype(bf16)
        xbc_ref[...] = zxd[:, SSM_WIDTH:SSM_WIDTH + CONV_DIM].astype(bf16)
        dt_ref[...] = zxd[:, SSM_WIDTH + CONV_DIM:]
        act, slope = _gelu_and_slope(uv.astype(f32))
        act = act.astype(bf16)
        act_ref[...] = act
        slope_ref[...] = slope.astype(bf16)
        ug, vg = act[:, :G].astype(f32), act[:, G:].astype(f32)
        _gm_mix(_layernorm(vg, lng_ref[...], lnb_ref[...]).astype(bf16), ws_ref, bst_ref[...], mixed_sc, tm)
        ya_ref[...] = _rms(ug * mixed_sc[...], gout_ref[...]).astype(bf16)

    ins = [h1, gmix, w_uv, w_zxd, ln_g, ln_b, w_s, b_st, gout]
    in_specs = [_rows(tm, D)] + [_full(x.shape) for x in ins[1:]]
    outs = [jax.ShapeDtypeStruct((T, D), bf16), jax.ShapeDtypeStruct((T, 2 * G), bf16), jax.ShapeDtypeStruct((T, 2 * G), bf16),
            jax.ShapeDtypeStruct((T, SSM_WIDTH), bf16), jax.ShapeDtypeStruct((T, CONV_DIM), bf16),
            jax.ShapeDtypeStruct((T, LANES), f32), jax.ShapeDtypeStruct((T, G), bf16)]
    out_specs = [_rows(tm, D), _rows(tm, 2 * G), _rows(tm, 2 * G), _rows(tm, SSM_WIDTH), _rows(tm, CONV_DIM), _rows(tm, LANES),
                 _rows(tm, G)]
    return pl.pallas_call(body, name="mix_fwd", grid=(T // tm,), in_specs=in_specs, out_specs=out_specs,
                          out_shape=outs, scratch_shapes=[pltpu.VMEM((tm, G), f32)],
                          compiler_params=_params(("parallel",)))(*ins)


def _mix_bwd(dh, h1, gmix, act, slope, dya, dzxd, w_uv, w_zxd, ln_g, ln_b, w_s, b_st, gout, tm=256):
    T, D = dh.shape
    tm = min(tm, T)
    G = GM_WIDTH

    def body(dh_ref, h_ref, g_ref, act_ref, slope_ref, dya_ref, dzxd_ref, wuv_ref, wzxd_ref, lng_ref, lnb_ref, ws_ref,
             bst_ref, gout_ref, dhi_ref, duv_ref, dg_ref, dlng_ref, dlnb_ref, dws_ref, dbst_ref, dgout_ref, mixed_sc, dvn_sc):
        @pl.when(pl.program_id(0) == 0)
        def _():
            for r in (dg_ref, dlng_ref, dlnb_ref, dws_ref, dbst_ref, dgout_ref):
                r[...] = jnp.zeros_like(r)

        dn_z = _dot(dzxd_ref[...], wzxd_ref[...])
        ug = act_ref[:, :G].astype(f32)
        vn, ln_vjp = jax.vjp(_layernorm, act_ref[:, G:].astype(f32), lng_ref[...], lnb_ref[...])
        vnb = vn.astype(bf16)
        _gm_mix(vnb, ws_ref, bst_ref[...], mixed_sc, tm)
        mixed = mixed_sc[...]
        _, out_vjp = jax.vjp(_rms, ug * mixed, gout_ref[...])
        dpre, dgout = out_vjp(dya_ref[...].astype(f32))
        dgout_ref[...] += dgout
        dug = dpre * mixed
        dmixed = dpre * ug
        mask = _tril_mask()
        lane = lax.broadcasted_iota(jnp.int32, (1, GM_HEADS), 1)
        dbst = jnp.zeros((CHUNK, GM_HEADS), f32)
        for h in range(GM_HEADS):
            wt = jnp.where(mask, ws_ref[h], 0.0).astype(bf16)
            cs = slice(h * CHUNK, (h + 1) * CHUNK)
            dw = jnp.zeros((CHUNK, CHUNK), f32)
            for q in range(tm // CHUNK):
                rs = slice(q * CHUNK, (q + 1) * CHUNK)
                dm = dmixed[rs, cs]
                dmb = dm.astype(bf16)
                dw = dw + _dot_nt(dmb, vnb[rs, cs])
                dbst = dbst + jnp.sum(dm, axis=1, keepdims=True) * (lane == h).astype(f32)
                dvn_sc[rs, cs] = _dot_tn(wt, dmb)
            dws_ref[h] += jnp.where(mask, dw, 0.0)
        dbst_ref[...] += dbst
        dvg, dlng, dlnb = ln_vjp(dvn_sc[...])
        duv = (jnp.concatenate([dug, dvg], axis=1) * slope_ref[...].astype(f32)).astype(bf16)
        duv_ref[...] = duv
        dlng_ref[...] += dlng
        dlnb_ref[...] += dlnb
        dn = dn_z + _dot(duv, wuv_ref[...])
        _, vjp = jax.vjp(_rms, h_ref[...], g_ref[...])
        dx, dg = vjp(dn)
        dhi_ref[...] = dh_ref[...] + dx
        dg_ref[...] += dg

    ins = [dh, h1, gmix, act, slope, dya, dzxd, w_uv, w_zxd, ln_g, ln_b, w_s, b_st, gout]
    in_specs = ([_rows(tm, D), _rows(tm, D), _full(gmix.shape), _rows(tm, 2 * G), _rows(tm, 2 * G), _rows(tm, G),
                 _rows(tm, dzxd.shape[1])] + [_full(x.shape) for x in ins[7:]])
    accs = (gmix, ln_g, ln_b, w_s, b_st, gout)
    outs = ([jax.ShapeDtypeStruct((T, D), f32), jax.ShapeDtypeStruct((T, 2 * G), bf16)]
            + [jax.ShapeDtypeStruct(x.shape, f32) for x in accs])
    out_specs = [_rows(tm, D), _rows(tm, 2 * G)] + [_acc(x.shape) for x in accs]
    return pl.pallas_call(body, name="mix_bwd", grid=(T // tm,), in_specs=in_specs, out_specs=out_specs,
                          out_shape=outs, scratch_shapes=[pltpu.VMEM((tm, G), f32), pltpu.VMEM((tm, G), f32)],
                          compiler_params=_params(("arbitrary",)))(*ins)


HALO = 16
PAIRS = SSM_HEADS // 2
PAIR_W = 2 * SSM_HEAD_DIM


def _split(x, n):
    parts = []
    for _ in range(n):
        p = x.astype(bf16)
        parts.append(p)
        x = x - p.astype(f32)
    return parts


def _dot_sel(x, sel_n, n):
    return _dot(jnp.concatenate(_split(x, n), axis=1), sel_n)


def _sel_dot(sel, x, n):
    return _dot(jnp.concatenate([sel] * n, axis=1), jnp.concatenate(_split(x, n), axis=0))


EXPAND_SPLIT = 3
REDUCE_SPLIT = 2


def _head_mats():
    ex = (jnp.arange(SSM_WIDTH)[None, :] // SSM_HEAD_DIM == jnp.arange(LANES)[:, None]).astype(bf16)
    return jnp.tile(ex, (EXPAND_SPLIT, 1)), jnp.tile(ex.T, (REDUCE_SPLIT, 1))


def _shift_mat(rows, cols, off):
    r = lax.broadcasted_iota(jnp.int32, (rows, cols), 0)
    c = lax.broadcasted_iota(jnp.int32, (rows, cols), 1)
    return (c == r + off).astype(bf16)


def _ssd_conv(c, xbc_ref, halo_ref, cw_ref, cb_ref):
    halo = halo_ref[...]
    ext = jnp.concatenate([jnp.where(c > 0, halo, jnp.zeros_like(halo)), xbc_ref[...]], axis=0)
    xc = cb_ref[...] + cw_ref[SSM_CONV - 1:SSM_CONV, :] * xbc_ref[...].astype(f32)
    for j in range(SSM_CONV - 1):
        xc = xc + cw_ref[j:j + 1, :] * _dot(_shift_mat(CHUNK, HALO + CHUNK, HALO - SSM_CONV + 1 + j), ext)
    return xc


def _ssd_front(dtr_ref, dtb_ref, alog_ref):
    dt = _softplus(dtr_ref[...] + dtb_ref[...])
    a = -jnp.exp(alog_ref[...])
    acs = jnp.dot(_tril_mask().astype(f32), dt * a, preferred_element_type=f32, precision=HIGHEST)
    return dt, a, acs


def _ssd_wide(xa, dt, acs, dsk, ex):
    dt_x = _dot_sel(dt, ex, EXPAND_SPLIT)
    acs_x = _dot_sel(acs, ex, EXPAND_SPLIT)
    dsk_x = _dot_sel(jnp.broadcast_to(dsk, (8, LANES)), ex, EXPAND_SPLIT)[0:1]
    e_x = jnp.exp(acs_x)
    r_x = jnp.exp(acs_x[CHUNK - 1:CHUNK, :] - acs_x)
    xs = xa[:, :SSM_WIDTH]
    xd = xs * dt_x
    return dt_x, dsk_x, e_x, r_x, xs, xd, xd * r_x


def _pair_stack(v, lo):
    return jnp.concatenate([jnp.where(lo, v, 0.0), jnp.where(lo, 0.0, v)], axis=0)


def _ssd_pair(j, acs, acs_t, cb):
    out = []
    tril = _tril_mask()
    for h in (2 * j, 2 * j + 1):
        dk = jnp.exp(jnp.where(tril, acs[:, h:h + 1] - acs_t[h:h + 1, :], -jnp.inf))
        out.append((dk, cb * dk))
    return out


def _pair_col(row_lo, tot, j):
    return jnp.exp(jnp.where(row_lo, tot[:, 2 * j:2 * j + 1], tot[:, 2 * j + 1:2 * j + 2]))


def _gated_norm(y, z, g):
    yg = y * (z * _sigmoid(z))
    half = SSM_WIDTH // SSM_GROUPS
    parts = []
    for k in range(SSM_GROUPS):
        s = yg[:, k * half:(k + 1) * half]
        parts.append(s * lax.rsqrt(jnp.mean(s * s, axis=-1, keepdims=True) + EPS))
    return jnp.concatenate(parts, axis=1) * g


def _group_mats(xa):
    out = []
    for g in range(SSM_GROUPS):
        bm = xa[:, SSM_WIDTH + g * SSM_STATE:SSM_WIDTH + (g + 1) * SSM_STATE].astype(bf16)
        cm = xa[:, SSM_WIDTH + (SSM_GROUPS + g) * SSM_STATE:SSM_WIDTH + (SSM_GROUPS + g + 1) * SSM_STATE].astype(bf16)
        out.append((cm, bm, _dot_nt(cm, bm)))
    return out


def _ssd_fwd(xbc, z, dtr, conv_w, conv_b, dt_bias, a_log, d_skip, ssm_norm):
    T = xbc.shape[0]
    nc = T // CHUNK
    N = SSM_STATE

    def body(xbc_ref, halo_ref, z_ref, dtr_ref, cw_ref, cb_ref, dtb_ref, alog_ref, dsk_ref, g_ref, ex_ref,
             yb_ref, xc_ref, sg_ref, y_ref, sprev_ref, s_sc):
        c = pl.program_id(0)

        @pl.when(c == 0)
        def _():
            s_sc[...] = jnp.zeros_like(s_sc)

        xc = _ssd_conv(c, xbc_ref, halo_ref, cw_ref, cb_ref)
        sg = _sigmoid(xc)
        xc_ref[...] = xc
        sg_ref[...] = sg
        xa = xc * sg
        dt, _, acs = _ssd_front(dtr_ref, dtb_ref, alog_ref)
        _, dsk_x, e_x, _, xs, xd, gm = _ssd_wide(xa, dt, acs, dsk_ref[...], ex_ref[...])
        acs_t = acs.T
        tot = acs[CHUNK - 1:CHUNK, :]
        groups = _group_mats(xa)
        lo = lax.broadcasted_iota(jnp.int32, (CHUNK, PAIR_W), 1) < SSM_HEAD_DIM
        row_lo = lax.broadcasted_iota(jnp.int32, (PAIR_W, 1), 0) < SSM_HEAD_DIM
        ys = []
        for j in range(PAIRS):
            cmb, bmb, cb = groups[j // (PAIRS // SSM_GROUPS)]
            ps = slice(j * PAIR_W, (j + 1) * PAIR_W)
            (_, m0), (_, m1) = _ssd_pair(j, acs, acs_t, cb)
            sp = s_sc[j]
            yd = _dot(jnp.concatenate([m0, m1], axis=1).astype(bf16), _pair_stack(xd[:, ps], lo).astype(bf16))
            ys.append(yd + e_x[:, ps] * _dot_nt(cmb, sp.astype(bf16)))
            sprev_ref[0, j] = sp
            s_sc[j] = _pair_col(row_lo, tot, j) * sp + _dot_tn(gm[:, ps].astype(bf16), bmb)
        y = jnp.concatenate(ys, axis=1) + xs * dsk_x
        y_ref[...] = y
        yb_ref[...] = _gated_norm(y, z_ref[...].astype(f32), g_ref[...]).astype(bf16)

    params = [conv_w, conv_b, dt_bias, a_log, d_skip, ssm_norm, _head_mats()[0]]
    hp = CHUNK // HALO
    in_specs = [_rows(CHUNK, CONV_DIM), pl.BlockSpec((HALO, CONV_DIM), lambda i: (jnp.maximum(i * hp - 1, 0), 0)),
                _rows(CHUNK, SSM_WIDTH), _rows(CHUNK, LANES)] + [_full(x.shape) for x in params]
    return pl.pallas_call(
        body, name="ssd_fwd", grid=(nc,), in_specs=in_specs,
        out_specs=[_rows(CHUNK, SSM_WIDTH), _rows(CHUNK, CONV_DIM), _rows(CHUNK, CONV_DIM), _rows(CHUNK, SSM_WIDTH),
                   pl.BlockSpec((1, PAIRS, PAIR_W, N), lambda i: (i, 0, 0, 0))],
        out_shape=[jax.ShapeDtypeStruct((T, SSM_WIDTH), bf16), jax.ShapeDtypeStruct((T, CONV_DIM), f32),
                   jax.ShapeDtypeStruct((T, CONV_DIM), f32), jax.ShapeDtypeStruct((T, SSM_WIDTH), f32),
                   jax.ShapeDtypeStruct((nc, PAIRS, PAIR_W, N), f32)],
        scratch_shapes=[pltpu.VMEM((PAIRS, PAIR_W, N), f32)],
        compiler_params=_params(("arbitrary",)))(xbc, xbc, z, dtr, *params)


def _ssd_bwd(xbc, xc, sg, y, z, dtr, sprev, dyb, conv_w, conv_b, dt_bias, a_log, d_skip, ssm_norm):
    T = xbc.shape[0]
    nc = T // CHUNK
    H, N = SSM_HEADS, SSM_STATE
    PG = PAIRS // SSM_GROUPS

    def body(xbc_ref, xc_ref, sg_ref, y_ref, z_ref, dtr_ref, sprev_ref, dyb_ref, cw_ref, cb_ref, dtb_ref, alog_ref, dsk_ref,
             g_ref, ex_ref, rd_ref, dzxd_ref, dcw_ref, dcb_ref, ddtb_ref, dalog_ref, ddsk_ref, dg_ref, ds_sc, next_sc):
        i = pl.program_id(0)

        @pl.when(i == 0)
        def _():
            ds_sc[...] = jnp.zeros_like(ds_sc)
            next_sc[...] = jnp.zeros_like(next_sc)
            for r_ in (dcw_ref, dcb_ref, ddtb_ref, dalog_ref, ddsk_ref, dg_ref):
                r_[...] = jnp.zeros_like(r_)

        xc = xc_ref[...]
        sg = sg_ref[...]
        xa = xc * sg
        dt, a, acs = _ssd_front(dtr_ref, dtb_ref, alog_ref)
        dt_x, dsk_x, e_x, r_x, xs, xd, gm = _ssd_wide(xa, dt, acs, dsk_ref[...], ex_ref[...])
        acs_t = acs.T
        tot = acs[CHUNK - 1:CHUNK, :]
        groups = _group_mats(xa)
        lo = lax.broadcasted_iota(jnp.int32, (CHUNK, PAIR_W), 1) < SSM_HEAD_DIM
        row_lo = lax.broadcasted_iota(jnp.int32, (PAIR_W, 1), 0) < SSM_HEAD_DIM
        pairs, zs = [], []
        for j in range(PAIRS):
            cmb, _, cb = groups[j // PG]
            pairs.append(_ssd_pair(j, acs, acs_t, cb))
            zs.append(_dot_nt(cmb, sprev_ref[0, j].astype(bf16)))
        zf = jnp.concatenate(zs, axis=1)
        _, gn_vjp = jax.vjp(_gated_norm, y_ref[...], z_ref[...].astype(f32), g_ref[...])
        dy, dz, dg = gn_vjp(dyb_ref[...].astype(f32))
        dg_ref[...] += dg
        dzxd_ref[:, :SSM_WIDTH] = dz.astype(bf16)

        lane = lax.broadcasted_iota(jnp.int32, (1, LANES), 1)
        sub = lax.broadcasted_iota(jnp.int32, (LANES, 1), 0)
        dacs = jnp.zeros((CHUNK, LANES), f32)
        dacs_r = jnp.zeros((LANES, CHUNK), f32)
        dtot = jnp.zeros((1, LANES), f32)
        dcb = [jnp.zeros((CHUNK, CHUNK), f32) for _ in range(SSM_GROUPS)]
        dcm = [jnp.zeros((CHUNK, N), f32) for _ in range(SSM_GROUPS)]
        dbm = [jnp.zeros((CHUNK, N), f32) for _ in range(SSM_GROUPS)]
        dxds, dgms = [], []
        for j in range(PAIRS):
            g = j // PG
            cmb, bmb, _ = groups[g]
            ps = slice(j * PAIR_W, (j + 1) * PAIR_W)
            (dk0, m0), (dk1, m1) = pairs[j]
            oh0, oh1 = (lane == 2 * j).astype(f32), (lane == 2 * j + 1).astype(f32)
            dyp = dy[:, ps]
            dy2 = _pair_stack(dyp, lo).astype(bf16)
            dm2 = _dot_nt(dy2, xd[:, ps].astype(bf16))
            m2 = jnp.concatenate([m0, m1], axis=0)
            dxds.append(_dot_tn(m2.astype(bf16), dy2))
            w2 = dm2 * m2
            rs = jnp.sum(w2, axis=1, keepdims=True)
            dacs = dacs + rs[:CHUNK] * oh0 + rs[CHUNK:] * oh1
            dacs_r = dacs_r - ((sub == 2 * j).astype(f32) * jnp.sum(w2[:CHUNK], axis=0, keepdims=True)
                               + (sub == 2 * j + 1).astype(f32) * jnp.sum(w2[CHUNK:], axis=0, keepdims=True))
            dcb[g] = dcb[g] + dm2[:CHUNK] * dk0 + dm2[CHUNK:] * dk1
            sp = sprev_ref[0, j]
            dzb = (dyp * e_x[:, ps]).astype(bf16)
            dcm[g] = dcm[g] + _dot(dzb, sp.astype(bf16))
            dsn = ds_sc[j]
            dsnb = dsn.astype(bf16)
            et = _pair_col(row_lo, tot, j)
            rr = jnp.sum(dsn * sp, axis=1, keepdims=True) * et
            dtot = dtot + jnp.sum(rr[:SSM_HEAD_DIM]) * oh0 + jnp.sum(rr[SSM_HEAD_DIM:]) * oh1
            dgms.append(_dot_nt(bmb, dsnb))
            dbm[g] = dbm[g] + _dot(gm[:, ps].astype(bf16), dsnb)
            ds_sc[j] = _dot_tn(dzb, cmb) + et * dsn
        dgm = jnp.concatenate(dgms, axis=1)
        dxd = jnp.concatenate(dxds, axis=1) + dgm * r_x
        dr = dgm * gm
        red = _dot_sel(jnp.concatenate([dy * e_x * zf - dr, dr, dxd * xs, dy * xs], axis=0), rd_ref[...], REDUCE_SPLIT)
        rowi = lax.broadcasted_iota(jnp.int32, (CHUNK, 1), 0)
        dtot = dtot + jnp.sum(red[CHUNK:2 * CHUNK], axis=0, keepdims=True)
        dacs = dacs + red[:CHUNK] + dacs_r.T + jnp.where(rowi == CHUNK - 1, dtot, 0.0)
        r2 = lax.broadcasted_iota(jnp.int32, (CHUNK, CHUNK), 0)
        c2 = lax.broadcasted_iota(jnp.int32, (CHUNK, CHUNK), 1)
        dadt = jnp.dot((c2 >= r2).astype(f32), dacs, preferred_element_type=f32, precision=HIGHEST)
        ddt = red[2 * CHUNK:3 * CHUNK] + dadt * a
        dalog_ref[...] += jnp.sum(dadt * dt, axis=0, keepdims=True) * a
        ddsk_ref[...] += jnp.sum(red[3 * CHUNK:], axis=0, keepdims=True)
        ddtr = jnp.where(lane < H, ddt * _sigmoid(dtr_ref[...] + dtb_ref[...]), 0.0)
        ddtb_ref[...] += jnp.sum(ddtr, axis=0, keepdims=True)
        dzxd_ref[:, SSM_WIDTH + CONV_DIM:] = ddtr.astype(bf16)
        dxa_bm, dxa_cm = [], []
        for g in range(SSM_GROUPS):
            cmb, bmb, _ = groups[g]
            dcbb = dcb[g].astype(bf16)
            dxa_bm.append(dbm[g] + _dot_tn(dcbb, cmb))
            dxa_cm.append(dcm[g] + _dot(dcbb, bmb))
        dxc = jnp.concatenate([dy * dsk_x + dxd * dt_x] + dxa_bm + dxa_cm, axis=1) * (sg * (1.0 + xc * (1.0 - sg)))
        ext = jnp.concatenate([dxc, next_sc[...]], axis=0)
        xin = xbc_ref[...].astype(f32)
        dxbc = cw_ref[SSM_CONV - 1:SSM_CONV, :] * dxc
        dcw = [jnp.sum(dxc * xin, axis=0, keepdims=True)]
        for s in range(1, SSM_CONV):
            later = _sel_dot(_shift_mat(CHUNK, CHUNK + HALO, s), ext, 2)
            dxbc = dxbc + cw_ref[SSM_CONV - 1 - s:SSM_CONV - s, :] * later
            dcw.insert(0, jnp.sum(later * xin, axis=0, keepdims=True))
        dzxd_ref[:, SSM_WIDTH:SSM_WIDTH + CONV_DIM] = dxbc.astype(bf16)
        dcw_ref[...] += jnp.concatenate(dcw, axis=0)
        dcb_ref[...] += jnp.sum(dxc, axis=0, keepdims=True)
        next_sc[...] = dxc[0:HALO, :]

    params = [conv_w, conv_b, dt_bias, a_log, d_skip, ssm_norm]
    mats = list(_head_mats())

    def rev(ncols):
        return pl.BlockSpec((CHUNK, ncols), lambda i: (nc - 1 - i, 0))

    in_specs = ([rev(CONV_DIM), rev(CONV_DIM), rev(CONV_DIM), rev(SSM_WIDTH), rev(SSM_WIDTH), rev(LANES),
                 pl.BlockSpec((1, PAIRS, PAIR_W, N), lambda i: (nc - 1 - i, 0, 0, 0)), rev(SSM_WIDTH)]
                + [_full(x.shape) for x in params + mats])
    return pl.pallas_call(
        body, name="ssd_bwd", grid=(nc,), in_specs=in_specs,
        out_specs=[rev(ZXD)] + [_acc(x.shape) for x in params],
        out_shape=[jax.ShapeDtypeStruct((T, ZXD), bf16)] + [jax.ShapeDtypeStruct(x.shape, f32) for x in params],
        scratch_shapes=[pltpu.VMEM((PAIRS, PAIR_W, N), f32), pltpu.VMEM((HALO, CONV_DIM), f32)],
        compiler_params=_params(("arbitrary",)))(xbc, xc, sg, y, z, dtr, sprev, dyb, *params, *mats)


def _tail(h3, p, tgt, gp, wpg, bpg, wpp, gf, tm=512):
    T, D = h3.shape
    tm = min(tm, T)

    def head(gpre, pp, h, gf_, t):
        gate = _sigmoid(gpre)
        y = _rms(h + gate * pp, gf_)
        err = y - t
        return 0.5 * jnp.sum(jnp.mean(err * err, axis=-1))

    def body(h_ref, p_ref, t_ref, gp_ref, wpg_ref, bpg_ref, wpp_ref, gf_ref,
             dh_ref, loss_ref, dgp_ref, dwpg_ref, dbpg_ref, dwpp_ref, dgf_ref):
        @pl.when(pl.program_id(0) == 0)
        def _():
            for r in (loss_ref, dgp_ref, dwpg_ref, dbpg_ref, dwpp_ref, dgf_ref):
                r[...] = jnp.zeros_like(r)

        h = h_ref[...]
        npf, np_vjp = jax.vjp(_rms, h, gp_ref[...])
        npb = npf.astype(bf16)
        pb = p_ref[...].astype(bf16)
        gpre = _dot(npb, wpg_ref[...]) + bpg_ref[...]
        kp, _, cp = wpp_ref.shape
        pp = jnp.concatenate([_dot(pb, wpp_ref[k]) for k in range(kp)], axis=1)
        loss, head_vjp = jax.vjp(head, gpre, pp, h, gf_ref[...], t_ref[...])
        dgpre, dpp, dh_a, dgf, _ = head_vjp(jnp.ones((), f32))
        loss_ref[...] += loss
        dgf_ref[...] += dgf
        dbpg_ref[...] += jnp.sum(dgpre, axis=0, keepdims=True)
        dgb = dgpre.astype(bf16)
        dwpg_ref[...] += _dot_tn(npb, dgb)
        dppb = dpp.astype(bf16)
        for k in range(kp):
            dwpp_ref[k] += _dot_tn(pb, dppb[:, k * cp:(k + 1) * cp])
        dh_b, dgp = np_vjp(_dot_nt(dgb, wpg_ref[...]))
        dgp_ref[...] += dgp
        dh_ref[...] = dh_a + dh_b

    ins = [h3, p, tgt, gp, wpg, bpg, wpp, gf]
    in_specs = [_rows(tm, D), _rows(tm, p.shape[1]), _rows(tm, D)] + [_full(x.shape) for x in ins[3:]]
    acc_shapes = [(1, LANES), gp.shape, wpg.shape, bpg.shape, wpp.shape, gf.shape]
    return pl.pallas_call(
        body, name="tail", grid=(T // tm,), in_specs=in_specs,
        out_specs=[_rows(tm, D)] + [_acc(s) for s in acc_shapes],
        out_shape=[jax.ShapeDtypeStruct((T, D), f32)] + [jax.ShapeDtypeStruct(s, f32) for s in acc_shapes],
        compiler_params=_params(("arbitrary",)))(*ins)


def _adamw(name, w, g, m, v, tr=256):
    R, C = w.shape
    tr = _row_tile(R, tr)

    def body(w_ref, g_ref, m_ref, v_ref, d_ref, mo_ref, vo_ref):
        g_ = g_ref[...]
        m_ = ADAM_B1 * m_ref[...] + (1.0 - ADAM_B1) * g_
        v_ = ADAM_B2 * v_ref[...] + (1.0 - ADAM_B2) * jnp.square(g_)
        m_hat = m_ / (1.0 - ADAM_B1 ** ADAM_STEP)
        v_hat = v_ / (1.0 - ADAM_B2 ** ADAM_STEP)
        d_ref[...] = -ADAM_LR * (m_hat / (jnp.sqrt(v_hat) + ADAM_EPS) + ADAM_WD * w_ref[...])
        mo_ref[...] = m_
        vo_ref[...] = v_

    spec = pl.BlockSpec((tr, C), lambda i: (i, 0))
    return pl.pallas_call(body, name=name, grid=(R // tr,), in_specs=[spec] * 4, out_specs=[spec] * 3,
                          out_shape=[jax.ShapeDtypeStruct((R, C), f32)] * 3,
                          compiler_params=_params(("parallel",)))(w, g, m, v)


HBM = pl.BlockSpec(memory_space=pltpu.HBM)


def _me():
    return lax.axis_index("x"), lax.axis_index("y"), lax.axis_index("c")


def _other_chips(x, y):
    return [(1 - x, y), (x, 1 - y), (1 - x, 1 - y)]


def _remote(src, dst, send_sem, recv_sem, dev):
    return pltpu.make_async_remote_copy(src_ref=src, dst_ref=dst, send_sem=send_sem, recv_sem=recv_sem,
                                        device_id=dev, device_id_type=MESH)


def _sems(n):
    return [pltpu.SemaphoreType.DMA((n,)), pltpu.SemaphoreType.DMA((n,))]


def _gather_weights(shards, split):
    n = len(shards)

    def body(*refs):
        ins, outs = refs[:n], refs[n:2 * n]
        own_send, own_recv, ici_send, ici_recv, d2d_send, d2d_recv = refs[2 * n:]
        x, y, c = _me()
        my_chip = 2 * x + y
        sibling = (x, y, 1 - c)
        chips = _other_chips(x, y)

        def rows(i, half):
            hr = shards[i].shape[0] // 2
            return pl.ds(half * hr, hr) if split[i] else pl.ds(0, shards[i].shape[0])

        sends = []
        for i in range(n):
            for j, chip in enumerate(chips):
                cp = _remote(ins[i].at[rows(i, c)], outs[i].at[my_chip, rows(i, c)],
                             ici_send.at[3 * i + j], ici_recv.at[3 * i + j], (*chip, c))
                cp.start()
                sends.append(cp)
            cp = _remote(ins[i], outs[i].at[my_chip], own_send.at[i], own_recv.at[i], sibling)
            cp.start()
            sends.append(cp)
        for i in range(n):
            for j, chip in enumerate(chips):
                s = 3 * i + j
                land = outs[i].at[2 * chip[0] + chip[1], rows(i, c)]
                _remote(land, land, ici_send.at[s], ici_recv.at[s], (*chip, c)).wait_recv()
                if split[i]:
                    cp = _remote(land, land, d2d_send.at[s], d2d_recv.at[s], sibling)
                    cp.start()
                    sends.append(cp)
        for i in range(n):
            _remote(ins[i], outs[i].at[my_chip], own_send.at[i], own_recv.at[i], sibling).wait_recv()
            if split[i]:
                for j, chip in enumerate(chips):
                    s = 3 * i + j
                    land = outs[i].at[2 * chip[0] + chip[1], rows(i, 1 - c)]
                    _remote(land, land, d2d_send.at[s], d2d_recv.at[s], sibling).wait_recv()
        for cp in sends:
            cp.wait_send()

    return pl.pallas_call(
        body, name="gather_weights", out_shape=[jax.ShapeDtypeStruct((N_CHIPS,) + s.shape, s.dtype) for s in shards],
        in_specs=[HBM] * n, out_specs=[HBM] * n,
        scratch_shapes=_sems(n) + _sems(3 * n) + _sems(3 * n))(*shards)


def _swap_halves(name, grads):
    n = len(grads)

    def body(*refs):
        ins, outs, send, recv = refs[:n], refs[n:2 * n], refs[2 * n], refs[2 * n + 1]
        x, y, c = _me()
        copies = []
        for i in range(n):
            hr = grads[i].shape[1] // 2
            cp = _remote(ins[i].at[:, pl.ds((1 - c) * hr, hr), :], outs[i], send.at[i], recv.at[i], (x, y, 1 - c))
            cp.start()
            copies.append(cp)
        for cp in copies:
            cp.wait()

    return pl.pallas_call(
        body, name=name,
        out_shape=[jax.ShapeDtypeStruct((g.shape[0], g.shape[1] // 2, g.shape[2]), g.dtype) for g in grads],
        in_specs=[HBM] * n, out_specs=[HBM] * n, scratch_shapes=_sems(n))(*grads)


def _add_halves(name, grads, other, c_idx, th=HALF_ROWS_BF16):
    K, R, C = grads.shape
    H = R // 2
    th = _row_tile(H, th, 16)
    nb = H // th

    def body(c_ref, g_ref, o_ref, out_ref):
        out_ref[...] = (g_ref[...].astype(f32) + o_ref[...].astype(f32)).astype(bf16)

    grid_spec = pltpu.PrefetchScalarGridSpec(
        num_scalar_prefetch=1, grid=(nb,),
        in_specs=[pl.BlockSpec((K, th, C), lambda i, c: (0, c[0] * nb + i, 0)),
                  pl.BlockSpec((K, th, C), lambda i, c: (0, i, 0))],
        out_specs=pl.BlockSpec((K, th, C), lambda i, c: (0, i, 0)))
    return pl.pallas_call(body, name=name, grid_spec=grid_spec,
                          out_shape=jax.ShapeDtypeStruct((K, H, C), bf16),
                          compiler_params=_params(("parallel",)))(c_idx, grads, other)


SEM = pl.BlockSpec(memory_space=pltpu.SEMAPHORE)
ANY = pl.BlockSpec(memory_space=pl.ANY)
EFFECT = pltpu.SideEffectType.DATAFLOW_SIDE_EFFECTING


def _copies_start(name, srcs, land_shapes, n_copies, make_copies, after):
    ns, nl = len(srcs), len(land_shapes)
    lands = [lax.empty(s.shape, s.dtype) for s in land_shapes]

    def body(*refs):
        src_refs, land_refs = refs[:ns], refs[ns:ns + nl]
        send, recv, token = refs[ns + nl + 1], refs[ns + nl + 2], refs[-1]
        for cp in make_copies(src_refs, land_refs, send, recv):
            cp.start()
        token[...] = jnp.zeros_like(token)

    buffers = list(srcs) + lands
    out = pl.pallas_call(
        body, name=name,
        out_shape=(pltpu.SemaphoreType.DMA((n_copies,)), pltpu.SemaphoreType.DMA((n_copies,)),
                   *[pltpu.HBM(b.shape, b.dtype) for b in buffers], jax.ShapeDtypeStruct((8, LANES), f32)),
        in_specs=[HBM] * (ns + nl) + [ANY],
        out_specs=(SEM, SEM, *[HBM] * (ns + nl), pl.BlockSpec(memory_space=pltpu.VMEM)),
        input_output_aliases={i: 2 + i for i in range(ns + nl)},
        compiler_params=pltpu.CompilerParams(has_side_effects=EFFECT),
    )(*[pltpu.with_memory_space_constraint(b, pltpu.HBM) for b in buffers], after)
    return out[0], out[1], list(out[2:2 + ns]), list(out[2 + ns:2 + ns + nl]), out[-1]


def _copies_wait(name, started, make_copies, after):
    send, recv, srcs, lands, _ = started
    ns, nl = len(srcs), len(lands)
    after = list(after)

    def body(*refs):
        src_refs, land_refs = refs[:ns], refs[ns:ns + nl]
        for cp in make_copies(src_refs, land_refs, refs[ns + nl], refs[ns + nl + 1]):
            cp.wait_send()
            cp.wait_recv()

    buffers = list(srcs) + list(lands)
    out = pl.pallas_call(
        body, name=name, out_shape=tuple(pltpu.HBM(b.shape, b.dtype) for b in buffers),
        in_specs=[HBM] * (ns + nl) + [SEM, SEM] + [ANY] * len(after), out_specs=tuple([HBM] * (ns + nl)),
        input_output_aliases={i: i for i in range(ns + nl)},
        compiler_params=pltpu.CompilerParams(has_side_effects=EFFECT),
    )(*buffers, send, recv, *after)
    return list(out[:ns]), list(out[ns:])


def _gather_copies(src_refs, land_refs, send, recv):
    x, y, c = _me()
    my_chip = 2 * x + y
    peers = [(*chip, c) for chip in _other_chips(x, y)] + [(x, y, 1 - c)]
    return [_remote(src_refs[i], land_refs[i].at[my_chip], send.at[4 * i + j], recv.at[4 * i + j], peer)
            for i in range(len(src_refs)) for j, peer in enumerate(peers)]


def _partial_copies(src_refs, land_refs, send, recv):
    x, y, c = _me()
    return [_remote(src_refs[i].at[2 * chip[0] + chip[1]], land_refs[i].at[j], send.at[3 * i + j], recv.at[3 * i + j], (*chip, c))
            for i in range(len(src_refs)) for j, chip in enumerate(_other_chips(x, y))]


def _small_copies(src_refs, land_refs, send, recv):
    x, y, c = _me()
    return [_remote(src_refs[0], land_refs[0].at[k - 1], send.at[k - 1], recv.at[k - 1], (x ^ (k >> 2), y ^ ((k >> 1) & 1), c ^ (k & 1)))
            for k in range(1, N_DEV)]


def _sum_small(own, slots, dev_idx):
    R, C = own.shape

    def body(dev_ref, own_ref, s_ref, o_ref):
        me = dev_ref[0]
        acc = jnp.zeros((R, C), f32)
        for d in range(N_DEV):
            k = me ^ d
            acc = acc + jnp.where(k == 0, own_ref[...], s_ref[jnp.maximum(k - 1, 0)])
        o_ref[...] = acc

    grid_spec = pltpu.PrefetchScalarGridSpec(
        num_scalar_prefetch=1, grid=(1,),
        in_specs=[pl.BlockSpec((R, C), lambda i, dev: (0, 0)), pl.BlockSpec((N_DEV - 1, R, C), lambda i, dev: (0, 0, 0))],
        out_specs=pl.BlockSpec((R, C), lambda i, dev: (0, 0)))
    return pl.pallas_call(body, name="sum_small", grid_spec=grid_spec, out_shape=jax.ShapeDtypeStruct((R, C), f32),
                          compiler_params=_params(("arbitrary",)))(dev_idx, own, slots)


def _sum_partials(name, part, recv, chip_idx, th=HALF_ROWS_BF16):
    K, H, C = part.shape
    th = _row_tile(H, th, 16)

    def body(chip_ref, p_ref, r_ref, o_ref):
        acc = p_ref[...].astype(f32)
        for j in range(3):
            acc = acc + r_ref[j].astype(f32)
        o_ref[...] = acc

    grid_spec = pltpu.PrefetchScalarGridSpec(
        num_scalar_prefetch=1, grid=(H // th,),
        in_specs=[pl.BlockSpec((None, th, C), lambda i, chip: (chip[0], i, 0)),
                  pl.BlockSpec((3, th, C), lambda i, chip: (0, i, 0))],
        out_specs=pl.BlockSpec((th, C), lambda i, chip: (i, 0)))
    return pl.pallas_call(body, name=name, grid_spec=grid_spec, out_shape=jax.ShapeDtypeStruct((H, C), f32),
                          compiler_params=_params(("parallel",)))(chip_idx, part, recv)


def _share_halves(name, halves):
    n = len(halves)

    def body(*refs):
        ins, outs, send, recv = refs[:n], refs[n:2 * n], refs[2 * n], refs[2 * n + 1]
        x, y, c = _me()
        copies = []
        for i in range(n):
            cp = _remote(ins[i], outs[i], send.at[i], recv.at[i], (x, y, 1 - c))
            cp.start()
            copies.append(cp)
        for cp in copies:
            cp.wait()

    return pl.pallas_call(
        body, name=name, out_shape=[jax.ShapeDtypeStruct(h.shape, h.dtype) for h in halves],
        in_specs=[HBM] * n, out_specs=[HBM] * n, scratch_shapes=_sems(n))(*halves)


def _adamw_big(name, w, g_mine, g_theirs, m, v, c_idx, tr=HALF_ROWS_F32):
    R, C = w.shape
    H = R // 2
    tr = _row_tile(H, tr)
    nb = H // tr

    def body(c_ref, w_ref, gm_ref, gt_ref, m_ref, v_ref, g_ref, d_ref, mo_ref, vo_ref):
        g_ = jnp.where(pl.program_id(0) // nb == c_ref[0], gm_ref[...], gt_ref[...])
        g_ref[...] = g_
        m_ = ADAM_B1 * m_ref[...] + (1.0 - ADAM_B1) * g_
        v_ = ADAM_B2 * v_ref[...] + (1.0 - ADAM_B2) * jnp.square(g_)
        m_hat = m_ / (1.0 - ADAM_B1 ** ADAM_STEP)
        v_hat = v_ / (1.0 - ADAM_B2 ** ADAM_STEP)
        d_ref[...] = -ADAM_LR * (m_hat / (jnp.sqrt(v_hat) + ADAM_EPS) + ADAM_WD * w_ref[...])
        mo_ref[...] = m_
        vo_ref[...] = v_

    full = pl.BlockSpec((tr, C), lambda i, c: (i, 0))
    mine = pl.BlockSpec((tr, C), lambda i, c: (jnp.where(i // nb == c[0], i % nb, 0), 0))
    theirs = pl.BlockSpec((tr, C), lambda i, c: (jnp.where(i // nb == c[0], 0, i % nb), 0))
    grid_spec = pltpu.PrefetchScalarGridSpec(num_scalar_prefetch=1, grid=(2 * nb,),
                                             in_specs=[full, mine, theirs, full, full], out_specs=[full] * 4)
    return pl.pallas_call(body, name=name, grid_spec=grid_spec, out_shape=[jax.ShapeDtypeStruct((R, C), f32)] * 4,
                          compiler_params=_params(("parallel",)))(c_idx, w, g_mine, g_theirs, m, v)


BIG = ("ffn1_w_gate", "ffn1_w_up", "ffn1_w_down", "w_in", "w_out", "ffn2_w_gate", "ffn2_w_up", "ffn2_w_down",
       "ple_w_gate", "ple_w_proj")


SMALL = ("ffn1_norm", "mix_norm", "gm_ln_g", "gm_ln_b", "gm_w_s", "gm_b_s", "gm_out_norm", "conv_b", "dt_bias", "a_log",
         "d_skip", "ssm_norm", "ffn2_norm", "ple_norm", "ple_b_gate", "final_norm")
SMALL_C = 1024


def _pack_small(vals):
    parts = []
    for v in vals:
        f = v.astype(f32).reshape(-1)
        parts.append(jnp.pad(f, (0, -f.shape[0] % SMALL_C)))
    flat = jnp.concatenate(parts)
    rows = flat.shape[0] // SMALL_C
    return jnp.pad(flat, (0, (-rows % 8) * SMALL_C)).reshape(-1, SMALL_C)


def _unpack_small(pack, shapes):
    flat = pack.reshape(-1)
    out, off = [], 0
    for s in shapes:
        n = 1
        for d in s:
            n *= d
        out.append(flat[off:off + n].reshape(s))
        off += n + (-n % SMALL_C)
    return out


def _pad_lanes(v):
    return jnp.pad(v, ((0, 0), (0, LANES - v.shape[1])))


def _pad_rows(a):
    pad = [(0, 0)] * a.ndim
    pad[-2] = (0, -a.shape[-2] % ROW_PAD)
    return jnp.pad(a, pad) if pad[-2][1] else a


FETCH = (("ffn1_w_gate", "ffn1_w_up", "ffn1_w_down"), ("w_in", "conv_w", "w_out"),
         ("ffn2_w_gate", "ffn2_w_up", "ffn2_w_down", "ple_w_gate", "ple_w_proj"))
TRANSPOSED = ("ffn1_w_gate", "ffn1_w_up", "ffn2_w_gate", "ffn2_w_up", "w_in")
ROW_PAD = 32
DONE = (("ffn2_w_gate", "ffn2_w_up", "ffn2_w_down", "w_out", "ple_w_gate", "ple_w_proj"), ("w_in",),
        ("ffn1_w_gate", "ffn1_w_up", "ffn1_w_down"))


def _local_step(x, p, tgt, fetch, S, on_grads):
    G = GM_WIDTH
    K = N_CHIPS
    b_st = S["gm_b_s"][0].T
    w_s = S["gm_w_s"][0]
    dtb, alog, dsk = _pad_lanes(S["dt_bias"]), _pad_lanes(S["a_log"]), _pad_lanes(S["d_skip"])
    gfin = S["final_norm"].reshape(1, -1)

    def rows(a):
        return a.reshape(-1, D_MODEL)

    def shards(a):
        return a.reshape(K, -1, D_MODEL)

    wg1, wu1, wd1 = [rows(a) for a in fetch(0, None)]
    h1, n1, a1, b1 = _ffn_fwd("ffn1_fwd", x, S["ffn1_norm"], wg1, wu1, wd1)
    w_in4, cw4, wo4 = fetch(1, h1)
    w_in = w_in4.reshape(IN_PROJ, D_MODEL)
    w_uv = w_in[:2 * G]
    w_zxd = jnp.pad(w_in[2 * G:], ((0, ZXD - (IN_PROJ - 2 * G)), (0, 0)))
    conv_w = jnp.transpose(cw4, (1, 0, 2)).reshape(SSM_CONV, CONV_DIM)
    wo = wo4.reshape(-1, D_MODEL)
    n2, act, slope, z, xbc, dtr, ya = _mix_fwd(h1, S["mix_norm"], w_uv, w_zxd, S["gm_ln_g"], S["gm_ln_b"], w_s, b_st,
                                               S["gm_out_norm"])
    yb, xc, sg, y_ssd, sprev = _ssd_fwd(xbc, z, dtr, conv_w, S["conv_b"], dtb, alog, dsk, S["ssm_norm"])
    wg2, wu2, wd2, wpg4, wpp4 = fetch(2, yb)
    wg2, wu2, wd2 = rows(wg2), rows(wu2), rows(wd2)
    h2, h3, n3, a2, b2 = _ffn_fwd("ffn2_fwd", h1, S["ffn2_norm"], wg2, wu2, wd2, pre=(ya, yb, wo))
    dh3, loss, dgp, dwpg, dbpg, dwpp, dgf = _tail(h3, p, tgt, S["ple_norm"], wpg4.reshape(-1, D_MODEL), S["ple_b_gate"], wpp4, gfin)
    dh2, da2, db2, hm2, dg_ffn2, dya, dyb = _ffn_bwd("ffn2_bwd", dh3, h2, S["ffn2_norm"], a2, b2, wg2, wu2, wd2, wo=wo, ga=G)
    dw_out = jnp.concatenate([_matmul_tn("dw_out_a", ya, dh2), _matmul_tn("dw_out_b", yb, dh2)], axis=0).reshape(wo4.shape)
    zero = on_grads(0, [shards(_matmul_tn("dw_ffn2_gate", da2, n3)), shards(_matmul_tn("dw_ffn2_up", db2, n3)),
                        shards(_matmul_tn("dw_ffn2_down", hm2, dh3, scale=0.5)), dw_out,
                        dwpg.astype(bf16).reshape(wpg4.shape), dwpp.astype(bf16)])
    dzxd, dcw, dcb, ddtb, dalog, ddsk, dgssm = _ssd_bwd(xbc, xc, sg, y_ssd, z, dtr, sprev, dyb, conv_w, S["conv_b"], dtb, alog, dsk,
                                                        S["ssm_norm"] + zero)
    dh1, duv, dg_mix, dlng, dlnb, dws, dbst, dgout = _mix_bwd(dh2, h1, S["mix_norm"], act, slope, dya, dzxd, w_uv, w_zxd, S["gm_ln_g"],
                                                              S["gm_ln_b"], w_s, b_st, S["gm_out_norm"])
    dw_in = jnp.concatenate([_matmul_tn("dw_in_uv", duv, n2), _matmul_tn("dw_in_zxd", dzxd, n2)[:IN_PROJ - 2 * G]], axis=0)
    zero = on_grads(1, [dw_in.reshape(w_in4.shape)])
    dx, da1, db1, hm1, dg_ffn1 = _ffn_bwd("ffn1_bwd", dh1, x, S["ffn1_norm"] + zero, a1, b1, wg1, wu1, wd1)
    zero = on_grads(2, [shards(_matmul_tn("dw_ffn1_gate", da1, n1)), shards(_matmul_tn("dw_ffn1_up", db1, n1)),
                        shards(_matmul_tn("dw_ffn1_down", hm1, dh1, scale=0.5))])
    loss = loss + zero
    nh = SSM_HEADS
    gS = {"ffn1_norm": dg_ffn1, "mix_norm": dg_mix, "gm_ln_g": dlng, "gm_ln_b": dlnb, "gm_w_s": dws[None], "gm_b_s": dbst.T[None],
          "gm_out_norm": dgout, "conv_b": dcb, "dt_bias": ddtb[:, :nh], "a_log": dalog[:, :nh], "d_skip": ddsk[:, :nh],
          "ssm_norm": dgssm, "ffn2_norm": dg_ffn2, "ple_norm": dgp, "ple_b_gate": dbpg, "final_norm": dgf.reshape(-1)}
    return loss, dx, dcw, gS


_WEIGHTS = ("ffn1_norm", "ffn1_w_gate", "ffn1_w_up", "ffn1_w_down", "mix_norm", "w_in", "gm_ln_g", "gm_ln_b", "gm_w_s", "gm_b_s",
            "gm_out_norm", "conv_w", "conv_b", "dt_bias", "a_log", "d_skip", "ssm_norm", "w_out", "ffn2_norm", "ffn2_w_gate",
            "ffn2_w_up", "ffn2_w_down", "ple_norm", "ple_w_gate", "ple_b_gate", "ple_w_proj", "final_norm")
_BIG_NAMES = BIG


def kernel(x, p, ffn1_norm, ffn1_w_gate, ffn1_w_up, ffn1_w_down, mix_norm, w_in, gm_ln_g, gm_ln_b, gm_w_s, gm_b_s, gm_out_norm, conv_w, conv_b, dt_bias, a_log, d_skip, ssm_norm, w_out, ffn2_norm, ffn2_w_gate, ffn2_w_up, ffn2_w_down, ple_norm, ple_w_gate, ple_b_gate, ple_w_proj, final_norm, loss_target, m_ffn1_norm, m_ffn1_w_gate, m_ffn1_w_up, m_ffn1_w_down, m_mix_norm, m_w_in, m_gm_ln_g, m_gm_ln_b, m_gm_w_s, m_gm_b_s, m_gm_out_norm, m_conv_w, m_conv_b, m_dt_bias, m_a_log, m_d_skip, m_ssm_norm, m_w_out, m_ffn2_norm, m_ffn2_w_gate, m_ffn2_w_up, m_ffn2_w_down, m_ple_norm, m_ple_w_gate, m_ple_b_gate, m_ple_w_proj, m_final_norm, v_ffn1_norm, v_ffn1_w_gate, v_ffn1_w_up, v_ffn1_w_down, v_mix_norm, v_w_in, v_gm_ln_g, v_gm_ln_b, v_gm_w_s, v_gm_b_s, v_gm_out_norm, v_conv_w, v_conv_b, v_dt_bias, v_a_log, v_d_skip, v_ssm_norm, v_w_out, v_ffn2_norm, v_ffn2_w_gate, v_ffn2_w_up, v_ffn2_w_down, v_ple_norm, v_ple_w_gate, v_ple_b_gate, v_ple_w_proj, v_final_norm):
    given = dict(locals())
    w = {n: given[n] for n in _WEIGHTS}
    m = {n: given["m_" + n] for n in _WEIGHTS}
    v = {n: given["v_" + n] for n in _WEIGHTS}

    c_idx = lax.axis_index("c").astype(jnp.int32).reshape(1)
    chip = 2 * lax.axis_index("x") + lax.axis_index("y")
    chip_idx = chip.astype(jnp.int32).reshape(1)

    shard = {n: (jnp.swapaxes(w[n][0], 0, 1) if n in TRANSPOSED else w[n][0]).astype(bf16) for n in BIG}
    shard["conv_w"] = w["conv_w"][0]
    first = _gather_weights([shard[n] for n in FETCH[0]], [True] * len(FETCH[0]))
    fetching, after = [], first[-1]
    for k in (1, 2):
        srcs = [shard[n] for n in FETCH[k]]
        lands = [jax.ShapeDtypeStruct((N_CHIPS,) + s.shape, s.dtype) for s in srcs]
        fetching.append(_copies_start("gather%d_start" % k, srcs, lands, 4 * len(srcs), _gather_copies, after))
        after = fetching[-1][4]

    def fetch(k, after_):
        return first if k == 0 else _copies_wait("gather%d_wait" % k, fetching[k - 1], _gather_copies, [after_])[1]

    exchanging = []

    def on_grads(k, grads):
        grads = [_pad_rows(g_) for g_ in grads]
        others = _swap_halves("swap%d" % k, grads)
        parts = [_add_halves("add_" + n, g_, o_, c_idx) for n, g_, o_ in zip(DONE[k], grads, others)]
        lands = [jax.ShapeDtypeStruct((3,) + p_.shape[1:], p_.dtype) for p_ in parts]
        exchanging.append(_copies_start("exchange%d_start" % k, parts, lands, 3 * len(parts), _partial_copies, c_idx))
        return exchanging[-1][4][0, 0]

    S = {n: w[n] for n in SMALL}
    S["ffn1_norm"] = S["ffn1_norm"] + after[0, 0]
    loss, dx, dcw, gS = _local_step(x[0], p[0, 0], loss_target[0], fetch, S, on_grads)

    small = _pack_small([gS[n] for n in SMALL] + [dcw, loss[:, :1]])
    small_lands = [jax.ShapeDtypeStruct((N_DEV - 1,) + small.shape, small.dtype)]
    small_st = _copies_start("small_start", [small], small_lands, N_DEV - 1, _small_copies, c_idx)

    g, delta, new_m, new_v = {}, {}, {}, {}
    after = [small_st[4]]
    for k in range(len(DONE)):
        parts, recv = _copies_wait("exchange%d_wait" % k, exchanging[k], _partial_copies, after)
        mine = [_sum_partials("sum_" + n, p_, r_, chip_idx) for n, p_, r_ in zip(DONE[k], parts, recv)]
        theirs = _share_halves("share%d" % k, mine)
        after = []
        for n, gm_, gt_ in zip(DONE[k], mine, theirs):
            flip = (lambda a: jnp.swapaxes(a, 0, 1)) if n in TRANSPOSED else (lambda a: a)
            rows = flip(w[n][0]).shape[0]
            w_, m_, v_ = [_pad_rows(flip(a[n][0])) for a in (w, m, v)]
            outs = _adamw_big("adamw_" + n, w_, gm_, gt_, m_, v_, c_idx)
            g[n], delta[n], new_m[n], new_v[n] = [flip(o[:rows])[None] for o in outs]
            after.append(outs[3])
    (own,), (slots,) = _copies_wait("small_wait", small_st, _small_copies, after)
    dev_idx = (2 * chip + lax.axis_index("c")).astype(jnp.int32).reshape(1)
    small_shapes = [w[n].shape for n in SMALL] + [dcw.shape, (1, 1)]
    small_sum = _unpack_small(_sum_small(own, slots, dev_idx), small_shapes)
    g.update({n: small_sum[i] for i, n in enumerate(SMALL)})
    cshard = w["conv_w"].shape[2]
    g["conv_w"] = lax.dynamic_slice_in_dim(small_sum[len(SMALL)], chip * cshard, cshard, axis=1)[None]
    loss_total = small_sum[len(SMALL) + 1].reshape(())
    sm_names = SMALL + ("conv_w",)
    sm_shapes = [w[n].shape for n in sm_names]
    d_s, m_s, v_s = _adamw("adamw_small", _pack_small([w[n] for n in sm_names]), _pack_small([g[n] for n in sm_names]),
                           _pack_small([m[n] for n in sm_names]), _pack_small([v[n] for n in sm_names]))
    for dst, src in ((delta, d_s), (new_m, m_s), (new_v, v_s)):
        for n, val in zip(sm_names, _unpack_small(src, sm_shapes)):
            dst[n] = val

    return (loss_total, dx[None], *[g[n] for n in _WEIGHTS], *[delta[n] for n in _WEIGHTS],
            *[new_m[n] for n in _WEIGHTS], *[new_v[n] for n in _WEIGHTS])
```

```python
import jax
import jax.numpy as jnp
from jax import lax
from jax.experimental import pallas as pl
from jax.experimental.pallas import tpu as pltpu

f32 = jnp.float32
bf16 = jnp.bfloat16
MESH = pl.DeviceIdType.MESH
HIGHEST = lax.Precision.HIGHEST

EPS = 1e-6
N_CHIPS = 4
N_DEV = 8
D_MODEL = 1024
GM_WIDTH = 1024
GM_HEADS = 8
CHUNK = 128
SSM_WIDTH = 1024
SSM_HEADS = 16
SSM_HEAD_DIM = 64
SSM_GROUPS = 2
SSM_STATE = 128
SSM_CONV = 4
CONV_DIM = SSM_WIDTH + 2 * SSM_GROUPS * SSM_STATE
IN_PROJ = 2 * GM_WIDTH + SSM_WIDTH + CONV_DIM + SSM_HEADS
LANES = 128
ZXD = SSM_WIDTH + CONV_DIM + LANES

ADAM_LR = 0.001
ADAM_B1 = 0.9
ADAM_B2 = 0.999
ADAM_EPS = 1e-08
ADAM_WD = 0.01
ADAM_STEP = 10

VMEM_LIMIT = 56 * 1024 * 1024
HALF_ROWS_BF16 = 592
HALF_ROWS_F32 = 320


def _dot(a, b):
    return jnp.dot(a, b, preferred_element_type=f32)


def _dot_nt(a, b):
    return lax.dot_general(a, b, (((1,), (1,)), ((), ())), preferred_element_type=f32)


def _dot_tn(a, b):
    return lax.dot_general(a, b, (((0,), (0,)), ((), ())), preferred_element_type=f32)


def _rms(x, g):
    return x * lax.rsqrt(jnp.mean(x * x, axis=-1, keepdims=True) + EPS) * g


def _layernorm(x, g, b):
    mu = jnp.mean(x, axis=-1, keepdims=True)
    xc = x - mu
    return xc * lax.rsqrt(jnp.mean(xc * xc, axis=-1, keepdims=True) + EPS) * g + b


def _sigmoid(x):
    return 1.0 / (1.0 + jnp.exp(-x))


def _softplus(x):
    return jnp.maximum(x, 0.0) + jnp.log(1.0 + jnp.exp(-jnp.abs(x)))


def _full(shape):
    nd = len(shape)
    return pl.BlockSpec(shape, lambda *_: (0,) * nd, pipeline_mode=pl.Buffered(1))


def _acc(shape):
    nd = len(shape)
    return pl.BlockSpec(shape, lambda *_: (0,) * nd)


def _rows(tm, ncols):
    return pl.BlockSpec((tm, ncols), lambda i: (i, 0))


def _params(sem):
    return pltpu.CompilerParams(dimension_semantics=sem, vmem_limit_bytes=VMEM_LIMIT)


def _row_tile(rows, target, mult=8):
    best = rows
    for t in range(mult, min(rows, target) + 1, mult):
        if rows % t == 0:
            best = t
    return best if best <= target else rows


def _ffn_fwd(name, h, g, wg, wu, wd, pre=None, tm=256):
    T, D = h.shape
    F = wg.shape[0]
    tm = min(tm, T)

    def body(*refs):
        if pre is None:
            h_ref, g_ref, wg_ref, wu_ref, wd_ref, ho_ref, n_ref, a_ref, b_ref = refs
            hin = h_ref[...]
        else:
            (h_ref, ya_ref, yb_ref, wo_ref, g_ref, wg_ref, wu_ref, wd_ref,
             hi_ref, ho_ref, n_ref, a_ref, b_ref) = refs
            ga = ya_ref.shape[1]
            hin = h_ref[...] + _dot(ya_ref[...], wo_ref[:ga, :]) + _dot(yb_ref[...], wo_ref[ga:, :])
            hi_ref[...] = hin
        n = _rms(hin, g_ref[...]).astype(bf16)
        n_ref[...] = n
        a = _dot_nt(n, wg_ref[...]).astype(bf16)
        b = _dot_nt(n, wu_ref[...]).astype(bf16)
        a_ref[...] = a
        b_ref[...] = b
        af = a.astype(f32)
        hm = (af * _sigmoid(af) * b.astype(f32)).astype(bf16)
        ho_ref[...] = hin + 0.5 * _dot(hm, wd_ref[...])

    ins = [h] + (list(pre) if pre is not None else []) + [g, wg, wu, wd]
    in_specs = [_rows(tm, D)]
    if pre is not None:
        in_specs += [_rows(tm, pre[0].shape[1]), _rows(tm, pre[1].shape[1]), _full(pre[2].shape)]
    in_specs += [_full(g.shape), _full(wg.shape), _full(wu.shape), _full(wd.shape)]
    outs = [jax.ShapeDtypeStruct((T, D), f32), jax.ShapeDtypeStruct((T, D), bf16),
            jax.ShapeDtypeStruct((T, F), bf16), jax.ShapeDtypeStruct((T, F), bf16)]
    out_specs = [_rows(tm, D), _rows(tm, D), _rows(tm, F), _rows(tm, F)]
    if pre is not None:
        outs = [jax.ShapeDtypeStruct((T, D), f32)] + outs
        out_specs = [_rows(tm, D)] + out_specs
    return pl.pallas_call(body, name=name, grid=(T // tm,), in_specs=in_specs, out_specs=out_specs,
                          out_shape=outs, compiler_params=_params(("parallel",)))(*ins)


def _ffn_bwd(name, dh, hin, g, a, b, wg, wu, wd, wo=None, ga=0, tm=256):
    T, D = dh.shape
    F = wg.shape[0]
    tm = min(tm, T)

    def body(*refs):
        if wo is None:
            (dh_ref, hin_ref, g_ref, a_ref, b_ref, wg_ref, wu_ref, wd_ref,
             dhi_ref, da_ref, db_ref, hm_ref, dg_ref) = refs
        else:
            (dh_ref, hin_ref, g_ref, a_ref, b_ref, wg_ref, wu_ref, wd_ref, wo_ref,
             dhi_ref, da_ref, db_ref, hm_ref, dg_ref, dya_ref, dyb_ref) = refs

        @pl.when(pl.program_id(0) == 0)
        def _():
            dg_ref[...] = jnp.zeros_like(dg_ref)

        dh_ = dh_ref[...]
        dhb = (0.5 * dh_).astype(bf16)
        dhm = _dot_nt(dhb, wd_ref[...])
        af = a_ref[...].astype(f32)
        bf = b_ref[...].astype(f32)
        sg = _sigmoid(af)
        sl_ = af * sg
        da = (dhm * bf * (sg * (1.0 + af * (1.0 - sg)))).astype(bf16)
        db = (dhm * sl_).astype(bf16)
        da_ref[...] = da
        db_ref[...] = db
        hm_ref[...] = (sl_ * bf).astype(bf16)
        dn = _dot(da, wg_ref[...]) + _dot(db, wu_ref[...])
        _, vjp = jax.vjp(_rms, hin_ref[...], g_ref[...])
        dx, dg = vjp(dn)
        dhi = dh_ + dx
        dhi_ref[...] = dhi
        dg_ref[...] += dg
        if wo is not None:
            dhib = dhi.astype(bf16)
            dya_ref[...] = _dot_nt(dhib, wo_ref[:ga, :]).astype(bf16)
            dyb_ref[...] = _dot_nt(dhib, wo_ref[ga:, :]).astype(bf16)

    ins = [dh, hin, g, a, b, wg, wu, wd]
    in_specs = [_rows(tm, D), _rows(tm, D), _full(g.shape), _rows(tm, F), _rows(tm, F),
                _full(wg.shape), _full(wu.shape), _full(wd.shape)]
    act = jax.ShapeDtypeStruct((T, F), bf16)
    outs = [jax.ShapeDtypeStruct((T, D), f32), act, act, act, jax.ShapeDtypeStruct(g.shape, f32)]
    out_specs = [_rows(tm, D), _rows(tm, F), _rows(tm, F), _rows(tm, F), _acc(g.shape)]
    if wo is not None:
        gb = wo.shape[0] - ga
        ins += [wo]
        in_specs += [_full(wo.shape)]
        outs += [jax.ShapeDtypeStruct((T, ga), bf16), jax.ShapeDtypeStruct((T, gb), bf16)]
        out_specs += [_rows(tm, ga), _rows(tm, gb)]
    return pl.pallas_call(body, name=name, grid=(T // tm,), in_specs=in_specs, out_specs=out_specs,
                          out_shape=outs, compiler_params=_params(("arbitrary",)))(*ins)


def _matmul_tn(name, a, b, scale=1.0, tk=2048):
    T, M = a.shape
    N = b.shape[1]
    tk = min(tk, T)
    nk = T // tk
    tn = LANES * max(d for d in range(1, N // LANES + 1) if (N // LANES) % d == 0 and (d == 1 or M * d * LANES * 4 <= 6 * 1024 * 1024))

    def body(a_ref, b_ref, o_ref, acc):
        k = pl.program_id(1)

        @pl.when(k == 0)
        def _():
            acc[...] = jnp.zeros_like(acc)

        bb = b_ref[...]
        if scale != 1.0:
            bb = bb * scale
        acc[...] += _dot_tn(a_ref[...].astype(bf16), bb.astype(bf16))

        @pl.when(k == nk - 1)
        def _():
            o_ref[...] = acc[...].astype(bf16)

    return pl.pallas_call(
        body, name=name, grid=(N // tn, nk),
        in_specs=[pl.BlockSpec((tk, M), lambda j, k: (k, 0)), pl.BlockSpec((tk, tn), lambda j, k: (k, j))],
        out_specs=pl.BlockSpec((M, tn), lambda j, k: (0, j)),
        out_shape=jax.ShapeDtypeStruct((M, N), bf16), scratch_shapes=[pltpu.VMEM((M, tn), f32)],
        compiler_params=_params(("parallel", "arbitrary")))(a, b)


def _gelu_and_slope(x):
    cdf = 0.5 * (1.0 + lax.erf(x * 0.7071067811865476))
    return x * cdf, cdf + x * (0.3989422804014327 * jnp.exp(-0.5 * x * x))


def _tril_mask():
    r = lax.broadcasted_iota(jnp.int32, (CHUNK, CHUNK), 0)
    c = lax.broadcasted_iota(jnp.int32, (CHUNK, CHUNK), 1)
    return c <= r


def _gm_mix(vnb, ws_ref, bst, mixed_sc, tm):
    mask = _tril_mask()
    for h in range(GM_HEADS):
        wt = jnp.where(mask, ws_ref[h], 0.0).astype(bf16)
        bias = bst[:, h:h + 1]
        for q in range(tm // CHUNK):
            rs = slice(q * CHUNK, (q + 1) * CHUNK)
            cs = slice(h * CHUNK, (h + 1) * CHUNK)
            mixed_sc[rs, cs] = _dot(wt, vnb[rs, cs]) + bias


def _mix_fwd(h1, gmix, w_uv, w_zxd, ln_g, ln_b, w_s, b_st, gout, tm=512):
    T, D = h1.shape
    tm = min(tm, T)
    G = GM_WIDTH

    def body(h_ref, g_ref, wuv_ref, wzxd_ref, lng_ref, lnb_ref, ws_ref, bst_ref, gout_ref,
             n_ref, act_ref, slope_ref, z_ref, xbc_ref, dt_ref, ya_ref, mixed_sc):
        n = _rms(h_ref[...], g_ref[...]).astype(bf16)
        n_ref[...] = n
        uv = _dot_nt(n, wuv_ref[...]).astype(bf16)
        zxd = _dot_nt(n, wzxd_ref[...])
        z_ref[...] = zxd[:, :SSM_WIDTH].astype(bf16)
        xbc_ref[...] = zxd[:, SSM_WIDTH:SSM_WIDTH + CONV_DIM].astype(bf16)
        dt_ref[...] = zxd[:, SSM_WIDTH + CONV_DIM:]
        act, slope = _gelu_and_slope(uv.astype(f32))
        act = act.astype(bf16)
        act_ref[...] = act
        slope_ref[...] = slope.astype(bf16)
        ug, vg = act[:, :G].astype(f32), act[:, G:].astype(f32)
        _gm_mix(_layernorm(vg, lng_ref[...], lnb_ref[...]).astype(bf16), ws_ref, bst_ref[...], mixed_sc, tm)
        ya_ref[...] = _rms(ug * mixed_sc[...], gout_ref[...]).astype(bf16)

    ins = [h1, gmix, w_uv, w_zxd, ln_g, ln_b, w_s, b_st, gout]
    in_specs = [_rows(tm, D)] + [_full(x.shape) for x in ins[1:]]
    outs = [jax.ShapeDtypeStruct((T, D), bf16), jax.ShapeDtypeStruct((T, 2 * G), bf16), jax.ShapeDtypeStruct((T, 2 * G), bf16),
            jax.ShapeDtypeStruct((T, SSM_WIDTH), bf16), jax.ShapeDtypeStruct((T, CONV_DIM), bf16),
            jax.ShapeDtypeStruct((T, LANES), f32), jax.ShapeDtypeStruct((T, G), bf16)]
    out_specs = [_rows(tm, D), _rows(tm, 2 * G), _rows(tm, 2 * G), _rows(tm, SSM_WIDTH), _rows(tm, CONV_DIM), _rows(tm, LANES),
                 _rows(tm, G)]
    return pl.pallas_call(body, name="mix_fwd", grid=(T // tm,), in_specs=in_specs, out_specs=out_specs,
                          out_shape=outs, scratch_shapes=[pltpu.VMEM((tm, G), f32)],
                          compiler_params=_params(("parallel",)))(*ins)


def _mix_bwd(dh, h1, gmix, act, slope, dya, dzxd, w_uv, w_zxd, ln_g, ln_b, w_s, b_st, gout, tm=256):
    T, D = dh.shape
    tm = min(tm, T)
    G = GM_WIDTH

    def body(dh_ref, h_ref, g_ref, act_ref, slope_ref, dya_ref, dzxd_ref, wuv_ref, wzxd_ref, lng_ref, lnb_ref, ws_ref,
             bst_ref, gout_ref, dhi_ref, duv_ref, dg_ref, dlng_ref, dlnb_ref, dws_ref, dbst_ref, dgout_ref, mixed_sc, dvn_sc):
        @pl.when(pl.program_id(0) == 0)
        def _():
            for r in (dg_ref, dlng_ref, dlnb_ref, dws_ref, dbst_ref, dgout_ref):
                r[...] = jnp.zeros_like(r)

        dn_z = _dot(dzxd_ref[...], wzxd_ref[...])
        ug = act_ref[:, :G].astype(f32)
        vn, ln_vjp = jax.vjp(_layernorm, act_ref[:, G:].astype(f32), lng_ref[...], lnb_ref[...])
        vnb = vn.astype(bf16)
        _gm_mix(vnb, ws_ref, bst_ref[...], mixed_sc, tm)
        mixed = mixed_sc[...]
        _, out_vjp = jax.vjp(_rms, ug * mixed, gout_ref[...])
        dpre, dgout = out_vjp(dya_ref[...].astype(f32))
        dgout_ref[...] += dgout
        dug = dpre * mixed
        dmixed = dpre * ug
        mask = _tril_mask()
        lane = lax.broadcasted_iota(jnp.int32, (1, GM_HEADS), 1)
        dbst = jnp.zeros((CHUNK, GM_HEADS), f32)
        for h in range(GM_HEADS):
            wt = jnp.where(mask, ws_ref[h], 0.0).astype(bf16)
            cs = slice(h * CHUNK, (h + 1) * CHUNK)
            dw = jnp.zeros((CHUNK, CHUNK), f32)
            for q in range(tm // CHUNK):
                rs = slice(q * CHUNK, (q + 1) * CHUNK)
                dm = dmixed[rs, cs]
                dmb = dm.astype(bf16)
                dw = dw + _dot_nt(dmb, vnb[rs, cs])
                dbst = dbst + jnp.sum(dm, axis=1, keepdims=True) * (lane == h).astype(f32)
                dvn_sc[rs, cs] = _dot_tn(wt, dmb)
            dws_ref[h] += jnp.where(mask, dw, 0.0)
        dbst_ref[...] += dbst
        dvg, dlng, dlnb = ln_vjp(dvn_sc[...])
        duv = (jnp.concatenate([dug, dvg], axis=1) * slope_ref[...].astype(f32)).astype(bf16)
        duv_ref[...] = duv
        dlng_ref[...] += dlng
        dlnb_ref[...] += dlnb
        dn = dn_z + _dot(duv, wuv_ref[...])
        _, vjp = jax.vjp(_rms, h_ref[...], g_ref[...])
        dx, dg = vjp(dn)
        dhi_ref[...] = dh_ref[...] + dx
        dg_ref[...] += dg

    ins = [dh, h1, gmix, act, slope, dya, dzxd, w_uv, w_zxd, ln_g, ln_b, w_s, b_st, gout]
    in_specs = ([_rows(tm, D), _rows(tm, D), _full(gmix.shape), _rows(tm, 2 * G), _rows(tm, 2 * G), _rows(tm, G),
                 _rows(tm, dzxd.shape[1])] + [_full(x.shape) for x in ins[7:]])
    accs = (gmix, ln_g, ln_b, w_s, b_st, gout)
    outs = ([jax.ShapeDtypeStruct((T, D), f32), jax.ShapeDtypeStruct((T, 2 * G), bf16)]
            + [jax.ShapeDtypeStruct(x.shape, f32) for x in accs])
    out_specs = [_rows(tm, D), _rows(tm, 2 * G)] + [_acc(x.shape) for x in accs]
    return pl.pallas_call(body, name="mix_bwd", grid=(T // tm,), in_specs=in_specs, out_specs=out_specs,
                          out_shape=outs, scratch_shapes=[pltpu.VMEM((tm, G), f32), pltpu.VMEM((tm, G), f32)],
                          compiler_params=_params(("arbitrary",)))(*ins)


HALO = 16
SSD_SUB = 4


class _RowsOf:
    def __init__(self, ref, rows):
        self.ref, self.rows = ref, rows

    def _index(self, idx):
        return (self.rows, slice(None)) if idx is Ellipsis else (self.rows,) + tuple(idx[1:])

    def __getitem__(self, idx):
        return self.ref[self._index(idx)]

    def __setitem__(self, idx, value):
        self.ref[self._index(idx)] = value
PAIRS = SSM_HEADS // 2
PAIR_W = 2 * SSM_HEAD_DIM


def _split(x, n):
    parts = []
    for _ in range(n):
        p = x.astype(bf16)
        parts.append(p)
        x = x - p.astype(f32)
    return parts


def _dot_sel(x, sel_n, n):
    return _dot(jnp.concatenate(_split(x, n), axis=1), sel_n)


def _sel_dot(sel, x, n):
    return _dot(jnp.concatenate([sel] * n, axis=1), jnp.concatenate(_split(x, n), axis=0))


EXPAND_SPLIT = 3
REDUCE_SPLIT = 2


def _head_mats():
    ex = (jnp.arange(SSM_WIDTH)[None, :] // SSM_HEAD_DIM == jnp.arange(LANES)[:, None]).astype(bf16)
    return jnp.tile(ex, (EXPAND_SPLIT, 1)), jnp.tile(ex.T, (REDUCE_SPLIT, 1))


def _shift_mat(rows, cols, off):
    r = lax.broadcasted_iota(jnp.int32, (rows, cols), 0)
    c = lax.broadcasted_iota(jnp.int32, (rows, cols), 1)
    return (c == r + off).astype(bf16)


def _ssd_conv(halo, x, cw_ref, cb_ref):
    ext = jnp.concatenate([halo, x], axis=0)
    xc = cb_ref[...] + cw_ref[SSM_CONV - 1:SSM_CONV, :] * x.astype(f32)
    for j in range(SSM_CONV - 1):
        xc = xc + cw_ref[j:j + 1, :] * _dot(_shift_mat(CHUNK, HALO + CHUNK, HALO - SSM_CONV + 1 + j), ext)
    return xc


def _ssd_front(dtr, dtb_ref, alog_ref):
    dt = _softplus(dtr + dtb_ref[...])
    a = -jnp.exp(alog_ref[...])
    acs = jnp.dot(_tril_mask().astype(f32), dt * a, preferred_element_type=f32, precision=HIGHEST)
    return dt, a, acs


def _ssd_wide(xa, dt, acs, dsk, ex):
    dt_x = _dot_sel(dt, ex, EXPAND_SPLIT)
    acs_x = _dot_sel(acs, ex, EXPAND_SPLIT)
    dsk_x = _dot_sel(jnp.broadcast_to(dsk, (8, LANES)), ex, EXPAND_SPLIT)[0:1]
    e_x = jnp.exp(acs_x)
    r_x = jnp.exp(acs_x[CHUNK - 1:CHUNK, :] - acs_x)
    xs = xa[:, :SSM_WIDTH]
    xd = xs * dt_x
    return dt_x, dsk_x, e_x, r_x, xs, xd, xd * r_x


def _pair_stack(v, lo):
    return jnp.concatenate([jnp.where(lo, v, 0.0), jnp.where(lo, 0.0, v)], axis=0)


def _ssd_pair(j, acs, acs_t, cb):
    out = []
    tril = _tril_mask()
    for h in (2 * j, 2 * j + 1):
        dk = jnp.exp(jnp.where(tril, acs[:, h:h + 1] - acs_t[h:h + 1, :], -jnp.inf))
        out.append((dk, cb * dk))
    return out


def _pair_col(row_lo, tot, j):
    return jnp.exp(jnp.where(row_lo, tot[:, 2 * j:2 * j + 1], tot[:, 2 * j + 1:2 * j + 2]))


def _gated_norm(y, z, g):
    yg = y * (z * _sigmoid(z))
    half = SSM_WIDTH // SSM_GROUPS
    parts = []
    for k in range(SSM_GROUPS):
        s = yg[:, k * half:(k + 1) * half]
        parts.append(s * lax.rsqrt(jnp.mean(s * s, axis=-1, keepdims=True) + EPS))
    return jnp.concatenate(parts, axis=1) * g


def _group_mats(xa):
    out = []
    for g in range(SSM_GROUPS):
        bm = xa[:, SSM_WIDTH + g * SSM_STATE:SSM_WIDTH + (g + 1) * SSM_STATE].astype(bf16)
        cm = xa[:, SSM_WIDTH + (SSM_GROUPS + g) * SSM_STATE:SSM_WIDTH + (SSM_GROUPS + g + 1) * SSM_STATE].astype(bf16)
        out.append((cm, bm, _dot_nt(cm, bm)))
    return out


def _ssd_fwd(xbc, z, dtr, conv_w, conv_b, dt_bias, a_log, d_skip, ssm_norm):
    T = xbc.shape[0]
    nc = T // CHUNK
    N = SSM_STATE

    def body(xbc_ref, halo_ref, z_ref, dtr_ref, cw_ref, cb_ref, dtb_ref, alog_ref, dsk_ref, g_ref, ex_ref,
             yb_ref, xc_ref, sg_ref, y_ref, sprev_ref, s_sc):
        i = pl.program_id(0)

        @pl.when(i == 0)
        def _():
            s_sc[...] = jnp.zeros_like(s_sc)

        lo = lax.broadcasted_iota(jnp.int32, (CHUNK, PAIR_W), 1) < SSM_HEAD_DIM
        row_lo = lax.broadcasted_iota(jnp.int32, (PAIR_W, 1), 0) < SSM_HEAD_DIM
        for k in range(SUB):
            rs = slice(k * CHUNK, (k + 1) * CHUNK)
            if k == 0:
                halo = halo_ref[...]
                halo = jnp.where(i > 0, halo, jnp.zeros_like(halo))
            else:
                halo = xbc_ref[k * CHUNK - HALO:k * CHUNK, :]
            xc = _ssd_conv(halo, xbc_ref[rs, :], cw_ref, cb_ref)
            sg = _sigmoid(xc)
            xc_ref[rs, :] = xc
            sg_ref[rs, :] = sg
            xa = xc * sg
            dt, _, acs = _ssd_front(dtr_ref[rs, :], dtb_ref, alog_ref)
            _, dsk_x, e_x, _, xs, xd, gm = _ssd_wide(xa, dt, acs, dsk_ref[...], ex_ref[...])
            acs_t = acs.T
            tot = acs[CHUNK - 1:CHUNK, :]
            groups = _group_mats(xa)
            ys = []
            for j in range(PAIRS):
                cmb, bmb, cb = groups[j // (PAIRS // SSM_GROUPS)]
                ps = slice(j * PAIR_W, (j + 1) * PAIR_W)
                (_, m0), (_, m1) = _ssd_pair(j, acs, acs_t, cb)
                sp = s_sc[j]
                yd = _dot(jnp.concatenate([m0, m1], axis=1).astype(bf16), _pair_stack(xd[:, ps], lo).astype(bf16))
                ys.append(yd + e_x[:, ps] * _dot_nt(cmb, sp.astype(bf16)))
                sprev_ref[k, j] = sp
                s_sc[j] = _pair_col(row_lo, tot, j) * sp + _dot_tn(gm[:, ps].astype(bf16), bmb)
            y = jnp.concatenate(ys, axis=1) + xs * dsk_x
            y_ref[rs, :] = y
            yb_ref[rs, :] = _gated_norm(y, z_ref[rs, :].astype(f32), g_ref[...]).astype(bf16)

    params = [conv_w, conv_b, dt_bias, a_log, d_skip, ssm_norm, _head_mats()[0]]
    SUB = SSD_SUB if nc % SSD_SUB == 0 else 1
    hp = SUB * CHUNK // HALO
    R = SUB * CHUNK
    in_specs = [_rows(R, CONV_DIM), pl.BlockSpec((HALO, CONV_DIM), lambda i: (jnp.maximum(i * hp - 1, 0), 0)),
                _rows(R, SSM_WIDTH), _rows(R, LANES)] + [_full(x.shape) for x in params]
    return pl.pallas_call(
        body, name="ssd_fwd", grid=(nc // SUB,), in_specs=in_specs,
        out_specs=[_rows(R, SSM_WIDTH), _rows(R, CONV_DIM), _rows(R, CONV_DIM), _rows(R, SSM_WIDTH),
                   pl.BlockSpec((SUB, PAIRS, PAIR_W, N), lambda i: (i, 0, 0, 0))],
        out_shape=[jax.ShapeDtypeStruct((T, SSM_WIDTH), bf16), jax.ShapeDtypeStruct((T, CONV_DIM), f32),
                   jax.ShapeDtypeStruct((T, CONV_DIM), f32), jax.ShapeDtypeStruct((T, SSM_WIDTH), f32),
                   jax.ShapeDtypeStruct((nc, PAIRS, PAIR_W, N), f32)],
        scratch_shapes=[pltpu.VMEM((PAIRS, PAIR_W, N), f32)],
        compiler_params=_params(("arbitrary",)))(xbc, xbc, z, dtr, *params)


def _ssd_bwd(xbc, xc, sg, y, z, dtr, sprev, dyb, conv_w, conv_b, dt_bias, a_log, d_skip, ssm_norm):
    T = xbc.shape[0]
    nc = T // CHUNK
    H, N = SSM_HEADS, SSM_STATE
    PG = PAIRS // SSM_GROUPS

    def chunk(xbc_ref, xc_ref, sg_ref, y_ref, z_ref, dtr_ref, sprev_k, dyb_ref, cw_ref, cb_ref, dtb_ref, alog_ref, dsk_ref,
              g_ref, ex_ref, rd_ref, dzxd_ref, dcw_ref, dcb_ref, ddtb_ref, dalog_ref, ddsk_ref, dg_ref, ds_sc, next_sc):
        xc = xc_ref[...]
        sg = sg_ref[...]
        xa = xc * sg
        dt, a, acs = _ssd_front(dtr_ref[...], dtb_ref, alog_ref)
        dt_x, dsk_x, e_x, r_x, xs, xd, gm = _ssd_wide(xa, dt, acs, dsk_ref[...], ex_ref[...])
        acs_t = acs.T
        tot = acs[CHUNK - 1:CHUNK, :]
        groups = _group_mats(xa)
        lo = lax.broadcasted_iota(jnp.int32, (CHUNK, PAIR_W), 1) < SSM_HEAD_DIM
        row_lo = lax.broadcasted_iota(jnp.int32, (PAIR_W, 1), 0) < SSM_HEAD_DIM
        pairs, zs = [], []
        for j in range(PAIRS):
            cmb, _, cb = groups[j // PG]
            pairs.append(_ssd_pair(j, acs, acs_t, cb))
            zs.append(_dot_nt(cmb, sprev_k[j].astype(bf16)))
        zf = jnp.concatenate(zs, axis=1)
        _, gn_vjp = jax.vjp(_gated_norm, y_ref[...], z_ref[...].astype(f32), g_ref[...])
        dy, dz, dg = gn_vjp(dyb_ref[...].astype(f32))
        dg_ref[...] += dg
        dzxd_ref[:, :SSM_WIDTH] = dz.astype(bf16)

        lane = lax.broadcasted_iota(jnp.int32, (1, LANES), 1)
        sub = lax.broadcasted_iota(jnp.int32, (LANES, 1), 0)
        dacs = jnp.zeros((CHUNK, LANES), f32)
        dacs_r = jnp.zeros((LANES, CHUNK), f32)
        dtot = jnp.zeros((1, LANES), f32)
        dcb = [jnp.zeros((CHUNK, CHUNK), f32) for _ in range(SSM_GROUPS)]
        dcm = [jnp.zeros((CHUNK, N), f32) for _ in range(SSM_GROUPS)]
        dbm = [jnp.zeros((CHUNK, N), f32) for _ in range(SSM_GROUPS)]
        dxds, dgms = [], []
        for j in range(PAIRS):
            g = j // PG
            cmb, bmb, _ = groups[g]
            ps = slice(j * PAIR_W, (j + 1) * PAIR_W)
            (dk0, m0), (dk1, m1) = pairs[j]
            oh0, oh1 = (lane == 2 * j).astype(f32), (lane == 2 * j + 1).astype(f32)
            dyp = dy[:, ps]
            dy2 = _pair_stack(dyp, lo).astype(bf16)
            dm2 = _dot_nt(dy2, xd[:, ps].astype(bf16))
            m2 = jnp.concatenate([m0, m1], axis=0)
            dxds.append(_dot_tn(m2.astype(bf16), dy2))
            w2 = dm2 * m2
            rs = jnp.sum(w2, axis=1, keepdims=True)
            dacs = dacs + rs[:CHUNK] * oh0 + rs[CHUNK:] * oh1
            dacs_r = dacs_r - ((sub == 2 * j).astype(f32) * jnp.sum(w2[:CHUNK], axis=0, keepdims=True)
                               + (sub == 2 * j + 1).astype(f32) * jnp.sum(w2[CHUNK:], axis=0, keepdims=True))
            dcb[g] = dcb[g] + dm2[:CHUNK] * dk0 + dm2[CHUNK:] * dk1
            sp = sprev_k[j]
            dzb = (dyp * e_x[:, ps]).astype(bf16)
            dcm[g] = dcm[g] + _dot(dzb, sp.astype(bf16))
            dsn = ds_sc[j]
            dsnb = dsn.astype(bf16)
            et = _pair_col(row_lo, tot, j)
            rr = jnp.sum(dsn * sp, axis=1, keepdims=True) * et
            dtot = dtot + jnp.sum(rr[:SSM_HEAD_DIM]) * oh0 + jnp.sum(rr[SSM_HEAD_DIM:]) * oh1
            dgms.append(_dot_nt(bmb, dsnb))
            dbm[g] = dbm[g] + _dot(gm[:, ps].astype(bf16), dsnb)
            ds_sc[j] = _dot_tn(dzb, cmb) + et * dsn
        dgm = jnp.concatenate(dgms, axis=1)
        dxd = jnp.concatenate(dxds, axis=1) + dgm * r_x
        dr = dgm * gm
        red = _dot_sel(jnp.concatenate([dy * e_x * zf - dr, dr, dxd * xs, dy * xs], axis=0), rd_ref[...], REDUCE_SPLIT)
        rowi = lax.broadcasted_iota(jnp.int32, (CHUNK, 1), 0)
        dtot = dtot + jnp.sum(red[CHUNK:2 * CHUNK], axis=0, keepdims=True)
        dacs = dacs + red[:CHUNK] + dacs_r.T + jnp.where(rowi == CHUNK - 1, dtot, 0.0)
        r2 = lax.broadcasted_iota(jnp.int32, (CHUNK, CHUNK), 0)
        c2 = lax.broadcasted_iota(jnp.int32, (CHUNK, CHUNK), 1)
        dadt = jnp.dot((c2 >= r2).astype(f32), dacs, preferred_element_type=f32, precision=HIGHEST)
        ddt = red[2 * CHUNK:3 * CHUNK] + dadt * a
        dalog_ref[...] += jnp.sum(dadt * dt, axis=0, keepdims=True) * a
        ddsk_ref[...] += jnp.sum(red[3 * CHUNK:], axis=0, keepdims=True)
        ddtr = jnp.where(lane < H, ddt * _sigmoid(dtr_ref[...] + dtb_ref[...]), 0.0)
        ddtb_ref[...] += jnp.sum(ddtr, axis=0, keepdims=True)
        dzxd_ref[:, SSM_WIDTH + CONV_DIM:] = ddtr.astype(bf16)
        dxa_bm, dxa_cm = [], []
        for g in range(SSM_GROUPS):
            cmb, bmb, _ = groups[g]
            dcbb = dcb[g].astype(bf16)
            dxa_bm.append(dbm[g] + _dot_tn(dcbb, cmb))
            dxa_cm.append(dcm[g] + _dot(dcbb, bmb))
        dxc = jnp.concatenate([dy * dsk_x + dxd * dt_x] + dxa_bm + dxa_cm, axis=1) * (sg * (1.0 + xc * (1.0 - sg)))
        ext = jnp.concatenate([dxc, next_sc[...]], axis=0)
        xin = xbc_ref[...].astype(f32)
        dxbc = cw_ref[SSM_CONV - 1:SSM_CONV, :] * dxc
        dcw = [jnp.sum(dxc * xin, axis=0, keepdims=True)]
        for s in range(1, SSM_CONV):
            later = _sel_dot(_shift_mat(CHUNK, CHUNK + HALO, s), ext, 2)
            dxbc = dxbc + cw_ref[SSM_CONV - 1 - s:SSM_CONV - s, :] * later
            dcw.insert(0, jnp.sum(later * xin, axis=0, keepdims=True))
        dzxd_ref[:, SSM_WIDTH:SSM_WIDTH + CONV_DIM] = dxbc.astype(bf16)
        dcw_ref[...] += jnp.concatenate(dcw, axis=0)
        dcb_ref[...] += jnp.sum(dxc, axis=0, keepdims=True)
        next_sc[...] = dxc[0:HALO, :]

    SUB = SSD_SUB if nc % SSD_SUB == 0 else 1
    nb = nc // SUB

    def body(xbc_ref, xc_ref, sg_ref, y_ref, z_ref, dtr_ref, sprev_ref, dyb_ref, cw_ref, cb_ref, dtb_ref, alog_ref, dsk_ref,
             g_ref, ex_ref, rd_ref, dzxd_ref, dcw_ref, dcb_ref, ddtb_ref, dalog_ref, ddsk_ref, dg_ref, ds_sc, next_sc):
        @pl.when(pl.program_id(0) == 0)
        def _():
            ds_sc[...] = jnp.zeros_like(ds_sc)
            next_sc[...] = jnp.zeros_like(next_sc)
            for r_ in (dcw_ref, dcb_ref, ddtb_ref, dalog_ref, ddsk_ref, dg_ref):
                r_[...] = jnp.zeros_like(r_)

        for k in reversed(range(SUB)):
            rows = slice(k * CHUNK, (k + 1) * CHUNK)
            tok = [_RowsOf(r_, rows) for r_ in (xbc_ref, xc_ref, sg_ref, y_ref, z_ref, dtr_ref)]
            chunk(*tok, sprev_ref.at[k], _RowsOf(dyb_ref, rows), cw_ref, cb_ref, dtb_ref, alog_ref, dsk_ref, g_ref, ex_ref,
                  rd_ref, _RowsOf(dzxd_ref, rows), dcw_ref, dcb_ref, ddtb_ref, dalog_ref, ddsk_ref, dg_ref, ds_sc, next_sc)

    params = [conv_w, conv_b, dt_bias, a_log, d_skip, ssm_norm]
    mats = list(_head_mats())

    def rev(ncols):
        return pl.BlockSpec((SUB * CHUNK, ncols), lambda i: (nb - 1 - i, 0))

    in_specs = ([rev(CONV_DIM), rev(CONV_DIM), rev(CONV_DIM), rev(SSM_WIDTH), rev(SSM_WIDTH), rev(LANES),
                 pl.BlockSpec((SUB, PAIRS, PAIR_W, N), lambda i: (nb - 1 - i, 0, 0, 0)), rev(SSM_WIDTH)]
                + [_full(x.shape) for x in params + mats])
    return pl.pallas_call(
        body, name="ssd_bwd", grid=(nb,), in_specs=in_specs,
        out_specs=[rev(ZXD)] + [_acc(x.shape) for x in params],
        out_shape=[jax.ShapeDtypeStruct((T, ZXD), bf16)] + [jax.ShapeDtypeStruct(x.shape, f32) for x in params],
        scratch_shapes=[pltpu.VMEM((PAIRS, PAIR_W, N), f32), pltpu.VMEM((HALO, CONV_DIM), f32)],
        compiler_params=_params(("arbitrary",)))(xbc, xc, sg, y, z, dtr, sprev, dyb, *params, *mats)


def _tail(h3, p, tgt, gp, wpg, bpg, wpp, gf, tm=512):
    T, D = h3.shape
    tm = min(tm, T)

    def head(gpre, pp, h, gf_, t):
        gate = _sigmoid(gpre)
        y = _rms(h + gate * pp, gf_)
        err = y - t
        return 0.5 * jnp.sum(jnp.mean(err * err, axis=-1))

    def body(h_ref, p_ref, t_ref, gp_ref, wpg_ref, bpg_ref, wpp_ref, gf_ref,
             dh_ref, loss_ref, dgp_ref, dwpg_ref, dbpg_ref, dwpp_ref, dgf_ref):
        @pl.when(pl.program_id(0) == 0)
        def _():
            for r in (loss_ref, dgp_ref, dwpg_ref, dbpg_ref, dwpp_ref, dgf_ref):
                r[...] = jnp.zeros_like(r)

        h = h_ref[...]
        npf, np_vjp = jax.vjp(_rms, h, gp_ref[...])
        npb = npf.astype(bf16)
        pb = p_ref[...].astype(bf16)
        gpre = _dot(npb, wpg_ref[...]) + bpg_ref[...]
        kp, _, cp = wpp_ref.shape
        pp = jnp.concatenate([_dot(pb, wpp_ref[k]) for k in range(kp)], axis=1)
        loss, head_vjp = jax.vjp(head, gpre, pp, h, gf_ref[...], t_ref[...])
        dgpre, dpp, dh_a, dgf, _ = head_vjp(jnp.ones((), f32))
        loss_ref[...] += loss
        dgf_ref[...] += dgf
        dbpg_ref[...] += jnp.sum(dgpre, axis=0, keepdims=True)
        dgb = dgpre.astype(bf16)
        dwpg_ref[...] += _dot_tn(npb, dgb)
        dppb = dpp.astype(bf16)
        for k in range(kp):
            dwpp_ref[k] += _dot_tn(pb, dppb[:, k * cp:(k + 1) * cp])
        dh_b, dgp = np_vjp(_dot_nt(dgb, wpg_ref[...]))
        dgp_ref[...] += dgp
        dh_ref[...] = dh_a + dh_b

    ins = [h3, p, tgt, gp, wpg, bpg, wpp, gf]
    in_specs = [_rows(tm, D), _rows(tm, p.shape[1]), _rows(tm, D)] + [_full(x.shape) for x in ins[3:]]
    acc_shapes = [(1, LANES), gp.shape, wpg.shape, bpg.shape, wpp.shape, gf.shape]
    return pl.pallas_call(
        body, name="tail", grid=(T // tm,), in_specs=in_specs,
        out_specs=[_rows(tm, D)] + [_acc(s) for s in acc_shapes],
        out_shape=[jax.ShapeDtypeStruct((T, D), f32)] + [jax.ShapeDtypeStruct(s, f32) for s in acc_shapes],
        compiler_params=_params(("arbitrary",)))(*ins)


def _adamw(name, w, g, m, v, tr=256):
    R, C = w.shape
    tr = _row_tile(R, tr)

    def body(w_ref, g_ref, m_ref, v_ref, d_ref, mo_ref, vo_ref):
        g_ = g_ref[...]
        m_ = ADAM_B1 * m_ref[...] + (1.0 - ADAM_B1) * g_
        v_ = ADAM_B2 * v_ref[...] + (1.0 - ADAM_B2) * jnp.square(g_)
        m_hat = m_ / (1.0 - ADAM_B1 ** ADAM_STEP)
        v_hat = v_ / (1.0 - ADAM_B2 ** ADAM_STEP)
        d_ref[...] = -ADAM_LR * (m_hat / (jnp.sqrt(v_hat) + ADAM_EPS) + ADAM_WD * w_ref[...])
        mo_ref[...] = m_
        vo_ref[...] = v_

    spec = pl.BlockSpec((tr, C), lambda i: (i, 0))
    return pl.pallas_call(body, name=name, grid=(R // tr,), in_specs=[spec] * 4, out_specs=[spec] * 3,
                          out_shape=[jax.ShapeDtypeStruct((R, C), f32)] * 3,
                          compiler_params=_params(("parallel",)))(w, g, m, v)


HBM = pl.BlockSpec(memory_space=pltpu.HBM)


def _me():
    return lax.axis_index("x"), lax.axis_index("y"), lax.axis_index("c")


def _other_chips(x, y):
    return [(1 - x, y), (x, 1 - y), (1 - x, 1 - y)]


def _remote(src, dst, send_sem, recv_sem, dev):
    return pltpu.make_async_remote_copy(src_ref=src, dst_ref=dst, send_sem=send_sem, recv_sem=recv_sem,
                                        device_id=dev, device_id_type=MESH)


def _sems(n):
    return [pltpu.SemaphoreType.DMA((n,)), pltpu.SemaphoreType.DMA((n,))]


def _gather_weights(shards, split):
    n = len(shards)

    def body(*refs):
        ins, outs = refs[:n], refs[n:2 * n]
        own_send, own_recv, ici_send, ici_recv, d2d_send, d2d_recv = refs[2 * n:]
        x, y, c = _me()
        my_chip = 2 * x + y
        sibling = (x, y, 1 - c)
        chips = _other_chips(x, y)

        def rows(i, half):
            hr = shards[i].shape[0] // 2
            return pl.ds(half * hr, hr) if split[i] else pl.ds(0, shards[i].shape[0])

        sends = []
        for i in range(n):
            for j, chip in enumerate(chips):
                cp = _remote(ins[i].at[rows(i, c)], outs[i].at[my_chip, rows(i, c)],
                             ici_send.at[3 * i + j], ici_recv.at[3 * i + j], (*chip, c))
                cp.start()
                sends.append(cp)
            cp = _remote(ins[i], outs[i].at[my_chip], own_send.at[i], own_recv.at[i], sibling)
            cp.start()
            sends.append(cp)
        for i in range(n):
            for j, chip in enumerate(chips):
                s = 3 * i + j
                land = outs[i].at[2 * chip[0] + chip[1], rows(i, c)]
                _remote(land, land, ici_send.at[s], ici_recv.at[s], (*chip, c)).wait_recv()
                if split[i]:
                    cp = _remote(land, land, d2d_send.at[s], d2d_recv.at[s], sibling)
                    cp.start()
                    sends.append(cp)
        for i in range(n):
            _remote(ins[i], outs[i].at[my_chip], own_send.at[i], own_recv.at[i], sibling).wait_recv()
            if split[i]:
                for j, chip in enumerate(chips):
                    s = 3 * i + j
                    land = outs[i].at[2 * chip[0] + chip[1], rows(i, 1 - c)]
                    _remote(land, land, d2d_send.at[s], d2d_recv.at[s], sibling).wait_recv()
        for cp in sends:
            cp.wait_send()

    return pl.pallas_call(
        body, name="gather_weights", out_shape=[jax.ShapeDtypeStruct((N_CHIPS,) + s.shape, s.dtype) for s in shards],
        in_specs=[HBM] * n, out_specs=[HBM] * n,
        scratch_shapes=_sems(n) + _sems(3 * n) + _sems(3 * n))(*shards)


def _swap_halves(name, grads):
    n = len(grads)

    def body(*refs):
        ins, outs, send, recv = refs[:n], refs[n:2 * n], refs[2 * n], refs[2 * n + 1]
        x, y, c = _me()
        copies = []
        for i in range(n):
            hr = grads[i].shape[1] // 2
            cp = _remote(ins[i].at[:, pl.ds((1 - c) * hr, hr), :], outs[i], send.at[i], recv.at[i], (x, y, 1 - c))
            cp.start()
            copies.append(cp)
        for cp in copies:
            cp.wait()

    return pl.pallas_call(
        body, name=name,
        out_shape=[jax.ShapeDtypeStruct((g.shape[0], g.shape[1] // 2, g.shape[2]), g.dtype) for g in grads],
        in_specs=[HBM] * n, out_specs=[HBM] * n, scratch_shapes=_sems(n))(*grads)


def _add_halves(name, grads, other, c_idx, th=HALF_ROWS_BF16):
    K, R, C = grads.shape
    H = R // 2
    th = _row_tile(H, th, 16)
    nb = H // th

    def body(c_ref, g_ref, o_ref, out_ref):
        out_ref[...] = (g_ref[...].astype(f32) + o_ref[...].astype(f32)).astype(bf16)

    grid_spec = pltpu.PrefetchScalarGridSpec(
        num_scalar_prefetch=1, grid=(nb,),
        in_specs=[pl.BlockSpec((K, th, C), lambda i, c: (0, c[0] * nb + i, 0)),
                  pl.BlockSpec((K, th, C), lambda i, c: (0, i, 0))],
        out_specs=pl.BlockSpec((K, th, C), lambda i, c: (0, i, 0)))
    return pl.pallas_call(body, name=name, grid_spec=grid_spec,
                          out_shape=jax.ShapeDtypeStruct((K, H, C), bf16),
                          compiler_params=_params(("parallel",)))(c_idx, grads, other)


SEM = pl.BlockSpec(memory_space=pltpu.SEMAPHORE)
ANY = pl.BlockSpec(memory_space=pl.ANY)
EFFECT = pltpu.SideEffectType.DATAFLOW_SIDE_EFFECTING


def _copies_start(name, srcs, land_shapes, n_copies, make_copies, after):
    ns, nl = len(srcs), len(land_shapes)
    lands = [lax.empty(s.shape, s.dtype) for s in land_shapes]

    def body(*refs):
        src_refs, land_refs = refs[:ns], refs[ns:ns + nl]
        send, recv, token = refs[ns + nl + 1], refs[ns + nl + 2], refs[-1]
        for cp in make_copies(src_refs, land_refs, send, recv):
            cp.start()
        token[...] = jnp.zeros_like(token)

    buffers = list(srcs) + lands
    out = pl.pallas_call(
        body, name=name,
        out_shape=(pltpu.SemaphoreType.DMA((n_copies,)), pltpu.SemaphoreType.DMA((n_copies,)),
                   *[pltpu.HBM(b.shape, b.dtype) for b in buffers], jax.ShapeDtypeStruct((8, LANES), f32)),
        in_specs=[HBM] * (ns + nl) + [ANY],
        out_specs=(SEM, SEM, *[HBM] * (ns + nl), pl.BlockSpec(memory_space=pltpu.VMEM)),
        input_output_aliases={i: 2 + i for i in range(ns + nl)},
        compiler_params=pltpu.CompilerParams(has_side_effects=EFFECT),
    )(*[pltpu.with_memory_space_constraint(b, pltpu.HBM) for b in buffers], after)
    return out[0], out[1], list(out[2:2 + ns]), list(out[2 + ns:2 + ns + nl]), out[-1]


def _copies_wait(name, started, make_copies, after):
    send, recv, srcs, lands, _ = started
    ns, nl = len(srcs), len(lands)
    after = list(after)

    def body(*refs):
        src_refs, land_refs = refs[:ns], refs[ns:ns + nl]
        for cp in make_copies(src_refs, land_refs, refs[ns + nl], refs[ns + nl + 1]):
            cp.wait_send()
            cp.wait_recv()

    buffers = list(srcs) + list(lands)
    out = pl.pallas_call(
        body, name=name, out_shape=tuple(pltpu.HBM(b.shape, b.dtype) for b in buffers),
        in_specs=[HBM] * (ns + nl) + [SEM, SEM] + [ANY] * len(after), out_specs=tuple([HBM] * (ns + nl)),
        input_output_aliases={i: i for i in range(ns + nl)},
        compiler_params=pltpu.CompilerParams(has_side_effects=EFFECT),
    )(*buffers, send, recv, *after)
    return list(out[:ns]), list(out[ns:])


def _gather_copies(src_refs, land_refs, send, recv):
    x, y, c = _me()
    my_chip = 2 * x + y
    peers = [(*chip, c) for chip in _other_chips(x, y)] + [(x, y, 1 - c)]
    return [_remote(src_refs[i], land_refs[i].at[my_chip], send.at[4 * i + j], recv.at[4 * i + j], peer)
            for i in range(len(src_refs)) for j, peer in enumerate(peers)]


def _partial_copies(src_refs, land_refs, send, recv):
    x, y, c = _me()
    return [_remote(src_refs[i].at[2 * chip[0] + chip[1]], land_refs[i].at[j], send.at[3 * i + j], recv.at[3 * i + j], (*chip, c))
            for i in range(len(src_refs)) for j, chip in enumerate(_other_chips(x, y))]


def _small_copies(src_refs, land_refs, send, recv):
    x, y, c = _me()
    return [_remote(src_refs[0], land_refs[0].at[k - 1], send.at[k - 1], recv.at[k - 1], (x ^ (k >> 2), y ^ ((k >> 1) & 1), c ^ (k & 1)))
            for k in range(1, N_DEV)]


def _sum_small(own, slots, dev_idx):
    R, C = own.shape

    def body(dev_ref, own_ref, s_ref, o_ref):
        me = dev_ref[0]
        acc = jnp.zeros((R, C), f32)
        for d in range(N_DEV):
            k = me ^ d
            acc = acc + jnp.where(k == 0, own_ref[...], s_ref[jnp.maximum(k - 1, 0)])
        o_ref[...] = acc

    grid_spec = pltpu.PrefetchScalarGridSpec(
        num_scalar_prefetch=1, grid=(1,),
        in_specs=[pl.BlockSpec((R, C), lambda i, dev: (0, 0)), pl.BlockSpec((N_DEV - 1, R, C), lambda i, dev: (0, 0, 0))],
        out_specs=pl.BlockSpec((R, C), lambda i, dev: (0, 0)))
    return pl.pallas_call(body, name="sum_small", grid_spec=grid_spec, out_shape=jax.ShapeDtypeStruct((R, C), f32),
                          compiler_params=_params(("arbitrary",)))(dev_idx, own, slots)


def _sum_partials(name, part, recv, chip_idx, th=HALF_ROWS_BF16):
    K, H, C = part.shape
    th = _row_tile(H, th, 16)

    def body(chip_ref, p_ref, r_ref, o_ref):
        acc = p_ref[...].astype(f32)
        for j in range(3):
            acc = acc + r_ref[j].astype(f32)
        o_ref[...] = acc

    grid_spec = pltpu.PrefetchScalarGridSpec(
        num_scalar_prefetch=1, grid=(H // th,),
        in_specs=[pl.BlockSpec((None, th, C), lambda i, chip: (chip[0], i, 0)),
                  pl.BlockSpec((3, th, C), lambda i, chip: (0, i, 0))],
        out_specs=pl.BlockSpec((th, C), lambda i, chip: (i, 0)))
    return pl.pallas_call(body, name=name, grid_spec=grid_spec, out_shape=jax.ShapeDtypeStruct((H, C), f32),
                          compiler_params=_params(("parallel",)))(chip_idx, part, recv)


def _share_halves(name, halves):
    n = len(halves)

    def body(*refs):
        ins, outs, send, recv = refs[:n], refs[n:2 * n], refs[2 * n], refs[2 * n + 1]
        x, y, c = _me()
        copies = []
        for i in range(n):
            cp = _remote(ins[i], outs[i], send.at[i], recv.at[i], (x, y, 1 - c))
            cp.start()
            copies.append(cp)
        for cp in copies:
            cp.wait()

    return pl.pallas_call(
        body, name=name, out_shape=[jax.ShapeDtypeStruct(h.shape, h.dtype) for h in halves],
        in_specs=[HBM] * n, out_specs=[HBM] * n, scratch_shapes=_sems(n))(*halves)


def _adamw_big(name, w, g_mine, g_theirs, m, v, c_idx, tr=HALF_ROWS_F32):
    R, C = w.shape
    H = R // 2
    tr = _row_tile(H, tr)
    nb = H // tr

    def body(c_ref, w_ref, gm_ref, gt_ref, m_ref, v_ref, g_ref, d_ref, mo_ref, vo_ref):
        g_ = jnp.where(pl.program_id(0) // nb == c_ref[0], gm_ref[...], gt_ref[...])
        g_ref[...] = g_
        m_ = ADAM_B1 * m_ref[...] + (1.0 - ADAM_B1) * g_
        v_ = ADAM_B2 * v_ref[...] + (1.0 - ADAM_B2) * jnp.square(g_)
        m_hat = m_ / (1.0 - ADAM_B1 ** ADAM_STEP)
        v_hat = v_ / (1.0 - ADAM_B2 ** ADAM_STEP)
        d_ref[...] = -ADAM_LR * (m_hat / (jnp.sqrt(v_hat) + ADAM_EPS) + ADAM_WD * w_ref[...])
        mo_ref[...] = m_
        vo_ref[...] = v_

    full = pl.BlockSpec((tr, C), lambda i, c: (i, 0))
    half = pl.BlockSpec((tr, C), lambda i, c: (i % nb, 0))
    grid_spec = pltpu.PrefetchScalarGridSpec(num_scalar_prefetch=1, grid=(2 * nb,),
                                             in_specs=[full, half, half, full, full], out_specs=[full] * 4)
    return pl.pallas_call(body, name=name, grid_spec=grid_spec, out_shape=[jax.ShapeDtypeStruct((R, C), f32)] * 4,
                          compiler_params=_params(("parallel",)))(c_idx, w, g_mine, g_theirs, m, v)


BIG = ("ffn1_w_gate", "ffn1_w_up", "ffn1_w_down", "w_in", "w_out", "ffn2_w_gate", "ffn2_w_up", "ffn2_w_down",
       "ple_w_gate", "ple_w_proj")


SMALL = ("ffn1_norm", "mix_norm", "gm_ln_g", "gm_ln_b", "gm_w_s", "gm_b_s", "gm_out_norm", "conv_b", "dt_bias", "a_log",
         "d_skip", "ssm_norm", "ffn2_norm", "ple_norm", "ple_b_gate", "final_norm")
SMALL_C = 1024


def _pack_small(vals):
    parts = []
    for v in vals:
        f = v.astype(f32).reshape(-1)
        parts.append(jnp.pad(f, (0, -f.shape[0] % SMALL_C)))
    flat = jnp.concatenate(parts)
    rows = flat.shape[0] // SMALL_C
    return jnp.pad(flat, (0, (-rows % 8) * SMALL_C)).reshape(-1, SMALL_C)


def _unpack_small(pack, shapes):
    flat = pack.reshape(-1)
    out, off = [], 0
    for s in shapes:
        n = 1
        for d in s:
            n *= d
        out.append(flat[off:off + n].reshape(s))
        off += n + (-n % SMALL_C)
    return out


def _pad_lanes(v):
    return jnp.pad(v, ((0, 0), (0, LANES - v.shape[1])))


def _pad_rows(a):
    pad = [(0, 0)] * a.ndim
    pad[-2] = (0, -a.shape[-2] % ROW_PAD)
    return jnp.pad(a, pad) if pad[-2][1] else a


FETCH = (("ffn1_w_gate", "ffn1_w_up", "ffn1_w_down"), ("w_in", "conv_w", "w_out"),
         ("ffn2_w_gate", "ffn2_w_up", "ffn2_w_down", "ple_w_gate", "ple_w_proj"))
TRANSPOSED = ("ffn1_w_gate", "ffn1_w_up", "ffn2_w_gate", "ffn2_w_up", "w_in")
ROW_PAD = 32
DONE = (("ffn2_w_gate", "ffn2_w_up", "ffn2_w_down", "w_out", "ple_w_gate", "ple_w_proj"), ("w_in",),
        ("ffn1_w_gate", "ffn1_w_up", "ffn1_w_down"))


def _local_step(x, p, tgt, fetch, S, on_grads):
    G = GM_WIDTH
    K = N_CHIPS
    b_st = S["gm_b_s"][0].T
    w_s = S["gm_w_s"][0]
    dtb, alog, dsk = _pad_lanes(S["dt_bias"]), _pad_lanes(S["a_log"]), _pad_lanes(S["d_skip"])
    gfin = S["final_norm"].reshape(1, -1)

    def rows(a):
        return a.reshape(-1, D_MODEL)

    def shards(a):
        return a.reshape(K, -1, D_MODEL)

    wg1, wu1, wd1 = [rows(a) for a in fetch(0, None)]
    h1, n1, a1, b1 = _ffn_fwd("ffn1_fwd", x, S["ffn1_norm"], wg1, wu1, wd1)
    w_in4, cw4, wo4 = fetch(1, h1)
    w_in = w_in4.reshape(IN_PROJ, D_MODEL)
    w_uv = w_in[:2 * G]
    w_zxd = jnp.pad(w_in[2 * G:], ((0, ZXD - (IN_PROJ - 2 * G)), (0, 0)))
    conv_w = jnp.transpose(cw4, (1, 0, 2)).reshape(SSM_CONV, CONV_DIM)
    wo = wo4.reshape(-1, D_MODEL)
    n2, act, slope, z, xbc, dtr, ya = _mix_fwd(h1, S["mix_norm"], w_uv, w_zxd, S["gm_ln_g"], S["gm_ln_b"], w_s, b_st,
                                               S["gm_out_norm"])
    yb, xc, sg, y_ssd, sprev = _ssd_fwd(xbc, z, dtr, conv_w, S["conv_b"], dtb, alog, dsk, S["ssm_norm"])
    wg2, wu2, wd2, wpg4, wpp4 = fetch(2, yb)
    wg2, wu2, wd2 = rows(wg2), rows(wu2), rows(wd2)
    h2, h3, n3, a2, b2 = _ffn_fwd("ffn2_fwd", h1, S["ffn2_norm"], wg2, wu2, wd2, pre=(ya, yb, wo))
    dh3, loss, dgp, dwpg, dbpg, dwpp, dgf = _tail(h3, p, tgt, S["ple_norm"], wpg4.reshape(-1, D_MODEL), S["ple_b_gate"], wpp4, gfin)
    dh2, da2, db2, hm2, dg_ffn2, dya, dyb = _ffn_bwd("ffn2_bwd", dh3, h2, S["ffn2_norm"], a2, b2, wg2, wu2, wd2, wo=wo, ga=G)
    dw_out = jnp.concatenate([_matmul_tn("dw_out_a", ya, dh2), _matmul_tn("dw_out_b", yb, dh2)], axis=0).reshape(wo4.shape)
    zero = on_grads(0, [shards(_matmul_tn("dw_ffn2_gate", da2, n3)), shards(_matmul_tn("dw_ffn2_up", db2, n3)),
                        shards(_matmul_tn("dw_ffn2_down", hm2, dh3, scale=0.5)), dw_out,
                        dwpg.astype(bf16).reshape(wpg4.shape), dwpp.astype(bf16)])
    dzxd, dcw, dcb, ddtb, dalog, ddsk, dgssm = _ssd_bwd(xbc, xc, sg, y_ssd, z, dtr, sprev, dyb, conv_w, S["conv_b"], dtb, alog, dsk,
                                                        S["ssm_norm"] + zero)
    dh1, duv, dg_mix, dlng, dlnb, dws, dbst, dgout = _mix_bwd(dh2, h1, S["mix_norm"], act, slope, dya, dzxd, w_uv, w_zxd, S["gm_ln_g"],
                                                              S["gm_ln_b"], w_s, b_st, S["gm_out_norm"])
    dw_in = jnp.concatenate([_matmul_tn("dw_in_uv", duv, n2), _matmul_tn("dw_in_zxd", dzxd, n2)[:IN_PROJ - 2 * G]], axis=0)
    zero = on_grads(1, [dw_in.reshape(w_in4.shape)])
    dx, da1, db1, hm1, dg_ffn1 = _ffn_bwd("ffn1_bwd", dh1, x, S["ffn1_norm"] + zero, a1, b1, wg1, wu1, wd1)
    zero = on_grads(2, [shards(_matmul_tn("dw_ffn1_gate", da1, n1)), shards(_matmul_tn("dw_ffn1_up", db1, n1)),
                        shards(_matmul_tn("dw_ffn1_down", hm1, dh1, scale=0.5))])
    loss = loss + zero
    nh = SSM_HEADS
    gS = {"ffn1_norm": dg_ffn1, "mix_norm": dg_mix, "gm_ln_g": dlng, "gm_ln_b": dlnb, "gm_w_s": dws[None], "gm_b_s": dbst.T[None],
          "gm_out_norm": dgout, "conv_b": dcb, "dt_bias": ddtb[:, :nh], "a_log": dalog[:, :nh], "d_skip": ddsk[:, :nh],
          "ssm_norm": dgssm, "ffn2_norm": dg_ffn2, "ple_norm": dgp, "ple_b_gate": dbpg, "final_norm": dgf.reshape(-1)}
    return loss, dx, dcw, gS


_WEIGHTS = ("ffn1_norm", "ffn1_w_gate", "ffn1_w_up", "ffn1_w_down", "mix_norm", "w_in", "gm_ln_g", "gm_ln_b", "gm_w_s", "gm_b_s",
            "gm_out_norm", "conv_w", "conv_b", "dt_bias", "a_log", "d_skip", "ssm_norm", "w_out", "ffn2_norm", "ffn2_w_gate",
            "ffn2_w_up", "ffn2_w_down", "ple_norm", "ple_w_gate", "ple_b_gate", "ple_w_proj", "final_norm")
_BIG_NAMES = BIG


def kernel(x, p, ffn1_norm, ffn1_w_gate, ffn1_w_up, ffn1_w_down, mix_norm, w_in, gm_ln_g, gm_ln_b, gm_w_s, gm_b_s, gm_out_norm, conv_w, conv_b, dt_bias, a_log, d_skip, ssm_norm, w_out, ffn2_norm, ffn2_w_gate, ffn2_w_up, ffn2_w_down, ple_norm, ple_w_gate, ple_b_gate, ple_w_proj, final_norm, loss_target, m_ffn1_norm, m_ffn1_w_gate, m_ffn1_w_up, m_ffn1_w_down, m_mix_norm, m_w_in, m_gm_ln_g, m_gm_ln_b, m_gm_w_s, m_gm_b_s, m_gm_out_norm, m_conv_w, m_conv_b, m_dt_bias, m_a_log, m_d_skip, m_ssm_norm, m_w_out, m_ffn2_norm, m_ffn2_w_gate, m_ffn2_w_up, m_ffn2_w_down, m_ple_norm, m_ple_w_gate, m_ple_b_gate, m_ple_w_proj, m_final_norm, v_ffn1_norm, v_ffn1_w_gate, v_ffn1_w_up, v_ffn1_w_down, v_mix_norm, v_w_in, v_gm_ln_g, v_gm_ln_b, v_gm_w_s, v_gm_b_s, v_gm_out_norm, v_conv_w, v_conv_b, v_dt_bias, v_a_log, v_d_skip, v_ssm_norm, v_w_out, v_ffn2_norm, v_ffn2_w_gate, v_ffn2_w_up, v_ffn2_w_down, v_ple_norm, v_ple_w_gate, v_ple_b_gate, v_ple_w_proj, v_final_norm):
    given = dict(locals())
    w = {n: given[n] for n in _WEIGHTS}
    m = {n: given["m_" + n] for n in _WEIGHTS}
    v = {n: given["v_" + n] for n in _WEIGHTS}

    c_idx = lax.axis_index("c").astype(jnp.int32).reshape(1)
    chip = 2 * lax.axis_index("x") + lax.axis_index("y")
    chip_idx = chip.astype(jnp.int32).reshape(1)

    shard = {n: (jnp.swapaxes(w[n][0], 0, 1) if n in TRANSPOSED else w[n][0]).astype(bf16) for n in BIG}
    shard["conv_w"] = w["conv_w"][0]
    first = _gather_weights([shard[n] for n in FETCH[0]], [True] * len(FETCH[0]))
    fetching, after = [], first[-1]
    for k in (1, 2):
        srcs = [shard[n] for n in FETCH[k]]
        lands = [jax.ShapeDtypeStruct((N_CHIPS,) + s.shape, s.dtype) for s in srcs]
        fetching.append(_copies_start("gather%d_start" % k, srcs, lands, 4 * len(srcs), _gather_copies, after))
        after = fetching[-1][4]

    def fetch(k, after_):
        return first if k == 0 else _copies_wait("gather%d_wait" % k, fetching[k - 1], _gather_copies, [after_])[1]

    exchanging = []

    def on_grads(k, grads):
        grads = [_pad_rows(g_) for g_ in grads]
        others = _swap_halves("swap%d" % k, grads)
        parts = [_add_halves("add_" + n, g_, o_, c_idx) for n, g_, o_ in zip(DONE[k], grads, others)]
        lands = [jax.ShapeDtypeStruct((3,) + p_.shape[1:], p_.dtype) for p_ in parts]
        exchanging.append(_copies_start("exchange%d_start" % k, parts, lands, 3 * len(parts), _partial_copies, c_idx))
        return exchanging[-1][4][0, 0]

    S = {n: w[n] for n in SMALL}
    S["ffn1_norm"] = S["ffn1_norm"] + after[0, 0]
    loss, dx, dcw, gS = _local_step(x[0], p[0, 0], loss_target[0], fetch, S, on_grads)

    small = _pack_small([gS[n] for n in SMALL] + [dcw, loss[:, :1]])
    small_lands = [jax.ShapeDtypeStruct((N_DEV - 1,) + small.shape, small.dtype)]
    small_st = _copies_start("small_start", [small], small_lands, N_DEV - 1, _small_copies, c_idx)

    g, delta, new_m, new_v = {}, {}, {}, {}
    after = [small_st[4]]
    for k in range(len(DONE)):
        parts, recv = _copies_wait("exchange%d_wait" % k, exchanging[k], _partial_copies, after)
        mine = [_sum_partials("sum_" + n, p_, r_, chip_idx) for n, p_, r_ in zip(DONE[k], parts, recv)]
        theirs = _share_halves("share%d" % k, mine)
        after = []
        for n, gm_, gt_ in zip(DONE[k], mine, theirs):
            flip = (lambda a: jnp.swapaxes(a, 0, 1)) if n in TRANSPOSED else (lambda a: a)
            rows = flip(w[n][0]).shape[0]
            w_, m_, v_ = [_pad_rows(flip(a[n][0])) for a in (w, m, v)]
            outs = _adamw_big("adamw_" + n, w_, gm_, gt_, m_, v_, c_idx)
            g[n], delta[n], new_m[n], new_v[n] = [flip(o[:rows])[None] for o in outs]
            after.append(outs[3])
    (own,), (slots,) = _copies_wait("small_wait", small_st, _small_copies, after)
    dev_idx = (2 * chip + lax.axis_index("c")).astype(jnp.int32).reshape(1)
    small_shapes = [w[n].shape for n in SMALL] + [dcw.shape, (1, 1)]
    small_sum = _unpack_small(_sum_small(own, slots, dev_idx), small_shapes)
    g.update({n: small_sum[i] for i, n in enumerate(SMALL)})
    cshard = w["conv_w"].shape[2]
    g["conv_w"] = lax.dynamic_slice_in_dim(small_sum[len(SMALL)], chip * cshard, cshard, axis=1)[None]
    loss_total = small_sum[len(SMALL) + 1].reshape(())
    sm_names = SMALL + ("conv_w",)
    sm_shapes = [w[n].shape for n in sm_names]
    d_s, m_s, v_s = _adamw("adamw_small", _pack_small([w[n] for n in sm_names]), _pack_small([g[n] for n in sm_names]),
                           _pack_small([m[n] for n in sm_names]), _pack_small([v[n] for n in sm_names]))
    for dst, src in ((delta, d_s), (new_m, m_s), (new_v, v_s)):
        for n, val in zip(sm_names, _unpack_small(src, sm_shapes)):
            dst[n] = val

    return (loss_total, dx[None], *[g[n] for n in _WEIGHTS], *[delta[n] for n in _WEIGHTS],
            *[new_m[n] for n in _WEIGHTS], *[new_v[n] for n in _WEIGHTS])
```

```python
import jax
import jax.numpy as jnp
from jax import lax
from jax.experimental import pallas as pl
from jax.experimental.pallas import tpu as pltpu

f32 = jnp.float32
bf16 = jnp.bfloat16
MESH = pl.DeviceIdType.MESH
HIGHEST = lax.Precision.HIGHEST

EPS = 1e-6
N_CHIPS = 4
N_DEV = 8
D_MODEL = 1024
GM_WIDTH = 1024
GM_HEADS = 8
CHUNK = 128
SSM_WIDTH = 1024
SSM_HEADS = 16
SSM_HEAD_DIM = 64
SSM_GROUPS = 2
SSM_STATE = 128
SSM_CONV = 4
CONV_DIM = SSM_WIDTH + 2 * SSM_GROUPS * SSM_STATE
IN_PROJ = 2 * GM_WIDTH + SSM_WIDTH + CONV_DIM + SSM_HEADS
LANES = 128
ZXD = SSM_WIDTH + CONV_DIM + LANES

ADAM_LR = 0.001
ADAM_B1 = 0.9
ADAM_B2 = 0.999
ADAM_EPS = 1e-08
ADAM_WD = 0.01
ADAM_STEP = 10

VMEM_LIMIT = 56 * 1024 * 1024
HALF_ROWS_BF16 = 592
HALF_ROWS_F32 = 320


def _dot(a, b):
    return jnp.dot(a, b, preferred_element_type=f32)


def _dot_nt(a, b):
    return lax.dot_general(a, b, (((1,), (1,)), ((), ())), preferred_element_type=f32)


def _dot_tn(a, b):
    return lax.dot_general(a, b, (((0,), (0,)), ((), ())), preferred_element_type=f32)


def _rms(x, g):
    return x * lax.rsqrt(jnp.mean(x * x, axis=-1, keepdims=True) + EPS) * g


def _layernorm(x, g, b):
    mu = jnp.mean(x, axis=-1, keepdims=True)
    xc = x - mu
    return xc * lax.rsqrt(jnp.mean(xc * xc, axis=-1, keepdims=True) + EPS) * g + b


def _sigmoid(x):
    return 1.0 / (1.0 + jnp.exp(-x))


def _softplus(x):
    return jnp.maximum(x, 0.0) + jnp.log(1.0 + jnp.exp(-jnp.abs(x)))


def _full(shape):
    nd = len(shape)
    return pl.BlockSpec(shape, lambda *_: (0,) * nd, pipeline_mode=pl.Buffered(1))


def _acc(shape):
    nd = len(shape)
    return pl.BlockSpec(shape, lambda *_: (0,) * nd)


def _rows(tm, ncols):
    return pl.BlockSpec((tm, ncols), lambda i: (i, 0))


def _params(sem):
    return pltpu.CompilerParams(dimension_semantics=sem, vmem_limit_bytes=VMEM_LIMIT)


def _row_tile(rows, target, mult=8):
    best = rows
    for t in range(mult, min(rows, target) + 1, mult):
        if rows % t == 0:
            best = t
    return best if best <= target else rows


def _ffn_fwd(name, h, g, wg, wu, wd, pre=None, tm=256):
    T, D = h.shape
    F = wg.shape[0]
    tm = min(tm, T)

    def body(*refs):
        if pre is None:
            h_ref, g_ref, wg_ref, wu_ref, wd_ref, ho_ref, n_ref, a_ref, b_ref = refs
            hin = h_ref[...]
        else:
            (h_ref, ya_ref, yb_ref, wo_ref, g_ref, wg_ref, wu_ref, wd_ref,
             hi_ref, ho_ref, n_ref, a_ref, b_ref) = refs
            ga = ya_ref.shape[1]
            hin = h_ref[...] + _dot(ya_ref[...], wo_ref[:ga, :]) + _dot(yb_ref[...], wo_ref[ga:, :])
            hi_ref[...] = hin
        n = _rms(hin, g_ref[...]).astype(bf16)
        n_ref[...] = n
        a = _dot_nt(n, wg_ref[...]).astype(bf16)
        b = _dot_nt(n, wu_ref[...]).astype(bf16)
        a_ref[...] = a
        b_ref[...] = b
        af = a.astype(f32)
        hm = (af * _sigmoid(af) * b.astype(f32)).astype(bf16)
        ho_ref[...] = hin + 0.5 * _dot(hm, wd_ref[...])

    ins = [h] + (list(pre) if pre is not None else []) + [g, wg, wu, wd]
    in_specs = [_rows(tm, D)]
    if pre is not None:
        in_specs += [_rows(tm, pre[0].shape[1]), _rows(tm, pre[1].shape[1]), _full(pre[2].shape)]
    in_specs += [_full(g.shape), _full(wg.shape), _full(wu.shape), _full(wd.shape)]
    outs = [jax.ShapeDtypeStruct((T, D), f32), jax.ShapeDtypeStruct((T, D), bf16),
            jax.ShapeDtypeStruct((T, F), bf16), jax.ShapeDtypeStruct((T, F), bf16)]
    out_specs = [_rows(tm, D), _rows(tm, D), _rows(tm, F), _rows(tm, F)]
    if pre is not None:
        outs = [jax.ShapeDtypeStruct((T, D), f32)] + outs
        out_specs = [_rows(tm, D)] + out_specs
    return pl.pallas_call(body, name=name, grid=(T // tm,), in_specs=in_specs, out_specs=out_specs,
                          out_shape=outs, compiler_params=_params(("parallel",)))(*ins)


def _ffn_bwd(name, dh, hin, g, a, b, wg, wu, wd, wo=None, ga=0, tm=256):
    T, D = dh.shape
    F = wg.shape[0]
    tm = min(tm, T)

    def body(*refs):
        if wo is None:
            (dh_ref, hin_ref, g_ref, a_ref, b_ref, wg_ref, wu_ref, wd_ref,
             dhi_ref, da_ref, db_ref, hm_ref, dg_ref) = refs
        else:
            (dh_ref, hin_ref, g_ref, a_ref, b_ref, wg_ref, wu_ref, wd_ref, wo_ref,
             dhi_ref, da_ref, db_ref, hm_ref, dg_ref, dya_ref, dyb_ref) = refs

        @pl.when(pl.program_id(0) == 0)
        def _():
            dg_ref[...] = jnp.zeros_like(dg_ref)

        dh_ = dh_ref[...]
        dhb = (0.5 * dh_).astype(bf16)
        dhm = _dot_nt(dhb, wd_ref[...])
        af = a_ref[...].astype(f32)
        bf = b_ref[...].astype(f32)
        sg = _sigmoid(af)
        sl_ = af * sg
        da = (dhm * bf * (sg * (1.0 + af * (1.0 - sg)))).astype(bf16)
        db = (dhm * sl_).astype(bf16)
        da_ref[...] = da
        db_ref[...] = db
        hm_ref[...] = (sl_ * bf).astype(bf16)
        dn = _dot(da, wg_ref[...]) + _dot(db, wu_ref[...])
        _, vjp = jax.vjp(_rms, hin_ref[...], g_ref[...])
        dx, dg = vjp(dn)
        dhi = dh_ + dx
        dhi_ref[...] = dhi
        dg_ref[...] += dg
        if wo is not None:
            dhib = dhi.astype(bf16)
            dya_ref[...] = _dot_nt(dhib, wo_ref[:ga, :]).astype(bf16)
            dyb_ref[...] = _dot_nt(dhib, wo_ref[ga:, :]).astype(bf16)

    ins = [dh, hin, g, a, b, wg, wu, wd]
    in_specs = [_rows(tm, D), _rows(tm, D), _full(g.shape), _rows(tm, F), _rows(tm, F),
                _full(wg.shape), _full(wu.shape), _full(wd.shape)]
    act = jax.ShapeDtypeStruct((T, F), bf16)
    outs = [jax.ShapeDtypeStruct((T, D), f32), act, act, act, jax.ShapeDtypeStruct(g.shape, f32)]
    out_specs = [_rows(tm, D), _rows(tm, F), _rows(tm, F), _rows(tm, F), _acc(g.shape)]
    if wo is not None:
        gb = wo.shape[0] - ga
        ins += [wo]
        in_specs += [_full(wo.shape)]
        outs += [jax.ShapeDtypeStruct((T, ga), bf16), jax.ShapeDtypeStruct((T, gb), bf16)]
        out_specs += [_rows(tm, ga), _rows(tm, gb)]
    return pl.pallas_call(body, name=name, grid=(T // tm,), in_specs=in_specs, out_specs=out_specs,
                          out_shape=outs, compiler_params=_params(("arbitrary",)))(*ins)


def _matmul_tn(name, a, b, scale=1.0, tk=2048):
    T, M = a.shape
    N = b.shape[1]
    tk = min(tk, T)
    nk = T // tk
    tn = LANES * max(d for d in range(1, N // LANES + 1) if (N // LANES) % d == 0 and (d == 1 or M * d * LANES * 4 <= 6 * 1024 * 1024))

    def body(a_ref, b_ref, o_ref, acc):
        k = pl.program_id(1)

        @pl.when(k == 0)
        def _():
            acc[...] = jnp.zeros_like(acc)

        bb = b_ref[...]
        if scale != 1.0:
            bb = bb * scale
        acc[...] += _dot_tn(a_ref[...].astype(bf16), bb.astype(bf16))

        @pl.when(k == nk - 1)
        def _():
            o_ref[...] = acc[...].astype(bf16)

    return pl.pallas_call(
        body, name=name, grid=(N // tn, nk),
        in_specs=[pl.BlockSpec((tk, M), lambda j, k: (k, 0)), pl.BlockSpec((tk, tn), lambda j, k: (k, j))],
        out_specs=pl.BlockSpec((M, tn), lambda j, k: (0, j)),
        out_shape=jax.ShapeDtypeStruct((M, N), bf16), scratch_shapes=[pltpu.VMEM((M, tn), f32)],
        compiler_params=_params(("parallel", "arbitrary")))(a, b)


def _gelu_and_slope(x):
    cdf = 0.5 * (1.0 + lax.erf(x * 0.7071067811865476))
    return x * cdf, cdf + x * (0.3989422804014327 * jnp.exp(-0.5 * x * x))


def _tril_mask():
    r = lax.broadcasted_iota(jnp.int32, (CHUNK, CHUNK), 0)
    c = lax.broadcasted_iota(jnp.int32, (CHUNK, CHUNK), 1)
    return c <= r


def _gm_mix(vnb, ws_ref, bst, mixed_sc, tm):
    mask = _tril_mask()
    for h in range(GM_HEADS):
        wt = jnp.where(mask, ws_ref[h], 0.0).astype(bf16)
        bias = bst[:, h:h + 1]
        for q in range(tm // CHUNK):
            rs = slice(q * CHUNK, (q + 1) * CHUNK)
            cs = slice(h * CHUNK, (h + 1) * CHUNK)
            mixed_sc[rs, cs] = _dot(wt, vnb[rs, cs]) + bias


def _mix_fwd(h1, gmix, w_uv, w_zxd, ln_g, ln_b, w_s, b_st, gout, tm=512):
    T, D = h1.shape
    tm = min(tm, T)
    G = GM_WIDTH

    def body(h_ref, g_ref, wuv_ref, wzxd_ref, lng_ref, lnb_ref, ws_ref, bst_ref, gout_ref,
             n_ref, act_ref, slope_ref, z_ref, xbc_ref, dt_ref, ya_ref, mixed_sc):
        n = _rms(h_ref[...], g_ref[...]).astype(bf16)
        n_ref[...] = n
        uv = _dot_nt(n, wuv_ref[...]).astype(bf16)
        zxd = _dot_nt(n, wzxd_ref[...])
        z_ref[...] = zxd[:, :SSM_WIDTH].astype(bf16)
        xbc_ref[...] = zxd[:, SSM_WIDTH:SSM_WIDTH + CONV_DIM].astype(bf16)
        dt_ref[...] = zxd[:, SSM_WIDTH + CONV_DIM:]
        act, slope = _gelu_and_slope(uv.astype(f32))
        act = act.astype(bf16)
        act_ref[...] = act
        slope_ref[...] = slope.astype(bf16)
        ug, vg = act[:, :G].astype(f32), act[:, G:].astype(f32)
        _gm_mix(_layernorm(vg, lng_ref[...], lnb_ref[...]).astype(bf16), ws_ref, bst_ref[...], mixed_sc, tm)
        ya_ref[...] = _rms(ug * mixed_sc[...], gout_ref[...]).astype(bf16)

    ins = [h1, gmix, w_uv, w_zxd, ln_g, ln_b, w_s, b_st, gout]
    in_specs = [_rows(tm, D)] + [_full(x.shape) for x in ins[1:]]
    outs = [jax.ShapeDtypeStruct((T, D), bf16), jax.ShapeDtypeStruct((T, 2 * G), bf16), jax.ShapeDtypeStruct((T, 2 * G), bf16),
            jax.ShapeDtypeStruct((T, SSM_WIDTH), bf16), jax.ShapeDtypeStruct((T, CONV_DIM), bf16),
            jax.ShapeDtypeStruct((T, LANES), f32), jax.ShapeDtypeStruct((T, G), bf16)]
    out_specs = [_rows(tm, D), _rows(tm, 2 * G), _rows(tm, 2 * G), _rows(tm, SSM_WIDTH), _rows(tm, CONV_DIM), _rows(tm, LANES),
                 _rows(tm, G)]
    return pl.pallas_call(body, name="mix_fwd", grid=(T // tm,), in_specs=in_specs, out_specs=out_specs,
                          out_shape=outs, scratch_shapes=[pltpu.VMEM((tm, G), f32)],
                          compiler_params=_params(("parallel",)))(*ins)


def _mix_bwd(dh, h1, gmix, act, slope, dya, dzxd, w_uv, w_zxd, ln_g, ln_b, w_s, b_st, gout, tm=256):
    T, D = dh.shape
    tm = min(tm, T)
    G = GM_WIDTH

    def body(dh_ref, h_ref, g_ref, act_ref, slope_ref, dya_ref, dzxd_ref, wuv_ref, wzxd_ref, lng_ref, lnb_ref, ws_ref,
             bst_ref, gout_ref, dhi_ref, duv_ref, dg_ref, dlng_ref, dlnb_ref, dws_ref, dbst_ref, dgout_ref, mixed_sc, dvn_sc):
        @pl.when(pl.program_id(0) == 0)
        def _():
            for r in (dg_ref, dlng_ref, dlnb_ref, dws_ref, dbst_ref, dgout_ref):
                r[...] = jnp.zeros_like(r)

        dn_z = _dot(dzxd_ref[...], wzxd_ref[...])
        ug = act_ref[:, :G].astype(f32)
        vn, ln_vjp = jax.vjp(_layernorm, act_ref[:, G:].astype(f32), lng_ref[...], lnb_ref[...])
        vnb = vn.astype(bf16)
        _gm_mix(vnb, ws_ref, bst_ref[...], mixed_sc, tm)
        mixed = mixed_sc[...]
        _, out_vjp = jax.vjp(_rms, ug * mixed, gout_ref[...])
        dpre, dgout = out_vjp(dya_ref[...].astype(f32))
        dgout_ref[...] += dgout
        dug = dpre * mixed
        dmixed = dpre * ug
        mask = _tril_mask()
        lane = lax.broadcasted_iota(jnp.int32, (1, GM_HEADS), 1)
        dbst = jnp.zeros((CHUNK, GM_HEADS), f32)
        for h in range(GM_HEADS):
            wt = jnp.where(mask, ws_ref[h], 0.0).astype(bf16)
            cs = slice(h * CHUNK, (h + 1) * CHUNK)
            dw = jnp.zeros((CHUNK, CHUNK), f32)
            for q in range(tm // CHUNK):
                rs = slice(q * CHUNK, (q + 1) * CHUNK)
                dm = dmixed[rs, cs]
                dmb = dm.astype(bf16)
                dw = dw + _dot_nt(dmb, vnb[rs, cs])
                dbst = dbst + jnp.sum(dm, axis=1, keepdims=True) * (lane == h).astype(f32)
                dvn_sc[rs, cs] = _dot_tn(wt, dmb)
            dws_ref[h] += jnp.where(mask, dw, 0.0)
        dbst_ref[...] += dbst
        dvg, dlng, dlnb = ln_vjp(dvn_sc[...])
        duv = (jnp.concatenate([dug, dvg], axis=1) * slope_ref[...].astype(f32)).astype(bf16)
        duv_ref[...] = duv
        dlng_ref[...] += dlng
        dlnb_ref[...] += dlnb
        dn = dn_z + _dot(duv, wuv_ref[...])
        _, vjp = jax.vjp(_rms, h_ref[...], g_ref[...])
        dx, dg = vjp(dn)
        dhi_ref[...] = dh_ref[...] + dx
        dg_ref[...] += dg

    ins = [dh, h1, gmix, act, slope, dya, dzxd, w_uv, w_zxd, ln_g, ln_b, w_s, b_st, gout]
    in_specs = ([_rows(tm, D), _rows(tm, D), _full(gmix.shape), _rows(tm, 2 * G), _rows(tm, 2 * G), _rows(tm, G),
                 _rows(tm, dzxd.shape[1])] + [_full(x.shape) for x in ins[7:]])
    accs = (gmix, ln_g, ln_b, w_s, b_st, gout)
    outs = ([jax.ShapeDtypeStruct((T, D), f32), jax.ShapeDtypeStruct((T, 2 * G), bf16)]
            + [jax.ShapeDtypeStruct(x.shape, f32) for x in accs])
    out_specs = [_rows(tm, D), _rows(tm, 2 * G)] + [_acc(x.shape) for x in accs]
    return pl.pallas_call(body, name="mix_bwd", grid=(T // tm,), in_specs=in_specs, out_specs=out_specs,
                          out_shape=outs, scratch_shapes=[pltpu.VMEM((tm, G), f32), pltpu.VMEM((tm, G), f32)],
                          compiler_params=_params(("arbitrary",)))(*ins)


HALO = 16
SSD_SUB = 4


class _RowsOf:
    def __init__(self, ref, rows):
        self.ref, self.rows = ref, rows

    def _index(self, idx):
        return (self.rows, slice(None)) if idx is Ellipsis else (self.rows,) + tuple(idx[1:])

    def __getitem__(self, idx):
        return self.ref[self._index(idx)]

    def __setitem__(self, idx, value):
        self.ref[self._index(idx)] = value
PAIRS = SSM_HEADS // 2
PAIR_W = 2 * SSM_HEAD_DIM


def _split(x, n):
    parts = []
    for _ in range(n):
        p = x.astype(bf16)
        parts.append(p)
        x = x - p.astype(f32)
    return parts


def _dot_sel(x, sel_n, n):
    return _dot(jnp.concatenate(_split(x, n), axis=1), sel_n)


def _sel_dot(sel, x, n):
    return _dot(jnp.concatenate([sel] * n, axis=1), jnp.concatenate(_split(x, n), axis=0))


EXPAND_SPLIT = 3
REDUCE_SPLIT = 2


def _head_mats():
    ex = (jnp.arange(SSM_WIDTH)[None, :] // SSM_HEAD_DIM == jnp.arange(LANES)[:, None]).astype(bf16)
    return jnp.tile(ex, (EXPAND_SPLIT, 1)), jnp.tile(ex.T, (REDUCE_SPLIT, 1))


def _shift_mat(rows, cols, off):
    r = lax.broadcasted_iota(jnp.int32, (rows, cols), 0)
    c = lax.broadcasted_iota(jnp.int32, (rows, cols), 1)
    return (c == r + off).astype(bf16)


def _ssd_conv(halo, x, cw_ref, cb_ref):
    ext = jnp.concatenate([halo, x], axis=0)
    xc = cb_ref[...] + cw_ref[SSM_CONV - 1:SSM_CONV, :] * x.astype(f32)
    for j in range(SSM_CONV - 1):
        xc = xc + cw_ref[j:j + 1, :] * _dot(_shift_mat(CHUNK, HALO + CHUNK, HALO - SSM_CONV + 1 + j), ext)
    return xc


def _ssd_front(dtr, dtb_ref, alog_ref):
    dt = _softplus(dtr + dtb_ref[...])
    a = -jnp.exp(alog_ref[...])
    acs = jnp.dot(_tril_mask().astype(f32), dt * a, preferred_element_type=f32, precision=HIGHEST)
    return dt, a, acs


def _ssd_wide(xa, dt, acs, dsk, ex):
    dt_x = _dot_sel(dt, ex, EXPAND_SPLIT)
    acs_x = _dot_sel(acs, ex, EXPAND_SPLIT)
    dsk_x = _dot_sel(jnp.broadcast_to(dsk, (8, LANES)), ex, EXPAND_SPLIT)[0:1]
    e_x = jnp.exp(acs_x)
    r_x = jnp.exp(acs_x[CHUNK - 1:CHUNK, :] - acs_x)
    xs = xa[:, :SSM_WIDTH]
    xd = xs * dt_x
    return dt_x, dsk_x, e_x, r_x, xs, xd, xd * r_x


def _pair_stack(v, lo):
    return jnp.concatenate([jnp.where(lo, v, 0.0), jnp.where(lo, 0.0, v)], axis=0)


def _ssd_pair(j, acs, acs_t, cb):
    out = []
    tril = _tril_mask()
    for h in (2 * j, 2 * j + 1):
        dk = jnp.exp(jnp.where(tril, acs[:, h:h + 1] - acs_t[h:h + 1, :], -jnp.inf))
        out.append((dk, cb * dk))
    return out


def _pair_col(row_lo, tot, j):
    return jnp.exp(jnp.where(row_lo, tot[:, 2 * j:2 * j + 1], tot[:, 2 * j + 1:2 * j + 2]))


def _gated_norm(y, z, g):
    yg = y * (z * _sigmoid(z))
    half = SSM_WIDTH // SSM_GROUPS
    parts = []
    for k in range(SSM_GROUPS):
        s = yg[:, k * half:(k + 1) * half]
        parts.append(s * lax.rsqrt(jnp.mean(s * s, axis=-1, keepdims=True) + EPS))
    return jnp.concatenate(parts, axis=1) * g


def _group_mats(xa):
    out = []
    for g in range(SSM_GROUPS):
        bm = xa[:, SSM_WIDTH + g * SSM_STATE:SSM_WIDTH + (g + 1) * SSM_STATE].astype(bf16)
        cm = xa[:, SSM_WIDTH + (SSM_GROUPS + g) * SSM_STATE:SSM_WIDTH + (SSM_GROUPS + g + 1) * SSM_STATE].astype(bf16)
        out.append((cm, bm, _dot_nt(cm, bm)))
    return out


def _ssd_fwd(xbc, z, dtr, conv_w, conv_b, dt_bias, a_log, d_skip, ssm_norm):
    T = xbc.shape[0]
    nc = T // CHUNK
    N = SSM_STATE

    def body(xbc_ref, halo_ref, z_ref, dtr_ref, cw_ref, cb_ref, dtb_ref, alog_ref, dsk_ref, g_ref, ex_ref,
             yb_ref, xc_ref, sg_ref, y_ref, sprev_ref, s_sc):
        i = pl.program_id(0)

        @pl.when(i == 0)
        def _():
            s_sc[...] = jnp.zeros_like(s_sc)

        lo = lax.broadcasted_iota(jnp.int32, (CHUNK, PAIR_W), 1) < SSM_HEAD_DIM
        row_lo = lax.broadcasted_iota(jnp.int32, (PAIR_W, 1), 0) < SSM_HEAD_DIM
        for k in range(SUB):
            rs = slice(k * CHUNK, (k + 1) * CHUNK)
            if k == 0:
                halo = halo_ref[...]
                halo = jnp.where(i > 0, halo, jnp.zeros_like(halo))
            else:
                halo = xbc_ref[k * CHUNK - HALO:k * CHUNK, :]
            xc = _ssd_conv(halo, xbc_ref[rs, :], cw_ref, cb_ref)
            sg = _sigmoid(xc)
            xc_ref[rs, :] = xc
            sg_ref[rs, :] = sg
            xa = xc * sg
            dt, _, acs = _ssd_front(dtr_ref[rs, :], dtb_ref, alog_ref)
            _, dsk_x, e_x, _, xs, xd, gm = _ssd_wide(xa, dt, acs, dsk_ref[...], ex_ref[...])
            acs_t = acs.T
            tot = acs[CHUNK - 1:CHUNK, :]
            groups = _group_mats(xa)
            ys = []
            for j in range(PAIRS):
                cmb, bmb, cb = groups[j // (PAIRS // SSM_GROUPS)]
                ps = slice(j * PAIR_W, (j + 1) * PAIR_W)
                (_, m0), (_, m1) = _ssd_pair(j, acs, acs_t, cb)
                sp = s_sc[j]
                yd = _dot(jnp.concatenate([m0, m1], axis=1).astype(bf16), _pair_stack(xd[:, ps], lo).astype(bf16))
                ys.append(yd + e_x[:, ps] * _dot_nt(cmb, sp.astype(bf16)))
                sprev_ref[k, j] = sp
                s_sc[j] = _pair_col(row_lo, tot, j) * sp + _dot_tn(gm[:, ps].astype(bf16), bmb)
            y = jnp.concatenate(ys, axis=1) + xs * dsk_x
            y_ref[rs, :] = y
            yb_ref[rs, :] = _gated_norm(y, z_ref[rs, :].astype(f32), g_ref[...]).astype(bf16)

    params = [conv_w, conv_b, dt_bias, a_log, d_skip, ssm_norm, _head_mats()[0]]
    SUB = SSD_SUB if nc % SSD_SUB == 0 else 1
    hp = SUB * CHUNK // HALO
    R = SUB * CHUNK
    in_specs = [_rows(R, CONV_DIM), pl.BlockSpec((HALO, CONV_DIM), lambda i: (jnp.maximum(i * hp - 1, 0), 0)),
                _rows(R, SSM_WIDTH), _rows(R, LANES)] + [_full(x.shape) for x in params]
    return pl.pallas_call(
        body, name="ssd_fwd", grid=(nc // SUB,), in_specs=in_specs,
        out_specs=[_rows(R, SSM_WIDTH), _rows(R, CONV_DIM), _rows(R, CONV_DIM), _rows(R, SSM_WIDTH),
                   pl.BlockSpec((SUB, PAIRS, PAIR_W, N), lambda i: (i, 0, 0, 0))],
        out_shape=[jax.ShapeDtypeStruct((T, SSM_WIDTH), bf16), jax.ShapeDtypeStruct((T, CONV_DIM), f32),
                   jax.ShapeDtypeStruct((T, CONV_DIM), f32), jax.ShapeDtypeStruct((T, SSM_WIDTH), f32),
                   jax.ShapeDtypeStruct((nc, PAIRS, PAIR_W, N), f32)],
        scratch_shapes=[pltpu.VMEM((PAIRS, PAIR_W, N), f32)],
        compiler_params=_params(("arbitrary",)))(xbc, xbc, z, dtr, *params)


def _ssd_bwd(xbc, xc, sg, y, z, dtr, sprev, dyb, conv_w, conv_b, dt_bias, a_log, d_skip, ssm_norm):
    T = xbc.shape[0]
    nc = T // CHUNK
    H, N = SSM_HEADS, SSM_STATE
    PG = PAIRS // SSM_GROUPS

    def chunk(xbc_ref, xc_ref, sg_ref, y_ref, z_ref, dtr_ref, sprev_k, dyb_ref, cw_ref, cb_ref, dtb_ref, alog_ref, dsk_ref,
              g_ref, ex_ref, rd_ref, dzxd_ref, dcw_ref, dcb_ref, ddtb_ref, dalog_ref, ddsk_ref, dg_ref, ds_sc, next_sc):
        xc = xc_ref[...]
        sg = sg_ref[...]
        xa = xc * sg
        dt, a, acs = _ssd_front(dtr_ref[...], dtb_ref, alog_ref)
        dt_x, dsk_x, e_x, r_x, xs, xd, gm = _ssd_wide(xa, dt, acs, dsk_ref[...], ex_ref[...])
        acs_t = acs.T
        tot = acs[CHUNK - 1:CHUNK, :]
        groups = _group_mats(xa)
        lo = lax.broadcasted_iota(jnp.int32, (CHUNK, PAIR_W), 1) < SSM_HEAD_DIM
        row_lo = lax.broadcasted_iota(jnp.int32, (PAIR_W, 1), 0) < SSM_HEAD_DIM
        pairs, zs = [], []
        for j in range(PAIRS):
            cmb, _, cb = groups[j // PG]
            pairs.append(_ssd_pair(j, acs, acs_t, cb))
            zs.append(_dot_nt(cmb, sprev_k[j].astype(bf16)))
        zf = jnp.concatenate(zs, axis=1)
        _, gn_vjp = jax.vjp(_gated_norm, y_ref[...], z_ref[...].astype(f32), g_ref[...])
        dy, dz, dg = gn_vjp(dyb_ref[...].astype(f32))
        dg_ref[...] += dg
        dzxd_ref[:, :SSM_WIDTH] = dz.astype(bf16)

        lane = lax.broadcasted_iota(jnp.int32, (1, LANES), 1)
        sub = lax.broadcasted_iota(jnp.int32, (LANES, 1), 0)
        dacs = jnp.zeros((CHUNK, LANES), f32)
        dacs_r = jnp.zeros((LANES, CHUNK), f32)
        dtot = jnp.zeros((1, LANES), f32)
        dcb = [jnp.zeros((CHUNK, CHUNK), f32) for _ in range(SSM_GROUPS)]
        dcm = [jnp.zeros((CHUNK, N), f32) for _ in range(SSM_GROUPS)]
        dbm = [jnp.zeros((CHUNK, N), f32) for _ in range(SSM_GROUPS)]
        dxds, dgms = [], []
        for j in range(PAIRS):
            g = j // PG
            cmb, bmb, _ = groups[g]
            ps = slice(j * PAIR_W, (j + 1) * PAIR_W)
            (dk0, m0), (dk1, m1) = pairs[j]
            oh0, oh1 = (lane == 2 * j).astype(f32), (lane == 2 * j + 1).astype(f32)
            dyp = dy[:, ps]
            dy2 = _pair_stack(dyp, lo).astype(bf16)
            dm2 = _dot_nt(dy2, xd[:, ps].astype(bf16))
            m2 = jnp.concatenate([m0, m1], axis=0)
            dxds.append(_dot_tn(m2.astype(bf16), dy2))
            w2 = dm2 * m2
            rs = jnp.sum(w2, axis=1, keepdims=True)
            dacs = dacs + rs[:CHUNK] * oh0 + rs[CHUNK:] * oh1
            dacs_r = dacs_r - ((sub == 2 * j).astype(f32) * jnp.sum(w2[:CHUNK], axis=0, keepdims=True)
                               + (sub == 2 * j + 1).astype(f32) * jnp.sum(w2[CHUNK:], axis=0, keepdims=True))
            dcb[g] = dcb[g] + dm2[:CHUNK] * dk0 + dm2[CHUNK:] * dk1
            sp = sprev_k[j]
            dzb = (dyp * e_x[:, ps]).astype(bf16)
            dcm[g] = dcm[g] + _dot(dzb, sp.astype(bf16))
            dsn = ds_sc[j]
            dsnb = dsn.astype(bf16)
            et = _pair_col(row_lo, tot, j)
            rr = jnp.sum(dsn * sp, axis=1, keepdims=True) * et
            dtot = dtot + jnp.sum(rr[:SSM_HEAD_DIM]) * oh0 + jnp.sum(rr[SSM_HEAD_DIM:]) * oh1
            dgms.append(_dot_nt(bmb, dsnb))
            dbm[g] = dbm[g] + _dot(gm[:, ps].astype(bf16), dsnb)
            ds_sc[j] = _dot_tn(dzb, cmb) + et * dsn
        dgm = jnp.concatenate(dgms, axis=1)
        dxd = jnp.concatenate(dxds, axis=1) + dgm * r_x
        dr = dgm * gm
        red = _dot_sel(jnp.concatenate([dy * e_x * zf - dr, dr, dxd * xs, dy * xs], axis=0), rd_ref[...], REDUCE_SPLIT)
        rowi = lax.broadcasted_iota(jnp.int32, (CHUNK, 1), 0)
        dtot = dtot + jnp.sum(red[CHUNK:2 * CHUNK], axis=0, keepdims=True)
        dacs = dacs + red[:CHUNK] + dacs_r.T + jnp.where(rowi == CHUNK - 1, dtot, 0.0)
        r2 = lax.broadcasted_iota(jnp.int32, (CHUNK, CHUNK), 0)
        c2 = lax.broadcasted_iota(jnp.int32, (CHUNK, CHUNK), 1)
        dadt = jnp.dot((c2 >= r2).astype(f32), dacs, preferred_element_type=f32, precision=HIGHEST)
        ddt = red[2 * CHUNK:3 * CHUNK] + dadt * a
        dalog_ref[...] += jnp.sum(dadt * dt, axis=0, keepdims=True) * a
        ddsk_ref[...] += jnp.sum(red[3 * CHUNK:], axis=0, keepdims=True)
        ddtr = jnp.where(lane < H, ddt * _sigmoid(dtr_ref[...] + dtb_ref[...]), 0.0)
        ddtb_ref[...] += jnp.sum(ddtr, axis=0, keepdims=True)
        dzxd_ref[:, SSM_WIDTH + CONV_DIM:] = ddtr.astype(bf16)
        dxa_bm, dxa_cm = [], []
        for g in range(SSM_GROUPS):
            cmb, bmb, _ = groups[g]
            dcbb = dcb[g].astype(bf16)
            dxa_bm.append(dbm[g] + _dot_tn(dcbb, cmb))
            dxa_cm.append(dcm[g] + _dot(dcbb, bmb))
        dxc = jnp.concatenate([dy * dsk_x + dxd * dt_x] + dxa_bm + dxa_cm, axis=1) * (sg * (1.0 + xc * (1.0 - sg)))
        ext = jnp.concatenate([dxc, next_sc[...]], axis=0)
        xin = xbc_ref[...].astype(f32)
        dxbc = cw_ref[SSM_CONV - 1:SSM_CONV, :] * dxc
        dcw = [jnp.sum(dxc * xin, axis=0, keepdims=True)]
        for s in range(1, SSM_CONV):
            later = _sel_dot(_shift_mat(CHUNK, CHUNK + HALO, s), ext, 2)
            dxbc = dxbc + cw_ref[SSM_CONV - 1 - s:SSM_CONV - s, :] * later
            dcw.insert(0, jnp.sum(later * xin, axis=0, keepdims=True))
        dzxd_ref[:, SSM_WIDTH:SSM_WIDTH + CONV_DIM] = dxbc.astype(bf16)
        dcw_ref[...] += jnp.concatenate(dcw, axis=0)
        dcb_ref[...] += jnp.sum(dxc, axis=0, keepdims=True)
        next_sc[...] = dxc[0:HALO, :]

    SUB = SSD_SUB if nc % SSD_SUB == 0 else 1
    nb = nc // SUB

    def body(xbc_ref, xc_ref, sg_ref, y_ref, z_ref, dtr_ref, sprev_ref, dyb_ref, cw_ref, cb_ref, dtb_ref, alog_ref, dsk_ref,
             g_ref, ex_ref, rd_ref, dzxd_ref, dcw_ref, dcb_ref, ddtb_ref, dalog_ref, ddsk_ref, dg_ref, ds_sc, next_sc):
        @pl.when(pl.program_id(0) == 0)
        def _():
            ds_sc[...] = jnp.zeros_like(ds_sc)
            next_sc[...] = jnp.zeros_like(next_sc)
            for r_ in (dcw_ref, dcb_ref, ddtb_ref, dalog_ref, ddsk_ref, dg_ref):
                r_[...] = jnp.zeros_like(r_)

        for k in reversed(range(SUB)):
            rows = slice(k * CHUNK, (k + 1) * CHUNK)
            tok = [_RowsOf(r_, rows) for r_ in (xbc_ref, xc_ref, sg_ref, y_ref, z_ref, dtr_ref)]
            chunk(*tok, sprev_ref.at[k], _RowsOf(dyb_ref, rows), cw_ref, cb_ref, dtb_ref, alog_ref, dsk_ref, g_ref, ex_ref,
                  rd_ref, _RowsOf(dzxd_ref, rows), dcw_ref, dcb_ref, ddtb_ref, dalog_ref, ddsk_ref, dg_ref, ds_sc, next_sc)

    params = [conv_w, conv_b, dt_bias, a_log, d_skip, ssm_norm]
    mats = list(_head_mats())

    def rev(ncols):
        return pl.BlockSpec((SUB * CHUNK, ncols), lambda i: (nb - 1 - i, 0))

    in_specs = ([rev(CONV_DIM), rev(CONV_DIM), rev(CONV_DIM), rev(SSM_WIDTH), rev(SSM_WIDTH), rev(LANES),
                 pl.BlockSpec((SUB, PAIRS, PAIR_W, N), lambda i: (nb - 1 - i, 0, 0, 0)), rev(SSM_WIDTH)]
                + [_full(x.shape) for x in params + mats])
    return pl.pallas_call(
        body, name="ssd_bwd", grid=(nb,), in_specs=in_specs,
        out_specs=[rev(ZXD)] + [_acc(x.shape) for x in params],
        out_shape=[jax.ShapeDtypeStruct((T, ZXD), bf16)] + [jax.ShapeDtypeStruct(x.shape, f32) for x in params],
        scratch_shapes=[pltpu.VMEM((PAIRS, PAIR_W, N), f32), pltpu.VMEM((HALO, CONV_DIM), f32)],
        compiler_params=_params(("arbitrary",)))(xbc, xc, sg, y, z, dtr, sprev, dyb, *params, *mats)


def _tail(h3, p, tgt, gp, wpg, bpg, wpp, gf, tm=512):
    T, D = h3.shape
    tm = min(tm, T)

    def head(gpre, pp, h, gf_, t):
        gate = _sigmoid(gpre)
        y = _rms(h + gate * pp, gf_)
        err = y - t
        return 0.5 * jnp.sum(jnp.mean(err * err, axis=-1))

    def body(h_ref, p_ref, t_ref, gp_ref, wpg_ref, bpg_ref, wpp_ref, gf_ref,
             dh_ref, loss_ref, dgp_ref, dwpg_ref, dbpg_ref, dwpp_ref, dgf_ref):
        @pl.when(pl.program_id(0) == 0)
        def _():
            for r in (loss_ref, dgp_ref, dwpg_ref, dbpg_ref, dwpp_ref, dgf_ref):
                r[...] = jnp.zeros_like(r)

        h = h_ref[...]
        npf, np_vjp = jax.vjp(_rms, h, gp_ref[...])
        npb = npf.astype(bf16)
        pb = p_ref[...].astype(bf16)
        gpre = _dot(npb, wpg_ref[...]) + bpg_ref[...]
        kp, _, cp = wpp_ref.shape
        pp = jnp.concatenate([_dot(pb, wpp_ref[k]) for k in range(kp)], axis=1)
        loss, head_vjp = jax.vjp(head, gpre, pp, h, gf_ref[...], t_ref[...])
        dgpre, dpp, dh_a, dgf, _ = head_vjp(jnp.ones((), f32))
        loss_ref[...] += loss
        dgf_ref[...] += dgf
        dbpg_ref[...] += jnp.sum(dgpre, axis=0, keepdims=True)
        dgb = dgpre.astype(bf16)
        dwpg_ref[...] += _dot_tn(npb, dgb)
        dppb = dpp.astype(bf16)
        for k in range(kp):
            dwpp_ref[k] += _dot_tn(pb, dppb[:, k * cp:(k + 1) * cp])
        dh_b, dgp = np_vjp(_dot_nt(dgb, wpg_ref[...]))
        dgp_ref[...] += dgp
        dh_ref[...] = dh_a + dh_b

    ins = [h3, p, tgt, gp, wpg, bpg, wpp, gf]
    in_specs = [_rows(tm, D), _rows(tm, p.shape[1]), _rows(tm, D)] + [_full(x.shape) for x in ins[3:]]
    acc_shapes = [(1, LANES), gp.shape, wpg.shape, bpg.shape, wpp.shape, gf.shape]
    return pl.pallas_call(
        body, name="tail", grid=(T // tm,), in_specs=in_specs,
        out_specs=[_rows(tm, D)] + [_acc(s) for s in acc_shapes],
        out_shape=[jax.ShapeDtypeStruct((T, D), f32)] + [jax.ShapeDtypeStruct(s, f32) for s in acc_shapes],
        compiler_params=_params(("arbitrary",)))(*ins)


def _adamw(name, w, g, m, v, tr=256):
    R, rest = w.shape[0], w.shape[1:]
    tr = _row_tile(R, tr, 8 if len(rest) == 1 else 1)

    def body(w_ref, g_ref, m_ref, v_ref, d_ref, mo_ref, vo_ref):
        g_ = g_ref[...]
        m_ = ADAM_B1 * m_ref[...] + (1.0 - ADAM_B1) * g_
        v_ = ADAM_B2 * v_ref[...] + (1.0 - ADAM_B2) * jnp.square(g_)
        m_hat = m_ / (1.0 - ADAM_B1 ** ADAM_STEP)
        v_hat = v_ / (1.0 - ADAM_B2 ** ADAM_STEP)
        d_ref[...] = -ADAM_LR * (m_hat / (jnp.sqrt(v_hat) + ADAM_EPS) + ADAM_WD * w_ref[...])
        mo_ref[...] = m_
        vo_ref[...] = v_

    spec = pl.BlockSpec((tr,) + rest, lambda i: (i,) + (0,) * len(rest))
    return pl.pallas_call(body, name=name, grid=(R // tr,), in_specs=[spec] * 4, out_specs=[spec] * 3,
                          out_shape=[jax.ShapeDtypeStruct(w.shape, f32)] * 3,
                          compiler_params=_params(("parallel",)))(w, g, m, v)


HBM = pl.BlockSpec(memory_space=pltpu.HBM)


def _me():
    return lax.axis_index("x"), lax.axis_index("y"), lax.axis_index("c")


def _other_chips(x, y):
    return [(1 - x, y), (x, 1 - y), (1 - x, 1 - y)]


def _remote(src, dst, send_sem, recv_sem, dev):
    return pltpu.make_async_remote_copy(src_ref=src, dst_ref=dst, send_sem=send_sem, recv_sem=recv_sem,
                                        device_id=dev, device_id_type=MESH)


def _sems(n):
    return [pltpu.SemaphoreType.DMA((n,)), pltpu.SemaphoreType.DMA((n,))]


def _gather_weights(shards, split):
    n = len(shards)

    def body(*refs):
        ins, outs = refs[:n], refs[n:2 * n]
        own_send, own_recv, ici_send, ici_recv, d2d_send, d2d_recv = refs[2 * n:]
        x, y, c = _me()
        my_chip = 2 * x + y
        sibling = (x, y, 1 - c)
        chips = _other_chips(x, y)

        def rows(i, half):
            hr = shards[i].shape[0] // 2
            return pl.ds(half * hr, hr) if split[i] else pl.ds(0, shards[i].shape[0])

        sends = []
        for i in range(n):
            for j, chip in enumerate(chips):
                cp = _remote(ins[i].at[rows(i, c)], outs[i].at[my_chip, rows(i, c)],
                             ici_send.at[3 * i + j], ici_recv.at[3 * i + j], (*chip, c))
                cp.start()
                sends.append(cp)
            cp = _remote(ins[i], outs[i].at[my_chip], own_send.at[i], own_recv.at[i], sibling)
            cp.start()
            sends.append(cp)
        for i in range(n):
            for j, chip in enumerate(chips):
                s = 3 * i + j
                land = outs[i].at[2 * chip[0] + chip[1], rows(i, c)]
                _remote(land, land, ici_send.at[s], ici_recv.at[s], (*chip, c)).wait_recv()
                if split[i]:
                    cp = _remote(land, land, d2d_send.at[s], d2d_recv.at[s], sibling)
                    cp.start()
                    sends.append(cp)
        for i in range(n):
            _remote(ins[i], outs[i].at[my_chip], own_send.at[i], own_recv.at[i], sibling).wait_recv()
            if split[i]:
                for j, chip in enumerate(chips):
                    s = 3 * i + j
                    land = outs[i].at[2 * chip[0] + chip[1], rows(i, 1 - c)]
                    _remote(land, land, d2d_send.at[s], d2d_recv.at[s], sibling).wait_recv()
        for cp in sends:
            cp.wait_send()

    return pl.pallas_call(
        body, name="gather_weights", out_shape=[jax.ShapeDtypeStruct((N_CHIPS,) + s.shape, s.dtype) for s in shards],
        in_specs=[HBM] * n, out_specs=[HBM] * n,
        scratch_shapes=_sems(n) + _sems(3 * n) + _sems(3 * n))(*shards)


def _swap_halves(name, grads):
    n = len(grads)

    def body(*refs):
        ins, outs, send, recv = refs[:n], refs[n:2 * n], refs[2 * n], refs[2 * n + 1]
        x, y, c = _me()
        copies = []
        for i in range(n):
            hr = grads[i].shape[1] // 2
            cp = _remote(ins[i].at[:, pl.ds((1 - c) * hr, hr), :], outs[i], send.at[i], recv.at[i], (x, y, 1 - c))
            cp.start()
            copies.append(cp)
        for cp in copies:
            cp.wait()

    return pl.pallas_call(
        body, name=name,
        out_shape=[jax.ShapeDtypeStruct((g.shape[0], g.shape[1] // 2, g.shape[2]), g.dtype) for g in grads],
        in_specs=[HBM] * n, out_specs=[HBM] * n, scratch_shapes=_sems(n))(*grads)


def _add_halves(name, grads, other, c_idx, th=HALF_ROWS_BF16):
    K, R, C = grads.shape
    H = R // 2
    th = _row_tile(H, th, 16)
    nb = H // th

    def body(c_ref, g_ref, o_ref, out_ref):
        out_ref[...] = (g_ref[...].astype(f32) + o_ref[...].astype(f32)).astype(bf16)

    grid_spec = pltpu.PrefetchScalarGridSpec(
        num_scalar_prefetch=1, grid=(nb,),
        in_specs=[pl.BlockSpec((K, th, C), lambda i, c: (0, c[0] * nb + i, 0)),
                  pl.BlockSpec((K, th, C), lambda i, c: (0, i, 0))],
        out_specs=pl.BlockSpec((K, th, C), lambda i, c: (0, i, 0)))
    return pl.pallas_call(body, name=name, grid_spec=grid_spec,
                          out_shape=jax.ShapeDtypeStruct((K, H, C), bf16),
                          compiler_params=_params(("parallel",)))(c_idx, grads, other)


SEM = pl.BlockSpec(memory_space=pltpu.SEMAPHORE)
ANY = pl.BlockSpec(memory_space=pl.ANY)
EFFECT = pltpu.SideEffectType.DATAFLOW_SIDE_EFFECTING


def _copies_start(name, srcs, land_shapes, n_copies, make_copies, after):
    ns, nl = len(srcs), len(land_shapes)
    lands = [lax.empty(s.shape, s.dtype) for s in land_shapes]

    def body(*refs):
        src_refs, land_refs = refs[:ns], refs[ns:ns + nl]
        send, recv, token = refs[ns + nl + 1], refs[ns + nl + 2], refs[-1]
        for cp in make_copies(src_refs, land_refs, send, recv):
            cp.start()
        token[...] = jnp.zeros_like(token)

    buffers = list(srcs) + lands
    out = pl.pallas_call(
        body, name=name,
        out_shape=(pltpu.SemaphoreType.DMA((n_copies,)), pltpu.SemaphoreType.DMA((n_copies,)),
                   *[pltpu.HBM(b.shape, b.dtype) for b in buffers], jax.ShapeDtypeStruct((8, LANES), f32)),
        in_specs=[HBM] * (ns + nl) + [ANY],
        out_specs=(SEM, SEM, *[HBM] * (ns + nl), pl.BlockSpec(memory_space=pltpu.VMEM)),
        input_output_aliases={i: 2 + i for i in range(ns + nl)},
        compiler_params=pltpu.CompilerParams(has_side_effects=EFFECT),
    )(*[pltpu.with_memory_space_constraint(b, pltpu.HBM) for b in buffers], after)
    return out[0], out[1], list(out[2:2 + ns]), list(out[2 + ns:2 + ns + nl]), out[-1]


def _copies_wait(name, started, make_copies, after):
    send, recv, srcs, lands, _ = started
    ns, nl = len(srcs), len(lands)
    after = list(after)

    def body(*refs):
        src_refs, land_refs = refs[:ns], refs[ns:ns + nl]
        for cp in make_copies(src_refs, land_refs, refs[ns + nl], refs[ns + nl + 1]):
            cp.wait_send()
            cp.wait_recv()

    buffers = list(srcs) + list(lands)
    out = pl.pallas_call(
        body, name=name, out_shape=tuple(pltpu.HBM(b.shape, b.dtype) for b in buffers),
        in_specs=[HBM] * (ns + nl) + [SEM, SEM] + [ANY] * len(after), out_specs=tuple([HBM] * (ns + nl)),
        input_output_aliases={i: i for i in range(ns + nl)},
        compiler_params=pltpu.CompilerParams(has_side_effects=EFFECT),
    )(*buffers, send, recv, *after)
    return list(out[:ns]), list(out[ns:])


def _gather_copies(src_refs, land_refs, send, recv):
    x, y, c = _me()
    my_chip = 2 * x + y
    peers = [(*chip, c) for chip in _other_chips(x, y)] + [(x, y, 1 - c)]
    return [_remote(src_refs[i], land_refs[i].at[my_chip], send.at[4 * i + j], recv.at[4 * i + j], peer)
            for i in range(len(src_refs)) for j, peer in enumerate(peers)]


def _partial_copies(src_refs, land_refs, send, recv):
    x, y, c = _me()
    return [_remote(src_refs[i].at[2 * chip[0] + chip[1]], land_refs[i].at[j], send.at[3 * i + j], recv.at[3 * i + j], (*chip, c))
            for i in range(len(src_refs)) for j, chip in enumerate(_other_chips(x, y))]


def _small_copies(src_refs, land_refs, send, recv):
    x, y, c = _me()
    return [_remote(src_refs[0], land_refs[0].at[k - 1], send.at[k - 1], recv.at[k - 1], (x ^ (k >> 2), y ^ ((k >> 1) & 1), c ^ (k & 1)))
            for k in range(1, N_DEV)]


def _sum_small(own, slots, dev_idx):
    R, C = own.shape

    def body(dev_ref, own_ref, s_ref, o_ref):
        me = dev_ref[0]
        acc = jnp.zeros((R, C), f32)
        for d in range(N_DEV):
            k = me ^ d
            acc = acc + jnp.where(k == 0, own_ref[...], s_ref[jnp.maximum(k - 1, 0)])
        o_ref[...] = acc

    grid_spec = pltpu.PrefetchScalarGridSpec(
        num_scalar_prefetch=1, grid=(1,),
        in_specs=[pl.BlockSpec((R, C), lambda i, dev: (0, 0)), pl.BlockSpec((N_DEV - 1, R, C), lambda i, dev: (0, 0, 0))],
        out_specs=pl.BlockSpec((R, C), lambda i, dev: (0, 0)))
    return pl.pallas_call(body, name="sum_small", grid_spec=grid_spec, out_shape=jax.ShapeDtypeStruct((R, C), f32),
                          compiler_params=_params(("arbitrary",)))(dev_idx, own, slots)


def _sum_partials(name, part, recv, chip_idx, th=HALF_ROWS_BF16):
    K, H, C = part.shape
    th = _row_tile(H, th, 16)

    def body(chip_ref, p_ref, r_ref, o_ref):
        acc = p_ref[...].astype(f32)
        for j in range(3):
            acc = acc + r_ref[j].astype(f32)
        o_ref[...] = acc

    grid_spec = pltpu.PrefetchScalarGridSpec(
        num_scalar_prefetch=1, grid=(H // th,),
        in_specs=[pl.BlockSpec((None, th, C), lambda i, chip: (chip[0], i, 0)),
                  pl.BlockSpec((3, th, C), lambda i, chip: (0, i, 0))],
        out_specs=pl.BlockSpec((th, C), lambda i, chip: (i, 0)))
    return pl.pallas_call(body, name=name, grid_spec=grid_spec, out_shape=jax.ShapeDtypeStruct((H, C), f32),
                          compiler_params=_params(("parallel",)))(chip_idx, part, recv)


def _share_halves(name, halves):
    n = len(halves)

    def body(*refs):
        ins, outs, send, recv = refs[:n], refs[n:2 * n], refs[2 * n], refs[2 * n + 1]
        x, y, c = _me()
        copies = []
        for i in range(n):
            cp = _remote(ins[i], outs[i], send.at[i], recv.at[i], (x, y, 1 - c))
            cp.start()
            copies.append(cp)
        for cp in copies:
            cp.wait()

    return pl.pallas_call(
        body, name=name, out_shape=[jax.ShapeDtypeStruct(h.shape, h.dtype) for h in halves],
        in_specs=[HBM] * n, out_specs=[HBM] * n, scratch_shapes=_sems(n))(*halves)


def _adamw_big(name, w, g_mine, g_theirs, m, v, c_idx, tr=HALF_ROWS_F32):
    R, C = w.shape
    H = R // 2
    tr = _row_tile(H, tr)
    nb = H // tr

    def body(c_ref, w_ref, gm_ref, gt_ref, m_ref, v_ref, g_ref, d_ref, mo_ref, vo_ref):
        g_ = jnp.where(pl.program_id(0) // nb == c_ref[0], gm_ref[...], gt_ref[...])
        g_ref[...] = g_
        m_ = ADAM_B1 * m_ref[...] + (1.0 - ADAM_B1) * g_
        v_ = ADAM_B2 * v_ref[...] + (1.0 - ADAM_B2) * jnp.square(g_)
        m_hat = m_ / (1.0 - ADAM_B1 ** ADAM_STEP)
        v_hat = v_ / (1.0 - ADAM_B2 ** ADAM_STEP)
        d_ref[...] = -ADAM_LR * (m_hat / (jnp.sqrt(v_hat) + ADAM_EPS) + ADAM_WD * w_ref[...])
        mo_ref[...] = m_
        vo_ref[...] = v_

    full = pl.BlockSpec((tr, C), lambda i, c: (i, 0))
    half = pl.BlockSpec((tr, C), lambda i, c: (i % nb, 0))
    grid_spec = pltpu.PrefetchScalarGridSpec(num_scalar_prefetch=1, grid=(2 * nb,),
                                             in_specs=[full, half, half, full, full], out_specs=[full] * 4)
    return pl.pallas_call(body, name=name, grid_spec=grid_spec, out_shape=[jax.ShapeDtypeStruct((R, C), f32)] * 4,
                          compiler_params=_params(("parallel",)))(c_idx, w, g_mine, g_theirs, m, v)


BIG = ("ffn1_w_gate", "ffn1_w_up", "ffn1_w_down", "w_in", "w_out", "ffn2_w_gate", "ffn2_w_up", "ffn2_w_down",
       "ple_w_gate", "ple_w_proj")


SMALL = ("ffn1_norm", "mix_norm", "gm_ln_g", "gm_ln_b", "gm_w_s", "gm_b_s", "gm_out_norm", "conv_b", "dt_bias", "a_log",
         "d_skip", "ssm_norm", "ffn2_norm", "ple_norm", "ple_b_gate", "final_norm")
SMALL_C = 1024


def _pack_small(vals):
    parts = []
    for v in vals:
        f = v.astype(f32).reshape(-1)
        parts.append(jnp.pad(f, (0, -f.shape[0] % SMALL_C)))
    flat = jnp.concatenate(parts)
    rows = flat.shape[0] // SMALL_C
    return jnp.pad(flat, (0, (-rows % 8) * SMALL_C)).reshape(-1, SMALL_C)


def _unpack_small(pack, shapes):
    flat = pack.reshape(-1)
    out, off = [], 0
    for s in shapes:
        n = 1
        for d in s:
            n *= d
        out.append(flat[off:off + n].reshape(s))
        off += n + (-n % SMALL_C)
    return out


def _pad_lanes(v):
    return jnp.pad(v, ((0, 0), (0, LANES - v.shape[1])))


def _pad_rows(a):
    pad = [(0, 0)] * a.ndim
    pad[-2] = (0, -a.shape[-2] % ROW_PAD)
    return jnp.pad(a, pad) if pad[-2][1] else a


FETCH = (("ffn1_w_gate", "ffn1_w_up", "ffn1_w_down"), ("w_in", "conv_w", "w_out"),
         ("ffn2_w_gate", "ffn2_w_up", "ffn2_w_down", "ple_w_gate", "ple_w_proj"))
TRANSPOSED = ("ffn1_w_gate", "ffn1_w_up", "ffn2_w_gate", "ffn2_w_up", "w_in")
ROW_PAD = 32
DONE = (("ffn2_w_gate", "ffn2_w_up", "ffn2_w_down", "w_out", "ple_w_gate", "ple_w_proj"), ("w_in",),
        ("ffn1_w_gate", "ffn1_w_up", "ffn1_w_down"))


def _local_step(x, p, tgt, fetch, S, on_grads):
    G = GM_WIDTH
    K = N_CHIPS
    b_st = S["gm_b_s"][0].T
    w_s = S["gm_w_s"][0]
    dtb, alog, dsk = _pad_lanes(S["dt_bias"]), _pad_lanes(S["a_log"]), _pad_lanes(S["d_skip"])
    gfin = S["final_norm"].reshape(1, -1)

    def rows(a):
        return a.reshape(-1, D_MODEL)

    def shards(a):
        return a.reshape(K, -1, D_MODEL)

    wg1, wu1, wd1 = [rows(a) for a in fetch(0, None)]
    h1, n1, a1, b1 = _ffn_fwd("ffn1_fwd", x, S["ffn1_norm"], wg1, wu1, wd1)
    w_in4, cw4, wo4 = fetch(1, h1)
    w_in = w_in4.reshape(IN_PROJ, D_MODEL)
    w_uv = w_in[:2 * G]
    w_zxd = jnp.pad(w_in[2 * G:], ((0, ZXD - (IN_PROJ - 2 * G)), (0, 0)))
    conv_w = jnp.transpose(cw4, (1, 0, 2)).reshape(SSM_CONV, CONV_DIM)
    wo = wo4.reshape(-1, D_MODEL)
    n2, act, slope, z, xbc, dtr, ya = _mix_fwd(h1, S["mix_norm"], w_uv, w_zxd, S["gm_ln_g"], S["gm_ln_b"], w_s, b_st,
                                               S["gm_out_norm"])
    yb, xc, sg, y_ssd, sprev = _ssd_fwd(xbc, z, dtr, conv_w, S["conv_b"], dtb, alog, dsk, S["ssm_norm"])
    wg2, wu2, wd2, wpg4, wpp4 = fetch(2, yb)
    wg2, wu2, wd2 = rows(wg2), rows(wu2), rows(wd2)
    h2, h3, n3, a2, b2 = _ffn_fwd("ffn2_fwd", h1, S["ffn2_norm"], wg2, wu2, wd2, pre=(ya, yb, wo))
    dh3, loss, dgp, dwpg, dbpg, dwpp, dgf = _tail(h3, p, tgt, S["ple_norm"], wpg4.reshape(-1, D_MODEL), S["ple_b_gate"], wpp4, gfin)
    dh2, da2, db2, hm2, dg_ffn2, dya, dyb = _ffn_bwd("ffn2_bwd", dh3, h2, S["ffn2_norm"], a2, b2, wg2, wu2, wd2, wo=wo, ga=G)
    dw_out = jnp.concatenate([_matmul_tn("dw_out_a", ya, dh2), _matmul_tn("dw_out_b", yb, dh2)], axis=0).reshape(wo4.shape)
    zero = on_grads(0, [shards(_matmul_tn("dw_ffn2_gate", da2, n3)), shards(_matmul_tn("dw_ffn2_up", db2, n3)),
                        shards(_matmul_tn("dw_ffn2_down", hm2, dh3, scale=0.5)), dw_out,
                        dwpg.astype(bf16).reshape(wpg4.shape), dwpp.astype(bf16)])
    dzxd, dcw, dcb, ddtb, dalog, ddsk, dgssm = _ssd_bwd(xbc, xc, sg, y_ssd, z, dtr, sprev, dyb, conv_w, S["conv_b"], dtb, alog, dsk,
                                                        S["ssm_norm"] + zero)
    dh1, duv, dg_mix, dlng, dlnb, dws, dbst, dgout = _mix_bwd(dh2, h1, S["mix_norm"], act, slope, dya, dzxd, w_uv, w_zxd, S["gm_ln_g"],
                                                              S["gm_ln_b"], w_s, b_st, S["gm_out_norm"])
    dw_in = jnp.concatenate([_matmul_tn("dw_in_uv", duv, n2), _matmul_tn("dw_in_zxd", dzxd, n2)[:IN_PROJ - 2 * G]], axis=0)
    zero = on_grads(1, [dw_in.reshape(w_in4.shape)])
    dx, da1, db1, hm1, dg_ffn1 = _ffn_bwd("ffn1_bwd", dh1, x, S["ffn1_norm"] + zero, a1, b1, wg1, wu1, wd1)
    zero = on_grads(2, [shards(_matmul_tn("dw_ffn1_gate", da1, n1)), shards(_matmul_tn("dw_ffn1_up", db1, n1)),
                        shards(_matmul_tn("dw_ffn1_down", hm1, dh1, scale=0.5))])
    loss = loss + zero
    nh = SSM_HEADS
    gS = {"ffn1_norm": dg_ffn1, "mix_norm": dg_mix, "gm_ln_g": dlng, "gm_ln_b": dlnb, "gm_w_s": dws[None], "gm_b_s": dbst.T[None],
          "gm_out_norm": dgout, "conv_b": dcb, "dt_bias": ddtb[:, :nh], "a_log": dalog[:, :nh], "d_skip": ddsk[:, :nh],
          "ssm_norm": dgssm, "ffn2_norm": dg_ffn2, "ple_norm": dgp, "ple_b_gate": dbpg, "final_norm": dgf.reshape(-1)}
    return loss, dx, dcw, gS


_WEIGHTS = ("ffn1_norm", "ffn1_w_gate", "ffn1_w_up", "ffn1_w_down", "mix_norm", "w_in", "gm_ln_g", "gm_ln_b", "gm_w_s", "gm_b_s",
            "gm_out_norm", "conv_w", "conv_b", "dt_bias", "a_log", "d_skip", "ssm_norm", "w_out", "ffn2_norm", "ffn2_w_gate",
            "ffn2_w_up", "ffn2_w_down", "ple_norm", "ple_w_gate", "ple_b_gate", "ple_w_proj", "final_norm")
_BIG_NAMES = BIG


def kernel(x, p, ffn1_norm, ffn1_w_gate, ffn1_w_up, ffn1_w_down, mix_norm, w_in, gm_ln_g, gm_ln_b, gm_w_s, gm_b_s, gm_out_norm, conv_w, conv_b, dt_bias, a_log, d_skip, ssm_norm, w_out, ffn2_norm, ffn2_w_gate, ffn2_w_up, ffn2_w_down, ple_norm, ple_w_gate, ple_b_gate, ple_w_proj, final_norm, loss_target, m_ffn1_norm, m_ffn1_w_gate, m_ffn1_w_up, m_ffn1_w_down, m_mix_norm, m_w_in, m_gm_ln_g, m_gm_ln_b, m_gm_w_s, m_gm_b_s, m_gm_out_norm, m_conv_w, m_conv_b, m_dt_bias, m_a_log, m_d_skip, m_ssm_norm, m_w_out, m_ffn2_norm, m_ffn2_w_gate, m_ffn2_w_up, m_ffn2_w_down, m_ple_norm, m_ple_w_gate, m_ple_b_gate, m_ple_w_proj, m_final_norm, v_ffn1_norm, v_ffn1_w_gate, v_ffn1_w_up, v_ffn1_w_down, v_mix_norm, v_w_in, v_gm_ln_g, v_gm_ln_b, v_gm_w_s, v_gm_b_s, v_gm_out_norm, v_conv_w, v_conv_b, v_dt_bias, v_a_log, v_d_skip, v_ssm_norm, v_w_out, v_ffn2_norm, v_ffn2_w_gate, v_ffn2_w_up, v_ffn2_w_down, v_ple_norm, v_ple_w_gate, v_ple_b_gate, v_ple_w_proj, v_final_norm):
    given = dict(locals())
    w = {n: given[n] for n in _WEIGHTS}
    m = {n: given["m_" + n] for n in _WEIGHTS}
    v = {n: given["v_" + n] for n in _WEIGHTS}

    c_idx = lax.axis_index("c").astype(jnp.int32).reshape(1)
    chip = 2 * lax.axis_index("x") + lax.axis_index("y")
    chip_idx = chip.astype(jnp.int32).reshape(1)

    shard = {n: (jnp.swapaxes(w[n][0], 0, 1) if n in TRANSPOSED else w[n][0]).astype(bf16) for n in BIG}
    shard["conv_w"] = w["conv_w"][0]
    first = _gather_weights([shard[n] for n in FETCH[0]], [True] * len(FETCH[0]))
    fetching, after = [], first[-1]
    for k in (1, 2):
        srcs = [shard[n] for n in FETCH[k]]
        lands = [jax.ShapeDtypeStruct((N_CHIPS,) + s.shape, s.dtype) for s in srcs]
        fetching.append(_copies_start("gather%d_start" % k, srcs, lands, 4 * len(srcs), _gather_copies, after))
        after = fetching[-1][4]

    def fetch(k, after_):
        return first if k == 0 else _copies_wait("gather%d_wait" % k, fetching[k - 1], _gather_copies, [after_])[1]

    exchanging = []

    def on_grads(k, grads):
        grads = [_pad_rows(g_) for g_ in grads]
        others = _swap_halves("swap%d" % k, grads)
        parts = [_add_halves("add_" + n, g_, o_, c_idx) for n, g_, o_ in zip(DONE[k], grads, others)]
        lands = [jax.ShapeDtypeStruct((3,) + p_.shape[1:], p_.dtype) for p_ in parts]
        exchanging.append(_copies_start("exchange%d_start" % k, parts, lands, 3 * len(parts), _partial_copies, c_idx))
        return exchanging[-1][4][0, 0]

    S = {n: w[n] for n in SMALL}
    S["ffn1_norm"] = S["ffn1_norm"] + after[0, 0]
    loss, dx, dcw, gS = _local_step(x[0], p[0, 0], loss_target[0], fetch, S, on_grads)

    small = _pack_small([gS[n] for n in SMALL] + [dcw, loss[:, :1]])
    small_lands = [jax.ShapeDtypeStruct((N_DEV - 1,) + small.shape, small.dtype)]
    small_st = _copies_start("small_start", [small], small_lands, N_DEV - 1, _small_copies, c_idx)

    g, delta, new_m, new_v = {}, {}, {}, {}
    after = [small_st[4]]
    for k in range(len(DONE)):
        parts, recv = _copies_wait("exchange%d_wait" % k, exchanging[k], _partial_copies, after)
        mine = [_sum_partials("sum_" + n, p_, r_, chip_idx) for n, p_, r_ in zip(DONE[k], parts, recv)]
        theirs = _share_halves("share%d" % k, mine)
        after = []
        for n, gm_, gt_ in zip(DONE[k], mine, theirs):
            flip = (lambda a: jnp.swapaxes(a, 0, 1)) if n in TRANSPOSED else (lambda a: a)
            rows = flip(w[n][0]).shape[0]
            if rows % ROW_PAD:
                def lin(a):
                    return jnp.transpose(a.reshape(-1, LANES, rows), (2, 0, 1))

                def back(a):
                    return jnp.transpose(a, (1, 2, 0)).reshape(1, -1, rows)

                gm_, gt_ = [a.reshape(a.shape[0], -1, LANES) for a in (gm_, gt_)]
                g_ = jnp.where(c_idx[0] == 0, jnp.concatenate([gm_, gt_]), jnp.concatenate([gt_, gm_]))[:rows]
                outs = [g_, *_adamw("adamw_" + n, lin(w[n]), g_, lin(m[n]), lin(v[n]), tr=HALF_ROWS_F32)]
                g[n], delta[n], new_m[n], new_v[n] = [back(o) for o in outs]
            else:
                outs = _adamw_big("adamw_" + n, flip(w[n][0]), gm_, gt_, flip(m[n][0]), flip(v[n][0]), c_idx)
                g[n], delta[n], new_m[n], new_v[n] = [flip(o)[None] for o in outs]
            after.append(outs[3])
    (own,), (slots,) = _copies_wait("small_wait", small_st, _small_copies, after)
    dev_idx = (2 * chip + lax.axis_index("c")).astype(jnp.int32).reshape(1)
    small_shapes = [w[n].shape for n in SMALL] + [dcw.shape, (1, 1)]
    small_sum = _unpack_small(_sum_small(own, slots, dev_idx), small_shapes)
    g.update({n: small_sum[i] for i, n in enumerate(SMALL)})
    cshard = w["conv_w"].shape[2]
    g["conv_w"] = lax.dynamic_slice_in_dim(small_sum[len(SMALL)], chip * cshard, cshard, axis=1)[None]
    loss_total = small_sum[len(SMALL) + 1].reshape(())
    sm_names = SMALL + ("conv_w",)
    sm_shapes = [w[n].shape for n in sm_names]
    d_s, m_s, v_s = _adamw("adamw_small", _pack_small([w[n] for n in sm_names]), _pack_small([g[n] for n in sm_names]),
                           _pack_small([m[n] for n in sm_names]), _pack_small([v[n] for n in sm_names]))
    for dst, src in ((delta, d_s), (new_m, m_s), (new_v, v_s)):
        for n, val in zip(sm_names, _unpack_small(src, sm_shapes)):
            dst[n] = val

    return (loss_total, dx[None], *[g[n] for n in _WEIGHTS], *[delta[n] for n in _WEIGHTS],
            *[new_m[n] for n in _WEIGHTS], *[new_v[n] for n in _WEIGHTS])
```

```python
import jax
import jax.numpy as jnp
from jax import lax
from jax.experimental import pallas as pl
from jax.experimental.pallas import tpu as pltpu

f32 = jnp.float32
bf16 = jnp.bfloat16
MESH = pl.DeviceIdType.MESH
HIGHEST = lax.Precision.HIGHEST

EPS = 1e-6
N_CHIPS = 4
N_DEV = 8
D_MODEL = 1024
GM_WIDTH = 1024
GM_HEADS = 8
CHUNK = 128
SSM_WIDTH = 1024
SSM_HEADS = 16
SSM_HEAD_DIM = 64
SSM_GROUPS = 2
SSM_STATE = 128
SSM_CONV = 4
CONV_DIM = SSM_WIDTH + 2 * SSM_GROUPS * SSM_STATE
IN_PROJ = 2 * GM_WIDTH + SSM_WIDTH + CONV_DIM + SSM_HEADS
LANES = 128
ZXD = SSM_WIDTH + CONV_DIM + LANES

ADAM_LR = 0.001
ADAM_B1 = 0.9
ADAM_B2 = 0.999
ADAM_EPS = 1e-08
ADAM_WD = 0.01
ADAM_STEP = 10

VMEM_LIMIT = 56 * 1024 * 1024
HALF_ROWS_BF16 = 592
HALF_ROWS_F32 = 320


def _dot(a, b):
    return jnp.dot(a, b, preferred_element_type=f32)


def _dot_nt(a, b):
    return lax.dot_general(a, b, (((1,), (1,)), ((), ())), preferred_element_type=f32)


def _dot_tn(a, b):
    return lax.dot_general(a, b, (((0,), (0,)), ((), ())), preferred_element_type=f32)


def _rms(x, g):
    return x * lax.rsqrt(jnp.mean(x * x, axis=-1, keepdims=True) + EPS) * g


def _layernorm(x, g, b):
    mu = jnp.mean(x, axis=-1, keepdims=True)
    xc = x - mu
    return xc * lax.rsqrt(jnp.mean(xc * xc, axis=-1, keepdims=True) + EPS) * g + b


def _sigmoid(x):
    return 1.0 / (1.0 + jnp.exp(-x))


def _softplus(x):
    return jnp.maximum(x, 0.0) + jnp.log(1.0 + jnp.exp(-jnp.abs(x)))


def _full(shape):
    nd = len(shape)
    return pl.BlockSpec(shape, lambda *_: (0,) * nd, pipeline_mode=pl.Buffered(1))


def _acc(shape):
    nd = len(shape)
    return pl.BlockSpec(shape, lambda *_: (0,) * nd)


def _rows(tm, ncols):
    return pl.BlockSpec((tm, ncols), lambda i: (i, 0))


def _params(sem):
    return pltpu.CompilerParams(dimension_semantics=sem, vmem_limit_bytes=VMEM_LIMIT)


def _row_tile(rows, target, mult=8):
    best = rows
    for t in range(mult, min(rows, target) + 1, mult):
        if rows % t == 0:
            best = t
    return best if best <= target else rows


def _ffn_fwd(name, h, g, wg, wu, wd, pre=None, tm=256):
    T, D = h.shape
    F = wg.shape[0]
    tm = min(tm, T)

    def body(*refs):
        if pre is None:
            h_ref, g_ref, wg_ref, wu_ref, wd_ref, ho_ref, n_ref, a_ref, b_ref = refs
            hin = h_ref[...]
        else:
            (h_ref, ya_ref, yb_ref, wo_ref, g_ref, wg_ref, wu_ref, wd_ref,
             hi_ref, ho_ref, n_ref, a_ref, b_ref) = refs
            ga = ya_ref.shape[1]
            hin = h_ref[...] + _dot(ya_ref[...], wo_ref[:ga, :]) + _dot(yb_ref[...], wo_ref[ga:, :])
            hi_ref[...] = hin
        n = _rms(hin, g_ref[...]).astype(bf16)
        n_ref[...] = n
        a = _dot_nt(n, wg_ref[...]).astype(bf16)
        b = _dot_nt(n, wu_ref[...]).astype(bf16)
        a_ref[...] = a
        b_ref[...] = b
        af = a.astype(f32)
        hm = (af * _sigmoid(af) * b.astype(f32)).astype(bf16)
        ho_ref[...] = hin + 0.5 * _dot(hm, wd_ref[...])

    ins = [h] + (list(pre) if pre is not None else []) + [g, wg, wu, wd]
    in_specs = [_rows(tm, D)]
    if pre is not None:
        in_specs += [_rows(tm, pre[0].shape[1]), _rows(tm, pre[1].shape[1]), _full(pre[2].shape)]
    in_specs += [_full(g.shape), _full(wg.shape), _full(wu.shape), _full(wd.shape)]
    outs = [jax.ShapeDtypeStruct((T, D), f32), jax.ShapeDtypeStruct((T, D), bf16),
            jax.ShapeDtypeStruct((T, F), bf16), jax.ShapeDtypeStruct((T, F), bf16)]
    out_specs = [_rows(tm, D), _rows(tm, D), _rows(tm, F), _rows(tm, F)]
    if pre is not None:
        outs = [jax.ShapeDtypeStruct((T, D), f32)] + outs
        out_specs = [_rows(tm, D)] + out_specs
    return pl.pallas_call(body, name=name, grid=(T // tm,), in_specs=in_specs, out_specs=out_specs,
                          out_shape=outs, compiler_params=_params(("parallel",)))(*ins)


def _ffn_bwd(name, dh, hin, g, a, b, wg, wu, wd, wo=None, ga=0, tm=256):
    T, D = dh.shape
    F = wg.shape[0]
    tm = min(tm, T)

    def body(*refs):
        if wo is None:
            (dh_ref, hin_ref, g_ref, a_ref, b_ref, wg_ref, wu_ref, wd_ref,
             dhi_ref, da_ref, db_ref, hm_ref, dg_ref) = refs
        else:
            (dh_ref, hin_ref, g_ref, a_ref, b_ref, wg_ref, wu_ref, wd_ref, wo_ref,
             dhi_ref, da_ref, db_ref, hm_ref, dg_ref, dya_ref, dyb_ref) = refs

        @pl.when(pl.program_id(0) == 0)
        def _():
            dg_ref[...] = jnp.zeros_like(dg_ref)

        dh_ = dh_ref[...]
        dhb = (0.5 * dh_).astype(bf16)
        dhm = _dot_nt(dhb, wd_ref[...])
        af = a_ref[...].astype(f32)
        bf = b_ref[...].astype(f32)
        sg = _sigmoid(af)
        sl_ = af * sg
        da = (dhm * bf * (sg * (1.0 + af * (1.0 - sg)))).astype(bf16)
        db = (dhm * sl_).astype(bf16)
        da_ref[...] = da
        db_ref[...] = db
        hm_ref[...] = (sl_ * bf).astype(bf16)
        dn = _dot(da, wg_ref[...]) + _dot(db, wu_ref[...])
        _, vjp = jax.vjp(_rms, hin_ref[...], g_ref[...])
        dx, dg = vjp(dn)
        dhi = dh_ + dx
        dhi_ref[...] = dhi
        dg_ref[...] += dg
        if wo is not None:
            dhib = dhi.astype(bf16)
            dya_ref[...] = _dot_nt(dhib, wo_ref[:ga, :]).astype(bf16)
            dyb_ref[...] = _dot_nt(dhib, wo_ref[ga:, :]).astype(bf16)

    ins = [dh, hin, g, a, b, wg, wu, wd]
    in_specs = [_rows(tm, D), _rows(tm, D), _full(g.shape), _rows(tm, F), _rows(tm, F),
                _full(wg.shape), _full(wu.shape), _full(wd.shape)]
    act = jax.ShapeDtypeStruct((T, F), bf16)
    outs = [jax.ShapeDtypeStruct((T, D), f32), act, act, act, jax.ShapeDtypeStruct(g.shape, f32)]
    out_specs = [_rows(tm, D), _rows(tm, F), _rows(tm, F), _rows(tm, F), _acc(g.shape)]
    if wo is not None:
        gb = wo.shape[0] - ga
        ins += [wo]
        in_specs += [_full(wo.shape)]
        outs += [jax.ShapeDtypeStruct((T, ga), bf16), jax.ShapeDtypeStruct((T, gb), bf16)]
        out_specs += [_rows(tm, ga), _rows(tm, gb)]
    return pl.pallas_call(body, name=name, grid=(T // tm,), in_specs=in_specs, out_specs=out_specs,
                          out_shape=outs, compiler_params=_params(("arbitrary",)))(*ins)


def _matmul_tn(name, a, b, scale=1.0, tk=2048):
    T, M = a.shape
    N = b.shape[1]
    tk = min(tk, T)
    nk = T // tk
    tn = LANES * max(d for d in range(1, N // LANES + 1) if (N // LANES) % d == 0 and (d == 1 or M * d * LANES * 4 <= 6 * 1024 * 1024))

    def body(a_ref, b_ref, o_ref, acc):
        k = pl.program_id(1)

        @pl.when(k == 0)
        def _():
            acc[...] = jnp.zeros_like(acc)

        bb = b_ref[...]
        if scale != 1.0:
            bb = bb * scale
        acc[...] += _dot_tn(a_ref[...].astype(bf16), bb.astype(bf16))

        @pl.when(k == nk - 1)
        def _():
            o_ref[...] = acc[...].astype(bf16)

    return pl.pallas_call(
        body, name=name, grid=(N // tn, nk),
        in_specs=[pl.BlockSpec((tk, M), lambda j, k: (k, 0)), pl.BlockSpec((tk, tn), lambda j, k: (k, j))],
        out_specs=pl.BlockSpec((M, tn), lambda j, k: (0, j)),
        out_shape=jax.ShapeDtypeStruct((M, N), bf16), scratch_shapes=[pltpu.VMEM((M, tn), f32)],
        compiler_params=_params(("parallel", "arbitrary")))(a, b)


def _gelu_and_slope(x):
    cdf = 0.5 * (1.0 + lax.erf(x * 0.7071067811865476))
    return x * cdf, cdf + x * (0.3989422804014327 * jnp.exp(-0.5 * x * x))


def _tril_mask():
    r = lax.broadcasted_iota(jnp.int32, (CHUNK, CHUNK), 0)
    c = lax.broadcasted_iota(jnp.int32, (CHUNK, CHUNK), 1)
    return c <= r


def _gm_mix(vnb, ws_ref, bst, mixed_sc, tm):
    mask = _tril_mask()
    for h in range(GM_HEADS):
        wt = jnp.where(mask, ws_ref[h], 0.0).astype(bf16)
        bias = bst[:, h:h + 1]
        for q in range(tm // CHUNK):
            rs = slice(q * CHUNK, (q + 1) * CHUNK)
            cs = slice(h * CHUNK, (h + 1) * CHUNK)
            mixed_sc[rs, cs] = _dot(wt, vnb[rs, cs]) + bias


def _mix_fwd(h1, gmix, w_uv, w_zxd, ln_g, ln_b, w_s, b_st, gout, tm=512):
    T, D = h1.shape
    tm = min(tm, T)
    G = GM_WIDTH

    def body(h_ref, g_ref, wuv_ref, wzxd_ref, lng_ref, lnb_ref, ws_ref, bst_ref, gout_ref,
             n_ref, act_ref, slope_ref, z_ref, xbc_ref, dt_ref, ya_ref, mixed_sc):
        n = _rms(h_ref[...], g_ref[...]).astype(bf16)
        n_ref[...] = n
        uv = _dot_nt(n, wuv_ref[...]).astype(bf16)
        zxd = _dot_nt(n, wzxd_ref[...])
        z_ref[...] = zxd[:, :SSM_WIDTH].astype(bf16)
        xbc_ref[...] = zxd[:, SSM_WIDTH:SSM_WIDTH + CONV_DIM].astype(bf16)
        dt_ref[...] = zxd[:, SSM_WIDTH + CONV_DIM:]
        act, slope = _gelu_and_slope(uv.astype(f32))
        act = act.astype(bf16)
        act_ref[...] = act
        slope_ref[...] = slope.astype(bf16)
        ug, vg = act[:, :G].astype(f32), act[:, G:].astype(f32)
        _gm_mix(_layernorm(vg, lng_ref[...], lnb_ref[...]).astype(bf16), ws_ref, bst_ref[...], mixed_sc, tm)
        ya_ref[...] = _rms(ug * mixed_sc[...], gout_ref[...]).astype(bf16)

    ins = [h1, gmix, w_uv, w_zxd, ln_g, ln_b, w_s, b_st, gout]
    in_specs = [_rows(tm, D)] + [_full(x.shape) for x in ins[1:]]
    outs = [jax.ShapeDtypeStruct((T, D), bf16), jax.ShapeDtypeStruct((T, 2 * G), bf16), jax.ShapeDtypeStruct((T, 2 * G), bf16),
            jax.ShapeDtypeStruct((T, SSM_WIDTH), bf16), jax.ShapeDtypeStruct((T, CONV_DIM), bf16),
            jax.ShapeDtypeStruct((T, LANES), f32), jax.ShapeDtypeStruct((T, G), bf16)]
    out_specs = [_rows(tm, D), _rows(tm, 2 * G), _rows(tm, 2 * G), _rows(tm, SSM_WIDTH), _rows(tm, CONV_DIM), _rows(tm, LANES),
                 _rows(tm, G)]
    return pl.pallas_call(body, name="mix_fwd", grid=(T // tm,), in_specs=in_specs, out_specs=out_specs,
                          out_shape=outs, scratch_shapes=[pltpu.VMEM((tm, G), f32)],
                          compiler_params=_params(("parallel",)))(*ins)


def _mix_bwd(dh, h1, gmix, act, slope, dya, dzxd, w_uv, w_zxd, ln_g, ln_b, w_s, b_st, gout, tm=256):
    T, D = dh.shape
    tm = min(tm, T)
    G = GM_WIDTH

    def body(dh_ref, h_ref, g_ref, act_ref, slope_ref, dya_ref, dzxd_ref, wuv_ref, wzxd_ref, lng_ref, lnb_ref, ws_ref,
             bst_ref, gout_ref, dhi_ref, duv_ref, dg_ref, dlng_ref, dlnb_ref, dws_ref, dbst_ref, dgout_ref, mixed_sc, dvn_sc):
        @pl.when(pl.program_id(0) == 0)
        def _():
            for r in (dg_ref, dlng_ref, dlnb_ref, dws_ref, dbst_ref, dgout_ref):
                r[...] = jnp.zeros_like(r)

        dn_z = _dot(dzxd_ref[...], wzxd_ref[...])
        ug = act_ref[:, :G].astype(f32)
        vn, ln_vjp = jax.vjp(_layernorm, act_ref[:, G:].astype(f32), lng_ref[...], lnb_ref[...])
        vnb = vn.astype(bf16)
        _gm_mix(vnb, ws_ref, bst_ref[...], mixed_sc, tm)
        mixed = mixed_sc[...]
        _, out_vjp = jax.vjp(_rms, ug * mixed, gout_ref[...])
        dpre, dgout = out_vjp(dya_ref[...].astype(f32))
        dgout_ref[...] += dgout
        dug = dpre * mixed
        dmixed = dpre * ug
        mask = _tril_mask()
        lane = lax.broadcasted_iota(jnp.int32, (1, GM_HEADS), 1)
        dbst = jnp.zeros((CHUNK, GM_HEADS), f32)
        for h in range(GM_HEADS):
            wt = jnp.where(mask, ws_ref[h], 0.0).astype(bf16)
            cs = slice(h * CHUNK, (h + 1) * CHUNK)
            dw = jnp.zeros((CHUNK, CHUNK), f32)
            for q in range(tm // CHUNK):
                rs = slice(q * CHUNK, (q + 1) * CHUNK)
                dm = dmixed[rs, cs]
                dmb = dm.astype(bf16)
                dw = dw + _dot_nt(dmb, vnb[rs, cs])
                dbst = dbst + jnp.sum(dm, axis=1, keepdims=True) * (lane == h).astype(f32)
                dvn_sc[rs, cs] = _dot_tn(wt, dmb)
            dws_ref[h] += jnp.where(mask, dw, 0.0)
        dbst_ref[...] += dbst
        dvg, dlng, dlnb = ln_vjp(dvn_sc[...])
        duv = (jnp.concatenate([dug, dvg], axis=1) * slope_ref[...].astype(f32)).astype(bf16)
        duv_ref[...] = duv
        dlng_ref[...] += dlng
        dlnb_ref[...] += dlnb
        dn = dn_z + _dot(duv, wuv_ref[...])
        _, vjp = jax.vjp(_rms, h_ref[...], g_ref[...])
        dx, dg = vjp(dn)
        dhi_ref[...] = dh_ref[...] + dx
        dg_ref[...] += dg

    ins = [dh, h1, gmix, act, slope, dya, dzxd, w_uv, w_zxd, ln_g, ln_b, w_s, b_st, gout]
    in_specs = ([_rows(tm, D), _rows(tm, D), _full(gmix.shape), _rows(tm, 2 * G), _rows(tm, 2 * G), _rows(tm, G),
                 _rows(tm, dzxd.shape[1])] + [_full(x.shape) for x in ins[7:]])
    accs = (gmix, ln_g, ln_b, w_s, b_st, gout)
    outs = ([jax.ShapeDtypeStruct((T, D), f32), jax.ShapeDtypeStruct((T, 2 * G), bf16)]
            + [jax.ShapeDtypeStruct(x.shape, f32) for x in accs])
    out_specs = [_rows(tm, D), _rows(tm, 2 * G)] + [_acc(x.shape) for x in accs]
    return pl.pallas_call(body, name="mix_bwd", grid=(T // tm,), in_specs=in_specs, out_specs=out_specs,
                          out_shape=outs, scratch_shapes=[pltpu.VMEM((tm, G), f32), pltpu.VMEM((tm, G), f32)],
                          compiler_params=_params(("arbitrary",)))(*ins)


HALO = 16
SSD_SUB = 4


class _RowsOf:
    def __init__(self, ref, rows):
        self.ref, self.rows = ref, rows

    def _index(self, idx):
        return (self.rows, slice(None)) if idx is Ellipsis else (self.rows,) + tuple(idx[1:])

    def __getitem__(self, idx):
        return self.ref[self._index(idx)]

    def __setitem__(self, idx, value):
        self.ref[self._index(idx)] = value
PAIRS = SSM_HEADS // 2
PAIR_W = 2 * SSM_HEAD_DIM


def _split(x, n):
    parts = []
    for _ in range(n):
        p = x.astype(bf16)
        parts.append(p)
        x = x - p.astype(f32)
    return parts


def _dot_sel(x, sel_n, n):
    return _dot(jnp.concatenate(_split(x, n), axis=1), sel_n)


def _sel_dot(sel, x, n):
    return _dot(jnp.concatenate([sel] * n, axis=1), jnp.concatenate(_split(x, n), axis=0))


EXPAND_SPLIT = 3
REDUCE_SPLIT = 2


def _head_mats():
    ex = (jnp.arange(SSM_WIDTH)[None, :] // SSM_HEAD_DIM == jnp.arange(LANES)[:, None]).astype(bf16)
    return jnp.tile(ex, (EXPAND_SPLIT, 1)), jnp.tile(ex.T, (REDUCE_SPLIT, 1))


def _shift_mat(rows, cols, off):
    r = lax.broadcasted_iota(jnp.int32, (rows, cols), 0)
    c = lax.broadcasted_iota(jnp.int32, (rows, cols), 1)
    return (c == r + off).astype(bf16)


def _ssd_conv(halo, x, cw_ref, cb_ref):
    ext = jnp.concatenate([halo, x], axis=0)
    xc = cb_ref[...] + cw_ref[SSM_CONV - 1:SSM_CONV, :] * x.astype(f32)
    for j in range(SSM_CONV - 1):
        xc = xc + cw_ref[j:j + 1, :] * _dot(_shift_mat(CHUNK, HALO + CHUNK, HALO - SSM_CONV + 1 + j), ext)
    return xc


def _ssd_front(dtr, dtb_ref, alog_ref):
    dt = _softplus(dtr + dtb_ref[...])
    a = -jnp.exp(alog_ref[...])
    acs = jnp.dot(_tril_mask().astype(f32), dt * a, preferred_element_type=f32, precision=HIGHEST)
    return dt, a, acs


def _ssd_wide(xa, dt, acs, dsk, ex):
    dt_x = _dot_sel(dt, ex, EXPAND_SPLIT)
    acs_x = _dot_sel(acs, ex, EXPAND_SPLIT)
    dsk_x = _dot_sel(jnp.broadcast_to(dsk, (8, LANES)), ex, EXPAND_SPLIT)[0:1]
    e_x = jnp.exp(acs_x)
    r_x = jnp.exp(acs_x[CHUNK - 1:CHUNK, :] - acs_x)
    xs = xa[:, :SSM_WIDTH]
    xd = xs * dt_x
    return dt_x, dsk_x, e_x, r_x, xs, xd, xd * r_x


def _pair_stack(v, lo):
    return jnp.concatenate([jnp.where(lo, v, 0.0), jnp.where(lo, 0.0, v)], axis=0)


def _ssd_pair(j, acs, acs_t, cb):
    out = []
    tril = _tril_mask()
    for h in (2 * j, 2 * j + 1):
        dk = jnp.exp(jnp.where(tril, acs[:, h:h + 1] - acs_t[h:h + 1, :], -jnp.inf))
        out.append((dk, cb * dk))
    return out


def _pair_col(row_lo, tot, j):
    return jnp.exp(jnp.where(row_lo, tot[:, 2 * j:2 * j + 1], tot[:, 2 * j + 1:2 * j + 2]))


def _gated_norm(y, z, g):
    yg = y * (z * _sigmoid(z))
    half = SSM_WIDTH // SSM_GROUPS
    parts = []
    for k in range(SSM_GROUPS):
        s = yg[:, k * half:(k + 1) * half]
        parts.append(s * lax.rsqrt(jnp.mean(s * s, axis=-1, keepdims=True) + EPS))
    return jnp.concatenate(parts, axis=1) * g


def _group_mats(xa):
    out = []
    for g in range(SSM_GROUPS):
        bm = xa[:, SSM_WIDTH + g * SSM_STATE:SSM_WIDTH + (g + 1) * SSM_STATE].astype(bf16)
        cm = xa[:, SSM_WIDTH + (SSM_GROUPS + g) * SSM_STATE:SSM_WIDTH + (SSM_GROUPS + g + 1) * SSM_STATE].astype(bf16)
        out.append((cm, bm, _dot_nt(cm, bm)))
    return out


def _ssd_fwd(xbc, z, dtr, conv_w, conv_b, dt_bias, a_log, d_skip, ssm_norm):
    T = xbc.shape[0]
    nc = T // CHUNK
    N = SSM_STATE

    def body(xbc_ref, halo_ref, z_ref, dtr_ref, cw_ref, cb_ref, dtb_ref, alog_ref, dsk_ref, g_ref, ex_ref,
             yb_ref, xc_ref, sg_ref, y_ref, sprev_ref, s_sc):
        i = pl.program_id(0)

        @pl.when(i == 0)
        def _():
            s_sc[...] = jnp.zeros_like(s_sc)

        lo = lax.broadcasted_iota(jnp.int32, (CHUNK, PAIR_W), 1) < SSM_HEAD_DIM
        row_lo = lax.broadcasted_iota(jnp.int32, (PAIR_W, 1), 0) < SSM_HEAD_DIM
        for k in range(SUB):
            rs = slice(k * CHUNK, (k + 1) * CHUNK)
            if k == 0:
                halo = halo_ref[...]
                halo = jnp.where(i > 0, halo, jnp.zeros_like(halo))
            else:
                halo = xbc_ref[k * CHUNK - HALO:k * CHUNK, :]
            xc = _ssd_conv(halo, xbc_ref[rs, :], cw_ref, cb_ref)
            sg = _sigmoid(xc)
            xc_ref[rs, :] = xc
            sg_ref[rs, :] = sg
            xa = xc * sg
            dt, _, acs = _ssd_front(dtr_ref[rs, :], dtb_ref, alog_ref)
            _, dsk_x, e_x, _, xs, xd, gm = _ssd_wide(xa, dt, acs, dsk_ref[...], ex_ref[...])
            acs_t = acs.T
            tot = acs[CHUNK - 1:CHUNK, :]
            groups = _group_mats(xa)
            ys = []
            for j in range(PAIRS):
                cmb, bmb, cb = groups[j // (PAIRS // SSM_GROUPS)]
                ps = slice(j * PAIR_W, (j + 1) * PAIR_W)
                (_, m0), (_, m1) = _ssd_pair(j, acs, acs_t, cb)
                sp = s_sc[j]
                yd = _dot(jnp.concatenate([m0, m1], axis=1).astype(bf16), _pair_stack(xd[:, ps], lo).astype(bf16))
                ys.append(yd + e_x[:, ps] * _dot_nt(cmb, sp.astype(bf16)))
                sprev_ref[k, j] = sp
                s_sc[j] = _pair_col(row_lo, tot, j) * sp + _dot_tn(gm[:, ps].astype(bf16), bmb)
            y = jnp.concatenate(ys, axis=1) + xs * dsk_x
            y_ref[rs, :] = y
            yb_ref[rs, :] = _gated_norm(y, z_ref[rs, :].astype(f32), g_ref[...]).astype(bf16)

    params = [conv_w, conv_b, dt_bias, a_log, d_skip, ssm_norm, _head_mats()[0]]
    SUB = SSD_SUB if nc % SSD_SUB == 0 else 1
    hp = SUB * CHUNK // HALO
    R = SUB * CHUNK
    in_specs = [_rows(R, CONV_DIM), pl.BlockSpec((HALO, CONV_DIM), lambda i: (jnp.maximum(i * hp - 1, 0), 0)),
                _rows(R, SSM_WIDTH), _rows(R, LANES)] + [_full(x.shape) for x in params]
    return pl.pallas_call(
        body, name="ssd_fwd", grid=(nc // SUB,), in_specs=in_specs,
        out_specs=[_rows(R, SSM_WIDTH), _rows(R, CONV_DIM), _rows(R, CONV_DIM), _rows(R, SSM_WIDTH),
                   pl.BlockSpec((SUB, PAIRS, PAIR_W, N), lambda i: (i, 0, 0, 0))],
        out_shape=[jax.ShapeDtypeStruct((T, SSM_WIDTH), bf16), jax.ShapeDtypeStruct((T, CONV_DIM), f32),
                   jax.ShapeDtypeStruct((T, CONV_DIM), f32), jax.ShapeDtypeStruct((T, SSM_WIDTH), f32),
                   jax.ShapeDtypeStruct((nc, PAIRS, PAIR_W, N), f32)],
        scratch_shapes=[pltpu.VMEM((PAIRS, PAIR_W, N), f32)],
        compiler_params=_params(("arbitrary",)))(xbc, xbc, z, dtr, *params)


def _ssd_bwd(xbc, xc, sg, y, z, dtr, sprev, dyb, conv_w, conv_b, dt_bias, a_log, d_skip, ssm_norm):
    T = xbc.shape[0]
    nc = T // CHUNK
    H, N = SSM_HEADS, SSM_STATE
    PG = PAIRS // SSM_GROUPS

    def chunk(xbc_ref, xc_ref, sg_ref, y_ref, z_ref, dtr_ref, sprev_k, dyb_ref, cw_ref, cb_ref, dtb_ref, alog_ref, dsk_ref,
              g_ref, ex_ref, rd_ref, dzxd_ref, dcw_ref, dcb_ref, ddtb_ref, dalog_ref, ddsk_ref, dg_ref, ds_sc, next_sc):
        xc = xc_ref[...]
        sg = sg_ref[...]
        xa = xc * sg
        dt, a, acs = _ssd_front(dtr_ref[...], dtb_ref, alog_ref)
        dt_x, dsk_x, e_x, r_x, xs, xd, gm = _ssd_wide(xa, dt, acs, dsk_ref[...], ex_ref[...])
        acs_t = acs.T
        tot = acs[CHUNK - 1:CHUNK, :]
        groups = _group_mats(xa)
        lo = lax.broadcasted_iota(jnp.int32, (CHUNK, PAIR_W), 1) < SSM_HEAD_DIM
        row_lo = lax.broadcasted_iota(jnp.int32, (PAIR_W, 1), 0) < SSM_HEAD_DIM
        pairs, zs = [], []
        for j in range(PAIRS):
            cmb, _, cb = groups[j // PG]
            pairs.append(_ssd_pair(j, acs, acs_t, cb))
            zs.append(_dot_nt(cmb, sprev_k[j].astype(bf16)))
        zf = jnp.concatenate(zs, axis=1)
        _, gn_vjp = jax.vjp(_gated_norm, y_ref[...], z_ref[...].astype(f32), g_ref[...])
        dy, dz, dg = gn_vjp(dyb_ref[...].astype(f32))
        dg_ref[...] += dg
        dzxd_ref[:, :SSM_WIDTH] = dz.astype(bf16)

        lane = lax.broadcasted_iota(jnp.int32, (1, LANES), 1)
        sub = lax.broadcasted_iota(jnp.int32, (LANES, 1), 0)
        dacs = jnp.zeros((CHUNK, LANES), f32)
        dacs_r = jnp.zeros((LANES, CHUNK), f32)
        dtot = jnp.zeros((1, LANES), f32)
        dcb = [jnp.zeros((CHUNK, CHUNK), f32) for _ in range(SSM_GROUPS)]
        dcm = [jnp.zeros((CHUNK, N), f32) for _ in range(SSM_GROUPS)]
        dbm = [jnp.zeros((CHUNK, N), f32) for _ in range(SSM_GROUPS)]
        dxds, dgms = [], []
        for j in range(PAIRS):
            g = j // PG
            cmb, bmb, _ = groups[g]
            ps = slice(j * PAIR_W, (j + 1) * PAIR_W)
            (dk0, m0), (dk1, m1) = pairs[j]
            oh0, oh1 = (lane == 2 * j).astype(f32), (lane == 2 * j + 1).astype(f32)
            dyp = dy[:, ps]
            dy2 = _pair_stack(dyp, lo).astype(bf16)
            dm2 = _dot_nt(dy2, xd[:, ps].astype(bf16))
            m2 = jnp.concatenate([m0, m1], axis=0)
            dxds.append(_dot_tn(m2.astype(bf16), dy2))
            w2 = dm2 * m2
            rs = jnp.sum(w2, axis=1, keepdims=True)
            dacs = dacs + rs[:CHUNK] * oh0 + rs[CHUNK:] * oh1
            dacs_r = dacs_r - ((sub == 2 * j).astype(f32) * jnp.sum(w2[:CHUNK], axis=0, keepdims=True)
                               + (sub == 2 * j + 1).astype(f32) * jnp.sum(w2[CHUNK:], axis=0, keepdims=True))
            dcb[g] = dcb[g] + dm2[:CHUNK] * dk0 + dm2[CHUNK:] * dk1
            sp = sprev_k[j]
            dzb = (dyp * e_x[:, ps]).astype(bf16)
            dcm[g] = dcm[g] + _dot(dzb, sp.astype(bf16))
            dsn = ds_sc[j]
            dsnb = dsn.astype(bf16)
            et = _pair_col(row_lo, tot, j)
            rr = jnp.sum(dsn * sp, axis=1, keepdims=True) * et
            dtot = dtot + jnp.sum(rr[:SSM_HEAD_DIM]) * oh0 + jnp.sum(rr[SSM_HEAD_DIM:]) * oh1
            dgms.append(_dot_nt(bmb, dsnb))
            dbm[g] = dbm[g] + _dot(gm[:, ps].astype(bf16), dsnb)
            ds_sc[j] = _dot_tn(dzb, cmb) + et * dsn
        dgm = jnp.concatenate(dgms, axis=1)
        dxd = jnp.concatenate(dxds, axis=1) + dgm * r_x
        dr = dgm * gm
        red = _dot_sel(jnp.concatenate([dy * e_x * zf - dr, dr, dxd * xs, dy * xs], axis=0), rd_ref[...], REDUCE_SPLIT)
        rowi = lax.broadcasted_iota(jnp.int32, (CHUNK, 1), 0)
        dtot = dtot + jnp.sum(red[CHUNK:2 * CHUNK], axis=0, keepdims=True)
        dacs = dacs + red[:CHUNK] + dacs_r.T + jnp.where(rowi == CHUNK - 1, dtot, 0.0)
        r2 = lax.broadcasted_iota(jnp.int32, (CHUNK, CHUNK), 0)
        c2 = lax.broadcasted_iota(jnp.int32, (CHUNK, CHUNK), 1)
        dadt = jnp.dot((c2 >= r2).astype(f32), dacs, preferred_element_type=f32, precision=HIGHEST)
        ddt = red[2 * CHUNK:3 * CHUNK] + dadt * a
        dalog_ref[...] += jnp.sum(dadt * dt, axis=0, keepdims=True) * a
        ddsk_ref[...] += jnp.sum(red[3 * CHUNK:], axis=0, keepdims=True)
        ddtr = jnp.where(lane < H, ddt * _sigmoid(dtr_ref[...] + dtb_ref[...]), 0.0)
        ddtb_ref[...] += jnp.sum(ddtr, axis=0, keepdims=True)
        dzxd_ref[:, SSM_WIDTH + CONV_DIM:] = ddtr.astype(bf16)
        dxa_bm, dxa_cm = [], []
        for g in range(SSM_GROUPS):
            cmb, bmb, _ = groups[g]
            dcbb = dcb[g].astype(bf16)
            dxa_bm.append(dbm[g] + _dot_tn(dcbb, cmb))
            dxa_cm.append(dcm[g] + _dot(dcbb, bmb))
        dxc = jnp.concatenate([dy * dsk_x + dxd * dt_x] + dxa_bm + dxa_cm, axis=1) * (sg * (1.0 + xc * (1.0 - sg)))
        ext = jnp.concatenate([dxc, next_sc[...]], axis=0)
        xin = xbc_ref[...].astype(f32)
        dxbc = cw_ref[SSM_CONV - 1:SSM_CONV, :] * dxc
        dcw = [jnp.sum(dxc * xin, axis=0, keepdims=True)]
        for s in range(1, SSM_CONV):
            later = _sel_dot(_shift_mat(CHUNK, CHUNK + HALO, s), ext, 2)
            dxbc = dxbc + cw_ref[SSM_CONV - 1 - s:SSM_CONV - s, :] * later
            dcw.insert(0, jnp.sum(later * xin, axis=0, keepdims=True))
        dzxd_ref[:, SSM_WIDTH:SSM_WIDTH + CONV_DIM] = dxbc.astype(bf16)
        dcw_ref[...] += jnp.concatenate(dcw, axis=0)
        dcb_ref[...] += jnp.sum(dxc, axis=0, keepdims=True)
        next_sc[...] = dxc[0:HALO, :]

    SUB = SSD_SUB if nc % SSD_SUB == 0 else 1
    nb = nc // SUB

    def body(xbc_ref, xc_ref, sg_ref, y_ref, z_ref, dtr_ref, sprev_ref, dyb_ref, cw_ref, cb_ref, dtb_ref, alog_ref, dsk_ref,
             g_ref, ex_ref, rd_ref, dzxd_ref, dcw_ref, dcb_ref, ddtb_ref, dalog_ref, ddsk_ref, dg_ref, ds_sc, next_sc):
        @pl.when(pl.program_id(0) == 0)
        def _():
            ds_sc[...] = jnp.zeros_like(ds_sc)
            next_sc[...] = jnp.zeros_like(next_sc)
            for r_ in (dcw_ref, dcb_ref, ddtb_ref, dalog_ref, ddsk_ref, dg_ref):
                r_[...] = jnp.zeros_like(r_)

        for k in reversed(range(SUB)):
            rows = slice(k * CHUNK, (k + 1) * CHUNK)
            tok = [_RowsOf(r_, rows) for r_ in (xbc_ref, xc_ref, sg_ref, y_ref, z_ref, dtr_ref)]
            chunk(*tok, sprev_ref.at[k], _RowsOf(dyb_ref, rows), cw_ref, cb_ref, dtb_ref, alog_ref, dsk_ref, g_ref, ex_ref,
                  rd_ref, _RowsOf(dzxd_ref, rows), dcw_ref, dcb_ref, ddtb_ref, dalog_ref, ddsk_ref, dg_ref, ds_sc, next_sc)

    params = [conv_w, conv_b, dt_bias, a_log, d_skip, ssm_norm]
    mats = list(_head_mats())

    def rev(ncols):
        return pl.BlockSpec((SUB * CHUNK, ncols), lambda i: (nb - 1 - i, 0))

    in_specs = ([rev(CONV_DIM), rev(CONV_DIM), rev(CONV_DIM), rev(SSM_WIDTH), rev(SSM_WIDTH), rev(LANES),
                 pl.BlockSpec((SUB, PAIRS, PAIR_W, N), lambda i: (nb - 1 - i, 0, 0, 0)), rev(SSM_WIDTH)]
                + [_full(x.shape) for x in params + mats])
    return pl.pallas_call(
        body, name="ssd_bwd", grid=(nb,), in_specs=in_specs,
        out_specs=[rev(ZXD)] + [_acc(x.shape) for x in params],
        out_shape=[jax.ShapeDtypeStruct((T, ZXD), bf16)] + [jax.ShapeDtypeStruct(x.shape, f32) for x in params],
        scratch_shapes=[pltpu.VMEM((PAIRS, PAIR_W, N), f32), pltpu.VMEM((HALO, CONV_DIM), f32)],
        compiler_params=_params(("arbitrary",)))(xbc, xc, sg, y, z, dtr, sprev, dyb, *params, *mats)


def _tail(h3, p, tgt, gp, wpg, bpg, wpp, gf, tm=512):
    T, D = h3.shape
    tm = min(tm, T)

    def head(gpre, pp, h, gf_, t):
        gate = _sigmoid(gpre)
        y = _rms(h + gate * pp, gf_)
        err = y - t
        return 0.5 * jnp.sum(jnp.mean(err * err, axis=-1))

    def body(h_ref, p_ref, t_ref, gp_ref, wpg_ref, bpg_ref, wpp_ref, gf_ref,
             dh_ref, loss_ref, dgp_ref, dwpg_ref, dbpg_ref, dwpp_ref, dgf_ref):
        @pl.when(pl.program_id(0) == 0)
        def _():
            for r in (loss_ref, dgp_ref, dwpg_ref, dbpg_ref, dwpp_ref, dgf_ref):
                r[...] = jnp.zeros_like(r)

        h = h_ref[...]
        npf, np_vjp = jax.vjp(_rms, h, gp_ref[...])
        npb = npf.astype(bf16)
        pb = p_ref[...].astype(bf16)
        gpre = _dot(npb, wpg_ref[...]) + bpg_ref[...]
        kp, _, cp = wpp_ref.shape
        pp = jnp.concatenate([_dot(pb, wpp_ref[k]) for k in range(kp)], axis=1)
        loss, head_vjp = jax.vjp(head, gpre, pp, h, gf_ref[...], t_ref[...])
        dgpre, dpp, dh_a, dgf, _ = head_vjp(jnp.ones((), f32))
        loss_ref[...] += loss
        dgf_ref[...] += dgf
        dbpg_ref[...] += jnp.sum(dgpre, axis=0, keepdims=True)
        dgb = dgpre.astype(bf16)
        dwpg_ref[...] += _dot_tn(npb, dgb)
        dppb = dpp.astype(bf16)
        for k in range(kp):
            dwpp_ref[k] += _dot_tn(pb, dppb[:, k * cp:(k + 1) * cp])
        dh_b, dgp = np_vjp(_dot_nt(dgb, wpg_ref[...]))
        dgp_ref[...] += dgp
        dh_ref[...] = dh_a + dh_b

    ins = [h3, p, tgt, gp, wpg, bpg, wpp, gf]
    in_specs = [_rows(tm, D), _rows(tm, p.shape[1]), _rows(tm, D)] + [_full(x.shape) for x in ins[3:]]
    acc_shapes = [(1, LANES), gp.shape, wpg.shape, bpg.shape, wpp.shape, gf.shape]
    return pl.pallas_call(
        body, name="tail", grid=(T // tm,), in_specs=in_specs,
        out_specs=[_rows(tm, D)] + [_acc(s) for s in acc_shapes],
        out_shape=[jax.ShapeDtypeStruct((T, D), f32)] + [jax.ShapeDtypeStruct(s, f32) for s in acc_shapes],
        compiler_params=_params(("arbitrary",)))(*ins)


def _adamw(name, w, g, m, v, tr=256):
    R, rest = w.shape[0], w.shape[1:]
    tr = _row_tile(R, tr, 8 if len(rest) == 1 else 1)

    def body(w_ref, g_ref, m_ref, v_ref, d_ref, mo_ref, vo_ref):
        g_ = g_ref[...]
        m_ = ADAM_B1 * m_ref[...] + (1.0 - ADAM_B1) * g_
        v_ = ADAM_B2 * v_ref[...] + (1.0 - ADAM_B2) * jnp.square(g_)
        m_hat = m_ / (1.0 - ADAM_B1 ** ADAM_STEP)
        v_hat = v_ / (1.0 - ADAM_B2 ** ADAM_STEP)
        d_ref[...] = -ADAM_LR * (m_hat / (jnp.sqrt(v_hat) + ADAM_EPS) + ADAM_WD * w_ref[...])
        mo_ref[...] = m_
        vo_ref[...] = v_

    spec = pl.BlockSpec((tr,) + rest, lambda i: (i,) + (0,) * len(rest))
    return pl.pallas_call(body, name=name, grid=(R // tr,), in_specs=[spec] * 4, out_specs=[spec] * 3,
                          out_shape=[jax.ShapeDtypeStruct(w.shape, f32)] * 3,
                          compiler_params=_params(("parallel",)))(w, g, m, v)


HBM = pl.BlockSpec(memory_space=pltpu.HBM)


def _me():
    return lax.axis_index("x"), lax.axis_index("y"), lax.axis_index("c")


def _other_chips(x, y):
    return [(1 - x, y), (x, 1 - y), (1 - x, 1 - y)]


def _remote(src, dst, send_sem, recv_sem, dev):
    return pltpu.make_async_remote_copy(src_ref=src, dst_ref=dst, send_sem=send_sem, recv_sem=recv_sem,
                                        device_id=dev, device_id_type=MESH)


def _sems(n):
    return [pltpu.SemaphoreType.DMA((n,)), pltpu.SemaphoreType.DMA((n,))]


def _gather_weights(shards, split):
    n = len(shards)

    def body(*refs):
        ins, outs = refs[:n], refs[n:2 * n]
        own_send, own_recv, ici_send, ici_recv, d2d_send, d2d_recv = refs[2 * n:]
        x, y, c = _me()
        my_chip = 2 * x + y
        sibling = (x, y, 1 - c)
        chips = _other_chips(x, y)

        def rows(i, half):
            hr = shards[i].shape[0] // 2
            return pl.ds(half * hr, hr) if split[i] else pl.ds(0, shards[i].shape[0])

        sends = []
        for i in range(n):
            for j, chip in enumerate(chips):
                cp = _remote(ins[i].at[rows(i, c)], outs[i].at[my_chip, rows(i, c)],
                             ici_send.at[3 * i + j], ici_recv.at[3 * i + j], (*chip, c))
                cp.start()
                sends.append(cp)
            cp = _remote(ins[i], outs[i].at[my_chip], own_send.at[i], own_recv.at[i], sibling)
            cp.start()
            sends.append(cp)
        for i in range(n):
            for j, chip in enumerate(chips):
                s = 3 * i + j
                land = outs[i].at[2 * chip[0] + chip[1], rows(i, c)]
                _remote(land, land, ici_send.at[s], ici_recv.at[s], (*chip, c)).wait_recv()
                if split[i]:
                    cp = _remote(land, land, d2d_send.at[s], d2d_recv.at[s], sibling)
                    cp.start()
                    sends.append(cp)
        for i in range(n):
            _remote(ins[i], outs[i].at[my_chip], own_send.at[i], own_recv.at[i], sibling).wait_recv()
            if split[i]:
                for j, chip in enumerate(chips):
                    s = 3 * i + j
                    land = outs[i].at[2 * chip[0] + chip[1], rows(i, 1 - c)]
                    _remote(land, land, d2d_send.at[s], d2d_recv.at[s], sibling).wait_recv()
        for cp in sends:
            cp.wait_send()

    return pl.pallas_call(
        body, name="gather_weights", out_shape=[jax.ShapeDtypeStruct((N_CHIPS,) + s.shape, s.dtype) for s in shards],
        in_specs=[HBM] * n, out_specs=[HBM] * n,
        scratch_shapes=_sems(n) + _sems(3 * n) + _sems(3 * n))(*shards)


def _swap_halves(name, grads):
    n = len(grads)

    def body(*refs):
        ins, outs, send, recv = refs[:n], refs[n:2 * n], refs[2 * n], refs[2 * n + 1]
        x, y, c = _me()
        copies = []
        for i in range(n):
            hr = grads[i].shape[1] // 2
            cp = _remote(ins[i].at[:, pl.ds((1 - c) * hr, hr), :], outs[i], send.at[i], recv.at[i], (x, y, 1 - c))
            cp.start()
            copies.append(cp)
        for cp in copies:
            cp.wait()

    return pl.pallas_call(
        body, name=name,
        out_shape=[jax.ShapeDtypeStruct((g.shape[0], g.shape[1] // 2, g.shape[2]), g.dtype) for g in grads],
        in_specs=[HBM] * n, out_specs=[HBM] * n, scratch_shapes=_sems(n))(*grads)


def _add_halves(name, grads, other, c_idx, th=HALF_ROWS_BF16):
    K, R, C = grads.shape
    H = R // 2
    th = _row_tile(H, th, 16)
    nb = H // th

    def body(c_ref, g_ref, o_ref, out_ref):
        out_ref[...] = (g_ref[...].astype(f32) + o_ref[...].astype(f32)).astype(bf16)

    grid_spec = pltpu.PrefetchScalarGridSpec(
        num_scalar_prefetch=1, grid=(nb,),
        in_specs=[pl.BlockSpec((K, th, C), lambda i, c: (0, c[0] * nb + i, 0)),
                  pl.BlockSpec((K, th, C), lambda i, c: (0, i, 0))],
        out_specs=pl.BlockSpec((K, th, C), lambda i, c: (0, i, 0)))
    return pl.pallas_call(body, name=name, grid_spec=grid_spec,
                          out_shape=jax.ShapeDtypeStruct((K, H, C), bf16),
                          compiler_params=_params(("parallel",)))(c_idx, grads, other)


SEM = pl.BlockSpec(memory_space=pltpu.SEMAPHORE)
ANY = pl.BlockSpec(memory_space=pl.ANY)
EFFECT = pltpu.SideEffectType.DATAFLOW_SIDE_EFFECTING


def _copies_start(name, srcs, land_shapes, n_copies, make_copies, after):
    ns, nl = len(srcs), len(land_shapes)
    lands = [lax.empty(s.shape, s.dtype) for s in land_shapes]

    def body(*refs):
        src_refs, land_refs = refs[:ns], refs[ns:ns + nl]
        send, recv, token = refs[ns + nl + 1], refs[ns + nl + 2], refs[-1]
        for cp in make_copies(src_refs, land_refs, send, recv):
            cp.start()
        token[...] = jnp.zeros_like(token)

    buffers = list(srcs) + lands
    out = pl.pallas_call(
        body, name=name,
        out_shape=(pltpu.SemaphoreType.DMA((n_copies,)), pltpu.SemaphoreType.DMA((n_copies,)),
                   *[pltpu.HBM(b.shape, b.dtype) for b in buffers], jax.ShapeDtypeStruct((8, LANES), f32)),
        in_specs=[HBM] * (ns + nl) + [ANY],
        out_specs=(SEM, SEM, *[HBM] * (ns + nl), pl.BlockSpec(memory_space=pltpu.VMEM)),
        input_output_aliases={i: 2 + i for i in range(ns + nl)},
        compiler_params=pltpu.CompilerParams(has_side_effects=EFFECT),
    )(*[pltpu.with_memory_space_constraint(b, pltpu.HBM) for b in buffers], after)
    return out[0], out[1], list(out[2:2 + ns]), list(out[2 + ns:2 + ns + nl]), out[-1]


def _copies_wait(name, started, make_copies, after):
    send, recv, srcs, lands, _ = started
    ns, nl = len(srcs), len(lands)
    after = list(after)

    def body(*refs):
        src_refs, land_refs = refs[:ns], refs[ns:ns + nl]
        for cp in make_copies(src_refs, land_refs, refs[ns + nl], refs[ns + nl + 1]):
            cp.wait_send()
            cp.wait_recv()

    buffers = list(srcs) + list(lands)
    out = pl.pallas_call(
        body, name=name, out_shape=tuple(pltpu.HBM(b.shape, b.dtype) for b in buffers),
        in_specs=[HBM] * (ns + nl) + [SEM, SEM] + [ANY] * len(after), out_specs=tuple([HBM] * (ns + nl)),
        input_output_aliases={i: i for i in range(ns + nl)},
        compiler_params=pltpu.CompilerParams(has_side_effects=EFFECT),
    )(*buffers, send, recv, *after)
    return list(out[:ns]), list(out[ns:])


def _gather_copies(src_refs, land_refs, send, recv):
    x, y, c = _me()
    my_chip = 2 * x + y
    peers = [(*chip, c) for chip in _other_chips(x, y)] + [(x, y, 1 - c)]
    return [_remote(src_refs[i], land_refs[i].at[my_chip], send.at[4 * i + j], recv.at[4 * i + j], peer)
            for i in range(len(src_refs)) for j, peer in enumerate(peers)]


def _partial_copies(src_refs, land_refs, send, recv):
    x, y, c = _me()
    return [_remote(src_refs[i].at[2 * chip[0] + chip[1]], land_refs[i].at[j], send.at[3 * i + j], recv.at[3 * i + j], (*chip, c))
            for i in range(len(src_refs)) for j, chip in enumerate(_other_chips(x, y))]


def _small_copies(src_refs, land_refs, send, recv):
    x, y, c = _me()
    return [_remote(src_refs[0], land_refs[0].at[k - 1], send.at[k - 1], recv.at[k - 1], (x ^ (k >> 2), y ^ ((k >> 1) & 1), c ^ (k & 1)))
            for k in range(1, N_DEV)]


def _sum_small(own, slots, dev_idx):
    R, C = own.shape

    def body(dev_ref, own_ref, s_ref, o_ref):
        me = dev_ref[0]
        acc = jnp.zeros((R, C), f32)
        for d in range(N_DEV):
            k = me ^ d
            acc = acc + jnp.where(k == 0, own_ref[...], s_ref[jnp.maximum(k - 1, 0)])
        o_ref[...] = acc

    grid_spec = pltpu.PrefetchScalarGridSpec(
        num_scalar_prefetch=1, grid=(1,),
        in_specs=[pl.BlockSpec((R, C), lambda i, dev: (0, 0)), pl.BlockSpec((N_DEV - 1, R, C), lambda i, dev: (0, 0, 0))],
        out_specs=pl.BlockSpec((R, C), lambda i, dev: (0, 0)))
    return pl.pallas_call(body, name="sum_small", grid_spec=grid_spec, out_shape=jax.ShapeDtypeStruct((R, C), f32),
                          compiler_params=_params(("arbitrary",)))(dev_idx, own, slots)


def _sum_partials(name, part, recv, chip_idx, th=HALF_ROWS_BF16):
    K, H, C = part.shape
    th = _row_tile(H, th, 16)

    def body(chip_ref, p_ref, r_ref, o_ref):
        acc = p_ref[...].astype(f32)
        for j in range(3):
            acc = acc + r_ref[j].astype(f32)
        o_ref[...] = acc

    grid_spec = pltpu.PrefetchScalarGridSpec(
        num_scalar_prefetch=1, grid=(H // th,),
        in_specs=[pl.BlockSpec((None, th, C), lambda i, chip: (chip[0], i, 0)),
                  pl.BlockSpec((3, th, C), lambda i, chip: (0, i, 0))],
        out_specs=pl.BlockSpec((th, C), lambda i, chip: (i, 0)))
    return pl.pallas_call(body, name=name, grid_spec=grid_spec, out_shape=jax.ShapeDtypeStruct((H, C), f32),
                          compiler_params=_params(("parallel",)))(chip_idx, part, recv)


def _share_halves(name, halves):
    n = len(halves)

    def body(*refs):
        ins, outs, send, recv = refs[:n], refs[n:2 * n], refs[2 * n], refs[2 * n + 1]
        x, y, c = _me()
        copies = []
        for i in range(n):
            cp = _remote(ins[i], outs[i], send.at[i], recv.at[i], (x, y, 1 - c))
            cp.start()
            copies.append(cp)
        for cp in copies:
            cp.wait()

    return pl.pallas_call(
        body, name=name, out_shape=[jax.ShapeDtypeStruct(h.shape, h.dtype) for h in halves],
        in_specs=[HBM] * n, out_specs=[HBM] * n, scratch_shapes=_sems(n))(*halves)


def _adamw_big(name, w, g_mine, g_theirs, m, v, c_idx, tr=HALF_ROWS_F32):
    R, C = w.shape
    H = R // 2
    tr = _row_tile(H, tr)
    nb = H // tr

    def body(c_ref, w_ref, gm_ref, gt_ref, m_ref, v_ref, g_ref, d_ref, mo_ref, vo_ref):
        g_ = jnp.where(pl.program_id(0) // nb == c_ref[0], gm_ref[...], gt_ref[...])
        g_ref[...] = g_
        m_ = ADAM_B1 * m_ref[...] + (1.0 - ADAM_B1) * g_
        v_ = ADAM_B2 * v_ref[...] + (1.0 - ADAM_B2) * jnp.square(g_)
        m_hat = m_ / (1.0 - ADAM_B1 ** ADAM_STEP)
        v_hat = v_ / (1.0 - ADAM_B2 ** ADAM_STEP)
        d_ref[...] = -ADAM_LR * (m_hat / (jnp.sqrt(v_hat) + ADAM_EPS) + ADAM_WD * w_ref[...])
        mo_ref[...] = m_
        vo_ref[...] = v_

    full = pl.BlockSpec((tr, C), lambda i, c: (i, 0))
    half = pl.BlockSpec((tr, C), lambda i, c: (i % nb, 0))
    grid_spec = pltpu.PrefetchScalarGridSpec(num_scalar_prefetch=1, grid=(2 * nb,),
                                             in_specs=[full, half, half, full, full], out_specs=[full] * 4)
    return pl.pallas_call(body, name=name, grid_spec=grid_spec, out_shape=[jax.ShapeDtypeStruct((R, C), f32)] * 4,
                          compiler_params=_params(("parallel",)))(c_idx, w, g_mine, g_theirs, m, v)


BIG = ("ffn1_w_gate", "ffn1_w_up", "ffn1_w_down", "w_in", "w_out", "ffn2_w_gate", "ffn2_w_up", "ffn2_w_down",
       "ple_w_gate", "ple_w_proj")


SMALL = ("ffn1_norm", "mix_norm", "gm_ln_g", "gm_ln_b", "gm_w_s", "gm_b_s", "gm_out_norm", "conv_b", "dt_bias", "a_log",
         "d_skip", "ssm_norm", "ffn2_norm", "ple_norm", "ple_b_gate", "final_norm")
SMALL_C = 1024


def _pack_small(vals):
    parts = []
    for v in vals:
        f = v.astype(f32).reshape(-1)
        parts.append(jnp.pad(f, (0, -f.shape[0] % SMALL_C)))
    flat = jnp.concatenate(parts)
    rows = flat.shape[0] // SMALL_C
    return jnp.pad(flat, (0, (-rows % 8) * SMALL_C)).reshape(-1, SMALL_C)


def _unpack_small(pack, shapes):
    flat = pack.reshape(-1)
    out, off = [], 0
    for s in shapes:
        n = 1
        for d in s:
            n *= d
        out.append(flat[off:off + n].reshape(s))
        off += n + (-n % SMALL_C)
    return out


def _pad_lanes(v):
    return jnp.pad(v, ((0, 0), (0, LANES - v.shape[1])))


def _pad_rows(a):
    pad = [(0, 0)] * a.ndim
    pad[-2] = (0, -a.shape[-2] % ROW_PAD)
    return jnp.pad(a, pad) if pad[-2][1] else a


def _rows_between(parts, lo, hi, pad=0):
    pieces, off = [], 0
    for p in parts:
        a, b = max(lo, off), min(hi, off + p.shape[0])
        if a < b:
            pieces.append(p[a - off:b - off])
        off += p.shape[0]
    return pieces + ([jnp.zeros((pad,) + parts[0].shape[1:], parts[0].dtype)] if pad else [])


FETCH = (("ffn1_w_gate", "ffn1_w_up", "ffn1_w_down"), ("w_in", "conv_w", "w_out"),
         ("ffn2_w_gate", "ffn2_w_up", "ffn2_w_down", "ple_w_gate", "ple_w_proj"))
TRANSPOSED = ("ffn1_w_gate", "ffn1_w_up", "ffn2_w_gate", "ffn2_w_up", "w_in")
ROW_PAD = 32
DONE = (("ffn2_w_gate", "ffn2_w_up", "ffn2_w_down", "w_out", "ple_w_gate", "ple_w_proj"), ("w_in",),
        ("ffn1_w_gate", "ffn1_w_up", "ffn1_w_down"))


def _local_step(x, p, tgt, fetch, S, on_grads):
    G = GM_WIDTH
    K = N_CHIPS
    b_st = S["gm_b_s"][0].T
    w_s = S["gm_w_s"][0]
    dtb, alog, dsk = _pad_lanes(S["dt_bias"]), _pad_lanes(S["a_log"]), _pad_lanes(S["d_skip"])
    gfin = S["final_norm"].reshape(1, -1)

    def rows(a):
        return a.reshape(-1, D_MODEL)

    def shards(a):
        return a.reshape(K, -1, D_MODEL)

    wg1, wu1, wd1 = [rows(a) for a in fetch(0, None)]
    h1, n1, a1, b1 = _ffn_fwd("ffn1_fwd", x, S["ffn1_norm"], wg1, wu1, wd1)
    w_in4, cw4, wo4 = fetch(1, h1)
    w_in = [w_in4[j] for j in range(K)]
    w_uv = jnp.concatenate(_rows_between(w_in, 0, 2 * G))
    w_zxd = jnp.concatenate(_rows_between(w_in, 2 * G, IN_PROJ, pad=ZXD - (IN_PROJ - 2 * G)))
    conv_w = jnp.transpose(cw4, (1, 0, 2)).reshape(SSM_CONV, CONV_DIM)
    wo = wo4.reshape(-1, D_MODEL)
    n2, act, slope, z, xbc, dtr, ya = _mix_fwd(h1, S["mix_norm"], w_uv, w_zxd, S["gm_ln_g"], S["gm_ln_b"], w_s, b_st,
                                               S["gm_out_norm"])
    yb, xc, sg, y_ssd, sprev = _ssd_fwd(xbc, z, dtr, conv_w, S["conv_b"], dtb, alog, dsk, S["ssm_norm"])
    wg2, wu2, wd2, wpg4, wpp4 = fetch(2, yb)
    wg2, wu2, wd2 = rows(wg2), rows(wu2), rows(wd2)
    h2, h3, n3, a2, b2 = _ffn_fwd("ffn2_fwd", h1, S["ffn2_norm"], wg2, wu2, wd2, pre=(ya, yb, wo))
    dh3, loss, dgp, dwpg, dbpg, dwpp, dgf = _tail(h3, p, tgt, S["ple_norm"], wpg4.reshape(-1, D_MODEL), S["ple_b_gate"], wpp4, gfin)
    dh2, da2, db2, hm2, dg_ffn2, dya, dyb = _ffn_bwd("ffn2_bwd", dh3, h2, S["ffn2_norm"], a2, b2, wg2, wu2, wd2, wo=wo, ga=G)
    dw_out = jnp.concatenate([_matmul_tn("dw_out_a", ya, dh2), _matmul_tn("dw_out_b", yb, dh2)], axis=0).reshape(wo4.shape)
    zero = on_grads(0, [shards(_matmul_tn("dw_ffn2_gate", da2, n3)), shards(_matmul_tn("dw_ffn2_up", db2, n3)),
                        shards(_matmul_tn("dw_ffn2_down", hm2, dh3, scale=0.5)), dw_out,
                        dwpg.astype(bf16).reshape(wpg4.shape), dwpp.astype(bf16)])
    dzxd, dcw, dcb, ddtb, dalog, ddsk, dgssm = _ssd_bwd(xbc, xc, sg, y_ssd, z, dtr, sprev, dyb, conv_w, S["conv_b"], dtb, alog, dsk,
                                                        S["ssm_norm"] + zero)
    dh1, duv, dg_mix, dlng, dlnb, dws, dbst, dgout = _mix_bwd(dh2, h1, S["mix_norm"], act, slope, dya, dzxd, w_uv, w_zxd, S["gm_ln_g"],
                                                              S["gm_ln_b"], w_s, b_st, S["gm_out_norm"])
    dw_in = jnp.concatenate([_matmul_tn("dw_in_uv", duv, n2), _matmul_tn("dw_in_zxd", dzxd, n2)[:IN_PROJ - 2 * G]], axis=0)
    zero = on_grads(1, [dw_in.reshape(w_in4.shape)])
    dx, da1, db1, hm1, dg_ffn1 = _ffn_bwd("ffn1_bwd", dh1, x, S["ffn1_norm"] + zero, a1, b1, wg1, wu1, wd1)
    zero = on_grads(2, [shards(_matmul_tn("dw_ffn1_gate", da1, n1)), shards(_matmul_tn("dw_ffn1_up", db1, n1)),
                        shards(_matmul_tn("dw_ffn1_down", hm1, dh1, scale=0.5))])
    loss = loss + zero
    nh = SSM_HEADS
    gS = {"ffn1_norm": dg_ffn1, "mix_norm": dg_mix, "gm_ln_g": dlng, "gm_ln_b": dlnb, "gm_w_s": dws[None], "gm_b_s": dbst.T[None],
          "gm_out_norm": dgout, "conv_b": dcb, "dt_bias": ddtb[:, :nh], "a_log": dalog[:, :nh], "d_skip": ddsk[:, :nh],
          "ssm_norm": dgssm, "ffn2_norm": dg_ffn2, "ple_norm": dgp, "ple_b_gate": dbpg, "final_norm": dgf.reshape(-1)}
    return loss, dx, dcw, gS


_WEIGHTS = ("ffn1_norm", "ffn1_w_gate", "ffn1_w_up", "ffn1_w_down", "mix_norm", "w_in", "gm_ln_g", "gm_ln_b", "gm_w_s", "gm_b_s",
            "gm_out_norm", "conv_w", "conv_b", "dt_bias", "a_log", "d_skip", "ssm_norm", "w_out", "ffn2_norm", "ffn2_w_gate",
            "ffn2_w_up", "ffn2_w_down", "ple_norm", "ple_w_gate", "ple_b_gate", "ple_w_proj", "final_norm")
_BIG_NAMES = BIG


def kernel(x, p, ffn1_norm, ffn1_w_gate, ffn1_w_up, ffn1_w_down, mix_norm, w_in, gm_ln_g, gm_ln_b, gm_w_s, gm_b_s, gm_out_norm, conv_w, conv_b, dt_bias, a_log, d_skip, ssm_norm, w_out, ffn2_norm, ffn2_w_gate, ffn2_w_up, ffn2_w_down, ple_norm, ple_w_gate, ple_b_gate, ple_w_proj, final_norm, loss_target, m_ffn1_norm, m_ffn1_w_gate, m_ffn1_w_up, m_ffn1_w_down, m_mix_norm, m_w_in, m_gm_ln_g, m_gm_ln_b, m_gm_w_s, m_gm_b_s, m_gm_out_norm, m_conv_w, m_conv_b, m_dt_bias, m_a_log, m_d_skip, m_ssm_norm, m_w_out, m_ffn2_norm, m_ffn2_w_gate, m_ffn2_w_up, m_ffn2_w_down, m_ple_norm, m_ple_w_gate, m_ple_b_gate, m_ple_w_proj, m_final_norm, v_ffn1_norm, v_ffn1_w_gate, v_ffn1_w_up, v_ffn1_w_down, v_mix_norm, v_w_in, v_gm_ln_g, v_gm_ln_b, v_gm_w_s, v_gm_b_s, v_gm_out_norm, v_conv_w, v_conv_b, v_dt_bias, v_a_log, v_d_skip, v_ssm_norm, v_w_out, v_ffn2_norm, v_ffn2_w_gate, v_ffn2_w_up, v_ffn2_w_down, v_ple_norm, v_ple_w_gate, v_ple_b_gate, v_ple_w_proj, v_final_norm):
    given = dict(locals())
    w = {n: given[n] for n in _WEIGHTS}
    m = {n: given["m_" + n] for n in _WEIGHTS}
    v = {n: given["v_" + n] for n in _WEIGHTS}

    c_idx = lax.axis_index("c").astype(jnp.int32).reshape(1)
    chip = 2 * lax.axis_index("x") + lax.axis_index("y")
    chip_idx = chip.astype(jnp.int32).reshape(1)

    shard = {n: (jnp.swapaxes(w[n][0], 0, 1) if n in TRANSPOSED else w[n][0]).astype(bf16) for n in BIG}
    shard["conv_w"] = w["conv_w"][0]
    first = _gather_weights([shard[n] for n in FETCH[0]], [True] * len(FETCH[0]))
    fetching, after = [], first[-1]
    for k in (1, 2):
        srcs = [shard[n] for n in FETCH[k]]
        lands = [jax.ShapeDtypeStruct((N_CHIPS,) + s.shape, s.dtype) for s in srcs]
        fetching.append(_copies_start("gather%d_start" % k, srcs, lands, 4 * len(srcs), _gather_copies, after))
        after = fetching[-1][4]

    def fetch(k, after_):
        return first if k == 0 else _copies_wait("gather%d_wait" % k, fetching[k - 1], _gather_copies, [after_])[1]

    exchanging = []

    def on_grads(k, grads):
        grads = [_pad_rows(g_) for g_ in grads]
        others = _swap_halves("swap%d" % k, grads)
        parts = [_add_halves("add_" + n, g_, o_, c_idx) for n, g_, o_ in zip(DONE[k], grads, others)]
        lands = [jax.ShapeDtypeStruct((3,) + p_.shape[1:], p_.dtype) for p_ in parts]
        exchanging.append(_copies_start("exchange%d_start" % k, parts, lands, 3 * len(parts), _partial_copies, c_idx))
        return exchanging[-1][4][0, 0]

    S = {n: w[n] for n in SMALL}
    S["ffn1_norm"] = S["ffn1_norm"] + after[0, 0]
    loss, dx, dcw, gS = _local_step(x[0], p[0, 0], loss_target[0], fetch, S, on_grads)

    small = _pack_small([gS[n] for n in SMALL] + [dcw, loss[:, :1]])
    small_lands = [jax.ShapeDtypeStruct((N_DEV - 1,) + small.shape, small.dtype)]
    small_st = _copies_start("small_start", [small], small_lands, N_DEV - 1, _small_copies, c_idx)

    g, delta, new_m, new_v = {}, {}, {}, {}
    after = [small_st[4]]
    for k in range(len(DONE)):
        parts, recv = _copies_wait("exchange%d_wait" % k, exchanging[k], _partial_copies, after)
        mine = [_sum_partials("sum_" + n, p_, r_, chip_idx) for n, p_, r_ in zip(DONE[k], parts, recv)]
        theirs = _share_halves("share%d" % k, mine)
        after = []
        for n, gm_, gt_ in zip(DONE[k], mine, theirs):
            flip = (lambda a: jnp.swapaxes(a, 0, 1)) if n in TRANSPOSED else (lambda a: a)
            rows = flip(w[n][0]).shape[0]
            if rows % ROW_PAD:
                def lin(a):
                    return jnp.transpose(a.reshape(-1, LANES, rows), (2, 0, 1))

                def back(a):
                    return jnp.transpose(a, (1, 2, 0)).reshape(1, -1, rows)

                gm_, gt_ = [a.reshape(a.shape[0], -1, LANES) for a in (gm_, gt_)]
                g_ = jnp.where(c_idx[0] == 0, jnp.concatenate([gm_, gt_]), jnp.concatenate([gt_, gm_]))[:rows]
                outs = [g_, *_adamw("adamw_" + n, lin(w[n]), g_, lin(m[n]), lin(v[n]), tr=HALF_ROWS_F32)]
                g[n], delta[n], new_m[n], new_v[n] = [back(o) for o in outs]
            else:
                outs = _adamw_big("adamw_" + n, flip(w[n][0]), gm_, gt_, flip(m[n][0]), flip(v[n][0]), c_idx)
                g[n], delta[n], new_m[n], new_v[n] = [flip(o)[None] for o in outs]
            after.append(outs[3])
    (own,), (slots,) = _copies_wait("small_wait", small_st, _small_copies, after)
    dev_idx = (2 * chip + lax.axis_index("c")).astype(jnp.int32).reshape(1)
    small_shapes = [w[n].shape for n in SMALL] + [dcw.shape, (1, 1)]
    small_sum = _unpack_small(_sum_small(own, slots, dev_idx), small_shapes)
    g.update({n: small_sum[i] for i, n in enumerate(SMALL)})
    cshard = w["conv_w"].shape[2]
    g["conv_w"] = lax.dynamic_slice_in_dim(small_sum[len(SMALL)], chip * cshard, cshard, axis=1)[None]
    loss_total = small_sum[len(SMALL) + 1].reshape(())
    sm_names = SMALL + ("conv_w",)
    sm_shapes = [w[n].shape for n in sm_names]
    d_s, m_s, v_s = _adamw("adamw_small", _pack_small([w[n] for n in sm_names]), _pack_small([g[n] for n in sm_names]),
                           _pack_small([m[n] for n in sm_names]), _pack_small([v[n] for n in sm_names]))
    for dst, src in ((delta, d_s), (new_m, m_s), (new_v, v_s)):
        for n, val in zip(sm_names, _unpack_small(src, sm_shapes)):
            dst[n] = val

    return (loss_total, dx[None], *[g[n] for n in _WEIGHTS], *[delta[n] for n in _WEIGHTS],
            *[new_m[n] for n in _WEIGHTS], *[new_v[n] for n in _WEIGHTS])
```

```python
import jax
import jax.numpy as jnp
from jax import lax
from jax.experimental import pallas as pl
from jax.experimental.pallas import tpu as pltpu

f32 = jnp.float32
bf16 = jnp.bfloat16
MESH = pl.DeviceIdType.MESH
HIGHEST = lax.Precision.HIGHEST

EPS = 1e-6
N_CHIPS = 4
N_DEV = 8
D_MODEL = 1024
GM_WIDTH = 1024
GM_HEADS = 8
CHUNK = 128
SSM_WIDTH = 1024
SSM_HEADS = 16
SSM_HEAD_DIM = 64
SSM_GROUPS = 2
SSM_STATE = 128
SSM_CONV = 4
CONV_DIM = SSM_WIDTH + 2 * SSM_GROUPS * SSM_STATE
IN_PROJ = 2 * GM_WIDTH + SSM_WIDTH + CONV_DIM + SSM_HEADS
LANES = 128
ZXD = SSM_WIDTH + CONV_DIM + LANES

ADAM_LR = 0.001
ADAM_B1 = 0.9
ADAM_B2 = 0.999
ADAM_EPS = 1e-08
ADAM_WD = 0.01
ADAM_STEP = 10

VMEM_LIMIT = 56 * 1024 * 1024
HALF_ROWS_BF16 = 592
HALF_ROWS_F32 = 320


def _dot(a, b):
    return jnp.dot(a, b, preferred_element_type=f32)


def _dot_nt(a, b):
    return lax.dot_general(a, b, (((1,), (1,)), ((), ())), preferred_element_type=f32)


def _dot_tn(a, b):
    return lax.dot_general(a, b, (((0,), (0,)), ((), ())), preferred_element_type=f32)


def _rms(x, g):
    return x * lax.rsqrt(jnp.mean(x * x, axis=-1, keepdims=True) + EPS) * g


def _layernorm(x, g, b):
    mu = jnp.mean(x, axis=-1, keepdims=True)
    xc = x - mu
    return xc * lax.rsqrt(jnp.mean(xc * xc, axis=-1, keepdims=True) + EPS) * g + b


def _sigmoid(x):
    return 1.0 / (1.0 + jnp.exp(-x))


def _softplus(x):
    return jnp.maximum(x, 0.0) + jnp.log(1.0 + jnp.exp(-jnp.abs(x)))


def _full(shape):
    nd = len(shape)
    return pl.BlockSpec(shape, lambda *_: (0,) * nd, pipeline_mode=pl.Buffered(1))


def _acc(shape):
    nd = len(shape)
    return pl.BlockSpec(shape, lambda *_: (0,) * nd)


def _rows(tm, ncols):
    return pl.BlockSpec((tm, ncols), lambda i: (i, 0))


def _params(sem):
    return pltpu.CompilerParams(dimension_semantics=sem, vmem_limit_bytes=VMEM_LIMIT)


def _row_tile(rows, target, mult=8):
    best = rows
    for t in range(mult, min(rows, target) + 1, mult):
        if rows % t == 0:
            best = t
    return best if best <= target else rows


def _ffn_fwd(name, h, g, wg, wu, wd, pre=None, tm=256):
    T, D = h.shape
    F = wg.shape[0]
    tm = min(tm, T)

    def body(*refs):
        if pre is None:
            h_ref, g_ref, wg_ref, wu_ref, wd_ref, ho_ref, n_ref, a_ref, b_ref = refs
            hin = h_ref[...]
        else:
            (h_ref, ya_ref, yb_ref, wo_ref, g_ref, wg_ref, wu_ref, wd_ref,
             hi_ref, ho_ref, n_ref, a_ref, b_ref) = refs
            ga = ya_ref.shape[1]
            hin = h_ref[...] + _dot(ya_ref[...], wo_ref[:ga, :]) + _dot(yb_ref[...], wo_ref[ga:, :])
            hi_ref[...] = hin
        n = _rms(hin, g_ref[...]).astype(bf16)
        n_ref[...] = n
        a = _dot_nt(n, wg_ref[...]).astype(bf16)
        b = _dot_nt(n, wu_ref[...]).astype(bf16)
        a_ref[...] = a
        b_ref[...] = b
        af = a.astype(f32)
        hm = (af * _sigmoid(af) * b.astype(f32)).astype(bf16)
        ho_ref[...] = hin + 0.5 * _dot(hm, wd_ref[...])

    ins = [h] + (list(pre) if pre is not None else []) + [g, wg, wu, wd]
    in_specs = [_rows(tm, D)]
    if pre is not None:
        in_specs += [_rows(tm, pre[0].shape[1]), _rows(tm, pre[1].shape[1]), _full(pre[2].shape)]
    in_specs += [_full(g.shape), _full(wg.shape), _full(wu.shape), _full(wd.shape)]
    outs = [jax.ShapeDtypeStruct((T, D), f32), jax.ShapeDtypeStruct((T, D), bf16),
            jax.ShapeDtypeStruct((T, F), bf16), jax.ShapeDtypeStruct((T, F), bf16)]
    out_specs = [_rows(tm, D), _rows(tm, D), _rows(tm, F), _rows(tm, F)]
    if pre is not None:
        outs = [jax.ShapeDtypeStruct((T, D), f32)] + outs
        out_specs = [_rows(tm, D)] + out_specs
    return pl.pallas_call(body, name=name, grid=(T // tm,), in_specs=in_specs, out_specs=out_specs,
                          out_shape=outs, compiler_params=_params(("parallel",)))(*ins)


def _ffn_bwd(name, dh, hin, g, a, b, wg, wu, wd, wo=None, ga=0, tm=256):
    T, D = dh.shape
    F = wg.shape[0]
    tm = min(tm, T)

    def body(*refs):
        if wo is None:
            (dh_ref, hin_ref, g_ref, a_ref, b_ref, wg_ref, wu_ref, wd_ref,
             dhi_ref, da_ref, db_ref, hm_ref, dg_ref) = refs
        else:
            (dh_ref, hin_ref, g_ref, a_ref, b_ref, wg_ref, wu_ref, wd_ref, wo_ref,
             dhi_ref, da_ref, db_ref, hm_ref, dg_ref, dya_ref, dyb_ref) = refs

        @pl.when(pl.program_id(0) == 0)
        def _():
            dg_ref[...] = jnp.zeros_like(dg_ref)

        dh_ = dh_ref[...]
        dhb = (0.5 * dh_).astype(bf16)
        dhm = _dot_nt(dhb, wd_ref[...])
        af = a_ref[...].astype(f32)
        bf = b_ref[...].astype(f32)
        sg = _sigmoid(af)
        sl_ = af * sg
        da = (dhm * bf * (sg * (1.0 + af * (1.0 - sg)))).astype(bf16)
        db = (dhm * sl_).astype(bf16)
        da_ref[...] = da
        db_ref[...] = db
        hm_ref[...] = (sl_ * bf).astype(bf16)
        dn = _dot(da, wg_ref[...]) + _dot(db, wu_ref[...])
        _, vjp = jax.vjp(_rms, hin_ref[...], g_ref[...])
        dx, dg = vjp(dn)
        dhi = dh_ + dx
        dhi_ref[...] = dhi
        dg_ref[...] += dg
        if wo is not None:
            dhib = dhi.astype(bf16)
            dya_ref[...] = _dot_nt(dhib, wo_ref[:ga, :]).astype(bf16)
            dyb_ref[...] = _dot_nt(dhib, wo_ref[ga:, :]).astype(bf16)

    ins = [dh, hin, g, a, b, wg, wu, wd]
    in_specs = [_rows(tm, D), _rows(tm, D), _full(g.shape), _rows(tm, F), _rows(tm, F),
                _full(wg.shape), _full(wu.shape), _full(wd.shape)]
    act = jax.ShapeDtypeStruct((T, F), bf16)
    outs = [jax.ShapeDtypeStruct((T, D), f32), act, act, act, jax.ShapeDtypeStruct(g.shape, f32)]
    out_specs = [_rows(tm, D), _rows(tm, F), _rows(tm, F), _rows(tm, F), _acc(g.shape)]
    if wo is not None:
        gb = wo.shape[0] - ga
        ins += [wo]
        in_specs += [_full(wo.shape)]
        outs += [jax.ShapeDtypeStruct((T, ga), bf16), jax.ShapeDtypeStruct((T, gb), bf16)]
        out_specs += [_rows(tm, ga), _rows(tm, gb)]
    return pl.pallas_call(body, name=name, grid=(T // tm,), in_specs=in_specs, out_specs=out_specs,
                          out_shape=outs, compiler_params=_params(("arbitrary",)))(*ins)


def _matmul_tn(name, a, b, scale=1.0, tk=2048):
    T, M = a.shape
    N = b.shape[1]
    tk = min(tk, T)
    nk = T // tk
    tn = LANES * max(d for d in range(1, N // LANES + 1) if (N // LANES) % d == 0 and (d == 1 or M * d * LANES * 4 <= 6 * 1024 * 1024))

    def body(a_ref, b_ref, o_ref, acc):
        k = pl.program_id(1)

        @pl.when(k == 0)
        def _():
            acc[...] = jnp.zeros_like(acc)

        bb = b_ref[...]
        if scale != 1.0:
            bb = bb * scale
        acc[...] += _dot_tn(a_ref[...].astype(bf16), bb.astype(bf16))

        @pl.when(k == nk - 1)
        def _():
            o_ref[...] = acc[...].astype(bf16)

    return pl.pallas_call(
        body, name=name, grid=(N // tn, nk),
        in_specs=[pl.BlockSpec((tk, M), lambda j, k: (k, 0)), pl.BlockSpec((tk, tn), lambda j, k: (k, j))],
        out_specs=pl.BlockSpec((M, tn), lambda j, k: (0, j)),
        out_shape=jax.ShapeDtypeStruct((M, N), bf16), scratch_shapes=[pltpu.VMEM((M, tn), f32)],
        compiler_params=_params(("parallel", "arbitrary")))(a, b)


def _gelu_and_slope(x):
    cdf = 0.5 * (1.0 + lax.erf(x * 0.7071067811865476))
    return x * cdf, cdf + x * (0.3989422804014327 * jnp.exp(-0.5 * x * x))


def _tril_mask():
    r = lax.broadcasted_iota(jnp.int32, (CHUNK, CHUNK), 0)
    c = lax.broadcasted_iota(jnp.int32, (CHUNK, CHUNK), 1)
    return c <= r


def _gm_mix(vnb, ws_ref, bst, mixed_sc, tm):
    mask = _tril_mask()
    for h in range(GM_HEADS):
        wt = jnp.where(mask, ws_ref[h], 0.0).astype(bf16)
        bias = bst[:, h:h + 1]
        for q in range(tm // CHUNK):
            rs = slice(q * CHUNK, (q + 1) * CHUNK)
            cs = slice(h * CHUNK, (h + 1) * CHUNK)
            mixed_sc[rs, cs] = _dot(wt, vnb[rs, cs]) + bias


def _mix_fwd(h1, gmix, w_uv, w_zxd, ln_g, ln_b, w_s, b_st, gout, tm=512):
    T, D = h1.shape
    tm = min(tm, T)
    G = GM_WIDTH

    def body(h_ref, g_ref, wuv_ref, wzxd_ref, lng_ref, lnb_ref, ws_ref, bst_ref, gout_ref,
             n_ref, act_ref, slope_ref, z_ref, xbc_ref, dt_ref, ya_ref, mixed_sc):
        n = _rms(h_ref[...], g_ref[...]).astype(bf16)
        n_ref[...] = n
        uv = _dot_nt(n, wuv_ref[...]).astype(bf16)
        zxd = _dot_nt(n, wzxd_ref[...])
        z_ref[...] = zxd[:, :SSM_WIDTH].astype(bf16)
        xbc_ref[...] = zxd[:, SSM_WIDTH:SSM_WIDTH + CONV_DIM].astype(bf16)
        dt_ref[...] = zxd[:, SSM_WIDTH + CONV_DIM:]
        act, slope = _gelu_and_slope(uv.astype(f32))
        act = act.astype(bf16)
        act_ref[...] = act
        slope_ref[...] = slope.astype(bf16)
        ug, vg = act[:, :G].astype(f32), act[:, G:].astype(f32)
        _gm_mix(_layernorm(vg, lng_ref[...], lnb_ref[...]).astype(bf16), ws_ref, bst_ref[...], mixed_sc, tm)
        ya_ref[...] = _rms(ug * mixed_sc[...], gout_ref[...]).astype(bf16)

    ins = [h1, gmix, w_uv, w_zxd, ln_g, ln_b, w_s, b_st, gout]
    in_specs = [_rows(tm, D)] + [_full(x.shape) for x in ins[1:]]
    outs = [jax.ShapeDtypeStruct((T, D), bf16), jax.ShapeDtypeStruct((T, 2 * G), bf16), jax.ShapeDtypeStruct((T, 2 * G), bf16),
            jax.ShapeDtypeStruct((T, SSM_WIDTH), bf16), jax.ShapeDtypeStruct((T, CONV_DIM), bf16),
            jax.ShapeDtypeStruct((T, LANES), f32), jax.ShapeDtypeStruct((T, G), bf16)]
    out_specs = [_rows(tm, D), _rows(tm, 2 * G), _rows(tm, 2 * G), _rows(tm, SSM_WIDTH), _rows(tm, CONV_DIM), _rows(tm, LANES),
                 _rows(tm, G)]
    return pl.pallas_call(body, name="mix_fwd", grid=(T // tm,), in_specs=in_specs, out_specs=out_specs,
                          out_shape=outs, scratch_shapes=[pltpu.VMEM((tm, G), f32)],
                          compiler_params=_params(("parallel",)))(*ins)


def _mix_bwd(dh, h1, gmix, act, slope, dya, dzxd, w_uv, w_zxd, ln_g, ln_b, w_s, b_st, gout, tm=256):
    T, D = dh.shape
    tm = min(tm, T)
    G = GM_WIDTH

    def body(dh_ref, h_ref, g_ref, act_ref, slope_ref, dya_ref, dzxd_ref, wuv_ref, wzxd_ref, lng_ref, lnb_ref, ws_ref,
             bst_ref, gout_ref, dhi_ref, duv_ref, dg_ref, dlng_ref, dlnb_ref, dws_ref, dbst_ref, dgout_ref, mixed_sc, dvn_sc):
        @pl.when(pl.program_id(0) == 0)
        def _():
            for r in (dg_ref, dlng_ref, dlnb_ref, dws_ref, dbst_ref, dgout_ref):
                r[...] = jnp.zeros_like(r)

        dn_z = _dot(dzxd_ref[...], wzxd_ref[...])
        ug = act_ref[:, :G].astype(f32)
        vn, ln_vjp = jax.vjp(_layernorm, act_ref[:, G:].astype(f32), lng_ref[...], lnb_ref[...])
        vnb = vn.astype(bf16)
        _gm_mix(vnb, ws_ref, bst_ref[...], mixed_sc, tm)
        mixed = mixed_sc[...]
        _, out_vjp = jax.vjp(_rms, ug * mixed, gout_ref[...])
        dpre, dgout = out_vjp(dya_ref[...].astype(f32))
        dgout_ref[...] += dgout
        dug = dpre * mixed
        dmixed = dpre * ug
        mask = _tril_mask()
        lane = lax.broadcasted_iota(jnp.int32, (1, GM_HEADS), 1)
        dbst = jnp.zeros((CHUNK, GM_HEADS), f32)
        for h in range(GM_HEADS):
            wt = jnp.where(mask, ws_ref[h], 0.0).astype(bf16)
            cs = slice(h * CHUNK, (h + 1) * CHUNK)
            dw = jnp.zeros((CHUNK, CHUNK), f32)
            for q in range(tm // CHUNK):
                rs = slice(q * CHUNK, (q + 1) * CHUNK)
                dm = dmixed[rs, cs]
                dmb = dm.astype(bf16)
                dw = dw + _dot_nt(dmb, vnb[rs, cs])
                dbst = dbst + jnp.sum(dm, axis=1, keepdims=True) * (lane == h).astype(f32)
                dvn_sc[rs, cs] = _dot_tn(wt, dmb)
            dws_ref[h] += jnp.where(mask, dw, 0.0)
        dbst_ref[...] += dbst
        dvg, dlng, dlnb = ln_vjp(dvn_sc[...])
        duv = (jnp.concatenate([dug, dvg], axis=1) * slope_ref[...].astype(f32)).astype(bf16)
        duv_ref[...] = duv
        dlng_ref[...] += dlng
        dlnb_ref[...] += dlnb
        dn = dn_z + _dot(duv, wuv_ref[...])
        _, vjp = jax.vjp(_rms, h_ref[...], g_ref[...])
        dx, dg = vjp(dn)
        dhi_ref[...] = dh_ref[...] + dx
        dg_ref[...] += dg

    ins = [dh, h1, gmix, act, slope, dya, dzxd, w_uv, w_zxd, ln_g, ln_b, w_s, b_st, gout]
    in_specs = ([_rows(tm, D), _rows(tm, D), _full(gmix.shape), _rows(tm, 2 * G), _rows(tm, 2 * G), _rows(tm, G),
                 _rows(tm, dzxd.shape[1])] + [_full(x.shape) for x in ins[7:]])
    accs = (gmix, ln_g, ln_b, w_s, b_st, gout)
    outs = ([jax.ShapeDtypeStruct((T, D), f32), jax.ShapeDtypeStruct((T, 2 * G), bf16)]
            + [jax.ShapeDtypeStruct(x.shape, f32) for x in accs])
    out_specs = [_rows(tm, D), _rows(tm, 2 * G)] + [_acc(x.shape) for x in accs]
    return pl.pallas_call(body, name="mix_bwd", grid=(T // tm,), in_specs=in_specs, out_specs=out_specs,
                          out_shape=outs, scratch_shapes=[pltpu.VMEM((tm, G), f32), pltpu.VMEM((tm, G), f32)],
                          compiler_params=_params(("arbitrary",)))(*ins)


HALO = 16
SSD_SUB = 4


class _RowsOf:
    def __init__(self, ref, rows):
        self.ref, self.rows = ref, rows

    def _index(self, idx):
        return (self.rows, slice(None)) if idx is Ellipsis else (self.rows,) + tuple(idx[1:])

    def __getitem__(self, idx):
        return self.ref[self._index(idx)]

    def __setitem__(self, idx, value):
        self.ref[self._index(idx)] = value
PAIRS = SSM_HEADS // 2
PAIR_W = 2 * SSM_HEAD_DIM


def _split(x, n):
    parts = []
    for _ in range(n):
        p = x.astype(bf16)
        parts.append(p)
        x = x - p.astype(f32)
    return parts


def _dot_sel(x, sel_n, n):
    return _dot(jnp.concatenate(_split(x, n), axis=1), sel_n)


def _sel_dot(sel, x, n):
    return _dot(jnp.concatenate([sel] * n, axis=1), jnp.concatenate(_split(x, n), axis=0))


EXPAND_SPLIT = 3
REDUCE_SPLIT = 2


def _head_mats():
    ex = (jnp.arange(SSM_WIDTH)[None, :] // SSM_HEAD_DIM == jnp.arange(LANES)[:, None]).astype(bf16)
    return jnp.tile(ex, (EXPAND_SPLIT, 1)), jnp.tile(ex.T, (REDUCE_SPLIT, 1))


def _shift_mat(rows, cols, off):
    r = lax.broadcasted_iota(jnp.int32, (rows, cols), 0)
    c = lax.broadcasted_iota(jnp.int32, (rows, cols), 1)
    return (c == r + off).astype(bf16)


def _ssd_conv(halo, x, cw_ref, cb_ref):
    ext = jnp.concatenate([halo, x], axis=0)
    xc = cb_ref[...] + cw_ref[SSM_CONV - 1:SSM_CONV, :] * x.astype(f32)
    for j in range(SSM_CONV - 1):
        xc = xc + cw_ref[j:j + 1, :] * _dot(_shift_mat(CHUNK, HALO + CHUNK, HALO - SSM_CONV + 1 + j), ext)
    return xc


def _ssd_front(dtr, dtb_ref, alog_ref):
    dt = _softplus(dtr + dtb_ref[...])
    a = -jnp.exp(alog_ref[...])
    acs = jnp.dot(_tril_mask().astype(f32), dt * a, preferred_element_type=f32, precision=HIGHEST)
    return dt, a, acs


def _ssd_wide(xa, dt, acs, dsk, ex):
    dt_x = _dot_sel(dt, ex, EXPAND_SPLIT)
    acs_x = _dot_sel(acs, ex, EXPAND_SPLIT)
    dsk_x = _dot_sel(jnp.broadcast_to(dsk, (8, LANES)), ex, EXPAND_SPLIT)[0:1]
    e_x = jnp.exp(acs_x)
    r_x = jnp.exp(acs_x[CHUNK - 1:CHUNK, :] - acs_x)
    xs = xa[:, :SSM_WIDTH]
    xd = xs * dt_x
    return dt_x, dsk_x, e_x, r_x, xs, xd, xd * r_x


def _pair_stack(v, lo):
    return jnp.concatenate([jnp.where(lo, v, 0.0), jnp.where(lo, 0.0, v)], axis=0)


def _ssd_pair(j, acs, acs_t, cb):
    out = []
    tril = _tril_mask()
    for h in (2 * j, 2 * j + 1):
        dk = jnp.exp(jnp.where(tril, acs[:, h:h + 1] - acs_t[h:h + 1, :], -jnp.inf))
        out.append((dk, cb * dk))
    return out


def _pair_col(row_lo, tot, j):
    return jnp.exp(jnp.where(row_lo, tot[:, 2 * j:2 * j + 1], tot[:, 2 * j + 1:2 * j + 2]))


def _gated_norm(y, z, g):
    yg = y * (z * _sigmoid(z))
    half = SSM_WIDTH // SSM_GROUPS
    parts = []
    for k in range(SSM_GROUPS):
        s = yg[:, k * half:(k + 1) * half]
        parts.append(s * lax.rsqrt(jnp.mean(s * s, axis=-1, keepdims=True) + EPS))
    return jnp.concatenate(parts, axis=1) * g


def _group_mats(xa):
    out = []
    for g in range(SSM_GROUPS):
        bm = xa[:, SSM_WIDTH + g * SSM_STATE:SSM_WIDTH + (g + 1) * SSM_STATE].astype(bf16)
        cm = xa[:, SSM_WIDTH + (SSM_GROUPS + g) * SSM_STATE:SSM_WIDTH + (SSM_GROUPS + g + 1) * SSM_STATE].astype(bf16)
        out.append((cm, bm, _dot_nt(cm, bm)))
    return out


def _ssd_fwd(xbc, z, dtr, conv_w, conv_b, dt_bias, a_log, d_skip, ssm_norm):
    T = xbc.shape[0]
    nc = T // CHUNK
    N = SSM_STATE

    def body(xbc_ref, halo_ref, z_ref, dtr_ref, cw_ref, cb_ref, dtb_ref, alog_ref, dsk_ref, g_ref, ex_ref,
             yb_ref, xc_ref, sg_ref, y_ref, sprev_ref, s_sc):
        i = pl.program_id(0)

        @pl.when(i == 0)
        def _():
            s_sc[...] = jnp.zeros_like(s_sc)

        lo = lax.broadcasted_iota(jnp.int32, (CHUNK, PAIR_W), 1) < SSM_HEAD_DIM
        row_lo = lax.broadcasted_iota(jnp.int32, (PAIR_W, 1), 0) < SSM_HEAD_DIM
        for k in range(SUB):
            rs = slice(k * CHUNK, (k + 1) * CHUNK)
            if k == 0:
                halo = halo_ref[...]
                halo = jnp.where(i > 0, halo, jnp.zeros_like(halo))
            else:
                halo = xbc_ref[k * CHUNK - HALO:k * CHUNK, :]
            xc = _ssd_conv(halo, xbc_ref[rs, :], cw_ref, cb_ref)
            sg = _sigmoid(xc)
            xc_ref[rs, :] = xc
            sg_ref[rs, :] = sg
            xa = xc * sg
            dt, _, acs = _ssd_front(dtr_ref[rs, :], dtb_ref, alog_ref)
            _, dsk_x, e_x, _, xs, xd, gm = _ssd_wide(xa, dt, acs, dsk_ref[...], ex_ref[...])
            acs_t = acs.T
            tot = acs[CHUNK - 1:CHUNK, :]
            groups = _group_mats(xa)
            ys = []
            for j in range(PAIRS):
                cmb, bmb, cb = groups[j // (PAIRS // SSM_GROUPS)]
                ps = slice(j * PAIR_W, (j + 1) * PAIR_W)
                (_, m0), (_, m1) = _ssd_pair(j, acs, acs_t, cb)
                sp = s_sc[j]
                yd = _dot(jnp.concatenate([m0, m1], axis=1).astype(bf16), _pair_stack(xd[:, ps], lo).astype(bf16))
                ys.append(yd + e_x[:, ps] * _dot_nt(cmb, sp.astype(bf16)))
                sprev_ref[k, j] = sp
                s_sc[j] = _pair_col(row_lo, tot, j) * sp + _dot_tn(gm[:, ps].astype(bf16), bmb)
            y = jnp.concatenate(ys, axis=1) + xs * dsk_x
            y_ref[rs, :] = y
            yb_ref[rs, :] = _gated_norm(y, z_ref[rs, :].astype(f32), g_ref[...]).astype(bf16)

    params = [conv_w, conv_b, dt_bias, a_log, d_skip, ssm_norm, _head_mats()[0]]
    SUB = SSD_SUB if nc % SSD_SUB == 0 else 1
    hp = SUB * CHUNK // HALO
    R = SUB * CHUNK
    in_specs = [_rows(R, CONV_DIM), pl.BlockSpec((HALO, CONV_DIM), lambda i: (jnp.maximum(i * hp - 1, 0), 0)),
                _rows(R, SSM_WIDTH), _rows(R, LANES)] + [_full(x.shape) for x in params]
    return pl.pallas_call(
        body, name="ssd_fwd", grid=(nc // SUB,), in_specs=in_specs,
        out_specs=[_rows(R, SSM_WIDTH), _rows(R, CONV_DIM), _rows(R, CONV_DIM), _rows(R, SSM_WIDTH),
                   pl.BlockSpec((SUB, PAIRS, PAIR_W, N), lambda i: (i, 0, 0, 0))],
        out_shape=[jax.ShapeDtypeStruct((T, SSM_WIDTH), bf16), jax.ShapeDtypeStruct((T, CONV_DIM), f32),
                   jax.ShapeDtypeStruct((T, CONV_DIM), f32), jax.ShapeDtypeStruct((T, SSM_WIDTH), f32),
                   jax.ShapeDtypeStruct((nc, PAIRS, PAIR_W, N), f32)],
        scratch_shapes=[pltpu.VMEM((PAIRS, PAIR_W, N), f32)],
        compiler_params=_params(("arbitrary",)))(xbc, xbc, z, dtr, *params)


def _ssd_bwd(xbc, xc, sg, y, z, dtr, sprev, dyb, conv_w, conv_b, dt_bias, a_log, d_skip, ssm_norm):
    T = xbc.shape[0]
    nc = T // CHUNK
    H, N = SSM_HEADS, SSM_STATE
    PG = PAIRS // SSM_GROUPS

    def chunk(xbc_ref, xc_ref, sg_ref, y_ref, z_ref, dtr_ref, sprev_k, dyb_ref, cw_ref, cb_ref, dtb_ref, alog_ref, dsk_ref,
              g_ref, ex_ref, rd_ref, dzxd_ref, dcw_ref, dcb_ref, ddtb_ref, dalog_ref, ddsk_ref, dg_ref, ds_sc, next_sc):
        xc = xc_ref[...]
        sg = sg_ref[...]
        xa = xc * sg
        dt, a, acs = _ssd_front(dtr_ref[...], dtb_ref, alog_ref)
        dt_x, dsk_x, e_x, r_x, xs, xd, gm = _ssd_wide(xa, dt, acs, dsk_ref[...], ex_ref[...])
        acs_t = acs.T
        tot = acs[CHUNK - 1:CHUNK, :]
        groups = _group_mats(xa)
        lo = lax.broadcasted_iota(jnp.int32, (CHUNK, PAIR_W), 1) < SSM_HEAD_DIM
        row_lo = lax.broadcasted_iota(jnp.int32, (PAIR_W, 1), 0) < SSM_HEAD_DIM
        pairs, zs = [], []
        for j in range(PAIRS):
            cmb, _, cb = groups[j // PG]
            pairs.append(_ssd_pair(j, acs, acs_t, cb))
            zs.append(_dot_nt(cmb, sprev_k[j].astype(bf16)))
        zf = jnp.concatenate(zs, axis=1)
        _, gn_vjp = jax.vjp(_gated_norm, y_ref[...], z_ref[...].astype(f32), g_ref[...])
        dy, dz, dg = gn_vjp(dyb_ref[...].astype(f32))
        dg_ref[...] += dg
        dzxd_ref[:, :SSM_WIDTH] = dz.astype(bf16)

        lane = lax.broadcasted_iota(jnp.int32, (1, LANES), 1)
        sub = lax.broadcasted_iota(jnp.int32, (LANES, 1), 0)
        dacs = jnp.zeros((CHUNK, LANES), f32)
        dacs_r = jnp.zeros((LANES, CHUNK), f32)
        dtot = jnp.zeros((1, LANES), f32)
        dcb = [jnp.zeros((CHUNK, CHUNK), f32) for _ in range(SSM_GROUPS)]
        dcm = [jnp.zeros((CHUNK, N), f32) for _ in range(SSM_GROUPS)]
        dbm = [jnp.zeros((CHUNK, N), f32) for _ in range(SSM_GROUPS)]
        dxds, dgms = [], []
        for j in range(PAIRS):
            g = j // PG
            cmb, bmb, _ = groups[g]
            ps = slice(j * PAIR_W, (j + 1) * PAIR_W)
            (dk0, m0), (dk1, m1) = pairs[j]
            oh0, oh1 = (lane == 2 * j).astype(f32), (lane == 2 * j + 1).astype(f32)
            dyp = dy[:, ps]
            dy2 = _pair_stack(dyp, lo).astype(bf16)
            dm2 = _dot_nt(dy2, xd[:, ps].astype(bf16))
            m2 = jnp.concatenate([m0, m1], axis=0)
            dxds.append(_dot_tn(m2.astype(bf16), dy2))
            w2 = dm2 * m2
            rs = jnp.sum(w2, axis=1, keepdims=True)
            dacs = dacs + rs[:CHUNK] * oh0 + rs[CHUNK:] * oh1
            dacs_r = dacs_r - ((sub == 2 * j).astype(f32) * jnp.sum(w2[:CHUNK], axis=0, keepdims=True)
                               + (sub == 2 * j + 1).astype(f32) * jnp.sum(w2[CHUNK:], axis=0, keepdims=True))
            dcb[g] = dcb[g] + dm2[:CHUNK] * dk0 + dm2[CHUNK:] * dk1
            sp = sprev_k[j]
            dzb = (dyp * e_x[:, ps]).astype(bf16)
            dcm[g] = dcm[g] + _dot(dzb, sp.astype(bf16))
            dsn = ds_sc[j]
            dsnb = dsn.astype(bf16)
            et = _pair_col(row_lo, tot, j)
            rr = jnp.sum(dsn * sp, axis=1, keepdims=True) * et
            dtot = dtot + jnp.sum(rr[:SSM_HEAD_DIM]) * oh0 + jnp.sum(rr[SSM_HEAD_DIM:]) * oh1
            dgms.append(_dot_nt(bmb, dsnb))
            dbm[g] = dbm[g] + _dot(gm[:, ps].astype(bf16), dsnb)
            ds_sc[j] = _dot_tn(dzb, cmb) + et * dsn
        dgm = jnp.concatenate(dgms, axis=1)
        dxd = jnp.concatenate(dxds, axis=1) + dgm * r_x
        dr = dgm * gm
        red = _dot_sel(jnp.concatenate([dy * e_x * zf - dr, dr, dxd * xs, dy * xs], axis=0), rd_ref[...], REDUCE_SPLIT)
        rowi = lax.broadcasted_iota(jnp.int32, (CHUNK, 1), 0)
        dtot = dtot + jnp.sum(red[CHUNK:2 * CHUNK], axis=0, keepdims=True)
        dacs = dacs + red[:CHUNK] + dacs_r.T + jnp.where(rowi == CHUNK - 1, dtot, 0.0)
        r2 = lax.broadcasted_iota(jnp.int32, (CHUNK, CHUNK), 0)
        c2 = lax.broadcasted_iota(jnp.int32, (CHUNK, CHUNK), 1)
        dadt = jnp.dot((c2 >= r2).astype(f32), dacs, preferred_element_type=f32, precision=HIGHEST)
        ddt = red[2 * CHUNK:3 * CHUNK] + dadt * a
        dalog_ref[...] += jnp.sum(dadt * dt, axis=0, keepdims=True) * a
        ddsk_ref[...] += jnp.sum(red[3 * CHUNK:], axis=0, keepdims=True)
        ddtr = jnp.where(lane < H, ddt * _sigmoid(dtr_ref[...] + dtb_ref[...]), 0.0)
        ddtb_ref[...] += jnp.sum(ddtr, axis=0, keepdims=True)
        dzxd_ref[:, SSM_WIDTH + CONV_DIM:] = ddtr.astype(bf16)
        dxa_bm, dxa_cm = [], []
        for g in range(SSM_GROUPS):
            cmb, bmb, _ = groups[g]
            dcbb = dcb[g].astype(bf16)
            dxa_bm.append(dbm[g] + _dot_tn(dcbb, cmb))
            dxa_cm.append(dcm[g] + _dot(dcbb, bmb))
        dxc = jnp.concatenate([dy * dsk_x + dxd * dt_x] + dxa_bm + dxa_cm, axis=1) * (sg * (1.0 + xc * (1.0 - sg)))
        ext = jnp.concatenate([dxc, next_sc[...]], axis=0)
        xin = xbc_ref[...].astype(f32)
        dxbc = cw_ref[SSM_CONV - 1:SSM_CONV, :] * dxc
        dcw = [jnp.sum(dxc * xin, axis=0, keepdims=True)]
        for s in range(1, SSM_CONV):
            later = _sel_dot(_shift_mat(CHUNK, CHUNK + HALO, s), ext, 2)
            dxbc = dxbc + cw_ref[SSM_CONV - 1 - s:SSM_CONV - s, :] * later
            dcw.insert(0, jnp.sum(later * xin, axis=0, keepdims=True))
        dzxd_ref[:, SSM_WIDTH:SSM_WIDTH + CONV_DIM] = dxbc.astype(bf16)
        dcw_ref[...] += jnp.concatenate(dcw, axis=0)
        dcb_ref[...] += jnp.sum(dxc, axis=0, keepdims=True)
        next_sc[...] = dxc[0:HALO, :]

    SUB = SSD_SUB if nc % SSD_SUB == 0 else 1
    nb = nc // SUB

    def body(xbc_ref, xc_ref, sg_ref, y_ref, z_ref, dtr_ref, sprev_ref, dyb_ref, cw_ref, cb_ref, dtb_ref, alog_ref, dsk_ref,
             g_ref, ex_ref, rd_ref, dzxd_ref, dcw_ref, dcb_ref, ddtb_ref, dalog_ref, ddsk_ref, dg_ref, ds_sc, next_sc):
        @pl.when(pl.program_id(0) == 0)
        def _():
            ds_sc[...] = jnp.zeros_like(ds_sc)
            next_sc[...] = jnp.zeros_like(next_sc)
            for r_ in (dcw_ref, dcb_ref, ddtb_ref, dalog_ref, ddsk_ref, dg_ref):
                r_[...] = jnp.zeros_like(r_)

        for k in reversed(range(SUB)):
            rows = slice(k * CHUNK, (k + 1) * CHUNK)
            tok = [_RowsOf(r_, rows) for r_ in (xbc_ref, xc_ref, sg_ref, y_ref, z_ref, dtr_ref)]
            chunk(*tok, sprev_ref.at[k], _RowsOf(dyb_ref, rows), cw_ref, cb_ref, dtb_ref, alog_ref, dsk_ref, g_ref, ex_ref,
                  rd_ref, _RowsOf(dzxd_ref, rows), dcw_ref, dcb_ref, ddtb_ref, dalog_ref, ddsk_ref, dg_ref, ds_sc, next_sc)

    params = [conv_w, conv_b, dt_bias, a_log, d_skip, ssm_norm]
    mats = list(_head_mats())

    def rev(ncols):
        return pl.BlockSpec((SUB * CHUNK, ncols), lambda i: (nb - 1 - i, 0))

    in_specs = ([rev(CONV_DIM), rev(CONV_DIM), rev(CONV_DIM), rev(SSM_WIDTH), rev(SSM_WIDTH), rev(LANES),
                 pl.BlockSpec((SUB, PAIRS, PAIR_W, N), lambda i: (nb - 1 - i, 0, 0, 0)), rev(SSM_WIDTH)]
                + [_full(x.shape) for x in params + mats])
    return pl.pallas_call(
        body, name="ssd_bwd", grid=(nb,), in_specs=in_specs,
        out_specs=[rev(ZXD)] + [_acc(x.shape) for x in params],
        out_shape=[jax.ShapeDtypeStruct((T, ZXD), bf16)] + [jax.ShapeDtypeStruct(x.shape, f32) for x in params],
        scratch_shapes=[pltpu.VMEM((PAIRS, PAIR_W, N), f32), pltpu.VMEM((HALO, CONV_DIM), f32)],
        compiler_params=_params(("arbitrary",)))(xbc, xc, sg, y, z, dtr, sprev, dyb, *params, *mats)


def _tail(h3, p, tgt, gp, wpg, bpg, wpp, gf, tm=512):
    T, D = h3.shape
    tm = min(tm, T)

    def head(gpre, pp, h, gf_, t):
        gate = _sigmoid(gpre)
        y = _rms(h + gate * pp, gf_)
        err = y - t
        return 0.5 * jnp.sum(jnp.mean(err * err, axis=-1))

    def body(h_ref, p_ref, t_ref, gp_ref, wpg_ref, bpg_ref, wpp_ref, gf_ref,
             dh_ref, loss_ref, dgp_ref, dwpg_ref, dbpg_ref, dwpp_ref, dgf_ref):
        @pl.when(pl.program_id(0) == 0)
        def _():
            for r in (loss_ref, dgp_ref, dwpg_ref, dbpg_ref, dwpp_ref, dgf_ref):
                r[...] = jnp.zeros_like(r)

        h = h_ref[...]
        npf, np_vjp = jax.vjp(_rms, h, gp_ref[...])
        npb = npf.astype(bf16)
        pb = p_ref[...].astype(bf16)
        gpre = _dot(npb, wpg_ref[...]) + bpg_ref[...]
        kp, _, cp = wpp_ref.shape
        pp = jnp.concatenate([_dot(pb, wpp_ref[k]) for k in range(kp)], axis=1)
        loss, head_vjp = jax.vjp(head, gpre, pp, h, gf_ref[...], t_ref[...])
        dgpre, dpp, dh_a, dgf, _ = head_vjp(jnp.ones((), f32))
        loss_ref[...] += loss
        dgf_ref[...] += dgf
        dbpg_ref[...] += jnp.sum(dgpre, axis=0, keepdims=True)
        dgb = dgpre.astype(bf16)
        dwpg_ref[...] += _dot_tn(npb, dgb)
        dppb = dpp.astype(bf16)
        for k in range(kp):
            dwpp_ref[k] += _dot_tn(pb, dppb[:, k * cp:(k + 1) * cp])
        dh_b, dgp = np_vjp(_dot_nt(dgb, wpg_ref[...]))
        dgp_ref[...] += dgp
        dh_ref[...] = dh_a + dh_b

    ins = [h3, p, tgt, gp, wpg, bpg, wpp, gf]
    in_specs = [_rows(tm, D), _rows(tm, p.shape[1]), _rows(tm, D)] + [_full(x.shape) for x in ins[3:]]
    acc_shapes = [(1, LANES), gp.shape, wpg.shape, bpg.shape, wpp.shape, gf.shape]
    return pl.pallas_call(
        body, name="tail", grid=(T // tm,), in_specs=in_specs,
        out_specs=[_rows(tm, D)] + [_acc(s) for s in acc_shapes],
        out_shape=[jax.ShapeDtypeStruct((T, D), f32)] + [jax.ShapeDtypeStruct(s, f32) for s in acc_shapes],
        compiler_params=_params(("arbitrary",)))(*ins)


def _adamw(name, w, g, m, v, tr=256):
    R, rest = w.shape[0], w.shape[1:]
    tr = _row_tile(R, tr, 8 if len(rest) == 1 else 1)

    def body(w_ref, g_ref, m_ref, v_ref, d_ref, mo_ref, vo_ref):
        g_ = g_ref[...]
        m_ = ADAM_B1 * m_ref[...] + (1.0 - ADAM_B1) * g_
        v_ = ADAM_B2 * v_ref[...] + (1.0 - ADAM_B2) * jnp.square(g_)
        m_hat = m_ / (1.0 - ADAM_B1 ** ADAM_STEP)
        v_hat = v_ / (1.0 - ADAM_B2 ** ADAM_STEP)
        d_ref[...] = -ADAM_LR * (m_hat / (jnp.sqrt(v_hat) + ADAM_EPS) + ADAM_WD * w_ref[...])
        mo_ref[...] = m_
        vo_ref[...] = v_

    spec = pl.BlockSpec((tr,) + rest, lambda i: (i,) + (0,) * len(rest))
    return pl.pallas_call(body, name=name, grid=(R // tr,), in_specs=[spec] * 4, out_specs=[spec] * 3,
                          out_shape=[jax.ShapeDtypeStruct(w.shape, f32)] * 3,
                          compiler_params=_params(("parallel",)))(w, g, m, v)


HBM = pl.BlockSpec(memory_space=pltpu.HBM)


def _me():
    return lax.axis_index("x"), lax.axis_index("y"), lax.axis_index("c")


def _other_chips(x, y):
    return [(1 - x, y), (x, 1 - y), (1 - x, 1 - y)]


def _remote(src, dst, send_sem, recv_sem, dev):
    return pltpu.make_async_remote_copy(src_ref=src, dst_ref=dst, send_sem=send_sem, recv_sem=recv_sem,
                                        device_id=dev, device_id_type=MESH)


def _sems(n):
    return [pltpu.SemaphoreType.DMA((n,)), pltpu.SemaphoreType.DMA((n,))]


def _gather_weights(shards, split):
    n = len(shards)

    def body(*refs):
        ins, outs = refs[:n], refs[n:2 * n]
        own_send, own_recv, ici_send, ici_recv, d2d_send, d2d_recv = refs[2 * n:]
        x, y, c = _me()
        my_chip = 2 * x + y
        sibling = (x, y, 1 - c)
        chips = _other_chips(x, y)

        def rows(i, half):
            hr = shards[i].shape[0] // 2
            return pl.ds(half * hr, hr) if split[i] else pl.ds(0, shards[i].shape[0])

        sends = []
        for i in range(n):
            for j, chip in enumerate(chips):
                cp = _remote(ins[i].at[rows(i, c)], outs[i].at[my_chip, rows(i, c)],
                             ici_send.at[3 * i + j], ici_recv.at[3 * i + j], (*chip, c))
                cp.start()
                sends.append(cp)
            cp = _remote(ins[i], outs[i].at[my_chip], own_send.at[i], own_recv.at[i], sibling)
            cp.start()
            sends.append(cp)
        for i in range(n):
            for j, chip in enumerate(chips):
                s = 3 * i + j
                land = outs[i].at[2 * chip[0] + chip[1], rows(i, c)]
                _remote(land, land, ici_send.at[s], ici_recv.at[s], (*chip, c)).wait_recv()
                if split[i]:
                    cp = _remote(land, land, d2d_send.at[s], d2d_recv.at[s], sibling)
                    cp.start()
                    sends.append(cp)
        for i in range(n):
            _remote(ins[i], outs[i].at[my_chip], own_send.at[i], own_recv.at[i], sibling).wait_recv()
            if split[i]:
                for j, chip in enumerate(chips):
                    s = 3 * i + j
                    land = outs[i].at[2 * chip[0] + chip[1], rows(i, 1 - c)]
                    _remote(land, land, d2d_send.at[s], d2d_recv.at[s], sibling).wait_recv()
        for cp in sends:
            cp.wait_send()

    return pl.pallas_call(
        body, name="gather_weights", out_shape=[jax.ShapeDtypeStruct((N_CHIPS,) + s.shape, s.dtype) for s in shards],
        in_specs=[HBM] * n, out_specs=[HBM] * n,
        scratch_shapes=_sems(n) + _sems(3 * n) + _sems(3 * n))(*shards)


def _swap_halves(name, grads):
    n = len(grads)

    def body(*refs):
        ins, outs, send, recv = refs[:n], refs[n:2 * n], refs[2 * n], refs[2 * n + 1]
        x, y, c = _me()
        copies = []
        for i in range(n):
            hr = grads[i].shape[1] // 2
            cp = _remote(ins[i].at[:, pl.ds((1 - c) * hr, hr), :], outs[i], send.at[i], recv.at[i], (x, y, 1 - c))
            cp.start()
            copies.append(cp)
        for cp in copies:
            cp.wait()

    return pl.pallas_call(
        body, name=name,
        out_shape=[jax.ShapeDtypeStruct((g.shape[0], g.shape[1] // 2, g.shape[2]), g.dtype) for g in grads],
        in_specs=[HBM] * n, out_specs=[HBM] * n, scratch_shapes=_sems(n))(*grads)


def _add_halves(name, grads, other, c_idx, th=HALF_ROWS_BF16):
    K, R, C = grads.shape
    H = R // 2
    th = _row_tile(H, th, 16)
    nb = H // th

    def body(c_ref, g_ref, o_ref, out_ref):
        out_ref[...] = (g_ref[...].astype(f32) + o_ref[...].astype(f32)).astype(bf16)

    grid_spec = pltpu.PrefetchScalarGridSpec(
        num_scalar_prefetch=1, grid=(nb,),
        in_specs=[pl.BlockSpec((K, th, C), lambda i, c: (0, c[0] * nb + i, 0)),
                  pl.BlockSpec((K, th, C), lambda i, c: (0, i, 0))],
        out_specs=pl.BlockSpec((K, th, C), lambda i, c: (0, i, 0)))
    return pl.pallas_call(body, name=name, grid_spec=grid_spec,
                          out_shape=jax.ShapeDtypeStruct((K, H, C), bf16),
                          compiler_params=_params(("parallel",)))(c_idx, grads, other)


SEM = pl.BlockSpec(memory_space=pltpu.SEMAPHORE)
ANY = pl.BlockSpec(memory_space=pl.ANY)
EFFECT = pltpu.SideEffectType.DATAFLOW_SIDE_EFFECTING


def _copies_start(name, srcs, land_shapes, n_copies, make_copies, after):
    ns, nl = len(srcs), len(land_shapes)
    lands = [lax.empty(s.shape, s.dtype) for s in land_shapes]

    def body(*refs):
        src_refs, land_refs = refs[:ns], refs[ns:ns + nl]
        send, recv, token = refs[ns + nl + 1], refs[ns + nl + 2], refs[-1]
        for cp in make_copies(src_refs, land_refs, send, recv):
            cp.start()
        token[...] = jnp.zeros_like(token)

    buffers = list(srcs) + lands
    out = pl.pallas_call(
        body, name=name,
        out_shape=(pltpu.SemaphoreType.DMA((n_copies,)), pltpu.SemaphoreType.DMA((n_copies,)),
                   *[pltpu.HBM(b.shape, b.dtype) for b in buffers], jax.ShapeDtypeStruct((8, LANES), f32)),
        in_specs=[HBM] * (ns + nl) + [ANY],
        out_specs=(SEM, SEM, *[HBM] * (ns + nl), pl.BlockSpec(memory_space=pltpu.VMEM)),
        input_output_aliases={i: 2 + i for i in range(ns + nl)},
        compiler_params=pltpu.CompilerParams(has_side_effects=EFFECT),
    )(*[pltpu.with_memory_space_constraint(b, pltpu.HBM) for b in buffers], after)
    return out[0], out[1], list(out[2:2 + ns]), list(out[2 + ns:2 + ns + nl]), out[-1]


def _copies_wait(name, started, make_copies, after):
    send, recv, srcs, lands, _ = started
    ns, nl = len(srcs), len(lands)
    after = list(after)

    def body(*refs):
        src_refs, land_refs = refs[:ns], refs[ns:ns + nl]
        for cp in make_copies(src_refs, land_refs, refs[ns + nl], refs[ns + nl + 1]):
            cp.wait_send()
            cp.wait_recv()

    buffers = list(srcs) + list(lands)
    out = pl.pallas_call(
        body, name=name, out_shape=tuple(pltpu.HBM(b.shape, b.dtype) for b in buffers),
        in_specs=[HBM] * (ns + nl) + [SEM, SEM] + [ANY] * len(after), out_specs=tuple([HBM] * (ns + nl)),
        input_output_aliases={i: i for i in range(ns + nl)},
        compiler_params=pltpu.CompilerParams(has_side_effects=EFFECT),
    )(*buffers, send, recv, *after)
    return list(out[:ns]), list(out[ns:])


def _gather_copies(src_refs, land_refs, send, recv):
    x, y, c = _me()
    my_chip = 2 * x + y
    peers = [(*chip, c) for chip in _other_chips(x, y)] + [(x, y, 1 - c)]
    return [_remote(src_refs[i], land_refs[i].at[my_chip], send.at[4 * i + j], recv.at[4 * i + j], peer)
            for i in range(len(src_refs)) for j, peer in enumerate(peers)]


def _swap_copies(src_refs, land_refs, send, recv):
    x, y, c = _me()
    copies = []
    for i in range(len(src_refs)):
        hr = src_refs[i].shape[1] // 2
        copies.append(_remote(src_refs[i].at[:, pl.ds((1 - c) * hr, hr), :], land_refs[i], send.at[i], recv.at[i], (x, y, 1 - c)))
    return copies


def _partial_copies(src_refs, land_refs, send, recv):
    x, y, c = _me()
    return [_remote(src_refs[i].at[2 * chip[0] + chip[1]], land_refs[i].at[j], send.at[3 * i + j], recv.at[3 * i + j], (*chip, c))
            for i in range(len(src_refs)) for j, chip in enumerate(_other_chips(x, y))]


def _small_copies(src_refs, land_refs, send, recv):
    x, y, c = _me()
    return [_remote(src_refs[0], land_refs[0].at[k - 1], send.at[k - 1], recv.at[k - 1], (x ^ (k >> 2), y ^ ((k >> 1) & 1), c ^ (k & 1)))
            for k in range(1, N_DEV)]


def _sum_small(own, slots, dev_idx):
    R, C = own.shape

    def body(dev_ref, own_ref, s_ref, o_ref):
        me = dev_ref[0]
        acc = jnp.zeros((R, C), f32)
        for d in range(N_DEV):
            k = me ^ d
            acc = acc + jnp.where(k == 0, own_ref[...], s_ref[jnp.maximum(k - 1, 0)])
        o_ref[...] = acc

    grid_spec = pltpu.PrefetchScalarGridSpec(
        num_scalar_prefetch=1, grid=(1,),
        in_specs=[pl.BlockSpec((R, C), lambda i, dev: (0, 0)), pl.BlockSpec((N_DEV - 1, R, C), lambda i, dev: (0, 0, 0))],
        out_specs=pl.BlockSpec((R, C), lambda i, dev: (0, 0)))
    return pl.pallas_call(body, name="sum_small", grid_spec=grid_spec, out_shape=jax.ShapeDtypeStruct((R, C), f32),
                          compiler_params=_params(("arbitrary",)))(dev_idx, own, slots)


def _sum_partials(name, part, recv, chip_idx, th=HALF_ROWS_BF16):
    K, H, C = part.shape
    th = _row_tile(H, th, 16)

    def body(chip_ref, p_ref, r_ref, o_ref):
        acc = p_ref[...].astype(f32)
        for j in range(3):
            acc = acc + r_ref[j].astype(f32)
        o_ref[...] = acc

    grid_spec = pltpu.PrefetchScalarGridSpec(
        num_scalar_prefetch=1, grid=(H // th,),
        in_specs=[pl.BlockSpec((None, th, C), lambda i, chip: (chip[0], i, 0)),
                  pl.BlockSpec((3, th, C), lambda i, chip: (0, i, 0))],
        out_specs=pl.BlockSpec((th, C), lambda i, chip: (i, 0)))
    return pl.pallas_call(body, name=name, grid_spec=grid_spec, out_shape=jax.ShapeDtypeStruct((H, C), f32),
                          compiler_params=_params(("parallel",)))(chip_idx, part, recv)


def _share_halves(name, halves):
    n = len(halves)

    def body(*refs):
        ins, outs, send, recv = refs[:n], refs[n:2 * n], refs[2 * n], refs[2 * n + 1]
        x, y, c = _me()
        copies = []
        for i in range(n):
            cp = _remote(ins[i], outs[i], send.at[i], recv.at[i], (x, y, 1 - c))
            cp.start()
            copies.append(cp)
        for cp in copies:
            cp.wait()

    return pl.pallas_call(
        body, name=name, out_shape=[jax.ShapeDtypeStruct(h.shape, h.dtype) for h in halves],
        in_specs=[HBM] * n, out_specs=[HBM] * n, scratch_shapes=_sems(n))(*halves)


def _adamw_big(name, w, g_mine, g_theirs, m, v, c_idx, tr=HALF_ROWS_F32):
    R, C = w.shape
    H = R // 2
    tr = _row_tile(H, tr)
    nb = H // tr

    def body(c_ref, w_ref, gm_ref, gt_ref, m_ref, v_ref, g_ref, d_ref, mo_ref, vo_ref):
        g_ = jnp.where(pl.program_id(0) // nb == c_ref[0], gm_ref[...], gt_ref[...])
        g_ref[...] = g_
        m_ = ADAM_B1 * m_ref[...] + (1.0 - ADAM_B1) * g_
        v_ = ADAM_B2 * v_ref[...] + (1.0 - ADAM_B2) * jnp.square(g_)
        m_hat = m_ / (1.0 - ADAM_B1 ** ADAM_STEP)
        v_hat = v_ / (1.0 - ADAM_B2 ** ADAM_STEP)
        d_ref[...] = -ADAM_LR * (m_hat / (jnp.sqrt(v_hat) + ADAM_EPS) + ADAM_WD * w_ref[...])
        mo_ref[...] = m_
        vo_ref[...] = v_

    full = pl.BlockSpec((tr, C), lambda i, c: (i, 0))
    half = pl.BlockSpec((tr, C), lambda i, c: (i % nb, 0))
    grid_spec = pltpu.PrefetchScalarGridSpec(num_scalar_prefetch=1, grid=(2 * nb,),
                                             in_specs=[full, half, half, full, full], out_specs=[full] * 4)
    return pl.pallas_call(body, name=name, grid_spec=grid_spec, out_shape=[jax.ShapeDtypeStruct((R, C), f32)] * 4,
                          compiler_params=_params(("parallel",)))(c_idx, w, g_mine, g_theirs, m, v)


BIG = ("ffn1_w_gate", "ffn1_w_up", "ffn1_w_down", "w_in", "w_out", "ffn2_w_gate", "ffn2_w_up", "ffn2_w_down",
       "ple_w_gate", "ple_w_proj")


SMALL = ("ffn1_norm", "mix_norm", "gm_ln_g", "gm_ln_b", "gm_w_s", "gm_b_s", "gm_out_norm", "conv_b", "dt_bias", "a_log",
         "d_skip", "ssm_norm", "ffn2_norm", "ple_norm", "ple_b_gate", "final_norm")
SMALL_C = 1024


def _pack_small(vals):
    parts = []
    for v in vals:
        f = v.astype(f32).reshape(-1)
        parts.append(jnp.pad(f, (0, -f.shape[0] % SMALL_C)))
    flat = jnp.concatenate(parts)
    rows = flat.shape[0] // SMALL_C
    return jnp.pad(flat, (0, (-rows % 8) * SMALL_C)).reshape(-1, SMALL_C)


def _unpack_small(pack, shapes):
    flat = pack.reshape(-1)
    out, off = [], 0
    for s in shapes:
        n = 1
        for d in s:
            n *= d
        out.append(flat[off:off + n].reshape(s))
        off += n + (-n % SMALL_C)
    return out


def _pad_lanes(v):
    return jnp.pad(v, ((0, 0), (0, LANES - v.shape[1])))


def _pad_rows(a):
    pad = [(0, 0)] * a.ndim
    pad[-2] = (0, -a.shape[-2] % ROW_PAD)
    return jnp.pad(a, pad) if pad[-2][1] else a


FETCH = (("ffn1_w_gate", "ffn1_w_up", "ffn1_w_down"), ("w_in", "conv_w", "w_out"),
         ("ffn2_w_gate", "ffn2_w_up", "ffn2_w_down", "ple_w_gate", "ple_w_proj"))
TRANSPOSED = ("ffn1_w_gate", "ffn1_w_up", "ffn2_w_gate", "ffn2_w_up", "w_in")
ROW_PAD = 32
DONE = (("ffn2_w_gate", "ffn2_w_up", "ffn2_w_down", "w_out", "ple_w_gate", "ple_w_proj"), ("w_in",),
        ("ffn1_w_gate", "ffn1_w_up", "ffn1_w_down"))


def _local_step(x, p, tgt, fetch, S, on_grads, on_later):
    G = GM_WIDTH
    K = N_CHIPS
    b_st = S["gm_b_s"][0].T
    w_s = S["gm_w_s"][0]
    dtb, alog, dsk = _pad_lanes(S["dt_bias"]), _pad_lanes(S["a_log"]), _pad_lanes(S["d_skip"])
    gfin = S["final_norm"].reshape(1, -1)

    def rows(a):
        return a.reshape(-1, D_MODEL)

    def shards(a):
        return a.reshape(K, -1, D_MODEL)

    wg1, wu1, wd1 = [rows(a) for a in fetch(0, None)]
    h1, n1, a1, b1 = _ffn_fwd("ffn1_fwd", x, S["ffn1_norm"], wg1, wu1, wd1)
    w_in4, cw4, wo4 = fetch(1, h1)
    w_in = w_in4.reshape(IN_PROJ, D_MODEL)
    w_uv = w_in[:2 * G]
    w_zxd = jnp.pad(w_in[2 * G:], ((0, ZXD - (IN_PROJ - 2 * G)), (0, 0)))
    conv_w = jnp.transpose(cw4, (1, 0, 2)).reshape(SSM_CONV, CONV_DIM)
    wo = wo4.reshape(-1, D_MODEL)
    n2, act, slope, z, xbc, dtr, ya = _mix_fwd(h1, S["mix_norm"], w_uv, w_zxd, S["gm_ln_g"], S["gm_ln_b"], w_s, b_st,
                                               S["gm_out_norm"])
    yb, xc, sg, y_ssd, sprev = _ssd_fwd(xbc, z, dtr, conv_w, S["conv_b"], dtb, alog, dsk, S["ssm_norm"])
    wg2, wu2, wd2, wpg4, wpp4 = fetch(2, yb)
    wg2, wu2, wd2 = rows(wg2), rows(wu2), rows(wd2)
    h2, h3, n3, a2, b2 = _ffn_fwd("ffn2_fwd", h1, S["ffn2_norm"], wg2, wu2, wd2, pre=(ya, yb, wo))
    dh3, loss, dgp, dwpg, dbpg, dwpp, dgf = _tail(h3, p, tgt, S["ple_norm"], wpg4.reshape(-1, D_MODEL), S["ple_b_gate"], wpp4, gfin)
    dh2, da2, db2, hm2, dg_ffn2, dya, dyb = _ffn_bwd("ffn2_bwd", dh3, h2, S["ffn2_norm"], a2, b2, wg2, wu2, wd2, wo=wo, ga=G)
    dw_out = jnp.concatenate([_matmul_tn("dw_out_a", ya, dh2), _matmul_tn("dw_out_b", yb, dh2)], axis=0).reshape(wo4.shape)
    zero = on_grads(0, [shards(_matmul_tn("dw_ffn2_gate", da2, n3)), shards(_matmul_tn("dw_ffn2_up", db2, n3)),
                        shards(_matmul_tn("dw_ffn2_down", hm2, dh3, scale=0.5)), dw_out,
                        dwpg.astype(bf16).reshape(wpg4.shape), dwpp.astype(bf16)])
    dzxd, dcw, dcb, ddtb, dalog, ddsk, dgssm = _ssd_bwd(xbc, xc, sg, y_ssd, z, dtr, sprev, dyb, conv_w, S["conv_b"], dtb, alog, dsk,
                                                        S["ssm_norm"] + zero)
    zero = on_later(0, dgssm)
    dh1, duv, dg_mix, dlng, dlnb, dws, dbst, dgout = _mix_bwd(dh2, h1, S["mix_norm"] + zero, act, slope, dya, dzxd, w_uv, w_zxd,
                                                              S["gm_ln_g"], S["gm_ln_b"], w_s, b_st, S["gm_out_norm"])
    dw_in = jnp.concatenate([_matmul_tn("dw_in_uv", duv, n2), _matmul_tn("dw_in_zxd", dzxd, n2)[:IN_PROJ - 2 * G]], axis=0)
    zero = on_grads(1, [dw_in.reshape(w_in4.shape)])
    dx, da1, db1, hm1, dg_ffn1 = _ffn_bwd("ffn1_bwd", dh1, x, S["ffn1_norm"] + zero, a1, b1, wg1, wu1, wd1)
    loss = loss + on_later(1, dg_ffn1)
    zero = on_grads(2, [shards(_matmul_tn("dw_ffn1_gate", da1, n1)), shards(_matmul_tn("dw_ffn1_up", db1, n1)),
                        shards(_matmul_tn("dw_ffn1_down", hm1, dh1, scale=0.5))])
    loss = loss + zero
    nh = SSM_HEADS
    gS = {"ffn1_norm": dg_ffn1, "mix_norm": dg_mix, "gm_ln_g": dlng, "gm_ln_b": dlnb, "gm_w_s": dws[None], "gm_b_s": dbst.T[None],
          "gm_out_norm": dgout, "conv_b": dcb, "dt_bias": ddtb[:, :nh], "a_log": dalog[:, :nh], "d_skip": ddsk[:, :nh],
          "ssm_norm": dgssm, "ffn2_norm": dg_ffn2, "ple_norm": dgp, "ple_b_gate": dbpg, "final_norm": dgf.reshape(-1)}
    return loss, dx, dcw, gS


_WEIGHTS = ("ffn1_norm", "ffn1_w_gate", "ffn1_w_up", "ffn1_w_down", "mix_norm", "w_in", "gm_ln_g", "gm_ln_b", "gm_w_s", "gm_b_s",
            "gm_out_norm", "conv_w", "conv_b", "dt_bias", "a_log", "d_skip", "ssm_norm", "w_out", "ffn2_norm", "ffn2_w_gate",
            "ffn2_w_up", "ffn2_w_down", "ple_norm", "ple_w_gate", "ple_b_gate", "ple_w_proj", "final_norm")
_BIG_NAMES = BIG


def kernel(x, p, ffn1_norm, ffn1_w_gate, ffn1_w_up, ffn1_w_down, mix_norm, w_in, gm_ln_g, gm_ln_b, gm_w_s, gm_b_s, gm_out_norm, conv_w, conv_b, dt_bias, a_log, d_skip, ssm_norm, w_out, ffn2_norm, ffn2_w_gate, ffn2_w_up, ffn2_w_down, ple_norm, ple_w_gate, ple_b_gate, ple_w_proj, final_norm, loss_target, m_ffn1_norm, m_ffn1_w_gate, m_ffn1_w_up, m_ffn1_w_down, m_mix_norm, m_w_in, m_gm_ln_g, m_gm_ln_b, m_gm_w_s, m_gm_b_s, m_gm_out_norm, m_conv_w, m_conv_b, m_dt_bias, m_a_log, m_d_skip, m_ssm_norm, m_w_out, m_ffn2_norm, m_ffn2_w_gate, m_ffn2_w_up, m_ffn2_w_down, m_ple_norm, m_ple_w_gate, m_ple_b_gate, m_ple_w_proj, m_final_norm, v_ffn1_norm, v_ffn1_w_gate, v_ffn1_w_up, v_ffn1_w_down, v_mix_norm, v_w_in, v_gm_ln_g, v_gm_ln_b, v_gm_w_s, v_gm_b_s, v_gm_out_norm, v_conv_w, v_conv_b, v_dt_bias, v_a_log, v_d_skip, v_ssm_norm, v_w_out, v_ffn2_norm, v_ffn2_w_gate, v_ffn2_w_up, v_ffn2_w_down, v_ple_norm, v_ple_w_gate, v_ple_b_gate, v_ple_w_proj, v_final_norm):
    given = dict(locals())
    w = {n: given[n] for n in _WEIGHTS}
    m = {n: given["m_" + n] for n in _WEIGHTS}
    v = {n: given["v_" + n] for n in _WEIGHTS}

    c_idx = lax.axis_index("c").astype(jnp.int32).reshape(1)
    chip = 2 * lax.axis_index("x") + lax.axis_index("y")
    chip_idx = chip.astype(jnp.int32).reshape(1)

    shard = {n: (jnp.swapaxes(w[n][0], 0, 1) if n in TRANSPOSED else w[n][0]).astype(bf16) for n in BIG}
    shard["conv_w"] = w["conv_w"][0]
    first = _gather_weights([shard[n] for n in FETCH[0]], [True] * len(FETCH[0]))
    fetching, after = [], first[-1]
    for k in (1, 2):
        srcs = [shard[n] for n in FETCH[k]]
        lands = [jax.ShapeDtypeStruct((N_CHIPS,) + s.shape, s.dtype) for s in srcs]
        fetching.append(_copies_start("gather%d_start" % k, srcs, lands, 4 * len(srcs), _gather_copies, after))
        after = fetching[-1][4]

    def fetch(k, after_):
        return first if k == 0 else _copies_wait("gather%d_wait" % k, fetching[k - 1], _gather_copies, [after_])[1]

    swapping, exchanging = {}, {}

    def exchange(k, grads, others):
        parts = [_add_halves("add_" + n, g_, o_, c_idx) for n, g_, o_ in zip(DONE[k], grads, others)]
        lands = [jax.ShapeDtypeStruct((3,) + p_.shape[1:], p_.dtype) for p_ in parts]
        exchanging[k] = _copies_start("exchange%d_start" % k, parts, lands, 3 * len(parts), _partial_copies, c_idx)
        return exchanging[k][4][0, 0]

    def on_grads(k, grads):
        grads = [_pad_rows(g_) for g_ in grads]
        if k == len(DONE) - 1:
            return exchange(k, grads, _swap_halves("swap%d" % k, grads))
        lands = [jax.ShapeDtypeStruct((g_.shape[0], g_.shape[1] // 2, g_.shape[2]), g_.dtype) for g_ in grads]
        swapping[k] = _copies_start("swap%d_start" % k, grads, lands, len(grads), _swap_copies, c_idx)
        return swapping[k][4][0, 0]

    def on_later(k, after_):
        return exchange(k, *_copies_wait("swap%d_wait" % k, swapping[k], _swap_copies, [after_]))

    S = {n: w[n] for n in SMALL}
    S["ffn1_norm"] = S["ffn1_norm"] + after[0, 0]
    loss, dx, dcw, gS = _local_step(x[0], p[0, 0], loss_target[0], fetch, S, on_grads, on_later)

    small = _pack_small([gS[n] for n in SMALL] + [dcw, loss[:, :1]])
    small_lands = [jax.ShapeDtypeStruct((N_DEV - 1,) + small.shape, small.dtype)]
    small_st = _copies_start("small_start", [small], small_lands, N_DEV - 1, _small_copies, c_idx)

    g, delta, new_m, new_v = {}, {}, {}, {}
    after = [small_st[4]]
    for k in range(len(DONE)):
        parts, recv = _copies_wait("exchange%d_wait" % k, exchanging[k], _partial_copies, after)
        mine = [_sum_partials("sum_" + n, p_, r_, chip_idx) for n, p_, r_ in zip(DONE[k], parts, recv)]
        theirs = _share_halves("share%d" % k, mine)
        after = []
        for n, gm_, gt_ in zip(DONE[k], mine, theirs):
            flip = (lambda a: jnp.swapaxes(a, 0, 1)) if n in TRANSPOSED else (lambda a: a)
            rows = flip(w[n][0]).shape[0]
            if rows % ROW_PAD:
                def lin(a):
                    return jnp.transpose(a.reshape(-1, LANES, rows), (2, 0, 1))

                def back(a):
                    return jnp.transpose(a, (1, 2, 0)).reshape(1, -1, rows)

                gm_, gt_ = [a.reshape(a.shape[0], -1, LANES) for a in (gm_, gt_)]
                g_ = jnp.where(c_idx[0] == 0, jnp.concatenate([gm_, gt_]), jnp.concatenate([gt_, gm_]))[:rows]
                outs = [g_, *_adamw("adamw_" + n, lin(w[n]), g_, lin(m[n]), lin(v[n]), tr=HALF_ROWS_F32)]
                g[n], delta[n], new_m[n], new_v[n] = [back(o) for o in outs]
            else:
                outs = _adamw_big("adamw_" + n, flip(w[n][0]), gm_, gt_, flip(m[n][0]), flip(v[n][0]), c_idx)
                g[n], delta[n], new_m[n], new_v[n] = [flip(o)[None] for o in outs]
            after.append(outs[3])
    (own,), (slots,) = _copies_wait("small_wait", small_st, _small_copies, after)
    dev_idx = (2 * chip + lax.axis_index("c")).astype(jnp.int32).reshape(1)
    small_shapes = [w[n].shape for n in SMALL] + [dcw.shape, (1, 1)]
    small_sum = _unpack_small(_sum_small(own, slots, dev_idx), small_shapes)
    g.update({n: small_sum[i] for i, n in enumerate(SMALL)})
    cshard = w["conv_w"].shape[2]
    g["conv_w"] = lax.dynamic_slice_in_dim(small_sum[len(SMALL)], chip * cshard, cshard, axis=1)[None]
    loss_total = small_sum[len(SMALL) + 1].reshape(())
    sm_names = SMALL + ("conv_w",)
    sm_shapes = [w[n].shape for n in sm_names]
    d_s, m_s, v_s = _adamw("adamw_small", _pack_small([w[n] for n in sm_names]), _pack_small([g[n] for n in sm_names]),
                           _pack_small([m[n] for n in sm_names]), _pack_small([v[n] for n in sm_names]))
    for dst, src in ((delta, d_s), (new_m, m_s), (new_v, v_s)):
        for n, val in zip(sm_names, _unpack_small(src, sm_shapes)):
            dst[n] = val

    return (loss_total, dx[None], *[g[n] for n in _WEIGHTS], *[delta[n] for n in _WEIGHTS],
            *[new_m[n] for n in _WEIGHTS], *[new_v[n] for n in _WEIGHTS])
```

```python
import jax
import jax.numpy as jnp
from jax import lax
from jax.experimental import pallas as pl
from jax.experimental.pallas import tpu as pltpu

f32 = jnp.float32
bf16 = jnp.bfloat16
MESH = pl.DeviceIdType.MESH
HIGHEST = lax.Precision.HIGHEST

EPS = 1e-6
N_CHIPS = 4
N_DEV = 8
D_MODEL = 1024
GM_WIDTH = 1024
GM_HEADS = 8
CHUNK = 128
SSM_WIDTH = 1024
SSM_HEADS = 16
SSM_HEAD_DIM = 64
SSM_GROUPS = 2
SSM_STATE = 128
SSM_CONV = 4
CONV_DIM = SSM_WIDTH + 2 * SSM_GROUPS * SSM_STATE
IN_PROJ = 2 * GM_WIDTH + SSM_WIDTH + CONV_DIM + SSM_HEADS
LANES = 128
ZXD = SSM_WIDTH + CONV_DIM + LANES

ADAM_LR = 0.001
ADAM_B1 = 0.9
ADAM_B2 = 0.999
ADAM_EPS = 1e-08
ADAM_WD = 0.01
ADAM_STEP = 10

VMEM_LIMIT = 56 * 1024 * 1024
HALF_ROWS_BF16 = 592
HALF_ROWS_F32 = 320


def _dot(a, b):
    return jnp.dot(a, b, preferred_element_type=f32)


def _dot_nt(a, b):
    return lax.dot_general(a, b, (((1,), (1,)), ((), ())), preferred_element_type=f32)


def _dot_tn(a, b):
    return lax.dot_general(a, b, (((0,), (0,)), ((), ())), preferred_element_type=f32)


def _rms(x, g):
    return x * lax.rsqrt(jnp.mean(x * x, axis=-1, keepdims=True) + EPS) * g


def _layernorm(x, g, b):
    mu = jnp.mean(x, axis=-1, keepdims=True)
    xc = x - mu
    return xc * lax.rsqrt(jnp.mean(xc * xc, axis=-1, keepdims=True) + EPS) * g + b


def _sigmoid(x):
    return 1.0 / (1.0 + jnp.exp(-x))


def _softplus(x):
    return jnp.maximum(x, 0.0) + jnp.log(1.0 + jnp.exp(-jnp.abs(x)))


def _full(shape):
    nd = len(shape)
    return pl.BlockSpec(shape, lambda *_: (0,) * nd, pipeline_mode=pl.Buffered(1))


def _acc(shape):
    nd = len(shape)
    return pl.BlockSpec(shape, lambda *_: (0,) * nd)


def _rows(tm, ncols):
    return pl.BlockSpec((tm, ncols), lambda i: (i, 0))


def _params(sem):
    return pltpu.CompilerParams(dimension_semantics=sem, vmem_limit_bytes=VMEM_LIMIT)


def _row_tile(rows, target, mult=8):
    best = rows
    for t in range(mult, min(rows, target) + 1, mult):
        if rows % t == 0:
            best = t
    return best if best <= target else rows


def _ffn_fwd(name, h, g, wg, wu, wd, pre=None, tm=256):
    T, D = h.shape
    F = wg.shape[0]
    tm = min(tm, T)

    def body(*refs):
        if pre is None:
            h_ref, g_ref, wg_ref, wu_ref, wd_ref, ho_ref, n_ref, a_ref, b_ref = refs
            hin = h_ref[...]
        else:
            (h_ref, ya_ref, yb_ref, wo_ref, g_ref, wg_ref, wu_ref, wd_ref,
             hi_ref, ho_ref, n_ref, a_ref, b_ref) = refs
            ga = ya_ref.shape[1]
            hin = h_ref[...] + _dot(ya_ref[...], wo_ref[:ga, :]) + _dot(yb_ref[...], wo_ref[ga:, :])
            hi_ref[...] = hin
        n = _rms(hin, g_ref[...]).astype(bf16)
        n_ref[...] = n
        a = _dot_nt(n, wg_ref[...]).astype(bf16)
        b = _dot_nt(n, wu_ref[...]).astype(bf16)
        a_ref[...] = a
        b_ref[...] = b
        af = a.astype(f32)
        hm = (af * _sigmoid(af) * b.astype(f32)).astype(bf16)
        ho_ref[...] = hin + 0.5 * _dot(hm, wd_ref[...])

    ins = [h] + (list(pre) if pre is not None else []) + [g, wg, wu, wd]
    in_specs = [_rows(tm, D)]
    if pre is not None:
        in_specs += [_rows(tm, pre[0].shape[1]), _rows(tm, pre[1].shape[1]), _full(pre[2].shape)]
    in_specs += [_full(g.shape), _full(wg.shape), _full(wu.shape), _full(wd.shape)]
    outs = [jax.ShapeDtypeStruct((T, D), f32), jax.ShapeDtypeStruct((T, D), bf16),
            jax.ShapeDtypeStruct((T, F), bf16), jax.ShapeDtypeStruct((T, F), bf16)]
    out_specs = [_rows(tm, D), _rows(tm, D), _rows(tm, F), _rows(tm, F)]
    if pre is not None:
        outs = [jax.ShapeDtypeStruct((T, D), f32)] + outs
        out_specs = [_rows(tm, D)] + out_specs
    return pl.pallas_call(body, name=name, grid=(T // tm,), in_specs=in_specs, out_specs=out_specs,
                          out_shape=outs, compiler_params=_params(("parallel",)))(*ins)


def _ffn_bwd(name, dh, hin, g, a, b, wg, wu, wd, wo=None, ga=0, tm=256):
    T, D = dh.shape
    F = wg.shape[0]
    tm = min(tm, T)

    def body(*refs):
        if wo is None:
            (dh_ref, hin_ref, g_ref, a_ref, b_ref, wg_ref, wu_ref, wd_ref,
             dhi_ref, da_ref, db_ref, hm_ref, dg_ref) = refs
        else:
            (dh_ref, hin_ref, g_ref, a_ref, b_ref, wg_ref, wu_ref, wd_ref, wo_ref,
             dhi_ref, da_ref, db_ref, hm_ref, dg_ref, dya_ref, dyb_ref) = refs

        @pl.when(pl.program_id(0) == 0)
        def _():
            dg_ref[...] = jnp.zeros_like(dg_ref)

        dh_ = dh_ref[...]
        dhb = (0.5 * dh_).astype(bf16)
        dhm = _dot_nt(dhb, wd_ref[...])
        af = a_ref[...].astype(f32)
        bf = b_ref[...].astype(f32)
        sg = _sigmoid(af)
        sl_ = af * sg
        da = (dhm * bf * (sg * (1.0 + af * (1.0 - sg)))).astype(bf16)
        db = (dhm * sl_).astype(bf16)
        da_ref[...] = da
        db_ref[...] = db
        hm_ref[...] = (sl_ * bf).astype(bf16)
        dn = _dot(da, wg_ref[...]) + _dot(db, wu_ref[...])
        _, vjp = jax.vjp(_rms, hin_ref[...], g_ref[...])
        dx, dg = vjp(dn)
        dhi = dh_ + dx
        dhi_ref[...] = dhi
        dg_ref[...] += dg
        if wo is not None:
            dhib = dhi.astype(bf16)
            dya_ref[...] = _dot_nt(dhib, wo_ref[:ga, :]).astype(bf16)
            dyb_ref[...] = _dot_nt(dhib, wo_ref[ga:, :]).astype(bf16)

    ins = [dh, hin, g, a, b, wg, wu, wd]
    in_specs = [_rows(tm, D), _rows(tm, D), _full(g.shape), _rows(tm, F), _rows(tm, F),
                _full(wg.shape), _full(wu.shape), _full(wd.shape)]
    act = jax.ShapeDtypeStruct((T, F), bf16)
    outs = [jax.ShapeDtypeStruct((T, D), f32), act, act, act, jax.ShapeDtypeStruct(g.shape, f32)]
    out_specs = [_rows(tm, D), _rows(tm, F), _rows(tm, F), _rows(tm, F), _acc(g.shape)]
    if wo is not None:
        gb = wo.shape[0] - ga
        ins += [wo]
        in_specs += [_full(wo.shape)]
        outs += [jax.ShapeDtypeStruct((T, ga), bf16), jax.ShapeDtypeStruct((T, gb), bf16)]
        out_specs += [_rows(tm, ga), _rows(tm, gb)]
    return pl.pallas_call(body, name=name, grid=(T // tm,), in_specs=in_specs, out_specs=out_specs,
                          out_shape=outs, compiler_params=_params(("arbitrary",)))(*ins)


def _matmul_tn(name, a, b, scale=1.0, tk=2048):
    T, M = a.shape
    N = b.shape[1]
    tk = min(tk, T)
    nk = T // tk
    tn = LANES * max(d for d in range(1, N // LANES + 1) if (N // LANES) % d == 0 and (d == 1 or M * d * LANES * 4 <= 6 * 1024 * 1024))

    def body(a_ref, b_ref, o_ref, acc):
        k = pl.program_id(1)

        @pl.when(k == 0)
        def _():
            acc[...] = jnp.zeros_like(acc)

        bb = b_ref[...]
        if scale != 1.0:
            bb = bb * scale
        acc[...] += _dot_tn(a_ref[...].astype(bf16), bb.astype(bf16))

        @pl.when(k == nk - 1)
        def _():
            o_ref[...] = acc[...].astype(bf16)

    return pl.pallas_call(
        body, name=name, grid=(N // tn, nk),
        in_specs=[pl.BlockSpec((tk, M), lambda j, k: (k, 0)), pl.BlockSpec((tk, tn), lambda j, k: (k, j))],
        out_specs=pl.BlockSpec((M, tn), lambda j, k: (0, j)),
        out_shape=jax.ShapeDtypeStruct((M, N), bf16), scratch_shapes=[pltpu.VMEM((M, tn), f32)],
        compiler_params=_params(("parallel", "arbitrary")))(a, b)


def _gelu_and_slope(x):
    cdf = 0.5 * (1.0 + lax.erf(x * 0.7071067811865476))
    return x * cdf, cdf + x * (0.3989422804014327 * jnp.exp(-0.5 * x * x))


def _tril_mask():
    r = lax.broadcasted_iota(jnp.int32, (CHUNK, CHUNK), 0)
    c = lax.broadcasted_iota(jnp.int32, (CHUNK, CHUNK), 1)
    return c <= r


def _gm_mix(vnb, ws_ref, bst, mixed_sc, tm):
    mask = _tril_mask()
    for h in range(GM_HEADS):
        wt = jnp.where(mask, ws_ref[h], 0.0).astype(bf16)
        bias = bst[:, h:h + 1]
        for q in range(tm // CHUNK):
            rs = slice(q * CHUNK, (q + 1) * CHUNK)
            cs = slice(h * CHUNK, (h + 1) * CHUNK)
            mixed_sc[rs, cs] = _dot(wt, vnb[rs, cs]) + bias


def _mix_fwd(h1, gmix, w_uv, w_zxd, ln_g, ln_b, w_s, b_st, gout, tm=512):
    T, D = h1.shape
    tm = min(tm, T)
    G = GM_WIDTH

    def body(h_ref, g_ref, wuv_ref, wzxd_ref, lng_ref, lnb_ref, ws_ref, bst_ref, gout_ref,
             n_ref, act_ref, slope_ref, z_ref, xbc_ref, dt_ref, ya_ref, mixed_sc):
        n = _rms(h_ref[...], g_ref[...]).astype(bf16)
        n_ref[...] = n
        uv = _dot_nt(n, wuv_ref[...]).astype(bf16)
        zxd = _dot_nt(n, wzxd_ref[...])
        z_ref[...] = zxd[:, :SSM_WIDTH].astype(bf16)
        xbc_ref[...] = zxd[:, SSM_WIDTH:SSM_WIDTH + CONV_DIM].astype(bf16)
        dt_ref[...] = zxd[:, SSM_WIDTH + CONV_DIM:]
        act, slope = _gelu_and_slope(uv.astype(f32))
        act = act.astype(bf16)
        act_ref[...] = act
        slope_ref[...] = slope.astype(bf16)
        ug, vg = act[:, :G].astype(f32), act[:, G:].astype(f32)
        _gm_mix(_layernorm(vg, lng_ref[...], lnb_ref[...]).astype(bf16), ws_ref, bst_ref[...], mixed_sc, tm)
        ya_ref[...] = _rms(ug * mixed_sc[...], gout_ref[...]).astype(bf16)

    ins = [h1, gmix, w_uv, w_zxd, ln_g, ln_b, w_s, b_st, gout]
    in_specs = [_rows(tm, D)] + [_full(x.shape) for x in ins[1:]]
    outs = [jax.ShapeDtypeStruct((T, D), bf16), jax.ShapeDtypeStruct((T, 2 * G), bf16), jax.ShapeDtypeStruct((T, 2 * G), bf16),
            jax.ShapeDtypeStruct((T, SSM_WIDTH), bf16), jax.ShapeDtypeStruct((T, CONV_DIM), bf16),
            jax.ShapeDtypeStruct((T, LANES), f32), jax.ShapeDtypeStruct((T, G), bf16)]
    out_specs = [_rows(tm, D), _rows(tm, 2 * G), _rows(tm, 2 * G), _rows(tm, SSM_WIDTH), _rows(tm, CONV_DIM), _rows(tm, LANES),
                 _rows(tm, G)]
    return pl.pallas_call(body, name="mix_fwd", grid=(T // tm,), in_specs=in_specs, out_specs=out_specs,
                          out_shape=outs, scratch_shapes=[pltpu.VMEM((tm, G), f32)],
                          compiler_params=_params(("parallel",)))(*ins)


def _mix_bwd(dh, h1, gmix, act, slope, dya, dzxd, w_uv, w_zxd, ln_g, ln_b, w_s, b_st, gout, tm=256):
    T, D = dh.shape
    tm = min(tm, T)
    G = GM_WIDTH

    def body(dh_ref, h_ref, g_ref, act_ref, slope_ref, dya_ref, dzxd_ref, wuv_ref, wzxd_ref, lng_ref, lnb_ref, ws_ref,
             bst_ref, gout_ref, dhi_ref, duv_ref, dg_ref, dlng_ref, dlnb_ref, dws_ref, dbst_ref, dgout_ref, mixed_sc, dvn_sc):
        @pl.when(pl.program_id(0) == 0)
        def _():
            for r in (dg_ref, dlng_ref, dlnb_ref, dws_ref, dbst_ref, dgout_ref):
                r[...] = jnp.zeros_like(r)

        dn_z = _dot(dzxd_ref[...], wzxd_ref[...])
        ug = act_ref[:, :G].astype(f32)
        vn, ln_vjp = jax.vjp(_layernorm, act_ref[:, G:].astype(f32), lng_ref[...], lnb_ref[...])
        vnb = vn.astype(bf16)
        _gm_mix(vnb, ws_ref, bst_ref[...], mixed_sc, tm)
        mixed = mixed_sc[...]
        _, out_vjp = jax.vjp(_rms, ug * mixed, gout_ref[...])
        dpre, dgout = out_vjp(dya_ref[...].astype(f32))
        dgout_ref[...] += dgout
        dug = dpre * mixed
        dmixed = dpre * ug
        mask = _tril_mask()
        lane = lax.broadcasted_iota(jnp.int32, (1, GM_HEADS), 1)
        dbst = jnp.zeros((CHUNK, GM_HEADS), f32)
        for h in range(GM_HEADS):
            wt = jnp.where(mask, ws_ref[h], 0.0).astype(bf16)
            cs = slice(h * CHUNK, (h + 1) * CHUNK)
            dw = jnp.zeros((CHUNK, CHUNK), f32)
            for q in range(tm // CHUNK):
                rs = slice(q * CHUNK, (q + 1) * CHUNK)
                dm = dmixed[rs, cs]
                dmb = dm.astype(bf16)
                dw = dw + _dot_nt(dmb, vnb[rs, cs])
                dbst = dbst + jnp.sum(dm, axis=1, keepdims=True) * (lane == h).astype(f32)
                dvn_sc[rs, cs] = _dot_tn(wt, dmb)
            dws_ref[h] += jnp.where(mask, dw, 0.0)
        dbst_ref[...] += dbst
        dvg, dlng, dlnb = ln_vjp(dvn_sc[...])
        duv = (jnp.concatenate([dug, dvg], axis=1) * slope_ref[...].astype(f32)).astype(bf16)
        duv_ref[...] = duv
        dlng_ref[...] += dlng
        dlnb_ref[...] += dlnb
        dn = dn_z + _dot(duv, wuv_ref[...])
        _, vjp = jax.vjp(_rms, h_ref[...], g_ref[...])
        dx, dg = vjp(dn)
        dhi_ref[...] = dh_ref[...] + dx
        dg_ref[...] += dg

    ins = [dh, h1, gmix, act, slope, dya, dzxd, w_uv, w_zxd, ln_g, ln_b, w_s, b_st, gout]
    in_specs = ([_rows(tm, D), _rows(tm, D), _full(gmix.shape), _rows(tm, 2 * G), _rows(tm, 2 * G), _rows(tm, G),
                 _rows(tm, dzxd.shape[1])] + [_full(x.shape) for x in ins[7:]])
    accs = (gmix, ln_g, ln_b, w_s, b_st, gout)
    outs = ([jax.ShapeDtypeStruct((T, D), f32), jax.ShapeDtypeStruct((T, 2 * G), bf16)]
            + [jax.ShapeDtypeStruct(x.shape, f32) for x in accs])
    out_specs = [_rows(tm, D), _rows(tm, 2 * G)] + [_acc(x.shape) for x in accs]
    return pl.pallas_call(body, name="mix_bwd", grid=(T // tm,), in_specs=in_specs, out_specs=out_specs,
                          out_shape=outs, scratch_shapes=[pltpu.VMEM((tm, G), f32), pltpu.VMEM((tm, G), f32)],
                          compiler_params=_params(("arbitrary",)))(*ins)


HALO = 16
SSD_SUB = 4


class _RowsOf:
    def __init__(self, ref, rows):
        self.ref, self.rows = ref, rows

    def _index(self, idx):
        return (self.rows, slice(None)) if idx is Ellipsis else (self.rows,) + tuple(idx[1:])

    def __getitem__(self, idx):
        return self.ref[self._index(idx)]

    def __setitem__(self, idx, value):
        self.ref[self._index(idx)] = value
PAIRS = SSM_HEADS // 2
PAIR_W = 2 * SSM_HEAD_DIM


def _split(x, n):
    parts = []
    for _ in range(n):
        p = x.astype(bf16)
        parts.append(p)
        x = x - p.astype(f32)
    return parts


def _dot_sel(x, sel_n, n):
    return _dot(jnp.concatenate(_split(x, n), axis=1), sel_n)


def _sel_dot(sel, x, n):
    return _dot(jnp.concatenate([sel] * n, axis=1), jnp.concatenate(_split(x, n), axis=0))


EXPAND_SPLIT = 3
REDUCE_SPLIT = 2


def _head_mats():
    ex = (jnp.arange(SSM_WIDTH)[None, :] // SSM_HEAD_DIM == jnp.arange(LANES)[:, None]).astype(bf16)
    return jnp.tile(ex, (EXPAND_SPLIT, 1)), jnp.tile(ex.T, (REDUCE_SPLIT, 1))


def _shift_mat(rows, cols, off):
    r = lax.broadcasted_iota(jnp.int32, (rows, cols), 0)
    c = lax.broadcasted_iota(jnp.int32, (rows, cols), 1)
    return (c == r + off).astype(bf16)


def _ssd_conv(halo, x, cw_ref, cb_ref):
    ext = jnp.concatenate([halo, x], axis=0)
    xc = cb_ref[...] + cw_ref[SSM_CONV - 1:SSM_CONV, :] * x.astype(f32)
    for j in range(SSM_CONV - 1):
        xc = xc + cw_ref[j:j + 1, :] * _dot(_shift_mat(CHUNK, HALO + CHUNK, HALO - SSM_CONV + 1 + j), ext)
    return xc


def _ssd_front(dtr, dtb_ref, alog_ref):
    dt = _softplus(dtr + dtb_ref[...])
    a = -jnp.exp(alog_ref[...])
    acs = jnp.dot(_tril_mask().astype(f32), dt * a, preferred_element_type=f32, precision=HIGHEST)
    return dt, a, acs


def _ssd_wide(xa, dt, acs, dsk, ex):
    dt_x = _dot_sel(dt, ex, EXPAND_SPLIT)
    acs_x = _dot_sel(acs, ex, EXPAND_SPLIT)
    dsk_x = _dot_sel(jnp.broadcast_to(dsk, (8, LANES)), ex, EXPAND_SPLIT)[0:1]
    e_x = jnp.exp(acs_x)
    r_x = jnp.exp(acs_x[CHUNK - 1:CHUNK, :] - acs_x)
    xs = xa[:, :SSM_WIDTH]
    xd = xs * dt_x
    return dt_x, dsk_x, e_x, r_x, xs, xd, xd * r_x


def _pair_stack(v, lo):
    return jnp.concatenate([jnp.where(lo, v, 0.0), jnp.where(lo, 0.0, v)], axis=0)


def _ssd_pair(j, acs, acs_t, cb):
    out = []
    tril = _tril_mask()
    for h in (2 * j, 2 * j + 1):
        dk = jnp.exp(jnp.where(tril, acs[:, h:h + 1] - acs_t[h:h + 1, :], -jnp.inf))
        out.append((dk, cb * dk))
    return out


def _pair_col(row_lo, tot, j):
    return jnp.exp(jnp.where(row_lo, tot[:, 2 * j:2 * j + 1], tot[:, 2 * j + 1:2 * j + 2]))


def _gated_norm(y, z, g):
    yg = y * (z * _sigmoid(z))
    half = SSM_WIDTH // SSM_GROUPS
    parts = []
    for k in range(SSM_GROUPS):
        s = yg[:, k * half:(k + 1) * half]
        parts.append(s * lax.rsqrt(jnp.mean(s * s, axis=-1, keepdims=True) + EPS))
    return jnp.concatenate(parts, axis=1) * g


def _group_mats(xa):
    out = []
    for g in range(SSM_GROUPS):
        bm = xa[:, SSM_WIDTH + g * SSM_STATE:SSM_WIDTH + (g + 1) * SSM_STATE].astype(bf16)
        cm = xa[:, SSM_WIDTH + (SSM_GROUPS + g) * SSM_STATE:SSM_WIDTH + (SSM_GROUPS + g + 1) * SSM_STATE].astype(bf16)
        out.append((cm, bm, _dot_nt(cm, bm)))
    return out


def _ssd_fwd(xbc, z, dtr, conv_w, conv_b, dt_bias, a_log, d_skip, ssm_norm):
    T = xbc.shape[0]
    nc = T // CHUNK
    N = SSM_STATE

    def body(xbc_ref, halo_ref, z_ref, dtr_ref, cw_ref, cb_ref, dtb_ref, alog_ref, dsk_ref, g_ref, ex_ref,
             yb_ref, xc_ref, sg_ref, y_ref, sprev_ref, s_sc):
        i = pl.program_id(0)

        @pl.when(i == 0)
        def _():
            s_sc[...] = jnp.zeros_like(s_sc)

        lo = lax.broadcasted_iota(jnp.int32, (CHUNK, PAIR_W), 1) < SSM_HEAD_DIM
        row_lo = lax.broadcasted_iota(jnp.int32, (PAIR_W, 1), 0) < SSM_HEAD_DIM
        for k in range(SUB):
            rs = slice(k * CHUNK, (k + 1) * CHUNK)
            if k == 0:
                halo = halo_ref[...]
                halo = jnp.where(i > 0, halo, jnp.zeros_like(halo))
            else:
                halo = xbc_ref[k * CHUNK - HALO:k * CHUNK, :]
            xc = _ssd_conv(halo, xbc_ref[rs, :], cw_ref, cb_ref)
            sg = _sigmoid(xc)
            xc_ref[rs, :] = xc
            sg_ref[rs, :] = sg
            xa = xc * sg
            dt, _, acs = _ssd_front(dtr_ref[rs, :], dtb_ref, alog_ref)
            _, dsk_x, e_x, _, xs, xd, gm = _ssd_wide(xa, dt, acs, dsk_ref[...], ex_ref[...])
            acs_t = acs.T
            tot = acs[CHUNK - 1:CHUNK, :]
            groups = _group_mats(xa)
            ys = []
            for j in range(PAIRS):
                cmb, bmb, cb = groups[j // (PAIRS // SSM_GROUPS)]
                ps = slice(j * PAIR_W, (j + 1) * PAIR_W)
                (_, m0), (_, m1) = _ssd_pair(j, acs, acs_t, cb)
                sp = s_sc[j]
                yd = _dot(jnp.concatenate([m0, m1], axis=1).astype(bf16), _pair_stack(xd[:, ps], lo).astype(bf16))
                ys.append(yd + e_x[:, ps] * _dot_nt(cmb, sp.astype(bf16)))
                sprev_ref[k, j] = sp
                s_sc[j] = _pair_col(row_lo, tot, j) * sp + _dot_tn(gm[:, ps].astype(bf16), bmb)
            y = jnp.concatenate(ys, axis=1) + xs * dsk_x
            y_ref[rs, :] = y
            yb_ref[rs, :] = _gated_norm(y, z_ref[rs, :].astype(f32), g_ref[...]).astype(bf16)

    params = [conv_w, conv_b, dt_bias, a_log, d_skip, ssm_norm, _head_mats()[0]]
    SUB = SSD_SUB if nc % SSD_SUB == 0 else 1
    hp = SUB * CHUNK // HALO
    R = SUB * CHUNK
    in_specs = [_rows(R, CONV_DIM), pl.BlockSpec((HALO, CONV_DIM), lambda i: (jnp.maximum(i * hp - 1, 0), 0)),
                _rows(R, SSM_WIDTH), _rows(R, LANES)] + [_full(x.shape) for x in params]
    return pl.pallas_call(
        body, name="ssd_fwd", grid=(nc // SUB,), in_specs=in_specs,
        out_specs=[_rows(R, SSM_WIDTH), _rows(R, CONV_DIM), _rows(R, CONV_DIM), _rows(R, SSM_WIDTH),
                   pl.BlockSpec((SUB, PAIRS, PAIR_W, N), lambda i: (i, 0, 0, 0))],
        out_shape=[jax.ShapeDtypeStruct((T, SSM_WIDTH), bf16), jax.ShapeDtypeStruct((T, CONV_DIM), f32),
                   jax.ShapeDtypeStruct((T, CONV_DIM), f32), jax.ShapeDtypeStruct((T, SSM_WIDTH), f32),
                   jax.ShapeDtypeStruct((nc, PAIRS, PAIR_W, N), f32)],
        scratch_shapes=[pltpu.VMEM((PAIRS, PAIR_W, N), f32)],
        compiler_params=_params(("arbitrary",)))(xbc, xbc, z, dtr, *params)


def _ssd_bwd(xbc, xc, sg, y, z, dtr, sprev, dyb, conv_w, conv_b, dt_bias, a_log, d_skip, ssm_norm):
    T = xbc.shape[0]
    nc = T // CHUNK
    H, N = SSM_HEADS, SSM_STATE
    PG = PAIRS // SSM_GROUPS

    def chunk(xbc_ref, xc_ref, sg_ref, y_ref, z_ref, dtr_ref, sprev_k, dyb_ref, cw_ref, cb_ref, dtb_ref, alog_ref, dsk_ref,
              g_ref, ex_ref, rd_ref, dzxd_ref, dcw_ref, dcb_ref, ddtb_ref, dalog_ref, ddsk_ref, dg_ref, ds_sc, next_sc):
        xc = xc_ref[...]
        sg = sg_ref[...]
        xa = xc * sg
        dt, a, acs = _ssd_front(dtr_ref[...], dtb_ref, alog_ref)
        dt_x, dsk_x, e_x, r_x, xs, xd, gm = _ssd_wide(xa, dt, acs, dsk_ref[...], ex_ref[...])
        acs_t = acs.T
        tot = acs[CHUNK - 1:CHUNK, :]
        groups = _group_mats(xa)
        lo = lax.broadcasted_iota(jnp.int32, (CHUNK, PAIR_W), 1) < SSM_HEAD_DIM
        row_lo = lax.broadcasted_iota(jnp.int32, (PAIR_W, 1), 0) < SSM_HEAD_DIM
        pairs, zs = [], []
        for j in range(PAIRS):
            cmb, _, cb = groups[j // PG]
            pairs.append(_ssd_pair(j, acs, acs_t, cb))
            zs.append(_dot_nt(cmb, sprev_k[j].astype(bf16)))
        zf = jnp.concatenate(zs, axis=1)
        _, gn_vjp = jax.vjp(_gated_norm, y_ref[...], z_ref[...].astype(f32), g_ref[...])
        dy, dz, dg = gn_vjp(dyb_ref[...].astype(f32))
        dg_ref[...] += dg
        dzxd_ref[:, :SSM_WIDTH] = dz.astype(bf16)

        lane = lax.broadcasted_iota(jnp.int32, (1, LANES), 1)
        sub = lax.broadcasted_iota(jnp.int32, (LANES, 1), 0)
        dacs = jnp.zeros((CHUNK, LANES), f32)
        dacs_r = jnp.zeros((LANES, CHUNK), f32)
        dtot = jnp.zeros((1, LANES), f32)
        dcb = [jnp.zeros((CHUNK, CHUNK), f32) for _ in range(SSM_GROUPS)]
        dcm = [jnp.zeros((CHUNK, N), f32) for _ in range(SSM_GROUPS)]
        dbm = [jnp.zeros((CHUNK, N), f32) for _ in range(SSM_GROUPS)]
        dxds, dgms = [], []
        for j in range(PAIRS):
            g = j // PG
            cmb, bmb, _ = groups[g]
            ps = slice(j * PAIR_W, (j + 1) * PAIR_W)
            (dk0, m0), (dk1, m1) = pairs[j]
            oh0, oh1 = (lane == 2 * j).astype(f32), (lane == 2 * j + 1).astype(f32)
            dyp = dy[:, ps]
            dy2 = _pair_stack(dyp, lo).astype(bf16)
            dm2 = _dot_nt(dy2, xd[:, ps].astype(bf16))
            m2 = jnp.concatenate([m0, m1], axis=0)
            dxds.append(_dot_tn(m2.astype(bf16), dy2))
            w2 = dm2 * m2
            rs = jnp.sum(w2, axis=1, keepdims=True)
            dacs = dacs + rs[:CHUNK] * oh0 + rs[CHUNK:] * oh1
            dacs_r = dacs_r - ((sub == 2 * j).astype(f32) * jnp.sum(w2[:CHUNK], axis=0, keepdims=True)
                               + (sub == 2 * j + 1).astype(f32) * jnp.sum(w2[CHUNK:], axis=0, keepdims=True))
            dcb[g] = dcb[g] + dm2[:CHUNK] * dk0 + dm2[CHUNK:] * dk1
            sp = sprev_k[j]
            dzb = (dyp * e_x[:, ps]).astype(bf16)
            dcm[g] = dcm[g] + _dot(dzb, sp.astype(bf16))
            dsn = ds_sc[j]
            dsnb = dsn.astype(bf16)
            et = _pair_col(row_lo, tot, j)
            rr = jnp.sum(dsn * sp, axis=1, keepdims=True) * et
            dtot = dtot + jnp.sum(rr[:SSM_HEAD_DIM]) * oh0 + jnp.sum(rr[SSM_HEAD_DIM:]) * oh1
            dgms.append(_dot_nt(bmb, dsnb))
            dbm[g] = dbm[g] + _dot(gm[:, ps].astype(bf16), dsnb)
            ds_sc[j] = _dot_tn(dzb, cmb) + et * dsn
        dgm = jnp.concatenate(dgms, axis=1)
        dxd = jnp.concatenate(dxds, axis=1) + dgm * r_x
        dr = dgm * gm
        red = _dot_sel(jnp.concatenate([dy * e_x * zf - dr, dr, dxd * xs, dy * xs], axis=0), rd_ref[...], REDUCE_SPLIT)
        rowi = lax.broadcasted_iota(jnp.int32, (CHUNK, 1), 0)
        dtot = dtot + jnp.sum(red[CHUNK:2 * CHUNK], axis=0, keepdims=True)
        dacs = dacs + red[:CHUNK] + dacs_r.T + jnp.where(rowi == CHUNK - 1, dtot, 0.0)
        r2 = lax.broadcasted_iota(jnp.int32, (CHUNK, CHUNK), 0)
        c2 = lax.broadcasted_iota(jnp.int32, (CHUNK, CHUNK), 1)
        dadt = jnp.dot((c2 >= r2).astype(f32), dacs, preferred_element_type=f32, precision=HIGHEST)
        ddt = red[2 * CHUNK:3 * CHUNK] + dadt * a
        dalog_ref[...] += jnp.sum(dadt * dt, axis=0, keepdims=True) * a
        ddsk_ref[...] += jnp.sum(red[3 * CHUNK:], axis=0, keepdims=True)
        ddtr = jnp.where(lane < H, ddt * _sigmoid(dtr_ref[...] + dtb_ref[...]), 0.0)
        ddtb_ref[...] += jnp.sum(ddtr, axis=0, keepdims=True)
        dzxd_ref[:, SSM_WIDTH + CONV_DIM:] = ddtr.astype(bf16)
        dxa_bm, dxa_cm = [], []
        for g in range(SSM_GROUPS):
            cmb, bmb, _ = groups[g]
            dcbb = dcb[g].astype(bf16)
            dxa_bm.append(dbm[g] + _dot_tn(dcbb, cmb))
            dxa_cm.append(dcm[g] + _dot(dcbb, bmb))
        dxc = jnp.concatenate([dy * dsk_x + dxd * dt_x] + dxa_bm + dxa_cm, axis=1) * (sg * (1.0 + xc * (1.0 - sg)))
        ext = jnp.concatenate([dxc, next_sc[...]], axis=0)
        xin = xbc_ref[...].astype(f32)
        dxbc = cw_ref[SSM_CONV - 1:SSM_CONV, :] * dxc
        dcw = [jnp.sum(dxc * xin, axis=0, keepdims=True)]
        for s in range(1, SSM_CONV):
            later = _sel_dot(_shift_mat(CHUNK, CHUNK + HALO, s), ext, 2)
            dxbc = dxbc + cw_ref[SSM_CONV - 1 - s:SSM_CONV - s, :] * later
            dcw.insert(0, jnp.sum(later * xin, axis=0, keepdims=True))
        dzxd_ref[:, SSM_WIDTH:SSM_WIDTH + CONV_DIM] = dxbc.astype(bf16)
        dcw_ref[...] += jnp.concatenate(dcw, axis=0)
        dcb_ref[...] += jnp.sum(dxc, axis=0, keepdims=True)
        next_sc[...] = dxc[0:HALO, :]

    SUB = SSD_SUB if nc % SSD_SUB == 0 else 1
    nb = nc // SUB

    def body(xbc_ref, xc_ref, sg_ref, y_ref, z_ref, dtr_ref, sprev_ref, dyb_ref, cw_ref, cb_ref, dtb_ref, alog_ref, dsk_ref,
             g_ref, ex_ref, rd_ref, dzxd_ref, dcw_ref, dcb_ref, ddtb_ref, dalog_ref, ddsk_ref, dg_ref, ds_sc, next_sc):
        @pl.when(pl.program_id(0) == 0)
        def _():
            ds_sc[...] = jnp.zeros_like(ds_sc)
            next_sc[...] = jnp.zeros_like(next_sc)
            for r_ in (dcw_ref, dcb_ref, ddtb_ref, dalog_ref, ddsk_ref, dg_ref):
                r_[...] = jnp.zeros_like(r_)

        for k in reversed(range(SUB)):
            rows = slice(k * CHUNK, (k + 1) * CHUNK)
            tok = [_RowsOf(r_, rows) for r_ in (xbc_ref, xc_ref, sg_ref, y_ref, z_ref, dtr_ref)]
            chunk(*tok, sprev_ref.at[k], _RowsOf(dyb_ref, rows), cw_ref, cb_ref, dtb_ref, alog_ref, dsk_ref, g_ref, ex_ref,
                  rd_ref, _RowsOf(dzxd_ref, rows), dcw_ref, dcb_ref, ddtb_ref, dalog_ref, ddsk_ref, dg_ref, ds_sc, next_sc)

    params = [conv_w, conv_b, dt_bias, a_log, d_skip, ssm_norm]
    mats = list(_head_mats())

    def rev(ncols):
        return pl.BlockSpec((SUB * CHUNK, ncols), lambda i: (nb - 1 - i, 0))

    in_specs = ([rev(CONV_DIM), rev(CONV_DIM), rev(CONV_DIM), rev(SSM_WIDTH), rev(SSM_WIDTH), rev(LANES),
                 pl.BlockSpec((SUB, PAIRS, PAIR_W, N), lambda i: (nb - 1 - i, 0, 0, 0)), rev(SSM_WIDTH)]
                + [_full(x.shape) for x in params + mats])
    return pl.pallas_call(
        body, name="ssd_bwd", grid=(nb,), in_specs=in_specs,
        out_specs=[rev(ZXD)] + [_acc(x.shape) for x in params],
        out_shape=[jax.ShapeDtypeStruct((T, ZXD), bf16)] + [jax.ShapeDtypeStruct(x.shape, f32) for x in params],
        scratch_shapes=[pltpu.VMEM((PAIRS, PAIR_W, N), f32), pltpu.VMEM((HALO, CONV_DIM), f32)],
        compiler_params=_params(("arbitrary",)))(xbc, xc, sg, y, z, dtr, sprev, dyb, *params, *mats)


def _tail(h3, p, tgt, gp, wpg, bpg, wpp, gf, tm=512):
    T, D = h3.shape
    tm = min(tm, T)

    def head(gpre, pp, h, gf_, t):
        gate = _sigmoid(gpre)
        y = _rms(h + gate * pp, gf_)
        err = y - t
        return 0.5 * jnp.sum(jnp.mean(err * err, axis=-1))

    def body(h_ref, p_ref, t_ref, gp_ref, wpg_ref, bpg_ref, wpp_ref, gf_ref,
             dh_ref, loss_ref, dgp_ref, dwpg_ref, dbpg_ref, dwpp_ref, dgf_ref):
        @pl.when(pl.program_id(0) == 0)
        def _():
            for r in (loss_ref, dgp_ref, dwpg_ref, dbpg_ref, dwpp_ref, dgf_ref):
                r[...] = jnp.zeros_like(r)

        h = h_ref[...]
        npf, np_vjp = jax.vjp(_rms, h, gp_ref[...])
        npb = npf.astype(bf16)
        pb = p_ref[...].astype(bf16)
        gpre = _dot(npb, wpg_ref[...]) + bpg_ref[...]
        kp, _, cp = wpp_ref.shape
        pp = jnp.concatenate([_dot(pb, wpp_ref[k]) for k in range(kp)], axis=1)
        loss, head_vjp = jax.vjp(head, gpre, pp, h, gf_ref[...], t_ref[...])
        dgpre, dpp, dh_a, dgf, _ = head_vjp(jnp.ones((), f32))
        loss_ref[...] += loss
        dgf_ref[...] += dgf
        dbpg_ref[...] += jnp.sum(dgpre, axis=0, keepdims=True)
        dgb = dgpre.astype(bf16)
        dwpg_ref[...] += _dot_tn(npb, dgb)
        dppb = dpp.astype(bf16)
        for k in range(kp):
            dwpp_ref[k] += _dot_tn(pb, dppb[:, k * cp:(k + 1) * cp])
        dh_b, dgp = np_vjp(_dot_nt(dgb, wpg_ref[...]))
        dgp_ref[...] += dgp
        dh_ref[...] = dh_a + dh_b

    ins = [h3, p, tgt, gp, wpg, bpg, wpp, gf]
    in_specs = [_rows(tm, D), _rows(tm, p.shape[1]), _rows(tm, D)] + [_full(x.shape) for x in ins[3:]]
    acc_shapes = [(1, LANES), gp.shape, wpg.shape, bpg.shape, wpp.shape, gf.shape]
    return pl.pallas_call(
        body, name="tail", grid=(T // tm,), in_specs=in_specs,
        out_specs=[_rows(tm, D)] + [_acc(s) for s in acc_shapes],
        out_shape=[jax.ShapeDtypeStruct((T, D), f32)] + [jax.ShapeDtypeStruct(s, f32) for s in acc_shapes],
        compiler_params=_params(("arbitrary",)))(*ins)


def _adamw(name, w, g, m, v, tr=256):
    R, rest = w.shape[0], w.shape[1:]
    tr = _row_tile(R, tr, 8 if len(rest) == 1 else 1)

    def body(w_ref, g_ref, m_ref, v_ref, d_ref, mo_ref, vo_ref):
        g_ = g_ref[...]
        m_ = ADAM_B1 * m_ref[...] + (1.0 - ADAM_B1) * g_
        v_ = ADAM_B2 * v_ref[...] + (1.0 - ADAM_B2) * jnp.square(g_)
        m_hat = m_ / (1.0 - ADAM_B1 ** ADAM_STEP)
        v_hat = v_ / (1.0 - ADAM_B2 ** ADAM_STEP)
        d_ref[...] = -ADAM_LR * (m_hat / (jnp.sqrt(v_hat) + ADAM_EPS) + ADAM_WD * w_ref[...])
        mo_ref[...] = m_
        vo_ref[...] = v_

    spec = pl.BlockSpec((tr,) + rest, lambda i: (i,) + (0,) * len(rest))
    return pl.pallas_call(body, name=name, grid=(R // tr,), in_specs=[spec] * 4, out_specs=[spec] * 3,
                          out_shape=[jax.ShapeDtypeStruct(w.shape, f32)] * 3,
                          compiler_params=_params(("parallel",)))(w, g, m, v)


HBM = pl.BlockSpec(memory_space=pltpu.HBM)


def _me():
    return lax.axis_index("x"), lax.axis_index("y"), lax.axis_index("c")


def _other_chips(x, y):
    return [(1 - x, y), (x, 1 - y), (1 - x, 1 - y)]


def _remote(src, dst, send_sem, recv_sem, dev):
    return pltpu.make_async_remote_copy(src_ref=src, dst_ref=dst, send_sem=send_sem, recv_sem=recv_sem,
                                        device_id=dev, device_id_type=MESH)


def _sems(n):
    return [pltpu.SemaphoreType.DMA((n,)), pltpu.SemaphoreType.DMA((n,))]


def _gather_weights(shards, split):
    n = len(shards)

    def body(*refs):
        ins, outs = refs[:n], refs[n:2 * n]
        own_send, own_recv, ici_send, ici_recv, d2d_send, d2d_recv = refs[2 * n:]
        x, y, c = _me()
        my_chip = 2 * x + y
        sibling = (x, y, 1 - c)
        chips = _other_chips(x, y)

        def rows(i, half):
            hr = shards[i].shape[0] // 2
            return pl.ds(half * hr, hr) if split[i] else pl.ds(0, shards[i].shape[0])

        sends = []
        for i in range(n):
            for j, chip in enumerate(chips):
                cp = _remote(ins[i].at[rows(i, c)], outs[i].at[my_chip, rows(i, c)],
                             ici_send.at[3 * i + j], ici_recv.at[3 * i + j], (*chip, c))
                cp.start()
                sends.append(cp)
            cp = _remote(ins[i], outs[i].at[my_chip], own_send.at[i], own_recv.at[i], sibling)
            cp.start()
            sends.append(cp)
        for i in range(n):
            for j, chip in enumerate(chips):
                s = 3 * i + j
                land = outs[i].at[2 * chip[0] + chip[1], rows(i, c)]
                _remote(land, land, ici_send.at[s], ici_recv.at[s], (*chip, c)).wait_recv()
                if split[i]:
                    cp = _remote(land, land, d2d_send.at[s], d2d_recv.at[s], sibling)
                    cp.start()
                    sends.append(cp)
        for i in range(n):
            _remote(ins[i], outs[i].at[my_chip], own_send.at[i], own_recv.at[i], sibling).wait_recv()
            if split[i]:
                for j, chip in enumerate(chips):
                    s = 3 * i + j
                    land = outs[i].at[2 * chip[0] + chip[1], rows(i, 1 - c)]
                    _remote(land, land, d2d_send.at[s], d2d_recv.at[s], sibling).wait_recv()
        for cp in sends:
            cp.wait_send()

    return pl.pallas_call(
        body, name="gather_weights", out_shape=[jax.ShapeDtypeStruct((N_CHIPS,) + s.shape, s.dtype) for s in shards],
        in_specs=[HBM] * n, out_specs=[HBM] * n,
        scratch_shapes=_sems(n) + _sems(3 * n) + _sems(3 * n))(*shards)


def _swap_halves(name, grads):
    n = len(grads)

    def body(*refs):
        ins, outs, send, recv = refs[:n], refs[n:2 * n], refs[2 * n], refs[2 * n + 1]
        x, y, c = _me()
        copies = []
        for i in range(n):
            hr = grads[i].shape[1] // 2
            cp = _remote(ins[i].at[:, pl.ds((1 - c) * hr, hr), :], outs[i], send.at[i], recv.at[i], (x, y, 1 - c))
            cp.start()
            copies.append(cp)
        for cp in copies:
            cp.wait()

    return pl.pallas_call(
        body, name=name,
        out_shape=[jax.ShapeDtypeStruct((g.shape[0], g.shape[1] // 2, g.shape[2]), g.dtype) for g in grads],
        in_specs=[HBM] * n, out_specs=[HBM] * n, scratch_shapes=_sems(n))(*grads)


def _add_halves(name, grads, other, c_idx, th=HALF_ROWS_BF16):
    K, R, C = grads.shape
    H = R // 2
    th = _row_tile(H, th, 16)
    nb = H // th

    def body(c_ref, g_ref, o_ref, out_ref):
        out_ref[...] = (g_ref[...].astype(f32) + o_ref[...].astype(f32)).astype(bf16)

    grid_spec = pltpu.PrefetchScalarGridSpec(
        num_scalar_prefetch=1, grid=(nb,),
        in_specs=[pl.BlockSpec((K, th, C), lambda i, c: (0, c[0] * nb + i, 0)),
                  pl.BlockSpec((K, th, C), lambda i, c: (0, i, 0))],
        out_specs=pl.BlockSpec((K, th, C), lambda i, c: (0, i, 0)))
    return pl.pallas_call(body, name=name, grid_spec=grid_spec,
                          out_shape=jax.ShapeDtypeStruct((K, H, C), bf16),
                          compiler_params=_params(("parallel",)))(c_idx, grads, other)


SEM = pl.BlockSpec(memory_space=pltpu.SEMAPHORE)
ANY = pl.BlockSpec(memory_space=pl.ANY)
EFFECT = pltpu.SideEffectType.DATAFLOW_SIDE_EFFECTING


def _copies_start(name, srcs, land_shapes, n_copies, make_copies, after):
    ns, nl = len(srcs), len(land_shapes)
    lands = [lax.empty(s.shape, s.dtype) for s in land_shapes]

    def body(*refs):
        src_refs, land_refs = refs[:ns], refs[ns:ns + nl]
        send, recv, token = refs[ns + nl + 1], refs[ns + nl + 2], refs[-1]
        for cp in make_copies(src_refs, land_refs, send, recv):
            cp.start()
        token[...] = jnp.zeros_like(token)

    buffers = list(srcs) + lands
    out = pl.pallas_call(
        body, name=name,
        out_shape=(pltpu.SemaphoreType.DMA((n_copies,)), pltpu.SemaphoreType.DMA((n_copies,)),
                   *[pltpu.HBM(b.shape, b.dtype) for b in buffers], jax.ShapeDtypeStruct((8, LANES), f32)),
        in_specs=[HBM] * (ns + nl) + [ANY],
        out_specs=(SEM, SEM, *[HBM] * (ns + nl), pl.BlockSpec(memory_space=pltpu.VMEM)),
        input_output_aliases={i: 2 + i for i in range(ns + nl)},
        compiler_params=pltpu.CompilerParams(has_side_effects=EFFECT),
    )(*[pltpu.with_memory_space_constraint(b, pltpu.HBM) for b in buffers], after)
    return out[0], out[1], list(out[2:2 + ns]), list(out[2 + ns:2 + ns + nl]), out[-1]


def _copies_wait(name, started, make_copies, after):
    send, recv, srcs, lands, _ = started
    ns, nl = len(srcs), len(lands)
    after = list(after)

    def body(*refs):
        src_refs, land_refs = refs[:ns], refs[ns:ns + nl]
        for cp in make_copies(src_refs, land_refs, refs[ns + nl], refs[ns + nl + 1]):
            cp.wait_send()
            cp.wait_recv()

    buffers = list(srcs) + list(lands)
    out = pl.pallas_call(
        body, name=name, out_shape=tuple(pltpu.HBM(b.shape, b.dtype) for b in buffers),
        in_specs=[HBM] * (ns + nl) + [SEM, SEM] + [ANY] * len(after), out_specs=tuple([HBM] * (ns + nl)),
        input_output_aliases={i: i for i in range(ns + nl)},
        compiler_params=pltpu.CompilerParams(has_side_effects=EFFECT),
    )(*buffers, send, recv, *after)
    return list(out[:ns]), list(out[ns:])


def _gather_copies(src_refs, land_refs, send, recv):
    x, y, c = _me()
    my_chip = 2 * x + y
    peers = [(*chip, c) for chip in _other_chips(x, y)] + [(x, y, 1 - c)]
    return [_remote(src_refs[i], land_refs[i].at[my_chip], send.at[4 * i + j], recv.at[4 * i + j], peer)
            for i in range(len(src_refs)) for j, peer in enumerate(peers)]


def _swap_copies(src_refs, land_refs, send, recv):
    x, y, c = _me()
    copies = []
    for i in range(len(src_refs)):
        hr = src_refs[i].shape[1] // 2
        copies.append(_remote(src_refs[i].at[:, pl.ds((1 - c) * hr, hr), :], land_refs[i], send.at[i], recv.at[i], (x, y, 1 - c)))
    return copies


def _share_copies(src_refs, land_refs, send, recv):
    x, y, c = _me()
    return [_remote(src_refs[i], land_refs[i], send.at[i], recv.at[i], (x, y, 1 - c)) for i in range(len(src_refs))]


def _partial_copies(src_refs, land_refs, send, recv):
    x, y, c = _me()
    return [_remote(src_refs[i].at[2 * chip[0] + chip[1]], land_refs[i].at[j], send.at[3 * i + j], recv.at[3 * i + j], (*chip, c))
            for i in range(len(src_refs)) for j, chip in enumerate(_other_chips(x, y))]


def _small_copies(src_refs, land_refs, send, recv):
    x, y, c = _me()
    return [_remote(src_refs[0], land_refs[0].at[k - 1], send.at[k - 1], recv.at[k - 1], (x ^ (k >> 2), y ^ ((k >> 1) & 1), c ^ (k & 1)))
            for k in range(1, N_DEV)]


def _sum_small(own, slots, dev_idx):
    R, C = own.shape

    def body(dev_ref, own_ref, s_ref, o_ref):
        me = dev_ref[0]
        acc = jnp.zeros((R, C), f32)
        for d in range(N_DEV):
            k = me ^ d
            acc = acc + jnp.where(k == 0, own_ref[...], s_ref[jnp.maximum(k - 1, 0)])
        o_ref[...] = acc

    grid_spec = pltpu.PrefetchScalarGridSpec(
        num_scalar_prefetch=1, grid=(1,),
        in_specs=[pl.BlockSpec((R, C), lambda i, dev: (0, 0)), pl.BlockSpec((N_DEV - 1, R, C), lambda i, dev: (0, 0, 0))],
        out_specs=pl.BlockSpec((R, C), lambda i, dev: (0, 0)))
    return pl.pallas_call(body, name="sum_small", grid_spec=grid_spec, out_shape=jax.ShapeDtypeStruct((R, C), f32),
                          compiler_params=_params(("arbitrary",)))(dev_idx, own, slots)


def _sum_partials(name, part, recv, chip_idx, th=HALF_ROWS_BF16):
    K, H, C = part.shape
    th = _row_tile(H, th, 16)

    def body(chip_ref, p_ref, r_ref, o_ref):
        acc = p_ref[...].astype(f32)
        for j in range(3):
            acc = acc + r_ref[j].astype(f32)
        o_ref[...] = acc

    grid_spec = pltpu.PrefetchScalarGridSpec(
        num_scalar_prefetch=1, grid=(H // th,),
        in_specs=[pl.BlockSpec((None, th, C), lambda i, chip: (chip[0], i, 0)),
                  pl.BlockSpec((3, th, C), lambda i, chip: (0, i, 0))],
        out_specs=pl.BlockSpec((th, C), lambda i, chip: (i, 0)))
    return pl.pallas_call(body, name=name, grid_spec=grid_spec, out_shape=jax.ShapeDtypeStruct((H, C), f32),
                          compiler_params=_params(("parallel",)))(chip_idx, part, recv)


def _adamw_big(name, w, g_mine, g_theirs, m, v, c_idx, tr=HALF_ROWS_F32):
    R, C = w.shape
    H = R // 2
    tr = _row_tile(H, tr)
    nb = H // tr

    def body(c_ref, w_ref, gm_ref, gt_ref, m_ref, v_ref, g_ref, d_ref, mo_ref, vo_ref):
        g_ = jnp.where(pl.program_id(0) // nb == c_ref[0], gm_ref[...], gt_ref[...])
        g_ref[...] = g_
        m_ = ADAM_B1 * m_ref[...] + (1.0 - ADAM_B1) * g_
        v_ = ADAM_B2 * v_ref[...] + (1.0 - ADAM_B2) * jnp.square(g_)
        m_hat = m_ / (1.0 - ADAM_B1 ** ADAM_STEP)
        v_hat = v_ / (1.0 - ADAM_B2 ** ADAM_STEP)
        d_ref[...] = -ADAM_LR * (m_hat / (jnp.sqrt(v_hat) + ADAM_EPS) + ADAM_WD * w_ref[...])
        mo_ref[...] = m_
        vo_ref[...] = v_

    full = pl.BlockSpec((tr, C), lambda i, c: (i, 0))
    half = pl.BlockSpec((tr, C), lambda i, c: (i % nb, 0))
    grid_spec = pltpu.PrefetchScalarGridSpec(num_scalar_prefetch=1, grid=(2 * nb,),
                                             in_specs=[full, half, half, full, full], out_specs=[full] * 4)
    return pl.pallas_call(body, name=name, grid_spec=grid_spec, out_shape=[jax.ShapeDtypeStruct((R, C), f32)] * 4,
                          compiler_params=_params(("parallel",)))(c_idx, w, g_mine, g_theirs, m, v)


BIG = ("ffn1_w_gate", "ffn1_w_up", "ffn1_w_down", "w_in", "w_out", "ffn2_w_gate", "ffn2_w_up", "ffn2_w_down",
       "ple_w_gate", "ple_w_proj")


SMALL = ("ffn1_norm", "mix_norm", "gm_ln_g", "gm_ln_b", "gm_w_s", "gm_b_s", "gm_out_norm", "conv_b", "dt_bias", "a_log",
         "d_skip", "ssm_norm", "ffn2_norm", "ple_norm", "ple_b_gate", "final_norm")
SMALL_C = 1024


def _pack_small(vals):
    parts = []
    for v in vals:
        f = v.astype(f32).reshape(-1)
        parts.append(jnp.pad(f, (0, -f.shape[0] % SMALL_C)))
    flat = jnp.concatenate(parts)
    rows = flat.shape[0] // SMALL_C
    return jnp.pad(flat, (0, (-rows % 8) * SMALL_C)).reshape(-1, SMALL_C)


def _unpack_small(pack, shapes):
    flat = pack.reshape(-1)
    out, off = [], 0
    for s in shapes:
        n = 1
        for d in s:
            n *= d
        out.append(flat[off:off + n].reshape(s))
        off += n + (-n % SMALL_C)
    return out


def _pad_lanes(v):
    return jnp.pad(v, ((0, 0), (0, LANES - v.shape[1])))


def _pad_rows(a):
    pad = [(0, 0)] * a.ndim
    pad[-2] = (0, -a.shape[-2] % ROW_PAD)
    return jnp.pad(a, pad) if pad[-2][1] else a


FETCH = (("ffn1_w_gate", "ffn1_w_up", "ffn1_w_down"), ("w_in", "conv_w", "w_out"),
         ("ffn2_w_gate", "ffn2_w_up", "ffn2_w_down", "ple_w_gate", "ple_w_proj"))
TRANSPOSED = ("ffn1_w_gate", "ffn1_w_up", "ffn2_w_gate", "ffn2_w_up", "w_in")
ROW_PAD = 32
DONE = (("ffn2_w_gate", "ffn2_w_up", "ffn2_w_down", "w_out", "ple_w_gate", "ple_w_proj"), ("w_in",),
        ("ffn1_w_gate", "ffn1_w_up", "ffn1_w_down"))


def _local_step(x, p, tgt, fetch, S, on_grads, on_later):
    G = GM_WIDTH
    K = N_CHIPS
    b_st = S["gm_b_s"][0].T
    w_s = S["gm_w_s"][0]
    dtb, alog, dsk = _pad_lanes(S["dt_bias"]), _pad_lanes(S["a_log"]), _pad_lanes(S["d_skip"])
    gfin = S["final_norm"].reshape(1, -1)

    def rows(a):
        return a.reshape(-1, D_MODEL)

    def shards(a):
        return a.reshape(K, -1, D_MODEL)

    wg1, wu1, wd1 = [rows(a) for a in fetch(0, None)]
    h1, n1, a1, b1 = _ffn_fwd("ffn1_fwd", x, S["ffn1_norm"], wg1, wu1, wd1)
    w_in4, cw4, wo4 = fetch(1, h1)
    w_in = w_in4.reshape(IN_PROJ, D_MODEL)
    w_uv = w_in[:2 * G]
    w_zxd = jnp.pad(w_in[2 * G:], ((0, ZXD - (IN_PROJ - 2 * G)), (0, 0)))
    conv_w = jnp.transpose(cw4, (1, 0, 2)).reshape(SSM_CONV, CONV_DIM)
    wo = wo4.reshape(-1, D_MODEL)
    n2, act, slope, z, xbc, dtr, ya = _mix_fwd(h1, S["mix_norm"], w_uv, w_zxd, S["gm_ln_g"], S["gm_ln_b"], w_s, b_st,
                                               S["gm_out_norm"])
    yb, xc, sg, y_ssd, sprev = _ssd_fwd(xbc, z, dtr, conv_w, S["conv_b"], dtb, alog, dsk, S["ssm_norm"])
    wg2, wu2, wd2, wpg4, wpp4 = fetch(2, yb)
    wg2, wu2, wd2 = rows(wg2), rows(wu2), rows(wd2)
    h2, h3, n3, a2, b2 = _ffn_fwd("ffn2_fwd", h1, S["ffn2_norm"], wg2, wu2, wd2, pre=(ya, yb, wo))
    dh3, loss, dgp, dwpg, dbpg, dwpp, dgf = _tail(h3, p, tgt, S["ple_norm"], wpg4.reshape(-1, D_MODEL), S["ple_b_gate"], wpp4, gfin)
    dh2, da2, db2, hm2, dg_ffn2, dya, dyb = _ffn_bwd("ffn2_bwd", dh3, h2, S["ffn2_norm"], a2, b2, wg2, wu2, wd2, wo=wo, ga=G)
    dw_out = jnp.concatenate([_matmul_tn("dw_out_a", ya, dh2), _matmul_tn("dw_out_b", yb, dh2)], axis=0).reshape(wo4.shape)
    zero = on_grads(0, [shards(_matmul_tn("dw_ffn2_gate", da2, n3)), shards(_matmul_tn("dw_ffn2_up", db2, n3)),
                        shards(_matmul_tn("dw_ffn2_down", hm2, dh3, scale=0.5)), dw_out,
                        dwpg.astype(bf16).reshape(wpg4.shape), dwpp.astype(bf16)])
    dzxd, dcw, dcb, ddtb, dalog, ddsk, dgssm = _ssd_bwd(xbc, xc, sg, y_ssd, z, dtr, sprev, dyb, conv_w, S["conv_b"], dtb, alog, dsk,
                                                        S["ssm_norm"] + zero)
    zero = on_later(0, dgssm)
    dh1, duv, dg_mix, dlng, dlnb, dws, dbst, dgout = _mix_bwd(dh2, h1, S["mix_norm"] + zero, act, slope, dya, dzxd, w_uv, w_zxd,
                                                              S["gm_ln_g"], S["gm_ln_b"], w_s, b_st, S["gm_out_norm"])
    dw_in = jnp.concatenate([_matmul_tn("dw_in_uv", duv, n2), _matmul_tn("dw_in_zxd", dzxd, n2)[:IN_PROJ - 2 * G]], axis=0)
    zero = on_grads(1, [dw_in.reshape(w_in4.shape)])
    dx, da1, db1, hm1, dg_ffn1 = _ffn_bwd("ffn1_bwd", dh1, x, S["ffn1_norm"] + zero, a1, b1, wg1, wu1, wd1)
    loss = loss + on_later(1, dg_ffn1)
    zero = on_grads(2, [shards(_matmul_tn("dw_ffn1_gate", da1, n1)), shards(_matmul_tn("dw_ffn1_up", db1, n1)),
                        shards(_matmul_tn("dw_ffn1_down", hm1, dh1, scale=0.5))])
    loss = loss + zero
    nh = SSM_HEADS
    gS = {"ffn1_norm": dg_ffn1, "mix_norm": dg_mix, "gm_ln_g": dlng, "gm_ln_b": dlnb, "gm_w_s": dws[None], "gm_b_s": dbst.T[None],
          "gm_out_norm": dgout, "conv_b": dcb, "dt_bias": ddtb[:, :nh], "a_log": dalog[:, :nh], "d_skip": ddsk[:, :nh],
          "ssm_norm": dgssm, "ffn2_norm": dg_ffn2, "ple_norm": dgp, "ple_b_gate": dbpg, "final_norm": dgf.reshape(-1)}
    return loss, dx, dcw, gS


_WEIGHTS = ("ffn1_norm", "ffn1_w_gate", "ffn1_w_up", "ffn1_w_down", "mix_norm", "w_in", "gm_ln_g", "gm_ln_b", "gm_w_s", "gm_b_s",
            "gm_out_norm", "conv_w", "conv_b", "dt_bias", "a_log", "d_skip", "ssm_norm", "w_out", "ffn2_norm", "ffn2_w_gate",
            "ffn2_w_up", "ffn2_w_down", "ple_norm", "ple_w_gate", "ple_b_gate", "ple_w_proj", "final_norm")
_BIG_NAMES = BIG


def kernel(x, p, ffn1_norm, ffn1_w_gate, ffn1_w_up, ffn1_w_down, mix_norm, w_in, gm_ln_g, gm_ln_b, gm_w_s, gm_b_s, gm_out_norm, conv_w, conv_b, dt_bias, a_log, d_skip, ssm_norm, w_out, ffn2_norm, ffn2_w_gate, ffn2_w_up, ffn2_w_down, ple_norm, ple_w_gate, ple_b_gate, ple_w_proj, final_norm, loss_target, m_ffn1_norm, m_ffn1_w_gate, m_ffn1_w_up, m_ffn1_w_down, m_mix_norm, m_w_in, m_gm_ln_g, m_gm_ln_b, m_gm_w_s, m_gm_b_s, m_gm_out_norm, m_conv_w, m_conv_b, m_dt_bias, m_a_log, m_d_skip, m_ssm_norm, m_w_out, m_ffn2_norm, m_ffn2_w_gate, m_ffn2_w_up, m_ffn2_w_down, m_ple_norm, m_ple_w_gate, m_ple_b_gate, m_ple_w_proj, m_final_norm, v_ffn1_norm, v_ffn1_w_gate, v_ffn1_w_up, v_ffn1_w_down, v_mix_norm, v_w_in, v_gm_ln_g, v_gm_ln_b, v_gm_w_s, v_gm_b_s, v_gm_out_norm, v_conv_w, v_conv_b, v_dt_bias, v_a_log, v_d_skip, v_ssm_norm, v_w_out, v_ffn2_norm, v_ffn2_w_gate, v_ffn2_w_up, v_ffn2_w_down, v_ple_norm, v_ple_w_gate, v_ple_b_gate, v_ple_w_proj, v_final_norm):
    given = dict(locals())
    w = {n: given[n] for n in _WEIGHTS}
    m = {n: given["m_" + n] for n in _WEIGHTS}
    v = {n: given["v_" + n] for n in _WEIGHTS}

    c_idx = lax.axis_index("c").astype(jnp.int32).reshape(1)
    chip = 2 * lax.axis_index("x") + lax.axis_index("y")
    chip_idx = chip.astype(jnp.int32).reshape(1)

    shard = {n: (jnp.swapaxes(w[n][0], 0, 1) if n in TRANSPOSED else w[n][0]).astype(bf16) for n in BIG}
    shard["conv_w"] = w["conv_w"][0]
    first = _gather_weights([shard[n] for n in FETCH[0]], [True] * len(FETCH[0]))
    fetching, after = [], first[-1]
    for k in (1, 2):
        srcs = [shard[n] for n in FETCH[k]]
        lands = [jax.ShapeDtypeStruct((N_CHIPS,) + s.shape, s.dtype) for s in srcs]
        fetching.append(_copies_start("gather%d_start" % k, srcs, lands, 4 * len(srcs), _gather_copies, after))
        after = fetching[-1][4]

    def fetch(k, after_):
        return first if k == 0 else _copies_wait("gather%d_wait" % k, fetching[k - 1], _gather_copies, [after_])[1]

    swapping, exchanging = {}, {}

    def exchange(k, grads, others):
        parts = [_add_halves("add_" + n, g_, o_, c_idx) for n, g_, o_ in zip(DONE[k], grads, others)]
        lands = [jax.ShapeDtypeStruct((3,) + p_.shape[1:], p_.dtype) for p_ in parts]
        exchanging[k] = _copies_start("exchange%d_start" % k, parts, lands, 3 * len(parts), _partial_copies, c_idx)
        return exchanging[k][4][0, 0]

    def on_grads(k, grads):
        grads = [_pad_rows(g_) for g_ in grads]
        if k == len(DONE) - 1:
            return exchange(k, grads, _swap_halves("swap%d" % k, grads))
        lands = [jax.ShapeDtypeStruct((g_.shape[0], g_.shape[1] // 2, g_.shape[2]), g_.dtype) for g_ in grads]
        swapping[k] = _copies_start("swap%d_start" % k, grads, lands, len(grads), _swap_copies, c_idx)
        return swapping[k][4][0, 0]

    def on_later(k, after_):
        return exchange(k, *_copies_wait("swap%d_wait" % k, swapping[k], _swap_copies, [after_]))

    S = {n: w[n] for n in SMALL}
    S["ffn1_norm"] = S["ffn1_norm"] + after[0, 0]
    loss, dx, dcw, gS = _local_step(x[0], p[0, 0], loss_target[0], fetch, S, on_grads, on_later)

    small = _pack_small([gS[n] for n in SMALL] + [dcw, loss[:, :1]])
    small_lands = [jax.ShapeDtypeStruct((N_DEV - 1,) + small.shape, small.dtype)]
    small_st = _copies_start("small_start", [small], small_lands, N_DEV - 1, _small_copies, c_idx)

    g, delta, new_m, new_v = {}, {}, {}, {}

    def share_start(k, after_):
        parts, recv = _copies_wait("exchange%d_wait" % k, exchanging[k], _partial_copies, after_)
        mine = [_sum_partials("sum_" + n, p_, r_, chip_idx) for n, p_, r_ in zip(DONE[k], parts, recv)]
        return _copies_start("share%d_start" % k, mine, mine, len(mine), _share_copies, c_idx)

    def update(k, sharing, after_):
        mine, theirs = _copies_wait("share%d_wait" % k, sharing, _share_copies, after_)
        after = []
        for n, gm_, gt_ in zip(DONE[k], mine, theirs):
            flip = (lambda a: jnp.swapaxes(a, 0, 1)) if n in TRANSPOSED else (lambda a: a)
            rows = flip(w[n][0]).shape[0]
            if rows % ROW_PAD:
                def lin(a):
                    return jnp.transpose(a.reshape(-1, LANES, rows), (2, 0, 1))

                def back(a):
                    return jnp.transpose(a, (1, 2, 0)).reshape(1, -1, rows)

                gm_, gt_ = [a.reshape(a.shape[0], -1, LANES) for a in (gm_, gt_)]
                g_ = jnp.where(c_idx[0] == 0, jnp.concatenate([gm_, gt_]), jnp.concatenate([gt_, gm_]))[:rows]
                outs = [g_, *_adamw("adamw_" + n, lin(w[n]), g_, lin(m[n]), lin(v[n]), tr=HALF_ROWS_F32)]
                g[n], delta[n], new_m[n], new_v[n] = [back(o) for o in outs]
            else:
                outs = _adamw_big("adamw_" + n, flip(w[n][0]), gm_, gt_, flip(m[n][0]), flip(v[n][0]), c_idx)
                g[n], delta[n], new_m[n], new_v[n] = [flip(o)[None] for o in outs]
            after.append(outs[3])
        return after

    sharing = [share_start(0, [small_st[4]])]
    after = [sharing[0][4]]
    for k in range(len(DONE)):
        if k + 1 < len(DONE):
            sharing.append(share_start(k + 1, after))
            after = [sharing[k + 1][4]]
        after = update(k, sharing[k], after)
    (own,), (slots,) = _copies_wait("small_wait", small_st, _small_copies, after)
    dev_idx = (2 * chip + lax.axis_index("c")).astype(jnp.int32).reshape(1)
    small_shapes = [w[n].shape for n in SMALL] + [dcw.shape, (1, 1)]
    small_sum = _unpack_small(_sum_small(own, slots, dev_idx), small_shapes)
    g.update({n: small_sum[i] for i, n in enumerate(SMALL)})
    cshard = w["conv_w"].shape[2]
    g["conv_w"] = lax.dynamic_slice_in_dim(small_sum[len(SMALL)], chip * cshard, cshard, axis=1)[None]
    loss_total = small_sum[len(SMALL) + 1].reshape(())
    sm_names = SMALL + ("conv_w",)
    sm_shapes = [w[n].shape for n in sm_names]
    d_s, m_s, v_s = _adamw("adamw_small", _pack_small([w[n] for n in sm_names]), _pack_small([g[n] for n in sm_names]),
                           _pack_small([m[n] for n in sm_names]), _pack_small([v[n] for n in sm_names]))
    for dst, src in ((delta, d_s), (new_m, m_s), (new_v, v_s)):
        for n, val in zip(sm_names, _unpack_small(src, sm_shapes)):
            dst[n] = val

    return (loss_total, dx[None], *[g[n] for n in _WEIGHTS], *[delta[n] for n in _WEIGHTS],
            *[new_m[n] for n in _WEIGHTS], *[new_v[n] for n in _WEIGHTS])
```

```python
import jax
import jax.numpy as jnp
from jax import lax
from jax.experimental import pallas as pl
from jax.experimental.pallas import tpu as pltpu

f32 = jnp.float32
bf16 = jnp.bfloat16
MESH = pl.DeviceIdType.MESH
HIGHEST = lax.Precision.HIGHEST

EPS = 1e-6
N_CHIPS = 4
N_DEV = 8
D_MODEL = 1024
GM_WIDTH = 1024
GM_HEADS = 8
CHUNK = 128
SSM_WIDTH = 1024
SSM_HEADS = 16
SSM_HEAD_DIM = 64
SSM_GROUPS = 2
SSM_STATE = 128
SSM_CONV = 4
CONV_DIM = SSM_WIDTH + 2 * SSM_GROUPS * SSM_STATE
IN_PROJ = 2 * GM_WIDTH + SSM_WIDTH + CONV_DIM + SSM_HEADS
LANES = 128
ZXD = SSM_WIDTH + CONV_DIM + LANES

ADAM_LR = 0.001
ADAM_B1 = 0.9
ADAM_B2 = 0.999
ADAM_EPS = 1e-08
ADAM_WD = 0.01
ADAM_STEP = 10

VMEM_LIMIT = 56 * 1024 * 1024
HALF_ROWS_BF16 = 592
HALF_ROWS_F32 = 320


def _dot(a, b):
    return jnp.dot(a, b, preferred_element_type=f32)


def _dot_nt(a, b):
    return lax.dot_general(a, b, (((1,), (1,)), ((), ())), preferred_element_type=f32)


def _dot_tn(a, b):
    return lax.dot_general(a, b, (((0,), (0,)), ((), ())), preferred_element_type=f32)


def _rms(x, g):
    return x * lax.rsqrt(jnp.mean(x * x, axis=-1, keepdims=True) + EPS) * g


def _layernorm(x, g, b):
    mu = jnp.mean(x, axis=-1, keepdims=True)
    xc = x - mu
    return xc * lax.rsqrt(jnp.mean(xc * xc, axis=-1, keepdims=True) + EPS) * g + b


def _sigmoid(x):
    return 1.0 / (1.0 + jnp.exp(-x))


def _softplus(x):
    return jnp.maximum(x, 0.0) + jnp.log(1.0 + jnp.exp(-jnp.abs(x)))


def _full(shape):
    nd = len(shape)
    return pl.BlockSpec(shape, lambda *_: (0,) * nd, pipeline_mode=pl.Buffered(1))


def _acc(shape):
    nd = len(shape)
    return pl.BlockSpec(shape, lambda *_: (0,) * nd)


def _rows(tm, ncols):
    return pl.BlockSpec((tm, ncols), lambda i: (i, 0))


def _params(sem):
    return pltpu.CompilerParams(dimension_semantics=sem, vmem_limit_bytes=VMEM_LIMIT)


def _row_tile(rows, target, mult=8):
    best = rows
    for t in range(mult, min(rows, target) + 1, mult):
        if rows % t == 0:
            best = t
    return best if best <= target else rows


def _ffn_fwd(name, h, g, wg, wu, wd, pre=None, tm=256):
    T, D = h.shape
    F = wg.shape[0]
    tm = min(tm, T)

    def body(*refs):
        if pre is None:
            h_ref, g_ref, wg_ref, wu_ref, wd_ref, ho_ref, n_ref, a_ref, b_ref = refs
            hin = h_ref[...]
        else:
            (h_ref, ya_ref, yb_ref, wo_ref, g_ref, wg_ref, wu_ref, wd_ref,
             hi_ref, ho_ref, n_ref, a_ref, b_ref) = refs
            ga = ya_ref.shape[1]
            hin = h_ref[...] + _dot(ya_ref[...], wo_ref[:ga, :]) + _dot(yb_ref[...], wo_ref[ga:, :])
            hi_ref[...] = hin
        n = _rms(hin, g_ref[...]).astype(bf16)
        n_ref[...] = n
        a = _dot_nt(n, wg_ref[...]).astype(bf16)
        b = _dot_nt(n, wu_ref[...]).astype(bf16)
        a_ref[...] = a
        b_ref[...] = b
        af = a.astype(f32)
        hm = (af * _sigmoid(af) * b.astype(f32)).astype(bf16)
        ho_ref[...] = hin + 0.5 * _dot(hm, wd_ref[...])

    ins = [h] + (list(pre) if pre is not None else []) + [g, wg, wu, wd]
    in_specs = [_rows(tm, D)]
    if pre is not None:
        in_specs += [_rows(tm, pre[0].shape[1]), _rows(tm, pre[1].shape[1]), _full(pre[2].shape)]
    in_specs += [_full(g.shape), _full(wg.shape), _full(wu.shape), _full(wd.shape)]
    outs = [jax.ShapeDtypeStruct((T, D), f32), jax.ShapeDtypeStruct((T, D), bf16),
            jax.ShapeDtypeStruct((T, F), bf16), jax.ShapeDtypeStruct((T, F), bf16)]
    out_specs = [_rows(tm, D), _rows(tm, D), _rows(tm, F), _rows(tm, F)]
    if pre is not None:
        outs = [jax.ShapeDtypeStruct((T, D), f32)] + outs
        out_specs = [_rows(tm, D)] + out_specs
    return pl.pallas_call(body, name=name, grid=(T // tm,), in_specs=in_specs, out_specs=out_specs,
                          out_shape=outs, compiler_params=_params(("parallel",)))(*ins)


def _ffn_bwd(name, dh, hin, g, a, b, wg, wu, wd, wo=None, ga=0, tm=256):
    T, D = dh.shape
    F = wg.shape[0]
    tm = min(tm, T)

    def body(*refs):
        if wo is None:
            (dh_ref, hin_ref, g_ref, a_ref, b_ref, wg_ref, wu_ref, wd_ref,
             dhi_ref, da_ref, db_ref, hm_ref, dg_ref) = refs
        else:
            (dh_ref, hin_ref, g_ref, a_ref, b_ref, wg_ref, wu_ref, wd_ref, wo_ref,
             dhi_ref, da_ref, db_ref, hm_ref, dg_ref, dya_ref, dyb_ref) = refs

        @pl.when(pl.program_id(0) == 0)
        def _():
            dg_ref[...] = jnp.zeros_like(dg_ref)

        dh_ = dh_ref[...]
        dhb = (0.5 * dh_).astype(bf16)
        dhm = _dot_nt(dhb, wd_ref[...])
        af = a_ref[...].astype(f32)
        bf = b_ref[...].astype(f32)
        sg = _sigmoid(af)
        sl_ = af * sg
        da = (dhm * bf * (sg * (1.0 + af * (1.0 - sg)))).astype(bf16)
        db = (dhm * sl_).astype(bf16)
        da_ref[...] = da
        db_ref[...] = db
        hm_ref[...] = (sl_ * bf).astype(bf16)
        dn = _dot(da, wg_ref[...]) + _dot(db, wu_ref[...])
        _, vjp = jax.vjp(_rms, hin_ref[...], g_ref[...])
        dx, dg = vjp(dn)
        dhi = dh_ + dx
        dhi_ref[...] = dhi
        dg_ref[...] += dg
        if wo is not None:
            dhib = dhi.astype(bf16)
            dya_ref[...] = _dot_nt(dhib, wo_ref[:ga, :]).astype(bf16)
            dyb_ref[...] = _dot_nt(dhib, wo_ref[ga:, :]).astype(bf16)

    ins = [dh, hin, g, a, b, wg, wu, wd]
    in_specs = [_rows(tm, D), _rows(tm, D), _full(g.shape), _rows(tm, F), _rows(tm, F),
                _full(wg.shape), _full(wu.shape), _full(wd.shape)]
    act = jax.ShapeDtypeStruct((T, F), bf16)
    outs = [jax.ShapeDtypeStruct((T, D), f32), act, act, act, jax.ShapeDtypeStruct(g.shape, f32)]
    out_specs = [_rows(tm, D), _rows(tm, F), _rows(tm, F), _rows(tm, F), _acc(g.shape)]
    if wo is not None:
        gb = wo.shape[0] - ga
        ins += [wo]
        in_specs += [_full(wo.shape)]
        outs += [jax.ShapeDtypeStruct((T, ga), bf16), jax.ShapeDtypeStruct((T, gb), bf16)]
        out_specs += [_rows(tm, ga), _rows(tm, gb)]
    return pl.pallas_call(body, name=name, grid=(T // tm,), in_specs=in_specs, out_specs=out_specs,
                          out_shape=outs, compiler_params=_params(("arbitrary",)))(*ins)


def _matmul_tn(name, a, b, scale=1.0, tk=2048):
    T, M = a.shape
    N = b.shape[1]
    tk = min(tk, T)
    nk = T // tk
    tn = LANES * max(d for d in range(1, N // LANES + 1) if (N // LANES) % d == 0 and (d == 1 or M * d * LANES * 4 <= 6 * 1024 * 1024))

    def body(a_ref, b_ref, o_ref, acc):
        k = pl.program_id(1)

        @pl.when(k == 0)
        def _():
            acc[...] = jnp.zeros_like(acc)

        bb = b_ref[...]
        if scale != 1.0:
            bb = bb * scale
        acc[...] += _dot_tn(a_ref[...].astype(bf16), bb.astype(bf16))

        @pl.when(k == nk - 1)
        def _():
            o_ref[...] = acc[...].astype(bf16)

    return pl.pallas_call(
        body, name=name, grid=(N // tn, nk),
        in_specs=[pl.BlockSpec((tk, M), lambda j, k: (k, 0)), pl.BlockSpec((tk, tn), lambda j, k: (k, j))],
        out_specs=pl.BlockSpec((M, tn), lambda j, k: (0, j)),
        out_shape=jax.ShapeDtypeStruct((M, N), bf16), scratch_shapes=[pltpu.VMEM((M, tn), f32)],
        compiler_params=_params(("parallel", "arbitrary")))(a, b)


def _gelu_and_slope(x):
    cdf = 0.5 * (1.0 + lax.erf(x * 0.7071067811865476))
    return x * cdf, cdf + x * (0.3989422804014327 * jnp.exp(-0.5 * x * x))


def _tril_mask():
    r = lax.broadcasted_iota(jnp.int32, (CHUNK, CHUNK), 0)
    c = lax.broadcasted_iota(jnp.int32, (CHUNK, CHUNK), 1)
    return c <= r


def _gm_mix(vnb, ws_ref, bst, mixed_sc, tm):
    mask = _tril_mask()
    for h in range(GM_HEADS):
        wt = jnp.where(mask, ws_ref[h], 0.0).astype(bf16)
        bias = bst[:, h:h + 1]
        for q in range(tm // CHUNK):
            rs = slice(q * CHUNK, (q + 1) * CHUNK)
            cs = slice(h * CHUNK, (h + 1) * CHUNK)
            mixed_sc[rs, cs] = _dot(wt, vnb[rs, cs]) + bias


def _mix_fwd(h1, gmix, w_uv, w_zxd, ln_g, ln_b, w_s, b_st, gout, tm=512):
    T, D = h1.shape
    tm = min(tm, T)
    G = GM_WIDTH

    def body(h_ref, g_ref, wuv_ref, wzxd_ref, lng_ref, lnb_ref, ws_ref, bst_ref, gout_ref,
             n_ref, act_ref, slope_ref, z_ref, xbc_ref, dt_ref, ya_ref, mixed_sc):
        n = _rms(h_ref[...], g_ref[...]).astype(bf16)
        n_ref[...] = n
        uv = _dot_nt(n, wuv_ref[...]).astype(bf16)
        zxd = _dot_nt(n, wzxd_ref[...])
        z_ref[...] = zxd[:, :SSM_WIDTH].astype(bf16)
        xbc_ref[...] = zxd[:, SSM_WIDTH:SSM_WIDTH + CONV_DIM].astype(bf16)
        dt_ref[...] = zxd[:, SSM_WIDTH + CONV_DIM:]
        act, slope = _gelu_and_slope(uv.astype(f32))
        act = act.astype(bf16)
        act_ref[...] = act
        slope_ref[...] = slope.astype(bf16)
        ug, vg = act[:, :G].astype(f32), act[:, G:].astype(f32)
        _gm_mix(_layernorm(vg, lng_ref[...], lnb_ref[...]).astype(bf16), ws_ref, bst_ref[...], mixed_sc, tm)
        ya_ref[...] = _rms(ug * mixed_sc[...], gout_ref[...]).astype(bf16)

    ins = [h1, gmix, w_uv, w_zxd, ln_g, ln_b, w_s, b_st, gout]
    in_specs = [_rows(tm, D)] + [_full(x.shape) for x in ins[1:]]
    outs = [jax.ShapeDtypeStruct((T, D), bf16), jax.ShapeDtypeStruct((T, 2 * G), bf16), jax.ShapeDtypeStruct((T, 2 * G), bf16),
            jax.ShapeDtypeStruct((T, SSM_WIDTH), bf16), jax.ShapeDtypeStruct((T, CONV_DIM), bf16),
            jax.ShapeDtypeStruct((T, LANES), f32), jax.ShapeDtypeStruct((T, G), bf16)]
    out_specs = [_rows(tm, D), _rows(tm, 2 * G), _rows(tm, 2 * G), _rows(tm, SSM_WIDTH), _rows(tm, CONV_DIM), _rows(tm, LANES),
                 _rows(tm, G)]
    return pl.pallas_call(body, name="mix_fwd", grid=(T // tm,), in_specs=in_specs, out_specs=out_specs,
                          out_shape=outs, scratch_shapes=[pltpu.VMEM((tm, G), f32)],
                          compiler_params=_params(("parallel",)))(*ins)


def _mix_bwd(dh, h1, gmix, act, slope, dya, dzxd, w_uv, w_zxd, ln_g, ln_b, w_s, b_st, gout, tm=256):
    T, D = dh.shape
    tm = min(tm, T)
    G = GM_WIDTH

    def body(dh_ref, h_ref, g_ref, act_ref, slope_ref, dya_ref, dzxd_ref, wuv_ref, wzxd_ref, lng_ref, lnb_ref, ws_ref,
             bst_ref, gout_ref, dhi_ref, duv_ref, dg_ref, dlng_ref, dlnb_ref, dws_ref, dbst_ref, dgout_ref, mixed_sc, dvn_sc):
        @pl.when(pl.program_id(0) == 0)
        def _():
            for r in (dg_ref, dlng_ref, dlnb_ref, dws_ref, dbst_ref, dgout_ref):
                r[...] = jnp.zeros_like(r)

        dn_z = _dot(dzxd_ref[...], wzxd_ref[...])
        ug = act_ref[:, :G].astype(f32)
        vn, ln_vjp = jax.vjp(_layernorm, act_ref[:, G:].astype(f32), lng_ref[...], lnb_ref[...])
        vnb = vn.astype(bf16)
        _gm_mix(vnb, ws_ref, bst_ref[...], mixed_sc, tm)
        mixed = mixed_sc[...]
        _, out_vjp = jax.vjp(_rms, ug * mixed, gout_ref[...])
        dpre, dgout = out_vjp(dya_ref[...].astype(f32))
        dgout_ref[...] += dgout
        dug = dpre * mixed
        dmixed = dpre * ug
        mask = _tril_mask()
        lane = lax.broadcasted_iota(jnp.int32, (1, GM_HEADS), 1)
        dbst = jnp.zeros((CHUNK, GM_HEADS), f32)
        for h in range(GM_HEADS):
            wt = jnp.where(mask, ws_ref[h], 0.0).astype(bf16)
            cs = slice(h * CHUNK, (h + 1) * CHUNK)
            dw = jnp.zeros((CHUNK, CHUNK), f32)
            for q in range(tm // CHUNK):
                rs = slice(q * CHUNK, (q + 1) * CHUNK)
                dm = dmixed[rs, cs]
                dmb = dm.astype(bf16)
                dw = dw + _dot_nt(dmb, vnb[rs, cs])
                dbst = dbst + jnp.sum(dm, axis=1, keepdims=True) * (lane == h).astype(f32)
                dvn_sc[rs, cs] = _dot_tn(wt, dmb)
            dws_ref[h] += jnp.where(mask, dw, 0.0)
        dbst_ref[...] += dbst
        dvg, dlng, dlnb = ln_vjp(dvn_sc[...])
        duv = (jnp.concatenate([dug, dvg], axis=1) * slope_ref[...].astype(f32)).astype(bf16)
        duv_ref[...] = duv
        dlng_ref[...] += dlng
        dlnb_ref[...] += dlnb
        dn = dn_z + _dot(duv, wuv_ref[...])
        _, vjp = jax.vjp(_rms, h_ref[...], g_ref[...])
        dx, dg = vjp(dn)
        dhi_ref[...] = dh_ref[...] + dx
        dg_ref[...] += dg

    ins = [dh, h1, gmix, act, slope, dya, dzxd, w_uv, w_zxd, ln_g, ln_b, w_s, b_st, gout]
    in_specs = ([_rows(tm, D), _rows(tm, D), _full(gmix.shape), _rows(tm, 2 * G), _rows(tm, 2 * G), _rows(tm, G),
                 _rows(tm, dzxd.shape[1])] + [_full(x.shape) for x in ins[7:]])
    accs = (gmix, ln_g, ln_b, w_s, b_st, gout)
    outs = ([jax.ShapeDtypeStruct((T, D), f32), jax.ShapeDtypeStruct((T, 2 * G), bf16)]
            + [jax.ShapeDtypeStruct(x.shape, f32) for x in accs])
    out_specs = [_rows(tm, D), _rows(tm, 2 * G)] + [_acc(x.shape) for x in accs]
    return pl.pallas_call(body, name="mix_bwd", grid=(T // tm,), in_specs=in_specs, out_specs=out_specs,
                          out_shape=outs, scratch_shapes=[pltpu.VMEM((tm, G), f32), pltpu.VMEM((tm, G), f32)],
                          compiler_params=_params(("arbitrary",)))(*ins)


HALO = 16
SSD_SUB = 4


class _RowsOf:
    def __init__(self, ref, rows):
        self.ref, self.rows = ref, rows

    def _index(self, idx):
        return (self.rows, slice(None)) if idx is Ellipsis else (self.rows,) + tuple(idx[1:])

    def __getitem__(self, idx):
        return self.ref[self._index(idx)]

    def __setitem__(self, idx, value):
        self.ref[self._index(idx)] = value
PAIRS = SSM_HEADS // 2
PAIR_W = 2 * SSM_HEAD_DIM


def _split(x, n):
    parts = []
    for _ in range(n):
        p = x.astype(bf16)
        parts.append(p)
        x = x - p.astype(f32)
    return parts


def _dot_sel(x, sel_n, n):
    return _dot(jnp.concatenate(_split(x, n), axis=1), sel_n)


def _sel_dot(sel, x, n):
    return _dot(jnp.concatenate([sel] * n, axis=1), jnp.concatenate(_split(x, n), axis=0))


EXPAND_SPLIT = 3
REDUCE_SPLIT = 2


def _head_mats():
    ex = (jnp.arange(SSM_WIDTH)[None, :] // SSM_HEAD_DIM == jnp.arange(LANES)[:, None]).astype(bf16)
    return jnp.tile(ex, (EXPAND_SPLIT, 1)), jnp.tile(ex.T, (REDUCE_SPLIT, 1))


def _shift_mat(rows, cols, off):
    r = lax.broadcasted_iota(jnp.int32, (rows, cols), 0)
    c = lax.broadcasted_iota(jnp.int32, (rows, cols), 1)
    return (c == r + off).astype(bf16)


def _ssd_conv(halo, x, cw_ref, cb_ref):
    ext = jnp.concatenate([halo, x], axis=0)
    xc = cb_ref[...] + cw_ref[SSM_CONV - 1:SSM_CONV, :] * x.astype(f32)
    for j in range(SSM_CONV - 1):
        xc = xc + cw_ref[j:j + 1, :] * _dot(_shift_mat(CHUNK, HALO + CHUNK, HALO - SSM_CONV + 1 + j), ext)
    return xc


def _ssd_front(dtr, dtb_ref, alog_ref):
    dt = _softplus(dtr + dtb_ref[...])
    a = -jnp.exp(alog_ref[...])
    acs = jnp.dot(_tril_mask().astype(f32), dt * a, preferred_element_type=f32, precision=HIGHEST)
    return dt, a, acs


def _ssd_wide(xa, dt, acs, dsk, ex):
    dt_x = _dot_sel(dt, ex, EXPAND_SPLIT)
    acs_x = _dot_sel(acs, ex, EXPAND_SPLIT)
    dsk_x = _dot_sel(jnp.broadcast_to(dsk, (8, LANES)), ex, EXPAND_SPLIT)[0:1]
    e_x = jnp.exp(acs_x)
    r_x = jnp.exp(acs_x[CHUNK - 1:CHUNK, :] - acs_x)
    xs = xa[:, :SSM_WIDTH]
    xd = xs * dt_x
    return dt_x, dsk_x, e_x, r_x, xs, xd, xd * r_x


def _pair_stack(v, lo):
    return jnp.concatenate([jnp.where(lo, v, 0.0), jnp.where(lo, 0.0, v)], axis=0)


def _ssd_pair(j, acs, acs_t, cb):
    out = []
    tril = _tril_mask()
    for h in (2 * j, 2 * j + 1):
        dk = jnp.exp(jnp.where(tril, acs[:, h:h + 1] - acs_t[h:h + 1, :], -jnp.inf))
        out.append((dk, cb * dk))
    return out


def _pair_col(row_lo, tot, j):
    return jnp.exp(jnp.where(row_lo, tot[:, 2 * j:2 * j + 1], tot[:, 2 * j + 1:2 * j + 2]))


def _gated_norm(y, z, g):
    yg = y * (z * _sigmoid(z))
    half = SSM_WIDTH // SSM_GROUPS
    parts = []
    for k in range(SSM_GROUPS):
        s = yg[:, k * half:(k + 1) * half]
        parts.append(s * lax.rsqrt(jnp.mean(s * s, axis=-1, keepdims=True) + EPS))
    return jnp.concatenate(parts, axis=1) * g


def _group_mats(xa):
    out = []
    for g in range(SSM_GROUPS):
        bm = xa[:, SSM_WIDTH + g * SSM_STATE:SSM_WIDTH + (g + 1) * SSM_STATE].astype(bf16)
        cm = xa[:, SSM_WIDTH + (SSM_GROUPS + g) * SSM_STATE:SSM_WIDTH + (SSM_GROUPS + g + 1) * SSM_STATE].astype(bf16)
        out.append((cm, bm, _dot_nt(cm, bm)))
    return out


def _ssd_fwd(xbc, z, dtr, conv_w, conv_b, dt_bias, a_log, d_skip, ssm_norm):
    T = xbc.shape[0]
    nc = T // CHUNK
    N = SSM_STATE

    def body(xbc_ref, halo_ref, z_ref, dtr_ref, cw_ref, cb_ref, dtb_ref, alog_ref, dsk_ref, g_ref, ex_ref,
             yb_ref, xc_ref, sg_ref, y_ref, sprev_ref, s_sc):
        i = pl.program_id(0)

        @pl.when(i == 0)
        def _():
            s_sc[...] = jnp.zeros_like(s_sc)

        lo = lax.broadcasted_iota(jnp.int32, (CHUNK, PAIR_W), 1) < SSM_HEAD_DIM
        row_lo = lax.broadcasted_iota(jnp.int32, (PAIR_W, 1), 0) < SSM_HEAD_DIM
        for k in range(SUB):
            rs = slice(k * CHUNK, (k + 1) * CHUNK)
            if k == 0:
                halo = halo_ref[...]
                halo = jnp.where(i > 0, halo, jnp.zeros_like(halo))
            else:
                halo = xbc_ref[k * CHUNK - HALO:k * CHUNK, :]
            xc = _ssd_conv(halo, xbc_ref[rs, :], cw_ref, cb_ref)
            sg = _sigmoid(xc)
            xc_ref[rs, :] = xc
            sg_ref[rs, :] = sg
            xa = xc * sg
            dt, _, acs = _ssd_front(dtr_ref[rs, :], dtb_ref, alog_ref)
            _, dsk_x, e_x, _, xs, xd, gm = _ssd_wide(xa, dt, acs, dsk_ref[...], ex_ref[...])
            acs_t = acs.T
            tot = acs[CHUNK - 1:CHUNK, :]
            groups = _group_mats(xa)
            ys = []
            for j in range(PAIRS):
                cmb, bmb, cb = groups[j // (PAIRS // SSM_GROUPS)]
                ps = slice(j * PAIR_W, (j + 1) * PAIR_W)
                (_, m0), (_, m1) = _ssd_pair(j, acs, acs_t, cb)
                sp = s_sc[j]
                yd = _dot(jnp.concatenate([m0, m1], axis=1).astype(bf16), _pair_stack(xd[:, ps], lo).astype(bf16))
                ys.append(yd + e_x[:, ps] * _dot_nt(cmb, sp.astype(bf16)))
                sprev_ref[k, j] = sp
                s_sc[j] = _pair_col(row_lo, tot, j) * sp + _dot_tn(gm[:, ps].astype(bf16), bmb)
            y = jnp.concatenate(ys, axis=1) + xs * dsk_x
            y_ref[rs, :] = y
            yb_ref[rs, :] = _gated_norm(y, z_ref[rs, :].astype(f32), g_ref[...]).astype(bf16)

    params = [conv_w, conv_b, dt_bias, a_log, d_skip, ssm_norm, _head_mats()[0]]
    SUB = SSD_SUB if nc % SSD_SUB == 0 else 1
    hp = SUB * CHUNK // HALO
    R = SUB * CHUNK
    in_specs = [_rows(R, CONV_DIM), pl.BlockSpec((HALO, CONV_DIM), lambda i: (jnp.maximum(i * hp - 1, 0), 0)),
                _rows(R, SSM_WIDTH), _rows(R, LANES)] + [_full(x.shape) for x in params]
    return pl.pallas_call(
        body, name="ssd_fwd", grid=(nc // SUB,), in_specs=in_specs,
        out_specs=[_rows(R, SSM_WIDTH), _rows(R, CONV_DIM), _rows(R, CONV_DIM), _rows(R, SSM_WIDTH),
                   pl.BlockSpec((SUB, PAIRS, PAIR_W, N), lambda i: (i, 0, 0, 0))],
        out_shape=[jax.ShapeDtypeStruct((T, SSM_WIDTH), bf16), jax.ShapeDtypeStruct((T, CONV_DIM), f32),
                   jax.ShapeDtypeStruct((T, CONV_DIM), f32), jax.ShapeDtypeStruct((T, SSM_WIDTH), f32),
                   jax.ShapeDtypeStruct((nc, PAIRS, PAIR_W, N), f32)],
        scratch_shapes=[pltpu.VMEM((PAIRS, PAIR_W, N), f32)],
        compiler_params=_params(("arbitrary",)))(xbc, xbc, z, dtr, *params)


def _ssd_bwd(xbc, xc, sg, y, z, dtr, sprev, dyb, conv_w, conv_b, dt_bias, a_log, d_skip, ssm_norm):
    T = xbc.shape[0]
    nc = T // CHUNK
    H, N = SSM_HEADS, SSM_STATE
    PG = PAIRS // SSM_GROUPS

    def chunk(xbc_ref, xc_ref, sg_ref, y_ref, z_ref, dtr_ref, sprev_k, dyb_ref, cw_ref, cb_ref, dtb_ref, alog_ref, dsk_ref,
              g_ref, ex_ref, rd_ref, dzxd_ref, dcw_ref, dcb_ref, ddtb_ref, dalog_ref, ddsk_ref, dg_ref, ds_sc, next_sc):
        xc = xc_ref[...]
        sg = sg_ref[...]
        xa = xc * sg
        dt, a, acs = _ssd_front(dtr_ref[...], dtb_ref, alog_ref)
        dt_x, dsk_x, e_x, r_x, xs, xd, gm = _ssd_wide(xa, dt, acs, dsk_ref[...], ex_ref[...])
        acs_t = acs.T
        tot = acs[CHUNK - 1:CHUNK, :]
        groups = _group_mats(xa)
        lo = lax.broadcasted_iota(jnp.int32, (CHUNK, PAIR_W), 1) < SSM_HEAD_DIM
        row_lo = lax.broadcasted_iota(jnp.int32, (PAIR_W, 1), 0) < SSM_HEAD_DIM
        pairs, zs = [], []
        for j in range(PAIRS):
            cmb, _, cb = groups[j // PG]
            pairs.append(_ssd_pair(j, acs, acs_t, cb))
            zs.append(_dot_nt(cmb, sprev_k[j].astype(bf16)))
        zf = jnp.concatenate(zs, axis=1)
        _, gn_vjp = jax.vjp(_gated_norm, y_ref[...], z_ref[...].astype(f32), g_ref[...])
        dy, dz, dg = gn_vjp(dyb_ref[...].astype(f32))
        dg_ref[...] += dg
        dzxd_ref[:, :SSM_WIDTH] = dz.astype(bf16)

        lane = lax.broadcasted_iota(jnp.int32, (1, LANES), 1)
        sub = lax.broadcasted_iota(jnp.int32, (LANES, 1), 0)
        dacs = jnp.zeros((CHUNK, LANES), f32)
        dacs_r = jnp.zeros((LANES, CHUNK), f32)
        dtot = jnp.zeros((1, LANES), f32)
        dcb = [jnp.zeros((CHUNK, CHUNK), f32) for _ in range(SSM_GROUPS)]
        dcm = [jnp.zeros((CHUNK, N), f32) for _ in range(SSM_GROUPS)]
        dbm = [jnp.zeros((CHUNK, N), f32) for _ in range(SSM_GROUPS)]
        dxds, dgms = [], []
        for j in range(PAIRS):
            g = j // PG
            cmb, bmb, _ = groups[g]
            ps = slice(j * PAIR_W, (j + 1) * PAIR_W)
            (dk0, m0), (dk1, m1) = pairs[j]
            oh0, oh1 = (lane == 2 * j).astype(f32), (lane == 2 * j + 1).astype(f32)
            dyp = dy[:, ps]
            dy2 = _pair_stack(dyp, lo).astype(bf16)
            dm2 = _dot_nt(dy2, xd[:, ps].astype(bf16))
            m2 = jnp.concatenate([m0, m1], axis=0)
            dxds.append(_dot_tn(m2.astype(bf16), dy2))
            w2 = dm2 * m2
            rs = jnp.sum(w2, axis=1, keepdims=True)
            dacs = dacs + rs[:CHUNK] * oh0 + rs[CHUNK:] * oh1
            dacs_r = dacs_r - ((sub == 2 * j).astype(f32) * jnp.sum(w2[:CHUNK], axis=0, keepdims=True)
                               + (sub == 2 * j + 1).astype(f32) * jnp.sum(w2[CHUNK:], axis=0, keepdims=True))
            dcb[g] = dcb[g] + dm2[:CHUNK] * dk0 + dm2[CHUNK:] * dk1
            sp = sprev_k[j]
            dzb = (dyp * e_x[:, ps]).astype(bf16)
            dcm[g] = dcm[g] + _dot(dzb, sp.astype(bf16))
            dsn = ds_sc[j]
            dsnb = dsn.astype(bf16)
            et = _pair_col(row_lo, tot, j)
            rr = jnp.sum(dsn * sp, axis=1, keepdims=True) * et
            dtot = dtot + jnp.sum(rr[:SSM_HEAD_DIM]) * oh0 + jnp.sum(rr[SSM_HEAD_DIM:]) * oh1
            dgms.append(_dot_nt(bmb, dsnb))
            dbm[g] = dbm[g] + _dot(gm[:, ps].astype(bf16), dsnb)
            ds_sc[j] = _dot_tn(dzb, cmb) + et * dsn
        dgm = jnp.concatenate(dgms, axis=1)
        dxd = jnp.concatenate(dxds, axis=1) + dgm * r_x
        dr = dgm * gm
        red = _dot_sel(jnp.concatenate([dy * e_x * zf - dr, dr, dxd * xs, dy * xs], axis=0), rd_ref[...], REDUCE_SPLIT)
        rowi = lax.broadcasted_iota(jnp.int32, (CHUNK, 1), 0)
        dtot = dtot + jnp.sum(red[CHUNK:2 * CHUNK], axis=0, keepdims=True)
        dacs = dacs + red[:CHUNK] + dacs_r.T + jnp.where(rowi == CHUNK - 1, dtot, 0.0)
        r2 = lax.broadcasted_iota(jnp.int32, (CHUNK, CHUNK), 0)
        c2 = lax.broadcasted_iota(jnp.int32, (CHUNK, CHUNK), 1)
        dadt = jnp.dot((c2 >= r2).astype(f32), dacs, preferred_element_type=f32, precision=HIGHEST)
        ddt = red[2 * CHUNK:3 * CHUNK] + dadt * a
        dalog_ref[...] += jnp.sum(dadt * dt, axis=0, keepdims=True) * a
        ddsk_ref[...] += jnp.sum(red[3 * CHUNK:], axis=0, keepdims=True)
        ddtr = jnp.where(lane < H, ddt * _sigmoid(dtr_ref[...] + dtb_ref[...]), 0.0)
        ddtb_ref[...] += jnp.sum(ddtr, axis=0, keepdims=True)
        dzxd_ref[:, SSM_WIDTH + CONV_DIM:] = ddtr.astype(bf16)
        dxa_bm, dxa_cm = [], []
        for g in range(SSM_GROUPS):
            cmb, bmb, _ = groups[g]
            dcbb = dcb[g].astype(bf16)
            dxa_bm.append(dbm[g] + _dot_tn(dcbb, cmb))
            dxa_cm.append(dcm[g] + _dot(dcbb, bmb))
        dxc = jnp.concatenate([dy * dsk_x + dxd * dt_x] + dxa_bm + dxa_cm, axis=1) * (sg * (1.0 + xc * (1.0 - sg)))
        ext = jnp.concatenate([dxc, next_sc[...]], axis=0)
        xin = xbc_ref[...].astype(f32)
        dxbc = cw_ref[SSM_CONV - 1:SSM_CONV, :] * dxc
        dcw = [jnp.sum(dxc * xin, axis=0, keepdims=True)]
        for s in range(1, SSM_CONV):
            later = _sel_dot(_shift_mat(CHUNK, CHUNK + HALO, s), ext, 2)
            dxbc = dxbc + cw_ref[SSM_CONV - 1 - s:SSM_CONV - s, :] * later
            dcw.insert(0, jnp.sum(later * xin, axis=0, keepdims=True))
        dzxd_ref[:, SSM_WIDTH:SSM_WIDTH + CONV_DIM] = dxbc.astype(bf16)
        dcw_ref[...] += jnp.concatenate(dcw, axis=0)
        dcb_ref[...] += jnp.sum(dxc, axis=0, keepdims=True)
        next_sc[...] = dxc[0:HALO, :]

    SUB = SSD_SUB if nc % SSD_SUB == 0 else 1
    nb = nc // SUB

    def body(xbc_ref, xc_ref, sg_ref, y_ref, z_ref, dtr_ref, sprev_ref, dyb_ref, cw_ref, cb_ref, dtb_ref, alog_ref, dsk_ref,
             g_ref, ex_ref, rd_ref, dzxd_ref, dcw_ref, dcb_ref, ddtb_ref, dalog_ref, ddsk_ref, dg_ref, ds_sc, next_sc):
        @pl.when(pl.program_id(0) == 0)
        def _():
            ds_sc[...] = jnp.zeros_like(ds_sc)
            next_sc[...] = jnp.zeros_like(next_sc)
            for r_ in (dcw_ref, dcb_ref, ddtb_ref, dalog_ref, ddsk_ref, dg_ref):
                r_[...] = jnp.zeros_like(r_)

        for k in reversed(range(SUB)):
            rows = slice(k * CHUNK, (k + 1) * CHUNK)
            tok = [_RowsOf(r_, rows) for r_ in (xbc_ref, xc_ref, sg_ref, y_ref, z_ref, dtr_ref)]
            chunk(*tok, sprev_ref.at[k], _RowsOf(dyb_ref, rows), cw_ref, cb_ref, dtb_ref, alog_ref, dsk_ref, g_ref, ex_ref,
                  rd_ref, _RowsOf(dzxd_ref, rows), dcw_ref, dcb_ref, ddtb_ref, dalog_ref, ddsk_ref, dg_ref, ds_sc, next_sc)

    params = [conv_w, conv_b, dt_bias, a_log, d_skip, ssm_norm]
    mats = list(_head_mats())

    def rev(ncols):
        return pl.BlockSpec((SUB * CHUNK, ncols), lambda i: (nb - 1 - i, 0))

    in_specs = ([rev(CONV_DIM), rev(CONV_DIM), rev(CONV_DIM), rev(SSM_WIDTH), rev(SSM_WIDTH), rev(LANES),
                 pl.BlockSpec((SUB, PAIRS, PAIR_W, N), lambda i: (nb - 1 - i, 0, 0, 0)), rev(SSM_WIDTH)]
                + [_full(x.shape) for x in params + mats])
    return pl.pallas_call(
        body, name="ssd_bwd", grid=(nb,), in_specs=in_specs,
        out_specs=[rev(ZXD)] + [_acc(x.shape) for x in params],
        out_shape=[jax.ShapeDtypeStruct((T, ZXD), bf16)] + [jax.ShapeDtypeStruct(x.shape, f32) for x in params],
        scratch_shapes=[pltpu.VMEM((PAIRS, PAIR_W, N), f32), pltpu.VMEM((HALO, CONV_DIM), f32)],
        compiler_params=_params(("arbitrary",)))(xbc, xc, sg, y, z, dtr, sprev, dyb, *params, *mats)


def _tail(h3, p, tgt, gp, wpg, bpg, wpp, gf, tm=512):
    T, D = h3.shape
    tm = min(tm, T)

    def head(gpre, pp, h, gf_, t):
        gate = _sigmoid(gpre)
        y = _rms(h + gate * pp, gf_)
        err = y - t
        return 0.5 * jnp.sum(jnp.mean(err * err, axis=-1))

    def body(h_ref, p_ref, t_ref, gp_ref, wpg_ref, bpg_ref, wpp_ref, gf_ref,
             dh_ref, loss_ref, dgp_ref, dwpg_ref, dbpg_ref, dwpp_ref, dgf_ref):
        @pl.when(pl.program_id(0) == 0)
        def _():
            for r in (loss_ref, dgp_ref, dwpg_ref, dbpg_ref, dwpp_ref, dgf_ref):
                r[...] = jnp.zeros_like(r)

        h = h_ref[...]
        npf, np_vjp = jax.vjp(_rms, h, gp_ref[...])
        npb = npf.astype(bf16)
        pb = p_ref[...].astype(bf16)
        gpre = _dot(npb, wpg_ref[...]) + bpg_ref[...]
        kp, _, cp = wpp_ref.shape
        pp = jnp.concatenate([_dot(pb, wpp_ref[k]) for k in range(kp)], axis=1)
        loss, head_vjp = jax.vjp(head, gpre, pp, h, gf_ref[...], t_ref[...])
        dgpre, dpp, dh_a, dgf, _ = head_vjp(jnp.ones((), f32))
        loss_ref[...] += loss
        dgf_ref[...] += dgf
        dbpg_ref[...] += jnp.sum(dgpre, axis=0, keepdims=True)
        dgb = dgpre.astype(bf16)
        dwpg_ref[...] += _dot_tn(npb, dgb)
        dppb = dpp.astype(bf16)
        for k in range(kp):
            dwpp_ref[k] += _dot_tn(pb, dppb[:, k * cp:(k + 1) * cp])
        dh_b, dgp = np_vjp(_dot_nt(dgb, wpg_ref[...]))
        dgp_ref[...] += dgp
        dh_ref[...] = dh_a + dh_b

    ins = [h3, p, tgt, gp, wpg, bpg, wpp, gf]
    in_specs = [_rows(tm, D), _rows(tm, p.shape[1]), _rows(tm, D)] + [_full(x.shape) for x in ins[3:]]
    acc_shapes = [(1, LANES), gp.shape, wpg.shape, bpg.shape, wpp.shape, gf.shape]
    return pl.pallas_call(
        body, name="tail", grid=(T // tm,), in_specs=in_specs,
        out_specs=[_rows(tm, D)] + [_acc(s) for s in acc_shapes],
        out_shape=[jax.ShapeDtypeStruct((T, D), f32)] + [jax.ShapeDtypeStruct(s, f32) for s in acc_shapes],
        compiler_params=_params(("arbitrary",)))(*ins)


def _adamw(name, w, g, m, v, tr=256):
    R, rest = w.shape[0], w.shape[1:]
    tr = _row_tile(R, tr, 8 if len(rest) == 1 else 1)

    def body(w_ref, g_ref, m_ref, v_ref, d_ref, mo_ref, vo_ref):
        g_ = g_ref[...]
        m_ = ADAM_B1 * m_ref[...] + (1.0 - ADAM_B1) * g_
        v_ = ADAM_B2 * v_ref[...] + (1.0 - ADAM_B2) * jnp.square(g_)
        m_hat = m_ / (1.0 - ADAM_B1 ** ADAM_STEP)
        v_hat = v_ / (1.0 - ADAM_B2 ** ADAM_STEP)
        d_ref[...] = -ADAM_LR * (m_hat / (jnp.sqrt(v_hat) + ADAM_EPS) + ADAM_WD * w_ref[...])
        mo_ref[...] = m_
        vo_ref[...] = v_

    spec = pl.BlockSpec((tr,) + rest, lambda i: (i,) + (0,) * len(rest))
    return pl.pallas_call(body, name=name, grid=(R // tr,), in_specs=[spec] * 4, out_specs=[spec] * 3,
                          out_shape=[jax.ShapeDtypeStruct(w.shape, f32)] * 3,
                          compiler_params=_params(("parallel",)))(w, g, m, v)


HBM = pl.BlockSpec(memory_space=pltpu.HBM)


def _me():
    return lax.axis_index("x"), lax.axis_index("y"), lax.axis_index("c")


def _other_chips(x, y):
    return [(1 - x, y), (x, 1 - y), (1 - x, 1 - y)]


def _remote(src, dst, send_sem, recv_sem, dev):
    return pltpu.make_async_remote_copy(src_ref=src, dst_ref=dst, send_sem=send_sem, recv_sem=recv_sem,
                                        device_id=dev, device_id_type=MESH)


def _sems(n):
    return [pltpu.SemaphoreType.DMA((n,)), pltpu.SemaphoreType.DMA((n,))]


def _gather_weights(shards, split):
    n = len(shards)

    def body(*refs):
        ins, outs = refs[:n], refs[n:2 * n]
        own_send, own_recv, ici_send, ici_recv, d2d_send, d2d_recv = refs[2 * n:]
        x, y, c = _me()
        my_chip = 2 * x + y
        sibling = (x, y, 1 - c)
        chips = _other_chips(x, y)

        def rows(i, half):
            hr = shards[i].shape[0] // 2
            return pl.ds(half * hr, hr) if split[i] else pl.ds(0, shards[i].shape[0])

        sends = []
        for i in range(n):
            for j, chip in enumerate(chips):
                cp = _remote(ins[i].at[rows(i, c)], outs[i].at[my_chip, rows(i, c)],
                             ici_send.at[3 * i + j], ici_recv.at[3 * i + j], (*chip, c))
                cp.start()
                sends.append(cp)
            cp = _remote(ins[i], outs[i].at[my_chip], own_send.at[i], own_recv.at[i], sibling)
            cp.start()
            sends.append(cp)
        for i in range(n):
            for j, chip in enumerate(chips):
                s = 3 * i + j
                land = outs[i].at[2 * chip[0] + chip[1], rows(i, c)]
                _remote(land, land, ici_send.at[s], ici_recv.at[s], (*chip, c)).wait_recv()
                if split[i]:
                    cp = _remote(land, land, d2d_send.at[s], d2d_recv.at[s], sibling)
                    cp.start()
                    sends.append(cp)
        for i in range(n):
            _remote(ins[i], outs[i].at[my_chip], own_send.at[i], own_recv.at[i], sibling).wait_recv()
            if split[i]:
                for j, chip in enumerate(chips):
                    s = 3 * i + j
                    land = outs[i].at[2 * chip[0] + chip[1], rows(i, 1 - c)]
                    _remote(land, land, d2d_send.at[s], d2d_recv.at[s], sibling).wait_recv()
        for cp in sends:
            cp.wait_send()

    return pl.pallas_call(
        body, name="gather_weights", out_shape=[jax.ShapeDtypeStruct((N_CHIPS,) + s.shape, s.dtype) for s in shards],
        in_specs=[HBM] * n, out_specs=[HBM] * n,
        scratch_shapes=_sems(n) + _sems(3 * n) + _sems(3 * n))(*shards)


def _swap_halves(name, grads):
    n = len(grads)

    def body(*refs):
        ins, outs, send, recv = refs[:n], refs[n:2 * n], refs[2 * n], refs[2 * n + 1]
        x, y, c = _me()
        copies = []
        for i in range(n):
            hr = grads[i].shape[1] // 2
            cp = _remote(ins[i].at[:, pl.ds((1 - c) * hr, hr), :], outs[i], send.at[i], recv.at[i], (x, y, 1 - c))
            cp.start()
            copies.append(cp)
        for cp in copies:
            cp.wait()

    return pl.pallas_call(
        body, name=name,
        out_shape=[jax.ShapeDtypeStruct((g.shape[0], g.shape[1] // 2, g.shape[2]), g.dtype) for g in grads],
        in_specs=[HBM] * n, out_specs=[HBM] * n, scratch_shapes=_sems(n))(*grads)


def _add_halves(name, grads, other, c_idx, th=HALF_ROWS_BF16):
    K, R, C = grads.shape
    H = R // 2
    th = _row_tile(H, th, 16)
    nb = H // th

    def body(c_ref, g_ref, o_ref, out_ref):
        out_ref[...] = (g_ref[...].astype(f32) + o_ref[...].astype(f32)).astype(bf16)

    grid_spec = pltpu.PrefetchScalarGridSpec(
        num_scalar_prefetch=1, grid=(nb,),
        in_specs=[pl.BlockSpec((K, th, C), lambda i, c: (0, c[0] * nb + i, 0)),
                  pl.BlockSpec((K, th, C), lambda i, c: (0, i, 0))],
        out_specs=pl.BlockSpec((K, th, C), lambda i, c: (0, i, 0)))
    return pl.pallas_call(body, name=name, grid_spec=grid_spec,
                          out_shape=jax.ShapeDtypeStruct((K, H, C), bf16),
                          compiler_params=_params(("parallel",)))(c_idx, grads, other)


SEM = pl.BlockSpec(memory_space=pltpu.SEMAPHORE)
ANY = pl.BlockSpec(memory_space=pl.ANY)
EFFECT = pltpu.SideEffectType.DATAFLOW_SIDE_EFFECTING


def _copies_start(name, srcs, land_shapes, n_copies, make_copies, after):
    ns, nl = len(srcs), len(land_shapes)
    lands = [lax.empty(s.shape, s.dtype) for s in land_shapes]

    def body(*refs):
        src_refs, land_refs = refs[:ns], refs[ns:ns + nl]
        send, recv, token = refs[ns + nl + 1], refs[ns + nl + 2], refs[-1]
        for cp in make_copies(src_refs, land_refs, send, recv):
            cp.start()
        token[...] = jnp.zeros_like(token)

    buffers = list(srcs) + lands
    out = pl.pallas_call(
        body, name=name,
        out_shape=(pltpu.SemaphoreType.DMA((n_copies,)), pltpu.SemaphoreType.DMA((n_copies,)),
                   *[pltpu.HBM(b.shape, b.dtype) for b in buffers], jax.ShapeDtypeStruct((8, LANES), f32)),
        in_specs=[HBM] * (ns + nl) + [ANY],
        out_specs=(SEM, SEM, *[HBM] * (ns + nl), pl.BlockSpec(memory_space=pltpu.VMEM)),
        input_output_aliases={i: 2 + i for i in range(ns + nl)},
        compiler_params=pltpu.CompilerParams(has_side_effects=EFFECT),
    )(*[pltpu.with_memory_space_constraint(b, pltpu.HBM) for b in buffers], after)
    return out[0], out[1], list(out[2:2 + ns]), list(out[2 + ns:2 + ns + nl]), out[-1]


def _copies_wait(name, started, make_copies, after):
    send, recv, srcs, lands, _ = started
    ns, nl = len(srcs), len(lands)
    after = list(after)

    def body(*refs):
        src_refs, land_refs = refs[:ns], refs[ns:ns + nl]
        for cp in make_copies(src_refs, land_refs, refs[ns + nl], refs[ns + nl + 1]):
            cp.wait_send()
            cp.wait_recv()

    buffers = list(srcs) + list(lands)
    out = pl.pallas_call(
        body, name=name, out_shape=tuple(pltpu.HBM(b.shape, b.dtype) for b in buffers),
        in_specs=[HBM] * (ns + nl) + [SEM, SEM] + [ANY] * len(after), out_specs=tuple([HBM] * (ns + nl)),
        input_output_aliases={i: i for i in range(ns + nl)},
        compiler_params=pltpu.CompilerParams(has_side_effects=EFFECT),
    )(*buffers, send, recv, *after)
    return list(out[:ns]), list(out[ns:])


def _gather_copies(src_refs, land_refs, send, recv):
    x, y, c = _me()
    my_chip = 2 * x + y
    peers = [(*chip, c) for chip in _other_chips(x, y)] + [(x, y, 1 - c)]
    return [_remote(src_refs[i], land_refs[i].at[my_chip], send.at[4 * i + j], recv.at[4 * i + j], peer)
            for i in range(len(src_refs)) for j, peer in enumerate(peers)]


def _swap_copies(src_refs, land_refs, send, recv):
    x, y, c = _me()
    copies = []
    for i in range(len(src_refs)):
        hr = src_refs[i].shape[1] // 2
        copies.append(_remote(src_refs[i].at[:, pl.ds((1 - c) * hr, hr), :], land_refs[i], send.at[i], recv.at[i], (x, y, 1 - c)))
    return copies


def _share_copies(src_refs, land_refs, send, recv):
    x, y, c = _me()
    return [_remote(src_refs[i], land_refs[i], send.at[i], recv.at[i], (x, y, 1 - c)) for i in range(len(src_refs))]


def _partial_copies(src_refs, land_refs, send, recv):
    x, y, c = _me()
    return [_remote(src_refs[i].at[2 * chip[0] + chip[1]], land_refs[i].at[j], send.at[3 * i + j], recv.at[3 * i + j], (*chip, c))
            for i in range(len(src_refs)) for j, chip in enumerate(_other_chips(x, y))]


def _small_copies(src_refs, land_refs, send, recv):
    x, y, c = _me()
    return [_remote(src_refs[0], land_refs[0].at[k - 1], send.at[k - 1], recv.at[k - 1], (x ^ (k >> 2), y ^ ((k >> 1) & 1), c ^ (k & 1)))
            for k in range(1, N_DEV)]


def _sum_small(own, slots, dev_idx):
    R, C = own.shape

    def body(dev_ref, own_ref, s_ref, o_ref):
        me = dev_ref[0]
        acc = jnp.zeros((R, C), f32)
        for d in range(N_DEV):
            k = me ^ d
            acc = acc + jnp.where(k == 0, own_ref[...], s_ref[jnp.maximum(k - 1, 0)])
        o_ref[...] = acc

    grid_spec = pltpu.PrefetchScalarGridSpec(
        num_scalar_prefetch=1, grid=(1,),
        in_specs=[pl.BlockSpec((R, C), lambda i, dev: (0, 0)), pl.BlockSpec((N_DEV - 1, R, C), lambda i, dev: (0, 0, 0))],
        out_specs=pl.BlockSpec((R, C), lambda i, dev: (0, 0)))
    return pl.pallas_call(body, name="sum_small", grid_spec=grid_spec, out_shape=jax.ShapeDtypeStruct((R, C), f32),
                          compiler_params=_params(("arbitrary",)))(dev_idx, own, slots)


def _sum_partials(name, part, recv, chip_idx, th=HALF_ROWS_BF16):
    K, H, C = part.shape
    th = _row_tile(H, th, 16)

    def body(chip_ref, p_ref, r_ref, o_ref):
        acc = p_ref[...].astype(f32)
        for j in range(3):
            acc = acc + r_ref[j].astype(f32)
        o_ref[...] = acc

    grid_spec = pltpu.PrefetchScalarGridSpec(
        num_scalar_prefetch=1, grid=(H // th,),
        in_specs=[pl.BlockSpec((None, th, C), lambda i, chip: (chip[0], i, 0)),
                  pl.BlockSpec((3, th, C), lambda i, chip: (0, i, 0))],
        out_specs=pl.BlockSpec((th, C), lambda i, chip: (i, 0)))
    return pl.pallas_call(body, name=name, grid_spec=grid_spec, out_shape=jax.ShapeDtypeStruct((H, C), f32),
                          compiler_params=_params(("parallel",)))(chip_idx, part, recv)


def _adamw_big(name, w, g_mine, g_theirs, m, v, c_idx, tr=HALF_ROWS_F32):
    R, C = w.shape
    H = R // 2
    tr = _row_tile(H, tr)
    nb = H // tr

    def body(c_ref, w_ref, gm_ref, gt_ref, m_ref, v_ref, g_ref, d_ref, mo_ref, vo_ref):
        g_ = jnp.where(pl.program_id(0) // nb == c_ref[0], gm_ref[...], gt_ref[...])
        g_ref[...] = g_
        m_ = ADAM_B1 * m_ref[...] + (1.0 - ADAM_B1) * g_
        v_ = ADAM_B2 * v_ref[...] + (1.0 - ADAM_B2) * jnp.square(g_)
        m_hat = m_ / (1.0 - ADAM_B1 ** ADAM_STEP)
        v_hat = v_ / (1.0 - ADAM_B2 ** ADAM_STEP)
        d_ref[...] = -ADAM_LR * (m_hat / (jnp.sqrt(v_hat) + ADAM_EPS) + ADAM_WD * w_ref[...])
        mo_ref[...] = m_
        vo_ref[...] = v_

    full = pl.BlockSpec((tr, C), lambda i, c: (i, 0))
    half = pl.BlockSpec((tr, C), lambda i, c: (i % nb, 0))
    grid_spec = pltpu.PrefetchScalarGridSpec(num_scalar_prefetch=1, grid=(2 * nb,),
                                             in_specs=[full, half, half, full, full], out_specs=[full] * 4)
    return pl.pallas_call(body, name=name, grid_spec=grid_spec, out_shape=[jax.ShapeDtypeStruct((R, C), f32)] * 4,
                          compiler_params=_params(("parallel",)))(c_idx, w, g_mine, g_theirs, m, v)


BIG = ("ffn1_w_gate", "ffn1_w_up", "ffn1_w_down", "w_in", "w_out", "ffn2_w_gate", "ffn2_w_up", "ffn2_w_down",
       "ple_w_gate", "ple_w_proj")


SMALL = ("ffn1_norm", "mix_norm", "gm_ln_g", "gm_ln_b", "gm_w_s", "gm_b_s", "gm_out_norm", "conv_b", "dt_bias", "a_log",
         "d_skip", "ssm_norm", "ffn2_norm", "ple_norm", "ple_b_gate", "final_norm")
SMALL_C = 1024


def _pack_small(vals):
    parts = []
    for v in vals:
        f = v.astype(f32).reshape(-1)
        parts.append(jnp.pad(f, (0, -f.shape[0] % SMALL_C)))
    flat = jnp.concatenate(parts)
    rows = flat.shape[0] // SMALL_C
    return jnp.pad(flat, (0, (-rows % 8) * SMALL_C)).reshape(-1, SMALL_C)


def _unpack_small(pack, shapes):
    flat = pack.reshape(-1)
    out, off = [], 0
    for s in shapes:
        n = 1
        for d in s:
            n *= d
        out.append(flat[off:off + n].reshape(s))
        off += n + (-n % SMALL_C)
    return out


def _pad_lanes(v):
    return jnp.pad(v, ((0, 0), (0, LANES - v.shape[1])))


def _pad_rows(a):
    pad = [(0, 0)] * a.ndim
    pad[-2] = (0, -a.shape[-2] % ROW_PAD)
    return jnp.pad(a, pad) if pad[-2][1] else a


FETCH = (("ffn1_w_gate", "ffn1_w_up", "ffn1_w_down"), ("w_in", "conv_w", "w_out"),
         ("ffn2_w_gate", "ffn2_w_up", "ffn2_w_down", "ple_w_gate", "ple_w_proj"))
TRANSPOSED = ("ffn1_w_gate", "ffn1_w_up", "ffn2_w_gate", "ffn2_w_up", "w_in")
ROW_PAD = 32
DONE = (("ffn2_w_gate", "ffn2_w_up", "ffn2_w_down", "w_out", "ple_w_gate", "ple_w_proj"), ("w_in",),
        ("ffn1_w_gate", "ffn1_w_up", "ffn1_w_down"))


def _local_step(x, p, tgt, fetch, S, on_grads, on_later):
    G = GM_WIDTH
    K = N_CHIPS
    b_st = S["gm_b_s"][0].T
    w_s = S["gm_w_s"][0]
    dtb, alog, dsk = _pad_lanes(S["dt_bias"]), _pad_lanes(S["a_log"]), _pad_lanes(S["d_skip"])
    gfin = S["final_norm"].reshape(1, -1)

    def rows(a):
        return a.reshape(-1, D_MODEL)

    def shards(a):
        return a.reshape(K, -1, D_MODEL)

    wg1, wu1, wd1 = [rows(a) for a in fetch(0, None)]
    h1, n1, a1, b1 = _ffn_fwd("ffn1_fwd", x, S["ffn1_norm"], wg1, wu1, wd1)
    w_in4, cw4, wo4 = fetch(1, h1)
    w_in = w_in4.reshape(IN_PROJ, D_MODEL)
    w_uv = w_in[:2 * G]
    w_zxd = jnp.pad(w_in[2 * G:], ((0, ZXD - (IN_PROJ - 2 * G)), (0, 0)))
    conv_w = jnp.transpose(cw4, (1, 0, 2)).reshape(SSM_CONV, CONV_DIM)
    wo = wo4.reshape(-1, D_MODEL)
    n2, act, slope, z, xbc, dtr, ya = _mix_fwd(h1, S["mix_norm"], w_uv, w_zxd, S["gm_ln_g"], S["gm_ln_b"], w_s, b_st,
                                               S["gm_out_norm"])
    yb, xc, sg, y_ssd, sprev = _ssd_fwd(xbc, z, dtr, conv_w, S["conv_b"], dtb, alog, dsk, S["ssm_norm"])
    wg2, wu2, wd2, wpg4, wpp4 = fetch(2, yb)
    wg2, wu2, wd2 = rows(wg2), rows(wu2), rows(wd2)
    h2, h3, n3, a2, b2 = _ffn_fwd("ffn2_fwd", h1, S["ffn2_norm"], wg2, wu2, wd2, pre=(ya, yb, wo))
    dh3, loss, dgp, dwpg, dbpg, dwpp, dgf = _tail(h3, p, tgt, S["ple_norm"], wpg4.reshape(-1, D_MODEL), S["ple_b_gate"], wpp4, gfin)
    dh2, da2, db2, hm2, dg_ffn2, dya, dyb = _ffn_bwd("ffn2_bwd", dh3, h2, S["ffn2_norm"], a2, b2, wg2, wu2, wd2, wo=wo, ga=G)
    dw_out = jnp.concatenate([_matmul_tn("dw_out_a", ya, dh2), _matmul_tn("dw_out_b", yb, dh2)], axis=0).reshape(wo4.shape)
    zero = on_grads(0, [shards(_matmul_tn("dw_ffn2_gate", da2, n3)), shards(_matmul_tn("dw_ffn2_up", db2, n3)),
                        shards(_matmul_tn("dw_ffn2_down", hm2, dh3, scale=0.5)), dw_out,
                        dwpg.astype(bf16).reshape(wpg4.shape), dwpp.astype(bf16)])
    dzxd, dcw, dcb, ddtb, dalog, ddsk, dgssm = _ssd_bwd(xbc, xc, sg, y_ssd, z, dtr, sprev, dyb, conv_w, S["conv_b"], dtb, alog, dsk,
                                                        S["ssm_norm"] + zero)
    zero = on_later(0, dgssm)
    dh1, duv, dg_mix, dlng, dlnb, dws, dbst, dgout = _mix_bwd(dh2, h1, S["mix_norm"] + zero, act, slope, dya, dzxd, w_uv, w_zxd,
                                                              S["gm_ln_g"], S["gm_ln_b"], w_s, b_st, S["gm_out_norm"])
    dw_in = jnp.concatenate([_matmul_tn("dw_in_uv", duv, n2), _matmul_tn("dw_in_zxd", dzxd, n2)[:IN_PROJ - 2 * G]], axis=0)
    zero = on_grads(1, [dw_in.reshape(w_in4.shape)])
    dx, da1, db1, hm1, dg_ffn1 = _ffn_bwd("ffn1_bwd", dh1, x, S["ffn1_norm"] + zero, a1, b1, wg1, wu1, wd1)
    loss = loss + on_later(1, dg_ffn1)
    zero = on_grads(2, [shards(_matmul_tn("dw_ffn1_gate", da1, n1)), shards(_matmul_tn("dw_ffn1_up", db1, n1)),
                        shards(_matmul_tn("dw_ffn1_down", hm1, dh1, scale=0.5))])
    loss = loss + zero
    nh = SSM_HEADS
    gS = {"ffn1_norm": dg_ffn1, "mix_norm": dg_mix, "gm_ln_g": dlng, "gm_ln_b": dlnb, "gm_w_s": dws[None], "gm_b_s": dbst.T[None],
          "gm_out_norm": dgout, "conv_b": dcb, "dt_bias": ddtb[:, :nh], "a_log": dalog[:, :nh], "d_skip": ddsk[:, :nh],
          "ssm_norm": dgssm, "ffn2_norm": dg_ffn2, "ple_norm": dgp, "ple_b_gate": dbpg, "final_norm": dgf.reshape(-1)}
    return loss, dx, dcw, gS


_WEIGHTS = ("ffn1_norm", "ffn1_w_gate", "ffn1_w_up", "ffn1_w_down", "mix_norm", "w_in", "gm_ln_g", "gm_ln_b", "gm_w_s", "gm_b_s",
            "gm_out_norm", "conv_w", "conv_b", "dt_bias", "a_log", "d_skip", "ssm_norm", "w_out", "ffn2_norm", "ffn2_w_gate",
            "ffn2_w_up", "ffn2_w_down", "ple_norm", "ple_w_gate", "ple_b_gate", "ple_w_proj", "final_norm")
_BIG_NAMES = BIG


def kernel(x, p, ffn1_norm, ffn1_w_gate, ffn1_w_up, ffn1_w_down, mix_norm, w_in, gm_ln_g, gm_ln_b, gm_w_s, gm_b_s, gm_out_norm, conv_w, conv_b, dt_bias, a_log, d_skip, ssm_norm, w_out, ffn2_norm, ffn2_w_gate, ffn2_w_up, ffn2_w_down, ple_norm, ple_w_gate, ple_b_gate, ple_w_proj, final_norm, loss_target, m_ffn1_norm, m_ffn1_w_gate, m_ffn1_w_up, m_ffn1_w_down, m_mix_norm, m_w_in, m_gm_ln_g, m_gm_ln_b, m_gm_w_s, m_gm_b_s, m_gm_out_norm, m_conv_w, m_conv_b, m_dt_bias, m_a_log, m_d_skip, m_ssm_norm, m_w_out, m_ffn2_norm, m_ffn2_w_gate, m_ffn2_w_up, m_ffn2_w_down, m_ple_norm, m_ple_w_gate, m_ple_b_gate, m_ple_w_proj, m_final_norm, v_ffn1_norm, v_ffn1_w_gate, v_ffn1_w_up, v_ffn1_w_down, v_mix_norm, v_w_in, v_gm_ln_g, v_gm_ln_b, v_gm_w_s, v_gm_b_s, v_gm_out_norm, v_conv_w, v_conv_b, v_dt_bias, v_a_log, v_d_skip, v_ssm_norm, v_w_out, v_ffn2_norm, v_ffn2_w_gate, v_ffn2_w_up, v_ffn2_w_down, v_ple_norm, v_ple_w_gate, v_ple_b_gate, v_ple_w_proj, v_final_norm):
    given = dict(locals())
    w = {n: given[n] for n in _WEIGHTS}
    m = {n: given["m_" + n] for n in _WEIGHTS}
    v = {n: given["v_" + n] for n in _WEIGHTS}

    c_idx = lax.axis_index("c").astype(jnp.int32).reshape(1)
    chip = 2 * lax.axis_index("x") + lax.axis_index("y")
    chip_idx = chip.astype(jnp.int32).reshape(1)

    shard = {n: (jnp.swapaxes(w[n][0], 0, 1) if n in TRANSPOSED else w[n][0]).astype(bf16) for n in BIG}
    shard["conv_w"] = w["conv_w"][0]
    first = _gather_weights([shard[n] for n in FETCH[0]], [True] * len(FETCH[0]))
    fetching, after = [], first[-1]
    for k in (1, 2):
        srcs = [shard[n] for n in FETCH[k]]
        lands = [jax.ShapeDtypeStruct((N_CHIPS,) + s.shape, s.dtype) for s in srcs]
        fetching.append(_copies_start("gather%d_start" % k, srcs, lands, 4 * len(srcs), _gather_copies, after))
        after = fetching[-1][4]

    def fetch(k, after_):
        return first if k == 0 else _copies_wait("gather%d_wait" % k, fetching[k - 1], _gather_copies, [after_])[1]

    swapping, exchanging = {}, {}

    def exchange(k, grads, others):
        parts = [_add_halves("add_" + n, g_, o_, c_idx) for n, g_, o_ in zip(DONE[k], grads, others)]
        lands = [jax.ShapeDtypeStruct((3,) + p_.shape[1:], p_.dtype) for p_ in parts]
        exchanging[k] = _copies_start("exchange%d_start" % k, parts, lands, 3 * len(parts), _partial_copies, c_idx)
        return exchanging[k][4][0, 0]

    def on_grads(k, grads):
        grads = [_pad_rows(g_) for g_ in grads]
        if k == len(DONE) - 1:
            return exchange(k, grads, _swap_halves("swap%d" % k, grads))
        lands = [jax.ShapeDtypeStruct((g_.shape[0], g_.shape[1] // 2, g_.shape[2]), g_.dtype) for g_ in grads]
        swapping[k] = _copies_start("swap%d_start" % k, grads, lands, len(grads), _swap_copies, c_idx)
        return swapping[k][4][0, 0]

    def on_later(k, after_):
        return exchange(k, *_copies_wait("swap%d_wait" % k, swapping[k], _swap_copies, [after_]))

    S = {n: w[n] for n in SMALL}
    S["ffn1_norm"] = S["ffn1_norm"] + after[0, 0]
    loss, dx, dcw, gS = _local_step(x[0], p[0, 0], loss_target[0], fetch, S, on_grads, on_later)

    small = _pack_small([gS[n] for n in SMALL] + [dcw, loss[:, :1]])
    small_lands = [jax.ShapeDtypeStruct((N_DEV - 1,) + small.shape, small.dtype)]
    small_st = _copies_start("small_start", [small], small_lands, N_DEV - 1, _small_copies, c_idx)

    g, delta, new_m, new_v = {}, {}, {}, {}

    def share_start(k, after_):
        parts, recv = _copies_wait("exchange%d_wait" % k, exchanging[k], _partial_copies, after_)
        mine = [_sum_partials("sum_" + n, p_, r_, chip_idx) for n, p_, r_ in zip(DONE[k], parts, recv)]
        return _copies_start("share%d_start" % k, mine, mine, len(mine), _share_copies, c_idx)

    def update(k, sharing, after_):
        mine, theirs = _copies_wait("share%d_wait" % k, sharing, _share_copies, after_)
        after = []
        for n, gm_, gt_ in zip(DONE[k], mine, theirs):
            flip = (lambda a: jnp.swapaxes(a, 0, 1)) if n in TRANSPOSED else (lambda a: a)
            rows = flip(w[n][0]).shape[0]
            if rows % ROW_PAD:
                assert n in TRANSPOSED
                def lin(a):
                    return jnp.transpose(a.reshape(-1, LANES, rows), (2, 0, 1))

                def back(a):
                    return jnp.transpose(a, (1, 2, 0)).reshape(1, -1, rows)

                gm_, gt_ = [a.reshape(a.shape[0], -1, LANES) for a in (gm_, gt_)]
                g_ = jnp.where(c_idx[0] == 0, jnp.concatenate([gm_, gt_]), jnp.concatenate([gt_, gm_]))[:rows]
                outs = [g_, *_adamw("adamw_" + n, lin(w[n]), g_, lin(m[n]), lin(v[n]), tr=HALF_ROWS_F32)]
                g[n], delta[n], new_m[n], new_v[n] = [back(o) for o in outs]
            else:
                outs = _adamw_big("adamw_" + n, flip(w[n][0]), gm_, gt_, flip(m[n][0]), flip(v[n][0]), c_idx)
                g[n], delta[n], new_m[n], new_v[n] = [flip(o)[None] for o in outs]
            after.append(outs[3])
        return after

    def update_small(after_):
        (own,), (slots,) = _copies_wait("small_wait", small_st, _small_copies, after_)
        dev_idx = (2 * chip + lax.axis_index("c")).astype(jnp.int32).reshape(1)
        small_shapes = [w[n].shape for n in SMALL] + [dcw.shape, (1, 1)]
        small_sum = _unpack_small(_sum_small(own, slots, dev_idx), small_shapes)
        g.update({n: small_sum[i] for i, n in enumerate(SMALL)})
        cshard = w["conv_w"].shape[2]
        g["conv_w"] = lax.dynamic_slice_in_dim(small_sum[len(SMALL)], chip * cshard, cshard, axis=1)[None]
        sm_names = SMALL + ("conv_w",)
        sm_shapes = [w[n].shape for n in sm_names]
        d_s, m_s, v_s = _adamw("adamw_small", _pack_small([w[n] for n in sm_names]), _pack_small([g[n] for n in sm_names]),
                               _pack_small([m[n] for n in sm_names]), _pack_small([v[n] for n in sm_names]))
        for dst, src in ((delta, d_s), (new_m, m_s), (new_v, v_s)):
            for n, val in zip(sm_names, _unpack_small(src, sm_shapes)):
                dst[n] = val
        return small_sum[len(SMALL) + 1].reshape(()), [v_s]

    assert len(DONE) >= 2
    sharing = [share_start(0, [small_st[4]])]
    after = [sharing[0][4]]
    for k in range(len(DONE)):
        if k + 1 < len(DONE):
            if k + 2 == len(DONE):
                loss_total, after = update_small(after)
            sharing.append(share_start(k + 1, after))
            after = [sharing[k + 1][4]]
        after = update(k, sharing[k], after)

    return (loss_total, dx[None], *[g[n] for n in _WEIGHTS], *[delta[n] for n in _WEIGHTS],
            *[new_m[n] for n in _WEIGHTS], *[new_v[n] for n in _WEIGHTS])
```

```python
import jax
import jax.numpy as jnp
from jax import lax
from jax.experimental import pallas as pl
from jax.experimental.pallas import tpu as pltpu

f32 = jnp.float32
bf16 = jnp.bfloat16
MESH = pl.DeviceIdType.MESH
HIGHEST = lax.Precision.HIGHEST

EPS = 1e-6
N_CHIPS = 4
N_DEV = 8
D_MODEL = 1024
GM_WIDTH = 1024
GM_HEADS = 8
CHUNK = 128
SSM_WIDTH = 1024
SSM_HEADS = 16
SSM_HEAD_DIM = 64
SSM_GROUPS = 2
SSM_STATE = 128
SSM_CONV = 4
CONV_DIM = SSM_WIDTH + 2 * SSM_GROUPS * SSM_STATE
IN_PROJ = 2 * GM_WIDTH + SSM_WIDTH + CONV_DIM + SSM_HEADS
LANES = 128
ZXD = SSM_WIDTH + CONV_DIM + LANES

ADAM_LR = 0.001
ADAM_B1 = 0.9
ADAM_B2 = 0.999
ADAM_EPS = 1e-08
ADAM_WD = 0.01
ADAM_STEP = 10

VMEM_LIMIT = 56 * 1024 * 1024
HALF_ROWS_BF16 = 592
HALF_ROWS_F32 = 320


def _dot(a, b):
    return jnp.dot(a, b, preferred_element_type=f32)


def _dot_nt(a, b):
    return lax.dot_general(a, b, (((1,), (1,)), ((), ())), preferred_element_type=f32)


def _dot_tn(a, b):
    return lax.dot_general(a, b, (((0,), (0,)), ((), ())), preferred_element_type=f32)


def _rms(x, g):
    return x * lax.rsqrt(jnp.mean(x * x, axis=-1, keepdims=True) + EPS) * g


def _layernorm(x, g, b):
    mu = jnp.mean(x, axis=-1, keepdims=True)
    xc = x - mu
    return xc * lax.rsqrt(jnp.mean(xc * xc, axis=-1, keepdims=True) + EPS) * g + b


def _sigmoid(x):
    return 1.0 / (1.0 + jnp.exp(-x))


def _softplus(x):
    return jnp.maximum(x, 0.0) + jnp.log(1.0 + jnp.exp(-jnp.abs(x)))


def _full(shape):
    nd = len(shape)
    return pl.BlockSpec(shape, lambda *_: (0,) * nd, pipeline_mode=pl.Buffered(1))


def _acc(shape):
    nd = len(shape)
    return pl.BlockSpec(shape, lambda *_: (0,) * nd)


def _rows(tm, ncols):
    return pl.BlockSpec((tm, ncols), lambda i: (i, 0))


def _params(sem):
    return pltpu.CompilerParams(dimension_semantics=sem, vmem_limit_bytes=VMEM_LIMIT)


def _row_tile(rows, target, mult=8):
    best = rows
    for t in range(mult, min(rows, target) + 1, mult):
        if rows % t == 0:
            best = t
    return best if best <= target else rows


def _ffn_fwd(name, h, g, wg, wu, wd, pre=None, tm=256):
    T, D = h.shape
    F = wg.shape[0]
    tm = min(tm, T)

    def body(*refs):
        if pre is None:
            h_ref, g_ref, wg_ref, wu_ref, wd_ref, ho_ref, n_ref, a_ref, b_ref = refs
            hin = h_ref[...]
        else:
            (h_ref, ya_ref, yb_ref, wo_ref, g_ref, wg_ref, wu_ref, wd_ref,
             hi_ref, ho_ref, n_ref, a_ref, b_ref) = refs
            ga = ya_ref.shape[1]
            hin = h_ref[...] + _dot(ya_ref[...], wo_ref[:ga, :]) + _dot(yb_ref[...], wo_ref[ga:, :])
            hi_ref[...] = hin
        n = _rms(hin, g_ref[...]).astype(bf16)
        n_ref[...] = n
        a = _dot_nt(n, wg_ref[...]).astype(bf16)
        b = _dot_nt(n, wu_ref[...]).astype(bf16)
        a_ref[...] = a
        b_ref[...] = b
        af = a.astype(f32)
        hm = (af * _sigmoid(af) * b.astype(f32)).astype(bf16)
        ho_ref[...] = hin + 0.5 * _dot(hm, wd_ref[...])

    ins = [h] + (list(pre) if pre is not None else []) + [g, wg, wu, wd]
    in_specs = [_rows(tm, D)]
    if pre is not None:
        in_specs += [_rows(tm, pre[0].shape[1]), _rows(tm, pre[1].shape[1]), _full(pre[2].shape)]
    in_specs += [_full(g.shape), _full(wg.shape), _full(wu.shape), _full(wd.shape)]
    outs = [jax.ShapeDtypeStruct((T, D), f32), jax.ShapeDtypeStruct((T, D), bf16),
            jax.ShapeDtypeStruct((T, F), bf16), jax.ShapeDtypeStruct((T, F), bf16)]
    out_specs = [_rows(tm, D), _rows(tm, D), _rows(tm, F), _rows(tm, F)]
    if pre is not None:
        outs = [jax.ShapeDtypeStruct((T, D), f32)] + outs
        out_specs = [_rows(tm, D)] + out_specs
    return pl.pallas_call(body, name=name, grid=(T // tm,), in_specs=in_specs, out_specs=out_specs,
                          out_shape=outs, compiler_params=_params(("parallel",)))(*ins)


def _ffn_bwd(name, dh, hin, g, a, b, wg, wu, wd, wo=None, ga=0, tm=256):
    T, D = dh.shape
    F = wg.shape[0]
    tm = min(tm, T)

    def body(*refs):
        if wo is None:
            (dh_ref, hin_ref, g_ref, a_ref, b_ref, wg_ref, wu_ref, wd_ref,
             dhi_ref, da_ref, db_ref, hm_ref, dg_ref) = refs
        else:
            (dh_ref, hin_ref, g_ref, a_ref, b_ref, wg_ref, wu_ref, wd_ref, wo_ref,
             dhi_ref, da_ref, db_ref, hm_ref, dg_ref, dya_ref, dyb_ref) = refs

        @pl.when(pl.program_id(0) == 0)
        def _():
            dg_ref[...] = jnp.zeros_like(dg_ref)

        dh_ = dh_ref[...]
        dhb = (0.5 * dh_).astype(bf16)
        dhm = _dot_nt(dhb, wd_ref[...])
        af = a_ref[...].astype(f32)
        bf = b_ref[...].astype(f32)
        sg = _sigmoid(af)
        sl_ = af * sg
        da = (dhm * bf * (sg * (1.0 + af * (1.0 - sg)))).astype(bf16)
        db = (dhm * sl_).astype(bf16)
        da_ref[...] = da
        db_ref[...] = db
        hm_ref[...] = (sl_ * bf).astype(bf16)
        dn = _dot(da, wg_ref[...]) + _dot(db, wu_ref[...])
        _, vjp = jax.vjp(_rms, hin_ref[...], g_ref[...])
        dx, dg = vjp(dn)
        dhi = dh_ + dx
        dhi_ref[...] = dhi
        dg_ref[...] += dg
        if wo is not None:
            dhib = dhi.astype(bf16)
            dya_ref[...] = _dot_nt(dhib, wo_ref[:ga, :]).astype(bf16)
            dyb_ref[...] = _dot_nt(dhib, wo_ref[ga:, :]).astype(bf16)

    ins = [dh, hin, g, a, b, wg, wu, wd]
    in_specs = [_rows(tm, D), _rows(tm, D), _full(g.shape), _rows(tm, F), _rows(tm, F),
                _full(wg.shape), _full(wu.shape), _full(wd.shape)]
    act = jax.ShapeDtypeStruct((T, F), bf16)
    outs = [jax.ShapeDtypeStruct((T, D), f32), act, act, act, jax.ShapeDtypeStruct(g.shape, f32)]
    out_specs = [_rows(tm, D), _rows(tm, F), _rows(tm, F), _rows(tm, F), _acc(g.shape)]
    if wo is not None:
        gb = wo.shape[0] - ga
        ins += [wo]
        in_specs += [_full(wo.shape)]
        outs += [jax.ShapeDtypeStruct((T, ga), bf16), jax.ShapeDtypeStruct((T, gb), bf16)]
        out_specs += [_rows(tm, ga), _rows(tm, gb)]
    return pl.pallas_call(body, name=name, grid=(T // tm,), in_specs=in_specs, out_specs=out_specs,
                          out_shape=outs, compiler_params=_params(("arbitrary",)))(*ins)


def _matmul_tn(name, a, b, scale=1.0, tk=2048):
    T, M = a.shape
    N = b.shape[1]
    tk = min(tk, T)
    nk = T // tk
    tn = LANES * max(d for d in range(1, N // LANES + 1) if (N // LANES) % d == 0 and (d == 1 or M * d * LANES * 4 <= 6 * 1024 * 1024))

    def body(a_ref, b_ref, o_ref, acc):
        k = pl.program_id(1)

        @pl.when(k == 0)
        def _():
            acc[...] = jnp.zeros_like(acc)

        bb = b_ref[...]
        if scale != 1.0:
            bb = bb * scale
        acc[...] += _dot_tn(a_ref[...].astype(bf16), bb.astype(bf16))

        @pl.when(k == nk - 1)
        def _():
            o_ref[...] = acc[...].astype(bf16)

    return pl.pallas_call(
        body, name=name, grid=(N // tn, nk),
        in_specs=[pl.BlockSpec((tk, M), lambda j, k: (k, 0)), pl.BlockSpec((tk, tn), lambda j, k: (k, j))],
        out_specs=pl.BlockSpec((M, tn), lambda j, k: (0, j)),
        out_shape=jax.ShapeDtypeStruct((M, N), bf16), scratch_shapes=[pltpu.VMEM((M, tn), f32)],
        compiler_params=_params(("parallel", "arbitrary")))(a, b)


def _gelu_and_slope(x):
    cdf = 0.5 * (1.0 + lax.erf(x * 0.7071067811865476))
    return x * cdf, cdf + x * (0.3989422804014327 * jnp.exp(-0.5 * x * x))


def _tril_mask():
    r = lax.broadcasted_iota(jnp.int32, (CHUNK, CHUNK), 0)
    c = lax.broadcasted_iota(jnp.int32, (CHUNK, CHUNK), 1)
    return c <= r


def _gm_mix(vnb, ws_ref, bst, mixed_sc, tm):
    mask = _tril_mask()
    for h in range(GM_HEADS):
        wt = jnp.where(mask, ws_ref[h], 0.0).astype(bf16)
        bias = bst[:, h:h + 1]
        for q in range(tm // CHUNK):
            rs = slice(q * CHUNK, (q + 1) * CHUNK)
            cs = slice(h * CHUNK, (h + 1) * CHUNK)
            mixed_sc[rs, cs] = _dot(wt, vnb[rs, cs]) + bias


def _mix_fwd(h1, gmix, w_uv, w_zxd, ln_g, ln_b, w_s, b_st, gout, tm=512):
    T, D = h1.shape
    tm = min(tm, T)
    G = GM_WIDTH

    def body(h_ref, g_ref, wuv_ref, wzxd_ref, lng_ref, lnb_ref, ws_ref, bst_ref, gout_ref,
             n_ref, act_ref, slope_ref, z_ref, xbc_ref, dt_ref, ya_ref, mixed_sc):
        n = _rms(h_ref[...], g_ref[...]).astype(bf16)
        n_ref[...] = n
        uv = _dot_nt(n, wuv_ref[...]).astype(bf16)
        zxd = _dot_nt(n, wzxd_ref[...])
        z_ref[...] = zxd[:, :SSM_WIDTH].astype(bf16)
        xbc_ref[...] = zxd[:, SSM_WIDTH:SSM_WIDTH + CONV_DIM].astype(bf16)
        dt_ref[...] = zxd[:, SSM_WIDTH + CONV_DIM:]
        act, slope = _gelu_and_slope(uv.astype(f32))
        act = act.astype(bf16)
        act_ref[...] = act
        slope_ref[...] = slope.astype(bf16)
        ug, vg = act[:, :G].astype(f32), act[:, G:].astype(f32)
        _gm_mix(_layernorm(vg, lng_ref[...], lnb_ref[...]).astype(bf16), ws_ref, bst_ref[...], mixed_sc, tm)
        ya_ref[...] = _rms(ug * mixed_sc[...], gout_ref[...]).astype(bf16)

    ins = [h1, gmix, w_uv, w_zxd, ln_g, ln_b, w_s, b_st, gout]
    in_specs = [_rows(tm, D)] + [_full(x.shape) for x in ins[1:]]
    outs = [jax.ShapeDtypeStruct((T, D), bf16), jax.ShapeDtypeStruct((T, 2 * G), bf16), jax.ShapeDtypeStruct((T, 2 * G), bf16),
            jax.ShapeDtypeStruct((T, SSM_WIDTH), bf16), jax.ShapeDtypeStruct((T, CONV_DIM), bf16),
            jax.ShapeDtypeStruct((T, LANES), f32), jax.ShapeDtypeStruct((T, G), bf16)]
    out_specs = [_rows(tm, D), _rows(tm, 2 * G), _rows(tm, 2 * G), _rows(tm, SSM_WIDTH), _rows(tm, CONV_DIM), _rows(tm, LANES),
                 _rows(tm, G)]
    return pl.pallas_call(body, name="mix_fwd", grid=(T // tm,), in_specs=in_specs, out_specs=out_specs,
                          out_shape=outs, scratch_shapes=[pltpu.VMEM((tm, G), f32)],
                          compiler_params=_params(("parallel",)))(*ins)


def _mix_bwd(dh, h1, gmix, act, slope, dya, dzxd, w_uv, w_zxd, ln_g, ln_b, w_s, b_st, gout, tm=256):
    T, D = dh.shape
    tm = min(tm, T)
    G = GM_WIDTH

    def body(dh_ref, h_ref, g_ref, act_ref, slope_ref, dya_ref, dzxd_ref, wuv_ref, wzxd_ref, lng_ref, lnb_ref, ws_ref,
             bst_ref, gout_ref, dhi_ref, duv_ref, dg_ref, dlng_ref, dlnb_ref, dws_ref, dbst_ref, dgout_ref, mixed_sc, dvn_sc):
        @pl.when(pl.program_id(0) == 0)
        def _():
            for r in (dg_ref, dlng_ref, dlnb_ref, dws_ref, dbst_ref, dgout_ref):
                r[...] = jnp.zeros_like(r)

        dn_z = _dot(dzxd_ref[...], wzxd_ref[...])
        ug = act_ref[:, :G].astype(f32)
        vn, ln_vjp = jax.vjp(_layernorm, act_ref[:, G:].astype(f32), lng_ref[...], lnb_ref[...])
        vnb = vn.astype(bf16)
        _gm_mix(vnb, ws_ref, bst_ref[...], mixed_sc, tm)
        mixed = mixed_sc[...]
        _, out_vjp = jax.vjp(_rms, ug * mixed, gout_ref[...])
        dpre, dgout = out_vjp(dya_ref[...].astype(f32))
        dgout_ref[...] += dgout
        dug = dpre * mixed
        dmixed = dpre * ug
        mask = _tril_mask()
        lane = lax.broadcasted_iota(jnp.int32, (1, GM_HEADS), 1)
        dbst = jnp.zeros((CHUNK, GM_HEADS), f32)
        for h in range(GM_HEADS):
            wt = jnp.where(mask, ws_ref[h], 0.0).astype(bf16)
            cs = slice(h * CHUNK, (h + 1) * CHUNK)
            dw = jnp.zeros((CHUNK, CHUNK), f32)
            for q in range(tm // CHUNK):
                rs = slice(q * CHUNK, (q + 1) * CHUNK)
                dm = dmixed[rs, cs]
                dmb = dm.astype(bf16)
                dw = dw + _dot_nt(dmb, vnb[rs, cs])
                dbst = dbst + jnp.sum(dm, axis=1, keepdims=True) * (lane == h).astype(f32)
                dvn_sc[rs, cs] = _dot_tn(wt, dmb)
            dws_ref[h] += jnp.where(mask, dw, 0.0)
        dbst_ref[...] += dbst
        dvg, dlng, dlnb = ln_vjp(dvn_sc[...])
        duv = (jnp.concatenate([dug, dvg], axis=1) * slope_ref[...].astype(f32)).astype(bf16)
        duv_ref[...] = duv
        dlng_ref[...] += dlng
        dlnb_ref[...] += dlnb
        dn = dn_z + _dot(duv, wuv_ref[...])
        _, vjp = jax.vjp(_rms, h_ref[...], g_ref[...])
        dx, dg = vjp(dn)
        dhi_ref[...] = dh_ref[...] + dx
        dg_ref[...] += dg

    ins = [dh, h1, gmix, act, slope, dya, dzxd, w_uv, w_zxd, ln_g, ln_b, w_s, b_st, gout]
    in_specs = ([_rows(tm, D), _rows(tm, D), _full(gmix.shape), _rows(tm, 2 * G), _rows(tm, 2 * G), _rows(tm, G),
                 _rows(tm, dzxd.shape[1])] + [_full(x.shape) for x in ins[7:]])
    accs = (gmix, ln_g, ln_b, w_s, b_st, gout)
    outs = ([jax.ShapeDtypeStruct((T, D), f32), jax.ShapeDtypeStruct((T, 2 * G), bf16)]
            + [jax.ShapeDtypeStruct(x.shape, f32) for x in accs])
    out_specs = [_rows(tm, D), _rows(tm, 2 * G)] + [_acc(x.shape) for x in accs]
    return pl.pallas_call(body, name="mix_bwd", grid=(T // tm,), in_specs=in_specs, out_specs=out_specs,
                          out_shape=outs, scratch_shapes=[pltpu.VMEM((tm, G), f32), pltpu.VMEM((tm, G), f32)],
                          compiler_params=_params(("arbitrary",)))(*ins)


HALO = 16
SSD_SUB = 4


class _RowsOf:
    def __init__(self, ref, rows):
        self.ref, self.rows = ref, rows

    def _index(self, idx):
        return (self.rows, slice(None)) if idx is Ellipsis else (self.rows,) + tuple(idx[1:])

    def __getitem__(self, idx):
        return self.ref[self._index(idx)]

    def __setitem__(self, idx, value):
        self.ref[self._index(idx)] = value
PAIRS = SSM_HEADS // 2
PAIR_W = 2 * SSM_HEAD_DIM


def _split(x, n):
    parts = []
    for _ in range(n):
        p = x.astype(bf16)
        parts.append(p)
        x = x - p.astype(f32)
    return parts


def _dot_sel(x, sel_n, n):
    return _dot(jnp.concatenate(_split(x, n), axis=1), sel_n)


def _sel_dot(sel, x, n):
    return _dot(jnp.concatenate([sel] * n, axis=1), jnp.concatenate(_split(x, n), axis=0))


EXPAND_SPLIT = 3
REDUCE_SPLIT = 2


def _head_mats():
    ex = (jnp.arange(SSM_WIDTH)[None, :] // SSM_HEAD_DIM == jnp.arange(LANES)[:, None]).astype(bf16)
    return jnp.tile(ex, (EXPAND_SPLIT, 1)), jnp.tile(ex.T, (REDUCE_SPLIT, 1))


def _shift_mat(rows, cols, off):
    r = lax.broadcasted_iota(jnp.int32, (rows, cols), 0)
    c = lax.broadcasted_iota(jnp.int32, (rows, cols), 1)
    return (c == r + off).astype(bf16)


def _ssd_conv(halo, x, cw_ref, cb_ref):
    ext = jnp.concatenate([halo, x], axis=0)
    xc = cb_ref[...] + cw_ref[SSM_CONV - 1:SSM_CONV, :] * x.astype(f32)
    for j in range(SSM_CONV - 1):
        xc = xc + cw_ref[j:j + 1, :] * _dot(_shift_mat(CHUNK, HALO + CHUNK, HALO - SSM_CONV + 1 + j), ext)
    return xc


def _ssd_front(dtr, dtb_ref, alog_ref):
    dt = _softplus(dtr + dtb_ref[...])
    a = -jnp.exp(alog_ref[...])
    acs = jnp.dot(_tril_mask().astype(f32), dt * a, preferred_element_type=f32, precision=HIGHEST)
    return dt, a, acs


def _ssd_wide(xa, dt, acs, dsk, ex):
    dt_x = _dot_sel(dt, ex, EXPAND_SPLIT)
    acs_x = _dot_sel(acs, ex, EXPAND_SPLIT)
    dsk_x = _dot_sel(jnp.broadcast_to(dsk, (8, LANES)), ex, EXPAND_SPLIT)[0:1]
    e_x = jnp.exp(acs_x)
    r_x = jnp.exp(acs_x[CHUNK - 1:CHUNK, :] - acs_x)
    xs = xa[:, :SSM_WIDTH]
    xd = xs * dt_x
    return dt_x, dsk_x, e_x, r_x, xs, xd, xd * r_x


def _pair_stack(v, lo):
    return jnp.concatenate([jnp.where(lo, v, 0.0), jnp.where(lo, 0.0, v)], axis=0)


def _ssd_pair(j, acs, acs_t, cb):
    out = []
    tril = _tril_mask()
    for h in (2 * j, 2 * j + 1):
        dk = jnp.exp(jnp.where(tril, acs[:, h:h + 1] - acs_t[h:h + 1, :], -jnp.inf))
        out.append((dk, cb * dk))
    return out


def _pair_col(row_lo, tot, j):
    return jnp.exp(jnp.where(row_lo, tot[:, 2 * j:2 * j + 1], tot[:, 2 * j + 1:2 * j + 2]))


def _gated_norm(y, z, g):
    yg = y * (z * _sigmoid(z))
    half = SSM_WIDTH // SSM_GROUPS
    parts = []
    for k in range(SSM_GROUPS):
        s = yg[:, k * half:(k + 1) * half]
        parts.append(s * lax.rsqrt(jnp.mean(s * s, axis=-1, keepdims=True) + EPS))
    return jnp.concatenate(parts, axis=1) * g


def _group_mats(xa):
    out = []
    for g in range(SSM_GROUPS):
        bm = xa[:, SSM_WIDTH + g * SSM_STATE:SSM_WIDTH + (g + 1) * SSM_STATE].astype(bf16)
        cm = xa[:, SSM_WIDTH + (SSM_GROUPS + g) * SSM_STATE:SSM_WIDTH + (SSM_GROUPS + g + 1) * SSM_STATE].astype(bf16)
        out.append((cm, bm, _dot_nt(cm, bm)))
    return out


def _ssd_fwd(xbc, z, dtr, conv_w, conv_b, dt_bias, a_log, d_skip, ssm_norm):
    T = xbc.shape[0]
    nc = T // CHUNK
    N = SSM_STATE

    def body(xbc_ref, halo_ref, z_ref, dtr_ref, cw_ref, cb_ref, dtb_ref, alog_ref, dsk_ref, g_ref, ex_ref,
             yb_ref, xc_ref, sg_ref, y_ref, sprev_ref, s_sc):
        i = pl.program_id(0)

        @pl.when(i == 0)
        def _():
            s_sc[...] = jnp.zeros_like(s_sc)

        lo = lax.broadcasted_iota(jnp.int32, (CHUNK, PAIR_W), 1) < SSM_HEAD_DIM
        row_lo = lax.broadcasted_iota(jnp.int32, (PAIR_W, 1), 0) < SSM_HEAD_DIM
        for k in range(SUB):
            rs = slice(k * CHUNK, (k + 1) * CHUNK)
            if k == 0:
                halo = halo_ref[...]
                halo = jnp.where(i > 0, halo, jnp.zeros_like(halo))
            else:
                halo = xbc_ref[k * CHUNK - HALO:k * CHUNK, :]
            xc = _ssd_conv(halo, xbc_ref[rs, :], cw_ref, cb_ref)
            sg = _sigmoid(xc)
            xc_ref[rs, :] = xc
            sg_ref[rs, :] = sg
            xa = xc * sg
            dt, _, acs = _ssd_front(dtr_ref[rs, :], dtb_ref, alog_ref)
            _, dsk_x, e_x, _, xs, xd, gm = _ssd_wide(xa, dt, acs, dsk_ref[...], ex_ref[...])
            acs_t = acs.T
            tot = acs[CHUNK - 1:CHUNK, :]
            groups = _group_mats(xa)
            ys = []
            for j in range(PAIRS):
                cmb, bmb, cb = groups[j // (PAIRS // SSM_GROUPS)]
                ps = slice(j * PAIR_W, (j + 1) * PAIR_W)
                (_, m0), (_, m1) = _ssd_pair(j, acs, acs_t, cb)
                sp = s_sc[j]
                yd = _dot(jnp.concatenate([m0, m1], axis=1).astype(bf16), _pair_stack(xd[:, ps], lo).astype(bf16))
                ys.append(yd + e_x[:, ps] * _dot_nt(cmb, sp.astype(bf16)))
                sprev_ref[k, j] = sp
                s_sc[j] = _pair_col(row_lo, tot, j) * sp + _dot_tn(gm[:, ps].astype(bf16), bmb)
            y = jnp.concatenate(ys, axis=1) + xs * dsk_x
            y_ref[rs, :] = y
            yb_ref[rs, :] = _gated_norm(y, z_ref[rs, :].astype(f32), g_ref[...]).astype(bf16)

    params = [conv_w, conv_b, dt_bias, a_log, d_skip, ssm_norm, _head_mats()[0]]
    SUB = SSD_SUB if nc % SSD_SUB == 0 else 1
    hp = SUB * CHUNK // HALO
    R = SUB * CHUNK
    in_specs = [_rows(R, CONV_DIM), pl.BlockSpec((HALO, CONV_DIM), lambda i: (jnp.maximum(i * hp - 1, 0), 0)),
                _rows(R, SSM_WIDTH), _rows(R, LANES)] + [_full(x.shape) for x in params]
    return pl.pallas_call(
        body, name="ssd_fwd", grid=(nc // SUB,), in_specs=in_specs,
        out_specs=[_rows(R, SSM_WIDTH), _rows(R, CONV_DIM), _rows(R, CONV_DIM), _rows(R, SSM_WIDTH),
                   pl.BlockSpec((SUB, PAIRS, PAIR_W, N), lambda i: (i, 0, 0, 0))],
        out_shape=[jax.ShapeDtypeStruct((T, SSM_WIDTH), bf16), jax.ShapeDtypeStruct((T, CONV_DIM), f32),
                   jax.ShapeDtypeStruct((T, CONV_DIM), f32), jax.ShapeDtypeStruct((T, SSM_WIDTH), f32),
                   jax.ShapeDtypeStruct((nc, PAIRS, PAIR_W, N), f32)],
        scratch_shapes=[pltpu.VMEM((PAIRS, PAIR_W, N), f32)],
        compiler_params=_params(("arbitrary",)))(xbc, xbc, z, dtr, *params)


def _ssd_bwd(xbc, xc, sg, y, z, dtr, sprev, dyb, conv_w, conv_b, dt_bias, a_log, d_skip, ssm_norm):
    T = xbc.shape[0]
    nc = T // CHUNK
    H, N = SSM_HEADS, SSM_STATE
    PG = PAIRS // SSM_GROUPS

    def chunk(xbc_ref, xc_ref, sg_ref, y_ref, z_ref, dtr_ref, sprev_k, dyb_ref, cw_ref, cb_ref, dtb_ref, alog_ref, dsk_ref,
              g_ref, ex_ref, rd_ref, dzxd_ref, dcw_ref, dcb_ref, ddtb_ref, dalog_ref, ddsk_ref, dg_ref, ds_sc, next_sc):
        xc = xc_ref[...]
        sg = sg_ref[...]
        xa = xc * sg
        dt, a, acs = _ssd_front(dtr_ref[...], dtb_ref, alog_ref)
        dt_x, dsk_x, e_x, r_x, xs, xd, gm = _ssd_wide(xa, dt, acs, dsk_ref[...], ex_ref[...])
        acs_t = acs.T
        tot = acs[CHUNK - 1:CHUNK, :]
        groups = _group_mats(xa)
        lo = lax.broadcasted_iota(jnp.int32, (CHUNK, PAIR_W), 1) < SSM_HEAD_DIM
        row_lo = lax.broadcasted_iota(jnp.int32, (PAIR_W, 1), 0) < SSM_HEAD_DIM
        pairs, zs = [], []
        for j in range(PAIRS):
            cmb, _, cb = groups[j // PG]
            pairs.append(_ssd_pair(j, acs, acs_t, cb))
            zs.append(_dot_nt(cmb, sprev_k[j].astype(bf16)))
        zf = jnp.concatenate(zs, axis=1)
        _, gn_vjp = jax.vjp(_gated_norm, y_ref[...], z_ref[...].astype(f32), g_ref[...])
        dy, dz, dg = gn_vjp(dyb_ref[...].astype(f32))
        dg_ref[...] += dg
        dzxd_ref[:, :SSM_WIDTH] = dz.astype(bf16)

        lane = lax.broadcasted_iota(jnp.int32, (1, LANES), 1)
        sub = lax.broadcasted_iota(jnp.int32, (LANES, 1), 0)
        dacs = jnp.zeros((CHUNK, LANES), f32)
        dacs_r = jnp.zeros((LANES, CHUNK), f32)
        dtot = jnp.zeros((1, LANES), f32)
        dcb = [jnp.zeros((CHUNK, CHUNK), f32) for _ in range(SSM_GROUPS)]
        dcm = [jnp.zeros((CHUNK, N), f32) for _ in range(SSM_GROUPS)]
        dbm = [jnp.zeros((CHUNK, N), f32) for _ in range(SSM_GROUPS)]
        dxds, dgms = [], []
        for j in range(PAIRS):
            g = j // PG
            cmb, bmb, _ = groups[g]
            ps = slice(j * PAIR_W, (j + 1) * PAIR_W)
            (dk0, m0), (dk1, m1) = pairs[j]
            oh0, oh1 = (lane == 2 * j).astype(f32), (lane == 2 * j + 1).astype(f32)
            dyp = dy[:, ps]
            dy2 = _pair_stack(dyp, lo).astype(bf16)
            dm2 = _dot_nt(dy2, xd[:, ps].astype(bf16))
            m2 = jnp.concatenate([m0, m1], axis=0)
            dxds.append(_dot_tn(m2.astype(bf16), dy2))
            w2 = dm2 * m2
            rs = jnp.sum(w2, axis=1, keepdims=True)
            dacs = dacs + rs[:CHUNK] * oh0 + rs[CHUNK:] * oh1
            dacs_r = dacs_r - ((sub == 2 * j).astype(f32) * jnp.sum(w2[:CHUNK], axis=0, keepdims=True)
                               + (sub == 2 * j + 1).astype(f32) * jnp.sum(w2[CHUNK:], axis=0, keepdims=True))
            dcb[g] = dcb[g] + dm2[:CHUNK] * dk0 + dm2[CHUNK:] * dk1
            sp = sprev_k[j]
            dzb = (dyp * e_x[:, ps]).astype(bf16)
            dcm[g] = dcm[g] + _dot(dzb, sp.astype(bf16))
            dsn = ds_sc[j]
            dsnb = dsn.astype(bf16)
            et = _pair_col(row_lo, tot, j)
            rr = jnp.sum(dsn * sp, axis=1, keepdims=True) * et
            dtot = dtot + jnp.sum(rr[:SSM_HEAD_DIM]) * oh0 + jnp.sum(rr[SSM_HEAD_DIM:]) * oh1
            dgms.append(_dot_nt(bmb, dsnb))
            dbm[g] = dbm[g] + _dot(gm[:, ps].astype(bf16), dsnb)
            ds_sc[j] = _dot_tn(dzb, cmb) + et * dsn
        dgm = jnp.concatenate(dgms, axis=1)
        dxd = jnp.concatenate(dxds, axis=1) + dgm * r_x
        dr = dgm * gm
        red = _dot_sel(jnp.concatenate([dy * e_x * zf - dr, dr, dxd * xs, dy * xs], axis=0), rd_ref[...], REDUCE_SPLIT)
        rowi = lax.broadcasted_iota(jnp.int32, (CHUNK, 1), 0)
        dtot = dtot + jnp.sum(red[CHUNK:2 * CHUNK], axis=0, keepdims=True)
        dacs = dacs + red[:CHUNK] + dacs_r.T + jnp.where(rowi == CHUNK - 1, dtot, 0.0)
        r2 = lax.broadcasted_iota(jnp.int32, (CHUNK, CHUNK), 0)
        c2 = lax.broadcasted_iota(jnp.int32, (CHUNK, CHUNK), 1)
        dadt = jnp.dot((c2 >= r2).astype(f32), dacs, preferred_element_type=f32, precision=HIGHEST)
        ddt = red[2 * CHUNK:3 * CHUNK] + dadt * a
        dalog_ref[...] += jnp.sum(dadt * dt, axis=0, keepdims=True) * a
        ddsk_ref[...] += jnp.sum(red[3 * CHUNK:], axis=0, keepdims=True)
        ddtr = jnp.where(lane < H, ddt * _sigmoid(dtr_ref[...] + dtb_ref[...]), 0.0)
        ddtb_ref[...] += jnp.sum(ddtr, axis=0, keepdims=True)
        dzxd_ref[:, SSM_WIDTH + CONV_DIM:] = ddtr.astype(bf16)
        dxa_bm, dxa_cm = [], []
        for g in range(SSM_GROUPS):
            cmb, bmb, _ = groups[g]
            dcbb = dcb[g].astype(bf16)
            dxa_bm.append(dbm[g] + _dot_tn(dcbb, cmb))
            dxa_cm.append(dcm[g] + _dot(dcbb, bmb))
        dxc = jnp.concatenate([dy * dsk_x + dxd * dt_x] + dxa_bm + dxa_cm, axis=1) * (sg * (1.0 + xc * (1.0 - sg)))
        ext = jnp.concatenate([dxc, next_sc[...]], axis=0)
        xin = xbc_ref[...].astype(f32)
        dxbc = cw_ref[SSM_CONV - 1:SSM_CONV, :] * dxc
        dcw = [jnp.sum(dxc * xin, axis=0, keepdims=True)]
        for s in range(1, SSM_CONV):
            later = _sel_dot(_shift_mat(CHUNK, CHUNK + HALO, s), ext, 2)
            dxbc = dxbc + cw_ref[SSM_CONV - 1 - s:SSM_CONV - s, :] * later
            dcw.insert(0, jnp.sum(later * xin, axis=0, keepdims=True))
        dzxd_ref[:, SSM_WIDTH:SSM_WIDTH + CONV_DIM] = dxbc.astype(bf16)
        dcw_ref[...] += jnp.concatenate(dcw, axis=0)
        dcb_ref[...] += jnp.sum(dxc, axis=0, keepdims=True)
        next_sc[...] = dxc[0:HALO, :]

    SUB = SSD_SUB if nc % SSD_SUB == 0 else 1
    nb = nc // SUB

    def body(xbc_ref, xc_ref, sg_ref, y_ref, z_ref, dtr_ref, sprev_ref, dyb_ref, cw_ref, cb_ref, dtb_ref, alog_ref, dsk_ref,
             g_ref, ex_ref, rd_ref, dzxd_ref, dcw_ref, dcb_ref, ddtb_ref, dalog_ref, ddsk_ref, dg_ref, ds_sc, next_sc):
        @pl.when(pl.program_id(0) == 0)
        def _():
            ds_sc[...] = jnp.zeros_like(ds_sc)
            next_sc[...] = jnp.zeros_like(next_sc)
            for r_ in (dcw_ref, dcb_ref, ddtb_ref, dalog_ref, ddsk_ref, dg_ref):
                r_[...] = jnp.zeros_like(r_)

        for k in reversed(range(SUB)):
            rows = slice(k * CHUNK, (k + 1) * CHUNK)
            tok = [_RowsOf(r_, rows) for r_ in (xbc_ref, xc_ref, sg_ref, y_ref, z_ref, dtr_ref)]
            chunk(*tok, sprev_ref.at[k], _RowsOf(dyb_ref, rows), cw_ref, cb_ref, dtb_ref, alog_ref, dsk_ref, g_ref, ex_ref,
                  rd_ref, _RowsOf(dzxd_ref, rows), dcw_ref, dcb_ref, ddtb_ref, dalog_ref, ddsk_ref, dg_ref, ds_sc, next_sc)

    params = [conv_w, conv_b, dt_bias, a_log, d_skip, ssm_norm]
    mats = list(_head_mats())

    def rev(ncols):
        return pl.BlockSpec((SUB * CHUNK, ncols), lambda i: (nb - 1 - i, 0))

    in_specs = ([rev(CONV_DIM), rev(CONV_DIM), rev(CONV_DIM), rev(SSM_WIDTH), rev(SSM_WIDTH), rev(LANES),
                 pl.BlockSpec((SUB, PAIRS, PAIR_W, N), lambda i: (nb - 1 - i, 0, 0, 0)), rev(SSM_WIDTH)]
                + [_full(x.shape) for x in params + mats])
    return pl.pallas_call(
        body, name="ssd_bwd", grid=(nb,), in_specs=in_specs,
        out_specs=[rev(ZXD)] + [_acc(x.shape) for x in params],
        out_shape=[jax.ShapeDtypeStruct((T, ZXD), bf16)] + [jax.ShapeDtypeStruct(x.shape, f32) for x in params],
        scratch_shapes=[pltpu.VMEM((PAIRS, PAIR_W, N), f32), pltpu.VMEM((HALO, CONV_DIM), f32)],
        compiler_params=_params(("arbitrary",)))(xbc, xc, sg, y, z, dtr, sprev, dyb, *params, *mats)


def _tail(h3, p, tgt, gp, wpg, bpg, wpp, gf, tm=512):
    T, D = h3.shape
    tm = min(tm, T)

    def head(gpre, pp, h, gf_, t):
        gate = _sigmoid(gpre)
        y = _rms(h + gate * pp, gf_)
        err = y - t
        return 0.5 * jnp.sum(jnp.mean(err * err, axis=-1))

    def body(h_ref, p_ref, t_ref, gp_ref, wpg_ref, bpg_ref, wpp_ref, gf_ref,
             dh_ref, loss_ref, dgp_ref, dwpg_ref, dbpg_ref, dwpp_ref, dgf_ref):
        @pl.when(pl.program_id(0) == 0)
        def _():
            for r in (loss_ref, dgp_ref, dwpg_ref, dbpg_ref, dwpp_ref, dgf_ref):
                r[...] = jnp.zeros_like(r)

        h = h_ref[...]
        npf, np_vjp = jax.vjp(_rms, h, gp_ref[...])
        npb = npf.astype(bf16)
        pb = p_ref[...].astype(bf16)
        gpre = _dot(npb, wpg_ref[...]) + bpg_ref[...]
        kp, _, cp = wpp_ref.shape
        pp = jnp.concatenate([_dot(pb, wpp_ref[k]) for k in range(kp)], axis=1)
        loss, head_vjp = jax.vjp(head, gpre, pp, h, gf_ref[...], t_ref[...])
        dgpre, dpp, dh_a, dgf, _ = head_vjp(jnp.ones((), f32))
        loss_ref[...] += loss
        dgf_ref[...] += dgf
        dbpg_ref[...] += jnp.sum(dgpre, axis=0, keepdims=True)
        dgb = dgpre.astype(bf16)
        dwpg_ref[...] += _dot_tn(npb, dgb)
        dppb = dpp.astype(bf16)
        for k in range(kp):
            dwpp_ref[k] += _dot_tn(pb, dppb[:, k * cp:(k + 1) * cp])
        dh_b, dgp = np_vjp(_dot_nt(dgb, wpg_ref[...]))
        dgp_ref[...] += dgp
        dh_ref[...] = dh_a + dh_b

    ins = [h3, p, tgt, gp, wpg, bpg, wpp, gf]
    in_specs = [_rows(tm, D), _rows(tm, p.shape[1]), _rows(tm, D)] + [_full(x.shape) for x in ins[3:]]
    acc_shapes = [(1, LANES), gp.shape, wpg.shape, bpg.shape, wpp.shape, gf.shape]
    return pl.pallas_call(
        body, name="tail", grid=(T // tm,), in_specs=in_specs,
        out_specs=[_rows(tm, D)] + [_acc(s) for s in acc_shapes],
        out_shape=[jax.ShapeDtypeStruct((T, D), f32)] + [jax.ShapeDtypeStruct(s, f32) for s in acc_shapes],
        compiler_params=_params(("arbitrary",)))(*ins)


def _adamw(name, w, g, m, v, tr=256):
    R, rest = w.shape[0], w.shape[1:]
    tr = _row_tile(R, tr, 8 if len(rest) == 1 else 1)

    def body(w_ref, g_ref, m_ref, v_ref, d_ref, mo_ref, vo_ref):
        g_ = g_ref[...]
        m_ = ADAM_B1 * m_ref[...] + (1.0 - ADAM_B1) * g_
        v_ = ADAM_B2 * v_ref[...] + (1.0 - ADAM_B2) * jnp.square(g_)
        m_hat = m_ / (1.0 - ADAM_B1 ** ADAM_STEP)
        v_hat = v_ / (1.0 - ADAM_B2 ** ADAM_STEP)
        d_ref[...] = -ADAM_LR * (m_hat / (jnp.sqrt(v_hat) + ADAM_EPS) + ADAM_WD * w_ref[...])
        mo_ref[...] = m_
        vo_ref[...] = v_

    spec = pl.BlockSpec((tr,) + rest, lambda i: (i,) + (0,) * len(rest))
    return pl.pallas_call(body, name=name, grid=(R // tr,), in_specs=[spec] * 4, out_specs=[spec] * 3,
                          out_shape=[jax.ShapeDtypeStruct(w.shape, f32)] * 3,
                          compiler_params=_params(("parallel",)))(w, g, m, v)


HBM = pl.BlockSpec(memory_space=pltpu.HBM)


def _me():
    return lax.axis_index("x"), lax.axis_index("y"), lax.axis_index("c")


def _other_chips(x, y):
    return [(1 - x, y), (x, 1 - y), (1 - x, 1 - y)]


def _remote(src, dst, send_sem, recv_sem, dev):
    return pltpu.make_async_remote_copy(src_ref=src, dst_ref=dst, send_sem=send_sem, recv_sem=recv_sem,
                                        device_id=dev, device_id_type=MESH)


def _sems(n):
    return [pltpu.SemaphoreType.DMA((n,)), pltpu.SemaphoreType.DMA((n,))]


def _gather_weights(shards, split):
    n = len(shards)

    def body(*refs):
        ins, outs = refs[:n], refs[n:2 * n]
        own_send, own_recv, ici_send, ici_recv, d2d_send, d2d_recv, pass_send, pass_recv = refs[2 * n:]
        x, y, c = _me()
        my_chip = 2 * x + y
        sibling = (x, y, 1 - c)
        chips = _other_chips(x, y)

        def rows(i, half):
            hr = shards[i].shape[0] // 2
            return pl.ds(half * hr, hr) if split[i] else pl.ds(0, shards[i].shape[0])

        def piece(i, half, k):
            q = shards[i].shape[0] // 4
            return pl.ds((2 * half + k) * q, q)

        sends = []
        for i in range(n):
            for j, chip in enumerate(chips):
                if split[i] and j == 2:
                    continue
                cp = _remote(ins[i].at[rows(i, c)], outs[i].at[my_chip, rows(i, c)],
                             ici_send.at[3 * i + j], ici_recv.at[3 * i + j], (*chip, c))
                cp.start()
                sends.append(cp)
            cp = _remote(ins[i], outs[i].at[my_chip], own_send.at[i], own_recv.at[i], sibling)
            cp.start()
            sends.append(cp)
        def to_sibling(i, j):
            chip = chips[j]
            land = outs[i].at[2 * chip[0] + chip[1], rows(i, c)]
            cp = _remote(land, land, d2d_send.at[3 * i + j], d2d_recv.at[3 * i + j], sibling)
            cp.start()
            sends.append(cp)

        for i in range(n):
            for j, chip in enumerate(chips):
                if split[i] and j == 2:
                    continue
                s = 3 * i + j
                slab = 2 * chip[0] + chip[1]
                land = outs[i].at[slab, rows(i, c)]
                _remote(land, land, ici_send.at[s], ici_recv.at[s], (*chip, c)).wait_recv()
                if split[i]:
                    part = outs[i].at[slab, piece(i, c, j)]
                    cp = _remote(part, part, pass_send.at[2 * i + j], pass_recv.at[2 * i + j], (*chips[1 - j], c))
                    cp.start()
                    sends.append(cp)
                    to_sibling(i, j)
        for i in range(n):
            if split[i]:
                slab = 2 * chips[2][0] + chips[2][1]
                for k in range(2):
                    part = outs[i].at[slab, piece(i, c, k)]
                    _remote(part, part, pass_send.at[2 * i + k], pass_recv.at[2 * i + k], (*chips[1 - k], c)).wait_recv()
                to_sibling(i, 2)
        for i in range(n):
            _remote(ins[i], outs[i].at[my_chip], own_send.at[i], own_recv.at[i], sibling).wait_recv()
            if split[i]:
                for j, chip in enumerate(chips):
                    s = 3 * i + j
                    land = outs[i].at[2 * chip[0] + chip[1], rows(i, 1 - c)]
                    _remote(land, land, d2d_send.at[s], d2d_recv.at[s], sibling).wait_recv()
        for cp in sends:
            cp.wait_send()

    return pl.pallas_call(
        body, name="gather_weights", out_shape=[jax.ShapeDtypeStruct((N_CHIPS,) + s.shape, s.dtype) for s in shards],
        in_specs=[HBM] * n, out_specs=[HBM] * n,
        scratch_shapes=_sems(n) + _sems(3 * n) + _sems(3 * n) + _sems(2 * n))(*shards)


def _swap_halves(name, grads):
    n = len(grads)

    def body(*refs):
        ins, outs, send, recv = refs[:n], refs[n:2 * n], refs[2 * n], refs[2 * n + 1]
        x, y, c = _me()
        copies = []
        for i in range(n):
            hr = grads[i].shape[1] // 2
            cp = _remote(ins[i].at[:, pl.ds((1 - c) * hr, hr), :], outs[i], send.at[i], recv.at[i], (x, y, 1 - c))
            cp.start()
            copies.append(cp)
        for cp in copies:
            cp.wait()

    return pl.pallas_call(
        body, name=name,
        out_shape=[jax.ShapeDtypeStruct((g.shape[0], g.shape[1] // 2, g.shape[2]), g.dtype) for g in grads],
        in_specs=[HBM] * n, out_specs=[HBM] * n, scratch_shapes=_sems(n))(*grads)


def _add_halves(name, grads, other, c_idx, th=HALF_ROWS_BF16):
    K, R, C = grads.shape
    H = R // 2
    th = _row_tile(H, th, 16)
    nb = H // th

    def body(c_ref, g_ref, o_ref, out_ref):
        out_ref[...] = (g_ref[...].astype(f32) + o_ref[...].astype(f32)).astype(bf16)

    grid_spec = pltpu.PrefetchScalarGridSpec(
        num_scalar_prefetch=1, grid=(nb,),
        in_specs=[pl.BlockSpec((K, th, C), lambda i, c: (0, c[0] * nb + i, 0)),
                  pl.BlockSpec((K, th, C), lambda i, c: (0, i, 0))],
        out_specs=pl.BlockSpec((K, th, C), lambda i, c: (0, i, 0)))
    return pl.pallas_call(body, name=name, grid_spec=grid_spec,
                          out_shape=jax.ShapeDtypeStruct((K, H, C), bf16),
                          compiler_params=_params(("parallel",)))(c_idx, grads, other)


SEM = pl.BlockSpec(memory_space=pltpu.SEMAPHORE)
ANY = pl.BlockSpec(memory_space=pl.ANY)
EFFECT = pltpu.SideEffectType.DATAFLOW_SIDE_EFFECTING


def _copies_start(name, srcs, land_shapes, n_copies, make_copies, after):
    ns, nl = len(srcs), len(land_shapes)
    lands = [lax.empty(s.shape, s.dtype) for s in land_shapes]

    def body(*refs):
        src_refs, land_refs = refs[:ns], refs[ns:ns + nl]
        send, recv, token = refs[ns + nl + 1], refs[ns + nl + 2], refs[-1]
        for cp in make_copies(src_refs, land_refs, send, recv):
            cp.start()
        token[...] = jnp.zeros_like(token)

    buffers = list(srcs) + lands
    out = pl.pallas_call(
        body, name=name,
        out_shape=(pltpu.SemaphoreType.DMA((n_copies,)), pltpu.SemaphoreType.DMA((n_copies,)),
                   *[pltpu.HBM(b.shape, b.dtype) for b in buffers], jax.ShapeDtypeStruct((8, LANES), f32)),
        in_specs=[HBM] * (ns + nl) + [ANY],
        out_specs=(SEM, SEM, *[HBM] * (ns + nl), pl.BlockSpec(memory_space=pltpu.VMEM)),
        input_output_aliases={i: 2 + i for i in range(ns + nl)},
        compiler_params=pltpu.CompilerParams(has_side_effects=EFFECT),
    )(*[pltpu.with_memory_space_constraint(b, pltpu.HBM) for b in buffers], after)
    return out[0], out[1], list(out[2:2 + ns]), list(out[2 + ns:2 + ns + nl]), out[-1]


def _copies_wait(name, started, make_copies, after):
    send, recv, srcs, lands, _ = started
    ns, nl = len(srcs), len(lands)
    after = list(after)

    def body(*refs):
        src_refs, land_refs = refs[:ns], refs[ns:ns + nl]
        for cp in make_copies(src_refs, land_refs, refs[ns + nl], refs[ns + nl + 1]):
            cp.wait_send()
            cp.wait_recv()

    buffers = list(srcs) + list(lands)
    out = pl.pallas_call(
        body, name=name, out_shape=tuple(pltpu.HBM(b.shape, b.dtype) for b in buffers),
        in_specs=[HBM] * (ns + nl) + [SEM, SEM] + [ANY] * len(after), out_specs=tuple([HBM] * (ns + nl)),
        input_output_aliases={i: i for i in range(ns + nl)},
        compiler_params=pltpu.CompilerParams(has_side_effects=EFFECT),
    )(*buffers, send, recv, *after)
    return list(out[:ns]), list(out[ns:])


def _gather_copies(src_refs, land_refs, send, recv):
    x, y, c = _me()
    my_chip = 2 * x + y
    peers = [(*chip, c) for chip in _other_chips(x, y)] + [(x, y, 1 - c)]
    return [_remote(src_refs[i], land_refs[i].at[my_chip], send.at[4 * i + j], recv.at[4 * i + j], peer)
            for i in range(len(src_refs)) for j, peer in enumerate(peers)]


def _swap_copies(src_refs, land_refs, send, recv):
    x, y, c = _me()
    copies = []
    for i in range(len(src_refs)):
        hr = src_refs[i].shape[1] // 2
        copies.append(_remote(src_refs[i].at[:, pl.ds((1 - c) * hr, hr), :], land_refs[i], send.at[i], recv.at[i], (x, y, 1 - c)))
    return copies


def _share_copies(src_refs, land_refs, send, recv):
    x, y, c = _me()
    return [_remote(src_refs[i], land_refs[i], send.at[i], recv.at[i], (x, y, 1 - c)) for i in range(len(src_refs))]


def _partial_copies(src_refs, land_refs, send, recv):
    x, y, c = _me()
    return [_remote(src_refs[i].at[2 * chip[0] + chip[1]], land_refs[i].at[j], send.at[3 * i + j], recv.at[3 * i + j], (*chip, c))
            for i in range(len(src_refs)) for j, chip in enumerate(_other_chips(x, y))]


def _small_copies(src_refs, land_refs, send, recv):
    x, y, c = _me()
    return [_remote(src_refs[0], land_refs[0].at[k - 1], send.at[k - 1], recv.at[k - 1], (x ^ (k >> 2), y ^ ((k >> 1) & 1), c ^ (k & 1)))
            for k in range(1, N_DEV)]


def _sum_small(own, slots, dev_idx):
    R, C = own.shape

    def body(dev_ref, own_ref, s_ref, o_ref):
        me = dev_ref[0]
        acc = jnp.zeros((R, C), f32)
        for d in range(N_DEV):
            k = me ^ d
            acc = acc + jnp.where(k == 0, own_ref[...], s_ref[jnp.maximum(k - 1, 0)])
        o_ref[...] = acc

    grid_spec = pltpu.PrefetchScalarGridSpec(
        num_scalar_prefetch=1, grid=(1,),
        in_specs=[pl.BlockSpec((R, C), lambda i, dev: (0, 0)), pl.BlockSpec((N_DEV - 1, R, C), lambda i, dev: (0, 0, 0))],
        out_specs=pl.BlockSpec((R, C), lambda i, dev: (0, 0)))
    return pl.pallas_call(body, name="sum_small", grid_spec=grid_spec, out_shape=jax.ShapeDtypeStruct((R, C), f32),
                          compiler_params=_params(("arbitrary",)))(dev_idx, own, slots)


def _sum_partials(name, part, recv, chip_idx, th=HALF_ROWS_BF16):
    K, H, C = part.shape
    th = _row_tile(H, th, 16)

    def body(chip_ref, p_ref, r_ref, o_ref):
        acc = p_ref[...].astype(f32)
        for j in range(3):
            acc = acc + r_ref[j].astype(f32)
        o_ref[...] = acc

    grid_spec = pltpu.PrefetchScalarGridSpec(
        num_scalar_prefetch=1, grid=(H // th,),
        in_specs=[pl.BlockSpec((None, th, C), lambda i, chip: (chip[0], i, 0)),
                  pl.BlockSpec((3, th, C), lambda i, chip: (0, i, 0))],
        out_specs=pl.BlockSpec((th, C), lambda i, chip: (i, 0)))
    return pl.pallas_call(body, name=name, grid_spec=grid_spec, out_shape=jax.ShapeDtypeStruct((H, C), f32),
                          compiler_params=_params(("parallel",)))(chip_idx, part, recv)


def _adamw_big(name, w, g_mine, g_theirs, m, v, c_idx, tr=HALF_ROWS_F32):
    R, C = w.shape
    H = R // 2
    tr = _row_tile(H, tr)
    nb = H // tr

    def body(c_ref, w_ref, gm_ref, gt_ref, m_ref, v_ref, g_ref, d_ref, mo_ref, vo_ref):
        g_ = jnp.where(pl.program_id(0) // nb == c_ref[0], gm_ref[...], gt_ref[...])
        g_ref[...] = g_
        m_ = ADAM_B1 * m_ref[...] + (1.0 - ADAM_B1) * g_
        v_ = ADAM_B2 * v_ref[...] + (1.0 - ADAM_B2) * jnp.square(g_)
        m_hat = m_ / (1.0 - ADAM_B1 ** ADAM_STEP)
        v_hat = v_ / (1.0 - ADAM_B2 ** ADAM_STEP)
        d_ref[...] = -ADAM_LR * (m_hat / (jnp.sqrt(v_hat) + ADAM_EPS) + ADAM_WD * w_ref[...])
        mo_ref[...] = m_
        vo_ref[...] = v_

    full = pl.BlockSpec((tr, C), lambda i, c: (i, 0))
    half = pl.BlockSpec((tr, C), lambda i, c: (i % nb, 0))
    grid_spec = pltpu.PrefetchScalarGridSpec(num_scalar_prefetch=1, grid=(2 * nb,),
                                             in_specs=[full, half, half, full, full], out_specs=[full] * 4)
    return pl.pallas_call(body, name=name, grid_spec=grid_spec, out_shape=[jax.ShapeDtypeStruct((R, C), f32)] * 4,
                          compiler_params=_params(("parallel",)))(c_idx, w, g_mine, g_theirs, m, v)


BIG = ("ffn1_w_gate", "ffn1_w_up", "ffn1_w_down", "w_in", "w_out", "ffn2_w_gate", "ffn2_w_up", "ffn2_w_down",
       "ple_w_gate", "ple_w_proj")


SMALL = ("ffn1_norm", "mix_norm", "gm_ln_g", "gm_ln_b", "gm_w_s", "gm_b_s", "gm_out_norm", "conv_b", "dt_bias", "a_log",
         "d_skip", "ssm_norm", "ffn2_norm", "ple_norm", "ple_b_gate", "final_norm")
SMALL_C = 1024


def _pack_small(vals):
    parts = []
    for v in vals:
        f = v.astype(f32).reshape(-1)
        parts.append(jnp.pad(f, (0, -f.shape[0] % SMALL_C)))
    flat = jnp.concatenate(parts)
    rows = flat.shape[0] // SMALL_C
    return jnp.pad(flat, (0, (-rows % 8) * SMALL_C)).reshape(-1, SMALL_C)


def _unpack_small(pack, shapes):
    flat = pack.reshape(-1)
    out, off = [], 0
    for s in shapes:
        n = 1
        for d in s:
            n *= d
        out.append(flat[off:off + n].reshape(s))
        off += n + (-n % SMALL_C)
    return out


def _pad_lanes(v):
    return jnp.pad(v, ((0, 0), (0, LANES - v.shape[1])))


def _pad_rows(a):
    pad = [(0, 0)] * a.ndim
    pad[-2] = (0, -a.shape[-2] % ROW_PAD)
    return jnp.pad(a, pad) if pad[-2][1] else a


FETCH = (("ffn1_w_gate", "ffn1_w_up", "ffn1_w_down"), ("w_in", "conv_w", "w_out"),
         ("ffn2_w_gate", "ffn2_w_up", "ffn2_w_down", "ple_w_gate", "ple_w_proj"))
TRANSPOSED = ("ffn1_w_gate", "ffn1_w_up", "ffn2_w_gate", "ffn2_w_up", "w_in")
ROW_PAD = 32
DONE = (("ffn2_w_gate", "ffn2_w_up", "ffn2_w_down", "w_out", "ple_w_gate", "ple_w_proj"), ("w_in",),
        ("ffn1_w_gate", "ffn1_w_up", "ffn1_w_down"))


def _local_step(x, p, tgt, fetch, S, on_grads, on_later):
    G = GM_WIDTH
    K = N_CHIPS
    b_st = S["gm_b_s"][0].T
    w_s = S["gm_w_s"][0]
    dtb, alog, dsk = _pad_lanes(S["dt_bias"]), _pad_lanes(S["a_log"]), _pad_lanes(S["d_skip"])
    gfin = S["final_norm"].reshape(1, -1)

    def rows(a):
        return a.reshape(-1, D_MODEL)

    def shards(a):
        return a.reshape(K, -1, D_MODEL)

    wg1, wu1, wd1 = [rows(a) for a in fetch(0, None)]
    h1, n1, a1, b1 = _ffn_fwd("ffn1_fwd", x, S["ffn1_norm"], wg1, wu1, wd1)
    w_in4, cw4, wo4 = fetch(1, h1)
    w_in = w_in4.reshape(IN_PROJ, D_MODEL)
    w_uv = w_in[:2 * G]
    w_zxd = jnp.pad(w_in[2 * G:], ((0, ZXD - (IN_PROJ - 2 * G)), (0, 0)))
    conv_w = jnp.transpose(cw4, (1, 0, 2)).reshape(SSM_CONV, CONV_DIM)
    wo = wo4.reshape(-1, D_MODEL)
    n2, act, slope, z, xbc, dtr, ya = _mix_fwd(h1, S["mix_norm"], w_uv, w_zxd, S["gm_ln_g"], S["gm_ln_b"], w_s, b_st,
                                               S["gm_out_norm"])
    yb, xc, sg, y_ssd, sprev = _ssd_fwd(xbc, z, dtr, conv_w, S["conv_b"], dtb, alog, dsk, S["ssm_norm"])
    wg2, wu2, wd2, wpg4, wpp4 = fetch(2, yb)
    wg2, wu2, wd2 = rows(wg2), rows(wu2), rows(wd2)
    h2, h3, n3, a2, b2 = _ffn_fwd("ffn2_fwd", h1, S["ffn2_norm"], wg2, wu2, wd2, pre=(ya, yb, wo))
    dh3, loss, dgp, dwpg, dbpg, dwpp, dgf = _tail(h3, p, tgt, S["ple_norm"], wpg4.reshape(-1, D_MODEL), S["ple_b_gate"], wpp4, gfin)
    dh2, da2, db2, hm2, dg_ffn2, dya, dyb = _ffn_bwd("ffn2_bwd", dh3, h2, S["ffn2_norm"], a2, b2, wg2, wu2, wd2, wo=wo, ga=G)
    dw_out = jnp.concatenate([_matmul_tn("dw_out_a", ya, dh2), _matmul_tn("dw_out_b", yb, dh2)], axis=0).reshape(wo4.shape)
    zero = on_grads(0, [shards(_matmul_tn("dw_ffn2_gate", da2, n3)), shards(_matmul_tn("dw_ffn2_up", db2, n3)),
                        shards(_matmul_tn("dw_ffn2_down", hm2, dh3, scale=0.5)), dw_out,
                        dwpg.astype(bf16).reshape(wpg4.shape), dwpp.astype(bf16)])
    dzxd, dcw, dcb, ddtb, dalog, ddsk, dgssm = _ssd_bwd(xbc, xc, sg, y_ssd, z, dtr, sprev, dyb, conv_w, S["conv_b"], dtb, alog, dsk,
                                                        S["ssm_norm"] + zero)
    zero = on_later(0, dgssm)
    dh1, duv, dg_mix, dlng, dlnb, dws, dbst, dgout = _mix_bwd(dh2, h1, S["mix_norm"] + zero, act, slope, dya, dzxd, w_uv, w_zxd,
                                                              S["gm_ln_g"], S["gm_ln_b"], w_s, b_st, S["gm_out_norm"])
    dw_in = jnp.concatenate([_matmul_tn("dw_in_uv", duv, n2), _matmul_tn("dw_in_zxd", dzxd, n2)[:IN_PROJ - 2 * G]], axis=0)
    zero = on_grads(1, [dw_in.reshape(w_in4.shape)])
    dx, da1, db1, hm1, dg_ffn1 = _ffn_bwd("ffn1_bwd", dh1, x, S["ffn1_norm"] + zero, a1, b1, wg1, wu1, wd1)
    loss = loss + on_later(1, dg_ffn1)
    zero = on_grads(2, [shards(_matmul_tn("dw_ffn1_gate", da1, n1)), shards(_matmul_tn("dw_ffn1_up", db1, n1)),
                        shards(_matmul_tn("dw_ffn1_down", hm1, dh1, scale=0.5))])
    loss = loss + zero
    nh = SSM_HEADS
    gS = {"ffn1_norm": dg_ffn1, "mix_norm": dg_mix, "gm_ln_g": dlng, "gm_ln_b": dlnb, "gm_w_s": dws[None], "gm_b_s": dbst.T[None],
          "gm_out_norm": dgout, "conv_b": dcb, "dt_bias": ddtb[:, :nh], "a_log": dalog[:, :nh], "d_skip": ddsk[:, :nh],
          "ssm_norm": dgssm, "ffn2_norm": dg_ffn2, "ple_norm": dgp, "ple_b_gate": dbpg, "final_norm": dgf.reshape(-1)}
    return loss, dx, dcw, gS


_WEIGHTS = ("ffn1_norm", "ffn1_w_gate", "ffn1_w_up", "ffn1_w_down", "mix_norm", "w_in", "gm_ln_g", "gm_ln_b", "gm_w_s", "gm_b_s",
            "gm_out_norm", "conv_w", "conv_b", "dt_bias", "a_log", "d_skip", "ssm_norm", "w_out", "ffn2_norm", "ffn2_w_gate",
            "ffn2_w_up", "ffn2_w_down", "ple_norm", "ple_w_gate", "ple_b_gate", "ple_w_proj", "final_norm")
_BIG_NAMES = BIG


def kernel(x, p, ffn1_norm, ffn1_w_gate, ffn1_w_up, ffn1_w_down, mix_norm, w_in, gm_ln_g, gm_ln_b, gm_w_s, gm_b_s, gm_out_norm, conv_w, conv_b, dt_bias, a_log, d_skip, ssm_norm, w_out, ffn2_norm, ffn2_w_gate, ffn2_w_up, ffn2_w_down, ple_norm, ple_w_gate, ple_b_gate, ple_w_proj, final_norm, loss_target, m_ffn1_norm, m_ffn1_w_gate, m_ffn1_w_up, m_ffn1_w_down, m_mix_norm, m_w_in, m_gm_ln_g, m_gm_ln_b, m_gm_w_s, m_gm_b_s, m_gm_out_norm, m_conv_w, m_conv_b, m_dt_bias, m_a_log, m_d_skip, m_ssm_norm, m_w_out, m_ffn2_norm, m_ffn2_w_gate, m_ffn2_w_up, m_ffn2_w_down, m_ple_norm, m_ple_w_gate, m_ple_b_gate, m_ple_w_proj, m_final_norm, v_ffn1_norm, v_ffn1_w_gate, v_ffn1_w_up, v_ffn1_w_down, v_mix_norm, v_w_in, v_gm_ln_g, v_gm_ln_b, v_gm_w_s, v_gm_b_s, v_gm_out_norm, v_conv_w, v_conv_b, v_dt_bias, v_a_log, v_d_skip, v_ssm_norm, v_w_out, v_ffn2_norm, v_ffn2_w_gate, v_ffn2_w_up, v_ffn2_w_down, v_ple_norm, v_ple_w_gate, v_ple_b_gate, v_ple_w_proj, v_final_norm):
    given = dict(locals())
    w = {n: given[n] for n in _WEIGHTS}
    m = {n: given["m_" + n] for n in _WEIGHTS}
    v = {n: given["v_" + n] for n in _WEIGHTS}

    c_idx = lax.axis_index("c").astype(jnp.int32).reshape(1)
    chip = 2 * lax.axis_index("x") + lax.axis_index("y")
    chip_idx = chip.astype(jnp.int32).reshape(1)

    shard = {n: (jnp.swapaxes(w[n][0], 0, 1) if n in TRANSPOSED else w[n][0]).astype(bf16) for n in BIG}
    shard["conv_w"] = w["conv_w"][0]
    first = _gather_weights([shard[n] for n in FETCH[0]], [True] * len(FETCH[0]))
    fetching, after = [], first[-1]
    for k in (1, 2):
        srcs = [shard[n] for n in FETCH[k]]
        lands = [jax.ShapeDtypeStruct((N_CHIPS,) + s.shape, s.dtype) for s in srcs]
        fetching.append(_copies_start("gather%d_start" % k, srcs, lands, 4 * len(srcs), _gather_copies, after))
        after = fetching[-1][4]

    def fetch(k, after_):
        return first if k == 0 else _copies_wait("gather%d_wait" % k, fetching[k - 1], _gather_copies, [after_])[1]

    swapping, exchanging = {}, {}

    def exchange(k, grads, others):
        parts = [_add_halves("add_" + n, g_, o_, c_idx) for n, g_, o_ in zip(DONE[k], grads, others)]
        lands = [jax.ShapeDtypeStruct((3,) + p_.shape[1:], p_.dtype) for p_ in parts]
        exchanging[k] = _copies_start("exchange%d_start" % k, parts, lands, 3 * len(parts), _partial_copies, c_idx)
        return exchanging[k][4][0, 0]

    def on_grads(k, grads):
        grads = [_pad_rows(g_) for g_ in grads]
        if k == len(DONE) - 1:
            return exchange(k, grads, _swap_halves("swap%d" % k, grads))
        lands = [jax.ShapeDtypeStruct((g_.shape[0], g_.shape[1] // 2, g_.shape[2]), g_.dtype) for g_ in grads]
        swapping[k] = _copies_start("swap%d_start" % k, grads, lands, len(grads), _swap_copies, c_idx)
        return swapping[k][4][0, 0]

    def on_later(k, after_):
        return exchange(k, *_copies_wait("swap%d_wait" % k, swapping[k], _swap_copies, [after_]))

    S = {n: w[n] for n in SMALL}
    S["ffn1_norm"] = S["ffn1_norm"] + after[0, 0]
    loss, dx, dcw, gS = _local_step(x[0], p[0, 0], loss_target[0], fetch, S, on_grads, on_later)

    small = _pack_small([gS[n] for n in SMALL] + [dcw, loss[:, :1]])
    small_lands = [jax.ShapeDtypeStruct((N_DEV - 1,) + small.shape, small.dtype)]
    small_st = _copies_start("small_start", [small], small_lands, N_DEV - 1, _small_copies, c_idx)

    g, delta, new_m, new_v = {}, {}, {}, {}

    def share_start(k, after_):
        parts, recv = _copies_wait("exchange%d_wait" % k, exchanging[k], _partial_copies, after_)
        mine = [_sum_partials("sum_" + n, p_, r_, chip_idx) for n, p_, r_ in zip(DONE[k], parts, recv)]
        return _copies_start("share%d_start" % k, mine, mine, len(mine), _share_copies, c_idx)

    def update(k, sharing, after_):
        mine, theirs = _copies_wait("share%d_wait" % k, sharing, _share_copies, after_)
        after = []
        for n, gm_, gt_ in zip(DONE[k], mine, theirs):
            flip = (lambda a: jnp.swapaxes(a, 0, 1)) if n in TRANSPOSED else (lambda a: a)
            rows = flip(w[n][0]).shape[0]
            if rows % ROW_PAD:
                def lin(a):
                    return jnp.transpose(a.reshape(-1, LANES, rows), (2, 0, 1))

                def back(a):
                    return jnp.transpose(a, (1, 2, 0)).reshape(1, -1, rows)

                gm_, gt_ = [a.reshape(a.shape[0], -1, LANES) for a in (gm_, gt_)]
                g_ = jnp.where(c_idx[0] == 0, jnp.concatenate([gm_, gt_]), jnp.concatenate([gt_, gm_]))[:rows]
                outs = [g_, *_adamw("adamw_" + n, lin(w[n]), g_, lin(m[n]), lin(v[n]), tr=HALF_ROWS_F32)]
                g[n], delta[n], new_m[n], new_v[n] = [back(o) for o in outs]
            else:
                outs = _adamw_big("adamw_" + n, flip(w[n][0]), gm_, gt_, flip(m[n][0]), flip(v[n][0]), c_idx)
                g[n], delta[n], new_m[n], new_v[n] = [flip(o)[None] for o in outs]
            after.append(outs[3])
        return after

    sharing = [share_start(0, [small_st[4]])]
    after = [sharing[0][4]]
    for k in range(len(DONE)):
        if k + 1 < len(DONE):
            sharing.append(share_start(k + 1, after))
            after = [sharing[k + 1][4]]
        after = update(k, sharing[k], after)
    (own,), (slots,) = _copies_wait("small_wait", small_st, _small_copies, after)
    dev_idx = (2 * chip + lax.axis_index("c")).astype(jnp.int32).reshape(1)
    small_shapes = [w[n].shape for n in SMALL] + [dcw.shape, (1, 1)]
    small_sum = _unpack_small(_sum_small(own, slots, dev_idx), small_shapes)
    g.update({n: small_sum[i] for i, n in enumerate(SMALL)})
    cshard = w["conv_w"].shape[2]
    g["conv_w"] = lax.dynamic_slice_in_dim(small_sum[len(SMALL)], chip * cshard, cshard, axis=1)[None]
    loss_total = small_sum[len(SMALL) + 1].reshape(())
    sm_names = SMALL + ("conv_w",)
    sm_shapes = [w[n].shape for n in sm_names]
    d_s, m_s, v_s = _adamw("adamw_small", _pack_small([w[n] for n in sm_names]), _pack_small([g[n] for n in sm_names]),
                           _pack_small([m[n] for n in sm_names]), _pack_small([v[n] for n in sm_names]))
    for dst, src in ((delta, d_s), (new_m, m_s), (new_v, v_s)):
        for n, val in zip(sm_names, _unpack_small(src, sm_shapes)):
            dst[n] = val

    return (loss_total, dx[None], *[g[n] for n in _WEIGHTS], *[delta[n] for n in _WEIGHTS],
            *[new_m[n] for n in _WEIGHTS], *[new_v[n] for n in _WEIGHTS])
```

```python
import jax
import jax.numpy as jnp
from jax import lax
from jax.experimental import pallas as pl
from jax.experimental.pallas import tpu as pltpu

f32 = jnp.float32
bf16 = jnp.bfloat16
MESH = pl.DeviceIdType.MESH
HIGHEST = lax.Precision.HIGHEST

EPS = 1e-6
N_CHIPS = 4
N_DEV = 8
D_MODEL = 1024
GM_WIDTH = 1024
GM_HEADS = 8
CHUNK = 128
SSM_WIDTH = 1024
SSM_HEADS = 16
SSM_HEAD_DIM = 64
SSM_GROUPS = 2
SSM_STATE = 128
SSM_CONV = 4
CONV_DIM = SSM_WIDTH + 2 * SSM_GROUPS * SSM_STATE
IN_PROJ = 2 * GM_WIDTH + SSM_WIDTH + CONV_DIM + SSM_HEADS
LANES = 128
ZXD = SSM_WIDTH + CONV_DIM + LANES

ADAM_LR = 0.001
ADAM_B1 = 0.9
ADAM_B2 = 0.999
ADAM_EPS = 1e-08
ADAM_WD = 0.01
ADAM_STEP = 10

VMEM_LIMIT = 56 * 1024 * 1024
HALF_ROWS_BF16 = 592
HALF_ROWS_F32 = 320


def _dot(a, b):
    return jnp.dot(a, b, preferred_element_type=f32)


def _dot_nt(a, b):
    return lax.dot_general(a, b, (((1,), (1,)), ((), ())), preferred_element_type=f32)


def _dot_tn(a, b):
    return lax.dot_general(a, b, (((0,), (0,)), ((), ())), preferred_element_type=f32)


def _rms(x, g):
    return x * lax.rsqrt(jnp.mean(x * x, axis=-1, keepdims=True) + EPS) * g


def _layernorm(x, g, b):
    mu = jnp.mean(x, axis=-1, keepdims=True)
    xc = x - mu
    return xc * lax.rsqrt(jnp.mean(xc * xc, axis=-1, keepdims=True) + EPS) * g + b


def _sigmoid(x):
    return 1.0 / (1.0 + jnp.exp(-x))


def _softplus(x):
    return jnp.maximum(x, 0.0) + jnp.log(1.0 + jnp.exp(-jnp.abs(x)))


def _full(shape):
    nd = len(shape)
    return pl.BlockSpec(shape, lambda *_: (0,) * nd, pipeline_mode=pl.Buffered(1))


def _acc(shape):
    nd = len(shape)
    return pl.BlockSpec(shape, lambda *_: (0,) * nd)


def _rows(tm, ncols):
    return pl.BlockSpec((tm, ncols), lambda i: (i, 0))


def _params(sem):
    return pltpu.CompilerParams(dimension_semantics=sem, vmem_limit_bytes=VMEM_LIMIT)


def _row_tile(rows, target, mult=8):
    best = rows
    for t in range(mult, min(rows, target) + 1, mult):
        if rows % t == 0:
            best = t
    return best if best <= target else rows


def _ffn_fwd(name, h, g, wg, wu, wd, pre=None, tm=256):
    T, D = h.shape
    F = wg.shape[0]
    tm = min(tm, T)

    def body(*refs):
        if pre is None:
            h_ref, g_ref, wg_ref, wu_ref, wd_ref, ho_ref, n_ref, a_ref, b_ref = refs
            hin = h_ref[...]
        else:
            (h_ref, ya_ref, yb_ref, wo_ref, g_ref, wg_ref, wu_ref, wd_ref,
             hi_ref, ho_ref, n_ref, a_ref, b_ref) = refs
            ga = ya_ref.shape[1]
            hin = h_ref[...] + _dot(ya_ref[...], wo_ref[:ga, :]) + _dot(yb_ref[...], wo_ref[ga:, :])
            hi_ref[...] = hin
        n = _rms(hin, g_ref[...]).astype(bf16)
        n_ref[...] = n
        a = _dot_nt(n, wg_ref[...]).astype(bf16)
        b = _dot_nt(n, wu_ref[...]).astype(bf16)
        a_ref[...] = a
        b_ref[...] = b
        af = a.astype(f32)
        hm = (af * _sigmoid(af) * b.astype(f32)).astype(bf16)
        ho_ref[...] = hin + 0.5 * _dot(hm, wd_ref[...])

    ins = [h] + (list(pre) if pre is not None else []) + [g, wg, wu, wd]
    in_specs = [_rows(tm, D)]
    if pre is not None:
        in_specs += [_rows(tm, pre[0].shape[1]), _rows(tm, pre[1].shape[1]), _full(pre[2].shape)]
    in_specs += [_full(g.shape), _full(wg.shape), _full(wu.shape), _full(wd.shape)]
    outs = [jax.ShapeDtypeStruct((T, D), f32), jax.ShapeDtypeStruct((T, D), bf16),
            jax.ShapeDtypeStruct((T, F), bf16), jax.ShapeDtypeStruct((T, F), bf16)]
    out_specs = [_rows(tm, D), _rows(tm, D), _rows(tm, F), _rows(tm, F)]
    if pre is not None:
        outs = [jax.ShapeDtypeStruct((T, D), f32)] + outs
        out_specs = [_rows(tm, D)] + out_specs
    return pl.pallas_call(body, name=name, grid=(T // tm,), in_specs=in_specs, out_specs=out_specs,
                          out_shape=outs, compiler_params=_params(("parallel",)))(*ins)


def _ffn_bwd(name, dh, hin, g, a, b, wg, wu, wd, wo=None, ga=0, tm=256):
    T, D = dh.shape
    F = wg.shape[0]
    tm = min(tm, T)

    def body(*refs):
        if wo is None:
            (dh_ref, hin_ref, g_ref, a_ref, b_ref, wg_ref, wu_ref, wd_ref,
             dhi_ref, da_ref, db_ref, hm_ref, dg_ref) = refs
        else:
            (dh_ref, hin_ref, g_ref, a_ref, b_ref, wg_ref, wu_ref, wd_ref, wo_ref,
             dhi_ref, da_ref, db_ref, hm_ref, dg_ref, dya_ref, dyb_ref) = refs

        @pl.when(pl.program_id(0) == 0)
        def _():
            dg_ref[...] = jnp.zeros_like(dg_ref)

        dh_ = dh_ref[...]
        dhb = (0.5 * dh_).astype(bf16)
        dhm = _dot_nt(dhb, wd_ref[...])
        af = a_ref[...].astype(f32)
        bf = b_ref[...].astype(f32)
        sg = _sigmoid(af)
        sl_ = af * sg
        da = (dhm * bf * (sg * (1.0 + af * (1.0 - sg)))).astype(bf16)
        db = (dhm * sl_).astype(bf16)
        da_ref[...] = da
        db_ref[...] = db
        hm_ref[...] = (sl_ * bf).astype(bf16)
        dn = _dot(da, wg_ref[...]) + _dot(db, wu_ref[...])
        _, vjp = jax.vjp(_rms, hin_ref[...], g_ref[...])
        dx, dg = vjp(dn)
        dhi = dh_ + dx
        dhi_ref[...] = dhi
        dg_ref[...] += dg
        if wo is not None:
            dhib = dhi.astype(bf16)
            dya_ref[...] = _dot_nt(dhib, wo_ref[:ga, :]).astype(bf16)
            dyb_ref[...] = _dot_nt(dhib, wo_ref[ga:, :]).astype(bf16)

    ins = [dh, hin, g, a, b, wg, wu, wd]
    in_specs = [_rows(tm, D), _rows(tm, D), _full(g.shape), _rows(tm, F), _rows(tm, F),
                _full(wg.shape), _full(wu.shape), _full(wd.shape)]
    act = jax.ShapeDtypeStruct((T, F), bf16)
    outs = [jax.ShapeDtypeStruct((T, D), f32), act, act, act, jax.ShapeDtypeStruct(g.shape, f32)]
    out_specs = [_rows(tm, D), _rows(tm, F), _rows(tm, F), _rows(tm, F), _acc(g.shape)]
    if wo is not None:
        gb = wo.shape[0] - ga
        ins += [wo]
        in_specs += [_full(wo.shape)]
        outs += [jax.ShapeDtypeStruct((T, ga), bf16), jax.ShapeDtypeStruct((T, gb), bf16)]
        out_specs += [_rows(tm, ga), _rows(tm, gb)]
    return pl.pallas_call(body, name=name, grid=(T // tm,), in_specs=in_specs, out_specs=out_specs,
                          out_shape=outs, compiler_params=_params(("arbitrary",)))(*ins)


def _matmul_tn(name, a, b, scale=1.0, tk=2048):
    T, M = a.shape
    N = b.shape[1]
    tk = min(tk, T)
    nk = T // tk
    tn = LANES * max(d for d in range(1, N // LANES + 1) if (N // LANES) % d == 0 and (d == 1 or M * d * LANES * 4 <= 6 * 1024 * 1024))

    def body(a_ref, b_ref, o_ref, acc):
        k = pl.program_id(1)

        @pl.when(k == 0)
        def _():
            acc[...] = jnp.zeros_like(acc)

        bb = b_ref[...]
        if scale != 1.0:
            bb = bb * scale
        acc[...] += _dot_tn(a_ref[...].astype(bf16), bb.astype(bf16))

        @pl.when(k == nk - 1)
        def _():
            o_ref[...] = acc[...].astype(bf16)

    return pl.pallas_call(
        body, name=name, grid=(N // tn, nk),
        in_specs=[pl.BlockSpec((tk, M), lambda j, k: (k, 0)), pl.BlockSpec((tk, tn), lambda j, k: (k, j))],
        out_specs=pl.BlockSpec((M, tn), lambda j, k: (0, j)),
        out_shape=jax.ShapeDtypeStruct((M, N), bf16), scratch_shapes=[pltpu.VMEM((M, tn), f32)],
        compiler_params=_params(("parallel", "arbitrary")))(a, b)


def _gelu_and_slope(x):
    cdf = 0.5 * (1.0 + lax.erf(x * 0.7071067811865476))
    return x * cdf, cdf + x * (0.3989422804014327 * jnp.exp(-0.5 * x * x))


def _tril_mask():
    r = lax.broadcasted_iota(jnp.int32, (CHUNK, CHUNK), 0)
    c = lax.broadcasted_iota(jnp.int32, (CHUNK, CHUNK), 1)
    return c <= r


def _gm_mix(vnb, ws_ref, bst, mixed_sc, tm):
    mask = _tril_mask()
    for h in range(GM_HEADS):
        wt = jnp.where(mask, ws_ref[h], 0.0).astype(bf16)
        bias = bst[:, h:h + 1]
        for q in range(tm // CHUNK):
            rs = slice(q * CHUNK, (q + 1) * CHUNK)
            cs = slice(h * CHUNK, (h + 1) * CHUNK)
            mixed_sc[rs, cs] = _dot(wt, vnb[rs, cs]) + bias


def _mix_fwd(h1, gmix, w_uv, w_zxd, ln_g, ln_b, w_s, b_st, gout, tm=512):
    T, D = h1.shape
    tm = min(tm, T)
    G = GM_WIDTH

    def body(h_ref, g_ref, wuv_ref, wzxd_ref, lng_ref, lnb_ref, ws_ref, bst_ref, gout_ref,
             n_ref, act_ref, slope_ref, z_ref, xbc_ref, dt_ref, ya_ref, mixed_sc):
        n = _rms(h_ref[...], g_ref[...]).astype(bf16)
        n_ref[...] = n
        uv = _dot_nt(n, wuv_ref[...]).astype(bf16)
        zxd = _dot_nt(n, wzxd_ref[...])
        z_ref[...] = zxd[:, :SSM_WIDTH].astype(bf16)
        xbc_ref[...] = zxd[:, SSM_WIDTH:SSM_WIDTH + CONV_DIM].astype(bf16)
        dt_ref[...] = zxd[:, SSM_WIDTH + CONV_DIM:]
        act, slope = _gelu_and_slope(uv.astype(f32))
        act = act.astype(bf16)
        act_ref[...] = act
        slope_ref[...] = slope.astype(bf16)
        ug, vg = act[:, :G].astype(f32), act[:, G:].astype(f32)
        _gm_mix(_layernorm(vg, lng_ref[...], lnb_ref[...]).astype(bf16), ws_ref, bst_ref[...], mixed_sc, tm)
        ya_ref[...] = _rms(ug * mixed_sc[...], gout_ref[...]).astype(bf16)

    ins = [h1, gmix, w_uv, w_zxd, ln_g, ln_b, w_s, b_st, gout]
    in_specs = [_rows(tm, D)] + [_full(x.shape) for x in ins[1:]]
    outs = [jax.ShapeDtypeStruct((T, D), bf16), jax.ShapeDtypeStruct((T, 2 * G), bf16), jax.ShapeDtypeStruct((T, 2 * G), bf16),
            jax.ShapeDtypeStruct((T, SSM_WIDTH), bf16), jax.ShapeDtypeStruct((T, CONV_DIM), bf16),
            jax.ShapeDtypeStruct((T, LANES), f32), jax.ShapeDtypeStruct((T, G), bf16)]
    out_specs = [_rows(tm, D), _rows(tm, 2 * G), _rows(tm, 2 * G), _rows(tm, SSM_WIDTH), _rows(tm, CONV_DIM), _rows(tm, LANES),
                 _rows(tm, G)]
    return pl.pallas_call(body, name="mix_fwd", grid=(T // tm,), in_specs=in_specs, out_specs=out_specs,
                          out_shape=outs, scratch_shapes=[pltpu.VMEM((tm, G), f32)],
                          compiler_params=_params(("parallel",)))(*ins)


def _mix_bwd(dh, h1, gmix, act, slope, dya, dzxd, w_uv, w_zxd, ln_g, ln_b, w_s, b_st, gout, tm=256):
    T, D = dh.shape
    tm = min(tm, T)
    G = GM_WIDTH

    def body(dh_ref, h_ref, g_ref, act_ref, slope_ref, dya_ref, dzxd_ref, wuv_ref, wzxd_ref, lng_ref, lnb_ref, ws_ref,
             bst_ref, gout_ref, dhi_ref, duv_ref, dg_ref, dlng_ref, dlnb_ref, dws_ref, dbst_ref, dgout_ref, mixed_sc, dvn_sc):
        @pl.when(pl.program_id(0) == 0)
        def _():
            for r in (dg_ref, dlng_ref, dlnb_ref, dws_ref, dbst_ref, dgout_ref):
                r[...] = jnp.zeros_like(r)

        dn_z = _dot(dzxd_ref[...], wzxd_ref[...])
        ug = act_ref[:, :G].astype(f32)
        vn, ln_vjp = jax.vjp(_layernorm, act_ref[:, G:].astype(f32), lng_ref[...], lnb_ref[...])
        vnb = vn.astype(bf16)
        _gm_mix(vnb, ws_ref, bst_ref[...], mixed_sc, tm)
        mixed = mixed_sc[...]
        _, out_vjp = jax.vjp(_rms, ug * mixed, gout_ref[...])
        dpre, dgout = out_vjp(dya_ref[...].astype(f32))
        dgout_ref[...] += dgout
        dug = dpre * mixed
        dmixed = dpre * ug
        mask = _tril_mask()
        lane = lax.broadcasted_iota(jnp.int32, (1, GM_HEADS), 1)
        dbst = jnp.zeros((CHUNK, GM_HEADS), f32)
        for h in range(GM_HEADS):
            wt = jnp.where(mask, ws_ref[h], 0.0).astype(bf16)
            cs = slice(h * CHUNK, (h + 1) * CHUNK)
            dw = jnp.zeros((CHUNK, CHUNK), f32)
            for q in range(tm // CHUNK):
                rs = slice(q * CHUNK, (q + 1) * CHUNK)
                dm = dmixed[rs, cs]
                dmb = dm.astype(bf16)
                dw = dw + _dot_nt(dmb, vnb[rs, cs])
                dbst = dbst + jnp.sum(dm, axis=1, keepdims=True) * (lane == h).astype(f32)
                dvn_sc[rs, cs] = _dot_tn(wt, dmb)
            dws_ref[h] += jnp.where(mask, dw, 0.0)
        dbst_ref[...] += dbst
        dvg, dlng, dlnb = ln_vjp(dvn_sc[...])
        duv = (jnp.concatenate([dug, dvg], axis=1) * slope_ref[...].astype(f32)).astype(bf16)
        duv_ref[...] = duv
        dlng_ref[...] += dlng
        dlnb_ref[...] += dlnb
        dn = dn_z + _dot(duv, wuv_ref[...])
        _, vjp = jax.vjp(_rms, h_ref[...], g_ref[...])
        dx, dg = vjp(dn)
        dhi_ref[...] = dh_ref[...] + dx
        dg_ref[...] += dg

    ins = [dh, h1, gmix, act, slope, dya, dzxd, w_uv, w_zxd, ln_g, ln_b, w_s, b_st, gout]
    in_specs = ([_rows(tm, D), _rows(tm, D), _full(gmix.shape), _rows(tm, 2 * G), _rows(tm, 2 * G), _rows(tm, G),
                 _rows(tm, dzxd.shape[1])] + [_full(x.shape) for x in ins[7:]])
    accs = (gmix, ln_g, ln_b, w_s, b_st, gout)
    outs = ([jax.ShapeDtypeStruct((T, D), f32), jax.ShapeDtypeStruct((T, 2 * G), bf16)]
            + [jax.ShapeDtypeStruct(x.shape, f32) for x in accs])
    out_specs = [_rows(tm, D), _rows(tm, 2 * G)] + [_acc(x.shape) for x in accs]
    return pl.pallas_call(body, name="mix_bwd", grid=(T // tm,), in_specs=in_specs, out_specs=out_specs,
                          out_shape=outs, scratch_shapes=[pltpu.VMEM((tm, G), f32), pltpu.VMEM((tm, G), f32)],
                          compiler_params=_params(("arbitrary",)))(*ins)


HALO = 16
SSD_SUB = 4


class _RowsOf:
    def __init__(self, ref, rows):
        self.ref, self.rows = ref, rows

    def _index(self, idx):
        return (self.rows, slice(None)) if idx is Ellipsis else (self.rows,) + tuple(idx[1:])

    def __getitem__(self, idx):
        return self.ref[self._index(idx)]

    def __setitem__(self, idx, value):
        self.ref[self._index(idx)] = value
PAIRS = SSM_HEADS // 2
PAIR_W = 2 * SSM_HEAD_DIM


def _split(x, n):
    parts = []
    for _ in range(n):
        p = x.astype(bf16)
        parts.append(p)
        x = x - p.astype(f32)
    return parts


def _dot_sel(x, sel_n, n):
    return _dot(jnp.concatenate(_split(x, n), axis=1), sel_n)


def _sel_dot(sel, x, n):
    return _dot(jnp.concatenate([sel] * n, axis=1), jnp.concatenate(_split(x, n), axis=0))


EXPAND_SPLIT = 3
REDUCE_SPLIT = 2


def _head_mats():
    ex = (jnp.arange(SSM_WIDTH)[None, :] // SSM_HEAD_DIM == jnp.arange(LANES)[:, None]).astype(bf16)
    return jnp.tile(ex, (EXPAND_SPLIT, 1)), jnp.tile(ex.T, (REDUCE_SPLIT, 1))


def _shift_mat(rows, cols, off):
    r = lax.broadcasted_iota(jnp.int32, (rows, cols), 0)
    c = lax.broadcasted_iota(jnp.int32, (rows, cols), 1)
    return (c == r + off).astype(bf16)


def _ssd_conv(halo, x, cw_ref, cb_ref):
    ext = jnp.concatenate([halo, x], axis=0)
    xc = cb_ref[...] + cw_ref[SSM_CONV - 1:SSM_CONV, :] * x.astype(f32)
    for j in range(SSM_CONV - 1):
        xc = xc + cw_ref[j:j + 1, :] * _dot(_shift_mat(CHUNK, HALO + CHUNK, HALO - SSM_CONV + 1 + j), ext)
    return xc


def _ssd_front(dtr, dtb_ref, alog_ref):
    dt = _softplus(dtr + dtb_ref[...])
    a = -jnp.exp(alog_ref[...])
    acs = jnp.dot(_tril_mask().astype(f32), dt * a, preferred_element_type=f32, precision=HIGHEST)
    return dt, a, acs


def _ssd_wide(xa, dt, acs, dsk, ex):
    dt_x = _dot_sel(dt, ex, EXPAND_SPLIT)
    acs_x = _dot_sel(acs, ex, EXPAND_SPLIT)
    dsk_x = _dot_sel(jnp.broadcast_to(dsk, (8, LANES)), ex, EXPAND_SPLIT)[0:1]
    e_x = jnp.exp(acs_x)
    r_x = jnp.exp(acs_x[CHUNK - 1:CHUNK, :] - acs_x)
    xs = xa[:, :SSM_WIDTH]
    xd = xs * dt_x
    return dt_x, dsk_x, e_x, r_x, xs, xd, xd * r_x


def _pair_stack(v, lo):
    return jnp.concatenate([jnp.where(lo, v, 0.0), jnp.where(lo, 0.0, v)], axis=0)


def _ssd_pair(j, acs, acs_t, cb):
    out = []
    tril = _tril_mask()
    for h in (2 * j, 2 * j + 1):
        dk = jnp.exp(jnp.where(tril, acs[:, h:h + 1] - acs_t[h:h + 1, :], -jnp.inf))
        out.append((dk, cb * dk))
    return out


def _pair_col(row_lo, tot, j):
    return jnp.exp(jnp.where(row_lo, tot[:, 2 * j:2 * j + 1], tot[:, 2 * j + 1:2 * j + 2]))


def _gated_norm(y, z, g):
    yg = y * (z * _sigmoid(z))
    half = SSM_WIDTH // SSM_GROUPS
    parts = []
    for k in range(SSM_GROUPS):
        s = yg[:, k * half:(k + 1) * half]
        parts.append(s * lax.rsqrt(jnp.mean(s * s, axis=-1, keepdims=True) + EPS))
    return jnp.concatenate(parts, axis=1) * g


def _group_mats(xa):
    out = []
    for g in range(SSM_GROUPS):
        bm = xa[:, SSM_WIDTH + g * SSM_STATE:SSM_WIDTH + (g + 1) * SSM_STATE].astype(bf16)
        cm = xa[:, SSM_WIDTH + (SSM_GROUPS + g) * SSM_STATE:SSM_WIDTH + (SSM_GROUPS + g + 1) * SSM_STATE].astype(bf16)
        out.append((cm, bm, _dot_nt(cm, bm)))
    return out


def _ssd_fwd(xbc, z, dtr, conv_w, conv_b, dt_bias, a_log, d_skip, ssm_norm):
    T = xbc.shape[0]
    nc = T // CHUNK
    N = SSM_STATE

    def body(xbc_ref, halo_ref, z_ref, dtr_ref, cw_ref, cb_ref, dtb_ref, alog_ref, dsk_ref, g_ref, ex_ref,
             yb_ref, xc_ref, sg_ref, y_ref, sprev_ref, s_sc):
        i = pl.program_id(0)

        @pl.when(i == 0)
        def _():
            s_sc[...] = jnp.zeros_like(s_sc)

        lo = lax.broadcasted_iota(jnp.int32, (CHUNK, PAIR_W), 1) < SSM_HEAD_DIM
        row_lo = lax.broadcasted_iota(jnp.int32, (PAIR_W, 1), 0) < SSM_HEAD_DIM
        for k in range(SUB):
            rs = slice(k * CHUNK, (k + 1) * CHUNK)
            if k == 0:
                halo = halo_ref[...]
                halo = jnp.where(i > 0, halo, jnp.zeros_like(halo))
            else:
                halo = xbc_ref[k * CHUNK - HALO:k * CHUNK, :]
            xc = _ssd_conv(halo, xbc_ref[rs, :], cw_ref, cb_ref)
            sg = _sigmoid(xc)
            xc_ref[rs, :] = xc
            sg_ref[rs, :] = sg
            xa = xc * sg
            dt, _, acs = _ssd_front(dtr_ref[rs, :], dtb_ref, alog_ref)
            _, dsk_x, e_x, _, xs, xd, gm = _ssd_wide(xa, dt, acs, dsk_ref[...], ex_ref[...])
            acs_t = acs.T
            tot = acs[CHUNK - 1:CHUNK, :]
            groups = _group_mats(xa)
            ys = []
            for j in range(PAIRS):
                cmb, bmb, cb = groups[j // (PAIRS // SSM_GROUPS)]
                ps = slice(j * PAIR_W, (j + 1) * PAIR_W)
                (_, m0), (_, m1) = _ssd_pair(j, acs, acs_t, cb)
                sp = s_sc[j]
                yd = _dot(jnp.concatenate([m0, m1], axis=1).astype(bf16), _pair_stack(xd[:, ps], lo).astype(bf16))
                ys.append(yd + e_x[:, ps] * _dot_nt(cmb, sp.astype(bf16)))
                sprev_ref[k, j] = sp
                s_sc[j] = _pair_col(row_lo, tot, j) * sp + _dot_tn(gm[:, ps].astype(bf16), bmb)
            y = jnp.concatenate(ys, axis=1) + xs * dsk_x
            y_ref[rs, :] = y
            yb_ref[rs, :] = _gated_norm(y, z_ref[rs, :].astype(f32), g_ref[...]).astype(bf16)

    params = [conv_w, conv_b, dt_bias, a_log, d_skip, ssm_norm, _head_mats()[0]]
    SUB = SSD_SUB if nc % SSD_SUB == 0 else 1
    hp = SUB * CHUNK // HALO
    R = SUB * CHUNK
    in_specs = [_rows(R, CONV_DIM), pl.BlockSpec((HALO, CONV_DIM), lambda i: (jnp.maximum(i * hp - 1, 0), 0)),
                _rows(R, SSM_WIDTH), _rows(R, LANES)] + [_full(x.shape) for x in params]
    return pl.pallas_call(
        body, name="ssd_fwd", grid=(nc // SUB,), in_specs=in_specs,
        out_specs=[_rows(R, SSM_WIDTH), _rows(R, CONV_DIM), _rows(R, CONV_DIM), _rows(R, SSM_WIDTH),
                   pl.BlockSpec((SUB, PAIRS, PAIR_W, N), lambda i: (i, 0, 0, 0))],
        out_shape=[jax.ShapeDtypeStruct((T, SSM_WIDTH), bf16), jax.ShapeDtypeStruct((T, CONV_DIM), f32),
                   jax.ShapeDtypeStruct((T, CONV_DIM), f32), jax.ShapeDtypeStruct((T, SSM_WIDTH), f32),
                   jax.ShapeDtypeStruct((nc, PAIRS, PAIR_W, N), f32)],
        scratch_shapes=[pltpu.VMEM((PAIRS, PAIR_W, N), f32)],
        compiler_params=_params(("arbitrary",)))(xbc, xbc, z, dtr, *params)


def _ssd_bwd(xbc, xc, sg, y, z, dtr, sprev, dyb, conv_w, conv_b, dt_bias, a_log, d_skip, ssm_norm):
    T = xbc.shape[0]
    nc = T // CHUNK
    H, N = SSM_HEADS, SSM_STATE
    PG = PAIRS // SSM_GROUPS

    def chunk(xbc_ref, xc_ref, sg_ref, y_ref, z_ref, dtr_ref, sprev_k, dyb_ref, cw_ref, cb_ref, dtb_ref, alog_ref, dsk_ref,
              g_ref, ex_ref, rd_ref, dzxd_ref, dcw_ref, dcb_ref, ddtb_ref, dalog_ref, ddsk_ref, dg_ref, ds_sc, next_sc):
        xc = xc_ref[...]
        sg = sg_ref[...]
        xa = xc * sg
        dt, a, acs = _ssd_front(dtr_ref[...], dtb_ref, alog_ref)
        dt_x, dsk_x, e_x, r_x, xs, xd, gm = _ssd_wide(xa, dt, acs, dsk_ref[...], ex_ref[...])
        acs_t = acs.T
        tot = acs[CHUNK - 1:CHUNK, :]
        groups = _group_mats(xa)
        lo = lax.broadcasted_iota(jnp.int32, (CHUNK, PAIR_W), 1) < SSM_HEAD_DIM
        row_lo = lax.broadcasted_iota(jnp.int32, (PAIR_W, 1), 0) < SSM_HEAD_DIM
        pairs, zs = [], []
        for j in range(PAIRS):
            cmb, _, cb = groups[j // PG]
            pairs.append(_ssd_pair(j, acs, acs_t, cb))
            zs.append(_dot_nt(cmb, sprev_k[j].astype(bf16)))
        zf = jnp.concatenate(zs, axis=1)
        _, gn_vjp = jax.vjp(_gated_norm, y_ref[...], z_ref[...].astype(f32), g_ref[...])
        dy, dz, dg = gn_vjp(dyb_ref[...].astype(f32))
        dg_ref[...] += dg
        dzxd_ref[:, :SSM_WIDTH] = dz.astype(bf16)

        lane = lax.broadcasted_iota(jnp.int32, (1, LANES), 1)
        sub = lax.broadcasted_iota(jnp.int32, (LANES, 1), 0)
        dacs = jnp.zeros((CHUNK, LANES), f32)
        dacs_r = jnp.zeros((LANES, CHUNK), f32)
        dtot = jnp.zeros((1, LANES), f32)
        dcb = [jnp.zeros((CHUNK, CHUNK), f32) for _ in range(SSM_GROUPS)]
        dcm = [jnp.zeros((CHUNK, N), f32) for _ in range(SSM_GROUPS)]
        dbm = [jnp.zeros((CHUNK, N), f32) for _ in range(SSM_GROUPS)]
        dxds, dgms = [], []
        for j in range(PAIRS):
            g = j // PG
            cmb, bmb, _ = groups[g]
            ps = slice(j * PAIR_W, (j + 1) * PAIR_W)
            (dk0, m0), (dk1, m1) = pairs[j]
            oh0, oh1 = (lane == 2 * j).astype(f32), (lane == 2 * j + 1).astype(f32)
            dyp = dy[:, ps]
            dy2 = _pair_stack(dyp, lo).astype(bf16)
            dm2 = _dot_nt(dy2, xd[:, ps].astype(bf16))
            m2 = jnp.concatenate([m0, m1], axis=0)
            dxds.append(_dot_tn(m2.astype(bf16), dy2))
            w2 = dm2 * m2
            rs = jnp.sum(w2, axis=1, keepdims=True)
            dacs = dacs + rs[:CHUNK] * oh0 + rs[CHUNK:] * oh1
            dacs_r = dacs_r - ((sub == 2 * j).astype(f32) * jnp.sum(w2[:CHUNK], axis=0, keepdims=True)
                               + (sub == 2 * j + 1).astype(f32) * jnp.sum(w2[CHUNK:], axis=0, keepdims=True))
            dcb[g] = dcb[g] + dm2[:CHUNK] * dk0 + dm2[CHUNK:] * dk1
            sp = sprev_k[j]
            dzb = (dyp * e_x[:, ps]).astype(bf16)
            dcm[g] = dcm[g] + _dot(dzb, sp.astype(bf16))
            dsn = ds_sc[j]
            dsnb = dsn.astype(bf16)
            et = _pair_col(row_lo, tot, j)
            rr = jnp.sum(dsn * sp, axis=1, keepdims=True) * et
            dtot = dtot + jnp.sum(rr[:SSM_HEAD_DIM]) * oh0 + jnp.sum(rr[SSM_HEAD_DIM:]) * oh1
            dgms.append(_dot_nt(bmb, dsnb))
            dbm[g] = dbm[g] + _dot(gm[:, ps].astype(bf16), dsnb)
            ds_sc[j] = _dot_tn(dzb, cmb) + et * dsn
        dgm = jnp.concatenate(dgms, axis=1)
        dxd = jnp.concatenate(dxds, axis=1) + dgm * r_x
        dr = dgm * gm
        red = _dot_sel(jnp.concatenate([dy * e_x * zf - dr, dr, dxd * xs, dy * xs], axis=0), rd_ref[...], REDUCE_SPLIT)
        rowi = lax.broadcasted_iota(jnp.int32, (CHUNK, 1), 0)
        dtot = dtot + jnp.sum(red[CHUNK:2 * CHUNK], axis=0, keepdims=True)
        dacs = dacs + red[:CHUNK] + dacs_r.T + jnp.where(rowi == CHUNK - 1, dtot, 0.0)
        r2 = lax.broadcasted_iota(jnp.int32, (CHUNK, CHUNK), 0)
        c2 = lax.broadcasted_iota(jnp.int32, (CHUNK, CHUNK), 1)
        dadt = jnp.dot((c2 >= r2).astype(f32), dacs, preferred_element_type=f32, precision=HIGHEST)
        ddt = red[2 * CHUNK:3 * CHUNK] + dadt * a
        dalog_ref[...] += jnp.sum(dadt * dt, axis=0, keepdims=True) * a
        ddsk_ref[...] += jnp.sum(red[3 * CHUNK:], axis=0, keepdims=True)
        ddtr = jnp.where(lane < H, ddt * _sigmoid(dtr_ref[...] + dtb_ref[...]), 0.0)
        ddtb_ref[...] += jnp.sum(ddtr, axis=0, keepdims=True)
        dzxd_ref[:, SSM_WIDTH + CONV_DIM:] = ddtr.astype(bf16)
        dxa_bm, dxa_cm = [], []
        for g in range(SSM_GROUPS):
            cmb, bmb, _ = groups[g]
            dcbb = dcb[g].astype(bf16)
            dxa_bm.append(dbm[g] + _dot_tn(dcbb, cmb))
            dxa_cm.append(dcm[g] + _dot(dcbb, bmb))
        dxc = jnp.concatenate([dy * dsk_x + dxd * dt_x] + dxa_bm + dxa_cm, axis=1) * (sg * (1.0 + xc * (1.0 - sg)))
        ext = jnp.concatenate([dxc, next_sc[...]], axis=0)
        xin = xbc_ref[...].astype(f32)
        dxbc = cw_ref[SSM_CONV - 1:SSM_CONV, :] * dxc
        dcw = [jnp.sum(dxc * xin, axis=0, keepdims=True)]
        for s in range(1, SSM_CONV):
            later = _sel_dot(_shift_mat(CHUNK, CHUNK + HALO, s), ext, 2)
            dxbc = dxbc + cw_ref[SSM_CONV - 1 - s:SSM_CONV - s, :] * later
            dcw.insert(0, jnp.sum(later * xin, axis=0, keepdims=True))
        dzxd_ref[:, SSM_WIDTH:SSM_WIDTH + CONV_DIM] = dxbc.astype(bf16)
        dcw_ref[...] += jnp.concatenate(dcw, axis=0)
        dcb_ref[...] += jnp.sum(dxc, axis=0, keepdims=True)
        next_sc[...] = dxc[0:HALO, :]

    SUB = SSD_SUB if nc % SSD_SUB == 0 else 1
    nb = nc // SUB

    def body(xbc_ref, xc_ref, sg_ref, y_ref, z_ref, dtr_ref, sprev_ref, dyb_ref, cw_ref, cb_ref, dtb_ref, alog_ref, dsk_ref,
             g_ref, ex_ref, rd_ref, dzxd_ref, dcw_ref, dcb_ref, ddtb_ref, dalog_ref, ddsk_ref, dg_ref, ds_sc, next_sc):
        @pl.when(pl.program_id(0) == 0)
        def _():
            ds_sc[...] = jnp.zeros_like(ds_sc)
            next_sc[...] = jnp.zeros_like(next_sc)
            for r_ in (dcw_ref, dcb_ref, ddtb_ref, dalog_ref, ddsk_ref, dg_ref):
                r_[...] = jnp.zeros_like(r_)

        for k in reversed(range(SUB)):
            rows = slice(k * CHUNK, (k + 1) * CHUNK)
            tok = [_RowsOf(r_, rows) for r_ in (xbc_ref, xc_ref, sg_ref, y_ref, z_ref, dtr_ref)]
            chunk(*tok, sprev_ref.at[k], _RowsOf(dyb_ref, rows), cw_ref, cb_ref, dtb_ref, alog_ref, dsk_ref, g_ref, ex_ref,
                  rd_ref, _RowsOf(dzxd_ref, rows), dcw_ref, dcb_ref, ddtb_ref, dalog_ref, ddsk_ref, dg_ref, ds_sc, next_sc)

    params = [conv_w, conv_b, dt_bias, a_log, d_skip, ssm_norm]
    mats = list(_head_mats())

    def rev(ncols):
        return pl.BlockSpec((SUB * CHUNK, ncols), lambda i: (nb - 1 - i, 0))

    in_specs = ([rev(CONV_DIM), rev(CONV_DIM), rev(CONV_DIM), rev(SSM_WIDTH), rev(SSM_WIDTH), rev(LANES),
                 pl.BlockSpec((SUB, PAIRS, PAIR_W, N), lambda i: (nb - 1 - i, 0, 0, 0)), rev(SSM_WIDTH)]
                + [_full(x.shape) for x in params + mats])
    return pl.pallas_call(
        body, name="ssd_bwd", grid=(nb,), in_specs=in_specs,
        out_specs=[rev(ZXD)] + [_acc(x.shape) for x in params],
        out_shape=[jax.ShapeDtypeStruct((T, ZXD), bf16)] + [jax.ShapeDtypeStruct(x.shape, f32) for x in params],
        scratch_shapes=[pltpu.VMEM((PAIRS, PAIR_W, N), f32), pltpu.VMEM((HALO, CONV_DIM), f32)],
        compiler_params=_params(("arbitrary",)))(xbc, xc, sg, y, z, dtr, sprev, dyb, *params, *mats)


def _tail(h3, p, tgt, gp, wpg, bpg, wpp, gf, tm=512):
    T, D = h3.shape
    tm = min(tm, T)

    def head(gpre, pp, h, gf_, t):
        gate = _sigmoid(gpre)
        y = _rms(h + gate * pp, gf_)
        err = y - t
        return 0.5 * jnp.sum(jnp.mean(err * err, axis=-1))

    def body(h_ref, p_ref, t_ref, gp_ref, wpg_ref, bpg_ref, wpp_ref, gf_ref,
             dh_ref, loss_ref, dgp_ref, dwpg_ref, dbpg_ref, dwpp_ref, dgf_ref):
        @pl.when(pl.program_id(0) == 0)
        def _():
            for r in (loss_ref, dgp_ref, dwpg_ref, dbpg_ref, dwpp_ref, dgf_ref):
                r[...] = jnp.zeros_like(r)

        h = h_ref[...]
        npf, np_vjp = jax.vjp(_rms, h, gp_ref[...])
        npb = npf.astype(bf16)
        pb = p_ref[...].astype(bf16)
        gpre = _dot(npb, wpg_ref[...]) + bpg_ref[...]
        kp, _, cp = wpp_ref.shape
        pp = jnp.concatenate([_dot(pb, wpp_ref[k]) for k in range(kp)], axis=1)
        loss, head_vjp = jax.vjp(head, gpre, pp, h, gf_ref[...], t_ref[...])
        dgpre, dpp, dh_a, dgf, _ = head_vjp(jnp.ones((), f32))
        loss_ref[...] += loss
        dgf_ref[...] += dgf
        dbpg_ref[...] += jnp.sum(dgpre, axis=0, keepdims=True)
        dgb = dgpre.astype(bf16)
        dwpg_ref[...] += _dot_tn(npb, dgb)
        dppb = dpp.astype(bf16)
        for k in range(kp):
            dwpp_ref[k] += _dot_tn(pb, dppb[:, k * cp:(k + 1) * cp])
        dh_b, dgp = np_vjp(_dot_nt(dgb, wpg_ref[...]))
        dgp_ref[...] += dgp
        dh_ref[...] = dh_a + dh_b

    ins = [h3, p, tgt, gp, wpg, bpg, wpp, gf]
    in_specs = [_rows(tm, D), _rows(tm, p.shape[1]), _rows(tm, D)] + [_full(x.shape) for x in ins[3:]]
    acc_shapes = [(1, LANES), gp.shape, wpg.shape, bpg.shape, wpp.shape, gf.shape]
    return pl.pallas_call(
        body, name="tail", grid=(T // tm,), in_specs=in_specs,
        out_specs=[_rows(tm, D)] + [_acc(s) for s in acc_shapes],
        out_shape=[jax.ShapeDtypeStruct((T, D), f32)] + [jax.ShapeDtypeStruct(s, f32) for s in acc_shapes],
        compiler_params=_params(("arbitrary",)))(*ins)


def _adamw(name, w, g, m, v, tr=256):
    R, rest = w.shape[0], w.shape[1:]
    tr = _row_tile(R, tr, 8 if len(rest) == 1 else 1)

    def body(w_ref, g_ref, m_ref, v_ref, d_ref, mo_ref, vo_ref):
        g_ = g_ref[...]
        m_ = ADAM_B1 * m_ref[...] + (1.0 - ADAM_B1) * g_
        v_ = ADAM_B2 * v_ref[...] + (1.0 - ADAM_B2) * jnp.square(g_)
        m_hat = m_ / (1.0 - ADAM_B1 ** ADAM_STEP)
        v_hat = v_ / (1.0 - ADAM_B2 ** ADAM_STEP)
        d_ref[...] = -ADAM_LR * (m_hat / (jnp.sqrt(v_hat) + ADAM_EPS) + ADAM_WD * w_ref[...])
        mo_ref[...] = m_
        vo_ref[...] = v_

    spec = pl.BlockSpec((tr,) + rest, lambda i: (i,) + (0,) * len(rest))
    return pl.pallas_call(body, name=name, grid=(R // tr,), in_specs=[spec] * 4, out_specs=[spec] * 3,
                          out_shape=[jax.ShapeDtypeStruct(w.shape, f32)] * 3,
                          compiler_params=_params(("parallel",)))(w, g, m, v)


HBM = pl.BlockSpec(memory_space=pltpu.HBM)


def _me():
    return lax.axis_index("x"), lax.axis_index("y"), lax.axis_index("c")


def _other_chips(x, y):
    return [(1 - x, y), (x, 1 - y), (1 - x, 1 - y)]


def _remote(src, dst, send_sem, recv_sem, dev):
    return pltpu.make_async_remote_copy(src_ref=src, dst_ref=dst, send_sem=send_sem, recv_sem=recv_sem,
                                        device_id=dev, device_id_type=MESH)


def _sems(n):
    return [pltpu.SemaphoreType.DMA((n,)), pltpu.SemaphoreType.DMA((n,))]


def _gather_weights(shards, split):
    n = len(shards)

    def body(*refs):
        ins, outs = refs[:n], refs[n:2 * n]
        own_send, own_recv, ici_send, ici_recv, d2d_send, d2d_recv, pass_send, pass_recv = refs[2 * n:]
        x, y, c = _me()
        my_chip = 2 * x + y
        sibling = (x, y, 1 - c)
        chips = _other_chips(x, y)

        def rows(i, half):
            hr = shards[i].shape[0] // 2
            return pl.ds(half * hr, hr) if split[i] else pl.ds(0, shards[i].shape[0])

        def piece(i, half, k):
            q = shards[i].shape[0] // 4
            return pl.ds((2 * half + k) * q, q)

        sends = []
        for i in range(n):
            for j, chip in enumerate(chips):
                if split[i] and j == 2:
                    continue
                cp = _remote(ins[i].at[rows(i, c)], outs[i].at[my_chip, rows(i, c)],
                             ici_send.at[3 * i + j], ici_recv.at[3 * i + j], (*chip, c))
                cp.start()
                sends.append(cp)
            cp = _remote(ins[i], outs[i].at[my_chip], own_send.at[i], own_recv.at[i], sibling)
            cp.start()
            sends.append(cp)
        def to_sibling(i, j):
            chip = chips[j]
            land = outs[i].at[2 * chip[0] + chip[1], rows(i, c)]
            cp = _remote(land, land, d2d_send.at[3 * i + j], d2d_recv.at[3 * i + j], sibling)
            cp.start()
            sends.append(cp)

        for i in range(n):
            for j, chip in enumerate(chips):
                if split[i] and j == 2:
                    continue
                s = 3 * i + j
                slab = 2 * chip[0] + chip[1]
                land = outs[i].at[slab, rows(i, c)]
                _remote(land, land, ici_send.at[s], ici_recv.at[s], (*chip, c)).wait_recv()
                if split[i]:
                    part = outs[i].at[slab, piece(i, c, j)]
                    cp = _remote(part, part, pass_send.at[2 * i + j], pass_recv.at[2 * i + j], (*chips[1 - j], c))
                    cp.start()
                    sends.append(cp)
                    to_sibling(i, j)
        for i in range(n):
            if split[i]:
                slab = 2 * chips[2][0] + chips[2][1]
                for k in range(2):
                    part = outs[i].at[slab, piece(i, c, k)]
                    _remote(part, part, pass_send.at[2 * i + k], pass_recv.at[2 * i + k], (*chips[1 - k], c)).wait_recv()
                to_sibling(i, 2)
        for i in range(n):
            _remote(ins[i], outs[i].at[my_chip], own_send.at[i], own_recv.at[i], sibling).wait_recv()
            if split[i]:
                for j, chip in enumerate(chips):
                    s = 3 * i + j
                    land = outs[i].at[2 * chip[0] + chip[1], rows(i, 1 - c)]
                    _remote(land, land, d2d_send.at[s], d2d_recv.at[s], sibling).wait_recv()
        for cp in sends:
            cp.wait_send()

    return pl.pallas_call(
        body, name="gather_weights", out_shape=[jax.ShapeDtypeStruct((N_CHIPS,) + s.shape, s.dtype) for s in shards],
        in_specs=[HBM] * n, out_specs=[HBM] * n,
        scratch_shapes=_sems(n) + _sems(3 * n) + _sems(3 * n) + _sems(2 * n))(*shards)


def _swap_halves(name, grads):
    n = len(grads)

    def body(*refs):
        ins, outs, send, recv = refs[:n], refs[n:2 * n], refs[2 * n], refs[2 * n + 1]
        x, y, c = _me()
        copies = []
        for i in range(n):
            hr = grads[i].shape[1] // 2
            cp = _remote(ins[i].at[:, pl.ds((1 - c) * hr, hr), :], outs[i], send.at[i], recv.at[i], (x, y, 1 - c))
            cp.start()
            copies.append(cp)
        for cp in copies:
            cp.wait()

    return pl.pallas_call(
        body, name=name,
        out_shape=[jax.ShapeDtypeStruct((g.shape[0], g.shape[1] // 2, g.shape[2]), g.dtype) for g in grads],
        in_specs=[HBM] * n, out_specs=[HBM] * n, scratch_shapes=_sems(n))(*grads)


def _add_halves(name, grads, other, c_idx, th=HALF_ROWS_BF16):
    K, R, C = grads.shape
    H = R // 2
    th = _row_tile(H, th, 16)
    nb = H // th

    def body(c_ref, g_ref, o_ref, out_ref):
        out_ref[...] = (g_ref[...].astype(f32) + o_ref[...].astype(f32)).astype(bf16)

    grid_spec = pltpu.PrefetchScalarGridSpec(
        num_scalar_prefetch=1, grid=(nb,),
        in_specs=[pl.BlockSpec((K, th, C), lambda i, c: (0, c[0] * nb + i, 0)),
                  pl.BlockSpec((K, th, C), lambda i, c: (0, i, 0))],
        out_specs=pl.BlockSpec((K, th, C), lambda i, c: (0, i, 0)))
    return pl.pallas_call(body, name=name, grid_spec=grid_spec,
                          out_shape=jax.ShapeDtypeStruct((K, H, C), bf16),
                          compiler_params=_params(("parallel",)))(c_idx, grads, other)


SEM = pl.BlockSpec(memory_space=pltpu.SEMAPHORE)
ANY = pl.BlockSpec(memory_space=pl.ANY)
EFFECT = pltpu.SideEffectType.DATAFLOW_SIDE_EFFECTING


def _copies_start(name, srcs, land_shapes, n_copies, make_copies, after):
    ns, nl = len(srcs), len(land_shapes)
    lands = [lax.empty(s.shape, s.dtype) for s in land_shapes]

    def body(*refs):
        src_refs, land_refs = refs[:ns], refs[ns:ns + nl]
        send, recv, token = refs[ns + nl + 1], refs[ns + nl + 2], refs[-1]
        for cp in make_copies(src_refs, land_refs, send, recv):
            cp.start()
        token[...] = jnp.zeros_like(token)

    buffers = list(srcs) + lands
    out = pl.pallas_call(
        body, name=name,
        out_shape=(pltpu.SemaphoreType.DMA((n_copies,)), pltpu.SemaphoreType.DMA((n_copies,)),
                   *[pltpu.HBM(b.shape, b.dtype) for b in buffers], jax.ShapeDtypeStruct((8, LANES), f32)),
        in_specs=[HBM] * (ns + nl) + [ANY],
        out_specs=(SEM, SEM, *[HBM] * (ns + nl), pl.BlockSpec(memory_space=pltpu.VMEM)),
        input_output_aliases={i: 2 + i for i in range(ns + nl)},
        compiler_params=pltpu.CompilerParams(has_side_effects=EFFECT),
    )(*[pltpu.with_memory_space_constraint(b, pltpu.HBM) for b in buffers], after)
    return out[0], out[1], list(out[2:2 + ns]), list(out[2 + ns:2 + ns + nl]), out[-1]


def _copies_wait(name, started, make_copies, after):
    send, recv, srcs, lands, _ = started
    ns, nl = len(srcs), len(lands)
    after = list(after)

    def body(*refs):
        src_refs, land_refs = refs[:ns], refs[ns:ns + nl]
        for cp in make_copies(src_refs, land_refs, refs[ns + nl], refs[ns + nl + 1]):
            cp.wait_send()
            cp.wait_recv()

    buffers = list(srcs) + list(lands)
    out = pl.pallas_call(
        body, name=name, out_shape=tuple(pltpu.HBM(b.shape, b.dtype) for b in buffers),
        in_specs=[HBM] * (ns + nl) + [SEM, SEM] + [ANY] * len(after), out_specs=tuple([HBM] * (ns + nl)),
        input_output_aliases={i: i for i in range(ns + nl)},
        compiler_params=pltpu.CompilerParams(has_side_effects=EFFECT),
    )(*buffers, send, recv, *after)
    return list(out[:ns]), list(out[ns:])


def _gather_copies(src_refs, land_refs, send, recv):
    x, y, c = _me()
    my_chip = 2 * x + y
    peers = [(*chip, c) for chip in _other_chips(x, y)] + [(x, y, 1 - c)]
    return [_remote(src_refs[i], land_refs[i].at[my_chip], send.at[4 * i + j], recv.at[4 * i + j], peer)
            for i in range(len(src_refs)) for j, peer in enumerate(peers)]


def _swap_copies(src_refs, land_refs, send, recv):
    x, y, c = _me()
    copies = []
    for i in range(len(src_refs)):
        hr = src_refs[i].shape[1] // 2
        copies.append(_remote(src_refs[i].at[:, pl.ds((1 - c) * hr, hr), :], land_refs[i], send.at[i], recv.at[i], (x, y, 1 - c)))
    return copies


def _share_copies(src_refs, land_refs, send, recv):
    x, y, c = _me()
    return [_remote(src_refs[i], land_refs[i], send.at[i], recv.at[i], (x, y, 1 - c)) for i in range(len(src_refs))]


def _partial_copies(src_refs, land_refs, send, recv):
    x, y, c = _me()
    return [_remote(src_refs[i].at[2 * chip[0] + chip[1]], land_refs[i].at[j], send.at[3 * i + j], recv.at[3 * i + j], (*chip, c))
            for i in range(len(src_refs)) for j, chip in enumerate(_other_chips(x, y))]


def _small_copies(src_refs, land_refs, send, recv):
    x, y, c = _me()
    return [_remote(src_refs[0], land_refs[0].at[k - 1], send.at[k - 1], recv.at[k - 1], (x ^ (k >> 2), y ^ ((k >> 1) & 1), c ^ (k & 1)))
            for k in range(1, N_DEV)]


def _sum_small(own, slots, dev_idx):
    R, C = own.shape

    def body(dev_ref, own_ref, s_ref, o_ref):
        me = dev_ref[0]
        acc = jnp.zeros((R, C), f32)
        for d in range(N_DEV):
            k = me ^ d
            acc = acc + jnp.where(k == 0, own_ref[...], s_ref[jnp.maximum(k - 1, 0)])
        o_ref[...] = acc

    grid_spec = pltpu.PrefetchScalarGridSpec(
        num_scalar_prefetch=1, grid=(1,),
        in_specs=[pl.BlockSpec((R, C), lambda i, dev: (0, 0)), pl.BlockSpec((N_DEV - 1, R, C), lambda i, dev: (0, 0, 0))],
        out_specs=pl.BlockSpec((R, C), lambda i, dev: (0, 0)))
    return pl.pallas_call(body, name="sum_small", grid_spec=grid_spec, out_shape=jax.ShapeDtypeStruct((R, C), f32),
                          compiler_params=_params(("arbitrary",)))(dev_idx, own, slots)


def _sum_partials(name, part, recv, chip_idx, th=HALF_ROWS_BF16):
    K, H, C = part.shape
    th = _row_tile(H, th, 16)

    def body(chip_ref, p_ref, r_ref, o_ref):
        acc = p_ref[...].astype(f32)
        for j in range(3):
            acc = acc + r_ref[j].astype(f32)
        o_ref[...] = acc

    grid_spec = pltpu.PrefetchScalarGridSpec(
        num_scalar_prefetch=1, grid=(H // th,),
        in_specs=[pl.BlockSpec((None, th, C), lambda i, chip: (chip[0], i, 0)),
                  pl.BlockSpec((3, th, C), lambda i, chip: (0, i, 0))],
        out_specs=pl.BlockSpec((th, C), lambda i, chip: (i, 0)))
    return pl.pallas_call(body, name=name, grid_spec=grid_spec, out_shape=jax.ShapeDtypeStruct((H, C), f32),
                          compiler_params=_params(("parallel",)))(chip_idx, part, recv)


def _adamw_big(name, w, g_mine, g_theirs, m, v, c_idx, tr=HALF_ROWS_F32):
    R, C = w.shape
    H = R // 2
    tr = _row_tile(H, tr)
    nb = H // tr

    def body(c_ref, w_ref, gm_ref, gt_ref, m_ref, v_ref, g_ref, d_ref, mo_ref, vo_ref):
        g_ = jnp.where(pl.program_id(0) // nb == c_ref[0], gm_ref[...], gt_ref[...])
        g_ref[...] = g_
        m_ = ADAM_B1 * m_ref[...] + (1.0 - ADAM_B1) * g_
        v_ = ADAM_B2 * v_ref[...] + (1.0 - ADAM_B2) * jnp.square(g_)
        m_hat = m_ / (1.0 - ADAM_B1 ** ADAM_STEP)
        v_hat = v_ / (1.0 - ADAM_B2 ** ADAM_STEP)
        d_ref[...] = -ADAM_LR * (m_hat / (jnp.sqrt(v_hat) + ADAM_EPS) + ADAM_WD * w_ref[...])
        mo_ref[...] = m_
        vo_ref[...] = v_

    full = pl.BlockSpec((tr, C), lambda i, c: (i, 0))
    half = pl.BlockSpec((tr, C), lambda i, c: (i % nb, 0))
    grid_spec = pltpu.PrefetchScalarGridSpec(num_scalar_prefetch=1, grid=(2 * nb,),
                                             in_specs=[full, half, half, full, full], out_specs=[full] * 4)
    return pl.pallas_call(body, name=name, grid_spec=grid_spec, out_shape=[jax.ShapeDtypeStruct((R, C), f32)] * 4,
                          compiler_params=_params(("parallel",)))(c_idx, w, g_mine, g_theirs, m, v)


BIG = ("ffn1_w_gate", "ffn1_w_up", "ffn1_w_down", "w_in", "w_out", "ffn2_w_gate", "ffn2_w_up", "ffn2_w_down",
       "ple_w_gate", "ple_w_proj")


SMALL = ("ffn1_norm", "mix_norm", "gm_ln_g", "gm_ln_b", "gm_w_s", "gm_b_s", "gm_out_norm", "conv_b", "dt_bias", "a_log",
         "d_skip", "ssm_norm", "ffn2_norm", "ple_norm", "ple_b_gate", "final_norm")
SMALL_C = 1024


def _pack_small(vals):
    parts = []
    for v in vals:
        f = v.astype(f32).reshape(-1)
        parts.append(jnp.pad(f, (0, -f.shape[0] % SMALL_C)))
    flat = jnp.concatenate(parts)
    rows = flat.shape[0] // SMALL_C
    return jnp.pad(flat, (0, (-rows % 8) * SMALL_C)).reshape(-1, SMALL_C)


def _unpack_small(pack, shapes):
    flat = pack.reshape(-1)
    out, off = [], 0
    for s in shapes:
        n = 1
        for d in s:
            n *= d
        out.append(flat[off:off + n].reshape(s))
        off += n + (-n % SMALL_C)
    return out


def _pad_lanes(v):
    return jnp.pad(v, ((0, 0), (0, LANES - v.shape[1])))


def _pad_rows(a):
    pad = [(0, 0)] * a.ndim
    pad[-2] = (0, -a.shape[-2] % ROW_PAD)
    return jnp.pad(a, pad) if pad[-2][1] else a


FETCH = (("ffn1_w_gate", "ffn1_w_up", "ffn1_w_down"), ("w_in", "conv_w", "w_out"),
         ("ffn2_w_gate", "ffn2_w_up", "ffn2_w_down", "ple_w_gate", "ple_w_proj"))
TRANSPOSED = ("ffn1_w_gate", "ffn1_w_up", "ffn2_w_gate", "ffn2_w_up", "w_in")
ROW_PAD = 32
DONE = (("ffn2_w_gate", "ffn2_w_up", "ffn2_w_down", "w_out", "ple_w_gate", "ple_w_proj"), ("w_in",),
        ("ffn1_w_down", "ffn1_w_up"), ("ffn1_w_gate",))


def _local_step(x, p, tgt, fetch, S, on_grads, on_later):
    G = GM_WIDTH
    K = N_CHIPS
    b_st = S["gm_b_s"][0].T
    w_s = S["gm_w_s"][0]
    dtb, alog, dsk = _pad_lanes(S["dt_bias"]), _pad_lanes(S["a_log"]), _pad_lanes(S["d_skip"])
    gfin = S["final_norm"].reshape(1, -1)

    def rows(a):
        return a.reshape(-1, D_MODEL)

    def shards(a):
        return a.reshape(K, -1, D_MODEL)

    wg1, wu1, wd1 = [rows(a) for a in fetch(0, None)]
    h1, n1, a1, b1 = _ffn_fwd("ffn1_fwd", x, S["ffn1_norm"], wg1, wu1, wd1)
    w_in4, cw4, wo4 = fetch(1, h1)
    w_in = w_in4.reshape(IN_PROJ, D_MODEL)
    w_uv = w_in[:2 * G]
    w_zxd = jnp.pad(w_in[2 * G:], ((0, ZXD - (IN_PROJ - 2 * G)), (0, 0)))
    conv_w = jnp.transpose(cw4, (1, 0, 2)).reshape(SSM_CONV, CONV_DIM)
    wo = wo4.reshape(-1, D_MODEL)
    n2, act, slope, z, xbc, dtr, ya = _mix_fwd(h1, S["mix_norm"], w_uv, w_zxd, S["gm_ln_g"], S["gm_ln_b"], w_s, b_st,
                                               S["gm_out_norm"])
    yb, xc, sg, y_ssd, sprev = _ssd_fwd(xbc, z, dtr, conv_w, S["conv_b"], dtb, alog, dsk, S["ssm_norm"])
    wg2, wu2, wd2, wpg4, wpp4 = fetch(2, yb)
    wg2, wu2, wd2 = rows(wg2), rows(wu2), rows(wd2)
    h2, h3, n3, a2, b2 = _ffn_fwd("ffn2_fwd", h1, S["ffn2_norm"], wg2, wu2, wd2, pre=(ya, yb, wo))
    dh3, loss, dgp, dwpg, dbpg, dwpp, dgf = _tail(h3, p, tgt, S["ple_norm"], wpg4.reshape(-1, D_MODEL), S["ple_b_gate"], wpp4, gfin)
    dh2, da2, db2, hm2, dg_ffn2, dya, dyb = _ffn_bwd("ffn2_bwd", dh3, h2, S["ffn2_norm"], a2, b2, wg2, wu2, wd2, wo=wo, ga=G)
    dw_out = jnp.concatenate([_matmul_tn("dw_out_a", ya, dh2), _matmul_tn("dw_out_b", yb, dh2)], axis=0).reshape(wo4.shape)
    zero = on_grads(0, [shards(_matmul_tn("dw_ffn2_gate", da2, n3)), shards(_matmul_tn("dw_ffn2_up", db2, n3)),
                        shards(_matmul_tn("dw_ffn2_down", hm2, dh3, scale=0.5)), dw_out,
                        dwpg.astype(bf16).reshape(wpg4.shape), dwpp.astype(bf16)])
    dzxd, dcw, dcb, ddtb, dalog, ddsk, dgssm = _ssd_bwd(xbc, xc, sg, y_ssd, z, dtr, sprev, dyb, conv_w, S["conv_b"], dtb, alog, dsk,
                                                        S["ssm_norm"] + zero)
    zero = on_later(0, dgssm)
    dh1, duv, dg_mix, dlng, dlnb, dws, dbst, dgout = _mix_bwd(dh2, h1, S["mix_norm"] + zero, act, slope, dya, dzxd, w_uv, w_zxd,
                                                              S["gm_ln_g"], S["gm_ln_b"], w_s, b_st, S["gm_out_norm"])
    dw_in = jnp.concatenate([_matmul_tn("dw_in_uv", duv, n2), _matmul_tn("dw_in_zxd", dzxd, n2)[:IN_PROJ - 2 * G]], axis=0)
    zero = on_grads(1, [dw_in.reshape(w_in4.shape)])
    dx, da1, db1, hm1, dg_ffn1 = _ffn_bwd("ffn1_bwd", dh1, x, S["ffn1_norm"] + zero, a1, b1, wg1, wu1, wd1)
    loss = loss + on_later(1, dg_ffn1)
    loss = loss + on_grads(2, [shards(_matmul_tn("dw_ffn1_down", hm1, dh1, scale=0.5)), shards(_matmul_tn("dw_ffn1_up", db1, n1))])
    dw_gate1 = _matmul_tn("dw_ffn1_gate", da1, n1)
    loss = loss + on_later(2, dw_gate1) + on_grads(3, [shards(dw_gate1)])
    nh = SSM_HEADS
    gS = {"ffn1_norm": dg_ffn1, "mix_norm": dg_mix, "gm_ln_g": dlng, "gm_ln_b": dlnb, "gm_w_s": dws[None], "gm_b_s": dbst.T[None],
          "gm_out_norm": dgout, "conv_b": dcb, "dt_bias": ddtb[:, :nh], "a_log": dalog[:, :nh], "d_skip": ddsk[:, :nh],
          "ssm_norm": dgssm, "ffn2_norm": dg_ffn2, "ple_norm": dgp, "ple_b_gate": dbpg, "final_norm": dgf.reshape(-1)}
    return loss, dx, dcw, gS


_WEIGHTS = ("ffn1_norm", "ffn1_w_gate", "ffn1_w_up", "ffn1_w_down", "mix_norm", "w_in", "gm_ln_g", "gm_ln_b", "gm_w_s", "gm_b_s",
            "gm_out_norm", "conv_w", "conv_b", "dt_bias", "a_log", "d_skip", "ssm_norm", "w_out", "ffn2_norm", "ffn2_w_gate",
            "ffn2_w_up", "ffn2_w_down", "ple_norm", "ple_w_gate", "ple_b_gate", "ple_w_proj", "final_norm")
_BIG_NAMES = BIG


def kernel(x, p, ffn1_norm, ffn1_w_gate, ffn1_w_up, ffn1_w_down, mix_norm, w_in, gm_ln_g, gm_ln_b, gm_w_s, gm_b_s, gm_out_norm, conv_w, conv_b, dt_bias, a_log, d_skip, ssm_norm, w_out, ffn2_norm, ffn2_w_gate, ffn2_w_up, ffn2_w_down, ple_norm, ple_w_gate, ple_b_gate, ple_w_proj, final_norm, loss_target, m_ffn1_norm, m_ffn1_w_gate, m_ffn1_w_up, m_ffn1_w_down, m_mix_norm, m_w_in, m_gm_ln_g, m_gm_ln_b, m_gm_w_s, m_gm_b_s, m_gm_out_norm, m_conv_w, m_conv_b, m_dt_bias, m_a_log, m_d_skip, m_ssm_norm, m_w_out, m_ffn2_norm, m_ffn2_w_gate, m_ffn2_w_up, m_ffn2_w_down, m_ple_norm, m_ple_w_gate, m_ple_b_gate, m_ple_w_proj, m_final_norm, v_ffn1_norm, v_ffn1_w_gate, v_ffn1_w_up, v_ffn1_w_down, v_mix_norm, v_w_in, v_gm_ln_g, v_gm_ln_b, v_gm_w_s, v_gm_b_s, v_gm_out_norm, v_conv_w, v_conv_b, v_dt_bias, v_a_log, v_d_skip, v_ssm_norm, v_w_out, v_ffn2_norm, v_ffn2_w_gate, v_ffn2_w_up, v_ffn2_w_down, v_ple_norm, v_ple_w_gate, v_ple_b_gate, v_ple_w_proj, v_final_norm):
    given = dict(locals())
    w = {n: given[n] for n in _WEIGHTS}
    m = {n: given["m_" + n] for n in _WEIGHTS}
    v = {n: given["v_" + n] for n in _WEIGHTS}

    c_idx = lax.axis_index("c").astype(jnp.int32).reshape(1)
    chip = 2 * lax.axis_index("x") + lax.axis_index("y")
    chip_idx = chip.astype(jnp.int32).reshape(1)

    shard = {n: (jnp.swapaxes(w[n][0], 0, 1) if n in TRANSPOSED else w[n][0]).astype(bf16) for n in BIG}
    shard["conv_w"] = w["conv_w"][0]
    first = _gather_weights([shard[n] for n in FETCH[0]], [True] * len(FETCH[0]))
    fetching, after = [], first[-1]
    for k in (1, 2):
        srcs = [shard[n] for n in FETCH[k]]
        lands = [jax.ShapeDtypeStruct((N_CHIPS,) + s.shape, s.dtype) for s in srcs]
        fetching.append(_copies_start("gather%d_start" % k, srcs, lands, 4 * len(srcs), _gather_copies, after))
        after = fetching[-1][4]

    def fetch(k, after_):
        return first if k == 0 else _copies_wait("gather%d_wait" % k, fetching[k - 1], _gather_copies, [after_])[1]

    swapping, exchanging = {}, {}

    def exchange(k, grads, others):
        parts = [_add_halves("add_" + n, g_, o_, c_idx) for n, g_, o_ in zip(DONE[k], grads, others)]
        lands = [jax.ShapeDtypeStruct((3,) + p_.shape[1:], p_.dtype) for p_ in parts]
        exchanging[k] = _copies_start("exchange%d_start" % k, parts, lands, 3 * len(parts), _partial_copies, c_idx)
        return exchanging[k][4][0, 0]

    def on_grads(k, grads):
        grads = [_pad_rows(g_) for g_ in grads]
        if k == len(DONE) - 1:
            return exchange(k, grads, _swap_halves("swap%d" % k, grads))
        lands = [jax.ShapeDtypeStruct((g_.shape[0], g_.shape[1] // 2, g_.shape[2]), g_.dtype) for g_ in grads]
        swapping[k] = _copies_start("swap%d_start" % k, grads, lands, len(grads), _swap_copies, c_idx)
        return swapping[k][4][0, 0]

    def on_later(k, after_):
        return exchange(k, *_copies_wait("swap%d_wait" % k, swapping[k], _swap_copies, [after_]))

    S = {n: w[n] for n in SMALL}
    S["ffn1_norm"] = S["ffn1_norm"] + after[0, 0]
    loss, dx, dcw, gS = _local_step(x[0], p[0, 0], loss_target[0], fetch, S, on_grads, on_later)

    small = _pack_small([gS[n] for n in SMALL] + [dcw, loss[:, :1]])
    small_lands = [jax.ShapeDtypeStruct((N_DEV - 1,) + small.shape, small.dtype)]
    small_st = _copies_start("small_start", [small], small_lands, N_DEV - 1, _small_copies, c_idx)

    g, delta, new_m, new_v = {}, {}, {}, {}

    def share_start(k, after_):
        parts, recv = _copies_wait("exchange%d_wait" % k, exchanging[k], _partial_copies, after_)
        mine = [_sum_partials("sum_" + n, p_, r_, chip_idx) for n, p_, r_ in zip(DONE[k], parts, recv)]
        return _copies_start("share%d_start" % k, mine, mine, len(mine), _share_copies, c_idx)

    def update(k, sharing, after_):
        mine, theirs = _copies_wait("share%d_wait" % k, sharing, _share_copies, after_)
        after = []
        for n, gm_, gt_ in zip(DONE[k], mine, theirs):
            flip = (lambda a: jnp.swapaxes(a, 0, 1)) if n in TRANSPOSED else (lambda a: a)
            rows = flip(w[n][0]).shape[0]
            if rows % ROW_PAD:
                def lin(a):
                    return jnp.transpose(a.reshape(-1, LANES, rows), (2, 0, 1))

                def back(a):
                    return jnp.transpose(a, (1, 2, 0)).reshape(1, -1, rows)

                gm_, gt_ = [a.reshape(a.shape[0], -1, LANES) for a in (gm_, gt_)]
                g_ = jnp.where(c_idx[0] == 0, jnp.concatenate([gm_, gt_]), jnp.concatenate([gt_, gm_]))[:rows]
                outs = [g_, *_adamw("adamw_" + n, lin(w[n]), g_, lin(m[n]), lin(v[n]), tr=HALF_ROWS_F32)]
                g[n], delta[n], new_m[n], new_v[n] = [back(o) for o in outs]
            else:
                outs = _adamw_big("adamw_" + n, flip(w[n][0]), gm_, gt_, flip(m[n][0]), flip(v[n][0]), c_idx)
                g[n], delta[n], new_m[n], new_v[n] = [flip(o)[None] for o in outs]
            after.append(outs[3])
        return after

    sharing = [share_start(0, [small_st[4]])]
    after = [sharing[0][4]]
    for k in range(len(DONE)):
        if k + 1 < len(DONE):
            sharing.append(share_start(k + 1, after))
            after = [sharing[k + 1][4]]
        after = update(k, sharing[k], after)
    (own,), (slots,) = _copies_wait("small_wait", small_st, _small_copies, after)
    dev_idx = (2 * chip + lax.axis_index("c")).astype(jnp.int32).reshape(1)
    small_shapes = [w[n].shape for n in SMALL] + [dcw.shape, (1, 1)]
    small_sum = _unpack_small(_sum_small(own, slots, dev_idx), small_shapes)
    g.update({n: small_sum[i] for i, n in enumerate(SMALL)})
    cshard = w["conv_w"].shape[2]
    g["conv_w"] = lax.dynamic_slice_in_dim(small_sum[len(SMALL)], chip * cshard, cshard, axis=1)[None]
    loss_total = small_sum[len(SMALL) + 1].reshape(())
    sm_names = SMALL + ("conv_w",)
    sm_shapes = [w[n].shape for n in sm_names]
    d_s, m_s, v_s = _adamw("adamw_small", _pack_small([w[n] for n in sm_names]), _pack_small([g[n] for n in sm_names]),
                           _pack_small([m[n] for n in sm_names]), _pack_small([v[n] for n in sm_names]))
    for dst, src in ((delta, d_s), (new_m, m_s), (new_v, v_s)):
        for n, val in zip(sm_names, _unpack_small(src, sm_shapes)):
            dst[n] = val

    return (loss_total, dx[None], *[g[n] for n in _WEIGHTS], *[delta[n] for n in _WEIGHTS],
            *[new_m[n] for n in _WEIGHTS], *[new_v[n] for n in _WEIGHTS])
```

```python
import jax
import jax.numpy as jnp
from jax import lax
from jax.experimental import pallas as pl
from jax.experimental.pallas import tpu as pltpu

f32 = jnp.float32
bf16 = jnp.bfloat16
MESH = pl.DeviceIdType.MESH
HIGHEST = lax.Precision.HIGHEST

EPS = 1e-6
N_CHIPS = 4
N_DEV = 8
D_MODEL = 1024
GM_WIDTH = 1024
GM_HEADS = 8
CHUNK = 128
SSM_WIDTH = 1024
SSM_HEADS = 16
SSM_HEAD_DIM = 64
SSM_GROUPS = 2
SSM_STATE = 128
SSM_CONV = 4
CONV_DIM = SSM_WIDTH + 2 * SSM_GROUPS * SSM_STATE
IN_PROJ = 2 * GM_WIDTH + SSM_WIDTH + CONV_DIM + SSM_HEADS
LANES = 128
ZXD = SSM_WIDTH + CONV_DIM + LANES

ADAM_LR = 0.001
ADAM_B1 = 0.9
ADAM_B2 = 0.999
ADAM_EPS = 1e-08
ADAM_WD = 0.01
ADAM_STEP = 10

VMEM_LIMIT = 56 * 1024 * 1024
HALF_ROWS_BF16 = 592
HALF_ROWS_F32 = 320


def _dot(a, b):
    return jnp.dot(a, b, preferred_element_type=f32)


def _dot_nt(a, b):
    return lax.dot_general(a, b, (((1,), (1,)), ((), ())), preferred_element_type=f32)


def _dot_tn(a, b):
    return lax.dot_general(a, b, (((0,), (0,)), ((), ())), preferred_element_type=f32)


def _rms(x, g):
    return x * lax.rsqrt(jnp.mean(x * x, axis=-1, keepdims=True) + EPS) * g


def _layernorm(x, g, b):
    mu = jnp.mean(x, axis=-1, keepdims=True)
    xc = x - mu
    return xc * lax.rsqrt(jnp.mean(xc * xc, axis=-1, keepdims=True) + EPS) * g + b


def _sigmoid(x):
    return 1.0 / (1.0 + jnp.exp(-x))


def _softplus(x):
    return jnp.maximum(x, 0.0) + jnp.log(1.0 + jnp.exp(-jnp.abs(x)))


def _full(shape):
    nd = len(shape)
    return pl.BlockSpec(shape, lambda *_: (0,) * nd, pipeline_mode=pl.Buffered(1))


def _acc(shape):
    nd = len(shape)
    return pl.BlockSpec(shape, lambda *_: (0,) * nd)


def _rows(tm, ncols):
    return pl.BlockSpec((tm, ncols), lambda i: (i, 0))


def _params(sem):
    return pltpu.CompilerParams(dimension_semantics=sem, vmem_limit_bytes=VMEM_LIMIT)


def _row_tile(rows, target, mult=8):
    best = rows
    for t in range(mult, min(rows, target) + 1, mult):
        if rows % t == 0:
            best = t
    return best if best <= target else rows


def _ffn_fwd(name, h, g, wg, wu, wd, pre=None, tm=256):
    T, D = h.shape
    F = wg.shape[0]
    tm = min(tm, T)

    def body(*refs):
        if pre is None:
            h_ref, g_ref, wg_ref, wu_ref, wd_ref, ho_ref, n_ref, a_ref, b_ref = refs
            hin = h_ref[...]
        else:
            (h_ref, ya_ref, yb_ref, wo_ref, g_ref, wg_ref, wu_ref, wd_ref,
             hi_ref, ho_ref, n_ref, a_ref, b_ref) = refs
            ga = ya_ref.shape[1]
            hin = h_ref[...] + _dot(ya_ref[...], wo_ref[:ga, :]) + _dot(yb_ref[...], wo_ref[ga:, :])
            hi_ref[...] = hin
        n = _rms(hin, g_ref[...]).astype(bf16)
        n_ref[...] = n
        a = _dot_nt(n, wg_ref[...]).astype(bf16)
        b = _dot_nt(n, wu_ref[...]).astype(bf16)
        a_ref[...] = a
        b_ref[...] = b
        af = a.astype(f32)
        hm = (af * _sigmoid(af) * b.astype(f32)).astype(bf16)
        ho_ref[...] = hin + 0.5 * _dot(hm, wd_ref[...])

    ins = [h] + (list(pre) if pre is not None else []) + [g, wg, wu, wd]
    in_specs = [_rows(tm, D)]
    if pre is not None:
        in_specs += [_rows(tm, pre[0].shape[1]), _rows(tm, pre[1].shape[1]), _full(pre[2].shape)]
    in_specs += [_full(g.shape), _full(wg.shape), _full(wu.shape), _full(wd.shape)]
    outs = [jax.ShapeDtypeStruct((T, D), f32), jax.ShapeDtypeStruct((T, D), bf16),
            jax.ShapeDtypeStruct((T, F), bf16), jax.ShapeDtypeStruct((T, F), bf16)]
    out_specs = [_rows(tm, D), _rows(tm, D), _rows(tm, F), _rows(tm, F)]
    if pre is not None:
        outs = [jax.ShapeDtypeStruct((T, D), f32)] + outs
        out_specs = [_rows(tm, D)] + out_specs
    return pl.pallas_call(body, name=name, grid=(T // tm,), in_specs=in_specs, out_specs=out_specs,
                          out_shape=outs, compiler_params=_params(("parallel",)))(*ins)


def _ffn_bwd(name, dh, hin, g, a, b, wg, wu, wd, wo=None, ga=0, tm=256):
    T, D = dh.shape
    F = wg.shape[0]
    tm = min(tm, T)

    def body(*refs):
        if wo is None:
            (dh_ref, hin_ref, g_ref, a_ref, b_ref, wg_ref, wu_ref, wd_ref,
             dhi_ref, da_ref, db_ref, hm_ref, dg_ref) = refs
        else:
            (dh_ref, hin_ref, g_ref, a_ref, b_ref, wg_ref, wu_ref, wd_ref, wo_ref,
             dhi_ref, da_ref, db_ref, hm_ref, dg_ref, dya_ref, dyb_ref) = refs

        @pl.when(pl.program_id(0) == 0)
        def _():
            dg_ref[...] = jnp.zeros_like(dg_ref)

        dh_ = dh_ref[...]
        dhb = (0.5 * dh_).astype(bf16)
        dhm = _dot_nt(dhb, wd_ref[...])
        af = a_ref[...].astype(f32)
        bf = b_ref[...].astype(f32)
        sg = _sigmoid(af)
        sl_ = af * sg
        da = (dhm * bf * (sg * (1.0 + af * (1.0 - sg)))).astype(bf16)
        db = (dhm * sl_).astype(bf16)
        da_ref[...] = da
        db_ref[...] = db
        hm_ref[...] = (sl_ * bf).astype(bf16)
        dn = _dot(da, wg_ref[...]) + _dot(db, wu_ref[...])
        _, vjp = jax.vjp(_rms, hin_ref[...], g_ref[...])
        dx, dg = vjp(dn)
        dhi = dh_ + dx
        dhi_ref[...] = dhi
        dg_ref[...] += dg
        if wo is not None:
            dhib = dhi.astype(bf16)
            dya_ref[...] = _dot_nt(dhib, wo_ref[:ga, :]).astype(bf16)
            dyb_ref[...] = _dot_nt(dhib, wo_ref[ga:, :]).astype(bf16)

    ins = [dh, hin, g, a, b, wg, wu, wd]
    in_specs = [_rows(tm, D), _rows(tm, D), _full(g.shape), _rows(tm, F), _rows(tm, F),
                _full(wg.shape), _full(wu.shape), _full(wd.shape)]
    act = jax.ShapeDtypeStruct((T, F), bf16)
    outs = [jax.ShapeDtypeStruct((T, D), f32), act, act, act, jax.ShapeDtypeStruct(g.shape, f32)]
    out_specs = [_rows(tm, D), _rows(tm, F), _rows(tm, F), _rows(tm, F), _acc(g.shape)]
    if wo is not None:
        gb = wo.shape[0] - ga
        ins += [wo]
        in_specs += [_full(wo.shape)]
        outs += [jax.ShapeDtypeStruct((T, ga), bf16), jax.ShapeDtypeStruct((T, gb), bf16)]
        out_specs += [_rows(tm, ga), _rows(tm, gb)]
    return pl.pallas_call(body, name=name, grid=(T // tm,), in_specs=in_specs, out_specs=out_specs,
                          out_shape=outs, compiler_params=_params(("arbitrary",)))(*ins)


def _matmul_tn(name, a, b, scale=1.0, tk=2048):
    T, M = a.shape
    N = b.shape[1]
    tk = min(tk, T)
    nk = T // tk
    tn = LANES * max(d for d in range(1, N // LANES + 1) if (N // LANES) % d == 0 and (d == 1 or M * d * LANES * 4 <= 6 * 1024 * 1024))

    def body(a_ref, b_ref, o_ref, acc):
        k = pl.program_id(1)

        @pl.when(k == 0)
        def _():
            acc[...] = jnp.zeros_like(acc)

        bb = b_ref[...]
        if scale != 1.0:
            bb = bb * scale
        acc[...] += _dot_tn(a_ref[...].astype(bf16), bb.astype(bf16))

        @pl.when(k == nk - 1)
        def _():
            o_ref[...] = acc[...].astype(bf16)

    return pl.pallas_call(
        body, name=name, grid=(N // tn, nk),
        in_specs=[pl.BlockSpec((tk, M), lambda j, k: (k, 0)), pl.BlockSpec((tk, tn), lambda j, k: (k, j))],
        out_specs=pl.BlockSpec((M, tn), lambda j, k: (0, j)),
        out_shape=jax.ShapeDtypeStruct((M, N), bf16), scratch_shapes=[pltpu.VMEM((M, tn), f32)],
        compiler_params=_params(("parallel", "arbitrary")))(a, b)


def _gelu_and_slope(x):
    cdf = 0.5 * (1.0 + lax.erf(x * 0.7071067811865476))
    return x * cdf, cdf + x * (0.3989422804014327 * jnp.exp(-0.5 * x * x))


def _tril_mask():
    r = lax.broadcasted_iota(jnp.int32, (CHUNK, CHUNK), 0)
    c = lax.broadcasted_iota(jnp.int32, (CHUNK, CHUNK), 1)
    return c <= r


def _gm_mix(vnb, ws_ref, bst, mixed_sc, tm):
    mask = _tril_mask()
    for h in range(GM_HEADS):
        wt = jnp.where(mask, ws_ref[h], 0.0).astype(bf16)
        bias = bst[:, h:h + 1]
        for q in range(tm // CHUNK):
            rs = slice(q * CHUNK, (q + 1) * CHUNK)
            cs = slice(h * CHUNK, (h + 1) * CHUNK)
            mixed_sc[rs, cs] = _dot(wt, vnb[rs, cs]) + bias


def _mix_fwd(h1, gmix, w_uv, w_zxd, ln_g, ln_b, w_s, b_st, gout, tm=512):
    T, D = h1.shape
    tm = min(tm, T)
    G = GM_WIDTH

    def body(h_ref, g_ref, wuv_ref, wzxd_ref, lng_ref, lnb_ref, ws_ref, bst_ref, gout_ref,
             n_ref, act_ref, slope_ref, z_ref, xbc_ref, dt_ref, ya_ref, mixed_sc):
        n = _rms(h_ref[...], g_ref[...]).astype(bf16)
        n_ref[...] = n
        uv = _dot_nt(n, wuv_ref[...]).astype(bf16)
        zxd = _dot_nt(n, wzxd_ref[...])
        z_ref[...] = zxd[:, :SSM_WIDTH].astype(bf16)
        xbc_ref[...] = zxd[:, SSM_WIDTH:SSM_WIDTH + CONV_DIM].astype(bf16)
        dt_ref[...] = zxd[:, SSM_WIDTH + CONV_DIM:]
        act, slope = _gelu_and_slope(uv.astype(f32))
        act = act.astype(bf16)
        act_ref[...] = act
        slope_ref[...] = slope.astype(bf16)
        ug, vg = act[:, :G].astype(f32), act[:, G:].astype(f32)
        _gm_mix(_layernorm(vg, lng_ref[...], lnb_ref[...]).astype(bf16), ws_ref, bst_ref[...], mixed_sc, tm)
        ya_ref[...] = _rms(ug * mixed_sc[...], gout_ref[...]).astype(bf16)

    ins = [h1, gmix, w_uv, w_zxd, ln_g, ln_b, w_s, b_st, gout]
    in_specs = [_rows(tm, D)] + [_full(x.shape) for x in ins[1:]]
    outs = [jax.ShapeDtypeStruct((T, D), bf16), jax.ShapeDtypeStruct((T, 2 * G), bf16), jax.ShapeDtypeStruct((T, 2 * G), bf16),
            jax.ShapeDtypeStruct((T, SSM_WIDTH), bf16), jax.ShapeDtypeStruct((T, CONV_DIM), bf16),
            jax.ShapeDtypeStruct((T, LANES), f32), jax.ShapeDtypeStruct((T, G), bf16)]
    out_specs = [_rows(tm, D), _rows(tm, 2 * G), _rows(tm, 2 * G), _rows(tm, SSM_WIDTH), _rows(tm, CONV_DIM), _rows(tm, LANES),
                 _rows(tm, G)]
    return pl.pallas_call(body, name="mix_fwd", grid=(T // tm,), in_specs=in_specs, out_specs=out_specs,
                          out_shape=outs, scratch_shapes=[pltpu.VMEM((tm, G), f32)],
                          compiler_params=_params(("parallel",)))(*ins)


def _mix_bwd(dh, h1, gmix, act, slope, dya, dzxd, w_uv, w_zxd, ln_g, ln_b, w_s, b_st, gout, tm=256):
    T, D = dh.shape
    tm = min(tm, T)
    G = GM_WIDTH

    def body(dh_ref, h_ref, g_ref, act_ref, slope_ref, dya_ref, dzxd_ref, wuv_ref, wzxd_ref, lng_ref, lnb_ref, ws_ref,
             bst_ref, gout_ref, dhi_ref, duv_ref, dg_ref, dlng_ref, dlnb_ref, dws_ref, dbst_ref, dgout_ref, mixed_sc, dvn_sc):
        @pl.when(pl.program_id(0) == 0)
        def _():
            for r in (dg_ref, dlng_ref, dlnb_ref, dws_ref, dbst_ref, dgout_ref):
                r[...] = jnp.zeros_like(r)

        dn_z = _dot(dzxd_ref[...], wzxd_ref[...])
        ug = act_ref[:, :G].astype(f32)
        vn, ln_vjp = jax.vjp(_layernorm, act_ref[:, G:].astype(f32), lng_ref[...], lnb_ref[...])
        vnb = vn.astype(bf16)
        _gm_mix(vnb, ws_ref, bst_ref[...], mixed_sc, tm)
        mixed = mixed_sc[...]
        _, out_vjp = jax.vjp(_rms, ug * mixed, gout_ref[...])
        dpre, dgout = out_vjp(dya_ref[...].astype(f32))
        dgout_ref[...] += dgout
        dug = dpre * mixed
        dmixed = dpre * ug
        mask = _tril_mask()
        lane = lax.broadcasted_iota(jnp.int32, (1, GM_HEADS), 1)
        dbst = jnp.zeros((CHUNK, GM_HEADS), f32)
        for h in range(GM_HEADS):
            wt = jnp.where(mask, ws_ref[h], 0.0).astype(bf16)
            cs = slice(h * CHUNK, (h + 1) * CHUNK)
            dw = jnp.zeros((CHUNK, CHUNK), f32)
            for q in range(tm // CHUNK):
                rs = slice(q * CHUNK, (q + 1) * CHUNK)
                dm = dmixed[rs, cs]
                dmb = dm.astype(bf16)
                dw = dw + _dot_nt(dmb, vnb[rs, cs])
                dbst = dbst + jnp.sum(dm, axis=1, keepdims=True) * (lane == h).astype(f32)
                dvn_sc[rs, cs] = _dot_tn(wt, dmb)
            dws_ref[h] += jnp.where(mask, dw, 0.0)
        dbst_ref[...] += dbst
        dvg, dlng, dlnb = ln_vjp(dvn_sc[...])
        duv = (jnp.concatenate([dug, dvg], axis=1) * slope_ref[...].astype(f32)).astype(bf16)
        duv_ref[...] = duv
        dlng_ref[...] += dlng
        dlnb_ref[...] += dlnb
        dn = dn_z + _dot(duv, wuv_ref[...])
        _, vjp = jax.vjp(_rms, h_ref[...], g_ref[...])
        dx, dg = vjp(dn)
        dhi_ref[...] = dh_ref[...] + dx
        dg_ref[...] += dg

    ins = [dh, h1, gmix, act, slope, dya, dzxd, w_uv, w_zxd, ln_g, ln_b, w_s, b_st, gout]
    in_specs = ([_rows(tm, D), _rows(tm, D), _full(gmix.shape), _rows(tm, 2 * G), _rows(tm, 2 * G), _rows(tm, G),
                 _rows(tm, dzxd.shape[1])] + [_full(x.shape) for x in ins[7:]])
    accs = (gmix, ln_g, ln_b, w_s, b_st, gout)
    outs = ([jax.ShapeDtypeStruct((T, D), f32), jax.ShapeDtypeStruct((T, 2 * G), bf16)]
            + [jax.ShapeDtypeStruct(x.shape, f32) for x in accs])
    out_specs = [_rows(tm, D), _rows(tm, 2 * G)] + [_acc(x.shape) for x in accs]
    return pl.pallas_call(body, name="mix_bwd", grid=(T // tm,), in_specs=in_specs, out_specs=out_specs,
                          out_shape=outs, scratch_shapes=[pltpu.VMEM((tm, G), f32), pltpu.VMEM((tm, G), f32)],
                          compiler_params=_params(("arbitrary",)))(*ins)


HALO = 16
SSD_SUB = 4


class _RowsOf:
    def __init__(self, ref, rows):
        self.ref, self.rows = ref, rows

    def _index(self, idx):
        return (self.rows, slice(None)) if idx is Ellipsis else (self.rows,) + tuple(idx[1:])

    def __getitem__(self, idx):
        return self.ref[self._index(idx)]

    def __setitem__(self, idx, value):
        self.ref[self._index(idx)] = value
PAIRS = SSM_HEADS // 2
PAIR_W = 2 * SSM_HEAD_DIM


def _split(x, n):
    parts = []
    for _ in range(n):
        p = x.astype(bf16)
        parts.append(p)
        x = x - p.astype(f32)
    return parts


def _dot_sel(x, sel_n, n):
    return _dot(jnp.concatenate(_split(x, n), axis=1), sel_n)


def _sel_dot(sel, x, n):
    return _dot(jnp.concatenate([sel] * n, axis=1), jnp.concatenate(_split(x, n), axis=0))


EXPAND_SPLIT = 3
REDUCE_SPLIT = 2


def _head_mats():
    ex = (jnp.arange(SSM_WIDTH)[None, :] // SSM_HEAD_DIM == jnp.arange(LANES)[:, None]).astype(bf16)
    return jnp.tile(ex, (EXPAND_SPLIT, 1)), jnp.tile(ex.T, (REDUCE_SPLIT, 1))


def _shift_mat(rows, cols, off):
    r = lax.broadcasted_iota(jnp.int32, (rows, cols), 0)
    c = lax.broadcasted_iota(jnp.int32, (rows, cols), 1)
    return (c == r + off).astype(bf16)


def _ssd_conv(halo, x, cw_ref, cb_ref):
    ext = jnp.concatenate([halo, x], axis=0)
    xc = cb_ref[...] + cw_ref[SSM_CONV - 1:SSM_CONV, :] * x.astype(f32)
    for j in range(SSM_CONV - 1):
        xc = xc + cw_ref[j:j + 1, :] * _dot(_shift_mat(CHUNK, HALO + CHUNK, HALO - SSM_CONV + 1 + j), ext)
    return xc


def _ssd_front(dtr, dtb_ref, alog_ref):
    dt = _softplus(dtr + dtb_ref[...])
    a = -jnp.exp(alog_ref[...])
    acs = jnp.dot(_tril_mask().astype(f32), dt * a, preferred_element_type=f32, precision=HIGHEST)
    return dt, a, acs


def _ssd_wide(xa, dt, acs, dsk, ex):
    dt_x = _dot_sel(dt, ex, EXPAND_SPLIT)
    acs_x = _dot_sel(acs, ex, EXPAND_SPLIT)
    dsk_x = _dot_sel(jnp.broadcast_to(dsk, (8, LANES)), ex, EXPAND_SPLIT)[0:1]
    e_x = jnp.exp(acs_x)
    r_x = jnp.exp(acs_x[CHUNK - 1:CHUNK, :] - acs_x)
    xs = xa[:, :SSM_WIDTH]
    xd = xs * dt_x
    return dt_x, dsk_x, e_x, r_x, xs, xd, xd * r_x


def _pair_stack(v, lo):
    return jnp.concatenate([jnp.where(lo, v, 0.0), jnp.where(lo, 0.0, v)], axis=0)


def _ssd_pair(j, acs, acs_t, cb):
    out = []
    tril = _tril_mask()
    for h in (2 * j, 2 * j + 1):
        dk = jnp.exp(jnp.where(tril, acs[:, h:h + 1] - acs_t[h:h + 1, :], -jnp.inf))
        out.append((dk, cb * dk))
    return out


def _pair_col(row_lo, tot, j):
    return jnp.exp(jnp.where(row_lo, tot[:, 2 * j:2 * j + 1], tot[:, 2 * j + 1:2 * j + 2]))


def _gated_norm(y, z, g):
    yg = y * (z * _sigmoid(z))
    half = SSM_WIDTH // SSM_GROUPS
    parts = []
    for k in range(SSM_GROUPS):
        s = yg[:, k * half:(k + 1) * half]
        parts.append(s * lax.rsqrt(jnp.mean(s * s, axis=-1, keepdims=True) + EPS))
    return jnp.concatenate(parts, axis=1) * g


def _group_mats(xa):
    out = []
    for g in range(SSM_GROUPS):
        bm = xa[:, SSM_WIDTH + g * SSM_STATE:SSM_WIDTH + (g + 1) * SSM_STATE].astype(bf16)
        cm = xa[:, SSM_WIDTH + (SSM_GROUPS + g) * SSM_STATE:SSM_WIDTH + (SSM_GROUPS + g + 1) * SSM_STATE].astype(bf16)
        out.append((cm, bm, _dot_nt(cm, bm)))
    return out


def _ssd_fwd(xbc, z, dtr, conv_w, conv_b, dt_bias, a_log, d_skip, ssm_norm):
    T = xbc.shape[0]
    nc = T // CHUNK
    N = SSM_STATE

    def body(xbc_ref, halo_ref, z_ref, dtr_ref, cw_ref, cb_ref, dtb_ref, alog_ref, dsk_ref, g_ref, ex_ref,
             yb_ref, xc_ref, sg_ref, y_ref, sprev_ref, s_sc):
        i = pl.program_id(0)

        @pl.when(i == 0)
        def _():
            s_sc[...] = jnp.zeros_like(s_sc)

        lo = lax.broadcasted_iota(jnp.int32, (CHUNK, PAIR_W), 1) < SSM_HEAD_DIM
        row_lo = lax.broadcasted_iota(jnp.int32, (PAIR_W, 1), 0) < SSM_HEAD_DIM
        for k in range(SUB):
            rs = slice(k * CHUNK, (k + 1) * CHUNK)
            if k == 0:
                halo = halo_ref[...]
                halo = jnp.where(i > 0, halo, jnp.zeros_like(halo))
            else:
                halo = xbc_ref[k * CHUNK - HALO:k * CHUNK, :]
            xc = _ssd_conv(halo, xbc_ref[rs, :], cw_ref, cb_ref)
            sg = _sigmoid(xc)
            xc_ref[rs, :] = xc
            sg_ref[rs, :] = sg
            xa = xc * sg
            dt, _, acs = _ssd_front(dtr_ref[rs, :], dtb_ref, alog_ref)
            _, dsk_x, e_x, _, xs, xd, gm = _ssd_wide(xa, dt, acs, dsk_ref[...], ex_ref[...])
            acs_t = acs.T
            tot = acs[CHUNK - 1:CHUNK, :]
            groups = _group_mats(xa)
            ys = []
            for j in range(PAIRS):
                cmb, bmb, cb = groups[j // (PAIRS // SSM_GROUPS)]
                ps = slice(j * PAIR_W, (j + 1) * PAIR_W)
                (_, m0), (_, m1) = _ssd_pair(j, acs, acs_t, cb)
                sp = s_sc[j]
                yd = _dot(jnp.concatenate([m0, m1], axis=1).astype(bf16), _pair_stack(xd[:, ps], lo).astype(bf16))
                ys.append(yd + e_x[:, ps] * _dot_nt(cmb, sp.astype(bf16)))
                sprev_ref[k, j] = sp
                s_sc[j] = _pair_col(row_lo, tot, j) * sp + _dot_tn(gm[:, ps].astype(bf16), bmb)
            y = jnp.concatenate(ys, axis=1) + xs * dsk_x
            y_ref[rs, :] = y
            yb_ref[rs, :] = _gated_norm(y, z_ref[rs, :].astype(f32), g_ref[...]).astype(bf16)

    params = [conv_w, conv_b, dt_bias, a_log, d_skip, ssm_norm, _head_mats()[0]]
    SUB = SSD_SUB if nc % SSD_SUB == 0 else 1
    hp = SUB * CHUNK // HALO
    R = SUB * CHUNK
    in_specs = [_rows(R, CONV_DIM), pl.BlockSpec((HALO, CONV_DIM), lambda i: (jnp.maximum(i * hp - 1, 0), 0)),
                _rows(R, SSM_WIDTH), _rows(R, LANES)] + [_full(x.shape) for x in params]
    return pl.pallas_call(
        body, name="ssd_fwd", grid=(nc // SUB,), in_specs=in_specs,
        out_specs=[_rows(R, SSM_WIDTH), _rows(R, CONV_DIM), _rows(R, CONV_DIM), _rows(R, SSM_WIDTH),
                   pl.BlockSpec((SUB, PAIRS, PAIR_W, N), lambda i: (i, 0, 0, 0))],
        out_shape=[jax.ShapeDtypeStruct((T, SSM_WIDTH), bf16), jax.ShapeDtypeStruct((T, CONV_DIM), f32),
                   jax.ShapeDtypeStruct((T, CONV_DIM), f32), jax.ShapeDtypeStruct((T, SSM_WIDTH), f32),
                   jax.ShapeDtypeStruct((nc, PAIRS, PAIR_W, N), f32)],
        scratch_shapes=[pltpu.VMEM((PAIRS, PAIR_W, N), f32)],
        compiler_params=_params(("arbitrary",)))(xbc, xbc, z, dtr, *params)


def _ssd_bwd(xbc, xc, sg, y, z, dtr, sprev, dyb, conv_w, conv_b, dt_bias, a_log, d_skip, ssm_norm):
    T = xbc.shape[0]
    nc = T // CHUNK
    H, N = SSM_HEADS, SSM_STATE
    PG = PAIRS // SSM_GROUPS

    def chunk(xbc_ref, xc_ref, sg_ref, y_ref, z_ref, dtr_ref, sprev_k, dyb_ref, cw_ref, cb_ref, dtb_ref, alog_ref, dsk_ref,
              g_ref, ex_ref, rd_ref, dzxd_ref, dcw_ref, dcb_ref, ddtb_ref, dalog_ref, ddsk_ref, dg_ref, ds_sc, next_sc):
        xc = xc_ref[...]
        sg = sg_ref[...]
        xa = xc * sg
        dt, a, acs = _ssd_front(dtr_ref[...], dtb_ref, alog_ref)
        dt_x, dsk_x, e_x, r_x, xs, xd, gm = _ssd_wide(xa, dt, acs, dsk_ref[...], ex_ref[...])
        acs_t = acs.T
        tot = acs[CHUNK - 1:CHUNK, :]
        groups = _group_mats(xa)
        lo = lax.broadcasted_iota(jnp.int32, (CHUNK, PAIR_W), 1) < SSM_HEAD_DIM
        row_lo = lax.broadcasted_iota(jnp.int32, (PAIR_W, 1), 0) < SSM_HEAD_DIM
        pairs, zs = [], []
        for j in range(PAIRS):
            cmb, _, cb = groups[j // PG]
            pairs.append(_ssd_pair(j, acs, acs_t, cb))
            zs.append(_dot_nt(cmb, sprev_k[j].astype(bf16)))
        zf = jnp.concatenate(zs, axis=1)
        _, gn_vjp = jax.vjp(_gated_norm, y_ref[...], z_ref[...].astype(f32), g_ref[...])
        dy, dz, dg = gn_vjp(dyb_ref[...].astype(f32))
        dg_ref[...] += dg
        dzxd_ref[:, :SSM_WIDTH] = dz.astype(bf16)

        lane = lax.broadcasted_iota(jnp.int32, (1, LANES), 1)
        sub = lax.broadcasted_iota(jnp.int32, (LANES, 1), 0)
        dacs = jnp.zeros((CHUNK, LANES), f32)
        dacs_r = jnp.zeros((LANES, CHUNK), f32)
        dtot = jnp.zeros((1, LANES), f32)
        dcb = [jnp.zeros((CHUNK, CHUNK), f32) for _ in range(SSM_GROUPS)]
        dcm = [jnp.zeros((CHUNK, N), f32) for _ in range(SSM_GROUPS)]
        dbm = [jnp.zeros((CHUNK, N), f32) for _ in range(SSM_GROUPS)]
        dxds, dgms = [], []
        for j in range(PAIRS):
            g = j // PG
            cmb, bmb, _ = groups[g]
            ps = slice(j * PAIR_W, (j + 1) * PAIR_W)
            (dk0, m0), (dk1, m1) = pairs[j]
            oh0, oh1 = (lane == 2 * j).astype(f32), (lane == 2 * j + 1).astype(f32)
            dyp = dy[:, ps]
            dy2 = _pair_stack(dyp, lo).astype(bf16)
            dm2 = _dot_nt(dy2, xd[:, ps].astype(bf16))
            m2 = jnp.concatenate([m0, m1], axis=0)
            dxds.append(_dot_tn(m2.astype(bf16), dy2))
            w2 = dm2 * m2
            rs = jnp.sum(w2, axis=1, keepdims=True)
            dacs = dacs + rs[:CHUNK] * oh0 + rs[CHUNK:] * oh1
            dacs_r = dacs_r - ((sub == 2 * j).astype(f32) * jnp.sum(w2[:CHUNK], axis=0, keepdims=True)
                               + (sub == 2 * j + 1).astype(f32) * jnp.sum(w2[CHUNK:], axis=0, keepdims=True))
            dcb[g] = dcb[g] + dm2[:CHUNK] * dk0 + dm2[CHUNK:] * dk1
            sp = sprev_k[j]
            dzb = (dyp * e_x[:, ps]).astype(bf16)
            dcm[g] = dcm[g] + _dot(dzb, sp.astype(bf16))
            dsn = ds_sc[j]
            dsnb = dsn.astype(bf16)
            et = _pair_col(row_lo, tot, j)
            rr = jnp.sum(dsn * sp, axis=1, keepdims=True) * et
            dtot = dtot + jnp.sum(rr[:SSM_HEAD_DIM]) * oh0 + jnp.sum(rr[SSM_HEAD_DIM:]) * oh1
            dgms.append(_dot_nt(bmb, dsnb))
            dbm[g] = dbm[g] + _dot(gm[:, ps].astype(bf16), dsnb)
            ds_sc[j] = _dot_tn(dzb, cmb) + et * dsn
        dgm = jnp.concatenate(dgms, axis=1)
        dxd = jnp.concatenate(dxds, axis=1) + dgm * r_x
        dr = dgm * gm
        red = _dot_sel(jnp.concatenate([dy * e_x * zf - dr, dr, dxd * xs, dy * xs], axis=0), rd_ref[...], REDUCE_SPLIT)
        rowi = lax.broadcasted_iota(jnp.int32, (CHUNK, 1), 0)
        dtot = dtot + jnp.sum(red[CHUNK:2 * CHUNK], axis=0, keepdims=True)
        dacs = dacs + red[:CHUNK] + dacs_r.T + jnp.where(rowi == CHUNK - 1, dtot, 0.0)
        r2 = lax.broadcasted_iota(jnp.int32, (CHUNK, CHUNK), 0)
        c2 = lax.broadcasted_iota(jnp.int32, (CHUNK, CHUNK), 1)
        dadt = jnp.dot((c2 >= r2).astype(f32), dacs, preferred_element_type=f32, precision=HIGHEST)
        ddt = red[2 * CHUNK:3 * CHUNK] + dadt * a
        dalog_ref[...] += jnp.sum(dadt * dt, axis=0, keepdims=True) * a
        ddsk_ref[...] += jnp.sum(red[3 * CHUNK:], axis=0, keepdims=True)
        ddtr = jnp.where(lane < H, ddt * _sigmoid(dtr_ref[...] + dtb_ref[...]), 0.0)
        ddtb_ref[...] += jnp.sum(ddtr, axis=0, keepdims=True)
        dzxd_ref[:, SSM_WIDTH + CONV_DIM:] = ddtr.astype(bf16)
        dxa_bm, dxa_cm = [], []
        for g in range(SSM_GROUPS):
            cmb, bmb, _ = groups[g]
            dcbb = dcb[g].astype(bf16)
            dxa_bm.append(dbm[g] + _dot_tn(dcbb, cmb))
            dxa_cm.append(dcm[g] + _dot(dcbb, bmb))
        dxc = jnp.concatenate([dy * dsk_x + dxd * dt_x] + dxa_bm + dxa_cm, axis=1) * (sg * (1.0 + xc * (1.0 - sg)))
        ext = jnp.concatenate([dxc, next_sc[...]], axis=0)
        xin = xbc_ref[...].astype(f32)
        dxbc = cw_ref[SSM_CONV - 1:SSM_CONV, :] * dxc
        dcw = [jnp.sum(dxc * xin, axis=0, keepdims=True)]
        for s in range(1, SSM_CONV):
            later = _sel_dot(_shift_mat(CHUNK, CHUNK + HALO, s), ext, 2)
            dxbc = dxbc + cw_ref[SSM_CONV - 1 - s:SSM_CONV - s, :] * later
            dcw.insert(0, jnp.sum(later * xin, axis=0, keepdims=True))
        dzxd_ref[:, SSM_WIDTH:SSM_WIDTH + CONV_DIM] = dxbc.astype(bf16)
        dcw_ref[...] += jnp.concatenate(dcw, axis=0)
        dcb_ref[...] += jnp.sum(dxc, axis=0, keepdims=True)
        next_sc[...] = dxc[0:HALO, :]

    SUB = SSD_SUB if nc % SSD_SUB == 0 else 1
    nb = nc // SUB

    def body(xbc_ref, xc_ref, sg_ref, y_ref, z_ref, dtr_ref, sprev_ref, dyb_ref, cw_ref, cb_ref, dtb_ref, alog_ref, dsk_ref,
             g_ref, ex_ref, rd_ref, dzxd_ref, dcw_ref, dcb_ref, ddtb_ref, dalog_ref, ddsk_ref, dg_ref, ds_sc, next_sc):
        @pl.when(pl.program_id(0) == 0)
        def _():
            ds_sc[...] = jnp.zeros_like(ds_sc)
            next_sc[...] = jnp.zeros_like(next_sc)
            for r_ in (dcw_ref, dcb_ref, ddtb_ref, dalog_ref, ddsk_ref, dg_ref):
                r_[...] = jnp.zeros_like(r_)

        for k in reversed(range(SUB)):
            rows = slice(k * CHUNK, (k + 1) * CHUNK)
            tok = [_RowsOf(r_, rows) for r_ in (xbc_ref, xc_ref, sg_ref, y_ref, z_ref, dtr_ref)]
            chunk(*tok, sprev_ref.at[k], _RowsOf(dyb_ref, rows), cw_ref, cb_ref, dtb_ref, alog_ref, dsk_ref, g_ref, ex_ref,
                  rd_ref, _RowsOf(dzxd_ref, rows), dcw_ref, dcb_ref, ddtb_ref, dalog_ref, ddsk_ref, dg_ref, ds_sc, next_sc)

    params = [conv_w, conv_b, dt_bias, a_log, d_skip, ssm_norm]
    mats = list(_head_mats())

    def rev(ncols):
        return pl.BlockSpec((SUB * CHUNK, ncols), lambda i: (nb - 1 - i, 0))

    in_specs = ([rev(CONV_DIM), rev(CONV_DIM), rev(CONV_DIM), rev(SSM_WIDTH), rev(SSM_WIDTH), rev(LANES),
                 pl.BlockSpec((SUB, PAIRS, PAIR_W, N), lambda i: (nb - 1 - i, 0, 0, 0)), rev(SSM_WIDTH)]
                + [_full(x.shape) for x in params + mats])
    return pl.pallas_call(
        body, name="ssd_bwd", grid=(nb,), in_specs=in_specs,
        out_specs=[rev(ZXD)] + [_acc(x.shape) for x in params],
        out_shape=[jax.ShapeDtypeStruct((T, ZXD), bf16)] + [jax.ShapeDtypeStruct(x.shape, f32) for x in params],
        scratch_shapes=[pltpu.VMEM((PAIRS, PAIR_W, N), f32), pltpu.VMEM((HALO, CONV_DIM), f32)],
        compiler_params=_params(("arbitrary",)))(xbc, xc, sg, y, z, dtr, sprev, dyb, *params, *mats)


def _tail(h3, p, tgt, gp, wpg, bpg, wpp, gf, tm=512):
    T, D = h3.shape
    tm = min(tm, T)

    def head(gpre, pp, h, gf_, t):
        gate = _sigmoid(gpre)
        y = _rms(h + gate * pp, gf_)
        err = y - t
        return 0.5 * jnp.sum(jnp.mean(err * err, axis=-1))

    def body(h_ref, p_ref, t_ref, gp_ref, wpg_ref, bpg_ref, wpp_ref, gf_ref,
             dh_ref, loss_ref, dgp_ref, dwpg_ref, dbpg_ref, dwpp_ref, dgf_ref):
        @pl.when(pl.program_id(0) == 0)
        def _():
            for r in (loss_ref, dgp_ref, dwpg_ref, dbpg_ref, dwpp_ref, dgf_ref):
                r[...] = jnp.zeros_like(r)

        h = h_ref[...]
        npf, np_vjp = jax.vjp(_rms, h, gp_ref[...])
        npb = npf.astype(bf16)
        pb = p_ref[...].astype(bf16)
        gpre = _dot(npb, wpg_ref[...]) + bpg_ref[...]
        kp, _, cp = wpp_ref.shape
        pp = jnp.concatenate([_dot(pb, wpp_ref[k]) for k in range(kp)], axis=1)
        loss, head_vjp = jax.vjp(head, gpre, pp, h, gf_ref[...], t_ref[...])
        dgpre, dpp, dh_a, dgf, _ = head_vjp(jnp.ones((), f32))
        loss_ref[...] += loss
        dgf_ref[...] += dgf
        dbpg_ref[...] += jnp.sum(dgpre, axis=0, keepdims=True)
        dgb = dgpre.astype(bf16)
        dwpg_ref[...] += _dot_tn(npb, dgb)
        dppb = dpp.astype(bf16)
        for k in range(kp):
            dwpp_ref[k] += _dot_tn(pb, dppb[:, k * cp:(k + 1) * cp])
        dh_b, dgp = np_vjp(_dot_nt(dgb, wpg_ref[...]))
        dgp_ref[...] += dgp
        dh_ref[...] = dh_a + dh_b

    ins = [h3, p, tgt, gp, wpg, bpg, wpp, gf]
    in_specs = [_rows(tm, D), _rows(tm, p.shape[1]), _rows(tm, D)] + [_full(x.shape) for x in ins[3:]]
    acc_shapes = [(1, LANES), gp.shape, wpg.shape, bpg.shape, wpp.shape, gf.shape]
    return pl.pallas_call(
        body, name="tail", grid=(T // tm,), in_specs=in_specs,
        out_specs=[_rows(tm, D)] + [_acc(s) for s in acc_shapes],
        out_shape=[jax.ShapeDtypeStruct((T, D), f32)] + [jax.ShapeDtypeStruct(s, f32) for s in acc_shapes],
        compiler_params=_params(("arbitrary",)))(*ins)


def _adamw(name, w, g, m, v, tr=256):
    R, rest = w.shape[0], w.shape[1:]
    tr = _row_tile(R, tr, 8 if len(rest) == 1 else 1)

    def body(w_ref, g_ref, m_ref, v_ref, d_ref, mo_ref, vo_ref):
        g_ = g_ref[...]
        m_ = ADAM_B1 * m_ref[...] + (1.0 - ADAM_B1) * g_
        v_ = ADAM_B2 * v_ref[...] + (1.0 - ADAM_B2) * jnp.square(g_)
        m_hat = m_ / (1.0 - ADAM_B1 ** ADAM_STEP)
        v_hat = v_ / (1.0 - ADAM_B2 ** ADAM_STEP)
        d_ref[...] = -ADAM_LR * (m_hat / (jnp.sqrt(v_hat) + ADAM_EPS) + ADAM_WD * w_ref[...])
        mo_ref[...] = m_
        vo_ref[...] = v_

    spec = pl.BlockSpec((tr,) + rest, lambda i: (i,) + (0,) * len(rest))
    return pl.pallas_call(body, name=name, grid=(R // tr,), in_specs=[spec] * 4, out_specs=[spec] * 3,
                          out_shape=[jax.ShapeDtypeStruct(w.shape, f32)] * 3,
                          compiler_params=_params(("parallel",)))(w, g, m, v)


HBM = pl.BlockSpec(memory_space=pltpu.HBM)


def _me():
    return lax.axis_index("x"), lax.axis_index("y"), lax.axis_index("c")


def _other_chips(x, y):
    return [(1 - x, y), (x, 1 - y), (1 - x, 1 - y)]


def _remote(src, dst, send_sem, recv_sem, dev):
    return pltpu.make_async_remote_copy(src_ref=src, dst_ref=dst, send_sem=send_sem, recv_sem=recv_sem,
                                        device_id=dev, device_id_type=MESH)


def _sems(n):
    return [pltpu.SemaphoreType.DMA((n,)), pltpu.SemaphoreType.DMA((n,))]


def _gather_weights(shards, split):
    n = len(shards)

    def body(*refs):
        ins, outs = refs[:n], refs[n:2 * n]
        own_send, own_recv, ici_send, ici_recv, d2d_send, d2d_recv, pass_send, pass_recv = refs[2 * n:]
        x, y, c = _me()
        my_chip = 2 * x + y
        sibling = (x, y, 1 - c)
        chips = _other_chips(x, y)

        def rows(i, half):
            hr = shards[i].shape[0] // 2
            return pl.ds(half * hr, hr) if split[i] else pl.ds(0, shards[i].shape[0])

        def piece(i, half, k):
            q = shards[i].shape[0] // 4
            return pl.ds((2 * half + k) * q, q)

        sends = []
        for i in range(n):
            for j, chip in enumerate(chips):
                if split[i] and j == 2:
                    continue
                cp = _remote(ins[i].at[rows(i, c)], outs[i].at[my_chip, rows(i, c)],
                             ici_send.at[3 * i + j], ici_recv.at[3 * i + j], (*chip, c))
                cp.start()
                sends.append(cp)
            cp = _remote(ins[i], outs[i].at[my_chip], own_send.at[i], own_recv.at[i], sibling)
            cp.start()
            sends.append(cp)
        def to_sibling(i, j):
            chip = chips[j]
            land = outs[i].at[2 * chip[0] + chip[1], rows(i, c)]
            cp = _remote(land, land, d2d_send.at[3 * i + j], d2d_recv.at[3 * i + j], sibling)
            cp.start()
            sends.append(cp)

        for i in range(n):
            for j, chip in enumerate(chips):
                if split[i] and j == 2:
                    continue
                s = 3 * i + j
                slab = 2 * chip[0] + chip[1]
                land = outs[i].at[slab, rows(i, c)]
                _remote(land, land, ici_send.at[s], ici_recv.at[s], (*chip, c)).wait_recv()
                if split[i]:
                    part = outs[i].at[slab, piece(i, c, j)]
                    cp = _remote(part, part, pass_send.at[2 * i + j], pass_recv.at[2 * i + j], (*chips[1 - j], c))
                    cp.start()
                    sends.append(cp)
                    to_sibling(i, j)
        for i in range(n):
            if split[i]:
                slab = 2 * chips[2][0] + chips[2][1]
                for k in range(2):
                    part = outs[i].at[slab, piece(i, c, k)]
                    _remote(part, part, pass_send.at[2 * i + k], pass_recv.at[2 * i + k], (*chips[1 - k], c)).wait_recv()
                to_sibling(i, 2)
        for i in range(n):
            _remote(ins[i], outs[i].at[my_chip], own_send.at[i], own_recv.at[i], sibling).wait_recv()
            if split[i]:
                for j, chip in enumerate(chips):
                    s = 3 * i + j
                    land = outs[i].at[2 * chip[0] + chip[1], rows(i, 1 - c)]
                    _remote(land, land, d2d_send.at[s], d2d_recv.at[s], sibling).wait_recv()
        for cp in sends:
            cp.wait_send()

    return pl.pallas_call(
        body, name="gather_weights", out_shape=[jax.ShapeDtypeStruct((N_CHIPS,) + s.shape, s.dtype) for s in shards],
        in_specs=[HBM] * n, out_specs=[HBM] * n,
        scratch_shapes=_sems(n) + _sems(3 * n) + _sems(3 * n) + _sems(2 * n))(*shards)


def _add_halves(name, grads, other, c_idx, th=HALF_ROWS_BF16):
    K, R, C = grads.shape
    H = R // 2
    th = _row_tile(H, th, 16)
    nb = H // th

    def body(c_ref, g_ref, o_ref, out_ref):
        out_ref[...] = (g_ref[...].astype(f32) + o_ref[...].astype(f32)).astype(bf16)

    grid_spec = pltpu.PrefetchScalarGridSpec(
        num_scalar_prefetch=1, grid=(nb,),
        in_specs=[pl.BlockSpec((K, th, C), lambda i, c: (0, c[0] * nb + i, 0)),
                  pl.BlockSpec((K, th, C), lambda i, c: (0, i, 0))],
        out_specs=pl.BlockSpec((K, th, C), lambda i, c: (0, i, 0)))
    return pl.pallas_call(body, name=name, grid_spec=grid_spec,
                          out_shape=jax.ShapeDtypeStruct((K, H, C), bf16),
                          compiler_params=_params(("parallel",)))(c_idx, grads, other)


SEM = pl.BlockSpec(memory_space=pltpu.SEMAPHORE)
ANY = pl.BlockSpec(memory_space=pl.ANY)
EFFECT = pltpu.SideEffectType.DATAFLOW_SIDE_EFFECTING


def _copies_start(name, srcs, land_shapes, n_copies, make_copies, after):
    ns, nl = len(srcs), len(land_shapes)
    lands = [lax.empty(s.shape, s.dtype) for s in land_shapes]

    def body(*refs):
        src_refs, land_refs = refs[:ns], refs[ns:ns + nl]
        send, recv, token = refs[ns + nl + 1], refs[ns + nl + 2], refs[-1]
        for cp in make_copies(src_refs, land_refs, send, recv):
            cp.start()
        token[...] = jnp.zeros_like(token)

    buffers = list(srcs) + lands
    out = pl.pallas_call(
        body, name=name,
        out_shape=(pltpu.SemaphoreType.DMA((n_copies,)), pltpu.SemaphoreType.DMA((n_copies,)),
                   *[pltpu.HBM(b.shape, b.dtype) for b in buffers], jax.ShapeDtypeStruct((8, LANES), f32)),
        in_specs=[HBM] * (ns + nl) + [ANY],
        out_specs=(SEM, SEM, *[HBM] * (ns + nl), pl.BlockSpec(memory_space=pltpu.VMEM)),
        input_output_aliases={i: 2 + i for i in range(ns + nl)},
        compiler_params=pltpu.CompilerParams(has_side_effects=EFFECT),
    )(*[pltpu.with_memory_space_constraint(b, pltpu.HBM) for b in buffers], after)
    return out[0], out[1], list(out[2:2 + ns]), list(out[2 + ns:2 + ns + nl]), out[-1]


def _copies_wait(name, started, make_copies, after):
    send, recv, srcs, lands, _ = started
    ns, nl = len(srcs), len(lands)
    after = list(after)

    def body(*refs):
        src_refs, land_refs = refs[:ns], refs[ns:ns + nl]
        for cp in make_copies(src_refs, land_refs, refs[ns + nl], refs[ns + nl + 1]):
            cp.wait_send()
            cp.wait_recv()

    buffers = list(srcs) + list(lands)
    out = pl.pallas_call(
        body, name=name, out_shape=tuple(pltpu.HBM(b.shape, b.dtype) for b in buffers),
        in_specs=[HBM] * (ns + nl) + [SEM, SEM] + [ANY] * len(after), out_specs=tuple([HBM] * (ns + nl)),
        input_output_aliases={i: i for i in range(ns + nl)},
        compiler_params=pltpu.CompilerParams(has_side_effects=EFFECT),
    )(*buffers, send, recv, *after)
    return list(out[:ns]), list(out[ns:])


def _gather_copies(src_refs, land_refs, send, recv):
    x, y, c = _me()
    my_chip = 2 * x + y
    peers = [(*chip, c) for chip in _other_chips(x, y)] + [(x, y, 1 - c)]
    return [_remote(src_refs[i], land_refs[i].at[my_chip], send.at[4 * i + j], recv.at[4 * i + j], peer)
            for i in range(len(src_refs)) for j, peer in enumerate(peers)]


def _swap_copies(src_refs, land_refs, send, recv):
    x, y, c = _me()
    copies = []
    for i in range(len(src_refs)):
        hr = src_refs[i].shape[1] // 2
        copies.append(_remote(src_refs[i].at[:, pl.ds((1 - c) * hr, hr), :], land_refs[i], send.at[i], recv.at[i], (x, y, 1 - c)))
    return copies


def _share_copies(src_refs, land_refs, send, recv):
    x, y, c = _me()
    return [_remote(src_refs[i], land_refs[i], send.at[i], recv.at[i], (x, y, 1 - c)) for i in range(len(src_refs))]


def _partial_copies(src_refs, land_refs, send, recv):
    x, y, c = _me()
    return [_remote(src_refs[i].at[2 * chip[0] + chip[1]], land_refs[i].at[j], send.at[3 * i + j], recv.at[3 * i + j], (*chip, c))
            for i in range(len(src_refs)) for j, chip in enumerate(_other_chips(x, y))]


def _small_copies(src_refs, land_refs, send, recv):
    x, y, c = _me()
    return [_remote(src_refs[0], land_refs[0].at[k - 1], send.at[k - 1], recv.at[k - 1], (x ^ (k >> 2), y ^ ((k >> 1) & 1), c ^ (k & 1)))
            for k in range(1, N_DEV)]


def _sum_small(own, slots, dev_idx):
    R, C = own.shape

    def body(dev_ref, own_ref, s_ref, o_ref):
        me = dev_ref[0]
        acc = jnp.zeros((R, C), f32)
        for d in range(N_DEV):
            k = me ^ d
            acc = acc + jnp.where(k == 0, own_ref[...], s_ref[jnp.maximum(k - 1, 0)])
        o_ref[...] = acc

    grid_spec = pltpu.PrefetchScalarGridSpec(
        num_scalar_prefetch=1, grid=(1,),
        in_specs=[pl.BlockSpec((R, C), lambda i, dev: (0, 0)), pl.BlockSpec((N_DEV - 1, R, C), lambda i, dev: (0, 0, 0))],
        out_specs=pl.BlockSpec((R, C), lambda i, dev: (0, 0)))
    return pl.pallas_call(body, name="sum_small", grid_spec=grid_spec, out_shape=jax.ShapeDtypeStruct((R, C), f32),
                          compiler_params=_params(("arbitrary",)))(dev_idx, own, slots)


def _sum_partials(name, part, recv, chip_idx, th=HALF_ROWS_BF16):
    K, H, C = part.shape
    th = _row_tile(H, th, 16)

    def body(chip_ref, p_ref, r_ref, o_ref):
        acc = p_ref[...].astype(f32)
        for j in range(3):
            acc = acc + r_ref[j].astype(f32)
        o_ref[...] = acc

    grid_spec = pltpu.PrefetchScalarGridSpec(
        num_scalar_prefetch=1, grid=(H // th,),
        in_specs=[pl.BlockSpec((None, th, C), lambda i, chip: (chip[0], i, 0)),
                  pl.BlockSpec((3, th, C), lambda i, chip: (0, i, 0))],
        out_specs=pl.BlockSpec((th, C), lambda i, chip: (i, 0)))
    return pl.pallas_call(body, name=name, grid_spec=grid_spec, out_shape=jax.ShapeDtypeStruct((H, C), f32),
                          compiler_params=_params(("parallel",)))(chip_idx, part, recv)


def _adamw_big(name, w, g_mine, g_theirs, m, v, c_idx, tr=HALF_ROWS_F32):
    R, C = w.shape
    H = R // 2
    tr = _row_tile(H, tr)
    nb = H // tr

    def body(c_ref, w_ref, gm_ref, gt_ref, m_ref, v_ref, g_ref, d_ref, mo_ref, vo_ref):
        g_ = jnp.where(pl.program_id(0) // nb == c_ref[0], gm_ref[...], gt_ref[...])
        g_ref[...] = g_
        m_ = ADAM_B1 * m_ref[...] + (1.0 - ADAM_B1) * g_
        v_ = ADAM_B2 * v_ref[...] + (1.0 - ADAM_B2) * jnp.square(g_)
        m_hat = m_ / (1.0 - ADAM_B1 ** ADAM_STEP)
        v_hat = v_ / (1.0 - ADAM_B2 ** ADAM_STEP)
        d_ref[...] = -ADAM_LR * (m_hat / (jnp.sqrt(v_hat) + ADAM_EPS) + ADAM_WD * w_ref[...])
        mo_ref[...] = m_
        vo_ref[...] = v_

    full = pl.BlockSpec((tr, C), lambda i, c: (i, 0))
    half = pl.BlockSpec((tr, C), lambda i, c: (i % nb, 0))
    grid_spec = pltpu.PrefetchScalarGridSpec(num_scalar_prefetch=1, grid=(2 * nb,),
                                             in_specs=[full, half, half, full, full], out_specs=[full] * 4)
    return pl.pallas_call(body, name=name, grid_spec=grid_spec, out_shape=[jax.ShapeDtypeStruct((R, C), f32)] * 4,
                          compiler_params=_params(("parallel",)))(c_idx, w, g_mine, g_theirs, m, v)


BIG = ("ffn1_w_gate", "ffn1_w_up", "ffn1_w_down", "w_in", "w_out", "ffn2_w_gate", "ffn2_w_up", "ffn2_w_down",
       "ple_w_gate", "ple_w_proj")


SMALL = ("ffn1_norm", "mix_norm", "gm_ln_g", "gm_ln_b", "gm_w_s", "gm_b_s", "gm_out_norm", "conv_b", "dt_bias", "a_log",
         "d_skip", "ssm_norm", "ffn2_norm", "ple_norm", "ple_b_gate", "final_norm")
SMALL_C = 1024


def _pack_small(vals):
    parts = []
    for v in vals:
        f = v.astype(f32).reshape(-1)
        parts.append(jnp.pad(f, (0, -f.shape[0] % SMALL_C)))
    flat = jnp.concatenate(parts)
    rows = flat.shape[0] // SMALL_C
    return jnp.pad(flat, (0, (-rows % 8) * SMALL_C)).reshape(-1, SMALL_C)


def _unpack_small(pack, shapes):
    flat = pack.reshape(-1)
    out, off = [], 0
    for s in shapes:
        n = 1
        for d in s:
            n *= d
        out.append(flat[off:off + n].reshape(s))
        off += n + (-n % SMALL_C)
    return out


def _pad_lanes(v):
    return jnp.pad(v, ((0, 0), (0, LANES - v.shape[1])))


def _pad_rows(a):
    pad = [(0, 0)] * a.ndim
    pad[-2] = (0, -a.shape[-2] % ROW_PAD)
    return jnp.pad(a, pad) if pad[-2][1] else a


FETCH = (("ffn1_w_gate", "ffn1_w_up", "ffn1_w_down"), ("w_in", "conv_w", "w_out"),
         ("ffn2_w_gate", "ffn2_w_up", "ffn2_w_down", "ple_w_gate", "ple_w_proj"))
TRANSPOSED = ("ffn1_w_gate", "ffn1_w_up", "ffn2_w_gate", "ffn2_w_up", "w_in")
ROW_PAD = 32
DONE = (("ffn2_w_gate", "ffn2_w_up", "ffn2_w_down", "w_out", "ple_w_gate", "ple_w_proj"), ("w_in",),
        ("ffn1_w_gate", "ffn1_w_up", "ffn1_w_down"))


def _local_step(x, p, tgt, fetch, S, on_grads, on_later):
    G = GM_WIDTH
    K = N_CHIPS
    b_st = S["gm_b_s"][0].T
    w_s = S["gm_w_s"][0]
    dtb, alog, dsk = _pad_lanes(S["dt_bias"]), _pad_lanes(S["a_log"]), _pad_lanes(S["d_skip"])
    gfin = S["final_norm"].reshape(1, -1)

    def rows(a):
        return a.reshape(-1, D_MODEL)

    def shards(a):
        return a.reshape(K, -1, D_MODEL)

    wg1, wu1, wd1 = [rows(a) for a in fetch(0, None)]
    h1, n1, a1, b1 = _ffn_fwd("ffn1_fwd", x, S["ffn1_norm"], wg1, wu1, wd1)
    w_in4, cw4, wo4 = fetch(1, h1)
    w_in = w_in4.reshape(IN_PROJ, D_MODEL)
    w_uv = w_in[:2 * G]
    w_zxd = jnp.pad(w_in[2 * G:], ((0, ZXD - (IN_PROJ - 2 * G)), (0, 0)))
    conv_w = jnp.transpose(cw4, (1, 0, 2)).reshape(SSM_CONV, CONV_DIM)
    wo = wo4.reshape(-1, D_MODEL)
    n2, act, slope, z, xbc, dtr, ya = _mix_fwd(h1, S["mix_norm"], w_uv, w_zxd, S["gm_ln_g"], S["gm_ln_b"], w_s, b_st,
                                               S["gm_out_norm"])
    yb, xc, sg, y_ssd, sprev = _ssd_fwd(xbc, z, dtr, conv_w, S["conv_b"], dtb, alog, dsk, S["ssm_norm"])
    wg2, wu2, wd2, wpg4, wpp4 = fetch(2, yb)
    wg2, wu2, wd2 = rows(wg2), rows(wu2), rows(wd2)
    h2, h3, n3, a2, b2 = _ffn_fwd("ffn2_fwd", h1, S["ffn2_norm"], wg2, wu2, wd2, pre=(ya, yb, wo))
    dh3, loss, dgp, dwpg, dbpg, dwpp, dgf = _tail(h3, p, tgt, S["ple_norm"], wpg4.reshape(-1, D_MODEL), S["ple_b_gate"], wpp4, gfin)
    dh2, da2, db2, hm2, dg_ffn2, dya, dyb = _ffn_bwd("ffn2_bwd", dh3, h2, S["ffn2_norm"], a2, b2, wg2, wu2, wd2, wo=wo, ga=G)
    dw_out = jnp.concatenate([_matmul_tn("dw_out_a", ya, dh2), _matmul_tn("dw_out_b", yb, dh2)], axis=0).reshape(wo4.shape)
    zero = on_grads(0, [shards(_matmul_tn("dw_ffn2_gate", da2, n3)), shards(_matmul_tn("dw_ffn2_up", db2, n3)),
                        shards(_matmul_tn("dw_ffn2_down", hm2, dh3, scale=0.5)), dw_out,
                        dwpg.astype(bf16).reshape(wpg4.shape), dwpp.astype(bf16)])
    dzxd, dcw, dcb, ddtb, dalog, ddsk, dgssm = _ssd_bwd(xbc, xc, sg, y_ssd, z, dtr, sprev, dyb, conv_w, S["conv_b"], dtb, alog, dsk,
                                                        S["ssm_norm"] + zero)
    zero = on_later(0, dgssm)
    dh1, duv, dg_mix, dlng, dlnb, dws, dbst, dgout = _mix_bwd(dh2, h1, S["mix_norm"] + zero, act, slope, dya, dzxd, w_uv, w_zxd,
                                                              S["gm_ln_g"], S["gm_ln_b"], w_s, b_st, S["gm_out_norm"])
    dw_in = jnp.concatenate([_matmul_tn("dw_in_uv", duv, n2), _matmul_tn("dw_in_zxd", dzxd, n2)[:IN_PROJ - 2 * G]], axis=0)
    zero = on_grads(1, [dw_in.reshape(w_in4.shape)])
    dx, da1, db1, hm1, dg_ffn1 = _ffn_bwd("ffn1_bwd", dh1, x, S["ffn1_norm"] + zero, a1, b1, wg1, wu1, wd1)
    loss = loss + on_later(1, dg_ffn1)
    zero = on_grads(2, [shards(_matmul_tn("dw_ffn1_gate", da1, n1)), shards(_matmul_tn("dw_ffn1_up", db1, n1)),
                        shards(_matmul_tn("dw_ffn1_down", hm1, dh1, scale=0.5))])
    loss = loss + zero
    nh = SSM_HEADS
    gS = {"ffn1_norm": dg_ffn1, "mix_norm": dg_mix, "gm_ln_g": dlng, "gm_ln_b": dlnb, "gm_w_s": dws[None], "gm_b_s": dbst.T[None],
          "gm_out_norm": dgout, "conv_b": dcb, "dt_bias": ddtb[:, :nh], "a_log": dalog[:, :nh], "d_skip": ddsk[:, :nh],
          "ssm_norm": dgssm, "ffn2_norm": dg_ffn2, "ple_norm": dgp, "ple_b_gate": dbpg, "final_norm": dgf.reshape(-1)}
    return loss, dx, dcw, gS


_WEIGHTS = ("ffn1_norm", "ffn1_w_gate", "ffn1_w_up", "ffn1_w_down", "mix_norm", "w_in", "gm_ln_g", "gm_ln_b", "gm_w_s", "gm_b_s",
            "gm_out_norm", "conv_w", "conv_b", "dt_bias", "a_log", "d_skip", "ssm_norm", "w_out", "ffn2_norm", "ffn2_w_gate",
            "ffn2_w_up", "ffn2_w_down", "ple_norm", "ple_w_gate", "ple_b_gate", "ple_w_proj", "final_norm")
_BIG_NAMES = BIG


def kernel(x, p, ffn1_norm, ffn1_w_gate, ffn1_w_up, ffn1_w_down, mix_norm, w_in, gm_ln_g, gm_ln_b, gm_w_s, gm_b_s, gm_out_norm, conv_w, conv_b, dt_bias, a_log, d_skip, ssm_norm, w_out, ffn2_norm, ffn2_w_gate, ffn2_w_up, ffn2_w_down, ple_norm, ple_w_gate, ple_b_gate, ple_w_proj, final_norm, loss_target, m_ffn1_norm, m_ffn1_w_gate, m_ffn1_w_up, m_ffn1_w_down, m_mix_norm, m_w_in, m_gm_ln_g, m_gm_ln_b, m_gm_w_s, m_gm_b_s, m_gm_out_norm, m_conv_w, m_conv_b, m_dt_bias, m_a_log, m_d_skip, m_ssm_norm, m_w_out, m_ffn2_norm, m_ffn2_w_gate, m_ffn2_w_up, m_ffn2_w_down, m_ple_norm, m_ple_w_gate, m_ple_b_gate, m_ple_w_proj, m_final_norm, v_ffn1_norm, v_ffn1_w_gate, v_ffn1_w_up, v_ffn1_w_down, v_mix_norm, v_w_in, v_gm_ln_g, v_gm_ln_b, v_gm_w_s, v_gm_b_s, v_gm_out_norm, v_conv_w, v_conv_b, v_dt_bias, v_a_log, v_d_skip, v_ssm_norm, v_w_out, v_ffn2_norm, v_ffn2_w_gate, v_ffn2_w_up, v_ffn2_w_down, v_ple_norm, v_ple_w_gate, v_ple_b_gate, v_ple_w_proj, v_final_norm):
    given = dict(locals())
    w = {n: given[n] for n in _WEIGHTS}
    m = {n: given["m_" + n] for n in _WEIGHTS}
    v = {n: given["v_" + n] for n in _WEIGHTS}

    c_idx = lax.axis_index("c").astype(jnp.int32).reshape(1)
    chip = 2 * lax.axis_index("x") + lax.axis_index("y")
    chip_idx = chip.astype(jnp.int32).reshape(1)

    shard = {n: (jnp.swapaxes(w[n][0], 0, 1) if n in TRANSPOSED else w[n][0]).astype(bf16) for n in BIG}
    shard["conv_w"] = w["conv_w"][0]
    first = _gather_weights([shard[n] for n in FETCH[0]], [True] * len(FETCH[0]))
    fetching, after = [], first[-1]
    for k in (1, 2):
        srcs = [shard[n] for n in FETCH[k]]
        lands = [jax.ShapeDtypeStruct((N_CHIPS,) + s.shape, s.dtype) for s in srcs]
        fetching.append(_copies_start("gather%d_start" % k, srcs, lands, 4 * len(srcs), _gather_copies, after))
        after = fetching[-1][4]

    def fetch(k, after_):
        return first if k == 0 else _copies_wait("gather%d_wait" % k, fetching[k - 1], _gather_copies, [after_])[1]

    swapping, exchanging = {}, {}

    def exchange(k, grads, others):
        parts = [_add_halves("add_" + n, g_, o_, c_idx) for n, g_, o_ in zip(DONE[k], grads, others)]
        lands = [jax.ShapeDtypeStruct((3,) + p_.shape[1:], p_.dtype) for p_ in parts]
        exchanging[k] = _copies_start("exchange%d_start" % k, parts, lands, 3 * len(parts), _partial_copies, c_idx)
        return exchanging[k][4][0, 0]

    def on_grads(k, grads):
        grads = [_pad_rows(g_) for g_ in grads]
        lands = [jax.ShapeDtypeStruct((g_.shape[0], g_.shape[1] // 2, g_.shape[2]), g_.dtype) for g_ in grads]
        swapping[k] = _copies_start("swap%d_start" % k, grads, lands, len(grads), _swap_copies, c_idx)
        return swapping[k][4][0, 0]

    def on_later(k, after_):
        return exchange(k, *_copies_wait("swap%d_wait" % k, swapping[k], _swap_copies, [after_]))

    S = {n: w[n] for n in SMALL}
    S["ffn1_norm"] = S["ffn1_norm"] + after[0, 0]
    loss, dx, dcw, gS = _local_step(x[0], p[0, 0], loss_target[0], fetch, S, on_grads, on_later)

    small = _pack_small([gS[n] for n in SMALL] + [dcw, loss[:, :1]])
    small_lands = [jax.ShapeDtypeStruct((N_DEV - 1,) + small.shape, small.dtype)]
    small_st = _copies_start("small_start", [small], small_lands, N_DEV - 1, _small_copies, c_idx)

    g, delta, new_m, new_v = {}, {}, {}, {}

    def share_start(k, after_):
        parts, recv = _copies_wait("exchange%d_wait" % k, exchanging[k], _partial_copies, after_)
        mine = [_sum_partials("sum_" + n, p_, r_, chip_idx) for n, p_, r_ in zip(DONE[k], parts, recv)]
        return _copies_start("share%d_start" % k, mine, mine, len(mine), _share_copies, c_idx)

    def update(k, sharing, after_):
        mine, theirs = _copies_wait("share%d_wait" % k, sharing, _share_copies, after_)
        after = []
        for n, gm_, gt_ in zip(DONE[k], mine, theirs):
            flip = (lambda a: jnp.swapaxes(a, 0, 1)) if n in TRANSPOSED else (lambda a: a)
            rows = flip(w[n][0]).shape[0]
            if rows % ROW_PAD:
                def lin(a):
                    return jnp.transpose(a.reshape(-1, LANES, rows), (2, 0, 1))

                def back(a):
                    return jnp.transpose(a, (1, 2, 0)).reshape(1, -1, rows)

                gm_, gt_ = [a.reshape(a.shape[0], -1, LANES) for a in (gm_, gt_)]
                g_ = jnp.where(c_idx[0] == 0, jnp.concatenate([gm_, gt_]), jnp.concatenate([gt_, gm_]))[:rows]
                outs = [g_, *_adamw("adamw_" + n, lin(w[n]), g_, lin(m[n]), lin(v[n]), tr=HALF_ROWS_F32)]
                g[n], delta[n], new_m[n], new_v[n] = [back(o) for o in outs]
            else:
                outs = _adamw_big("adamw_" + n, flip(w[n][0]), gm_, gt_, flip(m[n][0]), flip(v[n][0]), c_idx)
                g[n], delta[n], new_m[n], new_v[n] = [flip(o)[None] for o in outs]
            after.append(outs[3])
        return after

    on_later(len(DONE) - 1, small_st[4])
    sharing = [share_start(0, [exchanging[len(DONE) - 1][4]])]
    after = [sharing[0][4]]
    for k in range(len(DONE)):
        if k + 1 < len(DONE):
            sharing.append(share_start(k + 1, after))
            after = [sharing[k + 1][4]]
        after = update(k, sharing[k], after)
    (own,), (slots,) = _copies_wait("small_wait", small_st, _small_copies, after)
    dev_idx = (2 * chip + lax.axis_index("c")).astype(jnp.int32).reshape(1)
    small_shapes = [w[n].shape for n in SMALL] + [dcw.shape, (1, 1)]
    small_sum = _unpack_small(_sum_small(own, slots, dev_idx), small_shapes)
    g.update({n: small_sum[i] for i, n in enumerate(SMALL)})
    cshard = w["conv_w"].shape[2]
    g["conv_w"] = lax.dynamic_slice_in_dim(small_sum[len(SMALL)], chip * cshard, cshard, axis=1)[None]
    loss_total = small_sum[len(SMALL) + 1].reshape(())
    sm_names = SMALL + ("conv_w",)
    sm_shapes = [w[n].shape for n in sm_names]
    d_s, m_s, v_s = _adamw("adamw_small", _pack_small([w[n] for n in sm_names]), _pack_small([g[n] for n in sm_names]),
                           _pack_small([m[n] for n in sm_names]), _pack_small([v[n] for n in sm_names]))
    for dst, src in ((delta, d_s), (new_m, m_s), (new_v, v_s)):
        for n, val in zip(sm_names, _unpack_small(src, sm_shapes)):
            dst[n] = val

    return (loss_total, dx[None], *[g[n] for n in _WEIGHTS], *[delta[n] for n in _WEIGHTS],
            *[new_m[n] for n in _WEIGHTS], *[new_v[n] for n in _WEIGHTS])
```
